```python
import math
import jax, jax.numpy as jnp
from jax import lax
import numpy as np

D_MODEL = 1024
BATCH = 32
SEQ = 2048
DEPTH = 1

MEM_LEN = 256
MEM_HEADS = 4
MEM_HEAD_DIM = 128
GM_WIDTH = D_MODEL // 2
GM_CHUNK = 128
GM_GROUPS = 4
GM_GROUP_W = GM_WIDTH // GM_GROUPS
MLA_HEADS = 8
MLA_NOPE = 128
MLA_ROPE = 64
MLA_V = 128
Q_LORA = 384
KV_LORA = 256
ROPE_BASE = 10000.0
Q_BLOCK = 128
D_FF = 4 * D_MODEL
N_BRANCH = 3
EPS = 1e-6
W_GM = 2 * GM_WIDTH
W_MLA = Q_LORA + KV_LORA + MLA_ROPE
W_MEMQ = MEM_HEADS * MEM_HEAD_DIM
W_GATE = N_BRANCH * D_MODEL
W_IN_COLS = W_GM + W_MLA + W_MEMQ + W_GATE

kernel_name = "hybrid_gmlp_mla_memory_gated_block"


def rmsnorm(x, g):
    xf = x.astype(jnp.float32)
    y = xf * lax.rsqrt(jnp.mean(xf * xf, axis=-1, keepdims=True) + EPS)
    return (y * g.astype(jnp.float32)).astype(x.dtype)


def layernorm(x, g, b):
    xf = x.astype(jnp.float32)
    mu = jnp.mean(xf, axis=-1, keepdims=True)
    xc = xf - mu
    y = xc * lax.rsqrt(jnp.mean(xc * xc, axis=-1, keepdims=True) + EPS)
    return (y * g.astype(jnp.float32) + b.astype(jnp.float32)).astype(x.dtype)


def rope_tables(positions):
    inv_freq = ROPE_BASE ** (-jnp.arange(0, MLA_ROPE, 2, dtype=jnp.float32) / MLA_ROPE)
    ang = positions.astype(jnp.float32)[..., None] * inv_freq
    return jnp.cos(ang), jnp.sin(ang)


def apply_rope(x, cos, sin):
    x1, x2 = jnp.split(x.astype(jnp.float32), 2, axis=-1)
    return jnp.concatenate([x1 * cos - x2 * sin, x2 * cos + x1 * sin], axis=-1).astype(x.dtype)


def gmlp_branch(z_u, z_v, g_ln, b_ln, w_s, b_s):
    B, S, _ = z_u.shape
    u = jax.nn.gelu(z_u)
    v = layernorm(jax.nn.gelu(z_v), g_ln, b_ln)
    v5 = v.reshape(B, S // GM_CHUNK, GM_CHUNK, GM_GROUPS, GM_GROUP_W)
    w_causal = jnp.tril(w_s).astype(v.dtype)
    mixed = jnp.einsum('gts,bnsgw->bntgw', w_causal, v5) + b_s.T[:, :, None].astype(v.dtype)
    return u * mixed.reshape(B, S, GM_WIDTH)


def mla_branch(c_q, c_kv, k_pe, cos, sin, g_cq, w_uq, g_ckv, w_ukv,
               g_q_nope, g_q_pe, g_k_nope, g_k_pe):
    B, S, _ = c_q.shape
    q = (rmsnorm(c_q, g_cq) @ w_uq).reshape(B, S, MLA_HEADS, MLA_NOPE + MLA_ROPE)
    q_nope, q_pe = q[..., :MLA_NOPE], q[..., MLA_NOPE:]
    kv = (rmsnorm(c_kv, g_ckv) @ w_ukv).reshape(B, S, MLA_HEADS, MLA_NOPE + MLA_V)
    k_nope, v = kv[..., :MLA_NOPE], kv[..., MLA_NOPE:]
    q_nope = rmsnorm(q_nope, g_q_nope)
    k_nope = rmsnorm(k_nope, g_k_nope)
    q_pe = apply_rope(rmsnorm(q_pe, g_q_pe), cos[:, :, None, :], sin[:, :, None, :])
    k_pe = apply_rope(rmsnorm(k_pe, g_k_pe), cos, sin)
    scale = 1.0 / math.sqrt(MLA_NOPE + MLA_ROPE)
    nb = S // Q_BLOCK
    qn_b = q_nope.reshape(B, nb, Q_BLOCK, MLA_HEADS, MLA_NOPE).transpose(1, 0, 2, 3, 4)
    qp_b = q_pe.reshape(B, nb, Q_BLOCK, MLA_HEADS, MLA_ROPE).transpose(1, 0, 2, 3, 4)
    key_pos = jnp.arange(S)

    def block(args):
        qn, qp, i = args
        s = (jnp.einsum('bqhd,bkhd->bhqk', qn, k_nope)
             + jnp.einsum('bqhd,bkd->bhqk', qp, k_pe)).astype(jnp.float32) * scale
        q_pos = i * Q_BLOCK + jnp.arange(Q_BLOCK)
        s = jnp.where((q_pos[:, None] >= key_pos[None, :])[None, None], s, -jnp.inf)
        p = jax.nn.softmax(s, axis=-1).astype(v.dtype)
        return jnp.einsum('bhqk,bkhd->bqhd', p, v)

    out = lax.map(block, (qn_b, qp_b, jnp.arange(nb)))
    return out.transpose(1, 0, 2, 3, 4).reshape(B, S, MLA_HEADS * MLA_V)


def memory_branch(q_m, mem, g_mem, w_mem_kv, g_mq, g_mk):
    B, S, _ = q_m.shape
    M = mem.shape[1]
    q = rmsnorm(q_m.reshape(B, S, MEM_HEADS, MEM_HEAD_DIM), g_mq)
    kv = (rmsnorm(mem, g_mem) @ w_mem_kv).reshape(B, M, 2, MEM_HEADS, MEM_HEAD_DIM)
    k = rmsnorm(kv[:, :, 0], g_mk)
    v = kv[:, :, 1]
    s = jnp.einsum('bshd,bmhd->bhsm', q, k).astype(jnp.float32) / math.sqrt(MEM_HEAD_DIM)
    p = jax.nn.softmax(s, axis=-1).astype(v.dtype)
    return jnp.einsum('bhsm,bmhd->bshd', p, v).reshape(B, S, MEM_HEADS * MEM_HEAD_DIM)


def _fwd_setup_inputs(seed: int = 0) -> dict:
    key = jax.random.key(seed)
    ks = iter(jax.random.split(key, 40))

    def w(shape, fan_in):
        return jax.random.normal(next(ks), (DEPTH,) + shape, jnp.float32) * (fan_in ** -0.5)

    def gain(n):
        return 1.0 + 0.02 * jax.random.normal(next(ks), (DEPTH, n), jnp.float32)

    x = jax.random.normal(next(ks), (BATCH, SEQ, D_MODEL), jnp.float32)
    mem = jax.random.normal(next(ks), (BATCH, MEM_LEN, D_MODEL), jnp.float32)
    offset = jax.random.randint(next(ks), (BATCH, 1), 0, 4096, dtype=jnp.int32)
    positions = (offset + jnp.arange(SEQ, dtype=jnp.int32)[None, :]).astype(jnp.int32)
    return {
        "x": x,
        "mem": mem,
        "positions": positions,
        "g_mix": gain(D_MODEL),
        "w_in": w((D_MODEL, W_IN_COLS), D_MODEL),
        "g_cq": gain(Q_LORA),
        "w_uq": w((Q_LORA, MLA_HEADS * (MLA_NOPE + MLA_ROPE)), Q_LORA),
        "g_ckv": gain(KV_LORA),
        "w_ukv": w((KV_LORA, MLA_HEADS * (MLA_NOPE + MLA_V)), KV_LORA),
        "g_q_nope": gain(MLA_NOPE),
        "g_q_pe": gain(MLA_ROPE),
        "g_k_nope": gain(MLA_NOPE),
        "g_k_pe": gain(MLA_ROPE),
        "g_gm_ln": gain(GM_WIDTH),
        "b_gm_ln": 0.02 * jax.random.normal(next(ks), (DEPTH, GM_WIDTH), jnp.float32),
        "w_spatial": w((GM_GROUPS, GM_CHUNK, GM_CHUNK), GM_CHUNK),
        "b_spatial": 1.0 + 0.02 * jax.random.normal(next(ks), (DEPTH, GM_GROUPS, GM_CHUNK), jnp.float32),
        "g_mem": gain(D_MODEL),
        "w_mem_kv": w((D_MODEL, 2 * MEM_HEADS * MEM_HEAD_DIM), D_MODEL),
        "g_mq": gain(MEM_HEAD_DIM),
        "g_mk": gain(MEM_HEAD_DIM),
        "w_o_gm": w((GM_WIDTH, D_MODEL), GM_WIDTH),
        "w_o_mla": w((MLA_HEADS * MLA_V, D_MODEL), MLA_HEADS * MLA_V),
        "w_o_mem": w((MEM_HEADS * MEM_HEAD_DIM, D_MODEL), MEM_HEADS * MEM_HEAD_DIM),
        "w_out": w((D_MODEL, D_MODEL), D_MODEL),
        "g_ffn": gain(D_MODEL),
        "w_ff1": w((D_MODEL, D_FF), D_MODEL),
        "w_ff2": w((D_FF, D_MODEL), D_FF),
    }


def _fwd_reference(x, mem, positions, g_mix, w_in, g_cq, w_uq, g_ckv, w_ukv,
              g_q_nope, g_q_pe, g_k_nope, g_k_pe, g_gm_ln, b_gm_ln, w_spatial, b_spatial,
              g_mem, w_mem_kv, g_mq, g_mk, w_o_gm, w_o_mla, w_o_mem, w_out,
              g_ffn, w_ff1, w_ff2):
    cos, sin = rope_tables(positions)
    split_at = [GM_WIDTH, W_GM, W_GM + Q_LORA, W_GM + Q_LORA + KV_LORA,
                W_GM + W_MLA, W_GM + W_MLA + W_MEMQ]
    for l in range(DEPTH):
        h = rmsnorm(x, g_mix[l])
        z = h @ w_in[l]
        z_u, z_v, c_q, c_kv, k_pe, q_m, z_g = jnp.split(z, split_at, axis=-1)
        y_gm = gmlp_branch(z_u, z_v, g_gm_ln[l], b_gm_ln[l], w_spatial[l], b_spatial[l]) @ w_o_gm[l]
        y_mla = mla_branch(c_q, c_kv, k_pe, cos, sin, g_cq[l], w_uq[l], g_ckv[l], w_ukv[l],
                           g_q_nope[l], g_q_pe[l], g_k_nope[l], g_k_pe[l]) @ w_o_mla[l]
        y_mem = memory_branch(q_m, mem, g_mem[l], w_mem_kv[l], g_mq[l], g_mk[l]) @ w_o_mem[l]
        gates = jax.nn.sigmoid(z_g).reshape(z_g.shape[:-1] + (N_BRANCH, D_MODEL))
        merged = gates[..., 0, :] * y_gm + gates[..., 1, :] * y_mla + gates[..., 2, :] * y_mem
        x = x + merged @ w_out[l]
        h2 = rmsnorm(x, g_ffn[l])
        x = x + jnp.square(jax.nn.relu(h2 @ w_ff1[l])) @ w_ff2[l]
    return x


import jax as _jax
import jax.numpy as _jnp

TWIN_FORMAT = 'train_step'
FWD_PARAMS = ['x', 'mem', 'positions', 'g_mix', 'w_in', 'g_cq', 'w_uq', 'g_ckv', 'w_ukv', 'g_q_nope', 'g_q_pe', 'g_k_nope', 'g_k_pe', 'g_gm_ln', 'b_gm_ln', 'w_spatial', 'b_spatial', 'g_mem', 'w_mem_kv', 'g_mq', 'g_mk', 'w_o_gm', 'w_o_mla', 'w_o_mem', 'w_out', 'g_ffn', 'w_ff1', 'w_ff2']
TWIN_WEIGHTS = ['g_mix', 'w_in', 'g_cq', 'w_uq', 'g_ckv', 'w_ukv', 'g_q_nope', 'g_q_pe', 'g_k_nope', 'g_k_pe', 'g_gm_ln', 'b_gm_ln', 'w_spatial', 'b_spatial', 'g_mem', 'w_mem_kv', 'g_mq', 'g_mk', 'w_o_gm', 'w_o_mla', 'w_o_mem', 'w_out', 'g_ffn', 'w_ff1', 'w_ff2']
TWIN_DIFF_INPUT = 'x'
TWIN_INPUTS = ['x', 'mem', 'positions', 'g_mix', 'w_in', 'g_cq', 'w_uq', 'g_ckv', 'w_ukv', 'g_q_nope', 'g_q_pe', 'g_k_nope', 'g_k_pe', 'g_gm_ln', 'b_gm_ln', 'w_spatial', 'b_spatial', 'g_mem', 'w_mem_kv', 'g_mq', 'g_mk', 'w_o_gm', 'w_o_mla', 'w_o_mem', 'w_out', 'g_ffn', 'w_ff1', 'w_ff2', 'loss_target', 'm_g_mix', 'm_w_in', 'm_g_cq', 'm_w_uq', 'm_g_ckv', 'm_w_ukv', 'm_g_q_nope', 'm_g_q_pe', 'm_g_k_nope', 'm_g_k_pe', 'm_g_gm_ln', 'm_b_gm_ln', 'm_w_spatial', 'm_b_spatial', 'm_g_mem', 'm_w_mem_kv', 'm_g_mq', 'm_g_mk', 'm_w_o_gm', 'm_w_o_mla', 'm_w_o_mem', 'm_w_out', 'm_g_ffn', 'm_w_ff1', 'm_w_ff2', 'v_g_mix', 'v_w_in', 'v_g_cq', 'v_w_uq', 'v_g_ckv', 'v_w_ukv', 'v_g_q_nope', 'v_g_q_pe', 'v_g_k_nope', 'v_g_k_pe', 'v_g_gm_ln', 'v_b_gm_ln', 'v_w_spatial', 'v_b_spatial', 'v_g_mem', 'v_w_mem_kv', 'v_g_mq', 'v_g_mk', 'v_w_o_gm', 'v_w_o_mla', 'v_w_o_mem', 'v_w_out', 'v_g_ffn', 'v_w_ff1', 'v_w_ff2']
TWIN_OUTPUTS = ['loss', 'grad_x', 'grad_g_mix', 'grad_w_in', 'grad_g_cq', 'grad_w_uq', 'grad_g_ckv', 'grad_w_ukv', 'grad_g_q_nope', 'grad_g_q_pe', 'grad_g_k_nope', 'grad_g_k_pe', 'grad_g_gm_ln', 'grad_b_gm_ln', 'grad_w_spatial', 'grad_b_spatial', 'grad_g_mem', 'grad_w_mem_kv', 'grad_g_mq', 'grad_g_mk', 'grad_w_o_gm', 'grad_w_o_mla', 'grad_w_o_mem', 'grad_w_out', 'grad_g_ffn', 'grad_w_ff1', 'grad_w_ff2', 'delta_g_mix', 'delta_w_in', 'delta_g_cq', 'delta_w_uq', 'delta_g_ckv', 'delta_w_ukv', 'delta_g_q_nope', 'delta_g_q_pe', 'delta_g_k_nope', 'delta_g_k_pe', 'delta_g_gm_ln', 'delta_b_gm_ln', 'delta_w_spatial', 'delta_b_spatial', 'delta_g_mem', 'delta_w_mem_kv', 'delta_g_mq', 'delta_g_mk', 'delta_w_o_gm', 'delta_w_o_mla', 'delta_w_o_mem', 'delta_w_out', 'delta_g_ffn', 'delta_w_ff1', 'delta_w_ff2', 'new_m_g_mix', 'new_m_w_in', 'new_m_g_cq', 'new_m_w_uq', 'new_m_g_ckv', 'new_m_w_ukv', 'new_m_g_q_nope', 'new_m_g_q_pe', 'new_m_g_k_nope', 'new_m_g_k_pe', 'new_m_g_gm_ln', 'new_m_b_gm_ln', 'new_m_w_spatial', 'new_m_b_spatial', 'new_m_g_mem', 'new_m_w_mem_kv', 'new_m_g_mq', 'new_m_g_mk', 'new_m_w_o_gm', 'new_m_w_o_mla', 'new_m_w_o_mem', 'new_m_w_out', 'new_m_g_ffn', 'new_m_w_ff1', 'new_m_w_ff2', 'new_v_g_mix', 'new_v_w_in', 'new_v_g_cq', 'new_v_w_uq', 'new_v_g_ckv', 'new_v_w_ukv', 'new_v_g_q_nope', 'new_v_g_q_pe', 'new_v_g_k_nope', 'new_v_g_k_pe', 'new_v_g_gm_ln', 'new_v_b_gm_ln', 'new_v_w_spatial', 'new_v_b_spatial', 'new_v_g_mem', 'new_v_w_mem_kv', 'new_v_g_mq', 'new_v_g_mk', 'new_v_w_o_gm', 'new_v_w_o_mla', 'new_v_w_o_mem', 'new_v_w_out', 'new_v_g_ffn', 'new_v_w_ff1', 'new_v_w_ff2']
TWIN_LEAF_KINDS = {'loss': 'loss', 'grad_x': 'grad_x', 'grad_g_mix': 'grad_w', 'grad_w_in': 'grad_w', 'grad_g_cq': 'grad_w', 'grad_w_uq': 'grad_w', 'grad_g_ckv': 'grad_w', 'grad_w_ukv': 'grad_w', 'grad_g_q_nope': 'grad_w', 'grad_g_q_pe': 'grad_w', 'grad_g_k_nope': 'grad_w', 'grad_g_k_pe': 'grad_w', 'grad_g_gm_ln': 'grad_w', 'grad_b_gm_ln': 'grad_w', 'grad_w_spatial': 'grad_w', 'grad_b_spatial': 'grad_w', 'grad_g_mem': 'grad_w', 'grad_w_mem_kv': 'grad_w', 'grad_g_mq': 'grad_w', 'grad_g_mk': 'grad_w', 'grad_w_o_gm': 'grad_w', 'grad_w_o_mla': 'grad_w', 'grad_w_o_mem': 'grad_w', 'grad_w_out': 'grad_w', 'grad_g_ffn': 'grad_w', 'grad_w_ff1': 'grad_w', 'grad_w_ff2': 'grad_w', 'delta_g_mix': 'delta_w', 'delta_w_in': 'delta_w', 'delta_g_cq': 'delta_w', 'delta_w_uq': 'delta_w', 'delta_g_ckv': 'delta_w', 'delta_w_ukv': 'delta_w', 'delta_g_q_nope': 'delta_w', 'delta_g_q_pe': 'delta_w', 'delta_g_k_nope': 'delta_w', 'delta_g_k_pe': 'delta_w', 'delta_g_gm_ln': 'delta_w', 'delta_b_gm_ln': 'delta_w', 'delta_w_spatial': 'delta_w', 'delta_b_spatial': 'delta_w', 'delta_g_mem': 'delta_w', 'delta_w_mem_kv': 'delta_w', 'delta_g_mq': 'delta_w', 'delta_g_mk': 'delta_w', 'delta_w_o_gm': 'delta_w', 'delta_w_o_mla': 'delta_w', 'delta_w_o_mem': 'delta_w', 'delta_w_out': 'delta_w', 'delta_g_ffn': 'delta_w', 'delta_w_ff1': 'delta_w', 'delta_w_ff2': 'delta_w', 'new_m_g_mix': 'new_m', 'new_m_w_in': 'new_m', 'new_m_g_cq': 'new_m', 'new_m_w_uq': 'new_m', 'new_m_g_ckv': 'new_m', 'new_m_w_ukv': 'new_m', 'new_m_g_q_nope': 'new_m', 'new_m_g_q_pe': 'new_m', 'new_m_g_k_nope': 'new_m', 'new_m_g_k_pe': 'new_m', 'new_m_g_gm_ln': 'new_m', 'new_m_b_gm_ln': 'new_m', 'new_m_w_spatial': 'new_m', 'new_m_b_spatial': 'new_m', 'new_m_g_mem': 'new_m', 'new_m_w_mem_kv': 'new_m', 'new_m_g_mq': 'new_m', 'new_m_g_mk': 'new_m', 'new_m_w_o_gm': 'new_m', 'new_m_w_o_mla': 'new_m', 'new_m_w_o_mem': 'new_m', 'new_m_w_out': 'new_m', 'new_m_g_ffn': 'new_m', 'new_m_w_ff1': 'new_m', 'new_m_w_ff2': 'new_m', 'new_v_g_mix': 'new_v', 'new_v_w_in': 'new_v', 'new_v_g_cq': 'new_v', 'new_v_w_uq': 'new_v', 'new_v_g_ckv': 'new_v', 'new_v_w_ukv': 'new_v', 'new_v_g_q_nope': 'new_v', 'new_v_g_q_pe': 'new_v', 'new_v_g_k_nope': 'new_v', 'new_v_g_k_pe': 'new_v', 'new_v_g_gm_ln': 'new_v', 'new_v_b_gm_ln': 'new_v', 'new_v_w_spatial': 'new_v', 'new_v_b_spatial': 'new_v', 'new_v_g_mem': 'new_v', 'new_v_w_mem_kv': 'new_v', 'new_v_g_mq': 'new_v', 'new_v_g_mk': 'new_v', 'new_v_w_o_gm': 'new_v', 'new_v_w_o_mla': 'new_v', 'new_v_w_o_mem': 'new_v', 'new_v_w_out': 'new_v', 'new_v_g_ffn': 'new_v', 'new_v_w_ff1': 'new_v', 'new_v_w_ff2': 'new_v'}


def _forward(args):
    return _fwd_reference(*[args[k] for k in FWD_PARAMS])


def _output_shape():
    out = _jax.eval_shape(lambda: _forward(_fwd_setup_inputs(0)))
    return out.shape, out.dtype

N_MICROBATCH = 1
ADAM_LR = 0.001
ADAM_B1 = 0.9
ADAM_B2 = 0.999
ADAM_EPS = 1e-08
ADAM_WD = 0.01
ADAM_STEP = 10
PER_EXAMPLE_BATCH_AXIS = {'x': 0, 'mem': 0, 'positions': 0, 'loss_target': 0}
SHARED_INPUTS = []
_WEIGHT_DTYPES = {'g_mix': _jnp.float32, 'w_in': _jnp.float32, 'g_cq': _jnp.float32, 'w_uq': _jnp.float32, 'g_ckv': _jnp.float32, 'w_ukv': _jnp.float32, 'g_q_nope': _jnp.float32, 'g_q_pe': _jnp.float32, 'g_k_nope': _jnp.float32, 'g_k_pe': _jnp.float32, 'g_gm_ln': _jnp.float32, 'b_gm_ln': _jnp.float32, 'w_spatial': _jnp.float32, 'b_spatial': _jnp.float32, 'g_mem': _jnp.float32, 'w_mem_kv': _jnp.float32, 'g_mq': _jnp.float32, 'g_mk': _jnp.float32, 'w_o_gm': _jnp.float32, 'w_o_mla': _jnp.float32, 'w_o_mem': _jnp.float32, 'w_out': _jnp.float32, 'g_ffn': _jnp.float32, 'w_ff1': _jnp.float32, 'w_ff2': _jnp.float32}
MOMENT_SCALE = {'g_mix': 1.413491e+01, 'w_in': 3.652366e-01, 'g_cq': 1.638864e-01, 'w_uq': 8.395524e-02, 'g_ckv': 6.755940e-01, 'w_ukv': 1.755876e-01, 'g_q_nope': 4.486701e-01, 'g_q_pe': 4.051457e-01, 'g_k_nope': 4.489645e-01, 'g_k_pe': 3.967340e-01, 'g_gm_ln': 7.216708e+00, 'b_gm_ln': 4.789098e-01, 'w_spatial': 5.922280e-01, 'b_spatial': 1.572128e+01, 'g_mem': 7.344053e-01, 'w_mem_kv': 7.598950e-01, 'g_mq': 9.464474e-01, 'g_mk': 9.406329e-01, 'w_o_gm': 8.901972e+00, 'w_o_mla': 2.379014e-01, 'w_o_mem': 8.476147e-01, 'w_out': 8.522028e+00, 'g_ffn': 1.919605e+02, 'w_ff1': 3.258280e+00, 'w_ff2': 1.692794e+01}


def _to_microbatches(a, axis):
    t = _jnp.moveaxis(a, axis, 0)
    t = t.reshape((N_MICROBATCH, t.shape[0] // N_MICROBATCH) + t.shape[1:])
    return _jnp.moveaxis(t, 1, axis + 1)


def setup_inputs(seed: int = 0) -> dict:
    inp = _fwd_setup_inputs(seed)
    key = _jax.random.fold_in(_jax.random.key(seed), 7919)
    shape, _ = _output_shape()
    out = dict(inp)
    out["loss_target"] = _jax.random.normal(_jax.random.fold_in(key, 0), shape, _jnp.float32)
    for i, name in enumerate(TWIN_WEIGHTS):
        w = inp[name].astype(_jnp.float32)
        if MOMENT_SCALE is None:
            s = _jnp.sqrt(_jnp.mean(_jnp.square(w)) + 1e-30)
        else:
            s = MOMENT_SCALE[name]
        km, kv = _jax.random.split(_jax.random.fold_in(key, i + 1))
        out[name] = w
        out["m_" + name] = s * _jax.random.normal(km, w.shape, _jnp.float32)
        out["v_" + name] = (s * s) * _jax.random.uniform(kv, w.shape, _jnp.float32, 0.5, 1.5)
    if N_MICROBATCH > 1:
        for name, axis in PER_EXAMPLE_BATCH_AXIS.items():
            out[name] = _to_microbatches(out[name], axis)
    return {'x': out['x'], 'mem': out['mem'], 'positions': out['positions'], 'g_mix': out['g_mix'], 'w_in': out['w_in'], 'g_cq': out['g_cq'], 'w_uq': out['w_uq'], 'g_ckv': out['g_ckv'], 'w_ukv': out['w_ukv'], 'g_q_nope': out['g_q_nope'], 'g_q_pe': out['g_q_pe'], 'g_k_nope': out['g_k_nope'], 'g_k_pe': out['g_k_pe'], 'g_gm_ln': out['g_gm_ln'], 'b_gm_ln': out['b_gm_ln'], 'w_spatial': out['w_spatial'], 'b_spatial': out['b_spatial'], 'g_mem': out['g_mem'], 'w_mem_kv': out['w_mem_kv'], 'g_mq': out['g_mq'], 'g_mk': out['g_mk'], 'w_o_gm': out['w_o_gm'], 'w_o_mla': out['w_o_mla'], 'w_o_mem': out['w_o_mem'], 'w_out': out['w_out'], 'g_ffn': out['g_ffn'], 'w_ff1': out['w_ff1'], 'w_ff2': out['w_ff2'], 'loss_target': out['loss_target'], 'm_g_mix': out['m_g_mix'], 'm_w_in': out['m_w_in'], 'm_g_cq': out['m_g_cq'], 'm_w_uq': out['m_w_uq'], 'm_g_ckv': out['m_g_ckv'], 'm_w_ukv': out['m_w_ukv'], 'm_g_q_nope': out['m_g_q_nope'], 'm_g_q_pe': out['m_g_q_pe'], 'm_g_k_nope': out['m_g_k_nope'], 'm_g_k_pe': out['m_g_k_pe'], 'm_g_gm_ln': out['m_g_gm_ln'], 'm_b_gm_ln': out['m_b_gm_ln'], 'm_w_spatial': out['m_w_spatial'], 'm_b_spatial': out['m_b_spatial'], 'm_g_mem': out['m_g_mem'], 'm_w_mem_kv': out['m_w_mem_kv'], 'm_g_mq': out['m_g_mq'], 'm_g_mk': out['m_g_mk'], 'm_w_o_gm': out['m_w_o_gm'], 'm_w_o_mla': out['m_w_o_mla'], 'm_w_o_mem': out['m_w_o_mem'], 'm_w_out': out['m_w_out'], 'm_g_ffn': out['m_g_ffn'], 'm_w_ff1': out['m_w_ff1'], 'm_w_ff2': out['m_w_ff2'], 'v_g_mix': out['v_g_mix'], 'v_w_in': out['v_w_in'], 'v_g_cq': out['v_g_cq'], 'v_w_uq': out['v_w_uq'], 'v_g_ckv': out['v_g_ckv'], 'v_w_ukv': out['v_w_ukv'], 'v_g_q_nope': out['v_g_q_nope'], 'v_g_q_pe': out['v_g_q_pe'], 'v_g_k_nope': out['v_g_k_nope'], 'v_g_k_pe': out['v_g_k_pe'], 'v_g_gm_ln': out['v_g_gm_ln'], 'v_b_gm_ln': out['v_b_gm_ln'], 'v_w_spatial': out['v_w_spatial'], 'v_b_spatial': out['v_b_spatial'], 'v_g_mem': out['v_g_mem'], 'v_w_mem_kv': out['v_w_mem_kv'], 'v_g_mq': out['v_g_mq'], 'v_g_mk': out['v_g_mk'], 'v_w_o_gm': out['v_w_o_gm'], 'v_w_o_mla': out['v_w_o_mla'], 'v_w_o_mem': out['v_w_o_mem'], 'v_w_out': out['v_w_out'], 'v_g_ffn': out['v_g_ffn'], 'v_w_ff1': out['v_w_ff1'], 'v_w_ff2': out['v_w_ff2']}


def _loss(weights, diff, rest, loss_target):
    with _jax.named_scope("forward"):
        args = {**rest, TWIN_DIFF_INPUT: diff, **{k: w.astype(_WEIGHT_DTYPES[k]) for k, w in weights.items()}}
        y = _forward(args)
    with _jax.named_scope("loss_head"):
        err = _jnp.square(y.astype(_jnp.float32) - loss_target)
        return 0.5 * _jnp.sum(_jnp.mean(err, axis=-1)) if err.ndim else 0.5 * err


def _adamw(w, g, m, v):
    m = ADAM_B1 * m + (1.0 - ADAM_B1) * g
    v = ADAM_B2 * v + (1.0 - ADAM_B2) * _jnp.square(g)
    m_hat = m / (1.0 - ADAM_B1 ** ADAM_STEP)
    v_hat = v / (1.0 - ADAM_B2 ** ADAM_STEP)
    delta = -ADAM_LR * (m_hat / (_jnp.sqrt(v_hat) + ADAM_EPS) + ADAM_WD * w)
    return delta, m, v


def reference(x, mem, positions, g_mix, w_in, g_cq, w_uq, g_ckv, w_ukv, g_q_nope, g_q_pe, g_k_nope, g_k_pe, g_gm_ln, b_gm_ln, w_spatial, b_spatial, g_mem, w_mem_kv, g_mq, g_mk, w_o_gm, w_o_mla, w_o_mem, w_out, g_ffn, w_ff1, w_ff2, loss_target, m_g_mix, m_w_in, m_g_cq, m_w_uq, m_g_ckv, m_w_ukv, m_g_q_nope, m_g_q_pe, m_g_k_nope, m_g_k_pe, m_g_gm_ln, m_b_gm_ln, m_w_spatial, m_b_spatial, m_g_mem, m_w_mem_kv, m_g_mq, m_g_mk, m_w_o_gm, m_w_o_mla, m_w_o_mem, m_w_out, m_g_ffn, m_w_ff1, m_w_ff2, v_g_mix, v_w_in, v_g_cq, v_w_uq, v_g_ckv, v_w_ukv, v_g_q_nope, v_g_q_pe, v_g_k_nope, v_g_k_pe, v_g_gm_ln, v_b_gm_ln, v_w_spatial, v_b_spatial, v_g_mem, v_w_mem_kv, v_g_mq, v_g_mk, v_w_o_gm, v_w_o_mla, v_w_o_mem, v_w_out, v_g_ffn, v_w_ff1, v_w_ff2):
    given = dict(x=x, mem=mem, positions=positions, g_mix=g_mix, w_in=w_in, g_cq=g_cq, w_uq=w_uq, g_ckv=g_ckv, w_ukv=w_ukv, g_q_nope=g_q_nope, g_q_pe=g_q_pe, g_k_nope=g_k_nope, g_k_pe=g_k_pe, g_gm_ln=g_gm_ln, b_gm_ln=b_gm_ln, w_spatial=w_spatial, b_spatial=b_spatial, g_mem=g_mem, w_mem_kv=w_mem_kv, g_mq=g_mq, g_mk=g_mk, w_o_gm=w_o_gm, w_o_mla=w_o_mla, w_o_mem=w_o_mem, w_out=w_out, g_ffn=g_ffn, w_ff1=w_ff1, w_ff2=w_ff2, loss_target=loss_target, m_g_mix=m_g_mix, m_w_in=m_w_in, m_g_cq=m_g_cq, m_w_uq=m_w_uq, m_g_ckv=m_g_ckv, m_w_ukv=m_w_ukv, m_g_q_nope=m_g_q_nope, m_g_q_pe=m_g_q_pe, m_g_k_nope=m_g_k_nope, m_g_k_pe=m_g_k_pe, m_g_gm_ln=m_g_gm_ln, m_b_gm_ln=m_b_gm_ln, m_w_spatial=m_w_spatial, m_b_spatial=m_b_spatial, m_g_mem=m_g_mem, m_w_mem_kv=m_w_mem_kv, m_g_mq=m_g_mq, m_g_mk=m_g_mk, m_w_o_gm=m_w_o_gm, m_w_o_mla=m_w_o_mla, m_w_o_mem=m_w_o_mem, m_w_out=m_w_out, m_g_ffn=m_g_ffn, m_w_ff1=m_w_ff1, m_w_ff2=m_w_ff2, v_g_mix=v_g_mix, v_w_in=v_w_in, v_g_cq=v_g_cq, v_w_uq=v_w_uq, v_g_ckv=v_g_ckv, v_w_ukv=v_w_ukv, v_g_q_nope=v_g_q_nope, v_g_q_pe=v_g_q_pe, v_g_k_nope=v_g_k_nope, v_g_k_pe=v_g_k_pe, v_g_gm_ln=v_g_gm_ln, v_b_gm_ln=v_b_gm_ln, v_w_spatial=v_w_spatial, v_b_spatial=v_b_spatial, v_g_mem=v_g_mem, v_w_mem_kv=v_w_mem_kv, v_g_mq=v_g_mq, v_g_mk=v_g_mk, v_w_o_gm=v_w_o_gm, v_w_o_mla=v_w_o_mla, v_w_o_mem=v_w_o_mem, v_w_out=v_w_out, v_g_ffn=v_g_ffn, v_w_ff1=v_w_ff1, v_w_ff2=v_w_ff2)
    weights = {n: given[n] for n in TWIN_WEIGHTS}
    shared = {n: given[n] for n in SHARED_INPUTS}
    per_example = {n: given[n] for n in ['x', 'mem', 'positions']}
    grad_fn = _jax.value_and_grad(_loss, argnums=(0, 1))

    def one_microbatch(ex, loss_target):
        ex = dict(ex)
        diff = ex.pop(TWIN_DIFF_INPUT)
        return grad_fn(weights, diff, {**shared, **ex}, loss_target)

    if N_MICROBATCH == 1:
        loss, (grad_w, grad_x) = one_microbatch(per_example, given["loss_target"])
    else:
        def body(carry, xs):
            loss_sum, grad_sum = carry
            l_k, (gw_k, gx_k) = one_microbatch(xs[0], xs[1])
            with _jax.named_scope("update"):
                return (loss_sum + l_k, _jax.tree.map(_jnp.add, grad_sum, gw_k)), gx_k

        init = (_jnp.zeros((), _jnp.float32), _jax.tree.map(_jnp.zeros_like, weights))
        (loss, grad_w), grad_x = _jax.lax.scan(body, init, (per_example, given["loss_target"]))
    with _jax.named_scope("update"):
        delta_w, new_m, new_v = {}, {}, {}
        for n in TWIN_WEIGHTS:
            delta_w[n], new_m[n], new_v[n] = _adamw(weights[n], grad_w[n], given["m_" + n], given["v_" + n])
    return (loss, grad_x, *[grad_w[n] for n in TWIN_WEIGHTS], *[delta_w[n] for n in TWIN_WEIGHTS],
            *[new_m[n] for n in TWIN_WEIGHTS], *[new_v[n] for n in TWIN_WEIGHTS])
```

```python
import functools
import math

import jax
import jax.numpy as jnp
from jax import lax
from jax.experimental import pallas as pl
from jax.experimental.pallas import tpu as pltpu

F32 = jnp.float32
BF16 = jnp.bfloat16

D_MODEL = 1024
MEM_HEADS = 4
HEAD = 128
GM_WIDTH = 512
GM_CHUNK = 128
GM_GROUPS = 4
MLA_HEADS = 8
MLA_ROPE = 64
Q_LORA = 384
KV_LORA = 256
D_FF = 4096
EPS = 1e-6
ROPE_BASE = 10000.0
MLA_SCALE = 1.0 / math.sqrt(HEAD + MLA_ROPE)
MEM_SCALE = 1.0 / math.sqrt(HEAD)
C_ZU, C_ZV, C_CQ, C_CKV, C_KPE, C_QM, C_ZG, C_END = 0, 512, 1024, 1408, 1664, 1728, 2240, 5312
Z_GATE, Z_QM, Z_MLA, Z_KPE, Z_COLS = 1024, 4096, 4608, 5248, 5376
MLA_W = 768
QCAT = 2 * HEAD
ADAM_LR, ADAM_B1, ADAM_B2, ADAM_EPS, ADAM_WD, ADAM_STEP = 0.001, 0.9, 0.999, 1e-08, 0.01, 10
N_DEV = 8
LANES = 128
VMEM_LIMIT = 48 * 1024 * 1024
NEG = -1e30

BIG = ['w_in', 'w_uq', 'w_ukv', 'w_mem_kv', 'w_o_gm', 'w_o_mla', 'w_o_mem', 'w_out', 'w_ff1', 'w_ff2']
BIG_AXIS = {'w_in': 1, 'w_uq': 1, 'w_ukv': 1, 'w_mem_kv': 0, 'w_o_gm': 1, 'w_o_mla': 0, 'w_o_mem': 1,
            'w_out': 0, 'w_ff1': 1, 'w_ff2': 0}
BIG_SHAPE = {'w_in': (1024, 5312), 'w_uq': (384, 1536), 'w_ukv': (256, 2048), 'w_mem_kv': (1024, 1024),
             'w_o_gm': (512, 1024), 'w_o_mla': (1024, 1024), 'w_o_mem': (512, 1024), 'w_out': (1024, 1024),
             'w_ff1': (1024, 4096), 'w_ff2': (4096, 1024)}
SMALL = ['g_mix', 'g_cq', 'g_ckv', 'g_q_nope', 'g_q_pe', 'g_k_nope', 'g_k_pe', 'g_gm_ln', 'b_gm_ln',
         'w_spatial', 'b_spatial', 'g_mem', 'g_mq', 'g_mk', 'g_ffn']
WEIGHTS = ['g_mix', 'w_in', 'g_cq', 'w_uq', 'g_ckv', 'w_ukv', 'g_q_nope', 'g_q_pe', 'g_k_nope', 'g_k_pe',
           'g_gm_ln', 'b_gm_ln', 'w_spatial', 'b_spatial', 'g_mem', 'w_mem_kv', 'g_mq', 'g_mk', 'w_o_gm',
           'w_o_mla', 'w_o_mem', 'w_out', 'g_ffn', 'w_ff1', 'w_ff2']


def _pick(n, target, mult=LANES):
    best = None
    t = mult
    while t <= min(n, target):
        if n % t == 0:
            best = t
        t += mult
    return best if best is not None else n


def _params(sem):
    return pltpu.CompilerParams(dimension_semantics=sem, vmem_limit_bytes=VMEM_LIMIT)


def _matmul(a, b, mode, out_dtype, name, add=None, relu2_a=False, relu2_grad=None,
            tm_t=512, tn_t=1024, tk_t=1024):
    if mode == 'nn':
        (M, K), (K2, N) = a.shape, b.shape
    elif mode == 'nt':
        (M, K), (N, K2) = a.shape, b.shape
    else:
        (K, M), (K2, N) = a.shape, b.shape
    assert K == K2, (name, a.shape, b.shape)
    tm, tn, tk = _pick(M, tm_t), _pick(N, tn_t), _pick(K, tk_t)
    nk = K // tk
    if mode == 'nn':
        a_spec = pl.BlockSpec((tm, tk), lambda i, j, k: (i, k))
        b_spec = pl.BlockSpec((tk, tn), lambda i, j, k: (k, j))
        dims = (((1,), (0,)), ((), ()))
    elif mode == 'nt':
        a_spec = pl.BlockSpec((tm, tk), lambda i, j, k: (i, k))
        b_spec = pl.BlockSpec((tn, tk), lambda i, j, k: (j, k))
        dims = (((1,), (1,)), ((), ()))
    else:
        a_spec = pl.BlockSpec((tk, tm), lambda i, j, k: (k, i))
        b_spec = pl.BlockSpec((tk, tn), lambda i, j, k: (k, j))
        dims = (((0,), (0,)), ((), ()))
    o_spec = pl.BlockSpec((tm, tn), lambda i, j, k: (i, j))
    has_add, has_e = add is not None, relu2_grad is not None

    def body(*refs):
        a_ref, b_ref = refs[0], refs[1]
        pos = 2
        add_ref = e_ref = None
        if has_add:
            add_ref = refs[pos]
            pos += 1
        if has_e:
            e_ref = refs[pos]
            pos += 1
        o_ref = refs[pos]
        acc_ref = refs[pos + 1] if nk > 1 else None

        av = a_ref[...]
        if relu2_a:
            av = jnp.maximum(av.astype(F32), 0.0)
            av = av * av
        prod = lax.dot_general(av.astype(BF16), b_ref[...].astype(BF16), dims, preferred_element_type=F32)

        def finish(r):
            if has_add:
                r = r + add_ref[...]
            if has_e:
                r = r * (2.0 * jnp.maximum(e_ref[...].astype(F32), 0.0))
            o_ref[...] = r.astype(out_dtype)

        if nk == 1:
            finish(prod)
        else:
            k = pl.program_id(2)

            @pl.when(k == 0)
            def _():
                acc_ref[...] = prod

            @pl.when(k > 0)
            def _():
                acc_ref[...] += prod

            @pl.when(k == nk - 1)
            def _():
                finish(acc_ref[...])

    ins, specs = [a, b], [a_spec, b_spec]
    if has_add:
        ins.append(add)
        specs.append(o_spec)
    if has_e:
        ins.append(relu2_grad)
        specs.append(o_spec)
    return pl.pallas_call(
        body, name=name, grid=(M // tm, N // tn, nk),
        in_specs=specs, out_specs=o_spec, out_shape=jax.ShapeDtypeStruct((M, N), out_dtype),
        scratch_shapes=[pltpu.VMEM((tm, tn), F32)] if nk > 1 else [],
        compiler_params=_params(("parallel", "parallel", "arbitrary")),
    )(*ins)


def _row_tile(rows, target=256):
    return _pick(rows, target, 8)


def _rowspec(tr, width, col=0):
    return pl.BlockSpec((tr, width), lambda i, col=col: (i, col))


def _fullspec(shape):
    nd = len(shape)
    return pl.BlockSpec(shape, lambda i, nd=nd: (0,) * nd)


def _rms(x, width):
    return lax.rsqrt(jnp.sum(x * x, axis=-1, keepdims=True) * (1.0 / width) + EPS)


def _rms_bwd_rows(x, g, dy, width):
    r = _rms(x, width)
    xh = x * r
    dn = dy * g
    dx = r * (dn - xh * (jnp.sum(dn * xh, axis=-1, keepdims=True) * (1.0 / width)))
    return dx, dy * xh


def _acc_rows(ref, val, first):
    s = jnp.sum(val, axis=0, keepdims=True)

    @pl.when(first)
    def _():
        ref[...] = s

    @pl.when(jnp.logical_not(first))
    def _():
        ref[...] += s


def _rms_fwd(x, g, name):
    rows, width = x.shape
    tr = _row_tile(rows)

    def body(x_ref, g_ref, o_ref):
        xv = x_ref[...]
        o_ref[...] = (xv * _rms(xv, width) * g_ref[...]).astype(BF16)

    return pl.pallas_call(
        body, name=name, grid=(rows // tr,),
        in_specs=[_rowspec(tr, width), _fullspec((1, width))], out_specs=_rowspec(tr, width),
        out_shape=jax.ShapeDtypeStruct((rows, width), BF16), compiler_params=_params(("parallel",)),
    )(x, g)


def _rms_bwd(x, g, dy, res, name):
    rows, width = x.shape
    tr = _row_tile(rows)
    has_res = res is not None

    def body(*refs):
        if has_res:
            x_ref, g_ref, dy_ref, res_ref, dx_ref, dg_ref = refs
        else:
            x_ref, g_ref, dy_ref, dx_ref, dg_ref = refs
        dx, dgv = _rms_bwd_rows(x_ref[...], g_ref[...], dy_ref[...], width)
        if has_res:
            dx = dx + res_ref[...]
        dx_ref[...] = dx
        _acc_rows(dg_ref, dgv, pl.program_id(0) == 0)

    ins = [x, g, dy] + ([res] if has_res else [])
    specs = [_rowspec(tr, width), _fullspec((1, width)), _rowspec(tr, width)] + ([_rowspec(tr, width)] if has_res else [])
    return pl.pallas_call(
        body, name=name, grid=(rows // tr,), in_specs=specs,
        out_specs=[_rowspec(tr, width), _fullspec((1, width))],
        out_shape=[jax.ShapeDtypeStruct((rows, width), F32), jax.ShapeDtypeStruct((1, width), F32)],
        compiler_params=_params(("arbitrary",)),
    )(*ins)


_GELU_C = math.sqrt(2.0 / math.pi)


def _gelu(x):
    t = jnp.tanh(_GELU_C * (x + 0.044715 * (x * x * x)))
    return 0.5 * x * (1.0 + t), t


def _gelu_grad(x, t):
    return 0.5 * (1.0 + t) + 0.5 * x * (1.0 - t * t) * (_GELU_C * (1.0 + 3.0 * 0.044715 * (x * x)))


def _gm_forward_rows(zu, zv, gln, bln, wc_ref, bst, n_chunk):
    u, tu = _gelu(zu)
    a, ta = _gelu(zv)
    mu = jnp.mean(a, axis=-1, keepdims=True)
    ac = a - mu
    rs = lax.rsqrt(jnp.mean(ac * ac, axis=-1, keepdims=True) + EPS)
    n = ac * rs
    v = n * gln + bln
    vb = v.astype(BF16)
    rows = []
    for c in range(n_chunk):
        cols = []
        for g in range(GM_GROUPS):
            vc = vb[c * GM_CHUNK:(c + 1) * GM_CHUNK, g * LANES:(g + 1) * LANES]
            mixed = jnp.dot(wc_ref[g], vc, preferred_element_type=F32) + bst[g]
            cols.append(mixed)
        rows.append(jnp.concatenate(cols, axis=1))
    mixed = jnp.concatenate(rows, axis=0) if n_chunk > 1 else rows[0]
    return u, tu, ta, n, rs, v, mixed


def _gm_fwd(z, gln, bln, wc, bst, name):
    rows = z.shape[0]
    tr = _pick(rows, 512, GM_CHUNK)
    n_chunk = tr // GM_CHUNK

    def body(zu_ref, zv_ref, gln_ref, bln_ref, wc_ref, bst_ref, o_ref):
        u, _, _, _, _, _, mixed = _gm_forward_rows(zu_ref[...], zv_ref[...], gln_ref[...], bln_ref[...], wc_ref,
                                                   bst_ref, n_chunk)
        o_ref[...] = (u * mixed).astype(BF16)

    return pl.pallas_call(
        body, name=name, grid=(rows // tr,),
        in_specs=[_rowspec(tr, GM_WIDTH, 0), _rowspec(tr, GM_WIDTH, 1), _fullspec((1, GM_WIDTH)), _fullspec((1, GM_WIDTH)),
                  _fullspec((GM_GROUPS, GM_CHUNK, GM_CHUNK)), _fullspec((GM_GROUPS, GM_CHUNK, LANES))],
        out_specs=_rowspec(tr, GM_WIDTH), out_shape=jax.ShapeDtypeStruct((rows, GM_WIDTH), BF16),
        compiler_params=_params(("parallel",)),
    )(z, z, gln, bln, wc, bst)


def _gm_bwd(z, dy, gln, bln, wc, wct, bst, name):
    rows = z.shape[0]
    tr = _pick(rows, 512, GM_CHUNK)
    n_chunk = tr // GM_CHUNK

    def body(zu_ref, zv_ref, dy_ref, gln_ref, bln_ref, wc_ref, wct_ref, bst_ref, dz_ref, dws_ref, dbs_ref, dgl_ref, dbl_ref):
        first = pl.program_id(0) == 0
        zu, zv, gln = zu_ref[...], zv_ref[...], gln_ref[...]
        u, tu, ta, n, rs, v, mixed = _gm_forward_rows(zu, zv, gln, bln_ref[...], wc_ref, bst_ref, n_chunk)
        dyv = dy_ref[...]
        dzu = dyv * mixed * _gelu_grad(zu, tu)
        dmix = dyv * u
        dmb = dmix.astype(BF16)
        vb = v.astype(BF16)
        dv_rows, dws, dbs = [], [None] * GM_GROUPS, None
        for c in range(n_chunk):
            rsl = slice(c * GM_CHUNK, (c + 1) * GM_CHUNK)
            cols = []
            for g in range(GM_GROUPS):
                csl = slice(g * LANES, (g + 1) * LANES)
                dmc = dmb[rsl, csl]
                cols.append(jnp.dot(wct_ref[g], dmc, preferred_element_type=F32))
                w_part = lax.dot_general(dmc, vb[rsl, csl], (((1,), (1,)), ((), ())), preferred_element_type=F32)
                dws[g] = w_part if dws[g] is None else dws[g] + w_part
            dv_rows.append(jnp.concatenate(cols, axis=1))
            dbs = dmix[rsl, :] if dbs is None else dbs + dmix[rsl, :]
        dv = jnp.concatenate(dv_rows, axis=0) if n_chunk > 1 else dv_rows[0]
        dn = dv * gln
        da = rs * (dn - jnp.mean(dn, axis=-1, keepdims=True) - n * jnp.mean(dn * n, axis=-1, keepdims=True))
        dzv = da * _gelu_grad(zv, ta)
        dz_ref[:, 0:GM_WIDTH] = dzu.astype(BF16)
        dz_ref[:, GM_WIDTH:2 * GM_WIDTH] = dzv.astype(BF16)
        _acc_rows(dgl_ref, dv * n, first)
        _acc_rows(dbl_ref, dv, first)

        @pl.when(first)
        def _():
            for g in range(GM_GROUPS):
                dws_ref[g] = dws[g]
            dbs_ref[...] = dbs

        @pl.when(jnp.logical_not(first))
        def _():
            for g in range(GM_GROUPS):
                dws_ref[g] += dws[g]
            dbs_ref[...] += dbs

    wspec = _fullspec((GM_GROUPS, GM_CHUNK, GM_CHUNK))
    return pl.pallas_call(
        body, name=name, grid=(rows // tr,),
        in_specs=[_rowspec(tr, GM_WIDTH, 0), _rowspec(tr, GM_WIDTH, 1), _rowspec(tr, GM_WIDTH), _fullspec((1, GM_WIDTH)),
                  _fullspec((1, GM_WIDTH)), wspec, wspec, wspec],
        out_specs=[_rowspec(tr, 2 * GM_WIDTH), wspec, _fullspec((GM_CHUNK, GM_WIDTH)), _fullspec((1, GM_WIDTH)),
                   _fullspec((1, GM_WIDTH))],
        out_shape=[jax.ShapeDtypeStruct((rows, 2 * GM_WIDTH), BF16), jax.ShapeDtypeStruct((GM_GROUPS, GM_CHUNK, GM_CHUNK), F32),
                   jax.ShapeDtypeStruct((GM_CHUNK, GM_WIDTH), F32), jax.ShapeDtypeStruct((1, GM_WIDTH), F32),
                   jax.ShapeDtypeStruct((1, GM_WIDTH), F32)],
        compiler_params=_params(("arbitrary",)),
    )(z, z, dy, gln, bln, wc, wct, bst)


def _lat_fwd(z, g_cq, g_ckv, name):
    rows = z.shape[0]
    tr = _row_tile(rows)

    def body(z_ref, gq_ref, gkv_ref, nq_ref, nkv_ref):
        zb = z_ref[...]
        cq, ckv = zb[:, 0:Q_LORA], zb[:, Q_LORA:Q_LORA + KV_LORA]
        nq_ref[...] = (cq * _rms(cq, Q_LORA) * gq_ref[...]).astype(BF16)
        nkv_ref[...] = (ckv * _rms(ckv, KV_LORA) * gkv_ref[...]).astype(BF16)

    return pl.pallas_call(
        body, name=name, grid=(rows // tr,),
        in_specs=[_rowspec(tr, MLA_W, Z_MLA // MLA_W), _fullspec((1, Q_LORA)), _fullspec((1, KV_LORA))],
        out_specs=[_rowspec(tr, Q_LORA), _rowspec(tr, KV_LORA)],
        out_shape=[jax.ShapeDtypeStruct((rows, Q_LORA), BF16), jax.ShapeDtypeStruct((rows, KV_LORA), BF16)],
        compiler_params=_params(("parallel",)),
    )(z, g_cq, g_ckv)


def _lat_bwd(z, dnq, dnkv, dkpe, g_cq, g_ckv, name):
    rows = z.shape[0]
    tr = _row_tile(rows)

    def body(z_ref, dnq_ref, dnkv_ref, dkpe_ref, gq_ref, gkv_ref, dz_ref, dgq_ref, dgkv_ref):
        first = pl.program_id(0) == 0
        zb = z_ref[...]
        dcq, dgq = _rms_bwd_rows(zb[:, 0:Q_LORA], gq_ref[...], dnq_ref[...], Q_LORA)
        dckv, dgkv = _rms_bwd_rows(zb[:, Q_LORA:Q_LORA + KV_LORA], gkv_ref[...], dnkv_ref[...], KV_LORA)
        dz_ref[:, 0:Q_LORA] = dcq.astype(BF16)
        dz_ref[:, Q_LORA:Q_LORA + KV_LORA] = dckv.astype(BF16)
        dz_ref[:, Q_LORA + KV_LORA:MLA_W] = dkpe_ref[...].astype(BF16)
        _acc_rows(dgq_ref, dgq, first)
        _acc_rows(dgkv_ref, dgkv, first)

    return pl.pallas_call(
        body, name=name, grid=(rows // tr,),
        in_specs=[_rowspec(tr, MLA_W, Z_MLA // MLA_W), _rowspec(tr, Q_LORA), _rowspec(tr, KV_LORA), _rowspec(tr, LANES),
                  _fullspec((1, Q_LORA)), _fullspec((1, KV_LORA))],
        out_specs=[_rowspec(tr, MLA_W), _fullspec((1, Q_LORA)), _fullspec((1, KV_LORA))],
        out_shape=[jax.ShapeDtypeStruct((rows, MLA_W), BF16), jax.ShapeDtypeStruct((1, Q_LORA), F32),
                   jax.ShapeDtypeStruct((1, KV_LORA), F32)],
        compiler_params=_params(("arbitrary",)),
    )(z, dnq, dnkv, dkpe, g_cq, g_ckv)


def _rope(y, cc, ss):
    return y * cc + pltpu.roll(y, 64, 1) * ss


def _rope_bwd(d, cc, ss):
    return d * cc + pltpu.roll(d * ss, 64, 1)


def _qk_fwd(q, kv, z, cc, ss, gqn, gqp, gkn, gkp, name):
    rows = q.shape[0]
    tr = _row_tile(rows)
    W = MLA_HEADS * HEAD

    def body(q_ref, kv_ref, kpe_ref, cc_ref, ss_ref, gqn_ref, gqp_ref, gkn_ref, gkp_ref, qc_ref, kc_ref, v_ref):
        cc, ss = cc_ref[...], ss_ref[...]
        kpe = kpe_ref[...]
        kp = _rope(kpe * _rms(kpe, MLA_ROPE) * gkp_ref[...], cc, ss).astype(BF16)
        for h in range(MLA_HEADS):
            qn = q_ref[:, h * HEAD:(h + 1) * HEAD]
            qp = q_ref[:, W + h * HEAD:W + (h + 1) * HEAD]
            kn = kv_ref[:, h * HEAD:(h + 1) * HEAD]
            qc_ref[:, h * QCAT:h * QCAT + HEAD] = (qn * _rms(qn, HEAD) * gqn_ref[...]).astype(BF16)
            qc_ref[:, h * QCAT + HEAD:(h + 1) * QCAT] = _rope(qp * _rms(qp, MLA_ROPE) * gqp_ref[...], cc, ss).astype(BF16)
            kc_ref[:, h * QCAT:h * QCAT + HEAD] = (kn * _rms(kn, HEAD) * gkn_ref[...]).astype(BF16)
            kc_ref[:, h * QCAT + HEAD:(h + 1) * QCAT] = kp
        v_ref[...] = kv_ref[:, W:2 * W].astype(BF16)

    g = _fullspec((1, HEAD))
    return pl.pallas_call(
        body, name=name, grid=(rows // tr,),
        in_specs=[_rowspec(tr, 2 * W), _rowspec(tr, 2 * W), _rowspec(tr, LANES, Z_KPE // LANES), _rowspec(tr, LANES),
                  _rowspec(tr, LANES), g, g, g, g],
        out_specs=[_rowspec(tr, MLA_HEADS * QCAT), _rowspec(tr, MLA_HEADS * QCAT), _rowspec(tr, W)],
        out_shape=[jax.ShapeDtypeStruct((rows, MLA_HEADS * QCAT), BF16), jax.ShapeDtypeStruct((rows, MLA_HEADS * QCAT), BF16),
                   jax.ShapeDtypeStruct((rows, W), BF16)],
        compiler_params=_params(("parallel",)),
    )(q, kv, z, cc, ss, gqn, gqp, gkn, gkp)


def _qk_bwd(q, kv, z, cc, ss, gqn, gqp, gkn, gkp, dqc, dkc, dv, name):
    rows = q.shape[0]
    tr = _row_tile(rows)
    W = MLA_HEADS * HEAD

    def body(q_ref, kv_ref, kpe_ref, cc_ref, ss_ref, gqn_ref, gqp_ref, gkn_ref, gkp_ref, dqc_ref, dkc_ref, dv_ref,
             dq_ref, dkv_ref, dkpe_ref, dgqn_ref, dgqp_ref, dgkn_ref, dgkp_ref):
        first = pl.program_id(0) == 0
        cc, ss = cc_ref[...], ss_ref[...]
        sqn = sqp = skn = dkp = None
        for h in range(MLA_HEADS):
            dx, dg = _rms_bwd_rows(q_ref[:, h * HEAD:(h + 1) * HEAD], gqn_ref[...], dqc_ref[:, h * QCAT:h * QCAT + HEAD], HEAD)
            dq_ref[:, h * HEAD:(h + 1) * HEAD] = dx.astype(BF16)
            sqn = dg if sqn is None else sqn + dg
            dy = _rope_bwd(dqc_ref[:, h * QCAT + HEAD:(h + 1) * QCAT], cc, ss)
            dx, dg = _rms_bwd_rows(q_ref[:, W + h * HEAD:W + (h + 1) * HEAD], gqp_ref[...], dy, MLA_ROPE)
            dq_ref[:, W + h * HEAD:W + (h + 1) * HEAD] = dx.astype(BF16)
            sqp = dg if sqp is None else sqp + dg
            dx, dg = _rms_bwd_rows(kv_ref[:, h * HEAD:(h + 1) * HEAD], gkn_ref[...], dkc_ref[:, h * QCAT:h * QCAT + HEAD], HEAD)
            dkv_ref[:, h * HEAD:(h + 1) * HEAD] = dx.astype(BF16)
            skn = dg if skn is None else skn + dg
            part = dkc_ref[:, h * QCAT + HEAD:(h + 1) * QCAT]
            dkp = part if dkp is None else dkp + part
        dkv_ref[:, W:2 * W] = dv_ref[...].astype(BF16)
        dx, dg = _rms_bwd_rows(kpe_ref[...], gkp_ref[...], _rope_bwd(dkp, cc, ss), MLA_ROPE)
        dkpe_ref[...] = dx
        _acc_rows(dgqn_ref, sqn, first)
        _acc_rows(dgqp_ref, sqp, first)
        _acc_rows(dgkn_ref, skn, first)
        _acc_rows(dgkp_ref, dg, first)

    g = _fullspec((1, HEAD))
    gs = jax.ShapeDtypeStruct((1, HEAD), F32)
    return pl.pallas_call(
        body, name=name, grid=(rows // tr,),
        in_specs=[_rowspec(tr, 2 * W), _rowspec(tr, 2 * W), _rowspec(tr, LANES, Z_KPE // LANES), _rowspec(tr, LANES),
                  _rowspec(tr, LANES), g, g, g, g, _rowspec(tr, MLA_HEADS * QCAT), _rowspec(tr, MLA_HEADS * QCAT),
                  _rowspec(tr, W)],
        out_specs=[_rowspec(tr, 2 * W), _rowspec(tr, 2 * W), _rowspec(tr, LANES), g, g, g, g],
        out_shape=[jax.ShapeDtypeStruct((rows, 2 * W), BF16), jax.ShapeDtypeStruct((rows, 2 * W), BF16),
                   jax.ShapeDtypeStruct((rows, LANES), F32), gs, gs, gs, gs],
        compiler_params=_params(("arbitrary",)),
    )(q, kv, z, cc, ss, gqn, gqp, gkn, gkp, dqc, dkc, dv)


def _headnorm_fwd(x, col, nheads, g, name):
    rows = x.shape[0]
    tr = _row_tile(rows)
    W = nheads * HEAD

    def body(x_ref, g_ref, o_ref):
        for h in range(nheads):
            xv = x_ref[:, h * HEAD:(h + 1) * HEAD]
            o_ref[:, h * HEAD:(h + 1) * HEAD] = (xv * _rms(xv, HEAD) * g_ref[...]).astype(BF16)

    return pl.pallas_call(
        body, name=name, grid=(rows // tr,),
        in_specs=[_rowspec(tr, W, col), _fullspec((1, HEAD))], out_specs=_rowspec(tr, W),
        out_shape=jax.ShapeDtypeStruct((rows, W), BF16), compiler_params=_params(("parallel",)),
    )(x, g)


def _headnorm_bwd(x, col, nheads, g, dy, tail, name):
    rows = x.shape[0]
    tr = _row_tile(rows)
    W = nheads * HEAD
    has_tail = tail is not None
    WO = 2 * W if has_tail else W

    def body(*refs):
        if has_tail:
            x_ref, g_ref, dy_ref, t_ref, dx_ref, dg_ref = refs
        else:
            x_ref, g_ref, dy_ref, dx_ref, dg_ref = refs
        acc = None
        for h in range(nheads):
            sl = slice(h * HEAD, (h + 1) * HEAD)
            dx, dg = _rms_bwd_rows(x_ref[:, sl], g_ref[...], dy_ref[:, sl], HEAD)
            dx_ref[:, sl] = dx.astype(BF16)
            acc = dg if acc is None else acc + dg
        if has_tail:
            dx_ref[:, W:2 * W] = t_ref[...].astype(BF16)
        _acc_rows(dg_ref, acc, pl.program_id(0) == 0)

    ins = [x, g, dy] + ([tail] if has_tail else [])
    specs = [_rowspec(tr, W, col), _fullspec((1, HEAD)), _rowspec(tr, W)] + ([_rowspec(tr, W)] if has_tail else [])
    return pl.pallas_call(
        body, name=name, grid=(rows // tr,), in_specs=specs,
        out_specs=[_rowspec(tr, WO), _fullspec((1, HEAD))],
        out_shape=[jax.ShapeDtypeStruct((rows, WO), BF16), jax.ShapeDtypeStruct((1, HEAD), F32)],
        compiler_params=_params(("arbitrary",)),
    )(*ins)


def _sigmoid(x):
    return 1.0 / (1.0 + jnp.exp(-x))


def _merge_fwd(z, y_gm, y_mla, y_mem, name):
    rows = z.shape[0]
    tr = _row_tile(rows)

    def body(g0_ref, g1_ref, g2_ref, a_ref, b_ref, c_ref, o_ref):
        m = _sigmoid(g0_ref[...]) * a_ref[...] + _sigmoid(g1_ref[...]) * b_ref[...] + _sigmoid(g2_ref[...]) * c_ref[...]
        o_ref[...] = m.astype(BF16)

    r = _rowspec(tr, D_MODEL)
    return pl.pallas_call(
        body, name=name, grid=(rows // tr,),
        in_specs=[_rowspec(tr, D_MODEL, 1), _rowspec(tr, D_MODEL, 2), _rowspec(tr, D_MODEL, 3), r, r, r],
        out_specs=r, out_shape=jax.ShapeDtypeStruct((rows, D_MODEL), BF16), compiler_params=_params(("parallel",)),
    )(z, z, z, y_gm, y_mla, y_mem)


def _merge_bwd(z, y_gm, y_mla, y_mem, dm, name):
    rows = z.shape[0]
    tr = _row_tile(rows)

    def body(g0_ref, g1_ref, g2_ref, a_ref, b_ref, c_ref, dm_ref, da_ref, db_ref, dc_ref, dzg_ref):
        dmv = dm_ref[...]
        for k, (g_ref, y_ref, dy_ref) in enumerate(((g0_ref, a_ref, da_ref), (g1_ref, b_ref, db_ref), (g2_ref, c_ref, dc_ref))):
            s = _sigmoid(g_ref[...])
            dy_ref[...] = (dmv * s).astype(BF16)
            dzg_ref[:, k * D_MODEL:(k + 1) * D_MODEL] = (dmv * y_ref[...] * s * (1.0 - s)).astype(BF16)

    r = _rowspec(tr, D_MODEL)
    o = jax.ShapeDtypeStruct((rows, D_MODEL), BF16)
    return pl.pallas_call(
        body, name=name, grid=(rows // tr,),
        in_specs=[_rowspec(tr, D_MODEL, 1), _rowspec(tr, D_MODEL, 2), _rowspec(tr, D_MODEL, 3), r, r, r, r],
        out_specs=[r, r, r, _rowspec(tr, 3 * D_MODEL)],
        out_shape=[o, o, o, jax.ShapeDtypeStruct((rows, 3 * D_MODEL), BF16)],
        compiler_params=_params(("parallel",)),
    )(z, z, z, y_gm, y_mla, y_mem, dm)


def _loss_head(y, target, name):
    rows, width = y.shape
    tr = _row_tile(rows)

    def body(y_ref, t_ref, dy_ref, l_ref):
        e = y_ref[...] - t_ref[...]
        dy_ref[...] = e * (1.0 / width)
        e2 = e * e
        part = e2[:, 0:LANES]
        for k in range(1, width // LANES):
            part = part + e2[:, k * LANES:(k + 1) * LANES]
        _acc_rows(l_ref, part, pl.program_id(0) == 0)

    return pl.pallas_call(
        body, name=name, grid=(rows // tr,),
        in_specs=[_rowspec(tr, width), _rowspec(tr, width)],
        out_specs=[_rowspec(tr, width), _fullspec((1, LANES))],
        out_shape=[jax.ShapeDtypeStruct((rows, width), F32), jax.ShapeDtypeStruct((1, LANES), F32)],
        compiler_params=_params(("arbitrary",)),
    )(y, target)


_NT = (((1,), (1,)), ((), ()))
_TN = (((0,), (0,)), ((), ()))


def _causal_mask(s, i, j, tq, tk):
    row = lax.broadcasted_iota(jnp.int32, (tq, tk), 0) + i * tq
    col = lax.broadcasted_iota(jnp.int32, (tq, tk), 1) + j * tk
    return jnp.where(row >= col, s, NEG)


def _attn_fwd(q, k, v, nb, nheads, dk, v_col0, scale, causal, name):
    S, Skv = q.shape[0] // nb, k.shape[0] // nb
    tq, tk = _pick(S, 256), _pick(Skv, 256)
    nq, nkv = S // tq, Skv // tk

    def body(q_ref, k_ref, v_ref, o_ref, lse_ref):
        def q_block(i, carry):
            qs = pl.ds(pl.multiple_of(i * tq, tq), tq)
            qb = q_ref[qs, :].astype(BF16)

            def kv_block(j, st):
                m, l, acc = st
                ks = pl.ds(pl.multiple_of(j * tk, tk), tk)
                s = lax.dot_general(qb, k_ref[ks, :].astype(BF16), _NT, preferred_element_type=F32) * scale
                if causal:
                    s = _causal_mask(s, i, j, tq, tk)
                m2 = jnp.maximum(m, jnp.max(s, axis=-1, keepdims=True))
                alpha = jnp.exp(m - m2)
                p = jnp.exp(s - m2)
                l2 = alpha * l + jnp.sum(p, axis=-1, keepdims=True)
                acc2 = alpha * acc + jnp.dot(p.astype(BF16), v_ref[ks, :].astype(BF16), preferred_element_type=F32)
                return m2, l2, acc2

            init = (jnp.full((tq, 1), NEG, F32), jnp.zeros((tq, 1), F32), jnp.zeros((tq, HEAD), F32))
            m, l, acc = lax.fori_loop(0, (i + 1) if causal else nkv, kv_block, init)
            o_ref[qs, :] = acc / l
            lse_ref[qs, :] = m + jnp.log(l)
            return carry

        lax.fori_loop(0, nq, q_block, 0)

    return pl.pallas_call(
        body, name=name, grid=(nb, nheads),
        in_specs=[pl.BlockSpec((S, dk), lambda b, h: (b, h)), pl.BlockSpec((Skv, dk), lambda b, h: (b, h)),
                  pl.BlockSpec((Skv, HEAD), lambda b, h: (b, v_col0 + h))],
        out_specs=[pl.BlockSpec((S, HEAD), lambda b, h: (b, h)), pl.BlockSpec((None, S, 1), lambda b, h: (h, b, 0))],
        out_shape=[jax.ShapeDtypeStruct((nb * S, nheads * HEAD), F32), jax.ShapeDtypeStruct((nheads, nb * S, 1), F32)],
        compiler_params=_params(("parallel", "parallel")),
    )(q, k, v)


def _attn_bwd(q, k, v, o, do, lse, nb, nheads, dk, v_col0, scale, causal, name):
    S, Skv = q.shape[0] // nb, k.shape[0] // nb
    tq, tk = _pick(S, 256), _pick(Skv, 256)
    nq, nkv = S // tq, Skv // tk

    def body(q_ref, k_ref, v_ref, o_ref, do_ref, lse_ref, dq_ref, dk_ref, dv_ref, delta_ref):
        delta_ref[...] = jnp.sum(o_ref[...] * do_ref[...], axis=-1, keepdims=True)
        dq_ref[...] = jnp.zeros((S, dk), F32)

        def kv_block(j, carry):
            ks = pl.ds(pl.multiple_of(j * tk, tk), tk)
            kb = k_ref[ks, :].astype(BF16)
            vb = v_ref[ks, :].astype(BF16)

            def q_block(i, st):
                dk_acc, dv_acc = st
                qs = pl.ds(pl.multiple_of(i * tq, tq), tq)
                qb = q_ref[qs, :].astype(BF16)
                dob = do_ref[qs, :].astype(BF16)
                s = lax.dot_general(qb, kb, _NT, preferred_element_type=F32) * scale
                if causal:
                    s = _causal_mask(s, i, j, tq, tk)
                p = jnp.exp(s - lse_ref[qs, :])
                dp = lax.dot_general(dob, vb, _NT, preferred_element_type=F32)
                ds = (p * (dp - delta_ref[qs, :]) * scale).astype(BF16)
                dv_acc = dv_acc + lax.dot_general(p.astype(BF16), dob, _TN, preferred_element_type=F32)
                dk_acc = dk_acc + lax.dot_general(ds, qb, _TN, preferred_element_type=F32)
                dq_ref[qs, :] += jnp.dot(ds, kb, preferred_element_type=F32)
                return dk_acc, dv_acc

            init = (jnp.zeros((tk, dk), F32), jnp.zeros((tk, HEAD), F32))
            dk_acc, dv_acc = lax.fori_loop(j if causal else 0, nq, q_block, init)
            dk_ref[ks, :] = dk_acc
            dv_ref[ks, :] = dv_acc
            return carry

        lax.fori_loop(0, nkv, kv_block, 0)

    return pl.pallas_call(
        body, name=name, grid=(nb, nheads),
        in_specs=[pl.BlockSpec((S, dk), lambda b, h: (b, h)), pl.BlockSpec((Skv, dk), lambda b, h: (b, h)),
                  pl.BlockSpec((Skv, HEAD), lambda b, h: (b, v_col0 + h)), pl.BlockSpec((S, HEAD), lambda b, h: (b, h)),
                  pl.BlockSpec((S, HEAD), lambda b, h: (b, h)), pl.BlockSpec((None, S, 1), lambda b, h: (h, b, 0))],
        out_specs=[pl.BlockSpec((S, dk), lambda b, h: (b, h)), pl.BlockSpec((Skv, dk), lambda b, h: (b, h)),
                   pl.BlockSpec((Skv, HEAD), lambda b, h: (b, h))],
        out_shape=[jax.ShapeDtypeStruct((nb * S, nheads * dk), F32), jax.ShapeDtypeStruct((nb * Skv, nheads * dk), F32),
                   jax.ShapeDtypeStruct((nb * Skv, nheads * HEAD), F32)],
        scratch_shapes=[pltpu.VMEM((S, 1), F32)],
        compiler_params=_params(("parallel", "parallel")),
    )(q, k, v, o, do, lse)


def _spread_rope(a):
    zero = jnp.zeros(a.shape[:-1] + (32,), a.dtype)
    return jnp.concatenate([a[..., :32], zero, a[..., 32:], zero], axis=-1)


def _gather_rope(a):
    return jnp.concatenate([a[..., 0:32], a[..., 64:96]], axis=-1)


def _win_layout(w):
    return jnp.concatenate([w[:, C_ZU:C_CQ], w[:, C_ZG:C_END], w[:, C_QM:C_ZG], w[:, C_CQ:C_CKV], w[:, C_CKV:C_KPE],
                            _spread_rope(w[:, C_KPE:C_QM])], axis=1)


def _win_unlayout(d):
    return jnp.concatenate([d[:, 0:Z_GATE], d[:, Z_MLA:Z_MLA + Q_LORA], d[:, Z_MLA + Q_LORA:Z_KPE],
                            _gather_rope(d[:, Z_KPE:Z_COLS]), d[:, Z_QM:Z_MLA], d[:, Z_GATE:Z_QM]], axis=1)


def _wuq_layout(w):
    r = w.reshape(Q_LORA, MLA_HEADS, HEAD + MLA_ROPE)
    return jnp.concatenate([r[:, :, :HEAD].reshape(Q_LORA, -1), _spread_rope(r[:, :, HEAD:]).reshape(Q_LORA, -1)], axis=1)


def _wuq_unlayout(d):
    n = d[:, :MLA_HEADS * HEAD].reshape(Q_LORA, MLA_HEADS, HEAD)
    p = _gather_rope(d[:, MLA_HEADS * HEAD:].reshape(Q_LORA, MLA_HEADS, HEAD))
    return jnp.concatenate([n, p], axis=-1).reshape(Q_LORA, -1)


def _wukv_layout(w):
    r = w.reshape(KV_LORA, MLA_HEADS, 2 * HEAD)
    return jnp.concatenate([r[:, :, :HEAD].reshape(KV_LORA, -1), r[:, :, HEAD:].reshape(KV_LORA, -1)], axis=1)


def _wukv_unlayout(d):
    k = d[:, :MLA_HEADS * HEAD].reshape(KV_LORA, MLA_HEADS, HEAD)
    v = d[:, MLA_HEADS * HEAD:].reshape(KV_LORA, MLA_HEADS, HEAD)
    return jnp.concatenate([k, v], axis=-1).reshape(KV_LORA, -1)


def _local_step(x, mem, positions, target, W, P):
    B, S, _ = x.shape
    M = mem.shape[1]
    T = B * S
    x2d = x.reshape(T, D_MODEL)
    mem2d = mem.reshape(B * M, D_MODEL)
    tgt2d = target.reshape(T, D_MODEL)

    def row(v):
        return v.reshape(1, -1).astype(F32)

    inv_freq = ROPE_BASE ** (-jnp.arange(0, MLA_ROPE, 2, dtype=F32) / MLA_ROPE)
    ang = positions.reshape(T).astype(F32)[:, None] * inv_freq
    cos, sin, zero = jnp.cos(ang), jnp.sin(ang), jnp.zeros_like(ang)
    cc = jnp.concatenate([cos, zero, cos, zero], axis=1)
    ss = jnp.concatenate([-sin, zero, sin, zero], axis=1)

    w_in = _win_layout(W['w_in']).astype(BF16)
    w_uq = _wuq_layout(W['w_uq']).astype(BF16)
    w_ukv = _wukv_layout(W['w_ukv']).astype(BF16)
    w_mem_kv, w_o_gm, w_o_mla, w_o_mem = (W[n].astype(BF16) for n in ('w_mem_kv', 'w_o_gm', 'w_o_mla', 'w_o_mem'))
    w_out, w_ff1, w_ff2 = (W[n].astype(BF16) for n in ('w_out', 'w_ff1', 'w_ff2'))
    g_mix, g_cq, g_ckv, g_ffn, g_mem = row(P['g_mix']), row(P['g_cq']), row(P['g_ckv']), row(P['g_ffn']), row(P['g_mem'])
    gqn, gkn, gmq, gmk = row(P['g_q_nope']), row(P['g_k_nope']), row(P['g_mq']), row(P['g_mk'])
    gqp, gkp = _spread_rope(row(P['g_q_pe'])), _spread_rope(row(P['g_k_pe']))
    gln, bln = row(P['g_gm_ln']), row(P['b_gm_ln'])
    wc = jnp.tril(P['w_spatial'].astype(F32))
    wct = jnp.swapaxes(wc, 1, 2).astype(BF16)
    wc = wc.astype(BF16)
    bst = jnp.broadcast_to(P['b_spatial'].astype(F32)[:, :, None], (GM_GROUPS, GM_CHUNK, LANES))

    h = _rms_fwd(x2d, g_mix, "rms_mix")
    z = _matmul(h, w_in, 'nn', F32, "mm_in", tn_t=768)
    ygm_pre = _gm_fwd(z, gln, bln, wc, bst, "gm_fwd")
    y_gm = _matmul(ygm_pre, w_o_gm, 'nn', F32, "mm_o_gm")
    nq, nkv = _lat_fwd(z, g_cq, g_ckv, "lat_fwd")
    q = _matmul(nq, w_uq, 'nn', F32, "mm_uq")
    kv = _matmul(nkv, w_ukv, 'nn', F32, "mm_ukv")
    qcat, kcat, vv = _qk_fwd(q, kv, z, cc, ss, gqn, gqp, gkn, gkp, "qk_fwd")
    o, lse = _attn_fwd(qcat, kcat, vv, B, MLA_HEADS, QCAT, 0, MLA_SCALE, True, "mla_attn_fwd")
    y_mla = _matmul(o, w_o_mla, 'nn', F32, "mm_o_mla")
    nm = _rms_fwd(mem2d, g_mem, "rms_mem")
    kvm = _matmul(nm, w_mem_kv, 'nn', F32, "mm_mem_kv")
    qm = _headnorm_fwd(z, Z_QM // (MEM_HEADS * HEAD), MEM_HEADS, gmq, "memq_fwd")
    km = _headnorm_fwd(kvm, 0, MEM_HEADS, gmk, "memk_fwd")
    om, lse_m = _attn_fwd(qm, km, kvm, B, MEM_HEADS, HEAD, MEM_HEADS, MEM_SCALE, False, "mem_attn_fwd")
    y_mem = _matmul(om, w_o_mem, 'nn', F32, "mm_o_mem")
    merged = _merge_fwd(z, y_gm, y_mla, y_mem, "merge_fwd")
    x1 = _matmul(merged, w_out, 'nn', F32, "mm_out", add=x2d)
    h2 = _rms_fwd(x1, g_ffn, "rms_ffn")
    a1 = _matmul(h2, w_ff1, 'nn', BF16, "mm_ff1")
    x2 = _matmul(a1, w_ff2, 'nn', F32, "mm_ff2", add=x1, relu2_a=True)
    dx2, loss_part = _loss_head(x2, tgt2d, "loss_head")

    G = {}
    G['w_ff2'] = _matmul(a1, dx2, 'tn', F32, "mm_d_ff2", relu2_a=True)
    da1 = _matmul(dx2, w_ff2, 'nt', BF16, "mm_da1", relu2_grad=a1)
    G['w_ff1'] = _matmul(h2, da1, 'tn', F32, "mm_d_ff1")
    dh2 = _matmul(da1, w_ff1, 'nt', F32, "mm_dh2")
    dx1, G['g_ffn'] = _rms_bwd(x1, g_ffn, dh2, dx2, "rms_ffn_bwd")
    G['w_out'] = _matmul(merged, dx1, 'tn', F32, "mm_d_out")
    dmerged = _matmul(dx1, w_out, 'nt', F32, "mm_dmerged")
    dy_gm, dy_mla, dy_mem, dzg = _merge_bwd(z, y_gm, y_mla, y_mem, dmerged, "merge_bwd")
    G['w_o_gm'] = _matmul(ygm_pre, dy_gm, 'tn', F32, "mm_d_o_gm")
    dygm_pre = _matmul(dy_gm, w_o_gm, 'nt', F32, "mm_dygm")
    dz_gm, dws, dbs, G['g_gm_ln'], G['b_gm_ln'] = _gm_bwd(z, dygm_pre, gln, bln, wc, wct, bst, "gm_bwd")
    G['w_spatial'] = jnp.tril(dws)
    G['b_spatial'] = jnp.sum(dbs.reshape(GM_CHUNK, GM_GROUPS, LANES), axis=-1).T
    G['w_o_mla'] = _matmul(o, dy_mla, 'tn', F32, "mm_d_o_mla")
    do = _matmul(dy_mla, w_o_mla, 'nt', F32, "mm_do")
    dqc, dkc, dvv = _attn_bwd(qcat, kcat, vv, o, do, lse, B, MLA_HEADS, QCAT, 0, MLA_SCALE, True, "mla_attn_bwd")
    dq, dkv, dkpe, G['g_q_nope'], dgqp, G['g_k_nope'], dgkp = _qk_bwd(q, kv, z, cc, ss, gqn, gqp, gkn, gkp, dqc, dkc, dvv,
                                                                     "qk_bwd")
    G['g_q_pe'], G['g_k_pe'] = _gather_rope(dgqp), _gather_rope(dgkp)
    G['w_uq'] = _wuq_unlayout(_matmul(nq, dq, 'tn', F32, "mm_d_uq"))
    dnq = _matmul(dq, w_uq, 'nt', F32, "mm_dnq")
    G['w_ukv'] = _wukv_unlayout(_matmul(nkv, dkv, 'tn', F32, "mm_d_ukv"))
    dnkv = _matmul(dkv, w_ukv, 'nt', F32, "mm_dnkv")
    dz_mla, G['g_cq'], G['g_ckv'] = _lat_bwd(z, dnq, dnkv, dkpe, g_cq, g_ckv, "lat_bwd")
    G['w_o_mem'] = _matmul(om, dy_mem, 'tn', F32, "mm_d_o_mem")
    dom = _matmul(dy_mem, w_o_mem, 'nt', F32, "mm_dom")
    dqm, dkm, dvm = _attn_bwd(qm, km, kvm, om, dom, lse_m, B, MEM_HEADS, HEAD, MEM_HEADS, MEM_SCALE, False, "mem_attn_bwd")
    dz_qm, G['g_mq'] = _headnorm_bwd(z, Z_QM // (MEM_HEADS * HEAD), MEM_HEADS, gmq, dqm, None, "memq_bwd")
    dkvm, G['g_mk'] = _headnorm_bwd(kvm, 0, MEM_HEADS, gmk, dkm, dvm, "memk_bwd")
    G['w_mem_kv'] = _matmul(nm, dkvm, 'tn', F32, "mm_d_mem_kv")
    dnm = _matmul(dkvm, w_mem_kv, 'nt', F32, "mm_dnm")
    _, G['g_mem'] = _rms_bwd(mem2d, g_mem, dnm, None, "rms_mem_bwd")
    dz = jnp.concatenate([dz_gm, dzg, dz_qm, dz_mla], axis=1)
    G['w_in'] = _win_unlayout(_matmul(h, dz, 'tn', F32, "mm_d_in", tn_t=768))
    dh = _matmul(dz, w_in, 'nt', F32, "mm_dh", tk_t=768)
    gx, G['g_mix'] = _rms_bwd(x2d, g_mix, dh, dx1, "rms_mix_bwd")
    return loss_part, gx.reshape(B, S, D_MODEL), G


MESH = pl.DeviceIdType.MESH
HBM_SPEC = pl.BlockSpec(memory_space=pltpu.HBM)


def _all_gather8(xs, name):
    def body(x_ref, out_ref, send_sems, recv_sems, local_sem):
        x, y, c = lax.axis_index("x"), lax.axis_index("y"), lax.axis_index("c")
        me, sibling = (x, y, c), (x, y, 1 - c)
        chips = [(1 - x, y), (x, 1 - y), (1 - x, 1 - y)]

        def rows(px, py, pc):
            return out_ref.at[4 * px + 2 * py + pc]

        def copy(k, block, to, src=None):
            return pltpu.make_async_remote_copy(
                src_ref=rows(*block) if src is None else src, dst_ref=rows(*block),
                send_sem=send_sems.at[k], recv_sem=recv_sems.at[k], device_id=to, device_id_type=MESH)

        mine = pltpu.make_async_copy(x_ref, rows(*me), local_sem)
        mine.start()
        first = [copy(0, me, sibling, src=x_ref)]
        first += [copy(1 + j, me, (*chip, c), src=x_ref) for j, chip in enumerate(chips)]
        for cp in first:
            cp.start()
        passed = [copy(4 + j, (*chip, c), sibling) for j, chip in enumerate(chips)]
        for j, chip in enumerate(chips):
            copy(1 + j, (*chip, c), me).wait_recv()
            passed[j].start()
        copy(0, sibling, me).wait_recv()
        for j, chip in enumerate(chips):
            copy(4 + j, (*chip, 1 - c), me).wait_recv()
        for cp in first + passed:
            cp.wait_send()
        mine.wait()

    return pl.pallas_call(
        body, name=name, in_specs=[HBM_SPEC], out_specs=HBM_SPEC,
        out_shape=jax.ShapeDtypeStruct((N_DEV,) + xs.shape, xs.dtype),
        scratch_shapes=[pltpu.SemaphoreType.DMA((7,)), pltpu.SemaphoreType.DMA((7,)), pltpu.SemaphoreType.DMA],
    )(xs)


def _swap_sibling(xs, name):
    def body(x_ref, out_ref, send_sem, recv_sem):
        x, y, c = lax.axis_index("x"), lax.axis_index("y"), lax.axis_index("c")
        cp = pltpu.make_async_remote_copy(src_ref=x_ref, dst_ref=out_ref, send_sem=send_sem, recv_sem=recv_sem,
                                          device_id=(x, y, 1 - c), device_id_type=MESH)
        cp.start()
        cp.wait()

    return pl.pallas_call(
        body, name=name, in_specs=[HBM_SPEC], out_specs=HBM_SPEC, out_shape=jax.ShapeDtypeStruct(xs.shape, xs.dtype),
        scratch_shapes=[pltpu.SemaphoreType.DMA, pltpu.SemaphoreType.DMA],
    )(xs)


def _swap_chips(xs, name):
    def body(x_ref, out_ref, send_sems, recv_sems):
        x, y, c = lax.axis_index("x"), lax.axis_index("y"), lax.axis_index("c")
        chips = [(1 - x, y), (x, 1 - y), (1 - x, 1 - y)]
        cps = [pltpu.make_async_remote_copy(src_ref=x_ref.at[2 * px + py], dst_ref=out_ref.at[k], send_sem=send_sems.at[k],
                                            recv_sem=recv_sems.at[k], device_id=(px, py, c), device_id_type=MESH)
               for k, (px, py) in enumerate(chips)]
        for cp in cps:
            cp.start()
        for cp in cps:
            cp.wait()

    return pl.pallas_call(
        body, name=name, in_specs=[HBM_SPEC], out_specs=HBM_SPEC,
        out_shape=jax.ShapeDtypeStruct((3,) + xs.shape[1:], xs.dtype),
        scratch_shapes=[pltpu.SemaphoreType.DMA((3,)), pltpu.SemaphoreType.DMA((3,))],
    )(xs)


def _add2(a, b, name):
    lead, rows = a.shape[0], a.shape[1]
    tr = _pick(rows, 1024, 8)

    def body(a_ref, b_ref, o_ref):
        o_ref[...] = a_ref[...] + b_ref[...]

    spec = pl.BlockSpec((None, tr, LANES), lambda k, i: (k, i, 0))
    return pl.pallas_call(
        body, name=name, grid=(lead, rows // tr), in_specs=[spec, spec], out_specs=spec,
        out_shape=jax.ShapeDtypeStruct(a.shape, F32), compiler_params=_params(("parallel", "parallel")),
    )(a, b)


def _adamw_rows(w, g, m, v):
    m2 = ADAM_B1 * m + (1.0 - ADAM_B1) * g
    v2 = ADAM_B2 * v + (1.0 - ADAM_B2) * (g * g)
    m_hat = m2 / (1.0 - ADAM_B1 ** ADAM_STEP)
    v_hat = v2 / (1.0 - ADAM_B2 ** ADAM_STEP)
    delta = -ADAM_LR * (m_hat / (jnp.sqrt(v_hat) + ADAM_EPS) + ADAM_WD * w)
    return delta, m2, v2


def _sum_adamw(parts, w, m, v, name):
    rows = w.shape[0]
    tr = _pick(rows, 512, 8)
    counts = [1 if p.ndim == 2 else p.shape[0] for p in parts]

    def body(*refs):
        p_refs = refs[:len(parts)]
        w_ref, m_ref, v_ref, g_ref, d_ref, m2_ref, v2_ref = refs[len(parts):]
        g = None
        for ref, n in zip(p_refs, counts):
            for k in range(n):
                val = ref[...] if ref.shape == (tr, LANES) else ref[k]
                g = val if g is None else g + val
        delta, m2, v2 = _adamw_rows(w_ref[...], g, m_ref[...], v_ref[...])
        g_ref[...] = g
        d_ref[...] = delta
        m2_ref[...] = m2
        v2_ref[...] = v2

    flat = pl.BlockSpec((tr, LANES), lambda i: (i, 0))
    specs = [flat if p.ndim == 2 else pl.BlockSpec((p.shape[0], tr, LANES), lambda i: (0, i, 0)) for p in parts]
    out = jax.ShapeDtypeStruct((rows, LANES), F32)
    return pl.pallas_call(
        body, name=name, grid=(rows // tr,), in_specs=specs + [flat, flat, flat], out_specs=[flat] * 4,
        out_shape=[out] * 4, compiler_params=_params(("parallel",)),
    )(*parts, w, m, v)


def _to_slab(parts):
    return jnp.concatenate([p.reshape(-1, LANES) for p in parts], axis=0)


def _small_rows(name):
    n = {'g_mix': 1024, 'g_cq': 384, 'g_ckv': 256, 'g_q_nope': 128, 'g_q_pe': 64, 'g_k_nope': 128, 'g_k_pe': 64,
         'g_gm_ln': 512, 'b_gm_ln': 512, 'w_spatial': GM_GROUPS * GM_CHUNK * GM_CHUNK, 'b_spatial': GM_GROUPS * GM_CHUNK,
         'g_mem': 1024, 'g_mq': 128, 'g_mk': 128, 'g_ffn': 1024}[name]
    return n, -(-n // LANES)


def _small_slab(d):
    parts = []
    for name in SMALL:
        n, rows = _small_rows(name)
        parts.append(jnp.pad(d[name].reshape(-1).astype(F32), (0, rows * LANES - n)).reshape(rows, LANES))
    slab = jnp.concatenate(parts, axis=0)
    return jnp.pad(slab, ((0, -slab.shape[0] % 8), (0, 0)))


def _small_unslab(slab, like):
    out, r = {}, 0
    for name in SMALL:
        n, rows = _small_rows(name)
        out[name] = slab[r:r + rows].reshape(-1)[:n].reshape(like[name].shape)
        r += rows
    return out


def _shard_rows(name):
    r, c = BIG_SHAPE[name]
    return r * c // N_DEV // LANES


def _full_from_gathered(gathered, name, r0):
    r, c = BIG_SHAPE[name]
    n = _shard_rows(name)
    blk = gathered[:, r0:r0 + n]
    if BIG_AXIS[name] == 0:
        return blk.reshape(r, c)
    return blk.reshape(N_DEV, r, c // N_DEV).transpose(1, 0, 2).reshape(r, c)


def _shards_of_full(g, name):
    r, c = BIG_SHAPE[name]
    if BIG_AXIS[name] == 0:
        return g.reshape(N_DEV, -1, LANES)
    return g.reshape(r, N_DEV, c // N_DEV).transpose(1, 0, 2).reshape(N_DEV, -1, LANES)


def kernel(x, mem, positions, g_mix, w_in, g_cq, w_uq, g_ckv, w_ukv, g_q_nope, g_q_pe, g_k_nope, g_k_pe, g_gm_ln, b_gm_ln, w_spatial, b_spatial, g_mem, w_mem_kv, g_mq, g_mk, w_o_gm, w_o_mla, w_o_mem, w_out, g_ffn, w_ff1, w_ff2, loss_target, m_g_mix, m_w_in, m_g_cq, m_w_uq, m_g_ckv, m_w_ukv, m_g_q_nope, m_g_q_pe, m_g_k_nope, m_g_k_pe, m_g_gm_ln, m_b_gm_ln, m_w_spatial, m_b_spatial, m_g_mem, m_w_mem_kv, m_g_mq, m_g_mk, m_w_o_gm, m_w_o_mla, m_w_o_mem, m_w_out, m_g_ffn, m_w_ff1, m_w_ff2, v_g_mix, v_w_in, v_g_cq, v_w_uq, v_g_ckv, v_w_ukv, v_g_q_nope, v_g_q_pe, v_g_k_nope, v_g_k_pe, v_g_gm_ln, v_b_gm_ln, v_w_spatial, v_b_spatial, v_g_mem, v_w_mem_kv, v_g_mq, v_g_mk, v_w_o_gm, v_w_o_mla, v_w_o_mem, v_w_out, v_g_ffn, v_w_ff1, v_w_ff2):
    given = dict(locals())
    w = {n: given[n][0] for n in WEIGHTS}
    mom = {n: given['m_' + n][0] for n in WEIGHTS}
    var = {n: given['v_' + n][0] for n in WEIGHTS}
    cx, cy, cc_ = lax.axis_index("x"), lax.axis_index("y"), lax.axis_index("c")

    gathered = _all_gather8(_to_slab([w[n].astype(BF16) for n in BIG]), "ag_weights")
    full, r0 = {}, 0
    for n in BIG:
        full[n] = _full_from_gathered(gathered, n, r0)
        r0 += _shard_rows(n)

    loss_part, grad_x, G = _local_step(x, mem, positions, loss_target, full, {n: w[n] for n in SMALL})
    loss = lax.psum(0.5 * jnp.sum(loss_part) / D_MODEL, ("x", "y", "c"))

    shards = jnp.concatenate([_shards_of_full(G[n], n) for n in BIG], axis=1)
    R = shards.shape[1]
    by_chip = shards.reshape(4, 2, R, LANES)
    keep = lax.dynamic_index_in_dim(by_chip, cc_, axis=1, keepdims=False)
    give = lax.dynamic_index_in_dim(by_chip, 1 - cc_, axis=1, keepdims=False)
    chip_sum = _add2(keep, _swap_sibling(give, "rs_sibling"), "rs_add")
    from_chips = _swap_chips(chip_sum, "rs_chips")
    own = lax.dynamic_index_in_dim(chip_sum, 2 * cx + cy, axis=0, keepdims=False)
    g_big, d_big, m_big, v_big = _sum_adamw([own, from_chips], _to_slab([w[n] for n in BIG]), _to_slab([mom[n] for n in BIG]),
                                            _to_slab([var[n] for n in BIG]), "adamw_big")

    parts = _all_gather8(_small_slab(G), "ag_small")
    g_sm, d_sm, m_sm, v_sm = _sum_adamw([parts], _small_slab(w), _small_slab(mom), _small_slab(var), "adamw_small")

    outs = {}
    for prefix, big_slab, small_slab in (("grad_", g_big, g_sm), ("delta_", d_big, d_sm), ("new_m_", m_big, m_sm),
                                         ("new_v_", v_big, v_sm)):
        r0 = 0
        for n in BIG:
            k = _shard_rows(n)
            outs[prefix + n] = big_slab[r0:r0 + k].reshape(given[n].shape)
            r0 += k
        sm = _small_unslab(small_slab, given)
        for n in SMALL:
            outs[prefix + n] = sm[n]
    return (loss, grad_x, *[outs[p + n] for p in ("grad_", "delta_", "new_m_", "new_v_") for n in WEIGHTS])
```

```python
import functools
import math

import jax
import jax.numpy as jnp
from jax import lax
from jax.experimental import pallas as pl
from jax.experimental.pallas import tpu as pltpu

F32 = jnp.float32
BF16 = jnp.bfloat16

D_MODEL = 1024
MEM_HEADS = 4
HEAD = 128
GM_WIDTH = 512
GM_CHUNK = 128
GM_GROUPS = 4
MLA_HEADS = 8
MLA_ROPE = 64
Q_LORA = 384
KV_LORA = 256
D_FF = 4096
EPS = 1e-6
ROPE_BASE = 10000.0
MLA_SCALE = 1.0 / math.sqrt(HEAD + MLA_ROPE)
MEM_SCALE = 1.0 / math.sqrt(HEAD)
LOG2E = 1.4426950408889634
LN2 = 0.6931471805599453
ATT_TILE = 256
C_ZU, C_ZV, C_CQ, C_CKV, C_KPE, C_QM, C_ZG, C_END = 0, 512, 1024, 1408, 1664, 1728, 2240, 5312
Z_GATE, Z_QM, Z_MLA, Z_KPE, Z_COLS = 1024, 4096, 4608, 5248, 5376
MLA_W = 768
QCAT = 2 * HEAD
ADAM_LR, ADAM_B1, ADAM_B2, ADAM_EPS, ADAM_WD, ADAM_STEP = 0.001, 0.9, 0.999, 1e-08, 0.01, 10
N_DEV = 8
LANES = 128
VMEM_LIMIT = 48 * 1024 * 1024
NEG = -1e30

BIG = ['w_in', 'w_uq', 'w_ukv', 'w_mem_kv', 'w_o_gm', 'w_o_mla', 'w_o_mem', 'w_out', 'w_ff1', 'w_ff2']
BIG_AXIS = {'w_in': 1, 'w_uq': 1, 'w_ukv': 1, 'w_mem_kv': 0, 'w_o_gm': 1, 'w_o_mla': 0, 'w_o_mem': 1,
            'w_out': 0, 'w_ff1': 1, 'w_ff2': 0}
BIG_SHAPE = {'w_in': (1024, 5312), 'w_uq': (384, 1536), 'w_ukv': (256, 2048), 'w_mem_kv': (1024, 1024),
             'w_o_gm': (512, 1024), 'w_o_mla': (1024, 1024), 'w_o_mem': (512, 1024), 'w_out': (1024, 1024),
             'w_ff1': (1024, 4096), 'w_ff2': (4096, 1024)}
SMALL = ['g_mix', 'g_cq', 'g_ckv', 'g_q_nope', 'g_q_pe', 'g_k_nope', 'g_k_pe', 'g_gm_ln', 'b_gm_ln',
         'w_spatial', 'b_spatial', 'g_mem', 'g_mq', 'g_mk', 'g_ffn']
WEIGHTS = ['g_mix', 'w_in', 'g_cq', 'w_uq', 'g_ckv', 'w_ukv', 'g_q_nope', 'g_q_pe', 'g_k_nope', 'g_k_pe',
           'g_gm_ln', 'b_gm_ln', 'w_spatial', 'b_spatial', 'g_mem', 'w_mem_kv', 'g_mq', 'g_mk', 'w_o_gm',
           'w_o_mla', 'w_o_mem', 'w_out', 'g_ffn', 'w_ff1', 'w_ff2']


def _pick(n, target, mult=LANES):
    best = None
    t = mult
    while t <= min(n, target):
        if n % t == 0:
            best = t
        t += mult
    return best if best is not None else n


def _params(sem):
    return pltpu.CompilerParams(dimension_semantics=sem, vmem_limit_bytes=VMEM_LIMIT)


def _matmul(a, b, mode, out_dtype, name, add=None, relu2_a=False, relu2_grad=None,
            tm_t=2048, tn_t=512, tk_t=1024):
    if mode == 'nn':
        (M, K), (K2, N) = a.shape, b.shape
    elif mode == 'nt':
        (M, K), (N, K2) = a.shape, b.shape
    else:
        (K, M), (K2, N) = a.shape, b.shape
    assert K == K2, (name, a.shape, b.shape)
    tm, tn, tk = _pick(M, tm_t), _pick(N, tn_t), _pick(K, tk_t)
    nk = K // tk
    if mode == 'nn':
        a_spec = pl.BlockSpec((tm, tk), lambda i, j, k: (i, k))
        b_spec = pl.BlockSpec((tk, tn), lambda i, j, k: (k, j))
        dims = (((1,), (0,)), ((), ()))
    elif mode == 'nt':
        a_spec = pl.BlockSpec((tm, tk), lambda i, j, k: (i, k))
        b_spec = pl.BlockSpec((tn, tk), lambda i, j, k: (j, k))
        dims = (((1,), (1,)), ((), ()))
    else:
        a_spec = pl.BlockSpec((tk, tm), lambda i, j, k: (k, i))
        b_spec = pl.BlockSpec((tk, tn), lambda i, j, k: (k, j))
        dims = (((0,), (0,)), ((), ()))
    o_spec = pl.BlockSpec((tm, tn), lambda i, j, k: (i, j))
    has_add, has_e = add is not None, relu2_grad is not None

    def body(*refs):
        a_ref, b_ref = refs[0], refs[1]
        pos = 2
        add_ref = e_ref = None
        if has_add:
            add_ref = refs[pos]
            pos += 1
        if has_e:
            e_ref = refs[pos]
            pos += 1
        o_ref = refs[pos]
        acc_ref = refs[pos + 1] if nk > 1 else None

        av = a_ref[...]
        if relu2_a:
            av = jnp.maximum(av.astype(F32), 0.0)
            av = av * av
        prod = lax.dot_general(av.astype(BF16), b_ref[...].astype(BF16), dims, preferred_element_type=F32)

        def finish(r):
            if has_add:
                r = r + add_ref[...]
            if has_e:
                r = r * (2.0 * jnp.maximum(e_ref[...].astype(F32), 0.0))
            o_ref[...] = r.astype(out_dtype)

        if nk == 1:
            finish(prod)
        else:
            k = pl.program_id(2)

            @pl.when(k == 0)
            def _():
                acc_ref[...] = prod

            @pl.when(k > 0)
            def _():
                acc_ref[...] += prod

            @pl.when(k == nk - 1)
            def _():
                finish(acc_ref[...])

    ins, specs = [a, b], [a_spec, b_spec]
    if has_add:
        ins.append(add)
        specs.append(o_spec)
    if has_e:
        ins.append(relu2_grad)
        specs.append(o_spec)
    return pl.pallas_call(
        body, name=name, grid=(M // tm, N // tn, nk),
        in_specs=specs, out_specs=o_spec, out_shape=jax.ShapeDtypeStruct((M, N), out_dtype),
        scratch_shapes=[pltpu.VMEM((tm, tn), F32)] if nk > 1 else [],
        compiler_params=_params(("parallel", "parallel", "arbitrary")),
    )(*ins)


def _row_tile(rows, target=256):
    return _pick(rows, target, 8)


def _rowspec(tr, width, col=0):
    return pl.BlockSpec((tr, width), lambda i, col=col: (i, col))


def _fullspec(shape):
    nd = len(shape)
    return pl.BlockSpec(shape, lambda i, nd=nd: (0,) * nd)


def _rms(x, width):
    return lax.rsqrt(jnp.sum(x * x, axis=-1, keepdims=True) * (1.0 / width) + EPS)


def _rms_bwd_rows(x, g, dy, width):
    r = _rms(x, width)
    xh = x * r
    dn = dy * g
    dx = r * (dn - xh * (jnp.sum(dn * xh, axis=-1, keepdims=True) * (1.0 / width)))
    return dx, dy * xh


def _acc_rows(ref, val, first):
    s = jnp.sum(val, axis=0, keepdims=True)

    @pl.when(first)
    def _():
        ref[...] = s

    @pl.when(jnp.logical_not(first))
    def _():
        ref[...] += s


def _rms_fwd(x, g, name):
    rows, width = x.shape
    tr = _row_tile(rows)

    def body(x_ref, g_ref, o_ref):
        xv = x_ref[...]
        o_ref[...] = (xv * _rms(xv, width) * g_ref[...]).astype(BF16)

    return pl.pallas_call(
        body, name=name, grid=(rows // tr,),
        in_specs=[_rowspec(tr, width), _fullspec((1, width))], out_specs=_rowspec(tr, width),
        out_shape=jax.ShapeDtypeStruct((rows, width), BF16), compiler_params=_params(("parallel",)),
    )(x, g)


def _rms_bwd(x, g, dy, res, name):
    rows, width = x.shape
    tr = _row_tile(rows)
    has_res = res is not None

    def body(*refs):
        if has_res:
            x_ref, g_ref, dy_ref, res_ref, dx_ref, dg_ref = refs
        else:
            x_ref, g_ref, dy_ref, dx_ref, dg_ref = refs
        dx, dgv = _rms_bwd_rows(x_ref[...], g_ref[...], dy_ref[...], width)
        if has_res:
            dx = dx + res_ref[...]
        dx_ref[...] = dx
        _acc_rows(dg_ref, dgv, pl.program_id(0) == 0)

    ins = [x, g, dy] + ([res] if has_res else [])
    specs = [_rowspec(tr, width), _fullspec((1, width)), _rowspec(tr, width)] + ([_rowspec(tr, width)] if has_res else [])
    return pl.pallas_call(
        body, name=name, grid=(rows // tr,), in_specs=specs,
        out_specs=[_rowspec(tr, width), _fullspec((1, width))],
        out_shape=[jax.ShapeDtypeStruct((rows, width), F32), jax.ShapeDtypeStruct((1, width), F32)],
        compiler_params=_params(("arbitrary",)),
    )(*ins)


_GELU_C = math.sqrt(2.0 / math.pi)


def _gelu(x):
    t = jnp.tanh(_GELU_C * (x + 0.044715 * (x * x * x)))
    return 0.5 * x * (1.0 + t), t


def _gelu_grad(x, t):
    return 0.5 * (1.0 + t) + 0.5 * x * (1.0 - t * t) * (_GELU_C * (1.0 + 3.0 * 0.044715 * (x * x)))


def _gm_forward_rows(zu, zv, gln, bln, wc_ref, bst, n_chunk):
    u, tu = _gelu(zu)
    a, ta = _gelu(zv)
    mu = jnp.mean(a, axis=-1, keepdims=True)
    ac = a - mu
    rs = lax.rsqrt(jnp.mean(ac * ac, axis=-1, keepdims=True) + EPS)
    n = ac * rs
    v = n * gln + bln
    vb = v.astype(BF16)
    rows = []
    for c in range(n_chunk):
        cols = []
        for g in range(GM_GROUPS):
            vc = vb[c * GM_CHUNK:(c + 1) * GM_CHUNK, g * LANES:(g + 1) * LANES]
            mixed = jnp.dot(wc_ref[g], vc, preferred_element_type=F32) + bst[g]
            cols.append(mixed)
        rows.append(jnp.concatenate(cols, axis=1))
    mixed = jnp.concatenate(rows, axis=0) if n_chunk > 1 else rows[0]
    return u, tu, ta, n, rs, v, mixed


def _gm_fwd(z, gln, bln, wc, bst, name):
    rows = z.shape[0]
    tr = _pick(rows, 512, GM_CHUNK)
    n_chunk = tr // GM_CHUNK

    def body(zu_ref, zv_ref, gln_ref, bln_ref, wc_ref, bst_ref, o_ref):
        u, _, _, _, _, _, mixed = _gm_forward_rows(zu_ref[...], zv_ref[...], gln_ref[...], bln_ref[...], wc_ref,
                                                   bst_ref, n_chunk)
        o_ref[...] = (u * mixed).astype(BF16)

    return pl.pallas_call(
        body, name=name, grid=(rows // tr,),
        in_specs=[_rowspec(tr, GM_WIDTH, 0), _rowspec(tr, GM_WIDTH, 1), _fullspec((1, GM_WIDTH)), _fullspec((1, GM_WIDTH)),
                  _fullspec((GM_GROUPS, GM_CHUNK, GM_CHUNK)), _fullspec((GM_GROUPS, GM_CHUNK, LANES))],
        out_specs=_rowspec(tr, GM_WIDTH), out_shape=jax.ShapeDtypeStruct((rows, GM_WIDTH), BF16),
        compiler_params=_params(("parallel",)),
    )(z, z, gln, bln, wc, bst)


def _gm_bwd(z, dy, gln, bln, wc, wct, bst, name):
    rows = z.shape[0]
    tr = _pick(rows, 512, GM_CHUNK)
    n_chunk = tr // GM_CHUNK

    def body(zu_ref, zv_ref, dy_ref, gln_ref, bln_ref, wc_ref, wct_ref, bst_ref, dz_ref, dws_ref, dbs_ref, dgl_ref, dbl_ref):
        first = pl.program_id(0) == 0
        zu, zv, gln = zu_ref[...], zv_ref[...], gln_ref[...]
        u, tu, ta, n, rs, v, mixed = _gm_forward_rows(zu, zv, gln, bln_ref[...], wc_ref, bst_ref, n_chunk)
        dyv = dy_ref[...]
        dzu = dyv * mixed * _gelu_grad(zu, tu)
        dmix = dyv * u
        dmb = dmix.astype(BF16)
        vb = v.astype(BF16)
        dv_rows, dws, dbs = [], [None] * GM_GROUPS, None
        for c in range(n_chunk):
            rsl = slice(c * GM_CHUNK, (c + 1) * GM_CHUNK)
            cols = []
            for g in range(GM_GROUPS):
                csl = slice(g * LANES, (g + 1) * LANES)
                dmc = dmb[rsl, csl]
                cols.append(jnp.dot(wct_ref[g], dmc, preferred_element_type=F32))
                w_part = lax.dot_general(dmc, vb[rsl, csl], (((1,), (1,)), ((), ())), preferred_element_type=F32)
                dws[g] = w_part if dws[g] is None else dws[g] + w_part
            dv_rows.append(jnp.concatenate(cols, axis=1))
            dbs = dmix[rsl, :] if dbs is None else dbs + dmix[rsl, :]
        dv = jnp.concatenate(dv_rows, axis=0) if n_chunk > 1 else dv_rows[0]
        dn = dv * gln
        da = rs * (dn - jnp.mean(dn, axis=-1, keepdims=True) - n * jnp.mean(dn * n, axis=-1, keepdims=True))
        dzv = da * _gelu_grad(zv, ta)
        dz_ref[:, 0:GM_WIDTH] = dzu.astype(BF16)
        dz_ref[:, GM_WIDTH:2 * GM_WIDTH] = dzv.astype(BF16)
        _acc_rows(dgl_ref, dv * n, first)
        _acc_rows(dbl_ref, dv, first)

        @pl.when(first)
        def _():
            for g in range(GM_GROUPS):
                dws_ref[g] = dws[g]
            dbs_ref[...] = dbs

        @pl.when(jnp.logical_not(first))
        def _():
            for g in range(GM_GROUPS):
                dws_ref[g] += dws[g]
            dbs_ref[...] += dbs

    wspec = _fullspec((GM_GROUPS, GM_CHUNK, GM_CHUNK))
    return pl.pallas_call(
        body, name=name, grid=(rows // tr,),
        in_specs=[_rowspec(tr, GM_WIDTH, 0), _rowspec(tr, GM_WIDTH, 1), _rowspec(tr, GM_WIDTH), _fullspec((1, GM_WIDTH)),
                  _fullspec((1, GM_WIDTH)), wspec, wspec, wspec],
        out_specs=[_rowspec(tr, 2 * GM_WIDTH), wspec, _fullspec((GM_CHUNK, GM_WIDTH)), _fullspec((1, GM_WIDTH)),
                   _fullspec((1, GM_WIDTH))],
        out_shape=[jax.ShapeDtypeStruct((rows, 2 * GM_WIDTH), BF16), jax.ShapeDtypeStruct((GM_GROUPS, GM_CHUNK, GM_CHUNK), F32),
                   jax.ShapeDtypeStruct((GM_CHUNK, GM_WIDTH), F32), jax.ShapeDtypeStruct((1, GM_WIDTH), F32),
                   jax.ShapeDtypeStruct((1, GM_WIDTH), F32)],
        compiler_params=_params(("arbitrary",)),
    )(z, z, dy, gln, bln, wc, wct, bst)


def _lat_fwd(z, g_cq, g_ckv, name):
    rows = z.shape[0]
    tr = _row_tile(rows)

    def body(z_ref, gq_ref, gkv_ref, nq_ref, nkv_ref):
        zb = z_ref[...]
        cq, ckv = zb[:, 0:Q_LORA], zb[:, Q_LORA:Q_LORA + KV_LORA]
        nq_ref[...] = (cq * _rms(cq, Q_LORA) * gq_ref[...]).astype(BF16)
        nkv_ref[...] = (ckv * _rms(ckv, KV_LORA) * gkv_ref[...]).astype(BF16)

    return pl.pallas_call(
        body, name=name, grid=(rows // tr,),
        in_specs=[_rowspec(tr, MLA_W, Z_MLA // MLA_W), _fullspec((1, Q_LORA)), _fullspec((1, KV_LORA))],
        out_specs=[_rowspec(tr, Q_LORA), _rowspec(tr, KV_LORA)],
        out_shape=[jax.ShapeDtypeStruct((rows, Q_LORA), BF16), jax.ShapeDtypeStruct((rows, KV_LORA), BF16)],
        compiler_params=_params(("parallel",)),
    )(z, g_cq, g_ckv)


def _lat_bwd(z, dnq, dnkv, dkpe, g_cq, g_ckv, name):
    rows = z.shape[0]
    tr = _row_tile(rows)

    def body(z_ref, dnq_ref, dnkv_ref, dkpe_ref, gq_ref, gkv_ref, dz_ref, dgq_ref, dgkv_ref):
        first = pl.program_id(0) == 0
        zb = z_ref[...]
        dcq, dgq = _rms_bwd_rows(zb[:, 0:Q_LORA], gq_ref[...], dnq_ref[...], Q_LORA)
        dckv, dgkv = _rms_bwd_rows(zb[:, Q_LORA:Q_LORA + KV_LORA], gkv_ref[...], dnkv_ref[...], KV_LORA)
        dz_ref[:, 0:Q_LORA] = dcq.astype(BF16)
        dz_ref[:, Q_LORA:Q_LORA + KV_LORA] = dckv.astype(BF16)
        dz_ref[:, Q_LORA + KV_LORA:MLA_W] = dkpe_ref[...].astype(BF16)
        _acc_rows(dgq_ref, dgq, first)
        _acc_rows(dgkv_ref, dgkv, first)

    return pl.pallas_call(
        body, name=name, grid=(rows // tr,),
        in_specs=[_rowspec(tr, MLA_W, Z_MLA // MLA_W), _rowspec(tr, Q_LORA), _rowspec(tr, KV_LORA), _rowspec(tr, LANES),
                  _fullspec((1, Q_LORA)), _fullspec((1, KV_LORA))],
        out_specs=[_rowspec(tr, MLA_W), _fullspec((1, Q_LORA)), _fullspec((1, KV_LORA))],
        out_shape=[jax.ShapeDtypeStruct((rows, MLA_W), BF16), jax.ShapeDtypeStruct((1, Q_LORA), F32),
                   jax.ShapeDtypeStruct((1, KV_LORA), F32)],
        compiler_params=_params(("arbitrary",)),
    )(z, dnq, dnkv, dkpe, g_cq, g_ckv)


def _rope(y, cc, ss):
    return y * cc + pltpu.roll(y, 64, 1) * ss


def _rope_bwd(d, cc, ss):
    return d * cc + pltpu.roll(d * ss, 64, 1)


def _qk_fwd(q, kv, z, cc, ss, gqn, gqp, gkn, gkp, name):
    rows = q.shape[0]
    tr = _row_tile(rows)
    W = MLA_HEADS * HEAD
    QS = MLA_SCALE * LOG2E

    def body(q_ref, kv_ref, kpe_ref, cc_ref, ss_ref, gqn_ref, gqp_ref, gkn_ref, gkp_ref, qc_ref, kc_ref, v_ref):
        cc, ss = cc_ref[...], ss_ref[...]
        kpe = kpe_ref[...]
        kp = _rope(kpe * _rms(kpe, MLA_ROPE) * gkp_ref[...], cc, ss).astype(BF16)
        for h in range(MLA_HEADS):
            qn = q_ref[:, h * HEAD:(h + 1) * HEAD]
            qp = q_ref[:, W + h * HEAD:W + (h + 1) * HEAD]
            kn = kv_ref[:, h * HEAD:(h + 1) * HEAD]
            qc_ref[:, h * QCAT:h * QCAT + HEAD] = (qn * _rms(qn, HEAD) * gqn_ref[...] * QS).astype(BF16)
            qc_ref[:, h * QCAT + HEAD:(h + 1) * QCAT] = (_rope(qp * _rms(qp, MLA_ROPE) * gqp_ref[...], cc, ss) * QS).astype(BF16)
            kc_ref[:, h * QCAT:h * QCAT + HEAD] = (kn * _rms(kn, HEAD) * gkn_ref[...]).astype(BF16)
            kc_ref[:, h * QCAT + HEAD:(h + 1) * QCAT] = kp
        v_ref[...] = kv_ref[:, W:2 * W].astype(BF16)

    g = _fullspec((1, HEAD))
    return pl.pallas_call(
        body, name=name, grid=(rows // tr,),
        in_specs=[_rowspec(tr, 2 * W), _rowspec(tr, 2 * W), _rowspec(tr, LANES, Z_KPE // LANES), _rowspec(tr, LANES),
                  _rowspec(tr, LANES), g, g, g, g],
        out_specs=[_rowspec(tr, MLA_HEADS * QCAT), _rowspec(tr, MLA_HEADS * QCAT), _rowspec(tr, W)],
        out_shape=[jax.ShapeDtypeStruct((rows, MLA_HEADS * QCAT), BF16), jax.ShapeDtypeStruct((rows, MLA_HEADS * QCAT), BF16),
                   jax.ShapeDtypeStruct((rows, W), BF16)],
        compiler_params=_params(("parallel",)),
    )(q, kv, z, cc, ss, gqn, gqp, gkn, gkp)


def _qk_bwd(q, kv, z, cc, ss, gqn, gqp, gkn, gkp, dqc, dkc, dv, name):
    rows = q.shape[0]
    tr = _row_tile(rows)
    W = MLA_HEADS * HEAD

    def body(q_ref, kv_ref, kpe_ref, cc_ref, ss_ref, gqn_ref, gqp_ref, gkn_ref, gkp_ref, dqc_ref, dkc_ref, dv_ref,
             dq_ref, dkv_ref, dkpe_ref, dgqn_ref, dgqp_ref, dgkn_ref, dgkp_ref):
        first = pl.program_id(0) == 0
        cc, ss = cc_ref[...], ss_ref[...]
        sqn = sqp = skn = dkp = None
        for h in range(MLA_HEADS):
            dx, dg = _rms_bwd_rows(q_ref[:, h * HEAD:(h + 1) * HEAD], gqn_ref[...], dqc_ref[:, h * QCAT:h * QCAT + HEAD], HEAD)
            dq_ref[:, h * HEAD:(h + 1) * HEAD] = dx.astype(BF16)
            sqn = dg if sqn is None else sqn + dg
            dy = _rope_bwd(dqc_ref[:, h * QCAT + HEAD:(h + 1) * QCAT], cc, ss)
            dx, dg = _rms_bwd_rows(q_ref[:, W + h * HEAD:W + (h + 1) * HEAD], gqp_ref[...], dy, MLA_ROPE)
            dq_ref[:, W + h * HEAD:W + (h + 1) * HEAD] = dx.astype(BF16)
            sqp = dg if sqp is None else sqp + dg
            dx, dg = _rms_bwd_rows(kv_ref[:, h * HEAD:(h + 1) * HEAD], gkn_ref[...], dkc_ref[:, h * QCAT:h * QCAT + HEAD], HEAD)
            dkv_ref[:, h * HEAD:(h + 1) * HEAD] = dx.astype(BF16)
            skn = dg if skn is None else skn + dg
            part = dkc_ref[:, h * QCAT + HEAD:(h + 1) * QCAT]
            dkp = part if dkp is None else dkp + part
        dkv_ref[:, W:2 * W] = dv_ref[...].astype(BF16)
        dx, dg = _rms_bwd_rows(kpe_ref[...], gkp_ref[...], _rope_bwd(dkp, cc, ss), MLA_ROPE)
        dkpe_ref[...] = dx
        _acc_rows(dgqn_ref, sqn, first)
        _acc_rows(dgqp_ref, sqp, first)
        _acc_rows(dgkn_ref, skn, first)
        _acc_rows(dgkp_ref, dg, first)

    g = _fullspec((1, HEAD))
    gs = jax.ShapeDtypeStruct((1, HEAD), F32)
    return pl.pallas_call(
        body, name=name, grid=(rows // tr,),
        in_specs=[_rowspec(tr, 2 * W), _rowspec(tr, 2 * W), _rowspec(tr, LANES, Z_KPE // LANES), _rowspec(tr, LANES),
                  _rowspec(tr, LANES), g, g, g, g, _rowspec(tr, MLA_HEADS * QCAT), _rowspec(tr, MLA_HEADS * QCAT),
                  _rowspec(tr, W)],
        out_specs=[_rowspec(tr, 2 * W), _rowspec(tr, 2 * W), _rowspec(tr, LANES), g, g, g, g],
        out_shape=[jax.ShapeDtypeStruct((rows, 2 * W), BF16), jax.ShapeDtypeStruct((rows, 2 * W), BF16),
                   jax.ShapeDtypeStruct((rows, LANES), F32), gs, gs, gs, gs],
        compiler_params=_params(("arbitrary",)),
    )(q, kv, z, cc, ss, gqn, gqp, gkn, gkp, dqc, dkc, dv)


def _headnorm_fwd(x, col, nheads, g, out_scale, name):
    rows = x.shape[0]
    tr = _row_tile(rows)
    W = nheads * HEAD

    def body(x_ref, g_ref, o_ref):
        for h in range(nheads):
            xv = x_ref[:, h * HEAD:(h + 1) * HEAD]
            o_ref[:, h * HEAD:(h + 1) * HEAD] = (xv * _rms(xv, HEAD) * g_ref[...] * out_scale).astype(BF16)

    return pl.pallas_call(
        body, name=name, grid=(rows // tr,),
        in_specs=[_rowspec(tr, W, col), _fullspec((1, HEAD))], out_specs=_rowspec(tr, W),
        out_shape=jax.ShapeDtypeStruct((rows, W), BF16), compiler_params=_params(("parallel",)),
    )(x, g)


def _headnorm_bwd(x, col, nheads, g, dy, tail, name):
    rows = x.shape[0]
    tr = _row_tile(rows)
    W = nheads * HEAD
    has_tail = tail is not None
    WO = 2 * W if has_tail else W

    def body(*refs):
        if has_tail:
            x_ref, g_ref, dy_ref, t_ref, dx_ref, dg_ref = refs
        else:
            x_ref, g_ref, dy_ref, dx_ref, dg_ref = refs
        acc = None
        for h in range(nheads):
            sl = slice(h * HEAD, (h + 1) * HEAD)
            dx, dg = _rms_bwd_rows(x_ref[:, sl], g_ref[...], dy_ref[:, sl], HEAD)
            dx_ref[:, sl] = dx.astype(BF16)
            acc = dg if acc is None else acc + dg
        if has_tail:
            dx_ref[:, W:2 * W] = t_ref[...].astype(BF16)
        _acc_rows(dg_ref, acc, pl.program_id(0) == 0)

    ins = [x, g, dy] + ([tail] if has_tail else [])
    specs = [_rowspec(tr, W, col), _fullspec((1, HEAD)), _rowspec(tr, W)] + ([_rowspec(tr, W)] if has_tail else [])
    return pl.pallas_call(
        body, name=name, grid=(rows // tr,), in_specs=specs,
        out_specs=[_rowspec(tr, WO), _fullspec((1, HEAD))],
        out_shape=[jax.ShapeDtypeStruct((rows, WO), BF16), jax.ShapeDtypeStruct((1, HEAD), F32)],
        compiler_params=_params(("arbitrary",)),
    )(*ins)


def _sigmoid(x):
    return 1.0 / (1.0 + jnp.exp(-x))


def _merge_fwd(z, y_gm, y_mla, y_mem, name):
    rows = z.shape[0]
    tr = _row_tile(rows)

    def body(g0_ref, g1_ref, g2_ref, a_ref, b_ref, c_ref, o_ref):
        m = _sigmoid(g0_ref[...]) * a_ref[...] + _sigmoid(g1_ref[...]) * b_ref[...] + _sigmoid(g2_ref[...]) * c_ref[...]
        o_ref[...] = m.astype(BF16)

    r = _rowspec(tr, D_MODEL)
    return pl.pallas_call(
        body, name=name, grid=(rows // tr,),
        in_specs=[_rowspec(tr, D_MODEL, 1), _rowspec(tr, D_MODEL, 2), _rowspec(tr, D_MODEL, 3), r, r, r],
        out_specs=r, out_shape=jax.ShapeDtypeStruct((rows, D_MODEL), BF16), compiler_params=_params(("parallel",)),
    )(z, z, z, y_gm, y_mla, y_mem)


def _merge_bwd(z, y_gm, y_mla, y_mem, dm, name):
    rows = z.shape[0]
    tr = _row_tile(rows)

    def body(g0_ref, g1_ref, g2_ref, a_ref, b_ref, c_ref, dm_ref, da_ref, db_ref, dc_ref, dzg_ref):
        dmv = dm_ref[...]
        for k, (g_ref, y_ref, dy_ref) in enumerate(((g0_ref, a_ref, da_ref), (g1_ref, b_ref, db_ref), (g2_ref, c_ref, dc_ref))):
            s = _sigmoid(g_ref[...])
            dy_ref[...] = (dmv * s).astype(BF16)
            dzg_ref[:, k * D_MODEL:(k + 1) * D_MODEL] = (dmv * y_ref[...] * s * (1.0 - s)).astype(BF16)

    r = _rowspec(tr, D_MODEL)
    o = jax.ShapeDtypeStruct((rows, D_MODEL), BF16)
    return pl.pallas_call(
        body, name=name, grid=(rows // tr,),
        in_specs=[_rowspec(tr, D_MODEL, 1), _rowspec(tr, D_MODEL, 2), _rowspec(tr, D_MODEL, 3), r, r, r, r],
        out_specs=[r, r, r, _rowspec(tr, 3 * D_MODEL)],
        out_shape=[o, o, o, jax.ShapeDtypeStruct((rows, 3 * D_MODEL), BF16)],
        compiler_params=_params(("parallel",)),
    )(z, z, z, y_gm, y_mla, y_mem, dm)


def _loss_head(y, target, name):
    rows, width = y.shape
    tr = _row_tile(rows)

    def body(y_ref, t_ref, dy_ref, l_ref):
        e = y_ref[...] - t_ref[...]
        dy_ref[...] = e * (1.0 / width)
        e2 = e * e
        part = e2[:, 0:LANES]
        for k in range(1, width // LANES):
            part = part + e2[:, k * LANES:(k + 1) * LANES]
        _acc_rows(l_ref, part, pl.program_id(0) == 0)

    return pl.pallas_call(
        body, name=name, grid=(rows // tr,),
        in_specs=[_rowspec(tr, width), _rowspec(tr, width)],
        out_specs=[_rowspec(tr, width), _fullspec((1, LANES))],
        out_shape=[jax.ShapeDtypeStruct((rows, width), F32), jax.ShapeDtypeStruct((1, LANES), F32)],
        compiler_params=_params(("arbitrary",)),
    )(y, target)


_NT = (((1,), (1,)), ((), ()))
_TN = (((0,), (0,)), ((), ()))


def _diag_mask(s):
    row = lax.broadcasted_iota(jnp.int32, s.shape, 0)
    col = lax.broadcasted_iota(jnp.int32, s.shape, 1)
    return jnp.where(row >= col, s, NEG)


def _attn_fwd(q, k, v, nb, nheads, dk, v_col0, causal, name):
    S, Skv = q.shape[0] // nb, k.shape[0] // nb
    tq = _pick(Skv, ATT_TILE) if causal else _pick(S, 4 * ATT_TILE)
    nq = S // tq

    def body(q_ref, k_ref, v_ref, o_ref, lse_ref):
        for i in range(nq):
            r0 = i * tq
            qb = q_ref[r0:r0 + tq, :]
            if causal:
                spans = ([(0, r0, False)] if i > 0 else []) + [(r0, r0 + tq, True)]
            else:
                spans = [(0, Skv, False)]
            scores = []
            for a, b, masked in spans:
                s = lax.dot_general(qb, k_ref[a:b, :], _NT, preferred_element_type=F32)
                scores.append(_diag_mask(s) if masked else s)
            m = functools.reduce(jnp.maximum, [jnp.max(s, axis=-1, keepdims=True) for s in scores])
            l = acc = None
            for s, (a, b, _) in zip(scores, spans):
                p = jnp.exp2(s - m)
                lp = jnp.sum(p, axis=-1, keepdims=True)
                ap = jnp.dot(p.astype(BF16), v_ref[a:b, :].astype(BF16), preferred_element_type=F32)
                l, acc = (lp, ap) if l is None else (l + lp, acc + ap)
            o_ref[r0:r0 + tq, :] = acc / l
            lse_ref[r0:r0 + tq, :] = m + jnp.log2(l)

    return pl.pallas_call(
        body, name=name, grid=(nb, nheads),
        in_specs=[pl.BlockSpec((S, dk), lambda b, h: (b, h)), pl.BlockSpec((Skv, dk), lambda b, h: (b, h)),
                  pl.BlockSpec((Skv, HEAD), lambda b, h: (b, v_col0 + h))],
        out_specs=[pl.BlockSpec((S, HEAD), lambda b, h: (b, h)), pl.BlockSpec((None, S, 1), lambda b, h: (h, b, 0))],
        out_shape=[jax.ShapeDtypeStruct((nb * S, nheads * HEAD), F32), jax.ShapeDtypeStruct((nheads, nb * S, 1), F32)],
        compiler_params=_params(("parallel", "parallel")),
    )(q, k, v)


def _attn_bwd(q, k, v, o, do, lse, nb, nheads, dk, v_col0, scale, causal, name):
    S, Skv = q.shape[0] // nb, k.shape[0] // nb
    tk = _pick(Skv, ATT_TILE)
    nkv = Skv // tk

    def body(q_ref, k_ref, v_ref, o_ref, do_ref, lse_ref, dq_ref, dk_ref, dv_ref, delta_ref, dob_ref):
        dov = do_ref[...]
        delta_ref[...] = jnp.sum(o_ref[...] * dov, axis=-1, keepdims=True)
        dob_ref[...] = dov.astype(BF16)

        for j in range(nkv):
            c0 = j * tk
            kb = k_ref[c0:c0 + tk, :]
            vb = v_ref[c0:c0 + tk, :].astype(BF16)
            if causal:
                spans = [(c0, c0 + tk, True)] + ([(c0 + tk, S, False)] if c0 + tk < S else [])
            else:
                spans = [(0, S, False)]
            dk_acc = dv_acc = None
            for a, b, masked in spans:
                qb = q_ref[a:b, :]
                dob = dob_ref[a:b, :]
                s = lax.dot_general(qb, kb, _NT, preferred_element_type=F32)
                if masked:
                    s = _diag_mask(s)
                p = jnp.exp2(s - lse_ref[a:b, :])
                dp = lax.dot_general(dob, vb, _NT, preferred_element_type=F32)
                ds = (p * (dp - delta_ref[a:b, :])).astype(BF16)
                dv_p = lax.dot_general(p.astype(BF16), dob, _TN, preferred_element_type=F32)
                dk_p = lax.dot_general(ds, qb, _TN, preferred_element_type=F32)
                dk_acc, dv_acc = (dk_p, dv_p) if dk_acc is None else (dk_acc + dk_p, dv_acc + dv_p)
                dq_p = jnp.dot(ds, kb, preferred_element_type=F32) * scale
                if j == 0:
                    dq_ref[a:b, :] = dq_p
                else:
                    dq_ref[a:b, :] += dq_p
            dk_ref[c0:c0 + tk, :] = dk_acc * LN2
            dv_ref[c0:c0 + tk, :] = dv_acc

    return pl.pallas_call(
        body, name=name, grid=(nb, nheads),
        in_specs=[pl.BlockSpec((S, dk), lambda b, h: (b, h)), pl.BlockSpec((Skv, dk), lambda b, h: (b, h)),
                  pl.BlockSpec((Skv, HEAD), lambda b, h: (b, v_col0 + h)), pl.BlockSpec((S, HEAD), lambda b, h: (b, h)),
                  pl.BlockSpec((S, HEAD), lambda b, h: (b, h)), pl.BlockSpec((None, S, 1), lambda b, h: (h, b, 0))],
        out_specs=[pl.BlockSpec((S, dk), lambda b, h: (b, h)), pl.BlockSpec((Skv, dk), lambda b, h: (b, h)),
                   pl.BlockSpec((Skv, HEAD), lambda b, h: (b, h))],
        out_shape=[jax.ShapeDtypeStruct((nb * S, nheads * dk), F32), jax.ShapeDtypeStruct((nb * Skv, nheads * dk), F32),
                   jax.ShapeDtypeStruct((nb * Skv, nheads * HEAD), F32)],
        scratch_shapes=[pltpu.VMEM((S, 1), F32), pltpu.VMEM((S, HEAD), BF16)],
        compiler_params=_params(("parallel", "parallel")),
    )(q, k, v, o, do, lse)


def _spread_rope(a):
    zero = jnp.zeros(a.shape[:-1] + (32,), a.dtype)
    return jnp.concatenate([a[..., :32], zero, a[..., 32:], zero], axis=-1)


def _gather_rope(a):
    return jnp.concatenate([a[..., 0:32], a[..., 64:96]], axis=-1)


def _win_layout(w):
    return jnp.concatenate([w[:, C_ZU:C_CQ], w[:, C_ZG:C_END], w[:, C_QM:C_ZG], w[:, C_CQ:C_CKV], w[:, C_CKV:C_KPE],
                            _spread_rope(w[:, C_KPE:C_QM])], axis=1)


def _win_unlayout(d):
    return jnp.concatenate([d[:, 0:Z_GATE], d[:, Z_MLA:Z_MLA + Q_LORA], d[:, Z_MLA + Q_LORA:Z_KPE],
                            _gather_rope(d[:, Z_KPE:Z_COLS]), d[:, Z_QM:Z_MLA], d[:, Z_GATE:Z_QM]], axis=1)


def _wuq_layout(w):
    r = w.reshape(Q_LORA, MLA_HEADS, HEAD + MLA_ROPE)
    return jnp.concatenate([r[:, :, :HEAD].reshape(Q_LORA, -1), _spread_rope(r[:, :, HEAD:]).reshape(Q_LORA, -1)], axis=1)


def _wuq_unlayout(d):
    n = d[:, :MLA_HEADS * HEAD].reshape(Q_LORA, MLA_HEADS, HEAD)
    p = _gather_rope(d[:, MLA_HEADS * HEAD:].reshape(Q_LORA, MLA_HEADS, HEAD))
    return jnp.concatenate([n, p], axis=-1).reshape(Q_LORA, -1)


def _wukv_layout(w):
    r = w.reshape(KV_LORA, MLA_HEADS, 2 * HEAD)
    return jnp.concatenate([r[:, :, :HEAD].reshape(KV_LORA, -1), r[:, :, HEAD:].reshape(KV_LORA, -1)], axis=1)


def _wukv_unlayout(d):
    k = d[:, :MLA_HEADS * HEAD].reshape(KV_LORA, MLA_HEADS, HEAD)
    v = d[:, MLA_HEADS * HEAD:].reshape(KV_LORA, MLA_HEADS, HEAD)
    return jnp.concatenate([k, v], axis=-1).reshape(KV_LORA, -1)


def _local_step(x, mem, positions, target, W, P):
    B, S, _ = x.shape
    M = mem.shape[1]
    T = B * S
    x2d = x.reshape(T, D_MODEL)
    mem2d = mem.reshape(B * M, D_MODEL)
    tgt2d = target.reshape(T, D_MODEL)

    def row(v):
        return v.reshape(1, -1).astype(F32)

    inv_freq = ROPE_BASE ** (-jnp.arange(0, MLA_ROPE, 2, dtype=F32) / MLA_ROPE)
    ang = positions.reshape(T).astype(F32)[:, None] * inv_freq
    cos, sin, zero = jnp.cos(ang), jnp.sin(ang), jnp.zeros_like(ang)
    cc = jnp.concatenate([cos, zero, cos, zero], axis=1)
    ss = jnp.concatenate([-sin, zero, sin, zero], axis=1)

    w_in = _win_layout(W['w_in']).astype(BF16)
    w_uq = _wuq_layout(W['w_uq']).astype(BF16)
    w_ukv = _wukv_layout(W['w_ukv']).astype(BF16)
    w_mem_kv, w_o_gm, w_o_mla, w_o_mem = (W[n].astype(BF16) for n in ('w_mem_kv', 'w_o_gm', 'w_o_mla', 'w_o_mem'))
    w_out, w_ff1, w_ff2 = (W[n].astype(BF16) for n in ('w_out', 'w_ff1', 'w_ff2'))
    g_mix, g_cq, g_ckv, g_ffn, g_mem = row(P['g_mix']), row(P['g_cq']), row(P['g_ckv']), row(P['g_ffn']), row(P['g_mem'])
    gqn, gkn, gmq, gmk = row(P['g_q_nope']), row(P['g_k_nope']), row(P['g_mq']), row(P['g_mk'])
    gqp, gkp = _spread_rope(row(P['g_q_pe'])), _spread_rope(row(P['g_k_pe']))
    gln, bln = row(P['g_gm_ln']), row(P['b_gm_ln'])
    wc = jnp.tril(P['w_spatial'].astype(F32))
    wct = jnp.swapaxes(wc, 1, 2).astype(BF16)
    wc = wc.astype(BF16)
    bst = jnp.broadcast_to(P['b_spatial'].astype(F32)[:, :, None], (GM_GROUPS, GM_CHUNK, LANES))

    h = _rms_fwd(x2d, g_mix, "rms_mix")
    z = _matmul(h, w_in, 'nn', F32, "mm_in", tn_t=768)
    ygm_pre = _gm_fwd(z, gln, bln, wc, bst, "gm_fwd")
    y_gm = _matmul(ygm_pre, w_o_gm, 'nn', F32, "mm_o_gm")
    nq, nkv = _lat_fwd(z, g_cq, g_ckv, "lat_fwd")
    q = _matmul(nq, w_uq, 'nn', F32, "mm_uq")
    kv = _matmul(nkv, w_ukv, 'nn', F32, "mm_ukv")
    qcat, kcat, vv = _qk_fwd(q, kv, z, cc, ss, gqn, gqp, gkn, gkp, "qk_fwd")
    o, lse = _attn_fwd(qcat, kcat, vv, B, MLA_HEADS, QCAT, 0, True, "mla_attn_fwd")
    y_mla = _matmul(o, w_o_mla, 'nn', F32, "mm_o_mla")
    nm = _rms_fwd(mem2d, g_mem, "rms_mem")
    kvm = _matmul(nm, w_mem_kv, 'nn', F32, "mm_mem_kv")
    qm = _headnorm_fwd(z, Z_QM // (MEM_HEADS * HEAD), MEM_HEADS, gmq, MEM_SCALE * LOG2E, "memq_fwd")
    km = _headnorm_fwd(kvm, 0, MEM_HEADS, gmk, 1.0, "memk_fwd")
    om, lse_m = _attn_fwd(qm, km, kvm, B, MEM_HEADS, HEAD, MEM_HEADS, False, "mem_attn_fwd")
    y_mem = _matmul(om, w_o_mem, 'nn', F32, "mm_o_mem")
    merged = _merge_fwd(z, y_gm, y_mla, y_mem, "merge_fwd")
    x1 = _matmul(merged, w_out, 'nn', F32, "mm_out", add=x2d)
    h2 = _rms_fwd(x1, g_ffn, "rms_ffn")
    a1 = _matmul(h2, w_ff1, 'nn', BF16, "mm_ff1")
    x2 = _matmul(a1, w_ff2, 'nn', F32, "mm_ff2", add=x1, relu2_a=True)
    dx2, loss_part = _loss_head(x2, tgt2d, "loss_head")

    G = {}
    G['w_ff2'] = _matmul(a1, dx2, 'tn', F32, "mm_d_ff2", relu2_a=True)
    da1 = _matmul(dx2, w_ff2, 'nt', BF16, "mm_da1", relu2_grad=a1)
    G['w_ff1'] = _matmul(h2, da1, 'tn', F32, "mm_d_ff1")
    dh2 = _matmul(da1, w_ff1, 'nt', F32, "mm_dh2")
    dx1, G['g_ffn'] = _rms_bwd(x1, g_ffn, dh2, dx2, "rms_ffn_bwd")
    G['w_out'] = _matmul(merged, dx1, 'tn', F32, "mm_d_out")
    dmerged = _matmul(dx1, w_out, 'nt', F32, "mm_dmerged")
    dy_gm, dy_mla, dy_mem, dzg = _merge_bwd(z, y_gm, y_mla, y_mem, dmerged, "merge_bwd")
    G['w_o_gm'] = _matmul(ygm_pre, dy_gm, 'tn', F32, "mm_d_o_gm")
    dygm_pre = _matmul(dy_gm, w_o_gm, 'nt', F32, "mm_dygm")
    dz_gm, dws, dbs, G['g_gm_ln'], G['b_gm_ln'] = _gm_bwd(z, dygm_pre, gln, bln, wc, wct, bst, "gm_bwd")
    G['w_spatial'] = jnp.tril(dws)
    G['b_spatial'] = jnp.sum(dbs.reshape(GM_CHUNK, GM_GROUPS, LANES), axis=-1).T
    G['w_o_mla'] = _matmul(o, dy_mla, 'tn', F32, "mm_d_o_mla")
    do = _matmul(dy_mla, w_o_mla, 'nt', F32, "mm_do")
    dqc, dkc, dvv = _attn_bwd(qcat, kcat, vv, o, do, lse, B, MLA_HEADS, QCAT, 0, MLA_SCALE, True, "mla_attn_bwd")
    dq, dkv, dkpe, G['g_q_nope'], dgqp, G['g_k_nope'], dgkp = _qk_bwd(q, kv, z, cc, ss, gqn, gqp, gkn, gkp, dqc, dkc, dvv,
                                                                     "qk_bwd")
    G['g_q_pe'], G['g_k_pe'] = _gather_rope(dgqp), _gather_rope(dgkp)
    G['w_uq'] = _wuq_unlayout(_matmul(nq, dq, 'tn', F32, "mm_d_uq"))
    dnq = _matmul(dq, w_uq, 'nt', F32, "mm_dnq")
    G['w_ukv'] = _wukv_unlayout(_matmul(nkv, dkv, 'tn', F32, "mm_d_ukv"))
    dnkv = _matmul(dkv, w_ukv, 'nt', F32, "mm_dnkv")
    dz_mla, G['g_cq'], G['g_ckv'] = _lat_bwd(z, dnq, dnkv, dkpe, g_cq, g_ckv, "lat_bwd")
    G['w_o_mem'] = _matmul(om, dy_mem, 'tn', F32, "mm_d_o_mem")
    dom = _matmul(dy_mem, w_o_mem, 'nt', F32, "mm_dom")
    dqm, dkm, dvm = _attn_bwd(qm, km, kvm, om, dom, lse_m, B, MEM_HEADS, HEAD, MEM_HEADS, MEM_SCALE, False, "mem_attn_bwd")
    dz_qm, G['g_mq'] = _headnorm_bwd(z, Z_QM // (MEM_HEADS * HEAD), MEM_HEADS, gmq, dqm, None, "memq_bwd")
    dkvm, G['g_mk'] = _headnorm_bwd(kvm, 0, MEM_HEADS, gmk, dkm, dvm, "memk_bwd")
    G['w_mem_kv'] = _matmul(nm, dkvm, 'tn', F32, "mm_d_mem_kv")
    dnm = _matmul(dkvm, w_mem_kv, 'nt', F32, "mm_dnm")
    _, G['g_mem'] = _rms_bwd(mem2d, g_mem, dnm, None, "rms_mem_bwd")
    dz = jnp.concatenate([dz_gm, dzg, dz_qm, dz_mla], axis=1)
    G['w_in'] = _win_unlayout(_matmul(h, dz, 'tn', F32, "mm_d_in", tn_t=768))
    dh = _matmul(dz, w_in, 'nt', F32, "mm_dh", tk_t=768)
    gx, G['g_mix'] = _rms_bwd(x2d, g_mix, dh, dx1, "rms_mix_bwd")
    return loss_part, gx.reshape(B, S, D_MODEL), G


MESH = pl.DeviceIdType.MESH
HBM_SPEC = pl.BlockSpec(memory_space=pltpu.HBM)


def _all_gather8(xs, name):
    def body(x_ref, out_ref, send_sems, recv_sems, local_sem):
        x, y, c = lax.axis_index("x"), lax.axis_index("y"), lax.axis_index("c")
        me, sibling = (x, y, c), (x, y, 1 - c)
        chips = [(1 - x, y), (x, 1 - y), (1 - x, 1 - y)]

        def rows(px, py, pc):
            return out_ref.at[4 * px + 2 * py + pc]

        def copy(k, block, to, src=None):
            return pltpu.make_async_remote_copy(
                src_ref=rows(*block) if src is None else src, dst_ref=rows(*block),
                send_sem=send_sems.at[k], recv_sem=recv_sems.at[k], device_id=to, device_id_type=MESH)

        mine = pltpu.make_async_copy(x_ref, rows(*me), local_sem)
        mine.start()
        first = [copy(0, me, sibling, src=x_ref)]
        first += [copy(1 + j, me, (*chip, c), src=x_ref) for j, chip in enumerate(chips)]
        for cp in first:
            cp.start()
        passed = [copy(4 + j, (*chip, c), sibling) for j, chip in enumerate(chips)]
        for j, chip in enumerate(chips):
            copy(1 + j, (*chip, c), me).wait_recv()
            passed[j].start()
        copy(0, sibling, me).wait_recv()
        for j, chip in enumerate(chips):
            copy(4 + j, (*chip, 1 - c), me).wait_recv()
        for cp in first + passed:
            cp.wait_send()
        mine.wait()

    return pl.pallas_call(
        body, name=name, in_specs=[HBM_SPEC], out_specs=HBM_SPEC,
        out_shape=jax.ShapeDtypeStruct((N_DEV,) + xs.shape, xs.dtype),
        scratch_shapes=[pltpu.SemaphoreType.DMA((7,)), pltpu.SemaphoreType.DMA((7,)), pltpu.SemaphoreType.DMA],
    )(xs)


def _swap_sibling(xs, name):
    def body(x_ref, out_ref, send_sem, recv_sem):
        x, y, c = lax.axis_index("x"), lax.axis_index("y"), lax.axis_index("c")
        cp = pltpu.make_async_remote_copy(src_ref=x_ref, dst_ref=out_ref, send_sem=send_sem, recv_sem=recv_sem,
                                          device_id=(x, y, 1 - c), device_id_type=MESH)
        cp.start()
        cp.wait()

    return pl.pallas_call(
        body, name=name, in_specs=[HBM_SPEC], out_specs=HBM_SPEC, out_shape=jax.ShapeDtypeStruct(xs.shape, xs.dtype),
        scratch_shapes=[pltpu.SemaphoreType.DMA, pltpu.SemaphoreType.DMA],
    )(xs)


def _swap_chips(xs, name):
    def body(x_ref, out_ref, send_sems, recv_sems):
        x, y, c = lax.axis_index("x"), lax.axis_index("y"), lax.axis_index("c")
        chips = [(1 - x, y), (x, 1 - y), (1 - x, 1 - y)]
        cps = [pltpu.make_async_remote_copy(src_ref=x_ref.at[2 * px + py], dst_ref=out_ref.at[k], send_sem=send_sems.at[k],
                                            recv_sem=recv_sems.at[k], device_id=(px, py, c), device_id_type=MESH)
               for k, (px, py) in enumerate(chips)]
        for cp in cps:
            cp.start()
        for cp in cps:
            cp.wait()

    return pl.pallas_call(
        body, name=name, in_specs=[HBM_SPEC], out_specs=HBM_SPEC,
        out_shape=jax.ShapeDtypeStruct((3,) + xs.shape[1:], xs.dtype),
        scratch_shapes=[pltpu.SemaphoreType.DMA((3,)), pltpu.SemaphoreType.DMA((3,))],
    )(xs)


def _add2(a, b, name):
    lead, rows = a.shape[0], a.shape[1]
    tr = _pick(rows, 1024, 8)

    def body(a_ref, b_ref, o_ref):
        o_ref[...] = a_ref[...] + b_ref[...]

    spec = pl.BlockSpec((None, tr, LANES), lambda k, i: (k, i, 0))
    return pl.pallas_call(
        body, name=name, grid=(lead, rows // tr), in_specs=[spec, spec], out_specs=spec,
        out_shape=jax.ShapeDtypeStruct(a.shape, F32), compiler_params=_params(("parallel", "parallel")),
    )(a, b)


def _adamw_rows(w, g, m, v):
    m2 = ADAM_B1 * m + (1.0 - ADAM_B1) * g
    v2 = ADAM_B2 * v + (1.0 - ADAM_B2) * (g * g)
    m_hat = m2 / (1.0 - ADAM_B1 ** ADAM_STEP)
    v_hat = v2 / (1.0 - ADAM_B2 ** ADAM_STEP)
    delta = -ADAM_LR * (m_hat / (jnp.sqrt(v_hat) + ADAM_EPS) + ADAM_WD * w)
    return delta, m2, v2


def _sum_adamw(parts, w, m, v, name):
    rows = w.shape[0]
    tr = _pick(rows, 512, 8)
    counts = [1 if p.ndim == 2 else p.shape[0] for p in parts]

    def body(*refs):
        p_refs = refs[:len(parts)]
        w_ref, m_ref, v_ref, g_ref, d_ref, m2_ref, v2_ref = refs[len(parts):]
        g = None
        for ref, n in zip(p_refs, counts):
            for k in range(n):
                val = ref[...] if ref.shape == (tr, LANES) else ref[k]
                g = val if g is None else g + val
        delta, m2, v2 = _adamw_rows(w_ref[...], g, m_ref[...], v_ref[...])
        g_ref[...] = g
        d_ref[...] = delta
        m2_ref[...] = m2
        v2_ref[...] = v2

    flat = pl.BlockSpec((tr, LANES), lambda i: (i, 0))
    specs = [flat if p.ndim == 2 else pl.BlockSpec((p.shape[0], tr, LANES), lambda i: (0, i, 0)) for p in parts]
    out = jax.ShapeDtypeStruct((rows, LANES), F32)
    return pl.pallas_call(
        body, name=name, grid=(rows // tr,), in_specs=specs + [flat, flat, flat], out_specs=[flat] * 4,
        out_shape=[out] * 4, compiler_params=_params(("parallel",)),
    )(*parts, w, m, v)


def _to_slab(parts):
    return jnp.concatenate([p.reshape(-1, LANES) for p in parts], axis=0)


def _small_rows(name):
    n = {'g_mix': 1024, 'g_cq': 384, 'g_ckv': 256, 'g_q_nope': 128, 'g_q_pe': 64, 'g_k_nope': 128, 'g_k_pe': 64,
         'g_gm_ln': 512, 'b_gm_ln': 512, 'w_spatial': GM_GROUPS * GM_CHUNK * GM_CHUNK, 'b_spatial': GM_GROUPS * GM_CHUNK,
         'g_mem': 1024, 'g_mq': 128, 'g_mk': 128, 'g_ffn': 1024}[name]
    return n, -(-n // LANES)


def _small_slab(d):
    parts = []
    for name in SMALL:
        n, rows = _small_rows(name)
        parts.append(jnp.pad(d[name].reshape(-1).astype(F32), (0, rows * LANES - n)).reshape(rows, LANES))
    slab = jnp.concatenate(parts, axis=0)
    return jnp.pad(slab, ((0, -slab.shape[0] % 8), (0, 0)))


def _small_unslab(slab, like):
    out, r = {}, 0
    for name in SMALL:
        n, rows = _small_rows(name)
        out[name] = slab[r:r + rows].reshape(-1)[:n].reshape(like[name].shape)
        r += rows
    return out


def _shard_rows(name):
    r, c = BIG_SHAPE[name]
    return r * c // N_DEV // LANES


def _full_from_gathered(gathered, name, r0):
    r, c = BIG_SHAPE[name]
    n = _shard_rows(name)
    blk = gathered[:, r0:r0 + n]
    if BIG_AXIS[name] == 0:
        return blk.reshape(r, c)
    return blk.reshape(N_DEV, r, c // N_DEV).transpose(1, 0, 2).reshape(r, c)


def _shards_of_full(g, name):
    r, c = BIG_SHAPE[name]
    if BIG_AXIS[name] == 0:
        return g.reshape(N_DEV, -1, LANES)
    return g.reshape(r, N_DEV, c // N_DEV).transpose(1, 0, 2).reshape(N_DEV, -1, LANES)


def kernel(x, mem, positions, g_mix, w_in, g_cq, w_uq, g_ckv, w_ukv, g_q_nope, g_q_pe, g_k_nope, g_k_pe, g_gm_ln, b_gm_ln, w_spatial, b_spatial, g_mem, w_mem_kv, g_mq, g_mk, w_o_gm, w_o_mla, w_o_mem, w_out, g_ffn, w_ff1, w_ff2, loss_target, m_g_mix, m_w_in, m_g_cq, m_w_uq, m_g_ckv, m_w_ukv, m_g_q_nope, m_g_q_pe, m_g_k_nope, m_g_k_pe, m_g_gm_ln, m_b_gm_ln, m_w_spatial, m_b_spatial, m_g_mem, m_w_mem_kv, m_g_mq, m_g_mk, m_w_o_gm, m_w_o_mla, m_w_o_mem, m_w_out, m_g_ffn, m_w_ff1, m_w_ff2, v_g_mix, v_w_in, v_g_cq, v_w_uq, v_g_ckv, v_w_ukv, v_g_q_nope, v_g_q_pe, v_g_k_nope, v_g_k_pe, v_g_gm_ln, v_b_gm_ln, v_w_spatial, v_b_spatial, v_g_mem, v_w_mem_kv, v_g_mq, v_g_mk, v_w_o_gm, v_w_o_mla, v_w_o_mem, v_w_out, v_g_ffn, v_w_ff1, v_w_ff2):
    given = dict(locals())
    w = {n: given[n][0] for n in WEIGHTS}
    mom = {n: given['m_' + n][0] for n in WEIGHTS}
    var = {n: given['v_' + n][0] for n in WEIGHTS}
    cx, cy, cc_ = lax.axis_index("x"), lax.axis_index("y"), lax.axis_index("c")

    gathered = _all_gather8(_to_slab([w[n].astype(BF16) for n in BIG]), "ag_weights")
    full, r0 = {}, 0
    for n in BIG:
        full[n] = _full_from_gathered(gathered, n, r0)
        r0 += _shard_rows(n)

    loss_part, grad_x, G = _local_step(x, mem, positions, loss_target, full, {n: w[n] for n in SMALL})
    loss = lax.psum(0.5 * jnp.sum(loss_part) / D_MODEL, ("x", "y", "c"))

    shards = jnp.concatenate([_shards_of_full(G[n], n) for n in BIG], axis=1)
    R = shards.shape[1]
    by_chip = shards.reshape(4, 2, R, LANES)
    keep = lax.dynamic_index_in_dim(by_chip, cc_, axis=1, keepdims=False)
    give = lax.dynamic_index_in_dim(by_chip, 1 - cc_, axis=1, keepdims=False)
    chip_sum = _add2(keep, _swap_sibling(give, "rs_sibling"), "rs_add")
    from_chips = _swap_chips(chip_sum, "rs_chips")
    own = lax.dynamic_index_in_dim(chip_sum, 2 * cx + cy, axis=0, keepdims=False)
    g_big, d_big, m_big, v_big = _sum_adamw([own, from_chips], _to_slab([w[n] for n in BIG]), _to_slab([mom[n] for n in BIG]),
                                            _to_slab([var[n] for n in BIG]), "adamw_big")

    parts = _all_gather8(_small_slab(G), "ag_small")
    g_sm, d_sm, m_sm, v_sm = _sum_adamw([parts], _small_slab(w), _small_slab(mom), _small_slab(var), "adamw_small")

    outs = {}
    for prefix, big_slab, small_slab in (("grad_", g_big, g_sm), ("delta_", d_big, d_sm), ("new_m_", m_big, m_sm),
                                         ("new_v_", v_big, v_sm)):
        r0 = 0
        for n in BIG:
            k = _shard_rows(n)
            outs[prefix + n] = big_slab[r0:r0 + k].reshape(given[n].shape)
            r0 += k
        sm = _small_unslab(small_slab, given)
        for n in SMALL:
            outs[prefix + n] = sm[n]
    return (loss, grad_x, *[outs[p + n] for p in ("grad_", "delta_", "new_m_", "new_v_") for n in WEIGHTS])
```

```python
import functools
import math

import jax
import jax.numpy as jnp
from jax import lax
from jax.experimental import pallas as pl
from jax.experimental.pallas import tpu as pltpu

F32 = jnp.float32
BF16 = jnp.bfloat16

D_MODEL = 1024
MEM_HEADS = 4
HEAD = 128
GM_WIDTH = 512
GM_CHUNK = 128
GM_GROUPS = 4
MLA_HEADS = 8
MLA_ROPE = 64
Q_LORA = 384
KV_LORA = 256
D_FF = 4096
EPS = 1e-6
ROPE_BASE = 10000.0
MLA_SCALE = 1.0 / math.sqrt(HEAD + MLA_ROPE)
MEM_SCALE = 1.0 / math.sqrt(HEAD)
LOG2E = 1.4426950408889634
LN2 = 0.6931471805599453
ATT_TILE = 256
C_ZU, C_ZV, C_CQ, C_CKV, C_KPE, C_QM, C_ZG, C_END = 0, 512, 1024, 1408, 1664, 1728, 2240, 5312
Z_GATE, Z_QM, Z_MLA, Z_KPE, Z_COLS = 1024, 4096, 4608, 5248, 5376
MLA_W = 768
QCAT = 2 * HEAD
ADAM_LR, ADAM_B1, ADAM_B2, ADAM_EPS, ADAM_WD, ADAM_STEP = 0.001, 0.9, 0.999, 1e-08, 0.01, 10
N_DEV = 8
LANES = 128
VMEM_LIMIT = 48 * 1024 * 1024
NEG = -1e30

BIG = ['w_in', 'w_uq', 'w_ukv', 'w_mem_kv', 'w_o_gm', 'w_o_mla', 'w_o_mem', 'w_out', 'w_ff1', 'w_ff2']
BIG_AXIS = {'w_in': 1, 'w_uq': 1, 'w_ukv': 1, 'w_mem_kv': 0, 'w_o_gm': 1, 'w_o_mla': 0, 'w_o_mem': 1,
            'w_out': 0, 'w_ff1': 1, 'w_ff2': 0}
BIG_SHAPE = {'w_in': (1024, 5312), 'w_uq': (384, 1536), 'w_ukv': (256, 2048), 'w_mem_kv': (1024, 1024),
             'w_o_gm': (512, 1024), 'w_o_mla': (1024, 1024), 'w_o_mem': (512, 1024), 'w_out': (1024, 1024),
             'w_ff1': (1024, 4096), 'w_ff2': (4096, 1024)}
SMALL = ['g_mix', 'g_cq', 'g_ckv', 'g_q_nope', 'g_q_pe', 'g_k_nope', 'g_k_pe', 'g_gm_ln', 'b_gm_ln',
         'w_spatial', 'b_spatial', 'g_mem', 'g_mq', 'g_mk', 'g_ffn']
WEIGHTS = ['g_mix', 'w_in', 'g_cq', 'w_uq', 'g_ckv', 'w_ukv', 'g_q_nope', 'g_q_pe', 'g_k_nope', 'g_k_pe',
           'g_gm_ln', 'b_gm_ln', 'w_spatial', 'b_spatial', 'g_mem', 'w_mem_kv', 'g_mq', 'g_mk', 'w_o_gm',
           'w_o_mla', 'w_o_mem', 'w_out', 'g_ffn', 'w_ff1', 'w_ff2']


def _pick(n, target, mult=LANES):
    best = None
    t = mult
    while t <= min(n, target):
        if n % t == 0:
            best = t
        t += mult
    return best if best is not None else n


def _params(sem):
    return pltpu.CompilerParams(dimension_semantics=sem, vmem_limit_bytes=VMEM_LIMIT)


MESH = pl.DeviceIdType.MESH
HBM_SPEC = pl.BlockSpec(memory_space=pltpu.HBM)


class _Exchange:
    def __init__(self, src, scatter):
        self.src, self.scatter = src, scatter
        self.out_shape = jax.ShapeDtypeStruct(src.shape if scatter else (N_DEV,) + src.shape, src.dtype)
        self.scratch = [pltpu.SemaphoreType.DMA((N_DEV - 1,)), pltpu.SemaphoreType.DMA((N_DEV - 1,)),
                        pltpu.SemaphoreType.DMA]

    def _copies(self, src_ref, dst_ref, send_sems, recv_sems, local_sem):
        x, y, c = lax.axis_index("x"), lax.axis_index("y"), lax.axis_index("c")
        me = 4 * x + 2 * y + c

        def mine_for(dev):
            return src_ref.at[dev] if self.scatter else src_ref

        local = pltpu.make_async_copy(mine_for(me), dst_ref.at[me], local_sem)
        remote = []
        for k in range(1, N_DEV):
            px = 1 - x if k & 4 else x
            py = 1 - y if k & 2 else y
            pc = 1 - c if k & 1 else c
            remote.append(pltpu.make_async_remote_copy(
                src_ref=mine_for(4 * px + 2 * py + pc), dst_ref=dst_ref.at[me], send_sem=send_sems.at[k - 1],
                recv_sem=recv_sems.at[k - 1], device_id=(px, py, pc), device_id_type=MESH))
        return local, remote

    def start(self, *refs):
        local, remote = self._copies(*refs)
        local.start()
        for cp in remote:
            cp.start()

    def wait(self, *refs):
        local, remote = self._copies(*refs)
        for cp in remote:
            cp.wait()
        local.wait()


def _ride(rider, refs, first, last):
    @pl.when(first)
    def _():
        rider.start(*refs)

    @pl.when(last)
    def _():
        rider.wait(*refs)


def _matmul(a, b, mode, out_dtype, name, add=None, relu2_a=False, relu2_grad=None,
            tm_t=2048, tn_t=512, tk_t=1024, rider=None):
    if mode == 'nn':
        (M, K), (K2, N) = a.shape, b.shape
    elif mode == 'nt':
        (M, K), (N, K2) = a.shape, b.shape
    else:
        (K, M), (K2, N) = a.shape, b.shape
    assert K == K2, (name, a.shape, b.shape)
    tm, tn, tk = _pick(M, tm_t), _pick(N, tn_t), _pick(K, tk_t)
    gm, gn, nk = M // tm, N // tn, K // tk
    if mode == 'nn':
        a_spec = pl.BlockSpec((tm, tk), lambda i, j, k: (i, k))
        b_spec = pl.BlockSpec((tk, tn), lambda i, j, k: (k, j))
        dims = (((1,), (0,)), ((), ()))
    elif mode == 'nt':
        a_spec = pl.BlockSpec((tm, tk), lambda i, j, k: (i, k))
        b_spec = pl.BlockSpec((tn, tk), lambda i, j, k: (j, k))
        dims = (((1,), (1,)), ((), ()))
    else:
        a_spec = pl.BlockSpec((tk, tm), lambda i, j, k: (k, i))
        b_spec = pl.BlockSpec((tk, tn), lambda i, j, k: (k, j))
        dims = (((0,), (0,)), ((), ()))
    o_spec = pl.BlockSpec((tm, tn), lambda i, j, k: (i, j))
    has_add, has_e = add is not None, relu2_grad is not None

    def body(*refs):
        a_ref, b_ref = refs[0], refs[1]
        pos = 2
        add_ref = e_ref = None
        if has_add:
            add_ref = refs[pos]
            pos += 1
        if has_e:
            e_ref = refs[pos]
            pos += 1
        if rider is not None:
            ride_refs = (refs[pos], refs[pos + 2]) + tuple(refs[len(refs) - 3:])
            o_ref = refs[pos + 1]
            pos += 1
            i, j, kk = pl.program_id(0), pl.program_id(1), pl.program_id(2)
            _ride(rider, ride_refs, (i == 0) & (j == 0) & (kk == 0), (i == gm - 1) & (j == gn - 1) & (kk == nk - 1))
        else:
            o_ref = refs[pos]
        acc_ref = refs[pos + 2 if rider is not None else pos + 1] if nk > 1 else None

        av = a_ref[...]
        if relu2_a:
            av = jnp.maximum(av.astype(F32), 0.0)
            av = av * av
        prod = lax.dot_general(av.astype(BF16), b_ref[...].astype(BF16), dims, preferred_element_type=F32)

        def finish(r):
            if has_add:
                r = r + add_ref[...]
            if has_e:
                r = r * (2.0 * jnp.maximum(e_ref[...].astype(F32), 0.0))
            o_ref[...] = r.astype(out_dtype)

        if nk == 1:
            finish(prod)
        else:
            k = pl.program_id(2)

            @pl.when(k == 0)
            def _():
                acc_ref[...] = prod

            @pl.when(k > 0)
            def _():
                acc_ref[...] += prod

            @pl.when(k == nk - 1)
            def _():
                finish(acc_ref[...])

    ins, specs = [a, b], [a_spec, b_spec]
    if has_add:
        ins.append(add)
        specs.append(o_spec)
    if has_e:
        ins.append(relu2_grad)
        specs.append(o_spec)
    out_specs, out_shape = o_spec, jax.ShapeDtypeStruct((M, N), out_dtype)
    scratch = [pltpu.VMEM((tm, tn), F32)] if nk > 1 else []
    sem = ("parallel", "parallel", "arbitrary")
    if rider is not None:
        ins.append(rider.src)
        specs.append(HBM_SPEC)
        out_specs, out_shape = [o_spec, HBM_SPEC], [out_shape, rider.out_shape]
        scratch = scratch + rider.scratch
        sem = ("arbitrary", "arbitrary", "arbitrary")
    return pl.pallas_call(
        body, name=name, grid=(gm, gn, nk), in_specs=specs, out_specs=out_specs, out_shape=out_shape,
        scratch_shapes=scratch, compiler_params=_params(sem),
    )(*ins)


def _row_tile(rows, target=256):
    return _pick(rows, target, 8)


def _rowspec(tr, width, col=0):
    return pl.BlockSpec((tr, width), lambda i, col=col: (i, col))


def _fullspec(shape):
    nd = len(shape)
    return pl.BlockSpec(shape, lambda i, nd=nd: (0,) * nd)


def _rms(x, width):
    return lax.rsqrt(jnp.sum(x * x, axis=-1, keepdims=True) * (1.0 / width) + EPS)


def _rms_bwd_rows(x, g, dy, width):
    r = _rms(x, width)
    xh = x * r
    dn = dy * g
    dx = r * (dn - xh * (jnp.sum(dn * xh, axis=-1, keepdims=True) * (1.0 / width)))
    return dx, dy * xh


def _acc_rows(ref, val, first):
    s = jnp.sum(val, axis=0, keepdims=True)

    @pl.when(first)
    def _():
        ref[...] = s

    @pl.when(jnp.logical_not(first))
    def _():
        ref[...] += s


def _rms_fwd(x, g, name):
    rows, width = x.shape
    tr = _row_tile(rows)

    def body(x_ref, g_ref, o_ref):
        xv = x_ref[...]
        o_ref[...] = (xv * _rms(xv, width) * g_ref[...]).astype(BF16)

    return pl.pallas_call(
        body, name=name, grid=(rows // tr,),
        in_specs=[_rowspec(tr, width), _fullspec((1, width))], out_specs=_rowspec(tr, width),
        out_shape=jax.ShapeDtypeStruct((rows, width), BF16), compiler_params=_params(("parallel",)),
    )(x, g)


def _rms_bwd(x, g, dy, res, name):
    rows, width = x.shape
    tr = _row_tile(rows)
    has_res = res is not None

    def body(*refs):
        if has_res:
            x_ref, g_ref, dy_ref, res_ref, dx_ref, dg_ref = refs
        else:
            x_ref, g_ref, dy_ref, dx_ref, dg_ref = refs
        dx, dgv = _rms_bwd_rows(x_ref[...], g_ref[...], dy_ref[...], width)
        if has_res:
            dx = dx + res_ref[...]
        dx_ref[...] = dx
        _acc_rows(dg_ref, dgv, pl.program_id(0) == 0)

    ins = [x, g, dy] + ([res] if has_res else [])
    specs = [_rowspec(tr, width), _fullspec((1, width)), _rowspec(tr, width)] + ([_rowspec(tr, width)] if has_res else [])
    return pl.pallas_call(
        body, name=name, grid=(rows // tr,), in_specs=specs,
        out_specs=[_rowspec(tr, width), _fullspec((1, width))],
        out_shape=[jax.ShapeDtypeStruct((rows, width), F32), jax.ShapeDtypeStruct((1, width), F32)],
        compiler_params=_params(("arbitrary",)),
    )(*ins)


_GELU_C = math.sqrt(2.0 / math.pi)


def _gelu(x):
    t = jnp.tanh(_GELU_C * (x + 0.044715 * (x * x * x)))
    return 0.5 * x * (1.0 + t), t


def _gelu_grad(x, t):
    return 0.5 * (1.0 + t) + 0.5 * x * (1.0 - t * t) * (_GELU_C * (1.0 + 3.0 * 0.044715 * (x * x)))


def _gm_forward_rows(zu, zv, gln, bln, wc_ref, bst, n_chunk):
    u, tu = _gelu(zu)
    a, ta = _gelu(zv)
    mu = jnp.mean(a, axis=-1, keepdims=True)
    ac = a - mu
    rs = lax.rsqrt(jnp.mean(ac * ac, axis=-1, keepdims=True) + EPS)
    n = ac * rs
    v = n * gln + bln
    vb = v.astype(BF16)
    rows = []
    for c in range(n_chunk):
        cols = []
        for g in range(GM_GROUPS):
            vc = vb[c * GM_CHUNK:(c + 1) * GM_CHUNK, g * LANES:(g + 1) * LANES]
            mixed = jnp.dot(wc_ref[g], vc, preferred_element_type=F32) + bst[g]
            cols.append(mixed)
        rows.append(jnp.concatenate(cols, axis=1))
    mixed = jnp.concatenate(rows, axis=0) if n_chunk > 1 else rows[0]
    return u, tu, ta, n, rs, v, mixed


def _gm_fwd(z, gln, bln, wc, bst, name):
    rows = z.shape[0]
    tr = _pick(rows, 512, GM_CHUNK)
    n_chunk = tr // GM_CHUNK

    def body(zu_ref, zv_ref, gln_ref, bln_ref, wc_ref, bst_ref, o_ref):
        u, _, _, _, _, _, mixed = _gm_forward_rows(zu_ref[...], zv_ref[...], gln_ref[...], bln_ref[...], wc_ref,
                                                   bst_ref, n_chunk)
        o_ref[...] = (u * mixed).astype(BF16)

    return pl.pallas_call(
        body, name=name, grid=(rows // tr,),
        in_specs=[_rowspec(tr, GM_WIDTH, 0), _rowspec(tr, GM_WIDTH, 1), _fullspec((1, GM_WIDTH)), _fullspec((1, GM_WIDTH)),
                  _fullspec((GM_GROUPS, GM_CHUNK, GM_CHUNK)), _fullspec((GM_GROUPS, GM_CHUNK, LANES))],
        out_specs=_rowspec(tr, GM_WIDTH), out_shape=jax.ShapeDtypeStruct((rows, GM_WIDTH), BF16),
        compiler_params=_params(("parallel",)),
    )(z, z, gln, bln, wc, bst)


def _gm_bwd(z, dy, gln, bln, wc, wct, bst, name):
    rows = z.shape[0]
    tr = _pick(rows, 512, GM_CHUNK)
    n_chunk = tr // GM_CHUNK

    def body(zu_ref, zv_ref, dy_ref, gln_ref, bln_ref, wc_ref, wct_ref, bst_ref, dz_ref, dws_ref, dbs_ref, dgl_ref, dbl_ref):
        first = pl.program_id(0) == 0
        zu, zv, gln = zu_ref[...], zv_ref[...], gln_ref[...]
        u, tu, ta, n, rs, v, mixed = _gm_forward_rows(zu, zv, gln, bln_ref[...], wc_ref, bst_ref, n_chunk)
        dyv = dy_ref[...]
        dzu = dyv * mixed * _gelu_grad(zu, tu)
        dmix = dyv * u
        dmb = dmix.astype(BF16)
        vb = v.astype(BF16)
        dv_rows, dws, dbs = [], [None] * GM_GROUPS, None
        for c in range(n_chunk):
            rsl = slice(c * GM_CHUNK, (c + 1) * GM_CHUNK)
            cols = []
            for g in range(GM_GROUPS):
                csl = slice(g * LANES, (g + 1) * LANES)
                dmc = dmb[rsl, csl]
                cols.append(jnp.dot(wct_ref[g], dmc, preferred_element_type=F32))
                w_part = lax.dot_general(dmc, vb[rsl, csl], (((1,), (1,)), ((), ())), preferred_element_type=F32)
                dws[g] = w_part if dws[g] is None else dws[g] + w_part
            dv_rows.append(jnp.concatenate(cols, axis=1))
            dbs = dmix[rsl, :] if dbs is None else dbs + dmix[rsl, :]
        dv = jnp.concatenate(dv_rows, axis=0) if n_chunk > 1 else dv_rows[0]
        dn = dv * gln
        da = rs * (dn - jnp.mean(dn, axis=-1, keepdims=True) - n * jnp.mean(dn * n, axis=-1, keepdims=True))
        dzv = da * _gelu_grad(zv, ta)
        dz_ref[:, 0:GM_WIDTH] = dzu.astype(BF16)
        dz_ref[:, GM_WIDTH:2 * GM_WIDTH] = dzv.astype(BF16)
        _acc_rows(dgl_ref, dv * n, first)
        _acc_rows(dbl_ref, dv, first)

        @pl.when(first)
        def _():
            for g in range(GM_GROUPS):
                dws_ref[g] = dws[g]
            dbs_ref[...] = dbs

        @pl.when(jnp.logical_not(first))
        def _():
            for g in range(GM_GROUPS):
                dws_ref[g] += dws[g]
            dbs_ref[...] += dbs

    wspec = _fullspec((GM_GROUPS, GM_CHUNK, GM_CHUNK))
    return pl.pallas_call(
        body, name=name, grid=(rows // tr,),
        in_specs=[_rowspec(tr, GM_WIDTH, 0), _rowspec(tr, GM_WIDTH, 1), _rowspec(tr, GM_WIDTH), _fullspec((1, GM_WIDTH)),
                  _fullspec((1, GM_WIDTH)), wspec, wspec, wspec],
        out_specs=[_rowspec(tr, 2 * GM_WIDTH), wspec, _fullspec((GM_CHUNK, GM_WIDTH)), _fullspec((1, GM_WIDTH)),
                   _fullspec((1, GM_WIDTH))],
        out_shape=[jax.ShapeDtypeStruct((rows, 2 * GM_WIDTH), BF16), jax.ShapeDtypeStruct((GM_GROUPS, GM_CHUNK, GM_CHUNK), F32),
                   jax.ShapeDtypeStruct((GM_CHUNK, GM_WIDTH), F32), jax.ShapeDtypeStruct((1, GM_WIDTH), F32),
                   jax.ShapeDtypeStruct((1, GM_WIDTH), F32)],
        compiler_params=_params(("arbitrary",)),
    )(z, z, dy, gln, bln, wc, wct, bst)


def _lat_fwd(z, g_cq, g_ckv, name):
    rows = z.shape[0]
    tr = _row_tile(rows)

    def body(z_ref, gq_ref, gkv_ref, nq_ref, nkv_ref):
        zb = z_ref[...]
        cq, ckv = zb[:, 0:Q_LORA], zb[:, Q_LORA:Q_LORA + KV_LORA]
        nq_ref[...] = (cq * _rms(cq, Q_LORA) * gq_ref[...]).astype(BF16)
        nkv_ref[...] = (ckv * _rms(ckv, KV_LORA) * gkv_ref[...]).astype(BF16)

    return pl.pallas_call(
        body, name=name, grid=(rows // tr,),
        in_specs=[_rowspec(tr, MLA_W, Z_MLA // MLA_W), _fullspec((1, Q_LORA)), _fullspec((1, KV_LORA))],
        out_specs=[_rowspec(tr, Q_LORA), _rowspec(tr, KV_LORA)],
        out_shape=[jax.ShapeDtypeStruct((rows, Q_LORA), BF16), jax.ShapeDtypeStruct((rows, KV_LORA), BF16)],
        compiler_params=_params(("parallel",)),
    )(z, g_cq, g_ckv)


def _lat_bwd(z, dnq, dnkv, dkpe, g_cq, g_ckv, name):
    rows = z.shape[0]
    tr = _row_tile(rows)

    def body(z_ref, dnq_ref, dnkv_ref, dkpe_ref, gq_ref, gkv_ref, dz_ref, dgq_ref, dgkv_ref):
        first = pl.program_id(0) == 0
        zb = z_ref[...]
        dcq, dgq = _rms_bwd_rows(zb[:, 0:Q_LORA], gq_ref[...], dnq_ref[...], Q_LORA)
        dckv, dgkv = _rms_bwd_rows(zb[:, Q_LORA:Q_LORA + KV_LORA], gkv_ref[...], dnkv_ref[...], KV_LORA)
        dz_ref[:, 0:Q_LORA] = dcq.astype(BF16)
        dz_ref[:, Q_LORA:Q_LORA + KV_LORA] = dckv.astype(BF16)
        dz_ref[:, Q_LORA + KV_LORA:MLA_W] = dkpe_ref[...].astype(BF16)
        _acc_rows(dgq_ref, dgq, first)
        _acc_rows(dgkv_ref, dgkv, first)

    return pl.pallas_call(
        body, name=name, grid=(rows // tr,),
        in_specs=[_rowspec(tr, MLA_W, Z_MLA // MLA_W), _rowspec(tr, Q_LORA), _rowspec(tr, KV_LORA), _rowspec(tr, LANES),
                  _fullspec((1, Q_LORA)), _fullspec((1, KV_LORA))],
        out_specs=[_rowspec(tr, MLA_W), _fullspec((1, Q_LORA)), _fullspec((1, KV_LORA))],
        out_shape=[jax.ShapeDtypeStruct((rows, MLA_W), BF16), jax.ShapeDtypeStruct((1, Q_LORA), F32),
                   jax.ShapeDtypeStruct((1, KV_LORA), F32)],
        compiler_params=_params(("arbitrary",)),
    )(z, dnq, dnkv, dkpe, g_cq, g_ckv)


def _rope(y, cc, ss):
    return y * cc + pltpu.roll(y, 64, 1) * ss


def _rope_bwd(d, cc, ss):
    return d * cc + pltpu.roll(d * ss, 64, 1)


def _qk_fwd(q, kv, z, cc, ss, gqn, gqp, gkn, gkp, name):
    rows = q.shape[0]
    tr = _row_tile(rows)
    W = MLA_HEADS * HEAD
    QS = MLA_SCALE * LOG2E

    def body(q_ref, kv_ref, kpe_ref, cc_ref, ss_ref, gqn_ref, gqp_ref, gkn_ref, gkp_ref, qc_ref, kc_ref, v_ref):
        cc, ss = cc_ref[...], ss_ref[...]
        kpe = kpe_ref[...]
        kp = _rope(kpe * _rms(kpe, MLA_ROPE) * gkp_ref[...], cc, ss).astype(BF16)
        for h in range(MLA_HEADS):
            qn = q_ref[:, h * HEAD:(h + 1) * HEAD]
            qp = q_ref[:, W + h * HEAD:W + (h + 1) * HEAD]
            kn = kv_ref[:, h * HEAD:(h + 1) * HEAD]
            qc_ref[:, h * QCAT:h * QCAT + HEAD] = (qn * _rms(qn, HEAD) * gqn_ref[...] * QS).astype(BF16)
            qc_ref[:, h * QCAT + HEAD:(h + 1) * QCAT] = (_rope(qp * _rms(qp, MLA_ROPE) * gqp_ref[...], cc, ss) * QS).astype(BF16)
            kc_ref[:, h * QCAT:h * QCAT + HEAD] = (kn * _rms(kn, HEAD) * gkn_ref[...]).astype(BF16)
            kc_ref[:, h * QCAT + HEAD:(h + 1) * QCAT] = kp
        v_ref[...] = kv_ref[:, W:2 * W].astype(BF16)

    g = _fullspec((1, HEAD))
    return pl.pallas_call(
        body, name=name, grid=(rows // tr,),
        in_specs=[_rowspec(tr, 2 * W), _rowspec(tr, 2 * W), _rowspec(tr, LANES, Z_KPE // LANES), _rowspec(tr, LANES),
                  _rowspec(tr, LANES), g, g, g, g],
        out_specs=[_rowspec(tr, MLA_HEADS * QCAT), _rowspec(tr, MLA_HEADS * QCAT), _rowspec(tr, W)],
        out_shape=[jax.ShapeDtypeStruct((rows, MLA_HEADS * QCAT), BF16), jax.ShapeDtypeStruct((rows, MLA_HEADS * QCAT), BF16),
                   jax.ShapeDtypeStruct((rows, W), BF16)],
        compiler_params=_params(("parallel",)),
    )(q, kv, z, cc, ss, gqn, gqp, gkn, gkp)


def _qk_bwd(q, kv, z, cc, ss, gqn, gqp, gkn, gkp, dqc, dkc, dv, name):
    rows = q.shape[0]
    tr = _row_tile(rows)
    W = MLA_HEADS * HEAD

    def body(q_ref, kv_ref, kpe_ref, cc_ref, ss_ref, gqn_ref, gqp_ref, gkn_ref, gkp_ref, dqc_ref, dkc_ref, dv_ref,
             dq_ref, dkv_ref, dkpe_ref, dgqn_ref, dgqp_ref, dgkn_ref, dgkp_ref):
        first = pl.program_id(0) == 0
        cc, ss = cc_ref[...], ss_ref[...]
        sqn = sqp = skn = dkp = None
        for h in range(MLA_HEADS):
            dx, dg = _rms_bwd_rows(q_ref[:, h * HEAD:(h + 1) * HEAD], gqn_ref[...], dqc_ref[:, h * QCAT:h * QCAT + HEAD], HEAD)
            dq_ref[:, h * HEAD:(h + 1) * HEAD] = dx.astype(BF16)
            sqn = dg if sqn is None else sqn + dg
            dy = _rope_bwd(dqc_ref[:, h * QCAT + HEAD:(h + 1) * QCAT], cc, ss)
            dx, dg = _rms_bwd_rows(q_ref[:, W + h * HEAD:W + (h + 1) * HEAD], gqp_ref[...], dy, MLA_ROPE)
            dq_ref[:, W + h * HEAD:W + (h + 1) * HEAD] = dx.astype(BF16)
            sqp = dg if sqp is None else sqp + dg
            dx, dg = _rms_bwd_rows(kv_ref[:, h * HEAD:(h + 1) * HEAD], gkn_ref[...], dkc_ref[:, h * QCAT:h * QCAT + HEAD], HEAD)
            dkv_ref[:, h * HEAD:(h + 1) * HEAD] = dx.astype(BF16)
            skn = dg if skn is None else skn + dg
            part = dkc_ref[:, h * QCAT + HEAD:(h + 1) * QCAT]
            dkp = part if dkp is None else dkp + part
        dkv_ref[:, W:2 * W] = dv_ref[...].astype(BF16)
        dx, dg = _rms_bwd_rows(kpe_ref[...], gkp_ref[...], _rope_bwd(dkp, cc, ss), MLA_ROPE)
        dkpe_ref[...] = dx
        _acc_rows(dgqn_ref, sqn, first)
        _acc_rows(dgqp_ref, sqp, first)
        _acc_rows(dgkn_ref, skn, first)
        _acc_rows(dgkp_ref, dg, first)

    g = _fullspec((1, HEAD))
    gs = jax.ShapeDtypeStruct((1, HEAD), F32)
    return pl.pallas_call(
        body, name=name, grid=(rows // tr,),
        in_specs=[_rowspec(tr, 2 * W), _rowspec(tr, 2 * W), _rowspec(tr, LANES, Z_KPE // LANES), _rowspec(tr, LANES),
                  _rowspec(tr, LANES), g, g, g, g, _rowspec(tr, MLA_HEADS * QCAT), _rowspec(tr, MLA_HEADS * QCAT),
                  _rowspec(tr, W)],
        out_specs=[_rowspec(tr, 2 * W), _rowspec(tr, 2 * W), _rowspec(tr, LANES), g, g, g, g],
        out_shape=[jax.ShapeDtypeStruct((rows, 2 * W), BF16), jax.ShapeDtypeStruct((rows, 2 * W), BF16),
                   jax.ShapeDtypeStruct((rows, LANES), F32), gs, gs, gs, gs],
        compiler_params=_params(("arbitrary",)),
    )(q, kv, z, cc, ss, gqn, gqp, gkn, gkp, dqc, dkc, dv)


def _headnorm_fwd(x, col, nheads, g, out_scale, name):
    rows = x.shape[0]
    tr = _row_tile(rows)
    W = nheads * HEAD

    def body(x_ref, g_ref, o_ref):
        for h in range(nheads):
            xv = x_ref[:, h * HEAD:(h + 1) * HEAD]
            o_ref[:, h * HEAD:(h + 1) * HEAD] = (xv * _rms(xv, HEAD) * g_ref[...] * out_scale).astype(BF16)

    return pl.pallas_call(
        body, name=name, grid=(rows // tr,),
        in_specs=[_rowspec(tr, W, col), _fullspec((1, HEAD))], out_specs=_rowspec(tr, W),
        out_shape=jax.ShapeDtypeStruct((rows, W), BF16), compiler_params=_params(("parallel",)),
    )(x, g)


def _headnorm_bwd(x, col, nheads, g, dy, tail, name):
    rows = x.shape[0]
    tr = _row_tile(rows)
    W = nheads * HEAD
    has_tail = tail is not None
    WO = 2 * W if has_tail else W

    def body(*refs):
        if has_tail:
            x_ref, g_ref, dy_ref, t_ref, dx_ref, dg_ref = refs
        else:
            x_ref, g_ref, dy_ref, dx_ref, dg_ref = refs
        acc = None
        for h in range(nheads):
            sl = slice(h * HEAD, (h + 1) * HEAD)
            dx, dg = _rms_bwd_rows(x_ref[:, sl], g_ref[...], dy_ref[:, sl], HEAD)
            dx_ref[:, sl] = dx.astype(BF16)
            acc = dg if acc is None else acc + dg
        if has_tail:
            dx_ref[:, W:2 * W] = t_ref[...].astype(BF16)
        _acc_rows(dg_ref, acc, pl.program_id(0) == 0)

    ins = [x, g, dy] + ([tail] if has_tail else [])
    specs = [_rowspec(tr, W, col), _fullspec((1, HEAD)), _rowspec(tr, W)] + ([_rowspec(tr, W)] if has_tail else [])
    return pl.pallas_call(
        body, name=name, grid=(rows // tr,), in_specs=specs,
        out_specs=[_rowspec(tr, WO), _fullspec((1, HEAD))],
        out_shape=[jax.ShapeDtypeStruct((rows, WO), BF16), jax.ShapeDtypeStruct((1, HEAD), F32)],
        compiler_params=_params(("arbitrary",)),
    )(*ins)


def _sigmoid(x):
    return 1.0 / (1.0 + jnp.exp(-x))


def _merge_fwd(z, y_gm, y_mla, y_mem, name):
    rows = z.shape[0]
    tr = _row_tile(rows)

    def body(g0_ref, g1_ref, g2_ref, a_ref, b_ref, c_ref, o_ref):
        m = _sigmoid(g0_ref[...]) * a_ref[...] + _sigmoid(g1_ref[...]) * b_ref[...] + _sigmoid(g2_ref[...]) * c_ref[...]
        o_ref[...] = m.astype(BF16)

    r = _rowspec(tr, D_MODEL)
    return pl.pallas_call(
        body, name=name, grid=(rows // tr,),
        in_specs=[_rowspec(tr, D_MODEL, 1), _rowspec(tr, D_MODEL, 2), _rowspec(tr, D_MODEL, 3), r, r, r],
        out_specs=r, out_shape=jax.ShapeDtypeStruct((rows, D_MODEL), BF16), compiler_params=_params(("parallel",)),
    )(z, z, z, y_gm, y_mla, y_mem)


def _merge_bwd(z, y_gm, y_mla, y_mem, dm, name):
    rows = z.shape[0]
    tr = _row_tile(rows)

    def body(g0_ref, g1_ref, g2_ref, a_ref, b_ref, c_ref, dm_ref, da_ref, db_ref, dc_ref, dzg_ref):
        dmv = dm_ref[...]
        for k, (g_ref, y_ref, dy_ref) in enumerate(((g0_ref, a_ref, da_ref), (g1_ref, b_ref, db_ref), (g2_ref, c_ref, dc_ref))):
            s = _sigmoid(g_ref[...])
            dy_ref[...] = (dmv * s).astype(BF16)
            dzg_ref[:, k * D_MODEL:(k + 1) * D_MODEL] = (dmv * y_ref[...] * s * (1.0 - s)).astype(BF16)

    r = _rowspec(tr, D_MODEL)
    o = jax.ShapeDtypeStruct((rows, D_MODEL), BF16)
    return pl.pallas_call(
        body, name=name, grid=(rows // tr,),
        in_specs=[_rowspec(tr, D_MODEL, 1), _rowspec(tr, D_MODEL, 2), _rowspec(tr, D_MODEL, 3), r, r, r, r],
        out_specs=[r, r, r, _rowspec(tr, 3 * D_MODEL)],
        out_shape=[o, o, o, jax.ShapeDtypeStruct((rows, 3 * D_MODEL), BF16)],
        compiler_params=_params(("parallel",)),
    )(z, z, z, y_gm, y_mla, y_mem, dm)


def _loss_head(y, target, name):
    rows, width = y.shape
    tr = _row_tile(rows)

    def body(y_ref, t_ref, dy_ref, l_ref):
        e = y_ref[...] - t_ref[...]
        dy_ref[...] = e * (1.0 / width)
        e2 = e * e
        part = e2[:, 0:LANES]
        for k in range(1, width // LANES):
            part = part + e2[:, k * LANES:(k + 1) * LANES]
        _acc_rows(l_ref, part, pl.program_id(0) == 0)

    return pl.pallas_call(
        body, name=name, grid=(rows // tr,),
        in_specs=[_rowspec(tr, width), _rowspec(tr, width)],
        out_specs=[_rowspec(tr, width), _fullspec((1, LANES))],
        out_shape=[jax.ShapeDtypeStruct((rows, width), F32), jax.ShapeDtypeStruct((1, LANES), F32)],
        compiler_params=_params(("arbitrary",)),
    )(y, target)


_NT = (((1,), (1,)), ((), ()))
_TN = (((0,), (0,)), ((), ()))


def _diag_mask(s):
    row = lax.broadcasted_iota(jnp.int32, s.shape, 0)
    col = lax.broadcasted_iota(jnp.int32, s.shape, 1)
    return jnp.where(row >= col, s, NEG)


def _attn_fwd(q, k, v, nb, nheads, dk, v_col0, causal, name, rider=None):
    S, Skv = q.shape[0] // nb, k.shape[0] // nb
    tq = _pick(Skv, ATT_TILE) if causal else _pick(S, 4 * ATT_TILE)
    nq = S // tq

    def body(*refs):
        if rider is not None:
            q_ref, k_ref, v_ref, src_ref, o_ref, lse_ref, dst_ref, send_sems, recv_sems, local_sem = refs
            b, h = pl.program_id(0), pl.program_id(1)
            _ride(rider, (src_ref, dst_ref, send_sems, recv_sems, local_sem), (b == 0) & (h == 0),
                  (b == nb - 1) & (h == nheads - 1))
        else:
            q_ref, k_ref, v_ref, o_ref, lse_ref = refs
        for i in range(nq):
            r0 = i * tq
            qb = q_ref[r0:r0 + tq, :]
            if causal:
                spans = ([(0, r0, False)] if i > 0 else []) + [(r0, r0 + tq, True)]
            else:
                spans = [(0, Skv, False)]
            scores = []
            for a, b, masked in spans:
                s = lax.dot_general(qb, k_ref[a:b, :], _NT, preferred_element_type=F32)
                scores.append(_diag_mask(s) if masked else s)
            m = functools.reduce(jnp.maximum, [jnp.max(s, axis=-1, keepdims=True) for s in scores])
            l = acc = None
            for s, (a, b, _) in zip(scores, spans):
                p = jnp.exp2(s - m)
                lp = jnp.sum(p, axis=-1, keepdims=True)
                ap = jnp.dot(p.astype(BF16), v_ref[a:b, :].astype(BF16), preferred_element_type=F32)
                l, acc = (lp, ap) if l is None else (l + lp, acc + ap)
            o_ref[r0:r0 + tq, :] = acc / l
            lse_ref[r0:r0 + tq, :] = m + jnp.log2(l)

    ins = [q, k, v]
    in_specs = [pl.BlockSpec((S, dk), lambda b, h: (b, h)), pl.BlockSpec((Skv, dk), lambda b, h: (b, h)),
                pl.BlockSpec((Skv, HEAD), lambda b, h: (b, v_col0 + h))]
    out_specs = [pl.BlockSpec((S, HEAD), lambda b, h: (b, h)), pl.BlockSpec((None, S, 1), lambda b, h: (h, b, 0))]
    out_shape = [jax.ShapeDtypeStruct((nb * S, nheads * HEAD), F32), jax.ShapeDtypeStruct((nheads, nb * S, 1), F32)]
    scratch, sem = [], ("parallel", "parallel")
    if rider is not None:
        ins, in_specs = ins + [rider.src], in_specs + [HBM_SPEC]
        out_specs, out_shape = out_specs + [HBM_SPEC], out_shape + [rider.out_shape]
        scratch, sem = rider.scratch, ("arbitrary", "arbitrary")
    return pl.pallas_call(
        body, name=name, grid=(nb, nheads), in_specs=in_specs, out_specs=out_specs, out_shape=out_shape,
        scratch_shapes=scratch, compiler_params=_params(sem),
    )(*ins)


def _attn_bwd(q, k, v, o, do, lse, nb, nheads, dk, v_col0, scale, causal, name, rider=None):
    S, Skv = q.shape[0] // nb, k.shape[0] // nb
    tk = _pick(Skv, ATT_TILE)
    nkv = Skv // tk

    def body(*refs):
        if rider is not None:
            (q_ref, k_ref, v_ref, o_ref, do_ref, lse_ref, src_ref, dq_ref, dk_ref, dv_ref, dst_ref, delta_ref, dob_ref,
             send_sems, recv_sems, local_sem) = refs
            b, h = pl.program_id(0), pl.program_id(1)
            _ride(rider, (src_ref, dst_ref, send_sems, recv_sems, local_sem), (b == 0) & (h == 0),
                  (b == nb - 1) & (h == nheads - 1))
        else:
            q_ref, k_ref, v_ref, o_ref, do_ref, lse_ref, dq_ref, dk_ref, dv_ref, delta_ref, dob_ref = refs
        dov = do_ref[...]
        delta_ref[...] = jnp.sum(o_ref[...] * dov, axis=-1, keepdims=True)
        dob_ref[...] = dov.astype(BF16)

        for j in range(nkv):
            c0 = j * tk
            kb = k_ref[c0:c0 + tk, :]
            vb = v_ref[c0:c0 + tk, :].astype(BF16)
            if causal:
                spans = [(c0, c0 + tk, True)] + ([(c0 + tk, S, False)] if c0 + tk < S else [])
            else:
                spans = [(0, S, False)]
            dk_acc = dv_acc = None
            for a, b, masked in spans:
                qb = q_ref[a:b, :]
                dob = dob_ref[a:b, :]
                s = lax.dot_general(qb, kb, _NT, preferred_element_type=F32)
                if masked:
                    s = _diag_mask(s)
                p = jnp.exp2(s - lse_ref[a:b, :])
                dp = lax.dot_general(dob, vb, _NT, preferred_element_type=F32)
                ds = (p * (dp - delta_ref[a:b, :])).astype(BF16)
                dv_p = lax.dot_general(p.astype(BF16), dob, _TN, preferred_element_type=F32)
                dk_p = lax.dot_general(ds, qb, _TN, preferred_element_type=F32)
                dk_acc, dv_acc = (dk_p, dv_p) if dk_acc is None else (dk_acc + dk_p, dv_acc + dv_p)
                dq_p = jnp.dot(ds, kb, preferred_element_type=F32) * scale
                if j == 0:
                    dq_ref[a:b, :] = dq_p
                else:
                    dq_ref[a:b, :] += dq_p
            dk_ref[c0:c0 + tk, :] = dk_acc * LN2
            dv_ref[c0:c0 + tk, :] = dv_acc

    ins = [q, k, v, o, do, lse]
    in_specs = [pl.BlockSpec((S, dk), lambda b, h: (b, h)), pl.BlockSpec((Skv, dk), lambda b, h: (b, h)),
                pl.BlockSpec((Skv, HEAD), lambda b, h: (b, v_col0 + h)), pl.BlockSpec((S, HEAD), lambda b, h: (b, h)),
                pl.BlockSpec((S, HEAD), lambda b, h: (b, h)), pl.BlockSpec((None, S, 1), lambda b, h: (h, b, 0))]
    out_specs = [pl.BlockSpec((S, dk), lambda b, h: (b, h)), pl.BlockSpec((Skv, dk), lambda b, h: (b, h)),
                 pl.BlockSpec((Skv, HEAD), lambda b, h: (b, h))]
    out_shape = [jax.ShapeDtypeStruct((nb * S, nheads * dk), F32), jax.ShapeDtypeStruct((nb * Skv, nheads * dk), F32),
                 jax.ShapeDtypeStruct((nb * Skv, nheads * HEAD), F32)]
    scratch, sem = [pltpu.VMEM((S, 1), F32), pltpu.VMEM((S, HEAD), BF16)], ("parallel", "parallel")
    if rider is not None:
        ins, in_specs = ins + [rider.src], in_specs + [HBM_SPEC]
        out_specs, out_shape = out_specs + [HBM_SPEC], out_shape + [rider.out_shape]
        scratch, sem = scratch + rider.scratch, ("arbitrary", "arbitrary")
    return pl.pallas_call(
        body, name=name, grid=(nb, nheads), in_specs=in_specs, out_specs=out_specs, out_shape=out_shape,
        scratch_shapes=scratch, compiler_params=_params(sem),
    )(*ins)


def _spread_rope(a):
    zero = jnp.zeros(a.shape[:-1] + (32,), a.dtype)
    return jnp.concatenate([a[..., :32], zero, a[..., 32:], zero], axis=-1)


def _gather_rope(a):
    return jnp.concatenate([a[..., 0:32], a[..., 64:96]], axis=-1)


def _win_layout(w):
    return jnp.concatenate([w[:, C_ZU:C_CQ], w[:, C_ZG:C_END], w[:, C_QM:C_ZG], w[:, C_CQ:C_CKV], w[:, C_CKV:C_KPE],
                            _spread_rope(w[:, C_KPE:C_QM])], axis=1)


def _win_unlayout(d):
    return jnp.concatenate([d[:, 0:Z_GATE], d[:, Z_MLA:Z_MLA + Q_LORA], d[:, Z_MLA + Q_LORA:Z_KPE],
                            _gather_rope(d[:, Z_KPE:Z_COLS]), d[:, Z_QM:Z_MLA], d[:, Z_GATE:Z_QM]], axis=1)


def _wuq_layout(w):
    r = w.reshape(Q_LORA, MLA_HEADS, HEAD + MLA_ROPE)
    return jnp.concatenate([r[:, :, :HEAD].reshape(Q_LORA, -1), _spread_rope(r[:, :, HEAD:]).reshape(Q_LORA, -1)], axis=1)


def _wuq_unlayout(d):
    n = d[:, :MLA_HEADS * HEAD].reshape(Q_LORA, MLA_HEADS, HEAD)
    p = _gather_rope(d[:, MLA_HEADS * HEAD:].reshape(Q_LORA, MLA_HEADS, HEAD))
    return jnp.concatenate([n, p], axis=-1).reshape(Q_LORA, -1)


def _wukv_layout(w):
    r = w.reshape(KV_LORA, MLA_HEADS, 2 * HEAD)
    return jnp.concatenate([r[:, :, :HEAD].reshape(KV_LORA, -1), r[:, :, HEAD:].reshape(KV_LORA, -1)], axis=1)


def _wukv_unlayout(d):
    k = d[:, :MLA_HEADS * HEAD].reshape(KV_LORA, MLA_HEADS, HEAD)
    v = d[:, MLA_HEADS * HEAD:].reshape(KV_LORA, MLA_HEADS, HEAD)
    return jnp.concatenate([k, v], axis=-1).reshape(KV_LORA, -1)


AG_MID = ['w_uq', 'w_ukv', 'w_mem_kv', 'w_o_gm', 'w_o_mla', 'w_o_mem', 'w_out']
AG_FFN = ['w_ff1', 'w_ff2']
RS_GROUPS = {'ff2': ['w_ff2'], 'ff1': ['w_ff1'], 'proj': ['w_out', 'w_o_gm', 'w_o_mla', 'w_o_mem'],
             'lat': ['w_uq', 'w_ukv', 'w_mem_kv'], 'in': ['w_in']}


def _unride(res, rider):
    if rider is None:
        return res, None
    return (res[0] if len(res) == 2 else tuple(res[:-1])), res[-1]


def _local_step(x, mem, positions, target, P, ws):
    B, S, _ = x.shape
    M = mem.shape[1]
    T = B * S
    x2d = x.reshape(T, D_MODEL)
    mem2d = mem.reshape(B * M, D_MODEL)
    tgt2d = target.reshape(T, D_MODEL)

    def row(v):
        return v.reshape(1, -1).astype(F32)

    inv_freq = ROPE_BASE ** (-jnp.arange(0, MLA_ROPE, 2, dtype=F32) / MLA_ROPE)
    ang = positions.reshape(T).astype(F32)[:, None] * inv_freq
    cos, sin, zero = jnp.cos(ang), jnp.sin(ang), jnp.zeros_like(ang)
    cc = jnp.concatenate([cos, zero, cos, zero], axis=1)
    ss = jnp.concatenate([-sin, zero, sin, zero], axis=1)

    w_in = _win_layout(ws.first()).astype(BF16)
    g_mix, g_cq, g_ckv, g_ffn, g_mem = row(P['g_mix']), row(P['g_cq']), row(P['g_ckv']), row(P['g_ffn']), row(P['g_mem'])
    gqn, gkn, gmq, gmk = row(P['g_q_nope']), row(P['g_k_nope']), row(P['g_mq']), row(P['g_mk'])
    gqp, gkp = _spread_rope(row(P['g_q_pe'])), _spread_rope(row(P['g_k_pe']))
    gln, bln = row(P['g_gm_ln']), row(P['b_gm_ln'])
    wc = jnp.tril(P['w_spatial'].astype(F32))
    wct = jnp.swapaxes(wc, 1, 2).astype(BF16)
    wc = wc.astype(BF16)
    bst = jnp.broadcast_to(P['b_spatial'].astype(F32)[:, :, None], (GM_GROUPS, GM_CHUNK, LANES))

    h = _rms_fwd(x2d, g_mix, "rms_mix")
    ride = ws.gather(AG_MID)
    z, got = _unride(_matmul(h, w_in, 'nn', F32, "mm_in", tn_t=768, rider=ride), ride)
    mid = ws.gathered(AG_MID, got)
    w_uq = _wuq_layout(mid['w_uq']).astype(BF16)
    w_ukv = _wukv_layout(mid['w_ukv']).astype(BF16)
    w_mem_kv, w_o_gm, w_o_mla, w_o_mem, w_out = (mid[n].astype(BF16) for n in ('w_mem_kv', 'w_o_gm', 'w_o_mla', 'w_o_mem',
                                                                                 'w_out'))
    ygm_pre = _gm_fwd(z, gln, bln, wc, bst, "gm_fwd")
    y_gm = _matmul(ygm_pre, w_o_gm, 'nn', F32, "mm_o_gm")
    nq, nkv = _lat_fwd(z, g_cq, g_ckv, "lat_fwd")
    q = _matmul(nq, w_uq, 'nn', F32, "mm_uq")
    kv = _matmul(nkv, w_ukv, 'nn', F32, "mm_ukv")
    qcat, kcat, vv = _qk_fwd(q, kv, z, cc, ss, gqn, gqp, gkn, gkp, "qk_fwd")
    ride = ws.gather(AG_FFN)
    (o, lse), got = _unride(_attn_fwd(qcat, kcat, vv, B, MLA_HEADS, QCAT, 0, True, "mla_attn_fwd", rider=ride), ride)
    ffn = ws.gathered(AG_FFN, got)
    w_ff1, w_ff2 = ffn['w_ff1'].astype(BF16), ffn['w_ff2'].astype(BF16)
    y_mla = _matmul(o, w_o_mla, 'nn', F32, "mm_o_mla")
    nm = _rms_fwd(mem2d, g_mem, "rms_mem")
    kvm = _matmul(nm, w_mem_kv, 'nn', F32, "mm_mem_kv")
    qm = _headnorm_fwd(z, Z_QM // (MEM_HEADS * HEAD), MEM_HEADS, gmq, MEM_SCALE * LOG2E, "memq_fwd")
    km = _headnorm_fwd(kvm, 0, MEM_HEADS, gmk, 1.0, "memk_fwd")
    om, lse_m = _attn_fwd(qm, km, kvm, B, MEM_HEADS, HEAD, MEM_HEADS, False, "mem_attn_fwd")
    y_mem = _matmul(om, w_o_mem, 'nn', F32, "mm_o_mem")
    merged = _merge_fwd(z, y_gm, y_mla, y_mem, "merge_fwd")
    x1 = _matmul(merged, w_out, 'nn', F32, "mm_out", add=x2d)
    h2 = _rms_fwd(x1, g_ffn, "rms_ffn")
    a1 = _matmul(h2, w_ff1, 'nn', BF16, "mm_ff1")
    x2 = _matmul(a1, w_ff2, 'nn', F32, "mm_ff2", add=x1, relu2_a=True)
    dx2, loss_part = _loss_head(x2, tgt2d, "loss_head")

    G = {}
    ride = ws.scatter('ff2', {'w_ff2': _matmul(a1, dx2, 'tn', BF16, "mm_d_ff2", relu2_a=True)})
    da1, got = _unride(_matmul(dx2, w_ff2, 'nt', BF16, "mm_da1", relu2_grad=a1, rider=ride), ride)
    ws.scattered('ff2', got)
    ride = ws.scatter('ff1', {'w_ff1': _matmul(h2, da1, 'tn', BF16, "mm_d_ff1")})
    dh2, got = _unride(_matmul(da1, w_ff1, 'nt', F32, "mm_dh2", rider=ride), ride)
    ws.scattered('ff1', got)
    dx1, G['g_ffn'] = _rms_bwd(x1, g_ffn, dh2, dx2, "rms_ffn_bwd")
    d_out = _matmul(merged, dx1, 'tn', BF16, "mm_d_out")
    dmerged = _matmul(dx1, w_out, 'nt', F32, "mm_dmerged")
    dy_gm, dy_mla, dy_mem, dzg = _merge_bwd(z, y_gm, y_mla, y_mem, dmerged, "merge_bwd")
    d_o_gm = _matmul(ygm_pre, dy_gm, 'tn', BF16, "mm_d_o_gm")
    d_o_mla = _matmul(o, dy_mla, 'tn', BF16, "mm_d_o_mla")
    d_o_mem = _matmul(om, dy_mem, 'tn', BF16, "mm_d_o_mem")
    dygm_pre = _matmul(dy_gm, w_o_gm, 'nt', F32, "mm_dygm")
    dz_gm, dws, dbs, G['g_gm_ln'], G['b_gm_ln'] = _gm_bwd(z, dygm_pre, gln, bln, wc, wct, bst, "gm_bwd")
    G['w_spatial'] = jnp.tril(dws)
    G['b_spatial'] = jnp.sum(dbs.reshape(GM_CHUNK, GM_GROUPS, LANES), axis=-1).T
    do = _matmul(dy_mla, w_o_mla, 'nt', F32, "mm_do")
    ride = ws.scatter('proj', {'w_out': d_out, 'w_o_gm': d_o_gm, 'w_o_mla': d_o_mla, 'w_o_mem': d_o_mem})
    (dqc, dkc, dvv), got = _unride(_attn_bwd(qcat, kcat, vv, o, do, lse, B, MLA_HEADS, QCAT, 0, MLA_SCALE, True,
                                             "mla_attn_bwd", rider=ride), ride)
    ws.scattered('proj', got)
    dq, dkv, dkpe, G['g_q_nope'], dgqp, G['g_k_nope'], dgkp = _qk_bwd(q, kv, z, cc, ss, gqn, gqp, gkn, gkp, dqc, dkc, dvv,
                                                                     "qk_bwd")
    G['g_q_pe'], G['g_k_pe'] = _gather_rope(dgqp), _gather_rope(dgkp)
    d_uq = _wuq_unlayout(_matmul(nq, dq, 'tn', BF16, "mm_d_uq"))
    dnq = _matmul(dq, w_uq, 'nt', F32, "mm_dnq")
    d_ukv = _wukv_unlayout(_matmul(nkv, dkv, 'tn', BF16, "mm_d_ukv"))
    dnkv = _matmul(dkv, w_ukv, 'nt', F32, "mm_dnkv")
    dz_mla, G['g_cq'], G['g_ckv'] = _lat_bwd(z, dnq, dnkv, dkpe, g_cq, g_ckv, "lat_bwd")
    dom = _matmul(dy_mem, w_o_mem, 'nt', F32, "mm_dom")
    dqm, dkm, dvm = _attn_bwd(qm, km, kvm, om, dom, lse_m, B, MEM_HEADS, HEAD, MEM_HEADS, MEM_SCALE, False, "mem_attn_bwd")
    dz_qm, G['g_mq'] = _headnorm_bwd(z, Z_QM // (MEM_HEADS * HEAD), MEM_HEADS, gmq, dqm, None, "memq_bwd")
    dkvm, G['g_mk'] = _headnorm_bwd(kvm, 0, MEM_HEADS, gmk, dkm, dvm, "memk_bwd")
    d_mem_kv = _matmul(nm, dkvm, 'tn', BF16, "mm_d_mem_kv")
    dnm = _matmul(dkvm, w_mem_kv, 'nt', F32, "mm_dnm")
    _, G['g_mem'] = _rms_bwd(mem2d, g_mem, dnm, None, "rms_mem_bwd")
    dz = jnp.concatenate([dz_gm, dzg, dz_qm, dz_mla], axis=1)
    ride = ws.scatter('lat', {'w_uq': d_uq, 'w_ukv': d_ukv, 'w_mem_kv': d_mem_kv})
    d_in, got = _unride(_matmul(h, dz, 'tn', BF16, "mm_d_in", tn_t=768, rider=ride), ride)
    ws.scattered('lat', got)
    ride = ws.scatter('in', {'w_in': _win_unlayout(d_in)})
    dh, got = _unride(_matmul(dz, w_in, 'nt', F32, "mm_dh", tk_t=768, rider=ride), ride)
    ws.scattered('in', got)
    gx, G['g_mix'] = _rms_bwd(x2d, g_mix, dh, dx1, "rms_mix_bwd")
    return loss_part, gx.reshape(B, S, D_MODEL), G


def _all_gather8(xs, name):
    def body(x_ref, out_ref, send_sems, recv_sems, local_sem):
        x, y, c = lax.axis_index("x"), lax.axis_index("y"), lax.axis_index("c")
        me, sibling = (x, y, c), (x, y, 1 - c)
        chips = [(1 - x, y), (x, 1 - y), (1 - x, 1 - y)]

        def rows(px, py, pc):
            return out_ref.at[4 * px + 2 * py + pc]

        def copy(k, block, to, src=None):
            return pltpu.make_async_remote_copy(
                src_ref=rows(*block) if src is None else src, dst_ref=rows(*block),
                send_sem=send_sems.at[k], recv_sem=recv_sems.at[k], device_id=to, device_id_type=MESH)

        mine = pltpu.make_async_copy(x_ref, rows(*me), local_sem)
        mine.start()
        first = [copy(0, me, sibling, src=x_ref)]
        first += [copy(1 + j, me, (*chip, c), src=x_ref) for j, chip in enumerate(chips)]
        for cp in first:
            cp.start()
        passed = [copy(4 + j, (*chip, c), sibling) for j, chip in enumerate(chips)]
        for j, chip in enumerate(chips):
            copy(1 + j, (*chip, c), me).wait_recv()
            passed[j].start()
        copy(0, sibling, me).wait_recv()
        for j, chip in enumerate(chips):
            copy(4 + j, (*chip, 1 - c), me).wait_recv()
        for cp in first + passed:
            cp.wait_send()
        mine.wait()

    return pl.pallas_call(
        body, name=name, in_specs=[HBM_SPEC], out_specs=HBM_SPEC,
        out_shape=jax.ShapeDtypeStruct((N_DEV,) + xs.shape, xs.dtype),
        scratch_shapes=[pltpu.SemaphoreType.DMA((7,)), pltpu.SemaphoreType.DMA((7,)), pltpu.SemaphoreType.DMA],
    )(xs)


def _adamw_rows(w, g, m, v):
    m2 = ADAM_B1 * m + (1.0 - ADAM_B1) * g
    v2 = ADAM_B2 * v + (1.0 - ADAM_B2) * (g * g)
    m_hat = m2 / (1.0 - ADAM_B1 ** ADAM_STEP)
    v_hat = v2 / (1.0 - ADAM_B2 ** ADAM_STEP)
    delta = -ADAM_LR * (m_hat / (jnp.sqrt(v_hat) + ADAM_EPS) + ADAM_WD * w)
    return delta, m2, v2


def _sum_adamw(parts, w, m, v, name):
    rows = w.shape[0]
    tr = _pick(rows, 512, 16)
    n = parts.shape[0]

    def body(p_ref, w_ref, m_ref, v_ref, g_ref, d_ref, m2_ref, v2_ref):
        g = p_ref[0].astype(F32)
        for k in range(1, n):
            g = g + p_ref[k].astype(F32)
        delta, m2, v2 = _adamw_rows(w_ref[...], g, m_ref[...], v_ref[...])
        g_ref[...] = g
        d_ref[...] = delta
        m2_ref[...] = m2
        v2_ref[...] = v2

    flat = pl.BlockSpec((tr, LANES), lambda i: (i, 0))
    out = jax.ShapeDtypeStruct((rows, LANES), F32)
    return pl.pallas_call(
        body, name=name, grid=(rows // tr,),
        in_specs=[pl.BlockSpec((n, tr, LANES), lambda i: (0, i, 0)), flat, flat, flat], out_specs=[flat] * 4,
        out_shape=[out] * 4, compiler_params=_params(("parallel",)),
    )(parts, w, m, v)


def _to_slab(parts):
    return jnp.concatenate([p.reshape(-1, LANES) for p in parts], axis=0)


def _small_rows(name):
    n = {'g_mix': 1024, 'g_cq': 384, 'g_ckv': 256, 'g_q_nope': 128, 'g_q_pe': 64, 'g_k_nope': 128, 'g_k_pe': 64,
         'g_gm_ln': 512, 'b_gm_ln': 512, 'w_spatial': GM_GROUPS * GM_CHUNK * GM_CHUNK, 'b_spatial': GM_GROUPS * GM_CHUNK,
         'g_mem': 1024, 'g_mq': 128, 'g_mk': 128, 'g_ffn': 1024}[name]
    return n, -(-n // LANES)


def _small_slab(d):
    parts = []
    for name in SMALL:
        n, rows = _small_rows(name)
        parts.append(jnp.pad(d[name].reshape(-1).astype(F32), (0, rows * LANES - n)).reshape(rows, LANES))
    slab = jnp.concatenate(parts, axis=0)
    return jnp.pad(slab, ((0, -slab.shape[0] % 8), (0, 0)))


def _small_unslab(slab, like):
    out, r = {}, 0
    for name in SMALL:
        n, rows = _small_rows(name)
        out[name] = slab[r:r + rows].reshape(-1)[:n].reshape(like[name].shape)
        r += rows
    return out


def _shard_rows(name):
    r, c = BIG_SHAPE[name]
    return r * c // N_DEV // LANES


def _full_from_gathered(gathered, name, r0):
    r, c = BIG_SHAPE[name]
    n = _shard_rows(name)
    blk = gathered[:, r0:r0 + n]
    if BIG_AXIS[name] == 0:
        return blk.reshape(r, c)
    return blk.reshape(N_DEV, r, c // N_DEV).transpose(1, 0, 2).reshape(r, c)


def _shards_of_full(g, name):
    r, c = BIG_SHAPE[name]
    if BIG_AXIS[name] == 0:
        return g.reshape(N_DEV, -1, LANES)
    return g.reshape(r, N_DEV, c // N_DEV).transpose(1, 0, 2).reshape(N_DEV, -1, LANES)


class _DistWeights:
    def __init__(self, shards):
        self.shards = shards
        self.received = {}

    def _slab(self, names):
        return _to_slab([self.shards[n].astype(BF16) for n in names])

    def first(self):
        return _full_from_gathered(_all_gather8(self._slab(['w_in']), "ag_w_in"), 'w_in', 0)

    def gather(self, names):
        return _Exchange(self._slab(names), scatter=False)

    def gathered(self, names, got):
        full, r0 = {}, 0
        for n in names:
            full[n] = _full_from_gathered(got, n, r0)
            r0 += _shard_rows(n)
        return full

    def scatter(self, key, grads):
        return _Exchange(jnp.concatenate([_shards_of_full(grads[n], n) for n in RS_GROUPS[key]], axis=1), scatter=True)

    def scattered(self, key, got):
        self.received[key] = got


def kernel(x, mem, positions, g_mix, w_in, g_cq, w_uq, g_ckv, w_ukv, g_q_nope, g_q_pe, g_k_nope, g_k_pe, g_gm_ln, b_gm_ln, w_spatial, b_spatial, g_mem, w_mem_kv, g_mq, g_mk, w_o_gm, w_o_mla, w_o_mem, w_out, g_ffn, w_ff1, w_ff2, loss_target, m_g_mix, m_w_in, m_g_cq, m_w_uq, m_g_ckv, m_w_ukv, m_g_q_nope, m_g_q_pe, m_g_k_nope, m_g_k_pe, m_g_gm_ln, m_b_gm_ln, m_w_spatial, m_b_spatial, m_g_mem, m_w_mem_kv, m_g_mq, m_g_mk, m_w_o_gm, m_w_o_mla, m_w_o_mem, m_w_out, m_g_ffn, m_w_ff1, m_w_ff2, v_g_mix, v_w_in, v_g_cq, v_w_uq, v_g_ckv, v_w_ukv, v_g_q_nope, v_g_q_pe, v_g_k_nope, v_g_k_pe, v_g_gm_ln, v_b_gm_ln, v_w_spatial, v_b_spatial, v_g_mem, v_w_mem_kv, v_g_mq, v_g_mk, v_w_o_gm, v_w_o_mla, v_w_o_mem, v_w_out, v_g_ffn, v_w_ff1, v_w_ff2):
    given = dict(locals())
    w = {n: given[n][0] for n in WEIGHTS}
    mom = {n: given['m_' + n][0] for n in WEIGHTS}
    var = {n: given['v_' + n][0] for n in WEIGHTS}

    ws = _DistWeights({n: w[n] for n in BIG})
    loss_part, grad_x, G = _local_step(x, mem, positions, loss_target, {n: w[n] for n in SMALL}, ws)
    loss = lax.psum(0.5 * jnp.sum(loss_part) / D_MODEL, ("x", "y", "c"))

    outs = {}
    for key, names in RS_GROUPS.items():
        slabs = _sum_adamw(ws.received[key], _to_slab([w[n] for n in names]), _to_slab([mom[n] for n in names]),
                           _to_slab([var[n] for n in names]), "adamw_" + key)
        for prefix, slab in zip(("grad_", "delta_", "new_m_", "new_v_"), slabs):
            r0 = 0
            for n in names:
                k = _shard_rows(n)
                outs[prefix + n] = slab[r0:r0 + k].reshape(given[n].shape)
                r0 += k

    parts = _all_gather8(_small_slab(G), "ag_small")
    small = _sum_adamw(parts, _small_slab(w), _small_slab(mom), _small_slab(var), "adamw_small")
    for prefix, small_slab in zip(("grad_", "delta_", "new_m_", "new_v_"), small):
        sm = _small_unslab(small_slab, given)
        for n in SMALL:
            outs[prefix + n] = sm[n]
    return (loss, grad_x, *[outs[p + n] for p in ("grad_", "delta_", "new_m_", "new_v_") for n in WEIGHTS])
```

```python
import functools
import math

import jax
import jax.numpy as jnp
from jax import lax
from jax.experimental import pallas as pl
from jax.experimental.pallas import tpu as pltpu

F32 = jnp.float32
BF16 = jnp.bfloat16

D_MODEL = 1024
MEM_HEADS = 4
HEAD = 128
GM_WIDTH = 512
GM_CHUNK = 128
GM_GROUPS = 4
MLA_HEADS = 8
MLA_ROPE = 64
Q_LORA = 384
KV_LORA = 256
D_FF = 4096
EPS = 1e-6
ROPE_BASE = 10000.0
MLA_SCALE = 1.0 / math.sqrt(HEAD + MLA_ROPE)
MEM_SCALE = 1.0 / math.sqrt(HEAD)
LOG2E = 1.4426950408889634
LN2 = 0.6931471805599453
ATT_TILE = 256
C_ZU, C_ZV, C_CQ, C_CKV, C_KPE, C_QM, C_ZG, C_END = 0, 512, 1024, 1408, 1664, 1728, 2240, 5312
Z_GATE, Z_QM, Z_MLA, Z_KPE, Z_COLS = 1024, 4096, 4608, 5248, 5376
MLA_W = 768
QCAT = 2 * HEAD
ADAM_LR, ADAM_B1, ADAM_B2, ADAM_EPS, ADAM_WD, ADAM_STEP = 0.001, 0.9, 0.999, 1e-08, 0.01, 10
N_DEV = 8
LANES = 128
VMEM_LIMIT = 48 * 1024 * 1024
NEG = -1e30

BIG = ['w_in', 'w_uq', 'w_ukv', 'w_mem_kv', 'w_o_gm', 'w_o_mla', 'w_o_mem', 'w_out', 'w_ff1', 'w_ff2']
BIG_AXIS = {'w_in': 1, 'w_uq': 1, 'w_ukv': 1, 'w_mem_kv': 0, 'w_o_gm': 1, 'w_o_mla': 0, 'w_o_mem': 1,
            'w_out': 0, 'w_ff1': 1, 'w_ff2': 0}
BIG_SHAPE = {'w_in': (1024, 5312), 'w_uq': (384, 1536), 'w_ukv': (256, 2048), 'w_mem_kv': (1024, 1024),
             'w_o_gm': (512, 1024), 'w_o_mla': (1024, 1024), 'w_o_mem': (512, 1024), 'w_out': (1024, 1024),
             'w_ff1': (1024, 4096), 'w_ff2': (4096, 1024)}
SMALL = ['g_mix', 'g_cq', 'g_ckv', 'g_q_nope', 'g_q_pe', 'g_k_nope', 'g_k_pe', 'g_gm_ln', 'b_gm_ln',
         'w_spatial', 'b_spatial', 'g_mem', 'g_mq', 'g_mk', 'g_ffn']
WEIGHTS = ['g_mix', 'w_in', 'g_cq', 'w_uq', 'g_ckv', 'w_ukv', 'g_q_nope', 'g_q_pe', 'g_k_nope', 'g_k_pe',
           'g_gm_ln', 'b_gm_ln', 'w_spatial', 'b_spatial', 'g_mem', 'w_mem_kv', 'g_mq', 'g_mk', 'w_o_gm',
           'w_o_mla', 'w_o_mem', 'w_out', 'g_ffn', 'w_ff1', 'w_ff2']


def _pick(n, target, mult=LANES):
    best = None
    t = mult
    while t <= min(n, target):
        if n % t == 0:
            best = t
        t += mult
    return best if best is not None else n


def _params(sem):
    return pltpu.CompilerParams(dimension_semantics=sem, vmem_limit_bytes=VMEM_LIMIT)


MESH = pl.DeviceIdType.MESH
HBM_SPEC = pl.BlockSpec(memory_space=pltpu.HBM)


class _Exchange:
    def __init__(self, srcs, scatter):
        self.srcs, self.scatter = list(srcs), scatter
        self.out_shapes = [jax.ShapeDtypeStruct(s.shape if scatter else (N_DEV,) + s.shape, s.dtype) for s in self.srcs]
        n = len(self.srcs)
        self.scratch = [pltpu.SemaphoreType.DMA((n, N_DEV - 1)), pltpu.SemaphoreType.DMA((n, N_DEV - 1)),
                        pltpu.SemaphoreType.DMA((n,))]

    def _copies(self, src_refs, dst_refs, send_sems, recv_sems, local_sems):
        x, y, c = lax.axis_index("x"), lax.axis_index("y"), lax.axis_index("c")
        me = 4 * x + 2 * y + c
        local, remote = [], []
        for a, (src_ref, dst_ref) in enumerate(zip(src_refs, dst_refs)):
            def mine_for(dev, src_ref=src_ref):
                return src_ref.at[dev] if self.scatter else src_ref

            local.append(pltpu.make_async_copy(mine_for(me), dst_ref.at[me], local_sems.at[a]))
            for k in range(1, N_DEV):
                px = 1 - x if k & 4 else x
                py = 1 - y if k & 2 else y
                pc = 1 - c if k & 1 else c
                remote.append(pltpu.make_async_remote_copy(
                    src_ref=mine_for(4 * px + 2 * py + pc), dst_ref=dst_ref.at[me], send_sem=send_sems.at[a, k - 1],
                    recv_sem=recv_sems.at[a, k - 1], device_id=(px, py, pc), device_id_type=MESH))
        return local, remote

    def start(self, *refs):
        local, remote = self._copies(*refs)
        for cp in local + remote:
            cp.start()

    def wait(self, *refs):
        local, remote = self._copies(*refs)
        for cp in remote + local:
            cp.wait()


def _call(body, rider, ins, *, name, grid, in_specs, out_specs, out_shape, scratch_shapes, sem):
    if rider is None:
        return pl.pallas_call(body, name=name, grid=grid, in_specs=in_specs, out_specs=out_specs, out_shape=out_shape,
                              scratch_shapes=scratch_shapes, compiler_params=_params(sem))(*ins)
    single = not isinstance(out_shape, (list, tuple))
    own_specs, own_shapes = ([out_specs], [out_shape]) if single else (list(out_specs), list(out_shape))
    n_in, n_out, n_sc, n_r = len(ins), len(own_shapes), len(scratch_shapes), len(rider.srcs)

    def carrying(*refs):
        own_in, srcs = refs[:n_in], refs[n_in:n_in + n_r]
        own_out, dsts = refs[n_in + n_r:n_in + n_r + n_out], refs[n_in + n_r + n_out:n_in + 2 * n_r + n_out]
        own_sc = refs[n_in + 2 * n_r + n_out:n_in + 2 * n_r + n_out + n_sc]
        sems = refs[n_in + 2 * n_r + n_out + n_sc:]
        first = last = None
        for d, steps in enumerate(grid):
            f, l = pl.program_id(d) == 0, pl.program_id(d) == steps - 1
            first, last = (f, l) if first is None else (first & f, last & l)

        @pl.when(first)
        def _():
            rider.start(srcs, dsts, *sems)

        body(*own_in, *own_out, *own_sc)

        @pl.when(last)
        def _():
            rider.wait(srcs, dsts, *sems)

    res = pl.pallas_call(
        carrying, name=name, grid=grid, in_specs=list(in_specs) + [HBM_SPEC] * n_r, out_specs=own_specs + [HBM_SPEC] * n_r,
        out_shape=own_shapes + rider.out_shapes, scratch_shapes=list(scratch_shapes) + rider.scratch,
        compiler_params=_params(("arbitrary",) * len(grid)),
    )(*ins, *rider.srcs)
    own = res[:n_out]
    return (own[0] if single else list(own)), list(res[n_out:])


def _matmul(a, b, mode, out_dtype, name, add=None, relu2_a=False, relu2_grad=None,
            tm_t=2048, tn_t=512, tk_t=1024, rider=None):
    if mode == 'nn':
        (M, K), (K2, N) = a.shape, b.shape
    elif mode == 'nt':
        (M, K), (N, K2) = a.shape, b.shape
    else:
        (K, M), (K2, N) = a.shape, b.shape
    assert K == K2, (name, a.shape, b.shape)
    tm, tn, tk = _pick(M, tm_t), _pick(N, tn_t), _pick(K, tk_t)
    gm, gn, nk = M // tm, N // tn, K // tk
    if mode == 'nn':
        a_spec = pl.BlockSpec((tm, tk), lambda i, j, k: (i, k))
        b_spec = pl.BlockSpec((tk, tn), lambda i, j, k: (k, j))
        dims = (((1,), (0,)), ((), ()))
    elif mode == 'nt':
        a_spec = pl.BlockSpec((tm, tk), lambda i, j, k: (i, k))
        b_spec = pl.BlockSpec((tn, tk), lambda i, j, k: (j, k))
        dims = (((1,), (1,)), ((), ()))
    else:
        a_spec = pl.BlockSpec((tk, tm), lambda i, j, k: (k, i))
        b_spec = pl.BlockSpec((tk, tn), lambda i, j, k: (k, j))
        dims = (((0,), (0,)), ((), ()))
    o_spec = pl.BlockSpec((tm, tn), lambda i, j, k: (i, j))
    has_add, has_e = add is not None, relu2_grad is not None

    def body(*refs):
        a_ref, b_ref = refs[0], refs[1]
        pos = 2
        add_ref = e_ref = None
        if has_add:
            add_ref = refs[pos]
            pos += 1
        if has_e:
            e_ref = refs[pos]
            pos += 1
        o_ref = refs[pos]
        acc_ref = refs[pos + 1] if nk > 1 else None

        av = a_ref[...]
        if relu2_a:
            av = jnp.maximum(av.astype(F32), 0.0)
            av = av * av
        prod = lax.dot_general(av.astype(BF16), b_ref[...].astype(BF16), dims, preferred_element_type=F32)

        def finish(r):
            if has_add:
                r = r + add_ref[...]
            if has_e:
                r = r * (2.0 * jnp.maximum(e_ref[...].astype(F32), 0.0))
            o_ref[...] = r.astype(out_dtype)

        if nk == 1:
            finish(prod)
        else:
            k = pl.program_id(2)

            @pl.when(k == 0)
            def _():
                acc_ref[...] = prod

            @pl.when(k > 0)
            def _():
                acc_ref[...] += prod

            @pl.when(k == nk - 1)
            def _():
                finish(acc_ref[...])

    ins, specs = [a, b], [a_spec, b_spec]
    if has_add:
        ins.append(add)
        specs.append(o_spec)
    if has_e:
        ins.append(relu2_grad)
        specs.append(o_spec)
    return _call(body, rider, ins, name=name, grid=(gm, gn, nk), in_specs=specs, out_specs=o_spec,
                 out_shape=jax.ShapeDtypeStruct((M, N), out_dtype),
                 scratch_shapes=[pltpu.VMEM((tm, tn), F32)] if nk > 1 else [], sem=("parallel", "parallel", "arbitrary"))


def _row_tile(rows, target=256):
    return _pick(rows, target, 8)


def _rowspec(tr, width, col=0):
    return pl.BlockSpec((tr, width), lambda i, col=col: (i, col))


def _fullspec(shape):
    nd = len(shape)
    return pl.BlockSpec(shape, lambda i, nd=nd: (0,) * nd)


def _rms(x, width):
    return lax.rsqrt(jnp.sum(x * x, axis=-1, keepdims=True) * (1.0 / width) + EPS)


def _rms_bwd_rows(x, g, dy, width):
    r = _rms(x, width)
    xh = x * r
    dn = dy * g
    dx = r * (dn - xh * (jnp.sum(dn * xh, axis=-1, keepdims=True) * (1.0 / width)))
    return dx, dy * xh


def _acc_rows(ref, val, first):
    s = jnp.sum(val, axis=0, keepdims=True)

    @pl.when(first)
    def _():
        ref[...] = s

    @pl.when(jnp.logical_not(first))
    def _():
        ref[...] += s


def _rms_fwd(x, g, name):
    rows, width = x.shape
    tr = _row_tile(rows)

    def body(x_ref, g_ref, o_ref):
        xv = x_ref[...]
        o_ref[...] = (xv * _rms(xv, width) * g_ref[...]).astype(BF16)

    return pl.pallas_call(
        body, name=name, grid=(rows // tr,),
        in_specs=[_rowspec(tr, width), _fullspec((1, width))], out_specs=_rowspec(tr, width),
        out_shape=jax.ShapeDtypeStruct((rows, width), BF16), compiler_params=_params(("parallel",)),
    )(x, g)


def _rms_bwd(x, g, dy, res, name):
    rows, width = x.shape
    tr = _row_tile(rows)
    has_res = res is not None

    def body(*refs):
        if has_res:
            x_ref, g_ref, dy_ref, res_ref, dx_ref, dg_ref = refs
        else:
            x_ref, g_ref, dy_ref, dx_ref, dg_ref = refs
        dx, dgv = _rms_bwd_rows(x_ref[...], g_ref[...], dy_ref[...], width)
        if has_res:
            dx = dx + res_ref[...]
        dx_ref[...] = dx
        _acc_rows(dg_ref, dgv, pl.program_id(0) == 0)

    ins = [x, g, dy] + ([res] if has_res else [])
    specs = [_rowspec(tr, width), _fullspec((1, width)), _rowspec(tr, width)] + ([_rowspec(tr, width)] if has_res else [])
    return pl.pallas_call(
        body, name=name, grid=(rows // tr,), in_specs=specs,
        out_specs=[_rowspec(tr, width), _fullspec((1, width))],
        out_shape=[jax.ShapeDtypeStruct((rows, width), F32), jax.ShapeDtypeStruct((1, width), F32)],
        compiler_params=_params(("arbitrary",)),
    )(*ins)


_GELU_C = math.sqrt(2.0 / math.pi)


def _gelu(x):
    t = jnp.tanh(_GELU_C * (x + 0.044715 * (x * x * x)))
    return 0.5 * x * (1.0 + t), t


def _gelu_grad(x, t):
    return 0.5 * (1.0 + t) + 0.5 * x * (1.0 - t * t) * (_GELU_C * (1.0 + 3.0 * 0.044715 * (x * x)))


def _gm_forward_rows(zu, zv, gln, bln, wc_ref, bst, n_chunk):
    u, tu = _gelu(zu)
    a, ta = _gelu(zv)
    mu = jnp.mean(a, axis=-1, keepdims=True)
    ac = a - mu
    rs = lax.rsqrt(jnp.mean(ac * ac, axis=-1, keepdims=True) + EPS)
    n = ac * rs
    v = n * gln + bln
    vb = v.astype(BF16)
    rows = []
    for c in range(n_chunk):
        cols = []
        for g in range(GM_GROUPS):
            vc = vb[c * GM_CHUNK:(c + 1) * GM_CHUNK, g * LANES:(g + 1) * LANES]
            mixed = jnp.dot(wc_ref[g], vc, preferred_element_type=F32) + bst[g]
            cols.append(mixed)
        rows.append(jnp.concatenate(cols, axis=1))
    mixed = jnp.concatenate(rows, axis=0) if n_chunk > 1 else rows[0]
    return u, tu, ta, n, rs, v, mixed


def _gm_fwd(z, gln, bln, wc, bst, name):
    rows = z.shape[0]
    tr = _pick(rows, 512, GM_CHUNK)
    n_chunk = tr // GM_CHUNK

    def body(zu_ref, zv_ref, gln_ref, bln_ref, wc_ref, bst_ref, o_ref):
        u, _, _, _, _, _, mixed = _gm_forward_rows(zu_ref[...], zv_ref[...], gln_ref[...], bln_ref[...], wc_ref,
                                                   bst_ref, n_chunk)
        o_ref[...] = (u * mixed).astype(BF16)

    return pl.pallas_call(
        body, name=name, grid=(rows // tr,),
        in_specs=[_rowspec(tr, GM_WIDTH, 0), _rowspec(tr, GM_WIDTH, 1), _fullspec((1, GM_WIDTH)), _fullspec((1, GM_WIDTH)),
                  _fullspec((GM_GROUPS, GM_CHUNK, GM_CHUNK)), _fullspec((GM_GROUPS, GM_CHUNK, LANES))],
        out_specs=_rowspec(tr, GM_WIDTH), out_shape=jax.ShapeDtypeStruct((rows, GM_WIDTH), BF16),
        compiler_params=_params(("parallel",)),
    )(z, z, gln, bln, wc, bst)


def _gm_bwd(z, dy, gln, bln, wc, wct, bst, name):
    rows = z.shape[0]
    tr = _pick(rows, 512, GM_CHUNK)
    n_chunk = tr // GM_CHUNK

    def body(zu_ref, zv_ref, dy_ref, gln_ref, bln_ref, wc_ref, wct_ref, bst_ref, dz_ref, dws_ref, dbs_ref, dgl_ref, dbl_ref):
        first = pl.program_id(0) == 0
        zu, zv, gln = zu_ref[...], zv_ref[...], gln_ref[...]
        u, tu, ta, n, rs, v, mixed = _gm_forward_rows(zu, zv, gln, bln_ref[...], wc_ref, bst_ref, n_chunk)
        dyv = dy_ref[...]
        dzu = dyv * mixed * _gelu_grad(zu, tu)
        dmix = dyv * u
        dmb = dmix.astype(BF16)
        vb = v.astype(BF16)
        dv_rows, dws, dbs = [], [None] * GM_GROUPS, None
        for c in range(n_chunk):
            rsl = slice(c * GM_CHUNK, (c + 1) * GM_CHUNK)
            cols = []
            for g in range(GM_GROUPS):
                csl = slice(g * LANES, (g + 1) * LANES)
                dmc = dmb[rsl, csl]
                cols.append(jnp.dot(wct_ref[g], dmc, preferred_element_type=F32))
                w_part = lax.dot_general(dmc, vb[rsl, csl], (((1,), (1,)), ((), ())), preferred_element_type=F32)
                dws[g] = w_part if dws[g] is None else dws[g] + w_part
            dv_rows.append(jnp.concatenate(cols, axis=1))
            dbs = dmix[rsl, :] if dbs is None else dbs + dmix[rsl, :]
        dv = jnp.concatenate(dv_rows, axis=0) if n_chunk > 1 else dv_rows[0]
        dn = dv * gln
        da = rs * (dn - jnp.mean(dn, axis=-1, keepdims=True) - n * jnp.mean(dn * n, axis=-1, keepdims=True))
        dzv = da * _gelu_grad(zv, ta)
        dz_ref[:, 0:GM_WIDTH] = dzu.astype(BF16)
        dz_ref[:, GM_WIDTH:2 * GM_WIDTH] = dzv.astype(BF16)
        _acc_rows(dgl_ref, dv * n, first)
        _acc_rows(dbl_ref, dv, first)

        @pl.when(first)
        def _():
            for g in range(GM_GROUPS):
                dws_ref[g] = dws[g]
            dbs_ref[...] = dbs

        @pl.when(jnp.logical_not(first))
        def _():
            for g in range(GM_GROUPS):
                dws_ref[g] += dws[g]
            dbs_ref[...] += dbs

    wspec = _fullspec((GM_GROUPS, GM_CHUNK, GM_CHUNK))
    return pl.pallas_call(
        body, name=name, grid=(rows // tr,),
        in_specs=[_rowspec(tr, GM_WIDTH, 0), _rowspec(tr, GM_WIDTH, 1), _rowspec(tr, GM_WIDTH), _fullspec((1, GM_WIDTH)),
                  _fullspec((1, GM_WIDTH)), wspec, wspec, wspec],
        out_specs=[_rowspec(tr, 2 * GM_WIDTH), wspec, _fullspec((GM_CHUNK, GM_WIDTH)), _fullspec((1, GM_WIDTH)),
                   _fullspec((1, GM_WIDTH))],
        out_shape=[jax.ShapeDtypeStruct((rows, 2 * GM_WIDTH), BF16), jax.ShapeDtypeStruct((GM_GROUPS, GM_CHUNK, GM_CHUNK), F32),
                   jax.ShapeDtypeStruct((GM_CHUNK, GM_WIDTH), F32), jax.ShapeDtypeStruct((1, GM_WIDTH), F32),
                   jax.ShapeDtypeStruct((1, GM_WIDTH), F32)],
        compiler_params=_params(("arbitrary",)),
    )(z, z, dy, gln, bln, wc, wct, bst)


def _lat_fwd(z, g_cq, g_ckv, name):
    rows = z.shape[0]
    tr = _row_tile(rows)

    def body(z_ref, gq_ref, gkv_ref, nq_ref, nkv_ref):
        zb = z_ref[...]
        cq, ckv = zb[:, 0:Q_LORA], zb[:, Q_LORA:Q_LORA + KV_LORA]
        nq_ref[...] = (cq * _rms(cq, Q_LORA) * gq_ref[...]).astype(BF16)
        nkv_ref[...] = (ckv * _rms(ckv, KV_LORA) * gkv_ref[...]).astype(BF16)

    return pl.pallas_call(
        body, name=name, grid=(rows // tr,),
        in_specs=[_rowspec(tr, MLA_W, Z_MLA // MLA_W), _fullspec((1, Q_LORA)), _fullspec((1, KV_LORA))],
        out_specs=[_rowspec(tr, Q_LORA), _rowspec(tr, KV_LORA)],
        out_shape=[jax.ShapeDtypeStruct((rows, Q_LORA), BF16), jax.ShapeDtypeStruct((rows, KV_LORA), BF16)],
        compiler_params=_params(("parallel",)),
    )(z, g_cq, g_ckv)


def _lat_bwd(z, dnq, dnkv, dkpe, g_cq, g_ckv, name):
    rows = z.shape[0]
    tr = _row_tile(rows)

    def body(z_ref, dnq_ref, dnkv_ref, dkpe_ref, gq_ref, gkv_ref, dz_ref, dgq_ref, dgkv_ref):
        first = pl.program_id(0) == 0
        zb = z_ref[...]
        dcq, dgq = _rms_bwd_rows(zb[:, 0:Q_LORA], gq_ref[...], dnq_ref[...], Q_LORA)
        dckv, dgkv = _rms_bwd_rows(zb[:, Q_LORA:Q_LORA + KV_LORA], gkv_ref[...], dnkv_ref[...], KV_LORA)
        dz_ref[:, 0:Q_LORA] = dcq.astype(BF16)
        dz_ref[:, Q_LORA:Q_LORA + KV_LORA] = dckv.astype(BF16)
        dz_ref[:, Q_LORA + KV_LORA:MLA_W] = dkpe_ref[...].astype(BF16)
        _acc_rows(dgq_ref, dgq, first)
        _acc_rows(dgkv_ref, dgkv, first)

    return pl.pallas_call(
        body, name=name, grid=(rows // tr,),
        in_specs=[_rowspec(tr, MLA_W, Z_MLA // MLA_W), _rowspec(tr, Q_LORA), _rowspec(tr, KV_LORA), _rowspec(tr, LANES),
                  _fullspec((1, Q_LORA)), _fullspec((1, KV_LORA))],
        out_specs=[_rowspec(tr, MLA_W), _fullspec((1, Q_LORA)), _fullspec((1, KV_LORA))],
        out_shape=[jax.ShapeDtypeStruct((rows, MLA_W), BF16), jax.ShapeDtypeStruct((1, Q_LORA), F32),
                   jax.ShapeDtypeStruct((1, KV_LORA), F32)],
        compiler_params=_params(("arbitrary",)),
    )(z, dnq, dnkv, dkpe, g_cq, g_ckv)


def _rope(y, cc, ss):
    return y * cc + pltpu.roll(y, 64, 1) * ss


def _rope_bwd(d, cc, ss):
    return d * cc + pltpu.roll(d * ss, 64, 1)


def _qk_fwd(q, kv, z, cc, ss, gqn, gqp, gkn, gkp, name):
    rows = q.shape[0]
    tr = _row_tile(rows)
    W = MLA_HEADS * HEAD
    QS = MLA_SCALE * LOG2E

    def body(q_ref, kv_ref, kpe_ref, cc_ref, ss_ref, gqn_ref, gqp_ref, gkn_ref, gkp_ref, qc_ref, kc_ref, v_ref):
        cc, ss = cc_ref[...], ss_ref[...]
        kpe = kpe_ref[...]
        kp = _rope(kpe * _rms(kpe, MLA_ROPE) * gkp_ref[...], cc, ss).astype(BF16)
        for h in range(MLA_HEADS):
            qn = q_ref[:, h * HEAD:(h + 1) * HEAD]
            qp = q_ref[:, W + h * HEAD:W + (h + 1) * HEAD]
            kn = kv_ref[:, h * HEAD:(h + 1) * HEAD]
            qc_ref[:, h * QCAT:h * QCAT + HEAD] = (qn * _rms(qn, HEAD) * gqn_ref[...] * QS).astype(BF16)
            qc_ref[:, h * QCAT + HEAD:(h + 1) * QCAT] = (_rope(qp * _rms(qp, MLA_ROPE) * gqp_ref[...], cc, ss) * QS).astype(BF16)
            kc_ref[:, h * QCAT:h * QCAT + HEAD] = (kn * _rms(kn, HEAD) * gkn_ref[...]).astype(BF16)
            kc_ref[:, h * QCAT + HEAD:(h + 1) * QCAT] = kp
        v_ref[...] = kv_ref[:, W:2 * W].astype(BF16)

    g = _fullspec((1, HEAD))
    return pl.pallas_call(
        body, name=name, grid=(rows // tr,),
        in_specs=[_rowspec(tr, 2 * W), _rowspec(tr, 2 * W), _rowspec(tr, LANES, Z_KPE // LANES), _rowspec(tr, LANES),
                  _rowspec(tr, LANES), g, g, g, g],
        out_specs=[_rowspec(tr, MLA_HEADS * QCAT), _rowspec(tr, MLA_HEADS * QCAT), _rowspec(tr, W)],
        out_shape=[jax.ShapeDtypeStruct((rows, MLA_HEADS * QCAT), BF16), jax.ShapeDtypeStruct((rows, MLA_HEADS * QCAT), BF16),
                   jax.ShapeDtypeStruct((rows, W), BF16)],
        compiler_params=_params(("parallel",)),
    )(q, kv, z, cc, ss, gqn, gqp, gkn, gkp)


def _qk_bwd(q, kv, z, cc, ss, gqn, gqp, gkn, gkp, dqc, dkc, dv, name):
    rows = q.shape[0]
    tr = _row_tile(rows)
    W = MLA_HEADS * HEAD

    def body(q_ref, kv_ref, kpe_ref, cc_ref, ss_ref, gqn_ref, gqp_ref, gkn_ref, gkp_ref, dqc_ref, dkc_ref, dv_ref,
             dq_ref, dkv_ref, dkpe_ref, dgqn_ref, dgqp_ref, dgkn_ref, dgkp_ref):
        first = pl.program_id(0) == 0
        cc, ss = cc_ref[...], ss_ref[...]
        sqn = sqp = skn = dkp = None
        for h in range(MLA_HEADS):
            dx, dg = _rms_bwd_rows(q_ref[:, h * HEAD:(h + 1) * HEAD], gqn_ref[...], dqc_ref[:, h * QCAT:h * QCAT + HEAD], HEAD)
            dq_ref[:, h * HEAD:(h + 1) * HEAD] = dx.astype(BF16)
            sqn = dg if sqn is None else sqn + dg
            dy = _rope_bwd(dqc_ref[:, h * QCAT + HEAD:(h + 1) * QCAT], cc, ss)
            dx, dg = _rms_bwd_rows(q_ref[:, W + h * HEAD:W + (h + 1) * HEAD], gqp_ref[...], dy, MLA_ROPE)
            dq_ref[:, W + h * HEAD:W + (h + 1) * HEAD] = dx.astype(BF16)
            sqp = dg if sqp is None else sqp + dg
            dx, dg = _rms_bwd_rows(kv_ref[:, h * HEAD:(h + 1) * HEAD], gkn_ref[...], dkc_ref[:, h * QCAT:h * QCAT + HEAD], HEAD)
            dkv_ref[:, h * HEAD:(h + 1) * HEAD] = dx.astype(BF16)
            skn = dg if skn is None else skn + dg
            part = dkc_ref[:, h * QCAT + HEAD:(h + 1) * QCAT]
            dkp = part if dkp is None else dkp + part
        dkv_ref[:, W:2 * W] = dv_ref[...].astype(BF16)
        dx, dg = _rms_bwd_rows(kpe_ref[...], gkp_ref[...], _rope_bwd(dkp, cc, ss), MLA_ROPE)
        dkpe_ref[...] = dx
        _acc_rows(dgqn_ref, sqn, first)
        _acc_rows(dgqp_ref, sqp, first)
        _acc_rows(dgkn_ref, skn, first)
        _acc_rows(dgkp_ref, dg, first)

    g = _fullspec((1, HEAD))
    gs = jax.ShapeDtypeStruct((1, HEAD), F32)
    return pl.pallas_call(
        body, name=name, grid=(rows // tr,),
        in_specs=[_rowspec(tr, 2 * W), _rowspec(tr, 2 * W), _rowspec(tr, LANES, Z_KPE // LANES), _rowspec(tr, LANES),
                  _rowspec(tr, LANES), g, g, g, g, _rowspec(tr, MLA_HEADS * QCAT), _rowspec(tr, MLA_HEADS * QCAT),
                  _rowspec(tr, W)],
        out_specs=[_rowspec(tr, 2 * W), _rowspec(tr, 2 * W), _rowspec(tr, LANES), g, g, g, g],
        out_shape=[jax.ShapeDtypeStruct((rows, 2 * W), BF16), jax.ShapeDtypeStruct((rows, 2 * W), BF16),
                   jax.ShapeDtypeStruct((rows, LANES), F32), gs, gs, gs, gs],
        compiler_params=_params(("arbitrary",)),
    )(q, kv, z, cc, ss, gqn, gqp, gkn, gkp, dqc, dkc, dv)


def _headnorm_fwd(x, col, nheads, g, out_scale, name):
    rows = x.shape[0]
    tr = _row_tile(rows)
    W = nheads * HEAD

    def body(x_ref, g_ref, o_ref):
        for h in range(nheads):
            xv = x_ref[:, h * HEAD:(h + 1) * HEAD]
            o_ref[:, h * HEAD:(h + 1) * HEAD] = (xv * _rms(xv, HEAD) * g_ref[...] * out_scale).astype(BF16)

    return pl.pallas_call(
        body, name=name, grid=(rows // tr,),
        in_specs=[_rowspec(tr, W, col), _fullspec((1, HEAD))], out_specs=_rowspec(tr, W),
        out_shape=jax.ShapeDtypeStruct((rows, W), BF16), compiler_params=_params(("parallel",)),
    )(x, g)


def _headnorm_bwd(x, col, nheads, g, dy, tail, name):
    rows = x.shape[0]
    tr = _row_tile(rows)
    W = nheads * HEAD
    has_tail = tail is not None
    WO = 2 * W if has_tail else W

    def body(*refs):
        if has_tail:
            x_ref, g_ref, dy_ref, t_ref, dx_ref, dg_ref = refs
        else:
            x_ref, g_ref, dy_ref, dx_ref, dg_ref = refs
        acc = None
        for h in range(nheads):
            sl = slice(h * HEAD, (h + 1) * HEAD)
            dx, dg = _rms_bwd_rows(x_ref[:, sl], g_ref[...], dy_ref[:, sl], HEAD)
            dx_ref[:, sl] = dx.astype(BF16)
            acc = dg if acc is None else acc + dg
        if has_tail:
            dx_ref[:, W:2 * W] = t_ref[...].astype(BF16)
        _acc_rows(dg_ref, acc, pl.program_id(0) == 0)

    ins = [x, g, dy] + ([tail] if has_tail else [])
    specs = [_rowspec(tr, W, col), _fullspec((1, HEAD)), _rowspec(tr, W)] + ([_rowspec(tr, W)] if has_tail else [])
    return pl.pallas_call(
        body, name=name, grid=(rows // tr,), in_specs=specs,
        out_specs=[_rowspec(tr, WO), _fullspec((1, HEAD))],
        out_shape=[jax.ShapeDtypeStruct((rows, WO), BF16), jax.ShapeDtypeStruct((1, HEAD), F32)],
        compiler_params=_params(("arbitrary",)),
    )(*ins)


def _sigmoid(x):
    return 1.0 / (1.0 + jnp.exp(-x))


def _merge_fwd(z, y_gm, y_mla, y_mem, name):
    rows = z.shape[0]
    tr = _row_tile(rows)

    def body(g0_ref, g1_ref, g2_ref, a_ref, b_ref, c_ref, o_ref):
        m = _sigmoid(g0_ref[...]) * a_ref[...] + _sigmoid(g1_ref[...]) * b_ref[...] + _sigmoid(g2_ref[...]) * c_ref[...]
        o_ref[...] = m.astype(BF16)

    r = _rowspec(tr, D_MODEL)
    return pl.pallas_call(
        body, name=name, grid=(rows // tr,),
        in_specs=[_rowspec(tr, D_MODEL, 1), _rowspec(tr, D_MODEL, 2), _rowspec(tr, D_MODEL, 3), r, r, r],
        out_specs=r, out_shape=jax.ShapeDtypeStruct((rows, D_MODEL), BF16), compiler_params=_params(("parallel",)),
    )(z, z, z, y_gm, y_mla, y_mem)


def _merge_bwd(z, y_gm, y_mla, y_mem, dm, name):
    rows = z.shape[0]
    tr = _row_tile(rows)

    def body(g0_ref, g1_ref, g2_ref, a_ref, b_ref, c_ref, dm_ref, da_ref, db_ref, dc_ref, dzg_ref):
        dmv = dm_ref[...]
        for k, (g_ref, y_ref, dy_ref) in enumerate(((g0_ref, a_ref, da_ref), (g1_ref, b_ref, db_ref), (g2_ref, c_ref, dc_ref))):
            s = _sigmoid(g_ref[...])
            dy_ref[...] = (dmv * s).astype(BF16)
            dzg_ref[:, k * D_MODEL:(k + 1) * D_MODEL] = (dmv * y_ref[...] * s * (1.0 - s)).astype(BF16)

    r = _rowspec(tr, D_MODEL)
    o = jax.ShapeDtypeStruct((rows, D_MODEL), BF16)
    return pl.pallas_call(
        body, name=name, grid=(rows // tr,),
        in_specs=[_rowspec(tr, D_MODEL, 1), _rowspec(tr, D_MODEL, 2), _rowspec(tr, D_MODEL, 3), r, r, r, r],
        out_specs=[r, r, r, _rowspec(tr, 3 * D_MODEL)],
        out_shape=[o, o, o, jax.ShapeDtypeStruct((rows, 3 * D_MODEL), BF16)],
        compiler_params=_params(("parallel",)),
    )(z, z, z, y_gm, y_mla, y_mem, dm)


def _loss_head(y, target, name):
    rows, width = y.shape
    tr = _row_tile(rows)

    def body(y_ref, t_ref, dy_ref, l_ref):
        e = y_ref[...] - t_ref[...]
        dy_ref[...] = e * (1.0 / width)
        e2 = e * e
        part = e2[:, 0:LANES]
        for k in range(1, width // LANES):
            part = part + e2[:, k * LANES:(k + 1) * LANES]
        _acc_rows(l_ref, part, pl.program_id(0) == 0)

    return pl.pallas_call(
        body, name=name, grid=(rows // tr,),
        in_specs=[_rowspec(tr, width), _rowspec(tr, width)],
        out_specs=[_rowspec(tr, width), _fullspec((1, LANES))],
        out_shape=[jax.ShapeDtypeStruct((rows, width), F32), jax.ShapeDtypeStruct((1, LANES), F32)],
        compiler_params=_params(("arbitrary",)),
    )(y, target)


_NT = (((1,), (1,)), ((), ()))
_TN = (((0,), (0,)), ((), ()))


def _diag_mask(s):
    row = lax.broadcasted_iota(jnp.int32, s.shape, 0)
    col = lax.broadcasted_iota(jnp.int32, s.shape, 1)
    return jnp.where(row >= col, s, NEG)


def _attn_fwd(q, k, v, nb, nheads, dk, v_col0, causal, name, rider=None):
    S, Skv = q.shape[0] // nb, k.shape[0] // nb
    tq = _pick(Skv, ATT_TILE) if causal else _pick(S, 4 * ATT_TILE)
    nq = S // tq

    def body(q_ref, k_ref, v_ref, o_ref, lse_ref):
        for i in range(nq):
            r0 = i * tq
            qb = q_ref[r0:r0 + tq, :]
            if causal:
                spans = ([(0, r0, False)] if i > 0 else []) + [(r0, r0 + tq, True)]
            else:
                spans = [(0, Skv, False)]
            scores = []
            for a, b, masked in spans:
                s = lax.dot_general(qb, k_ref[a:b, :], _NT, preferred_element_type=F32)
                scores.append(_diag_mask(s) if masked else s)
            m = functools.reduce(jnp.maximum, [jnp.max(s, axis=-1, keepdims=True) for s in scores])
            l = acc = None
            for s, (a, b, _) in zip(scores, spans):
                p = jnp.exp2(s - m)
                lp = jnp.sum(p, axis=-1, keepdims=True)
                ap = jnp.dot(p.astype(BF16), v_ref[a:b, :].astype(BF16), preferred_element_type=F32)
                l, acc = (lp, ap) if l is None else (l + lp, acc + ap)
            o_ref[r0:r0 + tq, :] = acc / l
            lse_ref[r0:r0 + tq, :] = m + jnp.log2(l)

    ins = [q, k, v]
    in_specs = [pl.BlockSpec((S, dk), lambda b, h: (b, h)), pl.BlockSpec((Skv, dk), lambda b, h: (b, h)),
                pl.BlockSpec((Skv, HEAD), lambda b, h: (b, v_col0 + h))]
    out_specs = [pl.BlockSpec((S, HEAD), lambda b, h: (b, h)), pl.BlockSpec((None, S, 1), lambda b, h: (h, b, 0))]
    out_shape = [jax.ShapeDtypeStruct((nb * S, nheads * HEAD), F32), jax.ShapeDtypeStruct((nheads, nb * S, 1), F32)]
    return _call(body, rider, ins, name=name, grid=(nb, nheads), in_specs=in_specs, out_specs=out_specs,
                 out_shape=out_shape, scratch_shapes=[], sem=("parallel", "parallel"))


def _attn_bwd(q, k, v, o, do, lse, nb, nheads, dk, v_col0, scale, causal, name, rider=None):
    S, Skv = q.shape[0] // nb, k.shape[0] // nb
    tk = _pick(Skv, ATT_TILE)
    nkv = Skv // tk

    def body(q_ref, k_ref, v_ref, o_ref, do_ref, lse_ref, dq_ref, dk_ref, dv_ref, delta_ref, dob_ref):
        dov = do_ref[...]
        delta_ref[...] = jnp.sum(o_ref[...] * dov, axis=-1, keepdims=True)
        dob_ref[...] = dov.astype(BF16)

        for j in range(nkv):
            c0 = j * tk
            kb = k_ref[c0:c0 + tk, :]
            vb = v_ref[c0:c0 + tk, :].astype(BF16)
            if causal:
                spans = [(c0, c0 + tk, True)] + ([(c0 + tk, S, False)] if c0 + tk < S else [])
            else:
                spans = [(0, S, False)]
            dk_acc = dv_acc = None
            for a, b, masked in spans:
                qb = q_ref[a:b, :]
                dob = dob_ref[a:b, :]
                s = lax.dot_general(qb, kb, _NT, preferred_element_type=F32)
                if masked:
                    s = _diag_mask(s)
                p = jnp.exp2(s - lse_ref[a:b, :])
                dp = lax.dot_general(dob, vb, _NT, preferred_element_type=F32)
                ds = (p * (dp - delta_ref[a:b, :])).astype(BF16)
                dv_p = lax.dot_general(p.astype(BF16), dob, _TN, preferred_element_type=F32)
                dk_p = lax.dot_general(ds, qb, _TN, preferred_element_type=F32)
                dk_acc, dv_acc = (dk_p, dv_p) if dk_acc is None else (dk_acc + dk_p, dv_acc + dv_p)
                dq_p = jnp.dot(ds, kb, preferred_element_type=F32) * scale
                if j == 0:
                    dq_ref[a:b, :] = dq_p
                else:
                    dq_ref[a:b, :] += dq_p
            dk_ref[c0:c0 + tk, :] = dk_acc * LN2
            dv_ref[c0:c0 + tk, :] = dv_acc

    ins = [q, k, v, o, do, lse]
    in_specs = [pl.BlockSpec((S, dk), lambda b, h: (b, h)), pl.BlockSpec((Skv, dk), lambda b, h: (b, h)),
                pl.BlockSpec((Skv, HEAD), lambda b, h: (b, v_col0 + h)), pl.BlockSpec((S, HEAD), lambda b, h: (b, h)),
                pl.BlockSpec((S, HEAD), lambda b, h: (b, h)), pl.BlockSpec((None, S, 1), lambda b, h: (h, b, 0))]
    out_specs = [pl.BlockSpec((S, dk), lambda b, h: (b, h)), pl.BlockSpec((Skv, dk), lambda b, h: (b, h)),
                 pl.BlockSpec((Skv, HEAD), lambda b, h: (b, h))]
    out_shape = [jax.ShapeDtypeStruct((nb * S, nheads * dk), F32), jax.ShapeDtypeStruct((nb * Skv, nheads * dk), F32),
                 jax.ShapeDtypeStruct((nb * Skv, nheads * HEAD), F32)]
    return _call(body, rider, ins, name=name, grid=(nb, nheads), in_specs=in_specs, out_specs=out_specs,
                 out_shape=out_shape, scratch_shapes=[pltpu.VMEM((S, 1), F32), pltpu.VMEM((S, HEAD), BF16)],
                 sem=("parallel", "parallel"))


def _spread_rope(a):
    zero = jnp.zeros(a.shape[:-1] + (32,), a.dtype)
    return jnp.concatenate([a[..., :32], zero, a[..., 32:], zero], axis=-1)


def _gather_rope(a):
    return jnp.concatenate([a[..., 0:32], a[..., 64:96]], axis=-1)


def _win_layout(w):
    return jnp.concatenate([w[:, C_ZU:C_CQ], w[:, C_ZG:C_END], w[:, C_QM:C_ZG], w[:, C_CQ:C_CKV], w[:, C_CKV:C_KPE],
                            _spread_rope(w[:, C_KPE:C_QM])], axis=1)


def _win_unlayout(d):
    return jnp.concatenate([d[:, 0:Z_GATE], d[:, Z_MLA:Z_MLA + Q_LORA], d[:, Z_MLA + Q_LORA:Z_KPE],
                            _gather_rope(d[:, Z_KPE:Z_COLS]), d[:, Z_QM:Z_MLA], d[:, Z_GATE:Z_QM]], axis=1)


def _wuq_layout(w):
    r = w.reshape(Q_LORA, MLA_HEADS, HEAD + MLA_ROPE)
    return jnp.concatenate([r[:, :, :HEAD].reshape(Q_LORA, -1), _spread_rope(r[:, :, HEAD:]).reshape(Q_LORA, -1)], axis=1)


def _wuq_unlayout(d):
    n = d[:, :MLA_HEADS * HEAD].reshape(Q_LORA, MLA_HEADS, HEAD)
    p = _gather_rope(d[:, MLA_HEADS * HEAD:].reshape(Q_LORA, MLA_HEADS, HEAD))
    return jnp.concatenate([n, p], axis=-1).reshape(Q_LORA, -1)


def _wukv_layout(w):
    r = w.reshape(KV_LORA, MLA_HEADS, 2 * HEAD)
    return jnp.concatenate([r[:, :, :HEAD].reshape(KV_LORA, -1), r[:, :, HEAD:].reshape(KV_LORA, -1)], axis=1)


def _wukv_unlayout(d):
    k = d[:, :MLA_HEADS * HEAD].reshape(KV_LORA, MLA_HEADS, HEAD)
    v = d[:, MLA_HEADS * HEAD:].reshape(KV_LORA, MLA_HEADS, HEAD)
    return jnp.concatenate([k, v], axis=-1).reshape(KV_LORA, -1)


AG_MID = ['w_uq', 'w_ukv', 'w_mem_kv', 'w_o_gm', 'w_o_mla', 'w_o_mem', 'w_out']
AG_FFN = ['w_ff1', 'w_ff2']
RS_GROUPS = {'ff2': ['w_ff2'], 'ff1': ['w_ff1'], 'proj': ['w_out', 'w_o_gm', 'w_o_mla', 'w_o_mem'],
             'lat': ['w_uq', 'w_ukv', 'w_mem_kv'], 'in': ['w_in']}


def _unride(res, rider):
    return (res, None) if rider is None else res


def _local_step(x, mem, positions, target, P, ws):
    B, S, _ = x.shape
    M = mem.shape[1]
    T = B * S
    x2d = x.reshape(T, D_MODEL)
    mem2d = mem.reshape(B * M, D_MODEL)
    tgt2d = target.reshape(T, D_MODEL)

    def row(v):
        return v.reshape(1, -1).astype(F32)

    inv_freq = ROPE_BASE ** (-jnp.arange(0, MLA_ROPE, 2, dtype=F32) / MLA_ROPE)
    ang = positions.reshape(T).astype(F32)[:, None] * inv_freq
    cos, sin, zero = jnp.cos(ang), jnp.sin(ang), jnp.zeros_like(ang)
    cc = jnp.concatenate([cos, zero, cos, zero], axis=1)
    ss = jnp.concatenate([-sin, zero, sin, zero], axis=1)

    w_in = _win_layout(ws.first()).astype(BF16)
    g_mix, g_cq, g_ckv, g_ffn, g_mem = row(P['g_mix']), row(P['g_cq']), row(P['g_ckv']), row(P['g_ffn']), row(P['g_mem'])
    gqn, gkn, gmq, gmk = row(P['g_q_nope']), row(P['g_k_nope']), row(P['g_mq']), row(P['g_mk'])
    gqp, gkp = _spread_rope(row(P['g_q_pe'])), _spread_rope(row(P['g_k_pe']))
    gln, bln = row(P['g_gm_ln']), row(P['b_gm_ln'])
    wc = jnp.tril(P['w_spatial'].astype(F32))
    wct = jnp.swapaxes(wc, 1, 2).astype(BF16)
    wc = wc.astype(BF16)
    bst = jnp.broadcast_to(P['b_spatial'].astype(F32)[:, :, None], (GM_GROUPS, GM_CHUNK, LANES))

    h = _rms_fwd(x2d, g_mix, "rms_mix")
    ride = ws.gather(AG_MID)
    z, got = _unride(_matmul(h, w_in, 'nn', F32, "mm_in", tn_t=768, rider=ride), ride)
    mid = ws.gathered(AG_MID, got)
    w_uq = _wuq_layout(mid['w_uq']).astype(BF16)
    w_ukv = _wukv_layout(mid['w_ukv']).astype(BF16)
    w_mem_kv, w_o_gm, w_o_mla, w_o_mem, w_out = (mid[n].astype(BF16) for n in ('w_mem_kv', 'w_o_gm', 'w_o_mla', 'w_o_mem',
                                                                                 'w_out'))
    ygm_pre = _gm_fwd(z, gln, bln, wc, bst, "gm_fwd")
    y_gm = _matmul(ygm_pre, w_o_gm, 'nn', F32, "mm_o_gm")
    nq, nkv = _lat_fwd(z, g_cq, g_ckv, "lat_fwd")
    q = _matmul(nq, w_uq, 'nn', F32, "mm_uq")
    kv = _matmul(nkv, w_ukv, 'nn', F32, "mm_ukv")
    qcat, kcat, vv = _qk_fwd(q, kv, z, cc, ss, gqn, gqp, gkn, gkp, "qk_fwd")
    ride = ws.gather(AG_FFN)
    (o, lse), got = _unride(_attn_fwd(qcat, kcat, vv, B, MLA_HEADS, QCAT, 0, True, "mla_attn_fwd", rider=ride), ride)
    ffn = ws.gathered(AG_FFN, got)
    w_ff1, w_ff2 = ffn['w_ff1'].astype(BF16), ffn['w_ff2'].astype(BF16)
    y_mla = _matmul(o, w_o_mla, 'nn', F32, "mm_o_mla")
    nm = _rms_fwd(mem2d, g_mem, "rms_mem")
    kvm = _matmul(nm, w_mem_kv, 'nn', F32, "mm_mem_kv")
    qm = _headnorm_fwd(z, Z_QM // (MEM_HEADS * HEAD), MEM_HEADS, gmq, MEM_SCALE * LOG2E, "memq_fwd")
    km = _headnorm_fwd(kvm, 0, MEM_HEADS, gmk, 1.0, "memk_fwd")
    om, lse_m = _attn_fwd(qm, km, kvm, B, MEM_HEADS, HEAD, MEM_HEADS, False, "mem_attn_fwd")
    y_mem = _matmul(om, w_o_mem, 'nn', F32, "mm_o_mem")
    merged = _merge_fwd(z, y_gm, y_mla, y_mem, "merge_fwd")
    x1 = _matmul(merged, w_out, 'nn', F32, "mm_out", add=x2d)
    h2 = _rms_fwd(x1, g_ffn, "rms_ffn")
    a1 = _matmul(h2, w_ff1, 'nn', BF16, "mm_ff1")
    x2 = _matmul(a1, w_ff2, 'nn', F32, "mm_ff2", add=x1, relu2_a=True)
    dx2, loss_part = _loss_head(x2, tgt2d, "loss_head")

    G = {}
    ride = ws.scatter('ff2', {'w_ff2': _matmul(a1, dx2, 'tn', BF16, "mm_d_ff2", relu2_a=True)})
    da1, got = _unride(_matmul(dx2, w_ff2, 'nt', BF16, "mm_da1", relu2_grad=a1, rider=ride), ride)
    ws.scattered('ff2', got)
    ride = ws.scatter('ff1', {'w_ff1': _matmul(h2, da1, 'tn', BF16, "mm_d_ff1")})
    dh2, got = _unride(_matmul(da1, w_ff1, 'nt', F32, "mm_dh2", rider=ride), ride)
    ws.scattered('ff1', got)
    dx1, G['g_ffn'] = _rms_bwd(x1, g_ffn, dh2, dx2, "rms_ffn_bwd")
    d_out = _matmul(merged, dx1, 'tn', BF16, "mm_d_out")
    dmerged = _matmul(dx1, w_out, 'nt', F32, "mm_dmerged")
    dy_gm, dy_mla, dy_mem, dzg = _merge_bwd(z, y_gm, y_mla, y_mem, dmerged, "merge_bwd")
    d_o_gm = _matmul(ygm_pre, dy_gm, 'tn', BF16, "mm_d_o_gm")
    d_o_mla = _matmul(o, dy_mla, 'tn', BF16, "mm_d_o_mla")
    d_o_mem = _matmul(om, dy_mem, 'tn', BF16, "mm_d_o_mem")
    dygm_pre = _matmul(dy_gm, w_o_gm, 'nt', F32, "mm_dygm")
    dz_gm, dws, dbs, G['g_gm_ln'], G['b_gm_ln'] = _gm_bwd(z, dygm_pre, gln, bln, wc, wct, bst, "gm_bwd")
    G['w_spatial'] = jnp.tril(dws)
    G['b_spatial'] = jnp.sum(dbs.reshape(GM_CHUNK, GM_GROUPS, LANES), axis=-1).T
    do = _matmul(dy_mla, w_o_mla, 'nt', F32, "mm_do")
    ride = ws.scatter('proj', {'w_out': d_out, 'w_o_gm': d_o_gm, 'w_o_mla': d_o_mla, 'w_o_mem': d_o_mem})
    (dqc, dkc, dvv), got = _unride(_attn_bwd(qcat, kcat, vv, o, do, lse, B, MLA_HEADS, QCAT, 0, MLA_SCALE, True,
                                             "mla_attn_bwd", rider=ride), ride)
    ws.scattered('proj', got)
    dq, dkv, dkpe, G['g_q_nope'], dgqp, G['g_k_nope'], dgkp = _qk_bwd(q, kv, z, cc, ss, gqn, gqp, gkn, gkp, dqc, dkc, dvv,
                                                                     "qk_bwd")
    G['g_q_pe'], G['g_k_pe'] = _gather_rope(dgqp), _gather_rope(dgkp)
    d_uq = _wuq_unlayout(_matmul(nq, dq, 'tn', BF16, "mm_d_uq"))
    dnq = _matmul(dq, w_uq, 'nt', F32, "mm_dnq")
    d_ukv = _wukv_unlayout(_matmul(nkv, dkv, 'tn', BF16, "mm_d_ukv"))
    dnkv = _matmul(dkv, w_ukv, 'nt', F32, "mm_dnkv")
    dz_mla, G['g_cq'], G['g_ckv'] = _lat_bwd(z, dnq, dnkv, dkpe, g_cq, g_ckv, "lat_bwd")
    dom = _matmul(dy_mem, w_o_mem, 'nt', F32, "mm_dom")
    dqm, dkm, dvm = _attn_bwd(qm, km, kvm, om, dom, lse_m, B, MEM_HEADS, HEAD, MEM_HEADS, MEM_SCALE, False, "mem_attn_bwd")
    dz_qm, G['g_mq'] = _headnorm_bwd(z, Z_QM // (MEM_HEADS * HEAD), MEM_HEADS, gmq, dqm, None, "memq_bwd")
    dkvm, G['g_mk'] = _headnorm_bwd(kvm, 0, MEM_HEADS, gmk, dkm, dvm, "memk_bwd")
    d_mem_kv = _matmul(nm, dkvm, 'tn', BF16, "mm_d_mem_kv")
    dnm = _matmul(dkvm, w_mem_kv, 'nt', F32, "mm_dnm")
    _, G['g_mem'] = _rms_bwd(mem2d, g_mem, dnm, None, "rms_mem_bwd")
    dz = jnp.concatenate([dz_gm, dzg, dz_qm, dz_mla], axis=1)
    ride = ws.scatter('lat', {'w_uq': d_uq, 'w_ukv': d_ukv, 'w_mem_kv': d_mem_kv})
    d_in, got = _unride(_matmul(h, dz, 'tn', BF16, "mm_d_in", tn_t=768, rider=ride), ride)
    ws.scattered('lat', got)
    ride = ws.scatter('in', {'w_in': _win_unlayout(d_in)})
    dh, got = _unride(_matmul(dz, w_in, 'nt', F32, "mm_dh", tk_t=768, rider=ride), ride)
    ws.scattered('in', got)
    gx, G['g_mix'] = _rms_bwd(x2d, g_mix, dh, dx1, "rms_mix_bwd")
    return loss_part, gx.reshape(B, S, D_MODEL), G


def _all_gather8(xs, name):
    def body(x_ref, out_ref, send_sems, recv_sems, local_sem):
        x, y, c = lax.axis_index("x"), lax.axis_index("y"), lax.axis_index("c")
        me, sibling = (x, y, c), (x, y, 1 - c)
        chips = [(1 - x, y), (x, 1 - y), (1 - x, 1 - y)]

        def rows(px, py, pc):
            return out_ref.at[4 * px + 2 * py + pc]

        def copy(k, block, to, src=None):
            return pltpu.make_async_remote_copy(
                src_ref=rows(*block) if src is None else src, dst_ref=rows(*block),
                send_sem=send_sems.at[k], recv_sem=recv_sems.at[k], device_id=to, device_id_type=MESH)

        mine = pltpu.make_async_copy(x_ref, rows(*me), local_sem)
        mine.start()
        first = [copy(0, me, sibling, src=x_ref)]
        first += [copy(1 + j, me, (*chip, c), src=x_ref) for j, chip in enumerate(chips)]
        for cp in first:
            cp.start()
        passed = [copy(4 + j, (*chip, c), sibling) for j, chip in enumerate(chips)]
        for j, chip in enumerate(chips):
            copy(1 + j, (*chip, c), me).wait_recv()
            passed[j].start()
        copy(0, sibling, me).wait_recv()
        for j, chip in enumerate(chips):
            copy(4 + j, (*chip, 1 - c), me).wait_recv()
        for cp in first + passed:
            cp.wait_send()
        mine.wait()

    return pl.pallas_call(
        body, name=name, in_specs=[HBM_SPEC], out_specs=HBM_SPEC,
        out_shape=jax.ShapeDtypeStruct((N_DEV,) + xs.shape, xs.dtype),
        scratch_shapes=[pltpu.SemaphoreType.DMA((7,)), pltpu.SemaphoreType.DMA((7,)), pltpu.SemaphoreType.DMA],
    )(xs)


def _adamw_rows(w, g, m, v):
    m2 = ADAM_B1 * m + (1.0 - ADAM_B1) * g
    v2 = ADAM_B2 * v + (1.0 - ADAM_B2) * (g * g)
    m_hat = m2 / (1.0 - ADAM_B1 ** ADAM_STEP)
    v_hat = v2 / (1.0 - ADAM_B2 ** ADAM_STEP)
    delta = -ADAM_LR * (m_hat / (jnp.sqrt(v_hat) + ADAM_EPS) + ADAM_WD * w)
    return delta, m2, v2


def _sum_adamw(parts, w, m, v, name):
    rows, cols = w.shape
    tr = _pick(rows, max(16, 65536 // cols), 16)
    n = parts.shape[0]

    def body(p_ref, w_ref, m_ref, v_ref, g_ref, d_ref, m2_ref, v2_ref):
        g = p_ref[0].astype(F32)
        for k in range(1, n):
            g = g + p_ref[k].astype(F32)
        delta, m2, v2 = _adamw_rows(w_ref[...], g, m_ref[...], v_ref[...])
        g_ref[...] = g
        d_ref[...] = delta
        m2_ref[...] = m2
        v2_ref[...] = v2

    flat = pl.BlockSpec((tr, cols), lambda i: (i, 0))
    out = jax.ShapeDtypeStruct((rows, cols), F32)
    return pl.pallas_call(
        body, name=name, grid=(rows // tr,),
        in_specs=[pl.BlockSpec((n, tr, cols), lambda i: (0, i, 0)), flat, flat, flat], out_specs=[flat] * 4,
        out_shape=[out] * 4, compiler_params=_params(("parallel",)),
    )(parts, w, m, v)


def _small_rows(name):
    n = {'g_mix': 1024, 'g_cq': 384, 'g_ckv': 256, 'g_q_nope': 128, 'g_q_pe': 64, 'g_k_nope': 128, 'g_k_pe': 64,
         'g_gm_ln': 512, 'b_gm_ln': 512, 'w_spatial': GM_GROUPS * GM_CHUNK * GM_CHUNK, 'b_spatial': GM_GROUPS * GM_CHUNK,
         'g_mem': 1024, 'g_mq': 128, 'g_mk': 128, 'g_ffn': 1024}[name]
    return n, -(-n // LANES)


def _small_slab(d):
    parts = []
    for name in SMALL:
        n, rows = _small_rows(name)
        parts.append(jnp.pad(d[name].reshape(-1).astype(F32), (0, rows * LANES - n)).reshape(rows, LANES))
    slab = jnp.concatenate(parts, axis=0)
    return jnp.pad(slab, ((0, -slab.shape[0] % 8), (0, 0)))


def _small_unslab(slab, like):
    out, r = {}, 0
    for name in SMALL:
        n, rows = _small_rows(name)
        out[name] = slab[r:r + rows].reshape(-1)[:n].reshape(like[name].shape)
        r += rows
    return out


def _full_from_gathered(gathered, name):
    r, c = BIG_SHAPE[name]
    if BIG_AXIS[name] == 0:
        return gathered.reshape(r, c)
    return gathered.transpose(1, 0, 2).reshape(r, c)


def _shards_of_full(g, name):
    r, c = BIG_SHAPE[name]
    if BIG_AXIS[name] == 0:
        return g.reshape(N_DEV, r // N_DEV, c)
    return g.reshape(r, N_DEV, c // N_DEV).transpose(1, 0, 2)


class _DistWeights:
    def __init__(self, shards):
        self.shards = shards
        self.received = {}

    def first(self):
        return _full_from_gathered(_all_gather8(self.shards['w_in'].astype(BF16), "ag_w_in"), 'w_in')

    def gather(self, names):
        return _Exchange([self.shards[n].astype(BF16) for n in names], scatter=False)

    def gathered(self, names, got):
        return {n: _full_from_gathered(g, n) for n, g in zip(names, got)}

    def scatter(self, key, grads):
        return _Exchange([_shards_of_full(grads[n], n) for n in RS_GROUPS[key]], scatter=True)

    def scattered(self, key, got):
        self.received.update(zip(RS_GROUPS[key], got))


def kernel(x, mem, positions, g_mix, w_in, g_cq, w_uq, g_ckv, w_ukv, g_q_nope, g_q_pe, g_k_nope, g_k_pe, g_gm_ln, b_gm_ln, w_spatial, b_spatial, g_mem, w_mem_kv, g_mq, g_mk, w_o_gm, w_o_mla, w_o_mem, w_out, g_ffn, w_ff1, w_ff2, loss_target, m_g_mix, m_w_in, m_g_cq, m_w_uq, m_g_ckv, m_w_ukv, m_g_q_nope, m_g_q_pe, m_g_k_nope, m_g_k_pe, m_g_gm_ln, m_b_gm_ln, m_w_spatial, m_b_spatial, m_g_mem, m_w_mem_kv, m_g_mq, m_g_mk, m_w_o_gm, m_w_o_mla, m_w_o_mem, m_w_out, m_g_ffn, m_w_ff1, m_w_ff2, v_g_mix, v_w_in, v_g_cq, v_w_uq, v_g_ckv, v_w_ukv, v_g_q_nope, v_g_q_pe, v_g_k_nope, v_g_k_pe, v_g_gm_ln, v_b_gm_ln, v_w_spatial, v_b_spatial, v_g_mem, v_w_mem_kv, v_g_mq, v_g_mk, v_w_o_gm, v_w_o_mla, v_w_o_mem, v_w_out, v_g_ffn, v_w_ff1, v_w_ff2):
    given = dict(locals())
    w = {n: given[n][0] for n in WEIGHTS}
    mom = {n: given['m_' + n][0] for n in WEIGHTS}
    var = {n: given['v_' + n][0] for n in WEIGHTS}

    ws = _DistWeights({n: w[n] for n in BIG})
    loss_part, grad_x, G = _local_step(x, mem, positions, loss_target, {n: w[n] for n in SMALL}, ws)
    loss = lax.psum(0.5 * jnp.sum(loss_part) / D_MODEL, ("x", "y", "c"))

    outs = {}
    for n in BIG:
        for prefix, res in zip(("grad_", "delta_", "new_m_", "new_v_"),
                               _sum_adamw(ws.received[n], w[n], mom[n], var[n], "adamw_" + n)):
            outs[prefix + n] = res[None]

    parts = _all_gather8(_small_slab(G), "ag_small")
    small = _sum_adamw(parts, _small_slab(w), _small_slab(mom), _small_slab(var), "adamw_small")
    for prefix, small_slab in zip(("grad_", "delta_", "new_m_", "new_v_"), small):
        sm = _small_unslab(small_slab, given)
        for n in SMALL:
            outs[prefix + n] = sm[n]
    return (loss, grad_x, *[outs[p + n] for p in ("grad_", "delta_", "new_m_", "new_v_") for n in WEIGHTS])
```

```python
import functools
import math

import jax
import jax.numpy as jnp
from jax import lax
from jax.experimental import pallas as pl
from jax.experimental.pallas import tpu as pltpu

F32 = jnp.float32
BF16 = jnp.bfloat16
ACT = BF16

D_MODEL = 1024
MEM_HEADS = 4
HEAD = 128
GM_WIDTH = 512
GM_CHUNK = 128
GM_GROUPS = 4
MLA_HEADS = 8
MLA_ROPE = 64
Q_LORA = 384
KV_LORA = 256
D_FF = 4096
EPS = 1e-6
ROPE_BASE = 10000.0
MLA_SCALE = 1.0 / math.sqrt(HEAD + MLA_ROPE)
MEM_SCALE = 1.0 / math.sqrt(HEAD)
LOG2E = 1.4426950408889634
LN2 = 0.6931471805599453
ATT_TILE = 256
C_ZU, C_ZV, C_CQ, C_CKV, C_KPE, C_QM, C_ZG, C_END = 0, 512, 1024, 1408, 1664, 1728, 2240, 5312
Z_GM, Z_QM, Z_MLA, Z_KPE, Z_COLS = 3072, 4096, 4608, 5248, 5376
MLA_W = 768
QCAT = 2 * HEAD
ADAM_LR, ADAM_B1, ADAM_B2, ADAM_EPS, ADAM_WD, ADAM_STEP = 0.001, 0.9, 0.999, 1e-08, 0.01, 10
N_DEV = 8
LANES = 128
VMEM_LIMIT = 48 * 1024 * 1024
NEG = -1e30

BIG = ['w_in', 'w_uq', 'w_ukv', 'w_mem_kv', 'w_o_gm', 'w_o_mla', 'w_o_mem', 'w_out', 'w_ff1', 'w_ff2']
BIG_AXIS = {'w_in': 1, 'w_uq': 1, 'w_ukv': 1, 'w_mem_kv': 0, 'w_o_gm': 1, 'w_o_mla': 0, 'w_o_mem': 1,
            'w_out': 0, 'w_ff1': 1, 'w_ff2': 0}
BIG_SHAPE = {'w_in': (1024, 5312), 'w_uq': (384, 1536), 'w_ukv': (256, 2048), 'w_mem_kv': (1024, 1024),
             'w_o_gm': (512, 1024), 'w_o_mla': (1024, 1024), 'w_o_mem': (512, 1024), 'w_out': (1024, 1024),
             'w_ff1': (1024, 4096), 'w_ff2': (4096, 1024)}
SMALL = ['g_mix', 'g_cq', 'g_ckv', 'g_q_nope', 'g_q_pe', 'g_k_nope', 'g_k_pe', 'g_gm_ln', 'b_gm_ln',
         'w_spatial', 'b_spatial', 'g_mem', 'g_mq', 'g_mk', 'g_ffn']
WEIGHTS = ['g_mix', 'w_in', 'g_cq', 'w_uq', 'g_ckv', 'w_ukv', 'g_q_nope', 'g_q_pe', 'g_k_nope', 'g_k_pe',
           'g_gm_ln', 'b_gm_ln', 'w_spatial', 'b_spatial', 'g_mem', 'w_mem_kv', 'g_mq', 'g_mk', 'w_o_gm',
           'w_o_mla', 'w_o_mem', 'w_out', 'g_ffn', 'w_ff1', 'w_ff2']


def _pick(n, target, mult=LANES):
    best = None
    t = mult
    while t <= min(n, target):
        if n % t == 0:
            best = t
        t += mult
    return best if best is not None else n


def _params(sem):
    return pltpu.CompilerParams(dimension_semantics=sem, vmem_limit_bytes=VMEM_LIMIT)


MESH = pl.DeviceIdType.MESH
HBM_SPEC = pl.BlockSpec(memory_space=pltpu.HBM)


class _Exchange:
    def __init__(self, srcs, scatter):
        self.srcs, self.scatter = list(srcs), scatter
        self.out_shapes = [jax.ShapeDtypeStruct(s.shape if scatter else (N_DEV,) + s.shape, s.dtype) for s in self.srcs]
        n = len(self.srcs)
        self.scratch = [pltpu.SemaphoreType.DMA((n, N_DEV - 1)), pltpu.SemaphoreType.DMA((n, N_DEV - 1)),
                        pltpu.SemaphoreType.DMA((n,))]

    def _copies(self, src_refs, dst_refs, send_sems, recv_sems, local_sems):
        x, y, c = lax.axis_index("x"), lax.axis_index("y"), lax.axis_index("c")
        me = 4 * x + 2 * y + c
        local, remote = [], []
        for a, (src_ref, dst_ref) in enumerate(zip(src_refs, dst_refs)):
            def mine_for(dev, src_ref=src_ref):
                return src_ref.at[dev] if self.scatter else src_ref

            local.append(pltpu.make_async_copy(mine_for(me), dst_ref.at[me], local_sems.at[a]))
            for k in range(1, N_DEV):
                px = 1 - x if k & 4 else x
                py = 1 - y if k & 2 else y
                pc = 1 - c if k & 1 else c
                remote.append(pltpu.make_async_remote_copy(
                    src_ref=mine_for(4 * px + 2 * py + pc), dst_ref=dst_ref.at[me], send_sem=send_sems.at[a, k - 1],
                    recv_sem=recv_sems.at[a, k - 1], device_id=(px, py, pc), device_id_type=MESH))
        return local, remote

    def start(self, *refs):
        local, remote = self._copies(*refs)
        for cp in local + remote:
            cp.start()

    def wait(self, *refs):
        local, remote = self._copies(*refs)
        for cp in remote + local:
            cp.wait()


def _call(body, rider, ins, *, name, grid, in_specs, out_specs, out_shape, scratch_shapes, sem):
    if rider is None:
        return pl.pallas_call(body, name=name, grid=grid, in_specs=in_specs, out_specs=out_specs, out_shape=out_shape,
                              scratch_shapes=scratch_shapes, compiler_params=_params(sem))(*ins)
    single = not isinstance(out_shape, (list, tuple))
    own_specs, own_shapes = ([out_specs], [out_shape]) if single else (list(out_specs), list(out_shape))
    n_in, n_out, n_sc, n_r = len(ins), len(own_shapes), len(scratch_shapes), len(rider.srcs)

    def carrying(*refs):
        own_in, srcs = refs[:n_in], refs[n_in:n_in + n_r]
        own_out, dsts = refs[n_in + n_r:n_in + n_r + n_out], refs[n_in + n_r + n_out:n_in + 2 * n_r + n_out]
        own_sc = refs[n_in + 2 * n_r + n_out:n_in + 2 * n_r + n_out + n_sc]
        sems = refs[n_in + 2 * n_r + n_out + n_sc:]
        first = last = None
        for d, steps in enumerate(grid):
            f, l = pl.program_id(d) == 0, pl.program_id(d) == steps - 1
            first, last = (f, l) if first is None else (first & f, last & l)

        @pl.when(first)
        def _():
            rider.start(srcs, dsts, *sems)

        body(*own_in, *own_out, *own_sc)

        @pl.when(last)
        def _():
            rider.wait(srcs, dsts, *sems)

    res = pl.pallas_call(
        carrying, name=name, grid=grid, in_specs=list(in_specs) + [HBM_SPEC] * n_r, out_specs=own_specs + [HBM_SPEC] * n_r,
        out_shape=own_shapes + rider.out_shapes, scratch_shapes=list(scratch_shapes) + rider.scratch,
        compiler_params=_params(("arbitrary",) * len(grid)),
    )(*ins, *rider.srcs)
    own = res[:n_out]
    return (own[0] if single else list(own)), list(res[n_out:])


def _matmul(a, b, mode, out_dtype, name, add=None, relu2_a=False, relu2_grad=None,
            tm_t=2048, tn_t=512, tk_t=1024, rider=None):
    if mode == 'nn':
        (M, K), (K2, N) = a.shape, b.shape
    elif mode == 'nt':
        (M, K), (N, K2) = a.shape, b.shape
    else:
        (K, M), (K2, N) = a.shape, b.shape
    assert K == K2, (name, a.shape, b.shape)
    tm, tn, tk = _pick(M, tm_t), _pick(N, tn_t), _pick(K, tk_t)
    gm, gn, nk = M // tm, N // tn, K // tk
    if mode == 'nn':
        a_spec = pl.BlockSpec((tm, tk), lambda i, j, k: (i, k))
        b_spec = pl.BlockSpec((tk, tn), lambda i, j, k: (k, j))
        dims = (((1,), (0,)), ((), ()))
    elif mode == 'nt':
        a_spec = pl.BlockSpec((tm, tk), lambda i, j, k: (i, k))
        b_spec = pl.BlockSpec((tn, tk), lambda i, j, k: (j, k))
        dims = (((1,), (1,)), ((), ()))
    else:
        a_spec = pl.BlockSpec((tk, tm), lambda i, j, k: (k, i))
        b_spec = pl.BlockSpec((tk, tn), lambda i, j, k: (k, j))
        dims = (((0,), (0,)), ((), ()))
    o_spec = pl.BlockSpec((tm, tn), lambda i, j, k: (i, j))
    has_add, has_e = add is not None, relu2_grad is not None

    def body(*refs):
        a_ref, b_ref = refs[0], refs[1]
        pos = 2
        add_ref = e_ref = None
        if has_add:
            add_ref = refs[pos]
            pos += 1
        if has_e:
            e_ref = refs[pos]
            pos += 1
        o_ref = refs[pos]
        acc_ref = refs[pos + 1] if nk > 1 else None

        av = a_ref[...]
        if relu2_a:
            av = jnp.maximum(av.astype(F32), 0.0)
            av = av * av
        prod = lax.dot_general(av.astype(BF16), b_ref[...].astype(BF16), dims, preferred_element_type=F32)

        def finish(r):
            if has_add:
                r = r + add_ref[...]
            if has_e:
                r = r * (2.0 * jnp.maximum(e_ref[...].astype(F32), 0.0))
            o_ref[...] = r.astype(out_dtype)

        if nk == 1:
            finish(prod)
        else:
            k = pl.program_id(2)

            @pl.when(k == 0)
            def _():
                acc_ref[...] = prod

            @pl.when(k > 0)
            def _():
                acc_ref[...] += prod

            @pl.when(k == nk - 1)
            def _():
                finish(acc_ref[...])

    ins, specs = [a, b], [a_spec, b_spec]
    if has_add:
        ins.append(add)
        specs.append(o_spec)
    if has_e:
        ins.append(relu2_grad)
        specs.append(o_spec)
    return _call(body, rider, ins, name=name, grid=(gm, gn, nk), in_specs=specs, out_specs=o_spec,
                 out_shape=jax.ShapeDtypeStruct((M, N), out_dtype),
                 scratch_shapes=[pltpu.VMEM((tm, tn), F32)] if nk > 1 else [], sem=("parallel", "parallel", "arbitrary"))


def _row_tile(rows, target=256):
    return _pick(rows, target, 8)


def _rowspec(tr, width, col=0):
    return pl.BlockSpec((tr, width), lambda i, col=col: (i, col))


def _fullspec(shape):
    nd = len(shape)
    return pl.BlockSpec(shape, lambda i, nd=nd: (0,) * nd)


def _rms(x, width):
    x = x.astype(F32)
    return lax.rsqrt(jnp.sum(x * x, axis=-1, keepdims=True) * (1.0 / width) + EPS)


def _rms_bwd_rows(x, g, dy, width):
    x, dy = x.astype(F32), dy.astype(F32)
    r = _rms(x, width)
    xh = x * r
    dn = dy * g
    dx = r * (dn - xh * (jnp.sum(dn * xh, axis=-1, keepdims=True) * (1.0 / width)))
    return dx, dy * xh


def _acc_rows(ref, val, first):
    s = jnp.sum(val, axis=0, keepdims=True)

    @pl.when(first)
    def _():
        ref[...] = s

    @pl.when(jnp.logical_not(first))
    def _():
        ref[...] += s


def _rms_fwd(x, g, name):
    rows, width = x.shape
    tr = _row_tile(rows)

    def body(x_ref, g_ref, o_ref):
        xv = x_ref[...]
        o_ref[...] = (xv * _rms(xv, width) * g_ref[...]).astype(BF16)

    return pl.pallas_call(
        body, name=name, grid=(rows // tr,),
        in_specs=[_rowspec(tr, width), _fullspec((1, width))], out_specs=_rowspec(tr, width),
        out_shape=jax.ShapeDtypeStruct((rows, width), BF16), compiler_params=_params(("parallel",)),
    )(x, g)


def _rms_bwd(x, g, dy, res, name):
    rows, width = x.shape
    tr = _row_tile(rows)
    has_res = res is not None

    def body(*refs):
        if has_res:
            x_ref, g_ref, dy_ref, res_ref, dx_ref, dg_ref = refs
        else:
            x_ref, g_ref, dy_ref, dx_ref, dg_ref = refs
        dx, dgv = _rms_bwd_rows(x_ref[...], g_ref[...], dy_ref[...], width)
        if has_res:
            dx = dx + res_ref[...]
        dx_ref[...] = dx
        _acc_rows(dg_ref, dgv, pl.program_id(0) == 0)

    ins = [x, g, dy] + ([res] if has_res else [])
    specs = [_rowspec(tr, width), _fullspec((1, width)), _rowspec(tr, width)] + ([_rowspec(tr, width)] if has_res else [])
    return pl.pallas_call(
        body, name=name, grid=(rows // tr,), in_specs=specs,
        out_specs=[_rowspec(tr, width), _fullspec((1, width))],
        out_shape=[jax.ShapeDtypeStruct((rows, width), F32), jax.ShapeDtypeStruct((1, width), F32)],
        compiler_params=_params(("arbitrary",)),
    )(*ins)


_GELU_C = math.sqrt(2.0 / math.pi)


def _gelu(x):
    t = jnp.tanh(_GELU_C * (x + 0.044715 * (x * x * x)))
    return 0.5 * x * (1.0 + t), t


def _gelu_grad(x, t):
    return 0.5 * (1.0 + t) + 0.5 * x * (1.0 - t * t) * (_GELU_C * (1.0 + 3.0 * 0.044715 * (x * x)))


def _gm_forward_rows(zu, zv, gln, bln, wc_ref, bst, n_chunk):
    u, tu = _gelu(zu)
    a, ta = _gelu(zv)
    mu = jnp.mean(a, axis=-1, keepdims=True)
    ac = a - mu
    rs = lax.rsqrt(jnp.mean(ac * ac, axis=-1, keepdims=True) + EPS)
    n = ac * rs
    v = n * gln + bln
    vb = v.astype(BF16)
    rows = []
    for c in range(n_chunk):
        cols = []
        for g in range(GM_GROUPS):
            vc = vb[c * GM_CHUNK:(c + 1) * GM_CHUNK, g * LANES:(g + 1) * LANES]
            mixed = jnp.dot(wc_ref[g], vc, preferred_element_type=F32) + bst[g]
            cols.append(mixed)
        rows.append(jnp.concatenate(cols, axis=1))
    mixed = jnp.concatenate(rows, axis=0) if n_chunk > 1 else rows[0]
    return u, tu, ta, n, rs, v, mixed


def _gm_fwd(z, gln, bln, wc, bst, name):
    rows = z.shape[0]
    tr = _pick(rows, 512, GM_CHUNK)
    n_chunk = tr // GM_CHUNK

    def body(zu_ref, zv_ref, gln_ref, bln_ref, wc_ref, bst_ref, o_ref):
        u, _, _, _, _, _, mixed = _gm_forward_rows(zu_ref[...].astype(F32), zv_ref[...].astype(F32), gln_ref[...], bln_ref[...], wc_ref,
                                                   bst_ref, n_chunk)
        o_ref[...] = (u * mixed).astype(BF16)

    return pl.pallas_call(
        body, name=name, grid=(rows // tr,),
        in_specs=[_rowspec(tr, GM_WIDTH, Z_GM // GM_WIDTH), _rowspec(tr, GM_WIDTH, Z_GM // GM_WIDTH + 1),_fullspec((1, GM_WIDTH)), _fullspec((1, GM_WIDTH)),
                  _fullspec((GM_GROUPS, GM_CHUNK, GM_CHUNK)), _fullspec((GM_GROUPS, GM_CHUNK, LANES))],
        out_specs=_rowspec(tr, GM_WIDTH), out_shape=jax.ShapeDtypeStruct((rows, GM_WIDTH), BF16),
        compiler_params=_params(("parallel",)),
    )(z, z, gln, bln, wc, bst)


ANY_SPEC = pl.BlockSpec(memory_space=pl.ANY)


def _gm_bwd(z, dy, gln, bln, wc, wct, bst, dz, name):
    rows = z.shape[0]
    tr = _pick(rows, 512, GM_CHUNK)
    n_chunk = tr // GM_CHUNK

    def body(zu_ref, zv_ref, dy_ref, gln_ref, bln_ref, wc_ref, wct_ref, bst_ref, _, dz_ref, dws_ref, dbs_ref, dgl_ref,
             dbl_ref):
        first = pl.program_id(0) == 0
        zu, zv, gln = zu_ref[...].astype(F32), zv_ref[...].astype(F32), gln_ref[...]
        u, tu, ta, n, rs, v, mixed = _gm_forward_rows(zu, zv, gln, bln_ref[...], wc_ref, bst_ref, n_chunk)
        dyv = dy_ref[...].astype(F32)
        dzu = dyv * mixed * _gelu_grad(zu, tu)
        dmix = dyv * u
        dmb = dmix.astype(BF16)
        vb = v.astype(BF16)
        dv_rows, dws, dbs = [], [None] * GM_GROUPS, None
        for c in range(n_chunk):
            rsl = slice(c * GM_CHUNK, (c + 1) * GM_CHUNK)
            cols = []
            for g in range(GM_GROUPS):
                csl = slice(g * LANES, (g + 1) * LANES)
                dmc = dmb[rsl, csl]
                cols.append(jnp.dot(wct_ref[g], dmc, preferred_element_type=F32))
                w_part = lax.dot_general(dmc, vb[rsl, csl], (((1,), (1,)), ((), ())), preferred_element_type=F32)
                dws[g] = w_part if dws[g] is None else dws[g] + w_part
            dv_rows.append(jnp.concatenate(cols, axis=1))
            dbs = dmix[rsl, :] if dbs is None else dbs + dmix[rsl, :]
        dv = jnp.concatenate(dv_rows, axis=0) if n_chunk > 1 else dv_rows[0]
        dn = dv * gln
        da = rs * (dn - jnp.mean(dn, axis=-1, keepdims=True) - n * jnp.mean(dn * n, axis=-1, keepdims=True))
        dzv = da * _gelu_grad(zv, ta)
        dz_ref[:, 0:GM_WIDTH] = dzu.astype(BF16)
        dz_ref[:, GM_WIDTH:2 * GM_WIDTH] = dzv.astype(BF16)
        _acc_rows(dgl_ref, dv * n, first)
        _acc_rows(dbl_ref, dv, first)

        @pl.when(first)
        def _():
            for g in range(GM_GROUPS):
                dws_ref[g] = dws[g]
            dbs_ref[...] = dbs

        @pl.when(jnp.logical_not(first))
        def _():
            for g in range(GM_GROUPS):
                dws_ref[g] += dws[g]
            dbs_ref[...] += dbs

    wspec = _fullspec((GM_GROUPS, GM_CHUNK, GM_CHUNK))
    return pl.pallas_call(
        body, name=name, grid=(rows // tr,),
        in_specs=[_rowspec(tr, GM_WIDTH, Z_GM // GM_WIDTH), _rowspec(tr, GM_WIDTH, Z_GM // GM_WIDTH + 1),
                  _rowspec(tr, GM_WIDTH), _fullspec((1, GM_WIDTH)), _fullspec((1, GM_WIDTH)), wspec, wspec, wspec, ANY_SPEC],
        out_specs=[_rowspec(tr, 2 * GM_WIDTH, Z_GM // (2 * GM_WIDTH)), wspec, _fullspec((GM_CHUNK, GM_WIDTH)),
                   _fullspec((1, GM_WIDTH)), _fullspec((1, GM_WIDTH))],
        out_shape=[jax.ShapeDtypeStruct(dz.shape, dz.dtype), jax.ShapeDtypeStruct((GM_GROUPS, GM_CHUNK, GM_CHUNK), F32),
                   jax.ShapeDtypeStruct((GM_CHUNK, GM_WIDTH), F32), jax.ShapeDtypeStruct((1, GM_WIDTH), F32),
                   jax.ShapeDtypeStruct((1, GM_WIDTH), F32)],
        input_output_aliases={8: 0}, compiler_params=_params(("arbitrary",)),
    )(z, z, dy, gln, bln, wc, wct, bst, dz)


def _lat_fwd(z, g_cq, g_ckv, name):
    rows = z.shape[0]
    tr = _row_tile(rows)

    def body(z_ref, gq_ref, gkv_ref, nq_ref, nkv_ref):
        zb = z_ref[...]
        cq, ckv = zb[:, 0:Q_LORA], zb[:, Q_LORA:Q_LORA + KV_LORA]
        nq_ref[...] = (cq * _rms(cq, Q_LORA) * gq_ref[...]).astype(BF16)
        nkv_ref[...] = (ckv * _rms(ckv, KV_LORA) * gkv_ref[...]).astype(BF16)

    return pl.pallas_call(
        body, name=name, grid=(rows // tr,),
        in_specs=[_rowspec(tr, MLA_W, Z_MLA // MLA_W), _fullspec((1, Q_LORA)), _fullspec((1, KV_LORA))],
        out_specs=[_rowspec(tr, Q_LORA), _rowspec(tr, KV_LORA)],
        out_shape=[jax.ShapeDtypeStruct((rows, Q_LORA), BF16), jax.ShapeDtypeStruct((rows, KV_LORA), BF16)],
        compiler_params=_params(("parallel",)),
    )(z, g_cq, g_ckv)


def _lat_bwd(z, dnq, dnkv, dkpe, g_cq, g_ckv, dz, name):
    rows = z.shape[0]
    tr = _row_tile(rows)

    def body(z_ref, dnq_ref, dnkv_ref, dkpe_ref, gq_ref, gkv_ref, _, dz_ref, dgq_ref, dgkv_ref):
        first = pl.program_id(0) == 0
        zb = z_ref[...]
        dcq, dgq = _rms_bwd_rows(zb[:, 0:Q_LORA], gq_ref[...], dnq_ref[...], Q_LORA)
        dckv, dgkv = _rms_bwd_rows(zb[:, Q_LORA:Q_LORA + KV_LORA], gkv_ref[...], dnkv_ref[...], KV_LORA)
        dz_ref[:, 0:Q_LORA] = dcq.astype(BF16)
        dz_ref[:, Q_LORA:Q_LORA + KV_LORA] = dckv.astype(BF16)
        dz_ref[:, Q_LORA + KV_LORA:MLA_W] = dkpe_ref[...].astype(BF16)
        _acc_rows(dgq_ref, dgq, first)
        _acc_rows(dgkv_ref, dgkv, first)

    return pl.pallas_call(
        body, name=name, grid=(rows // tr,),
        in_specs=[_rowspec(tr, MLA_W, Z_MLA // MLA_W), _rowspec(tr, Q_LORA), _rowspec(tr, KV_LORA), _rowspec(tr, LANES),
                  _fullspec((1, Q_LORA)), _fullspec((1, KV_LORA)), ANY_SPEC],
        out_specs=[_rowspec(tr, MLA_W, Z_MLA // MLA_W), _fullspec((1, Q_LORA)), _fullspec((1, KV_LORA))],
        out_shape=[jax.ShapeDtypeStruct(dz.shape, dz.dtype), jax.ShapeDtypeStruct((1, Q_LORA), F32),
                   jax.ShapeDtypeStruct((1, KV_LORA), F32)],
        input_output_aliases={6: 0}, compiler_params=_params(("arbitrary",)),
    )(z, dnq, dnkv, dkpe, g_cq, g_ckv, dz)


def _rope(y, cc, ss):
    return y * cc + pltpu.roll(y, 64, 1) * ss


def _rope_bwd(d, cc, ss):
    return d * cc + pltpu.roll(d * ss, 64, 1)


def _qk_fwd(q, kv, z, cc, ss, gqn, gqp, gkn, gkp, name):
    rows = q.shape[0]
    tr = _row_tile(rows)
    W = MLA_HEADS * HEAD
    QS = MLA_SCALE * LOG2E

    def body(q_ref, kv_ref, kpe_ref, cc_ref, ss_ref, gqn_ref, gqp_ref, gkn_ref, gkp_ref, qc_ref, kc_ref, v_ref):
        cc, ss = cc_ref[...], ss_ref[...]
        kpe = kpe_ref[...]
        kp = _rope(kpe * _rms(kpe, MLA_ROPE) * gkp_ref[...], cc, ss).astype(BF16)
        for h in range(MLA_HEADS):
            qn = q_ref[:, h * HEAD:(h + 1) * HEAD]
            qp = q_ref[:, W + h * HEAD:W + (h + 1) * HEAD]
            kn = kv_ref[:, h * HEAD:(h + 1) * HEAD]
            qc_ref[:, h * QCAT:h * QCAT + HEAD] = (qn * _rms(qn, HEAD) * gqn_ref[...] * QS).astype(BF16)
            qc_ref[:, h * QCAT + HEAD:(h + 1) * QCAT] = (_rope(qp * _rms(qp, MLA_ROPE) * gqp_ref[...], cc, ss) * QS).astype(BF16)
            kc_ref[:, h * QCAT:h * QCAT + HEAD] = (kn * _rms(kn, HEAD) * gkn_ref[...]).astype(BF16)
            kc_ref[:, h * QCAT + HEAD:(h + 1) * QCAT] = kp
        v_ref[...] = kv_ref[:, W:2 * W].astype(BF16)

    g = _fullspec((1, HEAD))
    return pl.pallas_call(
        body, name=name, grid=(rows // tr,),
        in_specs=[_rowspec(tr, 2 * W), _rowspec(tr, 2 * W), _rowspec(tr, LANES, Z_KPE // LANES), _rowspec(tr, LANES),
                  _rowspec(tr, LANES), g, g, g, g],
        out_specs=[_rowspec(tr, MLA_HEADS * QCAT), _rowspec(tr, MLA_HEADS * QCAT), _rowspec(tr, W)],
        out_shape=[jax.ShapeDtypeStruct((rows, MLA_HEADS * QCAT), BF16), jax.ShapeDtypeStruct((rows, MLA_HEADS * QCAT), BF16),
                   jax.ShapeDtypeStruct((rows, W), BF16)],
        compiler_params=_params(("parallel",)),
    )(q, kv, z, cc, ss, gqn, gqp, gkn, gkp)


def _qk_bwd(q, kv, z, cc, ss, gqn, gqp, gkn, gkp, dqc, dkc, dv, name):
    rows = q.shape[0]
    tr = _row_tile(rows)
    W = MLA_HEADS * HEAD

    def body(q_ref, kv_ref, kpe_ref, cc_ref, ss_ref, gqn_ref, gqp_ref, gkn_ref, gkp_ref, dqc_ref, dkc_ref, dv_ref,
             dq_ref, dkv_ref, dkpe_ref, dgqn_ref, dgqp_ref, dgkn_ref, dgkp_ref):
        first = pl.program_id(0) == 0
        cc, ss = cc_ref[...], ss_ref[...]
        sqn = sqp = skn = dkp = None
        for h in range(MLA_HEADS):
            dx, dg = _rms_bwd_rows(q_ref[:, h * HEAD:(h + 1) * HEAD], gqn_ref[...], dqc_ref[:, h * QCAT:h * QCAT + HEAD], HEAD)
            dq_ref[:, h * HEAD:(h + 1) * HEAD] = dx.astype(BF16)
            sqn = dg if sqn is None else sqn + dg
            dy = _rope_bwd(dqc_ref[:, h * QCAT + HEAD:(h + 1) * QCAT], cc, ss)
            dx, dg = _rms_bwd_rows(q_ref[:, W + h * HEAD:W + (h + 1) * HEAD], gqp_ref[...], dy, MLA_ROPE)
            dq_ref[:, W + h * HEAD:W + (h + 1) * HEAD] = dx.astype(BF16)
            sqp = dg if sqp is None else sqp + dg
            dx, dg = _rms_bwd_rows(kv_ref[:, h * HEAD:(h + 1) * HEAD], gkn_ref[...], dkc_ref[:, h * QCAT:h * QCAT + HEAD], HEAD)
            dkv_ref[:, h * HEAD:(h + 1) * HEAD] = dx.astype(BF16)
            skn = dg if skn is None else skn + dg
            part = dkc_ref[:, h * QCAT + HEAD:(h + 1) * QCAT].astype(F32)
            dkp = part if dkp is None else dkp + part
        dkv_ref[:, W:2 * W] = dv_ref[...].astype(BF16)
        dx, dg = _rms_bwd_rows(kpe_ref[...], gkp_ref[...], _rope_bwd(dkp, cc, ss), MLA_ROPE)
        dkpe_ref[...] = dx
        _acc_rows(dgqn_ref, sqn, first)
        _acc_rows(dgqp_ref, sqp, first)
        _acc_rows(dgkn_ref, skn, first)
        _acc_rows(dgkp_ref, dg, first)

    g = _fullspec((1, HEAD))
    gs = jax.ShapeDtypeStruct((1, HEAD), F32)
    return pl.pallas_call(
        body, name=name, grid=(rows // tr,),
        in_specs=[_rowspec(tr, 2 * W), _rowspec(tr, 2 * W), _rowspec(tr, LANES, Z_KPE // LANES), _rowspec(tr, LANES),
                  _rowspec(tr, LANES), g, g, g, g, _rowspec(tr, MLA_HEADS * QCAT), _rowspec(tr, MLA_HEADS * QCAT),
                  _rowspec(tr, W)],
        out_specs=[_rowspec(tr, 2 * W), _rowspec(tr, 2 * W), _rowspec(tr, LANES), g, g, g, g],
        out_shape=[jax.ShapeDtypeStruct((rows, 2 * W), BF16), jax.ShapeDtypeStruct((rows, 2 * W), BF16),
                   jax.ShapeDtypeStruct((rows, LANES), F32), gs, gs, gs, gs],
        compiler_params=_params(("arbitrary",)),
    )(q, kv, z, cc, ss, gqn, gqp, gkn, gkp, dqc, dkc, dv)


def _headnorm_fwd(x, col, nheads, g, out_scale, name):
    rows = x.shape[0]
    tr = _row_tile(rows)
    W = nheads * HEAD

    def body(x_ref, g_ref, o_ref):
        for h in range(nheads):
            xv = x_ref[:, h * HEAD:(h + 1) * HEAD]
            o_ref[:, h * HEAD:(h + 1) * HEAD] = (xv * _rms(xv, HEAD) * g_ref[...] * out_scale).astype(BF16)

    return pl.pallas_call(
        body, name=name, grid=(rows // tr,),
        in_specs=[_rowspec(tr, W, col), _fullspec((1, HEAD))], out_specs=_rowspec(tr, W),
        out_shape=jax.ShapeDtypeStruct((rows, W), BF16), compiler_params=_params(("parallel",)),
    )(x, g)


def _headnorm_bwd(x, col, nheads, g, dy, tail, name, into=None):
    rows = x.shape[0]
    tr = _row_tile(rows)
    W = nheads * HEAD
    has_tail = tail is not None
    WO = 2 * W if has_tail else W

    def body(*refs):
        if into is not None:
            x_ref, g_ref, dy_ref, _, dx_ref, dg_ref = refs
        elif has_tail:
            x_ref, g_ref, dy_ref, t_ref, dx_ref, dg_ref = refs
        else:
            x_ref, g_ref, dy_ref, dx_ref, dg_ref = refs
        acc = None
        for h in range(nheads):
            sl = slice(h * HEAD, (h + 1) * HEAD)
            dx, dg = _rms_bwd_rows(x_ref[:, sl], g_ref[...], dy_ref[:, sl], HEAD)
            dx_ref[:, sl] = dx.astype(BF16)
            acc = dg if acc is None else acc + dg
        if has_tail:
            dx_ref[:, W:2 * W] = t_ref[...].astype(BF16)
        _acc_rows(dg_ref, acc, pl.program_id(0) == 0)

    ins = [x, g, dy] + ([tail] if has_tail else [])
    specs = [_rowspec(tr, W, col), _fullspec((1, HEAD)), _rowspec(tr, W)] + ([_rowspec(tr, W)] if has_tail else [])
    dx_spec, dx_shape, aliases = _rowspec(tr, WO), jax.ShapeDtypeStruct((rows, WO), BF16), {}
    if into is not None:
        assert not has_tail
        ins, specs = ins + [into[0]], specs + [ANY_SPEC]
        dx_spec, dx_shape, aliases = _rowspec(tr, W, into[1]), jax.ShapeDtypeStruct(into[0].shape, into[0].dtype), {3: 0}
    return pl.pallas_call(
        body, name=name, grid=(rows // tr,), in_specs=specs,
        out_specs=[dx_spec, _fullspec((1, HEAD))], out_shape=[dx_shape, jax.ShapeDtypeStruct((1, HEAD), F32)],
        input_output_aliases=aliases, compiler_params=_params(("arbitrary",)),
    )(*ins)


def _sigmoid(x):
    return 1.0 / (1.0 + jnp.exp(-x.astype(F32)))


def _merge_fwd(z, y_gm, y_mla, y_mem, name):
    rows = z.shape[0]
    tr = _row_tile(rows)

    def body(g0_ref, g1_ref, g2_ref, a_ref, b_ref, c_ref, o_ref):
        m = _sigmoid(g0_ref[...]) * a_ref[...] + _sigmoid(g1_ref[...]) * b_ref[...] + _sigmoid(g2_ref[...]) * c_ref[...]
        o_ref[...] = m.astype(BF16)

    r = _rowspec(tr, D_MODEL)
    return pl.pallas_call(
        body, name=name, grid=(rows // tr,),
        in_specs=[_rowspec(tr, D_MODEL, 0), _rowspec(tr, D_MODEL, 1), _rowspec(tr, D_MODEL, 2),r, r, r],
        out_specs=r, out_shape=jax.ShapeDtypeStruct((rows, D_MODEL), BF16), compiler_params=_params(("parallel",)),
    )(z, z, z, y_gm, y_mla, y_mem)


def _merge_bwd(z, y_gm, y_mla, y_mem, dm, name):
    rows = z.shape[0]
    tr = _row_tile(rows)

    def body(g0_ref, g1_ref, g2_ref, a_ref, b_ref, c_ref, dm_ref, da_ref, db_ref, dc_ref, dzg_ref):
        dmv = dm_ref[...].astype(F32)
        for k, (g_ref, y_ref, dy_ref) in enumerate(((g0_ref, a_ref, da_ref), (g1_ref, b_ref, db_ref), (g2_ref, c_ref, dc_ref))):
            s = _sigmoid(g_ref[...])
            dy_ref[...] = (dmv * s).astype(BF16)
            dzg_ref[:, k * D_MODEL:(k + 1) * D_MODEL] = (dmv * y_ref[...] * s * (1.0 - s)).astype(BF16)

    r = _rowspec(tr, D_MODEL)
    o = jax.ShapeDtypeStruct((rows, D_MODEL), BF16)
    return pl.pallas_call(
        body, name=name, grid=(rows // tr,),
        in_specs=[_rowspec(tr, D_MODEL, 0), _rowspec(tr, D_MODEL, 1), _rowspec(tr, D_MODEL, 2),r, r, r, r],
        out_specs=[r, r, r, _rowspec(tr, 3 * D_MODEL, 0)],
        out_shape=[o, o, o, jax.ShapeDtypeStruct((rows, Z_COLS), BF16)],
        compiler_params=_params(("parallel",)),
    )(z, z, z, y_gm, y_mla, y_mem, dm)


def _loss_head(y, target, name):
    rows, width = y.shape
    tr = _row_tile(rows)

    def body(y_ref, t_ref, dy_ref, l_ref):
        e = y_ref[...] - t_ref[...]
        dy_ref[...] = e * (1.0 / width)
        e2 = e * e
        part = e2[:, 0:LANES]
        for k in range(1, width // LANES):
            part = part + e2[:, k * LANES:(k + 1) * LANES]
        _acc_rows(l_ref, part, pl.program_id(0) == 0)

    return pl.pallas_call(
        body, name=name, grid=(rows // tr,),
        in_specs=[_rowspec(tr, width), _rowspec(tr, width)],
        out_specs=[_rowspec(tr, width), _fullspec((1, LANES))],
        out_shape=[jax.ShapeDtypeStruct((rows, width), F32), jax.ShapeDtypeStruct((1, LANES), F32)],
        compiler_params=_params(("arbitrary",)),
    )(y, target)


_NT = (((1,), (1,)), ((), ()))
_TN = (((0,), (0,)), ((), ()))


def _diag_mask(s):
    row = lax.broadcasted_iota(jnp.int32, s.shape, 0)
    col = lax.broadcasted_iota(jnp.int32, s.shape, 1)
    return jnp.where(row >= col, s, NEG)


def _attn_fwd(q, k, v, nb, nheads, dk, v_col0, causal, name, rider=None):
    S, Skv = q.shape[0] // nb, k.shape[0] // nb
    tq = _pick(Skv, ATT_TILE) if causal else _pick(S, 4 * ATT_TILE)
    nq = S // tq

    def body(q_ref, k_ref, v_ref, o_ref, lse_ref):
        for i in range(nq):
            r0 = i * tq
            qb = q_ref[r0:r0 + tq, :]
            if causal:
                spans = ([(0, r0, False)] if i > 0 else []) + [(r0, r0 + tq, True)]
            else:
                spans = [(0, Skv, False)]
            scores = []
            for a, b, masked in spans:
                s = lax.dot_general(qb, k_ref[a:b, :], _NT, preferred_element_type=F32)
                scores.append(_diag_mask(s) if masked else s)
            m = functools.reduce(jnp.maximum, [jnp.max(s, axis=-1, keepdims=True) for s in scores])
            l = acc = None
            for s, (a, b, _) in zip(scores, spans):
                p = jnp.exp2(s - m)
                lp = jnp.sum(p, axis=-1, keepdims=True)
                ap = jnp.dot(p.astype(BF16), v_ref[a:b, :].astype(BF16), preferred_element_type=F32)
                l, acc = (lp, ap) if l is None else (l + lp, acc + ap)
            o_ref[r0:r0 + tq, :] = (acc / l).astype(BF16)
            lse_ref[r0:r0 + tq, :] = m + jnp.log2(l)

    ins = [q, k, v]
    in_specs = [pl.BlockSpec((S, dk), lambda b, h: (b, h)), pl.BlockSpec((Skv, dk), lambda b, h: (b, h)),
                pl.BlockSpec((Skv, HEAD), lambda b, h: (b, v_col0 + h))]
    out_specs = [pl.BlockSpec((S, HEAD), lambda b, h: (b, h)), pl.BlockSpec((None, S, 1), lambda b, h: (h, b, 0))]
    out_shape = [jax.ShapeDtypeStruct((nb * S, nheads * HEAD), BF16), jax.ShapeDtypeStruct((nheads, nb * S, 1), F32)]
    return _call(body, rider, ins, name=name, grid=(nb, nheads), in_specs=in_specs, out_specs=out_specs,
                 out_shape=out_shape, scratch_shapes=[], sem=("parallel", "parallel"))


def _attn_bwd(q, k, v, o, do, lse, nb, nheads, dk, v_col0, scale, causal, name, rider=None):
    S, Skv = q.shape[0] // nb, k.shape[0] // nb
    tk = _pick(Skv, ATT_TILE)
    nkv = Skv // tk

    def body(q_ref, k_ref, v_ref, o_ref, do_ref, lse_ref, dq_ref, dk_ref, dv_ref, delta_ref, dob_ref, dqa_ref):
        dov = do_ref[...]
        delta_ref[...] = jnp.sum(o_ref[...].astype(F32) * dov.astype(F32), axis=-1, keepdims=True)
        dob_ref[...] = dov.astype(BF16)

        for j in range(nkv):
            c0 = j * tk
            kb = k_ref[c0:c0 + tk, :]
            vb = v_ref[c0:c0 + tk, :].astype(BF16)
            if causal:
                spans = [(c0, c0 + tk, True)] + ([(c0 + tk, S, False)] if c0 + tk < S else [])
            else:
                spans = [(0, S, False)]
            dk_acc = dv_acc = None
            for a, b, masked in spans:
                qb = q_ref[a:b, :]
                dob = dob_ref[a:b, :]
                s = lax.dot_general(qb, kb, _NT, preferred_element_type=F32)
                if masked:
                    s = _diag_mask(s)
                p = jnp.exp2(s - lse_ref[a:b, :])
                dp = lax.dot_general(dob, vb, _NT, preferred_element_type=F32)
                ds = (p * (dp - delta_ref[a:b, :])).astype(BF16)
                dv_p = lax.dot_general(p.astype(BF16), dob, _TN, preferred_element_type=F32)
                dk_p = lax.dot_general(ds, qb, _TN, preferred_element_type=F32)
                dk_acc, dv_acc = (dk_p, dv_p) if dk_acc is None else (dk_acc + dk_p, dv_acc + dv_p)
                dq_p = jnp.dot(ds, kb, preferred_element_type=F32) * scale
                if j == 0:
                    dqa_ref[a:b, :] = dq_p
                else:
                    dqa_ref[a:b, :] += dq_p
            dk_ref[c0:c0 + tk, :] = (dk_acc * LN2).astype(BF16)
            dv_ref[c0:c0 + tk, :] = dv_acc.astype(BF16)
        dq_ref[...] = dqa_ref[...].astype(BF16)

    ins = [q, k, v, o, do, lse]
    in_specs = [pl.BlockSpec((S, dk), lambda b, h: (b, h)), pl.BlockSpec((Skv, dk), lambda b, h: (b, h)),
                pl.BlockSpec((Skv, HEAD), lambda b, h: (b, v_col0 + h)), pl.BlockSpec((S, HEAD), lambda b, h: (b, h)),
                pl.BlockSpec((S, HEAD), lambda b, h: (b, h)), pl.BlockSpec((None, S, 1), lambda b, h: (h, b, 0))]
    out_specs = [pl.BlockSpec((S, dk), lambda b, h: (b, h)), pl.BlockSpec((Skv, dk), lambda b, h: (b, h)),
                 pl.BlockSpec((Skv, HEAD), lambda b, h: (b, h))]
    out_shape = [jax.ShapeDtypeStruct((nb * S, nheads * dk), BF16), jax.ShapeDtypeStruct((nb * Skv, nheads * dk), BF16),
                 jax.ShapeDtypeStruct((nb * Skv, nheads * HEAD), BF16)]
    return _call(body, rider, ins, name=name, grid=(nb, nheads), in_specs=in_specs, out_specs=out_specs,
                 out_shape=out_shape,
                 scratch_shapes=[pltpu.VMEM((S, 1), F32), pltpu.VMEM((S, HEAD), BF16), pltpu.VMEM((S, dk), F32)],
                 sem=("parallel", "parallel"))


def _spread_rope(a):
    zero = jnp.zeros(a.shape[:-1] + (32,), a.dtype)
    return jnp.concatenate([a[..., :32], zero, a[..., 32:], zero], axis=-1)


def _gather_rope(a):
    return jnp.concatenate([a[..., 0:32], a[..., 64:96]], axis=-1)


def _win_layout(w):
    return jnp.concatenate([w[:, C_ZG:C_END], w[:, C_ZU:C_CQ], w[:, C_QM:C_ZG], w[:, C_CQ:C_CKV], w[:, C_CKV:C_KPE],
                            _spread_rope(w[:, C_KPE:C_QM])], axis=1)


def _win_unlayout(d):
    return jnp.concatenate([d[:, Z_GM:Z_QM], d[:, Z_MLA:Z_MLA + Q_LORA], d[:, Z_MLA + Q_LORA:Z_KPE],
                            _gather_rope(d[:, Z_KPE:Z_COLS]), d[:, Z_QM:Z_MLA], d[:, 0:Z_GM]], axis=1)


def _wuq_layout(w):
    r = w.reshape(Q_LORA, MLA_HEADS, HEAD + MLA_ROPE)
    return jnp.concatenate([r[:, :, :HEAD].reshape(Q_LORA, -1), _spread_rope(r[:, :, HEAD:]).reshape(Q_LORA, -1)], axis=1)


def _wuq_unlayout(d):
    n = d[:, :MLA_HEADS * HEAD].reshape(Q_LORA, MLA_HEADS, HEAD)
    p = _gather_rope(d[:, MLA_HEADS * HEAD:].reshape(Q_LORA, MLA_HEADS, HEAD))
    return jnp.concatenate([n, p], axis=-1).reshape(Q_LORA, -1)


def _wukv_layout(w):
    r = w.reshape(KV_LORA, MLA_HEADS, 2 * HEAD)
    return jnp.concatenate([r[:, :, :HEAD].reshape(KV_LORA, -1), r[:, :, HEAD:].reshape(KV_LORA, -1)], axis=1)


def _wukv_unlayout(d):
    k = d[:, :MLA_HEADS * HEAD].reshape(KV_LORA, MLA_HEADS, HEAD)
    v = d[:, MLA_HEADS * HEAD:].reshape(KV_LORA, MLA_HEADS, HEAD)
    return jnp.concatenate([k, v], axis=-1).reshape(KV_LORA, -1)


AG_MID = ['w_uq', 'w_ukv', 'w_mem_kv', 'w_o_gm', 'w_o_mla', 'w_o_mem', 'w_out']
AG_FFN = ['w_ff1', 'w_ff2']
RS_GROUPS = {'ff2': ['w_ff2'], 'ff1': ['w_ff1'], 'proj': ['w_out', 'w_o_gm', 'w_o_mla', 'w_o_mem'],
             'lat': ['w_uq', 'w_ukv', 'w_mem_kv'], 'in': ['w_in']}


def _unride(res, rider):
    return (res, None) if rider is None else res


def _local_step(x, mem, positions, target, P, ws):
    B, S, _ = x.shape
    M = mem.shape[1]
    T = B * S
    x2d = x.reshape(T, D_MODEL)
    mem2d = mem.reshape(B * M, D_MODEL)
    tgt2d = target.reshape(T, D_MODEL)

    def row(v):
        return v.reshape(1, -1).astype(F32)

    inv_freq = ROPE_BASE ** (-jnp.arange(0, MLA_ROPE, 2, dtype=F32) / MLA_ROPE)
    ang = positions.reshape(T).astype(F32)[:, None] * inv_freq
    cos, sin, zero = jnp.cos(ang), jnp.sin(ang), jnp.zeros_like(ang)
    cc = jnp.concatenate([cos, zero, cos, zero], axis=1)
    ss = jnp.concatenate([-sin, zero, sin, zero], axis=1)

    w_in = _win_layout(ws.first()).astype(BF16)
    g_mix, g_cq, g_ckv, g_ffn, g_mem = row(P['g_mix']), row(P['g_cq']), row(P['g_ckv']), row(P['g_ffn']), row(P['g_mem'])
    gqn, gkn, gmq, gmk = row(P['g_q_nope']), row(P['g_k_nope']), row(P['g_mq']), row(P['g_mk'])
    gqp, gkp = _spread_rope(row(P['g_q_pe'])), _spread_rope(row(P['g_k_pe']))
    gln, bln = row(P['g_gm_ln']), row(P['b_gm_ln'])
    wc = jnp.tril(P['w_spatial'].astype(F32))
    wct = jnp.swapaxes(wc, 1, 2).astype(BF16)
    wc = wc.astype(BF16)
    bst = jnp.broadcast_to(P['b_spatial'].astype(F32)[:, :, None], (GM_GROUPS, GM_CHUNK, LANES))

    h = _rms_fwd(x2d, g_mix, "rms_mix")
    ride = ws.gather(AG_MID)
    z, got = _unride(_matmul(h, w_in, 'nn', ACT, "mm_in", tn_t=768, rider=ride), ride)
    mid = ws.gathered(AG_MID, got)
    w_uq = _wuq_layout(mid['w_uq']).astype(BF16)
    w_ukv = _wukv_layout(mid['w_ukv']).astype(BF16)
    w_mem_kv, w_o_gm, w_o_mla, w_o_mem, w_out = (mid[n].astype(BF16) for n in ('w_mem_kv', 'w_o_gm', 'w_o_mla', 'w_o_mem',
                                                                                 'w_out'))
    ygm_pre = _gm_fwd(z, gln, bln, wc, bst, "gm_fwd")
    y_gm = _matmul(ygm_pre, w_o_gm, 'nn', ACT, "mm_o_gm")
    nq, nkv = _lat_fwd(z, g_cq, g_ckv, "lat_fwd")
    q = _matmul(nq, w_uq, 'nn', ACT, "mm_uq")
    kv = _matmul(nkv, w_ukv, 'nn', ACT, "mm_ukv")
    qcat, kcat, vv = _qk_fwd(q, kv, z, cc, ss, gqn, gqp, gkn, gkp, "qk_fwd")
    ride = ws.gather(AG_FFN)
    (o, lse), got = _unride(_attn_fwd(qcat, kcat, vv, B, MLA_HEADS, QCAT, 0, True, "mla_attn_fwd", rider=ride), ride)
    ffn = ws.gathered(AG_FFN, got)
    w_ff1, w_ff2 = ffn['w_ff1'].astype(BF16), ffn['w_ff2'].astype(BF16)
    y_mla = _matmul(o, w_o_mla, 'nn', ACT, "mm_o_mla")
    nm = _rms_fwd(mem2d, g_mem, "rms_mem")
    kvm = _matmul(nm, w_mem_kv, 'nn', ACT, "mm_mem_kv")
    qm = _headnorm_fwd(z, Z_QM // (MEM_HEADS * HEAD), MEM_HEADS, gmq, MEM_SCALE * LOG2E, "memq_fwd")
    km = _headnorm_fwd(kvm, 0, MEM_HEADS, gmk, 1.0, "memk_fwd")
    om, lse_m = _attn_fwd(qm, km, kvm, B, MEM_HEADS, HEAD, MEM_HEADS, False, "mem_attn_fwd")
    y_mem = _matmul(om, w_o_mem, 'nn', ACT, "mm_o_mem")
    merged = _merge_fwd(z, y_gm, y_mla, y_mem, "merge_fwd")
    x1 = _matmul(merged, w_out, 'nn', F32, "mm_out", add=x2d)
    h2 = _rms_fwd(x1, g_ffn, "rms_ffn")
    a1 = _matmul(h2, w_ff1, 'nn', BF16, "mm_ff1")
    x2 = _matmul(a1, w_ff2, 'nn', F32, "mm_ff2", add=x1, relu2_a=True)
    dx2, loss_part = _loss_head(x2, tgt2d, "loss_head")

    G = {}
    ride = ws.scatter('ff2', {'w_ff2': _matmul(a1, dx2, 'tn', BF16, "mm_d_ff2", relu2_a=True)})
    da1, got = _unride(_matmul(dx2, w_ff2, 'nt', BF16, "mm_da1", relu2_grad=a1, rider=ride), ride)
    ws.scattered('ff2', got)
    ride = ws.scatter('ff1', {'w_ff1': _matmul(h2, da1, 'tn', BF16, "mm_d_ff1")})
    dh2, got = _unride(_matmul(da1, w_ff1, 'nt', ACT, "mm_dh2", rider=ride), ride)
    ws.scattered('ff1', got)
    dx1, G['g_ffn'] = _rms_bwd(x1, g_ffn, dh2, dx2, "rms_ffn_bwd")
    d_out = _matmul(merged, dx1, 'tn', BF16, "mm_d_out")
    dmerged = _matmul(dx1, w_out, 'nt', ACT, "mm_dmerged")
    dy_gm, dy_mla, dy_mem, dz = _merge_bwd(z, y_gm, y_mla, y_mem, dmerged, "merge_bwd")
    d_o_gm = _matmul(ygm_pre, dy_gm, 'tn', BF16, "mm_d_o_gm")
    d_o_mla = _matmul(o, dy_mla, 'tn', BF16, "mm_d_o_mla")
    d_o_mem = _matmul(om, dy_mem, 'tn', BF16, "mm_d_o_mem")
    dygm_pre = _matmul(dy_gm, w_o_gm, 'nt', ACT, "mm_dygm")
    dz, dws, dbs, G['g_gm_ln'], G['b_gm_ln'] = _gm_bwd(z, dygm_pre, gln, bln, wc, wct, bst, dz, "gm_bwd")
    G['w_spatial'] = jnp.tril(dws)
    G['b_spatial'] = jnp.sum(dbs.reshape(GM_CHUNK, GM_GROUPS, LANES), axis=-1).T
    do = _matmul(dy_mla, w_o_mla, 'nt', ACT, "mm_do")
    ride = ws.scatter('proj', {'w_out': d_out, 'w_o_gm': d_o_gm, 'w_o_mla': d_o_mla, 'w_o_mem': d_o_mem})
    (dqc, dkc, dvv), got = _unride(_attn_bwd(qcat, kcat, vv, o, do, lse, B, MLA_HEADS, QCAT, 0, MLA_SCALE, True,
                                             "mla_attn_bwd", rider=ride), ride)
    ws.scattered('proj', got)
    dq, dkv, dkpe, G['g_q_nope'], dgqp, G['g_k_nope'], dgkp = _qk_bwd(q, kv, z, cc, ss, gqn, gqp, gkn, gkp, dqc, dkc, dvv,
                                                                     "qk_bwd")
    G['g_q_pe'], G['g_k_pe'] = _gather_rope(dgqp), _gather_rope(dgkp)
    d_uq = _wuq_unlayout(_matmul(nq, dq, 'tn', BF16, "mm_d_uq"))
    dnq = _matmul(dq, w_uq, 'nt', ACT, "mm_dnq")
    d_ukv = _wukv_unlayout(_matmul(nkv, dkv, 'tn', BF16, "mm_d_ukv"))
    dnkv = _matmul(dkv, w_ukv, 'nt', ACT, "mm_dnkv")
    dz, G['g_cq'], G['g_ckv'] = _lat_bwd(z, dnq, dnkv, dkpe, g_cq, g_ckv, dz, "lat_bwd")
    dom = _matmul(dy_mem, w_o_mem, 'nt', ACT, "mm_dom")
    dqm, dkm, dvm = _attn_bwd(qm, km, kvm, om, dom, lse_m, B, MEM_HEADS, HEAD, MEM_HEADS, MEM_SCALE, False, "mem_attn_bwd")
    dz, G['g_mq'] = _headnorm_bwd(z, Z_QM // (MEM_HEADS * HEAD), MEM_HEADS, gmq, dqm, None, "memq_bwd",
                                  into=(dz, Z_QM // (MEM_HEADS * HEAD)))
    dkvm, G['g_mk'] = _headnorm_bwd(kvm, 0, MEM_HEADS, gmk, dkm, dvm, "memk_bwd")
    d_mem_kv = _matmul(nm, dkvm, 'tn', BF16, "mm_d_mem_kv")
    dnm = _matmul(dkvm, w_mem_kv, 'nt', ACT, "mm_dnm")
    _, G['g_mem'] = _rms_bwd(mem2d, g_mem, dnm, None, "rms_mem_bwd")
    ride = ws.scatter('lat', {'w_uq': d_uq, 'w_ukv': d_ukv, 'w_mem_kv': d_mem_kv})
    d_in, got = _unride(_matmul(h, dz, 'tn', BF16, "mm_d_in", tn_t=768, rider=ride), ride)
    ws.scattered('lat', got)
    ride = ws.scatter('in', {'w_in': _win_unlayout(d_in)})
    dh, got = _unride(_matmul(dz, w_in, 'nt', ACT, "mm_dh", tk_t=768, rider=ride), ride)
    ws.scattered('in', got)
    gx, G['g_mix'] = _rms_bwd(x2d, g_mix, dh, dx1, "rms_mix_bwd")
    return loss_part, gx.reshape(B, S, D_MODEL), G


def _all_gather8(xs, name):
    def body(x_ref, out_ref, send_sems, recv_sems, local_sem):
        x, y, c = lax.axis_index("x"), lax.axis_index("y"), lax.axis_index("c")
        me, sibling = (x, y, c), (x, y, 1 - c)
        chips = [(1 - x, y), (x, 1 - y), (1 - x, 1 - y)]

        def rows(px, py, pc):
            return out_ref.at[4 * px + 2 * py + pc]

        def copy(k, block, to, src=None):
            return pltpu.make_async_remote_copy(
                src_ref=rows(*block) if src is None else src, dst_ref=rows(*block),
                send_sem=send_sems.at[k], recv_sem=recv_sems.at[k], device_id=to, device_id_type=MESH)

        mine = pltpu.make_async_copy(x_ref, rows(*me), local_sem)
        mine.start()
        first = [copy(0, me, sibling, src=x_ref)]
        first += [copy(1 + j, me, (*chip, c), src=x_ref) for j, chip in enumerate(chips)]
        for cp in first:
            cp.start()
        passed = [copy(4 + j, (*chip, c), sibling) for j, chip in enumerate(chips)]
        for j, chip in enumerate(chips):
            copy(1 + j, (*chip, c), me).wait_recv()
            passed[j].start()
        copy(0, sibling, me).wait_recv()
        for j, chip in enumerate(chips):
            copy(4 + j, (*chip, 1 - c), me).wait_recv()
        for cp in first + passed:
            cp.wait_send()
        mine.wait()

    return pl.pallas_call(
        body, name=name, in_specs=[HBM_SPEC], out_specs=HBM_SPEC,
        out_shape=jax.ShapeDtypeStruct((N_DEV,) + xs.shape, xs.dtype),
        scratch_shapes=[pltpu.SemaphoreType.DMA((7,)), pltpu.SemaphoreType.DMA((7,)), pltpu.SemaphoreType.DMA],
    )(xs)


def _adamw_rows(w, g, m, v):
    m2 = ADAM_B1 * m + (1.0 - ADAM_B1) * g
    v2 = ADAM_B2 * v + (1.0 - ADAM_B2) * (g * g)
    m_hat = m2 / (1.0 - ADAM_B1 ** ADAM_STEP)
    v_hat = v2 / (1.0 - ADAM_B2 ** ADAM_STEP)
    delta = -ADAM_LR * (m_hat / (jnp.sqrt(v_hat) + ADAM_EPS) + ADAM_WD * w)
    return delta, m2, v2


def _sum_adamw(parts, w, m, v, name):
    rows, cols = w.shape
    tr = _pick(rows, max(16, 65536 // cols), 16)
    n = parts.shape[0]

    def body(p_ref, w_ref, m_ref, v_ref, g_ref, d_ref, m2_ref, v2_ref):
        g = p_ref[0].astype(F32)
        for k in range(1, n):
            g = g + p_ref[k].astype(F32)
        delta, m2, v2 = _adamw_rows(w_ref[...], g, m_ref[...], v_ref[...])
        g_ref[...] = g
        d_ref[...] = delta
        m2_ref[...] = m2
        v2_ref[...] = v2

    flat = pl.BlockSpec((tr, cols), lambda i: (i, 0))
    out = jax.ShapeDtypeStruct((rows, cols), F32)
    return pl.pallas_call(
        body, name=name, grid=(rows // tr,),
        in_specs=[pl.BlockSpec((n, tr, cols), lambda i: (0, i, 0)), flat, flat, flat], out_specs=[flat] * 4,
        out_shape=[out] * 4, compiler_params=_params(("parallel",)),
    )(parts, w, m, v)


def _small_rows(name):
    n = {'g_mix': 1024, 'g_cq': 384, 'g_ckv': 256, 'g_q_nope': 128, 'g_q_pe': 64, 'g_k_nope': 128, 'g_k_pe': 64,
         'g_gm_ln': 512, 'b_gm_ln': 512, 'w_spatial': GM_GROUPS * GM_CHUNK * GM_CHUNK, 'b_spatial': GM_GROUPS * GM_CHUNK,
         'g_mem': 1024, 'g_mq': 128, 'g_mk': 128, 'g_ffn': 1024}[name]
    return n, -(-n // LANES)


def _small_slab(d):
    parts = []
    for name in SMALL:
        n, rows = _small_rows(name)
        parts.append(jnp.pad(d[name].reshape(-1).astype(F32), (0, rows * LANES - n)).reshape(rows, LANES))
    slab = jnp.concatenate(parts, axis=0)
    return jnp.pad(slab, ((0, -slab.shape[0] % 8), (0, 0)))


def _small_unslab(slab, like):
    out, r = {}, 0
    for name in SMALL:
        n, rows = _small_rows(name)
        out[name] = slab[r:r + rows].reshape(-1)[:n].reshape(like[name].shape)
        r += rows
    return out


def _full_from_gathered(gathered, name):
    r, c = BIG_SHAPE[name]
    if BIG_AXIS[name] == 0:
        return gathered.reshape(r, c)
    return gathered.transpose(1, 0, 2).reshape(r, c)


def _shards_of_full(g, name):
    r, c = BIG_SHAPE[name]
    if BIG_AXIS[name] == 0:
        return g.reshape(N_DEV, r // N_DEV, c)
    return g.reshape(r, N_DEV, c // N_DEV).transpose(1, 0, 2)


class _DistWeights:
    def __init__(self, shards):
        self.shards = shards
        self.received = {}

    def first(self):
        return _full_from_gathered(_all_gather8(self.shards['w_in'].astype(BF16), "ag_w_in"), 'w_in')

    def gather(self, names):
        return _Exchange([self.shards[n].astype(BF16) for n in names], scatter=False)

    def gathered(self, names, got):
        return {n: _full_from_gathered(g, n) for n, g in zip(names, got)}

    def scatter(self, key, grads):
        return _Exchange([_shards_of_full(grads[n], n) for n in RS_GROUPS[key]], scatter=True)

    def scattered(self, key, got):
        self.received.update(zip(RS_GROUPS[key], got))


def kernel(x, mem, positions, g_mix, w_in, g_cq, w_uq, g_ckv, w_ukv, g_q_nope, g_q_pe, g_k_nope, g_k_pe, g_gm_ln, b_gm_ln, w_spatial, b_spatial, g_mem, w_mem_kv, g_mq, g_mk, w_o_gm, w_o_mla, w_o_mem, w_out, g_ffn, w_ff1, w_ff2, loss_target, m_g_mix, m_w_in, m_g_cq, m_w_uq, m_g_ckv, m_w_ukv, m_g_q_nope, m_g_q_pe, m_g_k_nope, m_g_k_pe, m_g_gm_ln, m_b_gm_ln, m_w_spatial, m_b_spatial, m_g_mem, m_w_mem_kv, m_g_mq, m_g_mk, m_w_o_gm, m_w_o_mla, m_w_o_mem, m_w_out, m_g_ffn, m_w_ff1, m_w_ff2, v_g_mix, v_w_in, v_g_cq, v_w_uq, v_g_ckv, v_w_ukv, v_g_q_nope, v_g_q_pe, v_g_k_nope, v_g_k_pe, v_g_gm_ln, v_b_gm_ln, v_w_spatial, v_b_spatial, v_g_mem, v_w_mem_kv, v_g_mq, v_g_mk, v_w_o_gm, v_w_o_mla, v_w_o_mem, v_w_out, v_g_ffn, v_w_ff1, v_w_ff2):
    given = dict(locals())
    w = {n: given[n][0] for n in WEIGHTS}
    mom = {n: given['m_' + n][0] for n in WEIGHTS}
    var = {n: given['v_' + n][0] for n in WEIGHTS}

    ws = _DistWeights({n: w[n] for n in BIG})
    loss_part, grad_x, G = _local_step(x, mem, positions, loss_target, {n: w[n] for n in SMALL}, ws)
    loss = lax.psum(0.5 * jnp.sum(loss_part) / D_MODEL, ("x", "y", "c"))

    outs = {}
    for n in BIG:
        for prefix, res in zip(("grad_", "delta_", "new_m_", "new_v_"),
                               _sum_adamw(ws.received[n], w[n], mom[n], var[n], "adamw_" + n)):
            outs[prefix + n] = res[None]

    parts = _all_gather8(_small_slab(G), "ag_small")
    small = _sum_adamw(parts, _small_slab(w), _small_slab(mom), _small_slab(var), "adamw_small")
    for prefix, small_slab in zip(("grad_", "delta_", "new_m_", "new_v_"), small):
        sm = _small_unslab(small_slab, given)
        for n in SMALL:
            outs[prefix + n] = sm[n]
    return (loss, grad_x, *[outs[p + n] for p in ("grad_", "delta_", "new_m_", "new_v_") for n in WEIGHTS])
```

```python
import functools
import math

import jax
import jax.numpy as jnp
from jax import lax
from jax.experimental import pallas as pl
from jax.experimental.pallas import tpu as pltpu

F32 = jnp.float32
BF16 = jnp.bfloat16
ACT = BF16

D_MODEL = 1024
MEM_HEADS = 4
HEAD = 128
GM_WIDTH = 512
GM_CHUNK = 128
GM_GROUPS = 4
MLA_HEADS = 8
MLA_ROPE = 64
Q_LORA = 384
KV_LORA = 256
D_FF = 4096
EPS = 1e-6
ROPE_BASE = 10000.0
MLA_SCALE = 1.0 / math.sqrt(HEAD + MLA_ROPE)
MEM_SCALE = 1.0 / math.sqrt(HEAD)
LOG2E = 1.4426950408889634
LN2 = 0.6931471805599453
ATT_TILE = 256
C_ZU, C_ZV, C_CQ, C_CKV, C_KPE, C_QM, C_ZG, C_END = 0, 512, 1024, 1408, 1664, 1728, 2240, 5312
Z_GM, Z_QM, Z_MLA, Z_KPE, Z_COLS = 3072, 4096, 4608, 5248, 5376
MLA_W = 768
QCAT = 2 * HEAD
ADAM_LR, ADAM_B1, ADAM_B2, ADAM_EPS, ADAM_WD, ADAM_STEP = 0.001, 0.9, 0.999, 1e-08, 0.01, 10
N_DEV = 8
LANES = 128
VMEM_LIMIT = 48 * 1024 * 1024
NEG = -1e30

BIG = ['w_in', 'w_uq', 'w_ukv', 'w_mem_kv', 'w_o_gm', 'w_o_mla', 'w_o_mem', 'w_out', 'w_ff1', 'w_ff2']
BIG_AXIS = {'w_in': 1, 'w_uq': 1, 'w_ukv': 1, 'w_mem_kv': 0, 'w_o_gm': 1, 'w_o_mla': 0, 'w_o_mem': 1,
            'w_out': 0, 'w_ff1': 1, 'w_ff2': 0}
BIG_SHAPE = {'w_in': (1024, 5312), 'w_uq': (384, 1536), 'w_ukv': (256, 2048), 'w_mem_kv': (1024, 1024),
             'w_o_gm': (512, 1024), 'w_o_mla': (1024, 1024), 'w_o_mem': (512, 1024), 'w_out': (1024, 1024),
             'w_ff1': (1024, 4096), 'w_ff2': (4096, 1024)}
SMALL = ['g_mix', 'g_cq', 'g_ckv', 'g_q_nope', 'g_q_pe', 'g_k_nope', 'g_k_pe', 'g_gm_ln', 'b_gm_ln',
         'w_spatial', 'b_spatial', 'g_mem', 'g_mq', 'g_mk', 'g_ffn']
WEIGHTS = ['g_mix', 'w_in', 'g_cq', 'w_uq', 'g_ckv', 'w_ukv', 'g_q_nope', 'g_q_pe', 'g_k_nope', 'g_k_pe',
           'g_gm_ln', 'b_gm_ln', 'w_spatial', 'b_spatial', 'g_mem', 'w_mem_kv', 'g_mq', 'g_mk', 'w_o_gm',
           'w_o_mla', 'w_o_mem', 'w_out', 'g_ffn', 'w_ff1', 'w_ff2']


def _pick(n, target, mult=LANES):
    best = None
    t = mult
    while t <= min(n, target):
        if n % t == 0:
            best = t
        t += mult
    return best if best is not None else n


def _params(sem):
    return pltpu.CompilerParams(dimension_semantics=sem, vmem_limit_bytes=VMEM_LIMIT)


MESH = pl.DeviceIdType.MESH
HBM_SPEC = pl.BlockSpec(memory_space=pltpu.HBM)


class _Exchange:
    def __init__(self, srcs, scatter):
        self.srcs, self.scatter = list(srcs), scatter
        self.out_shapes = [jax.ShapeDtypeStruct(s.shape if scatter else (N_DEV,) + s.shape, s.dtype) for s in self.srcs]
        n = len(self.srcs)
        self.scratch = [pltpu.SemaphoreType.DMA((n, N_DEV - 1)), pltpu.SemaphoreType.DMA((n, N_DEV - 1)),
                        pltpu.SemaphoreType.DMA((n,))]

    def _copies(self, src_refs, dst_refs, send_sems, recv_sems, local_sems):
        x, y, c = lax.axis_index("x"), lax.axis_index("y"), lax.axis_index("c")
        me = 4 * x + 2 * y + c
        local, remote = [], []
        for a, (src_ref, dst_ref) in enumerate(zip(src_refs, dst_refs)):
            def mine_for(dev, src_ref=src_ref):
                return src_ref.at[dev] if self.scatter else src_ref

            local.append(pltpu.make_async_copy(mine_for(me), dst_ref.at[me], local_sems.at[a]))
            for k in range(1, N_DEV):
                px = 1 - x if k & 4 else x
                py = 1 - y if k & 2 else y
                pc = 1 - c if k & 1 else c
                remote.append(pltpu.make_async_remote_copy(
                    src_ref=mine_for(4 * px + 2 * py + pc), dst_ref=dst_ref.at[me], send_sem=send_sems.at[a, k - 1],
                    recv_sem=recv_sems.at[a, k - 1], device_id=(px, py, pc), device_id_type=MESH))
        return local, remote

    def start(self, *refs):
        local, remote = self._copies(*refs)
        for cp in local + remote:
            cp.start()

    def wait(self, *refs):
        local, remote = self._copies(*refs)
        for cp in remote + local:
            cp.wait()


def _call(body, rider, ins, *, name, grid, in_specs, out_specs, out_shape, scratch_shapes, sem):
    if rider is None:
        return pl.pallas_call(body, name=name, grid=grid, in_specs=in_specs, out_specs=out_specs, out_shape=out_shape,
                              scratch_shapes=scratch_shapes, compiler_params=_params(sem))(*ins)
    single = not isinstance(out_shape, (list, tuple))
    own_specs, own_shapes = ([out_specs], [out_shape]) if single else (list(out_specs), list(out_shape))
    n_in, n_out, n_sc, n_r = len(ins), len(own_shapes), len(scratch_shapes), len(rider.srcs)

    def carrying(*refs):
        own_in, srcs = refs[:n_in], refs[n_in:n_in + n_r]
        own_out, dsts = refs[n_in + n_r:n_in + n_r + n_out], refs[n_in + n_r + n_out:n_in + 2 * n_r + n_out]
        own_sc = refs[n_in + 2 * n_r + n_out:n_in + 2 * n_r + n_out + n_sc]
        sems = refs[n_in + 2 * n_r + n_out + n_sc:]
        first = last = None
        for d, steps in enumerate(grid):
            f, l = pl.program_id(d) == 0, pl.program_id(d) == steps - 1
            first, last = (f, l) if first is None else (first & f, last & l)

        @pl.when(first)
        def _():
            rider.start(srcs, dsts, *sems)

        body(*own_in, *own_out, *own_sc)

        @pl.when(last)
        def _():
            rider.wait(srcs, dsts, *sems)

    res = pl.pallas_call(
        carrying, name=name, grid=grid, in_specs=list(in_specs) + [HBM_SPEC] * n_r, out_specs=own_specs + [HBM_SPEC] * n_r,
        out_shape=own_shapes + rider.out_shapes, scratch_shapes=list(scratch_shapes) + rider.scratch,
        compiler_params=_params(("arbitrary",) * len(grid)),
    )(*ins, *rider.srcs)
    own = res[:n_out]
    return (own[0] if single else list(own)), list(res[n_out:])


def _matmul(a, b, mode, out_dtype, name, add=None, relu2_a=False, relu2_grad=None,
            tm_t=None, tn_t=None, tk_t=None, rider=None):
    if mode == 'nn':
        (M, K), (K2, N) = a.shape, b.shape
    elif mode == 'nt':
        (M, K), (N, K2) = a.shape, b.shape
    else:
        (K, M), (K2, N) = a.shape, b.shape
    assert K == K2, (name, a.shape, b.shape)
    d_tm, d_tn, d_tk = (1024, 1024, 2048) if mode == 'tn' else (2048, 512, 1024)
    tm, tn, tk = _pick(M, tm_t or d_tm), _pick(N, tn_t or d_tn), _pick(K, tk_t or d_tk)
    gm, gn, nk = M // tm, N // tn, K // tk
    if mode == 'nn':
        a_spec = pl.BlockSpec((tm, tk), lambda i, j, k: (i, k))
        b_spec = pl.BlockSpec((tk, tn), lambda i, j, k: (k, j))
        dims = (((1,), (0,)), ((), ()))
    elif mode == 'nt':
        a_spec = pl.BlockSpec((tm, tk), lambda i, j, k: (i, k))
        b_spec = pl.BlockSpec((tn, tk), lambda i, j, k: (j, k))
        dims = (((1,), (1,)), ((), ()))
    else:
        a_spec = pl.BlockSpec((tk, tm), lambda i, j, k: (k, i))
        b_spec = pl.BlockSpec((tk, tn), lambda i, j, k: (k, j))
        dims = (((0,), (0,)), ((), ()))
    o_spec = pl.BlockSpec((tm, tn), lambda i, j, k: (i, j))
    has_add, has_e = add is not None, relu2_grad is not None

    def body(*refs):
        a_ref, b_ref = refs[0], refs[1]
        pos = 2
        add_ref = e_ref = None
        if has_add:
            add_ref = refs[pos]
            pos += 1
        if has_e:
            e_ref = refs[pos]
            pos += 1
        o_ref = refs[pos]
        acc_ref = refs[pos + 1] if nk > 1 else None

        av = a_ref[...]
        if relu2_a:
            av = jnp.maximum(av.astype(F32), 0.0)
            av = av * av
        prod = lax.dot_general(av.astype(BF16), b_ref[...].astype(BF16), dims, preferred_element_type=F32)

        def finish(r):
            if has_add:
                r = r + add_ref[...]
            if has_e:
                r = r * (2.0 * jnp.maximum(e_ref[...].astype(F32), 0.0))
            o_ref[...] = r.astype(out_dtype)

        if nk == 1:
            finish(prod)
        else:
            k = pl.program_id(2)

            @pl.when(k == 0)
            def _():
                acc_ref[...] = prod

            @pl.when(k > 0)
            def _():
                acc_ref[...] += prod

            @pl.when(k == nk - 1)
            def _():
                finish(acc_ref[...])

    ins, specs = [a, b], [a_spec, b_spec]
    if has_add:
        ins.append(add)
        specs.append(o_spec)
    if has_e:
        ins.append(relu2_grad)
        specs.append(o_spec)
    return _call(body, rider, ins, name=name, grid=(gm, gn, nk), in_specs=specs, out_specs=o_spec,
                 out_shape=jax.ShapeDtypeStruct((M, N), out_dtype),
                 scratch_shapes=[pltpu.VMEM((tm, tn), F32)] if nk > 1 else [], sem=("parallel", "parallel", "arbitrary"))


ROW_BLOCK_BYTES = 12 * 1024 * 1024


def _row_tile(rows, row_bytes):
    return _pick(rows, max(16, min(1024, ROW_BLOCK_BYTES // row_bytes)), 16)


def _rowspec(tr, width, col=0):
    return pl.BlockSpec((tr, width), lambda i, col=col: (i, col))


def _fullspec(shape):
    nd = len(shape)
    return pl.BlockSpec(shape, lambda i, nd=nd: (0,) * nd)


def _rms(x, width):
    x = x.astype(F32)
    return lax.rsqrt(jnp.sum(x * x, axis=-1, keepdims=True) * (1.0 / width) + EPS)


def _rms_bwd_rows(x, g, dy, width):
    x, dy = x.astype(F32), dy.astype(F32)
    r = _rms(x, width)
    xh = x * r
    dn = dy * g
    dx = r * (dn - xh * (jnp.sum(dn * xh, axis=-1, keepdims=True) * (1.0 / width)))
    return dx, dy * xh


def _acc_rows(ref, val, first):
    s = jnp.sum(val, axis=0, keepdims=True)

    @pl.when(first)
    def _():
        ref[...] = s

    @pl.when(jnp.logical_not(first))
    def _():
        ref[...] += s


def _rms_fwd(x, g, name):
    rows, width = x.shape
    tr = _row_tile(rows, 6 * width)

    def body(x_ref, g_ref, o_ref):
        xv = x_ref[...]
        o_ref[...] = (xv * _rms(xv, width) * g_ref[...]).astype(BF16)

    return pl.pallas_call(
        body, name=name, grid=(rows // tr,),
        in_specs=[_rowspec(tr, width), _fullspec((1, width))], out_specs=_rowspec(tr, width),
        out_shape=jax.ShapeDtypeStruct((rows, width), BF16), compiler_params=_params(("parallel",)),
    )(x, g)


def _rms_bwd(x, g, dy, res, name):
    rows, width = x.shape
    tr = _row_tile(rows, 16 * width)
    has_res = res is not None

    def body(*refs):
        if has_res:
            x_ref, g_ref, dy_ref, res_ref, dx_ref, dg_ref = refs
        else:
            x_ref, g_ref, dy_ref, dx_ref, dg_ref = refs
        dx, dgv = _rms_bwd_rows(x_ref[...], g_ref[...], dy_ref[...], width)
        if has_res:
            dx = dx + res_ref[...]
        dx_ref[...] = dx
        _acc_rows(dg_ref, dgv, pl.program_id(0) == 0)

    ins = [x, g, dy] + ([res] if has_res else [])
    specs = [_rowspec(tr, width), _fullspec((1, width)), _rowspec(tr, width)] + ([_rowspec(tr, width)] if has_res else [])
    return pl.pallas_call(
        body, name=name, grid=(rows // tr,), in_specs=specs,
        out_specs=[_rowspec(tr, width), _fullspec((1, width))],
        out_shape=[jax.ShapeDtypeStruct((rows, width), F32), jax.ShapeDtypeStruct((1, width), F32)],
        compiler_params=_params(("arbitrary",)),
    )(*ins)


_GELU_C = math.sqrt(2.0 / math.pi)


def _gelu(x):
    t = jnp.tanh(_GELU_C * (x + 0.044715 * (x * x * x)))
    return 0.5 * x * (1.0 + t), t


def _gelu_grad(x, t):
    return 0.5 * (1.0 + t) + 0.5 * x * (1.0 - t * t) * (_GELU_C * (1.0 + 3.0 * 0.044715 * (x * x)))


def _gm_forward_rows(zu, zv, gln, bln, wc_ref, bst, n_chunk):
    u, tu = _gelu(zu)
    a, ta = _gelu(zv)
    mu = jnp.mean(a, axis=-1, keepdims=True)
    ac = a - mu
    rs = lax.rsqrt(jnp.mean(ac * ac, axis=-1, keepdims=True) + EPS)
    n = ac * rs
    v = n * gln + bln
    vb = v.astype(BF16)
    rows = []
    for c in range(n_chunk):
        cols = []
        for g in range(GM_GROUPS):
            vc = vb[c * GM_CHUNK:(c + 1) * GM_CHUNK, g * LANES:(g + 1) * LANES]
            mixed = jnp.dot(wc_ref[g], vc, preferred_element_type=F32) + bst[g]
            cols.append(mixed)
        rows.append(jnp.concatenate(cols, axis=1))
    mixed = jnp.concatenate(rows, axis=0) if n_chunk > 1 else rows[0]
    return u, tu, ta, n, rs, v, mixed


def _gm_fwd(z, gln, bln, wc, bst, name):
    rows = z.shape[0]
    tr = _pick(rows, 512, GM_CHUNK)
    n_chunk = tr // GM_CHUNK

    def body(zu_ref, zv_ref, gln_ref, bln_ref, wc_ref, bst_ref, o_ref):
        u, _, _, _, _, _, mixed = _gm_forward_rows(zu_ref[...].astype(F32), zv_ref[...].astype(F32), gln_ref[...], bln_ref[...], wc_ref,
                                                   bst_ref, n_chunk)
        o_ref[...] = (u * mixed).astype(BF16)

    return pl.pallas_call(
        body, name=name, grid=(rows // tr,),
        in_specs=[_rowspec(tr, GM_WIDTH, Z_GM // GM_WIDTH), _rowspec(tr, GM_WIDTH, Z_GM // GM_WIDTH + 1),_fullspec((1, GM_WIDTH)), _fullspec((1, GM_WIDTH)),
                  _fullspec((GM_GROUPS, GM_CHUNK, GM_CHUNK)), _fullspec((GM_GROUPS, GM_CHUNK, LANES))],
        out_specs=_rowspec(tr, GM_WIDTH), out_shape=jax.ShapeDtypeStruct((rows, GM_WIDTH), BF16),
        compiler_params=_params(("parallel",)),
    )(z, z, gln, bln, wc, bst)


ANY_SPEC = pl.BlockSpec(memory_space=pl.ANY)


def _gm_bwd(z, dy, gln, bln, wc, wct, bst, dz, name):
    rows = z.shape[0]
    tr = _pick(rows, 512, GM_CHUNK)
    n_chunk = tr // GM_CHUNK

    def body(zu_ref, zv_ref, dy_ref, gln_ref, bln_ref, wc_ref, wct_ref, bst_ref, _, dz_ref, dws_ref, dbs_ref, dgl_ref,
             dbl_ref):
        first = pl.program_id(0) == 0
        zu, zv, gln = zu_ref[...].astype(F32), zv_ref[...].astype(F32), gln_ref[...]
        u, tu, ta, n, rs, v, mixed = _gm_forward_rows(zu, zv, gln, bln_ref[...], wc_ref, bst_ref, n_chunk)
        dyv = dy_ref[...].astype(F32)
        dzu = dyv * mixed * _gelu_grad(zu, tu)
        dmix = dyv * u
        dmb = dmix.astype(BF16)
        vb = v.astype(BF16)
        dv_rows, dws, dbs = [], [None] * GM_GROUPS, None
        for c in range(n_chunk):
            rsl = slice(c * GM_CHUNK, (c + 1) * GM_CHUNK)
            cols = []
            for g in range(GM_GROUPS):
                csl = slice(g * LANES, (g + 1) * LANES)
                dmc = dmb[rsl, csl]
                cols.append(jnp.dot(wct_ref[g], dmc, preferred_element_type=F32))
                w_part = lax.dot_general(dmc, vb[rsl, csl], (((1,), (1,)), ((), ())), preferred_element_type=F32)
                dws[g] = w_part if dws[g] is None else dws[g] + w_part
            dv_rows.append(jnp.concatenate(cols, axis=1))
            dbs = dmix[rsl, :] if dbs is None else dbs + dmix[rsl, :]
        dv = jnp.concatenate(dv_rows, axis=0) if n_chunk > 1 else dv_rows[0]
        dn = dv * gln
        da = rs * (dn - jnp.mean(dn, axis=-1, keepdims=True) - n * jnp.mean(dn * n, axis=-1, keepdims=True))
        dzv = da * _gelu_grad(zv, ta)
        dz_ref[:, 0:GM_WIDTH] = dzu.astype(BF16)
        dz_ref[:, GM_WIDTH:2 * GM_WIDTH] = dzv.astype(BF16)
        _acc_rows(dgl_ref, dv * n, first)
        _acc_rows(dbl_ref, dv, first)

        @pl.when(first)
        def _():
            for g in range(GM_GROUPS):
                dws_ref[g] = dws[g]
            dbs_ref[...] = dbs

        @pl.when(jnp.logical_not(first))
        def _():
            for g in range(GM_GROUPS):
                dws_ref[g] += dws[g]
            dbs_ref[...] += dbs

    wspec = _fullspec((GM_GROUPS, GM_CHUNK, GM_CHUNK))
    return pl.pallas_call(
        body, name=name, grid=(rows // tr,),
        in_specs=[_rowspec(tr, GM_WIDTH, Z_GM // GM_WIDTH), _rowspec(tr, GM_WIDTH, Z_GM // GM_WIDTH + 1),
                  _rowspec(tr, GM_WIDTH), _fullspec((1, GM_WIDTH)), _fullspec((1, GM_WIDTH)), wspec, wspec, wspec, ANY_SPEC],
        out_specs=[_rowspec(tr, 2 * GM_WIDTH, Z_GM // (2 * GM_WIDTH)), wspec, _fullspec((GM_CHUNK, GM_WIDTH)),
                   _fullspec((1, GM_WIDTH)), _fullspec((1, GM_WIDTH))],
        out_shape=[jax.ShapeDtypeStruct(dz.shape, dz.dtype), jax.ShapeDtypeStruct((GM_GROUPS, GM_CHUNK, GM_CHUNK), F32),
                   jax.ShapeDtypeStruct((GM_CHUNK, GM_WIDTH), F32), jax.ShapeDtypeStruct((1, GM_WIDTH), F32),
                   jax.ShapeDtypeStruct((1, GM_WIDTH), F32)],
        input_output_aliases={8: 0}, compiler_params=_params(("arbitrary",)),
    )(z, z, dy, gln, bln, wc, wct, bst, dz)


def _lat_fwd(z, g_cq, g_ckv, name):
    rows = z.shape[0]
    tr = _row_tile(rows, 4 * MLA_W)

    def body(z_ref, gq_ref, gkv_ref, nq_ref, nkv_ref):
        zb = z_ref[...]
        cq, ckv = zb[:, 0:Q_LORA], zb[:, Q_LORA:Q_LORA + KV_LORA]
        nq_ref[...] = (cq * _rms(cq, Q_LORA) * gq_ref[...]).astype(BF16)
        nkv_ref[...] = (ckv * _rms(ckv, KV_LORA) * gkv_ref[...]).astype(BF16)

    return pl.pallas_call(
        body, name=name, grid=(rows // tr,),
        in_specs=[_rowspec(tr, MLA_W, Z_MLA // MLA_W), _fullspec((1, Q_LORA)), _fullspec((1, KV_LORA))],
        out_specs=[_rowspec(tr, Q_LORA), _rowspec(tr, KV_LORA)],
        out_shape=[jax.ShapeDtypeStruct((rows, Q_LORA), BF16), jax.ShapeDtypeStruct((rows, KV_LORA), BF16)],
        compiler_params=_params(("parallel",)),
    )(z, g_cq, g_ckv)


def _lat_bwd(z, dnq, dnkv, dkpe, g_cq, g_ckv, dz, name):
    rows = z.shape[0]
    tr = _row_tile(rows, 8 * MLA_W)

    def body(z_ref, dnq_ref, dnkv_ref, dkpe_ref, gq_ref, gkv_ref, _, dz_ref, dgq_ref, dgkv_ref):
        first = pl.program_id(0) == 0
        zb = z_ref[...]
        dcq, dgq = _rms_bwd_rows(zb[:, 0:Q_LORA], gq_ref[...], dnq_ref[...], Q_LORA)
        dckv, dgkv = _rms_bwd_rows(zb[:, Q_LORA:Q_LORA + KV_LORA], gkv_ref[...], dnkv_ref[...], KV_LORA)
        dz_ref[:, 0:Q_LORA] = dcq.astype(BF16)
        dz_ref[:, Q_LORA:Q_LORA + KV_LORA] = dckv.astype(BF16)
        dz_ref[:, Q_LORA + KV_LORA:MLA_W] = dkpe_ref[...].astype(BF16)
        _acc_rows(dgq_ref, dgq, first)
        _acc_rows(dgkv_ref, dgkv, first)

    return pl.pallas_call(
        body, name=name, grid=(rows // tr,),
        in_specs=[_rowspec(tr, MLA_W, Z_MLA // MLA_W), _rowspec(tr, Q_LORA), _rowspec(tr, KV_LORA), _rowspec(tr, LANES),
                  _fullspec((1, Q_LORA)), _fullspec((1, KV_LORA)), ANY_SPEC],
        out_specs=[_rowspec(tr, MLA_W, Z_MLA // MLA_W), _fullspec((1, Q_LORA)), _fullspec((1, KV_LORA))],
        out_shape=[jax.ShapeDtypeStruct(dz.shape, dz.dtype), jax.ShapeDtypeStruct((1, Q_LORA), F32),
                   jax.ShapeDtypeStruct((1, KV_LORA), F32)],
        input_output_aliases={6: 0}, compiler_params=_params(("arbitrary",)),
    )(z, dnq, dnkv, dkpe, g_cq, g_ckv, dz)


def _rope(y, cc, ss):
    return y * cc + pltpu.roll(y, 64, 1) * ss


def _rope_bwd(d, cc, ss):
    return d * cc + pltpu.roll(d * ss, 64, 1)


def _qk_fwd(q, kv, z, cc, ss, gqn, gqp, gkn, gkp, name):
    rows = q.shape[0]
    W = MLA_HEADS * HEAD
    tr = _row_tile(rows, 20 * W)
    QS = MLA_SCALE * LOG2E

    def body(q_ref, kv_ref, kpe_ref, cc_ref, ss_ref, gqn_ref, gqp_ref, gkn_ref, gkp_ref, qc_ref, kc_ref, v_ref):
        cc, ss = cc_ref[...], ss_ref[...]
        kpe = kpe_ref[...]
        kp = _rope(kpe * _rms(kpe, MLA_ROPE) * gkp_ref[...], cc, ss).astype(BF16)
        for h in range(MLA_HEADS):
            qn = q_ref[:, h * HEAD:(h + 1) * HEAD]
            qp = q_ref[:, W + h * HEAD:W + (h + 1) * HEAD]
            kn = kv_ref[:, h * HEAD:(h + 1) * HEAD]
            qc_ref[:, h * QCAT:h * QCAT + HEAD] = (qn * _rms(qn, HEAD) * gqn_ref[...] * QS).astype(BF16)
            qc_ref[:, h * QCAT + HEAD:(h + 1) * QCAT] = (_rope(qp * _rms(qp, MLA_ROPE) * gqp_ref[...], cc, ss) * QS).astype(BF16)
            kc_ref[:, h * QCAT:h * QCAT + HEAD] = (kn * _rms(kn, HEAD) * gkn_ref[...]).astype(BF16)
            kc_ref[:, h * QCAT + HEAD:(h + 1) * QCAT] = kp
        v_ref[...] = kv_ref[:, W:2 * W].astype(BF16)

    g = _fullspec((1, HEAD))
    return pl.pallas_call(
        body, name=name, grid=(rows // tr,),
        in_specs=[_rowspec(tr, 2 * W), _rowspec(tr, 2 * W), _rowspec(tr, LANES, Z_KPE // LANES), _rowspec(tr, LANES),
                  _rowspec(tr, LANES), g, g, g, g],
        out_specs=[_rowspec(tr, MLA_HEADS * QCAT), _rowspec(tr, MLA_HEADS * QCAT), _rowspec(tr, W)],
        out_shape=[jax.ShapeDtypeStruct((rows, MLA_HEADS * QCAT), BF16), jax.ShapeDtypeStruct((rows, MLA_HEADS * QCAT), BF16),
                   jax.ShapeDtypeStruct((rows, W), BF16)],
        compiler_params=_params(("parallel",)),
    )(q, kv, z, cc, ss, gqn, gqp, gkn, gkp)


def _qk_bwd(q, kv, z, cc, ss, gqn, gqp, gkn, gkp, dqc, dkc, dv, name):
    rows = q.shape[0]
    W = MLA_HEADS * HEAD
    tr = _row_tile(rows, 40 * W)

    def body(q_ref, kv_ref, kpe_ref, cc_ref, ss_ref, gqn_ref, gqp_ref, gkn_ref, gkp_ref, dqc_ref, dkc_ref, dv_ref,
             dq_ref, dkv_ref, dkpe_ref, dgqn_ref, dgqp_ref, dgkn_ref, dgkp_ref):
        first = pl.program_id(0) == 0
        cc, ss = cc_ref[...], ss_ref[...]
        sqn = sqp = skn = dkp = None
        for h in range(MLA_HEADS):
            dx, dg = _rms_bwd_rows(q_ref[:, h * HEAD:(h + 1) * HEAD], gqn_ref[...], dqc_ref[:, h * QCAT:h * QCAT + HEAD], HEAD)
            dq_ref[:, h * HEAD:(h + 1) * HEAD] = dx.astype(BF16)
            sqn = dg if sqn is None else sqn + dg
            dy = _rope_bwd(dqc_ref[:, h * QCAT + HEAD:(h + 1) * QCAT], cc, ss)
            dx, dg = _rms_bwd_rows(q_ref[:, W + h * HEAD:W + (h + 1) * HEAD], gqp_ref[...], dy, MLA_ROPE)
            dq_ref[:, W + h * HEAD:W + (h + 1) * HEAD] = dx.astype(BF16)
            sqp = dg if sqp is None else sqp + dg
            dx, dg = _rms_bwd_rows(kv_ref[:, h * HEAD:(h + 1) * HEAD], gkn_ref[...], dkc_ref[:, h * QCAT:h * QCAT + HEAD], HEAD)
            dkv_ref[:, h * HEAD:(h + 1) * HEAD] = dx.astype(BF16)
            skn = dg if skn is None else skn + dg
            part = dkc_ref[:, h * QCAT + HEAD:(h + 1) * QCAT].astype(F32)
            dkp = part if dkp is None else dkp + part
        dkv_ref[:, W:2 * W] = dv_ref[...].astype(BF16)
        dx, dg = _rms_bwd_rows(kpe_ref[...], gkp_ref[...], _rope_bwd(dkp, cc, ss), MLA_ROPE)
        dkpe_ref[...] = dx
        _acc_rows(dgqn_ref, sqn, first)
        _acc_rows(dgqp_ref, sqp, first)
        _acc_rows(dgkn_ref, skn, first)
        _acc_rows(dgkp_ref, dg, first)

    g = _fullspec((1, HEAD))
    gs = jax.ShapeDtypeStruct((1, HEAD), F32)
    return pl.pallas_call(
        body, name=name, grid=(rows // tr,),
        in_specs=[_rowspec(tr, 2 * W), _rowspec(tr, 2 * W), _rowspec(tr, LANES, Z_KPE // LANES), _rowspec(tr, LANES),
                  _rowspec(tr, LANES), g, g, g, g, _rowspec(tr, MLA_HEADS * QCAT), _rowspec(tr, MLA_HEADS * QCAT),
                  _rowspec(tr, W)],
        out_specs=[_rowspec(tr, 2 * W), _rowspec(tr, 2 * W), _rowspec(tr, LANES), g, g, g, g],
        out_shape=[jax.ShapeDtypeStruct((rows, 2 * W), BF16), jax.ShapeDtypeStruct((rows, 2 * W), BF16),
                   jax.ShapeDtypeStruct((rows, LANES), F32), gs, gs, gs, gs],
        compiler_params=_params(("arbitrary",)),
    )(q, kv, z, cc, ss, gqn, gqp, gkn, gkp, dqc, dkc, dv)


def _headnorm_fwd(x, col, nheads, g, out_scale, name):
    rows = x.shape[0]
    W = nheads * HEAD
    tr = _row_tile(rows, 6 * W)

    def body(x_ref, g_ref, o_ref):
        for h in range(nheads):
            xv = x_ref[:, h * HEAD:(h + 1) * HEAD]
            o_ref[:, h * HEAD:(h + 1) * HEAD] = (xv * _rms(xv, HEAD) * g_ref[...] * out_scale).astype(BF16)

    return pl.pallas_call(
        body, name=name, grid=(rows // tr,),
        in_specs=[_rowspec(tr, W, col), _fullspec((1, HEAD))], out_specs=_rowspec(tr, W),
        out_shape=jax.ShapeDtypeStruct((rows, W), BF16), compiler_params=_params(("parallel",)),
    )(x, g)


def _headnorm_bwd(x, col, nheads, g, dy, tail, name, into=None):
    rows = x.shape[0]
    W = nheads * HEAD
    tr = _row_tile(rows, 12 * W)
    has_tail = tail is not None
    WO = 2 * W if has_tail else W

    def body(*refs):
        if into is not None:
            x_ref, g_ref, dy_ref, _, dx_ref, dg_ref = refs
        elif has_tail:
            x_ref, g_ref, dy_ref, t_ref, dx_ref, dg_ref = refs
        else:
            x_ref, g_ref, dy_ref, dx_ref, dg_ref = refs
        acc = None
        for h in range(nheads):
            sl = slice(h * HEAD, (h + 1) * HEAD)
            dx, dg = _rms_bwd_rows(x_ref[:, sl], g_ref[...], dy_ref[:, sl], HEAD)
            dx_ref[:, sl] = dx.astype(BF16)
            acc = dg if acc is None else acc + dg
        if has_tail:
            dx_ref[:, W:2 * W] = t_ref[...].astype(BF16)
        _acc_rows(dg_ref, acc, pl.program_id(0) == 0)

    ins = [x, g, dy] + ([tail] if has_tail else [])
    specs = [_rowspec(tr, W, col), _fullspec((1, HEAD)), _rowspec(tr, W)] + ([_rowspec(tr, W)] if has_tail else [])
    dx_spec, dx_shape, aliases = _rowspec(tr, WO), jax.ShapeDtypeStruct((rows, WO), BF16), {}
    if into is not None:
        assert not has_tail
        ins, specs = ins + [into[0]], specs + [ANY_SPEC]
        dx_spec, dx_shape, aliases = _rowspec(tr, W, into[1]), jax.ShapeDtypeStruct(into[0].shape, into[0].dtype), {3: 0}
    return pl.pallas_call(
        body, name=name, grid=(rows // tr,), in_specs=specs,
        out_specs=[dx_spec, _fullspec((1, HEAD))], out_shape=[dx_shape, jax.ShapeDtypeStruct((1, HEAD), F32)],
        input_output_aliases=aliases, compiler_params=_params(("arbitrary",)),
    )(*ins)


def _sigmoid(x):
    return 1.0 / (1.0 + jnp.exp(-x.astype(F32)))


def _merge_fwd(z, y_gm, y_mla, y_mem, name):
    rows = z.shape[0]
    tr = _row_tile(rows, 14 * D_MODEL)

    def body(g0_ref, g1_ref, g2_ref, a_ref, b_ref, c_ref, o_ref):
        m = _sigmoid(g0_ref[...]) * a_ref[...] + _sigmoid(g1_ref[...]) * b_ref[...] + _sigmoid(g2_ref[...]) * c_ref[...]
        o_ref[...] = m.astype(BF16)

    r = _rowspec(tr, D_MODEL)
    return pl.pallas_call(
        body, name=name, grid=(rows // tr,),
        in_specs=[_rowspec(tr, D_MODEL, 0), _rowspec(tr, D_MODEL, 1), _rowspec(tr, D_MODEL, 2),r, r, r],
        out_specs=r, out_shape=jax.ShapeDtypeStruct((rows, D_MODEL), BF16), compiler_params=_params(("parallel",)),
    )(z, z, z, y_gm, y_mla, y_mem)


def _merge_bwd(z, y_gm, y_mla, y_mem, dm, name):
    rows = z.shape[0]
    tr = _row_tile(rows, 24 * D_MODEL)

    def body(g0_ref, g1_ref, g2_ref, a_ref, b_ref, c_ref, dm_ref, da_ref, db_ref, dc_ref, dzg_ref):
        dmv = dm_ref[...].astype(F32)
        for k, (g_ref, y_ref, dy_ref) in enumerate(((g0_ref, a_ref, da_ref), (g1_ref, b_ref, db_ref), (g2_ref, c_ref, dc_ref))):
            s = _sigmoid(g_ref[...])
            dy_ref[...] = (dmv * s).astype(BF16)
            dzg_ref[:, k * D_MODEL:(k + 1) * D_MODEL] = (dmv * y_ref[...] * s * (1.0 - s)).astype(BF16)

    r = _rowspec(tr, D_MODEL)
    o = jax.ShapeDtypeStruct((rows, D_MODEL), BF16)
    return pl.pallas_call(
        body, name=name, grid=(rows // tr,),
        in_specs=[_rowspec(tr, D_MODEL, 0), _rowspec(tr, D_MODEL, 1), _rowspec(tr, D_MODEL, 2),r, r, r, r],
        out_specs=[r, r, r, _rowspec(tr, 3 * D_MODEL, 0)],
        out_shape=[o, o, o, jax.ShapeDtypeStruct((rows, Z_COLS), BF16)],
        compiler_params=_params(("parallel",)),
    )(z, z, z, y_gm, y_mla, y_mem, dm)


def _loss_head(y, target, name):
    rows, width = y.shape
    tr = _row_tile(rows, 12 * width)

    def body(y_ref, t_ref, dy_ref, l_ref):
        e = y_ref[...] - t_ref[...]
        dy_ref[...] = e * (1.0 / width)
        e2 = e * e
        part = e2[:, 0:LANES]
        for k in range(1, width // LANES):
            part = part + e2[:, k * LANES:(k + 1) * LANES]
        _acc_rows(l_ref, part, pl.program_id(0) == 0)

    return pl.pallas_call(
        body, name=name, grid=(rows // tr,),
        in_specs=[_rowspec(tr, width), _rowspec(tr, width)],
        out_specs=[_rowspec(tr, width), _fullspec((1, LANES))],
        out_shape=[jax.ShapeDtypeStruct((rows, width), F32), jax.ShapeDtypeStruct((1, LANES), F32)],
        compiler_params=_params(("arbitrary",)),
    )(y, target)


_NT = (((1,), (1,)), ((), ()))
_TN = (((0,), (0,)), ((), ()))


def _diag_mask(s):
    row = lax.broadcasted_iota(jnp.int32, s.shape, 0)
    col = lax.broadcasted_iota(jnp.int32, s.shape, 1)
    return jnp.where(row >= col, s, NEG)


def _attn_fwd(q, k, v, nb, nheads, dk, v_col0, causal, name, rider=None):
    S, Skv = q.shape[0] // nb, k.shape[0] // nb
    tq = _pick(Skv, ATT_TILE) if causal else _pick(S, 4 * ATT_TILE)
    nq = S // tq

    def body(q_ref, k_ref, v_ref, o_ref, lse_ref):
        for i in range(nq):
            r0 = i * tq
            qb = q_ref[r0:r0 + tq, :]
            if causal:
                spans = ([(0, r0, False)] if i > 0 else []) + [(r0, r0 + tq, True)]
            else:
                spans = [(0, Skv, False)]
            scores = []
            for a, b, masked in spans:
                s = lax.dot_general(qb, k_ref[a:b, :], _NT, preferred_element_type=F32)
                scores.append(_diag_mask(s) if masked else s)
            m = functools.reduce(jnp.maximum, [jnp.max(s, axis=-1, keepdims=True) for s in scores])
            l = acc = None
            for s, (a, b, _) in zip(scores, spans):
                p = jnp.exp2(s - m)
                lp = jnp.sum(p, axis=-1, keepdims=True)
                ap = jnp.dot(p.astype(BF16), v_ref[a:b, :].astype(BF16), preferred_element_type=F32)
                l, acc = (lp, ap) if l is None else (l + lp, acc + ap)
            o_ref[r0:r0 + tq, :] = (acc / l).astype(BF16)
            lse_ref[r0:r0 + tq, :] = m + jnp.log2(l)

    ins = [q, k, v]
    in_specs = [pl.BlockSpec((S, dk), lambda b, h: (b, h)), pl.BlockSpec((Skv, dk), lambda b, h: (b, h)),
                pl.BlockSpec((Skv, HEAD), lambda b, h: (b, v_col0 + h))]
    out_specs = [pl.BlockSpec((S, HEAD), lambda b, h: (b, h)), pl.BlockSpec((None, S, 1), lambda b, h: (h, b, 0))]
    out_shape = [jax.ShapeDtypeStruct((nb * S, nheads * HEAD), BF16), jax.ShapeDtypeStruct((nheads, nb * S, 1), F32)]
    return _call(body, rider, ins, name=name, grid=(nb, nheads), in_specs=in_specs, out_specs=out_specs,
                 out_shape=out_shape, scratch_shapes=[], sem=("parallel", "parallel"))


def _attn_bwd(q, k, v, o, do, lse, nb, nheads, dk, v_col0, scale, causal, name, rider=None):
    S, Skv = q.shape[0] // nb, k.shape[0] // nb
    tk = _pick(Skv, ATT_TILE)
    nkv = Skv // tk

    def body(q_ref, k_ref, v_ref, o_ref, do_ref, lse_ref, dq_ref, dk_ref, dv_ref, delta_ref, dob_ref, dqa_ref):
        dov = do_ref[...]
        delta_ref[...] = jnp.sum(o_ref[...].astype(F32) * dov.astype(F32), axis=-1, keepdims=True)
        dob_ref[...] = dov.astype(BF16)

        for j in range(nkv):
            c0 = j * tk
            kb = k_ref[c0:c0 + tk, :]
            vb = v_ref[c0:c0 + tk, :].astype(BF16)
            if causal:
                spans = [(c0, c0 + tk, True)] + ([(c0 + tk, S, False)] if c0 + tk < S else [])
            else:
                spans = [(0, S, False)]
            dk_acc = dv_acc = None
            for a, b, masked in spans:
                qb = q_ref[a:b, :]
                dob = dob_ref[a:b, :]
                s = lax.dot_general(qb, kb, _NT, preferred_element_type=F32)
                if masked:
                    s = _diag_mask(s)
                p = jnp.exp2(s - lse_ref[a:b, :])
                dp = lax.dot_general(dob, vb, _NT, preferred_element_type=F32)
                ds = (p * (dp - delta_ref[a:b, :])).astype(BF16)
                dv_p = lax.dot_general(p.astype(BF16), dob, _TN, preferred_element_type=F32)
                dk_p = lax.dot_general(ds, qb, _TN, preferred_element_type=F32)
                dk_acc, dv_acc = (dk_p, dv_p) if dk_acc is None else (dk_acc + dk_p, dv_acc + dv_p)
                dq_p = jnp.dot(ds, kb, preferred_element_type=F32) * scale
                if j == 0:
                    dqa_ref[a:b, :] = dq_p
                else:
                    dqa_ref[a:b, :] += dq_p
            dk_ref[c0:c0 + tk, :] = (dk_acc * LN2).astype(BF16)
            dv_ref[c0:c0 + tk, :] = dv_acc.astype(BF16)
        dq_ref[...] = dqa_ref[...].astype(BF16)

    ins = [q, k, v, o, do, lse]
    in_specs = [pl.BlockSpec((S, dk), lambda b, h: (b, h)), pl.BlockSpec((Skv, dk), lambda b, h: (b, h)),
                pl.BlockSpec((Skv, HEAD), lambda b, h: (b, v_col0 + h)), pl.BlockSpec((S, HEAD), lambda b, h: (b, h)),
                pl.BlockSpec((S, HEAD), lambda b, h: (b, h)), pl.BlockSpec((None, S, 1), lambda b, h: (h, b, 0))]
    out_specs = [pl.BlockSpec((S, dk), lambda b, h: (b, h)), pl.BlockSpec((Skv, dk), lambda b, h: (b, h)),
                 pl.BlockSpec((Skv, HEAD), lambda b, h: (b, h))]
    out_shape = [jax.ShapeDtypeStruct((nb * S, nheads * dk), BF16), jax.ShapeDtypeStruct((nb * Skv, nheads * dk), BF16),
                 jax.ShapeDtypeStruct((nb * Skv, nheads * HEAD), BF16)]
    return _call(body, rider, ins, name=name, grid=(nb, nheads), in_specs=in_specs, out_specs=out_specs,
                 out_shape=out_shape,
                 scratch_shapes=[pltpu.VMEM((S, 1), F32), pltpu.VMEM((S, HEAD), BF16), pltpu.VMEM((S, dk), F32)],
                 sem=("parallel", "parallel"))


def _spread_rope(a):
    zero = jnp.zeros(a.shape[:-1] + (32,), a.dtype)
    return jnp.concatenate([a[..., :32], zero, a[..., 32:], zero], axis=-1)


def _gather_rope(a):
    return jnp.concatenate([a[..., 0:32], a[..., 64:96]], axis=-1)


def _win_layout(w):
    return jnp.concatenate([w[:, C_ZG:C_END], w[:, C_ZU:C_CQ], w[:, C_QM:C_ZG], w[:, C_CQ:C_CKV], w[:, C_CKV:C_KPE],
                            _spread_rope(w[:, C_KPE:C_QM])], axis=1)


def _win_unlayout(d):
    return jnp.concatenate([d[:, Z_GM:Z_QM], d[:, Z_MLA:Z_MLA + Q_LORA], d[:, Z_MLA + Q_LORA:Z_KPE],
                            _gather_rope(d[:, Z_KPE:Z_COLS]), d[:, Z_QM:Z_MLA], d[:, 0:Z_GM]], axis=1)


def _wuq_layout(w):
    r = w.reshape(Q_LORA, MLA_HEADS, HEAD + MLA_ROPE)
    return jnp.concatenate([r[:, :, :HEAD].reshape(Q_LORA, -1), _spread_rope(r[:, :, HEAD:]).reshape(Q_LORA, -1)], axis=1)


def _wuq_unlayout(d):
    n = d[:, :MLA_HEADS * HEAD].reshape(Q_LORA, MLA_HEADS, HEAD)
    p = _gather_rope(d[:, MLA_HEADS * HEAD:].reshape(Q_LORA, MLA_HEADS, HEAD))
    return jnp.concatenate([n, p], axis=-1).reshape(Q_LORA, -1)


def _wukv_layout(w):
    r = w.reshape(KV_LORA, MLA_HEADS, 2 * HEAD)
    return jnp.concatenate([r[:, :, :HEAD].reshape(KV_LORA, -1), r[:, :, HEAD:].reshape(KV_LORA, -1)], axis=1)


def _wukv_unlayout(d):
    k = d[:, :MLA_HEADS * HEAD].reshape(KV_LORA, MLA_HEADS, HEAD)
    v = d[:, MLA_HEADS * HEAD:].reshape(KV_LORA, MLA_HEADS, HEAD)
    return jnp.concatenate([k, v], axis=-1).reshape(KV_LORA, -1)


AG_MID = ['w_uq', 'w_ukv', 'w_mem_kv', 'w_o_gm', 'w_o_mla', 'w_o_mem', 'w_out']
AG_FFN = ['w_ff1', 'w_ff2']
RS_GROUPS = {'ff2': ['w_ff2'], 'ff1': ['w_ff1'], 'proj': ['w_out', 'w_o_gm', 'w_o_mla', 'w_o_mem'],
             'lat': ['w_uq', 'w_ukv', 'w_mem_kv'], 'in': ['w_in']}


def _unride(res, rider):
    return (res, None) if rider is None else res


def _local_step(x, mem, positions, target, P, ws):
    B, S, _ = x.shape
    M = mem.shape[1]
    T = B * S
    x2d = x.reshape(T, D_MODEL)
    mem2d = mem.reshape(B * M, D_MODEL)
    tgt2d = target.reshape(T, D_MODEL)

    def row(v):
        return v.reshape(1, -1).astype(F32)

    inv_freq = ROPE_BASE ** (-jnp.arange(0, MLA_ROPE, 2, dtype=F32) / MLA_ROPE)
    ang = positions.reshape(T).astype(F32)[:, None] * inv_freq
    cos, sin, zero = jnp.cos(ang), jnp.sin(ang), jnp.zeros_like(ang)
    cc = jnp.concatenate([cos, zero, cos, zero], axis=1)
    ss = jnp.concatenate([-sin, zero, sin, zero], axis=1)

    w_in = _win_layout(ws.first()).astype(BF16)
    g_mix, g_cq, g_ckv, g_ffn, g_mem = row(P['g_mix']), row(P['g_cq']), row(P['g_ckv']), row(P['g_ffn']), row(P['g_mem'])
    gqn, gkn, gmq, gmk = row(P['g_q_nope']), row(P['g_k_nope']), row(P['g_mq']), row(P['g_mk'])
    gqp, gkp = _spread_rope(row(P['g_q_pe'])), _spread_rope(row(P['g_k_pe']))
    gln, bln = row(P['g_gm_ln']), row(P['b_gm_ln'])
    wc = jnp.tril(P['w_spatial'].astype(F32))
    wct = jnp.swapaxes(wc, 1, 2).astype(BF16)
    wc = wc.astype(BF16)
    bst = jnp.broadcast_to(P['b_spatial'].astype(F32)[:, :, None], (GM_GROUPS, GM_CHUNK, LANES))

    h = _rms_fwd(x2d, g_mix, "rms_mix")
    ride = ws.gather(AG_MID)
    z, got = _unride(_matmul(h, w_in, 'nn', ACT, "mm_in", tn_t=768, rider=ride), ride)
    mid = ws.gathered(AG_MID, got)
    w_uq = _wuq_layout(mid['w_uq']).astype(BF16)
    w_ukv = _wukv_layout(mid['w_ukv']).astype(BF16)
    w_mem_kv, w_o_gm, w_o_mla, w_o_mem, w_out = (mid[n].astype(BF16) for n in ('w_mem_kv', 'w_o_gm', 'w_o_mla', 'w_o_mem',
                                                                                 'w_out'))
    ygm_pre = _gm_fwd(z, gln, bln, wc, bst, "gm_fwd")
    y_gm = _matmul(ygm_pre, w_o_gm, 'nn', ACT, "mm_o_gm")
    nq, nkv = _lat_fwd(z, g_cq, g_ckv, "lat_fwd")
    q = _matmul(nq, w_uq, 'nn', ACT, "mm_uq")
    kv = _matmul(nkv, w_ukv, 'nn', ACT, "mm_ukv")
    qcat, kcat, vv = _qk_fwd(q, kv, z, cc, ss, gqn, gqp, gkn, gkp, "qk_fwd")
    ride = ws.gather(AG_FFN)
    (o, lse), got = _unride(_attn_fwd(qcat, kcat, vv, B, MLA_HEADS, QCAT, 0, True, "mla_attn_fwd", rider=ride), ride)
    ffn = ws.gathered(AG_FFN, got)
    w_ff1, w_ff2 = ffn['w_ff1'].astype(BF16), ffn['w_ff2'].astype(BF16)
    y_mla = _matmul(o, w_o_mla, 'nn', ACT, "mm_o_mla")
    nm = _rms_fwd(mem2d, g_mem, "rms_mem")
    kvm = _matmul(nm, w_mem_kv, 'nn', ACT, "mm_mem_kv")
    qm = _headnorm_fwd(z, Z_QM // (MEM_HEADS * HEAD), MEM_HEADS, gmq, MEM_SCALE * LOG2E, "memq_fwd")
    km = _headnorm_fwd(kvm, 0, MEM_HEADS, gmk, 1.0, "memk_fwd")
    om, lse_m = _attn_fwd(qm, km, kvm, B, MEM_HEADS, HEAD, MEM_HEADS, False, "mem_attn_fwd")
    y_mem = _matmul(om, w_o_mem, 'nn', ACT, "mm_o_mem")
    merged = _merge_fwd(z, y_gm, y_mla, y_mem, "merge_fwd")
    x1 = _matmul(merged, w_out, 'nn', F32, "mm_out", add=x2d)
    h2 = _rms_fwd(x1, g_ffn, "rms_ffn")
    a1 = _matmul(h2, w_ff1, 'nn', BF16, "mm_ff1")
    x2 = _matmul(a1, w_ff2, 'nn', F32, "mm_ff2", add=x1, relu2_a=True)
    dx2, loss_part = _loss_head(x2, tgt2d, "loss_head")

    G = {}
    ride = ws.scatter('ff2', {'w_ff2': _matmul(a1, dx2, 'tn', BF16, "mm_d_ff2", relu2_a=True)})
    da1, got = _unride(_matmul(dx2, w_ff2, 'nt', BF16, "mm_da1", relu2_grad=a1, rider=ride), ride)
    ws.scattered('ff2', got)
    ride = ws.scatter('ff1', {'w_ff1': _matmul(h2, da1, 'tn', BF16, "mm_d_ff1")})
    dh2, got = _unride(_matmul(da1, w_ff1, 'nt', ACT, "mm_dh2", rider=ride), ride)
    ws.scattered('ff1', got)
    dx1, G['g_ffn'] = _rms_bwd(x1, g_ffn, dh2, dx2, "rms_ffn_bwd")
    d_out = _matmul(merged, dx1, 'tn', BF16, "mm_d_out")
    dmerged = _matmul(dx1, w_out, 'nt', ACT, "mm_dmerged")
    dy_gm, dy_mla, dy_mem, dz = _merge_bwd(z, y_gm, y_mla, y_mem, dmerged, "merge_bwd")
    d_o_gm = _matmul(ygm_pre, dy_gm, 'tn', BF16, "mm_d_o_gm")
    d_o_mla = _matmul(o, dy_mla, 'tn', BF16, "mm_d_o_mla")
    d_o_mem = _matmul(om, dy_mem, 'tn', BF16, "mm_d_o_mem")
    dygm_pre = _matmul(dy_gm, w_o_gm, 'nt', ACT, "mm_dygm")
    dz, dws, dbs, G['g_gm_ln'], G['b_gm_ln'] = _gm_bwd(z, dygm_pre, gln, bln, wc, wct, bst, dz, "gm_bwd")
    G['w_spatial'] = jnp.tril(dws)
    G['b_spatial'] = jnp.sum(dbs.reshape(GM_CHUNK, GM_GROUPS, LANES), axis=-1).T
    do = _matmul(dy_mla, w_o_mla, 'nt', ACT, "mm_do")
    ride = ws.scatter('proj', {'w_out': d_out, 'w_o_gm': d_o_gm, 'w_o_mla': d_o_mla, 'w_o_mem': d_o_mem})
    (dqc, dkc, dvv), got = _unride(_attn_bwd(qcat, kcat, vv, o, do, lse, B, MLA_HEADS, QCAT, 0, MLA_SCALE, True,
                                             "mla_attn_bwd", rider=ride), ride)
    ws.scattered('proj', got)
    dq, dkv, dkpe, G['g_q_nope'], dgqp, G['g_k_nope'], dgkp = _qk_bwd(q, kv, z, cc, ss, gqn, gqp, gkn, gkp, dqc, dkc, dvv,
                                                                     "qk_bwd")
    G['g_q_pe'], G['g_k_pe'] = _gather_rope(dgqp), _gather_rope(dgkp)
    d_uq = _wuq_unlayout(_matmul(nq, dq, 'tn', BF16, "mm_d_uq"))
    dnq = _matmul(dq, w_uq, 'nt', ACT, "mm_dnq")
    d_ukv = _wukv_unlayout(_matmul(nkv, dkv, 'tn', BF16, "mm_d_ukv"))
    dnkv = _matmul(dkv, w_ukv, 'nt', ACT, "mm_dnkv")
    dz, G['g_cq'], G['g_ckv'] = _lat_bwd(z, dnq, dnkv, dkpe, g_cq, g_ckv, dz, "lat_bwd")
    dom = _matmul(dy_mem, w_o_mem, 'nt', ACT, "mm_dom")
    dqm, dkm, dvm = _attn_bwd(qm, km, kvm, om, dom, lse_m, B, MEM_HEADS, HEAD, MEM_HEADS, MEM_SCALE, False, "mem_attn_bwd")
    dz, G['g_mq'] = _headnorm_bwd(z, Z_QM // (MEM_HEADS * HEAD), MEM_HEADS, gmq, dqm, None, "memq_bwd",
                                  into=(dz, Z_QM // (MEM_HEADS * HEAD)))
    dkvm, G['g_mk'] = _headnorm_bwd(kvm, 0, MEM_HEADS, gmk, dkm, dvm, "memk_bwd")
    d_mem_kv = _matmul(nm, dkvm, 'tn', BF16, "mm_d_mem_kv")
    dnm = _matmul(dkvm, w_mem_kv, 'nt', ACT, "mm_dnm")
    _, G['g_mem'] = _rms_bwd(mem2d, g_mem, dnm, None, "rms_mem_bwd")
    ride = ws.scatter('lat', {'w_uq': d_uq, 'w_ukv': d_ukv, 'w_mem_kv': d_mem_kv})
    d_in, got = _unride(_matmul(h, dz, 'tn', BF16, "mm_d_in", tn_t=768, rider=ride), ride)
    ws.scattered('lat', got)
    ride = ws.scatter('in', {'w_in': _win_unlayout(d_in)})
    dh, got = _unride(_matmul(dz, w_in, 'nt', ACT, "mm_dh", tk_t=768, rider=ride), ride)
    ws.scattered('in', got)
    gx, G['g_mix'] = _rms_bwd(x2d, g_mix, dh, dx1, "rms_mix_bwd")
    return loss_part, gx.reshape(B, S, D_MODEL), G


def _all_gather8(xs, name):
    def body(x_ref, out_ref, send_sems, recv_sems, local_sem):
        x, y, c = lax.axis_index("x"), lax.axis_index("y"), lax.axis_index("c")
        me, sibling = (x, y, c), (x, y, 1 - c)
        chips = [(1 - x, y), (x, 1 - y), (1 - x, 1 - y)]

        def rows(px, py, pc):
            return out_ref.at[4 * px + 2 * py + pc]

        def copy(k, block, to, src=None):
            return pltpu.make_async_remote_copy(
                src_ref=rows(*block) if src is None else src, dst_ref=rows(*block),
                send_sem=send_sems.at[k], recv_sem=recv_sems.at[k], device_id=to, device_id_type=MESH)

        mine = pltpu.make_async_copy(x_ref, rows(*me), local_sem)
        mine.start()
        first = [copy(0, me, sibling, src=x_ref)]
        first += [copy(1 + j, me, (*chip, c), src=x_ref) for j, chip in enumerate(chips)]
        for cp in first:
            cp.start()
        passed = [copy(4 + j, (*chip, c), sibling) for j, chip in enumerate(chips)]
        for j, chip in enumerate(chips):
            copy(1 + j, (*chip, c), me).wait_recv()
            passed[j].start()
        copy(0, sibling, me).wait_recv()
        for j, chip in enumerate(chips):
            copy(4 + j, (*chip, 1 - c), me).wait_recv()
        for cp in first + passed:
            cp.wait_send()
        mine.wait()

    return pl.pallas_call(
        body, name=name, in_specs=[HBM_SPEC], out_specs=HBM_SPEC,
        out_shape=jax.ShapeDtypeStruct((N_DEV,) + xs.shape, xs.dtype),
        scratch_shapes=[pltpu.SemaphoreType.DMA((7,)), pltpu.SemaphoreType.DMA((7,)), pltpu.SemaphoreType.DMA],
    )(xs)


def _adamw_rows(w, g, m, v):
    m2 = ADAM_B1 * m + (1.0 - ADAM_B1) * g
    v2 = ADAM_B2 * v + (1.0 - ADAM_B2) * (g * g)
    m_hat = m2 / (1.0 - ADAM_B1 ** ADAM_STEP)
    v_hat = v2 / (1.0 - ADAM_B2 ** ADAM_STEP)
    delta = -ADAM_LR * (m_hat / (jnp.sqrt(v_hat) + ADAM_EPS) + ADAM_WD * w)
    return delta, m2, v2


def _sum_adamw(parts, w, m, v, name):
    rows, cols = w.shape
    tr = _pick(rows, max(16, 65536 // cols), 16)
    n = parts.shape[0]

    def body(p_ref, w_ref, m_ref, v_ref, g_ref, d_ref, m2_ref, v2_ref):
        g = p_ref[0].astype(F32)
        for k in range(1, n):
            g = g + p_ref[k].astype(F32)
        delta, m2, v2 = _adamw_rows(w_ref[...], g, m_ref[...], v_ref[...])
        g_ref[...] = g
        d_ref[...] = delta
        m2_ref[...] = m2
        v2_ref[...] = v2

    flat = pl.BlockSpec((tr, cols), lambda i: (i, 0))
    out = jax.ShapeDtypeStruct((rows, cols), F32)
    return pl.pallas_call(
        body, name=name, grid=(rows // tr,),
        in_specs=[pl.BlockSpec((n, tr, cols), lambda i: (0, i, 0)), flat, flat, flat], out_specs=[flat] * 4,
        out_shape=[out] * 4, compiler_params=_params(("parallel",)),
    )(parts, w, m, v)


def _small_rows(name):
    n = {'g_mix': 1024, 'g_cq': 384, 'g_ckv': 256, 'g_q_nope': 128, 'g_q_pe': 64, 'g_k_nope': 128, 'g_k_pe': 64,
         'g_gm_ln': 512, 'b_gm_ln': 512, 'w_spatial': GM_GROUPS * GM_CHUNK * GM_CHUNK, 'b_spatial': GM_GROUPS * GM_CHUNK,
         'g_mem': 1024, 'g_mq': 128, 'g_mk': 128, 'g_ffn': 1024}[name]
    return n, -(-n // LANES)


def _small_slab(d):
    parts = []
    for name in SMALL:
        n, rows = _small_rows(name)
        parts.append(jnp.pad(d[name].reshape(-1).astype(F32), (0, rows * LANES - n)).reshape(rows, LANES))
    slab = jnp.concatenate(parts, axis=0)
    return jnp.pad(slab, ((0, -slab.shape[0] % 8), (0, 0)))


def _small_unslab(slab, like):
    out, r = {}, 0
    for name in SMALL:
        n, rows = _small_rows(name)
        out[name] = slab[r:r + rows].reshape(-1)[:n].reshape(like[name].shape)
        r += rows
    return out


def _full_from_gathered(gathered, name):
    r, c = BIG_SHAPE[name]
    if BIG_AXIS[name] == 0:
        return gathered.reshape(r, c)
    return gathered.transpose(1, 0, 2).reshape(r, c)


def _shards_of_full(g, name):
    r, c = BIG_SHAPE[name]
    if BIG_AXIS[name] == 0:
        return g.reshape(N_DEV, r // N_DEV, c)
    return g.reshape(r, N_DEV, c // N_DEV).transpose(1, 0, 2)


class _DistWeights:
    def __init__(self, shards):
        self.shards = shards
        self.received = {}

    def first(self):
        return _full_from_gathered(_all_gather8(self.shards['w_in'].astype(BF16), "ag_w_in"), 'w_in')

    def gather(self, names):
        return _Exchange([self.shards[n].astype(BF16) for n in names], scatter=False)

    def gathered(self, names, got):
        return {n: _full_from_gathered(g, n) for n, g in zip(names, got)}

    def scatter(self, key, grads):
        return _Exchange([_shards_of_full(grads[n], n) for n in RS_GROUPS[key]], scatter=True)

    def scattered(self, key, got):
        self.received.update(zip(RS_GROUPS[key], got))


def kernel(x, mem, positions, g_mix, w_in, g_cq, w_uq, g_ckv, w_ukv, g_q_nope, g_q_pe, g_k_nope, g_k_pe, g_gm_ln, b_gm_ln, w_spatial, b_spatial, g_mem, w_mem_kv, g_mq, g_mk, w_o_gm, w_o_mla, w_o_mem, w_out, g_ffn, w_ff1, w_ff2, loss_target, m_g_mix, m_w_in, m_g_cq, m_w_uq, m_g_ckv, m_w_ukv, m_g_q_nope, m_g_q_pe, m_g_k_nope, m_g_k_pe, m_g_gm_ln, m_b_gm_ln, m_w_spatial, m_b_spatial, m_g_mem, m_w_mem_kv, m_g_mq, m_g_mk, m_w_o_gm, m_w_o_mla, m_w_o_mem, m_w_out, m_g_ffn, m_w_ff1, m_w_ff2, v_g_mix, v_w_in, v_g_cq, v_w_uq, v_g_ckv, v_w_ukv, v_g_q_nope, v_g_q_pe, v_g_k_nope, v_g_k_pe, v_g_gm_ln, v_b_gm_ln, v_w_spatial, v_b_spatial, v_g_mem, v_w_mem_kv, v_g_mq, v_g_mk, v_w_o_gm, v_w_o_mla, v_w_o_mem, v_w_out, v_g_ffn, v_w_ff1, v_w_ff2):
    given = dict(locals())
    w = {n: given[n][0] for n in WEIGHTS}
    mom = {n: given['m_' + n][0] for n in WEIGHTS}
    var = {n: given['v_' + n][0] for n in WEIGHTS}

    ws = _DistWeights({n: w[n] for n in BIG})
    loss_part, grad_x, G = _local_step(x, mem, positions, loss_target, {n: w[n] for n in SMALL}, ws)
    loss = lax.psum(0.5 * jnp.sum(loss_part) / D_MODEL, ("x", "y", "c"))

    outs = {}
    for n in BIG:
        for prefix, res in zip(("grad_", "delta_", "new_m_", "new_v_"),
                               _sum_adamw(ws.received[n], w[n], mom[n], var[n], "adamw_" + n)):
            outs[prefix + n] = res[None]

    parts = _all_gather8(_small_slab(G), "ag_small")
    small = _sum_adamw(parts, _small_slab(w), _small_slab(mom), _small_slab(var), "adamw_small")
    for prefix, small_slab in zip(("grad_", "delta_", "new_m_", "new_v_"), small):
        sm = _small_unslab(small_slab, given)
        for n in SMALL:
            outs[prefix + n] = sm[n]
    return (loss, grad_x, *[outs[p + n] for p in ("grad_", "delta_", "new_m_", "new_v_") for n in WEIGHTS])
```

```python
import functools
import math

import jax
import jax.numpy as jnp
from jax import lax
from jax.experimental import pallas as pl
from jax.experimental.pallas import tpu as pltpu

F32 = jnp.float32
BF16 = jnp.bfloat16
ACT = BF16

D_MODEL = 1024
MEM_HEADS = 4
HEAD = 128
GM_WIDTH = 512
GM_CHUNK = 128
GM_GROUPS = 4
MLA_HEADS = 8
MLA_ROPE = 64
Q_LORA = 384
KV_LORA = 256
D_FF = 4096
EPS = 1e-6
ROPE_BASE = 10000.0
MLA_SCALE = 1.0 / math.sqrt(HEAD + MLA_ROPE)
MEM_SCALE = 1.0 / math.sqrt(HEAD)
LOG2E = 1.4426950408889634
LN2 = 0.6931471805599453
ATT_TILE = 256
C_ZU, C_ZV, C_CQ, C_CKV, C_KPE, C_QM, C_ZG, C_END = 0, 512, 1024, 1408, 1664, 1728, 2240, 5312
Z_GM, Z_QM, Z_MLA, Z_KPE, Z_COLS = 3072, 4096, 4608, 5248, 5376
MLA_W = 768
QCAT = 2 * HEAD
ADAM_LR, ADAM_B1, ADAM_B2, ADAM_EPS, ADAM_WD, ADAM_STEP = 0.001, 0.9, 0.999, 1e-08, 0.01, 10
N_DEV = 8
LANES = 128
VMEM_LIMIT = 48 * 1024 * 1024
MAX_K_TILE = 8192
NEG = -1e30

BIG = ['w_in', 'w_uq', 'w_ukv', 'w_mem_kv', 'w_o_gm', 'w_o_mla', 'w_o_mem', 'w_out', 'w_ff1', 'w_ff2']
BIG_AXIS = {'w_in': 1, 'w_uq': 1, 'w_ukv': 1, 'w_mem_kv': 0, 'w_o_gm': 1, 'w_o_mla': 0, 'w_o_mem': 1,
            'w_out': 0, 'w_ff1': 1, 'w_ff2': 0}
BIG_SHAPE = {'w_in': (1024, 5312), 'w_uq': (384, 1536), 'w_ukv': (256, 2048), 'w_mem_kv': (1024, 1024),
             'w_o_gm': (512, 1024), 'w_o_mla': (1024, 1024), 'w_o_mem': (512, 1024), 'w_out': (1024, 1024),
             'w_ff1': (1024, 4096), 'w_ff2': (4096, 1024)}
SMALL = ['g_mix', 'g_cq', 'g_ckv', 'g_q_nope', 'g_q_pe', 'g_k_nope', 'g_k_pe', 'g_gm_ln', 'b_gm_ln',
         'w_spatial', 'b_spatial', 'g_mem', 'g_mq', 'g_mk', 'g_ffn']
WEIGHTS = ['g_mix', 'w_in', 'g_cq', 'w_uq', 'g_ckv', 'w_ukv', 'g_q_nope', 'g_q_pe', 'g_k_nope', 'g_k_pe',
           'g_gm_ln', 'b_gm_ln', 'w_spatial', 'b_spatial', 'g_mem', 'w_mem_kv', 'g_mq', 'g_mk', 'w_o_gm',
           'w_o_mla', 'w_o_mem', 'w_out', 'g_ffn', 'w_ff1', 'w_ff2']


def _pick(n, target, mult=LANES):
    best = None
    t = mult
    while t <= min(n, target):
        if n % t == 0:
            best = t
        t += mult
    return best if best is not None else n


def _params(sem):
    return pltpu.CompilerParams(dimension_semantics=sem, vmem_limit_bytes=VMEM_LIMIT)


MESH = pl.DeviceIdType.MESH
HBM_SPEC = pl.BlockSpec(memory_space=pltpu.HBM)


class _Exchange:
    def __init__(self, srcs, scatter):
        self.srcs, self.scatter = list(srcs), scatter
        self.out_shapes = [jax.ShapeDtypeStruct(s.shape if scatter else (N_DEV,) + s.shape, s.dtype) for s in self.srcs]
        n = len(self.srcs)
        self.scratch = [pltpu.SemaphoreType.DMA((n, N_DEV - 1)), pltpu.SemaphoreType.DMA((n, N_DEV - 1)),
                        pltpu.SemaphoreType.DMA((n,))]

    def _copies(self, src_refs, dst_refs, send_sems, recv_sems, local_sems):
        x, y, c = lax.axis_index("x"), lax.axis_index("y"), lax.axis_index("c")
        me = 4 * x + 2 * y + c
        local, remote = [], []
        for a, (src_ref, dst_ref) in enumerate(zip(src_refs, dst_refs)):
            def mine_for(dev, src_ref=src_ref):
                return src_ref.at[dev] if self.scatter else src_ref

            local.append(pltpu.make_async_copy(mine_for(me), dst_ref.at[me], local_sems.at[a]))
            for k in range(1, N_DEV):
                px = 1 - x if k & 4 else x
                py = 1 - y if k & 2 else y
                pc = 1 - c if k & 1 else c
                remote.append(pltpu.make_async_remote_copy(
                    src_ref=mine_for(4 * px + 2 * py + pc), dst_ref=dst_ref.at[me], send_sem=send_sems.at[a, k - 1],
                    recv_sem=recv_sems.at[a, k - 1], device_id=(px, py, pc), device_id_type=MESH))
        return local, remote

    def start(self, *refs):
        local, remote = self._copies(*refs)
        for cp in local + remote:
            cp.start()

    def wait(self, *refs):
        local, remote = self._copies(*refs)
        for cp in remote + local:
            cp.wait()


def _call(body, rider, ins, *, name, grid, in_specs, out_specs, out_shape, scratch_shapes, sem):
    if rider is None:
        return pl.pallas_call(body, name=name, grid=grid, in_specs=in_specs, out_specs=out_specs, out_shape=out_shape,
                              scratch_shapes=scratch_shapes, compiler_params=_params(sem))(*ins)
    single = not isinstance(out_shape, (list, tuple))
    own_specs, own_shapes = ([out_specs], [out_shape]) if single else (list(out_specs), list(out_shape))
    n_in, n_out, n_sc, n_r = len(ins), len(own_shapes), len(scratch_shapes), len(rider.srcs)

    def carrying(*refs):
        own_in, srcs = refs[:n_in], refs[n_in:n_in + n_r]
        own_out, dsts = refs[n_in + n_r:n_in + n_r + n_out], refs[n_in + n_r + n_out:n_in + 2 * n_r + n_out]
        own_sc = refs[n_in + 2 * n_r + n_out:n_in + 2 * n_r + n_out + n_sc]
        sems = refs[n_in + 2 * n_r + n_out + n_sc:]
        first = last = None
        for d, steps in enumerate(grid):
            f, l = pl.program_id(d) == 0, pl.program_id(d) == steps - 1
            first, last = (f, l) if first is None else (first & f, last & l)

        @pl.when(first)
        def _():
            rider.start(srcs, dsts, *sems)

        body(*own_in, *own_out, *own_sc)

        @pl.when(last)
        def _():
            rider.wait(srcs, dsts, *sems)

    res = pl.pallas_call(
        carrying, name=name, grid=grid, in_specs=list(in_specs) + [HBM_SPEC] * n_r, out_specs=own_specs + [HBM_SPEC] * n_r,
        out_shape=own_shapes + rider.out_shapes, scratch_shapes=list(scratch_shapes) + rider.scratch,
        compiler_params=_params(("arbitrary",) * len(grid)),
    )(*ins, *rider.srcs)
    own = res[:n_out]
    return (own[0] if single else list(own)), list(res[n_out:])


def _matmul(a, b, mode, out_dtype, name, add=None, relu2_a=False, relu2_grad=None,
            tm_t=None, tn_t=None, tk_t=None, rider=None):
    if mode == 'nn':
        (M, K), (K2, N) = a.shape, b.shape
    elif mode == 'nt':
        (M, K), (N, K2) = a.shape, b.shape
    else:
        (K, M), (K2, N) = a.shape, b.shape
    assert K == K2, (name, a.shape, b.shape)
    if mode == 'tn':
        d_tm, d_tn, d_tk = 1024, 1024, 2048
    else:
        d_tm, d_tn, d_tk = (2048 if K <= 1024 else 1024), 512, MAX_K_TILE
    tm, tn, tk = _pick(M, tm_t or d_tm), _pick(N, tn_t or d_tn), _pick(K, tk_t or d_tk)
    gm, gn, nk = M // tm, N // tn, K // tk
    if mode == 'nn':
        a_spec = pl.BlockSpec((tm, tk), lambda i, j, k: (i, k))
        b_spec = pl.BlockSpec((tk, tn), lambda i, j, k: (k, j))
        dims = (((1,), (0,)), ((), ()))
    elif mode == 'nt':
        a_spec = pl.BlockSpec((tm, tk), lambda i, j, k: (i, k))
        b_spec = pl.BlockSpec((tn, tk), lambda i, j, k: (j, k))
        dims = (((1,), (1,)), ((), ()))
    else:
        a_spec = pl.BlockSpec((tk, tm), lambda i, j, k: (k, i))
        b_spec = pl.BlockSpec((tk, tn), lambda i, j, k: (k, j))
        dims = (((0,), (0,)), ((), ()))
    o_spec = pl.BlockSpec((tm, tn), lambda i, j, k: (i, j))
    has_add, has_e = add is not None, relu2_grad is not None

    def body(*refs):
        a_ref, b_ref = refs[0], refs[1]
        pos = 2
        add_ref = e_ref = None
        if has_add:
            add_ref = refs[pos]
            pos += 1
        if has_e:
            e_ref = refs[pos]
            pos += 1
        o_ref = refs[pos]
        acc_ref = refs[pos + 1] if nk > 1 else None

        av = a_ref[...]
        if relu2_a:
            av = jnp.maximum(av, 0)
            av = av * av
        prod = lax.dot_general(av.astype(BF16), b_ref[...].astype(BF16), dims, preferred_element_type=F32)

        def finish(r):
            if has_add:
                r = r + add_ref[...]
            if has_e:
                r = r * (2.0 * jnp.maximum(e_ref[...].astype(F32), 0.0))
            o_ref[...] = r.astype(out_dtype)

        if nk == 1:
            finish(prod)
        else:
            k = pl.program_id(2)

            @pl.when(k == 0)
            def _():
                acc_ref[...] = prod

            @pl.when(k > 0)
            def _():
                acc_ref[...] += prod

            @pl.when(k == nk - 1)
            def _():
                finish(acc_ref[...])

    ins, specs = [a, b], [a_spec, b_spec]
    if has_add:
        ins.append(add)
        specs.append(o_spec)
    if has_e:
        ins.append(relu2_grad)
        specs.append(o_spec)
    return _call(body, rider, ins, name=name, grid=(gm, gn, nk), in_specs=specs, out_specs=o_spec,
                 out_shape=jax.ShapeDtypeStruct((M, N), out_dtype),
                 scratch_shapes=[pltpu.VMEM((tm, tn), F32)] if nk > 1 else [], sem=("parallel", "parallel", "arbitrary"))


ROW_BLOCK_BYTES = 12 * 1024 * 1024


def _row_tile(rows, row_bytes):
    return _pick(rows, max(16, min(1024, ROW_BLOCK_BYTES // row_bytes)), 16)


def _rowspec(tr, width, col=0):
    return pl.BlockSpec((tr, width), lambda i, col=col: (i, col))


def _fullspec(shape):
    nd = len(shape)
    return pl.BlockSpec(shape, lambda i, nd=nd: (0,) * nd)


def _rms(x, width):
    x = x.astype(F32)
    return lax.rsqrt(jnp.sum(x * x, axis=-1, keepdims=True) * (1.0 / width) + EPS)


def _rms_bwd_rows(x, g, dy, width):
    x, dy = x.astype(F32), dy.astype(F32)
    r = _rms(x, width)
    xh = x * r
    dn = dy * g
    dx = r * (dn - xh * (jnp.sum(dn * xh, axis=-1, keepdims=True) * (1.0 / width)))
    return dx, dy * xh


def _acc_rows(ref, val, first):
    s = jnp.sum(val, axis=0, keepdims=True)

    @pl.when(first)
    def _():
        ref[...] = s

    @pl.when(jnp.logical_not(first))
    def _():
        ref[...] += s


def _rms_fwd(x, g, name):
    rows, width = x.shape
    tr = _row_tile(rows, 6 * width)

    def body(x_ref, g_ref, o_ref):
        xv = x_ref[...]
        o_ref[...] = (xv * _rms(xv, width) * g_ref[...]).astype(BF16)

    return pl.pallas_call(
        body, name=name, grid=(rows // tr,),
        in_specs=[_rowspec(tr, width), _fullspec((1, width))], out_specs=_rowspec(tr, width),
        out_shape=jax.ShapeDtypeStruct((rows, width), BF16), compiler_params=_params(("parallel",)),
    )(x, g)


def _rms_bwd(x, g, dy, res, name, dx_dtypes=(F32,)):
    rows, width = x.shape
    tr = _row_tile(rows, 18 * width)
    has_res = res is not None
    n_in = 4 if has_res else 3

    def body(*refs):
        x_ref, g_ref, dy_ref = refs[:3]
        dx, dgv = _rms_bwd_rows(x_ref[...], g_ref[...], dy_ref[...], width)
        if has_res:
            dx = dx + refs[3][...]
        for ref, dt in zip(refs[n_in:], dx_dtypes):
            ref[...] = dx.astype(dt)
        _acc_rows(refs[-1], dgv, pl.program_id(0) == 0)

    ins = [x, g, dy] + ([res] if has_res else [])
    specs = [_rowspec(tr, width), _fullspec((1, width)), _rowspec(tr, width)] + ([_rowspec(tr, width)] if has_res else [])
    return pl.pallas_call(
        body, name=name, grid=(rows // tr,), in_specs=specs,
        out_specs=[_rowspec(tr, width)] * len(dx_dtypes) + [_fullspec((1, width))],
        out_shape=[jax.ShapeDtypeStruct((rows, width), dt) for dt in dx_dtypes] + [jax.ShapeDtypeStruct((1, width), F32)],
        compiler_params=_params(("arbitrary",)),
    )(*ins)


_GELU_C = math.sqrt(2.0 / math.pi)


def _gelu(x):
    t = jnp.tanh(_GELU_C * (x + 0.044715 * (x * x * x)))
    return 0.5 * x * (1.0 + t), t


def _gelu_grad(x, t):
    return 0.5 * (1.0 + t) + 0.5 * x * (1.0 - t * t) * (_GELU_C * (1.0 + 3.0 * 0.044715 * (x * x)))


def _gm_forward_rows(zu, zv, gln, bln, wc_ref, bst, n_chunk):
    u, tu = _gelu(zu)
    a, ta = _gelu(zv)
    mu = jnp.mean(a, axis=-1, keepdims=True)
    ac = a - mu
    rs = lax.rsqrt(jnp.mean(ac * ac, axis=-1, keepdims=True) + EPS)
    n = ac * rs
    v = n * gln + bln
    vb = v.astype(BF16)
    rows = []
    for c in range(n_chunk):
        cols = []
        for g in range(GM_GROUPS):
            vc = vb[c * GM_CHUNK:(c + 1) * GM_CHUNK, g * LANES:(g + 1) * LANES]
            mixed = jnp.dot(wc_ref[g], vc, preferred_element_type=F32) + bst[g]
            cols.append(mixed)
        rows.append(jnp.concatenate(cols, axis=1))
    mixed = jnp.concatenate(rows, axis=0) if n_chunk > 1 else rows[0]
    return u, tu, ta, n, rs, v, mixed


def _gm_fwd(z, gln, bln, wc, bst, name):
    rows = z.shape[0]
    tr = _pick(rows, 512, GM_CHUNK)
    n_chunk = tr // GM_CHUNK

    def body(zu_ref, zv_ref, gln_ref, bln_ref, wc_ref, bst_ref, o_ref):
        u, _, _, _, _, _, mixed = _gm_forward_rows(zu_ref[...].astype(F32), zv_ref[...].astype(F32), gln_ref[...], bln_ref[...], wc_ref,
                                                   bst_ref, n_chunk)
        o_ref[...] = (u * mixed).astype(BF16)

    return pl.pallas_call(
        body, name=name, grid=(rows // tr,),
        in_specs=[_rowspec(tr, GM_WIDTH, Z_GM // GM_WIDTH), _rowspec(tr, GM_WIDTH, Z_GM // GM_WIDTH + 1),_fullspec((1, GM_WIDTH)), _fullspec((1, GM_WIDTH)),
                  _fullspec((GM_GROUPS, GM_CHUNK, GM_CHUNK)), _fullspec((GM_GROUPS, GM_CHUNK, LANES))],
        out_specs=_rowspec(tr, GM_WIDTH), out_shape=jax.ShapeDtypeStruct((rows, GM_WIDTH), BF16),
        compiler_params=_params(("parallel",)),
    )(z, z, gln, bln, wc, bst)


ANY_SPEC = pl.BlockSpec(memory_space=pl.ANY)


def _gm_bwd(z, dy, gln, bln, wc, wct, bst, dz, name):
    rows = z.shape[0]
    tr = _pick(rows, 512, GM_CHUNK)
    n_chunk = tr // GM_CHUNK

    def body(zu_ref, zv_ref, dy_ref, gln_ref, bln_ref, wc_ref, wct_ref, bst_ref, _, dz_ref, dws_ref, dbs_ref, dgl_ref,
             dbl_ref):
        first = pl.program_id(0) == 0
        zu, zv, gln = zu_ref[...].astype(F32), zv_ref[...].astype(F32), gln_ref[...]
        u, tu, ta, n, rs, v, mixed = _gm_forward_rows(zu, zv, gln, bln_ref[...], wc_ref, bst_ref, n_chunk)
        dyv = dy_ref[...].astype(F32)
        dzu = dyv * mixed * _gelu_grad(zu, tu)
        dmix = dyv * u
        dmb = dmix.astype(BF16)
        vb = v.astype(BF16)
        dv_rows, dws, dbs = [], [None] * GM_GROUPS, None
        for c in range(n_chunk):
            rsl = slice(c * GM_CHUNK, (c + 1) * GM_CHUNK)
            cols = []
            for g in range(GM_GROUPS):
                csl = slice(g * LANES, (g + 1) * LANES)
                dmc = dmb[rsl, csl]
                cols.append(jnp.dot(wct_ref[g], dmc, preferred_element_type=F32))
                w_part = lax.dot_general(dmc, vb[rsl, csl], (((1,), (1,)), ((), ())), preferred_element_type=F32)
                dws[g] = w_part if dws[g] is None else dws[g] + w_part
            dv_rows.append(jnp.concatenate(cols, axis=1))
            dbs = dmix[rsl, :] if dbs is None else dbs + dmix[rsl, :]
        dv = jnp.concatenate(dv_rows, axis=0) if n_chunk > 1 else dv_rows[0]
        dn = dv * gln
        da = rs * (dn - jnp.mean(dn, axis=-1, keepdims=True) - n * jnp.mean(dn * n, axis=-1, keepdims=True))
        dzv = da * _gelu_grad(zv, ta)
        dz_ref[:, 0:GM_WIDTH] = dzu.astype(BF16)
        dz_ref[:, GM_WIDTH:2 * GM_WIDTH] = dzv.astype(BF16)
        _acc_rows(dgl_ref, dv * n, first)
        _acc_rows(dbl_ref, dv, first)

        @pl.when(first)
        def _():
            for g in range(GM_GROUPS):
                dws_ref[g] = dws[g]
            dbs_ref[...] = dbs

        @pl.when(jnp.logical_not(first))
        def _():
            for g in range(GM_GROUPS):
                dws_ref[g] += dws[g]
            dbs_ref[...] += dbs

    wspec = _fullspec((GM_GROUPS, GM_CHUNK, GM_CHUNK))
    return pl.pallas_call(
        body, name=name, grid=(rows // tr,),
        in_specs=[_rowspec(tr, GM_WIDTH, Z_GM // GM_WIDTH), _rowspec(tr, GM_WIDTH, Z_GM // GM_WIDTH + 1),
                  _rowspec(tr, GM_WIDTH), _fullspec((1, GM_WIDTH)), _fullspec((1, GM_WIDTH)), wspec, wspec, wspec, ANY_SPEC],
        out_specs=[_rowspec(tr, 2 * GM_WIDTH, Z_GM // (2 * GM_WIDTH)), wspec, _fullspec((GM_CHUNK, GM_WIDTH)),
                   _fullspec((1, GM_WIDTH)), _fullspec((1, GM_WIDTH))],
        out_shape=[jax.ShapeDtypeStruct(dz.shape, dz.dtype), jax.ShapeDtypeStruct((GM_GROUPS, GM_CHUNK, GM_CHUNK), F32),
                   jax.ShapeDtypeStruct((GM_CHUNK, GM_WIDTH), F32), jax.ShapeDtypeStruct((1, GM_WIDTH), F32),
                   jax.ShapeDtypeStruct((1, GM_WIDTH), F32)],
        input_output_aliases={8: 0}, compiler_params=_params(("arbitrary",)),
    )(z, z, dy, gln, bln, wc, wct, bst, dz)


def _lat_fwd(z, g_cq, g_ckv, name):
    rows = z.shape[0]
    tr = _row_tile(rows, 4 * MLA_W)

    def body(z_ref, gq_ref, gkv_ref, nq_ref, nkv_ref):
        zb = z_ref[...]
        cq, ckv = zb[:, 0:Q_LORA], zb[:, Q_LORA:Q_LORA + KV_LORA]
        nq_ref[...] = (cq * _rms(cq, Q_LORA) * gq_ref[...]).astype(BF16)
        nkv_ref[...] = (ckv * _rms(ckv, KV_LORA) * gkv_ref[...]).astype(BF16)

    return pl.pallas_call(
        body, name=name, grid=(rows // tr,),
        in_specs=[_rowspec(tr, MLA_W, Z_MLA // MLA_W), _fullspec((1, Q_LORA)), _fullspec((1, KV_LORA))],
        out_specs=[_rowspec(tr, Q_LORA), _rowspec(tr, KV_LORA)],
        out_shape=[jax.ShapeDtypeStruct((rows, Q_LORA), BF16), jax.ShapeDtypeStruct((rows, KV_LORA), BF16)],
        compiler_params=_params(("parallel",)),
    )(z, g_cq, g_ckv)


def _lat_bwd(z, dnq, dnkv, dkpe, g_cq, g_ckv, dz, name):
    rows = z.shape[0]
    tr = _row_tile(rows, 8 * MLA_W)

    def body(z_ref, dnq_ref, dnkv_ref, dkpe_ref, gq_ref, gkv_ref, _, dz_ref, dgq_ref, dgkv_ref):
        first = pl.program_id(0) == 0
        zb = z_ref[...]
        dcq, dgq = _rms_bwd_rows(zb[:, 0:Q_LORA], gq_ref[...], dnq_ref[...], Q_LORA)
        dckv, dgkv = _rms_bwd_rows(zb[:, Q_LORA:Q_LORA + KV_LORA], gkv_ref[...], dnkv_ref[...], KV_LORA)
        dz_ref[:, 0:Q_LORA] = dcq.astype(BF16)
        dz_ref[:, Q_LORA:Q_LORA + KV_LORA] = dckv.astype(BF16)
        dz_ref[:, Q_LORA + KV_LORA:MLA_W] = dkpe_ref[...].astype(BF16)
        _acc_rows(dgq_ref, dgq, first)
        _acc_rows(dgkv_ref, dgkv, first)

    return pl.pallas_call(
        body, name=name, grid=(rows // tr,),
        in_specs=[_rowspec(tr, MLA_W, Z_MLA // MLA_W), _rowspec(tr, Q_LORA), _rowspec(tr, KV_LORA), _rowspec(tr, LANES),
                  _fullspec((1, Q_LORA)), _fullspec((1, KV_LORA)), ANY_SPEC],
        out_specs=[_rowspec(tr, MLA_W, Z_MLA // MLA_W), _fullspec((1, Q_LORA)), _fullspec((1, KV_LORA))],
        out_shape=[jax.ShapeDtypeStruct(dz.shape, dz.dtype), jax.ShapeDtypeStruct((1, Q_LORA), F32),
                   jax.ShapeDtypeStruct((1, KV_LORA), F32)],
        input_output_aliases={6: 0}, compiler_params=_params(("arbitrary",)),
    )(z, dnq, dnkv, dkpe, g_cq, g_ckv, dz)


def _rope(y, cc, ss):
    return y * cc + pltpu.roll(y, 64, 1) * ss


def _rope_bwd(d, cc, ss):
    return d * cc + pltpu.roll(d * ss, 64, 1)


def _qk_fwd(q, kv, z, cc, ss, gqn, gqp, gkn, gkp, name):
    rows = q.shape[0]
    W = MLA_HEADS * HEAD
    tr = _row_tile(rows, 20 * W)
    QS = MLA_SCALE * LOG2E

    def body(q_ref, kv_ref, kpe_ref, cc_ref, ss_ref, gqn_ref, gqp_ref, gkn_ref, gkp_ref, qc_ref, kc_ref, v_ref):
        cc, ss = cc_ref[...], ss_ref[...]
        kpe = kpe_ref[...]
        kp = _rope(kpe * _rms(kpe, MLA_ROPE) * gkp_ref[...], cc, ss).astype(BF16)
        for h in range(MLA_HEADS):
            qn = q_ref[:, h * HEAD:(h + 1) * HEAD]
            qp = q_ref[:, W + h * HEAD:W + (h + 1) * HEAD]
            kn = kv_ref[:, h * HEAD:(h + 1) * HEAD]
            qc_ref[:, h * QCAT:h * QCAT + HEAD] = (qn * _rms(qn, HEAD) * gqn_ref[...] * QS).astype(BF16)
            qc_ref[:, h * QCAT + HEAD:(h + 1) * QCAT] = (_rope(qp * _rms(qp, MLA_ROPE) * gqp_ref[...], cc, ss) * QS).astype(BF16)
            kc_ref[:, h * QCAT:h * QCAT + HEAD] = (kn * _rms(kn, HEAD) * gkn_ref[...]).astype(BF16)
            kc_ref[:, h * QCAT + HEAD:(h + 1) * QCAT] = kp
        v_ref[...] = kv_ref[:, W:2 * W].astype(BF16)

    g = _fullspec((1, HEAD))
    return pl.pallas_call(
        body, name=name, grid=(rows // tr,),
        in_specs=[_rowspec(tr, 2 * W), _rowspec(tr, 2 * W), _rowspec(tr, LANES, Z_KPE // LANES), _rowspec(tr, LANES),
                  _rowspec(tr, LANES), g, g, g, g],
        out_specs=[_rowspec(tr, MLA_HEADS * QCAT), _rowspec(tr, MLA_HEADS * QCAT), _rowspec(tr, W)],
        out_shape=[jax.ShapeDtypeStruct((rows, MLA_HEADS * QCAT), BF16), jax.ShapeDtypeStruct((rows, MLA_HEADS * QCAT), BF16),
                   jax.ShapeDtypeStruct((rows, W), BF16)],
        compiler_params=_params(("parallel",)),
    )(q, kv, z, cc, ss, gqn, gqp, gkn, gkp)


def _qk_bwd(q, kv, z, cc, ss, gqn, gqp, gkn, gkp, dqc, dkc, dv, name):
    rows = q.shape[0]
    W = MLA_HEADS * HEAD
    tr = _row_tile(rows, 40 * W)

    def body(q_ref, kv_ref, kpe_ref, cc_ref, ss_ref, gqn_ref, gqp_ref, gkn_ref, gkp_ref, dqc_ref, dkc_ref, dv_ref,
             dq_ref, dkv_ref, dkpe_ref, dgqn_ref, dgqp_ref, dgkn_ref, dgkp_ref):
        first = pl.program_id(0) == 0
        cc, ss = cc_ref[...], ss_ref[...]
        sqn = sqp = skn = dkp = None
        for h in range(MLA_HEADS):
            dx, dg = _rms_bwd_rows(q_ref[:, h * HEAD:(h + 1) * HEAD], gqn_ref[...], dqc_ref[:, h * QCAT:h * QCAT + HEAD], HEAD)
            dq_ref[:, h * HEAD:(h + 1) * HEAD] = dx.astype(BF16)
            sqn = dg if sqn is None else sqn + dg
            dy = _rope_bwd(dqc_ref[:, h * QCAT + HEAD:(h + 1) * QCAT], cc, ss)
            dx, dg = _rms_bwd_rows(q_ref[:, W + h * HEAD:W + (h + 1) * HEAD], gqp_ref[...], dy, MLA_ROPE)
            dq_ref[:, W + h * HEAD:W + (h + 1) * HEAD] = dx.astype(BF16)
            sqp = dg if sqp is None else sqp + dg
            dx, dg = _rms_bwd_rows(kv_ref[:, h * HEAD:(h + 1) * HEAD], gkn_ref[...], dkc_ref[:, h * QCAT:h * QCAT + HEAD], HEAD)
            dkv_ref[:, h * HEAD:(h + 1) * HEAD] = dx.astype(BF16)
            skn = dg if skn is None else skn + dg
            part = dkc_ref[:, h * QCAT + HEAD:(h + 1) * QCAT].astype(F32)
            dkp = part if dkp is None else dkp + part
        dkv_ref[:, W:2 * W] = dv_ref[...].astype(BF16)
        dx, dg = _rms_bwd_rows(kpe_ref[...], gkp_ref[...], _rope_bwd(dkp, cc, ss), MLA_ROPE)
        dkpe_ref[...] = dx
        _acc_rows(dgqn_ref, sqn, first)
        _acc_rows(dgqp_ref, sqp, first)
        _acc_rows(dgkn_ref, skn, first)
        _acc_rows(dgkp_ref, dg, first)

    g = _fullspec((1, HEAD))
    gs = jax.ShapeDtypeStruct((1, HEAD), F32)
    return pl.pallas_call(
        body, name=name, grid=(rows // tr,),
        in_specs=[_rowspec(tr, 2 * W), _rowspec(tr, 2 * W), _rowspec(tr, LANES, Z_KPE // LANES), _rowspec(tr, LANES),
                  _rowspec(tr, LANES), g, g, g, g, _rowspec(tr, MLA_HEADS * QCAT), _rowspec(tr, MLA_HEADS * QCAT),
                  _rowspec(tr, W)],
        out_specs=[_rowspec(tr, 2 * W), _rowspec(tr, 2 * W), _rowspec(tr, LANES), g, g, g, g],
        out_shape=[jax.ShapeDtypeStruct((rows, 2 * W), BF16), jax.ShapeDtypeStruct((rows, 2 * W), BF16),
                   jax.ShapeDtypeStruct((rows, LANES), F32), gs, gs, gs, gs],
        compiler_params=_params(("arbitrary",)),
    )(q, kv, z, cc, ss, gqn, gqp, gkn, gkp, dqc, dkc, dv)


def _headnorm_fwd(x, col, nheads, g, out_scale, name):
    rows = x.shape[0]
    W = nheads * HEAD
    tr = _row_tile(rows, 6 * W)

    def body(x_ref, g_ref, o_ref):
        for h in range(nheads):
            xv = x_ref[:, h * HEAD:(h + 1) * HEAD]
            o_ref[:, h * HEAD:(h + 1) * HEAD] = (xv * _rms(xv, HEAD) * g_ref[...] * out_scale).astype(BF16)

    return pl.pallas_call(
        body, name=name, grid=(rows // tr,),
        in_specs=[_rowspec(tr, W, col), _fullspec((1, HEAD))], out_specs=_rowspec(tr, W),
        out_shape=jax.ShapeDtypeStruct((rows, W), BF16), compiler_params=_params(("parallel",)),
    )(x, g)


def _headnorm_bwd(x, col, nheads, g, dy, tail, name, into=None):
    rows = x.shape[0]
    W = nheads * HEAD
    tr = _row_tile(rows, 12 * W)
    has_tail = tail is not None
    WO = 2 * W if has_tail else W

    def body(*refs):
        if into is not None:
            x_ref, g_ref, dy_ref, _, dx_ref, dg_ref = refs
        elif has_tail:
            x_ref, g_ref, dy_ref, t_ref, dx_ref, dg_ref = refs
        else:
            x_ref, g_ref, dy_ref, dx_ref, dg_ref = refs
        acc = None
        for h in range(nheads):
            sl = slice(h * HEAD, (h + 1) * HEAD)
            dx, dg = _rms_bwd_rows(x_ref[:, sl], g_ref[...], dy_ref[:, sl], HEAD)
            dx_ref[:, sl] = dx.astype(BF16)
            acc = dg if acc is None else acc + dg
        if has_tail:
            dx_ref[:, W:2 * W] = t_ref[...].astype(BF16)
        _acc_rows(dg_ref, acc, pl.program_id(0) == 0)

    ins = [x, g, dy] + ([tail] if has_tail else [])
    specs = [_rowspec(tr, W, col), _fullspec((1, HEAD)), _rowspec(tr, W)] + ([_rowspec(tr, W)] if has_tail else [])
    dx_spec, dx_shape, aliases = _rowspec(tr, WO), jax.ShapeDtypeStruct((rows, WO), BF16), {}
    if into is not None:
        assert not has_tail
        ins, specs = ins + [into[0]], specs + [ANY_SPEC]
        dx_spec, dx_shape, aliases = _rowspec(tr, W, into[1]), jax.ShapeDtypeStruct(into[0].shape, into[0].dtype), {3: 0}
    return pl.pallas_call(
        body, name=name, grid=(rows // tr,), in_specs=specs,
        out_specs=[dx_spec, _fullspec((1, HEAD))], out_shape=[dx_shape, jax.ShapeDtypeStruct((1, HEAD), F32)],
        input_output_aliases=aliases, compiler_params=_params(("arbitrary",)),
    )(*ins)


def _sigmoid(x):
    return 1.0 / (1.0 + jnp.exp(-x.astype(F32)))


def _merge_fwd(z, y_gm, y_mla, y_mem, name):
    rows = z.shape[0]
    tr = _row_tile(rows, 14 * D_MODEL)

    def body(g0_ref, g1_ref, g2_ref, a_ref, b_ref, c_ref, o_ref):
        m = _sigmoid(g0_ref[...]) * a_ref[...] + _sigmoid(g1_ref[...]) * b_ref[...] + _sigmoid(g2_ref[...]) * c_ref[...]
        o_ref[...] = m.astype(BF16)

    r = _rowspec(tr, D_MODEL)
    return pl.pallas_call(
        body, name=name, grid=(rows // tr,),
        in_specs=[_rowspec(tr, D_MODEL, 0), _rowspec(tr, D_MODEL, 1), _rowspec(tr, D_MODEL, 2),r, r, r],
        out_specs=r, out_shape=jax.ShapeDtypeStruct((rows, D_MODEL), BF16), compiler_params=_params(("parallel",)),
    )(z, z, z, y_gm, y_mla, y_mem)


def _merge_bwd(z, y_gm, y_mla, y_mem, dm, name):
    rows = z.shape[0]
    tr = _row_tile(rows, 24 * D_MODEL)

    def body(g0_ref, g1_ref, g2_ref, a_ref, b_ref, c_ref, dm_ref, da_ref, db_ref, dc_ref, dzg_ref):
        dmv = dm_ref[...].astype(F32)
        for k, (g_ref, y_ref, dy_ref) in enumerate(((g0_ref, a_ref, da_ref), (g1_ref, b_ref, db_ref), (g2_ref, c_ref, dc_ref))):
            s = _sigmoid(g_ref[...])
            dy_ref[...] = (dmv * s).astype(BF16)
            dzg_ref[:, k * D_MODEL:(k + 1) * D_MODEL] = (dmv * y_ref[...] * s * (1.0 - s)).astype(BF16)

    r = _rowspec(tr, D_MODEL)
    o = jax.ShapeDtypeStruct((rows, D_MODEL), BF16)
    return pl.pallas_call(
        body, name=name, grid=(rows // tr,),
        in_specs=[_rowspec(tr, D_MODEL, 0), _rowspec(tr, D_MODEL, 1), _rowspec(tr, D_MODEL, 2),r, r, r, r],
        out_specs=[r, r, r, _rowspec(tr, 3 * D_MODEL, 0)],
        out_shape=[o, o, o, jax.ShapeDtypeStruct((rows, Z_COLS), BF16)],
        compiler_params=_params(("parallel",)),
    )(z, z, z, y_gm, y_mla, y_mem, dm)


def _loss_head(y, target, name):
    rows, width = y.shape
    tr = _row_tile(rows, 14 * width)

    def body(y_ref, t_ref, dy_ref, dyb_ref, l_ref):
        e = y_ref[...] - t_ref[...]
        dy = e * (1.0 / width)
        dy_ref[...] = dy
        dyb_ref[...] = dy.astype(BF16)
        e2 = e * e
        part = e2[:, 0:LANES]
        for k in range(1, width // LANES):
            part = part + e2[:, k * LANES:(k + 1) * LANES]
        _acc_rows(l_ref, part, pl.program_id(0) == 0)

    return pl.pallas_call(
        body, name=name, grid=(rows // tr,),
        in_specs=[_rowspec(tr, width), _rowspec(tr, width)],
        out_specs=[_rowspec(tr, width), _rowspec(tr, width), _fullspec((1, LANES))],
        out_shape=[jax.ShapeDtypeStruct((rows, width), F32), jax.ShapeDtypeStruct((rows, width), BF16),
                   jax.ShapeDtypeStruct((1, LANES), F32)],
        compiler_params=_params(("arbitrary",)),
    )(y, target)


_NT = (((1,), (1,)), ((), ()))
_TN = (((0,), (0,)), ((), ()))


def _diag_mask(s):
    row = lax.broadcasted_iota(jnp.int32, s.shape, 0)
    col = lax.broadcasted_iota(jnp.int32, s.shape, 1)
    return jnp.where(row >= col, s, NEG)


def _attn_fwd(q, k, v, nb, nheads, dk, v_col0, causal, name, rider=None):
    S, Skv = q.shape[0] // nb, k.shape[0] // nb
    tq = _pick(Skv, ATT_TILE) if causal else _pick(S, 4 * ATT_TILE)
    nq = S // tq

    def body(q_ref, k_ref, v_ref, o_ref, lse_ref):
        for i in range(nq):
            r0 = i * tq
            qb = q_ref[r0:r0 + tq, :]
            if causal:
                spans = ([(0, r0, False)] if i > 0 else []) + [(r0, r0 + tq, True)]
            else:
                spans = [(0, Skv, False)]
            scores = []
            for a, b, masked in spans:
                s = lax.dot_general(qb, k_ref[a:b, :], _NT, preferred_element_type=F32)
                scores.append(_diag_mask(s) if masked else s)
            m = functools.reduce(jnp.maximum, [jnp.max(s, axis=-1, keepdims=True) for s in scores])
            l = acc = None
            for s, (a, b, _) in zip(scores, spans):
                p = jnp.exp2(s - m)
                lp = jnp.sum(p, axis=-1, keepdims=True)
                ap = jnp.dot(p.astype(BF16), v_ref[a:b, :].astype(BF16), preferred_element_type=F32)
                l, acc = (lp, ap) if l is None else (l + lp, acc + ap)
            o_ref[r0:r0 + tq, :] = (acc / l).astype(BF16)
            lse_ref[r0:r0 + tq, :] = m + jnp.log2(l)

    ins = [q, k, v]
    in_specs = [pl.BlockSpec((S, dk), lambda b, h: (b, h)), pl.BlockSpec((Skv, dk), lambda b, h: (b, h)),
                pl.BlockSpec((Skv, HEAD), lambda b, h: (b, v_col0 + h))]
    out_specs = [pl.BlockSpec((S, HEAD), lambda b, h: (b, h)), pl.BlockSpec((None, S, 1), lambda b, h: (h, b, 0))]
    out_shape = [jax.ShapeDtypeStruct((nb * S, nheads * HEAD), BF16), jax.ShapeDtypeStruct((nheads, nb * S, 1), F32)]
    return _call(body, rider, ins, name=name, grid=(nb, nheads), in_specs=in_specs, out_specs=out_specs,
                 out_shape=out_shape, scratch_shapes=[], sem=("parallel", "parallel"))


def _attn_bwd(q, k, v, o, do, lse, nb, nheads, dk, v_col0, scale, causal, name, rider=None):
    S, Skv = q.shape[0] // nb, k.shape[0] // nb
    tk = _pick(Skv, ATT_TILE)
    nkv = Skv // tk

    def body(q_ref, k_ref, v_ref, o_ref, do_ref, lse_ref, dq_ref, dk_ref, dv_ref, delta_ref, dob_ref, dqa_ref):
        dov = do_ref[...]
        delta_ref[...] = jnp.sum(o_ref[...].astype(F32) * dov.astype(F32), axis=-1, keepdims=True)
        dob_ref[...] = dov.astype(BF16)

        for j in range(nkv):
            c0 = j * tk
            kb = k_ref[c0:c0 + tk, :]
            vb = v_ref[c0:c0 + tk, :].astype(BF16)
            if causal:
                spans = [(c0, c0 + tk, True)] + ([(c0 + tk, S, False)] if c0 + tk < S else [])
            else:
                spans = [(0, S, False)]
            dk_acc = dv_acc = None
            for a, b, masked in spans:
                qb = q_ref[a:b, :]
                dob = dob_ref[a:b, :]
                s = lax.dot_general(qb, kb, _NT, preferred_element_type=F32)
                if masked:
                    s = _diag_mask(s)
                p = jnp.exp2(s - lse_ref[a:b, :])
                dp = lax.dot_general(dob, vb, _NT, preferred_element_type=F32)
                ds = (p * (dp - delta_ref[a:b, :])).astype(BF16)
                dv_p = lax.dot_general(p.astype(BF16), dob, _TN, preferred_element_type=F32)
                dk_p = lax.dot_general(ds, qb, _TN, preferred_element_type=F32)
                dk_acc, dv_acc = (dk_p, dv_p) if dk_acc is None else (dk_acc + dk_p, dv_acc + dv_p)
                dq_p = jnp.dot(ds, kb, preferred_element_type=F32) * scale
                if j == 0:
                    dqa_ref[a:b, :] = dq_p
                else:
                    dqa_ref[a:b, :] += dq_p
            dk_ref[c0:c0 + tk, :] = (dk_acc * LN2).astype(BF16)
            dv_ref[c0:c0 + tk, :] = dv_acc.astype(BF16)
        dq_ref[...] = dqa_ref[...].astype(BF16)

    ins = [q, k, v, o, do, lse]
    in_specs = [pl.BlockSpec((S, dk), lambda b, h: (b, h)), pl.BlockSpec((Skv, dk), lambda b, h: (b, h)),
                pl.BlockSpec((Skv, HEAD), lambda b, h: (b, v_col0 + h)), pl.BlockSpec((S, HEAD), lambda b, h: (b, h)),
                pl.BlockSpec((S, HEAD), lambda b, h: (b, h)), pl.BlockSpec((None, S, 1), lambda b, h: (h, b, 0))]
    out_specs = [pl.BlockSpec((S, dk), lambda b, h: (b, h)), pl.BlockSpec((Skv, dk), lambda b, h: (b, h)),
                 pl.BlockSpec((Skv, HEAD), lambda b, h: (b, h))]
    out_shape = [jax.ShapeDtypeStruct((nb * S, nheads * dk), BF16), jax.ShapeDtypeStruct((nb * Skv, nheads * dk), BF16),
                 jax.ShapeDtypeStruct((nb * Skv, nheads * HEAD), BF16)]
    return _call(body, rider, ins, name=name, grid=(nb, nheads), in_specs=in_specs, out_specs=out_specs,
                 out_shape=out_shape,
                 scratch_shapes=[pltpu.VMEM((S, 1), F32), pltpu.VMEM((S, HEAD), BF16), pltpu.VMEM((S, dk), F32)],
                 sem=("parallel", "parallel"))


def _spread_rope(a):
    zero = jnp.zeros(a.shape[:-1] + (32,), a.dtype)
    return jnp.concatenate([a[..., :32], zero, a[..., 32:], zero], axis=-1)


def _gather_rope(a):
    return jnp.concatenate([a[..., 0:32], a[..., 64:96]], axis=-1)


def _win_layout(w):
    return jnp.concatenate([w[:, C_ZG:C_END], w[:, C_ZU:C_CQ], w[:, C_QM:C_ZG], w[:, C_CQ:C_CKV], w[:, C_CKV:C_KPE],
                            _spread_rope(w[:, C_KPE:C_QM])], axis=1)


def _win_unlayout(d):
    return jnp.concatenate([d[:, Z_GM:Z_QM], d[:, Z_MLA:Z_MLA + Q_LORA], d[:, Z_MLA + Q_LORA:Z_KPE],
                            _gather_rope(d[:, Z_KPE:Z_COLS]), d[:, Z_QM:Z_MLA], d[:, 0:Z_GM]], axis=1)


def _wuq_layout(w):
    r = w.reshape(Q_LORA, MLA_HEADS, HEAD + MLA_ROPE)
    return jnp.concatenate([r[:, :, :HEAD].reshape(Q_LORA, -1), _spread_rope(r[:, :, HEAD:]).reshape(Q_LORA, -1)], axis=1)


def _wuq_unlayout(d):
    n = d[:, :MLA_HEADS * HEAD].reshape(Q_LORA, MLA_HEADS, HEAD)
    p = _gather_rope(d[:, MLA_HEADS * HEAD:].reshape(Q_LORA, MLA_HEADS, HEAD))
    return jnp.concatenate([n, p], axis=-1).reshape(Q_LORA, -1)


def _wukv_layout(w):
    r = w.reshape(KV_LORA, MLA_HEADS, 2 * HEAD)
    return jnp.concatenate([r[:, :, :HEAD].reshape(KV_LORA, -1), r[:, :, HEAD:].reshape(KV_LORA, -1)], axis=1)


def _wukv_unlayout(d):
    k = d[:, :MLA_HEADS * HEAD].reshape(KV_LORA, MLA_HEADS, HEAD)
    v = d[:, MLA_HEADS * HEAD:].reshape(KV_LORA, MLA_HEADS, HEAD)
    return jnp.concatenate([k, v], axis=-1).reshape(KV_LORA, -1)


AG_MID = ['w_uq', 'w_ukv', 'w_mem_kv', 'w_o_gm', 'w_o_mla', 'w_o_mem', 'w_out']
AG_FFN = ['w_ff1', 'w_ff2']
RS_GROUPS = {'ff2': ['w_ff2'], 'ff1': ['w_ff1'], 'proj': ['w_out', 'w_o_gm', 'w_o_mla', 'w_o_mem'],
             'lat': ['w_uq', 'w_ukv', 'w_mem_kv'], 'in': ['w_in']}


def _unride(res, rider):
    return (res, None) if rider is None else res


def _local_step(x, mem, positions, target, P, ws):
    B, S, _ = x.shape
    M = mem.shape[1]
    T = B * S
    x2d = x.reshape(T, D_MODEL)
    mem2d = mem.reshape(B * M, D_MODEL)
    tgt2d = target.reshape(T, D_MODEL)

    def row(v):
        return v.reshape(1, -1).astype(F32)

    inv_freq = ROPE_BASE ** (-jnp.arange(0, MLA_ROPE, 2, dtype=F32) / MLA_ROPE)
    ang = positions.reshape(T).astype(F32)[:, None] * inv_freq
    cos, sin, zero = jnp.cos(ang), jnp.sin(ang), jnp.zeros_like(ang)
    cc = jnp.concatenate([cos, zero, cos, zero], axis=1)
    ss = jnp.concatenate([-sin, zero, sin, zero], axis=1)

    w_in = _win_layout(ws.first()).astype(BF16)
    g_mix, g_cq, g_ckv, g_ffn, g_mem = row(P['g_mix']), row(P['g_cq']), row(P['g_ckv']), row(P['g_ffn']), row(P['g_mem'])
    gqn, gkn, gmq, gmk = row(P['g_q_nope']), row(P['g_k_nope']), row(P['g_mq']), row(P['g_mk'])
    gqp, gkp = _spread_rope(row(P['g_q_pe'])), _spread_rope(row(P['g_k_pe']))
    gln, bln = row(P['g_gm_ln']), row(P['b_gm_ln'])
    wc = jnp.tril(P['w_spatial'].astype(F32))
    wct = jnp.swapaxes(wc, 1, 2).astype(BF16)
    wc = wc.astype(BF16)
    bst = jnp.broadcast_to(P['b_spatial'].astype(F32)[:, :, None], (GM_GROUPS, GM_CHUNK, LANES))

    h = _rms_fwd(x2d, g_mix, "rms_mix")
    ride = ws.gather(AG_MID)
    z, got = _unride(_matmul(h, w_in, 'nn', ACT, "mm_in", tn_t=768, rider=ride), ride)
    mid = ws.gathered(AG_MID, got)
    w_uq = _wuq_layout(mid['w_uq']).astype(BF16)
    w_ukv = _wukv_layout(mid['w_ukv']).astype(BF16)
    w_mem_kv, w_o_gm, w_o_mla, w_o_mem, w_out = (mid[n].astype(BF16) for n in ('w_mem_kv', 'w_o_gm', 'w_o_mla', 'w_o_mem',
                                                                                 'w_out'))
    ygm_pre = _gm_fwd(z, gln, bln, wc, bst, "gm_fwd")
    y_gm = _matmul(ygm_pre, w_o_gm, 'nn', ACT, "mm_o_gm")
    nq, nkv = _lat_fwd(z, g_cq, g_ckv, "lat_fwd")
    q = _matmul(nq, w_uq, 'nn', ACT, "mm_uq")
    kv = _matmul(nkv, w_ukv, 'nn', ACT, "mm_ukv")
    qcat, kcat, vv = _qk_fwd(q, kv, z, cc, ss, gqn, gqp, gkn, gkp, "qk_fwd")
    ride = ws.gather(AG_FFN)
    (o, lse), got = _unride(_attn_fwd(qcat, kcat, vv, B, MLA_HEADS, QCAT, 0, True, "mla_attn_fwd", rider=ride), ride)
    ffn = ws.gathered(AG_FFN, got)
    w_ff1, w_ff2 = ffn['w_ff1'].astype(BF16), ffn['w_ff2'].astype(BF16)
    y_mla = _matmul(o, w_o_mla, 'nn', ACT, "mm_o_mla")
    nm = _rms_fwd(mem2d, g_mem, "rms_mem")
    kvm = _matmul(nm, w_mem_kv, 'nn', ACT, "mm_mem_kv")
    qm = _headnorm_fwd(z, Z_QM // (MEM_HEADS * HEAD), MEM_HEADS, gmq, MEM_SCALE * LOG2E, "memq_fwd")
    km = _headnorm_fwd(kvm, 0, MEM_HEADS, gmk, 1.0, "memk_fwd")
    om, lse_m = _attn_fwd(qm, km, kvm, B, MEM_HEADS, HEAD, MEM_HEADS, False, "mem_attn_fwd")
    y_mem = _matmul(om, w_o_mem, 'nn', ACT, "mm_o_mem")
    merged = _merge_fwd(z, y_gm, y_mla, y_mem, "merge_fwd")
    x1 = _matmul(merged, w_out, 'nn', F32, "mm_out", add=x2d)
    h2 = _rms_fwd(x1, g_ffn, "rms_ffn")
    a1 = _matmul(h2, w_ff1, 'nn', BF16, "mm_ff1")
    x2 = _matmul(a1, w_ff2, 'nn', F32, "mm_ff2", add=x1, relu2_a=True)
    dx2, dx2b, loss_part = _loss_head(x2, tgt2d, "loss_head")

    G = {}
    ride = ws.scatter('ff2', {'w_ff2': _matmul(a1, dx2b, 'tn', BF16, "mm_d_ff2", relu2_a=True)})
    da1, got = _unride(_matmul(dx2b, w_ff2, 'nt', BF16, "mm_da1", relu2_grad=a1, rider=ride), ride)
    ws.scattered('ff2', got)
    ride = ws.scatter('ff1', {'w_ff1': _matmul(h2, da1, 'tn', BF16, "mm_d_ff1")})
    dh2, got = _unride(_matmul(da1, w_ff1, 'nt', ACT, "mm_dh2", rider=ride), ride)
    ws.scattered('ff1', got)
    dx1, dx1b, G['g_ffn'] = _rms_bwd(x1, g_ffn, dh2, dx2, "rms_ffn_bwd", dx_dtypes=(F32, BF16))
    d_out = _matmul(merged, dx1b, 'tn', BF16, "mm_d_out")
    dmerged = _matmul(dx1b, w_out, 'nt', ACT, "mm_dmerged")
    dy_gm, dy_mla, dy_mem, dz = _merge_bwd(z, y_gm, y_mla, y_mem, dmerged, "merge_bwd")
    d_o_gm = _matmul(ygm_pre, dy_gm, 'tn', BF16, "mm_d_o_gm")
    d_o_mla = _matmul(o, dy_mla, 'tn', BF16, "mm_d_o_mla")
    d_o_mem = _matmul(om, dy_mem, 'tn', BF16, "mm_d_o_mem")
    dygm_pre = _matmul(dy_gm, w_o_gm, 'nt', ACT, "mm_dygm")
    dz, dws, dbs, G['g_gm_ln'], G['b_gm_ln'] = _gm_bwd(z, dygm_pre, gln, bln, wc, wct, bst, dz, "gm_bwd")
    G['w_spatial'] = jnp.tril(dws)
    G['b_spatial'] = jnp.sum(dbs.reshape(GM_CHUNK, GM_GROUPS, LANES), axis=-1).T
    do = _matmul(dy_mla, w_o_mla, 'nt', ACT, "mm_do")
    ride = ws.scatter('proj', {'w_out': d_out, 'w_o_gm': d_o_gm, 'w_o_mla': d_o_mla, 'w_o_mem': d_o_mem})
    (dqc, dkc, dvv), got = _unride(_attn_bwd(qcat, kcat, vv, o, do, lse, B, MLA_HEADS, QCAT, 0, MLA_SCALE, True,
                                             "mla_attn_bwd", rider=ride), ride)
    ws.scattered('proj', got)
    dq, dkv, dkpe, G['g_q_nope'], dgqp, G['g_k_nope'], dgkp = _qk_bwd(q, kv, z, cc, ss, gqn, gqp, gkn, gkp, dqc, dkc, dvv,
                                                                     "qk_bwd")
    G['g_q_pe'], G['g_k_pe'] = _gather_rope(dgqp), _gather_rope(dgkp)
    d_uq = _wuq_unlayout(_matmul(nq, dq, 'tn', BF16, "mm_d_uq"))
    dnq = _matmul(dq, w_uq, 'nt', ACT, "mm_dnq")
    d_ukv = _wukv_unlayout(_matmul(nkv, dkv, 'tn', BF16, "mm_d_ukv"))
    dnkv = _matmul(dkv, w_ukv, 'nt', ACT, "mm_dnkv")
    dz, G['g_cq'], G['g_ckv'] = _lat_bwd(z, dnq, dnkv, dkpe, g_cq, g_ckv, dz, "lat_bwd")
    dom = _matmul(dy_mem, w_o_mem, 'nt', ACT, "mm_dom")
    dqm, dkm, dvm = _attn_bwd(qm, km, kvm, om, dom, lse_m, B, MEM_HEADS, HEAD, MEM_HEADS, MEM_SCALE, False, "mem_attn_bwd")
    dz, G['g_mq'] = _headnorm_bwd(z, Z_QM // (MEM_HEADS * HEAD), MEM_HEADS, gmq, dqm, None, "memq_bwd",
                                  into=(dz, Z_QM // (MEM_HEADS * HEAD)))
    dkvm, G['g_mk'] = _headnorm_bwd(kvm, 0, MEM_HEADS, gmk, dkm, dvm, "memk_bwd")
    d_mem_kv = _matmul(nm, dkvm, 'tn', BF16, "mm_d_mem_kv")
    dnm = _matmul(dkvm, w_mem_kv, 'nt', ACT, "mm_dnm")
    G['g_mem'], = _rms_bwd(mem2d, g_mem, dnm, None, "rms_mem_bwd", dx_dtypes=())
    ride = ws.scatter('lat', {'w_uq': d_uq, 'w_ukv': d_ukv, 'w_mem_kv': d_mem_kv})
    d_in, got = _unride(_matmul(h, dz, 'tn', BF16, "mm_d_in", tn_t=768, rider=ride), ride)
    ws.scattered('lat', got)
    ride = ws.scatter('in', {'w_in': _win_unlayout(d_in)})
    dh, got = _unride(_matmul(dz, w_in, 'nt', ACT, "mm_dh", rider=ride), ride)
    ws.scattered('in', got)
    gx, G['g_mix'] = _rms_bwd(x2d, g_mix, dh, dx1, "rms_mix_bwd")
    return loss_part, gx.reshape(B, S, D_MODEL), G


def _all_gather8(xs, name):
    def body(x_ref, out_ref, send_sems, recv_sems, local_sem):
        x, y, c = lax.axis_index("x"), lax.axis_index("y"), lax.axis_index("c")
        me, sibling = (x, y, c), (x, y, 1 - c)
        chips = [(1 - x, y), (x, 1 - y), (1 - x, 1 - y)]

        def rows(px, py, pc):
            return out_ref.at[4 * px + 2 * py + pc]

        def copy(k, block, to, src=None):
            return pltpu.make_async_remote_copy(
                src_ref=rows(*block) if src is None else src, dst_ref=rows(*block),
                send_sem=send_sems.at[k], recv_sem=recv_sems.at[k], device_id=to, device_id_type=MESH)

        mine = pltpu.make_async_copy(x_ref, rows(*me), local_sem)
        mine.start()
        first = [copy(0, me, sibling, src=x_ref)]
        first += [copy(1 + j, me, (*chip, c), src=x_ref) for j, chip in enumerate(chips)]
        for cp in first:
            cp.start()
        passed = [copy(4 + j, (*chip, c), sibling) for j, chip in enumerate(chips)]
        for j, chip in enumerate(chips):
            copy(1 + j, (*chip, c), me).wait_recv()
            passed[j].start()
        copy(0, sibling, me).wait_recv()
        for j, chip in enumerate(chips):
            copy(4 + j, (*chip, 1 - c), me).wait_recv()
        for cp in first + passed:
            cp.wait_send()
        mine.wait()

    return pl.pallas_call(
        body, name=name, in_specs=[HBM_SPEC], out_specs=HBM_SPEC,
        out_shape=jax.ShapeDtypeStruct((N_DEV,) + xs.shape, xs.dtype),
        scratch_shapes=[pltpu.SemaphoreType.DMA((7,)), pltpu.SemaphoreType.DMA((7,)), pltpu.SemaphoreType.DMA],
    )(xs)


def _adamw_rows(w, g, m, v):
    m2 = ADAM_B1 * m + (1.0 - ADAM_B1) * g
    v2 = ADAM_B2 * v + (1.0 - ADAM_B2) * (g * g)
    m_hat = m2 / (1.0 - ADAM_B1 ** ADAM_STEP)
    v_hat = v2 / (1.0 - ADAM_B2 ** ADAM_STEP)
    delta = -ADAM_LR * (m_hat / (jnp.sqrt(v_hat) + ADAM_EPS) + ADAM_WD * w)
    return delta, m2, v2


def _sum_adamw(parts, w, m, v, name):
    rows, cols = w.shape
    tr = _pick(rows, max(16, 65536 // cols), 16)
    n = parts.shape[0]

    def body(p_ref, w_ref, m_ref, v_ref, g_ref, d_ref, m2_ref, v2_ref):
        g = p_ref[0].astype(F32)
        for k in range(1, n):
            g = g + p_ref[k].astype(F32)
        delta, m2, v2 = _adamw_rows(w_ref[...], g, m_ref[...], v_ref[...])
        g_ref[...] = g
        d_ref[...] = delta
        m2_ref[...] = m2
        v2_ref[...] = v2

    flat = pl.BlockSpec((tr, cols), lambda i: (i, 0))
    out = jax.ShapeDtypeStruct((rows, cols), F32)
    return pl.pallas_call(
        body, name=name, grid=(rows // tr,),
        in_specs=[pl.BlockSpec((n, tr, cols), lambda i: (0, i, 0)), flat, flat, flat], out_specs=[flat] * 4,
        out_shape=[out] * 4, compiler_params=_params(("parallel",)),
    )(parts, w, m, v)


def _small_rows(name):
    n = {'g_mix': 1024, 'g_cq': 384, 'g_ckv': 256, 'g_q_nope': 128, 'g_q_pe': 64, 'g_k_nope': 128, 'g_k_pe': 64,
         'g_gm_ln': 512, 'b_gm_ln': 512, 'w_spatial': GM_GROUPS * GM_CHUNK * GM_CHUNK, 'b_spatial': GM_GROUPS * GM_CHUNK,
         'g_mem': 1024, 'g_mq': 128, 'g_mk': 128, 'g_ffn': 1024}[name]
    return n, -(-n // LANES)


def _small_slab(d):
    parts = []
    for name in SMALL:
        n, rows = _small_rows(name)
        parts.append(jnp.pad(d[name].reshape(-1).astype(F32), (0, rows * LANES - n)).reshape(rows, LANES))
    slab = jnp.concatenate(parts, axis=0)
    return jnp.pad(slab, ((0, -slab.shape[0] % 8), (0, 0)))


def _small_unslab(slab, like):
    out, r = {}, 0
    for name in SMALL:
        n, rows = _small_rows(name)
        out[name] = slab[r:r + rows].reshape(-1)[:n].reshape(like[name].shape)
        r += rows
    return out


def _full_from_gathered(gathered, name):
    r, c = BIG_SHAPE[name]
    if BIG_AXIS[name] == 0:
        return gathered.reshape(r, c)
    return gathered.transpose(1, 0, 2).reshape(r, c)


def _shards_of_full(g, name):
    r, c = BIG_SHAPE[name]
    if BIG_AXIS[name] == 0:
        return g.reshape(N_DEV, r // N_DEV, c)
    return g.reshape(r, N_DEV, c // N_DEV).transpose(1, 0, 2)


class _DistWeights:
    def __init__(self, shards):
        self.shards = shards
        self.received = {}

    def first(self):
        return _full_from_gathered(_all_gather8(self.shards['w_in'].astype(BF16), "ag_w_in"), 'w_in')

    def gather(self, names):
        return _Exchange([self.shards[n].astype(BF16) for n in names], scatter=False)

    def gathered(self, names, got):
        return {n: _full_from_gathered(g, n) for n, g in zip(names, got)}

    def scatter(self, key, grads):
        return _Exchange([_shards_of_full(grads[n], n) for n in RS_GROUPS[key]], scatter=True)

    def scattered(self, key, got):
        self.received.update(zip(RS_GROUPS[key], got))


def kernel(x, mem, positions, g_mix, w_in, g_cq, w_uq, g_ckv, w_ukv, g_q_nope, g_q_pe, g_k_nope, g_k_pe, g_gm_ln, b_gm_ln, w_spatial, b_spatial, g_mem, w_mem_kv, g_mq, g_mk, w_o_gm, w_o_mla, w_o_mem, w_out, g_ffn, w_ff1, w_ff2, loss_target, m_g_mix, m_w_in, m_g_cq, m_w_uq, m_g_ckv, m_w_ukv, m_g_q_nope, m_g_q_pe, m_g_k_nope, m_g_k_pe, m_g_gm_ln, m_b_gm_ln, m_w_spatial, m_b_spatial, m_g_mem, m_w_mem_kv, m_g_mq, m_g_mk, m_w_o_gm, m_w_o_mla, m_w_o_mem, m_w_out, m_g_ffn, m_w_ff1, m_w_ff2, v_g_mix, v_w_in, v_g_cq, v_w_uq, v_g_ckv, v_w_ukv, v_g_q_nope, v_g_q_pe, v_g_k_nope, v_g_k_pe, v_g_gm_ln, v_b_gm_ln, v_w_spatial, v_b_spatial, v_g_mem, v_w_mem_kv, v_g_mq, v_g_mk, v_w_o_gm, v_w_o_mla, v_w_o_mem, v_w_out, v_g_ffn, v_w_ff1, v_w_ff2):
    given = dict(locals())
    w = {n: given[n][0] for n in WEIGHTS}
    mom = {n: given['m_' + n][0] for n in WEIGHTS}
    var = {n: given['v_' + n][0] for n in WEIGHTS}

    ws = _DistWeights({n: w[n] for n in BIG})
    loss_part, grad_x, G = _local_step(x, mem, positions, loss_target, {n: w[n] for n in SMALL}, ws)
    loss = lax.psum(0.5 * jnp.sum(loss_part) / D_MODEL, ("x", "y", "c"))

    outs = {}
    for n in BIG:
        for prefix, res in zip(("grad_", "delta_", "new_m_", "new_v_"),
                               _sum_adamw(ws.received[n], w[n], mom[n], var[n], "adamw_" + n)):
            outs[prefix + n] = res[None]

    parts = _all_gather8(_small_slab(G), "ag_small")
    small = _sum_adamw(parts, _small_slab(w), _small_slab(mom), _small_slab(var), "adamw_small")
    for prefix, small_slab in zip(("grad_", "delta_", "new_m_", "new_v_"), small):
        sm = _small_unslab(small_slab, given)
        for n in SMALL:
            outs[prefix + n] = sm[n]
    return (loss, grad_x, *[outs[p + n] for p in ("grad_", "delta_", "new_m_", "new_v_") for n in WEIGHTS])
```

```python
import functools
import math

import jax
import jax.numpy as jnp
from jax import lax
from jax.experimental import pallas as pl
from jax.experimental.pallas import tpu as pltpu

F32 = jnp.float32
BF16 = jnp.bfloat16
ACT = BF16

D_MODEL = 1024
MEM_HEADS = 4
HEAD = 128
GM_WIDTH = 512
GM_CHUNK = 128
GM_GROUPS = 4
MLA_HEADS = 8
MLA_ROPE = 64
Q_LORA = 384
KV_LORA = 256
D_FF = 4096
EPS = 1e-6
ROPE_BASE = 10000.0
MLA_SCALE = 1.0 / math.sqrt(HEAD + MLA_ROPE)
MEM_SCALE = 1.0 / math.sqrt(HEAD)
LOG2E = 1.4426950408889634
LN2 = 0.6931471805599453
ATT_TILE = 256
C_ZU, C_ZV, C_CQ, C_CKV, C_KPE, C_QM, C_ZG, C_END = 0, 512, 1024, 1408, 1664, 1728, 2240, 5312
Z_GM, Z_QM, Z_MLA, Z_KPE, Z_COLS = 3072, 4096, 4608, 5248, 5376
MLA_W = 768
QCAT = 2 * HEAD
ADAM_LR, ADAM_B1, ADAM_B2, ADAM_EPS, ADAM_WD, ADAM_STEP = 0.001, 0.9, 0.999, 1e-08, 0.01, 10
N_DEV = 8
LANES = 128
VMEM_LIMIT = 48 * 1024 * 1024
MAX_K_TILE = 8192
NEG = -1e30

BIG = ['w_in', 'w_uq', 'w_ukv', 'w_mem_kv', 'w_o_gm', 'w_o_mla', 'w_o_mem', 'w_out', 'w_ff1', 'w_ff2']
BIG_AXIS = {'w_in': 1, 'w_uq': 1, 'w_ukv': 1, 'w_mem_kv': 0, 'w_o_gm': 1, 'w_o_mla': 0, 'w_o_mem': 1,
            'w_out': 0, 'w_ff1': 1, 'w_ff2': 0}
BIG_SHAPE = {'w_in': (1024, 5312), 'w_uq': (384, 1536), 'w_ukv': (256, 2048), 'w_mem_kv': (1024, 1024),
             'w_o_gm': (512, 1024), 'w_o_mla': (1024, 1024), 'w_o_mem': (512, 1024), 'w_out': (1024, 1024),
             'w_ff1': (1024, 4096), 'w_ff2': (4096, 1024)}
SMALL = ['g_mix', 'g_cq', 'g_ckv', 'g_q_nope', 'g_q_pe', 'g_k_nope', 'g_k_pe', 'g_gm_ln', 'b_gm_ln',
         'w_spatial', 'b_spatial', 'g_mem', 'g_mq', 'g_mk', 'g_ffn']
WEIGHTS = ['g_mix', 'w_in', 'g_cq', 'w_uq', 'g_ckv', 'w_ukv', 'g_q_nope', 'g_q_pe', 'g_k_nope', 'g_k_pe',
           'g_gm_ln', 'b_gm_ln', 'w_spatial', 'b_spatial', 'g_mem', 'w_mem_kv', 'g_mq', 'g_mk', 'w_o_gm',
           'w_o_mla', 'w_o_mem', 'w_out', 'g_ffn', 'w_ff1', 'w_ff2']


def _pick(n, target, mult=LANES):
    best = None
    t = mult
    while t <= min(n, target):
        if n % t == 0:
            best = t
        t += mult
    return best if best is not None else n


def _params(sem):
    return pltpu.CompilerParams(dimension_semantics=sem, vmem_limit_bytes=VMEM_LIMIT)


MESH = pl.DeviceIdType.MESH
HBM_SPEC = pl.BlockSpec(memory_space=pltpu.HBM)


class _Exchange:
    def __init__(self, srcs, scatter, rows=None, into=None):
        self.srcs, self.scatter, self.rows, self.into = list(srcs), scatter, rows, into
        self.out_shapes = [jax.ShapeDtypeStruct(s.shape if scatter else (N_DEV,) + s.shape, s.dtype) for s in self.srcs]
        n = len(self.srcs)
        self.scratch = [pltpu.SemaphoreType.DMA((n, N_DEV - 1)), pltpu.SemaphoreType.DMA((n, N_DEV - 1)),
                        pltpu.SemaphoreType.DMA((n,))]

    def _copies(self, src_refs, dst_refs, send_sems, recv_sems, local_sems):
        x, y, c = lax.axis_index("x"), lax.axis_index("y"), lax.axis_index("c")
        me = 4 * x + 2 * y + c
        local, remote = [], []
        for a, (src_ref, dst_ref) in enumerate(zip(src_refs, dst_refs)):
            def block(ref, dev):
                return ref.at[dev] if self.rows is None else ref.at[dev, pl.ds(self.rows[0], self.rows[1])]

            def mine_for(dev, src_ref=src_ref):
                return block(src_ref, dev) if self.scatter else src_ref

            local.append(pltpu.make_async_copy(mine_for(me), block(dst_ref, me), local_sems.at[a]))
            for k in range(1, N_DEV):
                px = 1 - x if k & 4 else x
                py = 1 - y if k & 2 else y
                pc = 1 - c if k & 1 else c
                remote.append(pltpu.make_async_remote_copy(
                    src_ref=mine_for(4 * px + 2 * py + pc), dst_ref=block(dst_ref, me), send_sem=send_sems.at[a, k - 1],
                    recv_sem=recv_sems.at[a, k - 1], device_id=(px, py, pc), device_id_type=MESH))
        return local, remote

    def start(self, *refs):
        local, remote = self._copies(*refs)
        for cp in local + remote:
            cp.start()

    def wait(self, *refs):
        local, remote = self._copies(*refs)
        for cp in remote + local:
            cp.wait()


class _Gather2:
    into = None

    def __init__(self, srcs):
        self.srcs = list(srcs)
        self.out_shapes = [jax.ShapeDtypeStruct((N_DEV,) + s.shape, s.dtype) for s in self.srcs]
        n = len(self.srcs)
        self.scratch = [pltpu.SemaphoreType.DMA((n, N_DEV - 1)), pltpu.SemaphoreType.DMA((n, N_DEV - 1)),
                        pltpu.SemaphoreType.DMA((n,))]

    def _plan(self, src_refs, dst_refs, send_sems, recv_sems, local_sems):
        x, y, c = lax.axis_index("x"), lax.axis_index("y"), lax.axis_index("c")
        chips = [(1 - x, y), (x, 1 - y), (1 - x, 1 - y)]
        plans = []
        for a, (src_ref, dst_ref) in enumerate(zip(src_refs, dst_refs)):
            def copy(k, block, to, src=None, a=a, dst_ref=dst_ref):
                at = dst_ref.at[4 * block[0] + 2 * block[1] + block[2]]
                return pltpu.make_async_remote_copy(src_ref=at if src is None else src, dst_ref=at,
                                                    send_sem=send_sems.at[a, k], recv_sem=recv_sems.at[a, k],
                                                    device_id=to, device_id_type=MESH)

            local = pltpu.make_async_copy(src_ref, dst_ref.at[4 * x + 2 * y + c], local_sems.at[a])
            first = [copy(0, (x, y, c), (x, y, 1 - c), src=src_ref)]
            first += [copy(1 + j, (x, y, c), (*chip, c), src=src_ref) for j, chip in enumerate(chips)]
            passed = [copy(4 + j, (*chip, c), (x, y, 1 - c)) for j, chip in enumerate(chips)]
            arrivals = [copy(1 + j, (*chip, c), (x, y, c)) for j, chip in enumerate(chips)]
            late = [copy(0, (x, y, 1 - c), (x, y, c))] + [copy(4 + j, (*chip, 1 - c), (x, y, c)) for j, chip in enumerate(chips)]
            plans.append((local, first, passed, arrivals, late))
        return plans

    def start(self, *refs):
        for local, first, _, _, _ in self._plan(*refs):
            local.start()
            for cp in first:
                cp.start()

    def wait(self, *refs):
        plans = self._plan(*refs)
        for _, _, passed, arrivals, _ in plans:
            for arrived, onward in zip(arrivals, passed):
                arrived.wait_recv()
                onward.start()
        for local, first, passed, _, late in plans:
            for cp in late:
                cp.wait_recv()
            for cp in first + passed:
                cp.wait_send()
            local.wait()


def _call(body, rider, ins, *, name, grid, in_specs, out_specs, out_shape, scratch_shapes, sem):
    if rider is None:
        return pl.pallas_call(body, name=name, grid=grid, in_specs=in_specs, out_specs=out_specs, out_shape=out_shape,
                              scratch_shapes=scratch_shapes, compiler_params=_params(sem))(*ins)
    single = not isinstance(out_shape, (list, tuple))
    own_specs, own_shapes = ([out_specs], [out_shape]) if single else (list(out_specs), list(out_shape))
    n_in, n_out, n_sc, n_r = len(ins), len(own_shapes), len(scratch_shapes), len(rider.srcs)
    into = list(rider.into) if rider.into is not None else []
    n_all_in = n_in + n_r + len(into)

    def carrying(*refs):
        own_in, srcs = refs[:n_in], refs[n_in:n_in + n_r]
        own_out, dsts = refs[n_all_in:n_all_in + n_out], refs[n_all_in + n_out:n_all_in + n_out + n_r]
        own_sc = refs[n_all_in + n_out + n_r:n_all_in + n_out + n_r + n_sc]
        sems = refs[n_all_in + n_out + n_r + n_sc:]
        first = last = None
        for d, steps in enumerate(grid):
            f, l = pl.program_id(d) == 0, pl.program_id(d) == steps - 1
            first, last = (f, l) if first is None else (first & f, last & l)

        @pl.when(first)
        def _():
            rider.start(srcs, dsts, *sems)

        body(*own_in, *own_out, *own_sc)

        @pl.when(last)
        def _():
            rider.wait(srcs, dsts, *sems)

    res = pl.pallas_call(
        carrying, name=name, grid=grid, in_specs=list(in_specs) + [HBM_SPEC] * (n_r + len(into)),
        out_specs=own_specs + [HBM_SPEC] * n_r, out_shape=own_shapes + rider.out_shapes,
        scratch_shapes=list(scratch_shapes) + rider.scratch,
        input_output_aliases={n_in + n_r + a: n_out + a for a in range(len(into))},
        compiler_params=_params(("arbitrary",) * len(grid)),
    )(*ins, *rider.srcs, *into)
    own = res[:n_out]
    return (own[0] if single else list(own)), list(res[n_out:])


def _matmul(a, b, mode, out_dtype, name, add=None, relu2_a=False, relu2_grad=None,
            tm_t=None, tn_t=None, tk_t=None, rider=None):
    if mode == 'nn':
        (M, K), (K2, N) = a.shape, b.shape
    elif mode == 'nt':
        (M, K), (N, K2) = a.shape, b.shape
    else:
        (K, M), (K2, N) = a.shape, b.shape
    assert K == K2, (name, a.shape, b.shape)
    if mode == 'tn':
        d_tm, d_tn, d_tk = 1024, 1024, 2048
    else:
        d_tm, d_tn, d_tk = (2048 if K <= 1024 else 1024), 512, MAX_K_TILE
    tm, tn, tk = _pick(M, tm_t or d_tm), _pick(N, tn_t or d_tn), _pick(K, tk_t or d_tk)
    gm, gn, nk = M // tm, N // tn, K // tk
    if mode == 'nn':
        a_spec = pl.BlockSpec((tm, tk), lambda i, j, k: (i, k))
        b_spec = pl.BlockSpec((tk, tn), lambda i, j, k: (k, j))
        dims = (((1,), (0,)), ((), ()))
    elif mode == 'nt':
        a_spec = pl.BlockSpec((tm, tk), lambda i, j, k: (i, k))
        b_spec = pl.BlockSpec((tn, tk), lambda i, j, k: (j, k))
        dims = (((1,), (1,)), ((), ()))
    else:
        a_spec = pl.BlockSpec((tk, tm), lambda i, j, k: (k, i))
        b_spec = pl.BlockSpec((tk, tn), lambda i, j, k: (k, j))
        dims = (((0,), (0,)), ((), ()))
    o_spec = pl.BlockSpec((tm, tn), lambda i, j, k: (i, j))
    has_add, has_e = add is not None, relu2_grad is not None

    def body(*refs):
        a_ref, b_ref = refs[0], refs[1]
        pos = 2
        add_ref = e_ref = None
        if has_add:
            add_ref = refs[pos]
            pos += 1
        if has_e:
            e_ref = refs[pos]
            pos += 1
        o_ref = refs[pos]
        acc_ref = refs[pos + 1] if nk > 1 else None

        av = a_ref[...]
        if relu2_a:
            av = jnp.maximum(av, 0)
            av = av * av
        prod = lax.dot_general(av.astype(BF16), b_ref[...].astype(BF16), dims, preferred_element_type=F32)

        def finish(r):
            if has_add:
                r = r + add_ref[...]
            if has_e:
                r = r * (2.0 * jnp.maximum(e_ref[...].astype(F32), 0.0))
            o_ref[...] = r.astype(out_dtype)

        if nk == 1:
            finish(prod)
        else:
            k = pl.program_id(2)

            @pl.when(k == 0)
            def _():
                acc_ref[...] = prod

            @pl.when(k > 0)
            def _():
                acc_ref[...] += prod

            @pl.when(k == nk - 1)
            def _():
                finish(acc_ref[...])

    ins, specs = [a, b], [a_spec, b_spec]
    if has_add:
        ins.append(add)
        specs.append(o_spec)
    if has_e:
        ins.append(relu2_grad)
        specs.append(o_spec)
    return _call(body, rider, ins, name=name, grid=(gm, gn, nk), in_specs=specs, out_specs=o_spec,
                 out_shape=jax.ShapeDtypeStruct((M, N), out_dtype),
                 scratch_shapes=[pltpu.VMEM((tm, tn), F32)] if nk > 1 else [], sem=("parallel", "parallel", "arbitrary"))


ROW_BLOCK_BYTES = 12 * 1024 * 1024


def _row_tile(rows, row_bytes):
    return _pick(rows, max(16, min(1024, ROW_BLOCK_BYTES // row_bytes)), 16)


def _rowspec(tr, width, col=0):
    return pl.BlockSpec((tr, width), lambda i, col=col: (i, col))


def _fullspec(shape):
    nd = len(shape)
    return pl.BlockSpec(shape, lambda i, nd=nd: (0,) * nd)


def _rms(x, width):
    x = x.astype(F32)
    return lax.rsqrt(jnp.sum(x * x, axis=-1, keepdims=True) * (1.0 / width) + EPS)


def _rms_bwd_rows(x, g, dy, width):
    x, dy = x.astype(F32), dy.astype(F32)
    r = _rms(x, width)
    xh = x * r
    dn = dy * g
    dx = r * (dn - xh * (jnp.sum(dn * xh, axis=-1, keepdims=True) * (1.0 / width)))
    return dx, dy * xh


def _acc_rows(ref, val, first):
    s = jnp.sum(val, axis=0, keepdims=True)

    @pl.when(first)
    def _():
        ref[...] = s

    @pl.when(jnp.logical_not(first))
    def _():
        ref[...] += s


def _rms_fwd(x, g, name):
    rows, width = x.shape
    tr = _row_tile(rows, 6 * width)

    def body(x_ref, g_ref, o_ref):
        xv = x_ref[...]
        o_ref[...] = (xv * _rms(xv, width) * g_ref[...]).astype(BF16)

    return pl.pallas_call(
        body, name=name, grid=(rows // tr,),
        in_specs=[_rowspec(tr, width), _fullspec((1, width))], out_specs=_rowspec(tr, width),
        out_shape=jax.ShapeDtypeStruct((rows, width), BF16), compiler_params=_params(("parallel",)),
    )(x, g)


def _rms_bwd(x, g, dy, res, name, dx_dtypes=(F32,), rider=None):
    rows, width = x.shape
    tr = _row_tile(rows, 18 * width)
    has_res = res is not None
    n_in = 4 if has_res else 3

    def body(*refs):
        x_ref, g_ref, dy_ref = refs[:3]
        dx, dgv = _rms_bwd_rows(x_ref[...], g_ref[...], dy_ref[...], width)
        if has_res:
            dx = dx + refs[3][...]
        for ref, dt in zip(refs[n_in:], dx_dtypes):
            ref[...] = dx.astype(dt)
        _acc_rows(refs[-1], dgv, pl.program_id(0) == 0)

    ins = [x, g, dy] + ([res] if has_res else [])
    specs = [_rowspec(tr, width), _fullspec((1, width)), _rowspec(tr, width)] + ([_rowspec(tr, width)] if has_res else [])
    return _call(
        body, rider, ins, name=name, grid=(rows // tr,), in_specs=specs,
        out_specs=[_rowspec(tr, width)] * len(dx_dtypes) + [_fullspec((1, width))],
        out_shape=[jax.ShapeDtypeStruct((rows, width), dt) for dt in dx_dtypes] + [jax.ShapeDtypeStruct((1, width), F32)],
        scratch_shapes=[], sem=("arbitrary",))


_GELU_C = math.sqrt(2.0 / math.pi)


def _gelu(x):
    t = jnp.tanh(_GELU_C * (x + 0.044715 * (x * x * x)))
    return 0.5 * x * (1.0 + t), t


def _gelu_grad(x, t):
    return 0.5 * (1.0 + t) + 0.5 * x * (1.0 - t * t) * (_GELU_C * (1.0 + 3.0 * 0.044715 * (x * x)))


def _gm_forward_rows(zu, zv, gln, bln, wc_ref, bst, n_chunk):
    u, tu = _gelu(zu)
    a, ta = _gelu(zv)
    mu = jnp.mean(a, axis=-1, keepdims=True)
    ac = a - mu
    rs = lax.rsqrt(jnp.mean(ac * ac, axis=-1, keepdims=True) + EPS)
    n = ac * rs
    v = n * gln + bln
    vb = v.astype(BF16)
    rows = []
    for c in range(n_chunk):
        cols = []
        for g in range(GM_GROUPS):
            vc = vb[c * GM_CHUNK:(c + 1) * GM_CHUNK, g * LANES:(g + 1) * LANES]
            mixed = jnp.dot(wc_ref[g], vc, preferred_element_type=F32) + bst[g]
            cols.append(mixed)
        rows.append(jnp.concatenate(cols, axis=1))
    mixed = jnp.concatenate(rows, axis=0) if n_chunk > 1 else rows[0]
    return u, tu, ta, n, rs, v, mixed


def _gm_fwd(z, gln, bln, wc, bst, name):
    rows = z.shape[0]
    tr = _pick(rows, 512, GM_CHUNK)
    n_chunk = tr // GM_CHUNK

    def body(zu_ref, zv_ref, gln_ref, bln_ref, wc_ref, bst_ref, o_ref):
        u, _, _, _, _, _, mixed = _gm_forward_rows(zu_ref[...].astype(F32), zv_ref[...].astype(F32), gln_ref[...], bln_ref[...], wc_ref,
                                                   bst_ref, n_chunk)
        o_ref[...] = (u * mixed).astype(BF16)

    return pl.pallas_call(
        body, name=name, grid=(rows // tr,),
        in_specs=[_rowspec(tr, GM_WIDTH, Z_GM // GM_WIDTH), _rowspec(tr, GM_WIDTH, Z_GM // GM_WIDTH + 1),_fullspec((1, GM_WIDTH)), _fullspec((1, GM_WIDTH)),
                  _fullspec((GM_GROUPS, GM_CHUNK, GM_CHUNK)), _fullspec((GM_GROUPS, GM_CHUNK, LANES))],
        out_specs=_rowspec(tr, GM_WIDTH), out_shape=jax.ShapeDtypeStruct((rows, GM_WIDTH), BF16),
        compiler_params=_params(("parallel",)),
    )(z, z, gln, bln, wc, bst)


ANY_SPEC = pl.BlockSpec(memory_space=pl.ANY)


def _gm_bwd(z, dy, gln, bln, wc, wct, bst, dz, name):
    rows = z.shape[0]
    tr = _pick(rows, 512, GM_CHUNK)
    n_chunk = tr // GM_CHUNK

    def body(zu_ref, zv_ref, dy_ref, gln_ref, bln_ref, wc_ref, wct_ref, bst_ref, _, dz_ref, dws_ref, dbs_ref, dgl_ref,
             dbl_ref):
        first = pl.program_id(0) == 0
        zu, zv, gln = zu_ref[...].astype(F32), zv_ref[...].astype(F32), gln_ref[...]
        u, tu, ta, n, rs, v, mixed = _gm_forward_rows(zu, zv, gln, bln_ref[...], wc_ref, bst_ref, n_chunk)
        dyv = dy_ref[...].astype(F32)
        dzu = dyv * mixed * _gelu_grad(zu, tu)
        dmix = dyv * u
        dmb = dmix.astype(BF16)
        vb = v.astype(BF16)
        dv_rows, dws, dbs = [], [None] * GM_GROUPS, None
        for c in range(n_chunk):
            rsl = slice(c * GM_CHUNK, (c + 1) * GM_CHUNK)
            cols = []
            for g in range(GM_GROUPS):
                csl = slice(g * LANES, (g + 1) * LANES)
                dmc = dmb[rsl, csl]
                cols.append(jnp.dot(wct_ref[g], dmc, preferred_element_type=F32))
                w_part = lax.dot_general(dmc, vb[rsl, csl], (((1,), (1,)), ((), ())), preferred_element_type=F32)
                dws[g] = w_part if dws[g] is None else dws[g] + w_part
            dv_rows.append(jnp.concatenate(cols, axis=1))
            dbs = dmix[rsl, :] if dbs is None else dbs + dmix[rsl, :]
        dv = jnp.concatenate(dv_rows, axis=0) if n_chunk > 1 else dv_rows[0]
        dn = dv * gln
        da = rs * (dn - jnp.mean(dn, axis=-1, keepdims=True) - n * jnp.mean(dn * n, axis=-1, keepdims=True))
        dzv = da * _gelu_grad(zv, ta)
        dz_ref[:, 0:GM_WIDTH] = dzu.astype(BF16)
        dz_ref[:, GM_WIDTH:2 * GM_WIDTH] = dzv.astype(BF16)
        _acc_rows(dgl_ref, dv * n, first)
        _acc_rows(dbl_ref, dv, first)

        @pl.when(first)
        def _():
            for g in range(GM_GROUPS):
                dws_ref[g] = dws[g]
            dbs_ref[...] = dbs

        @pl.when(jnp.logical_not(first))
        def _():
            for g in range(GM_GROUPS):
                dws_ref[g] += dws[g]
            dbs_ref[...] += dbs

    wspec = _fullspec((GM_GROUPS, GM_CHUNK, GM_CHUNK))
    return pl.pallas_call(
        body, name=name, grid=(rows // tr,),
        in_specs=[_rowspec(tr, GM_WIDTH, Z_GM // GM_WIDTH), _rowspec(tr, GM_WIDTH, Z_GM // GM_WIDTH + 1),
                  _rowspec(tr, GM_WIDTH), _fullspec((1, GM_WIDTH)), _fullspec((1, GM_WIDTH)), wspec, wspec, wspec, ANY_SPEC],
        out_specs=[_rowspec(tr, 2 * GM_WIDTH, Z_GM // (2 * GM_WIDTH)), wspec, _fullspec((GM_CHUNK, GM_WIDTH)),
                   _fullspec((1, GM_WIDTH)), _fullspec((1, GM_WIDTH))],
        out_shape=[jax.ShapeDtypeStruct(dz.shape, dz.dtype), jax.ShapeDtypeStruct((GM_GROUPS, GM_CHUNK, GM_CHUNK), F32),
                   jax.ShapeDtypeStruct((GM_CHUNK, GM_WIDTH), F32), jax.ShapeDtypeStruct((1, GM_WIDTH), F32),
                   jax.ShapeDtypeStruct((1, GM_WIDTH), F32)],
        input_output_aliases={8: 0}, compiler_params=_params(("arbitrary",)),
    )(z, z, dy, gln, bln, wc, wct, bst, dz)


def _lat_fwd(z, g_cq, g_ckv, name):
    rows = z.shape[0]
    tr = _row_tile(rows, 4 * MLA_W)

    def body(z_ref, gq_ref, gkv_ref, nq_ref, nkv_ref):
        zb = z_ref[...]
        cq, ckv = zb[:, 0:Q_LORA], zb[:, Q_LORA:Q_LORA + KV_LORA]
        nq_ref[...] = (cq * _rms(cq, Q_LORA) * gq_ref[...]).astype(BF16)
        nkv_ref[...] = (ckv * _rms(ckv, KV_LORA) * gkv_ref[...]).astype(BF16)

    return pl.pallas_call(
        body, name=name, grid=(rows // tr,),
        in_specs=[_rowspec(tr, MLA_W, Z_MLA // MLA_W), _fullspec((1, Q_LORA)), _fullspec((1, KV_LORA))],
        out_specs=[_rowspec(tr, Q_LORA), _rowspec(tr, KV_LORA)],
        out_shape=[jax.ShapeDtypeStruct((rows, Q_LORA), BF16), jax.ShapeDtypeStruct((rows, KV_LORA), BF16)],
        compiler_params=_params(("parallel",)),
    )(z, g_cq, g_ckv)


def _lat_bwd(z, dnq, dnkv, dkpe, g_cq, g_ckv, dz, name):
    rows = z.shape[0]
    tr = _row_tile(rows, 8 * MLA_W)

    def body(z_ref, dnq_ref, dnkv_ref, dkpe_ref, gq_ref, gkv_ref, _, dz_ref, dgq_ref, dgkv_ref):
        first = pl.program_id(0) == 0
        zb = z_ref[...]
        dcq, dgq = _rms_bwd_rows(zb[:, 0:Q_LORA], gq_ref[...], dnq_ref[...], Q_LORA)
        dckv, dgkv = _rms_bwd_rows(zb[:, Q_LORA:Q_LORA + KV_LORA], gkv_ref[...], dnkv_ref[...], KV_LORA)
        dz_ref[:, 0:Q_LORA] = dcq.astype(BF16)
        dz_ref[:, Q_LORA:Q_LORA + KV_LORA] = dckv.astype(BF16)
        dz_ref[:, Q_LORA + KV_LORA:MLA_W] = dkpe_ref[...].astype(BF16)
        _acc_rows(dgq_ref, dgq, first)
        _acc_rows(dgkv_ref, dgkv, first)

    return pl.pallas_call(
        body, name=name, grid=(rows // tr,),
        in_specs=[_rowspec(tr, MLA_W, Z_MLA // MLA_W), _rowspec(tr, Q_LORA), _rowspec(tr, KV_LORA), _rowspec(tr, LANES),
                  _fullspec((1, Q_LORA)), _fullspec((1, KV_LORA)), ANY_SPEC],
        out_specs=[_rowspec(tr, MLA_W, Z_MLA // MLA_W), _fullspec((1, Q_LORA)), _fullspec((1, KV_LORA))],
        out_shape=[jax.ShapeDtypeStruct(dz.shape, dz.dtype), jax.ShapeDtypeStruct((1, Q_LORA), F32),
                   jax.ShapeDtypeStruct((1, KV_LORA), F32)],
        input_output_aliases={6: 0}, compiler_params=_params(("arbitrary",)),
    )(z, dnq, dnkv, dkpe, g_cq, g_ckv, dz)


def _rope(y, cc, ss):
    return y * cc + pltpu.roll(y, 64, 1) * ss


def _rope_bwd(d, cc, ss):
    return d * cc + pltpu.roll(d * ss, 64, 1)


def _qk_fwd(q, kv, z, cc, ss, gqn, gqp, gkn, gkp, name):
    rows = q.shape[0]
    W = MLA_HEADS * HEAD
    tr = _row_tile(rows, 20 * W)
    QS = MLA_SCALE * LOG2E

    def body(q_ref, kv_ref, kpe_ref, cc_ref, ss_ref, gqn_ref, gqp_ref, gkn_ref, gkp_ref, qc_ref, kc_ref, v_ref):
        cc, ss = cc_ref[...], ss_ref[...]
        kpe = kpe_ref[...]
        kp = _rope(kpe * _rms(kpe, MLA_ROPE) * gkp_ref[...], cc, ss).astype(BF16)
        for h in range(MLA_HEADS):
            qn = q_ref[:, h * HEAD:(h + 1) * HEAD]
            qp = q_ref[:, W + h * HEAD:W + (h + 1) * HEAD]
            kn = kv_ref[:, h * HEAD:(h + 1) * HEAD]
            qc_ref[:, h * QCAT:h * QCAT + HEAD] = (qn * _rms(qn, HEAD) * gqn_ref[...] * QS).astype(BF16)
            qc_ref[:, h * QCAT + HEAD:(h + 1) * QCAT] = (_rope(qp * _rms(qp, MLA_ROPE) * gqp_ref[...], cc, ss) * QS).astype(BF16)
            kc_ref[:, h * QCAT:h * QCAT + HEAD] = (kn * _rms(kn, HEAD) * gkn_ref[...]).astype(BF16)
            kc_ref[:, h * QCAT + HEAD:(h + 1) * QCAT] = kp
        v_ref[...] = kv_ref[:, W:2 * W].astype(BF16)

    g = _fullspec((1, HEAD))
    return pl.pallas_call(
        body, name=name, grid=(rows // tr,),
        in_specs=[_rowspec(tr, 2 * W), _rowspec(tr, 2 * W), _rowspec(tr, LANES, Z_KPE // LANES), _rowspec(tr, LANES),
                  _rowspec(tr, LANES), g, g, g, g],
        out_specs=[_rowspec(tr, MLA_HEADS * QCAT), _rowspec(tr, MLA_HEADS * QCAT), _rowspec(tr, W)],
        out_shape=[jax.ShapeDtypeStruct((rows, MLA_HEADS * QCAT), BF16), jax.ShapeDtypeStruct((rows, MLA_HEADS * QCAT), BF16),
                   jax.ShapeDtypeStruct((rows, W), BF16)],
        compiler_params=_params(("parallel",)),
    )(q, kv, z, cc, ss, gqn, gqp, gkn, gkp)


def _qk_bwd(q, kv, z, cc, ss, gqn, gqp, gkn, gkp, dqc, dkc, dv, name):
    rows = q.shape[0]
    W = MLA_HEADS * HEAD
    tr = _row_tile(rows, 40 * W)

    def body(q_ref, kv_ref, kpe_ref, cc_ref, ss_ref, gqn_ref, gqp_ref, gkn_ref, gkp_ref, dqc_ref, dkc_ref, dv_ref,
             dq_ref, dkv_ref, dkpe_ref, dgqn_ref, dgqp_ref, dgkn_ref, dgkp_ref):
        first = pl.program_id(0) == 0
        cc, ss = cc_ref[...], ss_ref[...]
        sqn = sqp = skn = dkp = None
        for h in range(MLA_HEADS):
            dx, dg = _rms_bwd_rows(q_ref[:, h * HEAD:(h + 1) * HEAD], gqn_ref[...], dqc_ref[:, h * QCAT:h * QCAT + HEAD], HEAD)
            dq_ref[:, h * HEAD:(h + 1) * HEAD] = dx.astype(BF16)
            sqn = dg if sqn is None else sqn + dg
            dy = _rope_bwd(dqc_ref[:, h * QCAT + HEAD:(h + 1) * QCAT], cc, ss)
            dx, dg = _rms_bwd_rows(q_ref[:, W + h * HEAD:W + (h + 1) * HEAD], gqp_ref[...], dy, MLA_ROPE)
            dq_ref[:, W + h * HEAD:W + (h + 1) * HEAD] = dx.astype(BF16)
            sqp = dg if sqp is None else sqp + dg
            dx, dg = _rms_bwd_rows(kv_ref[:, h * HEAD:(h + 1) * HEAD], gkn_ref[...], dkc_ref[:, h * QCAT:h * QCAT + HEAD], HEAD)
            dkv_ref[:, h * HEAD:(h + 1) * HEAD] = dx.astype(BF16)
            skn = dg if skn is None else skn + dg
            part = dkc_ref[:, h * QCAT + HEAD:(h + 1) * QCAT].astype(F32)
            dkp = part if dkp is None else dkp + part
        dkv_ref[:, W:2 * W] = dv_ref[...].astype(BF16)
        dx, dg = _rms_bwd_rows(kpe_ref[...], gkp_ref[...], _rope_bwd(dkp, cc, ss), MLA_ROPE)
        dkpe_ref[...] = dx
        _acc_rows(dgqn_ref, sqn, first)
        _acc_rows(dgqp_ref, sqp, first)
        _acc_rows(dgkn_ref, skn, first)
        _acc_rows(dgkp_ref, dg, first)

    g = _fullspec((1, HEAD))
    gs = jax.ShapeDtypeStruct((1, HEAD), F32)
    return pl.pallas_call(
        body, name=name, grid=(rows // tr,),
        in_specs=[_rowspec(tr, 2 * W), _rowspec(tr, 2 * W), _rowspec(tr, LANES, Z_KPE // LANES), _rowspec(tr, LANES),
                  _rowspec(tr, LANES), g, g, g, g, _rowspec(tr, MLA_HEADS * QCAT), _rowspec(tr, MLA_HEADS * QCAT),
                  _rowspec(tr, W)],
        out_specs=[_rowspec(tr, 2 * W), _rowspec(tr, 2 * W), _rowspec(tr, LANES), g, g, g, g],
        out_shape=[jax.ShapeDtypeStruct((rows, 2 * W), BF16), jax.ShapeDtypeStruct((rows, 2 * W), BF16),
                   jax.ShapeDtypeStruct((rows, LANES), F32), gs, gs, gs, gs],
        compiler_params=_params(("arbitrary",)),
    )(q, kv, z, cc, ss, gqn, gqp, gkn, gkp, dqc, dkc, dv)


def _headnorm_fwd(x, col, nheads, g, out_scale, name):
    rows = x.shape[0]
    W = nheads * HEAD
    tr = _row_tile(rows, 6 * W)

    def body(x_ref, g_ref, o_ref):
        for h in range(nheads):
            xv = x_ref[:, h * HEAD:(h + 1) * HEAD]
            o_ref[:, h * HEAD:(h + 1) * HEAD] = (xv * _rms(xv, HEAD) * g_ref[...] * out_scale).astype(BF16)

    return pl.pallas_call(
        body, name=name, grid=(rows // tr,),
        in_specs=[_rowspec(tr, W, col), _fullspec((1, HEAD))], out_specs=_rowspec(tr, W),
        out_shape=jax.ShapeDtypeStruct((rows, W), BF16), compiler_params=_params(("parallel",)),
    )(x, g)


def _headnorm_bwd(x, col, nheads, g, dy, tail, name, into=None):
    rows = x.shape[0]
    W = nheads * HEAD
    tr = _row_tile(rows, 12 * W)
    has_tail = tail is not None
    WO = 2 * W if has_tail else W

    def body(*refs):
        if into is not None:
            x_ref, g_ref, dy_ref, _, dx_ref, dg_ref = refs
        elif has_tail:
            x_ref, g_ref, dy_ref, t_ref, dx_ref, dg_ref = refs
        else:
            x_ref, g_ref, dy_ref, dx_ref, dg_ref = refs
        acc = None
        for h in range(nheads):
            sl = slice(h * HEAD, (h + 1) * HEAD)
            dx, dg = _rms_bwd_rows(x_ref[:, sl], g_ref[...], dy_ref[:, sl], HEAD)
            dx_ref[:, sl] = dx.astype(BF16)
            acc = dg if acc is None else acc + dg
        if has_tail:
            dx_ref[:, W:2 * W] = t_ref[...].astype(BF16)
        _acc_rows(dg_ref, acc, pl.program_id(0) == 0)

    ins = [x, g, dy] + ([tail] if has_tail else [])
    specs = [_rowspec(tr, W, col), _fullspec((1, HEAD)), _rowspec(tr, W)] + ([_rowspec(tr, W)] if has_tail else [])
    dx_spec, dx_shape, aliases = _rowspec(tr, WO), jax.ShapeDtypeStruct((rows, WO), BF16), {}
    if into is not None:
        assert not has_tail
        ins, specs = ins + [into[0]], specs + [ANY_SPEC]
        dx_spec, dx_shape, aliases = _rowspec(tr, W, into[1]), jax.ShapeDtypeStruct(into[0].shape, into[0].dtype), {3: 0}
    return pl.pallas_call(
        body, name=name, grid=(rows // tr,), in_specs=specs,
        out_specs=[dx_spec, _fullspec((1, HEAD))], out_shape=[dx_shape, jax.ShapeDtypeStruct((1, HEAD), F32)],
        input_output_aliases=aliases, compiler_params=_params(("arbitrary",)),
    )(*ins)


def _sigmoid(x):
    return 1.0 / (1.0 + jnp.exp(-x.astype(F32)))


def _merge_fwd(z, y_gm, y_mla, y_mem, name):
    rows = z.shape[0]
    tr = _row_tile(rows, 14 * D_MODEL)

    def body(g0_ref, g1_ref, g2_ref, a_ref, b_ref, c_ref, o_ref):
        m = _sigmoid(g0_ref[...]) * a_ref[...] + _sigmoid(g1_ref[...]) * b_ref[...] + _sigmoid(g2_ref[...]) * c_ref[...]
        o_ref[...] = m.astype(BF16)

    r = _rowspec(tr, D_MODEL)
    return pl.pallas_call(
        body, name=name, grid=(rows // tr,),
        in_specs=[_rowspec(tr, D_MODEL, 0), _rowspec(tr, D_MODEL, 1), _rowspec(tr, D_MODEL, 2),r, r, r],
        out_specs=r, out_shape=jax.ShapeDtypeStruct((rows, D_MODEL), BF16), compiler_params=_params(("parallel",)),
    )(z, z, z, y_gm, y_mla, y_mem)


def _merge_bwd(z, y_gm, y_mla, y_mem, dm, name):
    rows = z.shape[0]
    tr = _row_tile(rows, 24 * D_MODEL)

    def body(g0_ref, g1_ref, g2_ref, a_ref, b_ref, c_ref, dm_ref, da_ref, db_ref, dc_ref, dzg_ref):
        dmv = dm_ref[...].astype(F32)
        for k, (g_ref, y_ref, dy_ref) in enumerate(((g0_ref, a_ref, da_ref), (g1_ref, b_ref, db_ref), (g2_ref, c_ref, dc_ref))):
            s = _sigmoid(g_ref[...])
            dy_ref[...] = (dmv * s).astype(BF16)
            dzg_ref[:, k * D_MODEL:(k + 1) * D_MODEL] = (dmv * y_ref[...] * s * (1.0 - s)).astype(BF16)

    r = _rowspec(tr, D_MODEL)
    o = jax.ShapeDtypeStruct((rows, D_MODEL), BF16)
    return pl.pallas_call(
        body, name=name, grid=(rows // tr,),
        in_specs=[_rowspec(tr, D_MODEL, 0), _rowspec(tr, D_MODEL, 1), _rowspec(tr, D_MODEL, 2),r, r, r, r],
        out_specs=[r, r, r, _rowspec(tr, 3 * D_MODEL, 0)],
        out_shape=[o, o, o, jax.ShapeDtypeStruct((rows, Z_COLS), BF16)],
        compiler_params=_params(("parallel",)),
    )(z, z, z, y_gm, y_mla, y_mem, dm)


def _loss_head(y, target, name):
    rows, width = y.shape
    tr = _row_tile(rows, 14 * width)

    def body(y_ref, t_ref, dy_ref, dyb_ref, l_ref):
        e = y_ref[...] - t_ref[...]
        dy = e * (1.0 / width)
        dy_ref[...] = dy
        dyb_ref[...] = dy.astype(BF16)
        e2 = e * e
        part = e2[:, 0:LANES]
        for k in range(1, width // LANES):
            part = part + e2[:, k * LANES:(k + 1) * LANES]
        _acc_rows(l_ref, part, pl.program_id(0) == 0)

    return pl.pallas_call(
        body, name=name, grid=(rows // tr,),
        in_specs=[_rowspec(tr, width), _rowspec(tr, width)],
        out_specs=[_rowspec(tr, width), _rowspec(tr, width), _fullspec((1, LANES))],
        out_shape=[jax.ShapeDtypeStruct((rows, width), F32), jax.ShapeDtypeStruct((rows, width), BF16),
                   jax.ShapeDtypeStruct((1, LANES), F32)],
        compiler_params=_params(("arbitrary",)),
    )(y, target)


_NT = (((1,), (1,)), ((), ()))
_TN = (((0,), (0,)), ((), ()))


def _diag_mask(s):
    row = lax.broadcasted_iota(jnp.int32, s.shape, 0)
    col = lax.broadcasted_iota(jnp.int32, s.shape, 1)
    return jnp.where(row >= col, s, NEG)


def _attn_fwd(q, k, v, nb, nheads, dk, v_col0, causal, name, rider=None):
    S, Skv = q.shape[0] // nb, k.shape[0] // nb
    tq = _pick(Skv, ATT_TILE) if causal else _pick(S, 4 * ATT_TILE)
    nq = S // tq

    def body(q_ref, k_ref, v_ref, o_ref, lse_ref):
        for i in range(nq):
            r0 = i * tq
            qb = q_ref[r0:r0 + tq, :]
            if causal:
                spans = ([(0, r0, False)] if i > 0 else []) + [(r0, r0 + tq, True)]
            else:
                spans = [(0, Skv, False)]
            scores = []
            for a, b, masked in spans:
                s = lax.dot_general(qb, k_ref[a:b, :], _NT, preferred_element_type=F32)
                scores.append(_diag_mask(s) if masked else s)
            m = functools.reduce(jnp.maximum, [jnp.max(s, axis=-1, keepdims=True) for s in scores])
            l = acc = None
            for s, (a, b, _) in zip(scores, spans):
                p = jnp.exp2(s - m)
                lp = jnp.sum(p, axis=-1, keepdims=True)
                ap = jnp.dot(p.astype(BF16), v_ref[a:b, :].astype(BF16), preferred_element_type=F32)
                l, acc = (lp, ap) if l is None else (l + lp, acc + ap)
            o_ref[r0:r0 + tq, :] = (acc / l).astype(BF16)
            lse_ref[r0:r0 + tq, :] = m + jnp.log2(l)

    ins = [q, k, v]
    in_specs = [pl.BlockSpec((S, dk), lambda b, h: (b, h)), pl.BlockSpec((Skv, dk), lambda b, h: (b, h)),
                pl.BlockSpec((Skv, HEAD), lambda b, h: (b, v_col0 + h))]
    out_specs = [pl.BlockSpec((S, HEAD), lambda b, h: (b, h)), pl.BlockSpec((None, S, 1), lambda b, h: (h, b, 0))]
    out_shape = [jax.ShapeDtypeStruct((nb * S, nheads * HEAD), BF16), jax.ShapeDtypeStruct((nheads, nb * S, 1), F32)]
    return _call(body, rider, ins, name=name, grid=(nb, nheads), in_specs=in_specs, out_specs=out_specs,
                 out_shape=out_shape, scratch_shapes=[], sem=("parallel", "parallel"))


def _attn_bwd(q, k, v, o, do, lse, nb, nheads, dk, v_col0, scale, causal, name, rider=None):
    S, Skv = q.shape[0] // nb, k.shape[0] // nb
    tk = _pick(Skv, ATT_TILE)
    nkv = Skv // tk

    def body(q_ref, k_ref, v_ref, o_ref, do_ref, lse_ref, dq_ref, dk_ref, dv_ref, delta_ref, dob_ref, dqa_ref):
        dov = do_ref[...]
        delta_ref[...] = jnp.sum(o_ref[...].astype(F32) * dov.astype(F32), axis=-1, keepdims=True)
        dob_ref[...] = dov.astype(BF16)

        for j in range(nkv):
            c0 = j * tk
            kb = k_ref[c0:c0 + tk, :]
            vb = v_ref[c0:c0 + tk, :].astype(BF16)
            if causal:
                spans = [(c0, c0 + tk, True)] + ([(c0 + tk, S, False)] if c0 + tk < S else [])
            else:
                spans = [(0, S, False)]
            dk_acc = dv_acc = None
            for a, b, masked in spans:
                qb = q_ref[a:b, :]
                dob = dob_ref[a:b, :]
                s = lax.dot_general(qb, kb, _NT, preferred_element_type=F32)
                if masked:
                    s = _diag_mask(s)
                p = jnp.exp2(s - lse_ref[a:b, :])
                dp = lax.dot_general(dob, vb, _NT, preferred_element_type=F32)
                ds = (p * (dp - delta_ref[a:b, :])).astype(BF16)
                dv_p = lax.dot_general(p.astype(BF16), dob, _TN, preferred_element_type=F32)
                dk_p = lax.dot_general(ds, qb, _TN, preferred_element_type=F32)
                dk_acc, dv_acc = (dk_p, dv_p) if dk_acc is None else (dk_acc + dk_p, dv_acc + dv_p)
                dq_p = jnp.dot(ds, kb, preferred_element_type=F32) * scale
                if j == 0:
                    dqa_ref[a:b, :] = dq_p
                else:
                    dqa_ref[a:b, :] += dq_p
            dk_ref[c0:c0 + tk, :] = (dk_acc * LN2).astype(BF16)
            dv_ref[c0:c0 + tk, :] = dv_acc.astype(BF16)
        dq_ref[...] = dqa_ref[...].astype(BF16)

    ins = [q, k, v, o, do, lse]
    in_specs = [pl.BlockSpec((S, dk), lambda b, h: (b, h)), pl.BlockSpec((Skv, dk), lambda b, h: (b, h)),
                pl.BlockSpec((Skv, HEAD), lambda b, h: (b, v_col0 + h)), pl.BlockSpec((S, HEAD), lambda b, h: (b, h)),
                pl.BlockSpec((S, HEAD), lambda b, h: (b, h)), pl.BlockSpec((None, S, 1), lambda b, h: (h, b, 0))]
    out_specs = [pl.BlockSpec((S, dk), lambda b, h: (b, h)), pl.BlockSpec((Skv, dk), lambda b, h: (b, h)),
                 pl.BlockSpec((Skv, HEAD), lambda b, h: (b, h))]
    out_shape = [jax.ShapeDtypeStruct((nb * S, nheads * dk), BF16), jax.ShapeDtypeStruct((nb * Skv, nheads * dk), BF16),
                 jax.ShapeDtypeStruct((nb * Skv, nheads * HEAD), BF16)]
    return _call(body, rider, ins, name=name, grid=(nb, nheads), in_specs=in_specs, out_specs=out_specs,
                 out_shape=out_shape,
                 scratch_shapes=[pltpu.VMEM((S, 1), F32), pltpu.VMEM((S, HEAD), BF16), pltpu.VMEM((S, dk), F32)],
                 sem=("parallel", "parallel"))


def _spread_rope(a):
    zero = jnp.zeros(a.shape[:-1] + (32,), a.dtype)
    return jnp.concatenate([a[..., :32], zero, a[..., 32:], zero], axis=-1)


def _gather_rope(a):
    return jnp.concatenate([a[..., 0:32], a[..., 64:96]], axis=-1)


def _win_layout(w):
    return jnp.concatenate([w[:, C_ZG:C_END], w[:, C_ZU:C_CQ], w[:, C_QM:C_ZG], w[:, C_CQ:C_CKV], w[:, C_CKV:C_KPE],
                            _spread_rope(w[:, C_KPE:C_QM])], axis=1)


def _win_unlayout(d):
    return jnp.concatenate([d[:, Z_GM:Z_QM], d[:, Z_MLA:Z_MLA + Q_LORA], d[:, Z_MLA + Q_LORA:Z_KPE],
                            _gather_rope(d[:, Z_KPE:Z_COLS]), d[:, Z_QM:Z_MLA], d[:, 0:Z_GM]], axis=1)


def _wuq_layout(w):
    r = w.reshape(Q_LORA, MLA_HEADS, HEAD + MLA_ROPE)
    return jnp.concatenate([r[:, :, :HEAD].reshape(Q_LORA, -1), _spread_rope(r[:, :, HEAD:]).reshape(Q_LORA, -1)], axis=1)


def _wuq_unlayout(d):
    n = d[:, :MLA_HEADS * HEAD].reshape(Q_LORA, MLA_HEADS, HEAD)
    p = _gather_rope(d[:, MLA_HEADS * HEAD:].reshape(Q_LORA, MLA_HEADS, HEAD))
    return jnp.concatenate([n, p], axis=-1).reshape(Q_LORA, -1)


def _wukv_layout(w):
    r = w.reshape(KV_LORA, MLA_HEADS, 2 * HEAD)
    return jnp.concatenate([r[:, :, :HEAD].reshape(KV_LORA, -1), r[:, :, HEAD:].reshape(KV_LORA, -1)], axis=1)


def _wukv_unlayout(d):
    k = d[:, :MLA_HEADS * HEAD].reshape(KV_LORA, MLA_HEADS, HEAD)
    v = d[:, MLA_HEADS * HEAD:].reshape(KV_LORA, MLA_HEADS, HEAD)
    return jnp.concatenate([k, v], axis=-1).reshape(KV_LORA, -1)


IN_SPLIT = 640
AG_MID = ['w_uq', 'w_ukv', 'w_mem_kv', 'w_o_gm', 'w_o_mla', 'w_o_mem', 'w_out']
AG_FFN = ['w_ff1', 'w_ff2']
RS_GROUPS = {'ff2': ['w_ff2'], 'proj': ['w_ff1', 'w_out', 'w_o_gm', 'w_o_mla', 'w_o_mem'],
             'lat': ['w_uq', 'w_ukv', 'w_mem_kv'], 'in': ['w_in']}


def _unride(res, rider):
    return (res, None) if rider is None else res


def _local_step(x, mem, positions, target, P, ws):
    B, S, _ = x.shape
    M = mem.shape[1]
    T = B * S
    x2d = x.reshape(T, D_MODEL)
    mem2d = mem.reshape(B * M, D_MODEL)
    tgt2d = target.reshape(T, D_MODEL)

    def row(v):
        return v.reshape(1, -1).astype(F32)

    inv_freq = ROPE_BASE ** (-jnp.arange(0, MLA_ROPE, 2, dtype=F32) / MLA_ROPE)
    ang = positions.reshape(T).astype(F32)[:, None] * inv_freq
    cos, sin, zero = jnp.cos(ang), jnp.sin(ang), jnp.zeros_like(ang)
    cc = jnp.concatenate([cos, zero, cos, zero], axis=1)
    ss = jnp.concatenate([-sin, zero, sin, zero], axis=1)

    w_in = _win_layout(ws.first()).astype(BF16)
    g_mix, g_cq, g_ckv, g_ffn, g_mem = row(P['g_mix']), row(P['g_cq']), row(P['g_ckv']), row(P['g_ffn']), row(P['g_mem'])
    gqn, gkn, gmq, gmk = row(P['g_q_nope']), row(P['g_k_nope']), row(P['g_mq']), row(P['g_mk'])
    gqp, gkp = _spread_rope(row(P['g_q_pe'])), _spread_rope(row(P['g_k_pe']))
    gln, bln = row(P['g_gm_ln']), row(P['b_gm_ln'])
    wc = jnp.tril(P['w_spatial'].astype(F32))
    wct = jnp.swapaxes(wc, 1, 2).astype(BF16)
    wc = wc.astype(BF16)
    bst = jnp.broadcast_to(P['b_spatial'].astype(F32)[:, :, None], (GM_GROUPS, GM_CHUNK, LANES))

    h = _rms_fwd(x2d, g_mix, "rms_mix")
    ride = ws.gather(AG_MID)
    z, got = _unride(_matmul(h, w_in, 'nn', ACT, "mm_in", tn_t=768, rider=ride), ride)
    mid = ws.gathered(AG_MID, got)
    w_uq = _wuq_layout(mid['w_uq']).astype(BF16)
    w_ukv = _wukv_layout(mid['w_ukv']).astype(BF16)
    w_mem_kv, w_o_gm, w_o_mla, w_o_mem, w_out = (mid[n].astype(BF16) for n in ('w_mem_kv', 'w_o_gm', 'w_o_mla', 'w_o_mem',
                                                                                 'w_out'))
    ygm_pre = _gm_fwd(z, gln, bln, wc, bst, "gm_fwd")
    y_gm = _matmul(ygm_pre, w_o_gm, 'nn', ACT, "mm_o_gm")
    nq, nkv = _lat_fwd(z, g_cq, g_ckv, "lat_fwd")
    q = _matmul(nq, w_uq, 'nn', ACT, "mm_uq")
    kv = _matmul(nkv, w_ukv, 'nn', ACT, "mm_ukv")
    qcat, kcat, vv = _qk_fwd(q, kv, z, cc, ss, gqn, gqp, gkn, gkp, "qk_fwd")
    ride = ws.gather(AG_FFN)
    (o, lse), got = _unride(_attn_fwd(qcat, kcat, vv, B, MLA_HEADS, QCAT, 0, True, "mla_attn_fwd", rider=ride), ride)
    ffn = ws.gathered(AG_FFN, got)
    w_ff1, w_ff2 = ffn['w_ff1'].astype(BF16), ffn['w_ff2'].astype(BF16)
    y_mla = _matmul(o, w_o_mla, 'nn', ACT, "mm_o_mla")
    nm = _rms_fwd(mem2d, g_mem, "rms_mem")
    kvm = _matmul(nm, w_mem_kv, 'nn', ACT, "mm_mem_kv")
    qm = _headnorm_fwd(z, Z_QM // (MEM_HEADS * HEAD), MEM_HEADS, gmq, MEM_SCALE * LOG2E, "memq_fwd")
    km = _headnorm_fwd(kvm, 0, MEM_HEADS, gmk, 1.0, "memk_fwd")
    om, lse_m = _attn_fwd(qm, km, kvm, B, MEM_HEADS, HEAD, MEM_HEADS, False, "mem_attn_fwd")
    y_mem = _matmul(om, w_o_mem, 'nn', ACT, "mm_o_mem")
    merged = _merge_fwd(z, y_gm, y_mla, y_mem, "merge_fwd")
    x1 = _matmul(merged, w_out, 'nn', F32, "mm_out", add=x2d)
    h2 = _rms_fwd(x1, g_ffn, "rms_ffn")
    a1 = _matmul(h2, w_ff1, 'nn', BF16, "mm_ff1")
    x2 = _matmul(a1, w_ff2, 'nn', F32, "mm_ff2", add=x1, relu2_a=True)
    dx2, dx2b, loss_part = _loss_head(x2, tgt2d, "loss_head")

    G = {}
    ride = ws.scatter('ff2', {'w_ff2': _matmul(a1, dx2b, 'tn', BF16, "mm_d_ff2", relu2_a=True)})
    da1, got = _unride(_matmul(dx2b, w_ff2, 'nt', BF16, "mm_da1", relu2_grad=a1, rider=ride), ride)
    ws.scattered('ff2', got)
    d_ff1 = _matmul(h2, da1, 'tn', BF16, "mm_d_ff1")
    dh2 = _matmul(da1, w_ff1, 'nt', ACT, "mm_dh2")
    dx1, dx1b, G['g_ffn'] = _rms_bwd(x1, g_ffn, dh2, dx2, "rms_ffn_bwd", dx_dtypes=(F32, BF16))
    d_out = _matmul(merged, dx1b, 'tn', BF16, "mm_d_out")
    dmerged = _matmul(dx1b, w_out, 'nt', ACT, "mm_dmerged")
    dy_gm, dy_mla, dy_mem, dz = _merge_bwd(z, y_gm, y_mla, y_mem, dmerged, "merge_bwd")
    d_o_gm = _matmul(ygm_pre, dy_gm, 'tn', BF16, "mm_d_o_gm")
    d_o_mla = _matmul(o, dy_mla, 'tn', BF16, "mm_d_o_mla")
    d_o_mem = _matmul(om, dy_mem, 'tn', BF16, "mm_d_o_mem")
    dygm_pre = _matmul(dy_gm, w_o_gm, 'nt', ACT, "mm_dygm")
    dz, dws, dbs, G['g_gm_ln'], G['b_gm_ln'] = _gm_bwd(z, dygm_pre, gln, bln, wc, wct, bst, dz, "gm_bwd")
    G['w_spatial'] = jnp.tril(dws)
    G['b_spatial'] = jnp.sum(dbs.reshape(GM_CHUNK, GM_GROUPS, LANES), axis=-1).T
    do = _matmul(dy_mla, w_o_mla, 'nt', ACT, "mm_do")
    ride = ws.scatter('proj', {'w_ff1': d_ff1, 'w_out': d_out, 'w_o_gm': d_o_gm, 'w_o_mla': d_o_mla, 'w_o_mem': d_o_mem})
    (dqc, dkc, dvv), got = _unride(_attn_bwd(qcat, kcat, vv, o, do, lse, B, MLA_HEADS, QCAT, 0, MLA_SCALE, True,
                                             "mla_attn_bwd", rider=ride), ride)
    ws.scattered('proj', got)
    dq, dkv, dkpe, G['g_q_nope'], dgqp, G['g_k_nope'], dgkp = _qk_bwd(q, kv, z, cc, ss, gqn, gqp, gkn, gkp, dqc, dkc, dvv,
                                                                     "qk_bwd")
    G['g_q_pe'], G['g_k_pe'] = _gather_rope(dgqp), _gather_rope(dgkp)
    d_uq = _wuq_unlayout(_matmul(nq, dq, 'tn', BF16, "mm_d_uq"))
    dnq = _matmul(dq, w_uq, 'nt', ACT, "mm_dnq")
    d_ukv = _wukv_unlayout(_matmul(nkv, dkv, 'tn', BF16, "mm_d_ukv"))
    dnkv = _matmul(dkv, w_ukv, 'nt', ACT, "mm_dnkv")
    dz, G['g_cq'], G['g_ckv'] = _lat_bwd(z, dnq, dnkv, dkpe, g_cq, g_ckv, dz, "lat_bwd")
    dom = _matmul(dy_mem, w_o_mem, 'nt', ACT, "mm_dom")
    dqm, dkm, dvm = _attn_bwd(qm, km, kvm, om, dom, lse_m, B, MEM_HEADS, HEAD, MEM_HEADS, MEM_SCALE, False, "mem_attn_bwd")
    dz, G['g_mq'] = _headnorm_bwd(z, Z_QM // (MEM_HEADS * HEAD), MEM_HEADS, gmq, dqm, None, "memq_bwd",
                                  into=(dz, Z_QM // (MEM_HEADS * HEAD)))
    dkvm, G['g_mk'] = _headnorm_bwd(kvm, 0, MEM_HEADS, gmk, dkm, dvm, "memk_bwd")
    d_mem_kv = _matmul(nm, dkvm, 'tn', BF16, "mm_d_mem_kv")
    dnm = _matmul(dkvm, w_mem_kv, 'nt', ACT, "mm_dnm")
    G['g_mem'], = _rms_bwd(mem2d, g_mem, dnm, None, "rms_mem_bwd", dx_dtypes=())
    ride = ws.scatter('lat', {'w_uq': d_uq, 'w_ukv': d_ukv, 'w_mem_kv': d_mem_kv})
    d_in, got = _unride(_matmul(h, dz, 'tn', BF16, "mm_d_in", tn_t=768, rider=ride), ride)
    ws.scattered('lat', got)
    ride = ws.scatter('in', {'w_in': _win_unlayout(d_in)}, rows=(0, IN_SPLIT))
    dh, got = _unride(_matmul(dz, w_in, 'nt', ACT, "mm_dh", rider=ride), ride)
    ride = ws.scatter_rest(got, rows=(IN_SPLIT, D_MODEL - IN_SPLIT))
    (gx, G['g_mix']), got = _unride(_rms_bwd(x2d, g_mix, dh, dx1, "rms_mix_bwd", rider=ride), ride)
    ws.scattered('in', got)
    return loss_part, gx.reshape(B, S, D_MODEL), G


def _all_gather8(xs, name):
    def body(x_ref, out_ref, send_sems, recv_sems, local_sem):
        x, y, c = lax.axis_index("x"), lax.axis_index("y"), lax.axis_index("c")
        me, sibling = (x, y, c), (x, y, 1 - c)
        chips = [(1 - x, y), (x, 1 - y), (1 - x, 1 - y)]

        def rows(px, py, pc):
            return out_ref.at[4 * px + 2 * py + pc]

        def copy(k, block, to, src=None):
            return pltpu.make_async_remote_copy(
                src_ref=rows(*block) if src is None else src, dst_ref=rows(*block),
                send_sem=send_sems.at[k], recv_sem=recv_sems.at[k], device_id=to, device_id_type=MESH)

        mine = pltpu.make_async_copy(x_ref, rows(*me), local_sem)
        mine.start()
        first = [copy(0, me, sibling, src=x_ref)]
        first += [copy(1 + j, me, (*chip, c), src=x_ref) for j, chip in enumerate(chips)]
        for cp in first:
            cp.start()
        passed = [copy(4 + j, (*chip, c), sibling) for j, chip in enumerate(chips)]
        for j, chip in enumerate(chips):
            copy(1 + j, (*chip, c), me).wait_recv()
            passed[j].start()
        copy(0, sibling, me).wait_recv()
        for j, chip in enumerate(chips):
            copy(4 + j, (*chip, 1 - c), me).wait_recv()
        for cp in first + passed:
            cp.wait_send()
        mine.wait()

    return pl.pallas_call(
        body, name=name, in_specs=[HBM_SPEC], out_specs=HBM_SPEC,
        out_shape=jax.ShapeDtypeStruct((N_DEV,) + xs.shape, xs.dtype),
        scratch_shapes=[pltpu.SemaphoreType.DMA((7,)), pltpu.SemaphoreType.DMA((7,)), pltpu.SemaphoreType.DMA],
    )(xs)


def _adamw_rows(w, g, m, v):
    m2 = ADAM_B1 * m + (1.0 - ADAM_B1) * g
    v2 = ADAM_B2 * v + (1.0 - ADAM_B2) * (g * g)
    m_hat = m2 / (1.0 - ADAM_B1 ** ADAM_STEP)
    v_hat = v2 / (1.0 - ADAM_B2 ** ADAM_STEP)
    delta = -ADAM_LR * (m_hat / (jnp.sqrt(v_hat) + ADAM_EPS) + ADAM_WD * w)
    return delta, m2, v2


def _sum_adamw(parts, w, m, v, name):
    rows, cols = w.shape
    tr = _pick(rows, max(16, 65536 // cols), 16)
    n = parts.shape[0]

    def body(p_ref, w_ref, m_ref, v_ref, g_ref, d_ref, m2_ref, v2_ref):
        g = p_ref[0].astype(F32)
        for k in range(1, n):
            g = g + p_ref[k].astype(F32)
        delta, m2, v2 = _adamw_rows(w_ref[...], g, m_ref[...], v_ref[...])
        g_ref[...] = g
        d_ref[...] = delta
        m2_ref[...] = m2
        v2_ref[...] = v2

    flat = pl.BlockSpec((tr, cols), lambda i: (i, 0))
    out = jax.ShapeDtypeStruct((rows, cols), F32)
    return pl.pallas_call(
        body, name=name, grid=(rows // tr,),
        in_specs=[pl.BlockSpec((n, tr, cols), lambda i: (0, i, 0)), flat, flat, flat], out_specs=[flat] * 4,
        out_shape=[out] * 4, compiler_params=_params(("parallel",)),
    )(parts, w, m, v)


def _small_rows(name):
    n = {'g_mix': 1024, 'g_cq': 384, 'g_ckv': 256, 'g_q_nope': 128, 'g_q_pe': 64, 'g_k_nope': 128, 'g_k_pe': 64,
         'g_gm_ln': 512, 'b_gm_ln': 512, 'w_spatial': GM_GROUPS * GM_CHUNK * GM_CHUNK, 'b_spatial': GM_GROUPS * GM_CHUNK,
         'g_mem': 1024, 'g_mq': 128, 'g_mk': 128, 'g_ffn': 1024}[name]
    return n, -(-n // (8 * LANES)) * 8


def _small_slab(d):
    parts = []
    for name in SMALL:
        n, rows = _small_rows(name)
        parts.append(jnp.pad(d[name].reshape(-1).astype(F32), (0, rows * LANES - n)).reshape(rows, LANES))
    return jnp.concatenate(parts, axis=0)


def _small_unslab(slab, like):
    out, r = {}, 0
    for name in SMALL:
        n, rows = _small_rows(name)
        out[name] = slab[r:r + rows].reshape(-1)[:n].reshape(like[name].shape)
        r += rows
    return out


def _full_from_gathered(gathered, name):
    r, c = BIG_SHAPE[name]
    if BIG_AXIS[name] == 0:
        return gathered.reshape(r, c)
    return gathered.transpose(1, 0, 2).reshape(r, c)


def _shards_of_full(g, name):
    r, c = BIG_SHAPE[name]
    if BIG_AXIS[name] == 0:
        return g.reshape(N_DEV, r // N_DEV, c)
    return g.reshape(r, N_DEV, c // N_DEV).transpose(1, 0, 2)


class _DistWeights:
    def __init__(self, shards):
        self.shards = shards
        self.received = {}

    def first(self):
        return _full_from_gathered(_all_gather8(self.shards['w_in'].astype(BF16), "ag_w_in"), 'w_in')

    def gather(self, names):
        return _Gather2([self.shards[n].astype(BF16) for n in names])

    def gathered(self, names, got):
        return {n: _full_from_gathered(g, n) for n, g in zip(names, got)}

    def scatter(self, key, grads, rows=None):
        self.leaving = [_shards_of_full(grads[n], n) for n in RS_GROUPS[key]]
        return _Exchange(self.leaving, scatter=True, rows=rows)

    def scatter_rest(self, got, rows):
        return _Exchange(self.leaving, scatter=True, rows=rows, into=got)

    def scattered(self, key, got):
        self.received.update(zip(RS_GROUPS[key], got))


def kernel(x, mem, positions, g_mix, w_in, g_cq, w_uq, g_ckv, w_ukv, g_q_nope, g_q_pe, g_k_nope, g_k_pe, g_gm_ln, b_gm_ln, w_spatial, b_spatial, g_mem, w_mem_kv, g_mq, g_mk, w_o_gm, w_o_mla, w_o_mem, w_out, g_ffn, w_ff1, w_ff2, loss_target, m_g_mix, m_w_in, m_g_cq, m_w_uq, m_g_ckv, m_w_ukv, m_g_q_nope, m_g_q_pe, m_g_k_nope, m_g_k_pe, m_g_gm_ln, m_b_gm_ln, m_w_spatial, m_b_spatial, m_g_mem, m_w_mem_kv, m_g_mq, m_g_mk, m_w_o_gm, m_w_o_mla, m_w_o_mem, m_w_out, m_g_ffn, m_w_ff1, m_w_ff2, v_g_mix, v_w_in, v_g_cq, v_w_uq, v_g_ckv, v_w_ukv, v_g_q_nope, v_g_q_pe, v_g_k_nope, v_g_k_pe, v_g_gm_ln, v_b_gm_ln, v_w_spatial, v_b_spatial, v_g_mem, v_w_mem_kv, v_g_mq, v_g_mk, v_w_o_gm, v_w_o_mla, v_w_o_mem, v_w_out, v_g_ffn, v_w_ff1, v_w_ff2):
    given = dict(locals())
    w = {n: given[n][0] for n in WEIGHTS}
    mom = {n: given['m_' + n][0] for n in WEIGHTS}
    var = {n: given['v_' + n][0] for n in WEIGHTS}

    ws = _DistWeights({n: w[n] for n in BIG})
    loss_part, grad_x, G = _local_step(x, mem, positions, loss_target, {n: w[n] for n in SMALL}, ws)
    loss = lax.psum(0.5 * jnp.sum(loss_part) / D_MODEL, ("x", "y", "c"))

    outs = {}
    for n in BIG:
        for prefix, res in zip(("grad_", "delta_", "new_m_", "new_v_"),
                               _sum_adamw(ws.received[n], w[n], mom[n], var[n], "adamw_" + n)):
            outs[prefix + n] = res[None]

    parts = _all_gather8(_small_slab(G), "ag_small")
    small = _sum_adamw(parts, _small_slab(w), _small_slab(mom), _small_slab(var), "adamw_small")
    for prefix, small_slab in zip(("grad_", "delta_", "new_m_", "new_v_"), small):
        sm = _small_unslab(small_slab, given)
        for n in SMALL:
            outs[prefix + n] = sm[n]
    return (loss, grad_x, *[outs[p + n] for p in ("grad_", "delta_", "new_m_", "new_v_") for n in WEIGHTS])
```

```python
import functools
import math

import jax
import jax.numpy as jnp
from jax import lax
from jax.experimental import pallas as pl
from jax.experimental.pallas import tpu as pltpu

F32 = jnp.float32
BF16 = jnp.bfloat16
ACT = BF16

D_MODEL = 1024
MEM_HEADS = 4
HEAD = 128
GM_WIDTH = 512
GM_CHUNK = 128
GM_GROUPS = 4
MLA_HEADS = 8
MLA_ROPE = 64
Q_LORA = 384
KV_LORA = 256
D_FF = 4096
EPS = 1e-6
ROPE_BASE = 10000.0
MLA_SCALE = 1.0 / math.sqrt(HEAD + MLA_ROPE)
MEM_SCALE = 1.0 / math.sqrt(HEAD)
LOG2E = 1.4426950408889634
LN2 = 0.6931471805599453
ATT_TILE = 256
C_ZU, C_ZV, C_CQ, C_CKV, C_KPE, C_QM, C_ZG, C_END = 0, 512, 1024, 1408, 1664, 1728, 2240, 5312
Z_GM, Z_QM, Z_MLA, Z_KPE, Z_COLS = 3072, 4096, 4608, 5248, 5376
MLA_W = 768
QCAT = 2 * HEAD
ADAM_LR, ADAM_B1, ADAM_B2, ADAM_EPS, ADAM_WD, ADAM_STEP = 0.001, 0.9, 0.999, 1e-08, 0.01, 10
N_DEV = 8
LANES = 128
VMEM_LIMIT = 48 * 1024 * 1024
MAX_K_TILE = 8192
NEG = -1e30

BIG = ['w_in', 'w_uq', 'w_ukv', 'w_mem_kv', 'w_o_gm', 'w_o_mla', 'w_o_mem', 'w_out', 'w_ff1', 'w_ff2']
BIG_AXIS = {'w_in': 1, 'w_uq': 1, 'w_ukv': 1, 'w_mem_kv': 0, 'w_o_gm': 1, 'w_o_mla': 0, 'w_o_mem': 1,
            'w_out': 0, 'w_ff1': 1, 'w_ff2': 0}
BIG_SHAPE = {'w_in': (1024, 5312), 'w_uq': (384, 1536), 'w_ukv': (256, 2048), 'w_mem_kv': (1024, 1024),
             'w_o_gm': (512, 1024), 'w_o_mla': (1024, 1024), 'w_o_mem': (512, 1024), 'w_out': (1024, 1024),
             'w_ff1': (1024, 4096), 'w_ff2': (4096, 1024)}
SMALL = ['g_mix', 'g_cq', 'g_ckv', 'g_q_nope', 'g_q_pe', 'g_k_nope', 'g_k_pe', 'g_gm_ln', 'b_gm_ln',
         'w_spatial', 'b_spatial', 'g_mem', 'g_mq', 'g_mk', 'g_ffn']
WEIGHTS = ['g_mix', 'w_in', 'g_cq', 'w_uq', 'g_ckv', 'w_ukv', 'g_q_nope', 'g_q_pe', 'g_k_nope', 'g_k_pe',
           'g_gm_ln', 'b_gm_ln', 'w_spatial', 'b_spatial', 'g_mem', 'w_mem_kv', 'g_mq', 'g_mk', 'w_o_gm',
           'w_o_mla', 'w_o_mem', 'w_out', 'g_ffn', 'w_ff1', 'w_ff2']


def _pick(n, target, mult=LANES):
    best = None
    t = mult
    while t <= min(n, target):
        if n % t == 0:
            best = t
        t += mult
    return best if best is not None else n


def _params(sem):
    return pltpu.CompilerParams(dimension_semantics=sem, vmem_limit_bytes=VMEM_LIMIT)


MESH = pl.DeviceIdType.MESH
HBM_SPEC = pl.BlockSpec(memory_space=pltpu.HBM)


class _Exchange:
    def __init__(self, srcs, scatter):
        self.srcs, self.scatter = list(srcs), scatter
        self.out_shapes = [jax.ShapeDtypeStruct(s.shape if scatter else (N_DEV,) + s.shape, s.dtype) for s in self.srcs]
        n = len(self.srcs)
        self.scratch = [pltpu.SemaphoreType.DMA((n, N_DEV - 1)), pltpu.SemaphoreType.DMA((n, N_DEV - 1)),
                        pltpu.SemaphoreType.DMA((n,))]

    def _copies(self, src_refs, dst_refs, send_sems, recv_sems, local_sems):
        x, y, c = lax.axis_index("x"), lax.axis_index("y"), lax.axis_index("c")
        me = 4 * x + 2 * y + c
        local, remote = [], []
        for a, (src_ref, dst_ref) in enumerate(zip(src_refs, dst_refs)):
            def mine_for(dev, src_ref=src_ref):
                return src_ref.at[dev] if self.scatter else src_ref

            local.append(pltpu.make_async_copy(mine_for(me), dst_ref.at[me], local_sems.at[a]))
            for k in range(1, N_DEV):
                px = 1 - x if k & 4 else x
                py = 1 - y if k & 2 else y
                pc = 1 - c if k & 1 else c
                remote.append(pltpu.make_async_remote_copy(
                    src_ref=mine_for(4 * px + 2 * py + pc), dst_ref=dst_ref.at[me], send_sem=send_sems.at[a, k - 1],
                    recv_sem=recv_sems.at[a, k - 1], device_id=(px, py, pc), device_id_type=MESH))
        return local, remote

    def start(self, *refs):
        local, remote = self._copies(*refs)
        for cp in local + remote:
            cp.start()

    def wait(self, *refs):
        local, remote = self._copies(*refs)
        for cp in remote + local:
            cp.wait()


class _Gather2:
    def __init__(self, srcs):
        self.srcs = list(srcs)
        self.out_shapes = [jax.ShapeDtypeStruct((N_DEV,) + s.shape, s.dtype) for s in self.srcs]
        n = len(self.srcs)
        self.scratch = [pltpu.SemaphoreType.DMA((n, N_DEV - 1)), pltpu.SemaphoreType.DMA((n, N_DEV - 1)),
                        pltpu.SemaphoreType.DMA((n,))]

    def _plan(self, src_refs, dst_refs, send_sems, recv_sems, local_sems):
        x, y, c = lax.axis_index("x"), lax.axis_index("y"), lax.axis_index("c")
        chips = [(1 - x, y), (x, 1 - y), (1 - x, 1 - y)]
        plans = []
        for a, (src_ref, dst_ref) in enumerate(zip(src_refs, dst_refs)):
            def copy(k, block, to, src=None, a=a, dst_ref=dst_ref):
                at = dst_ref.at[4 * block[0] + 2 * block[1] + block[2]]
                return pltpu.make_async_remote_copy(src_ref=at if src is None else src, dst_ref=at,
                                                    send_sem=send_sems.at[a, k], recv_sem=recv_sems.at[a, k],
                                                    device_id=to, device_id_type=MESH)

            local = pltpu.make_async_copy(src_ref, dst_ref.at[4 * x + 2 * y + c], local_sems.at[a])
            first = [copy(0, (x, y, c), (x, y, 1 - c), src=src_ref)]
            first += [copy(1 + j, (x, y, c), (*chip, c), src=src_ref) for j, chip in enumerate(chips)]
            passed = [copy(4 + j, (*chip, c), (x, y, 1 - c)) for j, chip in enumerate(chips)]
            arrivals = [copy(1 + j, (*chip, c), (x, y, c)) for j, chip in enumerate(chips)]
            late = [copy(0, (x, y, 1 - c), (x, y, c))] + [copy(4 + j, (*chip, 1 - c), (x, y, c)) for j, chip in enumerate(chips)]
            plans.append((local, first, passed, arrivals, late))
        return plans

    def start(self, *refs):
        for local, first, _, _, _ in self._plan(*refs):
            local.start()
            for cp in first:
                cp.start()

    def wait(self, *refs):
        plans = self._plan(*refs)
        for _, _, passed, arrivals, _ in plans:
            for arrived, onward in zip(arrivals, passed):
                arrived.wait_recv()
                onward.start()
        for local, first, passed, _, late in plans:
            for cp in late:
                cp.wait_recv()
            for cp in first + passed:
                cp.wait_send()
            local.wait()


def _call(body, rider, ins, *, name, grid, in_specs, out_specs, out_shape, scratch_shapes, sem):
    if rider is None:
        return pl.pallas_call(body, name=name, grid=grid, in_specs=in_specs, out_specs=out_specs, out_shape=out_shape,
                              scratch_shapes=scratch_shapes, compiler_params=_params(sem))(*ins)
    single = not isinstance(out_shape, (list, tuple))
    own_specs, own_shapes = ([out_specs], [out_shape]) if single else (list(out_specs), list(out_shape))
    n_in, n_out, n_sc, n_r = len(ins), len(own_shapes), len(scratch_shapes), len(rider.srcs)
    n_all_in = n_in + n_r

    def carrying(*refs):
        own_in, srcs = refs[:n_in], refs[n_in:n_in + n_r]
        own_out, dsts = refs[n_all_in:n_all_in + n_out], refs[n_all_in + n_out:n_all_in + n_out + n_r]
        own_sc = refs[n_all_in + n_out + n_r:n_all_in + n_out + n_r + n_sc]
        sems = refs[n_all_in + n_out + n_r + n_sc:]
        first = last = None
        for d, steps in enumerate(grid):
            f, l = pl.program_id(d) == 0, pl.program_id(d) == steps - 1
            first, last = (f, l) if first is None else (first & f, last & l)

        @pl.when(first)
        def _():
            rider.start(srcs, dsts, *sems)

        body(*own_in, *own_out, *own_sc)

        @pl.when(last)
        def _():
            rider.wait(srcs, dsts, *sems)

    res = pl.pallas_call(
        carrying, name=name, grid=grid, in_specs=list(in_specs) + [HBM_SPEC] * n_r,
        out_specs=own_specs + [HBM_SPEC] * n_r, out_shape=own_shapes + rider.out_shapes,
        scratch_shapes=list(scratch_shapes) + rider.scratch, compiler_params=_params(("arbitrary",) * len(grid)),
    )(*ins, *rider.srcs)
    own = res[:n_out]
    return (own[0] if single else list(own)), list(res[n_out:])


def _matmul(a, b, mode, out_dtype, name, add=None, relu2_a=False, relu2_grad=None,
            tm_t=None, tn_t=None, tk_t=None, rider=None):
    if mode == 'nn':
        (M, K), (K2, N) = a.shape, b.shape
    elif mode == 'nt':
        (M, K), (N, K2) = a.shape, b.shape
    else:
        (K, M), (K2, N) = a.shape, b.shape
    assert K == K2, (name, a.shape, b.shape)
    if mode == 'tn':
        d_tm, d_tn, d_tk = 1024, 1024, 2048
    else:
        d_tm, d_tn, d_tk = (2048 if K <= 1024 else 1024), 512, MAX_K_TILE
    tm, tn, tk = _pick(M, tm_t or d_tm), _pick(N, tn_t or d_tn), _pick(K, tk_t or d_tk)
    gm, gn, nk = M // tm, N // tn, K // tk
    if mode == 'nn':
        a_spec = pl.BlockSpec((tm, tk), lambda i, j, k: (i, k))
        b_spec = pl.BlockSpec((tk, tn), lambda i, j, k: (k, j))
        dims = (((1,), (0,)), ((), ()))
    elif mode == 'nt':
        a_spec = pl.BlockSpec((tm, tk), lambda i, j, k: (i, k))
        b_spec = pl.BlockSpec((tn, tk), lambda i, j, k: (j, k))
        dims = (((1,), (1,)), ((), ()))
    else:
        a_spec = pl.BlockSpec((tk, tm), lambda i, j, k: (k, i))
        b_spec = pl.BlockSpec((tk, tn), lambda i, j, k: (k, j))
        dims = (((0,), (0,)), ((), ()))
    o_spec = pl.BlockSpec((tm, tn), lambda i, j, k: (i, j))
    has_add, has_e = add is not None, relu2_grad is not None

    def body(*refs):
        a_ref, b_ref = refs[0], refs[1]
        pos = 2
        add_ref = e_ref = None
        if has_add:
            add_ref = refs[pos]
            pos += 1
        if has_e:
            e_ref = refs[pos]
            pos += 1
        o_ref = refs[pos]
        acc_ref = refs[pos + 1] if nk > 1 else None

        av = a_ref[...]
        if relu2_a:
            av = jnp.maximum(av, 0)
            av = av * av
        prod = lax.dot_general(av.astype(BF16), b_ref[...].astype(BF16), dims, preferred_element_type=F32)

        def finish(r):
            if has_add:
                r = r + add_ref[...]
            if has_e:
                r = r * (2.0 * jnp.maximum(e_ref[...].astype(F32), 0.0))
            o_ref[...] = r.astype(out_dtype)

        if nk == 1:
            finish(prod)
        else:
            k = pl.program_id(2)

            @pl.when(k == 0)
            def _():
                acc_ref[...] = prod

            @pl.when(k > 0)
            def _():
                acc_ref[...] += prod

            @pl.when(k == nk - 1)
            def _():
                finish(acc_ref[...])

    ins, specs = [a, b], [a_spec, b_spec]
    if has_add:
        ins.append(add)
        specs.append(o_spec)
    if has_e:
        ins.append(relu2_grad)
        specs.append(o_spec)
    return _call(body, rider, ins, name=name, grid=(gm, gn, nk), in_specs=specs, out_specs=o_spec,
                 out_shape=jax.ShapeDtypeStruct((M, N), out_dtype),
                 scratch_shapes=[pltpu.VMEM((tm, tn), F32)] if nk > 1 else [], sem=("parallel", "parallel", "arbitrary"))


ROW_BLOCK_BYTES = 12 * 1024 * 1024


def _row_tile(rows, row_bytes):
    return _pick(rows, max(16, min(1024, ROW_BLOCK_BYTES // row_bytes)), 16)


def _rowspec(tr, width, col=0):
    return pl.BlockSpec((tr, width), lambda i, col=col: (i, col))


def _fullspec(shape):
    nd = len(shape)
    return pl.BlockSpec(shape, lambda i, nd=nd: (0,) * nd)


def _rms(x, width):
    x = x.astype(F32)
    return lax.rsqrt(jnp.sum(x * x, axis=-1, keepdims=True) * (1.0 / width) + EPS)


def _rms_bwd_rows(x, g, dy, width):
    x, dy = x.astype(F32), dy.astype(F32)
    r = _rms(x, width)
    xh = x * r
    dn = dy * g
    dx = r * (dn - xh * (jnp.sum(dn * xh, axis=-1, keepdims=True) * (1.0 / width)))
    return dx, dy * xh


def _acc_rows(ref, val, first):
    s = jnp.sum(val, axis=0, keepdims=True)

    @pl.when(first)
    def _():
        ref[...] = s

    @pl.when(jnp.logical_not(first))
    def _():
        ref[...] += s


def _rms_fwd(x, g, name, rider=None):
    rows, width = x.shape
    tr = _row_tile(rows, 6 * width)

    def body(x_ref, g_ref, o_ref):
        xv = x_ref[...]
        o_ref[...] = (xv * _rms(xv, width) * g_ref[...]).astype(BF16)

    return _call(body, rider, [x, g], name=name, grid=(rows // tr,),
                 in_specs=[_rowspec(tr, width), _fullspec((1, width))], out_specs=_rowspec(tr, width),
                 out_shape=jax.ShapeDtypeStruct((rows, width), BF16), scratch_shapes=[], sem=("parallel",))


def _rms_bwd(x, g, dy, res, name, dx_dtypes=(F32,)):
    rows, width = x.shape
    tr = _row_tile(rows, 18 * width)
    has_res = res is not None
    n_in = 4 if has_res else 3

    def body(*refs):
        x_ref, g_ref, dy_ref = refs[:3]
        dx, dgv = _rms_bwd_rows(x_ref[...], g_ref[...], dy_ref[...], width)
        if has_res:
            dx = dx + refs[3][...]
        for ref, dt in zip(refs[n_in:], dx_dtypes):
            ref[...] = dx.astype(dt)
        _acc_rows(refs[-1], dgv, pl.program_id(0) == 0)

    ins = [x, g, dy] + ([res] if has_res else [])
    specs = [_rowspec(tr, width), _fullspec((1, width)), _rowspec(tr, width)] + ([_rowspec(tr, width)] if has_res else [])
    return pl.pallas_call(
        body, name=name, grid=(rows // tr,), in_specs=specs,
        out_specs=[_rowspec(tr, width)] * len(dx_dtypes) + [_fullspec((1, width))],
        out_shape=[jax.ShapeDtypeStruct((rows, width), dt) for dt in dx_dtypes] + [jax.ShapeDtypeStruct((1, width), F32)],
        compiler_params=_params(("arbitrary",)),
    )(*ins)


_GELU_C = math.sqrt(2.0 / math.pi)


def _gelu(x):
    t = jnp.tanh(_GELU_C * (x + 0.044715 * (x * x * x)))
    return 0.5 * x * (1.0 + t), t


def _gelu_grad(x, t):
    return 0.5 * (1.0 + t) + 0.5 * x * (1.0 - t * t) * (_GELU_C * (1.0 + 3.0 * 0.044715 * (x * x)))


def _gm_forward_rows(zu, zv, gln, bln, wc_ref, bst, n_chunk):
    u, tu = _gelu(zu)
    a, ta = _gelu(zv)
    mu = jnp.mean(a, axis=-1, keepdims=True)
    ac = a - mu
    rs = lax.rsqrt(jnp.mean(ac * ac, axis=-1, keepdims=True) + EPS)
    n = ac * rs
    v = n * gln + bln
    vb = v.astype(BF16)
    rows = []
    for c in range(n_chunk):
        cols = []
        for g in range(GM_GROUPS):
            vc = vb[c * GM_CHUNK:(c + 1) * GM_CHUNK, g * LANES:(g + 1) * LANES]
            mixed = jnp.dot(wc_ref[g], vc, preferred_element_type=F32) + bst[g]
            cols.append(mixed)
        rows.append(jnp.concatenate(cols, axis=1))
    mixed = jnp.concatenate(rows, axis=0) if n_chunk > 1 else rows[0]
    return u, tu, ta, n, rs, v, mixed


def _gm_fwd(z, gln, bln, wc, bst, name):
    rows = z.shape[0]
    tr = _pick(rows, 512, GM_CHUNK)
    n_chunk = tr // GM_CHUNK

    def body(zu_ref, zv_ref, gln_ref, bln_ref, wc_ref, bst_ref, o_ref):
        u, _, _, _, _, _, mixed = _gm_forward_rows(zu_ref[...].astype(F32), zv_ref[...].astype(F32), gln_ref[...], bln_ref[...], wc_ref,
                                                   bst_ref, n_chunk)
        o_ref[...] = (u * mixed).astype(BF16)

    return pl.pallas_call(
        body, name=name, grid=(rows // tr,),
        in_specs=[_rowspec(tr, GM_WIDTH, Z_GM // GM_WIDTH), _rowspec(tr, GM_WIDTH, Z_GM // GM_WIDTH + 1),_fullspec((1, GM_WIDTH)), _fullspec((1, GM_WIDTH)),
                  _fullspec((GM_GROUPS, GM_CHUNK, GM_CHUNK)), _fullspec((GM_GROUPS, GM_CHUNK, LANES))],
        out_specs=_rowspec(tr, GM_WIDTH), out_shape=jax.ShapeDtypeStruct((rows, GM_WIDTH), BF16),
        compiler_params=_params(("parallel",)),
    )(z, z, gln, bln, wc, bst)


ANY_SPEC = pl.BlockSpec(memory_space=pl.ANY)


def _gm_bwd(z, dy, gln, bln, wc, wct, bst, dz, name):
    rows = z.shape[0]
    tr = _pick(rows, 512, GM_CHUNK)
    n_chunk = tr // GM_CHUNK

    def body(zu_ref, zv_ref, dy_ref, gln_ref, bln_ref, wc_ref, wct_ref, bst_ref, _, dz_ref, dws_ref, dbs_ref, dgl_ref,
             dbl_ref):
        first = pl.program_id(0) == 0
        zu, zv, gln = zu_ref[...].astype(F32), zv_ref[...].astype(F32), gln_ref[...]
        u, tu, ta, n, rs, v, mixed = _gm_forward_rows(zu, zv, gln, bln_ref[...], wc_ref, bst_ref, n_chunk)
        dyv = dy_ref[...].astype(F32)
        dzu = dyv * mixed * _gelu_grad(zu, tu)
        dmix = dyv * u
        dmb = dmix.astype(BF16)
        vb = v.astype(BF16)
        dv_rows, dws, dbs = [], [None] * GM_GROUPS, None
        for c in range(n_chunk):
            rsl = slice(c * GM_CHUNK, (c + 1) * GM_CHUNK)
            cols = []
            for g in range(GM_GROUPS):
                csl = slice(g * LANES, (g + 1) * LANES)
                dmc = dmb[rsl, csl]
                cols.append(jnp.dot(wct_ref[g], dmc, preferred_element_type=F32))
                w_part = lax.dot_general(dmc, vb[rsl, csl], (((1,), (1,)), ((), ())), preferred_element_type=F32)
                dws[g] = w_part if dws[g] is None else dws[g] + w_part
            dv_rows.append(jnp.concatenate(cols, axis=1))
            dbs = dmix[rsl, :] if dbs is None else dbs + dmix[rsl, :]
        dv = jnp.concatenate(dv_rows, axis=0) if n_chunk > 1 else dv_rows[0]
        dn = dv * gln
        da = rs * (dn - jnp.mean(dn, axis=-1, keepdims=True) - n * jnp.mean(dn * n, axis=-1, keepdims=True))
        dzv = da * _gelu_grad(zv, ta)
        dz_ref[:, 0:GM_WIDTH] = dzu.astype(BF16)
        dz_ref[:, GM_WIDTH:2 * GM_WIDTH] = dzv.astype(BF16)
        _acc_rows(dgl_ref, dv * n, first)
        _acc_rows(dbl_ref, dv, first)

        @pl.when(first)
        def _():
            for g in range(GM_GROUPS):
                dws_ref[g] = dws[g]
            dbs_ref[...] = dbs

        @pl.when(jnp.logical_not(first))
        def _():
            for g in range(GM_GROUPS):
                dws_ref[g] += dws[g]
            dbs_ref[...] += dbs

    wspec = _fullspec((GM_GROUPS, GM_CHUNK, GM_CHUNK))
    return pl.pallas_call(
        body, name=name, grid=(rows // tr,),
        in_specs=[_rowspec(tr, GM_WIDTH, Z_GM // GM_WIDTH), _rowspec(tr, GM_WIDTH, Z_GM // GM_WIDTH + 1),
                  _rowspec(tr, GM_WIDTH), _fullspec((1, GM_WIDTH)), _fullspec((1, GM_WIDTH)), wspec, wspec, wspec, ANY_SPEC],
        out_specs=[_rowspec(tr, 2 * GM_WIDTH, Z_GM // (2 * GM_WIDTH)), wspec, _fullspec((GM_CHUNK, GM_WIDTH)),
                   _fullspec((1, GM_WIDTH)), _fullspec((1, GM_WIDTH))],
        out_shape=[jax.ShapeDtypeStruct(dz.shape, dz.dtype), jax.ShapeDtypeStruct((GM_GROUPS, GM_CHUNK, GM_CHUNK), F32),
                   jax.ShapeDtypeStruct((GM_CHUNK, GM_WIDTH), F32), jax.ShapeDtypeStruct((1, GM_WIDTH), F32),
                   jax.ShapeDtypeStruct((1, GM_WIDTH), F32)],
        input_output_aliases={8: 0}, compiler_params=_params(("arbitrary",)),
    )(z, z, dy, gln, bln, wc, wct, bst, dz)


def _lat_fwd(z, g_cq, g_ckv, name):
    rows = z.shape[0]
    tr = _row_tile(rows, 4 * MLA_W)

    def body(z_ref, gq_ref, gkv_ref, nq_ref, nkv_ref):
        zb = z_ref[...]
        cq, ckv = zb[:, 0:Q_LORA], zb[:, Q_LORA:Q_LORA + KV_LORA]
        nq_ref[...] = (cq * _rms(cq, Q_LORA) * gq_ref[...]).astype(BF16)
        nkv_ref[...] = (ckv * _rms(ckv, KV_LORA) * gkv_ref[...]).astype(BF16)

    return pl.pallas_call(
        body, name=name, grid=(rows // tr,),
        in_specs=[_rowspec(tr, MLA_W, Z_MLA // MLA_W), _fullspec((1, Q_LORA)), _fullspec((1, KV_LORA))],
        out_specs=[_rowspec(tr, Q_LORA), _rowspec(tr, KV_LORA)],
        out_shape=[jax.ShapeDtypeStruct((rows, Q_LORA), BF16), jax.ShapeDtypeStruct((rows, KV_LORA), BF16)],
        compiler_params=_params(("parallel",)),
    )(z, g_cq, g_ckv)


def _lat_bwd(z, dnq, dnkv, dkpe, g_cq, g_ckv, dz, name):
    rows = z.shape[0]
    tr = _row_tile(rows, 8 * MLA_W)

    def body(z_ref, dnq_ref, dnkv_ref, dkpe_ref, gq_ref, gkv_ref, _, dz_ref, dgq_ref, dgkv_ref):
        first = pl.program_id(0) == 0
        zb = z_ref[...]
        dcq, dgq = _rms_bwd_rows(zb[:, 0:Q_LORA], gq_ref[...], dnq_ref[...], Q_LORA)
        dckv, dgkv = _rms_bwd_rows(zb[:, Q_LORA:Q_LORA + KV_LORA], gkv_ref[...], dnkv_ref[...], KV_LORA)
        dz_ref[:, 0:Q_LORA] = dcq.astype(BF16)
        dz_ref[:, Q_LORA:Q_LORA + KV_LORA] = dckv.astype(BF16)
        dz_ref[:, Q_LORA + KV_LORA:MLA_W] = dkpe_ref[...].astype(BF16)
        _acc_rows(dgq_ref, dgq, first)
        _acc_rows(dgkv_ref, dgkv, first)

    return pl.pallas_call(
        body, name=name, grid=(rows // tr,),
        in_specs=[_rowspec(tr, MLA_W, Z_MLA // MLA_W), _rowspec(tr, Q_LORA), _rowspec(tr, KV_LORA), _rowspec(tr, LANES),
                  _fullspec((1, Q_LORA)), _fullspec((1, KV_LORA)), ANY_SPEC],
        out_specs=[_rowspec(tr, MLA_W, Z_MLA // MLA_W), _fullspec((1, Q_LORA)), _fullspec((1, KV_LORA))],
        out_shape=[jax.ShapeDtypeStruct(dz.shape, dz.dtype), jax.ShapeDtypeStruct((1, Q_LORA), F32),
                   jax.ShapeDtypeStruct((1, KV_LORA), F32)],
        input_output_aliases={6: 0}, compiler_params=_params(("arbitrary",)),
    )(z, dnq, dnkv, dkpe, g_cq, g_ckv, dz)


def _rope(y, cc, ss):
    return y * cc + pltpu.roll(y, 64, 1) * ss


def _rope_bwd(d, cc, ss):
    return d * cc + pltpu.roll(d * ss, 64, 1)


def _qk_fwd(q, kv, z, cc, ss, gqn, gqp, gkn, gkp, name):
    rows = q.shape[0]
    W = MLA_HEADS * HEAD
    tr = _row_tile(rows, 20 * W)
    QS = MLA_SCALE * LOG2E

    def body(q_ref, kv_ref, kpe_ref, cc_ref, ss_ref, gqn_ref, gqp_ref, gkn_ref, gkp_ref, qc_ref, kc_ref, v_ref):
        cc, ss = cc_ref[...], ss_ref[...]
        kpe = kpe_ref[...]
        kp = _rope(kpe * _rms(kpe, MLA_ROPE) * gkp_ref[...], cc, ss).astype(BF16)
        for h in range(MLA_HEADS):
            qn = q_ref[:, h * HEAD:(h + 1) * HEAD]
            qp = q_ref[:, W + h * HEAD:W + (h + 1) * HEAD]
            kn = kv_ref[:, h * HEAD:(h + 1) * HEAD]
            qc_ref[:, h * QCAT:h * QCAT + HEAD] = (qn * _rms(qn, HEAD) * gqn_ref[...] * QS).astype(BF16)
            qc_ref[:, h * QCAT + HEAD:(h + 1) * QCAT] = (_rope(qp * _rms(qp, MLA_ROPE) * gqp_ref[...], cc, ss) * QS).astype(BF16)
            kc_ref[:, h * QCAT:h * QCAT + HEAD] = (kn * _rms(kn, HEAD) * gkn_ref[...]).astype(BF16)
            kc_ref[:, h * QCAT + HEAD:(h + 1) * QCAT] = kp
        v_ref[...] = kv_ref[:, W:2 * W].astype(BF16)

    g = _fullspec((1, HEAD))
    return pl.pallas_call(
        body, name=name, grid=(rows // tr,),
        in_specs=[_rowspec(tr, 2 * W), _rowspec(tr, 2 * W), _rowspec(tr, LANES, Z_KPE // LANES), _rowspec(tr, LANES),
                  _rowspec(tr, LANES), g, g, g, g],
        out_specs=[_rowspec(tr, MLA_HEADS * QCAT), _rowspec(tr, MLA_HEADS * QCAT), _rowspec(tr, W)],
        out_shape=[jax.ShapeDtypeStruct((rows, MLA_HEADS * QCAT), BF16), jax.ShapeDtypeStruct((rows, MLA_HEADS * QCAT), BF16),
                   jax.ShapeDtypeStruct((rows, W), BF16)],
        compiler_params=_params(("parallel",)),
    )(q, kv, z, cc, ss, gqn, gqp, gkn, gkp)


def _qk_bwd(q, kv, z, cc, ss, gqn, gqp, gkn, gkp, dqc, dkc, dv, name):
    rows = q.shape[0]
    W = MLA_HEADS * HEAD
    tr = _row_tile(rows, 40 * W)

    def body(q_ref, kv_ref, kpe_ref, cc_ref, ss_ref, gqn_ref, gqp_ref, gkn_ref, gkp_ref, dqc_ref, dkc_ref, dv_ref,
             dq_ref, dkv_ref, dkpe_ref, dgqn_ref, dgqp_ref, dgkn_ref, dgkp_ref):
        first = pl.program_id(0) == 0
        cc, ss = cc_ref[...], ss_ref[...]
        sqn = sqp = skn = dkp = None
        for h in range(MLA_HEADS):
            dx, dg = _rms_bwd_rows(q_ref[:, h * HEAD:(h + 1) * HEAD], gqn_ref[...], dqc_ref[:, h * QCAT:h * QCAT + HEAD], HEAD)
            dq_ref[:, h * HEAD:(h + 1) * HEAD] = dx.astype(BF16)
            sqn = dg if sqn is None else sqn + dg
            dy = _rope_bwd(dqc_ref[:, h * QCAT + HEAD:(h + 1) * QCAT], cc, ss)
            dx, dg = _rms_bwd_rows(q_ref[:, W + h * HEAD:W + (h + 1) * HEAD], gqp_ref[...], dy, MLA_ROPE)
            dq_ref[:, W + h * HEAD:W + (h + 1) * HEAD] = dx.astype(BF16)
            sqp = dg if sqp is None else sqp + dg
            dx, dg = _rms_bwd_rows(kv_ref[:, h * HEAD:(h + 1) * HEAD], gkn_ref[...], dkc_ref[:, h * QCAT:h * QCAT + HEAD], HEAD)
            dkv_ref[:, h * HEAD:(h + 1) * HEAD] = dx.astype(BF16)
            skn = dg if skn is None else skn + dg
            part = dkc_ref[:, h * QCAT + HEAD:(h + 1) * QCAT].astype(F32)
            dkp = part if dkp is None else dkp + part
        dkv_ref[:, W:2 * W] = dv_ref[...].astype(BF16)
        dx, dg = _rms_bwd_rows(kpe_ref[...], gkp_ref[...], _rope_bwd(dkp, cc, ss), MLA_ROPE)
        dkpe_ref[...] = dx
        _acc_rows(dgqn_ref, sqn, first)
        _acc_rows(dgqp_ref, sqp, first)
        _acc_rows(dgkn_ref, skn, first)
        _acc_rows(dgkp_ref, dg, first)

    g = _fullspec((1, HEAD))
    gs = jax.ShapeDtypeStruct((1, HEAD), F32)
    return pl.pallas_call(
        body, name=name, grid=(rows // tr,),
        in_specs=[_rowspec(tr, 2 * W), _rowspec(tr, 2 * W), _rowspec(tr, LANES, Z_KPE // LANES), _rowspec(tr, LANES),
                  _rowspec(tr, LANES), g, g, g, g, _rowspec(tr, MLA_HEADS * QCAT), _rowspec(tr, MLA_HEADS * QCAT),
                  _rowspec(tr, W)],
        out_specs=[_rowspec(tr, 2 * W), _rowspec(tr, 2 * W), _rowspec(tr, LANES), g, g, g, g],
        out_shape=[jax.ShapeDtypeStruct((rows, 2 * W), BF16), jax.ShapeDtypeStruct((rows, 2 * W), BF16),
                   jax.ShapeDtypeStruct((rows, LANES), F32), gs, gs, gs, gs],
        compiler_params=_params(("arbitrary",)),
    )(q, kv, z, cc, ss, gqn, gqp, gkn, gkp, dqc, dkc, dv)


def _headnorm_fwd(x, col, nheads, g, out_scale, name):
    rows = x.shape[0]
    W = nheads * HEAD
    tr = _row_tile(rows, 6 * W)

    def body(x_ref, g_ref, o_ref):
        for h in range(nheads):
            xv = x_ref[:, h * HEAD:(h + 1) * HEAD]
            o_ref[:, h * HEAD:(h + 1) * HEAD] = (xv * _rms(xv, HEAD) * g_ref[...] * out_scale).astype(BF16)

    return pl.pallas_call(
        body, name=name, grid=(rows // tr,),
        in_specs=[_rowspec(tr, W, col), _fullspec((1, HEAD))], out_specs=_rowspec(tr, W),
        out_shape=jax.ShapeDtypeStruct((rows, W), BF16), compiler_params=_params(("parallel",)),
    )(x, g)


def _headnorm_bwd(x, col, nheads, g, dy, tail, name, into=None):
    rows = x.shape[0]
    W = nheads * HEAD
    tr = _row_tile(rows, 12 * W)
    has_tail = tail is not None
    WO = 2 * W if has_tail else W

    def body(*refs):
        if into is not None:
            x_ref, g_ref, dy_ref, _, dx_ref, dg_ref = refs
        elif has_tail:
            x_ref, g_ref, dy_ref, t_ref, dx_ref, dg_ref = refs
        else:
            x_ref, g_ref, dy_ref, dx_ref, dg_ref = refs
        acc = None
        for h in range(nheads):
            sl = slice(h * HEAD, (h + 1) * HEAD)
            dx, dg = _rms_bwd_rows(x_ref[:, sl], g_ref[...], dy_ref[:, sl], HEAD)
            dx_ref[:, sl] = dx.astype(BF16)
            acc = dg if acc is None else acc + dg
        if has_tail:
            dx_ref[:, W:2 * W] = t_ref[...].astype(BF16)
        _acc_rows(dg_ref, acc, pl.program_id(0) == 0)

    ins = [x, g, dy] + ([tail] if has_tail else [])
    specs = [_rowspec(tr, W, col), _fullspec((1, HEAD)), _rowspec(tr, W)] + ([_rowspec(tr, W)] if has_tail else [])
    dx_spec, dx_shape, aliases = _rowspec(tr, WO), jax.ShapeDtypeStruct((rows, WO), BF16), {}
    if into is not None:
        assert not has_tail
        ins, specs = ins + [into[0]], specs + [ANY_SPEC]
        dx_spec, dx_shape, aliases = _rowspec(tr, W, into[1]), jax.ShapeDtypeStruct(into[0].shape, into[0].dtype), {3: 0}
    return pl.pallas_call(
        body, name=name, grid=(rows // tr,), in_specs=specs,
        out_specs=[dx_spec, _fullspec((1, HEAD))], out_shape=[dx_shape, jax.ShapeDtypeStruct((1, HEAD), F32)],
        input_output_aliases=aliases, compiler_params=_params(("arbitrary",)),
    )(*ins)


def _sigmoid(x):
    return 1.0 / (1.0 + jnp.exp(-x.astype(F32)))


def _merge_fwd(z, y_gm, y_mla, y_mem, name):
    rows = z.shape[0]
    tr = _row_tile(rows, 14 * D_MODEL)

    def body(g0_ref, g1_ref, g2_ref, a_ref, b_ref, c_ref, o_ref):
        m = _sigmoid(g0_ref[...]) * a_ref[...] + _sigmoid(g1_ref[...]) * b_ref[...] + _sigmoid(g2_ref[...]) * c_ref[...]
        o_ref[...] = m.astype(BF16)

    r = _rowspec(tr, D_MODEL)
    return pl.pallas_call(
        body, name=name, grid=(rows // tr,),
        in_specs=[_rowspec(tr, D_MODEL, 0), _rowspec(tr, D_MODEL, 1), _rowspec(tr, D_MODEL, 2),r, r, r],
        out_specs=r, out_shape=jax.ShapeDtypeStruct((rows, D_MODEL), BF16), compiler_params=_params(("parallel",)),
    )(z, z, z, y_gm, y_mla, y_mem)


def _merge_bwd(z, y_gm, y_mla, y_mem, dm, name):
    rows = z.shape[0]
    tr = _row_tile(rows, 24 * D_MODEL)

    def body(g0_ref, g1_ref, g2_ref, a_ref, b_ref, c_ref, dm_ref, da_ref, db_ref, dc_ref, dzg_ref):
        dmv = dm_ref[...].astype(F32)
        for k, (g_ref, y_ref, dy_ref) in enumerate(((g0_ref, a_ref, da_ref), (g1_ref, b_ref, db_ref), (g2_ref, c_ref, dc_ref))):
            s = _sigmoid(g_ref[...])
            dy_ref[...] = (dmv * s).astype(BF16)
            dzg_ref[:, k * D_MODEL:(k + 1) * D_MODEL] = (dmv * y_ref[...] * s * (1.0 - s)).astype(BF16)

    r = _rowspec(tr, D_MODEL)
    o = jax.ShapeDtypeStruct((rows, D_MODEL), BF16)
    return pl.pallas_call(
        body, name=name, grid=(rows // tr,),
        in_specs=[_rowspec(tr, D_MODEL, 0), _rowspec(tr, D_MODEL, 1), _rowspec(tr, D_MODEL, 2),r, r, r, r],
        out_specs=[r, r, r, _rowspec(tr, 3 * D_MODEL, 0)],
        out_shape=[o, o, o, jax.ShapeDtypeStruct((rows, Z_COLS), BF16)],
        compiler_params=_params(("parallel",)),
    )(z, z, z, y_gm, y_mla, y_mem, dm)


def _loss_head(y, target, name):
    rows, width = y.shape
    tr = _row_tile(rows, 14 * width)

    def body(y_ref, t_ref, dy_ref, dyb_ref, l_ref):
        e = y_ref[...] - t_ref[...]
        dy = e * (1.0 / width)
        dy_ref[...] = dy
        dyb_ref[...] = dy.astype(BF16)
        e2 = e * e
        part = e2[:, 0:LANES]
        for k in range(1, width // LANES):
            part = part + e2[:, k * LANES:(k + 1) * LANES]
        _acc_rows(l_ref, part, pl.program_id(0) == 0)

    return pl.pallas_call(
        body, name=name, grid=(rows // tr,),
        in_specs=[_rowspec(tr, width), _rowspec(tr, width)],
        out_specs=[_rowspec(tr, width), _rowspec(tr, width), _fullspec((1, LANES))],
        out_shape=[jax.ShapeDtypeStruct((rows, width), F32), jax.ShapeDtypeStruct((rows, width), BF16),
                   jax.ShapeDtypeStruct((1, LANES), F32)],
        compiler_params=_params(("arbitrary",)),
    )(y, target)


_NT = (((1,), (1,)), ((), ()))
_TN = (((0,), (0,)), ((), ()))


def _diag_mask(s):
    row = lax.broadcasted_iota(jnp.int32, s.shape, 0)
    col = lax.broadcasted_iota(jnp.int32, s.shape, 1)
    return jnp.where(row >= col, s, NEG)


def _attn_fwd(q, k, v, nb, nheads, dk, v_col0, causal, name, rider=None):
    S, Skv = q.shape[0] // nb, k.shape[0] // nb
    tq = _pick(Skv, ATT_TILE) if causal else _pick(S, 4 * ATT_TILE)
    nq = S // tq

    def body(q_ref, k_ref, v_ref, o_ref, lse_ref):
        for i in range(nq):
            r0 = i * tq
            qb = q_ref[r0:r0 + tq, :]
            if causal:
                spans = ([(0, r0, False)] if i > 0 else []) + [(r0, r0 + tq, True)]
            else:
                spans = [(0, Skv, False)]
            scores = []
            for a, b, masked in spans:
                s = lax.dot_general(qb, k_ref[a:b, :], _NT, preferred_element_type=F32)
                scores.append(_diag_mask(s) if masked else s)
            m = functools.reduce(jnp.maximum, [jnp.max(s, axis=-1, keepdims=True) for s in scores])
            l = acc = None
            for s, (a, b, _) in zip(scores, spans):
                p = jnp.exp2(s - m)
                lp = jnp.sum(p, axis=-1, keepdims=True)
                ap = jnp.dot(p.astype(BF16), v_ref[a:b, :].astype(BF16), preferred_element_type=F32)
                l, acc = (lp, ap) if l is None else (l + lp, acc + ap)
            o_ref[r0:r0 + tq, :] = (acc / l).astype(BF16)
            lse_ref[r0:r0 + tq, :] = m + jnp.log2(l)

    ins = [q, k, v]
    in_specs = [pl.BlockSpec((S, dk), lambda b, h: (b, h)), pl.BlockSpec((Skv, dk), lambda b, h: (b, h)),
                pl.BlockSpec((Skv, HEAD), lambda b, h: (b, v_col0 + h))]
    out_specs = [pl.BlockSpec((S, HEAD), lambda b, h: (b, h)), pl.BlockSpec((None, S, 1), lambda b, h: (h, b, 0))]
    out_shape = [jax.ShapeDtypeStruct((nb * S, nheads * HEAD), BF16), jax.ShapeDtypeStruct((nheads, nb * S, 1), F32)]
    return _call(body, rider, ins, name=name, grid=(nb, nheads), in_specs=in_specs, out_specs=out_specs,
                 out_shape=out_shape, scratch_shapes=[], sem=("parallel", "parallel"))


def _attn_bwd(q, k, v, o, do, lse, nb, nheads, dk, v_col0, scale, causal, name, rider=None):
    S, Skv = q.shape[0] // nb, k.shape[0] // nb
    tk = _pick(Skv, ATT_TILE)
    nkv = Skv // tk

    def body(q_ref, k_ref, v_ref, o_ref, do_ref, lse_ref, dq_ref, dk_ref, dv_ref, delta_ref, dob_ref, dqa_ref):
        dov = do_ref[...]
        delta_ref[...] = jnp.sum(o_ref[...].astype(F32) * dov.astype(F32), axis=-1, keepdims=True)
        dob_ref[...] = dov.astype(BF16)

        for j in range(nkv):
            c0 = j * tk
            kb = k_ref[c0:c0 + tk, :]
            vb = v_ref[c0:c0 + tk, :].astype(BF16)
            if causal:
                spans = [(c0, c0 + tk, True)] + ([(c0 + tk, S, False)] if c0 + tk < S else [])
            else:
                spans = [(0, S, False)]
            dk_acc = dv_acc = None
            for a, b, masked in spans:
                qb = q_ref[a:b, :]
                dob = dob_ref[a:b, :]
                s = lax.dot_general(qb, kb, _NT, preferred_element_type=F32)
                if masked:
                    s = _diag_mask(s)
                p = jnp.exp2(s - lse_ref[a:b, :])
                dp = lax.dot_general(dob, vb, _NT, preferred_element_type=F32)
                ds = (p * (dp - delta_ref[a:b, :])).astype(BF16)
                dv_p = lax.dot_general(p.astype(BF16), dob, _TN, preferred_element_type=F32)
                dk_p = lax.dot_general(ds, qb, _TN, preferred_element_type=F32)
                dk_acc, dv_acc = (dk_p, dv_p) if dk_acc is None else (dk_acc + dk_p, dv_acc + dv_p)
                dq_p = jnp.dot(ds, kb, preferred_element_type=F32) * scale
                if j == 0:
                    dqa_ref[a:b, :] = dq_p
                else:
                    dqa_ref[a:b, :] += dq_p
            dk_ref[c0:c0 + tk, :] = (dk_acc * LN2).astype(BF16)
            dv_ref[c0:c0 + tk, :] = dv_acc.astype(BF16)
        dq_ref[...] = dqa_ref[...].astype(BF16)

    ins = [q, k, v, o, do, lse]
    in_specs = [pl.BlockSpec((S, dk), lambda b, h: (b, h)), pl.BlockSpec((Skv, dk), lambda b, h: (b, h)),
                pl.BlockSpec((Skv, HEAD), lambda b, h: (b, v_col0 + h)), pl.BlockSpec((S, HEAD), lambda b, h: (b, h)),
                pl.BlockSpec((S, HEAD), lambda b, h: (b, h)), pl.BlockSpec((None, S, 1), lambda b, h: (h, b, 0))]
    out_specs = [pl.BlockSpec((S, dk), lambda b, h: (b, h)), pl.BlockSpec((Skv, dk), lambda b, h: (b, h)),
                 pl.BlockSpec((Skv, HEAD), lambda b, h: (b, h))]
    out_shape = [jax.ShapeDtypeStruct((nb * S, nheads * dk), BF16), jax.ShapeDtypeStruct((nb * Skv, nheads * dk), BF16),
                 jax.ShapeDtypeStruct((nb * Skv, nheads * HEAD), BF16)]
    return _call(body, rider, ins, name=name, grid=(nb, nheads), in_specs=in_specs, out_specs=out_specs,
                 out_shape=out_shape,
                 scratch_shapes=[pltpu.VMEM((S, 1), F32), pltpu.VMEM((S, HEAD), BF16), pltpu.VMEM((S, dk), F32)],
                 sem=("parallel", "parallel"))


def _spread_rope(a):
    zero = jnp.zeros(a.shape[:-1] + (32,), a.dtype)
    return jnp.concatenate([a[..., :32], zero, a[..., 32:], zero], axis=-1)


def _gather_rope(a):
    return jnp.concatenate([a[..., 0:32], a[..., 64:96]], axis=-1)


def _win_layout(w):
    return jnp.concatenate([w[:, C_ZG:C_END], w[:, C_ZU:C_CQ], w[:, C_QM:C_ZG], w[:, C_CQ:C_CKV], w[:, C_CKV:C_KPE],
                            _spread_rope(w[:, C_KPE:C_QM])], axis=1)


def _win_unlayout(d):
    return jnp.concatenate([d[:, Z_GM:Z_QM], d[:, Z_MLA:Z_MLA + Q_LORA], d[:, Z_MLA + Q_LORA:Z_KPE],
                            _gather_rope(d[:, Z_KPE:Z_COLS]), d[:, Z_QM:Z_MLA], d[:, 0:Z_GM]], axis=1)


def _wuq_layout(w):
    r = w.reshape(Q_LORA, MLA_HEADS, HEAD + MLA_ROPE)
    return jnp.concatenate([r[:, :, :HEAD].reshape(Q_LORA, -1), _spread_rope(r[:, :, HEAD:]).reshape(Q_LORA, -1)], axis=1)


def _wuq_unlayout(d):
    n = d[:, :MLA_HEADS * HEAD].reshape(Q_LORA, MLA_HEADS, HEAD)
    p = _gather_rope(d[:, MLA_HEADS * HEAD:].reshape(Q_LORA, MLA_HEADS, HEAD))
    return jnp.concatenate([n, p], axis=-1).reshape(Q_LORA, -1)


def _wukv_layout(w):
    r = w.reshape(KV_LORA, MLA_HEADS, 2 * HEAD)
    return jnp.concatenate([r[:, :, :HEAD].reshape(KV_LORA, -1), r[:, :, HEAD:].reshape(KV_LORA, -1)], axis=1)


def _wukv_unlayout(d):
    k = d[:, :MLA_HEADS * HEAD].reshape(KV_LORA, MLA_HEADS, HEAD)
    v = d[:, MLA_HEADS * HEAD:].reshape(KV_LORA, MLA_HEADS, HEAD)
    return jnp.concatenate([k, v], axis=-1).reshape(KV_LORA, -1)


AG_MID = ['w_uq', 'w_ukv', 'w_mem_kv', 'w_o_gm', 'w_o_mla', 'w_o_mem', 'w_out']
AG_FFN = ['w_ff1', 'w_ff2']
RS_GROUPS = {'ff2': ['w_ff2'], 'proj': ['w_ff1', 'w_out', 'w_o_gm', 'w_o_mla', 'w_o_mem'],
             'lat': ['w_uq', 'w_ukv', 'w_mem_kv'], 'in': ['w_in']}


def _unride(res, rider):
    return (res, None) if rider is None else res


def _local_step(x, mem, positions, target, P, ws):
    B, S, _ = x.shape
    M = mem.shape[1]
    T = B * S
    x2d = x.reshape(T, D_MODEL)
    mem2d = mem.reshape(B * M, D_MODEL)
    tgt2d = target.reshape(T, D_MODEL)

    def row(v):
        return v.reshape(1, -1).astype(F32)

    inv_freq = ROPE_BASE ** (-jnp.arange(0, MLA_ROPE, 2, dtype=F32) / MLA_ROPE)
    ang = positions.reshape(T).astype(F32)[:, None] * inv_freq
    cos, sin, zero = jnp.cos(ang), jnp.sin(ang), jnp.zeros_like(ang)
    cc = jnp.concatenate([cos, zero, cos, zero], axis=1)
    ss = jnp.concatenate([-sin, zero, sin, zero], axis=1)

    g_mix, g_cq, g_ckv, g_ffn, g_mem = row(P['g_mix']), row(P['g_cq']), row(P['g_ckv']), row(P['g_ffn']), row(P['g_mem'])
    gqn, gkn, gmq, gmk = row(P['g_q_nope']), row(P['g_k_nope']), row(P['g_mq']), row(P['g_mk'])
    gqp, gkp = _spread_rope(row(P['g_q_pe'])), _spread_rope(row(P['g_k_pe']))
    gln, bln = row(P['g_gm_ln']), row(P['b_gm_ln'])
    wc = jnp.tril(P['w_spatial'].astype(F32))
    wct = jnp.swapaxes(wc, 1, 2).astype(BF16)
    wc = wc.astype(BF16)
    bst = jnp.broadcast_to(P['b_spatial'].astype(F32)[:, :, None], (GM_GROUPS, GM_CHUNK, LANES))

    ride = ws.gather(['w_in'])
    h, got = _unride(_rms_fwd(x2d, g_mix, "rms_mix", rider=ride), ride)
    w_in = _win_layout(ws.gathered(['w_in'], got)['w_in']).astype(BF16)
    ride = ws.gather(AG_MID)
    z, got = _unride(_matmul(h, w_in, 'nn', ACT, "mm_in", tn_t=768, rider=ride), ride)
    mid = ws.gathered(AG_MID, got)
    w_uq = _wuq_layout(mid['w_uq']).astype(BF16)
    w_ukv = _wukv_layout(mid['w_ukv']).astype(BF16)
    w_mem_kv, w_o_gm, w_o_mla, w_o_mem, w_out = (mid[n].astype(BF16) for n in ('w_mem_kv', 'w_o_gm', 'w_o_mla', 'w_o_mem',
                                                                                 'w_out'))
    ygm_pre = _gm_fwd(z, gln, bln, wc, bst, "gm_fwd")
    y_gm = _matmul(ygm_pre, w_o_gm, 'nn', ACT, "mm_o_gm")
    nq, nkv = _lat_fwd(z, g_cq, g_ckv, "lat_fwd")
    q = _matmul(nq, w_uq, 'nn', ACT, "mm_uq")
    kv = _matmul(nkv, w_ukv, 'nn', ACT, "mm_ukv")
    qcat, kcat, vv = _qk_fwd(q, kv, z, cc, ss, gqn, gqp, gkn, gkp, "qk_fwd")
    ride = ws.gather(AG_FFN)
    (o, lse), got = _unride(_attn_fwd(qcat, kcat, vv, B, MLA_HEADS, QCAT, 0, True, "mla_attn_fwd", rider=ride), ride)
    ffn = ws.gathered(AG_FFN, got)
    w_ff1, w_ff2 = ffn['w_ff1'].astype(BF16), ffn['w_ff2'].astype(BF16)
    y_mla = _matmul(o, w_o_mla, 'nn', ACT, "mm_o_mla")
    nm = _rms_fwd(mem2d, g_mem, "rms_mem")
    kvm = _matmul(nm, w_mem_kv, 'nn', ACT, "mm_mem_kv")
    qm = _headnorm_fwd(z, Z_QM // (MEM_HEADS * HEAD), MEM_HEADS, gmq, MEM_SCALE * LOG2E, "memq_fwd")
    km = _headnorm_fwd(kvm, 0, MEM_HEADS, gmk, 1.0, "memk_fwd")
    om, lse_m = _attn_fwd(qm, km, kvm, B, MEM_HEADS, HEAD, MEM_HEADS, False, "mem_attn_fwd")
    y_mem = _matmul(om, w_o_mem, 'nn', ACT, "mm_o_mem")
    merged = _merge_fwd(z, y_gm, y_mla, y_mem, "merge_fwd")
    x1 = _matmul(merged, w_out, 'nn', F32, "mm_out", add=x2d)
    h2 = _rms_fwd(x1, g_ffn, "rms_ffn")
    a1 = _matmul(h2, w_ff1, 'nn', BF16, "mm_ff1")
    x2 = _matmul(a1, w_ff2, 'nn', F32, "mm_ff2", add=x1, relu2_a=True)
    dx2, dx2b, loss_part = _loss_head(x2, tgt2d, "loss_head")

    G = {}
    ride = ws.scatter('ff2', {'w_ff2': _matmul(a1, dx2b, 'tn', BF16, "mm_d_ff2", relu2_a=True)})
    da1, got = _unride(_matmul(dx2b, w_ff2, 'nt', BF16, "mm_da1", relu2_grad=a1, rider=ride), ride)
    ws.scattered('ff2', got)
    d_ff1 = _matmul(h2, da1, 'tn', BF16, "mm_d_ff1")
    dh2 = _matmul(da1, w_ff1, 'nt', ACT, "mm_dh2")
    dx1, dx1b, G['g_ffn'] = _rms_bwd(x1, g_ffn, dh2, dx2, "rms_ffn_bwd", dx_dtypes=(F32, BF16))
    d_out = _matmul(merged, dx1b, 'tn', BF16, "mm_d_out")
    dmerged = _matmul(dx1b, w_out, 'nt', ACT, "mm_dmerged")
    dy_gm, dy_mla, dy_mem, dz = _merge_bwd(z, y_gm, y_mla, y_mem, dmerged, "merge_bwd")
    d_o_gm = _matmul(ygm_pre, dy_gm, 'tn', BF16, "mm_d_o_gm")
    d_o_mla = _matmul(o, dy_mla, 'tn', BF16, "mm_d_o_mla")
    d_o_mem = _matmul(om, dy_mem, 'tn', BF16, "mm_d_o_mem")
    dygm_pre = _matmul(dy_gm, w_o_gm, 'nt', ACT, "mm_dygm")
    dz, dws, dbs, G['g_gm_ln'], G['b_gm_ln'] = _gm_bwd(z, dygm_pre, gln, bln, wc, wct, bst, dz, "gm_bwd")
    G['w_spatial'] = jnp.tril(dws)
    G['b_spatial'] = jnp.sum(dbs.reshape(GM_CHUNK, GM_GROUPS, LANES), axis=-1).T
    do = _matmul(dy_mla, w_o_mla, 'nt', ACT, "mm_do")
    ride = ws.scatter('proj', {'w_ff1': d_ff1, 'w_out': d_out, 'w_o_gm': d_o_gm, 'w_o_mla': d_o_mla, 'w_o_mem': d_o_mem})
    (dqc, dkc, dvv), got = _unride(_attn_bwd(qcat, kcat, vv, o, do, lse, B, MLA_HEADS, QCAT, 0, MLA_SCALE, True,
                                             "mla_attn_bwd", rider=ride), ride)
    ws.scattered('proj', got)
    dq, dkv, dkpe, G['g_q_nope'], dgqp, G['g_k_nope'], dgkp = _qk_bwd(q, kv, z, cc, ss, gqn, gqp, gkn, gkp, dqc, dkc, dvv,
                                                                     "qk_bwd")
    G['g_q_pe'], G['g_k_pe'] = _gather_rope(dgqp), _gather_rope(dgkp)
    d_uq = _wuq_unlayout(_matmul(nq, dq, 'tn', BF16, "mm_d_uq"))
    dnq = _matmul(dq, w_uq, 'nt', ACT, "mm_dnq")
    d_ukv = _wukv_unlayout(_matmul(nkv, dkv, 'tn', BF16, "mm_d_ukv"))
    dnkv = _matmul(dkv, w_ukv, 'nt', ACT, "mm_dnkv")
    dz, G['g_cq'], G['g_ckv'] = _lat_bwd(z, dnq, dnkv, dkpe, g_cq, g_ckv, dz, "lat_bwd")
    dom = _matmul(dy_mem, w_o_mem, 'nt', ACT, "mm_dom")
    dqm, dkm, dvm = _attn_bwd(qm, km, kvm, om, dom, lse_m, B, MEM_HEADS, HEAD, MEM_HEADS, MEM_SCALE, False, "mem_attn_bwd")
    dz, G['g_mq'] = _headnorm_bwd(z, Z_QM // (MEM_HEADS * HEAD), MEM_HEADS, gmq, dqm, None, "memq_bwd",
                                  into=(dz, Z_QM // (MEM_HEADS * HEAD)))
    dkvm, G['g_mk'] = _headnorm_bwd(kvm, 0, MEM_HEADS, gmk, dkm, dvm, "memk_bwd")
    d_mem_kv = _matmul(nm, dkvm, 'tn', BF16, "mm_d_mem_kv")
    dnm = _matmul(dkvm, w_mem_kv, 'nt', ACT, "mm_dnm")
    G['g_mem'], = _rms_bwd(mem2d, g_mem, dnm, None, "rms_mem_bwd", dx_dtypes=())
    ride = ws.scatter('lat', {'w_uq': d_uq, 'w_ukv': d_ukv, 'w_mem_kv': d_mem_kv})
    d_in, got = _unride(_matmul(h, dz, 'tn', BF16, "mm_d_in", tn_t=768, rider=ride), ride)
    ws.scattered('lat', got)
    ride = ws.scatter('in', {'w_in': _win_unlayout(d_in)})
    dh, got = _unride(_matmul(dz, w_in, 'nt', ACT, "mm_dh", rider=ride), ride)
    ws.scattered('in', got)
    gx, G['g_mix'] = _rms_bwd(x2d, g_mix, dh, dx1, "rms_mix_bwd")
    return loss_part, gx.reshape(B, S, D_MODEL), G


def _all_gather8(xs, name):
    def body(x_ref, out_ref, send_sems, recv_sems, local_sem):
        x, y, c = lax.axis_index("x"), lax.axis_index("y"), lax.axis_index("c")
        me, sibling = (x, y, c), (x, y, 1 - c)
        chips = [(1 - x, y), (x, 1 - y), (1 - x, 1 - y)]

        def rows(px, py, pc):
            return out_ref.at[4 * px + 2 * py + pc]

        def copy(k, block, to, src=None):
            return pltpu.make_async_remote_copy(
                src_ref=rows(*block) if src is None else src, dst_ref=rows(*block),
                send_sem=send_sems.at[k], recv_sem=recv_sems.at[k], device_id=to, device_id_type=MESH)

        mine = pltpu.make_async_copy(x_ref, rows(*me), local_sem)
        mine.start()
        first = [copy(0, me, sibling, src=x_ref)]
        first += [copy(1 + j, me, (*chip, c), src=x_ref) for j, chip in enumerate(chips)]
        for cp in first:
            cp.start()
        passed = [copy(4 + j, (*chip, c), sibling) for j, chip in enumerate(chips)]
        for j, chip in enumerate(chips):
            copy(1 + j, (*chip, c), me).wait_recv()
            passed[j].start()
        copy(0, sibling, me).wait_recv()
        for j, chip in enumerate(chips):
            copy(4 + j, (*chip, 1 - c), me).wait_recv()
        for cp in first + passed:
            cp.wait_send()
        mine.wait()

    return pl.pallas_call(
        body, name=name, in_specs=[HBM_SPEC], out_specs=HBM_SPEC,
        out_shape=jax.ShapeDtypeStruct((N_DEV,) + xs.shape, xs.dtype),
        scratch_shapes=[pltpu.SemaphoreType.DMA((7,)), pltpu.SemaphoreType.DMA((7,)), pltpu.SemaphoreType.DMA],
    )(xs)


def _adamw_rows(w, g, m, v):
    m2 = ADAM_B1 * m + (1.0 - ADAM_B1) * g
    v2 = ADAM_B2 * v + (1.0 - ADAM_B2) * (g * g)
    m_hat = m2 / (1.0 - ADAM_B1 ** ADAM_STEP)
    v_hat = v2 / (1.0 - ADAM_B2 ** ADAM_STEP)
    delta = -ADAM_LR * (m_hat / (jnp.sqrt(v_hat) + ADAM_EPS) + ADAM_WD * w)
    return delta, m2, v2


def _sum_adamw(parts, w, m, v, name):
    rows, cols = w.shape
    tr = _pick(rows, max(16, 65536 // cols), 16)
    n = parts.shape[0]

    def body(p_ref, w_ref, m_ref, v_ref, g_ref, d_ref, m2_ref, v2_ref):
        g = p_ref[0].astype(F32)
        for k in range(1, n):
            g = g + p_ref[k].astype(F32)
        delta, m2, v2 = _adamw_rows(w_ref[...], g, m_ref[...], v_ref[...])
        g_ref[...] = g
        d_ref[...] = delta
        m2_ref[...] = m2
        v2_ref[...] = v2

    flat = pl.BlockSpec((tr, cols), lambda i: (i, 0))
    out = jax.ShapeDtypeStruct((rows, cols), F32)
    return pl.pallas_call(
        body, name=name, grid=(rows // tr,),
        in_specs=[pl.BlockSpec((n, tr, cols), lambda i: (0, i, 0)), flat, flat, flat], out_specs=[flat] * 4,
        out_shape=[out] * 4, compiler_params=_params(("parallel",)),
    )(parts, w, m, v)


def _small_rows(name):
    n = {'g_mix': 1024, 'g_cq': 384, 'g_ckv': 256, 'g_q_nope': 128, 'g_q_pe': 64, 'g_k_nope': 128, 'g_k_pe': 64,
         'g_gm_ln': 512, 'b_gm_ln': 512, 'w_spatial': GM_GROUPS * GM_CHUNK * GM_CHUNK, 'b_spatial': GM_GROUPS * GM_CHUNK,
         'g_mem': 1024, 'g_mq': 128, 'g_mk': 128, 'g_ffn': 1024}[name]
    return n, -(-n // (8 * LANES)) * 8


def _small_slab(d):
    parts = []
    for name in SMALL:
        n, rows = _small_rows(name)
        parts.append(jnp.pad(d[name].reshape(-1).astype(F32), (0, rows * LANES - n)).reshape(rows, LANES))
    return jnp.concatenate(parts, axis=0)


def _small_unslab(slab, like):
    out, r = {}, 0
    for name in SMALL:
        n, rows = _small_rows(name)
        out[name] = slab[r:r + rows].reshape(-1)[:n].reshape(like[name].shape)
        r += rows
    return out


def _full_from_gathered(gathered, name):
    r, c = BIG_SHAPE[name]
    if BIG_AXIS[name] == 0:
        return gathered.reshape(r, c)
    return gathered.transpose(1, 0, 2).reshape(r, c)


def _shards_of_full(g, name):
    r, c = BIG_SHAPE[name]
    if BIG_AXIS[name] == 0:
        return g.reshape(N_DEV, r // N_DEV, c)
    return g.reshape(r, N_DEV, c // N_DEV).transpose(1, 0, 2)


class _DistWeights:
    def __init__(self, shards):
        self.shards = shards
        self.received = {}

    def gather(self, names):
        return _Gather2([self.shards[n].astype(BF16) for n in names])

    def gathered(self, names, got):
        return {n: _full_from_gathered(g, n) for n, g in zip(names, got)}

    def scatter(self, key, grads):
        return _Exchange([_shards_of_full(grads[n], n) for n in RS_GROUPS[key]], scatter=True)

    def scattered(self, key, got):
        self.received.update(zip(RS_GROUPS[key], got))


def kernel(x, mem, positions, g_mix, w_in, g_cq, w_uq, g_ckv, w_ukv, g_q_nope, g_q_pe, g_k_nope, g_k_pe, g_gm_ln, b_gm_ln, w_spatial, b_spatial, g_mem, w_mem_kv, g_mq, g_mk, w_o_gm, w_o_mla, w_o_mem, w_out, g_ffn, w_ff1, w_ff2, loss_target, m_g_mix, m_w_in, m_g_cq, m_w_uq, m_g_ckv, m_w_ukv, m_g_q_nope, m_g_q_pe, m_g_k_nope, m_g_k_pe, m_g_gm_ln, m_b_gm_ln, m_w_spatial, m_b_spatial, m_g_mem, m_w_mem_kv, m_g_mq, m_g_mk, m_w_o_gm, m_w_o_mla, m_w_o_mem, m_w_out, m_g_ffn, m_w_ff1, m_w_ff2, v_g_mix, v_w_in, v_g_cq, v_w_uq, v_g_ckv, v_w_ukv, v_g_q_nope, v_g_q_pe, v_g_k_nope, v_g_k_pe, v_g_gm_ln, v_b_gm_ln, v_w_spatial, v_b_spatial, v_g_mem, v_w_mem_kv, v_g_mq, v_g_mk, v_w_o_gm, v_w_o_mla, v_w_o_mem, v_w_out, v_g_ffn, v_w_ff1, v_w_ff2):
    given = dict(locals())
    w = {n: given[n][0] for n in WEIGHTS}
    mom = {n: given['m_' + n][0] for n in WEIGHTS}
    var = {n: given['v_' + n][0] for n in WEIGHTS}

    ws = _DistWeights({n: w[n] for n in BIG})
    loss_part, grad_x, G = _local_step(x, mem, positions, loss_target, {n: w[n] for n in SMALL}, ws)

    outs = {}
    for n in BIG:
        for prefix, res in zip(("grad_", "delta_", "new_m_", "new_v_"),
                               _sum_adamw(ws.received[n], w[n], mom[n], var[n], "adamw_" + n)):
            outs[prefix + n] = res[None]

    tail = jnp.zeros((8, LANES), F32)
    parts = _all_gather8(jnp.concatenate([_small_slab(G), jnp.pad(loss_part, ((0, 7), (0, 0)))], axis=0), "ag_small")
    small = _sum_adamw(parts, *[jnp.concatenate([_small_slab(d), tail], axis=0) for d in (w, mom, var)], "adamw_small")
    loss = 0.5 * jnp.sum(small[0][-8:]) / D_MODEL
    for prefix, small_slab in zip(("grad_", "delta_", "new_m_", "new_v_"), small):
        sm = _small_unslab(small_slab, given)
        for n in SMALL:
            outs[prefix + n] = sm[n]
    return (loss, grad_x, *[outs[p + n] for p in ("grad_", "delta_", "new_m_", "new_v_") for n in WEIGHTS])
```

```python
import functools
import math

import jax
import jax.numpy as jnp
from jax import lax
from jax.experimental import pallas as pl
from jax.experimental.pallas import tpu as pltpu

F32 = jnp.float32
BF16 = jnp.bfloat16
ACT = BF16

D_MODEL = 1024
MEM_HEADS = 4
HEAD = 128
GM_WIDTH = 512
GM_CHUNK = 128
GM_GROUPS = 4
MLA_HEADS = 8
MLA_ROPE = 64
Q_LORA = 384
KV_LORA = 256
D_FF = 4096
EPS = 1e-6
ROPE_BASE = 10000.0
MLA_SCALE = 1.0 / math.sqrt(HEAD + MLA_ROPE)
MEM_SCALE = 1.0 / math.sqrt(HEAD)
LOG2E = 1.4426950408889634
LN2 = 0.6931471805599453
ATT_TILE = 256
C_ZU, C_ZV, C_CQ, C_CKV, C_KPE, C_QM, C_ZG, C_END = 0, 512, 1024, 1408, 1664, 1728, 2240, 5312
Z_GM, Z_QM, Z_MLA, Z_KPE, Z_COLS = 3072, 4096, 4608, 5248, 5376
MLA_W = 768
QCAT = 2 * HEAD
ADAM_LR, ADAM_B1, ADAM_B2, ADAM_EPS, ADAM_WD, ADAM_STEP = 0.001, 0.9, 0.999, 1e-08, 0.01, 10
N_DEV = 8
LANES = 128
VMEM_LIMIT = 48 * 1024 * 1024
MAX_K_TILE = 8192
NEG = -1e30

BIG = ['w_in', 'w_uq', 'w_ukv', 'w_mem_kv', 'w_o_gm', 'w_o_mla', 'w_o_mem', 'w_out', 'w_ff1', 'w_ff2']
BIG_AXIS = {'w_in': 1, 'w_uq': 1, 'w_ukv': 1, 'w_mem_kv': 0, 'w_o_gm': 1, 'w_o_mla': 0, 'w_o_mem': 1,
            'w_out': 0, 'w_ff1': 1, 'w_ff2': 0}
BIG_SHAPE = {'w_in': (1024, 5312), 'w_uq': (384, 1536), 'w_ukv': (256, 2048), 'w_mem_kv': (1024, 1024),
             'w_o_gm': (512, 1024), 'w_o_mla': (1024, 1024), 'w_o_mem': (512, 1024), 'w_out': (1024, 1024),
             'w_ff1': (1024, 4096), 'w_ff2': (4096, 1024)}
SMALL = ['g_mix', 'g_cq', 'g_ckv', 'g_q_nope', 'g_q_pe', 'g_k_nope', 'g_k_pe', 'g_gm_ln', 'b_gm_ln',
         'w_spatial', 'b_spatial', 'g_mem', 'g_mq', 'g_mk', 'g_ffn']
WEIGHTS = ['g_mix', 'w_in', 'g_cq', 'w_uq', 'g_ckv', 'w_ukv', 'g_q_nope', 'g_q_pe', 'g_k_nope', 'g_k_pe',
           'g_gm_ln', 'b_gm_ln', 'w_spatial', 'b_spatial', 'g_mem', 'w_mem_kv', 'g_mq', 'g_mk', 'w_o_gm',
           'w_o_mla', 'w_o_mem', 'w_out', 'g_ffn', 'w_ff1', 'w_ff2']


def _pick(n, target, mult=LANES):
    best = None
    t = mult
    while t <= min(n, target):
        if n % t == 0:
            best = t
        t += mult
    return best if best is not None else n


def _params(sem):
    return pltpu.CompilerParams(dimension_semantics=sem, vmem_limit_bytes=VMEM_LIMIT)


MESH = pl.DeviceIdType.MESH
HBM_SPEC = pl.BlockSpec(memory_space=pltpu.HBM)


class _Exchange:
    def __init__(self, srcs, scatter):
        self.srcs, self.scatter = list(srcs), scatter
        self.out_shapes = [jax.ShapeDtypeStruct(s.shape if scatter else (N_DEV,) + s.shape, s.dtype) for s in self.srcs]
        n = len(self.srcs)
        self.scratch = [pltpu.SemaphoreType.DMA((n, N_DEV - 1)), pltpu.SemaphoreType.DMA((n, N_DEV - 1)),
                        pltpu.SemaphoreType.DMA((n,))]

    def _copies(self, src_refs, dst_refs, send_sems, recv_sems, local_sems):
        x, y, c = lax.axis_index("x"), lax.axis_index("y"), lax.axis_index("c")
        me = 4 * x + 2 * y + c
        local, remote = [], []
        for a, (src_ref, dst_ref) in enumerate(zip(src_refs, dst_refs)):
            def mine_for(dev, src_ref=src_ref):
                return src_ref.at[dev] if self.scatter else src_ref

            local.append(pltpu.make_async_copy(mine_for(me), dst_ref.at[me], local_sems.at[a]))
            for k in range(1, N_DEV):
                px = 1 - x if k & 4 else x
                py = 1 - y if k & 2 else y
                pc = 1 - c if k & 1 else c
                remote.append(pltpu.make_async_remote_copy(
                    src_ref=mine_for(4 * px + 2 * py + pc), dst_ref=dst_ref.at[me], send_sem=send_sems.at[a, k - 1],
                    recv_sem=recv_sems.at[a, k - 1], device_id=(px, py, pc), device_id_type=MESH))
        return local, remote

    def start(self, *refs):
        local, remote = self._copies(*refs)
        for cp in local + remote:
            cp.start()

    def wait(self, *refs):
        local, remote = self._copies(*refs)
        for cp in remote + local:
            cp.wait()


class _Gather2:
    def __init__(self, srcs):
        self.srcs = list(srcs)
        self.out_shapes = [jax.ShapeDtypeStruct((N_DEV,) + s.shape, s.dtype) for s in self.srcs]
        n = len(self.srcs)
        self.scratch = [pltpu.SemaphoreType.DMA((n, N_DEV - 1)), pltpu.SemaphoreType.DMA((n, N_DEV - 1)),
                        pltpu.SemaphoreType.DMA((n,))]

    def _plan(self, src_refs, dst_refs, send_sems, recv_sems, local_sems):
        x, y, c = lax.axis_index("x"), lax.axis_index("y"), lax.axis_index("c")
        chips = [(1 - x, y), (x, 1 - y), (1 - x, 1 - y)]
        plans = []
        for a, (src_ref, dst_ref) in enumerate(zip(src_refs, dst_refs)):
            def copy(k, block, to, src=None, a=a, dst_ref=dst_ref):
                at = dst_ref.at[4 * block[0] + 2 * block[1] + block[2]]
                return pltpu.make_async_remote_copy(src_ref=at if src is None else src, dst_ref=at,
                                                    send_sem=send_sems.at[a, k], recv_sem=recv_sems.at[a, k],
                                                    device_id=to, device_id_type=MESH)

            local = pltpu.make_async_copy(src_ref, dst_ref.at[4 * x + 2 * y + c], local_sems.at[a])
            first = [copy(0, (x, y, c), (x, y, 1 - c), src=src_ref)]
            first += [copy(1 + j, (x, y, c), (*chip, c), src=src_ref) for j, chip in enumerate(chips)]
            passed = [copy(4 + j, (*chip, c), (x, y, 1 - c)) for j, chip in enumerate(chips)]
            arrivals = [copy(1 + j, (*chip, c), (x, y, c)) for j, chip in enumerate(chips)]
            late = [copy(0, (x, y, 1 - c), (x, y, c))] + [copy(4 + j, (*chip, 1 - c), (x, y, c)) for j, chip in enumerate(chips)]
            plans.append((local, first, passed, arrivals, late))
        return plans

    def start(self, *refs):
        for local, first, _, _, _ in self._plan(*refs):
            local.start()
            for cp in first:
                cp.start()

    def wait(self, *refs):
        plans = self._plan(*refs)
        for _, _, passed, arrivals, _ in plans:
            for arrived, onward in zip(arrivals, passed):
                arrived.wait_recv()
                onward.start()
        for local, first, passed, _, late in plans:
            for cp in late:
                cp.wait_recv()
            for cp in first + passed:
                cp.wait_send()
            local.wait()


def _call(body, rider, ins, *, name, grid, in_specs, out_specs, out_shape, scratch_shapes, sem):
    if rider is None:
        return pl.pallas_call(body, name=name, grid=grid, in_specs=in_specs, out_specs=out_specs, out_shape=out_shape,
                              scratch_shapes=scratch_shapes, compiler_params=_params(sem))(*ins)
    single = not isinstance(out_shape, (list, tuple))
    own_specs, own_shapes = ([out_specs], [out_shape]) if single else (list(out_specs), list(out_shape))
    n_in, n_out, n_sc, n_r = len(ins), len(own_shapes), len(scratch_shapes), len(rider.srcs)
    n_all_in = n_in + n_r

    def carrying(*refs):
        own_in, srcs = refs[:n_in], refs[n_in:n_in + n_r]
        own_out, dsts = refs[n_all_in:n_all_in + n_out], refs[n_all_in + n_out:n_all_in + n_out + n_r]
        own_sc = refs[n_all_in + n_out + n_r:n_all_in + n_out + n_r + n_sc]
        sems = refs[n_all_in + n_out + n_r + n_sc:]
        first = last = None
        for d, steps in enumerate(grid):
            f, l = pl.program_id(d) == 0, pl.program_id(d) == steps - 1
            first, last = (f, l) if first is None else (first & f, last & l)

        @pl.when(first)
        def _():
            rider.start(srcs, dsts, *sems)

        body(*own_in, *own_out, *own_sc)

        @pl.when(last)
        def _():
            rider.wait(srcs, dsts, *sems)

    res = pl.pallas_call(
        carrying, name=name, grid=grid, in_specs=list(in_specs) + [HBM_SPEC] * n_r,
        out_specs=own_specs + [HBM_SPEC] * n_r, out_shape=own_shapes + rider.out_shapes,
        scratch_shapes=list(scratch_shapes) + rider.scratch, compiler_params=_params(("arbitrary",) * len(grid)),
    )(*ins, *rider.srcs)
    own = res[:n_out]
    return (own[0] if single else list(own)), list(res[n_out:])


def _matmul(a, b, mode, out_dtype, name, add=None, relu2_a=False, relu2_grad=None,
            tm_t=None, tn_t=None, tk_t=None, rider=None, m_rows=None, col_shards=None, sq_err_target=None):
    if mode == 'nn':
        (M, K), (K2, N) = a.shape, b.shape
    elif mode == 'nt':
        (M, K), (N, K2) = a.shape, b.shape
    else:
        (K, M), (K2, N) = a.shape, b.shape
    assert K == K2, (name, a.shape, b.shape)
    m_first = 0
    if m_rows is not None:
        assert mode == 'tn'
        m_first, M = m_rows
    if col_shards is not None:
        assert add is None and relu2_grad is None and sq_err_target is None and tn_t is None
        tn_t = N // col_shards
    if mode == 'tn':
        d_tm, d_tn, d_tk = 1024, 1024, 2048
    else:
        d_tm, d_tn, d_tk = (2048 if K <= 1024 else 1024), 512, MAX_K_TILE
    tm, tn, tk = _pick(M, tm_t or d_tm), _pick(N, tn_t or d_tn), _pick(K, tk_t or d_tk)
    gm, gn, nk = M // tm, N // tn, K // tk
    if mode == 'nn':
        a_spec = pl.BlockSpec((tm, tk), lambda i, j, k: (i, k))
        b_spec = pl.BlockSpec((tk, tn), lambda i, j, k: (k, j))
        dims = (((1,), (0,)), ((), ()))
    elif mode == 'nt':
        a_spec = pl.BlockSpec((tm, tk), lambda i, j, k: (i, k))
        b_spec = pl.BlockSpec((tn, tk), lambda i, j, k: (j, k))
        dims = (((1,), (1,)), ((), ()))
    else:
        assert m_first % tm == 0
        a_spec = pl.BlockSpec((tk, tm), lambda i, j, k: (k, m_first // tm + i))
        b_spec = pl.BlockSpec((tk, tn), lambda i, j, k: (k, j))
        dims = (((0,), (0,)), ((), ()))
    o_spec = pl.BlockSpec((tm, tn), lambda i, j, k: (i, j))
    has_add, has_e, has_t = add is not None, relu2_grad is not None, sq_err_target is not None
    assert not has_t or (nk == 1 and tn % LANES == 0)

    def body(*refs):
        a_ref, b_ref = refs[0], refs[1]
        pos = 2
        add_ref = e_ref = t_ref = None
        if has_add:
            add_ref = refs[pos]
            pos += 1
        if has_e:
            e_ref = refs[pos]
            pos += 1
        if has_t:
            t_ref = refs[pos]
            pos += 1
        o_ref = refs[pos]
        acc_ref = refs[pos + 1] if nk > 1 else None

        av = a_ref[...]
        if relu2_a:
            av = jnp.maximum(av, 0)
            av = av * av
        prod = lax.dot_general(av.astype(BF16), b_ref[...].astype(BF16), dims, preferred_element_type=F32)

        def finish(r):
            if has_add:
                r = r + add_ref[...]
            if has_e:
                r = r * (2.0 * jnp.maximum(e_ref[...].astype(F32), 0.0))
            if has_t:
                err = r - t_ref[...]
                r = err * (1.0 / N)
                refs[pos + 1][...] = r.astype(BF16)
                sq = err * err
                part = sq[:, 0:LANES]
                for c in range(1, tn // LANES):
                    part = part + sq[:, c * LANES:(c + 1) * LANES]
                _acc_rows(refs[pos + 2], part, (pl.program_id(0) == 0) & (pl.program_id(1) == 0))
            o_ref[...] = r.astype(out_dtype)

        if nk == 1:
            finish(prod)
        else:
            k = pl.program_id(2)

            @pl.when(k == 0)
            def _():
                acc_ref[...] = prod

            @pl.when(k > 0)
            def _():
                acc_ref[...] += prod

            @pl.when(k == nk - 1)
            def _():
                finish(acc_ref[...])

    ins, specs = [a, b], [a_spec, b_spec]
    if has_add:
        ins.append(add)
        specs.append(o_spec)
    if has_e:
        ins.append(relu2_grad)
        specs.append(o_spec)
    out_specs, out_shape, sem = o_spec, jax.ShapeDtypeStruct((M, N), out_dtype), ("parallel", "parallel", "arbitrary")
    if has_t:
        ins.append(sq_err_target)
        specs.append(o_spec)
        out_specs = [o_spec, o_spec, pl.BlockSpec((1, LANES), lambda i, j, k: (0, 0))]
        out_shape = [out_shape, jax.ShapeDtypeStruct((M, N), BF16), jax.ShapeDtypeStruct((1, LANES), F32)]
        sem = ("arbitrary", "arbitrary", "arbitrary")
    if col_shards is not None:
        out_specs = pl.BlockSpec((None, tm, tn), lambda i, j, k: (j, i, 0))
        out_shape = jax.ShapeDtypeStruct((col_shards, M, tn), out_dtype)
    return _call(body, rider, ins, name=name, grid=(gm, gn, nk), in_specs=specs, out_specs=out_specs, out_shape=out_shape,
                 scratch_shapes=[pltpu.VMEM((tm, tn), F32)] if nk > 1 else [], sem=sem)


ROW_BLOCK_BYTES = 12 * 1024 * 1024


def _row_tile(rows, row_bytes):
    return _pick(rows, max(16, min(1024, ROW_BLOCK_BYTES // row_bytes)), 16)


def _rowspec(tr, width, col=0):
    return pl.BlockSpec((tr, width), lambda i, col=col: (i, col))


def _fullspec(shape):
    nd = len(shape)
    return pl.BlockSpec(shape, lambda i, nd=nd: (0,) * nd)


def _rms(x, width):
    x = x.astype(F32)
    return lax.rsqrt(jnp.sum(x * x, axis=-1, keepdims=True) * (1.0 / width) + EPS)


def _rms_bwd_rows(x, g, dy, width):
    x, dy = x.astype(F32), dy.astype(F32)
    r = _rms(x, width)
    xh = x * r
    dn = dy * g
    dx = r * (dn - xh * (jnp.sum(dn * xh, axis=-1, keepdims=True) * (1.0 / width)))
    return dx, dy * xh


def _acc_rows(ref, val, first):
    s = jnp.sum(val, axis=0, keepdims=True)

    @pl.when(first)
    def _():
        ref[...] = s

    @pl.when(jnp.logical_not(first))
    def _():
        ref[...] += s


def _rms_fwd(x, g, name, rider=None):
    rows, width = x.shape
    tr = _row_tile(rows, 6 * width)

    def body(x_ref, g_ref, o_ref):
        xv = x_ref[...]
        o_ref[...] = (xv * _rms(xv, width) * g_ref[...]).astype(BF16)

    return _call(body, rider, [x, g], name=name, grid=(rows // tr,),
                 in_specs=[_rowspec(tr, width), _fullspec((1, width))], out_specs=_rowspec(tr, width),
                 out_shape=jax.ShapeDtypeStruct((rows, width), BF16), scratch_shapes=[], sem=("parallel",))


def _rms_bwd(x, g, dy, res, name, dx_dtypes=(F32,)):
    rows, width = x.shape
    tr = _row_tile(rows, 18 * width)
    has_res = res is not None
    n_in = 4 if has_res else 3

    def body(*refs):
        x_ref, g_ref, dy_ref = refs[:3]
        dx, dgv = _rms_bwd_rows(x_ref[...], g_ref[...], dy_ref[...], width)
        if has_res:
            dx = dx + refs[3][...]
        for ref, dt in zip(refs[n_in:], dx_dtypes):
            ref[...] = dx.astype(dt)
        _acc_rows(refs[-1], dgv, pl.program_id(0) == 0)

    ins = [x, g, dy] + ([res] if has_res else [])
    specs = [_rowspec(tr, width), _fullspec((1, width)), _rowspec(tr, width)] + ([_rowspec(tr, width)] if has_res else [])
    return pl.pallas_call(
        body, name=name, grid=(rows // tr,), in_specs=specs,
        out_specs=[_rowspec(tr, width)] * len(dx_dtypes) + [_fullspec((1, width))],
        out_shape=[jax.ShapeDtypeStruct((rows, width), dt) for dt in dx_dtypes] + [jax.ShapeDtypeStruct((1, width), F32)],
        compiler_params=_params(("arbitrary",)),
    )(*ins)


_GELU_C = math.sqrt(2.0 / math.pi)


def _gelu(x):
    t = jnp.tanh(_GELU_C * (x + 0.044715 * (x * x * x)))
    return 0.5 * x * (1.0 + t), t


def _gelu_grad(x, t):
    return 0.5 * (1.0 + t) + 0.5 * x * (1.0 - t * t) * (_GELU_C * (1.0 + 3.0 * 0.044715 * (x * x)))


def _gm_forward_rows(zu, zv, gln, bln, wc_ref, bst, n_chunk):
    u, tu = _gelu(zu)
    a, ta = _gelu(zv)
    mu = jnp.mean(a, axis=-1, keepdims=True)
    ac = a - mu
    rs = lax.rsqrt(jnp.mean(ac * ac, axis=-1, keepdims=True) + EPS)
    n = ac * rs
    v = n * gln + bln
    vb = v.astype(BF16)
    rows = []
    for c in range(n_chunk):
        cols = []
        for g in range(GM_GROUPS):
            vc = vb[c * GM_CHUNK:(c + 1) * GM_CHUNK, g * LANES:(g + 1) * LANES]
            mixed = jnp.dot(wc_ref[g], vc, preferred_element_type=F32) + bst[g]
            cols.append(mixed)
        rows.append(jnp.concatenate(cols, axis=1))
    mixed = jnp.concatenate(rows, axis=0) if n_chunk > 1 else rows[0]
    return u, tu, ta, n, rs, v, mixed


def _gm_fwd(z, gln, bln, wc, bst, name):
    rows = z.shape[0]
    tr = _pick(rows, 512, GM_CHUNK)
    n_chunk = tr // GM_CHUNK

    def body(zu_ref, zv_ref, gln_ref, bln_ref, wc_ref, bst_ref, o_ref):
        u, _, _, _, _, _, mixed = _gm_forward_rows(zu_ref[...].astype(F32), zv_ref[...].astype(F32), gln_ref[...], bln_ref[...], wc_ref,
                                                   bst_ref, n_chunk)
        o_ref[...] = (u * mixed).astype(BF16)

    return pl.pallas_call(
        body, name=name, grid=(rows // tr,),
        in_specs=[_rowspec(tr, GM_WIDTH, Z_GM // GM_WIDTH), _rowspec(tr, GM_WIDTH, Z_GM // GM_WIDTH + 1),_fullspec((1, GM_WIDTH)), _fullspec((1, GM_WIDTH)),
                  _fullspec((GM_GROUPS, GM_CHUNK, GM_CHUNK)), _fullspec((GM_GROUPS, GM_CHUNK, LANES))],
        out_specs=_rowspec(tr, GM_WIDTH), out_shape=jax.ShapeDtypeStruct((rows, GM_WIDTH), BF16),
        compiler_params=_params(("parallel",)),
    )(z, z, gln, bln, wc, bst)


ANY_SPEC = pl.BlockSpec(memory_space=pl.ANY)


def _gm_bwd(z, dy, gln, bln, wc, wct, bst, dz, name):
    rows = z.shape[0]
    tr = _pick(rows, 512, GM_CHUNK)
    n_chunk = tr // GM_CHUNK

    def body(zu_ref, zv_ref, dy_ref, gln_ref, bln_ref, wc_ref, wct_ref, bst_ref, _, dz_ref, dws_ref, dbs_ref, dgl_ref,
             dbl_ref):
        first = pl.program_id(0) == 0
        zu, zv, gln = zu_ref[...].astype(F32), zv_ref[...].astype(F32), gln_ref[...]
        u, tu, ta, n, rs, v, mixed = _gm_forward_rows(zu, zv, gln, bln_ref[...], wc_ref, bst_ref, n_chunk)
        dyv = dy_ref[...].astype(F32)
        dzu = dyv * mixed * _gelu_grad(zu, tu)
        dmix = dyv * u
        dmb = dmix.astype(BF16)
        vb = v.astype(BF16)
        dv_rows, dws, dbs = [], [None] * GM_GROUPS, None
        for c in range(n_chunk):
            rsl = slice(c * GM_CHUNK, (c + 1) * GM_CHUNK)
            cols = []
            for g in range(GM_GROUPS):
                csl = slice(g * LANES, (g + 1) * LANES)
                dmc = dmb[rsl, csl]
                cols.append(jnp.dot(wct_ref[g], dmc, preferred_element_type=F32))
                w_part = lax.dot_general(dmc, vb[rsl, csl], (((1,), (1,)), ((), ())), preferred_element_type=F32)
                dws[g] = w_part if dws[g] is None else dws[g] + w_part
            dv_rows.append(jnp.concatenate(cols, axis=1))
            dbs = dmix[rsl, :] if dbs is None else dbs + dmix[rsl, :]
        dv = jnp.concatenate(dv_rows, axis=0) if n_chunk > 1 else dv_rows[0]
        dn = dv * gln
        da = rs * (dn - jnp.mean(dn, axis=-1, keepdims=True) - n * jnp.mean(dn * n, axis=-1, keepdims=True))
        dzv = da * _gelu_grad(zv, ta)
        dz_ref[:, 0:GM_WIDTH] = dzu.astype(BF16)
        dz_ref[:, GM_WIDTH:2 * GM_WIDTH] = dzv.astype(BF16)
        _acc_rows(dgl_ref, dv * n, first)
        _acc_rows(dbl_ref, dv, first)

        @pl.when(first)
        def _():
            for g in range(GM_GROUPS):
                dws_ref[g] = dws[g]
            dbs_ref[...] = dbs

        @pl.when(jnp.logical_not(first))
        def _():
            for g in range(GM_GROUPS):
                dws_ref[g] += dws[g]
            dbs_ref[...] += dbs

    wspec = _fullspec((GM_GROUPS, GM_CHUNK, GM_CHUNK))
    return pl.pallas_call(
        body, name=name, grid=(rows // tr,),
        in_specs=[_rowspec(tr, GM_WIDTH, Z_GM // GM_WIDTH), _rowspec(tr, GM_WIDTH, Z_GM // GM_WIDTH + 1),
                  _rowspec(tr, GM_WIDTH), _fullspec((1, GM_WIDTH)), _fullspec((1, GM_WIDTH)), wspec, wspec, wspec, ANY_SPEC],
        out_specs=[_rowspec(tr, 2 * GM_WIDTH, Z_GM // (2 * GM_WIDTH)), wspec, _fullspec((GM_CHUNK, GM_WIDTH)),
                   _fullspec((1, GM_WIDTH)), _fullspec((1, GM_WIDTH))],
        out_shape=[jax.ShapeDtypeStruct(dz.shape, dz.dtype), jax.ShapeDtypeStruct((GM_GROUPS, GM_CHUNK, GM_CHUNK), F32),
                   jax.ShapeDtypeStruct((GM_CHUNK, GM_WIDTH), F32), jax.ShapeDtypeStruct((1, GM_WIDTH), F32),
                   jax.ShapeDtypeStruct((1, GM_WIDTH), F32)],
        input_output_aliases={8: 0}, compiler_params=_params(("arbitrary",)),
    )(z, z, dy, gln, bln, wc, wct, bst, dz)


def _lat_fwd(z, g_cq, g_ckv, name):
    rows = z.shape[0]
    tr = _row_tile(rows, 4 * MLA_W)

    def body(z_ref, gq_ref, gkv_ref, nq_ref, nkv_ref):
        zb = z_ref[...]
        cq, ckv = zb[:, 0:Q_LORA], zb[:, Q_LORA:Q_LORA + KV_LORA]
        nq_ref[...] = (cq * _rms(cq, Q_LORA) * gq_ref[...]).astype(BF16)
        nkv_ref[...] = (ckv * _rms(ckv, KV_LORA) * gkv_ref[...]).astype(BF16)

    return pl.pallas_call(
        body, name=name, grid=(rows // tr,),
        in_specs=[_rowspec(tr, MLA_W, Z_MLA // MLA_W), _fullspec((1, Q_LORA)), _fullspec((1, KV_LORA))],
        out_specs=[_rowspec(tr, Q_LORA), _rowspec(tr, KV_LORA)],
        out_shape=[jax.ShapeDtypeStruct((rows, Q_LORA), BF16), jax.ShapeDtypeStruct((rows, KV_LORA), BF16)],
        compiler_params=_params(("parallel",)),
    )(z, g_cq, g_ckv)


def _lat_bwd(z, dnq, dnkv, dkpe, g_cq, g_ckv, dz, name):
    rows = z.shape[0]
    tr = _row_tile(rows, 8 * MLA_W)

    def body(z_ref, dnq_ref, dnkv_ref, dkpe_ref, gq_ref, gkv_ref, _, dz_ref, dgq_ref, dgkv_ref):
        first = pl.program_id(0) == 0
        zb = z_ref[...]
        dcq, dgq = _rms_bwd_rows(zb[:, 0:Q_LORA], gq_ref[...], dnq_ref[...], Q_LORA)
        dckv, dgkv = _rms_bwd_rows(zb[:, Q_LORA:Q_LORA + KV_LORA], gkv_ref[...], dnkv_ref[...], KV_LORA)
        dz_ref[:, 0:Q_LORA] = dcq.astype(BF16)
        dz_ref[:, Q_LORA:Q_LORA + KV_LORA] = dckv.astype(BF16)
        dz_ref[:, Q_LORA + KV_LORA:MLA_W] = dkpe_ref[...].astype(BF16)
        _acc_rows(dgq_ref, dgq, first)
        _acc_rows(dgkv_ref, dgkv, first)

    return pl.pallas_call(
        body, name=name, grid=(rows // tr,),
        in_specs=[_rowspec(tr, MLA_W, Z_MLA // MLA_W), _rowspec(tr, Q_LORA), _rowspec(tr, KV_LORA), _rowspec(tr, LANES),
                  _fullspec((1, Q_LORA)), _fullspec((1, KV_LORA)), ANY_SPEC],
        out_specs=[_rowspec(tr, MLA_W, Z_MLA // MLA_W), _fullspec((1, Q_LORA)), _fullspec((1, KV_LORA))],
        out_shape=[jax.ShapeDtypeStruct(dz.shape, dz.dtype), jax.ShapeDtypeStruct((1, Q_LORA), F32),
                   jax.ShapeDtypeStruct((1, KV_LORA), F32)],
        input_output_aliases={6: 0}, compiler_params=_params(("arbitrary",)),
    )(z, dnq, dnkv, dkpe, g_cq, g_ckv, dz)


def _rope(y, cc, ss):
    return y * cc + pltpu.roll(y, 64, 1) * ss


def _rope_bwd(d, cc, ss):
    return d * cc + pltpu.roll(d * ss, 64, 1)


def _qk_fwd(q, kv, z, cc, ss, gqn, gqp, gkn, gkp, name):
    rows = q.shape[0]
    W = MLA_HEADS * HEAD
    tr = _row_tile(rows, 20 * W)
    QS = MLA_SCALE * LOG2E

    def body(q_ref, kv_ref, kpe_ref, cc_ref, ss_ref, gqn_ref, gqp_ref, gkn_ref, gkp_ref, qc_ref, kc_ref, v_ref):
        cc, ss = cc_ref[...], ss_ref[...]
        kpe = kpe_ref[...]
        kp = _rope(kpe * _rms(kpe, MLA_ROPE) * gkp_ref[...], cc, ss).astype(BF16)
        for h in range(MLA_HEADS):
            qn = q_ref[:, h * HEAD:(h + 1) * HEAD]
            qp = q_ref[:, W + h * HEAD:W + (h + 1) * HEAD]
            kn = kv_ref[:, h * HEAD:(h + 1) * HEAD]
            qc_ref[:, h * QCAT:h * QCAT + HEAD] = (qn * _rms(qn, HEAD) * gqn_ref[...] * QS).astype(BF16)
            qc_ref[:, h * QCAT + HEAD:(h + 1) * QCAT] = (_rope(qp * _rms(qp, MLA_ROPE) * gqp_ref[...], cc, ss) * QS).astype(BF16)
            kc_ref[:, h * QCAT:h * QCAT + HEAD] = (kn * _rms(kn, HEAD) * gkn_ref[...]).astype(BF16)
            kc_ref[:, h * QCAT + HEAD:(h + 1) * QCAT] = kp
        v_ref[...] = kv_ref[:, W:2 * W].astype(BF16)

    g = _fullspec((1, HEAD))
    return pl.pallas_call(
        body, name=name, grid=(rows // tr,),
        in_specs=[_rowspec(tr, 2 * W), _rowspec(tr, 2 * W), _rowspec(tr, LANES, Z_KPE // LANES), _rowspec(tr, LANES),
                  _rowspec(tr, LANES), g, g, g, g],
        out_specs=[_rowspec(tr, MLA_HEADS * QCAT), _rowspec(tr, MLA_HEADS * QCAT), _rowspec(tr, W)],
        out_shape=[jax.ShapeDtypeStruct((rows, MLA_HEADS * QCAT), BF16), jax.ShapeDtypeStruct((rows, MLA_HEADS * QCAT), BF16),
                   jax.ShapeDtypeStruct((rows, W), BF16)],
        compiler_params=_params(("parallel",)),
    )(q, kv, z, cc, ss, gqn, gqp, gkn, gkp)


def _qk_bwd(q, kv, z, cc, ss, gqn, gqp, gkn, gkp, dqc, dkc, dv, name):
    rows = q.shape[0]
    W = MLA_HEADS * HEAD
    tr = _row_tile(rows, 40 * W)

    def body(q_ref, kv_ref, kpe_ref, cc_ref, ss_ref, gqn_ref, gqp_ref, gkn_ref, gkp_ref, dqc_ref, dkc_ref, dv_ref,
             dq_ref, dkv_ref, dkpe_ref, dgqn_ref, dgqp_ref, dgkn_ref, dgkp_ref):
        first = pl.program_id(0) == 0
        cc, ss = cc_ref[...], ss_ref[...]
        sqn = sqp = skn = dkp = None
        for h in range(MLA_HEADS):
            dx, dg = _rms_bwd_rows(q_ref[:, h * HEAD:(h + 1) * HEAD], gqn_ref[...], dqc_ref[:, h * QCAT:h * QCAT + HEAD], HEAD)
            dq_ref[:, h * HEAD:(h + 1) * HEAD] = dx.astype(BF16)
            sqn = dg if sqn is None else sqn + dg
            dy = _rope_bwd(dqc_ref[:, h * QCAT + HEAD:(h + 1) * QCAT], cc, ss)
            dx, dg = _rms_bwd_rows(q_ref[:, W + h * HEAD:W + (h + 1) * HEAD], gqp_ref[...], dy, MLA_ROPE)
            dq_ref[:, W + h * HEAD:W + (h + 1) * HEAD] = dx.astype(BF16)
            sqp = dg if sqp is None else sqp + dg
            dx, dg = _rms_bwd_rows(kv_ref[:, h * HEAD:(h + 1) * HEAD], gkn_ref[...], dkc_ref[:, h * QCAT:h * QCAT + HEAD], HEAD)
            dkv_ref[:, h * HEAD:(h + 1) * HEAD] = dx.astype(BF16)
            skn = dg if skn is None else skn + dg
            part = dkc_ref[:, h * QCAT + HEAD:(h + 1) * QCAT].astype(F32)
            dkp = part if dkp is None else dkp + part
        dkv_ref[:, W:2 * W] = dv_ref[...].astype(BF16)
        dx, dg = _rms_bwd_rows(kpe_ref[...], gkp_ref[...], _rope_bwd(dkp, cc, ss), MLA_ROPE)
        dkpe_ref[...] = dx
        _acc_rows(dgqn_ref, sqn, first)
        _acc_rows(dgqp_ref, sqp, first)
        _acc_rows(dgkn_ref, skn, first)
        _acc_rows(dgkp_ref, dg, first)

    g = _fullspec((1, HEAD))
    gs = jax.ShapeDtypeStruct((1, HEAD), F32)
    return pl.pallas_call(
        body, name=name, grid=(rows // tr,),
        in_specs=[_rowspec(tr, 2 * W), _rowspec(tr, 2 * W), _rowspec(tr, LANES, Z_KPE // LANES), _rowspec(tr, LANES),
                  _rowspec(tr, LANES), g, g, g, g, _rowspec(tr, MLA_HEADS * QCAT), _rowspec(tr, MLA_HEADS * QCAT),
                  _rowspec(tr, W)],
        out_specs=[_rowspec(tr, 2 * W), _rowspec(tr, 2 * W), _rowspec(tr, LANES), g, g, g, g],
        out_shape=[jax.ShapeDtypeStruct((rows, 2 * W), BF16), jax.ShapeDtypeStruct((rows, 2 * W), BF16),
                   jax.ShapeDtypeStruct((rows, LANES), F32), gs, gs, gs, gs],
        compiler_params=_params(("arbitrary",)),
    )(q, kv, z, cc, ss, gqn, gqp, gkn, gkp, dqc, dkc, dv)


def _headnorm_fwd(x, col, nheads, g, out_scale, name):
    rows = x.shape[0]
    W = nheads * HEAD
    tr = _row_tile(rows, 6 * W)

    def body(x_ref, g_ref, o_ref):
        for h in range(nheads):
            xv = x_ref[:, h * HEAD:(h + 1) * HEAD]
            o_ref[:, h * HEAD:(h + 1) * HEAD] = (xv * _rms(xv, HEAD) * g_ref[...] * out_scale).astype(BF16)

    return pl.pallas_call(
        body, name=name, grid=(rows // tr,),
        in_specs=[_rowspec(tr, W, col), _fullspec((1, HEAD))], out_specs=_rowspec(tr, W),
        out_shape=jax.ShapeDtypeStruct((rows, W), BF16), compiler_params=_params(("parallel",)),
    )(x, g)


def _headnorm_bwd(x, col, nheads, g, dy, tail, name, into=None):
    rows = x.shape[0]
    W = nheads * HEAD
    tr = _row_tile(rows, 12 * W)
    has_tail = tail is not None
    WO = 2 * W if has_tail else W

    def body(*refs):
        if into is not None:
            x_ref, g_ref, dy_ref, _, dx_ref, dg_ref = refs
        elif has_tail:
            x_ref, g_ref, dy_ref, t_ref, dx_ref, dg_ref = refs
        else:
            x_ref, g_ref, dy_ref, dx_ref, dg_ref = refs
        acc = None
        for h in range(nheads):
            sl = slice(h * HEAD, (h + 1) * HEAD)
            dx, dg = _rms_bwd_rows(x_ref[:, sl], g_ref[...], dy_ref[:, sl], HEAD)
            dx_ref[:, sl] = dx.astype(BF16)
            acc = dg if acc is None else acc + dg
        if has_tail:
            dx_ref[:, W:2 * W] = t_ref[...].astype(BF16)
        _acc_rows(dg_ref, acc, pl.program_id(0) == 0)

    ins = [x, g, dy] + ([tail] if has_tail else [])
    specs = [_rowspec(tr, W, col), _fullspec((1, HEAD)), _rowspec(tr, W)] + ([_rowspec(tr, W)] if has_tail else [])
    dx_spec, dx_shape, aliases = _rowspec(tr, WO), jax.ShapeDtypeStruct((rows, WO), BF16), {}
    if into is not None:
        assert not has_tail
        ins, specs = ins + [into[0]], specs + [ANY_SPEC]
        dx_spec, dx_shape, aliases = _rowspec(tr, W, into[1]), jax.ShapeDtypeStruct(into[0].shape, into[0].dtype), {3: 0}
    return pl.pallas_call(
        body, name=name, grid=(rows // tr,), in_specs=specs,
        out_specs=[dx_spec, _fullspec((1, HEAD))], out_shape=[dx_shape, jax.ShapeDtypeStruct((1, HEAD), F32)],
        input_output_aliases=aliases, compiler_params=_params(("arbitrary",)),
    )(*ins)


def _sigmoid(x):
    return 1.0 / (1.0 + jnp.exp(-x.astype(F32)))


def _merge_fwd(z, y_gm, y_mla, y_mem, name):
    rows = z.shape[0]
    tr = _row_tile(rows, 14 * D_MODEL)

    def body(g0_ref, g1_ref, g2_ref, a_ref, b_ref, c_ref, o_ref):
        m = _sigmoid(g0_ref[...]) * a_ref[...] + _sigmoid(g1_ref[...]) * b_ref[...] + _sigmoid(g2_ref[...]) * c_ref[...]
        o_ref[...] = m.astype(BF16)

    r = _rowspec(tr, D_MODEL)
    return pl.pallas_call(
        body, name=name, grid=(rows // tr,),
        in_specs=[_rowspec(tr, D_MODEL, 0), _rowspec(tr, D_MODEL, 1), _rowspec(tr, D_MODEL, 2),r, r, r],
        out_specs=r, out_shape=jax.ShapeDtypeStruct((rows, D_MODEL), BF16), compiler_params=_params(("parallel",)),
    )(z, z, z, y_gm, y_mla, y_mem)


def _merge_bwd(z, y_gm, y_mla, y_mem, dm, name):
    rows = z.shape[0]
    tr = _row_tile(rows, 24 * D_MODEL)

    def body(g0_ref, g1_ref, g2_ref, a_ref, b_ref, c_ref, dm_ref, da_ref, db_ref, dc_ref, dzg_ref):
        dmv = dm_ref[...].astype(F32)
        for k, (g_ref, y_ref, dy_ref) in enumerate(((g0_ref, a_ref, da_ref), (g1_ref, b_ref, db_ref), (g2_ref, c_ref, dc_ref))):
            s = _sigmoid(g_ref[...])
            dy_ref[...] = (dmv * s).astype(BF16)
            dzg_ref[:, k * D_MODEL:(k + 1) * D_MODEL] = (dmv * y_ref[...] * s * (1.0 - s)).astype(BF16)

    r = _rowspec(tr, D_MODEL)
    o = jax.ShapeDtypeStruct((rows, D_MODEL), BF16)
    return pl.pallas_call(
        body, name=name, grid=(rows // tr,),
        in_specs=[_rowspec(tr, D_MODEL, 0), _rowspec(tr, D_MODEL, 1), _rowspec(tr, D_MODEL, 2),r, r, r, r],
        out_specs=[r, r, r, _rowspec(tr, 3 * D_MODEL, 0)],
        out_shape=[o, o, o, jax.ShapeDtypeStruct((rows, Z_COLS), BF16)],
        compiler_params=_params(("parallel",)),
    )(z, z, z, y_gm, y_mla, y_mem, dm)


_NT = (((1,), (1,)), ((), ()))
_TN = (((0,), (0,)), ((), ()))


def _diag_mask(s):
    row = lax.broadcasted_iota(jnp.int32, s.shape, 0)
    col = lax.broadcasted_iota(jnp.int32, s.shape, 1)
    return jnp.where(row >= col, s, NEG)


def _attn_fwd(q, k, v, nb, nheads, dk, v_col0, causal, name, rider=None):
    S, Skv = q.shape[0] // nb, k.shape[0] // nb
    tq = _pick(Skv, ATT_TILE) if causal else _pick(S, 4 * ATT_TILE)
    nq = S // tq

    def body(q_ref, k_ref, v_ref, o_ref, lse_ref):
        for i in range(nq):
            r0 = i * tq
            qb = q_ref[r0:r0 + tq, :]
            if causal:
                spans = ([(0, r0, False)] if i > 0 else []) + [(r0, r0 + tq, True)]
            else:
                spans = [(0, Skv, False)]
            scores = []
            for a, b, masked in spans:
                s = lax.dot_general(qb, k_ref[a:b, :], _NT, preferred_element_type=F32)
                scores.append(_diag_mask(s) if masked else s)
            m = functools.reduce(jnp.maximum, [jnp.max(s, axis=-1, keepdims=True) for s in scores])
            l = acc = None
            for s, (a, b, _) in zip(scores, spans):
                p = jnp.exp2(s - m)
                lp = jnp.sum(p, axis=-1, keepdims=True)
                ap = jnp.dot(p.astype(BF16), v_ref[a:b, :].astype(BF16), preferred_element_type=F32)
                l, acc = (lp, ap) if l is None else (l + lp, acc + ap)
            o_ref[r0:r0 + tq, :] = (acc / l).astype(BF16)
            lse_ref[r0:r0 + tq, :] = m + jnp.log2(l)

    ins = [q, k, v]
    in_specs = [pl.BlockSpec((S, dk), lambda b, h: (b, h)), pl.BlockSpec((Skv, dk), lambda b, h: (b, h)),
                pl.BlockSpec((Skv, HEAD), lambda b, h: (b, v_col0 + h))]
    out_specs = [pl.BlockSpec((S, HEAD), lambda b, h: (b, h)), pl.BlockSpec((None, S, 1), lambda b, h: (h, b, 0))]
    out_shape = [jax.ShapeDtypeStruct((nb * S, nheads * HEAD), BF16), jax.ShapeDtypeStruct((nheads, nb * S, 1), F32)]
    return _call(body, rider, ins, name=name, grid=(nb, nheads), in_specs=in_specs, out_specs=out_specs,
                 out_shape=out_shape, scratch_shapes=[], sem=("parallel", "parallel"))


def _attn_bwd(q, k, v, o, do, lse, nb, nheads, dk, v_col0, scale, causal, name, rider=None):
    S, Skv = q.shape[0] // nb, k.shape[0] // nb
    tk = _pick(Skv, ATT_TILE)
    nkv = Skv // tk

    def body(q_ref, k_ref, v_ref, o_ref, do_ref, lse_ref, dq_ref, dk_ref, dv_ref, delta_ref, dob_ref, dqa_ref):
        dov = do_ref[...]
        delta_ref[...] = jnp.sum(o_ref[...].astype(F32) * dov.astype(F32), axis=-1, keepdims=True)
        dob_ref[...] = dov.astype(BF16)

        for j in range(nkv):
            c0 = j * tk
            kb = k_ref[c0:c0 + tk, :]
            vb = v_ref[c0:c0 + tk, :].astype(BF16)
            if causal:
                spans = [(c0, c0 + tk, True)] + ([(c0 + tk, S, False)] if c0 + tk < S else [])
            else:
                spans = [(0, S, False)]
            dk_acc = dv_acc = None
            for a, b, masked in spans:
                qb = q_ref[a:b, :]
                dob = dob_ref[a:b, :]
                s = lax.dot_general(qb, kb, _NT, preferred_element_type=F32)
                if masked:
                    s = _diag_mask(s)
                p = jnp.exp2(s - lse_ref[a:b, :])
                dp = lax.dot_general(dob, vb, _NT, preferred_element_type=F32)
                ds = (p * (dp - delta_ref[a:b, :])).astype(BF16)
                dv_p = lax.dot_general(p.astype(BF16), dob, _TN, preferred_element_type=F32)
                dk_p = lax.dot_general(ds, qb, _TN, preferred_element_type=F32)
                dk_acc, dv_acc = (dk_p, dv_p) if dk_acc is None else (dk_acc + dk_p, dv_acc + dv_p)
                dq_p = jnp.dot(ds, kb, preferred_element_type=F32) * scale
                if j == 0:
                    dqa_ref[a:b, :] = dq_p
                else:
                    dqa_ref[a:b, :] += dq_p
            dk_ref[c0:c0 + tk, :] = (dk_acc * LN2).astype(BF16)
            dv_ref[c0:c0 + tk, :] = dv_acc.astype(BF16)
        dq_ref[...] = dqa_ref[...].astype(BF16)

    ins = [q, k, v, o, do, lse]
    in_specs = [pl.BlockSpec((S, dk), lambda b, h: (b, h)), pl.BlockSpec((Skv, dk), lambda b, h: (b, h)),
                pl.BlockSpec((Skv, HEAD), lambda b, h: (b, v_col0 + h)), pl.BlockSpec((S, HEAD), lambda b, h: (b, h)),
                pl.BlockSpec((S, HEAD), lambda b, h: (b, h)), pl.BlockSpec((None, S, 1), lambda b, h: (h, b, 0))]
    out_specs = [pl.BlockSpec((S, dk), lambda b, h: (b, h)), pl.BlockSpec((Skv, dk), lambda b, h: (b, h)),
                 pl.BlockSpec((Skv, HEAD), lambda b, h: (b, h))]
    out_shape = [jax.ShapeDtypeStruct((nb * S, nheads * dk), BF16), jax.ShapeDtypeStruct((nb * Skv, nheads * dk), BF16),
                 jax.ShapeDtypeStruct((nb * Skv, nheads * HEAD), BF16)]
    return _call(body, rider, ins, name=name, grid=(nb, nheads), in_specs=in_specs, out_specs=out_specs,
                 out_shape=out_shape,
                 scratch_shapes=[pltpu.VMEM((S, 1), F32), pltpu.VMEM((S, HEAD), BF16), pltpu.VMEM((S, dk), F32)],
                 sem=("parallel", "parallel"))


def _spread_rope(a):
    zero = jnp.zeros(a.shape[:-1] + (32,), a.dtype)
    return jnp.concatenate([a[..., :32], zero, a[..., 32:], zero], axis=-1)


def _gather_rope(a):
    return jnp.concatenate([a[..., 0:32], a[..., 64:96]], axis=-1)


def _win_layout(w):
    return jnp.concatenate([w[:, C_ZG:C_END], w[:, C_ZU:C_CQ], w[:, C_QM:C_ZG], w[:, C_CQ:C_CKV], w[:, C_CKV:C_KPE],
                            _spread_rope(w[:, C_KPE:C_QM])], axis=1)


def _win_unlayout(d):
    return jnp.concatenate([d[:, Z_GM:Z_QM], d[:, Z_MLA:Z_MLA + Q_LORA], d[:, Z_MLA + Q_LORA:Z_KPE],
                            _gather_rope(d[:, Z_KPE:Z_COLS]), d[:, Z_QM:Z_MLA], d[:, 0:Z_GM]], axis=1)


def _wuq_layout(w):
    r = w.reshape(Q_LORA, MLA_HEADS, HEAD + MLA_ROPE)
    return jnp.concatenate([r[:, :, :HEAD].reshape(Q_LORA, -1), _spread_rope(r[:, :, HEAD:]).reshape(Q_LORA, -1)], axis=1)


def _wuq_unlayout(d):
    n = d[:, :MLA_HEADS * HEAD].reshape(Q_LORA, MLA_HEADS, HEAD)
    p = _gather_rope(d[:, MLA_HEADS * HEAD:].reshape(Q_LORA, MLA_HEADS, HEAD))
    return jnp.concatenate([n, p], axis=-1).reshape(Q_LORA, -1)


def _wukv_layout(w):
    r = w.reshape(KV_LORA, MLA_HEADS, 2 * HEAD)
    return jnp.concatenate([r[:, :, :HEAD].reshape(KV_LORA, -1), r[:, :, HEAD:].reshape(KV_LORA, -1)], axis=1)


def _wukv_unlayout(d):
    k = d[:, :MLA_HEADS * HEAD].reshape(KV_LORA, MLA_HEADS, HEAD)
    v = d[:, MLA_HEADS * HEAD:].reshape(KV_LORA, MLA_HEADS, HEAD)
    return jnp.concatenate([k, v], axis=-1).reshape(KV_LORA, -1)


AG_MID = ['w_uq', 'w_ukv', 'w_mem_kv', 'w_o_gm', 'w_o_mla', 'w_o_mem', 'w_out']
AG_FFN = ['w_ff1', 'w_ff2']
RS_GROUPS = {'ff2': ['w_ff2'], 'proj': ['w_ff1', 'w_out', 'w_o_gm', 'w_o_mla', 'w_o_mem'],
             'lat': ['w_uq', 'w_ukv', 'w_mem_kv'], 'in_top': ['w_in'], 'in_bot': ['w_in']}


def _unride(res, rider):
    return (res, None) if rider is None else res


def _local_step(x, mem, positions, target, P, ws):
    B, S, _ = x.shape
    M = mem.shape[1]
    T = B * S
    x2d = x.reshape(T, D_MODEL)
    mem2d = mem.reshape(B * M, D_MODEL)
    tgt2d = target.reshape(T, D_MODEL)

    def row(v):
        return v.reshape(1, -1).astype(F32)

    inv_freq = ROPE_BASE ** (-jnp.arange(0, MLA_ROPE, 2, dtype=F32) / MLA_ROPE)
    ang = positions.reshape(T).astype(F32)[:, None] * inv_freq
    cos, sin, zero = jnp.cos(ang), jnp.sin(ang), jnp.zeros_like(ang)
    cc = jnp.concatenate([cos, zero, cos, zero], axis=1)
    ss = jnp.concatenate([-sin, zero, sin, zero], axis=1)

    g_mix, g_cq, g_ckv, g_ffn, g_mem = row(P['g_mix']), row(P['g_cq']), row(P['g_ckv']), row(P['g_ffn']), row(P['g_mem'])
    gqn, gkn, gmq, gmk = row(P['g_q_nope']), row(P['g_k_nope']), row(P['g_mq']), row(P['g_mk'])
    gqp, gkp = _spread_rope(row(P['g_q_pe'])), _spread_rope(row(P['g_k_pe']))
    gln, bln = row(P['g_gm_ln']), row(P['b_gm_ln'])
    wc = jnp.tril(P['w_spatial'].astype(F32))
    wct = jnp.swapaxes(wc, 1, 2).astype(BF16)
    wc = wc.astype(BF16)
    bst = jnp.broadcast_to(P['b_spatial'].astype(F32)[:, :, None], (GM_GROUPS, GM_CHUNK, LANES))

    ride = ws.gather(['w_in'])
    h, got = _unride(_rms_fwd(x2d, g_mix, "rms_mix", rider=ride), ride)
    w_in = _win_layout(ws.gathered(['w_in'], got)['w_in']).astype(BF16)
    ride = ws.gather(AG_MID)
    z, got = _unride(_matmul(h, w_in, 'nn', ACT, "mm_in", tn_t=768, rider=ride), ride)
    mid = ws.gathered(AG_MID, got)
    w_uq = _wuq_layout(mid['w_uq']).astype(BF16)
    w_ukv = _wukv_layout(mid['w_ukv']).astype(BF16)
    w_mem_kv, w_o_gm, w_o_mla, w_o_mem, w_out = (mid[n].astype(BF16) for n in ('w_mem_kv', 'w_o_gm', 'w_o_mla', 'w_o_mem',
                                                                                 'w_out'))
    ygm_pre = _gm_fwd(z, gln, bln, wc, bst, "gm_fwd")
    y_gm = _matmul(ygm_pre, w_o_gm, 'nn', ACT, "mm_o_gm")
    nq, nkv = _lat_fwd(z, g_cq, g_ckv, "lat_fwd")
    q = _matmul(nq, w_uq, 'nn', ACT, "mm_uq")
    kv = _matmul(nkv, w_ukv, 'nn', ACT, "mm_ukv")
    qcat, kcat, vv = _qk_fwd(q, kv, z, cc, ss, gqn, gqp, gkn, gkp, "qk_fwd")
    ride = ws.gather(AG_FFN)
    (o, lse), got = _unride(_attn_fwd(qcat, kcat, vv, B, MLA_HEADS, QCAT, 0, True, "mla_attn_fwd", rider=ride), ride)
    ffn = ws.gathered(AG_FFN, got)
    w_ff1, w_ff2 = ffn['w_ff1'].astype(BF16), ffn['w_ff2'].astype(BF16)
    y_mla = _matmul(o, w_o_mla, 'nn', ACT, "mm_o_mla")
    nm = _rms_fwd(mem2d, g_mem, "rms_mem")
    kvm = _matmul(nm, w_mem_kv, 'nn', ACT, "mm_mem_kv")
    qm = _headnorm_fwd(z, Z_QM // (MEM_HEADS * HEAD), MEM_HEADS, gmq, MEM_SCALE * LOG2E, "memq_fwd")
    km = _headnorm_fwd(kvm, 0, MEM_HEADS, gmk, 1.0, "memk_fwd")
    om, lse_m = _attn_fwd(qm, km, kvm, B, MEM_HEADS, HEAD, MEM_HEADS, False, "mem_attn_fwd")
    y_mem = _matmul(om, w_o_mem, 'nn', ACT, "mm_o_mem")
    merged = _merge_fwd(z, y_gm, y_mla, y_mem, "merge_fwd")
    x1 = _matmul(merged, w_out, 'nn', F32, "mm_out", add=x2d)
    h2 = _rms_fwd(x1, g_ffn, "rms_ffn")
    a1 = _matmul(h2, w_ff1, 'nn', BF16, "mm_ff1")
    dx2, dx2b, loss_part = _matmul(a1, w_ff2, 'nn', F32, "mm_ff2", add=x1, relu2_a=True, sq_err_target=tgt2d)

    G = {}
    ride = ws.scatter('ff2', {'w_ff2': _matmul(a1, dx2b, 'tn', BF16, "mm_d_ff2", relu2_a=True)})
    da1, got = _unride(_matmul(dx2b, w_ff2, 'nt', BF16, "mm_da1", relu2_grad=a1, rider=ride), ride)
    ws.scattered('ff2', got)
    d_ff1 = _matmul(h2, da1, 'tn', BF16, "mm_d_ff1", col_shards=N_DEV)
    dh2 = _matmul(da1, w_ff1, 'nt', ACT, "mm_dh2")
    dx1, dx1b, G['g_ffn'] = _rms_bwd(x1, g_ffn, dh2, dx2, "rms_ffn_bwd", dx_dtypes=(F32, BF16))
    d_out = _matmul(merged, dx1b, 'tn', BF16, "mm_d_out")
    dmerged = _matmul(dx1b, w_out, 'nt', ACT, "mm_dmerged")
    dy_gm, dy_mla, dy_mem, dz = _merge_bwd(z, y_gm, y_mla, y_mem, dmerged, "merge_bwd")
    d_o_gm = _matmul(ygm_pre, dy_gm, 'tn', BF16, "mm_d_o_gm")
    d_o_mla = _matmul(o, dy_mla, 'tn', BF16, "mm_d_o_mla")
    d_o_mem = _matmul(om, dy_mem, 'tn', BF16, "mm_d_o_mem")
    dygm_pre = _matmul(dy_gm, w_o_gm, 'nt', ACT, "mm_dygm")
    dz, dws, dbs, G['g_gm_ln'], G['b_gm_ln'] = _gm_bwd(z, dygm_pre, gln, bln, wc, wct, bst, dz, "gm_bwd")
    G['w_spatial'] = jnp.tril(dws)
    G['b_spatial'] = jnp.sum(dbs.reshape(GM_CHUNK, GM_GROUPS, LANES), axis=-1).T
    do = _matmul(dy_mla, w_o_mla, 'nt', ACT, "mm_do")
    ride = ws.scatter('proj', {'w_ff1': d_ff1, 'w_out': d_out, 'w_o_gm': d_o_gm, 'w_o_mla': d_o_mla, 'w_o_mem': d_o_mem})
    (dqc, dkc, dvv), got = _unride(_attn_bwd(qcat, kcat, vv, o, do, lse, B, MLA_HEADS, QCAT, 0, MLA_SCALE, True,
                                             "mla_attn_bwd", rider=ride), ride)
    ws.scattered('proj', got)
    dq, dkv, dkpe, G['g_q_nope'], dgqp, G['g_k_nope'], dgkp = _qk_bwd(q, kv, z, cc, ss, gqn, gqp, gkn, gkp, dqc, dkc, dvv,
                                                                     "qk_bwd")
    G['g_q_pe'], G['g_k_pe'] = _gather_rope(dgqp), _gather_rope(dgkp)
    d_uq = _wuq_unlayout(_matmul(nq, dq, 'tn', BF16, "mm_d_uq"))
    dnq = _matmul(dq, w_uq, 'nt', ACT, "mm_dnq")
    d_ukv = _wukv_unlayout(_matmul(nkv, dkv, 'tn', BF16, "mm_d_ukv"))
    dnkv = _matmul(dkv, w_ukv, 'nt', ACT, "mm_dnkv")
    dz, G['g_cq'], G['g_ckv'] = _lat_bwd(z, dnq, dnkv, dkpe, g_cq, g_ckv, dz, "lat_bwd")
    dom = _matmul(dy_mem, w_o_mem, 'nt', ACT, "mm_dom")
    dqm, dkm, dvm = _attn_bwd(qm, km, kvm, om, dom, lse_m, B, MEM_HEADS, HEAD, MEM_HEADS, MEM_SCALE, False, "mem_attn_bwd")
    dz, G['g_mq'] = _headnorm_bwd(z, Z_QM // (MEM_HEADS * HEAD), MEM_HEADS, gmq, dqm, None, "memq_bwd",
                                  into=(dz, Z_QM // (MEM_HEADS * HEAD)))
    dkvm, G['g_mk'] = _headnorm_bwd(kvm, 0, MEM_HEADS, gmk, dkm, dvm, "memk_bwd")
    d_mem_kv = _matmul(nm, dkvm, 'tn', BF16, "mm_d_mem_kv")
    dnm = _matmul(dkvm, w_mem_kv, 'nt', ACT, "mm_dnm")
    G['g_mem'], = _rms_bwd(mem2d, g_mem, dnm, None, "rms_mem_bwd", dx_dtypes=())
    half = D_MODEL // 2
    ride = ws.scatter('lat', {'w_uq': d_uq, 'w_ukv': d_ukv, 'w_mem_kv': d_mem_kv})
    d_top, got = _unride(_matmul(h, dz, 'tn', BF16, "mm_d_in_top", tn_t=768, m_rows=(0, half), rider=ride), ride)
    ws.scattered('lat', got)
    ride = ws.scatter('in_top', {'w_in': _win_unlayout(d_top)})
    d_bot, got = _unride(_matmul(h, dz, 'tn', BF16, "mm_d_in_bot", tn_t=768, m_rows=(half, half), rider=ride), ride)
    ws.scattered('in_top', got)
    ride = ws.scatter('in_bot', {'w_in': _win_unlayout(d_bot)})
    dh, got = _unride(_matmul(dz, w_in, 'nt', ACT, "mm_dh", rider=ride), ride)
    ws.scattered('in_bot', got)
    gx, G['g_mix'] = _rms_bwd(x2d, g_mix, dh, dx1, "rms_mix_bwd")
    return loss_part, gx.reshape(B, S, D_MODEL), G


def _all_gather8(xs, name):
    def body(x_ref, out_ref, send_sems, recv_sems, local_sem):
        x, y, c = lax.axis_index("x"), lax.axis_index("y"), lax.axis_index("c")
        me, sibling = (x, y, c), (x, y, 1 - c)
        chips = [(1 - x, y), (x, 1 - y), (1 - x, 1 - y)]

        def rows(px, py, pc):
            return out_ref.at[4 * px + 2 * py + pc]

        def copy(k, block, to, src=None):
            return pltpu.make_async_remote_copy(
                src_ref=rows(*block) if src is None else src, dst_ref=rows(*block),
                send_sem=send_sems.at[k], recv_sem=recv_sems.at[k], device_id=to, device_id_type=MESH)

        mine = pltpu.make_async_copy(x_ref, rows(*me), local_sem)
        mine.start()
        first = [copy(0, me, sibling, src=x_ref)]
        first += [copy(1 + j, me, (*chip, c), src=x_ref) for j, chip in enumerate(chips)]
        for cp in first:
            cp.start()
        passed = [copy(4 + j, (*chip, c), sibling) for j, chip in enumerate(chips)]
        for j, chip in enumerate(chips):
            copy(1 + j, (*chip, c), me).wait_recv()
            passed[j].start()
        copy(0, sibling, me).wait_recv()
        for j, chip in enumerate(chips):
            copy(4 + j, (*chip, 1 - c), me).wait_recv()
        for cp in first + passed:
            cp.wait_send()
        mine.wait()

    return pl.pallas_call(
        body, name=name, in_specs=[HBM_SPEC], out_specs=HBM_SPEC,
        out_shape=jax.ShapeDtypeStruct((N_DEV,) + xs.shape, xs.dtype),
        scratch_shapes=[pltpu.SemaphoreType.DMA((7,)), pltpu.SemaphoreType.DMA((7,)), pltpu.SemaphoreType.DMA],
    )(xs)


def _adamw_rows(w, g, m, v):
    m2 = ADAM_B1 * m + (1.0 - ADAM_B1) * g
    v2 = ADAM_B2 * v + (1.0 - ADAM_B2) * (g * g)
    m_hat = m2 / (1.0 - ADAM_B1 ** ADAM_STEP)
    v_hat = v2 / (1.0 - ADAM_B2 ** ADAM_STEP)
    delta = -ADAM_LR * (m_hat / (jnp.sqrt(v_hat) + ADAM_EPS) + ADAM_WD * w)
    return delta, m2, v2


def _sum_adamw(parts, w, m, v, name):
    rows, cols = w.shape
    assert sum(p.shape[1] for p in parts) == rows
    tr = _pick(min(p.shape[1] for p in parts), max(16, 65536 // cols), 16)
    n = parts[0].shape[0]
    counts = [p.shape[1] // tr for p in parts]
    starts = [sum(counts[:k]) for k in range(len(parts))]

    def body(*refs):
        p_refs = refs[:len(parts)]
        w_ref, m_ref, v_ref, g_ref, d_ref, m2_ref, v2_ref = refs[len(parts):]
        g = None
        for p_ref, start in zip(p_refs, starts):
            gk = p_ref[0].astype(F32)
            for k in range(1, n):
                gk = gk + p_ref[k].astype(F32)
            g = gk if g is None else jnp.where(pl.program_id(0) >= start, gk, g)
        delta, m2, v2 = _adamw_rows(w_ref[...], g, m_ref[...], v_ref[...])
        g_ref[...] = g
        d_ref[...] = delta
        m2_ref[...] = m2
        v2_ref[...] = v2

    flat = pl.BlockSpec((tr, cols), lambda i: (i, 0))
    out = jax.ShapeDtypeStruct((rows, cols), F32)
    p_specs = [pl.BlockSpec((n, tr, cols), lambda i, s=s, c=c: (0, jnp.clip(i - s, 0, c - 1), 0))
               for s, c in zip(starts, counts)]
    return pl.pallas_call(
        body, name=name, grid=(rows // tr,), in_specs=p_specs + [flat, flat, flat], out_specs=[flat] * 4,
        out_shape=[out] * 4, compiler_params=_params(("parallel",)),
    )(*parts, w, m, v)


def _small_rows(name):
    n = {'g_mix': 1024, 'g_cq': 384, 'g_ckv': 256, 'g_q_nope': 128, 'g_q_pe': 64, 'g_k_nope': 128, 'g_k_pe': 64,
         'g_gm_ln': 512, 'b_gm_ln': 512, 'w_spatial': GM_GROUPS * GM_CHUNK * GM_CHUNK, 'b_spatial': GM_GROUPS * GM_CHUNK,
         'g_mem': 1024, 'g_mq': 128, 'g_mk': 128, 'g_ffn': 1024}[name]
    return n, -(-n // (8 * LANES)) * 8


def _small_slab(d):
    parts = []
    for name in SMALL:
        n, rows = _small_rows(name)
        parts.append(jnp.pad(d[name].reshape(-1).astype(F32), (0, rows * LANES - n)).reshape(rows, LANES))
    return jnp.concatenate(parts, axis=0)


def _small_unslab(slab, like):
    out, r = {}, 0
    for name in SMALL:
        n, rows = _small_rows(name)
        out[name] = slab[r:r + rows].reshape(-1)[:n].reshape(like[name].shape)
        r += rows
    return out


def _full_from_gathered(gathered, name):
    r, c = BIG_SHAPE[name]
    if BIG_AXIS[name] == 0:
        return gathered.reshape(r, c)
    return gathered.transpose(1, 0, 2).reshape(r, c)


def _shards_of_full(g, name):
    if g.ndim == 3:
        return g
    r, c = BIG_SHAPE[name]
    if BIG_AXIS[name] == 0:
        return g.reshape(N_DEV, r // N_DEV, c)
    return g.reshape(g.shape[0], N_DEV, c // N_DEV).transpose(1, 0, 2)


class _DistWeights:
    def __init__(self, shards):
        self.shards = shards
        self.received = {}

    def gather(self, names):
        return _Gather2([self.shards[n].astype(BF16) for n in names])

    def gathered(self, names, got):
        return {n: _full_from_gathered(g, n) for n, g in zip(names, got)}

    def scatter(self, key, grads):
        return _Exchange([_shards_of_full(grads[n], n) for n in RS_GROUPS[key]], scatter=True)

    def scattered(self, key, got):
        for n, g in zip(RS_GROUPS[key], got):
            self.received.setdefault(n, []).append(g)


def kernel(x, mem, positions, g_mix, w_in, g_cq, w_uq, g_ckv, w_ukv, g_q_nope, g_q_pe, g_k_nope, g_k_pe, g_gm_ln, b_gm_ln, w_spatial, b_spatial, g_mem, w_mem_kv, g_mq, g_mk, w_o_gm, w_o_mla, w_o_mem, w_out, g_ffn, w_ff1, w_ff2, loss_target, m_g_mix, m_w_in, m_g_cq, m_w_uq, m_g_ckv, m_w_ukv, m_g_q_nope, m_g_q_pe, m_g_k_nope, m_g_k_pe, m_g_gm_ln, m_b_gm_ln, m_w_spatial, m_b_spatial, m_g_mem, m_w_mem_kv, m_g_mq, m_g_mk, m_w_o_gm, m_w_o_mla, m_w_o_mem, m_w_out, m_g_ffn, m_w_ff1, m_w_ff2, v_g_mix, v_w_in, v_g_cq, v_w_uq, v_g_ckv, v_w_ukv, v_g_q_nope, v_g_q_pe, v_g_k_nope, v_g_k_pe, v_g_gm_ln, v_b_gm_ln, v_w_spatial, v_b_spatial, v_g_mem, v_w_mem_kv, v_g_mq, v_g_mk, v_w_o_gm, v_w_o_mla, v_w_o_mem, v_w_out, v_g_ffn, v_w_ff1, v_w_ff2):
    given = dict(locals())
    w = {n: given[n][0] for n in WEIGHTS}
    mom = {n: given['m_' + n][0] for n in WEIGHTS}
    var = {n: given['v_' + n][0] for n in WEIGHTS}

    ws = _DistWeights({n: w[n] for n in BIG})
    loss_part, grad_x, G = _local_step(x, mem, positions, loss_target, {n: w[n] for n in SMALL}, ws)

    outs = {}
    for n in BIG:
        for prefix, res in zip(("grad_", "delta_", "new_m_", "new_v_"),
                               _sum_adamw(ws.received[n], w[n], mom[n], var[n], "adamw_" + n)):
            outs[prefix + n] = res[None]

    tail = jnp.zeros((8, LANES), F32)
    parts = _all_gather8(jnp.concatenate([_small_slab(G), jnp.pad(loss_part, ((0, 7), (0, 0)))], axis=0), "ag_small")
    small = _sum_adamw([parts], *[jnp.concatenate([_small_slab(d), tail], axis=0) for d in (w, mom, var)], "adamw_small")
    loss = 0.5 * jnp.sum(small[0][-8:]) / D_MODEL
    for prefix, small_slab in zip(("grad_", "delta_", "new_m_", "new_v_"), small):
        sm = _small_unslab(small_slab, given)
        for n in SMALL:
            outs[prefix + n] = sm[n]
    return (loss, grad_x, *[outs[p + n] for p in ("grad_", "delta_", "new_m_", "new_v_") for n in WEIGHTS])
```

```python
import functools
import math

import jax
import jax.numpy as jnp
from jax import lax
from jax.experimental import pallas as pl
from jax.experimental.pallas import tpu as pltpu

F32 = jnp.float32
BF16 = jnp.bfloat16
ACT = BF16

D_MODEL = 1024
MEM_HEADS = 4
HEAD = 128
GM_WIDTH = 512
GM_CHUNK = 128
GM_GROUPS = 4
MLA_HEADS = 8
MLA_ROPE = 64
Q_LORA = 384
KV_LORA = 256
D_FF = 4096
EPS = 1e-6
ROPE_BASE = 10000.0
MLA_SCALE = 1.0 / math.sqrt(HEAD + MLA_ROPE)
MEM_SCALE = 1.0 / math.sqrt(HEAD)
LOG2E = 1.4426950408889634
LN2 = 0.6931471805599453
ATT_TILE = 256
C_ZU, C_ZV, C_CQ, C_CKV, C_KPE, C_QM, C_ZG, C_END = 0, 512, 1024, 1408, 1664, 1728, 2240, 5312
Z_GM, Z_QM, Z_MLA, Z_KPE, Z_COLS = 3072, 4096, 4608, 5248, 5376
MLA_W = 768
QCAT = 2 * HEAD
ADAM_LR, ADAM_B1, ADAM_B2, ADAM_EPS, ADAM_WD, ADAM_STEP = 0.001, 0.9, 0.999, 1e-08, 0.01, 10
N_DEV = 8
LANES = 128
VMEM_LIMIT = 48 * 1024 * 1024
MAX_K_TILE = 8192
NEG = -1e30

BIG = ['w_in', 'w_uq', 'w_ukv', 'w_mem_kv', 'w_o_gm', 'w_o_mla', 'w_o_mem', 'w_out', 'w_ff1', 'w_ff2']
BIG_AXIS = {'w_in': 1, 'w_uq': 1, 'w_ukv': 1, 'w_mem_kv': 0, 'w_o_gm': 1, 'w_o_mla': 0, 'w_o_mem': 1,
            'w_out': 0, 'w_ff1': 1, 'w_ff2': 0}
BIG_SHAPE = {'w_in': (1024, 5312), 'w_uq': (384, 1536), 'w_ukv': (256, 2048), 'w_mem_kv': (1024, 1024),
             'w_o_gm': (512, 1024), 'w_o_mla': (1024, 1024), 'w_o_mem': (512, 1024), 'w_out': (1024, 1024),
             'w_ff1': (1024, 4096), 'w_ff2': (4096, 1024)}
SMALL = ['g_mix', 'g_cq', 'g_ckv', 'g_q_nope', 'g_q_pe', 'g_k_nope', 'g_k_pe', 'g_gm_ln', 'b_gm_ln',
         'w_spatial', 'b_spatial', 'g_mem', 'g_mq', 'g_mk', 'g_ffn']
WEIGHTS = ['g_mix', 'w_in', 'g_cq', 'w_uq', 'g_ckv', 'w_ukv', 'g_q_nope', 'g_q_pe', 'g_k_nope', 'g_k_pe',
           'g_gm_ln', 'b_gm_ln', 'w_spatial', 'b_spatial', 'g_mem', 'w_mem_kv', 'g_mq', 'g_mk', 'w_o_gm',
           'w_o_mla', 'w_o_mem', 'w_out', 'g_ffn', 'w_ff1', 'w_ff2']


def _pick(n, target, mult=LANES):
    best = None
    t = mult
    while t <= min(n, target):
        if n % t == 0:
            best = t
        t += mult
    return best if best is not None else n


def _params(sem):
    return pltpu.CompilerParams(dimension_semantics=sem, vmem_limit_bytes=VMEM_LIMIT)


MESH = pl.DeviceIdType.MESH
HBM_SPEC = pl.BlockSpec(memory_space=pltpu.HBM)


class _Exchange:
    def __init__(self, srcs, scatter):
        self.srcs, self.scatter = list(srcs), scatter
        self.out_shapes = [jax.ShapeDtypeStruct(s.shape if scatter else (N_DEV,) + s.shape, s.dtype) for s in self.srcs]
        n = len(self.srcs)
        self.scratch = [pltpu.SemaphoreType.DMA((n, N_DEV - 1)), pltpu.SemaphoreType.DMA((n, N_DEV - 1)),
                        pltpu.SemaphoreType.DMA((n,))]

    def _copies(self, src_refs, dst_refs, send_sems, recv_sems, local_sems):
        x, y, c = lax.axis_index("x"), lax.axis_index("y"), lax.axis_index("c")
        me = 4 * x + 2 * y + c
        local, remote = [], []
        for a, (src_ref, dst_ref) in enumerate(zip(src_refs, dst_refs)):
            def mine_for(dev, src_ref=src_ref):
                return src_ref.at[dev] if self.scatter else src_ref

            local.append(pltpu.make_async_copy(mine_for(me), dst_ref.at[me], local_sems.at[a]))
            for k in range(1, N_DEV):
                px = 1 - x if k & 4 else x
                py = 1 - y if k & 2 else y
                pc = 1 - c if k & 1 else c
                remote.append(pltpu.make_async_remote_copy(
                    src_ref=mine_for(4 * px + 2 * py + pc), dst_ref=dst_ref.at[me], send_sem=send_sems.at[a, k - 1],
                    recv_sem=recv_sems.at[a, k - 1], device_id=(px, py, pc), device_id_type=MESH))
        return local, remote

    def start(self, *refs):
        local, remote = self._copies(*refs)
        for cp in local + remote:
            cp.start()

    def wait(self, *refs):
        local, remote = self._copies(*refs)
        for cp in remote + local:
            cp.wait()


class _Gather2:
    def __init__(self, srcs):
        self.srcs = list(srcs)
        self.out_shapes = [jax.ShapeDtypeStruct((N_DEV,) + s.shape, s.dtype) for s in self.srcs]
        n = len(self.srcs)
        self.scratch = [pltpu.SemaphoreType.DMA((n, N_DEV - 1)), pltpu.SemaphoreType.DMA((n, N_DEV - 1)),
                        pltpu.SemaphoreType.DMA((n,))]

    def _plan(self, src_refs, dst_refs, send_sems, recv_sems, local_sems):
        x, y, c = lax.axis_index("x"), lax.axis_index("y"), lax.axis_index("c")
        chips = [(1 - x, y), (x, 1 - y), (1 - x, 1 - y)]
        plans = []
        for a, (src_ref, dst_ref) in enumerate(zip(src_refs, dst_refs)):
            def copy(k, block, to, src=None, a=a, dst_ref=dst_ref):
                at = dst_ref.at[4 * block[0] + 2 * block[1] + block[2]]
                return pltpu.make_async_remote_copy(src_ref=at if src is None else src, dst_ref=at,
                                                    send_sem=send_sems.at[a, k], recv_sem=recv_sems.at[a, k],
                                                    device_id=to, device_id_type=MESH)

            local = pltpu.make_async_copy(src_ref, dst_ref.at[4 * x + 2 * y + c], local_sems.at[a])
            first = [copy(0, (x, y, c), (x, y, 1 - c), src=src_ref)]
            first += [copy(1 + j, (x, y, c), (*chip, c), src=src_ref) for j, chip in enumerate(chips)]
            passed = [copy(4 + j, (*chip, c), (x, y, 1 - c)) for j, chip in enumerate(chips)]
            arrivals = [copy(1 + j, (*chip, c), (x, y, c)) for j, chip in enumerate(chips)]
            late = [copy(0, (x, y, 1 - c), (x, y, c))] + [copy(4 + j, (*chip, 1 - c), (x, y, c)) for j, chip in enumerate(chips)]
            plans.append((local, first, passed, arrivals, late))
        return plans

    def start(self, *refs):
        for local, first, _, _, _ in self._plan(*refs):
            local.start()
            for cp in first:
                cp.start()

    def wait(self, *refs):
        plans = self._plan(*refs)
        for _, _, passed, arrivals, _ in plans:
            for arrived, onward in zip(arrivals, passed):
                arrived.wait_recv()
                onward.start()
        for local, first, passed, _, late in plans:
            for cp in late:
                cp.wait_recv()
            for cp in first + passed:
                cp.wait_send()
            local.wait()


def _call(body, rider, ins, *, name, grid, in_specs, out_specs, out_shape, scratch_shapes, sem):
    if rider is None:
        return pl.pallas_call(body, name=name, grid=grid, in_specs=in_specs, out_specs=out_specs, out_shape=out_shape,
                              scratch_shapes=scratch_shapes, compiler_params=_params(sem))(*ins)
    single = not isinstance(out_shape, (list, tuple))
    own_specs, own_shapes = ([out_specs], [out_shape]) if single else (list(out_specs), list(out_shape))
    n_in, n_out, n_sc, n_r = len(ins), len(own_shapes), len(scratch_shapes), len(rider.srcs)
    n_all_in = n_in + n_r

    def carrying(*refs):
        own_in, srcs = refs[:n_in], refs[n_in:n_in + n_r]
        own_out, dsts = refs[n_all_in:n_all_in + n_out], refs[n_all_in + n_out:n_all_in + n_out + n_r]
        own_sc = refs[n_all_in + n_out + n_r:n_all_in + n_out + n_r + n_sc]
        sems = refs[n_all_in + n_out + n_r + n_sc:]
        first = last = None
        for d, steps in enumerate(grid):
            f, l = pl.program_id(d) == 0, pl.program_id(d) == steps - 1
            first, last = (f, l) if first is None else (first & f, last & l)

        @pl.when(first)
        def _():
            rider.start(srcs, dsts, *sems)

        body(*own_in, *own_out, *own_sc)

        @pl.when(last)
        def _():
            rider.wait(srcs, dsts, *sems)

    res = pl.pallas_call(
        carrying, name=name, grid=grid, in_specs=list(in_specs) + [HBM_SPEC] * n_r,
        out_specs=own_specs + [HBM_SPEC] * n_r, out_shape=own_shapes + rider.out_shapes,
        scratch_shapes=list(scratch_shapes) + rider.scratch, compiler_params=_params(("arbitrary",) * len(grid)),
    )(*ins, *rider.srcs)
    own = res[:n_out]
    return (own[0] if single else list(own)), list(res[n_out:])


def _matmul(a, b, mode, out_dtype, name, add=None, relu2_a=False, relu2_grad=None,
            tm_t=None, tn_t=None, tk_t=None, rider=None, m_rows=None, col_shards=None, sq_err_target=None):
    if mode == 'nn':
        (M, K), (K2, N) = a.shape, b.shape
    elif mode == 'nt':
        (M, K), (N, K2) = a.shape, b.shape
    else:
        (K, M), (K2, N) = a.shape, b.shape
    assert K == K2, (name, a.shape, b.shape)
    m_first = 0
    if m_rows is not None:
        assert mode == 'tn'
        m_first, M = m_rows
    if col_shards is not None:
        assert add is None and relu2_grad is None and sq_err_target is None and tn_t is None
        tn_t = N // col_shards
    if mode == 'tn':
        d_tm, d_tn, d_tk = 1024, 1024, 2048
    else:
        d_tm, d_tn, d_tk = (2048 if K <= 1024 else 1024), 512, MAX_K_TILE
    tm, tn, tk = _pick(M, tm_t or d_tm), _pick(N, tn_t or d_tn), _pick(K, tk_t or d_tk)
    gm, gn, nk = M // tm, N // tn, K // tk
    if mode == 'nn':
        a_spec = pl.BlockSpec((tm, tk), lambda i, j, k: (i, k))
        b_spec = pl.BlockSpec((tk, tn), lambda i, j, k: (k, j))
        dims = (((1,), (0,)), ((), ()))
    elif mode == 'nt':
        a_spec = pl.BlockSpec((tm, tk), lambda i, j, k: (i, k))
        b_spec = pl.BlockSpec((tn, tk), lambda i, j, k: (j, k))
        dims = (((1,), (1,)), ((), ()))
    else:
        assert m_first % tm == 0
        a_spec = pl.BlockSpec((tk, tm), lambda i, j, k: (k, m_first // tm + i))
        b_spec = pl.BlockSpec((tk, tn), lambda i, j, k: (k, j))
        dims = (((0,), (0,)), ((), ()))
    o_spec = pl.BlockSpec((tm, tn), lambda i, j, k: (i, j))
    has_add, has_e, has_t = add is not None, relu2_grad is not None, sq_err_target is not None
    assert not has_t or (nk == 1 and tn % LANES == 0)

    def body(*refs):
        a_ref, b_ref = refs[0], refs[1]
        pos = 2
        add_ref = e_ref = t_ref = None
        if has_add:
            add_ref = refs[pos]
            pos += 1
        if has_e:
            e_ref = refs[pos]
            pos += 1
        if has_t:
            t_ref = refs[pos]
            pos += 1
        o_ref = refs[pos]
        acc_ref = refs[pos + 1] if nk > 1 else None

        av = a_ref[...]
        if relu2_a:
            av = jnp.maximum(av, 0)
            av = av * av
        prod = lax.dot_general(av.astype(BF16), b_ref[...].astype(BF16), dims, preferred_element_type=F32)

        def finish(r):
            if has_add:
                r = r + add_ref[...]
            if has_e:
                r = r * (2.0 * jnp.maximum(e_ref[...].astype(F32), 0.0))
            if has_t:
                err = r - t_ref[...]
                r = err * (1.0 / N)
                refs[pos + 1][...] = r.astype(BF16)
                sq = err * err
                part = sq[:, 0:LANES]
                for c in range(1, tn // LANES):
                    part = part + sq[:, c * LANES:(c + 1) * LANES]
                _acc_rows(refs[pos + 2], part, (pl.program_id(0) == 0) & (pl.program_id(1) == 0))
            o_ref[...] = r.astype(out_dtype)

        if nk == 1:
            finish(prod)
        else:
            k = pl.program_id(2)

            @pl.when(k == 0)
            def _():
                acc_ref[...] = prod

            @pl.when(k > 0)
            def _():
                acc_ref[...] += prod

            @pl.when(k == nk - 1)
            def _():
                finish(acc_ref[...])

    ins, specs = [a, b], [a_spec, b_spec]
    if has_add:
        ins.append(add)
        specs.append(o_spec)
    if has_e:
        ins.append(relu2_grad)
        specs.append(o_spec)
    out_specs, out_shape, sem = o_spec, jax.ShapeDtypeStruct((M, N), out_dtype), ("parallel", "parallel", "arbitrary")
    if has_t:
        ins.append(sq_err_target)
        specs.append(o_spec)
        out_specs = [o_spec, o_spec, pl.BlockSpec((1, LANES), lambda i, j, k: (0, 0))]
        out_shape = [out_shape, jax.ShapeDtypeStruct((M, N), BF16), jax.ShapeDtypeStruct((1, LANES), F32)]
        sem = ("arbitrary", "arbitrary", "arbitrary")
    if col_shards is not None:
        out_specs = pl.BlockSpec((None, tm, tn), lambda i, j, k: (j, i, 0))
        out_shape = jax.ShapeDtypeStruct((col_shards, M, tn), out_dtype)
    return _call(body, rider, ins, name=name, grid=(gm, gn, nk), in_specs=specs, out_specs=out_specs, out_shape=out_shape,
                 scratch_shapes=[pltpu.VMEM((tm, tn), F32)] if nk > 1 else [], sem=sem)


ROW_BLOCK_BYTES = 12 * 1024 * 1024


def _row_tile(rows, row_bytes):
    return _pick(rows, max(16, min(1024, ROW_BLOCK_BYTES // row_bytes)), 16)


def _rowspec(tr, width, col=0):
    return pl.BlockSpec((tr, width), lambda i, col=col: (i, col))


def _fullspec(shape):
    nd = len(shape)
    return pl.BlockSpec(shape, lambda i, nd=nd: (0,) * nd)


def _rms(x, width):
    x = x.astype(F32)
    return lax.rsqrt(jnp.sum(x * x, axis=-1, keepdims=True) * (1.0 / width) + EPS)


def _rms_bwd_rows(x, g, dy, width):
    x, dy = x.astype(F32), dy.astype(F32)
    r = _rms(x, width)
    xh = x * r
    dn = dy * g
    dx = r * (dn - xh * (jnp.sum(dn * xh, axis=-1, keepdims=True) * (1.0 / width)))
    return dx, dy * xh


def _acc_rows(ref, val, first):
    s = jnp.sum(val, axis=0, keepdims=True)

    @pl.when(first)
    def _():
        ref[...] = s

    @pl.when(jnp.logical_not(first))
    def _():
        ref[...] += s


def _rms_fwd(x, g, name, rider=None):
    rows, width = x.shape
    tr = _row_tile(rows, 6 * width)

    def body(x_ref, g_ref, o_ref):
        xv = x_ref[...]
        o_ref[...] = (xv * _rms(xv, width) * g_ref[...]).astype(BF16)

    return _call(body, rider, [x, g], name=name, grid=(rows // tr,),
                 in_specs=[_rowspec(tr, width), _fullspec((1, width))], out_specs=_rowspec(tr, width),
                 out_shape=jax.ShapeDtypeStruct((rows, width), BF16), scratch_shapes=[], sem=("parallel",))


def _rms_bwd(x, g, dy, res, name, dx_dtypes=(F32,)):
    rows, width = x.shape
    tr = _row_tile(rows, 18 * width)
    has_res = res is not None
    n_in = 4 if has_res else 3

    def body(*refs):
        x_ref, g_ref, dy_ref = refs[:3]
        dx, dgv = _rms_bwd_rows(x_ref[...], g_ref[...], dy_ref[...], width)
        if has_res:
            dx = dx + refs[3][...]
        for ref, dt in zip(refs[n_in:], dx_dtypes):
            ref[...] = dx.astype(dt)
        _acc_rows(refs[-1], dgv, pl.program_id(0) == 0)

    ins = [x, g, dy] + ([res] if has_res else [])
    specs = [_rowspec(tr, width), _fullspec((1, width)), _rowspec(tr, width)] + ([_rowspec(tr, width)] if has_res else [])
    return pl.pallas_call(
        body, name=name, grid=(rows // tr,), in_specs=specs,
        out_specs=[_rowspec(tr, width)] * len(dx_dtypes) + [_fullspec((1, width))],
        out_shape=[jax.ShapeDtypeStruct((rows, width), dt) for dt in dx_dtypes] + [jax.ShapeDtypeStruct((1, width), F32)],
        compiler_params=_params(("arbitrary",)),
    )(*ins)


_GELU_C = math.sqrt(2.0 / math.pi)


def _gelu(x):
    t = jnp.tanh(_GELU_C * (x + 0.044715 * (x * x * x)))
    return 0.5 * x * (1.0 + t), t


def _gelu_grad(x, t):
    return 0.5 * (1.0 + t) + 0.5 * x * (1.0 - t * t) * (_GELU_C * (1.0 + 3.0 * 0.044715 * (x * x)))


def _gm_forward_rows(zu, zv, gln, bln, wc_ref, bst, n_chunk):
    u, tu = _gelu(zu)
    a, ta = _gelu(zv)
    mu = jnp.mean(a, axis=-1, keepdims=True)
    ac = a - mu
    rs = lax.rsqrt(jnp.mean(ac * ac, axis=-1, keepdims=True) + EPS)
    n = ac * rs
    v = n * gln + bln
    vb = v.astype(BF16)
    rows = []
    for c in range(n_chunk):
        cols = []
        for g in range(GM_GROUPS):
            vc = vb[c * GM_CHUNK:(c + 1) * GM_CHUNK, g * LANES:(g + 1) * LANES]
            mixed = jnp.dot(wc_ref[g], vc, preferred_element_type=F32) + bst[g]
            cols.append(mixed)
        rows.append(jnp.concatenate(cols, axis=1))
    mixed = jnp.concatenate(rows, axis=0) if n_chunk > 1 else rows[0]
    return u, tu, ta, n, rs, v, mixed


def _gm_fwd(z, gln, bln, wc, bst, name):
    rows = z.shape[0]
    tr = _pick(rows, 512, GM_CHUNK)
    n_chunk = tr // GM_CHUNK

    def body(zu_ref, zv_ref, gln_ref, bln_ref, wc_ref, bst_ref, o_ref):
        u, _, _, _, _, _, mixed = _gm_forward_rows(zu_ref[...].astype(F32), zv_ref[...].astype(F32), gln_ref[...], bln_ref[...], wc_ref,
                                                   bst_ref, n_chunk)
        o_ref[...] = (u * mixed).astype(BF16)

    return pl.pallas_call(
        body, name=name, grid=(rows // tr,),
        in_specs=[_rowspec(tr, GM_WIDTH, Z_GM // GM_WIDTH), _rowspec(tr, GM_WIDTH, Z_GM // GM_WIDTH + 1),_fullspec((1, GM_WIDTH)), _fullspec((1, GM_WIDTH)),
                  _fullspec((GM_GROUPS, GM_CHUNK, GM_CHUNK)), _fullspec((GM_GROUPS, GM_CHUNK, LANES))],
        out_specs=_rowspec(tr, GM_WIDTH), out_shape=jax.ShapeDtypeStruct((rows, GM_WIDTH), BF16),
        compiler_params=_params(("parallel",)),
    )(z, z, gln, bln, wc, bst)


ANY_SPEC = pl.BlockSpec(memory_space=pl.ANY)


def _gm_bwd(z, dy, gln, bln, wc, wct, bst, dz, name):
    rows = z.shape[0]
    tr = _pick(rows, 512, GM_CHUNK)
    n_chunk = tr // GM_CHUNK

    def body(zu_ref, zv_ref, dy_ref, gln_ref, bln_ref, wc_ref, wct_ref, bst_ref, _, dz_ref, dws_ref, dbs_ref, dgl_ref,
             dbl_ref):
        first = pl.program_id(0) == 0
        zu, zv, gln = zu_ref[...].astype(F32), zv_ref[...].astype(F32), gln_ref[...]
        u, tu, ta, n, rs, v, mixed = _gm_forward_rows(zu, zv, gln, bln_ref[...], wc_ref, bst_ref, n_chunk)
        dyv = dy_ref[...].astype(F32)
        dzu = dyv * mixed * _gelu_grad(zu, tu)
        dmix = dyv * u
        dmb = dmix.astype(BF16)
        vb = v.astype(BF16)
        dv_rows, dws, dbs = [], [None] * GM_GROUPS, None
        for c in range(n_chunk):
            rsl = slice(c * GM_CHUNK, (c + 1) * GM_CHUNK)
            cols = []
            for g in range(GM_GROUPS):
                csl = slice(g * LANES, (g + 1) * LANES)
                dmc = dmb[rsl, csl]
                cols.append(jnp.dot(wct_ref[g], dmc, preferred_element_type=F32))
                w_part = lax.dot_general(dmc, vb[rsl, csl], (((1,), (1,)), ((), ())), preferred_element_type=F32)
                dws[g] = w_part if dws[g] is None else dws[g] + w_part
            dv_rows.append(jnp.concatenate(cols, axis=1))
            dbs = dmix[rsl, :] if dbs is None else dbs + dmix[rsl, :]
        dv = jnp.concatenate(dv_rows, axis=0) if n_chunk > 1 else dv_rows[0]
        dn = dv * gln
        da = rs * (dn - jnp.mean(dn, axis=-1, keepdims=True) - n * jnp.mean(dn * n, axis=-1, keepdims=True))
        dzv = da * _gelu_grad(zv, ta)
        dz_ref[:, 0:GM_WIDTH] = dzu.astype(BF16)
        dz_ref[:, GM_WIDTH:2 * GM_WIDTH] = dzv.astype(BF16)
        _acc_rows(dgl_ref, dv * n, first)
        _acc_rows(dbl_ref, dv, first)

        @pl.when(first)
        def _():
            for g in range(GM_GROUPS):
                dws_ref[g] = dws[g]
            dbs_ref[...] = dbs

        @pl.when(jnp.logical_not(first))
        def _():
            for g in range(GM_GROUPS):
                dws_ref[g] += dws[g]
            dbs_ref[...] += dbs

    wspec = _fullspec((GM_GROUPS, GM_CHUNK, GM_CHUNK))
    return pl.pallas_call(
        body, name=name, grid=(rows // tr,),
        in_specs=[_rowspec(tr, GM_WIDTH, Z_GM // GM_WIDTH), _rowspec(tr, GM_WIDTH, Z_GM // GM_WIDTH + 1),
                  _rowspec(tr, GM_WIDTH), _fullspec((1, GM_WIDTH)), _fullspec((1, GM_WIDTH)), wspec, wspec, wspec, ANY_SPEC],
        out_specs=[_rowspec(tr, 2 * GM_WIDTH, Z_GM // (2 * GM_WIDTH)), wspec, _fullspec((GM_CHUNK, GM_WIDTH)),
                   _fullspec((1, GM_WIDTH)), _fullspec((1, GM_WIDTH))],
        out_shape=[jax.ShapeDtypeStruct(dz.shape, dz.dtype), jax.ShapeDtypeStruct((GM_GROUPS, GM_CHUNK, GM_CHUNK), F32),
                   jax.ShapeDtypeStruct((GM_CHUNK, GM_WIDTH), F32), jax.ShapeDtypeStruct((1, GM_WIDTH), F32),
                   jax.ShapeDtypeStruct((1, GM_WIDTH), F32)],
        input_output_aliases={8: 0}, compiler_params=_params(("arbitrary",)),
    )(z, z, dy, gln, bln, wc, wct, bst, dz)


def _lat_fwd(z, g_cq, g_ckv, name):
    rows = z.shape[0]
    tr = _row_tile(rows, 4 * MLA_W)

    def body(z_ref, gq_ref, gkv_ref, nq_ref, nkv_ref):
        zb = z_ref[...]
        cq, ckv = zb[:, 0:Q_LORA], zb[:, Q_LORA:Q_LORA + KV_LORA]
        nq_ref[...] = (cq * _rms(cq, Q_LORA) * gq_ref[...]).astype(BF16)
        nkv_ref[...] = (ckv * _rms(ckv, KV_LORA) * gkv_ref[...]).astype(BF16)

    return pl.pallas_call(
        body, name=name, grid=(rows // tr,),
        in_specs=[_rowspec(tr, MLA_W, Z_MLA // MLA_W), _fullspec((1, Q_LORA)), _fullspec((1, KV_LORA))],
        out_specs=[_rowspec(tr, Q_LORA), _rowspec(tr, KV_LORA)],
        out_shape=[jax.ShapeDtypeStruct((rows, Q_LORA), BF16), jax.ShapeDtypeStruct((rows, KV_LORA), BF16)],
        compiler_params=_params(("parallel",)),
    )(z, g_cq, g_ckv)


def _lat_bwd(z, dnq, dnkv, dkpe, g_cq, g_ckv, dz, name):
    rows = z.shape[0]
    tr = _row_tile(rows, 8 * MLA_W)

    def body(z_ref, dnq_ref, dnkv_ref, dkpe_ref, gq_ref, gkv_ref, _, dz_ref, dgq_ref, dgkv_ref):
        first = pl.program_id(0) == 0
        zb = z_ref[...]
        dcq, dgq = _rms_bwd_rows(zb[:, 0:Q_LORA], gq_ref[...], dnq_ref[...], Q_LORA)
        dckv, dgkv = _rms_bwd_rows(zb[:, Q_LORA:Q_LORA + KV_LORA], gkv_ref[...], dnkv_ref[...], KV_LORA)
        dz_ref[:, 0:Q_LORA] = dcq.astype(BF16)
        dz_ref[:, Q_LORA:Q_LORA + KV_LORA] = dckv.astype(BF16)
        dz_ref[:, Q_LORA + KV_LORA:MLA_W] = dkpe_ref[...].astype(BF16)
        _acc_rows(dgq_ref, dgq, first)
        _acc_rows(dgkv_ref, dgkv, first)

    return pl.pallas_call(
        body, name=name, grid=(rows // tr,),
        in_specs=[_rowspec(tr, MLA_W, Z_MLA // MLA_W), _rowspec(tr, Q_LORA), _rowspec(tr, KV_LORA), _rowspec(tr, LANES),
                  _fullspec((1, Q_LORA)), _fullspec((1, KV_LORA)), ANY_SPEC],
        out_specs=[_rowspec(tr, MLA_W, Z_MLA // MLA_W), _fullspec((1, Q_LORA)), _fullspec((1, KV_LORA))],
        out_shape=[jax.ShapeDtypeStruct(dz.shape, dz.dtype), jax.ShapeDtypeStruct((1, Q_LORA), F32),
                   jax.ShapeDtypeStruct((1, KV_LORA), F32)],
        input_output_aliases={6: 0}, compiler_params=_params(("arbitrary",)),
    )(z, dnq, dnkv, dkpe, g_cq, g_ckv, dz)


def _rope(y, cc, ss):
    return y * cc + pltpu.roll(y, 64, 1) * ss


def _rope_bwd(d, cc, ss):
    return d * cc + pltpu.roll(d * ss, 64, 1)


def _qk_fwd(q, kv, z, cc, ss, gqn, gqp, gkn, gkp, name, rider=None):
    rows = q.shape[0]
    W = MLA_HEADS * HEAD
    tr = _row_tile(rows, 20 * W)
    QS = MLA_SCALE * LOG2E

    def body(q_ref, kv_ref, kpe_ref, cc_ref, ss_ref, gqn_ref, gqp_ref, gkn_ref, gkp_ref, qc_ref, kc_ref, v_ref):
        cc, ss = cc_ref[...], ss_ref[...]
        kpe = kpe_ref[...]
        kp = _rope(kpe * _rms(kpe, MLA_ROPE) * gkp_ref[...], cc, ss).astype(BF16)
        for h in range(MLA_HEADS):
            qn = q_ref[:, h * HEAD:(h + 1) * HEAD]
            qp = q_ref[:, W + h * HEAD:W + (h + 1) * HEAD]
            kn = kv_ref[:, h * HEAD:(h + 1) * HEAD]
            qc_ref[:, h * QCAT:h * QCAT + HEAD] = (qn * _rms(qn, HEAD) * gqn_ref[...] * QS).astype(BF16)
            qc_ref[:, h * QCAT + HEAD:(h + 1) * QCAT] = (_rope(qp * _rms(qp, MLA_ROPE) * gqp_ref[...], cc, ss) * QS).astype(BF16)
            kc_ref[:, h * QCAT:h * QCAT + HEAD] = (kn * _rms(kn, HEAD) * gkn_ref[...]).astype(BF16)
            kc_ref[:, h * QCAT + HEAD:(h + 1) * QCAT] = kp
        v_ref[...] = kv_ref[:, W:2 * W].astype(BF16)

    g = _fullspec((1, HEAD))
    return _call(
        body, rider, [q, kv, z, cc, ss, gqn, gqp, gkn, gkp], name=name, grid=(rows // tr,),
        in_specs=[_rowspec(tr, 2 * W), _rowspec(tr, 2 * W), _rowspec(tr, LANES, Z_KPE // LANES), _rowspec(tr, LANES),
                  _rowspec(tr, LANES), g, g, g, g],
        out_specs=[_rowspec(tr, MLA_HEADS * QCAT), _rowspec(tr, MLA_HEADS * QCAT), _rowspec(tr, W)],
        out_shape=[jax.ShapeDtypeStruct((rows, MLA_HEADS * QCAT), BF16), jax.ShapeDtypeStruct((rows, MLA_HEADS * QCAT), BF16),
                   jax.ShapeDtypeStruct((rows, W), BF16)],
        scratch_shapes=[], sem=("parallel",))


def _qk_bwd(q, kv, z, cc, ss, gqn, gqp, gkn, gkp, dqc, dkc, dv, name):
    rows = q.shape[0]
    W = MLA_HEADS * HEAD
    tr = _row_tile(rows, 40 * W)

    def body(q_ref, kv_ref, kpe_ref, cc_ref, ss_ref, gqn_ref, gqp_ref, gkn_ref, gkp_ref, dqc_ref, dkc_ref, dv_ref,
             dq_ref, dkv_ref, dkpe_ref, dgqn_ref, dgqp_ref, dgkn_ref, dgkp_ref):
        first = pl.program_id(0) == 0
        cc, ss = cc_ref[...], ss_ref[...]
        sqn = sqp = skn = dkp = None
        for h in range(MLA_HEADS):
            dx, dg = _rms_bwd_rows(q_ref[:, h * HEAD:(h + 1) * HEAD], gqn_ref[...], dqc_ref[:, h * QCAT:h * QCAT + HEAD], HEAD)
            dq_ref[:, h * HEAD:(h + 1) * HEAD] = dx.astype(BF16)
            sqn = dg if sqn is None else sqn + dg
            dy = _rope_bwd(dqc_ref[:, h * QCAT + HEAD:(h + 1) * QCAT], cc, ss)
            dx, dg = _rms_bwd_rows(q_ref[:, W + h * HEAD:W + (h + 1) * HEAD], gqp_ref[...], dy, MLA_ROPE)
            dq_ref[:, W + h * HEAD:W + (h + 1) * HEAD] = dx.astype(BF16)
            sqp = dg if sqp is None else sqp + dg
            dx, dg = _rms_bwd_rows(kv_ref[:, h * HEAD:(h + 1) * HEAD], gkn_ref[...], dkc_ref[:, h * QCAT:h * QCAT + HEAD], HEAD)
            dkv_ref[:, h * HEAD:(h + 1) * HEAD] = dx.astype(BF16)
            skn = dg if skn is None else skn + dg
            part = dkc_ref[:, h * QCAT + HEAD:(h + 1) * QCAT].astype(F32)
            dkp = part if dkp is None else dkp + part
        dkv_ref[:, W:2 * W] = dv_ref[...].astype(BF16)
        dx, dg = _rms_bwd_rows(kpe_ref[...], gkp_ref[...], _rope_bwd(dkp, cc, ss), MLA_ROPE)
        dkpe_ref[...] = dx
        _acc_rows(dgqn_ref, sqn, first)
        _acc_rows(dgqp_ref, sqp, first)
        _acc_rows(dgkn_ref, skn, first)
        _acc_rows(dgkp_ref, dg, first)

    g = _fullspec((1, HEAD))
    gs = jax.ShapeDtypeStruct((1, HEAD), F32)
    return pl.pallas_call(
        body, name=name, grid=(rows // tr,),
        in_specs=[_rowspec(tr, 2 * W), _rowspec(tr, 2 * W), _rowspec(tr, LANES, Z_KPE // LANES), _rowspec(tr, LANES),
                  _rowspec(tr, LANES), g, g, g, g, _rowspec(tr, MLA_HEADS * QCAT), _rowspec(tr, MLA_HEADS * QCAT),
                  _rowspec(tr, W)],
        out_specs=[_rowspec(tr, 2 * W), _rowspec(tr, 2 * W), _rowspec(tr, LANES), g, g, g, g],
        out_shape=[jax.ShapeDtypeStruct((rows, 2 * W), BF16), jax.ShapeDtypeStruct((rows, 2 * W), BF16),
                   jax.ShapeDtypeStruct((rows, LANES), F32), gs, gs, gs, gs],
        compiler_params=_params(("arbitrary",)),
    )(q, kv, z, cc, ss, gqn, gqp, gkn, gkp, dqc, dkc, dv)


def _headnorm_fwd(x, col, nheads, g, out_scale, name):
    rows = x.shape[0]
    W = nheads * HEAD
    tr = _row_tile(rows, 6 * W)

    def body(x_ref, g_ref, o_ref):
        for h in range(nheads):
            xv = x_ref[:, h * HEAD:(h + 1) * HEAD]
            o_ref[:, h * HEAD:(h + 1) * HEAD] = (xv * _rms(xv, HEAD) * g_ref[...] * out_scale).astype(BF16)

    return pl.pallas_call(
        body, name=name, grid=(rows // tr,),
        in_specs=[_rowspec(tr, W, col), _fullspec((1, HEAD))], out_specs=_rowspec(tr, W),
        out_shape=jax.ShapeDtypeStruct((rows, W), BF16), compiler_params=_params(("parallel",)),
    )(x, g)


def _headnorm_bwd(x, col, nheads, g, dy, tail, name, into=None):
    rows = x.shape[0]
    W = nheads * HEAD
    tr = _row_tile(rows, 12 * W)
    has_tail = tail is not None
    WO = 2 * W if has_tail else W

    def body(*refs):
        if into is not None:
            x_ref, g_ref, dy_ref, _, dx_ref, dg_ref = refs
        elif has_tail:
            x_ref, g_ref, dy_ref, t_ref, dx_ref, dg_ref = refs
        else:
            x_ref, g_ref, dy_ref, dx_ref, dg_ref = refs
        acc = None
        for h in range(nheads):
            sl = slice(h * HEAD, (h + 1) * HEAD)
            dx, dg = _rms_bwd_rows(x_ref[:, sl], g_ref[...], dy_ref[:, sl], HEAD)
            dx_ref[:, sl] = dx.astype(BF16)
            acc = dg if acc is None else acc + dg
        if has_tail:
            dx_ref[:, W:2 * W] = t_ref[...].astype(BF16)
        _acc_rows(dg_ref, acc, pl.program_id(0) == 0)

    ins = [x, g, dy] + ([tail] if has_tail else [])
    specs = [_rowspec(tr, W, col), _fullspec((1, HEAD)), _rowspec(tr, W)] + ([_rowspec(tr, W)] if has_tail else [])
    dx_spec, dx_shape, aliases = _rowspec(tr, WO), jax.ShapeDtypeStruct((rows, WO), BF16), {}
    if into is not None:
        assert not has_tail
        ins, specs = ins + [into[0]], specs + [ANY_SPEC]
        dx_spec, dx_shape, aliases = _rowspec(tr, W, into[1]), jax.ShapeDtypeStruct(into[0].shape, into[0].dtype), {3: 0}
    return pl.pallas_call(
        body, name=name, grid=(rows // tr,), in_specs=specs,
        out_specs=[dx_spec, _fullspec((1, HEAD))], out_shape=[dx_shape, jax.ShapeDtypeStruct((1, HEAD), F32)],
        input_output_aliases=aliases, compiler_params=_params(("arbitrary",)),
    )(*ins)


def _sigmoid(x):
    return 1.0 / (1.0 + jnp.exp(-x.astype(F32)))


def _merge_fwd(z, y_gm, y_mla, y_mem, name):
    rows = z.shape[0]
    tr = _row_tile(rows, 14 * D_MODEL)

    def body(g0_ref, g1_ref, g2_ref, a_ref, b_ref, c_ref, o_ref):
        m = _sigmoid(g0_ref[...]) * a_ref[...] + _sigmoid(g1_ref[...]) * b_ref[...] + _sigmoid(g2_ref[...]) * c_ref[...]
        o_ref[...] = m.astype(BF16)

    r = _rowspec(tr, D_MODEL)
    return pl.pallas_call(
        body, name=name, grid=(rows // tr,),
        in_specs=[_rowspec(tr, D_MODEL, 0), _rowspec(tr, D_MODEL, 1), _rowspec(tr, D_MODEL, 2),r, r, r],
        out_specs=r, out_shape=jax.ShapeDtypeStruct((rows, D_MODEL), BF16), compiler_params=_params(("parallel",)),
    )(z, z, z, y_gm, y_mla, y_mem)


def _merge_bwd(z, y_gm, y_mla, y_mem, dm, name):
    rows = z.shape[0]
    tr = _row_tile(rows, 24 * D_MODEL)

    def body(g0_ref, g1_ref, g2_ref, a_ref, b_ref, c_ref, dm_ref, da_ref, db_ref, dc_ref, dzg_ref):
        dmv = dm_ref[...].astype(F32)
        for k, (g_ref, y_ref, dy_ref) in enumerate(((g0_ref, a_ref, da_ref), (g1_ref, b_ref, db_ref), (g2_ref, c_ref, dc_ref))):
            s = _sigmoid(g_ref[...])
            dy_ref[...] = (dmv * s).astype(BF16)
            dzg_ref[:, k * D_MODEL:(k + 1) * D_MODEL] = (dmv * y_ref[...] * s * (1.0 - s)).astype(BF16)

    r = _rowspec(tr, D_MODEL)
    o = jax.ShapeDtypeStruct((rows, D_MODEL), BF16)
    return pl.pallas_call(
        body, name=name, grid=(rows // tr,),
        in_specs=[_rowspec(tr, D_MODEL, 0), _rowspec(tr, D_MODEL, 1), _rowspec(tr, D_MODEL, 2),r, r, r, r],
        out_specs=[r, r, r, _rowspec(tr, 3 * D_MODEL, 0)],
        out_shape=[o, o, o, jax.ShapeDtypeStruct((rows, Z_COLS), BF16)],
        compiler_params=_params(("parallel",)),
    )(z, z, z, y_gm, y_mla, y_mem, dm)


_NT = (((1,), (1,)), ((), ()))
_TN = (((0,), (0,)), ((), ()))


def _diag_mask(s):
    row = lax.broadcasted_iota(jnp.int32, s.shape, 0)
    col = lax.broadcasted_iota(jnp.int32, s.shape, 1)
    return jnp.where(row >= col, s, NEG)


def _attn_fwd(q, k, v, nb, nheads, dk, v_col0, causal, name, rider=None):
    S, Skv = q.shape[0] // nb, k.shape[0] // nb
    tq = _pick(Skv, ATT_TILE) if causal else _pick(S, 4 * ATT_TILE)
    nq = S // tq

    def body(q_ref, k_ref, v_ref, o_ref, lse_ref):
        for i in range(nq):
            r0 = i * tq
            qb = q_ref[r0:r0 + tq, :]
            if causal:
                spans = ([(0, r0, False)] if i > 0 else []) + [(r0, r0 + tq, True)]
            else:
                spans = [(0, Skv, False)]
            scores = []
            for a, b, masked in spans:
                s = lax.dot_general(qb, k_ref[a:b, :], _NT, preferred_element_type=F32)
                scores.append(_diag_mask(s) if masked else s)
            m = functools.reduce(jnp.maximum, [jnp.max(s, axis=-1, keepdims=True) for s in scores])
            l = acc = None
            for s, (a, b, _) in zip(scores, spans):
                p = jnp.exp2(s - m)
                lp = jnp.sum(p, axis=-1, keepdims=True)
                ap = jnp.dot(p.astype(BF16), v_ref[a:b, :].astype(BF16), preferred_element_type=F32)
                l, acc = (lp, ap) if l is None else (l + lp, acc + ap)
            o_ref[r0:r0 + tq, :] = (acc / l).astype(BF16)
            lse_ref[r0:r0 + tq, :] = m + jnp.log2(l)

    ins = [q, k, v]
    in_specs = [pl.BlockSpec((S, dk), lambda b, h: (b, h)), pl.BlockSpec((Skv, dk), lambda b, h: (b, h)),
                pl.BlockSpec((Skv, HEAD), lambda b, h: (b, v_col0 + h))]
    out_specs = [pl.BlockSpec((S, HEAD), lambda b, h: (b, h)), pl.BlockSpec((None, S, 1), lambda b, h: (h, b, 0))]
    out_shape = [jax.ShapeDtypeStruct((nb * S, nheads * HEAD), BF16), jax.ShapeDtypeStruct((nheads, nb * S, 1), F32)]
    return _call(body, rider, ins, name=name, grid=(nb, nheads), in_specs=in_specs, out_specs=out_specs,
                 out_shape=out_shape, scratch_shapes=[], sem=("parallel", "parallel"))


def _attn_bwd(q, k, v, o, do, lse, nb, nheads, dk, v_col0, scale, causal, name, rider=None):
    S, Skv = q.shape[0] // nb, k.shape[0] // nb
    tk = _pick(Skv, ATT_TILE)
    nkv = Skv // tk

    def body(q_ref, k_ref, v_ref, o_ref, do_ref, lse_ref, dq_ref, dk_ref, dv_ref, delta_ref, dob_ref, dqa_ref):
        dov = do_ref[...]
        delta_ref[...] = jnp.sum(o_ref[...].astype(F32) * dov.astype(F32), axis=-1, keepdims=True)
        dob_ref[...] = dov.astype(BF16)

        for j in range(nkv):
            c0 = j * tk
            kb = k_ref[c0:c0 + tk, :]
            vb = v_ref[c0:c0 + tk, :].astype(BF16)
            if causal:
                spans = [(c0, c0 + tk, True)] + ([(c0 + tk, S, False)] if c0 + tk < S else [])
            else:
                spans = [(0, S, False)]
            dk_acc = dv_acc = None
            for a, b, masked in spans:
                qb = q_ref[a:b, :]
                dob = dob_ref[a:b, :]
                s = lax.dot_general(qb, kb, _NT, preferred_element_type=F32)
                if masked:
                    s = _diag_mask(s)
                p = jnp.exp2(s - lse_ref[a:b, :])
                dp = lax.dot_general(dob, vb, _NT, preferred_element_type=F32)
                ds = (p * (dp - delta_ref[a:b, :])).astype(BF16)
                dv_p = lax.dot_general(p.astype(BF16), dob, _TN, preferred_element_type=F32)
                dk_p = lax.dot_general(ds, qb, _TN, preferred_element_type=F32)
                dk_acc, dv_acc = (dk_p, dv_p) if dk_acc is None else (dk_acc + dk_p, dv_acc + dv_p)
                dq_p = jnp.dot(ds, kb, preferred_element_type=F32) * scale
                if j == 0:
                    dqa_ref[a:b, :] = dq_p
                else:
                    dqa_ref[a:b, :] += dq_p
            dk_ref[c0:c0 + tk, :] = (dk_acc * LN2).astype(BF16)
            dv_ref[c0:c0 + tk, :] = dv_acc.astype(BF16)
        dq_ref[...] = dqa_ref[...].astype(BF16)

    ins = [q, k, v, o, do, lse]
    in_specs = [pl.BlockSpec((S, dk), lambda b, h: (b, h)), pl.BlockSpec((Skv, dk), lambda b, h: (b, h)),
                pl.BlockSpec((Skv, HEAD), lambda b, h: (b, v_col0 + h)), pl.BlockSpec((S, HEAD), lambda b, h: (b, h)),
                pl.BlockSpec((S, HEAD), lambda b, h: (b, h)), pl.BlockSpec((None, S, 1), lambda b, h: (h, b, 0))]
    out_specs = [pl.BlockSpec((S, dk), lambda b, h: (b, h)), pl.BlockSpec((Skv, dk), lambda b, h: (b, h)),
                 pl.BlockSpec((Skv, HEAD), lambda b, h: (b, h))]
    out_shape = [jax.ShapeDtypeStruct((nb * S, nheads * dk), BF16), jax.ShapeDtypeStruct((nb * Skv, nheads * dk), BF16),
                 jax.ShapeDtypeStruct((nb * Skv, nheads * HEAD), BF16)]
    return _call(body, rider, ins, name=name, grid=(nb, nheads), in_specs=in_specs, out_specs=out_specs,
                 out_shape=out_shape,
                 scratch_shapes=[pltpu.VMEM((S, 1), F32), pltpu.VMEM((S, HEAD), BF16), pltpu.VMEM((S, dk), F32)],
                 sem=("parallel", "parallel"))


def _spread_rope(a):
    zero = jnp.zeros(a.shape[:-1] + (32,), a.dtype)
    return jnp.concatenate([a[..., :32], zero, a[..., 32:], zero], axis=-1)


def _gather_rope(a):
    return jnp.concatenate([a[..., 0:32], a[..., 64:96]], axis=-1)


def _win_layout(w):
    return jnp.concatenate([w[:, C_ZG:C_END], w[:, C_ZU:C_CQ], w[:, C_QM:C_ZG], w[:, C_CQ:C_CKV], w[:, C_CKV:C_KPE],
                            _spread_rope(w[:, C_KPE:C_QM])], axis=1)


def _win_unlayout(d):
    return jnp.concatenate([d[:, Z_GM:Z_QM], d[:, Z_MLA:Z_MLA + Q_LORA], d[:, Z_MLA + Q_LORA:Z_KPE],
                            _gather_rope(d[:, Z_KPE:Z_COLS]), d[:, Z_QM:Z_MLA], d[:, 0:Z_GM]], axis=1)


def _wuq_layout(w):
    r = w.reshape(Q_LORA, MLA_HEADS, HEAD + MLA_ROPE)
    return jnp.concatenate([r[:, :, :HEAD].reshape(Q_LORA, -1), _spread_rope(r[:, :, HEAD:]).reshape(Q_LORA, -1)], axis=1)


def _wuq_unlayout(d):
    n = d[:, :MLA_HEADS * HEAD].reshape(Q_LORA, MLA_HEADS, HEAD)
    p = _gather_rope(d[:, MLA_HEADS * HEAD:].reshape(Q_LORA, MLA_HEADS, HEAD))
    return jnp.concatenate([n, p], axis=-1).reshape(Q_LORA, -1)


def _wukv_layout(w):
    r = w.reshape(KV_LORA, MLA_HEADS, 2 * HEAD)
    return jnp.concatenate([r[:, :, :HEAD].reshape(KV_LORA, -1), r[:, :, HEAD:].reshape(KV_LORA, -1)], axis=1)


def _wukv_unlayout(d):
    k = d[:, :MLA_HEADS * HEAD].reshape(KV_LORA, MLA_HEADS, HEAD)
    v = d[:, MLA_HEADS * HEAD:].reshape(KV_LORA, MLA_HEADS, HEAD)
    return jnp.concatenate([k, v], axis=-1).reshape(KV_LORA, -1)


AG_EARLY = ['w_uq', 'w_ukv', 'w_o_gm']
AG_MID = ['w_o_mla', 'w_mem_kv', 'w_o_mem', 'w_out']
RS_GROUPS = {'ff2': ['w_ff2'], 'proj': ['w_ff1', 'w_out', 'w_o_gm', 'w_o_mla', 'w_o_mem'],
             'lat': ['w_uq', 'w_ukv', 'w_mem_kv'], 'in_top': ['w_in'], 'in_bot': ['w_in']}


def _unride(res, rider):
    return (res, None) if rider is None else res


def _local_step(x, mem, positions, target, P, ws):
    B, S, _ = x.shape
    M = mem.shape[1]
    T = B * S
    x2d = x.reshape(T, D_MODEL)
    mem2d = mem.reshape(B * M, D_MODEL)
    tgt2d = target.reshape(T, D_MODEL)

    def row(v):
        return v.reshape(1, -1).astype(F32)

    inv_freq = ROPE_BASE ** (-jnp.arange(0, MLA_ROPE, 2, dtype=F32) / MLA_ROPE)
    ang = positions.reshape(T).astype(F32)[:, None] * inv_freq
    cos, sin, zero = jnp.cos(ang), jnp.sin(ang), jnp.zeros_like(ang)
    cc = jnp.concatenate([cos, zero, cos, zero], axis=1)
    ss = jnp.concatenate([-sin, zero, sin, zero], axis=1)

    g_mix, g_cq, g_ckv, g_ffn, g_mem = row(P['g_mix']), row(P['g_cq']), row(P['g_ckv']), row(P['g_ffn']), row(P['g_mem'])
    gqn, gkn, gmq, gmk = row(P['g_q_nope']), row(P['g_k_nope']), row(P['g_mq']), row(P['g_mk'])
    gqp, gkp = _spread_rope(row(P['g_q_pe'])), _spread_rope(row(P['g_k_pe']))
    gln, bln = row(P['g_gm_ln']), row(P['b_gm_ln'])
    wc = jnp.tril(P['w_spatial'].astype(F32))
    wct = jnp.swapaxes(wc, 1, 2).astype(BF16)
    wc = wc.astype(BF16)
    bst = jnp.broadcast_to(P['b_spatial'].astype(F32)[:, :, None], (GM_GROUPS, GM_CHUNK, LANES))

    ride = ws.gather(['w_in'])
    h, got = _unride(_rms_fwd(x2d, g_mix, "rms_mix", rider=ride), ride)
    w_in = _win_layout(ws.gathered(['w_in'], got)['w_in']).astype(BF16)
    ride = ws.gather(AG_EARLY)
    z, got = _unride(_matmul(h, w_in, 'nn', ACT, "mm_in", tn_t=768, rider=ride), ride)
    early = ws.gathered(AG_EARLY, got)
    w_uq, w_ukv, w_o_gm = _wuq_layout(early['w_uq']).astype(BF16), _wukv_layout(early['w_ukv']).astype(BF16), early['w_o_gm']
    ygm_pre = _gm_fwd(z, gln, bln, wc, bst, "gm_fwd")
    y_gm = _matmul(ygm_pre, w_o_gm, 'nn', ACT, "mm_o_gm")
    nq, nkv = _lat_fwd(z, g_cq, g_ckv, "lat_fwd")
    q = _matmul(nq, w_uq, 'nn', ACT, "mm_uq")
    kv = _matmul(nkv, w_ukv, 'nn', ACT, "mm_ukv")
    ride = ws.gather(AG_MID)
    (qcat, kcat, vv), got = _unride(_qk_fwd(q, kv, z, cc, ss, gqn, gqp, gkn, gkp, "qk_fwd", rider=ride), ride)
    mid = ws.gathered(AG_MID, got)
    w_o_mla, w_mem_kv, w_o_mem, w_out = (mid[n] for n in AG_MID)
    ride = ws.gather(['w_ff1'])
    (o, lse), got = _unride(_attn_fwd(qcat, kcat, vv, B, MLA_HEADS, QCAT, 0, True, "mla_attn_fwd", rider=ride), ride)
    w_ff1 = ws.gathered(['w_ff1'], got)['w_ff1']
    y_mla = _matmul(o, w_o_mla, 'nn', ACT, "mm_o_mla")
    nm = _rms_fwd(mem2d, g_mem, "rms_mem")
    kvm = _matmul(nm, w_mem_kv, 'nn', ACT, "mm_mem_kv")
    qm = _headnorm_fwd(z, Z_QM // (MEM_HEADS * HEAD), MEM_HEADS, gmq, MEM_SCALE * LOG2E, "memq_fwd")
    km = _headnorm_fwd(kvm, 0, MEM_HEADS, gmk, 1.0, "memk_fwd")
    om, lse_m = _attn_fwd(qm, km, kvm, B, MEM_HEADS, HEAD, MEM_HEADS, False, "mem_attn_fwd")
    y_mem = _matmul(om, w_o_mem, 'nn', ACT, "mm_o_mem")
    merged = _merge_fwd(z, y_gm, y_mla, y_mem, "merge_fwd")
    x1 = _matmul(merged, w_out, 'nn', F32, "mm_out", add=x2d)
    h2 = _rms_fwd(x1, g_ffn, "rms_ffn")
    ride = ws.gather(['w_ff2'])
    a1, got = _unride(_matmul(h2, w_ff1, 'nn', BF16, "mm_ff1", rider=ride), ride)
    w_ff2 = ws.gathered(['w_ff2'], got)['w_ff2']
    dx2, dx2b, loss_part = _matmul(a1, w_ff2, 'nn', F32, "mm_ff2", add=x1, relu2_a=True, sq_err_target=tgt2d)

    G = {}
    ride = ws.scatter('ff2', {'w_ff2': _matmul(a1, dx2b, 'tn', BF16, "mm_d_ff2", relu2_a=True)})
    da1, got = _unride(_matmul(dx2b, w_ff2, 'nt', BF16, "mm_da1", relu2_grad=a1, rider=ride), ride)
    ws.scattered('ff2', got)
    d_ff1 = _matmul(h2, da1, 'tn', BF16, "mm_d_ff1", col_shards=N_DEV)
    dh2 = _matmul(da1, w_ff1, 'nt', ACT, "mm_dh2")
    dx1, dx1b, G['g_ffn'] = _rms_bwd(x1, g_ffn, dh2, dx2, "rms_ffn_bwd", dx_dtypes=(F32, BF16))
    d_out = _matmul(merged, dx1b, 'tn', BF16, "mm_d_out")
    dmerged = _matmul(dx1b, w_out, 'nt', ACT, "mm_dmerged")
    dy_gm, dy_mla, dy_mem, dz = _merge_bwd(z, y_gm, y_mla, y_mem, dmerged, "merge_bwd")
    d_o_gm = _matmul(ygm_pre, dy_gm, 'tn', BF16, "mm_d_o_gm")
    d_o_mla = _matmul(o, dy_mla, 'tn', BF16, "mm_d_o_mla")
    d_o_mem = _matmul(om, dy_mem, 'tn', BF16, "mm_d_o_mem")
    dygm_pre = _matmul(dy_gm, w_o_gm, 'nt', ACT, "mm_dygm")
    dz, dws, dbs, G['g_gm_ln'], G['b_gm_ln'] = _gm_bwd(z, dygm_pre, gln, bln, wc, wct, bst, dz, "gm_bwd")
    G['w_spatial'] = jnp.tril(dws)
    G['b_spatial'] = jnp.sum(dbs.reshape(GM_CHUNK, GM_GROUPS, LANES), axis=-1).T
    do = _matmul(dy_mla, w_o_mla, 'nt', ACT, "mm_do")
    ride = ws.scatter('proj', {'w_ff1': d_ff1, 'w_out': d_out, 'w_o_gm': d_o_gm, 'w_o_mla': d_o_mla, 'w_o_mem': d_o_mem})
    (dqc, dkc, dvv), got = _unride(_attn_bwd(qcat, kcat, vv, o, do, lse, B, MLA_HEADS, QCAT, 0, MLA_SCALE, True,
                                             "mla_attn_bwd", rider=ride), ride)
    ws.scattered('proj', got)
    dq, dkv, dkpe, G['g_q_nope'], dgqp, G['g_k_nope'], dgkp = _qk_bwd(q, kv, z, cc, ss, gqn, gqp, gkn, gkp, dqc, dkc, dvv,
                                                                     "qk_bwd")
    G['g_q_pe'], G['g_k_pe'] = _gather_rope(dgqp), _gather_rope(dgkp)
    d_uq = _wuq_unlayout(_matmul(nq, dq, 'tn', BF16, "mm_d_uq"))
    dnq = _matmul(dq, w_uq, 'nt', ACT, "mm_dnq")
    d_ukv = _wukv_unlayout(_matmul(nkv, dkv, 'tn', BF16, "mm_d_ukv"))
    dnkv = _matmul(dkv, w_ukv, 'nt', ACT, "mm_dnkv")
    dz, G['g_cq'], G['g_ckv'] = _lat_bwd(z, dnq, dnkv, dkpe, g_cq, g_ckv, dz, "lat_bwd")
    dom = _matmul(dy_mem, w_o_mem, 'nt', ACT, "mm_dom")
    dqm, dkm, dvm = _attn_bwd(qm, km, kvm, om, dom, lse_m, B, MEM_HEADS, HEAD, MEM_HEADS, MEM_SCALE, False, "mem_attn_bwd")
    dz, G['g_mq'] = _headnorm_bwd(z, Z_QM // (MEM_HEADS * HEAD), MEM_HEADS, gmq, dqm, None, "memq_bwd",
                                  into=(dz, Z_QM // (MEM_HEADS * HEAD)))
    dkvm, G['g_mk'] = _headnorm_bwd(kvm, 0, MEM_HEADS, gmk, dkm, dvm, "memk_bwd")
    d_mem_kv = _matmul(nm, dkvm, 'tn', BF16, "mm_d_mem_kv")
    dnm = _matmul(dkvm, w_mem_kv, 'nt', ACT, "mm_dnm")
    G['g_mem'], = _rms_bwd(mem2d, g_mem, dnm, None, "rms_mem_bwd", dx_dtypes=())
    half = D_MODEL // 2
    ride = ws.scatter('lat', {'w_uq': d_uq, 'w_ukv': d_ukv, 'w_mem_kv': d_mem_kv})
    d_top, got = _unride(_matmul(h, dz, 'tn', BF16, "mm_d_in_top", tn_t=768, m_rows=(0, half), rider=ride), ride)
    ws.scattered('lat', got)
    ride = ws.scatter('in_top', {'w_in': _win_unlayout(d_top)})
    d_bot, got = _unride(_matmul(h, dz, 'tn', BF16, "mm_d_in_bot", tn_t=768, m_rows=(half, half), rider=ride), ride)
    ws.scattered('in_top', got)
    ride = ws.scatter('in_bot', {'w_in': _win_unlayout(d_bot)})
    dh, got = _unride(_matmul(dz, w_in, 'nt', ACT, "mm_dh", rider=ride), ride)
    ws.scattered('in_bot', got)
    gx, G['g_mix'] = _rms_bwd(x2d, g_mix, dh, dx1, "rms_mix_bwd")
    return loss_part, gx.reshape(B, S, D_MODEL), G


def _all_gather8(xs, name):
    def body(x_ref, out_ref, send_sems, recv_sems, local_sem):
        x, y, c = lax.axis_index("x"), lax.axis_index("y"), lax.axis_index("c")
        me, sibling = (x, y, c), (x, y, 1 - c)
        chips = [(1 - x, y), (x, 1 - y), (1 - x, 1 - y)]

        def rows(px, py, pc):
            return out_ref.at[4 * px + 2 * py + pc]

        def copy(k, block, to, src=None):
            return pltpu.make_async_remote_copy(
                src_ref=rows(*block) if src is None else src, dst_ref=rows(*block),
                send_sem=send_sems.at[k], recv_sem=recv_sems.at[k], device_id=to, device_id_type=MESH)

        mine = pltpu.make_async_copy(x_ref, rows(*me), local_sem)
        mine.start()
        first = [copy(0, me, sibling, src=x_ref)]
        first += [copy(1 + j, me, (*chip, c), src=x_ref) for j, chip in enumerate(chips)]
        for cp in first:
            cp.start()
        passed = [copy(4 + j, (*chip, c), sibling) for j, chip in enumerate(chips)]
        for j, chip in enumerate(chips):
            copy(1 + j, (*chip, c), me).wait_recv()
            passed[j].start()
        copy(0, sibling, me).wait_recv()
        for j, chip in enumerate(chips):
            copy(4 + j, (*chip, 1 - c), me).wait_recv()
        for cp in first + passed:
            cp.wait_send()
        mine.wait()

    return pl.pallas_call(
        body, name=name, in_specs=[HBM_SPEC], out_specs=HBM_SPEC,
        out_shape=jax.ShapeDtypeStruct((N_DEV,) + xs.shape, xs.dtype),
        scratch_shapes=[pltpu.SemaphoreType.DMA((7,)), pltpu.SemaphoreType.DMA((7,)), pltpu.SemaphoreType.DMA],
    )(xs)


def _adamw_rows(w, g, m, v):
    m2 = ADAM_B1 * m + (1.0 - ADAM_B1) * g
    v2 = ADAM_B2 * v + (1.0 - ADAM_B2) * (g * g)
    m_hat = m2 / (1.0 - ADAM_B1 ** ADAM_STEP)
    v_hat = v2 / (1.0 - ADAM_B2 ** ADAM_STEP)
    delta = -ADAM_LR * (m_hat / (jnp.sqrt(v_hat) + ADAM_EPS) + ADAM_WD * w)
    return delta, m2, v2


def _sum_adamw(parts, w, m, v, name):
    rows, cols = w.shape
    assert sum(p.shape[1] for p in parts) == rows
    tr = _pick(min(p.shape[1] for p in parts), max(16, 65536 // cols), 16)
    n = parts[0].shape[0]
    counts = [p.shape[1] // tr for p in parts]
    starts = [sum(counts[:k]) for k in range(len(parts))]

    def body(*refs):
        p_refs = refs[:len(parts)]
        w_ref, m_ref, v_ref, g_ref, d_ref, m2_ref, v2_ref = refs[len(parts):]
        g = None
        for p_ref, start in zip(p_refs, starts):
            gk = p_ref[0].astype(F32)
            for k in range(1, n):
                gk = gk + p_ref[k].astype(F32)
            g = gk if g is None else jnp.where(pl.program_id(0) >= start, gk, g)
        delta, m2, v2 = _adamw_rows(w_ref[...], g, m_ref[...], v_ref[...])
        g_ref[...] = g
        d_ref[...] = delta
        m2_ref[...] = m2
        v2_ref[...] = v2

    flat = pl.BlockSpec((tr, cols), lambda i: (i, 0))
    out = jax.ShapeDtypeStruct((rows, cols), F32)
    p_specs = [pl.BlockSpec((n, tr, cols), lambda i, s=s, c=c: (0, jnp.clip(i - s, 0, c - 1), 0))
               for s, c in zip(starts, counts)]
    return pl.pallas_call(
        body, name=name, grid=(rows // tr,), in_specs=p_specs + [flat, flat, flat], out_specs=[flat] * 4,
        out_shape=[out] * 4, compiler_params=_params(("parallel",)),
    )(*parts, w, m, v)


def _small_rows(name):
    n = {'g_mix': 1024, 'g_cq': 384, 'g_ckv': 256, 'g_q_nope': 128, 'g_q_pe': 64, 'g_k_nope': 128, 'g_k_pe': 64,
         'g_gm_ln': 512, 'b_gm_ln': 512, 'w_spatial': GM_GROUPS * GM_CHUNK * GM_CHUNK, 'b_spatial': GM_GROUPS * GM_CHUNK,
         'g_mem': 1024, 'g_mq': 128, 'g_mk': 128, 'g_ffn': 1024}[name]
    return n, -(-n // (8 * LANES)) * 8


def _small_slab(d):
    parts = []
    for name in SMALL:
        n, rows = _small_rows(name)
        parts.append(jnp.pad(d[name].reshape(-1).astype(F32), (0, rows * LANES - n)).reshape(rows, LANES))
    return jnp.concatenate(parts, axis=0)


def _small_unslab(slab, like):
    out, r = {}, 0
    for name in SMALL:
        n, rows = _small_rows(name)
        out[name] = slab[r:r + rows].reshape(-1)[:n].reshape(like[name].shape)
        r += rows
    return out


def _full_from_gathered(gathered, name):
    r, c = BIG_SHAPE[name]
    if BIG_AXIS[name] == 0:
        return gathered.reshape(r, c)
    return gathered.transpose(1, 0, 2).reshape(r, c)


def _shards_of_full(g, name):
    if g.ndim == 3:
        return g
    r, c = BIG_SHAPE[name]
    if BIG_AXIS[name] == 0:
        return g.reshape(N_DEV, r // N_DEV, c)
    return g.reshape(g.shape[0], N_DEV, c // N_DEV).transpose(1, 0, 2)


class _DistWeights:
    def __init__(self, shards):
        self.shards = shards
        self.received = {}

    def gather(self, names):
        return _Gather2([self.shards[n].astype(BF16) for n in names])

    def gathered(self, names, got):
        return {n: _full_from_gathered(g, n) for n, g in zip(names, got)}

    def scatter(self, key, grads):
        return _Exchange([_shards_of_full(grads[n], n) for n in RS_GROUPS[key]], scatter=True)

    def scattered(self, key, got):
        for n, g in zip(RS_GROUPS[key], got):
            self.received.setdefault(n, []).append(g)


def kernel(x, mem, positions, g_mix, w_in, g_cq, w_uq, g_ckv, w_ukv, g_q_nope, g_q_pe, g_k_nope, g_k_pe, g_gm_ln, b_gm_ln, w_spatial, b_spatial, g_mem, w_mem_kv, g_mq, g_mk, w_o_gm, w_o_mla, w_o_mem, w_out, g_ffn, w_ff1, w_ff2, loss_target, m_g_mix, m_w_in, m_g_cq, m_w_uq, m_g_ckv, m_w_ukv, m_g_q_nope, m_g_q_pe, m_g_k_nope, m_g_k_pe, m_g_gm_ln, m_b_gm_ln, m_w_spatial, m_b_spatial, m_g_mem, m_w_mem_kv, m_g_mq, m_g_mk, m_w_o_gm, m_w_o_mla, m_w_o_mem, m_w_out, m_g_ffn, m_w_ff1, m_w_ff2, v_g_mix, v_w_in, v_g_cq, v_w_uq, v_g_ckv, v_w_ukv, v_g_q_nope, v_g_q_pe, v_g_k_nope, v_g_k_pe, v_g_gm_ln, v_b_gm_ln, v_w_spatial, v_b_spatial, v_g_mem, v_w_mem_kv, v_g_mq, v_g_mk, v_w_o_gm, v_w_o_mla, v_w_o_mem, v_w_out, v_g_ffn, v_w_ff1, v_w_ff2):
    given = dict(locals())
    w = {n: given[n][0] for n in WEIGHTS}
    mom = {n: given['m_' + n][0] for n in WEIGHTS}
    var = {n: given['v_' + n][0] for n in WEIGHTS}

    ws = _DistWeights({n: w[n] for n in BIG})
    loss_part, grad_x, G = _local_step(x, mem, positions, loss_target, {n: w[n] for n in SMALL}, ws)

    outs = {}
    for n in BIG:
        for prefix, res in zip(("grad_", "delta_", "new_m_", "new_v_"),
                               _sum_adamw(ws.received[n], w[n], mom[n], var[n], "adamw_" + n)):
            outs[prefix + n] = res[None]

    tail = jnp.zeros((8, LANES), F32)
    parts = _all_gather8(jnp.concatenate([_small_slab(G), jnp.pad(loss_part, ((0, 7), (0, 0)))], axis=0), "ag_small")
    small = _sum_adamw([parts], *[jnp.concatenate([_small_slab(d), tail], axis=0) for d in (w, mom, var)], "adamw_small")
    loss = 0.5 * jnp.sum(small[0][-8:]) / D_MODEL
    for prefix, small_slab in zip(("grad_", "delta_", "new_m_", "new_v_"), small):
        sm = _small_unslab(small_slab, given)
        for n in SMALL:
            outs[prefix + n] = sm[n]
    return (loss, grad_x, *[outs[p + n] for p in ("grad_", "delta_", "new_m_", "new_v_") for n in WEIGHTS])
```

```python
import functools
import math

import jax
import jax.numpy as jnp
from jax import lax
from jax.experimental import pallas as pl
from jax.experimental.pallas import tpu as pltpu

F32 = jnp.float32
BF16 = jnp.bfloat16
ACT = BF16

D_MODEL = 1024
MEM_HEADS = 4
HEAD = 128
GM_WIDTH = 512
GM_CHUNK = 128
GM_GROUPS = 4
MLA_HEADS = 8
MLA_ROPE = 64
Q_LORA = 384
KV_LORA = 256
D_FF = 4096
EPS = 1e-6
ROPE_BASE = 10000.0
MLA_SCALE = 1.0 / math.sqrt(HEAD + MLA_ROPE)
MEM_SCALE = 1.0 / math.sqrt(HEAD)
LOG2E = 1.4426950408889634
LN2 = 0.6931471805599453
ATT_TILE = 256
C_ZU, C_ZV, C_CQ, C_CKV, C_KPE, C_QM, C_ZG, C_END = 0, 512, 1024, 1408, 1664, 1728, 2240, 5312
Z_GM, Z_QM, Z_MLA, Z_KPE, Z_COLS = 3072, 4096, 4608, 5248, 5376
MLA_W = 768
QCAT = 2 * HEAD
ADAM_LR, ADAM_B1, ADAM_B2, ADAM_EPS, ADAM_WD, ADAM_STEP = 0.001, 0.9, 0.999, 1e-08, 0.01, 10
N_DEV = 8
LANES = 128
VMEM_LIMIT = 48 * 1024 * 1024
MAX_K_TILE = 8192
NEG = -1e30

BIG = ['w_in', 'w_uq', 'w_ukv', 'w_mem_kv', 'w_o_gm', 'w_o_mla', 'w_o_mem', 'w_out', 'w_ff1', 'w_ff2']
BIG_AXIS = {'w_in': 1, 'w_uq': 1, 'w_ukv': 1, 'w_mem_kv': 0, 'w_o_gm': 1, 'w_o_mla': 0, 'w_o_mem': 1,
            'w_out': 0, 'w_ff1': 1, 'w_ff2': 0}
BIG_SHAPE = {'w_in': (1024, 5312), 'w_uq': (384, 1536), 'w_ukv': (256, 2048), 'w_mem_kv': (1024, 1024),
             'w_o_gm': (512, 1024), 'w_o_mla': (1024, 1024), 'w_o_mem': (512, 1024), 'w_out': (1024, 1024),
             'w_ff1': (1024, 4096), 'w_ff2': (4096, 1024)}
SMALL = ['g_mix', 'g_cq', 'g_ckv', 'g_q_nope', 'g_q_pe', 'g_k_nope', 'g_k_pe', 'g_gm_ln', 'b_gm_ln',
         'w_spatial', 'b_spatial', 'g_mem', 'g_mq', 'g_mk', 'g_ffn']
WEIGHTS = ['g_mix', 'w_in', 'g_cq', 'w_uq', 'g_ckv', 'w_ukv', 'g_q_nope', 'g_q_pe', 'g_k_nope', 'g_k_pe',
           'g_gm_ln', 'b_gm_ln', 'w_spatial', 'b_spatial', 'g_mem', 'w_mem_kv', 'g_mq', 'g_mk', 'w_o_gm',
           'w_o_mla', 'w_o_mem', 'w_out', 'g_ffn', 'w_ff1', 'w_ff2']


def _pick(n, target, mult=LANES):
    best = None
    t = mult
    while t <= min(n, target):
        if n % t == 0:
            best = t
        t += mult
    return best if best is not None else n


def _params(sem):
    return pltpu.CompilerParams(dimension_semantics=sem, vmem_limit_bytes=VMEM_LIMIT)


MESH = pl.DeviceIdType.MESH
HBM_SPEC = pl.BlockSpec(memory_space=pltpu.HBM)


class _Exchange:
    def __init__(self, srcs, scatter):
        self.srcs, self.scatter = list(srcs), scatter
        self.out_shapes = [jax.ShapeDtypeStruct(s.shape if scatter else (N_DEV,) + s.shape, s.dtype) for s in self.srcs]
        n = len(self.srcs)
        self.scratch = [pltpu.SemaphoreType.DMA((n, N_DEV - 1)), pltpu.SemaphoreType.DMA((n, N_DEV - 1)),
                        pltpu.SemaphoreType.DMA((n,))]

    def _copies(self, src_refs, dst_refs, send_sems, recv_sems, local_sems):
        x, y, c = lax.axis_index("x"), lax.axis_index("y"), lax.axis_index("c")
        me = 4 * x + 2 * y + c
        local, remote = [], []
        for a, (src_ref, dst_ref) in enumerate(zip(src_refs, dst_refs)):
            def mine_for(dev, src_ref=src_ref):
                return src_ref.at[dev] if self.scatter else src_ref

            local.append(pltpu.make_async_copy(mine_for(me), dst_ref.at[me], local_sems.at[a]))
            for k in range(1, N_DEV):
                px = 1 - x if k & 4 else x
                py = 1 - y if k & 2 else y
                pc = 1 - c if k & 1 else c
                remote.append(pltpu.make_async_remote_copy(
                    src_ref=mine_for(4 * px + 2 * py + pc), dst_ref=dst_ref.at[me], send_sem=send_sems.at[a, k - 1],
                    recv_sem=recv_sems.at[a, k - 1], device_id=(px, py, pc), device_id_type=MESH))
        return local, remote

    def start(self, *refs):
        local, remote = self._copies(*refs)
        for cp in local + remote:
            cp.start()

    def wait(self, *refs):
        local, remote = self._copies(*refs)
        for cp in remote + local:
            cp.wait()


class _Gather2:
    def __init__(self, srcs):
        self.srcs = list(srcs)
        self.out_shapes = [jax.ShapeDtypeStruct((N_DEV,) + s.shape, s.dtype) for s in self.srcs]
        n = len(self.srcs)
        self.scratch = [pltpu.SemaphoreType.DMA((n, N_DEV - 1)), pltpu.SemaphoreType.DMA((n, N_DEV - 1)),
                        pltpu.SemaphoreType.DMA((n,))]

    def _plan(self, src_refs, dst_refs, send_sems, recv_sems, local_sems):
        x, y, c = lax.axis_index("x"), lax.axis_index("y"), lax.axis_index("c")
        chips = [(1 - x, y), (x, 1 - y), (1 - x, 1 - y)]
        plans = []
        for a, (src_ref, dst_ref) in enumerate(zip(src_refs, dst_refs)):
            def copy(k, block, to, src=None, a=a, dst_ref=dst_ref):
                at = dst_ref.at[4 * block[0] + 2 * block[1] + block[2]]
                return pltpu.make_async_remote_copy(src_ref=at if src is None else src, dst_ref=at,
                                                    send_sem=send_sems.at[a, k], recv_sem=recv_sems.at[a, k],
                                                    device_id=to, device_id_type=MESH)

            local = pltpu.make_async_copy(src_ref, dst_ref.at[4 * x + 2 * y + c], local_sems.at[a])
            first = [copy(0, (x, y, c), (x, y, 1 - c), src=src_ref)]
            first += [copy(1 + j, (x, y, c), (*chip, c), src=src_ref) for j, chip in enumerate(chips)]
            passed = [copy(4 + j, (*chip, c), (x, y, 1 - c)) for j, chip in enumerate(chips)]
            arrivals = [copy(1 + j, (*chip, c), (x, y, c)) for j, chip in enumerate(chips)]
            late = [copy(0, (x, y, 1 - c), (x, y, c))] + [copy(4 + j, (*chip, 1 - c), (x, y, c)) for j, chip in enumerate(chips)]
            plans.append((local, first, passed, arrivals, late))
        return plans

    def start(self, *refs):
        for local, first, _, _, _ in self._plan(*refs):
            local.start()
            for cp in first:
                cp.start()

    def wait(self, *refs):
        plans = self._plan(*refs)
        for _, _, passed, arrivals, _ in plans:
            for arrived, onward in zip(arrivals, passed):
                arrived.wait_recv()
                onward.start()
        for local, first, passed, _, late in plans:
            for cp in late:
                cp.wait_recv()
            for cp in first + passed:
                cp.wait_send()
            local.wait()


def _call(body, rider, ins, *, name, grid, in_specs, out_specs, out_shape, scratch_shapes, sem):
    if rider is None:
        return pl.pallas_call(body, name=name, grid=grid, in_specs=in_specs, out_specs=out_specs, out_shape=out_shape,
                              scratch_shapes=scratch_shapes, compiler_params=_params(sem))(*ins)
    single = not isinstance(out_shape, (list, tuple))
    own_specs, own_shapes = ([out_specs], [out_shape]) if single else (list(out_specs), list(out_shape))
    n_in, n_out, n_sc, n_r = len(ins), len(own_shapes), len(scratch_shapes), len(rider.srcs)
    n_all_in = n_in + n_r

    def carrying(*refs):
        own_in, srcs = refs[:n_in], refs[n_in:n_in + n_r]
        own_out, dsts = refs[n_all_in:n_all_in + n_out], refs[n_all_in + n_out:n_all_in + n_out + n_r]
        own_sc = refs[n_all_in + n_out + n_r:n_all_in + n_out + n_r + n_sc]
        sems = refs[n_all_in + n_out + n_r + n_sc:]
        first = last = None
        for d, steps in enumerate(grid):
            f, l = pl.program_id(d) == 0, pl.program_id(d) == steps - 1
            first, last = (f, l) if first is None else (first & f, last & l)

        @pl.when(first)
        def _():
            rider.start(srcs, dsts, *sems)

        body(*own_in, *own_out, *own_sc)

        @pl.when(last)
        def _():
            rider.wait(srcs, dsts, *sems)

    res = pl.pallas_call(
        carrying, name=name, grid=grid, in_specs=list(in_specs) + [HBM_SPEC] * n_r,
        out_specs=own_specs + [HBM_SPEC] * n_r, out_shape=own_shapes + rider.out_shapes,
        scratch_shapes=list(scratch_shapes) + rider.scratch, compiler_params=_params(("arbitrary",) * len(grid)),
    )(*ins, *rider.srcs)
    own = res[:n_out]
    return (own[0] if single else list(own)), list(res[n_out:])


def _matmul(a, b, mode, out_dtype, name, add=None, relu2_a=False, relu2_grad=None,
            tm_t=None, tn_t=None, tk_t=None, rider=None, m_rows=None, col_shards=None, sq_err_target=None):
    if mode == 'nn':
        (M, K), (K2, N) = a.shape, b.shape
    elif mode == 'nt':
        (M, K), (N, K2) = a.shape, b.shape
    else:
        (K, M), (K2, N) = a.shape, b.shape
    assert K == K2, (name, a.shape, b.shape)
    m_first = 0
    if m_rows is not None:
        assert mode == 'tn'
        m_first, M = m_rows
    if col_shards is not None:
        assert add is None and relu2_grad is None and sq_err_target is None and tn_t is None
        tn_t = N // col_shards
    if mode == 'tn':
        d_tm, d_tn, d_tk = 1024, 1024, 2048
    else:
        d_tm, d_tn, d_tk = (2048 if K <= 1024 else 1024), 512, MAX_K_TILE
    tm, tn, tk = _pick(M, tm_t or d_tm), _pick(N, tn_t or d_tn), _pick(K, tk_t or d_tk)
    gm, gn, nk = M // tm, N // tn, K // tk
    if mode == 'nn':
        a_spec = pl.BlockSpec((tm, tk), lambda i, j, k: (i, k))
        b_spec = pl.BlockSpec((tk, tn), lambda i, j, k: (k, j))
        dims = (((1,), (0,)), ((), ()))
    elif mode == 'nt':
        a_spec = pl.BlockSpec((tm, tk), lambda i, j, k: (i, k))
        b_spec = pl.BlockSpec((tn, tk), lambda i, j, k: (j, k))
        dims = (((1,), (1,)), ((), ()))
    else:
        assert m_first % tm == 0
        a_spec = pl.BlockSpec((tk, tm), lambda i, j, k: (k, m_first // tm + i))
        b_spec = pl.BlockSpec((tk, tn), lambda i, j, k: (k, j))
        dims = (((0,), (0,)), ((), ()))
    o_spec = pl.BlockSpec((tm, tn), lambda i, j, k: (i, j))
    has_add, has_e, has_t = add is not None, relu2_grad is not None, sq_err_target is not None
    assert not has_t or (nk == 1 and tn % LANES == 0)

    def body(*refs):
        a_ref, b_ref = refs[0], refs[1]
        pos = 2
        add_ref = e_ref = t_ref = None
        if has_add:
            add_ref = refs[pos]
            pos += 1
        if has_e:
            e_ref = refs[pos]
            pos += 1
        if has_t:
            t_ref = refs[pos]
            pos += 1
        o_ref = refs[pos]
        acc_ref = refs[pos + 1] if nk > 1 else None

        av = a_ref[...]
        if relu2_a:
            av = jnp.maximum(av, 0)
            av = av * av
        prod = lax.dot_general(av.astype(BF16), b_ref[...].astype(BF16), dims, preferred_element_type=F32)

        def finish(r):
            if has_add:
                r = r + add_ref[...]
            if has_e:
                r = r * (2.0 * jnp.maximum(e_ref[...].astype(F32), 0.0))
            if has_t:
                err = r - t_ref[...]
                r = err * (1.0 / N)
                refs[pos + 1][...] = r.astype(BF16)
                sq = err * err
                part = sq[:, 0:LANES]
                for c in range(1, tn // LANES):
                    part = part + sq[:, c * LANES:(c + 1) * LANES]
                _acc_rows(refs[pos + 2], part, (pl.program_id(0) == 0) & (pl.program_id(1) == 0))
            o_ref[...] = r.astype(out_dtype)

        if nk == 1:
            finish(prod)
        else:
            k = pl.program_id(2)

            @pl.when(k == 0)
            def _():
                acc_ref[...] = prod

            @pl.when(k > 0)
            def _():
                acc_ref[...] += prod

            @pl.when(k == nk - 1)
            def _():
                finish(acc_ref[...])

    ins, specs = [a, b], [a_spec, b_spec]
    if has_add:
        ins.append(add)
        specs.append(o_spec)
    if has_e:
        ins.append(relu2_grad)
        specs.append(o_spec)
    out_specs, out_shape, sem = o_spec, jax.ShapeDtypeStruct((M, N), out_dtype), ("parallel", "parallel", "arbitrary")
    if has_t:
        ins.append(sq_err_target)
        specs.append(o_spec)
        out_specs = [o_spec, o_spec, pl.BlockSpec((1, LANES), lambda i, j, k: (0, 0))]
        out_shape = [out_shape, jax.ShapeDtypeStruct((M, N), BF16), jax.ShapeDtypeStruct((1, LANES), F32)]
        sem = ("arbitrary", "arbitrary", "arbitrary")
    if col_shards is not None:
        out_specs = pl.BlockSpec((None, tm, tn), lambda i, j, k: (j, i, 0))
        out_shape = jax.ShapeDtypeStruct((col_shards, M, tn), out_dtype)
    return _call(body, rider, ins, name=name, grid=(gm, gn, nk), in_specs=specs, out_specs=out_specs, out_shape=out_shape,
                 scratch_shapes=[pltpu.VMEM((tm, tn), F32)] if nk > 1 else [], sem=sem)


ROW_BLOCK_BYTES = 12 * 1024 * 1024


def _row_tile(rows, row_bytes):
    return _pick(rows, max(16, min(1024, ROW_BLOCK_BYTES // row_bytes)), 16)


def _rowspec(tr, width, col=0):
    return pl.BlockSpec((tr, width), lambda i, col=col: (i, col))


def _fullspec(shape):
    nd = len(shape)
    return pl.BlockSpec(shape, lambda i, nd=nd: (0,) * nd)


def _rms(x, width):
    x = x.astype(F32)
    return lax.rsqrt(jnp.sum(x * x, axis=-1, keepdims=True) * (1.0 / width) + EPS)


def _rms_bwd_rows(x, g, dy, width):
    x, dy = x.astype(F32), dy.astype(F32)
    r = _rms(x, width)
    xh = x * r
    dn = dy * g
    dx = r * (dn - xh * (jnp.sum(dn * xh, axis=-1, keepdims=True) * (1.0 / width)))
    return dx, dy * xh


def _acc_rows(ref, val, first):
    s = jnp.sum(val, axis=0, keepdims=True)

    @pl.when(first)
    def _():
        ref[...] = s

    @pl.when(jnp.logical_not(first))
    def _():
        ref[...] += s


def _rms_fwd(x, g, name, rider=None):
    rows, width = x.shape
    tr = _row_tile(rows, 6 * width)

    def body(x_ref, g_ref, o_ref):
        xv = x_ref[...]
        o_ref[...] = (xv * _rms(xv, width) * g_ref[...]).astype(BF16)

    return _call(body, rider, [x, g], name=name, grid=(rows // tr,),
                 in_specs=[_rowspec(tr, width), _fullspec((1, width))], out_specs=_rowspec(tr, width),
                 out_shape=jax.ShapeDtypeStruct((rows, width), BF16), scratch_shapes=[], sem=("parallel",))


def _rms_bwd(x, g, dy, res, name, dx_dtypes=(F32,)):
    rows, width = x.shape
    tr = _row_tile(rows, 18 * width)
    has_res = res is not None
    n_in = 4 if has_res else 3

    def body(*refs):
        x_ref, g_ref, dy_ref = refs[:3]
        dx, dgv = _rms_bwd_rows(x_ref[...], g_ref[...], dy_ref[...], width)
        if has_res:
            dx = dx + refs[3][...]
        for ref, dt in zip(refs[n_in:], dx_dtypes):
            ref[...] = dx.astype(dt)
        _acc_rows(refs[-1], dgv, pl.program_id(0) == 0)

    ins = [x, g, dy] + ([res] if has_res else [])
    specs = [_rowspec(tr, width), _fullspec((1, width)), _rowspec(tr, width)] + ([_rowspec(tr, width)] if has_res else [])
    return pl.pallas_call(
        body, name=name, grid=(rows // tr,), in_specs=specs,
        out_specs=[_rowspec(tr, width)] * len(dx_dtypes) + [_fullspec((1, width))],
        out_shape=[jax.ShapeDtypeStruct((rows, width), dt) for dt in dx_dtypes] + [jax.ShapeDtypeStruct((1, width), F32)],
        compiler_params=_params(("arbitrary",)),
    )(*ins)


_GELU_C = math.sqrt(2.0 / math.pi)


def _gelu(x):
    t = jnp.tanh(_GELU_C * (x + 0.044715 * (x * x * x)))
    return 0.5 * x * (1.0 + t), t


def _gelu_grad(x, t):
    return 0.5 * (1.0 + t) + 0.5 * x * (1.0 - t * t) * (_GELU_C * (1.0 + 3.0 * 0.044715 * (x * x)))


def _gm_forward_rows(zu, zv, gln, bln, wc_ref, bst, n_chunk):
    u, tu = _gelu(zu)
    a, ta = _gelu(zv)
    mu = jnp.mean(a, axis=-1, keepdims=True)
    ac = a - mu
    rs = lax.rsqrt(jnp.mean(ac * ac, axis=-1, keepdims=True) + EPS)
    n = ac * rs
    v = n * gln + bln
    vb = v.astype(BF16)
    rows = []
    for c in range(n_chunk):
        cols = []
        for g in range(GM_GROUPS):
            vc = vb[c * GM_CHUNK:(c + 1) * GM_CHUNK, g * LANES:(g + 1) * LANES]
            mixed = jnp.dot(wc_ref[g], vc, preferred_element_type=F32) + bst[g]
            cols.append(mixed)
        rows.append(jnp.concatenate(cols, axis=1))
    mixed = jnp.concatenate(rows, axis=0) if n_chunk > 1 else rows[0]
    return u, tu, ta, n, rs, v, mixed


def _gm_fwd(z, gln, bln, wc, bst, name):
    rows = z.shape[0]
    tr = _pick(rows, 512, GM_CHUNK)
    n_chunk = tr // GM_CHUNK

    def body(zu_ref, zv_ref, gln_ref, bln_ref, wc_ref, bst_ref, o_ref):
        u, _, _, _, _, _, mixed = _gm_forward_rows(zu_ref[...].astype(F32), zv_ref[...].astype(F32), gln_ref[...], bln_ref[...], wc_ref,
                                                   bst_ref, n_chunk)
        o_ref[...] = (u * mixed).astype(BF16)

    return pl.pallas_call(
        body, name=name, grid=(rows // tr,),
        in_specs=[_rowspec(tr, GM_WIDTH, Z_GM // GM_WIDTH), _rowspec(tr, GM_WIDTH, Z_GM // GM_WIDTH + 1),_fullspec((1, GM_WIDTH)), _fullspec((1, GM_WIDTH)),
                  _fullspec((GM_GROUPS, GM_CHUNK, GM_CHUNK)), _fullspec((GM_GROUPS, GM_CHUNK, LANES))],
        out_specs=_rowspec(tr, GM_WIDTH), out_shape=jax.ShapeDtypeStruct((rows, GM_WIDTH), BF16),
        compiler_params=_params(("parallel",)),
    )(z, z, gln, bln, wc, bst)


ANY_SPEC = pl.BlockSpec(memory_space=pl.ANY)


def _gm_bwd(z, dy, gln, bln, wc, wct, bst, dz, name):
    rows = z.shape[0]
    tr = _pick(rows, 512, GM_CHUNK)
    n_chunk = tr // GM_CHUNK

    def body(zu_ref, zv_ref, dy_ref, gln_ref, bln_ref, wc_ref, wct_ref, bst_ref, _, dz_ref, dws_ref, dbs_ref, dgl_ref,
             dbl_ref):
        first = pl.program_id(0) == 0
        zu, zv, gln = zu_ref[...].astype(F32), zv_ref[...].astype(F32), gln_ref[...]
        u, tu, ta, n, rs, v, mixed = _gm_forward_rows(zu, zv, gln, bln_ref[...], wc_ref, bst_ref, n_chunk)
        dyv = dy_ref[...].astype(F32)
        dzu = dyv * mixed * _gelu_grad(zu, tu)
        dmix = dyv * u
        dmb = dmix.astype(BF16)
        vb = v.astype(BF16)
        dv_rows, dws, dbs = [], [None] * GM_GROUPS, None
        for c in range(n_chunk):
            rsl = slice(c * GM_CHUNK, (c + 1) * GM_CHUNK)
            cols = []
            for g in range(GM_GROUPS):
                csl = slice(g * LANES, (g + 1) * LANES)
                dmc = dmb[rsl, csl]
                cols.append(jnp.dot(wct_ref[g], dmc, preferred_element_type=F32))
                w_part = lax.dot_general(dmc, vb[rsl, csl], (((1,), (1,)), ((), ())), preferred_element_type=F32)
                dws[g] = w_part if dws[g] is None else dws[g] + w_part
            dv_rows.append(jnp.concatenate(cols, axis=1))
            dbs = dmix[rsl, :] if dbs is None else dbs + dmix[rsl, :]
        dv = jnp.concatenate(dv_rows, axis=0) if n_chunk > 1 else dv_rows[0]
        dn = dv * gln
        da = rs * (dn - jnp.mean(dn, axis=-1, keepdims=True) - n * jnp.mean(dn * n, axis=-1, keepdims=True))
        dzv = da * _gelu_grad(zv, ta)
        dz_ref[:, 0:GM_WIDTH] = dzu.astype(BF16)
        dz_ref[:, GM_WIDTH:2 * GM_WIDTH] = dzv.astype(BF16)
        _acc_rows(dgl_ref, dv * n, first)
        _acc_rows(dbl_ref, dv, first)

        @pl.when(first)
        def _():
            for g in range(GM_GROUPS):
                dws_ref[g] = dws[g]
            dbs_ref[...] = dbs

        @pl.when(jnp.logical_not(first))
        def _():
            for g in range(GM_GROUPS):
                dws_ref[g] += dws[g]
            dbs_ref[...] += dbs

    wspec = _fullspec((GM_GROUPS, GM_CHUNK, GM_CHUNK))
    return pl.pallas_call(
        body, name=name, grid=(rows // tr,),
        in_specs=[_rowspec(tr, GM_WIDTH, Z_GM // GM_WIDTH), _rowspec(tr, GM_WIDTH, Z_GM // GM_WIDTH + 1),
                  _rowspec(tr, GM_WIDTH), _fullspec((1, GM_WIDTH)), _fullspec((1, GM_WIDTH)), wspec, wspec, wspec, ANY_SPEC],
        out_specs=[_rowspec(tr, 2 * GM_WIDTH, Z_GM // (2 * GM_WIDTH)), wspec, _fullspec((GM_CHUNK, GM_WIDTH)),
                   _fullspec((1, GM_WIDTH)), _fullspec((1, GM_WIDTH))],
        out_shape=[jax.ShapeDtypeStruct(dz.shape, dz.dtype), jax.ShapeDtypeStruct((GM_GROUPS, GM_CHUNK, GM_CHUNK), F32),
                   jax.ShapeDtypeStruct((GM_CHUNK, GM_WIDTH), F32), jax.ShapeDtypeStruct((1, GM_WIDTH), F32),
                   jax.ShapeDtypeStruct((1, GM_WIDTH), F32)],
        input_output_aliases={8: 0}, compiler_params=_params(("arbitrary",)),
    )(z, z, dy, gln, bln, wc, wct, bst, dz)


def _lat_fwd(z, g_cq, g_ckv, name):
    rows = z.shape[0]
    tr = _row_tile(rows, 4 * MLA_W)

    def body(z_ref, gq_ref, gkv_ref, nq_ref, nkv_ref):
        zb = z_ref[...]
        cq, ckv = zb[:, 0:Q_LORA], zb[:, Q_LORA:Q_LORA + KV_LORA]
        nq_ref[...] = (cq * _rms(cq, Q_LORA) * gq_ref[...]).astype(BF16)
        nkv_ref[...] = (ckv * _rms(ckv, KV_LORA) * gkv_ref[...]).astype(BF16)

    return pl.pallas_call(
        body, name=name, grid=(rows // tr,),
        in_specs=[_rowspec(tr, MLA_W, Z_MLA // MLA_W), _fullspec((1, Q_LORA)), _fullspec((1, KV_LORA))],
        out_specs=[_rowspec(tr, Q_LORA), _rowspec(tr, KV_LORA)],
        out_shape=[jax.ShapeDtypeStruct((rows, Q_LORA), BF16), jax.ShapeDtypeStruct((rows, KV_LORA), BF16)],
        compiler_params=_params(("parallel",)),
    )(z, g_cq, g_ckv)


def _lat_bwd(z, dnq, dnkv, dkpe, g_cq, g_ckv, dz, name):
    rows = z.shape[0]
    tr = _row_tile(rows, 8 * MLA_W)

    def body(z_ref, dnq_ref, dnkv_ref, dkpe_ref, gq_ref, gkv_ref, _, dz_ref, dgq_ref, dgkv_ref):
        first = pl.program_id(0) == 0
        zb = z_ref[...]
        dcq, dgq = _rms_bwd_rows(zb[:, 0:Q_LORA], gq_ref[...], dnq_ref[...], Q_LORA)
        dckv, dgkv = _rms_bwd_rows(zb[:, Q_LORA:Q_LORA + KV_LORA], gkv_ref[...], dnkv_ref[...], KV_LORA)
        dz_ref[:, 0:Q_LORA] = dcq.astype(BF16)
        dz_ref[:, Q_LORA:Q_LORA + KV_LORA] = dckv.astype(BF16)
        dz_ref[:, Q_LORA + KV_LORA:MLA_W] = dkpe_ref[...].astype(BF16)
        _acc_rows(dgq_ref, dgq, first)
        _acc_rows(dgkv_ref, dgkv, first)

    return pl.pallas_call(
        body, name=name, grid=(rows // tr,),
        in_specs=[_rowspec(tr, MLA_W, Z_MLA // MLA_W), _rowspec(tr, Q_LORA), _rowspec(tr, KV_LORA), _rowspec(tr, LANES),
                  _fullspec((1, Q_LORA)), _fullspec((1, KV_LORA)), ANY_SPEC],
        out_specs=[_rowspec(tr, MLA_W, Z_MLA // MLA_W), _fullspec((1, Q_LORA)), _fullspec((1, KV_LORA))],
        out_shape=[jax.ShapeDtypeStruct(dz.shape, dz.dtype), jax.ShapeDtypeStruct((1, Q_LORA), F32),
                   jax.ShapeDtypeStruct((1, KV_LORA), F32)],
        input_output_aliases={6: 0}, compiler_params=_params(("arbitrary",)),
    )(z, dnq, dnkv, dkpe, g_cq, g_ckv, dz)


def _rope(y, cc, ss):
    return y * cc + pltpu.roll(y, 64, 1) * ss


def _rope_bwd(d, cc, ss):
    return d * cc + pltpu.roll(d * ss, 64, 1)


def _qk_fwd(q, kv, z, cc, ss, gqn, gqp, gkn, gkp, name):
    rows = q.shape[0]
    W = MLA_HEADS * HEAD
    tr = _row_tile(rows, 20 * W)
    QS = MLA_SCALE * LOG2E

    def body(q_ref, kv_ref, kpe_ref, cc_ref, ss_ref, gqn_ref, gqp_ref, gkn_ref, gkp_ref, qc_ref, kc_ref, v_ref):
        cc, ss = cc_ref[...], ss_ref[...]
        kpe = kpe_ref[...]
        kp = _rope(kpe * _rms(kpe, MLA_ROPE) * gkp_ref[...], cc, ss).astype(BF16)
        for h in range(MLA_HEADS):
            qn = q_ref[:, h * HEAD:(h + 1) * HEAD]
            qp = q_ref[:, W + h * HEAD:W + (h + 1) * HEAD]
            kn = kv_ref[:, h * HEAD:(h + 1) * HEAD]
            qc_ref[:, h * QCAT:h * QCAT + HEAD] = (qn * _rms(qn, HEAD) * gqn_ref[...] * QS).astype(BF16)
            qc_ref[:, h * QCAT + HEAD:(h + 1) * QCAT] = (_rope(qp * _rms(qp, MLA_ROPE) * gqp_ref[...], cc, ss) * QS).astype(BF16)
            kc_ref[:, h * QCAT:h * QCAT + HEAD] = (kn * _rms(kn, HEAD) * gkn_ref[...]).astype(BF16)
            kc_ref[:, h * QCAT + HEAD:(h + 1) * QCAT] = kp
        v_ref[...] = kv_ref[:, W:2 * W].astype(BF16)

    g = _fullspec((1, HEAD))
    return _call(
        body, None, [q, kv, z, cc, ss, gqn, gqp, gkn, gkp], name=name, grid=(rows // tr,),
        in_specs=[_rowspec(tr, 2 * W), _rowspec(tr, 2 * W), _rowspec(tr, LANES, Z_KPE // LANES), _rowspec(tr, LANES),
                  _rowspec(tr, LANES), g, g, g, g],
        out_specs=[_rowspec(tr, MLA_HEADS * QCAT), _rowspec(tr, MLA_HEADS * QCAT), _rowspec(tr, W)],
        out_shape=[jax.ShapeDtypeStruct((rows, MLA_HEADS * QCAT), BF16), jax.ShapeDtypeStruct((rows, MLA_HEADS * QCAT), BF16),
                   jax.ShapeDtypeStruct((rows, W), BF16)],
        scratch_shapes=[], sem=("parallel",))


def _qk_bwd(q, kv, z, cc, ss, gqn, gqp, gkn, gkp, dqc, dkc, dv, name):
    rows = q.shape[0]
    W = MLA_HEADS * HEAD
    tr = _row_tile(rows, 40 * W)

    def body(q_ref, kv_ref, kpe_ref, cc_ref, ss_ref, gqn_ref, gqp_ref, gkn_ref, gkp_ref, dqc_ref, dkc_ref, dv_ref,
             dq_ref, dkv_ref, dkpe_ref, dgqn_ref, dgqp_ref, dgkn_ref, dgkp_ref):
        first = pl.program_id(0) == 0
        cc, ss = cc_ref[...], ss_ref[...]
        sqn = sqp = skn = dkp = None
        for h in range(MLA_HEADS):
            dx, dg = _rms_bwd_rows(q_ref[:, h * HEAD:(h + 1) * HEAD], gqn_ref[...], dqc_ref[:, h * QCAT:h * QCAT + HEAD], HEAD)
            dq_ref[:, h * HEAD:(h + 1) * HEAD] = dx.astype(BF16)
            sqn = dg if sqn is None else sqn + dg
            dy = _rope_bwd(dqc_ref[:, h * QCAT + HEAD:(h + 1) * QCAT], cc, ss)
            dx, dg = _rms_bwd_rows(q_ref[:, W + h * HEAD:W + (h + 1) * HEAD], gqp_ref[...], dy, MLA_ROPE)
            dq_ref[:, W + h * HEAD:W + (h + 1) * HEAD] = dx.astype(BF16)
            sqp = dg if sqp is None else sqp + dg
            dx, dg = _rms_bwd_rows(kv_ref[:, h * HEAD:(h + 1) * HEAD], gkn_ref[...], dkc_ref[:, h * QCAT:h * QCAT + HEAD], HEAD)
            dkv_ref[:, h * HEAD:(h + 1) * HEAD] = dx.astype(BF16)
            skn = dg if skn is None else skn + dg
            part = dkc_ref[:, h * QCAT + HEAD:(h + 1) * QCAT].astype(F32)
            dkp = part if dkp is None else dkp + part
        dkv_ref[:, W:2 * W] = dv_ref[...].astype(BF16)
        dx, dg = _rms_bwd_rows(kpe_ref[...], gkp_ref[...], _rope_bwd(dkp, cc, ss), MLA_ROPE)
        dkpe_ref[...] = dx
        _acc_rows(dgqn_ref, sqn, first)
        _acc_rows(dgqp_ref, sqp, first)
        _acc_rows(dgkn_ref, skn, first)
        _acc_rows(dgkp_ref, dg, first)

    g = _fullspec((1, HEAD))
    gs = jax.ShapeDtypeStruct((1, HEAD), F32)
    return pl.pallas_call(
        body, name=name, grid=(rows // tr,),
        in_specs=[_rowspec(tr, 2 * W), _rowspec(tr, 2 * W), _rowspec(tr, LANES, Z_KPE // LANES), _rowspec(tr, LANES),
                  _rowspec(tr, LANES), g, g, g, g, _rowspec(tr, MLA_HEADS * QCAT), _rowspec(tr, MLA_HEADS * QCAT),
                  _rowspec(tr, W)],
        out_specs=[_rowspec(tr, 2 * W), _rowspec(tr, 2 * W), _rowspec(tr, LANES), g, g, g, g],
        out_shape=[jax.ShapeDtypeStruct((rows, 2 * W), BF16), jax.ShapeDtypeStruct((rows, 2 * W), BF16),
                   jax.ShapeDtypeStruct((rows, LANES), F32), gs, gs, gs, gs],
        compiler_params=_params(("arbitrary",)),
    )(q, kv, z, cc, ss, gqn, gqp, gkn, gkp, dqc, dkc, dv)


def _headnorm_fwd(x, col, nheads, g, out_scale, name):
    rows = x.shape[0]
    W = nheads * HEAD
    tr = _row_tile(rows, 6 * W)

    def body(x_ref, g_ref, o_ref):
        for h in range(nheads):
            xv = x_ref[:, h * HEAD:(h + 1) * HEAD]
            o_ref[:, h * HEAD:(h + 1) * HEAD] = (xv * _rms(xv, HEAD) * g_ref[...] * out_scale).astype(BF16)

    return pl.pallas_call(
        body, name=name, grid=(rows // tr,),
        in_specs=[_rowspec(tr, W, col), _fullspec((1, HEAD))], out_specs=_rowspec(tr, W),
        out_shape=jax.ShapeDtypeStruct((rows, W), BF16), compiler_params=_params(("parallel",)),
    )(x, g)


def _headnorm_bwd(x, col, nheads, g, dy, tail, name, into=None):
    rows = x.shape[0]
    W = nheads * HEAD
    tr = _row_tile(rows, 12 * W)
    has_tail = tail is not None
    WO = 2 * W if has_tail else W

    def body(*refs):
        if into is not None:
            x_ref, g_ref, dy_ref, _, dx_ref, dg_ref = refs
        elif has_tail:
            x_ref, g_ref, dy_ref, t_ref, dx_ref, dg_ref = refs
        else:
            x_ref, g_ref, dy_ref, dx_ref, dg_ref = refs
        acc = None
        for h in range(nheads):
            sl = slice(h * HEAD, (h + 1) * HEAD)
            dx, dg = _rms_bwd_rows(x_ref[:, sl], g_ref[...], dy_ref[:, sl], HEAD)
            dx_ref[:, sl] = dx.astype(BF16)
            acc = dg if acc is None else acc + dg
        if has_tail:
            dx_ref[:, W:2 * W] = t_ref[...].astype(BF16)
        _acc_rows(dg_ref, acc, pl.program_id(0) == 0)

    ins = [x, g, dy] + ([tail] if has_tail else [])
    specs = [_rowspec(tr, W, col), _fullspec((1, HEAD)), _rowspec(tr, W)] + ([_rowspec(tr, W)] if has_tail else [])
    dx_spec, dx_shape, aliases = _rowspec(tr, WO), jax.ShapeDtypeStruct((rows, WO), BF16), {}
    if into is not None:
        assert not has_tail
        ins, specs = ins + [into[0]], specs + [ANY_SPEC]
        dx_spec, dx_shape, aliases = _rowspec(tr, W, into[1]), jax.ShapeDtypeStruct(into[0].shape, into[0].dtype), {3: 0}
    return pl.pallas_call(
        body, name=name, grid=(rows // tr,), in_specs=specs,
        out_specs=[dx_spec, _fullspec((1, HEAD))], out_shape=[dx_shape, jax.ShapeDtypeStruct((1, HEAD), F32)],
        input_output_aliases=aliases, compiler_params=_params(("arbitrary",)),
    )(*ins)


def _sigmoid(x):
    return 1.0 / (1.0 + jnp.exp(-x.astype(F32)))


def _merge_fwd(z, y_gm, y_mla, y_mem, name):
    rows = z.shape[0]
    tr = _row_tile(rows, 14 * D_MODEL)

    def body(g0_ref, g1_ref, g2_ref, a_ref, b_ref, c_ref, o_ref):
        m = _sigmoid(g0_ref[...]) * a_ref[...] + _sigmoid(g1_ref[...]) * b_ref[...] + _sigmoid(g2_ref[...]) * c_ref[...]
        o_ref[...] = m.astype(BF16)

    r = _rowspec(tr, D_MODEL)
    return pl.pallas_call(
        body, name=name, grid=(rows // tr,),
        in_specs=[_rowspec(tr, D_MODEL, 0), _rowspec(tr, D_MODEL, 1), _rowspec(tr, D_MODEL, 2),r, r, r],
        out_specs=r, out_shape=jax.ShapeDtypeStruct((rows, D_MODEL), BF16), compiler_params=_params(("parallel",)),
    )(z, z, z, y_gm, y_mla, y_mem)


def _merge_bwd(z, y_gm, y_mla, y_mem, dm, name):
    rows = z.shape[0]
    tr = _row_tile(rows, 24 * D_MODEL)

    def body(g0_ref, g1_ref, g2_ref, a_ref, b_ref, c_ref, dm_ref, da_ref, db_ref, dc_ref, dzg_ref):
        dmv = dm_ref[...].astype(F32)
        for k, (g_ref, y_ref, dy_ref) in enumerate(((g0_ref, a_ref, da_ref), (g1_ref, b_ref, db_ref), (g2_ref, c_ref, dc_ref))):
            s = _sigmoid(g_ref[...])
            dy_ref[...] = (dmv * s).astype(BF16)
            dzg_ref[:, k * D_MODEL:(k + 1) * D_MODEL] = (dmv * y_ref[...] * s * (1.0 - s)).astype(BF16)

    r = _rowspec(tr, D_MODEL)
    o = jax.ShapeDtypeStruct((rows, D_MODEL), BF16)
    return pl.pallas_call(
        body, name=name, grid=(rows // tr,),
        in_specs=[_rowspec(tr, D_MODEL, 0), _rowspec(tr, D_MODEL, 1), _rowspec(tr, D_MODEL, 2),r, r, r, r],
        out_specs=[r, r, r, _rowspec(tr, 3 * D_MODEL, 0)],
        out_shape=[o, o, o, jax.ShapeDtypeStruct((rows, Z_COLS), BF16)],
        compiler_params=_params(("parallel",)),
    )(z, z, z, y_gm, y_mla, y_mem, dm)


_NT = (((1,), (1,)), ((), ()))
_TN = (((0,), (0,)), ((), ()))


def _diag_mask(s):
    row = lax.broadcasted_iota(jnp.int32, s.shape, 0)
    col = lax.broadcasted_iota(jnp.int32, s.shape, 1)
    return jnp.where(row >= col, s, NEG)


def _attn_fwd(q, k, v, nb, nheads, dk, v_col0, causal, name, rider=None):
    S, Skv = q.shape[0] // nb, k.shape[0] // nb
    tq = _pick(Skv, ATT_TILE) if causal else _pick(S, 4 * ATT_TILE)
    nq = S // tq

    def body(q_ref, k_ref, v_ref, o_ref, lse_ref):
        for i in range(nq):
            r0 = i * tq
            qb = q_ref[r0:r0 + tq, :]
            if causal:
                spans = ([(0, r0, False)] if i > 0 else []) + [(r0, r0 + tq, True)]
            else:
                spans = [(0, Skv, False)]
            scores = []
            for a, b, masked in spans:
                s = lax.dot_general(qb, k_ref[a:b, :], _NT, preferred_element_type=F32)
                scores.append(_diag_mask(s) if masked else s)
            m = functools.reduce(jnp.maximum, [jnp.max(s, axis=-1, keepdims=True) for s in scores])
            l = acc = None
            for s, (a, b, _) in zip(scores, spans):
                p = jnp.exp2(s - m)
                lp = jnp.sum(p, axis=-1, keepdims=True)
                ap = jnp.dot(p.astype(BF16), v_ref[a:b, :].astype(BF16), preferred_element_type=F32)
                l, acc = (lp, ap) if l is None else (l + lp, acc + ap)
            o_ref[r0:r0 + tq, :] = (acc / l).astype(BF16)
            lse_ref[r0:r0 + tq, :] = m + jnp.log2(l)

    ins = [q, k, v]
    in_specs = [pl.BlockSpec((S, dk), lambda b, h: (b, h)), pl.BlockSpec((Skv, dk), lambda b, h: (b, h)),
                pl.BlockSpec((Skv, HEAD), lambda b, h: (b, v_col0 + h))]
    out_specs = [pl.BlockSpec((S, HEAD), lambda b, h: (b, h)), pl.BlockSpec((None, S, 1), lambda b, h: (h, b, 0))]
    out_shape = [jax.ShapeDtypeStruct((nb * S, nheads * HEAD), BF16), jax.ShapeDtypeStruct((nheads, nb * S, 1), F32)]
    return _call(body, rider, ins, name=name, grid=(nb, nheads), in_specs=in_specs, out_specs=out_specs,
                 out_shape=out_shape, scratch_shapes=[], sem=("parallel", "parallel"))


def _attn_bwd(q, k, v, o, do, lse, nb, nheads, dk, v_col0, scale, causal, name, rider=None):
    S, Skv = q.shape[0] // nb, k.shape[0] // nb
    tk = _pick(Skv, ATT_TILE)
    nkv = Skv // tk

    def body(q_ref, k_ref, v_ref, o_ref, do_ref, lse_ref, dq_ref, dk_ref, dv_ref, delta_ref, dob_ref, dqa_ref):
        dov = do_ref[...]
        delta_ref[...] = jnp.sum(o_ref[...].astype(F32) * dov.astype(F32), axis=-1, keepdims=True)
        dob_ref[...] = dov.astype(BF16)

        for j in range(nkv):
            c0 = j * tk
            kb = k_ref[c0:c0 + tk, :]
            vb = v_ref[c0:c0 + tk, :].astype(BF16)
            if causal:
                spans = [(c0, c0 + tk, True)] + ([(c0 + tk, S, False)] if c0 + tk < S else [])
            else:
                spans = [(0, S, False)]
            dk_acc = dv_acc = None
            for a, b, masked in spans:
                qb = q_ref[a:b, :]
                dob = dob_ref[a:b, :]
                s = lax.dot_general(qb, kb, _NT, preferred_element_type=F32)
                if masked:
                    s = _diag_mask(s)
                p = jnp.exp2(s - lse_ref[a:b, :])
                dp = lax.dot_general(dob, vb, _NT, preferred_element_type=F32)
                ds = (p * (dp - delta_ref[a:b, :])).astype(BF16)
                dv_p = lax.dot_general(p.astype(BF16), dob, _TN, preferred_element_type=F32)
                dk_p = lax.dot_general(ds, qb, _TN, preferred_element_type=F32)
                dk_acc, dv_acc = (dk_p, dv_p) if dk_acc is None else (dk_acc + dk_p, dv_acc + dv_p)
                dq_p = jnp.dot(ds, kb, preferred_element_type=F32) * scale
                if j == 0:
                    dqa_ref[a:b, :] = dq_p
                else:
                    dqa_ref[a:b, :] += dq_p
            dk_ref[c0:c0 + tk, :] = (dk_acc * LN2).astype(BF16)
            dv_ref[c0:c0 + tk, :] = dv_acc.astype(BF16)
        dq_ref[...] = dqa_ref[...].astype(BF16)

    ins = [q, k, v, o, do, lse]
    in_specs = [pl.BlockSpec((S, dk), lambda b, h: (b, h)), pl.BlockSpec((Skv, dk), lambda b, h: (b, h)),
                pl.BlockSpec((Skv, HEAD), lambda b, h: (b, v_col0 + h)), pl.BlockSpec((S, HEAD), lambda b, h: (b, h)),
                pl.BlockSpec((S, HEAD), lambda b, h: (b, h)), pl.BlockSpec((None, S, 1), lambda b, h: (h, b, 0))]
    out_specs = [pl.BlockSpec((S, dk), lambda b, h: (b, h)), pl.BlockSpec((Skv, dk), lambda b, h: (b, h)),
                 pl.BlockSpec((Skv, HEAD), lambda b, h: (b, h))]
    out_shape = [jax.ShapeDtypeStruct((nb * S, nheads * dk), BF16), jax.ShapeDtypeStruct((nb * Skv, nheads * dk), BF16),
                 jax.ShapeDtypeStruct((nb * Skv, nheads * HEAD), BF16)]
    return _call(body, rider, ins, name=name, grid=(nb, nheads), in_specs=in_specs, out_specs=out_specs,
                 out_shape=out_shape,
                 scratch_shapes=[pltpu.VMEM((S, 1), F32), pltpu.VMEM((S, HEAD), BF16), pltpu.VMEM((S, dk), F32)],
                 sem=("parallel", "parallel"))


def _spread_rope(a):
    zero = jnp.zeros(a.shape[:-1] + (32,), a.dtype)
    return jnp.concatenate([a[..., :32], zero, a[..., 32:], zero], axis=-1)


def _gather_rope(a):
    return jnp.concatenate([a[..., 0:32], a[..., 64:96]], axis=-1)


def _win_layout(w):
    return jnp.concatenate([w[:, C_ZG:C_END], w[:, C_ZU:C_CQ], w[:, C_QM:C_ZG], w[:, C_CQ:C_CKV], w[:, C_CKV:C_KPE],
                            _spread_rope(w[:, C_KPE:C_QM])], axis=1)


def _win_unlayout(d):
    return jnp.concatenate([d[:, Z_GM:Z_QM], d[:, Z_MLA:Z_MLA + Q_LORA], d[:, Z_MLA + Q_LORA:Z_KPE],
                            _gather_rope(d[:, Z_KPE:Z_COLS]), d[:, Z_QM:Z_MLA], d[:, 0:Z_GM]], axis=1)


def _wuq_layout(w):
    r = w.reshape(Q_LORA, MLA_HEADS, HEAD + MLA_ROPE)
    return jnp.concatenate([r[:, :, :HEAD].reshape(Q_LORA, -1), _spread_rope(r[:, :, HEAD:]).reshape(Q_LORA, -1)], axis=1)


def _wuq_unlayout(d):
    n = d[:, :MLA_HEADS * HEAD].reshape(Q_LORA, MLA_HEADS, HEAD)
    p = _gather_rope(d[:, MLA_HEADS * HEAD:].reshape(Q_LORA, MLA_HEADS, HEAD))
    return jnp.concatenate([n, p], axis=-1).reshape(Q_LORA, -1)


def _wukv_layout(w):
    r = w.reshape(KV_LORA, MLA_HEADS, 2 * HEAD)
    return jnp.concatenate([r[:, :, :HEAD].reshape(KV_LORA, -1), r[:, :, HEAD:].reshape(KV_LORA, -1)], axis=1)


def _wukv_unlayout(d):
    k = d[:, :MLA_HEADS * HEAD].reshape(KV_LORA, MLA_HEADS, HEAD)
    v = d[:, MLA_HEADS * HEAD:].reshape(KV_LORA, MLA_HEADS, HEAD)
    return jnp.concatenate([k, v], axis=-1).reshape(KV_LORA, -1)


AG_MID = ['w_uq', 'w_ukv', 'w_mem_kv', 'w_o_gm', 'w_o_mla', 'w_o_mem', 'w_out']
AG_FFN = ['w_ff1', 'w_ff2']
RS_GROUPS = {'ffn_proj': ['w_ff2', 'w_ff1', 'w_out', 'w_o_gm', 'w_o_mla', 'w_o_mem'], 'in_top': ['w_in'],
             'in_bot_lat': ['w_in', 'w_uq', 'w_ukv', 'w_mem_kv']}


def _unride(res, rider):
    return (res, None) if rider is None else res


def _local_step(x, mem, positions, target, P, ws):
    B, S, _ = x.shape
    M = mem.shape[1]
    T = B * S
    x2d = x.reshape(T, D_MODEL)
    mem2d = mem.reshape(B * M, D_MODEL)
    tgt2d = target.reshape(T, D_MODEL)

    def row(v):
        return v.reshape(1, -1).astype(F32)

    inv_freq = ROPE_BASE ** (-jnp.arange(0, MLA_ROPE, 2, dtype=F32) / MLA_ROPE)
    ang = positions.reshape(T).astype(F32)[:, None] * inv_freq
    cos, sin, zero = jnp.cos(ang), jnp.sin(ang), jnp.zeros_like(ang)
    cc = jnp.concatenate([cos, zero, cos, zero], axis=1)
    ss = jnp.concatenate([-sin, zero, sin, zero], axis=1)

    g_mix, g_cq, g_ckv, g_ffn, g_mem = row(P['g_mix']), row(P['g_cq']), row(P['g_ckv']), row(P['g_ffn']), row(P['g_mem'])
    gqn, gkn, gmq, gmk = row(P['g_q_nope']), row(P['g_k_nope']), row(P['g_mq']), row(P['g_mk'])
    gqp, gkp = _spread_rope(row(P['g_q_pe'])), _spread_rope(row(P['g_k_pe']))
    gln, bln = row(P['g_gm_ln']), row(P['b_gm_ln'])
    wc = jnp.tril(P['w_spatial'].astype(F32))
    wct = jnp.swapaxes(wc, 1, 2).astype(BF16)
    wc = wc.astype(BF16)
    bst = jnp.broadcast_to(P['b_spatial'].astype(F32)[:, :, None], (GM_GROUPS, GM_CHUNK, LANES))

    ride = ws.gather(['w_in'])
    h, got = _unride(_rms_fwd(x2d, g_mix, "rms_mix", rider=ride), ride)
    w_in = _win_layout(ws.gathered(['w_in'], got)['w_in']).astype(BF16)
    ride = ws.gather(AG_MID)
    z, got = _unride(_matmul(h, w_in, 'nn', ACT, "mm_in", tn_t=768, rider=ride), ride)
    mid = ws.gathered(AG_MID, got)
    w_uq, w_ukv = _wuq_layout(mid['w_uq']).astype(BF16), _wukv_layout(mid['w_ukv']).astype(BF16)
    w_mem_kv, w_o_gm, w_o_mla, w_o_mem, w_out = (mid[n] for n in ('w_mem_kv', 'w_o_gm', 'w_o_mla', 'w_o_mem', 'w_out'))
    ygm_pre = _gm_fwd(z, gln, bln, wc, bst, "gm_fwd")
    y_gm = _matmul(ygm_pre, w_o_gm, 'nn', ACT, "mm_o_gm")
    nq, nkv = _lat_fwd(z, g_cq, g_ckv, "lat_fwd")
    q = _matmul(nq, w_uq, 'nn', ACT, "mm_uq")
    kv = _matmul(nkv, w_ukv, 'nn', ACT, "mm_ukv")
    qcat, kcat, vv = _qk_fwd(q, kv, z, cc, ss, gqn, gqp, gkn, gkp, "qk_fwd")
    ride = ws.gather(AG_FFN)
    (o, lse), got = _unride(_attn_fwd(qcat, kcat, vv, B, MLA_HEADS, QCAT, 0, True, "mla_attn_fwd", rider=ride), ride)
    ffn = ws.gathered(AG_FFN, got)
    w_ff1, w_ff2 = ffn['w_ff1'], ffn['w_ff2']
    y_mla = _matmul(o, w_o_mla, 'nn', ACT, "mm_o_mla")
    nm = _rms_fwd(mem2d, g_mem, "rms_mem")
    kvm = _matmul(nm, w_mem_kv, 'nn', ACT, "mm_mem_kv")
    qm = _headnorm_fwd(z, Z_QM // (MEM_HEADS * HEAD), MEM_HEADS, gmq, MEM_SCALE * LOG2E, "memq_fwd")
    km = _headnorm_fwd(kvm, 0, MEM_HEADS, gmk, 1.0, "memk_fwd")
    om, lse_m = _attn_fwd(qm, km, kvm, B, MEM_HEADS, HEAD, MEM_HEADS, False, "mem_attn_fwd")
    y_mem = _matmul(om, w_o_mem, 'nn', ACT, "mm_o_mem")
    merged = _merge_fwd(z, y_gm, y_mla, y_mem, "merge_fwd")
    x1 = _matmul(merged, w_out, 'nn', F32, "mm_out", add=x2d)
    h2 = _rms_fwd(x1, g_ffn, "rms_ffn")
    a1 = _matmul(h2, w_ff1, 'nn', BF16, "mm_ff1")
    dx2, dx2b, loss_part = _matmul(a1, w_ff2, 'nn', F32, "mm_ff2", add=x1, relu2_a=True, sq_err_target=tgt2d)

    G = {}
    d_ff2 = _matmul(a1, dx2b, 'tn', BF16, "mm_d_ff2", relu2_a=True)
    da1 = _matmul(dx2b, w_ff2, 'nt', BF16, "mm_da1", relu2_grad=a1)
    d_ff1 = _matmul(h2, da1, 'tn', BF16, "mm_d_ff1", col_shards=N_DEV)
    dh2 = _matmul(da1, w_ff1, 'nt', ACT, "mm_dh2")
    dx1, dx1b, G['g_ffn'] = _rms_bwd(x1, g_ffn, dh2, dx2, "rms_ffn_bwd", dx_dtypes=(F32, BF16))
    d_out = _matmul(merged, dx1b, 'tn', BF16, "mm_d_out")
    dmerged = _matmul(dx1b, w_out, 'nt', ACT, "mm_dmerged")
    dy_gm, dy_mla, dy_mem, dz = _merge_bwd(z, y_gm, y_mla, y_mem, dmerged, "merge_bwd")
    d_o_gm = _matmul(ygm_pre, dy_gm, 'tn', BF16, "mm_d_o_gm")
    d_o_mla = _matmul(o, dy_mla, 'tn', BF16, "mm_d_o_mla")
    d_o_mem = _matmul(om, dy_mem, 'tn', BF16, "mm_d_o_mem")
    dygm_pre = _matmul(dy_gm, w_o_gm, 'nt', ACT, "mm_dygm")
    dz, dws, dbs, G['g_gm_ln'], G['b_gm_ln'] = _gm_bwd(z, dygm_pre, gln, bln, wc, wct, bst, dz, "gm_bwd")
    G['w_spatial'] = jnp.tril(dws)
    G['b_spatial'] = jnp.sum(dbs.reshape(GM_CHUNK, GM_GROUPS, LANES), axis=-1).T
    do = _matmul(dy_mla, w_o_mla, 'nt', ACT, "mm_do")
    ride = ws.scatter('ffn_proj', {'w_ff2': d_ff2, 'w_ff1': d_ff1, 'w_out': d_out, 'w_o_gm': d_o_gm, 'w_o_mla': d_o_mla,
                                   'w_o_mem': d_o_mem})
    (dqc, dkc, dvv), got = _unride(_attn_bwd(qcat, kcat, vv, o, do, lse, B, MLA_HEADS, QCAT, 0, MLA_SCALE, True,
                                             "mla_attn_bwd", rider=ride), ride)
    ws.scattered('ffn_proj', got)
    dq, dkv, dkpe, G['g_q_nope'], dgqp, G['g_k_nope'], dgkp = _qk_bwd(q, kv, z, cc, ss, gqn, gqp, gkn, gkp, dqc, dkc, dvv,
                                                                     "qk_bwd")
    G['g_q_pe'], G['g_k_pe'] = _gather_rope(dgqp), _gather_rope(dgkp)
    d_uq = _wuq_unlayout(_matmul(nq, dq, 'tn', BF16, "mm_d_uq"))
    dnq = _matmul(dq, w_uq, 'nt', ACT, "mm_dnq")
    d_ukv = _wukv_unlayout(_matmul(nkv, dkv, 'tn', BF16, "mm_d_ukv"))
    dnkv = _matmul(dkv, w_ukv, 'nt', ACT, "mm_dnkv")
    dz, G['g_cq'], G['g_ckv'] = _lat_bwd(z, dnq, dnkv, dkpe, g_cq, g_ckv, dz, "lat_bwd")
    dom = _matmul(dy_mem, w_o_mem, 'nt', ACT, "mm_dom")
    dqm, dkm, dvm = _attn_bwd(qm, km, kvm, om, dom, lse_m, B, MEM_HEADS, HEAD, MEM_HEADS, MEM_SCALE, False, "mem_attn_bwd")
    dz, G['g_mq'] = _headnorm_bwd(z, Z_QM // (MEM_HEADS * HEAD), MEM_HEADS, gmq, dqm, None, "memq_bwd",
                                  into=(dz, Z_QM // (MEM_HEADS * HEAD)))
    dkvm, G['g_mk'] = _headnorm_bwd(kvm, 0, MEM_HEADS, gmk, dkm, dvm, "memk_bwd")
    d_mem_kv = _matmul(nm, dkvm, 'tn', BF16, "mm_d_mem_kv")
    dnm = _matmul(dkvm, w_mem_kv, 'nt', ACT, "mm_dnm")
    G['g_mem'], = _rms_bwd(mem2d, g_mem, dnm, None, "rms_mem_bwd", dx_dtypes=())
    half = D_MODEL // 2
    d_top = _matmul(h, dz, 'tn', BF16, "mm_d_in_top", tn_t=768, m_rows=(0, half))
    ride = ws.scatter('in_top', {'w_in': _win_unlayout(d_top)})
    d_bot, got = _unride(_matmul(h, dz, 'tn', BF16, "mm_d_in_bot", tn_t=768, m_rows=(half, half), rider=ride), ride)
    ws.scattered('in_top', got)
    ride = ws.scatter('in_bot_lat', {'w_in': _win_unlayout(d_bot), 'w_uq': d_uq, 'w_ukv': d_ukv, 'w_mem_kv': d_mem_kv})
    dh, got = _unride(_matmul(dz, w_in, 'nt', ACT, "mm_dh", rider=ride), ride)
    ws.scattered('in_bot_lat', got)
    gx, G['g_mix'] = _rms_bwd(x2d, g_mix, dh, dx1, "rms_mix_bwd")
    return loss_part, gx.reshape(B, S, D_MODEL), G


def _all_gather8(xs, name):
    def body(x_ref, out_ref, send_sems, recv_sems, local_sem):
        x, y, c = lax.axis_index("x"), lax.axis_index("y"), lax.axis_index("c")
        me, sibling = (x, y, c), (x, y, 1 - c)
        chips = [(1 - x, y), (x, 1 - y), (1 - x, 1 - y)]

        def rows(px, py, pc):
            return out_ref.at[4 * px + 2 * py + pc]

        def copy(k, block, to, src=None):
            return pltpu.make_async_remote_copy(
                src_ref=rows(*block) if src is None else src, dst_ref=rows(*block),
                send_sem=send_sems.at[k], recv_sem=recv_sems.at[k], device_id=to, device_id_type=MESH)

        mine = pltpu.make_async_copy(x_ref, rows(*me), local_sem)
        mine.start()
        first = [copy(0, me, sibling, src=x_ref)]
        first += [copy(1 + j, me, (*chip, c), src=x_ref) for j, chip in enumerate(chips)]
        for cp in first:
            cp.start()
        passed = [copy(4 + j, (*chip, c), sibling) for j, chip in enumerate(chips)]
        for j, chip in enumerate(chips):
            copy(1 + j, (*chip, c), me).wait_recv()
            passed[j].start()
        copy(0, sibling, me).wait_recv()
        for j, chip in enumerate(chips):
            copy(4 + j, (*chip, 1 - c), me).wait_recv()
        for cp in first + passed:
            cp.wait_send()
        mine.wait()

    return pl.pallas_call(
        body, name=name, in_specs=[HBM_SPEC], out_specs=HBM_SPEC,
        out_shape=jax.ShapeDtypeStruct((N_DEV,) + xs.shape, xs.dtype),
        scratch_shapes=[pltpu.SemaphoreType.DMA((7,)), pltpu.SemaphoreType.DMA((7,)), pltpu.SemaphoreType.DMA],
    )(xs)


def _adamw_rows(w, g, m, v):
    m2 = ADAM_B1 * m + (1.0 - ADAM_B1) * g
    v2 = ADAM_B2 * v + (1.0 - ADAM_B2) * (g * g)
    m_hat = m2 / (1.0 - ADAM_B1 ** ADAM_STEP)
    v_hat = v2 / (1.0 - ADAM_B2 ** ADAM_STEP)
    delta = -ADAM_LR * (m_hat / (jnp.sqrt(v_hat) + ADAM_EPS) + ADAM_WD * w)
    return delta, m2, v2


def _sum_adamw(parts, w, m, v, name):
    rows, cols = w.shape
    assert sum(p.shape[1] for p in parts) == rows
    tr = _pick(min(p.shape[1] for p in parts), max(16, 65536 // cols), 16)
    n = parts[0].shape[0]
    counts = [p.shape[1] // tr for p in parts]
    starts = [sum(counts[:k]) for k in range(len(parts))]

    def body(*refs):
        p_refs = refs[:len(parts)]
        w_ref, m_ref, v_ref, g_ref, d_ref, m2_ref, v2_ref = refs[len(parts):]
        g = None
        for p_ref, start in zip(p_refs, starts):
            gk = p_ref[0].astype(F32)
            for k in range(1, n):
                gk = gk + p_ref[k].astype(F32)
            g = gk if g is None else jnp.where(pl.program_id(0) >= start, gk, g)
        delta, m2, v2 = _adamw_rows(w_ref[...], g, m_ref[...], v_ref[...])
        g_ref[...] = g
        d_ref[...] = delta
        m2_ref[...] = m2
        v2_ref[...] = v2

    flat = pl.BlockSpec((tr, cols), lambda i: (i, 0))
    out = jax.ShapeDtypeStruct((rows, cols), F32)
    p_specs = [pl.BlockSpec((n, tr, cols), lambda i, s=s, c=c: (0, jnp.clip(i - s, 0, c - 1), 0))
               for s, c in zip(starts, counts)]
    return pl.pallas_call(
        body, name=name, grid=(rows // tr,), in_specs=p_specs + [flat, flat, flat], out_specs=[flat] * 4,
        out_shape=[out] * 4, compiler_params=_params(("parallel",)),
    )(*parts, w, m, v)


def _small_rows(name):
    n = {'g_mix': 1024, 'g_cq': 384, 'g_ckv': 256, 'g_q_nope': 128, 'g_q_pe': 64, 'g_k_nope': 128, 'g_k_pe': 64,
         'g_gm_ln': 512, 'b_gm_ln': 512, 'w_spatial': GM_GROUPS * GM_CHUNK * GM_CHUNK, 'b_spatial': GM_GROUPS * GM_CHUNK,
         'g_mem': 1024, 'g_mq': 128, 'g_mk': 128, 'g_ffn': 1024}[name]
    return n, -(-n // (8 * LANES)) * 8


def _small_slab(d):
    parts = []
    for name in SMALL:
        n, rows = _small_rows(name)
        parts.append(jnp.pad(d[name].reshape(-1).astype(F32), (0, rows * LANES - n)).reshape(rows, LANES))
    return jnp.concatenate(parts, axis=0)


def _small_unslab(slab, like):
    out, r = {}, 0
    for name in SMALL:
        n, rows = _small_rows(name)
        out[name] = slab[r:r + rows].reshape(-1)[:n].reshape(like[name].shape)
        r += rows
    return out


def _full_from_gathered(gathered, name):
    r, c = BIG_SHAPE[name]
    if BIG_AXIS[name] == 0:
        return gathered.reshape(r, c)
    return gathered.transpose(1, 0, 2).reshape(r, c)


def _shards_of_full(g, name):
    if g.ndim == 3:
        return g
    r, c = BIG_SHAPE[name]
    if BIG_AXIS[name] == 0:
        return g.reshape(N_DEV, r // N_DEV, c)
    return g.reshape(g.shape[0], N_DEV, c // N_DEV).transpose(1, 0, 2)


class _DistWeights:
    def __init__(self, shards):
        self.shards = shards
        self.received = {}

    def gather(self, names):
        return _Gather2([self.shards[n].astype(BF16) for n in names])

    def gathered(self, names, got):
        return {n: _full_from_gathered(g, n) for n, g in zip(names, got)}

    def scatter(self, key, grads):
        return _Exchange([_shards_of_full(grads[n], n) for n in RS_GROUPS[key]], scatter=True)

    def scattered(self, key, got):
        for n, g in zip(RS_GROUPS[key], got):
            self.received.setdefault(n, []).append(g)


def kernel(x, mem, positions, g_mix, w_in, g_cq, w_uq, g_ckv, w_ukv, g_q_nope, g_q_pe, g_k_nope, g_k_pe, g_gm_ln, b_gm_ln, w_spatial, b_spatial, g_mem, w_mem_kv, g_mq, g_mk, w_o_gm, w_o_mla, w_o_mem, w_out, g_ffn, w_ff1, w_ff2, loss_target, m_g_mix, m_w_in, m_g_cq, m_w_uq, m_g_ckv, m_w_ukv, m_g_q_nope, m_g_q_pe, m_g_k_nope, m_g_k_pe, m_g_gm_ln, m_b_gm_ln, m_w_spatial, m_b_spatial, m_g_mem, m_w_mem_kv, m_g_mq, m_g_mk, m_w_o_gm, m_w_o_mla, m_w_o_mem, m_w_out, m_g_ffn, m_w_ff1, m_w_ff2, v_g_mix, v_w_in, v_g_cq, v_w_uq, v_g_ckv, v_w_ukv, v_g_q_nope, v_g_q_pe, v_g_k_nope, v_g_k_pe, v_g_gm_ln, v_b_gm_ln, v_w_spatial, v_b_spatial, v_g_mem, v_w_mem_kv, v_g_mq, v_g_mk, v_w_o_gm, v_w_o_mla, v_w_o_mem, v_w_out, v_g_ffn, v_w_ff1, v_w_ff2):
    given = dict(locals())
    w = {n: given[n][0] for n in WEIGHTS}
    mom = {n: given['m_' + n][0] for n in WEIGHTS}
    var = {n: given['v_' + n][0] for n in WEIGHTS}

    ws = _DistWeights({n: w[n] for n in BIG})
    loss_part, grad_x, G = _local_step(x, mem, positions, loss_target, {n: w[n] for n in SMALL}, ws)

    outs = {}
    for n in BIG:
        for prefix, res in zip(("grad_", "delta_", "new_m_", "new_v_"),
                               _sum_adamw(ws.received[n], w[n], mom[n], var[n], "adamw_" + n)):
            outs[prefix + n] = res[None]

    tail = jnp.zeros((8, LANES), F32)
    parts = _all_gather8(jnp.concatenate([_small_slab(G), jnp.pad(loss_part, ((0, 7), (0, 0)))], axis=0), "ag_small")
    small = _sum_adamw([parts], *[jnp.concatenate([_small_slab(d), tail], axis=0) for d in (w, mom, var)], "adamw_small")
    loss = 0.5 * jnp.sum(small[0][-8:]) / D_MODEL
    for prefix, small_slab in zip(("grad_", "delta_", "new_m_", "new_v_"), small):
        sm = _small_unslab(small_slab, given)
        for n in SMALL:
            outs[prefix + n] = sm[n]
    return (loss, grad_x, *[outs[p + n] for p in ("grad_", "delta_", "new_m_", "new_v_") for n in WEIGHTS])
```

```python
import functools
import math

import jax
import jax.numpy as jnp
from jax import lax
from jax.experimental import pallas as pl
from jax.experimental.pallas import tpu as pltpu

F32 = jnp.float32
BF16 = jnp.bfloat16
ACT = BF16

D_MODEL = 1024
MEM_HEADS = 4
HEAD = 128
GM_WIDTH = 512
GM_CHUNK = 128
GM_GROUPS = 4
MLA_HEADS = 8
MLA_ROPE = 64
Q_LORA = 384
KV_LORA = 256
D_FF = 4096
EPS = 1e-6
ROPE_BASE = 10000.0
MLA_SCALE = 1.0 / math.sqrt(HEAD + MLA_ROPE)
MEM_SCALE = 1.0 / math.sqrt(HEAD)
LOG2E = 1.4426950408889634
LN2 = 0.6931471805599453
ATT_TILE = 256
C_ZU, C_ZV, C_CQ, C_CKV, C_KPE, C_QM, C_ZG, C_END = 0, 512, 1024, 1408, 1664, 1728, 2240, 5312
Z_GM, Z_QM, Z_MLA, Z_KPE, Z_COLS = 3072, 4096, 4608, 5248, 5376
MLA_W = 768
QCAT = 2 * HEAD
ADAM_LR, ADAM_B1, ADAM_B2, ADAM_EPS, ADAM_WD, ADAM_STEP = 0.001, 0.9, 0.999, 1e-08, 0.01, 10
N_DEV = 8
LANES = 128
VMEM_LIMIT = 48 * 1024 * 1024
MAX_K_TILE = 8192
NEG = -1e30

BIG = ['w_in', 'w_uq', 'w_ukv', 'w_mem_kv', 'w_o_gm', 'w_o_mla', 'w_o_mem', 'w_out', 'w_ff1', 'w_ff2']
BIG_AXIS = {'w_in': 1, 'w_uq': 1, 'w_ukv': 1, 'w_mem_kv': 0, 'w_o_gm': 1, 'w_o_mla': 0, 'w_o_mem': 1,
            'w_out': 0, 'w_ff1': 1, 'w_ff2': 0}
BIG_SHAPE = {'w_in': (1024, 5312), 'w_uq': (384, 1536), 'w_ukv': (256, 2048), 'w_mem_kv': (1024, 1024),
             'w_o_gm': (512, 1024), 'w_o_mla': (1024, 1024), 'w_o_mem': (512, 1024), 'w_out': (1024, 1024),
             'w_ff1': (1024, 4096), 'w_ff2': (4096, 1024)}
SMALL = ['g_mix', 'g_cq', 'g_ckv', 'g_q_nope', 'g_q_pe', 'g_k_nope', 'g_k_pe', 'g_gm_ln', 'b_gm_ln',
         'w_spatial', 'b_spatial', 'g_mem', 'g_mq', 'g_mk', 'g_ffn']
WEIGHTS = ['g_mix', 'w_in', 'g_cq', 'w_uq', 'g_ckv', 'w_ukv', 'g_q_nope', 'g_q_pe', 'g_k_nope', 'g_k_pe',
           'g_gm_ln', 'b_gm_ln', 'w_spatial', 'b_spatial', 'g_mem', 'w_mem_kv', 'g_mq', 'g_mk', 'w_o_gm',
           'w_o_mla', 'w_o_mem', 'w_out', 'g_ffn', 'w_ff1', 'w_ff2']


def _pick(n, target, mult=LANES):
    best = None
    t = mult
    while t <= min(n, target):
        if n % t == 0:
            best = t
        t += mult
    return best if best is not None else n


def _params(sem):
    return pltpu.CompilerParams(dimension_semantics=sem, vmem_limit_bytes=VMEM_LIMIT)


MESH = pl.DeviceIdType.MESH
HBM_SPEC = pl.BlockSpec(memory_space=pltpu.HBM)


class _Exchange:
    def __init__(self, srcs, scatter):
        self.srcs, self.scatter = list(srcs), scatter
        self.out_shapes = [jax.ShapeDtypeStruct(s.shape if scatter else (N_DEV,) + s.shape, s.dtype) for s in self.srcs]
        n = len(self.srcs)
        self.scratch = [pltpu.SemaphoreType.DMA((n, N_DEV - 1)), pltpu.SemaphoreType.DMA((n, N_DEV - 1)),
                        pltpu.SemaphoreType.DMA((n,))]

    def _copies(self, src_refs, dst_refs, send_sems, recv_sems, local_sems):
        x, y, c = lax.axis_index("x"), lax.axis_index("y"), lax.axis_index("c")
        me = 4 * x + 2 * y + c
        local, remote = [], []
        for a, (src_ref, dst_ref) in enumerate(zip(src_refs, dst_refs)):
            def mine_for(dev, src_ref=src_ref):
                return src_ref.at[dev] if self.scatter else src_ref

            local.append(pltpu.make_async_copy(mine_for(me), dst_ref.at[me], local_sems.at[a]))
            for k in range(1, N_DEV):
                px = 1 - x if k & 4 else x
                py = 1 - y if k & 2 else y
                pc = 1 - c if k & 1 else c
                remote.append(pltpu.make_async_remote_copy(
                    src_ref=mine_for(4 * px + 2 * py + pc), dst_ref=dst_ref.at[me], send_sem=send_sems.at[a, k - 1],
                    recv_sem=recv_sems.at[a, k - 1], device_id=(px, py, pc), device_id_type=MESH))
        return local, remote

    def start(self, *refs):
        local, remote = self._copies(*refs)
        for cp in local + remote:
            cp.start()

    def wait(self, *refs):
        local, remote = self._copies(*refs)
        for cp in remote + local:
            cp.wait()


class _Gather2:
    def __init__(self, srcs):
        self.srcs = list(srcs)
        self.out_shapes = [jax.ShapeDtypeStruct((N_DEV,) + s.shape, s.dtype) for s in self.srcs]
        n = len(self.srcs)
        self.scratch = [pltpu.SemaphoreType.DMA((n, N_DEV - 1)), pltpu.SemaphoreType.DMA((n, N_DEV - 1)),
                        pltpu.SemaphoreType.DMA((n,))]

    def _plan(self, src_refs, dst_refs, send_sems, recv_sems, local_sems):
        x, y, c = lax.axis_index("x"), lax.axis_index("y"), lax.axis_index("c")
        chips = [(1 - x, y), (x, 1 - y), (1 - x, 1 - y)]
        plans = []
        for a, (src_ref, dst_ref) in enumerate(zip(src_refs, dst_refs)):
            def copy(k, block, to, src=None, a=a, dst_ref=dst_ref):
                at = dst_ref.at[4 * block[0] + 2 * block[1] + block[2]]
                return pltpu.make_async_remote_copy(src_ref=at if src is None else src, dst_ref=at,
                                                    send_sem=send_sems.at[a, k], recv_sem=recv_sems.at[a, k],
                                                    device_id=to, device_id_type=MESH)

            local = pltpu.make_async_copy(src_ref, dst_ref.at[4 * x + 2 * y + c], local_sems.at[a])
            first = [copy(0, (x, y, c), (x, y, 1 - c), src=src_ref)]
            first += [copy(1 + j, (x, y, c), (*chip, c), src=src_ref) for j, chip in enumerate(chips)]
            passed = [copy(4 + j, (*chip, c), (x, y, 1 - c)) for j, chip in enumerate(chips)]
            arrivals = [copy(1 + j, (*chip, c), (x, y, c)) for j, chip in enumerate(chips)]
            late = [copy(0, (x, y, 1 - c), (x, y, c))] + [copy(4 + j, (*chip, 1 - c), (x, y, c)) for j, chip in enumerate(chips)]
            plans.append((local, first, passed, arrivals, late))
        return plans

    def start(self, *refs):
        for local, first, _, _, _ in self._plan(*refs):
            local.start()
            for cp in first:
                cp.start()

    def wait(self, *refs):
        plans = self._plan(*refs)
        for _, _, passed, arrivals, _ in plans:
            for arrived, onward in zip(arrivals, passed):
                arrived.wait_recv()
                onward.start()
        for local, first, passed, _, late in plans:
            for cp in late:
                cp.wait_recv()
            for cp in first + passed:
                cp.wait_send()
            local.wait()


def _call(body, rider, ins, *, name, grid, in_specs, out_specs, out_shape, scratch_shapes, sem):
    if rider is None:
        return pl.pallas_call(body, name=name, grid=grid, in_specs=in_specs, out_specs=out_specs, out_shape=out_shape,
                              scratch_shapes=scratch_shapes, compiler_params=_params(sem))(*ins)
    single = not isinstance(out_shape, (list, tuple))
    own_specs, own_shapes = ([out_specs], [out_shape]) if single else (list(out_specs), list(out_shape))
    n_in, n_out, n_sc, n_r = len(ins), len(own_shapes), len(scratch_shapes), len(rider.srcs)
    n_all_in = n_in + n_r

    def carrying(*refs):
        own_in, srcs = refs[:n_in], refs[n_in:n_in + n_r]
        own_out, dsts = refs[n_all_in:n_all_in + n_out], refs[n_all_in + n_out:n_all_in + n_out + n_r]
        own_sc = refs[n_all_in + n_out + n_r:n_all_in + n_out + n_r + n_sc]
        sems = refs[n_all_in + n_out + n_r + n_sc:]
        first = last = None
        for d, steps in enumerate(grid):
            f, l = pl.program_id(d) == 0, pl.program_id(d) == steps - 1
            first, last = (f, l) if first is None else (first & f, last & l)

        @pl.when(first)
        def _():
            rider.start(srcs, dsts, *sems)

        body(*own_in, *own_out, *own_sc)

        @pl.when(last)
        def _():
            rider.wait(srcs, dsts, *sems)

    res = pl.pallas_call(
        carrying, name=name, grid=grid, in_specs=list(in_specs) + [HBM_SPEC] * n_r,
        out_specs=own_specs + [HBM_SPEC] * n_r, out_shape=own_shapes + rider.out_shapes,
        scratch_shapes=list(scratch_shapes) + rider.scratch, compiler_params=_params(("arbitrary",) * len(grid)),
    )(*ins, *rider.srcs)
    own = res[:n_out]
    return (own[0] if single else list(own)), list(res[n_out:])


def _matmul(a, b, mode, out_dtype, name, add=None, relu2_a=False, relu2_grad=None,
            tm_t=None, tn_t=None, tk_t=None, rider=None, m_rows=None, col_shards=None, sq_err_target=None):
    if mode == 'nn':
        (M, K), (K2, N) = a.shape, b.shape
    elif mode == 'nt':
        (M, K), (N, K2) = a.shape, b.shape
    else:
        (K, M), (K2, N) = a.shape, b.shape
    assert K == K2, (name, a.shape, b.shape)
    m_first = 0
    if m_rows is not None:
        assert mode == 'tn'
        m_first, M = m_rows
    if col_shards is not None:
        assert add is None and relu2_grad is None and sq_err_target is None and tn_t is None
        tn_t = N // col_shards
    if mode == 'tn':
        d_tm, d_tn, d_tk = 1024, 1024, 2048
    else:
        d_tm, d_tn, d_tk = (2048 if K <= 1024 else 1024), 512, MAX_K_TILE
    tm, tn, tk = _pick(M, tm_t or d_tm), _pick(N, tn_t or d_tn), _pick(K, tk_t or d_tk)
    gm, gn, nk = M // tm, N // tn, K // tk
    if mode == 'nn':
        a_spec = pl.BlockSpec((tm, tk), lambda i, j, k: (i, k))
        b_spec = pl.BlockSpec((tk, tn), lambda i, j, k: (k, j))
        dims = (((1,), (0,)), ((), ()))
    elif mode == 'nt':
        a_spec = pl.BlockSpec((tm, tk), lambda i, j, k: (i, k))
        b_spec = pl.BlockSpec((tn, tk), lambda i, j, k: (j, k))
        dims = (((1,), (1,)), ((), ()))
    else:
        assert m_first % tm == 0
        a_spec = pl.BlockSpec((tk, tm), lambda i, j, k: (k, m_first // tm + i))
        b_spec = pl.BlockSpec((tk, tn), lambda i, j, k: (k, j))
        dims = (((0,), (0,)), ((), ()))
    o_spec = pl.BlockSpec((tm, tn), lambda i, j, k: (i, j))
    has_add, has_e, has_t = add is not None, relu2_grad is not None, sq_err_target is not None
    assert not has_t or (nk == 1 and tn % LANES == 0)

    def body(*refs):
        a_ref, b_ref = refs[0], refs[1]
        pos = 2
        add_ref = e_ref = t_ref = None
        if has_add:
            add_ref = refs[pos]
            pos += 1
        if has_e:
            e_ref = refs[pos]
            pos += 1
        if has_t:
            t_ref = refs[pos]
            pos += 1
        o_ref = refs[pos]
        acc_ref = refs[pos + 1] if nk > 1 else None

        av = a_ref[...]
        if relu2_a:
            av = jnp.maximum(av, 0)
            av = av * av
        prod = lax.dot_general(av.astype(BF16), b_ref[...].astype(BF16), dims, preferred_element_type=F32)

        def finish(r):
            if has_add:
                r = r + add_ref[...]
            if has_e:
                r = r * (2.0 * jnp.maximum(e_ref[...].astype(F32), 0.0))
            if has_t:
                err = r - t_ref[...]
                r = err * (1.0 / N)
                refs[pos + 1][...] = r.astype(BF16)
                sq = err * err
                part = sq[:, 0:LANES]
                for c in range(1, tn // LANES):
                    part = part + sq[:, c * LANES:(c + 1) * LANES]
                _acc_rows(refs[pos + 2], part, (pl.program_id(0) == 0) & (pl.program_id(1) == 0))
            o_ref[...] = r.astype(out_dtype)

        if nk == 1:
            finish(prod)
        else:
            k = pl.program_id(2)

            @pl.when(k == 0)
            def _():
                acc_ref[...] = prod

            @pl.when(k > 0)
            def _():
                acc_ref[...] += prod

            @pl.when(k == nk - 1)
            def _():
                finish(acc_ref[...])

    ins, specs = [a, b], [a_spec, b_spec]
    if has_add:
        ins.append(add)
        specs.append(o_spec)
    if has_e:
        ins.append(relu2_grad)
        specs.append(o_spec)
    out_specs, out_shape, sem = o_spec, jax.ShapeDtypeStruct((M, N), out_dtype), ("parallel", "parallel", "arbitrary")
    if has_t:
        ins.append(sq_err_target)
        specs.append(o_spec)
        out_specs = [o_spec, o_spec, pl.BlockSpec((1, LANES), lambda i, j, k: (0, 0))]
        out_shape = [out_shape, jax.ShapeDtypeStruct((M, N), BF16), jax.ShapeDtypeStruct((1, LANES), F32)]
        sem = ("arbitrary", "arbitrary", "arbitrary")
    if col_shards is not None:
        out_specs = pl.BlockSpec((None, tm, tn), lambda i, j, k: (j, i, 0))
        out_shape = jax.ShapeDtypeStruct((col_shards, M, tn), out_dtype)
    return _call(body, rider, ins, name=name, grid=(gm, gn, nk), in_specs=specs, out_specs=out_specs, out_shape=out_shape,
                 scratch_shapes=[pltpu.VMEM((tm, tn), F32)] if nk > 1 else [], sem=sem)


ROW_BLOCK_BYTES = 12 * 1024 * 1024


def _row_tile(rows, row_bytes):
    return _pick(rows, max(16, min(1024, ROW_BLOCK_BYTES // row_bytes)), 16)


def _rowspec(tr, width, col=0):
    return pl.BlockSpec((tr, width), lambda i, col=col: (i, col))


def _fullspec(shape):
    nd = len(shape)
    return pl.BlockSpec(shape, lambda i, nd=nd: (0,) * nd)


def _rms(x, width):
    x = x.astype(F32)
    return lax.rsqrt(jnp.sum(x * x, axis=-1, keepdims=True) * (1.0 / width) + EPS)


def _rms_bwd_rows(x, g, dy, width):
    x, dy = x.astype(F32), dy.astype(F32)
    r = _rms(x, width)
    xh = x * r
    dn = dy * g
    dx = r * (dn - xh * (jnp.sum(dn * xh, axis=-1, keepdims=True) * (1.0 / width)))
    return dx, dy * xh


def _acc_rows(ref, val, first):
    s = jnp.sum(val, axis=0, keepdims=True)

    @pl.when(first)
    def _():
        ref[...] = s

    @pl.when(jnp.logical_not(first))
    def _():
        ref[...] += s


def _rms_fwd(x, g, name, rider=None):
    rows, width = x.shape
    tr = _row_tile(rows, 6 * width)

    def body(x_ref, g_ref, o_ref):
        xv = x_ref[...]
        o_ref[...] = (xv * _rms(xv, width) * g_ref[...]).astype(BF16)

    return _call(body, rider, [x, g], name=name, grid=(rows // tr,),
                 in_specs=[_rowspec(tr, width), _fullspec((1, width))], out_specs=_rowspec(tr, width),
                 out_shape=jax.ShapeDtypeStruct((rows, width), BF16), scratch_shapes=[], sem=("parallel",))


def _rms_bwd(x, g, dy, res, name, dx_dtypes=(F32,)):
    rows, width = x.shape
    tr = _row_tile(rows, 18 * width)
    has_res = res is not None
    n_in = 4 if has_res else 3

    def body(*refs):
        x_ref, g_ref, dy_ref = refs[:3]
        dx, dgv = _rms_bwd_rows(x_ref[...], g_ref[...], dy_ref[...], width)
        if has_res:
            dx = dx + refs[3][...]
        for ref, dt in zip(refs[n_in:], dx_dtypes):
            ref[...] = dx.astype(dt)
        _acc_rows(refs[-1], dgv, pl.program_id(0) == 0)

    ins = [x, g, dy] + ([res] if has_res else [])
    specs = [_rowspec(tr, width), _fullspec((1, width)), _rowspec(tr, width)] + ([_rowspec(tr, width)] if has_res else [])
    return pl.pallas_call(
        body, name=name, grid=(rows // tr,), in_specs=specs,
        out_specs=[_rowspec(tr, width)] * len(dx_dtypes) + [_fullspec((1, width))],
        out_shape=[jax.ShapeDtypeStruct((rows, width), dt) for dt in dx_dtypes] + [jax.ShapeDtypeStruct((1, width), F32)],
        compiler_params=_params(("arbitrary",)),
    )(*ins)


_GELU_C = math.sqrt(2.0 / math.pi)


def _gelu(x):
    t = jnp.tanh(_GELU_C * (x + 0.044715 * (x * x * x)))
    return 0.5 * x * (1.0 + t), t


def _gelu_grad(x, t):
    return 0.5 * (1.0 + t) + 0.5 * x * (1.0 - t * t) * (_GELU_C * (1.0 + 3.0 * 0.044715 * (x * x)))


def _gm_forward_rows(zu, zv, gln, bln, wc_ref, bst, n_chunk):
    u, tu = _gelu(zu)
    a, ta = _gelu(zv)
    mu = jnp.mean(a, axis=-1, keepdims=True)
    ac = a - mu
    rs = lax.rsqrt(jnp.mean(ac * ac, axis=-1, keepdims=True) + EPS)
    n = ac * rs
    v = n * gln + bln
    vb = v.astype(BF16)
    rows = []
    for c in range(n_chunk):
        cols = []
        for g in range(GM_GROUPS):
            vc = vb[c * GM_CHUNK:(c + 1) * GM_CHUNK, g * LANES:(g + 1) * LANES]
            mixed = jnp.dot(wc_ref[g], vc, preferred_element_type=F32) + bst[g]
            cols.append(mixed)
        rows.append(jnp.concatenate(cols, axis=1))
    mixed = jnp.concatenate(rows, axis=0) if n_chunk > 1 else rows[0]
    return u, tu, ta, n, rs, v, mixed


def _gm_fwd(z, gln, bln, wc, bst, name):
    rows = z.shape[0]
    tr = _pick(rows, 512, GM_CHUNK)
    n_chunk = tr // GM_CHUNK

    def body(zu_ref, zv_ref, gln_ref, bln_ref, wc_ref, bst_ref, o_ref):
        u, _, _, _, _, _, mixed = _gm_forward_rows(zu_ref[...].astype(F32), zv_ref[...].astype(F32), gln_ref[...], bln_ref[...], wc_ref,
                                                   bst_ref, n_chunk)
        o_ref[...] = (u * mixed).astype(BF16)

    return pl.pallas_call(
        body, name=name, grid=(rows // tr,),
        in_specs=[_rowspec(tr, GM_WIDTH, Z_GM // GM_WIDTH), _rowspec(tr, GM_WIDTH, Z_GM // GM_WIDTH + 1),_fullspec((1, GM_WIDTH)), _fullspec((1, GM_WIDTH)),
                  _fullspec((GM_GROUPS, GM_CHUNK, GM_CHUNK)), _fullspec((GM_GROUPS, GM_CHUNK, LANES))],
        out_specs=_rowspec(tr, GM_WIDTH), out_shape=jax.ShapeDtypeStruct((rows, GM_WIDTH), BF16),
        compiler_params=_params(("parallel",)),
    )(z, z, gln, bln, wc, bst)


ANY_SPEC = pl.BlockSpec(memory_space=pl.ANY)


def _gm_bwd(z, dy, gln, bln, wc, wct, bst, dz, name):
    rows = z.shape[0]
    tr = _pick(rows, 512, GM_CHUNK)
    n_chunk = tr // GM_CHUNK

    def body(zu_ref, zv_ref, dy_ref, gln_ref, bln_ref, wc_ref, wct_ref, bst_ref, _, dz_ref, dws_ref, dbs_ref, dgl_ref,
             dbl_ref):
        first = pl.program_id(0) == 0
        zu, zv, gln = zu_ref[...].astype(F32), zv_ref[...].astype(F32), gln_ref[...]
        u, tu, ta, n, rs, v, mixed = _gm_forward_rows(zu, zv, gln, bln_ref[...], wc_ref, bst_ref, n_chunk)
        dyv = dy_ref[...].astype(F32)
        dzu = dyv * mixed * _gelu_grad(zu, tu)
        dmix = dyv * u
        dmb = dmix.astype(BF16)
        vb = v.astype(BF16)
        dv_rows, dws, dbs = [], [None] * GM_GROUPS, None
        for c in range(n_chunk):
            rsl = slice(c * GM_CHUNK, (c + 1) * GM_CHUNK)
            cols = []
            for g in range(GM_GROUPS):
                csl = slice(g * LANES, (g + 1) * LANES)
                dmc = dmb[rsl, csl]
                cols.append(jnp.dot(wct_ref[g], dmc, preferred_element_type=F32))
                w_part = lax.dot_general(dmc, vb[rsl, csl], (((1,), (1,)), ((), ())), preferred_element_type=F32)
                dws[g] = w_part if dws[g] is None else dws[g] + w_part
            dv_rows.append(jnp.concatenate(cols, axis=1))
            dbs = dmix[rsl, :] if dbs is None else dbs + dmix[rsl, :]
        dv = jnp.concatenate(dv_rows, axis=0) if n_chunk > 1 else dv_rows[0]
        dn = dv * gln
        da = rs * (dn - jnp.mean(dn, axis=-1, keepdims=True) - n * jnp.mean(dn * n, axis=-1, keepdims=True))
        dzv = da * _gelu_grad(zv, ta)
        dz_ref[:, 0:GM_WIDTH] = dzu.astype(BF16)
        dz_ref[:, GM_WIDTH:2 * GM_WIDTH] = dzv.astype(BF16)
        _acc_rows(dgl_ref, dv * n, first)
        _acc_rows(dbl_ref, dv, first)

        @pl.when(first)
        def _():
            for g in range(GM_GROUPS):
                dws_ref[g] = dws[g]
            dbs_ref[...] = dbs

        @pl.when(jnp.logical_not(first))
        def _():
            for g in range(GM_GROUPS):
                dws_ref[g] += dws[g]
            dbs_ref[...] += dbs

    wspec = _fullspec((GM_GROUPS, GM_CHUNK, GM_CHUNK))
    return pl.pallas_call(
        body, name=name, grid=(rows // tr,),
        in_specs=[_rowspec(tr, GM_WIDTH, Z_GM // GM_WIDTH), _rowspec(tr, GM_WIDTH, Z_GM // GM_WIDTH + 1),
                  _rowspec(tr, GM_WIDTH), _fullspec((1, GM_WIDTH)), _fullspec((1, GM_WIDTH)), wspec, wspec, wspec, ANY_SPEC],
        out_specs=[_rowspec(tr, 2 * GM_WIDTH, Z_GM // (2 * GM_WIDTH)), wspec, _fullspec((GM_CHUNK, GM_WIDTH)),
                   _fullspec((1, GM_WIDTH)), _fullspec((1, GM_WIDTH))],
        out_shape=[jax.ShapeDtypeStruct(dz.shape, dz.dtype), jax.ShapeDtypeStruct((GM_GROUPS, GM_CHUNK, GM_CHUNK), F32),
                   jax.ShapeDtypeStruct((GM_CHUNK, GM_WIDTH), F32), jax.ShapeDtypeStruct((1, GM_WIDTH), F32),
                   jax.ShapeDtypeStruct((1, GM_WIDTH), F32)],
        input_output_aliases={8: 0}, compiler_params=_params(("arbitrary",)),
    )(z, z, dy, gln, bln, wc, wct, bst, dz)


def _lat_fwd(z, g_cq, g_ckv, name):
    rows = z.shape[0]
    tr = _row_tile(rows, 4 * MLA_W)

    def body(z_ref, gq_ref, gkv_ref, nq_ref, nkv_ref):
        zb = z_ref[...]
        cq, ckv = zb[:, 0:Q_LORA], zb[:, Q_LORA:Q_LORA + KV_LORA]
        nq_ref[...] = (cq * _rms(cq, Q_LORA) * gq_ref[...]).astype(BF16)
        nkv_ref[...] = (ckv * _rms(ckv, KV_LORA) * gkv_ref[...]).astype(BF16)

    return pl.pallas_call(
        body, name=name, grid=(rows // tr,),
        in_specs=[_rowspec(tr, MLA_W, Z_MLA // MLA_W), _fullspec((1, Q_LORA)), _fullspec((1, KV_LORA))],
        out_specs=[_rowspec(tr, Q_LORA), _rowspec(tr, KV_LORA)],
        out_shape=[jax.ShapeDtypeStruct((rows, Q_LORA), BF16), jax.ShapeDtypeStruct((rows, KV_LORA), BF16)],
        compiler_params=_params(("parallel",)),
    )(z, g_cq, g_ckv)


def _lat_bwd(z, dnq, dnkv, dkpe, g_cq, g_ckv, dz, name):
    rows = z.shape[0]
    tr = _row_tile(rows, 8 * MLA_W)

    def body(z_ref, dnq_ref, dnkv_ref, dkpe_ref, gq_ref, gkv_ref, _, dz_ref, dgq_ref, dgkv_ref):
        first = pl.program_id(0) == 0
        zb = z_ref[...]
        dcq, dgq = _rms_bwd_rows(zb[:, 0:Q_LORA], gq_ref[...], dnq_ref[...], Q_LORA)
        dckv, dgkv = _rms_bwd_rows(zb[:, Q_LORA:Q_LORA + KV_LORA], gkv_ref[...], dnkv_ref[...], KV_LORA)
        dz_ref[:, 0:Q_LORA] = dcq.astype(BF16)
        dz_ref[:, Q_LORA:Q_LORA + KV_LORA] = dckv.astype(BF16)
        dz_ref[:, Q_LORA + KV_LORA:MLA_W] = dkpe_ref[...].astype(BF16)
        _acc_rows(dgq_ref, dgq, first)
        _acc_rows(dgkv_ref, dgkv, first)

    return pl.pallas_call(
        body, name=name, grid=(rows // tr,),
        in_specs=[_rowspec(tr, MLA_W, Z_MLA // MLA_W), _rowspec(tr, Q_LORA), _rowspec(tr, KV_LORA), _rowspec(tr, LANES),
                  _fullspec((1, Q_LORA)), _fullspec((1, KV_LORA)), ANY_SPEC],
        out_specs=[_rowspec(tr, MLA_W, Z_MLA // MLA_W), _fullspec((1, Q_LORA)), _fullspec((1, KV_LORA))],
        out_shape=[jax.ShapeDtypeStruct(dz.shape, dz.dtype), jax.ShapeDtypeStruct((1, Q_LORA), F32),
                   jax.ShapeDtypeStruct((1, KV_LORA), F32)],
        input_output_aliases={6: 0}, compiler_params=_params(("arbitrary",)),
    )(z, dnq, dnkv, dkpe, g_cq, g_ckv, dz)


def _rope(y, cc, ss):
    return y * cc + pltpu.roll(y, 64, 1) * ss


def _rope_bwd(d, cc, ss):
    return d * cc + pltpu.roll(d * ss, 64, 1)


def _qk_fwd(q, kv, z, cc, ss, gqn, gqp, gkn, gkp, name):
    rows = q.shape[0]
    W = MLA_HEADS * HEAD
    tr = _row_tile(rows, 20 * W)
    QS = MLA_SCALE * LOG2E

    def body(q_ref, kv_ref, kpe_ref, cc_ref, ss_ref, gqn_ref, gqp_ref, gkn_ref, gkp_ref, qc_ref, kc_ref, v_ref):
        cc, ss = cc_ref[...], ss_ref[...]
        kpe = kpe_ref[...]
        kp = _rope(kpe * _rms(kpe, MLA_ROPE) * gkp_ref[...], cc, ss).astype(BF16)
        for h in range(MLA_HEADS):
            qn = q_ref[:, h * HEAD:(h + 1) * HEAD]
            qp = q_ref[:, W + h * HEAD:W + (h + 1) * HEAD]
            kn = kv_ref[:, h * HEAD:(h + 1) * HEAD]
            qc_ref[:, h * QCAT:h * QCAT + HEAD] = (qn * _rms(qn, HEAD) * gqn_ref[...] * QS).astype(BF16)
            qc_ref[:, h * QCAT + HEAD:(h + 1) * QCAT] = (_rope(qp * _rms(qp, MLA_ROPE) * gqp_ref[...], cc, ss) * QS).astype(BF16)
            kc_ref[:, h * QCAT:h * QCAT + HEAD] = (kn * _rms(kn, HEAD) * gkn_ref[...]).astype(BF16)
            kc_ref[:, h * QCAT + HEAD:(h + 1) * QCAT] = kp
        v_ref[...] = kv_ref[:, W:2 * W].astype(BF16)

    g = _fullspec((1, HEAD))
    return _call(
        body, None, [q, kv, z, cc, ss, gqn, gqp, gkn, gkp], name=name, grid=(rows // tr,),
        in_specs=[_rowspec(tr, 2 * W), _rowspec(tr, 2 * W), _rowspec(tr, LANES, Z_KPE // LANES), _rowspec(tr, LANES),
                  _rowspec(tr, LANES), g, g, g, g],
        out_specs=[_rowspec(tr, MLA_HEADS * QCAT), _rowspec(tr, MLA_HEADS * QCAT), _rowspec(tr, W)],
        out_shape=[jax.ShapeDtypeStruct((rows, MLA_HEADS * QCAT), BF16), jax.ShapeDtypeStruct((rows, MLA_HEADS * QCAT), BF16),
                   jax.ShapeDtypeStruct((rows, W), BF16)],
        scratch_shapes=[], sem=("parallel",))


def _qk_bwd(q, kv, z, cc, ss, gqn, gqp, gkn, gkp, dqc, dkc, dv, name):
    rows = q.shape[0]
    W = MLA_HEADS * HEAD
    tr = _row_tile(rows, 40 * W)

    def body(q_ref, kv_ref, kpe_ref, cc_ref, ss_ref, gqn_ref, gqp_ref, gkn_ref, gkp_ref, dqc_ref, dkc_ref, dv_ref,
             dq_ref, dkv_ref, dkpe_ref, dgqn_ref, dgqp_ref, dgkn_ref, dgkp_ref):
        first = pl.program_id(0) == 0
        cc, ss = cc_ref[...], ss_ref[...]
        sqn = sqp = skn = dkp = None
        for h in range(MLA_HEADS):
            dx, dg = _rms_bwd_rows(q_ref[:, h * HEAD:(h + 1) * HEAD], gqn_ref[...], dqc_ref[:, h * QCAT:h * QCAT + HEAD], HEAD)
            dq_ref[:, h * HEAD:(h + 1) * HEAD] = dx.astype(BF16)
            sqn = dg if sqn is None else sqn + dg
            dy = _rope_bwd(dqc_ref[:, h * QCAT + HEAD:(h + 1) * QCAT], cc, ss)
            dx, dg = _rms_bwd_rows(q_ref[:, W + h * HEAD:W + (h + 1) * HEAD], gqp_ref[...], dy, MLA_ROPE)
            dq_ref[:, W + h * HEAD:W + (h + 1) * HEAD] = dx.astype(BF16)
            sqp = dg if sqp is None else sqp + dg
            dx, dg = _rms_bwd_rows(kv_ref[:, h * HEAD:(h + 1) * HEAD], gkn_ref[...], dkc_ref[:, h * QCAT:h * QCAT + HEAD], HEAD)
            dkv_ref[:, h * HEAD:(h + 1) * HEAD] = dx.astype(BF16)
            skn = dg if skn is None else skn + dg
            part = dkc_ref[:, h * QCAT + HEAD:(h + 1) * QCAT].astype(F32)
            dkp = part if dkp is None else dkp + part
        dkv_ref[:, W:2 * W] = dv_ref[...].astype(BF16)
        dx, dg = _rms_bwd_rows(kpe_ref[...], gkp_ref[...], _rope_bwd(dkp, cc, ss), MLA_ROPE)
        dkpe_ref[...] = dx
        _acc_rows(dgqn_ref, sqn, first)
        _acc_rows(dgqp_ref, sqp, first)
        _acc_rows(dgkn_ref, skn, first)
        _acc_rows(dgkp_ref, dg, first)

    g = _fullspec((1, HEAD))
    gs = jax.ShapeDtypeStruct((1, HEAD), F32)
    return pl.pallas_call(
        body, name=name, grid=(rows // tr,),
        in_specs=[_rowspec(tr, 2 * W), _rowspec(tr, 2 * W), _rowspec(tr, LANES, Z_KPE // LANES), _rowspec(tr, LANES),
                  _rowspec(tr, LANES), g, g, g, g, _rowspec(tr, MLA_HEADS * QCAT), _rowspec(tr, MLA_HEADS * QCAT),
                  _rowspec(tr, W)],
        out_specs=[_rowspec(tr, 2 * W), _rowspec(tr, 2 * W), _rowspec(tr, LANES), g, g, g, g],
        out_shape=[jax.ShapeDtypeStruct((rows, 2 * W), BF16), jax.ShapeDtypeStruct((rows, 2 * W), BF16),
                   jax.ShapeDtypeStruct((rows, LANES), F32), gs, gs, gs, gs],
        compiler_params=_params(("arbitrary",)),
    )(q, kv, z, cc, ss, gqn, gqp, gkn, gkp, dqc, dkc, dv)


def _headnorm_fwd(x, col, nheads, g, out_scale, name):
    rows = x.shape[0]
    W = nheads * HEAD
    tr = _row_tile(rows, 6 * W)

    def body(x_ref, g_ref, o_ref):
        for h in range(nheads):
            xv = x_ref[:, h * HEAD:(h + 1) * HEAD]
            o_ref[:, h * HEAD:(h + 1) * HEAD] = (xv * _rms(xv, HEAD) * g_ref[...] * out_scale).astype(BF16)

    return pl.pallas_call(
        body, name=name, grid=(rows // tr,),
        in_specs=[_rowspec(tr, W, col), _fullspec((1, HEAD))], out_specs=_rowspec(tr, W),
        out_shape=jax.ShapeDtypeStruct((rows, W), BF16), compiler_params=_params(("parallel",)),
    )(x, g)


def _headnorm_bwd(x, col, nheads, g, dy, tail, name, into=None):
    rows = x.shape[0]
    W = nheads * HEAD
    tr = _row_tile(rows, 12 * W)
    has_tail = tail is not None
    WO = 2 * W if has_tail else W

    def body(*refs):
        if into is not None:
            x_ref, g_ref, dy_ref, _, dx_ref, dg_ref = refs
        elif has_tail:
            x_ref, g_ref, dy_ref, t_ref, dx_ref, dg_ref = refs
        else:
            x_ref, g_ref, dy_ref, dx_ref, dg_ref = refs
        acc = None
        for h in range(nheads):
            sl = slice(h * HEAD, (h + 1) * HEAD)
            dx, dg = _rms_bwd_rows(x_ref[:, sl], g_ref[...], dy_ref[:, sl], HEAD)
            dx_ref[:, sl] = dx.astype(BF16)
            acc = dg if acc is None else acc + dg
        if has_tail:
            dx_ref[:, W:2 * W] = t_ref[...].astype(BF16)
        _acc_rows(dg_ref, acc, pl.program_id(0) == 0)

    ins = [x, g, dy] + ([tail] if has_tail else [])
    specs = [_rowspec(tr, W, col), _fullspec((1, HEAD)), _rowspec(tr, W)] + ([_rowspec(tr, W)] if has_tail else [])
    dx_spec, dx_shape, aliases = _rowspec(tr, WO), jax.ShapeDtypeStruct((rows, WO), BF16), {}
    if into is not None:
        assert not has_tail
        ins, specs = ins + [into[0]], specs + [ANY_SPEC]
        dx_spec, dx_shape, aliases = _rowspec(tr, W, into[1]), jax.ShapeDtypeStruct(into[0].shape, into[0].dtype), {3: 0}
    return pl.pallas_call(
        body, name=name, grid=(rows // tr,), in_specs=specs,
        out_specs=[dx_spec, _fullspec((1, HEAD))], out_shape=[dx_shape, jax.ShapeDtypeStruct((1, HEAD), F32)],
        input_output_aliases=aliases, compiler_params=_params(("arbitrary",)),
    )(*ins)


def _sigmoid(x):
    return 1.0 / (1.0 + jnp.exp(-x.astype(F32)))


def _merge_fwd(z, y_gm, y_mla, y_mem, name):
    rows = z.shape[0]
    tr = _row_tile(rows, 14 * D_MODEL)

    def body(g0_ref, g1_ref, g2_ref, a_ref, b_ref, c_ref, o_ref):
        m = _sigmoid(g0_ref[...]) * a_ref[...] + _sigmoid(g1_ref[...]) * b_ref[...] + _sigmoid(g2_ref[...]) * c_ref[...]
        o_ref[...] = m.astype(BF16)

    r = _rowspec(tr, D_MODEL)
    return pl.pallas_call(
        body, name=name, grid=(rows // tr,),
        in_specs=[_rowspec(tr, D_MODEL, 0), _rowspec(tr, D_MODEL, 1), _rowspec(tr, D_MODEL, 2),r, r, r],
        out_specs=r, out_shape=jax.ShapeDtypeStruct((rows, D_MODEL), BF16), compiler_params=_params(("parallel",)),
    )(z, z, z, y_gm, y_mla, y_mem)


def _merge_bwd(z, y_gm, y_mla, y_mem, dm, name):
    rows = z.shape[0]
    tr = _row_tile(rows, 24 * D_MODEL)

    def body(g0_ref, g1_ref, g2_ref, a_ref, b_ref, c_ref, dm_ref, da_ref, db_ref, dc_ref, dzg_ref):
        dmv = dm_ref[...].astype(F32)
        for k, (g_ref, y_ref, dy_ref) in enumerate(((g0_ref, a_ref, da_ref), (g1_ref, b_ref, db_ref), (g2_ref, c_ref, dc_ref))):
            s = _sigmoid(g_ref[...])
            dy_ref[...] = (dmv * s).astype(BF16)
            dzg_ref[:, k * D_MODEL:(k + 1) * D_MODEL] = (dmv * y_ref[...] * s * (1.0 - s)).astype(BF16)

    r = _rowspec(tr, D_MODEL)
    o = jax.ShapeDtypeStruct((rows, D_MODEL), BF16)
    return pl.pallas_call(
        body, name=name, grid=(rows // tr,),
        in_specs=[_rowspec(tr, D_MODEL, 0), _rowspec(tr, D_MODEL, 1), _rowspec(tr, D_MODEL, 2),r, r, r, r],
        out_specs=[r, r, r, _rowspec(tr, 3 * D_MODEL, 0)],
        out_shape=[o, o, o, jax.ShapeDtypeStruct((rows, Z_COLS), BF16)],
        compiler_params=_params(("parallel",)),
    )(z, z, z, y_gm, y_mla, y_mem, dm)


_NT = (((1,), (1,)), ((), ()))
_TN = (((0,), (0,)), ((), ()))


def _diag_mask(s):
    row = lax.broadcasted_iota(jnp.int32, s.shape, 0)
    col = lax.broadcasted_iota(jnp.int32, s.shape, 1)
    return jnp.where(row >= col, s, NEG)


def _attn_fwd(q, k, v, nb, nheads, dk, v_col0, causal, name, rider=None):
    S, Skv = q.shape[0] // nb, k.shape[0] // nb
    tq = _pick(Skv, ATT_TILE) if causal else _pick(S, 4 * ATT_TILE)
    nq = S // tq

    def body(q_ref, k_ref, v_ref, o_ref, lse_ref):
        for i in range(nq):
            r0 = i * tq
            qb = q_ref[r0:r0 + tq, :]
            if causal:
                spans = ([(0, r0, False)] if i > 0 else []) + [(r0, r0 + tq, True)]
            else:
                spans = [(0, Skv, False)]
            scores = []
            for a, b, masked in spans:
                s = lax.dot_general(qb, k_ref[a:b, :], _NT, preferred_element_type=F32)
                scores.append(_diag_mask(s) if masked else s)
            m = functools.reduce(jnp.maximum, [jnp.max(s, axis=-1, keepdims=True) for s in scores])
            l = acc = None
            for s, (a, b, _) in zip(scores, spans):
                p = jnp.exp2(s - m)
                lp = jnp.sum(p, axis=-1, keepdims=True)
                ap = jnp.dot(p.astype(BF16), v_ref[a:b, :].astype(BF16), preferred_element_type=F32)
                l, acc = (lp, ap) if l is None else (l + lp, acc + ap)
            o_ref[r0:r0 + tq, :] = (acc / l).astype(BF16)
            lse_ref[r0:r0 + tq, :] = m + jnp.log2(l)

    ins = [q, k, v]
    in_specs = [pl.BlockSpec((S, dk), lambda b, h: (b, h)), pl.BlockSpec((Skv, dk), lambda b, h: (b, h)),
                pl.BlockSpec((Skv, HEAD), lambda b, h: (b, v_col0 + h))]
    out_specs = [pl.BlockSpec((S, HEAD), lambda b, h: (b, h)), pl.BlockSpec((None, S, 1), lambda b, h: (h, b, 0))]
    out_shape = [jax.ShapeDtypeStruct((nb * S, nheads * HEAD), BF16), jax.ShapeDtypeStruct((nheads, nb * S, 1), F32)]
    return _call(body, rider, ins, name=name, grid=(nb, nheads), in_specs=in_specs, out_specs=out_specs,
                 out_shape=out_shape, scratch_shapes=[], sem=("parallel", "parallel"))


def _attn_bwd(q, k, v, o, do, lse, nb, nheads, dk, v_col0, scale, causal, name, rider=None):
    S, Skv = q.shape[0] // nb, k.shape[0] // nb
    tk = _pick(Skv, ATT_TILE)
    nkv = Skv // tk

    def body(q_ref, k_ref, v_ref, o_ref, do_ref, lse_ref, dq_ref, dk_ref, dv_ref, delta_ref, dob_ref, dqa_ref):
        dov = do_ref[...]
        delta_ref[...] = jnp.sum(o_ref[...].astype(F32) * dov.astype(F32), axis=-1, keepdims=True)
        dob_ref[...] = dov.astype(BF16)

        for j in range(nkv):
            c0 = j * tk
            kb = k_ref[c0:c0 + tk, :]
            vb = v_ref[c0:c0 + tk, :].astype(BF16)
            if causal:
                spans = [(c0, c0 + tk, True)] + ([(c0 + tk, S, False)] if c0 + tk < S else [])
            else:
                spans = [(0, S, False)]
            dk_acc = dv_acc = None
            for a, b, masked in spans:
                qb = q_ref[a:b, :]
                dob = dob_ref[a:b, :]
                s = lax.dot_general(qb, kb, _NT, preferred_element_type=F32)
                if masked:
                    s = _diag_mask(s)
                p = jnp.exp2(s - lse_ref[a:b, :])
                dp = lax.dot_general(dob, vb, _NT, preferred_element_type=F32)
                ds = (p * (dp - delta_ref[a:b, :])).astype(BF16)
                dv_p = lax.dot_general(p.astype(BF16), dob, _TN, preferred_element_type=F32)
                dk_p = lax.dot_general(ds, qb, _TN, preferred_element_type=F32)
                dk_acc, dv_acc = (dk_p, dv_p) if dk_acc is None else (dk_acc + dk_p, dv_acc + dv_p)
                dq_p = jnp.dot(ds, kb, preferred_element_type=F32) * scale
                if j == 0:
                    dqa_ref[a:b, :] = dq_p
                else:
                    dqa_ref[a:b, :] += dq_p
            dk_ref[c0:c0 + tk, :] = (dk_acc * LN2).astype(BF16)
            dv_ref[c0:c0 + tk, :] = dv_acc.astype(BF16)
        dq_ref[...] = dqa_ref[...].astype(BF16)

    ins = [q, k, v, o, do, lse]
    in_specs = [pl.BlockSpec((S, dk), lambda b, h: (b, h)), pl.BlockSpec((Skv, dk), lambda b, h: (b, h)),
                pl.BlockSpec((Skv, HEAD), lambda b, h: (b, v_col0 + h)), pl.BlockSpec((S, HEAD), lambda b, h: (b, h)),
                pl.BlockSpec((S, HEAD), lambda b, h: (b, h)), pl.BlockSpec((None, S, 1), lambda b, h: (h, b, 0))]
    out_specs = [pl.BlockSpec((S, dk), lambda b, h: (b, h)), pl.BlockSpec((Skv, dk), lambda b, h: (b, h)),
                 pl.BlockSpec((Skv, HEAD), lambda b, h: (b, h))]
    out_shape = [jax.ShapeDtypeStruct((nb * S, nheads * dk), BF16), jax.ShapeDtypeStruct((nb * Skv, nheads * dk), BF16),
                 jax.ShapeDtypeStruct((nb * Skv, nheads * HEAD), BF16)]
    return _call(body, rider, ins, name=name, grid=(nb, nheads), in_specs=in_specs, out_specs=out_specs,
                 out_shape=out_shape,
                 scratch_shapes=[pltpu.VMEM((S, 1), F32), pltpu.VMEM((S, HEAD), BF16), pltpu.VMEM((S, dk), F32)],
                 sem=("parallel", "parallel"))


def _spread_rope(a):
    zero = jnp.zeros(a.shape[:-1] + (32,), a.dtype)
    return jnp.concatenate([a[..., :32], zero, a[..., 32:], zero], axis=-1)


def _gather_rope(a):
    return jnp.concatenate([a[..., 0:32], a[..., 64:96]], axis=-1)


def _win_layout(w):
    return jnp.concatenate([w[:, C_ZG:C_END], w[:, C_ZU:C_CQ], w[:, C_QM:C_ZG], w[:, C_CQ:C_CKV], w[:, C_CKV:C_KPE],
                            _spread_rope(w[:, C_KPE:C_QM])], axis=1)


def _win_unlayout(d):
    return jnp.concatenate([d[:, Z_GM:Z_QM], d[:, Z_MLA:Z_MLA + Q_LORA], d[:, Z_MLA + Q_LORA:Z_KPE],
                            _gather_rope(d[:, Z_KPE:Z_COLS]), d[:, Z_QM:Z_MLA], d[:, 0:Z_GM]], axis=1)


def _wuq_layout(w):
    r = w.reshape(Q_LORA, MLA_HEADS, HEAD + MLA_ROPE)
    return jnp.concatenate([r[:, :, :HEAD].reshape(Q_LORA, -1), _spread_rope(r[:, :, HEAD:]).reshape(Q_LORA, -1)], axis=1)


def _wuq_unlayout(d):
    n = d[:, :MLA_HEADS * HEAD].reshape(Q_LORA, MLA_HEADS, HEAD)
    p = _gather_rope(d[:, MLA_HEADS * HEAD:].reshape(Q_LORA, MLA_HEADS, HEAD))
    return jnp.concatenate([n, p], axis=-1).reshape(Q_LORA, -1)


def _wukv_layout(w):
    r = w.reshape(KV_LORA, MLA_HEADS, 2 * HEAD)
    return jnp.concatenate([r[:, :, :HEAD].reshape(KV_LORA, -1), r[:, :, HEAD:].reshape(KV_LORA, -1)], axis=1)


def _wukv_unlayout(d):
    k = d[:, :MLA_HEADS * HEAD].reshape(KV_LORA, MLA_HEADS, HEAD)
    v = d[:, MLA_HEADS * HEAD:].reshape(KV_LORA, MLA_HEADS, HEAD)
    return jnp.concatenate([k, v], axis=-1).reshape(KV_LORA, -1)


AG_MID = ['w_uq', 'w_ukv', 'w_mem_kv', 'w_o_gm', 'w_o_mla', 'w_o_mem', 'w_out']
AG_FFN = ['w_ff1', 'w_ff2']
RS_GROUPS = {'ffn_proj': ['w_ff2', 'w_ff1', 'w_out', 'w_o_gm', 'w_o_mla', 'w_o_mem'],
             'lat': ['w_uq', 'w_ukv', 'w_mem_kv'], 'in_top': ['w_in'], 'in_bot': ['w_in']}


def _unride(res, rider):
    return (res, None) if rider is None else res


def _local_step(x, mem, positions, target, P, ws):
    B, S, _ = x.shape
    M = mem.shape[1]
    T = B * S
    x2d = x.reshape(T, D_MODEL)
    mem2d = mem.reshape(B * M, D_MODEL)
    tgt2d = target.reshape(T, D_MODEL)

    def row(v):
        return v.reshape(1, -1).astype(F32)

    inv_freq = ROPE_BASE ** (-jnp.arange(0, MLA_ROPE, 2, dtype=F32) / MLA_ROPE)
    zero = jnp.zeros_like(inv_freq)
    ang = positions.reshape(T).astype(F32)[:, None] * jnp.concatenate([inv_freq, zero, inv_freq, zero])
    cc = jnp.cos(ang) * jnp.concatenate([zero + 1.0, zero, zero + 1.0, zero])
    ss = jnp.sin(ang) * jnp.concatenate([zero - 1.0, zero, zero + 1.0, zero])

    g_mix, g_cq, g_ckv, g_ffn, g_mem = row(P['g_mix']), row(P['g_cq']), row(P['g_ckv']), row(P['g_ffn']), row(P['g_mem'])
    gqn, gkn, gmq, gmk = row(P['g_q_nope']), row(P['g_k_nope']), row(P['g_mq']), row(P['g_mk'])
    gqp, gkp = _spread_rope(row(P['g_q_pe'])), _spread_rope(row(P['g_k_pe']))
    gln, bln = row(P['g_gm_ln']), row(P['b_gm_ln'])
    wc = jnp.tril(P['w_spatial'].astype(F32))
    wct = jnp.swapaxes(wc, 1, 2).astype(BF16)
    wc = wc.astype(BF16)
    bst = jnp.broadcast_to(P['b_spatial'].astype(F32)[:, :, None], (GM_GROUPS, GM_CHUNK, LANES))

    ride = ws.gather(['w_in'])
    h, got = _unride(_rms_fwd(x2d, g_mix, "rms_mix", rider=ride), ride)
    w_in = _win_layout(ws.gathered(['w_in'], got)['w_in']).astype(BF16)
    ride = ws.gather(AG_MID)
    z, got = _unride(_matmul(h, w_in, 'nn', ACT, "mm_in", tn_t=768, rider=ride), ride)
    mid = ws.gathered(AG_MID, got)
    w_uq, w_ukv = _wuq_layout(mid['w_uq']).astype(BF16), _wukv_layout(mid['w_ukv']).astype(BF16)
    w_mem_kv, w_o_gm, w_o_mla, w_o_mem, w_out = (mid[n] for n in ('w_mem_kv', 'w_o_gm', 'w_o_mla', 'w_o_mem', 'w_out'))
    ygm_pre = _gm_fwd(z, gln, bln, wc, bst, "gm_fwd")
    y_gm = _matmul(ygm_pre, w_o_gm, 'nn', ACT, "mm_o_gm")
    nq, nkv = _lat_fwd(z, g_cq, g_ckv, "lat_fwd")
    q = _matmul(nq, w_uq, 'nn', ACT, "mm_uq")
    kv = _matmul(nkv, w_ukv, 'nn', ACT, "mm_ukv")
    qcat, kcat, vv = _qk_fwd(q, kv, z, cc, ss, gqn, gqp, gkn, gkp, "qk_fwd")
    ride = ws.gather(AG_FFN)
    (o, lse), got = _unride(_attn_fwd(qcat, kcat, vv, B, MLA_HEADS, QCAT, 0, True, "mla_attn_fwd", rider=ride), ride)
    ffn = ws.gathered(AG_FFN, got)
    w_ff1, w_ff2 = ffn['w_ff1'], ffn['w_ff2']
    y_mla = _matmul(o, w_o_mla, 'nn', ACT, "mm_o_mla")
    nm = _rms_fwd(mem2d, g_mem, "rms_mem")
    kvm = _matmul(nm, w_mem_kv, 'nn', ACT, "mm_mem_kv")
    qm = _headnorm_fwd(z, Z_QM // (MEM_HEADS * HEAD), MEM_HEADS, gmq, MEM_SCALE * LOG2E, "memq_fwd")
    km = _headnorm_fwd(kvm, 0, MEM_HEADS, gmk, 1.0, "memk_fwd")
    om, lse_m = _attn_fwd(qm, km, kvm, B, MEM_HEADS, HEAD, MEM_HEADS, False, "mem_attn_fwd")
    y_mem = _matmul(om, w_o_mem, 'nn', ACT, "mm_o_mem")
    merged = _merge_fwd(z, y_gm, y_mla, y_mem, "merge_fwd")
    x1 = _matmul(merged, w_out, 'nn', F32, "mm_out", add=x2d)
    h2 = _rms_fwd(x1, g_ffn, "rms_ffn")
    a1 = _matmul(h2, w_ff1, 'nn', BF16, "mm_ff1")
    dx2, dx2b, loss_part = _matmul(a1, w_ff2, 'nn', F32, "mm_ff2", add=x1, relu2_a=True, sq_err_target=tgt2d)

    G = {}
    d_ff2 = _matmul(a1, dx2b, 'tn', BF16, "mm_d_ff2", relu2_a=True)
    da1 = _matmul(dx2b, w_ff2, 'nt', BF16, "mm_da1", relu2_grad=a1)
    d_ff1 = _matmul(h2, da1, 'tn', BF16, "mm_d_ff1", col_shards=N_DEV)
    dh2 = _matmul(da1, w_ff1, 'nt', ACT, "mm_dh2")
    dx1, dx1b, G['g_ffn'] = _rms_bwd(x1, g_ffn, dh2, dx2, "rms_ffn_bwd", dx_dtypes=(F32, BF16))
    d_out = _matmul(merged, dx1b, 'tn', BF16, "mm_d_out")
    dmerged = _matmul(dx1b, w_out, 'nt', ACT, "mm_dmerged")
    dy_gm, dy_mla, dy_mem, dz = _merge_bwd(z, y_gm, y_mla, y_mem, dmerged, "merge_bwd")
    d_o_gm = _matmul(ygm_pre, dy_gm, 'tn', BF16, "mm_d_o_gm")
    d_o_mla = _matmul(o, dy_mla, 'tn', BF16, "mm_d_o_mla")
    d_o_mem = _matmul(om, dy_mem, 'tn', BF16, "mm_d_o_mem")
    dygm_pre = _matmul(dy_gm, w_o_gm, 'nt', ACT, "mm_dygm")
    dz, dws, dbs, G['g_gm_ln'], G['b_gm_ln'] = _gm_bwd(z, dygm_pre, gln, bln, wc, wct, bst, dz, "gm_bwd")
    G['w_spatial'] = jnp.tril(dws)
    G['b_spatial'] = jnp.sum(dbs.reshape(GM_CHUNK, GM_GROUPS, LANES), axis=-1).T
    do = _matmul(dy_mla, w_o_mla, 'nt', ACT, "mm_do")
    ride = ws.scatter('ffn_proj', {'w_ff2': d_ff2, 'w_ff1': d_ff1, 'w_out': d_out, 'w_o_gm': d_o_gm, 'w_o_mla': d_o_mla,
                                   'w_o_mem': d_o_mem})
    (dqc, dkc, dvv), got = _unride(_attn_bwd(qcat, kcat, vv, o, do, lse, B, MLA_HEADS, QCAT, 0, MLA_SCALE, True,
                                             "mla_attn_bwd", rider=ride), ride)
    ws.scattered('ffn_proj', got)
    dq, dkv, dkpe, G['g_q_nope'], dgqp, G['g_k_nope'], dgkp = _qk_bwd(q, kv, z, cc, ss, gqn, gqp, gkn, gkp, dqc, dkc, dvv,
                                                                     "qk_bwd")
    G['g_q_pe'], G['g_k_pe'] = _gather_rope(dgqp), _gather_rope(dgkp)
    d_uq = _wuq_unlayout(_matmul(nq, dq, 'tn', BF16, "mm_d_uq"))
    dnq = _matmul(dq, w_uq, 'nt', ACT, "mm_dnq")
    d_ukv = _wukv_unlayout(_matmul(nkv, dkv, 'tn', BF16, "mm_d_ukv"))
    dnkv = _matmul(dkv, w_ukv, 'nt', ACT, "mm_dnkv")
    dz, G['g_cq'], G['g_ckv'] = _lat_bwd(z, dnq, dnkv, dkpe, g_cq, g_ckv, dz, "lat_bwd")
    dom = _matmul(dy_mem, w_o_mem, 'nt', ACT, "mm_dom")
    dqm, dkm, dvm = _attn_bwd(qm, km, kvm, om, dom, lse_m, B, MEM_HEADS, HEAD, MEM_HEADS, MEM_SCALE, False, "mem_attn_bwd")
    dz, G['g_mq'] = _headnorm_bwd(z, Z_QM // (MEM_HEADS * HEAD), MEM_HEADS, gmq, dqm, None, "memq_bwd",
                                  into=(dz, Z_QM // (MEM_HEADS * HEAD)))
    dkvm, G['g_mk'] = _headnorm_bwd(kvm, 0, MEM_HEADS, gmk, dkm, dvm, "memk_bwd")
    d_mem_kv = _matmul(nm, dkvm, 'tn', BF16, "mm_d_mem_kv")
    dnm = _matmul(dkvm, w_mem_kv, 'nt', ACT, "mm_dnm")
    G['g_mem'], = _rms_bwd(mem2d, g_mem, dnm, None, "rms_mem_bwd", dx_dtypes=())
    half = D_MODEL // 2
    ride = ws.scatter('lat', {'w_uq': d_uq, 'w_ukv': d_ukv, 'w_mem_kv': d_mem_kv})
    d_top, got = _unride(_matmul(h, dz, 'tn', BF16, "mm_d_in_top", tn_t=768, m_rows=(0, half), rider=ride), ride)
    ws.scattered('lat', got)
    ride = ws.scatter('in_top', {'w_in': _win_unlayout(d_top)})
    d_bot, got = _unride(_matmul(h, dz, 'tn', BF16, "mm_d_in_bot", tn_t=768, m_rows=(half, half), rider=ride), ride)
    ws.scattered('in_top', got)
    ride = ws.scatter('in_bot', {'w_in': _win_unlayout(d_bot)})
    dh, got = _unride(_matmul(dz, w_in, 'nt', ACT, "mm_dh", rider=ride), ride)
    ws.scattered('in_bot', got)
    gx, G['g_mix'] = _rms_bwd(x2d, g_mix, dh, dx1, "rms_mix_bwd")
    return loss_part, gx.reshape(B, S, D_MODEL), G


def _all_gather8(xs, name):
    def body(x_ref, out_ref, send_sems, recv_sems, local_sem):
        x, y, c = lax.axis_index("x"), lax.axis_index("y"), lax.axis_index("c")
        me, sibling = (x, y, c), (x, y, 1 - c)
        chips = [(1 - x, y), (x, 1 - y), (1 - x, 1 - y)]

        def rows(px, py, pc):
            return out_ref.at[4 * px + 2 * py + pc]

        def copy(k, block, to, src=None):
            return pltpu.make_async_remote_copy(
                src_ref=rows(*block) if src is None else src, dst_ref=rows(*block),
                send_sem=send_sems.at[k], recv_sem=recv_sems.at[k], device_id=to, device_id_type=MESH)

        mine = pltpu.make_async_copy(x_ref, rows(*me), local_sem)
        mine.start()
        first = [copy(0, me, sibling, src=x_ref)]
        first += [copy(1 + j, me, (*chip, c), src=x_ref) for j, chip in enumerate(chips)]
        for cp in first:
            cp.start()
        passed = [copy(4 + j, (*chip, c), sibling) for j, chip in enumerate(chips)]
        for j, chip in enumerate(chips):
            copy(1 + j, (*chip, c), me).wait_recv()
            passed[j].start()
        copy(0, sibling, me).wait_recv()
        for j, chip in enumerate(chips):
            copy(4 + j, (*chip, 1 - c), me).wait_recv()
        for cp in first + passed:
            cp.wait_send()
        mine.wait()

    return pl.pallas_call(
        body, name=name, in_specs=[HBM_SPEC], out_specs=HBM_SPEC,
        out_shape=jax.ShapeDtypeStruct((N_DEV,) + xs.shape, xs.dtype),
        scratch_shapes=[pltpu.SemaphoreType.DMA((7,)), pltpu.SemaphoreType.DMA((7,)), pltpu.SemaphoreType.DMA],
    )(xs)


def _adamw_rows(w, g, m, v):
    m2 = ADAM_B1 * m + (1.0 - ADAM_B1) * g
    v2 = ADAM_B2 * v + (1.0 - ADAM_B2) * (g * g)
    m_hat = m2 / (1.0 - ADAM_B1 ** ADAM_STEP)
    v_hat = v2 / (1.0 - ADAM_B2 ** ADAM_STEP)
    delta = -ADAM_LR * (m_hat / (jnp.sqrt(v_hat) + ADAM_EPS) + ADAM_WD * w)
    return delta, m2, v2


def _sum_adamw(parts, w, m, v, name):
    rows, cols = w.shape
    assert sum(p.shape[1] for p in parts) == rows
    tr = _pick(min(p.shape[1] for p in parts), max(16, 65536 // cols), 16)
    n = parts[0].shape[0]
    counts = [p.shape[1] // tr for p in parts]
    starts = [sum(counts[:k]) for k in range(len(parts))]

    def body(*refs):
        p_refs = refs[:len(parts)]
        w_ref, m_ref, v_ref, g_ref, d_ref, m2_ref, v2_ref = refs[len(parts):]
        g = None
        for p_ref, start in zip(p_refs, starts):
            gk = p_ref[0].astype(F32)
            for k in range(1, n):
                gk = gk + p_ref[k].astype(F32)
            g = gk if g is None else jnp.where(pl.program_id(0) >= start, gk, g)
        delta, m2, v2 = _adamw_rows(w_ref[...], g, m_ref[...], v_ref[...])
        g_ref[...] = g
        d_ref[...] = delta
        m2_ref[...] = m2
        v2_ref[...] = v2

    flat = pl.BlockSpec((tr, cols), lambda i: (i, 0))
    out = jax.ShapeDtypeStruct((rows, cols), F32)
    p_specs = [pl.BlockSpec((n, tr, cols), lambda i, s=s, c=c: (0, jnp.clip(i - s, 0, c - 1), 0))
               for s, c in zip(starts, counts)]
    return pl.pallas_call(
        body, name=name, grid=(rows // tr,), in_specs=p_specs + [flat, flat, flat], out_specs=[flat] * 4,
        out_shape=[out] * 4, compiler_params=_params(("parallel",)),
    )(*parts, w, m, v)


def _small_rows(name):
    n = {'g_mix': 1024, 'g_cq': 384, 'g_ckv': 256, 'g_q_nope': 128, 'g_q_pe': 64, 'g_k_nope': 128, 'g_k_pe': 64,
         'g_gm_ln': 512, 'b_gm_ln': 512, 'w_spatial': GM_GROUPS * GM_CHUNK * GM_CHUNK, 'b_spatial': GM_GROUPS * GM_CHUNK,
         'g_mem': 1024, 'g_mq': 128, 'g_mk': 128, 'g_ffn': 1024}[name]
    return n, -(-n // (8 * LANES)) * 8


def _small_slab(d):
    parts = []
    for name in SMALL:
        n, rows = _small_rows(name)
        parts.append(jnp.pad(d[name].reshape(-1).astype(F32), (0, rows * LANES - n)).reshape(rows, LANES))
    return jnp.concatenate(parts, axis=0)


def _small_unslab(slab, like):
    out, r = {}, 0
    for name in SMALL:
        n, rows = _small_rows(name)
        out[name] = slab[r:r + rows].reshape(-1)[:n].reshape(like[name].shape)
        r += rows
    return out


def _full_from_gathered(gathered, name):
    r, c = BIG_SHAPE[name]
    if BIG_AXIS[name] == 0:
        return gathered.reshape(r, c)
    return gathered.transpose(1, 0, 2).reshape(r, c)


def _shards_of_full(g, name):
    if g.ndim == 3:
        return g
    r, c = BIG_SHAPE[name]
    if BIG_AXIS[name] == 0:
        return g.reshape(N_DEV, r // N_DEV, c)
    return g.reshape(g.shape[0], N_DEV, c // N_DEV).transpose(1, 0, 2)


class _DistWeights:
    def __init__(self, shards):
        self.shards = shards
        self.received = {}

    def gather(self, names):
        return _Gather2([self.shards[n].astype(BF16) for n in names])

    def gathered(self, names, got):
        return {n: _full_from_gathered(g, n) for n, g in zip(names, got)}

    def scatter(self, key, grads):
        return _Exchange([_shards_of_full(grads[n], n) for n in RS_GROUPS[key]], scatter=True)

    def scattered(self, key, got):
        for n, g in zip(RS_GROUPS[key], got):
            self.received.setdefault(n, []).append(g)


def kernel(x, mem, positions, g_mix, w_in, g_cq, w_uq, g_ckv, w_ukv, g_q_nope, g_q_pe, g_k_nope, g_k_pe, g_gm_ln, b_gm_ln, w_spatial, b_spatial, g_mem, w_mem_kv, g_mq, g_mk, w_o_gm, w_o_mla, w_o_mem, w_out, g_ffn, w_ff1, w_ff2, loss_target, m_g_mix, m_w_in, m_g_cq, m_w_uq, m_g_ckv, m_w_ukv, m_g_q_nope, m_g_q_pe, m_g_k_nope, m_g_k_pe, m_g_gm_ln, m_b_gm_ln, m_w_spatial, m_b_spatial, m_g_mem, m_w_mem_kv, m_g_mq, m_g_mk, m_w_o_gm, m_w_o_mla, m_w_o_mem, m_w_out, m_g_ffn, m_w_ff1, m_w_ff2, v_g_mix, v_w_in, v_g_cq, v_w_uq, v_g_ckv, v_w_ukv, v_g_q_nope, v_g_q_pe, v_g_k_nope, v_g_k_pe, v_g_gm_ln, v_b_gm_ln, v_w_spatial, v_b_spatial, v_g_mem, v_w_mem_kv, v_g_mq, v_g_mk, v_w_o_gm, v_w_o_mla, v_w_o_mem, v_w_out, v_g_ffn, v_w_ff1, v_w_ff2):
    given = dict(locals())
    w = {n: given[n][0] for n in WEIGHTS}
    mom = {n: given['m_' + n][0] for n in WEIGHTS}
    var = {n: given['v_' + n][0] for n in WEIGHTS}

    ws = _DistWeights({n: w[n] for n in BIG})
    loss_part, grad_x, G = _local_step(x, mem, positions, loss_target, {n: w[n] for n in SMALL}, ws)

    outs = {}
    for n in BIG:
        for prefix, res in zip(("grad_", "delta_", "new_m_", "new_v_"),
                               _sum_adamw(ws.received[n], w[n], mom[n], var[n], "adamw_" + n)):
            outs[prefix + n] = res[None]

    tail = jnp.zeros((8, LANES), F32)
    parts = _all_gather8(jnp.concatenate([_small_slab(G), jnp.pad(loss_part, ((0, 7), (0, 0)))], axis=0), "ag_small")
    small = _sum_adamw([parts], *[jnp.concatenate([_small_slab(d), tail], axis=0) for d in (w, mom, var)], "adamw_small")
    loss = 0.5 * jnp.sum(small[0][-8:]) / D_MODEL
    for prefix, small_slab in zip(("grad_", "delta_", "new_m_", "new_v_"), small):
        sm = _small_unslab(small_slab, given)
        for n in SMALL:
            outs[prefix + n] = sm[n]
    return (loss, grad_x, *[outs[p + n] for p in ("grad_", "delta_", "new_m_", "new_v_") for n in WEIGHTS])
```

```python
import functools
import math

import jax
import jax.numpy as jnp
from jax import lax
from jax.experimental import pallas as pl
from jax.experimental.pallas import tpu as pltpu

F32 = jnp.float32
BF16 = jnp.bfloat16
ACT = BF16

D_MODEL = 1024
MEM_HEADS = 4
HEAD = 128
GM_WIDTH = 512
GM_CHUNK = 128
GM_GROUPS = 4
MLA_HEADS = 8
MLA_ROPE = 64
Q_LORA = 384
KV_LORA = 256
D_FF = 4096
EPS = 1e-6
ROPE_BASE = 10000.0
MLA_SCALE = 1.0 / math.sqrt(HEAD + MLA_ROPE)
MEM_SCALE = 1.0 / math.sqrt(HEAD)
LOG2E = 1.4426950408889634
LN2 = 0.6931471805599453
ATT_TILE = 256
C_ZU, C_ZV, C_CQ, C_CKV, C_KPE, C_QM, C_ZG, C_END = 0, 512, 1024, 1408, 1664, 1728, 2240, 5312
Z_GM, Z_QM, Z_MLA, Z_KPE, Z_COLS = 3072, 4096, 4608, 5248, 5376
MLA_W = 768
QCAT = 2 * HEAD
ADAM_LR, ADAM_B1, ADAM_B2, ADAM_EPS, ADAM_WD, ADAM_STEP = 0.001, 0.9, 0.999, 1e-08, 0.01, 10
N_DEV = 8
LANES = 128
VMEM_LIMIT = 48 * 1024 * 1024
MAX_K_TILE = 8192
NEG = -1e30

BIG = ['w_in', 'w_uq', 'w_ukv', 'w_mem_kv', 'w_o_gm', 'w_o_mla', 'w_o_mem', 'w_out', 'w_ff1', 'w_ff2']
BIG_AXIS = {'w_in': 1, 'w_uq': 1, 'w_ukv': 1, 'w_mem_kv': 0, 'w_o_gm': 1, 'w_o_mla': 0, 'w_o_mem': 1,
            'w_out': 0, 'w_ff1': 1, 'w_ff2': 0}
BIG_SHAPE = {'w_in': (1024, 5312), 'w_uq': (384, 1536), 'w_ukv': (256, 2048), 'w_mem_kv': (1024, 1024),
             'w_o_gm': (512, 1024), 'w_o_mla': (1024, 1024), 'w_o_mem': (512, 1024), 'w_out': (1024, 1024),
             'w_ff1': (1024, 4096), 'w_ff2': (4096, 1024)}
SMALL = ['g_mix', 'g_cq', 'g_ckv', 'g_q_nope', 'g_q_pe', 'g_k_nope', 'g_k_pe', 'g_gm_ln', 'b_gm_ln',
         'w_spatial', 'b_spatial', 'g_mem', 'g_mq', 'g_mk', 'g_ffn']
WEIGHTS = ['g_mix', 'w_in', 'g_cq', 'w_uq', 'g_ckv', 'w_ukv', 'g_q_nope', 'g_q_pe', 'g_k_nope', 'g_k_pe',
           'g_gm_ln', 'b_gm_ln', 'w_spatial', 'b_spatial', 'g_mem', 'w_mem_kv', 'g_mq', 'g_mk', 'w_o_gm',
           'w_o_mla', 'w_o_mem', 'w_out', 'g_ffn', 'w_ff1', 'w_ff2']


def _pick(n, target, mult=LANES):
    best = None
    t = mult
    while t <= min(n, target):
        if n % t == 0:
            best = t
        t += mult
    return best if best is not None else n


def _params(sem):
    return pltpu.CompilerParams(dimension_semantics=sem, vmem_limit_bytes=VMEM_LIMIT)


MESH = pl.DeviceIdType.MESH
HBM_SPEC = pl.BlockSpec(memory_space=pltpu.HBM)


class _Exchange:
    def __init__(self, srcs, scatter):
        self.srcs, self.scatter = list(srcs), scatter
        self.out_shapes = [jax.ShapeDtypeStruct(s.shape if scatter else (N_DEV,) + s.shape, s.dtype) for s in self.srcs]
        n = len(self.srcs)
        self.scratch = [pltpu.SemaphoreType.DMA((n, N_DEV - 1)), pltpu.SemaphoreType.DMA((n, N_DEV - 1)),
                        pltpu.SemaphoreType.DMA((n,))]

    def _copies(self, src_refs, dst_refs, send_sems, recv_sems, local_sems):
        x, y, c = lax.axis_index("x"), lax.axis_index("y"), lax.axis_index("c")
        me = 4 * x + 2 * y + c
        local, remote = [], []
        for a, (src_ref, dst_ref) in enumerate(zip(src_refs, dst_refs)):
            def mine_for(dev, src_ref=src_ref):
                return src_ref.at[dev] if self.scatter else src_ref

            local.append(pltpu.make_async_copy(mine_for(me), dst_ref.at[me], local_sems.at[a]))
            for k in range(1, N_DEV):
                px = 1 - x if k & 4 else x
                py = 1 - y if k & 2 else y
                pc = 1 - c if k & 1 else c
                remote.append(pltpu.make_async_remote_copy(
                    src_ref=mine_for(4 * px + 2 * py + pc), dst_ref=dst_ref.at[me], send_sem=send_sems.at[a, k - 1],
                    recv_sem=recv_sems.at[a, k - 1], device_id=(px, py, pc), device_id_type=MESH))
        return local, remote

    def start(self, *refs):
        local, remote = self._copies(*refs)
        for cp in local + remote:
            cp.start()

    def wait(self, *refs):
        local, remote = self._copies(*refs)
        for cp in remote + local:
            cp.wait()


class _Gather2:
    def __init__(self, srcs):
        self.srcs = list(srcs)
        self.out_shapes = [jax.ShapeDtypeStruct((N_DEV,) + s.shape, s.dtype) for s in self.srcs]
        n = len(self.srcs)
        self.scratch = [pltpu.SemaphoreType.DMA((n, N_DEV - 1)), pltpu.SemaphoreType.DMA((n, N_DEV - 1)),
                        pltpu.SemaphoreType.DMA((n,))]

    def _plan(self, src_refs, dst_refs, send_sems, recv_sems, local_sems):
        x, y, c = lax.axis_index("x"), lax.axis_index("y"), lax.axis_index("c")
        chips = [(1 - x, y), (x, 1 - y), (1 - x, 1 - y)]
        plans = []
        for a, (src_ref, dst_ref) in enumerate(zip(src_refs, dst_refs)):
            def copy(k, block, to, src=None, a=a, dst_ref=dst_ref):
                at = dst_ref.at[4 * block[0] + 2 * block[1] + block[2]]
                return pltpu.make_async_remote_copy(src_ref=at if src is None else src, dst_ref=at,
                                                    send_sem=send_sems.at[a, k], recv_sem=recv_sems.at[a, k],
                                                    device_id=to, device_id_type=MESH)

            local = pltpu.make_async_copy(src_ref, dst_ref.at[4 * x + 2 * y + c], local_sems.at[a])
            first = [copy(0, (x, y, c), (x, y, 1 - c), src=src_ref)]
            first += [copy(1 + j, (x, y, c), (*chip, c), src=src_ref) for j, chip in enumerate(chips)]
            passed = [copy(4 + j, (*chip, c), (x, y, 1 - c)) for j, chip in enumerate(chips)]
            arrivals = [copy(1 + j, (*chip, c), (x, y, c)) for j, chip in enumerate(chips)]
            late = [copy(0, (x, y, 1 - c), (x, y, c))] + [copy(4 + j, (*chip, 1 - c), (x, y, c)) for j, chip in enumerate(chips)]
            plans.append((local, first, passed, arrivals, late))
        return plans

    def start(self, *refs):
        for local, first, _, _, _ in self._plan(*refs):
            local.start()
            for cp in first:
                cp.start()

    def wait(self, *refs):
        plans = self._plan(*refs)
        for _, _, passed, arrivals, _ in plans:
            for arrived, onward in zip(arrivals, passed):
                arrived.wait_recv()
                onward.start()
        for local, first, passed, _, late in plans:
            for cp in late:
                cp.wait_recv()
            for cp in first + passed:
                cp.wait_send()
            local.wait()


def _call(body, rider, ins, *, name, grid, in_specs, out_specs, out_shape, scratch_shapes, sem):
    if rider is None:
        return pl.pallas_call(body, name=name, grid=grid, in_specs=in_specs, out_specs=out_specs, out_shape=out_shape,
                              scratch_shapes=scratch_shapes, compiler_params=_params(sem))(*ins)
    single = not isinstance(out_shape, (list, tuple))
    own_specs, own_shapes = ([out_specs], [out_shape]) if single else (list(out_specs), list(out_shape))
    n_in, n_out, n_sc, n_r = len(ins), len(own_shapes), len(scratch_shapes), len(rider.srcs)
    n_all_in = n_in + n_r

    def carrying(*refs):
        own_in, srcs = refs[:n_in], refs[n_in:n_in + n_r]
        own_out, dsts = refs[n_all_in:n_all_in + n_out], refs[n_all_in + n_out:n_all_in + n_out + n_r]
        own_sc = refs[n_all_in + n_out + n_r:n_all_in + n_out + n_r + n_sc]
        sems = refs[n_all_in + n_out + n_r + n_sc:]
        first = last = None
        for d, steps in enumerate(grid):
            f, l = pl.program_id(d) == 0, pl.program_id(d) == steps - 1
            first, last = (f, l) if first is None else (first & f, last & l)

        @pl.when(first)
        def _():
            rider.start(srcs, dsts, *sems)

        body(*own_in, *own_out, *own_sc)

        @pl.when(last)
        def _():
            rider.wait(srcs, dsts, *sems)

    res = pl.pallas_call(
        carrying, name=name, grid=grid, in_specs=list(in_specs) + [HBM_SPEC] * n_r,
        out_specs=own_specs + [HBM_SPEC] * n_r, out_shape=own_shapes + rider.out_shapes,
        scratch_shapes=list(scratch_shapes) + rider.scratch, compiler_params=_params(("arbitrary",) * len(grid)),
    )(*ins, *rider.srcs)
    own = res[:n_out]
    return (own[0] if single else list(own)), list(res[n_out:])


def _matmul(a, b, mode, out_dtype, name, add=None, relu2_a=False, relu2_grad=None,
            tm_t=None, tn_t=None, tk_t=None, rider=None, m_rows=None, col_shards=None, sq_err_target=None):
    if mode == 'nn':
        (M, K), (K2, N) = a.shape, b.shape
    elif mode == 'nt':
        (M, K), (N, K2) = a.shape, b.shape
    else:
        (K, M), (K2, N) = a.shape, b.shape
    assert K == K2, (name, a.shape, b.shape)
    m_first = 0
    if m_rows is not None:
        assert mode == 'tn'
        m_first, M = m_rows
    if col_shards is not None:
        assert add is None and relu2_grad is None and sq_err_target is None and tn_t is None
        tn_t = N // col_shards
    if mode == 'tn':
        d_tm, d_tn, d_tk = 1024, 1024, 2048
    else:
        d_tm, d_tn, d_tk = (2048 if K <= 1024 else 1024), 512, MAX_K_TILE
    tm, tn, tk = _pick(M, tm_t or d_tm), _pick(N, tn_t or d_tn), _pick(K, tk_t or d_tk)
    gm, gn, nk = M // tm, N // tn, K // tk
    if mode == 'nn':
        a_spec = pl.BlockSpec((tm, tk), lambda i, j, k: (i, k))
        b_spec = pl.BlockSpec((tk, tn), lambda i, j, k: (k, j))
        dims = (((1,), (0,)), ((), ()))
    elif mode == 'nt':
        a_spec = pl.BlockSpec((tm, tk), lambda i, j, k: (i, k))
        b_spec = pl.BlockSpec((tn, tk), lambda i, j, k: (j, k))
        dims = (((1,), (1,)), ((), ()))
    else:
        assert m_first % tm == 0
        a_spec = pl.BlockSpec((tk, tm), lambda i, j, k: (k, m_first // tm + i))
        b_spec = pl.BlockSpec((tk, tn), lambda i, j, k: (k, j))
        dims = (((0,), (0,)), ((), ()))
    o_spec = pl.BlockSpec((tm, tn), lambda i, j, k: (i, j))
    has_add, has_e, has_t = add is not None, relu2_grad is not None, sq_err_target is not None
    assert not has_t or (nk == 1 and tn % LANES == 0)

    def body(*refs):
        a_ref, b_ref = refs[0], refs[1]
        pos = 2
        add_ref = e_ref = t_ref = None
        if has_add:
            add_ref = refs[pos]
            pos += 1
        if has_e:
            e_ref = refs[pos]
            pos += 1
        if has_t:
            t_ref = refs[pos]
            pos += 1
        o_ref = refs[pos]
        acc_ref = refs[pos + 1] if nk > 1 else None

        av = a_ref[...]
        if relu2_a:
            av = jnp.maximum(av, 0)
            av = av * av
        prod = lax.dot_general(av.astype(BF16), b_ref[...].astype(BF16), dims, preferred_element_type=F32)

        def finish(r):
            if has_add:
                r = r + add_ref[...]
            if has_e:
                r = r * (2.0 * jnp.maximum(e_ref[...].astype(F32), 0.0))
            if has_t:
                err = r - t_ref[...]
                r = err * (1.0 / N)
                refs[pos + 1][...] = r.astype(BF16)
                sq = err * err
                part = sq[:, 0:LANES]
                for c in range(1, tn // LANES):
                    part = part + sq[:, c * LANES:(c + 1) * LANES]
                _acc_rows(refs[pos + 2], part, (pl.program_id(0) == 0) & (pl.program_id(1) == 0))
            o_ref[...] = r.astype(out_dtype)

        if nk == 1:
            finish(prod)
        else:
            k = pl.program_id(2)

            @pl.when(k == 0)
            def _():
                acc_ref[...] = prod

            @pl.when(k > 0)
            def _():
                acc_ref[...] += prod

            @pl.when(k == nk - 1)
            def _():
                finish(acc_ref[...])

    ins, specs = [a, b], [a_spec, b_spec]
    if has_add:
        ins.append(add)
        specs.append(o_spec)
    if has_e:
        ins.append(relu2_grad)
        specs.append(o_spec)
    out_specs, out_shape, sem = o_spec, jax.ShapeDtypeStruct((M, N), out_dtype), ("parallel", "parallel", "arbitrary")
    if has_t:
        ins.append(sq_err_target)
        specs.append(o_spec)
        out_specs = [o_spec, o_spec, pl.BlockSpec((1, LANES), lambda i, j, k: (0, 0))]
        out_shape = [out_shape, jax.ShapeDtypeStruct((M, N), BF16), jax.ShapeDtypeStruct((1, LANES), F32)]
        sem = ("arbitrary", "arbitrary", "arbitrary")
    if col_shards is not None:
        out_specs = pl.BlockSpec((None, tm, tn), lambda i, j, k: (j, i, 0))
        out_shape = jax.ShapeDtypeStruct((col_shards, M, tn), out_dtype)
    return _call(body, rider, ins, name=name, grid=(gm, gn, nk), in_specs=specs, out_specs=out_specs, out_shape=out_shape,
                 scratch_shapes=[pltpu.VMEM((tm, tn), F32)] if nk > 1 else [], sem=sem)


ROW_BLOCK_BYTES = 12 * 1024 * 1024


def _row_tile(rows, row_bytes):
    return _pick(rows, max(16, min(1024, ROW_BLOCK_BYTES // row_bytes)), 16)


def _rowspec(tr, width, col=0):
    return pl.BlockSpec((tr, width), lambda i, col=col: (i, col))


def _fullspec(shape):
    nd = len(shape)
    return pl.BlockSpec(shape, lambda i, nd=nd: (0,) * nd)


def _rms(x, width):
    x = x.astype(F32)
    return lax.rsqrt(jnp.sum(x * x, axis=-1, keepdims=True) * (1.0 / width) + EPS)


def _rms_bwd_rows(x, g, dy, width):
    x, dy = x.astype(F32), dy.astype(F32)
    r = _rms(x, width)
    xh = x * r
    dn = dy * g
    dx = r * (dn - xh * (jnp.sum(dn * xh, axis=-1, keepdims=True) * (1.0 / width)))
    return dx, dy * xh


def _acc_rows(ref, val, first):
    s = jnp.sum(val, axis=0, keepdims=True)

    @pl.when(first)
    def _():
        ref[...] = s

    @pl.when(jnp.logical_not(first))
    def _():
        ref[...] += s


def _rms_fwd(x, g, name, rider=None):
    rows, width = x.shape
    tr = _row_tile(rows, 6 * width)

    def body(x_ref, g_ref, o_ref):
        xv = x_ref[...]
        o_ref[...] = (xv * _rms(xv, width) * g_ref[...]).astype(BF16)

    return _call(body, rider, [x, g], name=name, grid=(rows // tr,),
                 in_specs=[_rowspec(tr, width), _fullspec((1, width))], out_specs=_rowspec(tr, width),
                 out_shape=jax.ShapeDtypeStruct((rows, width), BF16), scratch_shapes=[], sem=("parallel",))


def _rms_bwd(x, g, dy, res, name, dx_dtypes=(F32,)):
    rows, width = x.shape
    tr = _row_tile(rows, 18 * width)
    has_res = res is not None
    n_in = 4 if has_res else 3

    def body(*refs):
        x_ref, g_ref, dy_ref = refs[:3]
        dx, dgv = _rms_bwd_rows(x_ref[...], g_ref[...], dy_ref[...], width)
        if has_res:
            dx = dx + refs[3][...]
        for ref, dt in zip(refs[n_in:], dx_dtypes):
            ref[...] = dx.astype(dt)
        _acc_rows(refs[-1], dgv, pl.program_id(0) == 0)

    ins = [x, g, dy] + ([res] if has_res else [])
    specs = [_rowspec(tr, width), _fullspec((1, width)), _rowspec(tr, width)] + ([_rowspec(tr, width)] if has_res else [])
    return pl.pallas_call(
        body, name=name, grid=(rows // tr,), in_specs=specs,
        out_specs=[_rowspec(tr, width)] * len(dx_dtypes) + [_fullspec((1, width))],
        out_shape=[jax.ShapeDtypeStruct((rows, width), dt) for dt in dx_dtypes] + [jax.ShapeDtypeStruct((1, width), F32)],
        compiler_params=_params(("arbitrary",)),
    )(*ins)


_GELU_C = math.sqrt(2.0 / math.pi)


def _gelu(x):
    t = jnp.tanh(_GELU_C * (x + 0.044715 * (x * x * x)))
    return 0.5 * x * (1.0 + t), t


def _gelu_grad(x, t):
    return 0.5 * (1.0 + t) + 0.5 * x * (1.0 - t * t) * (_GELU_C * (1.0 + 3.0 * 0.044715 * (x * x)))


def _gm_forward_rows(zu, zv, gln, bln, wc_ref, bst, n_chunk):
    u, tu = _gelu(zu)
    a, ta = _gelu(zv)
    mu = jnp.mean(a, axis=-1, keepdims=True)
    ac = a - mu
    rs = lax.rsqrt(jnp.mean(ac * ac, axis=-1, keepdims=True) + EPS)
    n = ac * rs
    v = n * gln + bln
    vb = v.astype(BF16)
    rows = []
    for c in range(n_chunk):
        cols = []
        for g in range(GM_GROUPS):
            vc = vb[c * GM_CHUNK:(c + 1) * GM_CHUNK, g * LANES:(g + 1) * LANES]
            mixed = jnp.dot(wc_ref[g], vc, preferred_element_type=F32) + bst[g]
            cols.append(mixed)
        rows.append(jnp.concatenate(cols, axis=1))
    mixed = jnp.concatenate(rows, axis=0) if n_chunk > 1 else rows[0]
    return u, tu, ta, n, rs, v, mixed


def _gm_fwd(z, gln, bln, wc, bst, name):
    rows = z.shape[0]
    tr = _pick(rows, 512, GM_CHUNK)
    n_chunk = tr // GM_CHUNK

    def body(zu_ref, zv_ref, gln_ref, bln_ref, wc_ref, bst_ref, o_ref):
        u, _, _, _, _, _, mixed = _gm_forward_rows(zu_ref[...].astype(F32), zv_ref[...].astype(F32), gln_ref[...], bln_ref[...], wc_ref,
                                                   bst_ref, n_chunk)
        o_ref[...] = (u * mixed).astype(BF16)

    return pl.pallas_call(
        body, name=name, grid=(rows // tr,),
        in_specs=[_rowspec(tr, GM_WIDTH, Z_GM // GM_WIDTH), _rowspec(tr, GM_WIDTH, Z_GM // GM_WIDTH + 1),_fullspec((1, GM_WIDTH)), _fullspec((1, GM_WIDTH)),
                  _fullspec((GM_GROUPS, GM_CHUNK, GM_CHUNK)), _fullspec((GM_GROUPS, GM_CHUNK, LANES))],
        out_specs=_rowspec(tr, GM_WIDTH), out_shape=jax.ShapeDtypeStruct((rows, GM_WIDTH), BF16),
        compiler_params=_params(("parallel",)),
    )(z, z, gln, bln, wc, bst)


ANY_SPEC = pl.BlockSpec(memory_space=pl.ANY)


def _gm_bwd(z, dy, gln, bln, wc, wct, bst, dz, name):
    rows = z.shape[0]
    tr = _pick(rows, 512, GM_CHUNK)
    n_chunk = tr // GM_CHUNK

    def body(zu_ref, zv_ref, dy_ref, gln_ref, bln_ref, wc_ref, wct_ref, bst_ref, _, dz_ref, dws_ref, dbs_ref, dgl_ref,
             dbl_ref):
        first = pl.program_id(0) == 0
        zu, zv, gln = zu_ref[...].astype(F32), zv_ref[...].astype(F32), gln_ref[...]
        u, tu, ta, n, rs, v, mixed = _gm_forward_rows(zu, zv, gln, bln_ref[...], wc_ref, bst_ref, n_chunk)
        dyv = dy_ref[...].astype(F32)
        dzu = dyv * mixed * _gelu_grad(zu, tu)
        dmix = dyv * u
        dmb = dmix.astype(BF16)
        vb = v.astype(BF16)
        dv_rows, dws, dbs = [], [None] * GM_GROUPS, None
        for c in range(n_chunk):
            rsl = slice(c * GM_CHUNK, (c + 1) * GM_CHUNK)
            cols = []
            for g in range(GM_GROUPS):
                csl = slice(g * LANES, (g + 1) * LANES)
                dmc = dmb[rsl, csl]
                cols.append(jnp.dot(wct_ref[g], dmc, preferred_element_type=F32))
                w_part = lax.dot_general(dmc, vb[rsl, csl], (((1,), (1,)), ((), ())), preferred_element_type=F32)
                dws[g] = w_part if dws[g] is None else dws[g] + w_part
            dv_rows.append(jnp.concatenate(cols, axis=1))
            dbs = dmix[rsl, :] if dbs is None else dbs + dmix[rsl, :]
        dv = jnp.concatenate(dv_rows, axis=0) if n_chunk > 1 else dv_rows[0]
        dn = dv * gln
        da = rs * (dn - jnp.mean(dn, axis=-1, keepdims=True) - n * jnp.mean(dn * n, axis=-1, keepdims=True))
        dzv = da * _gelu_grad(zv, ta)
        dz_ref[:, 0:GM_WIDTH] = dzu.astype(BF16)
        dz_ref[:, GM_WIDTH:2 * GM_WIDTH] = dzv.astype(BF16)
        _acc_rows(dgl_ref, dv * n, first)
        _acc_rows(dbl_ref, dv, first)

        @pl.when(first)
        def _():
            for g in range(GM_GROUPS):
                dws_ref[g] = dws[g]
            dbs_ref[...] = dbs

        @pl.when(jnp.logical_not(first))
        def _():
            for g in range(GM_GROUPS):
                dws_ref[g] += dws[g]
            dbs_ref[...] += dbs

    wspec = _fullspec((GM_GROUPS, GM_CHUNK, GM_CHUNK))
    return pl.pallas_call(
        body, name=name, grid=(rows // tr,),
        in_specs=[_rowspec(tr, GM_WIDTH, Z_GM // GM_WIDTH), _rowspec(tr, GM_WIDTH, Z_GM // GM_WIDTH + 1),
                  _rowspec(tr, GM_WIDTH), _fullspec((1, GM_WIDTH)), _fullspec((1, GM_WIDTH)), wspec, wspec, wspec, ANY_SPEC],
        out_specs=[_rowspec(tr, 2 * GM_WIDTH, Z_GM // (2 * GM_WIDTH)), wspec, _fullspec((GM_CHUNK, GM_WIDTH)),
                   _fullspec((1, GM_WIDTH)), _fullspec((1, GM_WIDTH))],
        out_shape=[jax.ShapeDtypeStruct(dz.shape, dz.dtype), jax.ShapeDtypeStruct((GM_GROUPS, GM_CHUNK, GM_CHUNK), F32),
                   jax.ShapeDtypeStruct((GM_CHUNK, GM_WIDTH), F32), jax.ShapeDtypeStruct((1, GM_WIDTH), F32),
                   jax.ShapeDtypeStruct((1, GM_WIDTH), F32)],
        input_output_aliases={8: 0}, compiler_params=_params(("arbitrary",)),
    )(z, z, dy, gln, bln, wc, wct, bst, dz)


def _lat_fwd(z, g_cq, g_ckv, name):
    rows = z.shape[0]
    tr = _row_tile(rows, 4 * MLA_W)

    def body(z_ref, gq_ref, gkv_ref, nq_ref, nkv_ref):
        zb = z_ref[...]
        cq, ckv = zb[:, 0:Q_LORA], zb[:, Q_LORA:Q_LORA + KV_LORA]
        nq_ref[...] = (cq * _rms(cq, Q_LORA) * gq_ref[...]).astype(BF16)
        nkv_ref[...] = (ckv * _rms(ckv, KV_LORA) * gkv_ref[...]).astype(BF16)

    return pl.pallas_call(
        body, name=name, grid=(rows // tr,),
        in_specs=[_rowspec(tr, MLA_W, Z_MLA // MLA_W), _fullspec((1, Q_LORA)), _fullspec((1, KV_LORA))],
        out_specs=[_rowspec(tr, Q_LORA), _rowspec(tr, KV_LORA)],
        out_shape=[jax.ShapeDtypeStruct((rows, Q_LORA), BF16), jax.ShapeDtypeStruct((rows, KV_LORA), BF16)],
        compiler_params=_params(("parallel",)),
    )(z, g_cq, g_ckv)


def _lat_bwd(z, dnq, dnkv, dkpe, g_cq, g_ckv, dz, name):
    rows = z.shape[0]
    tr = _row_tile(rows, 8 * MLA_W)

    def body(z_ref, dnq_ref, dnkv_ref, dkpe_ref, gq_ref, gkv_ref, _, dz_ref, dgq_ref, dgkv_ref):
        first = pl.program_id(0) == 0
        zb = z_ref[...]
        dcq, dgq = _rms_bwd_rows(zb[:, 0:Q_LORA], gq_ref[...], dnq_ref[...], Q_LORA)
        dckv, dgkv = _rms_bwd_rows(zb[:, Q_LORA:Q_LORA + KV_LORA], gkv_ref[...], dnkv_ref[...], KV_LORA)
        dz_ref[:, 0:Q_LORA] = dcq.astype(BF16)
        dz_ref[:, Q_LORA:Q_LORA + KV_LORA] = dckv.astype(BF16)
        dz_ref[:, Q_LORA + KV_LORA:MLA_W] = dkpe_ref[...].astype(BF16)
        _acc_rows(dgq_ref, dgq, first)
        _acc_rows(dgkv_ref, dgkv, first)

    return pl.pallas_call(
        body, name=name, grid=(rows // tr,),
        in_specs=[_rowspec(tr, MLA_W, Z_MLA // MLA_W), _rowspec(tr, Q_LORA), _rowspec(tr, KV_LORA), _rowspec(tr, LANES),
                  _fullspec((1, Q_LORA)), _fullspec((1, KV_LORA)), ANY_SPEC],
        out_specs=[_rowspec(tr, MLA_W, Z_MLA // MLA_W), _fullspec((1, Q_LORA)), _fullspec((1, KV_LORA))],
        out_shape=[jax.ShapeDtypeStruct(dz.shape, dz.dtype), jax.ShapeDtypeStruct((1, Q_LORA), F32),
                   jax.ShapeDtypeStruct((1, KV_LORA), F32)],
        input_output_aliases={6: 0}, compiler_params=_params(("arbitrary",)),
    )(z, dnq, dnkv, dkpe, g_cq, g_ckv, dz)


def _rope(y, cc, ss):
    return y * cc + pltpu.roll(y, 64, 1) * ss


def _rope_bwd(d, cc, ss):
    return d * cc + pltpu.roll(d * ss, 64, 1)


def _qk_fwd(q, kv, z, cc, ss, gqn, gqp, gkn, gkp, name):
    rows = q.shape[0]
    W = MLA_HEADS * HEAD
    tr = _row_tile(rows, 20 * W)
    QS = MLA_SCALE * LOG2E

    def body(q_ref, kv_ref, kpe_ref, cc_ref, ss_ref, gqn_ref, gqp_ref, gkn_ref, gkp_ref, qc_ref, kc_ref, v_ref):
        cc, ss = cc_ref[...], ss_ref[...]
        kpe = kpe_ref[...]
        kp = _rope(kpe * _rms(kpe, MLA_ROPE) * gkp_ref[...], cc, ss).astype(BF16)
        for h in range(MLA_HEADS):
            qn = q_ref[:, h * HEAD:(h + 1) * HEAD]
            qp = q_ref[:, W + h * HEAD:W + (h + 1) * HEAD]
            kn = kv_ref[:, h * HEAD:(h + 1) * HEAD]
            qc_ref[:, h * QCAT:h * QCAT + HEAD] = (qn * _rms(qn, HEAD) * gqn_ref[...] * QS).astype(BF16)
            qc_ref[:, h * QCAT + HEAD:(h + 1) * QCAT] = (_rope(qp * _rms(qp, MLA_ROPE) * gqp_ref[...], cc, ss) * QS).astype(BF16)
            kc_ref[:, h * QCAT:h * QCAT + HEAD] = (kn * _rms(kn, HEAD) * gkn_ref[...]).astype(BF16)
            kc_ref[:, h * QCAT + HEAD:(h + 1) * QCAT] = kp
        v_ref[...] = kv_ref[:, W:2 * W].astype(BF16)

    g = _fullspec((1, HEAD))
    return _call(
        body, None, [q, kv, z, cc, ss, gqn, gqp, gkn, gkp], name=name, grid=(rows // tr,),
        in_specs=[_rowspec(tr, 2 * W), _rowspec(tr, 2 * W), _rowspec(tr, LANES, Z_KPE // LANES), _rowspec(tr, LANES),
                  _rowspec(tr, LANES), g, g, g, g],
        out_specs=[_rowspec(tr, MLA_HEADS * QCAT), _rowspec(tr, MLA_HEADS * QCAT), _rowspec(tr, W)],
        out_shape=[jax.ShapeDtypeStruct((rows, MLA_HEADS * QCAT), BF16), jax.ShapeDtypeStruct((rows, MLA_HEADS * QCAT), BF16),
                   jax.ShapeDtypeStruct((rows, W), BF16)],
        scratch_shapes=[], sem=("parallel",))


def _qk_bwd(q, kv, z, cc, ss, gqn, gqp, gkn, gkp, dqc, dkc, dv, name):
    rows = q.shape[0]
    W = MLA_HEADS * HEAD
    tr = _row_tile(rows, 40 * W)

    def body(q_ref, kv_ref, kpe_ref, cc_ref, ss_ref, gqn_ref, gqp_ref, gkn_ref, gkp_ref, dqc_ref, dkc_ref, dv_ref,
             dq_ref, dkv_ref, dkpe_ref, dgqn_ref, dgqp_ref, dgkn_ref, dgkp_ref):
        first = pl.program_id(0) == 0
        cc, ss = cc_ref[...], ss_ref[...]
        sqn = sqp = skn = dkp = None
        for h in range(MLA_HEADS):
            dx, dg = _rms_bwd_rows(q_ref[:, h * HEAD:(h + 1) * HEAD], gqn_ref[...], dqc_ref[:, h * QCAT:h * QCAT + HEAD], HEAD)
            dq_ref[:, h * HEAD:(h + 1) * HEAD] = dx.astype(BF16)
            sqn = dg if sqn is None else sqn + dg
            dy = _rope_bwd(dqc_ref[:, h * QCAT + HEAD:(h + 1) * QCAT], cc, ss)
            dx, dg = _rms_bwd_rows(q_ref[:, W + h * HEAD:W + (h + 1) * HEAD], gqp_ref[...], dy, MLA_ROPE)
            dq_ref[:, W + h * HEAD:W + (h + 1) * HEAD] = dx.astype(BF16)
            sqp = dg if sqp is None else sqp + dg
            dx, dg = _rms_bwd_rows(kv_ref[:, h * HEAD:(h + 1) * HEAD], gkn_ref[...], dkc_ref[:, h * QCAT:h * QCAT + HEAD], HEAD)
            dkv_ref[:, h * HEAD:(h + 1) * HEAD] = dx.astype(BF16)
            skn = dg if skn is None else skn + dg
            part = dkc_ref[:, h * QCAT + HEAD:(h + 1) * QCAT].astype(F32)
            dkp = part if dkp is None else dkp + part
        dkv_ref[:, W:2 * W] = dv_ref[...].astype(BF16)
        dx, dg = _rms_bwd_rows(kpe_ref[...], gkp_ref[...], _rope_bwd(dkp, cc, ss), MLA_ROPE)
        dkpe_ref[...] = dx
        _acc_rows(dgqn_ref, sqn, first)
        _acc_rows(dgqp_ref, sqp, first)
        _acc_rows(dgkn_ref, skn, first)
        _acc_rows(dgkp_ref, dg, first)

    g = _fullspec((1, HEAD))
    gs = jax.ShapeDtypeStruct((1, HEAD), F32)
    return pl.pallas_call(
        body, name=name, grid=(rows // tr,),
        in_specs=[_rowspec(tr, 2 * W), _rowspec(tr, 2 * W), _rowspec(tr, LANES, Z_KPE // LANES), _rowspec(tr, LANES),
                  _rowspec(tr, LANES), g, g, g, g, _rowspec(tr, MLA_HEADS * QCAT), _rowspec(tr, MLA_HEADS * QCAT),
                  _rowspec(tr, W)],
        out_specs=[_rowspec(tr, 2 * W), _rowspec(tr, 2 * W), _rowspec(tr, LANES), g, g, g, g],
        out_shape=[jax.ShapeDtypeStruct((rows, 2 * W), BF16), jax.ShapeDtypeStruct((rows, 2 * W), BF16),
                   jax.ShapeDtypeStruct((rows, LANES), F32), gs, gs, gs, gs],
        compiler_params=_params(("arbitrary",)),
    )(q, kv, z, cc, ss, gqn, gqp, gkn, gkp, dqc, dkc, dv)


def _headnorm_fwd(x, col, nheads, g, out_scale, name):
    rows = x.shape[0]
    W = nheads * HEAD
    tr = _row_tile(rows, 6 * W)

    def body(x_ref, g_ref, o_ref):
        for h in range(nheads):
            xv = x_ref[:, h * HEAD:(h + 1) * HEAD]
            o_ref[:, h * HEAD:(h + 1) * HEAD] = (xv * _rms(xv, HEAD) * g_ref[...] * out_scale).astype(BF16)

    return pl.pallas_call(
        body, name=name, grid=(rows // tr,),
        in_specs=[_rowspec(tr, W, col), _fullspec((1, HEAD))], out_specs=_rowspec(tr, W),
        out_shape=jax.ShapeDtypeStruct((rows, W), BF16), compiler_params=_params(("parallel",)),
    )(x, g)


def _headnorm_bwd(x, col, nheads, g, dy, tail, name, into=None):
    rows = x.shape[0]
    W = nheads * HEAD
    tr = _row_tile(rows, 12 * W)
    has_tail = tail is not None
    WO = 2 * W if has_tail else W

    def body(*refs):
        if into is not None:
            x_ref, g_ref, dy_ref, _, dx_ref, dg_ref = refs
        elif has_tail:
            x_ref, g_ref, dy_ref, t_ref, dx_ref, dg_ref = refs
        else:
            x_ref, g_ref, dy_ref, dx_ref, dg_ref = refs
        acc = None
        for h in range(nheads):
            sl = slice(h * HEAD, (h + 1) * HEAD)
            dx, dg = _rms_bwd_rows(x_ref[:, sl], g_ref[...], dy_ref[:, sl], HEAD)
            dx_ref[:, sl] = dx.astype(BF16)
            acc = dg if acc is None else acc + dg
        if has_tail:
            dx_ref[:, W:2 * W] = t_ref[...].astype(BF16)
        _acc_rows(dg_ref, acc, pl.program_id(0) == 0)

    ins = [x, g, dy] + ([tail] if has_tail else [])
    specs = [_rowspec(tr, W, col), _fullspec((1, HEAD)), _rowspec(tr, W)] + ([_rowspec(tr, W)] if has_tail else [])
    dx_spec, dx_shape, aliases = _rowspec(tr, WO), jax.ShapeDtypeStruct((rows, WO), BF16), {}
    if into is not None:
        assert not has_tail
        ins, specs = ins + [into[0]], specs + [ANY_SPEC]
        dx_spec, dx_shape, aliases = _rowspec(tr, W, into[1]), jax.ShapeDtypeStruct(into[0].shape, into[0].dtype), {3: 0}
    return pl.pallas_call(
        body, name=name, grid=(rows // tr,), in_specs=specs,
        out_specs=[dx_spec, _fullspec((1, HEAD))], out_shape=[dx_shape, jax.ShapeDtypeStruct((1, HEAD), F32)],
        input_output_aliases=aliases, compiler_params=_params(("arbitrary",)),
    )(*ins)


def _sigmoid(x):
    return 1.0 / (1.0 + jnp.exp(-x.astype(F32)))


def _merge_fwd(z, y_gm, y_mla, y_mem, name):
    rows = z.shape[0]
    tr = _row_tile(rows, 14 * D_MODEL)

    def body(g0_ref, g1_ref, g2_ref, a_ref, b_ref, c_ref, o_ref):
        m = _sigmoid(g0_ref[...]) * a_ref[...] + _sigmoid(g1_ref[...]) * b_ref[...] + _sigmoid(g2_ref[...]) * c_ref[...]
        o_ref[...] = m.astype(BF16)

    r = _rowspec(tr, D_MODEL)
    return pl.pallas_call(
        body, name=name, grid=(rows // tr,),
        in_specs=[_rowspec(tr, D_MODEL, 0), _rowspec(tr, D_MODEL, 1), _rowspec(tr, D_MODEL, 2),r, r, r],
        out_specs=r, out_shape=jax.ShapeDtypeStruct((rows, D_MODEL), BF16), compiler_params=_params(("parallel",)),
    )(z, z, z, y_gm, y_mla, y_mem)


def _merge_bwd(z, y_gm, y_mla, y_mem, dm, name):
    rows = z.shape[0]
    tr = _row_tile(rows, 24 * D_MODEL)

    def body(g0_ref, g1_ref, g2_ref, a_ref, b_ref, c_ref, dm_ref, da_ref, db_ref, dc_ref, dzg_ref):
        dmv = dm_ref[...].astype(F32)
        for k, (g_ref, y_ref, dy_ref) in enumerate(((g0_ref, a_ref, da_ref), (g1_ref, b_ref, db_ref), (g2_ref, c_ref, dc_ref))):
            s = _sigmoid(g_ref[...])
            dy_ref[...] = (dmv * s).astype(BF16)
            dzg_ref[:, k * D_MODEL:(k + 1) * D_MODEL] = (dmv * y_ref[...] * s * (1.0 - s)).astype(BF16)

    r = _rowspec(tr, D_MODEL)
    o = jax.ShapeDtypeStruct((rows, D_MODEL), BF16)
    return pl.pallas_call(
        body, name=name, grid=(rows // tr,),
        in_specs=[_rowspec(tr, D_MODEL, 0), _rowspec(tr, D_MODEL, 1), _rowspec(tr, D_MODEL, 2),r, r, r, r],
        out_specs=[r, r, r, _rowspec(tr, 3 * D_MODEL, 0)],
        out_shape=[o, o, o, jax.ShapeDtypeStruct((rows, Z_COLS), BF16)],
        compiler_params=_params(("parallel",)),
    )(z, z, z, y_gm, y_mla, y_mem, dm)


_NT = (((1,), (1,)), ((), ()))
_TN = (((0,), (0,)), ((), ()))


def _diag_mask(s):
    row = lax.broadcasted_iota(jnp.int32, s.shape, 0)
    col = lax.broadcasted_iota(jnp.int32, s.shape, 1)
    return jnp.where(row >= col, s, NEG)


def _attn_fwd(q, k, v, nb, nheads, dk, v_col0, causal, name, rider=None):
    S, Skv = q.shape[0] // nb, k.shape[0] // nb
    tq = _pick(Skv, ATT_TILE) if causal else _pick(S, 4 * ATT_TILE)
    nq = S // tq

    def body(q_ref, k_ref, v_ref, o_ref, lse_ref):
        for i in range(nq):
            r0 = i * tq
            qb = q_ref[r0:r0 + tq, :]
            if causal:
                spans = ([(0, r0, False)] if i > 0 else []) + [(r0, r0 + tq, True)]
            else:
                spans = [(0, Skv, False)]
            scores = []
            for a, b, masked in spans:
                s = lax.dot_general(qb, k_ref[a:b, :], _NT, preferred_element_type=F32)
                scores.append(_diag_mask(s) if masked else s)
            m = functools.reduce(jnp.maximum, [jnp.max(s, axis=-1, keepdims=True) for s in scores])
            l = acc = None
            for s, (a, b, _) in zip(scores, spans):
                p = jnp.exp2(s - m)
                lp = jnp.sum(p, axis=-1, keepdims=True)
                ap = jnp.dot(p.astype(BF16), v_ref[a:b, :].astype(BF16), preferred_element_type=F32)
                l, acc = (lp, ap) if l is None else (l + lp, acc + ap)
            o_ref[r0:r0 + tq, :] = (acc / l).astype(BF16)
            lse_ref[r0:r0 + tq, :] = m + jnp.log2(l)

    ins = [q, k, v]
    in_specs = [pl.BlockSpec((S, dk), lambda b, h: (b, h)), pl.BlockSpec((Skv, dk), lambda b, h: (b, h)),
                pl.BlockSpec((Skv, HEAD), lambda b, h: (b, v_col0 + h))]
    out_specs = [pl.BlockSpec((S, HEAD), lambda b, h: (b, h)), pl.BlockSpec((None, S, 1), lambda b, h: (h, b, 0))]
    out_shape = [jax.ShapeDtypeStruct((nb * S, nheads * HEAD), BF16), jax.ShapeDtypeStruct((nheads, nb * S, 1), F32)]
    return _call(body, rider, ins, name=name, grid=(nb, nheads), in_specs=in_specs, out_specs=out_specs,
                 out_shape=out_shape, scratch_shapes=[], sem=("parallel", "parallel"))


def _attn_bwd(q, k, v, o, do, lse, nb, nheads, dk, v_col0, scale, causal, name, rider=None):
    S, Skv = q.shape[0] // nb, k.shape[0] // nb
    tk = _pick(Skv, ATT_TILE)
    nkv = Skv // tk

    def body(q_ref, k_ref, v_ref, o_ref, do_ref, lse_ref, dq_ref, dk_ref, dv_ref, delta_ref, dob_ref, dqa_ref):
        dov = do_ref[...]
        delta_ref[...] = jnp.sum(o_ref[...].astype(F32) * dov.astype(F32), axis=-1, keepdims=True)
        dob_ref[...] = dov.astype(BF16)

        for j in range(nkv):
            c0 = j * tk
            kb = k_ref[c0:c0 + tk, :]
            vb = v_ref[c0:c0 + tk, :].astype(BF16)
            if causal:
                spans = [(c0, c0 + tk, True)] + ([(c0 + tk, S, False)] if c0 + tk < S else [])
            else:
                spans = [(0, S, False)]
            dk_acc = dv_acc = None
            for a, b, masked in spans:
                qb = q_ref[a:b, :]
                dob = dob_ref[a:b, :]
                s = lax.dot_general(qb, kb, _NT, preferred_element_type=F32)
                if masked:
                    s = _diag_mask(s)
                p = jnp.exp2(s - lse_ref[a:b, :])
                dp = lax.dot_general(dob, vb, _NT, preferred_element_type=F32)
                ds = (p * (dp - delta_ref[a:b, :])).astype(BF16)
                dv_p = lax.dot_general(p.astype(BF16), dob, _TN, preferred_element_type=F32)
                dk_p = lax.dot_general(ds, qb, _TN, preferred_element_type=F32)
                dk_acc, dv_acc = (dk_p, dv_p) if dk_acc is None else (dk_acc + dk_p, dv_acc + dv_p)
                dq_p = jnp.dot(ds, kb, preferred_element_type=F32) * scale
                if j == 0:
                    dqa_ref[a:b, :] = dq_p
                else:
                    dqa_ref[a:b, :] += dq_p
            dk_ref[c0:c0 + tk, :] = (dk_acc * LN2).astype(BF16)
            dv_ref[c0:c0 + tk, :] = dv_acc.astype(BF16)
        dq_ref[...] = dqa_ref[...].astype(BF16)

    ins = [q, k, v, o, do, lse]
    in_specs = [pl.BlockSpec((S, dk), lambda b, h: (b, h)), pl.BlockSpec((Skv, dk), lambda b, h: (b, h)),
                pl.BlockSpec((Skv, HEAD), lambda b, h: (b, v_col0 + h)), pl.BlockSpec((S, HEAD), lambda b, h: (b, h)),
                pl.BlockSpec((S, HEAD), lambda b, h: (b, h)), pl.BlockSpec((None, S, 1), lambda b, h: (h, b, 0))]
    out_specs = [pl.BlockSpec((S, dk), lambda b, h: (b, h)), pl.BlockSpec((Skv, dk), lambda b, h: (b, h)),
                 pl.BlockSpec((Skv, HEAD), lambda b, h: (b, h))]
    out_shape = [jax.ShapeDtypeStruct((nb * S, nheads * dk), BF16), jax.ShapeDtypeStruct((nb * Skv, nheads * dk), BF16),
                 jax.ShapeDtypeStruct((nb * Skv, nheads * HEAD), BF16)]
    return _call(body, rider, ins, name=name, grid=(nb, nheads), in_specs=in_specs, out_specs=out_specs,
                 out_shape=out_shape,
                 scratch_shapes=[pltpu.VMEM((S, 1), F32), pltpu.VMEM((S, HEAD), BF16), pltpu.VMEM((S, dk), F32)],
                 sem=("parallel", "parallel"))


def _spread_rope(a):
    zero = jnp.zeros(a.shape[:-1] + (32,), a.dtype)
    return jnp.concatenate([a[..., :32], zero, a[..., 32:], zero], axis=-1)


def _gather_rope(a):
    return jnp.concatenate([a[..., 0:32], a[..., 64:96]], axis=-1)


def _win_layout(w):
    return jnp.concatenate([w[:, C_ZG:C_END], w[:, C_ZU:C_CQ], w[:, C_QM:C_ZG], w[:, C_CQ:C_CKV], w[:, C_CKV:C_KPE],
                            _spread_rope(w[:, C_KPE:C_QM])], axis=1)


def _win_unlayout(d):
    return jnp.concatenate([d[:, Z_GM:Z_QM], d[:, Z_MLA:Z_MLA + Q_LORA], d[:, Z_MLA + Q_LORA:Z_KPE],
                            _gather_rope(d[:, Z_KPE:Z_COLS]), d[:, Z_QM:Z_MLA], d[:, 0:Z_GM]], axis=1)


def _wuq_layout(w):
    r = w.reshape(Q_LORA, MLA_HEADS, HEAD + MLA_ROPE)
    return jnp.concatenate([r[:, :, :HEAD].reshape(Q_LORA, -1), _spread_rope(r[:, :, HEAD:]).reshape(Q_LORA, -1)], axis=1)


def _wuq_unlayout(d):
    n = d[:, :MLA_HEADS * HEAD].reshape(Q_LORA, MLA_HEADS, HEAD)
    p = _gather_rope(d[:, MLA_HEADS * HEAD:].reshape(Q_LORA, MLA_HEADS, HEAD))
    return jnp.concatenate([n, p], axis=-1).reshape(Q_LORA, -1)


def _wukv_layout(w):
    r = w.reshape(KV_LORA, MLA_HEADS, 2 * HEAD)
    return jnp.concatenate([r[:, :, :HEAD].reshape(KV_LORA, -1), r[:, :, HEAD:].reshape(KV_LORA, -1)], axis=1)


def _wukv_unlayout(d):
    k = d[:, :MLA_HEADS * HEAD].reshape(KV_LORA, MLA_HEADS, HEAD)
    v = d[:, MLA_HEADS * HEAD:].reshape(KV_LORA, MLA_HEADS, HEAD)
    return jnp.concatenate([k, v], axis=-1).reshape(KV_LORA, -1)


AG_MID = ['w_uq', 'w_ukv', 'w_mem_kv', 'w_o_gm', 'w_o_mla', 'w_o_mem', 'w_out']
AG_FFN = ['w_ff1', 'w_ff2']
RS_GROUPS = {'ffn_proj': ['w_ff2', 'w_ff1', 'w_out', 'w_o_gm', 'w_o_mla', 'w_o_mem'],
             'lat': ['w_uq', 'w_ukv', 'w_mem_kv'], 'in_top': ['w_in'], 'in_bot': ['w_in']}


def _unride(res, rider):
    return (res, None) if rider is None else res


def _local_step(x, mem, positions, target, P, ws):
    B, S, _ = x.shape
    M = mem.shape[1]
    T = B * S
    x2d = x.reshape(T, D_MODEL)
    mem2d = mem.reshape(B * M, D_MODEL)
    tgt2d = target.reshape(T, D_MODEL)

    def row(v):
        return v.reshape(1, -1).astype(F32)

    inv_freq = ROPE_BASE ** (-jnp.arange(0, MLA_ROPE, 2, dtype=F32) / MLA_ROPE)
    zero = jnp.zeros_like(inv_freq)
    ang = positions.reshape(T).astype(F32)[:, None] * jnp.concatenate([inv_freq, zero, inv_freq, zero])
    cc = jnp.cos(ang) * jnp.concatenate([zero + 1.0, zero, zero + 1.0, zero])
    ss = jnp.sin(ang) * jnp.concatenate([zero - 1.0, zero, zero + 1.0, zero])

    g_mix, g_cq, g_ckv, g_ffn, g_mem = row(P['g_mix']), row(P['g_cq']), row(P['g_ckv']), row(P['g_ffn']), row(P['g_mem'])
    gqn, gkn, gmq, gmk = row(P['g_q_nope']), row(P['g_k_nope']), row(P['g_mq']), row(P['g_mk'])
    gqp, gkp = _spread_rope(row(P['g_q_pe'])), _spread_rope(row(P['g_k_pe']))
    gln, bln = row(P['g_gm_ln']), row(P['b_gm_ln'])
    wc = jnp.tril(P['w_spatial'].astype(F32))
    wct = jnp.swapaxes(wc, 1, 2).astype(BF16)
    wc = wc.astype(BF16)
    bst = jnp.broadcast_to(P['b_spatial'].astype(F32)[:, :, None], (GM_GROUPS, GM_CHUNK, LANES))

    ride = ws.gather(['w_in'])
    h, got = _unride(_rms_fwd(x2d, g_mix, "rms_mix", rider=ride), ride)
    w_in = _win_layout(ws.gathered(['w_in'], got)['w_in']).astype(BF16)
    ride = ws.gather(AG_MID)
    z, got = _unride(_matmul(h, w_in, 'nn', ACT, "mm_in", tn_t=768, rider=ride), ride)
    mid = ws.gathered(AG_MID, got)
    w_uq, w_ukv = _wuq_layout(mid['w_uq']).astype(BF16), _wukv_layout(mid['w_ukv']).astype(BF16)
    w_mem_kv, w_o_gm, w_o_mla, w_o_mem, w_out = (mid[n] for n in ('w_mem_kv', 'w_o_gm', 'w_o_mla', 'w_o_mem', 'w_out'))
    ygm_pre = _gm_fwd(z, gln, bln, wc, bst, "gm_fwd")
    y_gm = _matmul(ygm_pre, w_o_gm, 'nn', ACT, "mm_o_gm")
    nq, nkv = _lat_fwd(z, g_cq, g_ckv, "lat_fwd")
    q = _matmul(nq, w_uq, 'nn', ACT, "mm_uq")
    kv = _matmul(nkv, w_ukv, 'nn', ACT, "mm_ukv")
    qcat, kcat, vv = _qk_fwd(q, kv, z, cc, ss, gqn, gqp, gkn, gkp, "qk_fwd")
    ride = ws.gather(AG_FFN)
    (o, lse), got = _unride(_attn_fwd(qcat, kcat, vv, B, MLA_HEADS, QCAT, 0, True, "mla_attn_fwd", rider=ride), ride)
    ffn = ws.gathered(AG_FFN, got)
    w_ff1, w_ff2 = ffn['w_ff1'], ffn['w_ff2']
    y_mla = _matmul(o, w_o_mla, 'nn', ACT, "mm_o_mla")
    nm = _rms_fwd(mem2d, g_mem, "rms_mem")
    kvm = _matmul(nm, w_mem_kv, 'nn', ACT, "mm_mem_kv")
    qm = _headnorm_fwd(z, Z_QM // (MEM_HEADS * HEAD), MEM_HEADS, gmq, MEM_SCALE * LOG2E, "memq_fwd")
    km = _headnorm_fwd(kvm, 0, MEM_HEADS, gmk, 1.0, "memk_fwd")
    om, lse_m = _attn_fwd(qm, km, kvm, B, MEM_HEADS, HEAD, MEM_HEADS, False, "mem_attn_fwd")
    y_mem = _matmul(om, w_o_mem, 'nn', ACT, "mm_o_mem")
    merged = _merge_fwd(z, y_gm, y_mla, y_mem, "merge_fwd")
    x1 = _matmul(merged, w_out, 'nn', F32, "mm_out", add=x2d)
    h2 = _rms_fwd(x1, g_ffn, "rms_ffn")
    a1 = _matmul(h2, w_ff1, 'nn', BF16, "mm_ff1")
    dx2, dx2b, loss_part = _matmul(a1, w_ff2, 'nn', F32, "mm_ff2", add=x1, relu2_a=True, sq_err_target=tgt2d)

    G = {}
    d_ff2 = _matmul(a1, dx2b, 'tn', BF16, "mm_d_ff2", relu2_a=True)
    da1 = _matmul(dx2b, w_ff2, 'nt', BF16, "mm_da1", relu2_grad=a1)
    d_ff1 = _matmul(h2, da1, 'tn', BF16, "mm_d_ff1", col_shards=N_DEV)
    dh2 = _matmul(da1, w_ff1, 'nt', ACT, "mm_dh2")
    dx1, dx1b, G['g_ffn'] = _rms_bwd(x1, g_ffn, dh2, dx2, "rms_ffn_bwd", dx_dtypes=(F32, BF16))
    d_out = _matmul(merged, dx1b, 'tn', BF16, "mm_d_out")
    dmerged = _matmul(dx1b, w_out, 'nt', ACT, "mm_dmerged")
    dy_gm, dy_mla, dy_mem, dz = _merge_bwd(z, y_gm, y_mla, y_mem, dmerged, "merge_bwd")
    d_o_gm = _matmul(ygm_pre, dy_gm, 'tn', BF16, "mm_d_o_gm")
    d_o_mla = _matmul(o, dy_mla, 'tn', BF16, "mm_d_o_mla")
    d_o_mem = _matmul(om, dy_mem, 'tn', BF16, "mm_d_o_mem")
    dygm_pre = _matmul(dy_gm, w_o_gm, 'nt', ACT, "mm_dygm")
    dz, dws, dbs, G['g_gm_ln'], G['b_gm_ln'] = _gm_bwd(z, dygm_pre, gln, bln, wc, wct, bst, dz, "gm_bwd")
    G['w_spatial'] = jnp.tril(dws)
    G['b_spatial'] = jnp.sum(dbs.reshape(GM_CHUNK, GM_GROUPS, LANES), axis=-1).T
    do = _matmul(dy_mla, w_o_mla, 'nt', ACT, "mm_do")
    ride = ws.scatter('ffn_proj', {'w_ff2': d_ff2, 'w_ff1': d_ff1, 'w_out': d_out, 'w_o_gm': d_o_gm, 'w_o_mla': d_o_mla,
                                   'w_o_mem': d_o_mem})
    (dqc, dkc, dvv), got = _unride(_attn_bwd(qcat, kcat, vv, o, do, lse, B, MLA_HEADS, QCAT, 0, MLA_SCALE, True,
                                             "mla_attn_bwd", rider=ride), ride)
    ws.scattered('ffn_proj', got)
    dq, dkv, dkpe, G['g_q_nope'], dgqp, G['g_k_nope'], dgkp = _qk_bwd(q, kv, z, cc, ss, gqn, gqp, gkn, gkp, dqc, dkc, dvv,
                                                                     "qk_bwd")
    G['g_q_pe'], G['g_k_pe'] = _gather_rope(dgqp), _gather_rope(dgkp)
    d_uq = _wuq_unlayout(_matmul(nq, dq, 'tn', BF16, "mm_d_uq"))
    dnq = _matmul(dq, w_uq, 'nt', ACT, "mm_dnq")
    d_ukv = _wukv_unlayout(_matmul(nkv, dkv, 'tn', BF16, "mm_d_ukv"))
    dnkv = _matmul(dkv, w_ukv, 'nt', ACT, "mm_dnkv")
    dz, G['g_cq'], G['g_ckv'] = _lat_bwd(z, dnq, dnkv, dkpe, g_cq, g_ckv, dz, "lat_bwd")
    dom = _matmul(dy_mem, w_o_mem, 'nt', ACT, "mm_dom")
    dqm, dkm, dvm = _attn_bwd(qm, km, kvm, om, dom, lse_m, B, MEM_HEADS, HEAD, MEM_HEADS, MEM_SCALE, False, "mem_attn_bwd")
    dz, G['g_mq'] = _headnorm_bwd(z, Z_QM // (MEM_HEADS * HEAD), MEM_HEADS, gmq, dqm, None, "memq_bwd",
                                  into=(dz, Z_QM // (MEM_HEADS * HEAD)))
    dkvm, G['g_mk'] = _headnorm_bwd(kvm, 0, MEM_HEADS, gmk, dkm, dvm, "memk_bwd")
    d_mem_kv = _matmul(nm, dkvm, 'tn', BF16, "mm_d_mem_kv")
    dnm = _matmul(dkvm, w_mem_kv, 'nt', ACT, "mm_dnm")
    G['g_mem'], = _rms_bwd(mem2d, g_mem, dnm, None, "rms_mem_bwd", dx_dtypes=())
    half = D_MODEL // 2
    ride = ws.scatter('lat', {'w_uq': d_uq, 'w_ukv': d_ukv, 'w_mem_kv': d_mem_kv})
    d_top, got = _unride(_matmul(h, dz, 'tn', BF16, "mm_d_in_top", tn_t=768, m_rows=(0, half), rider=ride), ride)
    ws.scattered('lat', got)
    ride = ws.scatter('in_top', {'w_in': _win_unlayout(d_top)})
    d_bot, got = _unride(_matmul(h, dz, 'tn', BF16, "mm_d_in_bot", tn_t=768, m_rows=(half, half), rider=ride), ride)
    ws.scattered('in_top', got)
    ride = ws.scatter('in_bot', {'w_in': _win_unlayout(d_bot)})
    dh, got = _unride(_matmul(dz, w_in, 'nt', ACT, "mm_dh", rider=ride), ride)
    ws.scattered('in_bot', got)
    gx, G['g_mix'] = _rms_bwd(x2d, g_mix, dh, dx1, "rms_mix_bwd")
    return loss_part, gx.reshape(B, S, D_MODEL), G


def _all_gather8(xs, name):
    def body(x_ref, out_ref, send_sems, recv_sems, local_sem):
        x, y, c = lax.axis_index("x"), lax.axis_index("y"), lax.axis_index("c")
        me, sibling = (x, y, c), (x, y, 1 - c)
        chips = [(1 - x, y), (x, 1 - y), (1 - x, 1 - y)]

        def rows(px, py, pc):
            return out_ref.at[4 * px + 2 * py + pc]

        def copy(k, block, to, src=None):
            return pltpu.make_async_remote_copy(
                src_ref=rows(*block) if src is None else src, dst_ref=rows(*block),
                send_sem=send_sems.at[k], recv_sem=recv_sems.at[k], device_id=to, device_id_type=MESH)

        mine = pltpu.make_async_copy(x_ref, rows(*me), local_sem)
        mine.start()
        first = [copy(0, me, sibling, src=x_ref)]
        first += [copy(1 + j, me, (*chip, c), src=x_ref) for j, chip in enumerate(chips)]
        for cp in first:
            cp.start()
        passed = [copy(4 + j, (*chip, c), sibling) for j, chip in enumerate(chips)]
        for j, chip in enumerate(chips):
            copy(1 + j, (*chip, c), me).wait_recv()
            passed[j].start()
        copy(0, sibling, me).wait_recv()
        for j, chip in enumerate(chips):
            copy(4 + j, (*chip, 1 - c), me).wait_recv()
        for cp in first + passed:
            cp.wait_send()
        mine.wait()

    return pl.pallas_call(
        body, name=name, in_specs=[HBM_SPEC], out_specs=HBM_SPEC,
        out_shape=jax.ShapeDtypeStruct((N_DEV,) + xs.shape, xs.dtype),
        scratch_shapes=[pltpu.SemaphoreType.DMA((7,)), pltpu.SemaphoreType.DMA((7,)), pltpu.SemaphoreType.DMA],
    )(xs)


def _adamw_rows(w, g, m, v):
    m2 = ADAM_B1 * m + (1.0 - ADAM_B1) * g
    v2 = ADAM_B2 * v + (1.0 - ADAM_B2) * (g * g)
    m_hat = m2 / (1.0 - ADAM_B1 ** ADAM_STEP)
    v_hat = v2 / (1.0 - ADAM_B2 ** ADAM_STEP)
    delta = -ADAM_LR * (m_hat / (jnp.sqrt(v_hat) + ADAM_EPS) + ADAM_WD * w)
    return delta, m2, v2


def _sum_adamw(parts, w, m, v, name):
    rows, cols = w.shape
    assert sum(p.shape[1] for p in parts) == rows
    tr = _pick(min(p.shape[1] for p in parts), max(16, 65536 // cols), 16)
    n = parts[0].shape[0]
    counts = [p.shape[1] // tr for p in parts]
    starts = [sum(counts[:k]) for k in range(len(parts))]

    def body(*refs):
        p_refs = refs[:len(parts)]
        w_ref, m_ref, v_ref, g_ref, d_ref, m2_ref, v2_ref = refs[len(parts):]
        g = None
        for p_ref, start in zip(p_refs, starts):
            gk = p_ref[0].astype(F32)
            for k in range(1, n):
                gk = gk + p_ref[k].astype(F32)
            g = gk if g is None else jnp.where(pl.program_id(0) >= start, gk, g)
        delta, m2, v2 = _adamw_rows(w_ref[...], g, m_ref[...], v_ref[...])
        g_ref[...] = g
        d_ref[...] = delta
        m2_ref[...] = m2
        v2_ref[...] = v2

    flat = pl.BlockSpec((tr, cols), lambda i: (i, 0))
    out = jax.ShapeDtypeStruct((rows, cols), F32)
    p_specs = [pl.BlockSpec((n, tr, cols), lambda i, s=s, c=c: (0, jnp.clip(i - s, 0, c - 1), 0))
               for s, c in zip(starts, counts)]
    return pl.pallas_call(
        body, name=name, grid=(rows // tr,), in_specs=p_specs + [flat, flat, flat], out_specs=[flat] * 4,
        out_shape=[out] * 4, compiler_params=_params(("parallel",)),
    )(*parts, w, m, v)


SMALL_WIDTH = {'g_mix': 1024, 'g_cq': 384, 'g_ckv': 256, 'g_q_nope': 128, 'g_q_pe': 128, 'g_k_nope': 128, 'g_k_pe': 128,
               'g_gm_ln': 512, 'b_gm_ln': 512, 'g_mem': 1024, 'g_mq': 128, 'g_mk': 128, 'g_ffn': 1024}
NARROW = ('g_q_pe', 'g_k_pe')


def _small_layout():
    layout, r = {}, 0
    for name in SMALL + ['loss']:
        rows = {'w_spatial': GM_GROUPS * GM_CHUNK, 'b_spatial': GM_GROUPS, 'loss': 1}.get(name) or SMALL_WIDTH[name] // LANES
        layout[name] = (r, rows)
        r += -(-rows // 8) * 8
    return layout, r


def _small_pack(grads, loss_part, name):
    layout, total = _small_layout()
    names = SMALL + ['loss']

    def body(*refs):
        out_ref = refs[-1]
        out_ref[...] = jnp.zeros((total, LANES), F32)
        for ref, n in zip(refs[:-1], names):
            r0, rows = layout[n]
            if n == 'w_spatial':
                for g in range(GM_GROUPS):
                    out_ref[r0 + g * GM_CHUNK:r0 + (g + 1) * GM_CHUNK, :] = ref[g]
            elif n == 'b_spatial':
                out_ref[r0:r0 + rows, :] = ref[...]
            else:
                for k in range(rows):
                    out_ref[r0 + k:r0 + k + 1, :] = ref[:, k * LANES:(k + 1) * LANES]

    return pl.pallas_call(body, name=name, out_shape=jax.ShapeDtypeStruct((total, LANES), F32))(
        *[grads[n] for n in SMALL], loss_part)


def _small_adamw(parts, w, m, v, name):
    layout, _ = _small_layout()
    n_dev = parts.shape[0]

    def body(*refs):
        p_ref = refs[0]
        ins = refs[1:1 + 3 * len(SMALL)]
        outs = refs[1 + 3 * len(SMALL):-1]

        def gsum(r0, rows):
            g = p_ref[0, r0:r0 + rows, :]
            for d in range(1, n_dev):
                g = g + p_ref[d, r0:r0 + rows, :]
            return g

        def step(idx, g, at):
            w_ref, m_ref, v_ref = ins[3 * idx:3 * idx + 3]
            delta, m2, v2 = _adamw_rows(w_ref[at], g, m_ref[at], v_ref[at])
            for ref, val in zip(outs[4 * idx:4 * idx + 4], (g, delta, m2, v2)):
                ref[at] = val

        for idx, n in enumerate(SMALL):
            r0, rows = layout[n]
            if n == 'w_spatial':
                for g in range(GM_GROUPS):
                    step(idx, gsum(r0 + g * GM_CHUNK, GM_CHUNK), (0, g))
            elif n == 'b_spatial':
                step(idx, gsum(r0, rows), (0,))
            else:
                for k in range(rows):
                    step(idx, gsum(r0 + k, 1), (slice(None), slice(k * LANES, (k + 1) * LANES)))
        refs[-1][...] = gsum(layout['loss'][0], 8)

    flat_in = [d[n] for n in SMALL for d in (w, m, v)]
    out_shape = [jax.ShapeDtypeStruct(w[n].shape, F32) for n in SMALL for _ in range(4)]
    res = pl.pallas_call(body, name=name, out_shape=out_shape + [jax.ShapeDtypeStruct((8, LANES), F32)])(parts, *flat_in)
    groups = [{n: res[4 * i + j] for i, n in enumerate(SMALL)} for j in range(4)]
    return groups, res[-1]


def _full_from_gathered(gathered, name):
    r, c = BIG_SHAPE[name]
    if BIG_AXIS[name] == 0:
        return gathered.reshape(r, c)
    return gathered.transpose(1, 0, 2).reshape(r, c)


def _shards_of_full(g, name):
    if g.ndim == 3:
        return g
    r, c = BIG_SHAPE[name]
    if BIG_AXIS[name] == 0:
        return g.reshape(N_DEV, r // N_DEV, c)
    return g.reshape(g.shape[0], N_DEV, c // N_DEV).transpose(1, 0, 2)


class _DistWeights:
    def __init__(self, shards):
        self.shards = shards
        self.received = {}

    def gather(self, names):
        return _Gather2([self.shards[n].astype(BF16) for n in names])

    def gathered(self, names, got):
        return {n: _full_from_gathered(g, n) for n, g in zip(names, got)}

    def scatter(self, key, grads):
        return _Exchange([_shards_of_full(grads[n], n) for n in RS_GROUPS[key]], scatter=True)

    def scattered(self, key, got):
        for n, g in zip(RS_GROUPS[key], got):
            self.received.setdefault(n, []).append(g)


def kernel(x, mem, positions, g_mix, w_in, g_cq, w_uq, g_ckv, w_ukv, g_q_nope, g_q_pe, g_k_nope, g_k_pe, g_gm_ln, b_gm_ln, w_spatial, b_spatial, g_mem, w_mem_kv, g_mq, g_mk, w_o_gm, w_o_mla, w_o_mem, w_out, g_ffn, w_ff1, w_ff2, loss_target, m_g_mix, m_w_in, m_g_cq, m_w_uq, m_g_ckv, m_w_ukv, m_g_q_nope, m_g_q_pe, m_g_k_nope, m_g_k_pe, m_g_gm_ln, m_b_gm_ln, m_w_spatial, m_b_spatial, m_g_mem, m_w_mem_kv, m_g_mq, m_g_mk, m_w_o_gm, m_w_o_mla, m_w_o_mem, m_w_out, m_g_ffn, m_w_ff1, m_w_ff2, v_g_mix, v_w_in, v_g_cq, v_w_uq, v_g_ckv, v_w_ukv, v_g_q_nope, v_g_q_pe, v_g_k_nope, v_g_k_pe, v_g_gm_ln, v_b_gm_ln, v_w_spatial, v_b_spatial, v_g_mem, v_w_mem_kv, v_g_mq, v_g_mk, v_w_o_gm, v_w_o_mla, v_w_o_mem, v_w_out, v_g_ffn, v_w_ff1, v_w_ff2):
    given = dict(locals())
    w = {n: given[n][0] for n in WEIGHTS}
    mom = {n: given['m_' + n][0] for n in WEIGHTS}
    var = {n: given['v_' + n][0] for n in WEIGHTS}

    ws = _DistWeights({n: w[n] for n in BIG})
    loss_part, grad_x, G = _local_step(x, mem, positions, loss_target, {n: w[n] for n in SMALL}, ws)

    outs = {}
    for n in BIG:
        for prefix, res in zip(("grad_", "delta_", "new_m_", "new_v_"),
                               _sum_adamw(ws.received[n], w[n], mom[n], var[n], "adamw_" + n)):
            outs[prefix + n] = res[None]

    def widen(d):
        return {n: (jnp.pad(d[n], ((0, 0), (0, LANES - MLA_ROPE))) if n in NARROW else d[n]) for n in SMALL}

    parts = _all_gather8(_small_pack(widen(G), loss_part, "small_pack"), "ag_small")
    small, loss_rows = _small_adamw(parts, *[widen({n: given[prefix + n] for n in SMALL}) for prefix in ("", "m_", "v_")],
                                    "adamw_small")
    loss = 0.5 * jnp.sum(loss_rows) / D_MODEL
    for prefix, group in zip(("grad_", "delta_", "new_m_", "new_v_"), small):
        for n in SMALL:
            outs[prefix + n] = group[n][:, :MLA_ROPE] if n in NARROW else group[n]
    return (loss, grad_x, *[outs[p + n] for p in ("grad_", "delta_", "new_m_", "new_v_") for n in WEIGHTS])
```

```python
import functools
import math

import jax
import jax.numpy as jnp
from jax import lax
from jax.experimental import pallas as pl
from jax.experimental.pallas import tpu as pltpu

F32 = jnp.float32
BF16 = jnp.bfloat16
ACT = BF16

D_MODEL = 1024
MEM_HEADS = 4
HEAD = 128
GM_WIDTH = 512
GM_CHUNK = 128
GM_GROUPS = 4
MLA_HEADS = 8
MLA_ROPE = 64
Q_LORA = 384
KV_LORA = 256
D_FF = 4096
EPS = 1e-6
ROPE_BASE = 10000.0
MLA_SCALE = 1.0 / math.sqrt(HEAD + MLA_ROPE)
MEM_SCALE = 1.0 / math.sqrt(HEAD)
LOG2E = 1.4426950408889634
LN2 = 0.6931471805599453
ATT_TILE = 256
C_ZU, C_ZV, C_CQ, C_CKV, C_KPE, C_QM, C_ZG, C_END = 0, 512, 1024, 1408, 1664, 1728, 2240, 5312
Z_GM, Z_QM, Z_MLA, Z_KPE, Z_COLS = 3072, 4096, 4608, 5248, 5376
MLA_W = 768
QCAT = 2 * HEAD
ADAM_LR, ADAM_B1, ADAM_B2, ADAM_EPS, ADAM_WD, ADAM_STEP = 0.001, 0.9, 0.999, 1e-08, 0.01, 10
N_DEV = 8
LANES = 128
VMEM_LIMIT = 48 * 1024 * 1024
MAX_K_TILE = 8192
NEG = -1e30

BIG = ['w_in', 'w_uq', 'w_ukv', 'w_mem_kv', 'w_o_gm', 'w_o_mla', 'w_o_mem', 'w_out', 'w_ff1', 'w_ff2']
BIG_AXIS = {'w_in': 1, 'w_uq': 1, 'w_ukv': 1, 'w_mem_kv': 0, 'w_o_gm': 1, 'w_o_mla': 0, 'w_o_mem': 1,
            'w_out': 0, 'w_ff1': 1, 'w_ff2': 0}
BIG_SHAPE = {'w_in': (1024, 5312), 'w_uq': (384, 1536), 'w_ukv': (256, 2048), 'w_mem_kv': (1024, 1024),
             'w_o_gm': (512, 1024), 'w_o_mla': (1024, 1024), 'w_o_mem': (512, 1024), 'w_out': (1024, 1024),
             'w_ff1': (1024, 4096), 'w_ff2': (4096, 1024)}
SMALL = ['g_mix', 'g_cq', 'g_ckv', 'g_q_nope', 'g_q_pe', 'g_k_nope', 'g_k_pe', 'g_gm_ln', 'b_gm_ln',
         'w_spatial', 'b_spatial', 'g_mem', 'g_mq', 'g_mk', 'g_ffn']
WEIGHTS = ['g_mix', 'w_in', 'g_cq', 'w_uq', 'g_ckv', 'w_ukv', 'g_q_nope', 'g_q_pe', 'g_k_nope', 'g_k_pe',
           'g_gm_ln', 'b_gm_ln', 'w_spatial', 'b_spatial', 'g_mem', 'w_mem_kv', 'g_mq', 'g_mk', 'w_o_gm',
           'w_o_mla', 'w_o_mem', 'w_out', 'g_ffn', 'w_ff1', 'w_ff2']


def _pick(n, target, mult=LANES):
    best = None
    t = mult
    while t <= min(n, target):
        if n % t == 0:
            best = t
        t += mult
    return best if best is not None else n


def _params(sem):
    return pltpu.CompilerParams(dimension_semantics=sem, vmem_limit_bytes=VMEM_LIMIT)


MESH = pl.DeviceIdType.MESH
HBM_SPEC = pl.BlockSpec(memory_space=pltpu.HBM)


class _Exchange:
    def __init__(self, srcs, scatter):
        self.srcs, self.scatter = list(srcs), scatter
        self.out_shapes = [jax.ShapeDtypeStruct(s.shape if scatter else (N_DEV,) + s.shape, s.dtype) for s in self.srcs]
        n = len(self.srcs)
        self.scratch = [pltpu.SemaphoreType.DMA((n, N_DEV - 1)), pltpu.SemaphoreType.DMA((n, N_DEV - 1)),
                        pltpu.SemaphoreType.DMA((n,))]

    def _copies(self, src_refs, dst_refs, send_sems, recv_sems, local_sems):
        x, y, c = lax.axis_index("x"), lax.axis_index("y"), lax.axis_index("c")
        me = 4 * x + 2 * y + c
        local, remote = [], []
        for a, (src_ref, dst_ref) in enumerate(zip(src_refs, dst_refs)):
            def mine_for(dev, src_ref=src_ref):
                return src_ref.at[dev] if self.scatter else src_ref

            local.append(pltpu.make_async_copy(mine_for(me), dst_ref.at[me], local_sems.at[a]))
            for k in range(1, N_DEV):
                px = 1 - x if k & 4 else x
                py = 1 - y if k & 2 else y
                pc = 1 - c if k & 1 else c
                remote.append(pltpu.make_async_remote_copy(
                    src_ref=mine_for(4 * px + 2 * py + pc), dst_ref=dst_ref.at[me], send_sem=send_sems.at[a, k - 1],
                    recv_sem=recv_sems.at[a, k - 1], device_id=(px, py, pc), device_id_type=MESH))
        return local, remote

    def start(self, *refs):
        local, remote = self._copies(*refs)
        for cp in local + remote:
            cp.start()

    def wait(self, *refs):
        local, remote = self._copies(*refs)
        for cp in remote + local:
            cp.wait()


class _Gather2:
    def __init__(self, srcs):
        self.srcs = list(srcs)
        self.out_shapes = [jax.ShapeDtypeStruct((N_DEV,) + s.shape, s.dtype) for s in self.srcs]
        n = len(self.srcs)
        self.scratch = [pltpu.SemaphoreType.DMA((n, N_DEV - 1)), pltpu.SemaphoreType.DMA((n, N_DEV - 1)),
                        pltpu.SemaphoreType.DMA((n,))]

    def _plan(self, src_refs, dst_refs, send_sems, recv_sems, local_sems):
        x, y, c = lax.axis_index("x"), lax.axis_index("y"), lax.axis_index("c")
        chips = [(1 - x, y), (x, 1 - y), (1 - x, 1 - y)]
        plans = []
        for a, (src_ref, dst_ref) in enumerate(zip(src_refs, dst_refs)):
            def copy(k, block, to, src=None, a=a, dst_ref=dst_ref):
                at = dst_ref.at[4 * block[0] + 2 * block[1] + block[2]]
                return pltpu.make_async_remote_copy(src_ref=at if src is None else src, dst_ref=at,
                                                    send_sem=send_sems.at[a, k], recv_sem=recv_sems.at[a, k],
                                                    device_id=to, device_id_type=MESH)

            local = pltpu.make_async_copy(src_ref, dst_ref.at[4 * x + 2 * y + c], local_sems.at[a])
            first = [copy(0, (x, y, c), (x, y, 1 - c), src=src_ref)]
            first += [copy(1 + j, (x, y, c), (*chip, c), src=src_ref) for j, chip in enumerate(chips)]
            passed = [copy(4 + j, (*chip, c), (x, y, 1 - c)) for j, chip in enumerate(chips)]
            arrivals = [copy(1 + j, (*chip, c), (x, y, c)) for j, chip in enumerate(chips)]
            late = [copy(0, (x, y, 1 - c), (x, y, c))] + [copy(4 + j, (*chip, 1 - c), (x, y, c)) for j, chip in enumerate(chips)]
            plans.append((local, first, passed, arrivals, late))
        return plans

    def start(self, *refs):
        for local, first, _, _, _ in self._plan(*refs):
            local.start()
            for cp in first:
                cp.start()

    def wait(self, *refs):
        plans = self._plan(*refs)
        for _, _, passed, arrivals, _ in plans:
            for arrived, onward in zip(arrivals, passed):
                arrived.wait_recv()
                onward.start()
        for local, first, passed, _, late in plans:
            for cp in late:
                cp.wait_recv()
            for cp in first + passed:
                cp.wait_send()
            local.wait()


def _call(body, rider, ins, *, name, grid, in_specs, out_specs, out_shape, scratch_shapes, sem):
    if rider is None:
        return pl.pallas_call(body, name=name, grid=grid, in_specs=in_specs, out_specs=out_specs, out_shape=out_shape,
                              scratch_shapes=scratch_shapes, compiler_params=_params(sem))(*ins)
    single = not isinstance(out_shape, (list, tuple))
    own_specs, own_shapes = ([out_specs], [out_shape]) if single else (list(out_specs), list(out_shape))
    n_in, n_out, n_sc, n_r = len(ins), len(own_shapes), len(scratch_shapes), len(rider.srcs)
    n_all_in = n_in + n_r

    def carrying(*refs):
        own_in, srcs = refs[:n_in], refs[n_in:n_in + n_r]
        own_out, dsts = refs[n_all_in:n_all_in + n_out], refs[n_all_in + n_out:n_all_in + n_out + n_r]
        own_sc = refs[n_all_in + n_out + n_r:n_all_in + n_out + n_r + n_sc]
        sems = refs[n_all_in + n_out + n_r + n_sc:]
        first = last = None
        for d, steps in enumerate(grid):
            f, l = pl.program_id(d) == 0, pl.program_id(d) == steps - 1
            first, last = (f, l) if first is None else (first & f, last & l)

        @pl.when(first)
        def _():
            rider.start(srcs, dsts, *sems)

        body(*own_in, *own_out, *own_sc)

        @pl.when(last)
        def _():
            rider.wait(srcs, dsts, *sems)

    res = pl.pallas_call(
        carrying, name=name, grid=grid, in_specs=list(in_specs) + [HBM_SPEC] * n_r,
        out_specs=own_specs + [HBM_SPEC] * n_r, out_shape=own_shapes + rider.out_shapes,
        scratch_shapes=list(scratch_shapes) + rider.scratch, compiler_params=_params(("arbitrary",) * len(grid)),
    )(*ins, *rider.srcs)
    own = res[:n_out]
    return (own[0] if single else list(own)), list(res[n_out:])


def _matmul(a, b, mode, out_dtype, name, add=None, relu2_a=False, relu2_grad=None,
            tm_t=None, tn_t=None, tk_t=None, rider=None, m_rows=None, col_shards=None, sq_err_target=None):
    if mode == 'nn':
        (M, K), (K2, N) = a.shape, b.shape
    elif mode == 'nt':
        (M, K), (N, K2) = a.shape, b.shape
    else:
        (K, M), (K2, N) = a.shape, b.shape
    assert K == K2, (name, a.shape, b.shape)
    m_first = 0
    if m_rows is not None:
        assert mode == 'tn'
        m_first, M = m_rows
    if col_shards is not None:
        assert add is None and relu2_grad is None and sq_err_target is None and tn_t is None
        tn_t = N // col_shards
    if mode == 'tn':
        d_tm, d_tn, d_tk = 1024, 1024, 2048
    else:
        d_tm, d_tn, d_tk = (2048 if K <= 1024 else 1024), 512, MAX_K_TILE
    tm, tn, tk = _pick(M, tm_t or d_tm), _pick(N, tn_t or d_tn), _pick(K, tk_t or d_tk)
    gm, gn, nk = M // tm, N // tn, K // tk
    if mode == 'nn':
        a_spec = pl.BlockSpec((tm, tk), lambda i, j, k: (i, k))
        b_spec = pl.BlockSpec((tk, tn), lambda i, j, k: (k, j))
        dims = (((1,), (0,)), ((), ()))
    elif mode == 'nt':
        a_spec = pl.BlockSpec((tm, tk), lambda i, j, k: (i, k))
        b_spec = pl.BlockSpec((tn, tk), lambda i, j, k: (j, k))
        dims = (((1,), (1,)), ((), ()))
    else:
        assert m_first % tm == 0
        a_spec = pl.BlockSpec((tk, tm), lambda i, j, k: (k, m_first // tm + i))
        b_spec = pl.BlockSpec((tk, tn), lambda i, j, k: (k, j))
        dims = (((0,), (0,)), ((), ()))
    o_spec = pl.BlockSpec((tm, tn), lambda i, j, k: (i, j))
    has_add, has_e, has_t = add is not None, relu2_grad is not None, sq_err_target is not None
    assert not has_t or (nk == 1 and tn % LANES == 0)

    def body(*refs):
        a_ref, b_ref = refs[0], refs[1]
        pos = 2
        add_ref = e_ref = t_ref = None
        if has_add:
            add_ref = refs[pos]
            pos += 1
        if has_e:
            e_ref = refs[pos]
            pos += 1
        if has_t:
            t_ref = refs[pos]
            pos += 1
        o_ref = refs[pos]
        acc_ref = refs[pos + 1] if nk > 1 else None

        av = a_ref[...]
        if relu2_a:
            av = jnp.maximum(av, 0)
            av = av * av
        prod = lax.dot_general(av.astype(BF16), b_ref[...].astype(BF16), dims, preferred_element_type=F32)

        def finish(r):
            if has_add:
                r = r + add_ref[...]
            if has_e:
                r = r * (2.0 * jnp.maximum(e_ref[...].astype(F32), 0.0))
            if has_t:
                err = r - t_ref[...]
                r = err * (1.0 / N)
                refs[pos + 1][...] = r.astype(BF16)
                sq = err * err
                part = sq[:, 0:LANES]
                for c in range(1, tn // LANES):
                    part = part + sq[:, c * LANES:(c + 1) * LANES]
                _acc_rows(refs[pos + 2], part, (pl.program_id(0) == 0) & (pl.program_id(1) == 0))
            o_ref[...] = r.astype(out_dtype)

        if nk == 1:
            finish(prod)
        else:
            k = pl.program_id(2)

            @pl.when(k == 0)
            def _():
                acc_ref[...] = prod

            @pl.when(k > 0)
            def _():
                acc_ref[...] += prod

            @pl.when(k == nk - 1)
            def _():
                finish(acc_ref[...])

    ins, specs = [a, b], [a_spec, b_spec]
    if has_add:
        ins.append(add)
        specs.append(o_spec)
    if has_e:
        ins.append(relu2_grad)
        specs.append(o_spec)
    out_specs, out_shape, sem = o_spec, jax.ShapeDtypeStruct((M, N), out_dtype), ("parallel", "parallel", "arbitrary")
    if has_t:
        ins.append(sq_err_target)
        specs.append(o_spec)
        out_specs = [o_spec, o_spec, pl.BlockSpec((1, LANES), lambda i, j, k: (0, 0))]
        out_shape = [out_shape, jax.ShapeDtypeStruct((M, N), BF16), jax.ShapeDtypeStruct((1, LANES), F32)]
        sem = ("arbitrary", "arbitrary", "arbitrary")
    if col_shards is not None:
        out_specs = pl.BlockSpec((None, tm, tn), lambda i, j, k: (j, i, 0))
        out_shape = jax.ShapeDtypeStruct((col_shards, M, tn), out_dtype)
    return _call(body, rider, ins, name=name, grid=(gm, gn, nk), in_specs=specs, out_specs=out_specs, out_shape=out_shape,
                 scratch_shapes=[pltpu.VMEM((tm, tn), F32)] if nk > 1 else [], sem=sem)


ROW_BLOCK_BYTES = 12 * 1024 * 1024


def _row_tile(rows, row_bytes):
    return _pick(rows, max(16, min(1024, ROW_BLOCK_BYTES // row_bytes)), 16)


def _rowspec(tr, width, col=0):
    return pl.BlockSpec((tr, width), lambda i, col=col: (i, col))


def _fullspec(shape):
    nd = len(shape)
    return pl.BlockSpec(shape, lambda i, nd=nd: (0,) * nd)


def _rms(x, width):
    x = x.astype(F32)
    return lax.rsqrt(jnp.sum(x * x, axis=-1, keepdims=True) * (1.0 / width) + EPS)


def _rms_bwd_rows(x, g, dy, width):
    x, dy = x.astype(F32), dy.astype(F32)
    r = _rms(x, width)
    xh = x * r
    dn = dy * g
    dx = r * (dn - xh * (jnp.sum(dn * xh, axis=-1, keepdims=True) * (1.0 / width)))
    return dx, dy * xh


def _acc_rows(ref, val, first):
    s = jnp.sum(val, axis=0, keepdims=True)

    @pl.when(first)
    def _():
        ref[...] = s

    @pl.when(jnp.logical_not(first))
    def _():
        ref[...] += s


def _rms_fwd(x, g, name, rider=None):
    rows, width = x.shape
    tr = _row_tile(rows, 6 * width)

    def body(x_ref, g_ref, o_ref):
        xv = x_ref[...]
        o_ref[...] = (xv * _rms(xv, width) * g_ref[...]).astype(BF16)

    return _call(body, rider, [x, g], name=name, grid=(rows // tr,),
                 in_specs=[_rowspec(tr, width), _fullspec((1, width))], out_specs=_rowspec(tr, width),
                 out_shape=jax.ShapeDtypeStruct((rows, width), BF16), scratch_shapes=[], sem=("parallel",))


def _rms_bwd(x, g, dy, res, name, dx_dtypes=(F32,)):
    rows, width = x.shape
    tr = _row_tile(rows, 18 * width)
    has_res = res is not None
    n_in = 4 if has_res else 3

    def body(*refs):
        x_ref, g_ref, dy_ref = refs[:3]
        dx, dgv = _rms_bwd_rows(x_ref[...], g_ref[...], dy_ref[...], width)
        if has_res:
            dx = dx + refs[3][...]
        for ref, dt in zip(refs[n_in:], dx_dtypes):
            ref[...] = dx.astype(dt)
        _acc_rows(refs[-1], dgv, pl.program_id(0) == 0)

    ins = [x, g, dy] + ([res] if has_res else [])
    specs = [_rowspec(tr, width), _fullspec((1, width)), _rowspec(tr, width)] + ([_rowspec(tr, width)] if has_res else [])
    return pl.pallas_call(
        body, name=name, grid=(rows // tr,), in_specs=specs,
        out_specs=[_rowspec(tr, width)] * len(dx_dtypes) + [_fullspec((1, width))],
        out_shape=[jax.ShapeDtypeStruct((rows, width), dt) for dt in dx_dtypes] + [jax.ShapeDtypeStruct((1, width), F32)],
        compiler_params=_params(("arbitrary",)),
    )(*ins)


_GELU_C = math.sqrt(2.0 / math.pi)


def _gelu(x):
    t = jnp.tanh(_GELU_C * (x + 0.044715 * (x * x * x)))
    return 0.5 * x * (1.0 + t), t


def _gelu_grad(x, t):
    return 0.5 * (1.0 + t) + 0.5 * x * (1.0 - t * t) * (_GELU_C * (1.0 + 3.0 * 0.044715 * (x * x)))


def _gm_forward_rows(zu, zv, gln, bln, wc_ref, bst, n_chunk):
    u, tu = _gelu(zu)
    a, ta = _gelu(zv)
    mu = jnp.mean(a, axis=-1, keepdims=True)
    ac = a - mu
    rs = lax.rsqrt(jnp.mean(ac * ac, axis=-1, keepdims=True) + EPS)
    n = ac * rs
    v = n * gln + bln
    vb = v.astype(BF16)
    rows = []
    for c in range(n_chunk):
        cols = []
        for g in range(GM_GROUPS):
            vc = vb[c * GM_CHUNK:(c + 1) * GM_CHUNK, g * LANES:(g + 1) * LANES]
            mixed = jnp.dot(wc_ref[g], vc, preferred_element_type=F32) + bst[g]
            cols.append(mixed)
        rows.append(jnp.concatenate(cols, axis=1))
    mixed = jnp.concatenate(rows, axis=0) if n_chunk > 1 else rows[0]
    return u, tu, ta, n, rs, v, mixed


def _gm_fwd(z, gln, bln, wc, bst, name):
    rows = z.shape[0]
    tr = _pick(rows, 512, GM_CHUNK)
    n_chunk = tr // GM_CHUNK

    def body(zu_ref, zv_ref, gln_ref, bln_ref, wc_ref, bst_ref, o_ref):
        u, _, _, _, _, _, mixed = _gm_forward_rows(zu_ref[...].astype(F32), zv_ref[...].astype(F32), gln_ref[...], bln_ref[...], wc_ref,
                                                   bst_ref, n_chunk)
        o_ref[...] = (u * mixed).astype(BF16)

    return pl.pallas_call(
        body, name=name, grid=(rows // tr,),
        in_specs=[_rowspec(tr, GM_WIDTH, Z_GM // GM_WIDTH), _rowspec(tr, GM_WIDTH, Z_GM // GM_WIDTH + 1),_fullspec((1, GM_WIDTH)), _fullspec((1, GM_WIDTH)),
                  _fullspec((GM_GROUPS, GM_CHUNK, GM_CHUNK)), _fullspec((GM_GROUPS, GM_CHUNK, LANES))],
        out_specs=_rowspec(tr, GM_WIDTH), out_shape=jax.ShapeDtypeStruct((rows, GM_WIDTH), BF16),
        compiler_params=_params(("parallel",)),
    )(z, z, gln, bln, wc, bst)


ANY_SPEC = pl.BlockSpec(memory_space=pl.ANY)


def _gm_bwd(z, dy, gln, bln, wc, wct, bst, dz, name):
    rows = z.shape[0]
    tr = _pick(rows, 512, GM_CHUNK)
    n_chunk = tr // GM_CHUNK

    def body(zu_ref, zv_ref, dy_ref, gln_ref, bln_ref, wc_ref, wct_ref, bst_ref, _, dz_ref, dws_ref, dbs_ref, dgl_ref,
             dbl_ref):
        first = pl.program_id(0) == 0
        zu, zv, gln = zu_ref[...].astype(F32), zv_ref[...].astype(F32), gln_ref[...]
        u, tu, ta, n, rs, v, mixed = _gm_forward_rows(zu, zv, gln, bln_ref[...], wc_ref, bst_ref, n_chunk)
        dyv = dy_ref[...].astype(F32)
        dzu = dyv * mixed * _gelu_grad(zu, tu)
        dmix = dyv * u
        dmb = dmix.astype(BF16)
        vb = v.astype(BF16)
        dv_rows, dws, dbs = [], [None] * GM_GROUPS, None
        for c in range(n_chunk):
            rsl = slice(c * GM_CHUNK, (c + 1) * GM_CHUNK)
            cols = []
            for g in range(GM_GROUPS):
                csl = slice(g * LANES, (g + 1) * LANES)
                dmc = dmb[rsl, csl]
                cols.append(jnp.dot(wct_ref[g], dmc, preferred_element_type=F32))
                w_part = lax.dot_general(dmc, vb[rsl, csl], (((1,), (1,)), ((), ())), preferred_element_type=F32)
                dws[g] = w_part if dws[g] is None else dws[g] + w_part
            dv_rows.append(jnp.concatenate(cols, axis=1))
            dbs = dmix[rsl, :] if dbs is None else dbs + dmix[rsl, :]
        dv = jnp.concatenate(dv_rows, axis=0) if n_chunk > 1 else dv_rows[0]
        dn = dv * gln
        da = rs * (dn - jnp.mean(dn, axis=-1, keepdims=True) - n * jnp.mean(dn * n, axis=-1, keepdims=True))
        dzv = da * _gelu_grad(zv, ta)
        dz_ref[:, 0:GM_WIDTH] = dzu.astype(BF16)
        dz_ref[:, GM_WIDTH:2 * GM_WIDTH] = dzv.astype(BF16)
        _acc_rows(dgl_ref, dv * n, first)
        _acc_rows(dbl_ref, dv, first)

        @pl.when(first)
        def _():
            for g in range(GM_GROUPS):
                dws_ref[g] = dws[g]
            dbs_ref[...] = dbs

        @pl.when(jnp.logical_not(first))
        def _():
            for g in range(GM_GROUPS):
                dws_ref[g] += dws[g]
            dbs_ref[...] += dbs

    wspec = _fullspec((GM_GROUPS, GM_CHUNK, GM_CHUNK))
    return pl.pallas_call(
        body, name=name, grid=(rows // tr,),
        in_specs=[_rowspec(tr, GM_WIDTH, Z_GM // GM_WIDTH), _rowspec(tr, GM_WIDTH, Z_GM // GM_WIDTH + 1),
                  _rowspec(tr, GM_WIDTH), _fullspec((1, GM_WIDTH)), _fullspec((1, GM_WIDTH)), wspec, wspec, wspec, ANY_SPEC],
        out_specs=[_rowspec(tr, 2 * GM_WIDTH, Z_GM // (2 * GM_WIDTH)), wspec, _fullspec((GM_CHUNK, GM_WIDTH)),
                   _fullspec((1, GM_WIDTH)), _fullspec((1, GM_WIDTH))],
        out_shape=[jax.ShapeDtypeStruct(dz.shape, dz.dtype), jax.ShapeDtypeStruct((GM_GROUPS, GM_CHUNK, GM_CHUNK), F32),
                   jax.ShapeDtypeStruct((GM_CHUNK, GM_WIDTH), F32), jax.ShapeDtypeStruct((1, GM_WIDTH), F32),
                   jax.ShapeDtypeStruct((1, GM_WIDTH), F32)],
        input_output_aliases={8: 0}, compiler_params=_params(("arbitrary",)),
    )(z, z, dy, gln, bln, wc, wct, bst, dz)


def _lat_fwd(z, g_cq, g_ckv, name):
    rows = z.shape[0]
    tr = _row_tile(rows, 4 * MLA_W)

    def body(z_ref, gq_ref, gkv_ref, nq_ref, nkv_ref):
        zb = z_ref[...]
        cq, ckv = zb[:, 0:Q_LORA], zb[:, Q_LORA:Q_LORA + KV_LORA]
        nq_ref[...] = (cq * _rms(cq, Q_LORA) * gq_ref[...]).astype(BF16)
        nkv_ref[...] = (ckv * _rms(ckv, KV_LORA) * gkv_ref[...]).astype(BF16)

    return pl.pallas_call(
        body, name=name, grid=(rows // tr,),
        in_specs=[_rowspec(tr, MLA_W, Z_MLA // MLA_W), _fullspec((1, Q_LORA)), _fullspec((1, KV_LORA))],
        out_specs=[_rowspec(tr, Q_LORA), _rowspec(tr, KV_LORA)],
        out_shape=[jax.ShapeDtypeStruct((rows, Q_LORA), BF16), jax.ShapeDtypeStruct((rows, KV_LORA), BF16)],
        compiler_params=_params(("parallel",)),
    )(z, g_cq, g_ckv)


def _lat_bwd(z, dnq, dnkv, dkpe, g_cq, g_ckv, dz, name):
    rows = z.shape[0]
    tr = _row_tile(rows, 8 * MLA_W)

    def body(z_ref, dnq_ref, dnkv_ref, dkpe_ref, gq_ref, gkv_ref, _, dz_ref, dgq_ref, dgkv_ref):
        first = pl.program_id(0) == 0
        zb = z_ref[...]
        dcq, dgq = _rms_bwd_rows(zb[:, 0:Q_LORA], gq_ref[...], dnq_ref[...], Q_LORA)
        dckv, dgkv = _rms_bwd_rows(zb[:, Q_LORA:Q_LORA + KV_LORA], gkv_ref[...], dnkv_ref[...], KV_LORA)
        dz_ref[:, 0:Q_LORA] = dcq.astype(BF16)
        dz_ref[:, Q_LORA:Q_LORA + KV_LORA] = dckv.astype(BF16)
        dz_ref[:, Q_LORA + KV_LORA:MLA_W] = dkpe_ref[...].astype(BF16)
        _acc_rows(dgq_ref, dgq, first)
        _acc_rows(dgkv_ref, dgkv, first)

    return pl.pallas_call(
        body, name=name, grid=(rows // tr,),
        in_specs=[_rowspec(tr, MLA_W, Z_MLA // MLA_W), _rowspec(tr, Q_LORA), _rowspec(tr, KV_LORA), _rowspec(tr, LANES),
                  _fullspec((1, Q_LORA)), _fullspec((1, KV_LORA)), ANY_SPEC],
        out_specs=[_rowspec(tr, MLA_W, Z_MLA // MLA_W), _fullspec((1, Q_LORA)), _fullspec((1, KV_LORA))],
        out_shape=[jax.ShapeDtypeStruct(dz.shape, dz.dtype), jax.ShapeDtypeStruct((1, Q_LORA), F32),
                   jax.ShapeDtypeStruct((1, KV_LORA), F32)],
        input_output_aliases={6: 0}, compiler_params=_params(("arbitrary",)),
    )(z, dnq, dnkv, dkpe, g_cq, g_ckv, dz)


def _rope(y, cc, ss):
    return y * cc + pltpu.roll(y, 64, 1) * ss


def _rope_bwd(d, cc, ss):
    return d * cc + pltpu.roll(d * ss, 64, 1)


def _qk_fwd(q, kv, z, cc, ss, gqn, gqp, gkn, gkp, name):
    rows = q.shape[0]
    W = MLA_HEADS * HEAD
    tr = _row_tile(rows, 20 * W)
    QS = MLA_SCALE * LOG2E

    def body(q_ref, kv_ref, kpe_ref, cc_ref, ss_ref, gqn_ref, gqp_ref, gkn_ref, gkp_ref, qc_ref, kc_ref, v_ref):
        cc, ss = cc_ref[...], ss_ref[...]
        kpe = kpe_ref[...]
        kp = _rope(kpe * _rms(kpe, MLA_ROPE) * gkp_ref[...], cc, ss).astype(BF16)
        for h in range(MLA_HEADS):
            qn = q_ref[:, h * HEAD:(h + 1) * HEAD]
            qp = q_ref[:, W + h * HEAD:W + (h + 1) * HEAD]
            kn = kv_ref[:, h * HEAD:(h + 1) * HEAD]
            qc_ref[:, h * QCAT:h * QCAT + HEAD] = (qn * _rms(qn, HEAD) * gqn_ref[...] * QS).astype(BF16)
            qc_ref[:, h * QCAT + HEAD:(h + 1) * QCAT] = (_rope(qp * _rms(qp, MLA_ROPE) * gqp_ref[...], cc, ss) * QS).astype(BF16)
            kc_ref[:, h * QCAT:h * QCAT + HEAD] = (kn * _rms(kn, HEAD) * gkn_ref[...]).astype(BF16)
            kc_ref[:, h * QCAT + HEAD:(h + 1) * QCAT] = kp
        v_ref[...] = kv_ref[:, W:2 * W].astype(BF16)

    g = _fullspec((1, HEAD))
    return _call(
        body, None, [q, kv, z, cc, ss, gqn, gqp, gkn, gkp], name=name, grid=(rows // tr,),
        in_specs=[_rowspec(tr, 2 * W), _rowspec(tr, 2 * W), _rowspec(tr, LANES, Z_KPE // LANES), _rowspec(tr, LANES),
                  _rowspec(tr, LANES), g, g, g, g],
        out_specs=[_rowspec(tr, MLA_HEADS * QCAT), _rowspec(tr, MLA_HEADS * QCAT), _rowspec(tr, W)],
        out_shape=[jax.ShapeDtypeStruct((rows, MLA_HEADS * QCAT), BF16), jax.ShapeDtypeStruct((rows, MLA_HEADS * QCAT), BF16),
                   jax.ShapeDtypeStruct((rows, W), BF16)],
        scratch_shapes=[], sem=("parallel",))


def _qk_bwd(q, kv, z, cc, ss, gqn, gqp, gkn, gkp, dqc, dkc, dv, name):
    rows = q.shape[0]
    W = MLA_HEADS * HEAD
    tr = _row_tile(rows, 40 * W)

    def body(q_ref, kv_ref, kpe_ref, cc_ref, ss_ref, gqn_ref, gqp_ref, gkn_ref, gkp_ref, dqc_ref, dkc_ref, dv_ref,
             dq_ref, dkv_ref, dkpe_ref, dgqn_ref, dgqp_ref, dgkn_ref, dgkp_ref):
        first = pl.program_id(0) == 0
        cc, ss = cc_ref[...], ss_ref[...]
        sqn = sqp = skn = dkp = None
        for h in range(MLA_HEADS):
            dx, dg = _rms_bwd_rows(q_ref[:, h * HEAD:(h + 1) * HEAD], gqn_ref[...], dqc_ref[:, h * QCAT:h * QCAT + HEAD], HEAD)
            dq_ref[:, h * HEAD:(h + 1) * HEAD] = dx.astype(BF16)
            sqn = dg if sqn is None else sqn + dg
            dy = _rope_bwd(dqc_ref[:, h * QCAT + HEAD:(h + 1) * QCAT], cc, ss)
            dx, dg = _rms_bwd_rows(q_ref[:, W + h * HEAD:W + (h + 1) * HEAD], gqp_ref[...], dy, MLA_ROPE)
            dq_ref[:, W + h * HEAD:W + (h + 1) * HEAD] = dx.astype(BF16)
            sqp = dg if sqp is None else sqp + dg
            dx, dg = _rms_bwd_rows(kv_ref[:, h * HEAD:(h + 1) * HEAD], gkn_ref[...], dkc_ref[:, h * QCAT:h * QCAT + HEAD], HEAD)
            dkv_ref[:, h * HEAD:(h + 1) * HEAD] = dx.astype(BF16)
            skn = dg if skn is None else skn + dg
            part = dkc_ref[:, h * QCAT + HEAD:(h + 1) * QCAT].astype(F32)
            dkp = part if dkp is None else dkp + part
        dkv_ref[:, W:2 * W] = dv_ref[...].astype(BF16)
        dx, dg = _rms_bwd_rows(kpe_ref[...], gkp_ref[...], _rope_bwd(dkp, cc, ss), MLA_ROPE)
        dkpe_ref[...] = dx
        _acc_rows(dgqn_ref, sqn, first)
        _acc_rows(dgqp_ref, sqp, first)
        _acc_rows(dgkn_ref, skn, first)
        _acc_rows(dgkp_ref, dg, first)

    g = _fullspec((1, HEAD))
    gs = jax.ShapeDtypeStruct((1, HEAD), F32)
    return pl.pallas_call(
        body, name=name, grid=(rows // tr,),
        in_specs=[_rowspec(tr, 2 * W), _rowspec(tr, 2 * W), _rowspec(tr, LANES, Z_KPE // LANES), _rowspec(tr, LANES),
                  _rowspec(tr, LANES), g, g, g, g, _rowspec(tr, MLA_HEADS * QCAT), _rowspec(tr, MLA_HEADS * QCAT),
                  _rowspec(tr, W)],
        out_specs=[_rowspec(tr, 2 * W), _rowspec(tr, 2 * W), _rowspec(tr, LANES), g, g, g, g],
        out_shape=[jax.ShapeDtypeStruct((rows, 2 * W), BF16), jax.ShapeDtypeStruct((rows, 2 * W), BF16),
                   jax.ShapeDtypeStruct((rows, LANES), F32), gs, gs, gs, gs],
        compiler_params=_params(("arbitrary",)),
    )(q, kv, z, cc, ss, gqn, gqp, gkn, gkp, dqc, dkc, dv)


def _headnorm_fwd(x, col, nheads, g, out_scale, name):
    rows = x.shape[0]
    W = nheads * HEAD
    tr = _row_tile(rows, 6 * W)

    def body(x_ref, g_ref, o_ref):
        for h in range(nheads):
            xv = x_ref[:, h * HEAD:(h + 1) * HEAD]
            o_ref[:, h * HEAD:(h + 1) * HEAD] = (xv * _rms(xv, HEAD) * g_ref[...] * out_scale).astype(BF16)

    return pl.pallas_call(
        body, name=name, grid=(rows // tr,),
        in_specs=[_rowspec(tr, W, col), _fullspec((1, HEAD))], out_specs=_rowspec(tr, W),
        out_shape=jax.ShapeDtypeStruct((rows, W), BF16), compiler_params=_params(("parallel",)),
    )(x, g)


def _headnorm_bwd(x, col, nheads, g, dy, tail, name, into=None):
    rows = x.shape[0]
    W = nheads * HEAD
    tr = _row_tile(rows, 12 * W)
    has_tail = tail is not None
    WO = 2 * W if has_tail else W

    def body(*refs):
        if into is not None:
            x_ref, g_ref, dy_ref, _, dx_ref, dg_ref = refs
        elif has_tail:
            x_ref, g_ref, dy_ref, t_ref, dx_ref, dg_ref = refs
        else:
            x_ref, g_ref, dy_ref, dx_ref, dg_ref = refs
        acc = None
        for h in range(nheads):
            sl = slice(h * HEAD, (h + 1) * HEAD)
            dx, dg = _rms_bwd_rows(x_ref[:, sl], g_ref[...], dy_ref[:, sl], HEAD)
            dx_ref[:, sl] = dx.astype(BF16)
            acc = dg if acc is None else acc + dg
        if has_tail:
            dx_ref[:, W:2 * W] = t_ref[...].astype(BF16)
        _acc_rows(dg_ref, acc, pl.program_id(0) == 0)

    ins = [x, g, dy] + ([tail] if has_tail else [])
    specs = [_rowspec(tr, W, col), _fullspec((1, HEAD)), _rowspec(tr, W)] + ([_rowspec(tr, W)] if has_tail else [])
    dx_spec, dx_shape, aliases = _rowspec(tr, WO), jax.ShapeDtypeStruct((rows, WO), BF16), {}
    if into is not None:
        assert not has_tail
        ins, specs = ins + [into[0]], specs + [ANY_SPEC]
        dx_spec, dx_shape, aliases = _rowspec(tr, W, into[1]), jax.ShapeDtypeStruct(into[0].shape, into[0].dtype), {3: 0}
    return pl.pallas_call(
        body, name=name, grid=(rows // tr,), in_specs=specs,
        out_specs=[dx_spec, _fullspec((1, HEAD))], out_shape=[dx_shape, jax.ShapeDtypeStruct((1, HEAD), F32)],
        input_output_aliases=aliases, compiler_params=_params(("arbitrary",)),
    )(*ins)


def _sigmoid(x):
    return 1.0 / (1.0 + jnp.exp(-x.astype(F32)))


def _merge_fwd(z, y_gm, y_mla, y_mem, name):
    rows = z.shape[0]
    tr = _row_tile(rows, 14 * D_MODEL)

    def body(g0_ref, g1_ref, g2_ref, a_ref, b_ref, c_ref, o_ref):
        m = _sigmoid(g0_ref[...]) * a_ref[...] + _sigmoid(g1_ref[...]) * b_ref[...] + _sigmoid(g2_ref[...]) * c_ref[...]
        o_ref[...] = m.astype(BF16)

    r = _rowspec(tr, D_MODEL)
    return pl.pallas_call(
        body, name=name, grid=(rows // tr,),
        in_specs=[_rowspec(tr, D_MODEL, 0), _rowspec(tr, D_MODEL, 1), _rowspec(tr, D_MODEL, 2),r, r, r],
        out_specs=r, out_shape=jax.ShapeDtypeStruct((rows, D_MODEL), BF16), compiler_params=_params(("parallel",)),
    )(z, z, z, y_gm, y_mla, y_mem)


def _merge_bwd(z, y_gm, y_mla, y_mem, dm, name):
    rows = z.shape[0]
    tr = _row_tile(rows, 24 * D_MODEL)

    def body(g0_ref, g1_ref, g2_ref, a_ref, b_ref, c_ref, dm_ref, da_ref, db_ref, dc_ref, dzg_ref):
        dmv = dm_ref[...].astype(F32)
        for k, (g_ref, y_ref, dy_ref) in enumerate(((g0_ref, a_ref, da_ref), (g1_ref, b_ref, db_ref), (g2_ref, c_ref, dc_ref))):
            s = _sigmoid(g_ref[...])
            dy_ref[...] = (dmv * s).astype(BF16)
            dzg_ref[:, k * D_MODEL:(k + 1) * D_MODEL] = (dmv * y_ref[...] * s * (1.0 - s)).astype(BF16)

    r = _rowspec(tr, D_MODEL)
    o = jax.ShapeDtypeStruct((rows, D_MODEL), BF16)
    return pl.pallas_call(
        body, name=name, grid=(rows // tr,),
        in_specs=[_rowspec(tr, D_MODEL, 0), _rowspec(tr, D_MODEL, 1), _rowspec(tr, D_MODEL, 2),r, r, r, r],
        out_specs=[r, r, r, _rowspec(tr, 3 * D_MODEL, 0)],
        out_shape=[o, o, o, jax.ShapeDtypeStruct((rows, Z_COLS), BF16)],
        compiler_params=_params(("parallel",)),
    )(z, z, z, y_gm, y_mla, y_mem, dm)


_NT = (((1,), (1,)), ((), ()))
_TN = (((0,), (0,)), ((), ()))


def _diag_mask(s):
    row = lax.broadcasted_iota(jnp.int32, s.shape, 0)
    col = lax.broadcasted_iota(jnp.int32, s.shape, 1)
    return jnp.where(row >= col, s, NEG)


def _attn_fwd(q, k, v, nb, nheads, dk, v_col0, causal, name, rider=None):
    S, Skv = q.shape[0] // nb, k.shape[0] // nb
    tq = _pick(Skv, ATT_TILE) if causal else _pick(S, 4 * ATT_TILE)
    nq = S // tq

    def body(q_ref, k_ref, v_ref, o_ref, lse_ref):
        for i in range(nq):
            r0 = i * tq
            qb = q_ref[r0:r0 + tq, :]
            if causal:
                spans = ([(0, r0, False)] if i > 0 else []) + [(r0, r0 + tq, True)]
            else:
                spans = [(0, Skv, False)]
            scores = []
            for a, b, masked in spans:
                s = lax.dot_general(qb, k_ref[a:b, :], _NT, preferred_element_type=F32)
                scores.append(_diag_mask(s) if masked else s)
            m = functools.reduce(jnp.maximum, [jnp.max(s, axis=-1, keepdims=True) for s in scores])
            l = acc = None
            for s, (a, b, _) in zip(scores, spans):
                p = jnp.exp2(s - m)
                lp = jnp.sum(p, axis=-1, keepdims=True)
                ap = jnp.dot(p.astype(BF16), v_ref[a:b, :].astype(BF16), preferred_element_type=F32)
                l, acc = (lp, ap) if l is None else (l + lp, acc + ap)
            o_ref[r0:r0 + tq, :] = (acc / l).astype(BF16)
            lse_ref[r0:r0 + tq, :] = m + jnp.log2(l)

    ins = [q, k, v]
    in_specs = [pl.BlockSpec((S, dk), lambda b, h: (b, h)), pl.BlockSpec((Skv, dk), lambda b, h: (b, h)),
                pl.BlockSpec((Skv, HEAD), lambda b, h: (b, v_col0 + h))]
    out_specs = [pl.BlockSpec((S, HEAD), lambda b, h: (b, h)), pl.BlockSpec((None, S, 1), lambda b, h: (h, b, 0))]
    out_shape = [jax.ShapeDtypeStruct((nb * S, nheads * HEAD), BF16), jax.ShapeDtypeStruct((nheads, nb * S, 1), F32)]
    return _call(body, rider, ins, name=name, grid=(nb, nheads), in_specs=in_specs, out_specs=out_specs,
                 out_shape=out_shape, scratch_shapes=[], sem=("parallel", "parallel"))


def _attn_bwd(q, k, v, o, do, lse, nb, nheads, dk, v_col0, scale, causal, name, rider=None):
    S, Skv = q.shape[0] // nb, k.shape[0] // nb
    tk = _pick(Skv, ATT_TILE)
    nkv = Skv // tk

    def body(q_ref, k_ref, v_ref, o_ref, do_ref, lse_ref, dq_ref, dk_ref, dv_ref, delta_ref, dob_ref, dqa_ref):
        dov = do_ref[...]
        delta_ref[...] = jnp.sum(o_ref[...].astype(F32) * dov.astype(F32), axis=-1, keepdims=True)
        dob_ref[...] = dov.astype(BF16)

        for j in range(nkv):
            c0 = j * tk
            kb = k_ref[c0:c0 + tk, :]
            vb = v_ref[c0:c0 + tk, :].astype(BF16)
            if causal:
                spans = [(c0, c0 + tk, True)] + ([(c0 + tk, S, False)] if c0 + tk < S else [])
            else:
                spans = [(0, S, False)]
            dk_acc = dv_acc = None
            for a, b, masked in spans:
                qb = q_ref[a:b, :]
                dob = dob_ref[a:b, :]
                s = lax.dot_general(qb, kb, _NT, preferred_element_type=F32)
                if masked:
                    s = _diag_mask(s)
                p = jnp.exp2(s - lse_ref[a:b, :])
                dp = lax.dot_general(dob, vb, _NT, preferred_element_type=F32)
                ds = (p * (dp - delta_ref[a:b, :])).astype(BF16)
                dv_p = lax.dot_general(p.astype(BF16), dob, _TN, preferred_element_type=F32)
                dk_p = lax.dot_general(ds, qb, _TN, preferred_element_type=F32)
                dk_acc, dv_acc = (dk_p, dv_p) if dk_acc is None else (dk_acc + dk_p, dv_acc + dv_p)
                dq_p = jnp.dot(ds, kb, preferred_element_type=F32) * scale
                if j == 0:
                    dqa_ref[a:b, :] = dq_p
                else:
                    dqa_ref[a:b, :] += dq_p
            dk_ref[c0:c0 + tk, :] = (dk_acc * LN2).astype(BF16)
            dv_ref[c0:c0 + tk, :] = dv_acc.astype(BF16)
        dq_ref[...] = dqa_ref[...].astype(BF16)

    ins = [q, k, v, o, do, lse]
    in_specs = [pl.BlockSpec((S, dk), lambda b, h: (b, h)), pl.BlockSpec((Skv, dk), lambda b, h: (b, h)),
                pl.BlockSpec((Skv, HEAD), lambda b, h: (b, v_col0 + h)), pl.BlockSpec((S, HEAD), lambda b, h: (b, h)),
                pl.BlockSpec((S, HEAD), lambda b, h: (b, h)), pl.BlockSpec((None, S, 1), lambda b, h: (h, b, 0))]
    out_specs = [pl.BlockSpec((S, dk), lambda b, h: (b, h)), pl.BlockSpec((Skv, dk), lambda b, h: (b, h)),
                 pl.BlockSpec((Skv, HEAD), lambda b, h: (b, h))]
    out_shape = [jax.ShapeDtypeStruct((nb * S, nheads * dk), BF16), jax.ShapeDtypeStruct((nb * Skv, nheads * dk), BF16),
                 jax.ShapeDtypeStruct((nb * Skv, nheads * HEAD), BF16)]
    return _call(body, rider, ins, name=name, grid=(nb, nheads), in_specs=in_specs, out_specs=out_specs,
                 out_shape=out_shape,
                 scratch_shapes=[pltpu.VMEM((S, 1), F32), pltpu.VMEM((S, HEAD), BF16), pltpu.VMEM((S, dk), F32)],
                 sem=("parallel", "parallel"))


def _spread_rope(a):
    zero = jnp.zeros(a.shape[:-1] + (32,), a.dtype)
    return jnp.concatenate([a[..., :32], zero, a[..., 32:], zero], axis=-1)


def _gather_rope(a):
    return jnp.concatenate([a[..., 0:32], a[..., 64:96]], axis=-1)


def _win_layout(w):
    return jnp.concatenate([w[:, C_ZG:C_END], w[:, C_ZU:C_CQ], w[:, C_QM:C_ZG], w[:, C_CQ:C_CKV], w[:, C_CKV:C_KPE],
                            _spread_rope(w[:, C_KPE:C_QM])], axis=1)


def _win_unlayout(d):
    return jnp.concatenate([d[:, Z_GM:Z_QM], d[:, Z_MLA:Z_MLA + Q_LORA], d[:, Z_MLA + Q_LORA:Z_KPE],
                            _gather_rope(d[:, Z_KPE:Z_COLS]), d[:, Z_QM:Z_MLA], d[:, 0:Z_GM]], axis=1)


def _wuq_layout(w):
    r = w.reshape(Q_LORA, MLA_HEADS, HEAD + MLA_ROPE)
    return jnp.concatenate([r[:, :, :HEAD].reshape(Q_LORA, -1), _spread_rope(r[:, :, HEAD:]).reshape(Q_LORA, -1)], axis=1)


def _wuq_unlayout(d):
    n = d[:, :MLA_HEADS * HEAD].reshape(Q_LORA, MLA_HEADS, HEAD)
    p = _gather_rope(d[:, MLA_HEADS * HEAD:].reshape(Q_LORA, MLA_HEADS, HEAD))
    return jnp.concatenate([n, p], axis=-1).reshape(Q_LORA, -1)


def _wukv_layout(w):
    r = w.reshape(KV_LORA, MLA_HEADS, 2 * HEAD)
    return jnp.concatenate([r[:, :, :HEAD].reshape(KV_LORA, -1), r[:, :, HEAD:].reshape(KV_LORA, -1)], axis=1)


def _wukv_unlayout(d):
    k = d[:, :MLA_HEADS * HEAD].reshape(KV_LORA, MLA_HEADS, HEAD)
    v = d[:, MLA_HEADS * HEAD:].reshape(KV_LORA, MLA_HEADS, HEAD)
    return jnp.concatenate([k, v], axis=-1).reshape(KV_LORA, -1)


AG_MID = ['w_uq', 'w_ukv', 'w_mem_kv', 'w_o_gm', 'w_o_mla', 'w_o_mem', 'w_out']
AG_FFN = ['w_ff1', 'w_ff2']
RS_GROUPS = {'ffn_proj': ['w_ff2', 'w_ff1', 'w_out', 'w_o_gm', 'w_o_mla', 'w_o_mem'],
             'lat': ['w_uq', 'w_ukv', 'w_mem_kv'], 'in_top': ['w_in'], 'in_bot': ['w_in']}


def _unride(res, rider):
    return (res, None) if rider is None else res


def _local_step(x, mem, positions, target, P, ws):
    B, S, _ = x.shape
    M = mem.shape[1]
    T = B * S
    x2d = x.reshape(T, D_MODEL)
    mem2d = mem.reshape(B * M, D_MODEL)
    tgt2d = target.reshape(T, D_MODEL)

    def row(v):
        return v.reshape(1, -1).astype(F32)

    inv_freq = ROPE_BASE ** (-jnp.arange(0, MLA_ROPE, 2, dtype=F32) / MLA_ROPE)
    zero = jnp.zeros_like(inv_freq)
    ang = positions.reshape(T).astype(F32)[:, None] * jnp.concatenate([inv_freq, zero, inv_freq, zero])
    cc = jnp.cos(ang) * jnp.concatenate([zero + 1.0, zero, zero + 1.0, zero])
    ss = jnp.sin(ang) * jnp.concatenate([zero - 1.0, zero, zero + 1.0, zero])

    g_mix, g_cq, g_ckv, g_ffn, g_mem = row(P['g_mix']), row(P['g_cq']), row(P['g_ckv']), row(P['g_ffn']), row(P['g_mem'])
    gqn, gkn, gmq, gmk = row(P['g_q_nope']), row(P['g_k_nope']), row(P['g_mq']), row(P['g_mk'])
    gqp, gkp = _spread_rope(row(P['g_q_pe'])), _spread_rope(row(P['g_k_pe']))
    gln, bln = row(P['g_gm_ln']), row(P['b_gm_ln'])
    wc = jnp.tril(P['w_spatial'].astype(F32))
    wct = jnp.swapaxes(wc, 1, 2).astype(BF16)
    wc = wc.astype(BF16)
    bst = jnp.broadcast_to(P['b_spatial'].astype(F32)[:, :, None], (GM_GROUPS, GM_CHUNK, LANES))

    ride = ws.gather(['w_in'])
    h, got = _unride(_rms_fwd(x2d, g_mix, "rms_mix", rider=ride), ride)
    w_in = _win_layout(ws.gathered(['w_in'], got)['w_in']).astype(BF16)
    ride = ws.gather(AG_MID)
    z, got = _unride(_matmul(h, w_in, 'nn', ACT, "mm_in", tn_t=768, rider=ride), ride)
    mid = ws.gathered(AG_MID, got)
    w_uq, w_ukv = _wuq_layout(mid['w_uq']).astype(BF16), _wukv_layout(mid['w_ukv']).astype(BF16)
    w_mem_kv, w_o_gm, w_o_mla, w_o_mem, w_out = (mid[n] for n in ('w_mem_kv', 'w_o_gm', 'w_o_mla', 'w_o_mem', 'w_out'))
    ygm_pre = _gm_fwd(z, gln, bln, wc, bst, "gm_fwd")
    y_gm = _matmul(ygm_pre, w_o_gm, 'nn', ACT, "mm_o_gm")
    nq, nkv = _lat_fwd(z, g_cq, g_ckv, "lat_fwd")
    q = _matmul(nq, w_uq, 'nn', ACT, "mm_uq")
    kv = _matmul(nkv, w_ukv, 'nn', ACT, "mm_ukv")
    qcat, kcat, vv = _qk_fwd(q, kv, z, cc, ss, gqn, gqp, gkn, gkp, "qk_fwd")
    ride = ws.gather(AG_FFN)
    (o, lse), got = _unride(_attn_fwd(qcat, kcat, vv, B, MLA_HEADS, QCAT, 0, True, "mla_attn_fwd", rider=ride), ride)
    ffn = ws.gathered(AG_FFN, got)
    w_ff1, w_ff2 = ffn['w_ff1'], ffn['w_ff2']
    y_mla = _matmul(o, w_o_mla, 'nn', ACT, "mm_o_mla")
    nm = _rms_fwd(mem2d, g_mem, "rms_mem")
    kvm = _matmul(nm, w_mem_kv, 'nn', ACT, "mm_mem_kv")
    qm = _headnorm_fwd(z, Z_QM // (MEM_HEADS * HEAD), MEM_HEADS, gmq, MEM_SCALE * LOG2E, "memq_fwd")
    km = _headnorm_fwd(kvm, 0, MEM_HEADS, gmk, 1.0, "memk_fwd")
    om, lse_m = _attn_fwd(qm, km, kvm, B, MEM_HEADS, HEAD, MEM_HEADS, False, "mem_attn_fwd")
    y_mem = _matmul(om, w_o_mem, 'nn', ACT, "mm_o_mem")
    merged = _merge_fwd(z, y_gm, y_mla, y_mem, "merge_fwd")
    x1 = _matmul(merged, w_out, 'nn', F32, "mm_out", add=x2d)
    h2 = _rms_fwd(x1, g_ffn, "rms_ffn")
    a1 = _matmul(h2, w_ff1, 'nn', BF16, "mm_ff1")
    dx2, dx2b, loss_part = _matmul(a1, w_ff2, 'nn', F32, "mm_ff2", add=x1, relu2_a=True, sq_err_target=tgt2d)

    G = {}
    d_ff2 = _matmul(a1, dx2b, 'tn', BF16, "mm_d_ff2", relu2_a=True)
    da1 = _matmul(dx2b, w_ff2, 'nt', BF16, "mm_da1", relu2_grad=a1)
    d_ff1 = _matmul(h2, da1, 'tn', BF16, "mm_d_ff1", col_shards=N_DEV)
    dh2 = _matmul(da1, w_ff1, 'nt', ACT, "mm_dh2")
    dx1, dx1b, G['g_ffn'] = _rms_bwd(x1, g_ffn, dh2, dx2, "rms_ffn_bwd", dx_dtypes=(F32, BF16))
    d_out = _matmul(merged, dx1b, 'tn', BF16, "mm_d_out")
    dmerged = _matmul(dx1b, w_out, 'nt', ACT, "mm_dmerged")
    dy_gm, dy_mla, dy_mem, dz = _merge_bwd(z, y_gm, y_mla, y_mem, dmerged, "merge_bwd")
    d_o_gm = _matmul(ygm_pre, dy_gm, 'tn', BF16, "mm_d_o_gm")
    d_o_mla = _matmul(o, dy_mla, 'tn', BF16, "mm_d_o_mla")
    d_o_mem = _matmul(om, dy_mem, 'tn', BF16, "mm_d_o_mem")
    dygm_pre = _matmul(dy_gm, w_o_gm, 'nt', ACT, "mm_dygm")
    dz, dws, dbs, G['g_gm_ln'], G['b_gm_ln'] = _gm_bwd(z, dygm_pre, gln, bln, wc, wct, bst, dz, "gm_bwd")
    G['w_spatial'] = jnp.tril(dws)
    G['b_spatial'] = jnp.sum(dbs.reshape(GM_CHUNK, GM_GROUPS, LANES), axis=-1).T
    do = _matmul(dy_mla, w_o_mla, 'nt', ACT, "mm_do")
    ride = ws.scatter('ffn_proj', {'w_ff2': d_ff2, 'w_ff1': d_ff1, 'w_out': d_out, 'w_o_gm': d_o_gm, 'w_o_mla': d_o_mla,
                                   'w_o_mem': d_o_mem})
    (dqc, dkc, dvv), got = _unride(_attn_bwd(qcat, kcat, vv, o, do, lse, B, MLA_HEADS, QCAT, 0, MLA_SCALE, True,
                                             "mla_attn_bwd", rider=ride), ride)
    ws.scattered('ffn_proj', got)
    dq, dkv, dkpe, G['g_q_nope'], dgqp, G['g_k_nope'], dgkp = _qk_bwd(q, kv, z, cc, ss, gqn, gqp, gkn, gkp, dqc, dkc, dvv,
                                                                     "qk_bwd")
    G['g_q_pe'], G['g_k_pe'] = _gather_rope(dgqp), _gather_rope(dgkp)
    d_uq = _wuq_unlayout(_matmul(nq, dq, 'tn', BF16, "mm_d_uq"))
    dnq = _matmul(dq, w_uq, 'nt', ACT, "mm_dnq")
    d_ukv = _wukv_unlayout(_matmul(nkv, dkv, 'tn', BF16, "mm_d_ukv"))
    dnkv = _matmul(dkv, w_ukv, 'nt', ACT, "mm_dnkv")
    dz, G['g_cq'], G['g_ckv'] = _lat_bwd(z, dnq, dnkv, dkpe, g_cq, g_ckv, dz, "lat_bwd")
    dom = _matmul(dy_mem, w_o_mem, 'nt', ACT, "mm_dom")
    dqm, dkm, dvm = _attn_bwd(qm, km, kvm, om, dom, lse_m, B, MEM_HEADS, HEAD, MEM_HEADS, MEM_SCALE, False, "mem_attn_bwd")
    dz, G['g_mq'] = _headnorm_bwd(z, Z_QM // (MEM_HEADS * HEAD), MEM_HEADS, gmq, dqm, None, "memq_bwd",
                                  into=(dz, Z_QM // (MEM_HEADS * HEAD)))
    dkvm, G['g_mk'] = _headnorm_bwd(kvm, 0, MEM_HEADS, gmk, dkm, dvm, "memk_bwd")
    d_mem_kv = _matmul(nm, dkvm, 'tn', BF16, "mm_d_mem_kv")
    dnm = _matmul(dkvm, w_mem_kv, 'nt', ACT, "mm_dnm")
    G['g_mem'], = _rms_bwd(mem2d, g_mem, dnm, None, "rms_mem_bwd", dx_dtypes=())
    half = D_MODEL // 2
    ride = ws.scatter('lat', {'w_uq': d_uq, 'w_ukv': d_ukv, 'w_mem_kv': d_mem_kv})
    d_top, got = _unride(_matmul(h, dz, 'tn', BF16, "mm_d_in_top", tn_t=768, m_rows=(0, half), rider=ride), ride)
    ws.scattered('lat', got)
    ride = ws.scatter('in_top', {'w_in': _win_unlayout(d_top)})
    d_bot, got = _unride(_matmul(h, dz, 'tn', BF16, "mm_d_in_bot", tn_t=768, m_rows=(half, half), rider=ride), ride)
    ws.scattered('in_top', got)
    ride = ws.scatter('in_bot', {'w_in': _win_unlayout(d_bot)})
    dh, got = _unride(_matmul(dz, w_in, 'nt', ACT, "mm_dh", rider=ride), ride)
    ws.scattered('in_bot', got)
    gx, G['g_mix'] = _rms_bwd(x2d, g_mix, dh, dx1, "rms_mix_bwd")
    return loss_part, gx.reshape(B, S, D_MODEL), G


def _all_gather8(xs, name):
    exchange = _Exchange([xs], scatter=False)

    def body(x_ref, out_ref, send_sems, recv_sems, local_sems):
        exchange.start([x_ref], [out_ref], send_sems, recv_sems, local_sems)
        exchange.wait([x_ref], [out_ref], send_sems, recv_sems, local_sems)

    return pl.pallas_call(body, name=name, in_specs=[HBM_SPEC], out_specs=HBM_SPEC, out_shape=exchange.out_shapes[0],
                          scratch_shapes=exchange.scratch)(xs)


def _adamw_rows(w, g, m, v):
    m2 = ADAM_B1 * m + (1.0 - ADAM_B1) * g
    v2 = ADAM_B2 * v + (1.0 - ADAM_B2) * (g * g)
    m_hat = m2 / (1.0 - ADAM_B1 ** ADAM_STEP)
    v_hat = v2 / (1.0 - ADAM_B2 ** ADAM_STEP)
    delta = -ADAM_LR * (m_hat / (jnp.sqrt(v_hat) + ADAM_EPS) + ADAM_WD * w)
    return delta, m2, v2


def _sum_adamw(parts, w, m, v, name):
    rows, cols = w.shape
    assert sum(p.shape[1] for p in parts) == rows
    tr = _pick(min(p.shape[1] for p in parts), max(16, 65536 // cols), 16)
    n = parts[0].shape[0]
    counts = [p.shape[1] // tr for p in parts]
    starts = [sum(counts[:k]) for k in range(len(parts))]

    def body(*refs):
        p_refs = refs[:len(parts)]
        w_ref, m_ref, v_ref, g_ref, d_ref, m2_ref, v2_ref = refs[len(parts):]
        g = None
        for p_ref, start in zip(p_refs, starts):
            gk = p_ref[0].astype(F32)
            for k in range(1, n):
                gk = gk + p_ref[k].astype(F32)
            g = gk if g is None else jnp.where(pl.program_id(0) >= start, gk, g)
        delta, m2, v2 = _adamw_rows(w_ref[...], g, m_ref[...], v_ref[...])
        g_ref[...] = g
        d_ref[...] = delta
        m2_ref[...] = m2
        v2_ref[...] = v2

    flat = pl.BlockSpec((tr, cols), lambda i: (i, 0))
    out = jax.ShapeDtypeStruct((rows, cols), F32)
    p_specs = [pl.BlockSpec((n, tr, cols), lambda i, s=s, c=c: (0, jnp.clip(i - s, 0, c - 1), 0))
               for s, c in zip(starts, counts)]
    return pl.pallas_call(
        body, name=name, grid=(rows // tr,), in_specs=p_specs + [flat, flat, flat], out_specs=[flat] * 4,
        out_shape=[out] * 4, compiler_params=_params(("parallel",)),
    )(*parts, w, m, v)


SMALL_WIDTH = {'g_mix': 1024, 'g_cq': 384, 'g_ckv': 256, 'g_q_nope': 128, 'g_q_pe': 128, 'g_k_nope': 128, 'g_k_pe': 128,
               'g_gm_ln': 512, 'b_gm_ln': 512, 'g_mem': 1024, 'g_mq': 128, 'g_mk': 128, 'g_ffn': 1024}
NARROW = ('g_q_pe', 'g_k_pe')


def _small_layout():
    layout, r = {}, 0
    for name in SMALL + ['loss']:
        rows = {'w_spatial': GM_GROUPS * GM_CHUNK, 'b_spatial': GM_GROUPS, 'loss': 1}.get(name) or SMALL_WIDTH[name] // LANES
        layout[name] = (r, rows)
        r += -(-rows // 8) * 8
    return layout, r


def _small_pack(grads, loss_part, name):
    layout, total = _small_layout()
    names = SMALL + ['loss']

    def body(*refs):
        out_ref = refs[-1]
        out_ref[...] = jnp.zeros((total, LANES), F32)
        for ref, n in zip(refs[:-1], names):
            r0, rows = layout[n]
            if n == 'w_spatial':
                for g in range(GM_GROUPS):
                    out_ref[r0 + g * GM_CHUNK:r0 + (g + 1) * GM_CHUNK, :] = ref[g]
            elif n == 'b_spatial':
                out_ref[r0:r0 + rows, :] = ref[...]
            else:
                for k in range(rows):
                    out_ref[r0 + k:r0 + k + 1, :] = ref[:, k * LANES:(k + 1) * LANES]

    return pl.pallas_call(body, name=name, out_shape=jax.ShapeDtypeStruct((total, LANES), F32))(
        *[grads[n] for n in SMALL], loss_part)


def _small_adamw(parts, w, m, v, name):
    layout, _ = _small_layout()
    n_dev = parts.shape[0]

    def body(*refs):
        p_ref = refs[0]
        ins = refs[1:1 + 3 * len(SMALL)]
        outs = refs[1 + 3 * len(SMALL):-1]

        def gsum(r0, rows):
            g = p_ref[0, r0:r0 + rows, :]
            for d in range(1, n_dev):
                g = g + p_ref[d, r0:r0 + rows, :]
            return g

        def step(idx, g, at):
            w_ref, m_ref, v_ref = ins[3 * idx:3 * idx + 3]
            delta, m2, v2 = _adamw_rows(w_ref[at], g, m_ref[at], v_ref[at])
            for ref, val in zip(outs[4 * idx:4 * idx + 4], (g, delta, m2, v2)):
                ref[at] = val

        for idx, n in enumerate(SMALL):
            r0, rows = layout[n]
            if n == 'w_spatial':
                for g in range(GM_GROUPS):
                    step(idx, gsum(r0 + g * GM_CHUNK, GM_CHUNK), (0, g))
            elif n == 'b_spatial':
                step(idx, gsum(r0, rows), (0,))
            else:
                for k in range(rows):
                    step(idx, gsum(r0 + k, 1), (slice(None), slice(k * LANES, (k + 1) * LANES)))
        refs[-1][...] = gsum(layout['loss'][0], 8)

    flat_in = [d[n] for n in SMALL for d in (w, m, v)]
    out_shape = [jax.ShapeDtypeStruct(w[n].shape, F32) for n in SMALL for _ in range(4)]
    res = pl.pallas_call(body, name=name, out_shape=out_shape + [jax.ShapeDtypeStruct((8, LANES), F32)])(parts, *flat_in)
    groups = [{n: res[4 * i + j] for i, n in enumerate(SMALL)} for j in range(4)]
    return groups, res[-1]


def _full_from_gathered(gathered, name):
    r, c = BIG_SHAPE[name]
    if BIG_AXIS[name] == 0:
        return gathered.reshape(r, c)
    return gathered.transpose(1, 0, 2).reshape(r, c)


def _shards_of_full(g, name):
    if g.ndim == 3:
        return g
    r, c = BIG_SHAPE[name]
    if BIG_AXIS[name] == 0:
        return g.reshape(N_DEV, r // N_DEV, c)
    return g.reshape(g.shape[0], N_DEV, c // N_DEV).transpose(1, 0, 2)


class _DistWeights:
    def __init__(self, shards):
        self.shards = shards
        self.received = {}

    def gather(self, names):
        return _Gather2([self.shards[n].astype(BF16) for n in names])

    def gathered(self, names, got):
        return {n: _full_from_gathered(g, n) for n, g in zip(names, got)}

    def scatter(self, key, grads):
        return _Exchange([_shards_of_full(grads[n], n) for n in RS_GROUPS[key]], scatter=True)

    def scattered(self, key, got):
        for n, g in zip(RS_GROUPS[key], got):
            self.received.setdefault(n, []).append(g)


def kernel(x, mem, positions, g_mix, w_in, g_cq, w_uq, g_ckv, w_ukv, g_q_nope, g_q_pe, g_k_nope, g_k_pe, g_gm_ln, b_gm_ln, w_spatial, b_spatial, g_mem, w_mem_kv, g_mq, g_mk, w_o_gm, w_o_mla, w_o_mem, w_out, g_ffn, w_ff1, w_ff2, loss_target, m_g_mix, m_w_in, m_g_cq, m_w_uq, m_g_ckv, m_w_ukv, m_g_q_nope, m_g_q_pe, m_g_k_nope, m_g_k_pe, m_g_gm_ln, m_b_gm_ln, m_w_spatial, m_b_spatial, m_g_mem, m_w_mem_kv, m_g_mq, m_g_mk, m_w_o_gm, m_w_o_mla, m_w_o_mem, m_w_out, m_g_ffn, m_w_ff1, m_w_ff2, v_g_mix, v_w_in, v_g_cq, v_w_uq, v_g_ckv, v_w_ukv, v_g_q_nope, v_g_q_pe, v_g_k_nope, v_g_k_pe, v_g_gm_ln, v_b_gm_ln, v_w_spatial, v_b_spatial, v_g_mem, v_w_mem_kv, v_g_mq, v_g_mk, v_w_o_gm, v_w_o_mla, v_w_o_mem, v_w_out, v_g_ffn, v_w_ff1, v_w_ff2):
    given = dict(locals())
    w = {n: given[n][0] for n in WEIGHTS}
    mom = {n: given['m_' + n][0] for n in WEIGHTS}
    var = {n: given['v_' + n][0] for n in WEIGHTS}

    ws = _DistWeights({n: w[n] for n in BIG})
    loss_part, grad_x, G = _local_step(x, mem, positions, loss_target, {n: w[n] for n in SMALL}, ws)

    outs = {}
    for n in BIG:
        for prefix, res in zip(("grad_", "delta_", "new_m_", "new_v_"),
                               _sum_adamw(ws.received[n], w[n], mom[n], var[n], "adamw_" + n)):
            outs[prefix + n] = res[None]

    def widen(d):
        return {n: (jnp.pad(d[n], ((0, 0), (0, LANES - MLA_ROPE))) if n in NARROW else d[n]) for n in SMALL}

    parts = _all_gather8(_small_pack(widen(G), loss_part, "small_pack"), "ag_small")
    small, loss_rows = _small_adamw(parts, *[widen({n: given[prefix + n] for n in SMALL}) for prefix in ("", "m_", "v_")],
                                    "adamw_small")
    loss = 0.5 * jnp.sum(loss_rows) / D_MODEL
    for prefix, group in zip(("grad_", "delta_", "new_m_", "new_v_"), small):
        for n in SMALL:
            outs[prefix + n] = group[n][:, :MLA_ROPE] if n in NARROW else group[n]
    return (loss, grad_x, *[outs[p + n] for p in ("grad_", "delta_", "new_m_", "new_v_") for n in WEIGHTS])
```

```python
import functools
import math

import jax
import jax.numpy as jnp
from jax import lax
from jax.experimental import pallas as pl
from jax.experimental.pallas import tpu as pltpu

F32 = jnp.float32
BF16 = jnp.bfloat16
ACT = BF16

D_MODEL = 1024
MEM_HEADS = 4
HEAD = 128
GM_WIDTH = 512
GM_CHUNK = 128
GM_GROUPS = 4
MLA_HEADS = 8
MLA_ROPE = 64
Q_LORA = 384
KV_LORA = 256
D_FF = 4096
EPS = 1e-6
ROPE_BASE = 10000.0
MLA_SCALE = 1.0 / math.sqrt(HEAD + MLA_ROPE)
MEM_SCALE = 1.0 / math.sqrt(HEAD)
LOG2E = 1.4426950408889634
LN2 = 0.6931471805599453
ATT_TILE = 256
C_ZU, C_ZV, C_CQ, C_CKV, C_KPE, C_QM, C_ZG, C_END = 0, 512, 1024, 1408, 1664, 1728, 2240, 5312
Z_GM, Z_QM, Z_MLA, Z_KPE, Z_COLS = 3072, 4096, 4608, 5248, 5376
MLA_W = 768
QCAT = 2 * HEAD
ADAM_LR, ADAM_B1, ADAM_B2, ADAM_EPS, ADAM_WD, ADAM_STEP = 0.001, 0.9, 0.999, 1e-08, 0.01, 10
N_DEV = 8
LANES = 128
VMEM_LIMIT = 48 * 1024 * 1024
MAX_K_TILE = 8192
NEG = -1e30

BIG = ['w_in', 'w_uq', 'w_ukv', 'w_mem_kv', 'w_o_gm', 'w_o_mla', 'w_o_mem', 'w_out', 'w_ff1', 'w_ff2']
BIG_AXIS = {'w_in': 1, 'w_uq': 1, 'w_ukv': 1, 'w_mem_kv': 0, 'w_o_gm': 1, 'w_o_mla': 0, 'w_o_mem': 1,
            'w_out': 0, 'w_ff1': 1, 'w_ff2': 0}
BIG_SHAPE = {'w_in': (1024, 5312), 'w_uq': (384, 1536), 'w_ukv': (256, 2048), 'w_mem_kv': (1024, 1024),
             'w_o_gm': (512, 1024), 'w_o_mla': (1024, 1024), 'w_o_mem': (512, 1024), 'w_out': (1024, 1024),
             'w_ff1': (1024, 4096), 'w_ff2': (4096, 1024)}
SMALL = ['g_mix', 'g_cq', 'g_ckv', 'g_q_nope', 'g_q_pe', 'g_k_nope', 'g_k_pe', 'g_gm_ln', 'b_gm_ln',
         'w_spatial', 'b_spatial', 'g_mem', 'g_mq', 'g_mk', 'g_ffn']
WEIGHTS = ['g_mix', 'w_in', 'g_cq', 'w_uq', 'g_ckv', 'w_ukv', 'g_q_nope', 'g_q_pe', 'g_k_nope', 'g_k_pe',
           'g_gm_ln', 'b_gm_ln', 'w_spatial', 'b_spatial', 'g_mem', 'w_mem_kv', 'g_mq', 'g_mk', 'w_o_gm',
           'w_o_mla', 'w_o_mem', 'w_out', 'g_ffn', 'w_ff1', 'w_ff2']


def _pick(n, target, mult=LANES):
    best = None
    t = mult
    while t <= min(n, target):
        if n % t == 0:
            best = t
        t += mult
    return best if best is not None else n


def _params(sem):
    return pltpu.CompilerParams(dimension_semantics=sem, vmem_limit_bytes=VMEM_LIMIT)


MESH = pl.DeviceIdType.MESH
HBM_SPEC = pl.BlockSpec(memory_space=pltpu.HBM)


class _Exchange:
    def __init__(self, srcs, scatter):
        self.srcs, self.scatter = list(srcs), scatter
        self.out_shapes = [jax.ShapeDtypeStruct(s.shape if scatter else (N_DEV,) + s.shape, s.dtype) for s in self.srcs]
        n = len(self.srcs)
        self.scratch = [pltpu.SemaphoreType.DMA((n, N_DEV - 1)), pltpu.SemaphoreType.DMA((n, N_DEV - 1)),
                        pltpu.SemaphoreType.DMA((n,))]

    def _copies(self, src_refs, dst_refs, send_sems, recv_sems, local_sems):
        x, y, c = lax.axis_index("x"), lax.axis_index("y"), lax.axis_index("c")
        me = 4 * x + 2 * y + c
        local, remote = [], []
        for a, (src_ref, dst_ref) in enumerate(zip(src_refs, dst_refs)):
            def mine_for(dev, src_ref=src_ref):
                return src_ref.at[dev] if self.scatter else src_ref

            local.append(pltpu.make_async_copy(mine_for(me), dst_ref.at[me], local_sems.at[a]))
            for k in range(1, N_DEV):
                px = 1 - x if k & 4 else x
                py = 1 - y if k & 2 else y
                pc = 1 - c if k & 1 else c
                remote.append(pltpu.make_async_remote_copy(
                    src_ref=mine_for(4 * px + 2 * py + pc), dst_ref=dst_ref.at[me], send_sem=send_sems.at[a, k - 1],
                    recv_sem=recv_sems.at[a, k - 1], device_id=(px, py, pc), device_id_type=MESH))
        return local, remote

    def start(self, *refs):
        local, remote = self._copies(*refs)
        for cp in local + remote:
            cp.start()

    def wait(self, *refs):
        local, remote = self._copies(*refs)
        for cp in remote + local:
            cp.wait()


class _Gather2:
    def __init__(self, srcs):
        self.srcs = list(srcs)
        self.out_shapes = [jax.ShapeDtypeStruct((N_DEV,) + s.shape, s.dtype) for s in self.srcs]
        n = len(self.srcs)
        self.scratch = [pltpu.SemaphoreType.DMA((n, N_DEV - 1)), pltpu.SemaphoreType.DMA((n, N_DEV - 1)),
                        pltpu.SemaphoreType.DMA((n,))]

    def _plan(self, src_refs, dst_refs, send_sems, recv_sems, local_sems):
        x, y, c = lax.axis_index("x"), lax.axis_index("y"), lax.axis_index("c")
        chips = [(1 - x, y), (x, 1 - y), (1 - x, 1 - y)]
        plans = []
        for a, (src_ref, dst_ref) in enumerate(zip(src_refs, dst_refs)):
            def copy(k, block, to, src=None, a=a, dst_ref=dst_ref):
                at = dst_ref.at[4 * block[0] + 2 * block[1] + block[2]]
                return pltpu.make_async_remote_copy(src_ref=at if src is None else src, dst_ref=at,
                                                    send_sem=send_sems.at[a, k], recv_sem=recv_sems.at[a, k],
                                                    device_id=to, device_id_type=MESH)

            local = pltpu.make_async_copy(src_ref, dst_ref.at[4 * x + 2 * y + c], local_sems.at[a])
            first = [copy(0, (x, y, c), (x, y, 1 - c), src=src_ref)]
            first += [copy(1 + j, (x, y, c), (*chip, c), src=src_ref) for j, chip in enumerate(chips)]
            passed = [copy(4 + j, (*chip, c), (x, y, 1 - c)) for j, chip in enumerate(chips)]
            arrivals = [copy(1 + j, (*chip, c), (x, y, c)) for j, chip in enumerate(chips)]
            late = [copy(0, (x, y, 1 - c), (x, y, c))] + [copy(4 + j, (*chip, 1 - c), (x, y, c)) for j, chip in enumerate(chips)]
            plans.append((local, first, passed, arrivals, late))
        return plans

    def start(self, *refs):
        for local, first, _, _, _ in self._plan(*refs):
            local.start()
            for cp in first:
                cp.start()

    def wait(self, *refs):
        plans = self._plan(*refs)
        for _, _, passed, arrivals, _ in plans:
            for arrived, onward in zip(arrivals, passed):
                arrived.wait_recv()
                onward.start()
        for local, first, passed, _, late in plans:
            for cp in late:
                cp.wait_recv()
            for cp in first + passed:
                cp.wait_send()
            local.wait()


def _call(body, rider, ins, *, name, grid, in_specs, out_specs, out_shape, scratch_shapes, sem):
    if rider is None:
        return pl.pallas_call(body, name=name, grid=grid, in_specs=in_specs, out_specs=out_specs, out_shape=out_shape,
                              scratch_shapes=scratch_shapes, compiler_params=_params(sem))(*ins)
    single = not isinstance(out_shape, (list, tuple))
    own_specs, own_shapes = ([out_specs], [out_shape]) if single else (list(out_specs), list(out_shape))
    n_in, n_out, n_sc, n_r = len(ins), len(own_shapes), len(scratch_shapes), len(rider.srcs)
    n_all_in = n_in + n_r

    def carrying(*refs):
        own_in, srcs = refs[:n_in], refs[n_in:n_in + n_r]
        own_out, dsts = refs[n_all_in:n_all_in + n_out], refs[n_all_in + n_out:n_all_in + n_out + n_r]
        own_sc = refs[n_all_in + n_out + n_r:n_all_in + n_out + n_r + n_sc]
        sems = refs[n_all_in + n_out + n_r + n_sc:]
        first = last = None
        for d, steps in enumerate(grid):
            f, l = pl.program_id(d) == 0, pl.program_id(d) == steps - 1
            first, last = (f, l) if first is None else (first & f, last & l)

        @pl.when(first)
        def _():
            rider.start(srcs, dsts, *sems)

        body(*own_in, *own_out, *own_sc)

        @pl.when(last)
        def _():
            rider.wait(srcs, dsts, *sems)

    res = pl.pallas_call(
        carrying, name=name, grid=grid, in_specs=list(in_specs) + [HBM_SPEC] * n_r,
        out_specs=own_specs + [HBM_SPEC] * n_r, out_shape=own_shapes + rider.out_shapes,
        scratch_shapes=list(scratch_shapes) + rider.scratch, compiler_params=_params(("arbitrary",) * len(grid)),
    )(*ins, *rider.srcs)
    own = res[:n_out]
    return (own[0] if single else list(own)), list(res[n_out:])


def _matmul(a, b, mode, out_dtype, name, add=None, relu2_a=False, relu2_grad=None,
            tm_t=None, tn_t=None, tk_t=None, rider=None, m_rows=None, col_shards=None, sq_err_target=None):
    if mode == 'nn':
        (M, K), (K2, N) = a.shape, b.shape
    elif mode == 'nt':
        (M, K), (N, K2) = a.shape, b.shape
    else:
        (K, M), (K2, N) = a.shape, b.shape
    assert K == K2, (name, a.shape, b.shape)
    m_first = 0
    if m_rows is not None:
        assert mode == 'tn'
        m_first, M = m_rows
    if col_shards is not None:
        assert add is None and relu2_grad is None and sq_err_target is None and tn_t is None
        tn_t = N // col_shards
    if mode == 'tn':
        d_tm, d_tn, d_tk = 1024, 1024, 2048
    else:
        d_tm, d_tn, d_tk = (2048 if K <= 1024 else 1024), 512, MAX_K_TILE
    tm, tn, tk = _pick(M, tm_t or d_tm), _pick(N, tn_t or d_tn), _pick(K, tk_t or d_tk)
    gm, gn, nk = M // tm, N // tn, K // tk
    if mode == 'nn':
        a_spec = pl.BlockSpec((tm, tk), lambda i, j, k: (i, k))
        b_spec = pl.BlockSpec((tk, tn), lambda i, j, k: (k, j))
        dims = (((1,), (0,)), ((), ()))
    elif mode == 'nt':
        a_spec = pl.BlockSpec((tm, tk), lambda i, j, k: (i, k))
        b_spec = pl.BlockSpec((tn, tk), lambda i, j, k: (j, k))
        dims = (((1,), (1,)), ((), ()))
    else:
        assert m_first % tm == 0
        a_spec = pl.BlockSpec((tk, tm), lambda i, j, k: (k, m_first // tm + i))
        b_spec = pl.BlockSpec((tk, tn), lambda i, j, k: (k, j))
        dims = (((0,), (0,)), ((), ()))
    o_spec = pl.BlockSpec((tm, tn), lambda i, j, k: (i, j))
    has_add, has_e, has_t = add is not None, relu2_grad is not None, sq_err_target is not None
    assert not has_t or (nk == 1 and tn % LANES == 0)

    def body(*refs):
        a_ref, b_ref = refs[0], refs[1]
        pos = 2
        add_ref = e_ref = t_ref = None
        if has_add:
            add_ref = refs[pos]
            pos += 1
        if has_e:
            e_ref = refs[pos]
            pos += 1
        if has_t:
            t_ref = refs[pos]
            pos += 1
        o_ref = refs[pos]
        acc_ref = refs[pos + 1] if nk > 1 else None

        av = a_ref[...]
        if relu2_a:
            av = jnp.maximum(av, 0)
            av = av * av
        prod = lax.dot_general(av.astype(BF16), b_ref[...].astype(BF16), dims, preferred_element_type=F32)

        def finish(r):
            if has_add:
                r = r + add_ref[...]
            if has_e:
                r = r * (2.0 * jnp.maximum(e_ref[...].astype(F32), 0.0))
            if has_t:
                err = r - t_ref[...]
                r = err * (1.0 / N)
                refs[pos + 1][...] = r.astype(BF16)
                sq = err * err
                part = sq[:, 0:LANES]
                for c in range(1, tn // LANES):
                    part = part + sq[:, c * LANES:(c + 1) * LANES]
                _acc_rows(refs[pos + 2], part, (pl.program_id(0) == 0) & (pl.program_id(1) == 0))
            o_ref[...] = r.astype(out_dtype)

        if nk == 1:
            finish(prod)
        else:
            k = pl.program_id(2)

            @pl.when(k == 0)
            def _():
                acc_ref[...] = prod

            @pl.when(k > 0)
            def _():
                acc_ref[...] += prod

            @pl.when(k == nk - 1)
            def _():
                finish(acc_ref[...])

    ins, specs = [a, b], [a_spec, b_spec]
    if has_add:
        ins.append(add)
        specs.append(o_spec)
    if has_e:
        ins.append(relu2_grad)
        specs.append(o_spec)
    out_specs, out_shape, sem = o_spec, jax.ShapeDtypeStruct((M, N), out_dtype), ("parallel", "parallel", "arbitrary")
    if has_t:
        ins.append(sq_err_target)
        specs.append(o_spec)
        out_specs = [o_spec, o_spec, pl.BlockSpec((1, LANES), lambda i, j, k: (0, 0))]
        out_shape = [out_shape, jax.ShapeDtypeStruct((M, N), BF16), jax.ShapeDtypeStruct((1, LANES), F32)]
        sem = ("arbitrary", "arbitrary", "arbitrary")
    if col_shards is not None:
        out_specs = pl.BlockSpec((None, tm, tn), lambda i, j, k: (j, i, 0))
        out_shape = jax.ShapeDtypeStruct((col_shards, M, tn), out_dtype)
    return _call(body, rider, ins, name=name, grid=(gm, gn, nk), in_specs=specs, out_specs=out_specs, out_shape=out_shape,
                 scratch_shapes=[pltpu.VMEM((tm, tn), F32)] if nk > 1 else [], sem=sem)


ROW_BLOCK_BYTES = 12 * 1024 * 1024


def _row_tile(rows, row_bytes):
    return _pick(rows, max(16, min(1024, ROW_BLOCK_BYTES // row_bytes)), 16)


def _rowspec(tr, width, col=0):
    return pl.BlockSpec((tr, width), lambda i, col=col: (i, col))


def _fullspec(shape):
    nd = len(shape)
    return pl.BlockSpec(shape, lambda i, nd=nd: (0,) * nd)


def _rms(x, width):
    x = x.astype(F32)
    return lax.rsqrt(jnp.sum(x * x, axis=-1, keepdims=True) * (1.0 / width) + EPS)


def _rms_bwd_rows(x, g, dy, width):
    x, dy = x.astype(F32), dy.astype(F32)
    r = _rms(x, width)
    xh = x * r
    dn = dy * g
    dx = r * (dn - xh * (jnp.sum(dn * xh, axis=-1, keepdims=True) * (1.0 / width)))
    return dx, dy * xh


def _acc_rows(ref, val, first):
    s = jnp.sum(val, axis=0, keepdims=True)

    @pl.when(first)
    def _():
        ref[...] = s

    @pl.when(jnp.logical_not(first))
    def _():
        ref[...] += s


def _rms_fwd(x, g, name, rider=None):
    rows, width = x.shape
    tr = _row_tile(rows, 6 * width)

    def body(x_ref, g_ref, o_ref):
        xv = x_ref[...]
        o_ref[...] = (xv * _rms(xv, width) * g_ref[...]).astype(BF16)

    return _call(body, rider, [x, g], name=name, grid=(rows // tr,),
                 in_specs=[_rowspec(tr, width), _fullspec((1, width))], out_specs=_rowspec(tr, width),
                 out_shape=jax.ShapeDtypeStruct((rows, width), BF16), scratch_shapes=[], sem=("parallel",))


def _rms_bwd(x, g, dy, res, name, dx_dtypes=(F32,)):
    rows, width = x.shape
    tr = _row_tile(rows, 18 * width)
    has_res = res is not None
    n_in = 4 if has_res else 3

    def body(*refs):
        x_ref, g_ref, dy_ref = refs[:3]
        dx, dgv = _rms_bwd_rows(x_ref[...], g_ref[...], dy_ref[...], width)
        if has_res:
            dx = dx + refs[3][...]
        for ref, dt in zip(refs[n_in:], dx_dtypes):
            ref[...] = dx.astype(dt)
        _acc_rows(refs[-1], dgv, pl.program_id(0) == 0)

    ins = [x, g, dy] + ([res] if has_res else [])
    specs = [_rowspec(tr, width), _fullspec((1, width)), _rowspec(tr, width)] + ([_rowspec(tr, width)] if has_res else [])
    return pl.pallas_call(
        body, name=name, grid=(rows // tr,), in_specs=specs,
        out_specs=[_rowspec(tr, width)] * len(dx_dtypes) + [_fullspec((1, width))],
        out_shape=[jax.ShapeDtypeStruct((rows, width), dt) for dt in dx_dtypes] + [jax.ShapeDtypeStruct((1, width), F32)],
        compiler_params=_params(("arbitrary",)),
    )(*ins)


_GELU_C = math.sqrt(2.0 / math.pi)


def _gelu(x):
    t = jnp.tanh(_GELU_C * (x + 0.044715 * (x * x * x)))
    return 0.5 * x * (1.0 + t), t


def _gelu_grad(x, t):
    return 0.5 * (1.0 + t) + 0.5 * x * (1.0 - t * t) * (_GELU_C * (1.0 + 3.0 * 0.044715 * (x * x)))


def _gm_forward_rows(zu, zv, gln, bln, wc_ref, bst, n_chunk):
    u, tu = _gelu(zu)
    a, ta = _gelu(zv)
    mu = jnp.mean(a, axis=-1, keepdims=True)
    ac = a - mu
    rs = lax.rsqrt(jnp.mean(ac * ac, axis=-1, keepdims=True) + EPS)
    n = ac * rs
    v = n * gln + bln
    vb = v.astype(BF16)
    rows = []
    for c in range(n_chunk):
        cols = []
        for g in range(GM_GROUPS):
            vc = vb[c * GM_CHUNK:(c + 1) * GM_CHUNK, g * LANES:(g + 1) * LANES]
            mixed = jnp.dot(wc_ref[g], vc, preferred_element_type=F32) + bst[g]
            cols.append(mixed)
        rows.append(jnp.concatenate(cols, axis=1))
    mixed = jnp.concatenate(rows, axis=0) if n_chunk > 1 else rows[0]
    return u, tu, ta, n, rs, v, mixed


def _gm_fwd(z, gln, bln, wc, bst, name):
    rows = z.shape[0]
    tr = _pick(rows, 512, GM_CHUNK)
    n_chunk = tr // GM_CHUNK

    def body(zu_ref, zv_ref, gln_ref, bln_ref, wc_ref, bst_ref, o_ref):
        u, _, _, _, _, _, mixed = _gm_forward_rows(zu_ref[...].astype(F32), zv_ref[...].astype(F32), gln_ref[...], bln_ref[...], wc_ref,
                                                   bst_ref, n_chunk)
        o_ref[...] = (u * mixed).astype(BF16)

    return pl.pallas_call(
        body, name=name, grid=(rows // tr,),
        in_specs=[_rowspec(tr, GM_WIDTH, Z_GM // GM_WIDTH), _rowspec(tr, GM_WIDTH, Z_GM // GM_WIDTH + 1),_fullspec((1, GM_WIDTH)), _fullspec((1, GM_WIDTH)),
                  _fullspec((GM_GROUPS, GM_CHUNK, GM_CHUNK)), _fullspec((GM_GROUPS, GM_CHUNK, LANES))],
        out_specs=_rowspec(tr, GM_WIDTH), out_shape=jax.ShapeDtypeStruct((rows, GM_WIDTH), BF16),
        compiler_params=_params(("parallel",)),
    )(z, z, gln, bln, wc, bst)


ANY_SPEC = pl.BlockSpec(memory_space=pl.ANY)


def _gm_bwd(z, dy, gln, bln, wc, wct, bst, dz, name):
    rows = z.shape[0]
    tr = _pick(rows, 512, GM_CHUNK)
    n_chunk = tr // GM_CHUNK

    def body(zu_ref, zv_ref, dy_ref, gln_ref, bln_ref, wc_ref, wct_ref, bst_ref, _, dz_ref, dws_ref, dbs_ref, dgl_ref,
             dbl_ref):
        first = pl.program_id(0) == 0
        zu, zv, gln = zu_ref[...].astype(F32), zv_ref[...].astype(F32), gln_ref[...]
        u, tu, ta, n, rs, v, mixed = _gm_forward_rows(zu, zv, gln, bln_ref[...], wc_ref, bst_ref, n_chunk)
        dyv = dy_ref[...].astype(F32)
        dzu = dyv * mixed * _gelu_grad(zu, tu)
        dmix = dyv * u
        dmb = dmix.astype(BF16)
        vb = v.astype(BF16)
        dv_rows, dws, dbs = [], [None] * GM_GROUPS, None
        for c in range(n_chunk):
            rsl = slice(c * GM_CHUNK, (c + 1) * GM_CHUNK)
            cols = []
            for g in range(GM_GROUPS):
                csl = slice(g * LANES, (g + 1) * LANES)
                dmc = dmb[rsl, csl]
                cols.append(jnp.dot(wct_ref[g], dmc, preferred_element_type=F32))
                w_part = lax.dot_general(dmc, vb[rsl, csl], (((1,), (1,)), ((), ())), preferred_element_type=F32)
                dws[g] = w_part if dws[g] is None else dws[g] + w_part
            dv_rows.append(jnp.concatenate(cols, axis=1))
            dbs = dmix[rsl, :] if dbs is None else dbs + dmix[rsl, :]
        dv = jnp.concatenate(dv_rows, axis=0) if n_chunk > 1 else dv_rows[0]
        dn = dv * gln
        da = rs * (dn - jnp.mean(dn, axis=-1, keepdims=True) - n * jnp.mean(dn * n, axis=-1, keepdims=True))
        dzv = da * _gelu_grad(zv, ta)
        dz_ref[:, 0:GM_WIDTH] = dzu.astype(BF16)
        dz_ref[:, GM_WIDTH:2 * GM_WIDTH] = dzv.astype(BF16)
        _acc_rows(dgl_ref, dv * n, first)
        _acc_rows(dbl_ref, dv, first)

        @pl.when(first)
        def _():
            for g in range(GM_GROUPS):
                dws_ref[g] = dws[g]
            dbs_ref[...] = dbs

        @pl.when(jnp.logical_not(first))
        def _():
            for g in range(GM_GROUPS):
                dws_ref[g] += dws[g]
            dbs_ref[...] += dbs

    wspec = _fullspec((GM_GROUPS, GM_CHUNK, GM_CHUNK))
    return pl.pallas_call(
        body, name=name, grid=(rows // tr,),
        in_specs=[_rowspec(tr, GM_WIDTH, Z_GM // GM_WIDTH), _rowspec(tr, GM_WIDTH, Z_GM // GM_WIDTH + 1),
                  _rowspec(tr, GM_WIDTH), _fullspec((1, GM_WIDTH)), _fullspec((1, GM_WIDTH)), wspec, wspec, wspec, ANY_SPEC],
        out_specs=[_rowspec(tr, 2 * GM_WIDTH, Z_GM // (2 * GM_WIDTH)), wspec, _fullspec((GM_CHUNK, GM_WIDTH)),
                   _fullspec((1, GM_WIDTH)), _fullspec((1, GM_WIDTH))],
        out_shape=[jax.ShapeDtypeStruct(dz.shape, dz.dtype), jax.ShapeDtypeStruct((GM_GROUPS, GM_CHUNK, GM_CHUNK), F32),
                   jax.ShapeDtypeStruct((GM_CHUNK, GM_WIDTH), F32), jax.ShapeDtypeStruct((1, GM_WIDTH), F32),
                   jax.ShapeDtypeStruct((1, GM_WIDTH), F32)],
        input_output_aliases={8: 0}, compiler_params=_params(("arbitrary",)),
    )(z, z, dy, gln, bln, wc, wct, bst, dz)


def _lat_fwd(z, g_cq, g_ckv, name):
    rows = z.shape[0]
    tr = _row_tile(rows, 4 * MLA_W)

    def body(z_ref, gq_ref, gkv_ref, nq_ref, nkv_ref):
        zb = z_ref[...]
        cq, ckv = zb[:, 0:Q_LORA], zb[:, Q_LORA:Q_LORA + KV_LORA]
        nq_ref[...] = (cq * _rms(cq, Q_LORA) * gq_ref[...]).astype(BF16)
        nkv_ref[...] = (ckv * _rms(ckv, KV_LORA) * gkv_ref[...]).astype(BF16)

    return pl.pallas_call(
        body, name=name, grid=(rows // tr,),
        in_specs=[_rowspec(tr, MLA_W, Z_MLA // MLA_W), _fullspec((1, Q_LORA)), _fullspec((1, KV_LORA))],
        out_specs=[_rowspec(tr, Q_LORA), _rowspec(tr, KV_LORA)],
        out_shape=[jax.ShapeDtypeStruct((rows, Q_LORA), BF16), jax.ShapeDtypeStruct((rows, KV_LORA), BF16)],
        compiler_params=_params(("parallel",)),
    )(z, g_cq, g_ckv)


def _lat_bwd(z, dnq, dnkv, dkpe, g_cq, g_ckv, dz, name):
    rows = z.shape[0]
    tr = _row_tile(rows, 8 * MLA_W)

    def body(z_ref, dnq_ref, dnkv_ref, dkpe_ref, gq_ref, gkv_ref, _, dz_ref, dgq_ref, dgkv_ref):
        first = pl.program_id(0) == 0
        zb = z_ref[...]
        dcq, dgq = _rms_bwd_rows(zb[:, 0:Q_LORA], gq_ref[...], dnq_ref[...], Q_LORA)
        dckv, dgkv = _rms_bwd_rows(zb[:, Q_LORA:Q_LORA + KV_LORA], gkv_ref[...], dnkv_ref[...], KV_LORA)
        dz_ref[:, 0:Q_LORA] = dcq.astype(BF16)
        dz_ref[:, Q_LORA:Q_LORA + KV_LORA] = dckv.astype(BF16)
        dz_ref[:, Q_LORA + KV_LORA:MLA_W] = dkpe_ref[...].astype(BF16)
        _acc_rows(dgq_ref, dgq, first)
        _acc_rows(dgkv_ref, dgkv, first)

    return pl.pallas_call(
        body, name=name, grid=(rows // tr,),
        in_specs=[_rowspec(tr, MLA_W, Z_MLA // MLA_W), _rowspec(tr, Q_LORA), _rowspec(tr, KV_LORA), _rowspec(tr, LANES),
                  _fullspec((1, Q_LORA)), _fullspec((1, KV_LORA)), ANY_SPEC],
        out_specs=[_rowspec(tr, MLA_W, Z_MLA // MLA_W), _fullspec((1, Q_LORA)), _fullspec((1, KV_LORA))],
        out_shape=[jax.ShapeDtypeStruct(dz.shape, dz.dtype), jax.ShapeDtypeStruct((1, Q_LORA), F32),
                   jax.ShapeDtypeStruct((1, KV_LORA), F32)],
        input_output_aliases={6: 0}, compiler_params=_params(("arbitrary",)),
    )(z, dnq, dnkv, dkpe, g_cq, g_ckv, dz)


def _rope(y, cc, ss):
    return y * cc + pltpu.roll(y, 64, 1) * ss


def _rope_bwd(d, cc, ss):
    return d * cc + pltpu.roll(d * ss, 64, 1)


def _qk_fwd(q, kv, z, cc, ss, gqn, gqp, gkn, gkp, name):
    rows = q.shape[0]
    W = MLA_HEADS * HEAD
    tr = _row_tile(rows, 20 * W)
    QS = MLA_SCALE * LOG2E

    def body(q_ref, kv_ref, kpe_ref, cc_ref, ss_ref, gqn_ref, gqp_ref, gkn_ref, gkp_ref, qc_ref, kc_ref, v_ref):
        cc, ss = cc_ref[...], ss_ref[...]
        kpe = kpe_ref[...]
        kp = _rope(kpe * _rms(kpe, MLA_ROPE) * gkp_ref[...], cc, ss).astype(BF16)
        for h in range(MLA_HEADS):
            qn = q_ref[:, h * HEAD:(h + 1) * HEAD]
            qp = q_ref[:, W + h * HEAD:W + (h + 1) * HEAD]
            kn = kv_ref[:, h * HEAD:(h + 1) * HEAD]
            qc_ref[:, h * QCAT:h * QCAT + HEAD] = (qn * _rms(qn, HEAD) * gqn_ref[...] * QS).astype(BF16)
            qc_ref[:, h * QCAT + HEAD:(h + 1) * QCAT] = (_rope(qp * _rms(qp, MLA_ROPE) * gqp_ref[...], cc, ss) * QS).astype(BF16)
            kc_ref[:, h * QCAT:h * QCAT + HEAD] = (kn * _rms(kn, HEAD) * gkn_ref[...]).astype(BF16)
            kc_ref[:, h * QCAT + HEAD:(h + 1) * QCAT] = kp
        v_ref[...] = kv_ref[:, W:2 * W].astype(BF16)

    g = _fullspec((1, HEAD))
    return _call(
        body, None, [q, kv, z, cc, ss, gqn, gqp, gkn, gkp], name=name, grid=(rows // tr,),
        in_specs=[_rowspec(tr, 2 * W), _rowspec(tr, 2 * W), _rowspec(tr, LANES, Z_KPE // LANES), _rowspec(tr, LANES),
                  _rowspec(tr, LANES), g, g, g, g],
        out_specs=[_rowspec(tr, MLA_HEADS * QCAT), _rowspec(tr, MLA_HEADS * QCAT), _rowspec(tr, W)],
        out_shape=[jax.ShapeDtypeStruct((rows, MLA_HEADS * QCAT), BF16), jax.ShapeDtypeStruct((rows, MLA_HEADS * QCAT), BF16),
                   jax.ShapeDtypeStruct((rows, W), BF16)],
        scratch_shapes=[], sem=("parallel",))


def _qk_bwd(q, kv, z, cc, ss, gqn, gqp, gkn, gkp, dqc, dkc, dv, name):
    rows = q.shape[0]
    W = MLA_HEADS * HEAD
    tr = _row_tile(rows, 40 * W)

    def body(q_ref, kv_ref, kpe_ref, cc_ref, ss_ref, gqn_ref, gqp_ref, gkn_ref, gkp_ref, dqc_ref, dkc_ref, dv_ref,
             dq_ref, dkv_ref, dkpe_ref, dgqn_ref, dgqp_ref, dgkn_ref, dgkp_ref):
        first = pl.program_id(0) == 0
        cc, ss = cc_ref[...], ss_ref[...]
        sqn = sqp = skn = dkp = None
        for h in range(MLA_HEADS):
            dx, dg = _rms_bwd_rows(q_ref[:, h * HEAD:(h + 1) * HEAD], gqn_ref[...], dqc_ref[:, h * QCAT:h * QCAT + HEAD], HEAD)
            dq_ref[:, h * HEAD:(h + 1) * HEAD] = dx.astype(BF16)
            sqn = dg if sqn is None else sqn + dg
            dy = _rope_bwd(dqc_ref[:, h * QCAT + HEAD:(h + 1) * QCAT], cc, ss)
            dx, dg = _rms_bwd_rows(q_ref[:, W + h * HEAD:W + (h + 1) * HEAD], gqp_ref[...], dy, MLA_ROPE)
            dq_ref[:, W + h * HEAD:W + (h + 1) * HEAD] = dx.astype(BF16)
            sqp = dg if sqp is None else sqp + dg
            dx, dg = _rms_bwd_rows(kv_ref[:, h * HEAD:(h + 1) * HEAD], gkn_ref[...], dkc_ref[:, h * QCAT:h * QCAT + HEAD], HEAD)
            dkv_ref[:, h * HEAD:(h + 1) * HEAD] = dx.astype(BF16)
            skn = dg if skn is None else skn + dg
            part = dkc_ref[:, h * QCAT + HEAD:(h + 1) * QCAT].astype(F32)
            dkp = part if dkp is None else dkp + part
        dkv_ref[:, W:2 * W] = dv_ref[...].astype(BF16)
        dx, dg = _rms_bwd_rows(kpe_ref[...], gkp_ref[...], _rope_bwd(dkp, cc, ss), MLA_ROPE)
        dkpe_ref[...] = dx
        _acc_rows(dgqn_ref, sqn, first)
        _acc_rows(dgqp_ref, sqp, first)
        _acc_rows(dgkn_ref, skn, first)
        _acc_rows(dgkp_ref, dg, first)

    g = _fullspec((1, HEAD))
    gs = jax.ShapeDtypeStruct((1, HEAD), F32)
    return pl.pallas_call(
        body, name=name, grid=(rows // tr,),
        in_specs=[_rowspec(tr, 2 * W), _rowspec(tr, 2 * W), _rowspec(tr, LANES, Z_KPE // LANES), _rowspec(tr, LANES),
                  _rowspec(tr, LANES), g, g, g, g, _rowspec(tr, MLA_HEADS * QCAT), _rowspec(tr, MLA_HEADS * QCAT),
                  _rowspec(tr, W)],
        out_specs=[_rowspec(tr, 2 * W), _rowspec(tr, 2 * W), _rowspec(tr, LANES), g, g, g, g],
        out_shape=[jax.ShapeDtypeStruct((rows, 2 * W), BF16), jax.ShapeDtypeStruct((rows, 2 * W), BF16),
                   jax.ShapeDtypeStruct((rows, LANES), F32), gs, gs, gs, gs],
        compiler_params=_params(("arbitrary",)),
    )(q, kv, z, cc, ss, gqn, gqp, gkn, gkp, dqc, dkc, dv)


def _headnorm_fwd(x, col, nheads, g, out_scale, name):
    rows = x.shape[0]
    W = nheads * HEAD
    tr = _row_tile(rows, 6 * W)

    def body(x_ref, g_ref, o_ref):
        for h in range(nheads):
            xv = x_ref[:, h * HEAD:(h + 1) * HEAD]
            o_ref[:, h * HEAD:(h + 1) * HEAD] = (xv * _rms(xv, HEAD) * g_ref[...] * out_scale).astype(BF16)

    return pl.pallas_call(
        body, name=name, grid=(rows // tr,),
        in_specs=[_rowspec(tr, W, col), _fullspec((1, HEAD))], out_specs=_rowspec(tr, W),
        out_shape=jax.ShapeDtypeStruct((rows, W), BF16), compiler_params=_params(("parallel",)),
    )(x, g)


def _headnorm_bwd(x, col, nheads, g, dy, tail, name, into=None):
    rows = x.shape[0]
    W = nheads * HEAD
    tr = _row_tile(rows, 12 * W)
    has_tail = tail is not None
    WO = 2 * W if has_tail else W

    def body(*refs):
        if into is not None:
            x_ref, g_ref, dy_ref, _, dx_ref, dg_ref = refs
        elif has_tail:
            x_ref, g_ref, dy_ref, t_ref, dx_ref, dg_ref = refs
        else:
            x_ref, g_ref, dy_ref, dx_ref, dg_ref = refs
        acc = None
        for h in range(nheads):
            sl = slice(h * HEAD, (h + 1) * HEAD)
            dx, dg = _rms_bwd_rows(x_ref[:, sl], g_ref[...], dy_ref[:, sl], HEAD)
            dx_ref[:, sl] = dx.astype(BF16)
            acc = dg if acc is None else acc + dg
        if has_tail:
            dx_ref[:, W:2 * W] = t_ref[...].astype(BF16)
        _acc_rows(dg_ref, acc, pl.program_id(0) == 0)

    ins = [x, g, dy] + ([tail] if has_tail else [])
    specs = [_rowspec(tr, W, col), _fullspec((1, HEAD)), _rowspec(tr, W)] + ([_rowspec(tr, W)] if has_tail else [])
    dx_spec, dx_shape, aliases = _rowspec(tr, WO), jax.ShapeDtypeStruct((rows, WO), BF16), {}
    if into is not None:
        assert not has_tail
        ins, specs = ins + [into[0]], specs + [ANY_SPEC]
        dx_spec, dx_shape, aliases = _rowspec(tr, W, into[1]), jax.ShapeDtypeStruct(into[0].shape, into[0].dtype), {3: 0}
    return pl.pallas_call(
        body, name=name, grid=(rows // tr,), in_specs=specs,
        out_specs=[dx_spec, _fullspec((1, HEAD))], out_shape=[dx_shape, jax.ShapeDtypeStruct((1, HEAD), F32)],
        input_output_aliases=aliases, compiler_params=_params(("arbitrary",)),
    )(*ins)


def _sigmoid(x):
    return 1.0 / (1.0 + jnp.exp(-x.astype(F32)))


def _merge_fwd(z, y_gm, y_mla, y_mem, name):
    rows = z.shape[0]
    tr = _row_tile(rows, 14 * D_MODEL)

    def body(g0_ref, g1_ref, g2_ref, a_ref, b_ref, c_ref, o_ref):
        m = _sigmoid(g0_ref[...]) * a_ref[...] + _sigmoid(g1_ref[...]) * b_ref[...] + _sigmoid(g2_ref[...]) * c_ref[...]
        o_ref[...] = m.astype(BF16)

    r = _rowspec(tr, D_MODEL)
    return pl.pallas_call(
        body, name=name, grid=(rows // tr,),
        in_specs=[_rowspec(tr, D_MODEL, 0), _rowspec(tr, D_MODEL, 1), _rowspec(tr, D_MODEL, 2),r, r, r],
        out_specs=r, out_shape=jax.ShapeDtypeStruct((rows, D_MODEL), BF16), compiler_params=_params(("parallel",)),
    )(z, z, z, y_gm, y_mla, y_mem)


def _merge_bwd(z, y_gm, y_mla, y_mem, dm, name):
    rows = z.shape[0]
    tr = _row_tile(rows, 24 * D_MODEL)

    def body(g0_ref, g1_ref, g2_ref, a_ref, b_ref, c_ref, dm_ref, da_ref, db_ref, dc_ref, dzg_ref):
        dmv = dm_ref[...].astype(F32)
        for k, (g_ref, y_ref, dy_ref) in enumerate(((g0_ref, a_ref, da_ref), (g1_ref, b_ref, db_ref), (g2_ref, c_ref, dc_ref))):
            s = _sigmoid(g_ref[...])
            dy_ref[...] = (dmv * s).astype(BF16)
            dzg_ref[:, k * D_MODEL:(k + 1) * D_MODEL] = (dmv * y_ref[...] * s * (1.0 - s)).astype(BF16)

    r = _rowspec(tr, D_MODEL)
    o = jax.ShapeDtypeStruct((rows, D_MODEL), BF16)
    return pl.pallas_call(
        body, name=name, grid=(rows // tr,),
        in_specs=[_rowspec(tr, D_MODEL, 0), _rowspec(tr, D_MODEL, 1), _rowspec(tr, D_MODEL, 2),r, r, r, r],
        out_specs=[r, r, r, _rowspec(tr, 3 * D_MODEL, 0)],
        out_shape=[o, o, o, jax.ShapeDtypeStruct((rows, Z_COLS), BF16)],
        compiler_params=_params(("parallel",)),
    )(z, z, z, y_gm, y_mla, y_mem, dm)


_NT = (((1,), (1,)), ((), ()))
_TN = (((0,), (0,)), ((), ()))


def _diag_mask(s):
    row = lax.broadcasted_iota(jnp.int32, s.shape, 0)
    col = lax.broadcasted_iota(jnp.int32, s.shape, 1)
    return jnp.where(row >= col, s, NEG)


def _attn_fwd(q, k, v, nb, nheads, dk, v_col0, causal, name, rider=None):
    S, Skv = q.shape[0] // nb, k.shape[0] // nb
    tq = _pick(Skv, ATT_TILE) if causal else _pick(S, 4 * ATT_TILE)
    nq = S // tq

    def body(q_ref, k_ref, v_ref, o_ref, lse_ref):
        for i in range(nq):
            r0 = i * tq
            qb = q_ref[r0:r0 + tq, :]
            if causal:
                spans = ([(0, r0, False)] if i > 0 else []) + [(r0, r0 + tq, True)]
            else:
                spans = [(0, Skv, False)]
            scores = []
            for a, b, masked in spans:
                s = lax.dot_general(qb, k_ref[a:b, :], _NT, preferred_element_type=F32)
                scores.append(_diag_mask(s) if masked else s)
            m = functools.reduce(jnp.maximum, [jnp.max(s, axis=-1, keepdims=True) for s in scores])
            l = acc = None
            for s, (a, b, _) in zip(scores, spans):
                p = jnp.exp2(s - m)
                lp = jnp.sum(p, axis=-1, keepdims=True)
                ap = jnp.dot(p.astype(BF16), v_ref[a:b, :].astype(BF16), preferred_element_type=F32)
                l, acc = (lp, ap) if l is None else (l + lp, acc + ap)
            o_ref[r0:r0 + tq, :] = (acc / l).astype(BF16)
            lse_ref[r0:r0 + tq, :] = m + jnp.log2(l)

    ins = [q, k, v]
    in_specs = [pl.BlockSpec((S, dk), lambda b, h: (b, h)), pl.BlockSpec((Skv, dk), lambda b, h: (b, h)),
                pl.BlockSpec((Skv, HEAD), lambda b, h: (b, v_col0 + h))]
    out_specs = [pl.BlockSpec((S, HEAD), lambda b, h: (b, h)), pl.BlockSpec((None, S, 1), lambda b, h: (h, b, 0))]
    out_shape = [jax.ShapeDtypeStruct((nb * S, nheads * HEAD), BF16), jax.ShapeDtypeStruct((nheads, nb * S, 1), F32)]
    return _call(body, rider, ins, name=name, grid=(nb, nheads), in_specs=in_specs, out_specs=out_specs,
                 out_shape=out_shape, scratch_shapes=[], sem=("parallel", "parallel"))


def _attn_bwd(q, k, v, o, do, lse, nb, nheads, dk, v_col0, scale, causal, name, rider=None):
    S, Skv = q.shape[0] // nb, k.shape[0] // nb
    tk = _pick(Skv, ATT_TILE)
    nkv = Skv // tk

    def body(q_ref, k_ref, v_ref, o_ref, do_ref, lse_ref, dq_ref, dk_ref, dv_ref, delta_ref, dob_ref, dqa_ref):
        dov = do_ref[...]
        delta_ref[...] = jnp.sum(o_ref[...].astype(F32) * dov.astype(F32), axis=-1, keepdims=True)
        dob_ref[...] = dov.astype(BF16)

        for j in range(nkv):
            c0 = j * tk
            kb = k_ref[c0:c0 + tk, :]
            vb = v_ref[c0:c0 + tk, :].astype(BF16)
            if causal:
                spans = [(c0, c0 + tk, True)] + ([(c0 + tk, S, False)] if c0 + tk < S else [])
            else:
                spans = [(0, S, False)]
            dk_acc = dv_acc = None
            for a, b, masked in spans:
                qb = q_ref[a:b, :]
                dob = dob_ref[a:b, :]
                s = lax.dot_general(qb, kb, _NT, preferred_element_type=F32)
                if masked:
                    s = _diag_mask(s)
                p = jnp.exp2(s - lse_ref[a:b, :])
                dp = lax.dot_general(dob, vb, _NT, preferred_element_type=F32)
                ds = (p * (dp - delta_ref[a:b, :])).astype(BF16)
                dv_p = lax.dot_general(p.astype(BF16), dob, _TN, preferred_element_type=F32)
                dk_p = lax.dot_general(ds, qb, _TN, preferred_element_type=F32)
                dk_acc, dv_acc = (dk_p, dv_p) if dk_acc is None else (dk_acc + dk_p, dv_acc + dv_p)
                dq_p = jnp.dot(ds, kb, preferred_element_type=F32) * scale
                if j == 0:
                    dqa_ref[a:b, :] = dq_p
                else:
                    dqa_ref[a:b, :] += dq_p
            dk_ref[c0:c0 + tk, :] = (dk_acc * LN2).astype(BF16)
            dv_ref[c0:c0 + tk, :] = dv_acc.astype(BF16)
        dq_ref[...] = dqa_ref[...].astype(BF16)

    ins = [q, k, v, o, do, lse]
    in_specs = [pl.BlockSpec((S, dk), lambda b, h: (b, h)), pl.BlockSpec((Skv, dk), lambda b, h: (b, h)),
                pl.BlockSpec((Skv, HEAD), lambda b, h: (b, v_col0 + h)), pl.BlockSpec((S, HEAD), lambda b, h: (b, h)),
                pl.BlockSpec((S, HEAD), lambda b, h: (b, h)), pl.BlockSpec((None, S, 1), lambda b, h: (h, b, 0))]
    out_specs = [pl.BlockSpec((S, dk), lambda b, h: (b, h)), pl.BlockSpec((Skv, dk), lambda b, h: (b, h)),
                 pl.BlockSpec((Skv, HEAD), lambda b, h: (b, h))]
    out_shape = [jax.ShapeDtypeStruct((nb * S, nheads * dk), BF16), jax.ShapeDtypeStruct((nb * Skv, nheads * dk), BF16),
                 jax.ShapeDtypeStruct((nb * Skv, nheads * HEAD), BF16)]
    return _call(body, rider, ins, name=name, grid=(nb, nheads), in_specs=in_specs, out_specs=out_specs,
                 out_shape=out_shape,
                 scratch_shapes=[pltpu.VMEM((S, 1), F32), pltpu.VMEM((S, HEAD), BF16), pltpu.VMEM((S, dk), F32)],
                 sem=("parallel", "parallel"))


def _spread_rope(a):
    zero = jnp.zeros(a.shape[:-1] + (32,), a.dtype)
    return jnp.concatenate([a[..., :32], zero, a[..., 32:], zero], axis=-1)


def _gather_rope(a):
    return jnp.concatenate([a[..., 0:32], a[..., 64:96]], axis=-1)


def _win_layout(w):
    return jnp.concatenate([w[:, C_ZG:C_END], w[:, C_ZU:C_CQ], w[:, C_QM:C_ZG], w[:, C_CQ:C_CKV], w[:, C_CKV:C_KPE],
                            _spread_rope(w[:, C_KPE:C_QM])], axis=1)


def _win_unlayout(d):
    return jnp.concatenate([d[:, Z_GM:Z_QM], d[:, Z_MLA:Z_MLA + Q_LORA], d[:, Z_MLA + Q_LORA:Z_KPE],
                            _gather_rope(d[:, Z_KPE:Z_COLS]), d[:, Z_QM:Z_MLA], d[:, 0:Z_GM]], axis=1)


def _wuq_layout(w):
    r = w.reshape(Q_LORA, MLA_HEADS, HEAD + MLA_ROPE)
    return jnp.concatenate([r[:, :, :HEAD].reshape(Q_LORA, -1), _spread_rope(r[:, :, HEAD:]).reshape(Q_LORA, -1)], axis=1)


def _wuq_unlayout(d):
    n = d[:, :MLA_HEADS * HEAD].reshape(Q_LORA, MLA_HEADS, HEAD)
    p = _gather_rope(d[:, MLA_HEADS * HEAD:].reshape(Q_LORA, MLA_HEADS, HEAD))
    return jnp.concatenate([n, p], axis=-1).reshape(Q_LORA, -1)


def _wukv_layout(w):
    r = w.reshape(KV_LORA, MLA_HEADS, 2 * HEAD)
    return jnp.concatenate([r[:, :, :HEAD].reshape(KV_LORA, -1), r[:, :, HEAD:].reshape(KV_LORA, -1)], axis=1)


def _wukv_unlayout(d):
    k = d[:, :MLA_HEADS * HEAD].reshape(KV_LORA, MLA_HEADS, HEAD)
    v = d[:, MLA_HEADS * HEAD:].reshape(KV_LORA, MLA_HEADS, HEAD)
    return jnp.concatenate([k, v], axis=-1).reshape(KV_LORA, -1)


AG_MID = ['w_uq', 'w_ukv', 'w_mem_kv', 'w_o_gm', 'w_o_mla', 'w_o_mem', 'w_out']
AG_FFN = ['w_ff1', 'w_ff2']
RS_GROUPS = {'ffn_proj': ['w_ff2', 'w_ff1', 'w_out', 'w_o_gm', 'w_o_mla', 'w_o_mem'],
             'lat': ['w_uq', 'w_ukv', 'w_mem_kv'], 'in_top': ['w_in'], 'in_bot': ['w_in']}


def _unride(res, rider):
    return (res, None) if rider is None else res


def _local_step(x, mem, positions, target, P, ws):
    B, S, _ = x.shape
    M = mem.shape[1]
    T = B * S
    x2d = x.reshape(T, D_MODEL)
    mem2d = mem.reshape(B * M, D_MODEL)
    tgt2d = target.reshape(T, D_MODEL)

    def row(v):
        return v.reshape(1, -1).astype(F32)

    inv_freq = ROPE_BASE ** (-jnp.arange(0, MLA_ROPE, 2, dtype=F32) / MLA_ROPE)
    zero = jnp.zeros_like(inv_freq)
    ang = positions.reshape(T).astype(F32)[:, None] * jnp.concatenate([inv_freq, zero, inv_freq, zero])
    cc = jnp.cos(ang) * jnp.concatenate([zero + 1.0, zero, zero + 1.0, zero])
    ss = jnp.sin(ang) * jnp.concatenate([zero - 1.0, zero, zero + 1.0, zero])

    g_mix, g_cq, g_ckv, g_ffn, g_mem = row(P['g_mix']), row(P['g_cq']), row(P['g_ckv']), row(P['g_ffn']), row(P['g_mem'])
    gqn, gkn, gmq, gmk = row(P['g_q_nope']), row(P['g_k_nope']), row(P['g_mq']), row(P['g_mk'])
    gqp, gkp = _spread_rope(row(P['g_q_pe'])), _spread_rope(row(P['g_k_pe']))
    gln, bln = row(P['g_gm_ln']), row(P['b_gm_ln'])
    wc = jnp.tril(P['w_spatial'].astype(F32))
    wct = jnp.swapaxes(wc, 1, 2).astype(BF16)
    wc = wc.astype(BF16)
    bst = jnp.broadcast_to(P['b_spatial'].astype(F32)[:, :, None], (GM_GROUPS, GM_CHUNK, LANES))

    ride = ws.gather(['w_in'])
    h, got = _unride(_rms_fwd(x2d, g_mix, "rms_mix", rider=ride), ride)
    w_in = _win_layout(ws.gathered(['w_in'], got)['w_in']).astype(BF16)
    ride = ws.gather(AG_MID)
    z, got = _unride(_matmul(h, w_in, 'nn', ACT, "mm_in", tn_t=1792, rider=ride), ride)
    mid = ws.gathered(AG_MID, got)
    w_uq, w_ukv = _wuq_layout(mid['w_uq']).astype(BF16), _wukv_layout(mid['w_ukv']).astype(BF16)
    w_mem_kv, w_o_gm, w_o_mla, w_o_mem, w_out = (mid[n] for n in ('w_mem_kv', 'w_o_gm', 'w_o_mla', 'w_o_mem', 'w_out'))
    ygm_pre = _gm_fwd(z, gln, bln, wc, bst, "gm_fwd")
    y_gm = _matmul(ygm_pre, w_o_gm, 'nn', ACT, "mm_o_gm")
    nq, nkv = _lat_fwd(z, g_cq, g_ckv, "lat_fwd")
    q = _matmul(nq, w_uq, 'nn', ACT, "mm_uq")
    kv = _matmul(nkv, w_ukv, 'nn', ACT, "mm_ukv")
    qcat, kcat, vv = _qk_fwd(q, kv, z, cc, ss, gqn, gqp, gkn, gkp, "qk_fwd")
    ride = ws.gather(AG_FFN)
    (o, lse), got = _unride(_attn_fwd(qcat, kcat, vv, B, MLA_HEADS, QCAT, 0, True, "mla_attn_fwd", rider=ride), ride)
    ffn = ws.gathered(AG_FFN, got)
    w_ff1, w_ff2 = ffn['w_ff1'], ffn['w_ff2']
    y_mla = _matmul(o, w_o_mla, 'nn', ACT, "mm_o_mla")
    nm = _rms_fwd(mem2d, g_mem, "rms_mem")
    kvm = _matmul(nm, w_mem_kv, 'nn', ACT, "mm_mem_kv")
    qm = _headnorm_fwd(z, Z_QM // (MEM_HEADS * HEAD), MEM_HEADS, gmq, MEM_SCALE * LOG2E, "memq_fwd")
    km = _headnorm_fwd(kvm, 0, MEM_HEADS, gmk, 1.0, "memk_fwd")
    om, lse_m = _attn_fwd(qm, km, kvm, B, MEM_HEADS, HEAD, MEM_HEADS, False, "mem_attn_fwd")
    y_mem = _matmul(om, w_o_mem, 'nn', ACT, "mm_o_mem")
    merged = _merge_fwd(z, y_gm, y_mla, y_mem, "merge_fwd")
    x1 = _matmul(merged, w_out, 'nn', F32, "mm_out", add=x2d)
    h2 = _rms_fwd(x1, g_ffn, "rms_ffn")
    a1 = _matmul(h2, w_ff1, 'nn', BF16, "mm_ff1", tn_t=1024)
    dx2, dx2b, loss_part = _matmul(a1, w_ff2, 'nn', F32, "mm_ff2", add=x1, relu2_a=True, sq_err_target=tgt2d)

    G = {}
    d_ff2 = _matmul(a1, dx2b, 'tn', BF16, "mm_d_ff2", relu2_a=True)
    da1 = _matmul(dx2b, w_ff2, 'nt', BF16, "mm_da1", relu2_grad=a1, tn_t=1024)
    d_ff1 = _matmul(h2, da1, 'tn', BF16, "mm_d_ff1", col_shards=N_DEV)
    dh2 = _matmul(da1, w_ff1, 'nt', ACT, "mm_dh2", tn_t=1024)
    dx1, dx1b, G['g_ffn'] = _rms_bwd(x1, g_ffn, dh2, dx2, "rms_ffn_bwd", dx_dtypes=(F32, BF16))
    d_out = _matmul(merged, dx1b, 'tn', BF16, "mm_d_out")
    dmerged = _matmul(dx1b, w_out, 'nt', ACT, "mm_dmerged")
    dy_gm, dy_mla, dy_mem, dz = _merge_bwd(z, y_gm, y_mla, y_mem, dmerged, "merge_bwd")
    d_o_gm = _matmul(ygm_pre, dy_gm, 'tn', BF16, "mm_d_o_gm")
    d_o_mla = _matmul(o, dy_mla, 'tn', BF16, "mm_d_o_mla")
    d_o_mem = _matmul(om, dy_mem, 'tn', BF16, "mm_d_o_mem")
    dygm_pre = _matmul(dy_gm, w_o_gm, 'nt', ACT, "mm_dygm")
    dz, dws, dbs, G['g_gm_ln'], G['b_gm_ln'] = _gm_bwd(z, dygm_pre, gln, bln, wc, wct, bst, dz, "gm_bwd")
    G['w_spatial'] = jnp.tril(dws)
    G['b_spatial'] = jnp.sum(dbs.reshape(GM_CHUNK, GM_GROUPS, LANES), axis=-1).T
    do = _matmul(dy_mla, w_o_mla, 'nt', ACT, "mm_do")
    ride = ws.scatter('ffn_proj', {'w_ff2': d_ff2, 'w_ff1': d_ff1, 'w_out': d_out, 'w_o_gm': d_o_gm, 'w_o_mla': d_o_mla,
                                   'w_o_mem': d_o_mem})
    (dqc, dkc, dvv), got = _unride(_attn_bwd(qcat, kcat, vv, o, do, lse, B, MLA_HEADS, QCAT, 0, MLA_SCALE, True,
                                             "mla_attn_bwd", rider=ride), ride)
    ws.scattered('ffn_proj', got)
    dq, dkv, dkpe, G['g_q_nope'], dgqp, G['g_k_nope'], dgkp = _qk_bwd(q, kv, z, cc, ss, gqn, gqp, gkn, gkp, dqc, dkc, dvv,
                                                                     "qk_bwd")
    G['g_q_pe'], G['g_k_pe'] = _gather_rope(dgqp), _gather_rope(dgkp)
    d_uq = _wuq_unlayout(_matmul(nq, dq, 'tn', BF16, "mm_d_uq"))
    dnq = _matmul(dq, w_uq, 'nt', ACT, "mm_dnq")
    d_ukv = _wukv_unlayout(_matmul(nkv, dkv, 'tn', BF16, "mm_d_ukv"))
    dnkv = _matmul(dkv, w_ukv, 'nt', ACT, "mm_dnkv")
    dz, G['g_cq'], G['g_ckv'] = _lat_bwd(z, dnq, dnkv, dkpe, g_cq, g_ckv, dz, "lat_bwd")
    dom = _matmul(dy_mem, w_o_mem, 'nt', ACT, "mm_dom")
    dqm, dkm, dvm = _attn_bwd(qm, km, kvm, om, dom, lse_m, B, MEM_HEADS, HEAD, MEM_HEADS, MEM_SCALE, False, "mem_attn_bwd")
    dz, G['g_mq'] = _headnorm_bwd(z, Z_QM // (MEM_HEADS * HEAD), MEM_HEADS, gmq, dqm, None, "memq_bwd",
                                  into=(dz, Z_QM // (MEM_HEADS * HEAD)))
    dkvm, G['g_mk'] = _headnorm_bwd(kvm, 0, MEM_HEADS, gmk, dkm, dvm, "memk_bwd")
    d_mem_kv = _matmul(nm, dkvm, 'tn', BF16, "mm_d_mem_kv")
    dnm = _matmul(dkvm, w_mem_kv, 'nt', ACT, "mm_dnm")
    G['g_mem'], = _rms_bwd(mem2d, g_mem, dnm, None, "rms_mem_bwd", dx_dtypes=())
    half = D_MODEL // 2
    ride = ws.scatter('lat', {'w_uq': d_uq, 'w_ukv': d_ukv, 'w_mem_kv': d_mem_kv})
    d_top, got = _unride(_matmul(h, dz, 'tn', BF16, "mm_d_in_top", tn_t=1792, m_rows=(0, half), rider=ride), ride)
    ws.scattered('lat', got)
    ride = ws.scatter('in_top', {'w_in': _win_unlayout(d_top)})
    d_bot, got = _unride(_matmul(h, dz, 'tn', BF16, "mm_d_in_bot", tn_t=1792, m_rows=(half, half), rider=ride), ride)
    ws.scattered('in_top', got)
    ride = ws.scatter('in_bot', {'w_in': _win_unlayout(d_bot)})
    dh, got = _unride(_matmul(dz, w_in, 'nt', ACT, "mm_dh", rider=ride), ride)
    ws.scattered('in_bot', got)
    gx, G['g_mix'] = _rms_bwd(x2d, g_mix, dh, dx1, "rms_mix_bwd")
    return loss_part, gx.reshape(B, S, D_MODEL), G


def _all_gather8(xs, name):
    def body(x_ref, out_ref, send_sems, recv_sems, local_sem):
        x, y, c = lax.axis_index("x"), lax.axis_index("y"), lax.axis_index("c")
        me, sibling = (x, y, c), (x, y, 1 - c)
        chips = [(1 - x, y), (x, 1 - y), (1 - x, 1 - y)]

        def rows(px, py, pc):
            return out_ref.at[4 * px + 2 * py + pc]

        def copy(k, block, to, src=None):
            return pltpu.make_async_remote_copy(
                src_ref=rows(*block) if src is None else src, dst_ref=rows(*block),
                send_sem=send_sems.at[k], recv_sem=recv_sems.at[k], device_id=to, device_id_type=MESH)

        mine = pltpu.make_async_copy(x_ref, rows(*me), local_sem)
        mine.start()
        first = [copy(0, me, sibling, src=x_ref)]
        first += [copy(1 + j, me, (*chip, c), src=x_ref) for j, chip in enumerate(chips)]
        for cp in first:
            cp.start()
        passed = [copy(4 + j, (*chip, c), sibling) for j, chip in enumerate(chips)]
        for j, chip in enumerate(chips):
            copy(1 + j, (*chip, c), me).wait_recv()
            passed[j].start()
        copy(0, sibling, me).wait_recv()
        for j, chip in enumerate(chips):
            copy(4 + j, (*chip, 1 - c), me).wait_recv()
        for cp in first + passed:
            cp.wait_send()
        mine.wait()

    return pl.pallas_call(
        body, name=name, in_specs=[HBM_SPEC], out_specs=HBM_SPEC,
        out_shape=jax.ShapeDtypeStruct((N_DEV,) + xs.shape, xs.dtype),
        scratch_shapes=[pltpu.SemaphoreType.DMA((7,)), pltpu.SemaphoreType.DMA((7,)), pltpu.SemaphoreType.DMA],
    )(xs)


def _adamw_rows(w, g, m, v):
    m2 = ADAM_B1 * m + (1.0 - ADAM_B1) * g
    v2 = ADAM_B2 * v + (1.0 - ADAM_B2) * (g * g)
    m_hat = m2 / (1.0 - ADAM_B1 ** ADAM_STEP)
    v_hat = v2 / (1.0 - ADAM_B2 ** ADAM_STEP)
    delta = -ADAM_LR * (m_hat / (jnp.sqrt(v_hat) + ADAM_EPS) + ADAM_WD * w)
    return delta, m2, v2


def _sum_adamw(parts, w, m, v, name):
    rows, cols = w.shape
    assert sum(p.shape[1] for p in parts) == rows
    tr = _pick(min(p.shape[1] for p in parts), max(16, 65536 // cols), 16)
    n = parts[0].shape[0]
    counts = [p.shape[1] // tr for p in parts]
    starts = [sum(counts[:k]) for k in range(len(parts))]

    def body(*refs):
        p_refs = refs[:len(parts)]
        w_ref, m_ref, v_ref, g_ref, d_ref, m2_ref, v2_ref = refs[len(parts):]
        g = None
        for p_ref, start in zip(p_refs, starts):
            gk = p_ref[0].astype(F32)
            for k in range(1, n):
                gk = gk + p_ref[k].astype(F32)
            g = gk if g is None else jnp.where(pl.program_id(0) >= start, gk, g)
        delta, m2, v2 = _adamw_rows(w_ref[...], g, m_ref[...], v_ref[...])
        g_ref[...] = g
        d_ref[...] = delta
        m2_ref[...] = m2
        v2_ref[...] = v2

    flat = pl.BlockSpec((tr, cols), lambda i: (i, 0))
    out = jax.ShapeDtypeStruct((rows, cols), F32)
    p_specs = [pl.BlockSpec((n, tr, cols), lambda i, s=s, c=c: (0, jnp.clip(i - s, 0, c - 1), 0))
               for s, c in zip(starts, counts)]
    return pl.pallas_call(
        body, name=name, grid=(rows // tr,), in_specs=p_specs + [flat, flat, flat], out_specs=[flat] * 4,
        out_shape=[out] * 4, compiler_params=_params(("parallel",)),
    )(*parts, w, m, v)


SMALL_WIDTH = {'g_mix': 1024, 'g_cq': 384, 'g_ckv': 256, 'g_q_nope': 128, 'g_q_pe': 128, 'g_k_nope': 128, 'g_k_pe': 128,
               'g_gm_ln': 512, 'b_gm_ln': 512, 'g_mem': 1024, 'g_mq': 128, 'g_mk': 128, 'g_ffn': 1024}
NARROW = ('g_q_pe', 'g_k_pe')


def _small_layout():
    layout, r = {}, 0
    for name in SMALL + ['loss']:
        rows = {'w_spatial': GM_GROUPS * GM_CHUNK, 'b_spatial': GM_GROUPS, 'loss': 1}.get(name) or SMALL_WIDTH[name] // LANES
        layout[name] = (r, rows)
        r += -(-rows // 8) * 8
    return layout, r


def _small_pack(grads, loss_part, name):
    layout, total = _small_layout()
    names = SMALL + ['loss']

    def body(*refs):
        out_ref = refs[-1]
        out_ref[...] = jnp.zeros((total, LANES), F32)
        for ref, n in zip(refs[:-1], names):
            r0, rows = layout[n]
            if n == 'w_spatial':
                for g in range(GM_GROUPS):
                    out_ref[r0 + g * GM_CHUNK:r0 + (g + 1) * GM_CHUNK, :] = ref[g]
            elif n == 'b_spatial':
                out_ref[r0:r0 + rows, :] = ref[...]
            else:
                for k in range(rows):
                    out_ref[r0 + k:r0 + k + 1, :] = ref[:, k * LANES:(k + 1) * LANES]

    return pl.pallas_call(body, name=name, out_shape=jax.ShapeDtypeStruct((total, LANES), F32))(
        *[grads[n] for n in SMALL], loss_part)


def _small_adamw(parts, w, m, v, name):
    layout, _ = _small_layout()
    n_dev = parts.shape[0]

    def body(*refs):
        p_ref = refs[0]
        ins = refs[1:1 + 3 * len(SMALL)]
        outs = refs[1 + 3 * len(SMALL):-1]

        def gsum(r0, rows):
            g = p_ref[0, r0:r0 + rows, :]
            for d in range(1, n_dev):
                g = g + p_ref[d, r0:r0 + rows, :]
            return g

        def step(idx, g, at):
            w_ref, m_ref, v_ref = ins[3 * idx:3 * idx + 3]
            delta, m2, v2 = _adamw_rows(w_ref[at], g, m_ref[at], v_ref[at])
            for ref, val in zip(outs[4 * idx:4 * idx + 4], (g, delta, m2, v2)):
                ref[at] = val

        for idx, n in enumerate(SMALL):
            r0, rows = layout[n]
            if n == 'w_spatial':
                for g in range(GM_GROUPS):
                    step(idx, gsum(r0 + g * GM_CHUNK, GM_CHUNK), (0, g))
            elif n == 'b_spatial':
                step(idx, gsum(r0, rows), (0,))
            else:
                for k in range(rows):
                    step(idx, gsum(r0 + k, 1), (slice(None), slice(k * LANES, (k + 1) * LANES)))
        refs[-1][...] = gsum(layout['loss'][0], 8)

    flat_in = [d[n] for n in SMALL for d in (w, m, v)]
    out_shape = [jax.ShapeDtypeStruct(w[n].shape, F32) for n in SMALL for _ in range(4)]
    res = pl.pallas_call(body, name=name, out_shape=out_shape + [jax.ShapeDtypeStruct((8, LANES), F32)])(parts, *flat_in)
    groups = [{n: res[4 * i + j] for i, n in enumerate(SMALL)} for j in range(4)]
    return groups, res[-1]


def _full_from_gathered(gathered, name):
    r, c = BIG_SHAPE[name]
    if BIG_AXIS[name] == 0:
        return gathered.reshape(r, c)
    return gathered.transpose(1, 0, 2).reshape(r, c)


def _shards_of_full(g, name):
    if g.ndim == 3:
        return g
    r, c = BIG_SHAPE[name]
    if BIG_AXIS[name] == 0:
        return g.reshape(N_DEV, r // N_DEV, c)
    return g.reshape(g.shape[0], N_DEV, c // N_DEV).transpose(1, 0, 2)


class _DistWeights:
    def __init__(self, shards):
        self.shards = shards
        self.received = {}

    def gather(self, names):
        return _Gather2([self.shards[n].astype(BF16) for n in names])

    def gathered(self, names, got):
        return {n: _full_from_gathered(g, n) for n, g in zip(names, got)}

    def scatter(self, key, grads):
        return _Exchange([_shards_of_full(grads[n], n) for n in RS_GROUPS[key]], scatter=True)

    def scattered(self, key, got):
        for n, g in zip(RS_GROUPS[key], got):
            self.received.setdefault(n, []).append(g)


def kernel(x, mem, positions, g_mix, w_in, g_cq, w_uq, g_ckv, w_ukv, g_q_nope, g_q_pe, g_k_nope, g_k_pe, g_gm_ln, b_gm_ln, w_spatial, b_spatial, g_mem, w_mem_kv, g_mq, g_mk, w_o_gm, w_o_mla, w_o_mem, w_out, g_ffn, w_ff1, w_ff2, loss_target, m_g_mix, m_w_in, m_g_cq, m_w_uq, m_g_ckv, m_w_ukv, m_g_q_nope, m_g_q_pe, m_g_k_nope, m_g_k_pe, m_g_gm_ln, m_b_gm_ln, m_w_spatial, m_b_spatial, m_g_mem, m_w_mem_kv, m_g_mq, m_g_mk, m_w_o_gm, m_w_o_mla, m_w_o_mem, m_w_out, m_g_ffn, m_w_ff1, m_w_ff2, v_g_mix, v_w_in, v_g_cq, v_w_uq, v_g_ckv, v_w_ukv, v_g_q_nope, v_g_q_pe, v_g_k_nope, v_g_k_pe, v_g_gm_ln, v_b_gm_ln, v_w_spatial, v_b_spatial, v_g_mem, v_w_mem_kv, v_g_mq, v_g_mk, v_w_o_gm, v_w_o_mla, v_w_o_mem, v_w_out, v_g_ffn, v_w_ff1, v_w_ff2):
    given = dict(locals())
    w = {n: given[n][0] for n in WEIGHTS}
    mom = {n: given['m_' + n][0] for n in WEIGHTS}
    var = {n: given['v_' + n][0] for n in WEIGHTS}

    ws = _DistWeights({n: w[n] for n in BIG})
    loss_part, grad_x, G = _local_step(x, mem, positions, loss_target, {n: w[n] for n in SMALL}, ws)

    outs = {}
    for n in BIG:
        for prefix, res in zip(("grad_", "delta_", "new_m_", "new_v_"),
                               _sum_adamw(ws.received[n], w[n], mom[n], var[n], "adamw_" + n)):
            outs[prefix + n] = res[None]

    def widen(d):
        return {n: (jnp.pad(d[n], ((0, 0), (0, LANES - MLA_ROPE))) if n in NARROW else d[n]) for n in SMALL}

    parts = _all_gather8(_small_pack(widen(G), loss_part, "small_pack"), "ag_small")
    small, loss_rows = _small_adamw(parts, *[widen({n: given[prefix + n] for n in SMALL}) for prefix in ("", "m_", "v_")],
                                    "adamw_small")
    loss = 0.5 * jnp.sum(loss_rows) / D_MODEL
    for prefix, group in zip(("grad_", "delta_", "new_m_", "new_v_"), small):
        for n in SMALL:
            outs[prefix + n] = group[n][:, :MLA_ROPE] if n in NARROW else group[n]
    return (loss, grad_x, *[outs[p + n] for p in ("grad_", "delta_", "new_m_", "new_v_") for n in WEIGHTS])
```

```python
import functools
import math

import jax
import jax.numpy as jnp
from jax import lax
from jax.experimental import pallas as pl
from jax.experimental.pallas import tpu as pltpu

F32 = jnp.float32
BF16 = jnp.bfloat16
ACT = BF16

D_MODEL = 1024
MEM_HEADS = 4
HEAD = 128
GM_WIDTH = 512
GM_CHUNK = 128
GM_GROUPS = 4
MLA_HEADS = 8
MLA_ROPE = 64
Q_LORA = 384
KV_LORA = 256
D_FF = 4096
EPS = 1e-6
ROPE_BASE = 10000.0
MLA_SCALE = 1.0 / math.sqrt(HEAD + MLA_ROPE)
MEM_SCALE = 1.0 / math.sqrt(HEAD)
LOG2E = 1.4426950408889634
LN2 = 0.6931471805599453
ATT_TILE = 256
C_ZU, C_ZV, C_CQ, C_CKV, C_KPE, C_QM, C_ZG, C_END = 0, 512, 1024, 1408, 1664, 1728, 2240, 5312
Z_GM, Z_QM, Z_MLA, Z_KPE, Z_COLS = 3072, 4096, 4608, 5248, 5376
MLA_W = 768
QCAT = 2 * HEAD
ADAM_LR, ADAM_B1, ADAM_B2, ADAM_EPS, ADAM_WD, ADAM_STEP = 0.001, 0.9, 0.999, 1e-08, 0.01, 10
N_DEV = 8
LANES = 128
VMEM_LIMIT = 48 * 1024 * 1024
MAX_K_TILE = 8192
NEG = -1e30

BIG = ['w_in', 'w_uq', 'w_ukv', 'w_mem_kv', 'w_o_gm', 'w_o_mla', 'w_o_mem', 'w_out', 'w_ff1', 'w_ff2']
BIG_AXIS = {'w_in': 1, 'w_uq': 1, 'w_ukv': 1, 'w_mem_kv': 0, 'w_o_gm': 1, 'w_o_mla': 0, 'w_o_mem': 1,
            'w_out': 0, 'w_ff1': 1, 'w_ff2': 0}
BIG_SHAPE = {'w_in': (1024, 5312), 'w_uq': (384, 1536), 'w_ukv': (256, 2048), 'w_mem_kv': (1024, 1024),
             'w_o_gm': (512, 1024), 'w_o_mla': (1024, 1024), 'w_o_mem': (512, 1024), 'w_out': (1024, 1024),
             'w_ff1': (1024, 4096), 'w_ff2': (4096, 1024)}
SMALL = ['g_mix', 'g_cq', 'g_ckv', 'g_q_nope', 'g_q_pe', 'g_k_nope', 'g_k_pe', 'g_gm_ln', 'b_gm_ln',
         'w_spatial', 'b_spatial', 'g_mem', 'g_mq', 'g_mk', 'g_ffn']
WEIGHTS = ['g_mix', 'w_in', 'g_cq', 'w_uq', 'g_ckv', 'w_ukv', 'g_q_nope', 'g_q_pe', 'g_k_nope', 'g_k_pe',
           'g_gm_ln', 'b_gm_ln', 'w_spatial', 'b_spatial', 'g_mem', 'w_mem_kv', 'g_mq', 'g_mk', 'w_o_gm',
           'w_o_mla', 'w_o_mem', 'w_out', 'g_ffn', 'w_ff1', 'w_ff2']


def _pick(n, target, mult=LANES):
    best = None
    t = mult
    while t <= min(n, target):
        if n % t == 0:
            best = t
        t += mult
    return best if best is not None else n


def _params(sem):
    return pltpu.CompilerParams(dimension_semantics=sem, vmem_limit_bytes=VMEM_LIMIT)


MESH = pl.DeviceIdType.MESH
HBM_SPEC = pl.BlockSpec(memory_space=pltpu.HBM)


class _Exchange:
    def __init__(self, srcs, scatter):
        self.srcs, self.scatter = list(srcs), scatter
        self.out_shapes = [jax.ShapeDtypeStruct(s.shape if scatter else (N_DEV,) + s.shape, s.dtype) for s in self.srcs]
        n = len(self.srcs)
        self.scratch = [pltpu.SemaphoreType.DMA((n, N_DEV - 1)), pltpu.SemaphoreType.DMA((n, N_DEV - 1)),
                        pltpu.SemaphoreType.DMA((n,))]

    def _copies(self, src_refs, dst_refs, send_sems, recv_sems, local_sems):
        x, y, c = lax.axis_index("x"), lax.axis_index("y"), lax.axis_index("c")
        me = 4 * x + 2 * y + c
        local, remote = [], []
        for a, (src_ref, dst_ref) in enumerate(zip(src_refs, dst_refs)):
            def mine_for(dev, src_ref=src_ref):
                return src_ref.at[dev] if self.scatter else src_ref

            local.append(pltpu.make_async_copy(mine_for(me), dst_ref.at[me], local_sems.at[a]))
            for k in range(1, N_DEV):
                px = 1 - x if k & 4 else x
                py = 1 - y if k & 2 else y
                pc = 1 - c if k & 1 else c
                remote.append(pltpu.make_async_remote_copy(
                    src_ref=mine_for(4 * px + 2 * py + pc), dst_ref=dst_ref.at[me], send_sem=send_sems.at[a, k - 1],
                    recv_sem=recv_sems.at[a, k - 1], device_id=(px, py, pc), device_id_type=MESH))
        return local, remote

    def start(self, *refs):
        local, remote = self._copies(*refs)
        for cp in local + remote:
            cp.start()

    def wait(self, *refs):
        local, remote = self._copies(*refs)
        for cp in remote + local:
            cp.wait()


class _Gather2:
    def __init__(self, srcs):
        self.srcs = list(srcs)
        self.out_shapes = [jax.ShapeDtypeStruct((N_DEV,) + s.shape, s.dtype) for s in self.srcs]
        n = len(self.srcs)
        self.scratch = [pltpu.SemaphoreType.DMA((n, N_DEV - 1)), pltpu.SemaphoreType.DMA((n, N_DEV - 1)),
                        pltpu.SemaphoreType.DMA((n,))]

    def _plan(self, src_refs, dst_refs, send_sems, recv_sems, local_sems):
        x, y, c = lax.axis_index("x"), lax.axis_index("y"), lax.axis_index("c")
        chips = [(1 - x, y), (x, 1 - y), (1 - x, 1 - y)]
        plans = []
        for a, (src_ref, dst_ref) in enumerate(zip(src_refs, dst_refs)):
            def copy(k, block, to, src=None, a=a, dst_ref=dst_ref):
                at = dst_ref.at[4 * block[0] + 2 * block[1] + block[2]]
                return pltpu.make_async_remote_copy(src_ref=at if src is None else src, dst_ref=at,
                                                    send_sem=send_sems.at[a, k], recv_sem=recv_sems.at[a, k],
                                                    device_id=to, device_id_type=MESH)

            local = pltpu.make_async_copy(src_ref, dst_ref.at[4 * x + 2 * y + c], local_sems.at[a])
            first = [copy(0, (x, y, c), (x, y, 1 - c), src=src_ref)]
            first += [copy(1 + j, (x, y, c), (*chip, c), src=src_ref) for j, chip in enumerate(chips)]
            passed = [copy(4 + j, (*chip, c), (x, y, 1 - c)) for j, chip in enumerate(chips)]
            arrivals = [copy(1 + j, (*chip, c), (x, y, c)) for j, chip in enumerate(chips)]
            late = [copy(0, (x, y, 1 - c), (x, y, c))] + [copy(4 + j, (*chip, 1 - c), (x, y, c)) for j, chip in enumerate(chips)]
            plans.append((local, first, passed, arrivals, late))
        return plans

    def start(self, *refs):
        for local, first, _, _, _ in self._plan(*refs):
            local.start()
            for cp in first:
                cp.start()

    def wait(self, *refs):
        plans = self._plan(*refs)
        for _, _, passed, arrivals, _ in plans:
            for arrived, onward in zip(arrivals, passed):
                arrived.wait_recv()
                onward.start()
        for local, first, passed, _, late in plans:
            for cp in late:
                cp.wait_recv()
            for cp in first + passed:
                cp.wait_send()
            local.wait()


def _call(body, rider, ins, *, name, grid, in_specs, out_specs, out_shape, scratch_shapes, sem):
    if rider is None:
        return pl.pallas_call(body, name=name, grid=grid, in_specs=in_specs, out_specs=out_specs, out_shape=out_shape,
                              scratch_shapes=scratch_shapes, compiler_params=_params(sem))(*ins)
    single = not isinstance(out_shape, (list, tuple))
    own_specs, own_shapes = ([out_specs], [out_shape]) if single else (list(out_specs), list(out_shape))
    n_in, n_out, n_sc, n_r = len(ins), len(own_shapes), len(scratch_shapes), len(rider.srcs)
    n_all_in = n_in + n_r

    def carrying(*refs):
        own_in, srcs = refs[:n_in], refs[n_in:n_in + n_r]
        own_out, dsts = refs[n_all_in:n_all_in + n_out], refs[n_all_in + n_out:n_all_in + n_out + n_r]
        own_sc = refs[n_all_in + n_out + n_r:n_all_in + n_out + n_r + n_sc]
        sems = refs[n_all_in + n_out + n_r + n_sc:]
        first = last = None
        for d, steps in enumerate(grid):
            f, l = pl.program_id(d) == 0, pl.program_id(d) == steps - 1
            first, last = (f, l) if first is None else (first & f, last & l)

        @pl.when(first)
        def _():
            rider.start(srcs, dsts, *sems)

        body(*own_in, *own_out, *own_sc)

        @pl.when(last)
        def _():
            rider.wait(srcs, dsts, *sems)

    res = pl.pallas_call(
        carrying, name=name, grid=grid, in_specs=list(in_specs) + [HBM_SPEC] * n_r,
        out_specs=own_specs + [HBM_SPEC] * n_r, out_shape=own_shapes + rider.out_shapes,
        scratch_shapes=list(scratch_shapes) + rider.scratch, compiler_params=_params(("arbitrary",) * len(grid)),
    )(*ins, *rider.srcs)
    own = res[:n_out]
    return (own[0] if single else list(own)), list(res[n_out:])


def _matmul(a, b, mode, out_dtype, name, add=None, relu2_a=False, relu2_grad=None,
            tm_t=None, tn_t=None, tk_t=None, rider=None, m_rows=None, col_shards=None, sq_err_target=None):
    if mode == 'nn':
        (M, K), (K2, N) = a.shape, b.shape
    elif mode == 'nt':
        (M, K), (N, K2) = a.shape, b.shape
    else:
        (K, M), (K2, N) = a.shape, b.shape
    assert K == K2, (name, a.shape, b.shape)
    m_first = 0
    if m_rows is not None:
        assert mode == 'tn'
        m_first, M = m_rows
    if col_shards is not None:
        assert add is None and relu2_grad is None and sq_err_target is None and tn_t is None
        tn_t = N // col_shards
    if mode == 'tn':
        d_tm, d_tn, d_tk = 1024, (2048 if M <= 512 else 1024), 2048
    else:
        wide = add is None and sq_err_target is None and jnp.dtype(out_dtype).itemsize == 2 and K <= D_FF
        d_tm, d_tn, d_tk = (2048 if K <= 1024 else 1024), (1024 if wide else 512), MAX_K_TILE
    tm, tn, tk = _pick(M, tm_t or d_tm), _pick(N, tn_t or d_tn), _pick(K, tk_t or d_tk)
    gm, gn, nk = M // tm, N // tn, K // tk
    if mode == 'nn':
        a_spec = pl.BlockSpec((tm, tk), lambda i, j, k: (i, k))
        b_spec = pl.BlockSpec((tk, tn), lambda i, j, k: (k, j))
        dims = (((1,), (0,)), ((), ()))
    elif mode == 'nt':
        a_spec = pl.BlockSpec((tm, tk), lambda i, j, k: (i, k))
        b_spec = pl.BlockSpec((tn, tk), lambda i, j, k: (j, k))
        dims = (((1,), (1,)), ((), ()))
    else:
        assert m_first % tm == 0
        a_spec = pl.BlockSpec((tk, tm), lambda i, j, k: (k, m_first // tm + i))
        b_spec = pl.BlockSpec((tk, tn), lambda i, j, k: (k, j))
        dims = (((0,), (0,)), ((), ()))
    o_spec = pl.BlockSpec((tm, tn), lambda i, j, k: (i, j))
    has_add, has_e, has_t = add is not None, relu2_grad is not None, sq_err_target is not None
    assert not has_t or (nk == 1 and tn % LANES == 0)

    def body(*refs):
        a_ref, b_ref = refs[0], refs[1]
        pos = 2
        add_ref = e_ref = t_ref = None
        if has_add:
            add_ref = refs[pos]
            pos += 1
        if has_e:
            e_ref = refs[pos]
            pos += 1
        if has_t:
            t_ref = refs[pos]
            pos += 1
        o_ref = refs[pos]
        acc_ref = refs[pos + 1] if nk > 1 else None

        av = a_ref[...]
        if relu2_a:
            av = jnp.maximum(av, 0)
            av = av * av
        prod = lax.dot_general(av.astype(BF16), b_ref[...].astype(BF16), dims, preferred_element_type=F32)

        def finish(r):
            if has_add:
                r = r + add_ref[...]
            if has_e:
                r = r * (2.0 * jnp.maximum(e_ref[...].astype(F32), 0.0))
            if has_t:
                err = r - t_ref[...]
                r = err * (1.0 / N)
                refs[pos + 1][...] = r.astype(BF16)
                sq = err * err
                part = sq[:, 0:LANES]
                for c in range(1, tn // LANES):
                    part = part + sq[:, c * LANES:(c + 1) * LANES]
                _acc_rows(refs[pos + 2], part, (pl.program_id(0) == 0) & (pl.program_id(1) == 0))
            o_ref[...] = r.astype(out_dtype)

        if nk == 1:
            finish(prod)
        else:
            k = pl.program_id(2)

            @pl.when(k == 0)
            def _():
                acc_ref[...] = prod

            @pl.when(k > 0)
            def _():
                acc_ref[...] += prod

            @pl.when(k == nk - 1)
            def _():
                finish(acc_ref[...])

    ins, specs = [a, b], [a_spec, b_spec]
    if has_add:
        ins.append(add)
        specs.append(o_spec)
    if has_e:
        ins.append(relu2_grad)
        specs.append(o_spec)
    out_specs, out_shape, sem = o_spec, jax.ShapeDtypeStruct((M, N), out_dtype), ("parallel", "parallel", "arbitrary")
    if has_t:
        ins.append(sq_err_target)
        specs.append(o_spec)
        out_specs = [o_spec, o_spec, pl.BlockSpec((1, LANES), lambda i, j, k: (0, 0))]
        out_shape = [out_shape, jax.ShapeDtypeStruct((M, N), BF16), jax.ShapeDtypeStruct((1, LANES), F32)]
        sem = ("arbitrary", "arbitrary", "arbitrary")
    if col_shards is not None:
        out_specs = pl.BlockSpec((None, tm, tn), lambda i, j, k: (j, i, 0))
        out_shape = jax.ShapeDtypeStruct((col_shards, M, tn), out_dtype)
    return _call(body, rider, ins, name=name, grid=(gm, gn, nk), in_specs=specs, out_specs=out_specs, out_shape=out_shape,
                 scratch_shapes=[pltpu.VMEM((tm, tn), F32)] if nk > 1 else [], sem=sem)


ROW_BLOCK_BYTES = 12 * 1024 * 1024


def _row_tile(rows, row_bytes):
    return _pick(rows, max(16, min(1024, ROW_BLOCK_BYTES // row_bytes)), 16)


def _rowspec(tr, width, col=0):
    return pl.BlockSpec((tr, width), lambda i, col=col: (i, col))


def _fullspec(shape):
    nd = len(shape)
    return pl.BlockSpec(shape, lambda i, nd=nd: (0,) * nd)


def _rms(x, width):
    x = x.astype(F32)
    return lax.rsqrt(jnp.sum(x * x, axis=-1, keepdims=True) * (1.0 / width) + EPS)


def _rms_bwd_rows(x, g, dy, width):
    x, dy = x.astype(F32), dy.astype(F32)
    r = _rms(x, width)
    xh = x * r
    dn = dy * g
    dx = r * (dn - xh * (jnp.sum(dn * xh, axis=-1, keepdims=True) * (1.0 / width)))
    return dx, dy * xh


def _acc_rows(ref, val, first):
    s = jnp.sum(val, axis=0, keepdims=True)

    @pl.when(first)
    def _():
        ref[...] = s

    @pl.when(jnp.logical_not(first))
    def _():
        ref[...] += s


def _rms_fwd(x, g, name, rider=None):
    rows, width = x.shape
    tr = _row_tile(rows, 6 * width)

    def body(x_ref, g_ref, o_ref):
        xv = x_ref[...]
        o_ref[...] = (xv * _rms(xv, width) * g_ref[...]).astype(BF16)

    return _call(body, rider, [x, g], name=name, grid=(rows // tr,),
                 in_specs=[_rowspec(tr, width), _fullspec((1, width))], out_specs=_rowspec(tr, width),
                 out_shape=jax.ShapeDtypeStruct((rows, width), BF16), scratch_shapes=[], sem=("parallel",))


def _rms_bwd(x, g, dy, res, name, dx_dtypes=(F32,)):
    rows, width = x.shape
    tr = _row_tile(rows, 18 * width)
    has_res = res is not None
    n_in = 4 if has_res else 3

    def body(*refs):
        x_ref, g_ref, dy_ref = refs[:3]
        dx, dgv = _rms_bwd_rows(x_ref[...], g_ref[...], dy_ref[...], width)
        if has_res:
            dx = dx + refs[3][...]
        for ref, dt in zip(refs[n_in:], dx_dtypes):
            ref[...] = dx.astype(dt)
        _acc_rows(refs[-1], dgv, pl.program_id(0) == 0)

    ins = [x, g, dy] + ([res] if has_res else [])
    specs = [_rowspec(tr, width), _fullspec((1, width)), _rowspec(tr, width)] + ([_rowspec(tr, width)] if has_res else [])
    return pl.pallas_call(
        body, name=name, grid=(rows // tr,), in_specs=specs,
        out_specs=[_rowspec(tr, width)] * len(dx_dtypes) + [_fullspec((1, width))],
        out_shape=[jax.ShapeDtypeStruct((rows, width), dt) for dt in dx_dtypes] + [jax.ShapeDtypeStruct((1, width), F32)],
        compiler_params=_params(("arbitrary",)),
    )(*ins)


_GELU_C = math.sqrt(2.0 / math.pi)


def _gelu(x):
    t = jnp.tanh(_GELU_C * (x + 0.044715 * (x * x * x)))
    return 0.5 * x * (1.0 + t), t


def _gelu_grad(x, t):
    return 0.5 * (1.0 + t) + 0.5 * x * (1.0 - t * t) * (_GELU_C * (1.0 + 3.0 * 0.044715 * (x * x)))


def _gm_forward_rows(zu, zv, gln, bln, wc_ref, bst, n_chunk):
    u, tu = _gelu(zu)
    a, ta = _gelu(zv)
    mu = jnp.mean(a, axis=-1, keepdims=True)
    ac = a - mu
    rs = lax.rsqrt(jnp.mean(ac * ac, axis=-1, keepdims=True) + EPS)
    n = ac * rs
    v = n * gln + bln
    vb = v.astype(BF16)
    rows = []
    for c in range(n_chunk):
        cols = []
        for g in range(GM_GROUPS):
            vc = vb[c * GM_CHUNK:(c + 1) * GM_CHUNK, g * LANES:(g + 1) * LANES]
            mixed = jnp.dot(wc_ref[g], vc, preferred_element_type=F32) + bst[g]
            cols.append(mixed)
        rows.append(jnp.concatenate(cols, axis=1))
    mixed = jnp.concatenate(rows, axis=0) if n_chunk > 1 else rows[0]
    return u, tu, ta, n, rs, v, mixed


def _gm_fwd(z, gln, bln, wc, bst, name):
    rows = z.shape[0]
    tr = _pick(rows, 512, GM_CHUNK)
    n_chunk = tr // GM_CHUNK

    def body(zu_ref, zv_ref, gln_ref, bln_ref, wc_ref, bst_ref, o_ref):
        u, _, _, _, _, _, mixed = _gm_forward_rows(zu_ref[...].astype(F32), zv_ref[...].astype(F32), gln_ref[...], bln_ref[...], wc_ref,
                                                   bst_ref, n_chunk)
        o_ref[...] = (u * mixed).astype(BF16)

    return pl.pallas_call(
        body, name=name, grid=(rows // tr,),
        in_specs=[_rowspec(tr, GM_WIDTH, Z_GM // GM_WIDTH), _rowspec(tr, GM_WIDTH, Z_GM // GM_WIDTH + 1),_fullspec((1, GM_WIDTH)), _fullspec((1, GM_WIDTH)),
                  _fullspec((GM_GROUPS, GM_CHUNK, GM_CHUNK)), _fullspec((GM_GROUPS, GM_CHUNK, LANES))],
        out_specs=_rowspec(tr, GM_WIDTH), out_shape=jax.ShapeDtypeStruct((rows, GM_WIDTH), BF16),
        compiler_params=_params(("parallel",)),
    )(z, z, gln, bln, wc, bst)


ANY_SPEC = pl.BlockSpec(memory_space=pl.ANY)


def _gm_bwd(z, dy, gln, bln, wc, wct, bst, dz, name):
    rows = z.shape[0]
    tr = _pick(rows, 512, GM_CHUNK)
    n_chunk = tr // GM_CHUNK

    def body(zu_ref, zv_ref, dy_ref, gln_ref, bln_ref, wc_ref, wct_ref, bst_ref, _, dz_ref, dws_ref, dbs_ref, dgl_ref,
             dbl_ref):
        first = pl.program_id(0) == 0
        zu, zv, gln = zu_ref[...].astype(F32), zv_ref[...].astype(F32), gln_ref[...]
        u, tu, ta, n, rs, v, mixed = _gm_forward_rows(zu, zv, gln, bln_ref[...], wc_ref, bst_ref, n_chunk)
        dyv = dy_ref[...].astype(F32)
        dzu = dyv * mixed * _gelu_grad(zu, tu)
        dmix = dyv * u
        dmb = dmix.astype(BF16)
        vb = v.astype(BF16)
        dv_rows, dws, dbs = [], [None] * GM_GROUPS, None
        for c in range(n_chunk):
            rsl = slice(c * GM_CHUNK, (c + 1) * GM_CHUNK)
            cols = []
            for g in range(GM_GROUPS):
                csl = slice(g * LANES, (g + 1) * LANES)
                dmc = dmb[rsl, csl]
                cols.append(jnp.dot(wct_ref[g], dmc, preferred_element_type=F32))
                w_part = lax.dot_general(dmc, vb[rsl, csl], (((1,), (1,)), ((), ())), preferred_element_type=F32)
                dws[g] = w_part if dws[g] is None else dws[g] + w_part
            dv_rows.append(jnp.concatenate(cols, axis=1))
            dbs = dmix[rsl, :] if dbs is None else dbs + dmix[rsl, :]
        dv = jnp.concatenate(dv_rows, axis=0) if n_chunk > 1 else dv_rows[0]
        dn = dv * gln
        da = rs * (dn - jnp.mean(dn, axis=-1, keepdims=True) - n * jnp.mean(dn * n, axis=-1, keepdims=True))
        dzv = da * _gelu_grad(zv, ta)
        dz_ref[:, 0:GM_WIDTH] = dzu.astype(BF16)
        dz_ref[:, GM_WIDTH:2 * GM_WIDTH] = dzv.astype(BF16)
        _acc_rows(dgl_ref, dv * n, first)
        _acc_rows(dbl_ref, dv, first)

        @pl.when(first)
        def _():
            for g in range(GM_GROUPS):
                dws_ref[g] = dws[g]
            dbs_ref[...] = dbs

        @pl.when(jnp.logical_not(first))
        def _():
            for g in range(GM_GROUPS):
                dws_ref[g] += dws[g]
            dbs_ref[...] += dbs

    wspec = _fullspec((GM_GROUPS, GM_CHUNK, GM_CHUNK))
    return pl.pallas_call(
        body, name=name, grid=(rows // tr,),
        in_specs=[_rowspec(tr, GM_WIDTH, Z_GM // GM_WIDTH), _rowspec(tr, GM_WIDTH, Z_GM // GM_WIDTH + 1),
                  _rowspec(tr, GM_WIDTH), _fullspec((1, GM_WIDTH)), _fullspec((1, GM_WIDTH)), wspec, wspec, wspec, ANY_SPEC],
        out_specs=[_rowspec(tr, 2 * GM_WIDTH, Z_GM // (2 * GM_WIDTH)), wspec, _fullspec((GM_CHUNK, GM_WIDTH)),
                   _fullspec((1, GM_WIDTH)), _fullspec((1, GM_WIDTH))],
        out_shape=[jax.ShapeDtypeStruct(dz.shape, dz.dtype), jax.ShapeDtypeStruct((GM_GROUPS, GM_CHUNK, GM_CHUNK), F32),
                   jax.ShapeDtypeStruct((GM_CHUNK, GM_WIDTH), F32), jax.ShapeDtypeStruct((1, GM_WIDTH), F32),
                   jax.ShapeDtypeStruct((1, GM_WIDTH), F32)],
        input_output_aliases={8: 0}, compiler_params=_params(("arbitrary",)),
    )(z, z, dy, gln, bln, wc, wct, bst, dz)


def _lat_fwd(z, g_cq, g_ckv, name):
    rows = z.shape[0]
    tr = _row_tile(rows, 4 * MLA_W)

    def body(z_ref, gq_ref, gkv_ref, nq_ref, nkv_ref):
        zb = z_ref[...]
        cq, ckv = zb[:, 0:Q_LORA], zb[:, Q_LORA:Q_LORA + KV_LORA]
        nq_ref[...] = (cq * _rms(cq, Q_LORA) * gq_ref[...]).astype(BF16)
        nkv_ref[...] = (ckv * _rms(ckv, KV_LORA) * gkv_ref[...]).astype(BF16)

    return pl.pallas_call(
        body, name=name, grid=(rows // tr,),
        in_specs=[_rowspec(tr, MLA_W, Z_MLA // MLA_W), _fullspec((1, Q_LORA)), _fullspec((1, KV_LORA))],
        out_specs=[_rowspec(tr, Q_LORA), _rowspec(tr, KV_LORA)],
        out_shape=[jax.ShapeDtypeStruct((rows, Q_LORA), BF16), jax.ShapeDtypeStruct((rows, KV_LORA), BF16)],
        compiler_params=_params(("parallel",)),
    )(z, g_cq, g_ckv)


def _lat_bwd(z, dnq, dnkv, dkpe, g_cq, g_ckv, dz, name):
    rows = z.shape[0]
    tr = _row_tile(rows, 8 * MLA_W)

    def body(z_ref, dnq_ref, dnkv_ref, dkpe_ref, gq_ref, gkv_ref, _, dz_ref, dgq_ref, dgkv_ref):
        first = pl.program_id(0) == 0
        zb = z_ref[...]
        dcq, dgq = _rms_bwd_rows(zb[:, 0:Q_LORA], gq_ref[...], dnq_ref[...], Q_LORA)
        dckv, dgkv = _rms_bwd_rows(zb[:, Q_LORA:Q_LORA + KV_LORA], gkv_ref[...], dnkv_ref[...], KV_LORA)
        dz_ref[:, 0:Q_LORA] = dcq.astype(BF16)
        dz_ref[:, Q_LORA:Q_LORA + KV_LORA] = dckv.astype(BF16)
        dz_ref[:, Q_LORA + KV_LORA:MLA_W] = dkpe_ref[...].astype(BF16)
        _acc_rows(dgq_ref, dgq, first)
        _acc_rows(dgkv_ref, dgkv, first)

    return pl.pallas_call(
        body, name=name, grid=(rows // tr,),
        in_specs=[_rowspec(tr, MLA_W, Z_MLA // MLA_W), _rowspec(tr, Q_LORA), _rowspec(tr, KV_LORA), _rowspec(tr, LANES),
                  _fullspec((1, Q_LORA)), _fullspec((1, KV_LORA)), ANY_SPEC],
        out_specs=[_rowspec(tr, MLA_W, Z_MLA // MLA_W), _fullspec((1, Q_LORA)), _fullspec((1, KV_LORA))],
        out_shape=[jax.ShapeDtypeStruct(dz.shape, dz.dtype), jax.ShapeDtypeStruct((1, Q_LORA), F32),
                   jax.ShapeDtypeStruct((1, KV_LORA), F32)],
        input_output_aliases={6: 0}, compiler_params=_params(("arbitrary",)),
    )(z, dnq, dnkv, dkpe, g_cq, g_ckv, dz)


def _rope(y, cc, ss):
    return y * cc + pltpu.roll(y, 64, 1) * ss


def _rope_bwd(d, cc, ss):
    return d * cc + pltpu.roll(d * ss, 64, 1)


def _qk_fwd(q, kv, z, cc, ss, gqn, gqp, gkn, gkp, name):
    rows = q.shape[0]
    W = MLA_HEADS * HEAD
    tr = _row_tile(rows, 20 * W)
    QS = MLA_SCALE * LOG2E

    def body(q_ref, kv_ref, kpe_ref, cc_ref, ss_ref, gqn_ref, gqp_ref, gkn_ref, gkp_ref, qc_ref, kc_ref, v_ref):
        cc, ss = cc_ref[...], ss_ref[...]
        kpe = kpe_ref[...]
        kp = _rope(kpe * _rms(kpe, MLA_ROPE) * gkp_ref[...], cc, ss).astype(BF16)
        for h in range(MLA_HEADS):
            qn = q_ref[:, h * HEAD:(h + 1) * HEAD]
            qp = q_ref[:, W + h * HEAD:W + (h + 1) * HEAD]
            kn = kv_ref[:, h * HEAD:(h + 1) * HEAD]
            qc_ref[:, h * QCAT:h * QCAT + HEAD] = (qn * _rms(qn, HEAD) * gqn_ref[...] * QS).astype(BF16)
            qc_ref[:, h * QCAT + HEAD:(h + 1) * QCAT] = (_rope(qp * _rms(qp, MLA_ROPE) * gqp_ref[...], cc, ss) * QS).astype(BF16)
            kc_ref[:, h * QCAT:h * QCAT + HEAD] = (kn * _rms(kn, HEAD) * gkn_ref[...]).astype(BF16)
            kc_ref[:, h * QCAT + HEAD:(h + 1) * QCAT] = kp
        v_ref[...] = kv_ref[:, W:2 * W].astype(BF16)

    g = _fullspec((1, HEAD))
    return _call(
        body, None, [q, kv, z, cc, ss, gqn, gqp, gkn, gkp], name=name, grid=(rows // tr,),
        in_specs=[_rowspec(tr, 2 * W), _rowspec(tr, 2 * W), _rowspec(tr, LANES, Z_KPE // LANES), _rowspec(tr, LANES),
                  _rowspec(tr, LANES), g, g, g, g],
        out_specs=[_rowspec(tr, MLA_HEADS * QCAT), _rowspec(tr, MLA_HEADS * QCAT), _rowspec(tr, W)],
        out_shape=[jax.ShapeDtypeStruct((rows, MLA_HEADS * QCAT), BF16), jax.ShapeDtypeStruct((rows, MLA_HEADS * QCAT), BF16),
                   jax.ShapeDtypeStruct((rows, W), BF16)],
        scratch_shapes=[], sem=("parallel",))


def _qk_bwd(q, kv, z, cc, ss, gqn, gqp, gkn, gkp, dqc, dkc, dv, name):
    rows = q.shape[0]
    W = MLA_HEADS * HEAD
    tr = _row_tile(rows, 40 * W)

    def body(q_ref, kv_ref, kpe_ref, cc_ref, ss_ref, gqn_ref, gqp_ref, gkn_ref, gkp_ref, dqc_ref, dkc_ref, dv_ref,
             dq_ref, dkv_ref, dkpe_ref, dgqn_ref, dgqp_ref, dgkn_ref, dgkp_ref):
        first = pl.program_id(0) == 0
        cc, ss = cc_ref[...], ss_ref[...]
        sqn = sqp = skn = dkp = None
        for h in range(MLA_HEADS):
            dx, dg = _rms_bwd_rows(q_ref[:, h * HEAD:(h + 1) * HEAD], gqn_ref[...], dqc_ref[:, h * QCAT:h * QCAT + HEAD], HEAD)
            dq_ref[:, h * HEAD:(h + 1) * HEAD] = dx.astype(BF16)
            sqn = dg if sqn is None else sqn + dg
            dy = _rope_bwd(dqc_ref[:, h * QCAT + HEAD:(h + 1) * QCAT], cc, ss)
            dx, dg = _rms_bwd_rows(q_ref[:, W + h * HEAD:W + (h + 1) * HEAD], gqp_ref[...], dy, MLA_ROPE)
            dq_ref[:, W + h * HEAD:W + (h + 1) * HEAD] = dx.astype(BF16)
            sqp = dg if sqp is None else sqp + dg
            dx, dg = _rms_bwd_rows(kv_ref[:, h * HEAD:(h + 1) * HEAD], gkn_ref[...], dkc_ref[:, h * QCAT:h * QCAT + HEAD], HEAD)
            dkv_ref[:, h * HEAD:(h + 1) * HEAD] = dx.astype(BF16)
            skn = dg if skn is None else skn + dg
            part = dkc_ref[:, h * QCAT + HEAD:(h + 1) * QCAT].astype(F32)
            dkp = part if dkp is None else dkp + part
        dkv_ref[:, W:2 * W] = dv_ref[...].astype(BF16)
        dx, dg = _rms_bwd_rows(kpe_ref[...], gkp_ref[...], _rope_bwd(dkp, cc, ss), MLA_ROPE)
        dkpe_ref[...] = dx
        _acc_rows(dgqn_ref, sqn, first)
        _acc_rows(dgqp_ref, sqp, first)
        _acc_rows(dgkn_ref, skn, first)
        _acc_rows(dgkp_ref, dg, first)

    g = _fullspec((1, HEAD))
    gs = jax.ShapeDtypeStruct((1, HEAD), F32)
    return pl.pallas_call(
        body, name=name, grid=(rows // tr,),
        in_specs=[_rowspec(tr, 2 * W), _rowspec(tr, 2 * W), _rowspec(tr, LANES, Z_KPE // LANES), _rowspec(tr, LANES),
                  _rowspec(tr, LANES), g, g, g, g, _rowspec(tr, MLA_HEADS * QCAT), _rowspec(tr, MLA_HEADS * QCAT),
                  _rowspec(tr, W)],
        out_specs=[_rowspec(tr, 2 * W), _rowspec(tr, 2 * W), _rowspec(tr, LANES), g, g, g, g],
        out_shape=[jax.ShapeDtypeStruct((rows, 2 * W), BF16), jax.ShapeDtypeStruct((rows, 2 * W), BF16),
                   jax.ShapeDtypeStruct((rows, LANES), F32), gs, gs, gs, gs],
        compiler_params=_params(("arbitrary",)),
    )(q, kv, z, cc, ss, gqn, gqp, gkn, gkp, dqc, dkc, dv)


def _headnorm_fwd(x, col, nheads, g, out_scale, name):
    rows = x.shape[0]
    W = nheads * HEAD
    tr = _row_tile(rows, 6 * W)

    def body(x_ref, g_ref, o_ref):
        for h in range(nheads):
            xv = x_ref[:, h * HEAD:(h + 1) * HEAD]
            o_ref[:, h * HEAD:(h + 1) * HEAD] = (xv * _rms(xv, HEAD) * g_ref[...] * out_scale).astype(BF16)

    return pl.pallas_call(
        body, name=name, grid=(rows // tr,),
        in_specs=[_rowspec(tr, W, col), _fullspec((1, HEAD))], out_specs=_rowspec(tr, W),
        out_shape=jax.ShapeDtypeStruct((rows, W), BF16), compiler_params=_params(("parallel",)),
    )(x, g)


def _headnorm_bwd(x, col, nheads, g, dy, tail, name, into=None):
    rows = x.shape[0]
    W = nheads * HEAD
    tr = _row_tile(rows, 12 * W)
    has_tail = tail is not None
    WO = 2 * W if has_tail else W

    def body(*refs):
        if into is not None:
            x_ref, g_ref, dy_ref, _, dx_ref, dg_ref = refs
        elif has_tail:
            x_ref, g_ref, dy_ref, t_ref, dx_ref, dg_ref = refs
        else:
            x_ref, g_ref, dy_ref, dx_ref, dg_ref = refs
        acc = None
        for h in range(nheads):
            sl = slice(h * HEAD, (h + 1) * HEAD)
            dx, dg = _rms_bwd_rows(x_ref[:, sl], g_ref[...], dy_ref[:, sl], HEAD)
            dx_ref[:, sl] = dx.astype(BF16)
            acc = dg if acc is None else acc + dg
        if has_tail:
            dx_ref[:, W:2 * W] = t_ref[...].astype(BF16)
        _acc_rows(dg_ref, acc, pl.program_id(0) == 0)

    ins = [x, g, dy] + ([tail] if has_tail else [])
    specs = [_rowspec(tr, W, col), _fullspec((1, HEAD)), _rowspec(tr, W)] + ([_rowspec(tr, W)] if has_tail else [])
    dx_spec, dx_shape, aliases = _rowspec(tr, WO), jax.ShapeDtypeStruct((rows, WO), BF16), {}
    if into is not None:
        assert not has_tail
        ins, specs = ins + [into[0]], specs + [ANY_SPEC]
        dx_spec, dx_shape, aliases = _rowspec(tr, W, into[1]), jax.ShapeDtypeStruct(into[0].shape, into[0].dtype), {3: 0}
    return pl.pallas_call(
        body, name=name, grid=(rows // tr,), in_specs=specs,
        out_specs=[dx_spec, _fullspec((1, HEAD))], out_shape=[dx_shape, jax.ShapeDtypeStruct((1, HEAD), F32)],
        input_output_aliases=aliases, compiler_params=_params(("arbitrary",)),
    )(*ins)


def _sigmoid(x):
    return 1.0 / (1.0 + jnp.exp(-x.astype(F32)))


def _merge_fwd(z, y_gm, y_mla, y_mem, name):
    rows = z.shape[0]
    tr = _row_tile(rows, 14 * D_MODEL)

    def body(g0_ref, g1_ref, g2_ref, a_ref, b_ref, c_ref, o_ref):
        m = _sigmoid(g0_ref[...]) * a_ref[...] + _sigmoid(g1_ref[...]) * b_ref[...] + _sigmoid(g2_ref[...]) * c_ref[...]
        o_ref[...] = m.astype(BF16)

    r = _rowspec(tr, D_MODEL)
    return pl.pallas_call(
        body, name=name, grid=(rows // tr,),
        in_specs=[_rowspec(tr, D_MODEL, 0), _rowspec(tr, D_MODEL, 1), _rowspec(tr, D_MODEL, 2),r, r, r],
        out_specs=r, out_shape=jax.ShapeDtypeStruct((rows, D_MODEL), BF16), compiler_params=_params(("parallel",)),
    )(z, z, z, y_gm, y_mla, y_mem)


def _merge_bwd(z, y_gm, y_mla, y_mem, dm, name):
    rows = z.shape[0]
    tr = _row_tile(rows, 24 * D_MODEL)

    def body(g0_ref, g1_ref, g2_ref, a_ref, b_ref, c_ref, dm_ref, da_ref, db_ref, dc_ref, dzg_ref):
        dmv = dm_ref[...].astype(F32)
        for k, (g_ref, y_ref, dy_ref) in enumerate(((g0_ref, a_ref, da_ref), (g1_ref, b_ref, db_ref), (g2_ref, c_ref, dc_ref))):
            s = _sigmoid(g_ref[...])
            dy_ref[...] = (dmv * s).astype(BF16)
            dzg_ref[:, k * D_MODEL:(k + 1) * D_MODEL] = (dmv * y_ref[...] * s * (1.0 - s)).astype(BF16)

    r = _rowspec(tr, D_MODEL)
    o = jax.ShapeDtypeStruct((rows, D_MODEL), BF16)
    return pl.pallas_call(
        body, name=name, grid=(rows // tr,),
        in_specs=[_rowspec(tr, D_MODEL, 0), _rowspec(tr, D_MODEL, 1), _rowspec(tr, D_MODEL, 2),r, r, r, r],
        out_specs=[r, r, r, _rowspec(tr, 3 * D_MODEL, 0)],
        out_shape=[o, o, o, jax.ShapeDtypeStruct((rows, Z_COLS), BF16)],
        compiler_params=_params(("parallel",)),
    )(z, z, z, y_gm, y_mla, y_mem, dm)


_NT = (((1,), (1,)), ((), ()))
_TN = (((0,), (0,)), ((), ()))


def _diag_mask(s):
    row = lax.broadcasted_iota(jnp.int32, s.shape, 0)
    col = lax.broadcasted_iota(jnp.int32, s.shape, 1)
    return jnp.where(row >= col, s, NEG)


def _attn_fwd(q, k, v, nb, nheads, dk, v_col0, causal, name, rider=None):
    S, Skv = q.shape[0] // nb, k.shape[0] // nb
    tq = _pick(Skv, ATT_TILE) if causal else _pick(S, 4 * ATT_TILE)
    nq = S // tq

    def body(q_ref, k_ref, v_ref, o_ref, lse_ref):
        for i in range(nq):
            r0 = i * tq
            qb = q_ref[r0:r0 + tq, :]
            if causal:
                spans = ([(0, r0, False)] if i > 0 else []) + [(r0, r0 + tq, True)]
            else:
                spans = [(0, Skv, False)]
            scores = []
            for a, b, masked in spans:
                s = lax.dot_general(qb, k_ref[a:b, :], _NT, preferred_element_type=F32)
                scores.append(_diag_mask(s) if masked else s)
            m = functools.reduce(jnp.maximum, [jnp.max(s, axis=-1, keepdims=True) for s in scores])
            l = acc = None
            for s, (a, b, _) in zip(scores, spans):
                p = jnp.exp2(s - m)
                lp = jnp.sum(p, axis=-1, keepdims=True)
                ap = jnp.dot(p.astype(BF16), v_ref[a:b, :].astype(BF16), preferred_element_type=F32)
                l, acc = (lp, ap) if l is None else (l + lp, acc + ap)
            o_ref[r0:r0 + tq, :] = (acc / l).astype(BF16)
            lse_ref[r0:r0 + tq, :] = m + jnp.log2(l)

    ins = [q, k, v]
    in_specs = [pl.BlockSpec((S, dk), lambda b, h: (b, h)), pl.BlockSpec((Skv, dk), lambda b, h: (b, h)),
                pl.BlockSpec((Skv, HEAD), lambda b, h: (b, v_col0 + h))]
    out_specs = [pl.BlockSpec((S, HEAD), lambda b, h: (b, h)), pl.BlockSpec((None, S, 1), lambda b, h: (h, b, 0))]
    out_shape = [jax.ShapeDtypeStruct((nb * S, nheads * HEAD), BF16), jax.ShapeDtypeStruct((nheads, nb * S, 1), F32)]
    return _call(body, rider, ins, name=name, grid=(nb, nheads), in_specs=in_specs, out_specs=out_specs,
                 out_shape=out_shape, scratch_shapes=[], sem=("parallel", "parallel"))


def _attn_bwd(q, k, v, o, do, lse, nb, nheads, dk, v_col0, scale, causal, name, rider=None):
    S, Skv = q.shape[0] // nb, k.shape[0] // nb
    tk = _pick(Skv, ATT_TILE)
    nkv = Skv // tk

    def body(q_ref, k_ref, v_ref, o_ref, do_ref, lse_ref, dq_ref, dk_ref, dv_ref, delta_ref, dob_ref, dqa_ref):
        dov = do_ref[...]
        delta_ref[...] = jnp.sum(o_ref[...].astype(F32) * dov.astype(F32), axis=-1, keepdims=True)
        dob_ref[...] = dov.astype(BF16)

        for j in range(nkv):
            c0 = j * tk
            kb = k_ref[c0:c0 + tk, :]
            vb = v_ref[c0:c0 + tk, :].astype(BF16)
            if causal:
                spans = [(c0, c0 + tk, True)] + ([(c0 + tk, S, False)] if c0 + tk < S else [])
            else:
                spans = [(0, S, False)]
            dk_acc = dv_acc = None
            for a, b, masked in spans:
                qb = q_ref[a:b, :]
                dob = dob_ref[a:b, :]
                s = lax.dot_general(qb, kb, _NT, preferred_element_type=F32)
                if masked:
                    s = _diag_mask(s)
                p = jnp.exp2(s - lse_ref[a:b, :])
                dp = lax.dot_general(dob, vb, _NT, preferred_element_type=F32)
                ds = (p * (dp - delta_ref[a:b, :])).astype(BF16)
                dv_p = lax.dot_general(p.astype(BF16), dob, _TN, preferred_element_type=F32)
                dk_p = lax.dot_general(ds, qb, _TN, preferred_element_type=F32)
                dk_acc, dv_acc = (dk_p, dv_p) if dk_acc is None else (dk_acc + dk_p, dv_acc + dv_p)
                dq_p = jnp.dot(ds, kb, preferred_element_type=F32) * scale
                if j == 0:
                    dqa_ref[a:b, :] = dq_p
                else:
                    dqa_ref[a:b, :] += dq_p
            dk_ref[c0:c0 + tk, :] = (dk_acc * LN2).astype(BF16)
            dv_ref[c0:c0 + tk, :] = dv_acc.astype(BF16)
        dq_ref[...] = dqa_ref[...].astype(BF16)

    ins = [q, k, v, o, do, lse]
    in_specs = [pl.BlockSpec((S, dk), lambda b, h: (b, h)), pl.BlockSpec((Skv, dk), lambda b, h: (b, h)),
                pl.BlockSpec((Skv, HEAD), lambda b, h: (b, v_col0 + h)), pl.BlockSpec((S, HEAD), lambda b, h: (b, h)),
                pl.BlockSpec((S, HEAD), lambda b, h: (b, h)), pl.BlockSpec((None, S, 1), lambda b, h: (h, b, 0))]
    out_specs = [pl.BlockSpec((S, dk), lambda b, h: (b, h)), pl.BlockSpec((Skv, dk), lambda b, h: (b, h)),
                 pl.BlockSpec((Skv, HEAD), lambda b, h: (b, h))]
    out_shape = [jax.ShapeDtypeStruct((nb * S, nheads * dk), BF16), jax.ShapeDtypeStruct((nb * Skv, nheads * dk), BF16),
                 jax.ShapeDtypeStruct((nb * Skv, nheads * HEAD), BF16)]
    return _call(body, rider, ins, name=name, grid=(nb, nheads), in_specs=in_specs, out_specs=out_specs,
                 out_shape=out_shape,
                 scratch_shapes=[pltpu.VMEM((S, 1), F32), pltpu.VMEM((S, HEAD), BF16), pltpu.VMEM((S, dk), F32)],
                 sem=("parallel", "parallel"))


def _spread_rope(a):
    zero = jnp.zeros(a.shape[:-1] + (32,), a.dtype)
    return jnp.concatenate([a[..., :32], zero, a[..., 32:], zero], axis=-1)


def _gather_rope(a):
    return jnp.concatenate([a[..., 0:32], a[..., 64:96]], axis=-1)


def _win_layout(w):
    return jnp.concatenate([w[:, C_ZG:C_END], w[:, C_ZU:C_CQ], w[:, C_QM:C_ZG], w[:, C_CQ:C_CKV], w[:, C_CKV:C_KPE],
                            _spread_rope(w[:, C_KPE:C_QM])], axis=1)


def _win_unlayout(d):
    return jnp.concatenate([d[:, Z_GM:Z_QM], d[:, Z_MLA:Z_MLA + Q_LORA], d[:, Z_MLA + Q_LORA:Z_KPE],
                            _gather_rope(d[:, Z_KPE:Z_COLS]), d[:, Z_QM:Z_MLA], d[:, 0:Z_GM]], axis=1)


def _wuq_layout(w):
    r = w.reshape(Q_LORA, MLA_HEADS, HEAD + MLA_ROPE)
    return jnp.concatenate([r[:, :, :HEAD].reshape(Q_LORA, -1), _spread_rope(r[:, :, HEAD:]).reshape(Q_LORA, -1)], axis=1)


def _wuq_unlayout(d):
    n = d[:, :MLA_HEADS * HEAD].reshape(Q_LORA, MLA_HEADS, HEAD)
    p = _gather_rope(d[:, MLA_HEADS * HEAD:].reshape(Q_LORA, MLA_HEADS, HEAD))
    return jnp.concatenate([n, p], axis=-1).reshape(Q_LORA, -1)


def _wukv_layout(w):
    r = w.reshape(KV_LORA, MLA_HEADS, 2 * HEAD)
    return jnp.concatenate([r[:, :, :HEAD].reshape(KV_LORA, -1), r[:, :, HEAD:].reshape(KV_LORA, -1)], axis=1)


def _wukv_unlayout(d):
    k = d[:, :MLA_HEADS * HEAD].reshape(KV_LORA, MLA_HEADS, HEAD)
    v = d[:, MLA_HEADS * HEAD:].reshape(KV_LORA, MLA_HEADS, HEAD)
    return jnp.concatenate([k, v], axis=-1).reshape(KV_LORA, -1)


AG_MID = ['w_uq', 'w_ukv', 'w_mem_kv', 'w_o_gm', 'w_o_mla', 'w_o_mem', 'w_out']
AG_FFN = ['w_ff1', 'w_ff2']
RS_GROUPS = {'ffn_proj': ['w_ff2', 'w_ff1', 'w_out', 'w_o_gm', 'w_o_mla', 'w_o_mem'],
             'lat': ['w_uq', 'w_ukv', 'w_mem_kv'], 'in_top': ['w_in'], 'in_bot': ['w_in']}


def _unride(res, rider):
    return (res, None) if rider is None else res


def _local_step(x, mem, positions, target, P, ws):
    B, S, _ = x.shape
    M = mem.shape[1]
    T = B * S
    x2d = x.reshape(T, D_MODEL)
    mem2d = mem.reshape(B * M, D_MODEL)
    tgt2d = target.reshape(T, D_MODEL)

    def row(v):
        return v.reshape(1, -1).astype(F32)

    inv_freq = ROPE_BASE ** (-jnp.arange(0, MLA_ROPE, 2, dtype=F32) / MLA_ROPE)
    zero = jnp.zeros_like(inv_freq)
    ang = positions.reshape(T).astype(F32)[:, None] * jnp.concatenate([inv_freq, zero, inv_freq, zero])
    cc = jnp.cos(ang) * jnp.concatenate([zero + 1.0, zero, zero + 1.0, zero])
    ss = jnp.sin(ang) * jnp.concatenate([zero - 1.0, zero, zero + 1.0, zero])

    g_mix, g_cq, g_ckv, g_ffn, g_mem = row(P['g_mix']), row(P['g_cq']), row(P['g_ckv']), row(P['g_ffn']), row(P['g_mem'])
    gqn, gkn, gmq, gmk = row(P['g_q_nope']), row(P['g_k_nope']), row(P['g_mq']), row(P['g_mk'])
    gqp, gkp = _spread_rope(row(P['g_q_pe'])), _spread_rope(row(P['g_k_pe']))
    gln, bln = row(P['g_gm_ln']), row(P['b_gm_ln'])
    wc = jnp.tril(P['w_spatial'].astype(F32))
    wct = jnp.swapaxes(wc, 1, 2).astype(BF16)
    wc = wc.astype(BF16)
    bst = jnp.broadcast_to(P['b_spatial'].astype(F32)[:, :, None], (GM_GROUPS, GM_CHUNK, LANES))

    ride = ws.gather(['w_in'])
    h, got = _unride(_rms_fwd(x2d, g_mix, "rms_mix", rider=ride), ride)
    w_in = _win_layout(ws.gathered(['w_in'], got)['w_in']).astype(BF16)
    ride = ws.gather(AG_MID)
    z, got = _unride(_matmul(h, w_in, 'nn', ACT, "mm_in", tn_t=1792, rider=ride), ride)
    mid = ws.gathered(AG_MID, got)
    w_uq, w_ukv = _wuq_layout(mid['w_uq']).astype(BF16), _wukv_layout(mid['w_ukv']).astype(BF16)
    w_mem_kv, w_o_gm, w_o_mla, w_o_mem, w_out = (mid[n] for n in ('w_mem_kv', 'w_o_gm', 'w_o_mla', 'w_o_mem', 'w_out'))
    ygm_pre = _gm_fwd(z, gln, bln, wc, bst, "gm_fwd")
    y_gm = _matmul(ygm_pre, w_o_gm, 'nn', ACT, "mm_o_gm")
    nq, nkv = _lat_fwd(z, g_cq, g_ckv, "lat_fwd")
    q = _matmul(nq, w_uq, 'nn', ACT, "mm_uq")
    kv = _matmul(nkv, w_ukv, 'nn', ACT, "mm_ukv")
    qcat, kcat, vv = _qk_fwd(q, kv, z, cc, ss, gqn, gqp, gkn, gkp, "qk_fwd")
    ride = ws.gather(AG_FFN)
    (o, lse), got = _unride(_attn_fwd(qcat, kcat, vv, B, MLA_HEADS, QCAT, 0, True, "mla_attn_fwd", rider=ride), ride)
    ffn = ws.gathered(AG_FFN, got)
    w_ff1, w_ff2 = ffn['w_ff1'], ffn['w_ff2']
    y_mla = _matmul(o, w_o_mla, 'nn', ACT, "mm_o_mla")
    nm = _rms_fwd(mem2d, g_mem, "rms_mem")
    kvm = _matmul(nm, w_mem_kv, 'nn', ACT, "mm_mem_kv")
    qm = _headnorm_fwd(z, Z_QM // (MEM_HEADS * HEAD), MEM_HEADS, gmq, MEM_SCALE * LOG2E, "memq_fwd")
    km = _headnorm_fwd(kvm, 0, MEM_HEADS, gmk, 1.0, "memk_fwd")
    om, lse_m = _attn_fwd(qm, km, kvm, B, MEM_HEADS, HEAD, MEM_HEADS, False, "mem_attn_fwd")
    y_mem = _matmul(om, w_o_mem, 'nn', ACT, "mm_o_mem")
    merged = _merge_fwd(z, y_gm, y_mla, y_mem, "merge_fwd")
    x1 = _matmul(merged, w_out, 'nn', F32, "mm_out", add=x2d)
    h2 = _rms_fwd(x1, g_ffn, "rms_ffn")
    a1 = _matmul(h2, w_ff1, 'nn', BF16, "mm_ff1")
    dx2, dx2b, loss_part = _matmul(a1, w_ff2, 'nn', F32, "mm_ff2", add=x1, relu2_a=True, sq_err_target=tgt2d)

    G = {}
    d_ff2 = _matmul(a1, dx2b, 'tn', BF16, "mm_d_ff2", relu2_a=True)
    da1 = _matmul(dx2b, w_ff2, 'nt', BF16, "mm_da1", relu2_grad=a1)
    d_ff1 = _matmul(h2, da1, 'tn', BF16, "mm_d_ff1", col_shards=N_DEV)
    dh2 = _matmul(da1, w_ff1, 'nt', ACT, "mm_dh2")
    dx1, dx1b, G['g_ffn'] = _rms_bwd(x1, g_ffn, dh2, dx2, "rms_ffn_bwd", dx_dtypes=(F32, BF16))
    d_out = _matmul(merged, dx1b, 'tn', BF16, "mm_d_out")
    dmerged = _matmul(dx1b, w_out, 'nt', ACT, "mm_dmerged")
    dy_gm, dy_mla, dy_mem, dz = _merge_bwd(z, y_gm, y_mla, y_mem, dmerged, "merge_bwd")
    d_o_gm = _matmul(ygm_pre, dy_gm, 'tn', BF16, "mm_d_o_gm")
    d_o_mla = _matmul(o, dy_mla, 'tn', BF16, "mm_d_o_mla")
    d_o_mem = _matmul(om, dy_mem, 'tn', BF16, "mm_d_o_mem")
    dygm_pre = _matmul(dy_gm, w_o_gm, 'nt', ACT, "mm_dygm")
    dz, dws, dbs, G['g_gm_ln'], G['b_gm_ln'] = _gm_bwd(z, dygm_pre, gln, bln, wc, wct, bst, dz, "gm_bwd")
    G['w_spatial'] = jnp.tril(dws)
    G['b_spatial'] = jnp.sum(dbs.reshape(GM_CHUNK, GM_GROUPS, LANES), axis=-1).T
    do = _matmul(dy_mla, w_o_mla, 'nt', ACT, "mm_do")
    ride = ws.scatter('ffn_proj', {'w_ff2': d_ff2, 'w_ff1': d_ff1, 'w_out': d_out, 'w_o_gm': d_o_gm, 'w_o_mla': d_o_mla,
                                   'w_o_mem': d_o_mem})
    (dqc, dkc, dvv), got = _unride(_attn_bwd(qcat, kcat, vv, o, do, lse, B, MLA_HEADS, QCAT, 0, MLA_SCALE, True,
                                             "mla_attn_bwd", rider=ride), ride)
    ws.scattered('ffn_proj', got)
    dq, dkv, dkpe, G['g_q_nope'], dgqp, G['g_k_nope'], dgkp = _qk_bwd(q, kv, z, cc, ss, gqn, gqp, gkn, gkp, dqc, dkc, dvv,
                                                                     "qk_bwd")
    G['g_q_pe'], G['g_k_pe'] = _gather_rope(dgqp), _gather_rope(dgkp)
    d_uq = _wuq_unlayout(_matmul(nq, dq, 'tn', BF16, "mm_d_uq"))
    dnq = _matmul(dq, w_uq, 'nt', ACT, "mm_dnq")
    d_ukv = _wukv_unlayout(_matmul(nkv, dkv, 'tn', BF16, "mm_d_ukv"))
    dnkv = _matmul(dkv, w_ukv, 'nt', ACT, "mm_dnkv")
    dz, G['g_cq'], G['g_ckv'] = _lat_bwd(z, dnq, dnkv, dkpe, g_cq, g_ckv, dz, "lat_bwd")
    dom = _matmul(dy_mem, w_o_mem, 'nt', ACT, "mm_dom")
    dqm, dkm, dvm = _attn_bwd(qm, km, kvm, om, dom, lse_m, B, MEM_HEADS, HEAD, MEM_HEADS, MEM_SCALE, False, "mem_attn_bwd")
    dz, G['g_mq'] = _headnorm_bwd(z, Z_QM // (MEM_HEADS * HEAD), MEM_HEADS, gmq, dqm, None, "memq_bwd",
                                  into=(dz, Z_QM // (MEM_HEADS * HEAD)))
    dkvm, G['g_mk'] = _headnorm_bwd(kvm, 0, MEM_HEADS, gmk, dkm, dvm, "memk_bwd")
    d_mem_kv = _matmul(nm, dkvm, 'tn', BF16, "mm_d_mem_kv")
    dnm = _matmul(dkvm, w_mem_kv, 'nt', ACT, "mm_dnm")
    G['g_mem'], = _rms_bwd(mem2d, g_mem, dnm, None, "rms_mem_bwd", dx_dtypes=())
    half = D_MODEL // 2
    ride = ws.scatter('lat', {'w_uq': d_uq, 'w_ukv': d_ukv, 'w_mem_kv': d_mem_kv})
    d_top, got = _unride(_matmul(h, dz, 'tn', BF16, "mm_d_in_top", tn_t=1792, m_rows=(0, half), rider=ride), ride)
    ws.scattered('lat', got)
    ride = ws.scatter('in_top', {'w_in': _win_unlayout(d_top)})
    d_bot, got = _unride(_matmul(h, dz, 'tn', BF16, "mm_d_in_bot", tn_t=1792, m_rows=(half, half), rider=ride), ride)
    ws.scattered('in_top', got)
    ride = ws.scatter('in_bot', {'w_in': _win_unlayout(d_bot)})
    dh, got = _unride(_matmul(dz, w_in, 'nt', ACT, "mm_dh", rider=ride), ride)
    ws.scattered('in_bot', got)
    gx, G['g_mix'] = _rms_bwd(x2d, g_mix, dh, dx1, "rms_mix_bwd")
    return loss_part, gx.reshape(B, S, D_MODEL), G


def _all_gather8(xs, name):
    def body(x_ref, out_ref, send_sems, recv_sems, local_sem):
        x, y, c = lax.axis_index("x"), lax.axis_index("y"), lax.axis_index("c")
        me, sibling = (x, y, c), (x, y, 1 - c)
        chips = [(1 - x, y), (x, 1 - y), (1 - x, 1 - y)]

        def rows(px, py, pc):
            return out_ref.at[4 * px + 2 * py + pc]

        def copy(k, block, to, src=None):
            return pltpu.make_async_remote_copy(
                src_ref=rows(*block) if src is None else src, dst_ref=rows(*block),
                send_sem=send_sems.at[k], recv_sem=recv_sems.at[k], device_id=to, device_id_type=MESH)

        mine = pltpu.make_async_copy(x_ref, rows(*me), local_sem)
        mine.start()
        first = [copy(0, me, sibling, src=x_ref)]
        first += [copy(1 + j, me, (*chip, c), src=x_ref) for j, chip in enumerate(chips)]
        for cp in first:
            cp.start()
        passed = [copy(4 + j, (*chip, c), sibling) for j, chip in enumerate(chips)]
        for j, chip in enumerate(chips):
            copy(1 + j, (*chip, c), me).wait_recv()
            passed[j].start()
        copy(0, sibling, me).wait_recv()
        for j, chip in enumerate(chips):
            copy(4 + j, (*chip, 1 - c), me).wait_recv()
        for cp in first + passed:
            cp.wait_send()
        mine.wait()

    return pl.pallas_call(
        body, name=name, in_specs=[HBM_SPEC], out_specs=HBM_SPEC,
        out_shape=jax.ShapeDtypeStruct((N_DEV,) + xs.shape, xs.dtype),
        scratch_shapes=[pltpu.SemaphoreType.DMA((7,)), pltpu.SemaphoreType.DMA((7,)), pltpu.SemaphoreType.DMA],
    )(xs)


def _adamw_rows(w, g, m, v):
    m2 = ADAM_B1 * m + (1.0 - ADAM_B1) * g
    v2 = ADAM_B2 * v + (1.0 - ADAM_B2) * (g * g)
    m_hat = m2 / (1.0 - ADAM_B1 ** ADAM_STEP)
    v_hat = v2 / (1.0 - ADAM_B2 ** ADAM_STEP)
    delta = -ADAM_LR * (m_hat / (jnp.sqrt(v_hat) + ADAM_EPS) + ADAM_WD * w)
    return delta, m2, v2


def _sum_adamw(parts, w, m, v, name):
    rows, cols = w.shape
    assert sum(p.shape[1] for p in parts) == rows
    tr = _pick(min(p.shape[1] for p in parts), max(16, 65536 // cols), 16)
    n = parts[0].shape[0]
    counts = [p.shape[1] // tr for p in parts]
    starts = [sum(counts[:k]) for k in range(len(parts))]

    def body(*refs):
        p_refs = refs[:len(parts)]
        w_ref, m_ref, v_ref, g_ref, d_ref, m2_ref, v2_ref = refs[len(parts):]
        g = None
        for p_ref, start in zip(p_refs, starts):
            gk = p_ref[0].astype(F32)
            for k in range(1, n):
                gk = gk + p_ref[k].astype(F32)
            g = gk if g is None else jnp.where(pl.program_id(0) >= start, gk, g)
        delta, m2, v2 = _adamw_rows(w_ref[...], g, m_ref[...], v_ref[...])
        g_ref[...] = g
        d_ref[...] = delta
        m2_ref[...] = m2
        v2_ref[...] = v2

    flat = pl.BlockSpec((tr, cols), lambda i: (i, 0))
    out = jax.ShapeDtypeStruct((rows, cols), F32)
    p_specs = [pl.BlockSpec((n, tr, cols), lambda i, s=s, c=c: (0, jnp.clip(i - s, 0, c - 1), 0))
               for s, c in zip(starts, counts)]
    return pl.pallas_call(
        body, name=name, grid=(rows // tr,), in_specs=p_specs + [flat, flat, flat], out_specs=[flat] * 4,
        out_shape=[out] * 4, compiler_params=_params(("parallel",)),
    )(*parts, w, m, v)


SMALL_WIDTH = {'g_mix': 1024, 'g_cq': 384, 'g_ckv': 256, 'g_q_nope': 128, 'g_q_pe': 128, 'g_k_nope': 128, 'g_k_pe': 128,
               'g_gm_ln': 512, 'b_gm_ln': 512, 'g_mem': 1024, 'g_mq': 128, 'g_mk': 128, 'g_ffn': 1024}
NARROW = ('g_q_pe', 'g_k_pe')


def _small_layout():
    layout, r = {}, 0
    for name in SMALL + ['loss']:
        rows = {'w_spatial': GM_GROUPS * GM_CHUNK, 'b_spatial': GM_GROUPS, 'loss': 1}.get(name) or SMALL_WIDTH[name] // LANES
        layout[name] = (r, rows)
        r += -(-rows // 8) * 8
    return layout, r


def _small_pack(grads, loss_part, name):
    layout, total = _small_layout()
    names = SMALL + ['loss']

    def body(*refs):
        out_ref = refs[-1]
        out_ref[...] = jnp.zeros((total, LANES), F32)
        for ref, n in zip(refs[:-1], names):
            r0, rows = layout[n]
            if n == 'w_spatial':
                for g in range(GM_GROUPS):
                    out_ref[r0 + g * GM_CHUNK:r0 + (g + 1) * GM_CHUNK, :] = ref[g]
            elif n == 'b_spatial':
                out_ref[r0:r0 + rows, :] = ref[...]
            else:
                for k in range(rows):
                    out_ref[r0 + k:r0 + k + 1, :] = ref[:, k * LANES:(k + 1) * LANES]

    return pl.pallas_call(body, name=name, out_shape=jax.ShapeDtypeStruct((total, LANES), F32))(
        *[grads[n] for n in SMALL], loss_part)


def _small_adamw(parts, w, m, v, name):
    layout, _ = _small_layout()
    n_dev = parts.shape[0]

    def body(*refs):
        p_ref = refs[0]
        ins = refs[1:1 + 3 * len(SMALL)]
        outs = refs[1 + 3 * len(SMALL):-1]

        def gsum(r0, rows):
            g = p_ref[0, r0:r0 + rows, :]
            for d in range(1, n_dev):
                g = g + p_ref[d, r0:r0 + rows, :]
            return g

        def step(idx, g, at):
            w_ref, m_ref, v_ref = ins[3 * idx:3 * idx + 3]
            delta, m2, v2 = _adamw_rows(w_ref[at], g, m_ref[at], v_ref[at])
            for ref, val in zip(outs[4 * idx:4 * idx + 4], (g, delta, m2, v2)):
                ref[at] = val

        for idx, n in enumerate(SMALL):
            r0, rows = layout[n]
            if n == 'w_spatial':
                for g in range(GM_GROUPS):
                    step(idx, gsum(r0 + g * GM_CHUNK, GM_CHUNK), (0, g))
            elif n == 'b_spatial':
                step(idx, gsum(r0, rows), (0,))
            else:
                for k in range(rows):
                    step(idx, gsum(r0 + k, 1), (slice(None), slice(k * LANES, (k + 1) * LANES)))
        refs[-1][...] = gsum(layout['loss'][0], 8)

    flat_in = [d[n] for n in SMALL for d in (w, m, v)]
    out_shape = [jax.ShapeDtypeStruct(w[n].shape, F32) for n in SMALL for _ in range(4)]
    res = pl.pallas_call(body, name=name, out_shape=out_shape + [jax.ShapeDtypeStruct((8, LANES), F32)])(parts, *flat_in)
    groups = [{n: res[4 * i + j] for i, n in enumerate(SMALL)} for j in range(4)]
    return groups, res[-1]


def _full_from_gathered(gathered, name):
    r, c = BIG_SHAPE[name]
    if BIG_AXIS[name] == 0:
        return gathered.reshape(r, c)
    return gathered.transpose(1, 0, 2).reshape(r, c)


def _shards_of_full(g, name):
    if g.ndim == 3:
        return g
    r, c = BIG_SHAPE[name]
    if BIG_AXIS[name] == 0:
        return g.reshape(N_DEV, r // N_DEV, c)
    return g.reshape(g.shape[0], N_DEV, c // N_DEV).transpose(1, 0, 2)


class _DistWeights:
    def __init__(self, shards):
        self.shards = shards
        self.received = {}

    def gather(self, names):
        return _Gather2([self.shards[n].astype(BF16) for n in names])

    def gathered(self, names, got):
        return {n: _full_from_gathered(g, n) for n, g in zip(names, got)}

    def scatter(self, key, grads):
        return _Exchange([_shards_of_full(grads[n], n) for n in RS_GROUPS[key]], scatter=True)

    def scattered(self, key, got):
        for n, g in zip(RS_GROUPS[key], got):
            self.received.setdefault(n, []).append(g)


def kernel(x, mem, positions, g_mix, w_in, g_cq, w_uq, g_ckv, w_ukv, g_q_nope, g_q_pe, g_k_nope, g_k_pe, g_gm_ln, b_gm_ln, w_spatial, b_spatial, g_mem, w_mem_kv, g_mq, g_mk, w_o_gm, w_o_mla, w_o_mem, w_out, g_ffn, w_ff1, w_ff2, loss_target, m_g_mix, m_w_in, m_g_cq, m_w_uq, m_g_ckv, m_w_ukv, m_g_q_nope, m_g_q_pe, m_g_k_nope, m_g_k_pe, m_g_gm_ln, m_b_gm_ln, m_w_spatial, m_b_spatial, m_g_mem, m_w_mem_kv, m_g_mq, m_g_mk, m_w_o_gm, m_w_o_mla, m_w_o_mem, m_w_out, m_g_ffn, m_w_ff1, m_w_ff2, v_g_mix, v_w_in, v_g_cq, v_w_uq, v_g_ckv, v_w_ukv, v_g_q_nope, v_g_q_pe, v_g_k_nope, v_g_k_pe, v_g_gm_ln, v_b_gm_ln, v_w_spatial, v_b_spatial, v_g_mem, v_w_mem_kv, v_g_mq, v_g_mk, v_w_o_gm, v_w_o_mla, v_w_o_mem, v_w_out, v_g_ffn, v_w_ff1, v_w_ff2):
    given = dict(locals())
    w = {n: given[n][0] for n in WEIGHTS}
    mom = {n: given['m_' + n][0] for n in WEIGHTS}
    var = {n: given['v_' + n][0] for n in WEIGHTS}

    ws = _DistWeights({n: w[n] for n in BIG})
    loss_part, grad_x, G = _local_step(x, mem, positions, loss_target, {n: w[n] for n in SMALL}, ws)

    outs = {}
    for n in BIG:
        for prefix, res in zip(("grad_", "delta_", "new_m_", "new_v_"),
                               _sum_adamw(ws.received[n], w[n], mom[n], var[n], "adamw_" + n)):
            outs[prefix + n] = res[None]

    def widen(d):
        return {n: (jnp.pad(d[n], ((0, 0), (0, LANES - MLA_ROPE))) if n in NARROW else d[n]) for n in SMALL}

    parts = _all_gather8(_small_pack(widen(G), loss_part, "small_pack"), "ag_small")
    small, loss_rows = _small_adamw(parts, *[widen({n: given[prefix + n] for n in SMALL}) for prefix in ("", "m_", "v_")],
                                    "adamw_small")
    loss = 0.5 * jnp.sum(loss_rows) / D_MODEL
    for prefix, group in zip(("grad_", "delta_", "new_m_", "new_v_"), small):
        for n in SMALL:
            outs[prefix + n] = group[n][:, :MLA_ROPE] if n in NARROW else group[n]
    return (loss, grad_x, *[outs[p + n] for p in ("grad_", "delta_", "new_m_", "new_v_") for n in WEIGHTS])
```

```python
import functools
import math

import jax
import jax.numpy as jnp
from jax import lax
from jax.experimental import pallas as pl
from jax.experimental.pallas import tpu as pltpu

F32 = jnp.float32
BF16 = jnp.bfloat16
ACT = BF16

D_MODEL = 1024
MEM_HEADS = 4
HEAD = 128
GM_WIDTH = 512
GM_CHUNK = 128
GM_GROUPS = 4
MLA_HEADS = 8
MLA_ROPE = 64
Q_LORA = 384
KV_LORA = 256
D_FF = 4096
EPS = 1e-6
ROPE_BASE = 10000.0
MLA_SCALE = 1.0 / math.sqrt(HEAD + MLA_ROPE)
MEM_SCALE = 1.0 / math.sqrt(HEAD)
LOG2E = 1.4426950408889634
LN2 = 0.6931471805599453
ATT_TILE = 256
C_ZU, C_ZV, C_CQ, C_CKV, C_KPE, C_QM, C_ZG, C_END = 0, 512, 1024, 1408, 1664, 1728, 2240, 5312
Z_GM, Z_QM, Z_MLA, Z_KPE, Z_COLS = 3072, 4096, 4608, 5248, 5376
MLA_W = 768
QCAT = 2 * HEAD
ADAM_LR, ADAM_B1, ADAM_B2, ADAM_EPS, ADAM_WD, ADAM_STEP = 0.001, 0.9, 0.999, 1e-08, 0.01, 10
N_DEV = 8
LANES = 128
VMEM_LIMIT = 48 * 1024 * 1024
MAX_K_TILE = 8192
NEG = -1e30

BIG = ['w_in', 'w_uq', 'w_ukv', 'w_mem_kv', 'w_o_gm', 'w_o_mla', 'w_o_mem', 'w_out', 'w_ff1', 'w_ff2']
BIG_AXIS = {'w_in': 1, 'w_uq': 1, 'w_ukv': 1, 'w_mem_kv': 0, 'w_o_gm': 1, 'w_o_mla': 0, 'w_o_mem': 1,
            'w_out': 0, 'w_ff1': 1, 'w_ff2': 0}
BIG_SHAPE = {'w_in': (1024, 5312), 'w_uq': (384, 1536), 'w_ukv': (256, 2048), 'w_mem_kv': (1024, 1024),
             'w_o_gm': (512, 1024), 'w_o_mla': (1024, 1024), 'w_o_mem': (512, 1024), 'w_out': (1024, 1024),
             'w_ff1': (1024, 4096), 'w_ff2': (4096, 1024)}
SMALL = ['g_mix', 'g_cq', 'g_ckv', 'g_q_nope', 'g_q_pe', 'g_k_nope', 'g_k_pe', 'g_gm_ln', 'b_gm_ln',
         'w_spatial', 'b_spatial', 'g_mem', 'g_mq', 'g_mk', 'g_ffn']
WEIGHTS = ['g_mix', 'w_in', 'g_cq', 'w_uq', 'g_ckv', 'w_ukv', 'g_q_nope', 'g_q_pe', 'g_k_nope', 'g_k_pe',
           'g_gm_ln', 'b_gm_ln', 'w_spatial', 'b_spatial', 'g_mem', 'w_mem_kv', 'g_mq', 'g_mk', 'w_o_gm',
           'w_o_mla', 'w_o_mem', 'w_out', 'g_ffn', 'w_ff1', 'w_ff2']


def _pick(n, target, mult=LANES):
    best = None
    t = mult
    while t <= min(n, target):
        if n % t == 0:
            best = t
        t += mult
    return best if best is not None else n


def _params(sem):
    return pltpu.CompilerParams(dimension_semantics=sem, vmem_limit_bytes=VMEM_LIMIT)


MESH = pl.DeviceIdType.MESH
HBM_SPEC = pl.BlockSpec(memory_space=pltpu.HBM)


class _Exchange:
    def __init__(self, srcs, scatter):
        self.srcs, self.scatter = list(srcs), scatter
        self.out_shapes = [jax.ShapeDtypeStruct(s.shape if scatter else (N_DEV,) + s.shape, s.dtype) for s in self.srcs]
        n = len(self.srcs)
        self.scratch = [pltpu.SemaphoreType.DMA((n, N_DEV - 1)), pltpu.SemaphoreType.DMA((n, N_DEV - 1)),
                        pltpu.SemaphoreType.DMA((n,))]

    def _copies(self, src_refs, dst_refs, send_sems, recv_sems, local_sems):
        x, y, c = lax.axis_index("x"), lax.axis_index("y"), lax.axis_index("c")
        me = 4 * x + 2 * y + c
        local, remote = [], []
        for a, (src_ref, dst_ref) in enumerate(zip(src_refs, dst_refs)):
            def mine_for(dev, src_ref=src_ref):
                return src_ref.at[dev] if self.scatter else src_ref

            local.append(pltpu.make_async_copy(mine_for(me), dst_ref.at[me], local_sems.at[a]))
            for k in range(1, N_DEV):
                px = 1 - x if k & 4 else x
                py = 1 - y if k & 2 else y
                pc = 1 - c if k & 1 else c
                remote.append(pltpu.make_async_remote_copy(
                    src_ref=mine_for(4 * px + 2 * py + pc), dst_ref=dst_ref.at[me], send_sem=send_sems.at[a, k - 1],
                    recv_sem=recv_sems.at[a, k - 1], device_id=(px, py, pc), device_id_type=MESH))
        return local, remote

    def start(self, *refs):
        local, remote = self._copies(*refs)
        for cp in local + remote:
            cp.start()

    def wait(self, *refs):
        local, remote = self._copies(*refs)
        for cp in remote + local:
            cp.wait()


class _Gather2:
    def __init__(self, srcs):
        self.srcs = list(srcs)
        self.out_shapes = [jax.ShapeDtypeStruct((N_DEV,) + s.shape, s.dtype) for s in self.srcs]
        n = len(self.srcs)
        self.scratch = [pltpu.SemaphoreType.DMA((n, N_DEV - 1)), pltpu.SemaphoreType.DMA((n, N_DEV - 1)),
                        pltpu.SemaphoreType.DMA((n,))]

    def _plan(self, src_refs, dst_refs, send_sems, recv_sems, local_sems):
        x, y, c = lax.axis_index("x"), lax.axis_index("y"), lax.axis_index("c")
        chips = [(1 - x, y), (x, 1 - y), (1 - x, 1 - y)]
        plans = []
        for a, (src_ref, dst_ref) in enumerate(zip(src_refs, dst_refs)):
            def copy(k, block, to, src=None, a=a, dst_ref=dst_ref):
                at = dst_ref.at[4 * block[0] + 2 * block[1] + block[2]]
                return pltpu.make_async_remote_copy(src_ref=at if src is None else src, dst_ref=at,
                                                    send_sem=send_sems.at[a, k], recv_sem=recv_sems.at[a, k],
                                                    device_id=to, device_id_type=MESH)

            local = pltpu.make_async_copy(src_ref, dst_ref.at[4 * x + 2 * y + c], local_sems.at[a])
            first = [copy(0, (x, y, c), (x, y, 1 - c), src=src_ref)]
            first += [copy(1 + j, (x, y, c), (*chip, c), src=src_ref) for j, chip in enumerate(chips)]
            passed = [copy(4 + j, (*chip, c), (x, y, 1 - c)) for j, chip in enumerate(chips)]
            arrivals = [copy(1 + j, (*chip, c), (x, y, c)) for j, chip in enumerate(chips)]
            late = [copy(0, (x, y, 1 - c), (x, y, c))] + [copy(4 + j, (*chip, 1 - c), (x, y, c)) for j, chip in enumerate(chips)]
            plans.append((local, first, passed, arrivals, late))
        return plans

    def start(self, *refs):
        for local, first, _, _, _ in self._plan(*refs):
            local.start()
            for cp in first:
                cp.start()

    def wait(self, *refs):
        plans = self._plan(*refs)
        for _, _, passed, arrivals, _ in plans:
            for arrived, onward in zip(arrivals, passed):
                arrived.wait_recv()
                onward.start()
        for local, first, passed, _, late in plans:
            for cp in late:
                cp.wait_recv()
            for cp in first + passed:
                cp.wait_send()
            local.wait()


def _call(body, rider, ins, *, name, grid, in_specs, out_specs, out_shape, scratch_shapes, sem):
    if rider is None:
        return pl.pallas_call(body, name=name, grid=grid, in_specs=in_specs, out_specs=out_specs, out_shape=out_shape,
                              scratch_shapes=scratch_shapes, compiler_params=_params(sem))(*ins)
    single = not isinstance(out_shape, (list, tuple))
    own_specs, own_shapes = ([out_specs], [out_shape]) if single else (list(out_specs), list(out_shape))
    n_in, n_out, n_sc, n_r = len(ins), len(own_shapes), len(scratch_shapes), len(rider.srcs)
    n_all_in = n_in + n_r

    def carrying(*refs):
        own_in, srcs = refs[:n_in], refs[n_in:n_in + n_r]
        own_out, dsts = refs[n_all_in:n_all_in + n_out], refs[n_all_in + n_out:n_all_in + n_out + n_r]
        own_sc = refs[n_all_in + n_out + n_r:n_all_in + n_out + n_r + n_sc]
        sems = refs[n_all_in + n_out + n_r + n_sc:]
        first = last = None
        for d, steps in enumerate(grid):
            f, l = pl.program_id(d) == 0, pl.program_id(d) == steps - 1
            first, last = (f, l) if first is None else (first & f, last & l)

        @pl.when(first)
        def _():
            rider.start(srcs, dsts, *sems)

        body(*own_in, *own_out, *own_sc)

        @pl.when(last)
        def _():
            rider.wait(srcs, dsts, *sems)

    res = pl.pallas_call(
        carrying, name=name, grid=grid, in_specs=list(in_specs) + [HBM_SPEC] * n_r,
        out_specs=own_specs + [HBM_SPEC] * n_r, out_shape=own_shapes + rider.out_shapes,
        scratch_shapes=list(scratch_shapes) + rider.scratch, compiler_params=_params(("arbitrary",) * len(grid)),
    )(*ins, *rider.srcs)
    own = res[:n_out]
    return (own[0] if single else list(own)), list(res[n_out:])


def _matmul(a, b, mode, out_dtype, name, add=None, relu2_a=False, relu2_grad=None,
            tm_t=None, tn_t=None, tk_t=None, rider=None, m_rows=None, col_shards=None, sq_err_target=None):
    if mode == 'nn':
        (M, K), (K2, N) = a.shape, b.shape
    elif mode == 'nt':
        (M, K), (N, K2) = a.shape, b.shape
    else:
        (K, M), (K2, N) = a.shape, b.shape
    assert K == K2, (name, a.shape, b.shape)
    m_first = 0
    if m_rows is not None:
        assert mode == 'tn'
        m_first, M = m_rows
    if col_shards is not None:
        assert add is None and relu2_grad is None and sq_err_target is None and tn_t is None
    if mode == 'tn':
        d_tm, d_tn, d_tk = 1024, (2048 if M <= 512 else 1024), 2048
    else:
        wide = add is None and sq_err_target is None and jnp.dtype(out_dtype).itemsize == 2 and K <= D_FF
        d_tm, d_tn, d_tk = (2048 if K <= 1024 else 1024), (1024 if wide else 512), MAX_K_TILE
    tm, tn, tk = _pick(M, tm_t or d_tm), _pick(N, tn_t or d_tn), _pick(K, tk_t or d_tk)
    gm, gn, nk = M // tm, N // tn, K // tk
    if mode == 'nn':
        a_spec = pl.BlockSpec((tm, tk), lambda i, j, k: (i, k))
        b_spec = pl.BlockSpec((tk, tn), lambda i, j, k: (k, j))
        dims = (((1,), (0,)), ((), ()))
    elif mode == 'nt':
        a_spec = pl.BlockSpec((tm, tk), lambda i, j, k: (i, k))
        b_spec = pl.BlockSpec((tn, tk), lambda i, j, k: (j, k))
        dims = (((1,), (1,)), ((), ()))
    else:
        assert m_first % tm == 0
        a_spec = pl.BlockSpec((tk, tm), lambda i, j, k: (k, m_first // tm + i))
        b_spec = pl.BlockSpec((tk, tn), lambda i, j, k: (k, j))
        dims = (((0,), (0,)), ((), ()))
    o_spec = pl.BlockSpec((tm, tn), lambda i, j, k: (i, j))
    shard_w = N // col_shards if col_shards is not None else tn
    assert tn % shard_w == 0
    has_add, has_e, has_t = add is not None, relu2_grad is not None, sq_err_target is not None
    assert not has_t or (nk == 1 and tn % LANES == 0)

    def body(*refs):
        a_ref, b_ref = refs[0], refs[1]
        pos = 2
        add_ref = e_ref = t_ref = None
        if has_add:
            add_ref = refs[pos]
            pos += 1
        if has_e:
            e_ref = refs[pos]
            pos += 1
        if has_t:
            t_ref = refs[pos]
            pos += 1
        o_ref = refs[pos]
        acc_ref = refs[pos + 1] if nk > 1 else None

        av = a_ref[...]
        if relu2_a:
            av = jnp.maximum(av, 0)
            av = av * av
        prod = lax.dot_general(av.astype(BF16), b_ref[...].astype(BF16), dims, preferred_element_type=F32)

        def finish(r):
            if has_add:
                r = r + add_ref[...]
            if has_e:
                r = r * (2.0 * jnp.maximum(e_ref[...].astype(F32), 0.0))
            if has_t:
                err = r - t_ref[...]
                r = err * (1.0 / N)
                refs[pos + 1][...] = r.astype(BF16)
                sq = err * err
                part = sq[:, 0:LANES]
                for c in range(1, tn // LANES):
                    part = part + sq[:, c * LANES:(c + 1) * LANES]
                _acc_rows(refs[pos + 2], part, (pl.program_id(0) == 0) & (pl.program_id(1) == 0))
            if col_shards is not None:
                for s in range(tn // shard_w):
                    o_ref[s] = r[:, s * shard_w:(s + 1) * shard_w].astype(out_dtype)
            else:
                o_ref[...] = r.astype(out_dtype)

        if nk == 1:
            finish(prod)
        else:
            k = pl.program_id(2)

            @pl.when(k == 0)
            def _():
                acc_ref[...] = prod

            @pl.when(k > 0)
            def _():
                acc_ref[...] += prod

            @pl.when(k == nk - 1)
            def _():
                finish(acc_ref[...])

    ins, specs = [a, b], [a_spec, b_spec]
    if has_add:
        ins.append(add)
        specs.append(o_spec)
    if has_e:
        ins.append(relu2_grad)
        specs.append(o_spec)
    out_specs, out_shape, sem = o_spec, jax.ShapeDtypeStruct((M, N), out_dtype), ("parallel", "parallel", "arbitrary")
    if has_t:
        ins.append(sq_err_target)
        specs.append(o_spec)
        out_specs = [o_spec, o_spec, pl.BlockSpec((1, LANES), lambda i, j, k: (0, 0))]
        out_shape = [out_shape, jax.ShapeDtypeStruct((M, N), BF16), jax.ShapeDtypeStruct((1, LANES), F32)]
        sem = ("arbitrary", "arbitrary", "arbitrary")
    if col_shards is not None:
        out_specs = pl.BlockSpec((tn // shard_w, tm, shard_w), lambda i, j, k: (j, i, 0))
        out_shape = jax.ShapeDtypeStruct((col_shards, M, shard_w), out_dtype)
    return _call(body, rider, ins, name=name, grid=(gm, gn, nk), in_specs=specs, out_specs=out_specs, out_shape=out_shape,
                 scratch_shapes=[pltpu.VMEM((tm, tn), F32)] if nk > 1 else [], sem=sem)


ROW_BLOCK_BYTES = 12 * 1024 * 1024


def _row_tile(rows, row_bytes):
    return _pick(rows, max(16, min(1024, ROW_BLOCK_BYTES // row_bytes)), 16)


def _rowspec(tr, width, col=0):
    return pl.BlockSpec((tr, width), lambda i, col=col: (i, col))


def _fullspec(shape):
    nd = len(shape)
    return pl.BlockSpec(shape, lambda i, nd=nd: (0,) * nd)


def _rms(x, width):
    x = x.astype(F32)
    return lax.rsqrt(jnp.sum(x * x, axis=-1, keepdims=True) * (1.0 / width) + EPS)


def _rms_bwd_rows(x, g, dy, width):
    x, dy = x.astype(F32), dy.astype(F32)
    r = _rms(x, width)
    xh = x * r
    dn = dy * g
    dx = r * (dn - xh * (jnp.sum(dn * xh, axis=-1, keepdims=True) * (1.0 / width)))
    return dx, dy * xh


def _acc_rows(ref, val, first):
    s = jnp.sum(val, axis=0, keepdims=True)

    @pl.when(first)
    def _():
        ref[...] = s

    @pl.when(jnp.logical_not(first))
    def _():
        ref[...] += s


def _rms_fwd(x, g, name, rider=None):
    rows, width = x.shape
    tr = _row_tile(rows, 6 * width)

    def body(x_ref, g_ref, o_ref):
        xv = x_ref[...]
        o_ref[...] = (xv * _rms(xv, width) * g_ref[...]).astype(BF16)

    return _call(body, rider, [x, g], name=name, grid=(rows // tr,),
                 in_specs=[_rowspec(tr, width), _fullspec((1, width))], out_specs=_rowspec(tr, width),
                 out_shape=jax.ShapeDtypeStruct((rows, width), BF16), scratch_shapes=[], sem=("parallel",))


def _rms_bwd(x, g, dy, res, name, dx_dtypes=(F32,)):
    rows, width = x.shape
    tr = _row_tile(rows, 18 * width)
    has_res = res is not None
    n_in = 4 if has_res else 3

    def body(*refs):
        x_ref, g_ref, dy_ref = refs[:3]
        dx, dgv = _rms_bwd_rows(x_ref[...], g_ref[...], dy_ref[...], width)
        if has_res:
            dx = dx + refs[3][...]
        for ref, dt in zip(refs[n_in:], dx_dtypes):
            ref[...] = dx.astype(dt)
        _acc_rows(refs[-1], dgv, pl.program_id(0) == 0)

    ins = [x, g, dy] + ([res] if has_res else [])
    specs = [_rowspec(tr, width), _fullspec((1, width)), _rowspec(tr, width)] + ([_rowspec(tr, width)] if has_res else [])
    return pl.pallas_call(
        body, name=name, grid=(rows // tr,), in_specs=specs,
        out_specs=[_rowspec(tr, width)] * len(dx_dtypes) + [_fullspec((1, width))],
        out_shape=[jax.ShapeDtypeStruct((rows, width), dt) for dt in dx_dtypes] + [jax.ShapeDtypeStruct((1, width), F32)],
        compiler_params=_params(("arbitrary",)),
    )(*ins)


_GELU_C = math.sqrt(2.0 / math.pi)


def _gelu(x):
    t = jnp.tanh(_GELU_C * (x + 0.044715 * (x * x * x)))
    return 0.5 * x * (1.0 + t), t


def _gelu_grad(x, t):
    return 0.5 * (1.0 + t) + 0.5 * x * (1.0 - t * t) * (_GELU_C * (1.0 + 3.0 * 0.044715 * (x * x)))


def _gm_forward_rows(zu, zv, gln, bln, wc_ref, bst, n_chunk):
    u, tu = _gelu(zu)
    a, ta = _gelu(zv)
    mu = jnp.mean(a, axis=-1, keepdims=True)
    ac = a - mu
    rs = lax.rsqrt(jnp.mean(ac * ac, axis=-1, keepdims=True) + EPS)
    n = ac * rs
    v = n * gln + bln
    vb = v.astype(BF16)
    rows = []
    for c in range(n_chunk):
        cols = []
        for g in range(GM_GROUPS):
            vc = vb[c * GM_CHUNK:(c + 1) * GM_CHUNK, g * LANES:(g + 1) * LANES]
            mixed = jnp.dot(wc_ref[g], vc, preferred_element_type=F32) + bst[g]
            cols.append(mixed)
        rows.append(jnp.concatenate(cols, axis=1))
    mixed = jnp.concatenate(rows, axis=0) if n_chunk > 1 else rows[0]
    return u, tu, ta, n, rs, v, mixed


def _gm_fwd(z, gln, bln, wc, bst, name):
    rows = z.shape[0]
    tr = _pick(rows, 512, GM_CHUNK)
    n_chunk = tr // GM_CHUNK

    def body(zu_ref, zv_ref, gln_ref, bln_ref, wc_ref, bst_ref, o_ref):
        u, _, _, _, _, _, mixed = _gm_forward_rows(zu_ref[...].astype(F32), zv_ref[...].astype(F32), gln_ref[...], bln_ref[...], wc_ref,
                                                   bst_ref, n_chunk)
        o_ref[...] = (u * mixed).astype(BF16)

    return pl.pallas_call(
        body, name=name, grid=(rows // tr,),
        in_specs=[_rowspec(tr, GM_WIDTH, Z_GM // GM_WIDTH), _rowspec(tr, GM_WIDTH, Z_GM // GM_WIDTH + 1),_fullspec((1, GM_WIDTH)), _fullspec((1, GM_WIDTH)),
                  _fullspec((GM_GROUPS, GM_CHUNK, GM_CHUNK)), _fullspec((GM_GROUPS, GM_CHUNK, LANES))],
        out_specs=_rowspec(tr, GM_WIDTH), out_shape=jax.ShapeDtypeStruct((rows, GM_WIDTH), BF16),
        compiler_params=_params(("parallel",)),
    )(z, z, gln, bln, wc, bst)


ANY_SPEC = pl.BlockSpec(memory_space=pl.ANY)


def _gm_bwd(z, dy, gln, bln, wc, wct, bst, dz, name):
    rows = z.shape[0]
    tr = _pick(rows, 512, GM_CHUNK)
    n_chunk = tr // GM_CHUNK

    def body(zu_ref, zv_ref, dy_ref, gln_ref, bln_ref, wc_ref, wct_ref, bst_ref, _, dz_ref, dws_ref, dbs_ref, dgl_ref,
             dbl_ref):
        first = pl.program_id(0) == 0
        zu, zv, gln = zu_ref[...].astype(F32), zv_ref[...].astype(F32), gln_ref[...]
        u, tu, ta, n, rs, v, mixed = _gm_forward_rows(zu, zv, gln, bln_ref[...], wc_ref, bst_ref, n_chunk)
        dyv = dy_ref[...].astype(F32)
        dzu = dyv * mixed * _gelu_grad(zu, tu)
        dmix = dyv * u
        dmb = dmix.astype(BF16)
        vb = v.astype(BF16)
        dv_rows, dws, dbs = [], [None] * GM_GROUPS, None
        for c in range(n_chunk):
            rsl = slice(c * GM_CHUNK, (c + 1) * GM_CHUNK)
            cols = []
            for g in range(GM_GROUPS):
                csl = slice(g * LANES, (g + 1) * LANES)
                dmc = dmb[rsl, csl]
                cols.append(jnp.dot(wct_ref[g], dmc, preferred_element_type=F32))
                w_part = lax.dot_general(dmc, vb[rsl, csl], (((1,), (1,)), ((), ())), preferred_element_type=F32)
                dws[g] = w_part if dws[g] is None else dws[g] + w_part
            dv_rows.append(jnp.concatenate(cols, axis=1))
            dbs = dmix[rsl, :] if dbs is None else dbs + dmix[rsl, :]
        dv = jnp.concatenate(dv_rows, axis=0) if n_chunk > 1 else dv_rows[0]
        dn = dv * gln
        da = rs * (dn - jnp.mean(dn, axis=-1, keepdims=True) - n * jnp.mean(dn * n, axis=-1, keepdims=True))
        dzv = da * _gelu_grad(zv, ta)
        dz_ref[:, 0:GM_WIDTH] = dzu.astype(BF16)
        dz_ref[:, GM_WIDTH:2 * GM_WIDTH] = dzv.astype(BF16)
        _acc_rows(dgl_ref, dv * n, first)
        _acc_rows(dbl_ref, dv, first)

        @pl.when(first)
        def _():
            for g in range(GM_GROUPS):
                dws_ref[g] = dws[g]
            dbs_ref[...] = dbs

        @pl.when(jnp.logical_not(first))
        def _():
            for g in range(GM_GROUPS):
                dws_ref[g] += dws[g]
            dbs_ref[...] += dbs

    wspec = _fullspec((GM_GROUPS, GM_CHUNK, GM_CHUNK))
    return pl.pallas_call(
        body, name=name, grid=(rows // tr,),
        in_specs=[_rowspec(tr, GM_WIDTH, Z_GM // GM_WIDTH), _rowspec(tr, GM_WIDTH, Z_GM // GM_WIDTH + 1),
                  _rowspec(tr, GM_WIDTH), _fullspec((1, GM_WIDTH)), _fullspec((1, GM_WIDTH)), wspec, wspec, wspec, ANY_SPEC],
        out_specs=[_rowspec(tr, 2 * GM_WIDTH, Z_GM // (2 * GM_WIDTH)), wspec, _fullspec((GM_CHUNK, GM_WIDTH)),
                   _fullspec((1, GM_WIDTH)), _fullspec((1, GM_WIDTH))],
        out_shape=[jax.ShapeDtypeStruct(dz.shape, dz.dtype), jax.ShapeDtypeStruct((GM_GROUPS, GM_CHUNK, GM_CHUNK), F32),
                   jax.ShapeDtypeStruct((GM_CHUNK, GM_WIDTH), F32), jax.ShapeDtypeStruct((1, GM_WIDTH), F32),
                   jax.ShapeDtypeStruct((1, GM_WIDTH), F32)],
        input_output_aliases={8: 0}, compiler_params=_params(("arbitrary",)),
    )(z, z, dy, gln, bln, wc, wct, bst, dz)


def _lat_fwd(z, g_cq, g_ckv, name):
    rows = z.shape[0]
    tr = _row_tile(rows, 4 * MLA_W)

    def body(z_ref, gq_ref, gkv_ref, nq_ref, nkv_ref):
        zb = z_ref[...]
        cq, ckv = zb[:, 0:Q_LORA], zb[:, Q_LORA:Q_LORA + KV_LORA]
        nq_ref[...] = (cq * _rms(cq, Q_LORA) * gq_ref[...]).astype(BF16)
        nkv_ref[...] = (ckv * _rms(ckv, KV_LORA) * gkv_ref[...]).astype(BF16)

    return pl.pallas_call(
        body, name=name, grid=(rows // tr,),
        in_specs=[_rowspec(tr, MLA_W, Z_MLA // MLA_W), _fullspec((1, Q_LORA)), _fullspec((1, KV_LORA))],
        out_specs=[_rowspec(tr, Q_LORA), _rowspec(tr, KV_LORA)],
        out_shape=[jax.ShapeDtypeStruct((rows, Q_LORA), BF16), jax.ShapeDtypeStruct((rows, KV_LORA), BF16)],
        compiler_params=_params(("parallel",)),
    )(z, g_cq, g_ckv)


def _lat_bwd(z, dnq, dnkv, dkpe, g_cq, g_ckv, dz, name):
    rows = z.shape[0]
    tr = _row_tile(rows, 8 * MLA_W)

    def body(z_ref, dnq_ref, dnkv_ref, dkpe_ref, gq_ref, gkv_ref, _, dz_ref, dgq_ref, dgkv_ref):
        first = pl.program_id(0) == 0
        zb = z_ref[...]
        dcq, dgq = _rms_bwd_rows(zb[:, 0:Q_LORA], gq_ref[...], dnq_ref[...], Q_LORA)
        dckv, dgkv = _rms_bwd_rows(zb[:, Q_LORA:Q_LORA + KV_LORA], gkv_ref[...], dnkv_ref[...], KV_LORA)
        dz_ref[:, 0:Q_LORA] = dcq.astype(BF16)
        dz_ref[:, Q_LORA:Q_LORA + KV_LORA] = dckv.astype(BF16)
        dz_ref[:, Q_LORA + KV_LORA:MLA_W] = dkpe_ref[...].astype(BF16)
        _acc_rows(dgq_ref, dgq, first)
        _acc_rows(dgkv_ref, dgkv, first)

    return pl.pallas_call(
        body, name=name, grid=(rows // tr,),
        in_specs=[_rowspec(tr, MLA_W, Z_MLA // MLA_W), _rowspec(tr, Q_LORA), _rowspec(tr, KV_LORA), _rowspec(tr, LANES),
                  _fullspec((1, Q_LORA)), _fullspec((1, KV_LORA)), ANY_SPEC],
        out_specs=[_rowspec(tr, MLA_W, Z_MLA // MLA_W), _fullspec((1, Q_LORA)), _fullspec((1, KV_LORA))],
        out_shape=[jax.ShapeDtypeStruct(dz.shape, dz.dtype), jax.ShapeDtypeStruct((1, Q_LORA), F32),
                   jax.ShapeDtypeStruct((1, KV_LORA), F32)],
        input_output_aliases={6: 0}, compiler_params=_params(("arbitrary",)),
    )(z, dnq, dnkv, dkpe, g_cq, g_ckv, dz)


def _rope(y, cc, ss):
    return y * cc + pltpu.roll(y, 64, 1) * ss


def _rope_bwd(d, cc, ss):
    return d * cc + pltpu.roll(d * ss, 64, 1)


def _qk_fwd(q, kv, z, cc, ss, gqn, gqp, gkn, gkp, name):
    rows = q.shape[0]
    W = MLA_HEADS * HEAD
    tr = _row_tile(rows, 20 * W)
    QS = MLA_SCALE * LOG2E

    def body(q_ref, kv_ref, kpe_ref, cc_ref, ss_ref, gqn_ref, gqp_ref, gkn_ref, gkp_ref, qc_ref, kc_ref, v_ref):
        cc, ss = cc_ref[...], ss_ref[...]
        kpe = kpe_ref[...]
        kp = _rope(kpe * _rms(kpe, MLA_ROPE) * gkp_ref[...], cc, ss).astype(BF16)
        for h in range(MLA_HEADS):
            qn = q_ref[:, h * HEAD:(h + 1) * HEAD]
            qp = q_ref[:, W + h * HEAD:W + (h + 1) * HEAD]
            kn = kv_ref[:, h * HEAD:(h + 1) * HEAD]
            qc_ref[:, h * QCAT:h * QCAT + HEAD] = (qn * _rms(qn, HEAD) * gqn_ref[...] * QS).astype(BF16)
            qc_ref[:, h * QCAT + HEAD:(h + 1) * QCAT] = (_rope(qp * _rms(qp, MLA_ROPE) * gqp_ref[...], cc, ss) * QS).astype(BF16)
            kc_ref[:, h * QCAT:h * QCAT + HEAD] = (kn * _rms(kn, HEAD) * gkn_ref[...]).astype(BF16)
            kc_ref[:, h * QCAT + HEAD:(h + 1) * QCAT] = kp
        v_ref[...] = kv_ref[:, W:2 * W].astype(BF16)

    g = _fullspec((1, HEAD))
    return _call(
        body, None, [q, kv, z, cc, ss, gqn, gqp, gkn, gkp], name=name, grid=(rows // tr,),
        in_specs=[_rowspec(tr, 2 * W), _rowspec(tr, 2 * W), _rowspec(tr, LANES, Z_KPE // LANES), _rowspec(tr, LANES),
                  _rowspec(tr, LANES), g, g, g, g],
        out_specs=[_rowspec(tr, MLA_HEADS * QCAT), _rowspec(tr, MLA_HEADS * QCAT), _rowspec(tr, W)],
        out_shape=[jax.ShapeDtypeStruct((rows, MLA_HEADS * QCAT), BF16), jax.ShapeDtypeStruct((rows, MLA_HEADS * QCAT), BF16),
                   jax.ShapeDtypeStruct((rows, W), BF16)],
        scratch_shapes=[], sem=("parallel",))


def _qk_bwd(q, kv, z, cc, ss, gqn, gqp, gkn, gkp, dqc, dkc, dv, name):
    rows = q.shape[0]
    W = MLA_HEADS * HEAD
    tr = _row_tile(rows, 40 * W)

    def body(q_ref, kv_ref, kpe_ref, cc_ref, ss_ref, gqn_ref, gqp_ref, gkn_ref, gkp_ref, dqc_ref, dkc_ref, dv_ref,
             dq_ref, dkv_ref, dkpe_ref, dgqn_ref, dgqp_ref, dgkn_ref, dgkp_ref):
        first = pl.program_id(0) == 0
        cc, ss = cc_ref[...], ss_ref[...]
        sqn = sqp = skn = dkp = None
        for h in range(MLA_HEADS):
            dx, dg = _rms_bwd_rows(q_ref[:, h * HEAD:(h + 1) * HEAD], gqn_ref[...], dqc_ref[:, h * QCAT:h * QCAT + HEAD], HEAD)
            dq_ref[:, h * HEAD:(h + 1) * HEAD] = dx.astype(BF16)
            sqn = dg if sqn is None else sqn + dg
            dy = _rope_bwd(dqc_ref[:, h * QCAT + HEAD:(h + 1) * QCAT], cc, ss)
            dx, dg = _rms_bwd_rows(q_ref[:, W + h * HEAD:W + (h + 1) * HEAD], gqp_ref[...], dy, MLA_ROPE)
            dq_ref[:, W + h * HEAD:W + (h + 1) * HEAD] = dx.astype(BF16)
            sqp = dg if sqp is None else sqp + dg
            dx, dg = _rms_bwd_rows(kv_ref[:, h * HEAD:(h + 1) * HEAD], gkn_ref[...], dkc_ref[:, h * QCAT:h * QCAT + HEAD], HEAD)
            dkv_ref[:, h * HEAD:(h + 1) * HEAD] = dx.astype(BF16)
            skn = dg if skn is None else skn + dg
            part = dkc_ref[:, h * QCAT + HEAD:(h + 1) * QCAT].astype(F32)
            dkp = part if dkp is None else dkp + part
        dkv_ref[:, W:2 * W] = dv_ref[...].astype(BF16)
        dx, dg = _rms_bwd_rows(kpe_ref[...], gkp_ref[...], _rope_bwd(dkp, cc, ss), MLA_ROPE)
        dkpe_ref[...] = dx
        _acc_rows(dgqn_ref, sqn, first)
        _acc_rows(dgqp_ref, sqp, first)
        _acc_rows(dgkn_ref, skn, first)
        _acc_rows(dgkp_ref, dg, first)

    g = _fullspec((1, HEAD))
    gs = jax.ShapeDtypeStruct((1, HEAD), F32)
    return pl.pallas_call(
        body, name=name, grid=(rows // tr,),
        in_specs=[_rowspec(tr, 2 * W), _rowspec(tr, 2 * W), _rowspec(tr, LANES, Z_KPE // LANES), _rowspec(tr, LANES),
                  _rowspec(tr, LANES), g, g, g, g, _rowspec(tr, MLA_HEADS * QCAT), _rowspec(tr, MLA_HEADS * QCAT),
                  _rowspec(tr, W)],
        out_specs=[_rowspec(tr, 2 * W), _rowspec(tr, 2 * W), _rowspec(tr, LANES), g, g, g, g],
        out_shape=[jax.ShapeDtypeStruct((rows, 2 * W), BF16), jax.ShapeDtypeStruct((rows, 2 * W), BF16),
                   jax.ShapeDtypeStruct((rows, LANES), F32), gs, gs, gs, gs],
        compiler_params=_params(("arbitrary",)),
    )(q, kv, z, cc, ss, gqn, gqp, gkn, gkp, dqc, dkc, dv)


def _headnorm_fwd(x, col, nheads, g, out_scale, name):
    rows = x.shape[0]
    W = nheads * HEAD
    tr = _row_tile(rows, 6 * W)

    def body(x_ref, g_ref, o_ref):
        for h in range(nheads):
            xv = x_ref[:, h * HEAD:(h + 1) * HEAD]
            o_ref[:, h * HEAD:(h + 1) * HEAD] = (xv * _rms(xv, HEAD) * g_ref[...] * out_scale).astype(BF16)

    return pl.pallas_call(
        body, name=name, grid=(rows // tr,),
        in_specs=[_rowspec(tr, W, col), _fullspec((1, HEAD))], out_specs=_rowspec(tr, W),
        out_shape=jax.ShapeDtypeStruct((rows, W), BF16), compiler_params=_params(("parallel",)),
    )(x, g)


def _headnorm_bwd(x, col, nheads, g, dy, tail, name, into=None):
    rows = x.shape[0]
    W = nheads * HEAD
    tr = _row_tile(rows, 12 * W)
    has_tail = tail is not None
    WO = 2 * W if has_tail else W

    def body(*refs):
        if into is not None:
            x_ref, g_ref, dy_ref, _, dx_ref, dg_ref = refs
        elif has_tail:
            x_ref, g_ref, dy_ref, t_ref, dx_ref, dg_ref = refs
        else:
            x_ref, g_ref, dy_ref, dx_ref, dg_ref = refs
        acc = None
        for h in range(nheads):
            sl = slice(h * HEAD, (h + 1) * HEAD)
            dx, dg = _rms_bwd_rows(x_ref[:, sl], g_ref[...], dy_ref[:, sl], HEAD)
            dx_ref[:, sl] = dx.astype(BF16)
            acc = dg if acc is None else acc + dg
        if has_tail:
            dx_ref[:, W:2 * W] = t_ref[...].astype(BF16)
        _acc_rows(dg_ref, acc, pl.program_id(0) == 0)

    ins = [x, g, dy] + ([tail] if has_tail else [])
    specs = [_rowspec(tr, W, col), _fullspec((1, HEAD)), _rowspec(tr, W)] + ([_rowspec(tr, W)] if has_tail else [])
    dx_spec, dx_shape, aliases = _rowspec(tr, WO), jax.ShapeDtypeStruct((rows, WO), BF16), {}
    if into is not None:
        assert not has_tail
        ins, specs = ins + [into[0]], specs + [ANY_SPEC]
        dx_spec, dx_shape, aliases = _rowspec(tr, W, into[1]), jax.ShapeDtypeStruct(into[0].shape, into[0].dtype), {3: 0}
    return pl.pallas_call(
        body, name=name, grid=(rows // tr,), in_specs=specs,
        out_specs=[dx_spec, _fullspec((1, HEAD))], out_shape=[dx_shape, jax.ShapeDtypeStruct((1, HEAD), F32)],
        input_output_aliases=aliases, compiler_params=_params(("arbitrary",)),
    )(*ins)


def _sigmoid(x):
    return 1.0 / (1.0 + jnp.exp(-x.astype(F32)))


def _merge_fwd(z, y_gm, y_mla, y_mem, name):
    rows = z.shape[0]
    tr = _row_tile(rows, 14 * D_MODEL)

    def body(g0_ref, g1_ref, g2_ref, a_ref, b_ref, c_ref, o_ref):
        m = _sigmoid(g0_ref[...]) * a_ref[...] + _sigmoid(g1_ref[...]) * b_ref[...] + _sigmoid(g2_ref[...]) * c_ref[...]
        o_ref[...] = m.astype(BF16)

    r = _rowspec(tr, D_MODEL)
    return pl.pallas_call(
        body, name=name, grid=(rows // tr,),
        in_specs=[_rowspec(tr, D_MODEL, 0), _rowspec(tr, D_MODEL, 1), _rowspec(tr, D_MODEL, 2),r, r, r],
        out_specs=r, out_shape=jax.ShapeDtypeStruct((rows, D_MODEL), BF16), compiler_params=_params(("parallel",)),
    )(z, z, z, y_gm, y_mla, y_mem)


def _merge_bwd(z, y_gm, y_mla, y_mem, dm, name):
    rows = z.shape[0]
    tr = _row_tile(rows, 24 * D_MODEL)

    def body(g0_ref, g1_ref, g2_ref, a_ref, b_ref, c_ref, dm_ref, da_ref, db_ref, dc_ref, dzg_ref):
        dmv = dm_ref[...].astype(F32)
        for k, (g_ref, y_ref, dy_ref) in enumerate(((g0_ref, a_ref, da_ref), (g1_ref, b_ref, db_ref), (g2_ref, c_ref, dc_ref))):
            s = _sigmoid(g_ref[...])
            dy_ref[...] = (dmv * s).astype(BF16)
            dzg_ref[:, k * D_MODEL:(k + 1) * D_MODEL] = (dmv * y_ref[...] * s * (1.0 - s)).astype(BF16)

    r = _rowspec(tr, D_MODEL)
    o = jax.ShapeDtypeStruct((rows, D_MODEL), BF16)
    return pl.pallas_call(
        body, name=name, grid=(rows // tr,),
        in_specs=[_rowspec(tr, D_MODEL, 0), _rowspec(tr, D_MODEL, 1), _rowspec(tr, D_MODEL, 2),r, r, r, r],
        out_specs=[r, r, r, _rowspec(tr, 3 * D_MODEL, 0)],
        out_shape=[o, o, o, jax.ShapeDtypeStruct((rows, Z_COLS), BF16)],
        compiler_params=_params(("parallel",)),
    )(z, z, z, y_gm, y_mla, y_mem, dm)


_NT = (((1,), (1,)), ((), ()))
_TN = (((0,), (0,)), ((), ()))


def _diag_mask(s):
    row = lax.broadcasted_iota(jnp.int32, s.shape, 0)
    col = lax.broadcasted_iota(jnp.int32, s.shape, 1)
    return jnp.where(row >= col, s, NEG)


def _attn_fwd(q, k, v, nb, nheads, dk, v_col0, causal, name, rider=None):
    S, Skv = q.shape[0] // nb, k.shape[0] // nb
    tq = _pick(Skv, ATT_TILE) if causal else _pick(S, 4 * ATT_TILE)
    nq = S // tq

    def body(q_ref, k_ref, v_ref, o_ref, lse_ref):
        for i in range(nq):
            r0 = i * tq
            qb = q_ref[r0:r0 + tq, :]
            if causal:
                spans = ([(0, r0, False)] if i > 0 else []) + [(r0, r0 + tq, True)]
            else:
                spans = [(0, Skv, False)]
            scores = []
            for a, b, masked in spans:
                s = lax.dot_general(qb, k_ref[a:b, :], _NT, preferred_element_type=F32)
                scores.append(_diag_mask(s) if masked else s)
            m = functools.reduce(jnp.maximum, [jnp.max(s, axis=-1, keepdims=True) for s in scores])
            l = acc = None
            for s, (a, b, _) in zip(scores, spans):
                p = jnp.exp2(s - m)
                lp = jnp.sum(p, axis=-1, keepdims=True)
                ap = jnp.dot(p.astype(BF16), v_ref[a:b, :].astype(BF16), preferred_element_type=F32)
                l, acc = (lp, ap) if l is None else (l + lp, acc + ap)
            o_ref[r0:r0 + tq, :] = (acc / l).astype(BF16)
            lse_ref[r0:r0 + tq, :] = m + jnp.log2(l)

    ins = [q, k, v]
    in_specs = [pl.BlockSpec((S, dk), lambda b, h: (b, h)), pl.BlockSpec((Skv, dk), lambda b, h: (b, h)),
                pl.BlockSpec((Skv, HEAD), lambda b, h: (b, v_col0 + h))]
    out_specs = [pl.BlockSpec((S, HEAD), lambda b, h: (b, h)), pl.BlockSpec((None, S, 1), lambda b, h: (h, b, 0))]
    out_shape = [jax.ShapeDtypeStruct((nb * S, nheads * HEAD), BF16), jax.ShapeDtypeStruct((nheads, nb * S, 1), F32)]
    return _call(body, rider, ins, name=name, grid=(nb, nheads), in_specs=in_specs, out_specs=out_specs,
                 out_shape=out_shape, scratch_shapes=[], sem=("parallel", "parallel"))


def _attn_bwd(q, k, v, o, do, lse, nb, nheads, dk, v_col0, scale, causal, name, rider=None):
    S, Skv = q.shape[0] // nb, k.shape[0] // nb
    tk = _pick(Skv, ATT_TILE)
    nkv = Skv // tk

    def body(q_ref, k_ref, v_ref, o_ref, do_ref, lse_ref, dq_ref, dk_ref, dv_ref, delta_ref, dob_ref, dqa_ref):
        dov = do_ref[...]
        delta_ref[...] = jnp.sum(o_ref[...].astype(F32) * dov.astype(F32), axis=-1, keepdims=True)
        dob_ref[...] = dov.astype(BF16)

        for j in range(nkv):
            c0 = j * tk
            kb = k_ref[c0:c0 + tk, :]
            vb = v_ref[c0:c0 + tk, :].astype(BF16)
            if causal:
                spans = [(c0, c0 + tk, True)] + ([(c0 + tk, S, False)] if c0 + tk < S else [])
            else:
                spans = [(0, S, False)]
            dk_acc = dv_acc = None
            for a, b, masked in spans:
                qb = q_ref[a:b, :]
                dob = dob_ref[a:b, :]
                s = lax.dot_general(qb, kb, _NT, preferred_element_type=F32)
                if masked:
                    s = _diag_mask(s)
                p = jnp.exp2(s - lse_ref[a:b, :])
                dp = lax.dot_general(dob, vb, _NT, preferred_element_type=F32)
                ds = (p * (dp - delta_ref[a:b, :])).astype(BF16)
                dv_p = lax.dot_general(p.astype(BF16), dob, _TN, preferred_element_type=F32)
                dk_p = lax.dot_general(ds, qb, _TN, preferred_element_type=F32)
                dk_acc, dv_acc = (dk_p, dv_p) if dk_acc is None else (dk_acc + dk_p, dv_acc + dv_p)
                dq_p = jnp.dot(ds, kb, preferred_element_type=F32) * scale
                if j == 0:
                    dqa_ref[a:b, :] = dq_p
                else:
                    dqa_ref[a:b, :] += dq_p
            dk_ref[c0:c0 + tk, :] = (dk_acc * LN2).astype(BF16)
            dv_ref[c0:c0 + tk, :] = dv_acc.astype(BF16)
        dq_ref[...] = dqa_ref[...].astype(BF16)

    ins = [q, k, v, o, do, lse]
    in_specs = [pl.BlockSpec((S, dk), lambda b, h: (b, h)), pl.BlockSpec((Skv, dk), lambda b, h: (b, h)),
                pl.BlockSpec((Skv, HEAD), lambda b, h: (b, v_col0 + h)), pl.BlockSpec((S, HEAD), lambda b, h: (b, h)),
                pl.BlockSpec((S, HEAD), lambda b, h: (b, h)), pl.BlockSpec((None, S, 1), lambda b, h: (h, b, 0))]
    out_specs = [pl.BlockSpec((S, dk), lambda b, h: (b, h)), pl.BlockSpec((Skv, dk), lambda b, h: (b, h)),
                 pl.BlockSpec((Skv, HEAD), lambda b, h: (b, h))]
    out_shape = [jax.ShapeDtypeStruct((nb * S, nheads * dk), BF16), jax.ShapeDtypeStruct((nb * Skv, nheads * dk), BF16),
                 jax.ShapeDtypeStruct((nb * Skv, nheads * HEAD), BF16)]
    return _call(body, rider, ins, name=name, grid=(nb, nheads), in_specs=in_specs, out_specs=out_specs,
                 out_shape=out_shape,
                 scratch_shapes=[pltpu.VMEM((S, 1), F32), pltpu.VMEM((S, HEAD), BF16), pltpu.VMEM((S, dk), F32)],
                 sem=("parallel", "parallel"))


def _spread_rope(a):
    zero = jnp.zeros(a.shape[:-1] + (32,), a.dtype)
    return jnp.concatenate([a[..., :32], zero, a[..., 32:], zero], axis=-1)


def _gather_rope(a):
    return jnp.concatenate([a[..., 0:32], a[..., 64:96]], axis=-1)


def _win_layout(w):
    return jnp.concatenate([w[:, C_ZG:C_END], w[:, C_ZU:C_CQ], w[:, C_QM:C_ZG], w[:, C_CQ:C_CKV], w[:, C_CKV:C_KPE],
                            _spread_rope(w[:, C_KPE:C_QM])], axis=1)


def _win_unlayout(d):
    return jnp.concatenate([d[:, Z_GM:Z_QM], d[:, Z_MLA:Z_MLA + Q_LORA], d[:, Z_MLA + Q_LORA:Z_KPE],
                            _gather_rope(d[:, Z_KPE:Z_COLS]), d[:, Z_QM:Z_MLA], d[:, 0:Z_GM]], axis=1)


def _wuq_layout(w):
    r = w.reshape(Q_LORA, MLA_HEADS, HEAD + MLA_ROPE)
    return jnp.concatenate([r[:, :, :HEAD].reshape(Q_LORA, -1), _spread_rope(r[:, :, HEAD:]).reshape(Q_LORA, -1)], axis=1)


def _wuq_unlayout(d):
    n = d[:, :MLA_HEADS * HEAD].reshape(Q_LORA, MLA_HEADS, HEAD)
    p = _gather_rope(d[:, MLA_HEADS * HEAD:].reshape(Q_LORA, MLA_HEADS, HEAD))
    return jnp.concatenate([n, p], axis=-1).reshape(Q_LORA, -1)


def _wukv_layout(w):
    r = w.reshape(KV_LORA, MLA_HEADS, 2 * HEAD)
    return jnp.concatenate([r[:, :, :HEAD].reshape(KV_LORA, -1), r[:, :, HEAD:].reshape(KV_LORA, -1)], axis=1)


def _wukv_unlayout(d):
    k = d[:, :MLA_HEADS * HEAD].reshape(KV_LORA, MLA_HEADS, HEAD)
    v = d[:, MLA_HEADS * HEAD:].reshape(KV_LORA, MLA_HEADS, HEAD)
    return jnp.concatenate([k, v], axis=-1).reshape(KV_LORA, -1)


AG_MID = ['w_uq', 'w_ukv', 'w_mem_kv', 'w_o_gm', 'w_o_mla', 'w_o_mem', 'w_out']
AG_FFN = ['w_ff1', 'w_ff2']
RS_GROUPS = {'ffn_proj': ['w_ff2', 'w_ff1', 'w_out', 'w_o_gm', 'w_o_mla', 'w_o_mem'],
             'lat': ['w_uq', 'w_ukv', 'w_mem_kv'], 'in_top': ['w_in'], 'in_bot': ['w_in']}


def _unride(res, rider):
    return (res, None) if rider is None else res


def _local_step(x, mem, positions, target, P, ws):
    B, S, _ = x.shape
    M = mem.shape[1]
    T = B * S
    x2d = x.reshape(T, D_MODEL)
    mem2d = mem.reshape(B * M, D_MODEL)
    tgt2d = target.reshape(T, D_MODEL)

    def row(v):
        return v.reshape(1, -1).astype(F32)

    inv_freq = ROPE_BASE ** (-jnp.arange(0, MLA_ROPE, 2, dtype=F32) / MLA_ROPE)
    zero = jnp.zeros_like(inv_freq)
    ang = positions.reshape(T).astype(F32)[:, None] * jnp.concatenate([inv_freq, zero, inv_freq, zero])
    cc = jnp.cos(ang) * jnp.concatenate([zero + 1.0, zero, zero + 1.0, zero])
    ss = jnp.sin(ang) * jnp.concatenate([zero - 1.0, zero, zero + 1.0, zero])

    g_mix, g_cq, g_ckv, g_ffn, g_mem = row(P['g_mix']), row(P['g_cq']), row(P['g_ckv']), row(P['g_ffn']), row(P['g_mem'])
    gqn, gkn, gmq, gmk = row(P['g_q_nope']), row(P['g_k_nope']), row(P['g_mq']), row(P['g_mk'])
    gqp, gkp = _spread_rope(row(P['g_q_pe'])), _spread_rope(row(P['g_k_pe']))
    gln, bln = row(P['g_gm_ln']), row(P['b_gm_ln'])
    wc = jnp.tril(P['w_spatial'].astype(F32))
    wct = jnp.swapaxes(wc, 1, 2).astype(BF16)
    wc = wc.astype(BF16)
    bst = jnp.broadcast_to(P['b_spatial'].astype(F32)[:, :, None], (GM_GROUPS, GM_CHUNK, LANES))

    ride = ws.gather(['w_in'])
    h, got = _unride(_rms_fwd(x2d, g_mix, "rms_mix", rider=ride), ride)
    w_in = _win_layout(ws.gathered(['w_in'], got)['w_in']).astype(BF16)
    ride = ws.gather(AG_MID)
    z, got = _unride(_matmul(h, w_in, 'nn', ACT, "mm_in", tn_t=1792, rider=ride), ride)
    mid = ws.gathered(AG_MID, got)
    w_uq, w_ukv = _wuq_layout(mid['w_uq']).astype(BF16), _wukv_layout(mid['w_ukv']).astype(BF16)
    w_mem_kv, w_o_gm, w_o_mla, w_o_mem, w_out = (mid[n] for n in ('w_mem_kv', 'w_o_gm', 'w_o_mla', 'w_o_mem', 'w_out'))
    ygm_pre = _gm_fwd(z, gln, bln, wc, bst, "gm_fwd")
    y_gm = _matmul(ygm_pre, w_o_gm, 'nn', ACT, "mm_o_gm")
    nq, nkv = _lat_fwd(z, g_cq, g_ckv, "lat_fwd")
    q = _matmul(nq, w_uq, 'nn', ACT, "mm_uq")
    kv = _matmul(nkv, w_ukv, 'nn', ACT, "mm_ukv")
    qcat, kcat, vv = _qk_fwd(q, kv, z, cc, ss, gqn, gqp, gkn, gkp, "qk_fwd")
    ride = ws.gather(AG_FFN)
    (o, lse), got = _unride(_attn_fwd(qcat, kcat, vv, B, MLA_HEADS, QCAT, 0, True, "mla_attn_fwd", rider=ride), ride)
    ffn = ws.gathered(AG_FFN, got)
    w_ff1, w_ff2 = ffn['w_ff1'], ffn['w_ff2']
    y_mla = _matmul(o, w_o_mla, 'nn', ACT, "mm_o_mla")
    nm = _rms_fwd(mem2d, g_mem, "rms_mem")
    kvm = _matmul(nm, w_mem_kv, 'nn', ACT, "mm_mem_kv")
    qm = _headnorm_fwd(z, Z_QM // (MEM_HEADS * HEAD), MEM_HEADS, gmq, MEM_SCALE * LOG2E, "memq_fwd")
    km = _headnorm_fwd(kvm, 0, MEM_HEADS, gmk, 1.0, "memk_fwd")
    om, lse_m = _attn_fwd(qm, km, kvm, B, MEM_HEADS, HEAD, MEM_HEADS, False, "mem_attn_fwd")
    y_mem = _matmul(om, w_o_mem, 'nn', ACT, "mm_o_mem")
    merged = _merge_fwd(z, y_gm, y_mla, y_mem, "merge_fwd")
    x1 = _matmul(merged, w_out, 'nn', F32, "mm_out", add=x2d)
    h2 = _rms_fwd(x1, g_ffn, "rms_ffn")
    a1 = _matmul(h2, w_ff1, 'nn', BF16, "mm_ff1")
    dx2, dx2b, loss_part = _matmul(a1, w_ff2, 'nn', F32, "mm_ff2", add=x1, relu2_a=True, sq_err_target=tgt2d)

    G = {}
    d_ff2 = _matmul(a1, dx2b, 'tn', BF16, "mm_d_ff2", relu2_a=True)
    da1 = _matmul(dx2b, w_ff2, 'nt', BF16, "mm_da1", relu2_grad=a1)
    d_ff1 = _matmul(h2, da1, 'tn', BF16, "mm_d_ff1", col_shards=N_DEV)
    dh2 = _matmul(da1, w_ff1, 'nt', ACT, "mm_dh2")
    dx1, dx1b, G['g_ffn'] = _rms_bwd(x1, g_ffn, dh2, dx2, "rms_ffn_bwd", dx_dtypes=(F32, BF16))
    d_out = _matmul(merged, dx1b, 'tn', BF16, "mm_d_out")
    dmerged = _matmul(dx1b, w_out, 'nt', ACT, "mm_dmerged")
    dy_gm, dy_mla, dy_mem, dz = _merge_bwd(z, y_gm, y_mla, y_mem, dmerged, "merge_bwd")
    d_o_gm = _matmul(ygm_pre, dy_gm, 'tn', BF16, "mm_d_o_gm")
    d_o_mla = _matmul(o, dy_mla, 'tn', BF16, "mm_d_o_mla")
    d_o_mem = _matmul(om, dy_mem, 'tn', BF16, "mm_d_o_mem")
    dygm_pre = _matmul(dy_gm, w_o_gm, 'nt', ACT, "mm_dygm")
    dz, dws, dbs, G['g_gm_ln'], G['b_gm_ln'] = _gm_bwd(z, dygm_pre, gln, bln, wc, wct, bst, dz, "gm_bwd")
    G['w_spatial'] = jnp.tril(dws)
    G['b_spatial'] = jnp.sum(dbs.reshape(GM_CHUNK, GM_GROUPS, LANES), axis=-1).T
    do = _matmul(dy_mla, w_o_mla, 'nt', ACT, "mm_do")
    ride = ws.scatter('ffn_proj', {'w_ff2': d_ff2, 'w_ff1': d_ff1, 'w_out': d_out, 'w_o_gm': d_o_gm, 'w_o_mla': d_o_mla,
                                   'w_o_mem': d_o_mem})
    (dqc, dkc, dvv), got = _unride(_attn_bwd(qcat, kcat, vv, o, do, lse, B, MLA_HEADS, QCAT, 0, MLA_SCALE, True,
                                             "mla_attn_bwd", rider=ride), ride)
    ws.scattered('ffn_proj', got)
    dq, dkv, dkpe, G['g_q_nope'], dgqp, G['g_k_nope'], dgkp = _qk_bwd(q, kv, z, cc, ss, gqn, gqp, gkn, gkp, dqc, dkc, dvv,
                                                                     "qk_bwd")
    G['g_q_pe'], G['g_k_pe'] = _gather_rope(dgqp), _gather_rope(dgkp)
    d_uq = _wuq_unlayout(_matmul(nq, dq, 'tn', BF16, "mm_d_uq"))
    dnq = _matmul(dq, w_uq, 'nt', ACT, "mm_dnq")
    d_ukv = _wukv_unlayout(_matmul(nkv, dkv, 'tn', BF16, "mm_d_ukv"))
    dnkv = _matmul(dkv, w_ukv, 'nt', ACT, "mm_dnkv")
    dz, G['g_cq'], G['g_ckv'] = _lat_bwd(z, dnq, dnkv, dkpe, g_cq, g_ckv, dz, "lat_bwd")
    dom = _matmul(dy_mem, w_o_mem, 'nt', ACT, "mm_dom")
    dqm, dkm, dvm = _attn_bwd(qm, km, kvm, om, dom, lse_m, B, MEM_HEADS, HEAD, MEM_HEADS, MEM_SCALE, False, "mem_attn_bwd")
    dz, G['g_mq'] = _headnorm_bwd(z, Z_QM // (MEM_HEADS * HEAD), MEM_HEADS, gmq, dqm, None, "memq_bwd",
                                  into=(dz, Z_QM // (MEM_HEADS * HEAD)))
    dkvm, G['g_mk'] = _headnorm_bwd(kvm, 0, MEM_HEADS, gmk, dkm, dvm, "memk_bwd")
    d_mem_kv = _matmul(nm, dkvm, 'tn', BF16, "mm_d_mem_kv")
    dnm = _matmul(dkvm, w_mem_kv, 'nt', ACT, "mm_dnm")
    G['g_mem'], = _rms_bwd(mem2d, g_mem, dnm, None, "rms_mem_bwd", dx_dtypes=())
    half = D_MODEL // 2
    ride = ws.scatter('lat', {'w_uq': d_uq, 'w_ukv': d_ukv, 'w_mem_kv': d_mem_kv})
    d_top, got = _unride(_matmul(h, dz, 'tn', BF16, "mm_d_in_top", tn_t=1792, m_rows=(0, half), rider=ride), ride)
    ws.scattered('lat', got)
    ride = ws.scatter('in_top', {'w_in': _win_unlayout(d_top)})
    d_bot, got = _unride(_matmul(h, dz, 'tn', BF16, "mm_d_in_bot", tn_t=1792, m_rows=(half, half), rider=ride), ride)
    ws.scattered('in_top', got)
    ride = ws.scatter('in_bot', {'w_in': _win_unlayout(d_bot)})
    dh, got = _unride(_matmul(dz, w_in, 'nt', ACT, "mm_dh", rider=ride), ride)
    ws.scattered('in_bot', got)
    gx, G['g_mix'] = _rms_bwd(x2d, g_mix, dh, dx1, "rms_mix_bwd")
    return loss_part, gx.reshape(B, S, D_MODEL), G


def _all_gather8(xs, name):
    def body(x_ref, out_ref, send_sems, recv_sems, local_sem):
        x, y, c = lax.axis_index("x"), lax.axis_index("y"), lax.axis_index("c")
        me, sibling = (x, y, c), (x, y, 1 - c)
        chips = [(1 - x, y), (x, 1 - y), (1 - x, 1 - y)]

        def rows(px, py, pc):
            return out_ref.at[4 * px + 2 * py + pc]

        def copy(k, block, to, src=None):
            return pltpu.make_async_remote_copy(
                src_ref=rows(*block) if src is None else src, dst_ref=rows(*block),
                send_sem=send_sems.at[k], recv_sem=recv_sems.at[k], device_id=to, device_id_type=MESH)

        mine = pltpu.make_async_copy(x_ref, rows(*me), local_sem)
        mine.start()
        first = [copy(0, me, sibling, src=x_ref)]
        first += [copy(1 + j, me, (*chip, c), src=x_ref) for j, chip in enumerate(chips)]
        for cp in first:
            cp.start()
        passed = [copy(4 + j, (*chip, c), sibling) for j, chip in enumerate(chips)]
        for j, chip in enumerate(chips):
            copy(1 + j, (*chip, c), me).wait_recv()
            passed[j].start()
        copy(0, sibling, me).wait_recv()
        for j, chip in enumerate(chips):
            copy(4 + j, (*chip, 1 - c), me).wait_recv()
        for cp in first + passed:
            cp.wait_send()
        mine.wait()

    return pl.pallas_call(
        body, name=name, in_specs=[HBM_SPEC], out_specs=HBM_SPEC,
        out_shape=jax.ShapeDtypeStruct((N_DEV,) + xs.shape, xs.dtype),
        scratch_shapes=[pltpu.SemaphoreType.DMA((7,)), pltpu.SemaphoreType.DMA((7,)), pltpu.SemaphoreType.DMA],
    )(xs)


def _adamw_rows(w, g, m, v):
    m2 = ADAM_B1 * m + (1.0 - ADAM_B1) * g
    v2 = ADAM_B2 * v + (1.0 - ADAM_B2) * (g * g)
    m_hat = m2 / (1.0 - ADAM_B1 ** ADAM_STEP)
    v_hat = v2 / (1.0 - ADAM_B2 ** ADAM_STEP)
    delta = -ADAM_LR * (m_hat / (jnp.sqrt(v_hat) + ADAM_EPS) + ADAM_WD * w)
    return delta, m2, v2


def _sum_adamw(parts, w, m, v, name):
    rows, cols = w.shape
    assert sum(p.shape[1] for p in parts) == rows
    tr = _pick(min(p.shape[1] for p in parts), max(16, 65536 // cols), 16)
    n = parts[0].shape[0]
    counts = [p.shape[1] // tr for p in parts]
    starts = [sum(counts[:k]) for k in range(len(parts))]

    def body(*refs):
        p_refs = refs[:len(parts)]
        w_ref, m_ref, v_ref, g_ref, d_ref, m2_ref, v2_ref = refs[len(parts):]
        g = None
        for p_ref, start in zip(p_refs, starts):
            gk = p_ref[0].astype(F32)
            for k in range(1, n):
                gk = gk + p_ref[k].astype(F32)
            g = gk if g is None else jnp.where(pl.program_id(0) >= start, gk, g)
        delta, m2, v2 = _adamw_rows(w_ref[...], g, m_ref[...], v_ref[...])
        g_ref[...] = g
        d_ref[...] = delta
        m2_ref[...] = m2
        v2_ref[...] = v2

    flat = pl.BlockSpec((tr, cols), lambda i: (i, 0))
    out = jax.ShapeDtypeStruct((rows, cols), F32)
    p_specs = [pl.BlockSpec((n, tr, cols), lambda i, s=s, c=c: (0, jnp.clip(i - s, 0, c - 1), 0))
               for s, c in zip(starts, counts)]
    return pl.pallas_call(
        body, name=name, grid=(rows // tr,), in_specs=p_specs + [flat, flat, flat], out_specs=[flat] * 4,
        out_shape=[out] * 4, compiler_params=_params(("parallel",)),
    )(*parts, w, m, v)


SMALL_WIDTH = {'g_mix': 1024, 'g_cq': 384, 'g_ckv': 256, 'g_q_nope': 128, 'g_q_pe': 128, 'g_k_nope': 128, 'g_k_pe': 128,
               'g_gm_ln': 512, 'b_gm_ln': 512, 'g_mem': 1024, 'g_mq': 128, 'g_mk': 128, 'g_ffn': 1024}
NARROW = ('g_q_pe', 'g_k_pe')


def _small_layout():
    layout, r = {}, 0
    for name in SMALL + ['loss']:
        rows = {'w_spatial': GM_GROUPS * GM_CHUNK, 'b_spatial': GM_GROUPS, 'loss': 1}.get(name) or SMALL_WIDTH[name] // LANES
        layout[name] = (r, rows)
        r += -(-rows // 8) * 8
    return layout, r


def _small_pack(grads, loss_part, name):
    layout, total = _small_layout()
    names = SMALL + ['loss']

    def body(*refs):
        out_ref = refs[-1]
        out_ref[...] = jnp.zeros((total, LANES), F32)
        for ref, n in zip(refs[:-1], names):
            r0, rows = layout[n]
            if n == 'w_spatial':
                for g in range(GM_GROUPS):
                    out_ref[r0 + g * GM_CHUNK:r0 + (g + 1) * GM_CHUNK, :] = ref[g]
            elif n == 'b_spatial':
                out_ref[r0:r0 + rows, :] = ref[...]
            else:
                for k in range(rows):
                    out_ref[r0 + k:r0 + k + 1, :] = ref[:, k * LANES:(k + 1) * LANES]

    return pl.pallas_call(body, name=name, out_shape=jax.ShapeDtypeStruct((total, LANES), F32))(
        *[grads[n] for n in SMALL], loss_part)


def _small_adamw(parts, w, m, v, name):
    layout, _ = _small_layout()
    n_dev = parts.shape[0]

    def body(*refs):
        p_ref = refs[0]
        ins = refs[1:1 + 3 * len(SMALL)]
        outs = refs[1 + 3 * len(SMALL):-1]

        def gsum(r0, rows):
            g = p_ref[0, r0:r0 + rows, :]
            for d in range(1, n_dev):
                g = g + p_ref[d, r0:r0 + rows, :]
            return g

        def step(idx, g, at):
            w_ref, m_ref, v_ref = ins[3 * idx:3 * idx + 3]
            delta, m2, v2 = _adamw_rows(w_ref[at], g, m_ref[at], v_ref[at])
            for ref, val in zip(outs[4 * idx:4 * idx + 4], (g, delta, m2, v2)):
                ref[at] = val

        for idx, n in enumerate(SMALL):
            r0, rows = layout[n]
            if n == 'w_spatial':
                for g in range(GM_GROUPS):
                    step(idx, gsum(r0 + g * GM_CHUNK, GM_CHUNK), (0, g))
            elif n == 'b_spatial':
                step(idx, gsum(r0, rows), (0,))
            else:
                for k in range(rows):
                    step(idx, gsum(r0 + k, 1), (slice(None), slice(k * LANES, (k + 1) * LANES)))
        refs[-1][...] = gsum(layout['loss'][0], 8)

    flat_in = [d[n] for n in SMALL for d in (w, m, v)]
    out_shape = [jax.ShapeDtypeStruct(w[n].shape, F32) for n in SMALL for _ in range(4)]
    res = pl.pallas_call(body, name=name, out_shape=out_shape + [jax.ShapeDtypeStruct((8, LANES), F32)])(parts, *flat_in)
    groups = [{n: res[4 * i + j] for i, n in enumerate(SMALL)} for j in range(4)]
    return groups, res[-1]


def _full_from_gathered(gathered, name):
    r, c = BIG_SHAPE[name]
    if BIG_AXIS[name] == 0:
        return gathered.reshape(r, c)
    return gathered.transpose(1, 0, 2).reshape(r, c)


def _shards_of_full(g, name):
    if g.ndim == 3:
        return g
    r, c = BIG_SHAPE[name]
    if BIG_AXIS[name] == 0:
        return g.reshape(N_DEV, r // N_DEV, c)
    return g.reshape(g.shape[0], N_DEV, c // N_DEV).transpose(1, 0, 2)


class _DistWeights:
    def __init__(self, shards):
        self.shards = shards
        self.received = {}

    def gather(self, names):
        return _Gather2([self.shards[n].astype(BF16) for n in names])

    def gathered(self, names, got):
        return {n: _full_from_gathered(g, n) for n, g in zip(names, got)}

    def scatter(self, key, grads):
        return _Exchange([_shards_of_full(grads[n], n) for n in RS_GROUPS[key]], scatter=True)

    def scattered(self, key, got):
        for n, g in zip(RS_GROUPS[key], got):
            self.received.setdefault(n, []).append(g)


def kernel(x, mem, positions, g_mix, w_in, g_cq, w_uq, g_ckv, w_ukv, g_q_nope, g_q_pe, g_k_nope, g_k_pe, g_gm_ln, b_gm_ln, w_spatial, b_spatial, g_mem, w_mem_kv, g_mq, g_mk, w_o_gm, w_o_mla, w_o_mem, w_out, g_ffn, w_ff1, w_ff2, loss_target, m_g_mix, m_w_in, m_g_cq, m_w_uq, m_g_ckv, m_w_ukv, m_g_q_nope, m_g_q_pe, m_g_k_nope, m_g_k_pe, m_g_gm_ln, m_b_gm_ln, m_w_spatial, m_b_spatial, m_g_mem, m_w_mem_kv, m_g_mq, m_g_mk, m_w_o_gm, m_w_o_mla, m_w_o_mem, m_w_out, m_g_ffn, m_w_ff1, m_w_ff2, v_g_mix, v_w_in, v_g_cq, v_w_uq, v_g_ckv, v_w_ukv, v_g_q_nope, v_g_q_pe, v_g_k_nope, v_g_k_pe, v_g_gm_ln, v_b_gm_ln, v_w_spatial, v_b_spatial, v_g_mem, v_w_mem_kv, v_g_mq, v_g_mk, v_w_o_gm, v_w_o_mla, v_w_o_mem, v_w_out, v_g_ffn, v_w_ff1, v_w_ff2):
    given = dict(locals())
    w = {n: given[n][0] for n in WEIGHTS}
    mom = {n: given['m_' + n][0] for n in WEIGHTS}
    var = {n: given['v_' + n][0] for n in WEIGHTS}

    ws = _DistWeights({n: w[n] for n in BIG})
    loss_part, grad_x, G = _local_step(x, mem, positions, loss_target, {n: w[n] for n in SMALL}, ws)

    outs = {}
    for n in BIG:
        for prefix, res in zip(("grad_", "delta_", "new_m_", "new_v_"),
                               _sum_adamw(ws.received[n], w[n], mom[n], var[n], "adamw_" + n)):
            outs[prefix + n] = res[None]

    def widen(d):
        return {n: (jnp.pad(d[n], ((0, 0), (0, LANES - MLA_ROPE))) if n in NARROW else d[n]) for n in SMALL}

    parts = _all_gather8(_small_pack(widen(G), loss_part, "small_pack"), "ag_small")
    small, loss_rows = _small_adamw(parts, *[widen({n: given[prefix + n] for n in SMALL}) for prefix in ("", "m_", "v_")],
                                    "adamw_small")
    loss = 0.5 * jnp.sum(loss_rows) / D_MODEL
    for prefix, group in zip(("grad_", "delta_", "new_m_", "new_v_"), small):
        for n in SMALL:
            outs[prefix + n] = group[n][:, :MLA_ROPE] if n in NARROW else group[n]
    return (loss, grad_x, *[outs[p + n] for p in ("grad_", "delta_", "new_m_", "new_v_") for n in WEIGHTS])
```

```python
import functools
import math

import jax
import jax.numpy as jnp
from jax import lax
from jax.experimental import pallas as pl
from jax.experimental.pallas import tpu as pltpu

F32 = jnp.float32
BF16 = jnp.bfloat16
ACT = BF16

D_MODEL = 1024
MEM_HEADS = 4
HEAD = 128
GM_WIDTH = 512
GM_CHUNK = 128
GM_GROUPS = 4
MLA_HEADS = 8
MLA_ROPE = 64
Q_LORA = 384
KV_LORA = 256
D_FF = 4096
EPS = 1e-6
ROPE_BASE = 10000.0
MLA_SCALE = 1.0 / math.sqrt(HEAD + MLA_ROPE)
MEM_SCALE = 1.0 / math.sqrt(HEAD)
LOG2E = 1.4426950408889634
LN2 = 0.6931471805599453
ATT_TILE = 256
C_ZU, C_ZV, C_CQ, C_CKV, C_KPE, C_QM, C_ZG, C_END = 0, 512, 1024, 1408, 1664, 1728, 2240, 5312
Z_GM, Z_QM, Z_MLA, Z_KPE, Z_COLS = 3072, 4096, 4608, 5248, 5376
MLA_W = 768
QCAT = 2 * HEAD
ADAM_LR, ADAM_B1, ADAM_B2, ADAM_EPS, ADAM_WD, ADAM_STEP = 0.001, 0.9, 0.999, 1e-08, 0.01, 10
N_DEV = 8
LANES = 128
VMEM_LIMIT = 48 * 1024 * 1024
MAX_K_TILE = 8192
NEG = -1e30

BIG = ['w_in', 'w_uq', 'w_ukv', 'w_mem_kv', 'w_o_gm', 'w_o_mla', 'w_o_mem', 'w_out', 'w_ff1', 'w_ff2']
BIG_AXIS = {'w_in': 1, 'w_uq': 1, 'w_ukv': 1, 'w_mem_kv': 0, 'w_o_gm': 1, 'w_o_mla': 0, 'w_o_mem': 1,
            'w_out': 0, 'w_ff1': 1, 'w_ff2': 0}
BIG_SHAPE = {'w_in': (1024, 5312), 'w_uq': (384, 1536), 'w_ukv': (256, 2048), 'w_mem_kv': (1024, 1024),
             'w_o_gm': (512, 1024), 'w_o_mla': (1024, 1024), 'w_o_mem': (512, 1024), 'w_out': (1024, 1024),
             'w_ff1': (1024, 4096), 'w_ff2': (4096, 1024)}
SMALL = ['g_mix', 'g_cq', 'g_ckv', 'g_q_nope', 'g_q_pe', 'g_k_nope', 'g_k_pe', 'g_gm_ln', 'b_gm_ln',
         'w_spatial', 'b_spatial', 'g_mem', 'g_mq', 'g_mk', 'g_ffn']
WEIGHTS = ['g_mix', 'w_in', 'g_cq', 'w_uq', 'g_ckv', 'w_ukv', 'g_q_nope', 'g_q_pe', 'g_k_nope', 'g_k_pe',
           'g_gm_ln', 'b_gm_ln', 'w_spatial', 'b_spatial', 'g_mem', 'w_mem_kv', 'g_mq', 'g_mk', 'w_o_gm',
           'w_o_mla', 'w_o_mem', 'w_out', 'g_ffn', 'w_ff1', 'w_ff2']


def _pick(n, target, mult=LANES):
    best = None
    t = mult
    while t <= min(n, target):
        if n % t == 0:
            best = t
        t += mult
    return best if best is not None else n


def _params(sem):
    return pltpu.CompilerParams(dimension_semantics=sem, vmem_limit_bytes=VMEM_LIMIT)


MESH = pl.DeviceIdType.MESH
HBM_SPEC = pl.BlockSpec(memory_space=pltpu.HBM)


class _Exchange:
    def __init__(self, srcs, scatter):
        self.srcs, self.scatter = list(srcs), scatter
        self.out_shapes = [jax.ShapeDtypeStruct(s.shape if scatter else (N_DEV,) + s.shape, s.dtype) for s in self.srcs]
        n = len(self.srcs)
        self.scratch = [pltpu.SemaphoreType.DMA((n, N_DEV - 1)), pltpu.SemaphoreType.DMA((n, N_DEV - 1)),
                        pltpu.SemaphoreType.DMA((n,))]

    def _copies(self, src_refs, dst_refs, send_sems, recv_sems, local_sems):
        x, y, c = lax.axis_index("x"), lax.axis_index("y"), lax.axis_index("c")
        me = 4 * x + 2 * y + c
        local, remote = [], []
        for a, (src_ref, dst_ref) in enumerate(zip(src_refs, dst_refs)):
            def mine_for(dev, src_ref=src_ref):
                return src_ref.at[dev] if self.scatter else src_ref

            local.append(pltpu.make_async_copy(mine_for(me), dst_ref.at[me], local_sems.at[a]))
            for k in range(1, N_DEV):
                px = 1 - x if k & 4 else x
                py = 1 - y if k & 2 else y
                pc = 1 - c if k & 1 else c
                remote.append(pltpu.make_async_remote_copy(
                    src_ref=mine_for(4 * px + 2 * py + pc), dst_ref=dst_ref.at[me], send_sem=send_sems.at[a, k - 1],
                    recv_sem=recv_sems.at[a, k - 1], device_id=(px, py, pc), device_id_type=MESH))
        return local, remote

    def start(self, *refs):
        local, remote = self._copies(*refs)
        for cp in local + remote:
            cp.start()

    def forward(self, *refs):
        pass

    def finish(self, *refs):
        local, remote = self._copies(*refs)
        for cp in remote + local:
            cp.wait()


class _Gather2:
    def __init__(self, srcs):
        self.srcs = list(srcs)
        self.out_shapes = [jax.ShapeDtypeStruct((N_DEV,) + s.shape, s.dtype) for s in self.srcs]
        n = len(self.srcs)
        self.scratch = [pltpu.SemaphoreType.DMA((n, N_DEV - 1)), pltpu.SemaphoreType.DMA((n, N_DEV - 1)),
                        pltpu.SemaphoreType.DMA((n,))]

    def _plan(self, src_refs, dst_refs, send_sems, recv_sems, local_sems):
        x, y, c = lax.axis_index("x"), lax.axis_index("y"), lax.axis_index("c")
        chips = [(1 - x, y), (x, 1 - y), (1 - x, 1 - y)]
        plans = []
        for a, (src_ref, dst_ref) in enumerate(zip(src_refs, dst_refs)):
            def copy(k, block, to, src=None, a=a, dst_ref=dst_ref):
                at = dst_ref.at[4 * block[0] + 2 * block[1] + block[2]]
                return pltpu.make_async_remote_copy(src_ref=at if src is None else src, dst_ref=at,
                                                    send_sem=send_sems.at[a, k], recv_sem=recv_sems.at[a, k],
                                                    device_id=to, device_id_type=MESH)

            local = pltpu.make_async_copy(src_ref, dst_ref.at[4 * x + 2 * y + c], local_sems.at[a])
            first = [copy(0, (x, y, c), (x, y, 1 - c), src=src_ref)]
            first += [copy(1 + j, (x, y, c), (*chip, c), src=src_ref) for j, chip in enumerate(chips)]
            passed = [copy(4 + j, (*chip, c), (x, y, 1 - c)) for j, chip in enumerate(chips)]
            arrivals = [copy(1 + j, (*chip, c), (x, y, c)) for j, chip in enumerate(chips)]
            late = [copy(0, (x, y, 1 - c), (x, y, c))] + [copy(4 + j, (*chip, 1 - c), (x, y, c)) for j, chip in enumerate(chips)]
            plans.append((local, first, passed, arrivals, late))
        return plans

    def start(self, *refs):
        for local, first, _, _, _ in self._plan(*refs):
            local.start()
            for cp in first:
                cp.start()

    def forward(self, *refs):
        for _, _, passed, arrivals, _ in self._plan(*refs):
            for arrived, onward in zip(arrivals, passed):
                arrived.wait_recv()
                onward.start()

    def finish(self, *refs):
        for local, first, passed, _, late in self._plan(*refs):
            for cp in late:
                cp.wait_recv()
            for cp in first + passed:
                cp.wait_send()
            local.wait()


def _call(body, rider, ins, *, name, grid, in_specs, out_specs, out_shape, scratch_shapes, sem):
    if rider is None:
        return pl.pallas_call(body, name=name, grid=grid, in_specs=in_specs, out_specs=out_specs, out_shape=out_shape,
                              scratch_shapes=scratch_shapes, compiler_params=_params(sem))(*ins)
    single = not isinstance(out_shape, (list, tuple))
    own_specs, own_shapes = ([out_specs], [out_shape]) if single else (list(out_specs), list(out_shape))
    n_in, n_out, n_sc, n_r = len(ins), len(own_shapes), len(scratch_shapes), len(rider.srcs)
    n_all_in = n_in + n_r

    def carrying(*refs):
        own_in, srcs = refs[:n_in], refs[n_in:n_in + n_r]
        own_out, dsts = refs[n_all_in:n_all_in + n_out], refs[n_all_in + n_out:n_all_in + n_out + n_r]
        own_sc = refs[n_all_in + n_out + n_r:n_all_in + n_out + n_r + n_sc]
        sems = refs[n_all_in + n_out + n_r + n_sc:]
        first = last = late = None
        for d, steps in enumerate(grid):
            f, l = pl.program_id(d) == 0, pl.program_id(d) == steps - 1
            t = pl.program_id(d) == ((3 * steps) // 4 if d == 0 else 0)
            first, last, late = (f, l, t) if first is None else (first & f, last & l, late & t)

        @pl.when(first)
        def _():
            rider.start(srcs, dsts, *sems)

        @pl.when(late)
        def _():
            rider.forward(srcs, dsts, *sems)

        body(*own_in, *own_out, *own_sc)

        @pl.when(last)
        def _():
            rider.finish(srcs, dsts, *sems)

    res = pl.pallas_call(
        carrying, name=name, grid=grid, in_specs=list(in_specs) + [HBM_SPEC] * n_r,
        out_specs=own_specs + [HBM_SPEC] * n_r, out_shape=own_shapes + rider.out_shapes,
        scratch_shapes=list(scratch_shapes) + rider.scratch, compiler_params=_params(("arbitrary",) * len(grid)),
    )(*ins, *rider.srcs)
    own = res[:n_out]
    return (own[0] if single else list(own)), list(res[n_out:])


def _matmul(a, b, mode, out_dtype, name, add=None, relu2_a=False, relu2_grad=None,
            tm_t=None, tn_t=None, tk_t=None, rider=None, m_rows=None, col_shards=None, sq_err_target=None):
    if mode == 'nn':
        (M, K), (K2, N) = a.shape, b.shape
    elif mode == 'nt':
        (M, K), (N, K2) = a.shape, b.shape
    else:
        (K, M), (K2, N) = a.shape, b.shape
    assert K == K2, (name, a.shape, b.shape)
    m_first = 0
    if m_rows is not None:
        assert mode == 'tn'
        m_first, M = m_rows
    if col_shards is not None:
        assert add is None and relu2_grad is None and sq_err_target is None and tn_t is None
    if mode == 'tn':
        d_tm, d_tn, d_tk = 1024, (2048 if M <= 512 else 1024), 2048
    else:
        wide = add is None and sq_err_target is None and jnp.dtype(out_dtype).itemsize == 2 and K <= D_FF
        d_tm, d_tn, d_tk = (2048 if K <= 1024 else 1024), (1024 if wide else 512), MAX_K_TILE
    tm, tn, tk = _pick(M, tm_t or d_tm), _pick(N, tn_t or d_tn), _pick(K, tk_t or d_tk)
    gm, gn, nk = M // tm, N // tn, K // tk
    if mode == 'nn':
        a_spec = pl.BlockSpec((tm, tk), lambda i, j, k: (i, k))
        b_spec = pl.BlockSpec((tk, tn), lambda i, j, k: (k, j))
        dims = (((1,), (0,)), ((), ()))
    elif mode == 'nt':
        a_spec = pl.BlockSpec((tm, tk), lambda i, j, k: (i, k))
        b_spec = pl.BlockSpec((tn, tk), lambda i, j, k: (j, k))
        dims = (((1,), (1,)), ((), ()))
    else:
        assert m_first % tm == 0
        a_spec = pl.BlockSpec((tk, tm), lambda i, j, k: (k, m_first // tm + i))
        b_spec = pl.BlockSpec((tk, tn), lambda i, j, k: (k, j))
        dims = (((0,), (0,)), ((), ()))
    o_spec = pl.BlockSpec((tm, tn), lambda i, j, k: (i, j))
    shard_w = N // col_shards if col_shards is not None else tn
    assert tn % shard_w == 0
    has_add, has_e, has_t = add is not None, relu2_grad is not None, sq_err_target is not None
    assert not has_t or (nk == 1 and tn % LANES == 0)

    def body(*refs):
        a_ref, b_ref = refs[0], refs[1]
        pos = 2
        add_ref = e_ref = t_ref = None
        if has_add:
            add_ref = refs[pos]
            pos += 1
        if has_e:
            e_ref = refs[pos]
            pos += 1
        if has_t:
            t_ref = refs[pos]
            pos += 1
        o_ref = refs[pos]
        acc_ref = refs[pos + 1] if nk > 1 else None

        av = a_ref[...]
        if relu2_a:
            av = jnp.maximum(av, 0)
            av = av * av
        prod = lax.dot_general(av.astype(BF16), b_ref[...].astype(BF16), dims, preferred_element_type=F32)

        def finish(r):
            if has_add:
                r = r + add_ref[...]
            if has_e:
                r = r * (2.0 * jnp.maximum(e_ref[...].astype(F32), 0.0))
            if has_t:
                err = r - t_ref[...]
                r = err * (1.0 / N)
                refs[pos + 1][...] = r.astype(BF16)
                sq = err * err
                part = sq[:, 0:LANES]
                for c in range(1, tn // LANES):
                    part = part + sq[:, c * LANES:(c + 1) * LANES]
                _acc_rows(refs[pos + 2], part, (pl.program_id(0) == 0) & (pl.program_id(1) == 0))
            if col_shards is not None:
                for s in range(tn // shard_w):
                    o_ref[s] = r[:, s * shard_w:(s + 1) * shard_w].astype(out_dtype)
            else:
                o_ref[...] = r.astype(out_dtype)

        if nk == 1:
            finish(prod)
        else:
            k = pl.program_id(2)

            @pl.when(k == 0)
            def _():
                acc_ref[...] = prod

            @pl.when(k > 0)
            def _():
                acc_ref[...] += prod

            @pl.when(k == nk - 1)
            def _():
                finish(acc_ref[...])

    ins, specs = [a, b], [a_spec, b_spec]
    if has_add:
        ins.append(add)
        specs.append(o_spec)
    if has_e:
        ins.append(relu2_grad)
        specs.append(o_spec)
    out_specs, out_shape, sem = o_spec, jax.ShapeDtypeStruct((M, N), out_dtype), ("parallel", "parallel", "arbitrary")
    if has_t:
        ins.append(sq_err_target)
        specs.append(o_spec)
        out_specs = [o_spec, o_spec, pl.BlockSpec((1, LANES), lambda i, j, k: (0, 0))]
        out_shape = [out_shape, jax.ShapeDtypeStruct((M, N), BF16), jax.ShapeDtypeStruct((1, LANES), F32)]
        sem = ("arbitrary", "arbitrary", "arbitrary")
    if col_shards is not None:
        out_specs = pl.BlockSpec((tn // shard_w, tm, shard_w), lambda i, j, k: (j, i, 0))
        out_shape = jax.ShapeDtypeStruct((col_shards, M, shard_w), out_dtype)
    return _call(body, rider, ins, name=name, grid=(gm, gn, nk), in_specs=specs, out_specs=out_specs, out_shape=out_shape,
                 scratch_shapes=[pltpu.VMEM((tm, tn), F32)] if nk > 1 else [], sem=sem)


ROW_BLOCK_BYTES = 12 * 1024 * 1024


def _row_tile(rows, row_bytes):
    return _pick(rows, max(16, min(1024, ROW_BLOCK_BYTES // row_bytes)), 16)


def _rowspec(tr, width, col=0):
    return pl.BlockSpec((tr, width), lambda i, col=col: (i, col))


def _fullspec(shape):
    nd = len(shape)
    return pl.BlockSpec(shape, lambda i, nd=nd: (0,) * nd)


def _rms(x, width):
    x = x.astype(F32)
    return lax.rsqrt(jnp.sum(x * x, axis=-1, keepdims=True) * (1.0 / width) + EPS)


def _rms_bwd_rows(x, g, dy, width):
    x, dy = x.astype(F32), dy.astype(F32)
    r = _rms(x, width)
    xh = x * r
    dn = dy * g
    dx = r * (dn - xh * (jnp.sum(dn * xh, axis=-1, keepdims=True) * (1.0 / width)))
    return dx, dy * xh


def _acc_rows(ref, val, first):
    s = jnp.sum(val, axis=0, keepdims=True)

    @pl.when(first)
    def _():
        ref[...] = s

    @pl.when(jnp.logical_not(first))
    def _():
        ref[...] += s


def _rms_fwd(x, g, name, rider=None):
    rows, width = x.shape
    tr = _row_tile(rows, 6 * width)

    def body(x_ref, g_ref, o_ref):
        xv = x_ref[...]
        o_ref[...] = (xv * _rms(xv, width) * g_ref[...]).astype(BF16)

    return _call(body, rider, [x, g], name=name, grid=(rows // tr,),
                 in_specs=[_rowspec(tr, width), _fullspec((1, width))], out_specs=_rowspec(tr, width),
                 out_shape=jax.ShapeDtypeStruct((rows, width), BF16), scratch_shapes=[], sem=("parallel",))


def _rms_bwd(x, g, dy, res, name, dx_dtypes=(F32,)):
    rows, width = x.shape
    tr = _row_tile(rows, 18 * width)
    has_res = res is not None
    n_in = 4 if has_res else 3

    def body(*refs):
        x_ref, g_ref, dy_ref = refs[:3]
        dx, dgv = _rms_bwd_rows(x_ref[...], g_ref[...], dy_ref[...], width)
        if has_res:
            dx = dx + refs[3][...]
        for ref, dt in zip(refs[n_in:], dx_dtypes):
            ref[...] = dx.astype(dt)
        _acc_rows(refs[-1], dgv, pl.program_id(0) == 0)

    ins = [x, g, dy] + ([res] if has_res else [])
    specs = [_rowspec(tr, width), _fullspec((1, width)), _rowspec(tr, width)] + ([_rowspec(tr, width)] if has_res else [])
    return pl.pallas_call(
        body, name=name, grid=(rows // tr,), in_specs=specs,
        out_specs=[_rowspec(tr, width)] * len(dx_dtypes) + [_fullspec((1, width))],
        out_shape=[jax.ShapeDtypeStruct((rows, width), dt) for dt in dx_dtypes] + [jax.ShapeDtypeStruct((1, width), F32)],
        compiler_params=_params(("arbitrary",)),
    )(*ins)


_GELU_C = math.sqrt(2.0 / math.pi)


def _gelu(x):
    t = jnp.tanh(_GELU_C * (x + 0.044715 * (x * x * x)))
    return 0.5 * x * (1.0 + t), t


def _gelu_grad(x, t):
    return 0.5 * (1.0 + t) + 0.5 * x * (1.0 - t * t) * (_GELU_C * (1.0 + 3.0 * 0.044715 * (x * x)))


def _gm_forward_rows(zu, zv, gln, bln, wc_ref, bst, n_chunk):
    u, tu = _gelu(zu)
    a, ta = _gelu(zv)
    mu = jnp.mean(a, axis=-1, keepdims=True)
    ac = a - mu
    rs = lax.rsqrt(jnp.mean(ac * ac, axis=-1, keepdims=True) + EPS)
    n = ac * rs
    v = n * gln + bln
    vb = v.astype(BF16)
    rows = []
    for c in range(n_chunk):
        cols = []
        for g in range(GM_GROUPS):
            vc = vb[c * GM_CHUNK:(c + 1) * GM_CHUNK, g * LANES:(g + 1) * LANES]
            mixed = jnp.dot(wc_ref[g], vc, preferred_element_type=F32) + bst[g]
            cols.append(mixed)
        rows.append(jnp.concatenate(cols, axis=1))
    mixed = jnp.concatenate(rows, axis=0) if n_chunk > 1 else rows[0]
    return u, tu, ta, n, rs, v, mixed


def _gm_fwd(z, gln, bln, wc, bst, name):
    rows = z.shape[0]
    tr = _pick(rows, 512, GM_CHUNK)
    n_chunk = tr // GM_CHUNK

    def body(zu_ref, zv_ref, gln_ref, bln_ref, wc_ref, bst_ref, o_ref):
        u, _, _, _, _, _, mixed = _gm_forward_rows(zu_ref[...].astype(F32), zv_ref[...].astype(F32), gln_ref[...], bln_ref[...], wc_ref,
                                                   bst_ref, n_chunk)
        o_ref[...] = (u * mixed).astype(BF16)

    return pl.pallas_call(
        body, name=name, grid=(rows // tr,),
        in_specs=[_rowspec(tr, GM_WIDTH, Z_GM // GM_WIDTH), _rowspec(tr, GM_WIDTH, Z_GM // GM_WIDTH + 1),_fullspec((1, GM_WIDTH)), _fullspec((1, GM_WIDTH)),
                  _fullspec((GM_GROUPS, GM_CHUNK, GM_CHUNK)), _fullspec((GM_GROUPS, GM_CHUNK, LANES))],
        out_specs=_rowspec(tr, GM_WIDTH), out_shape=jax.ShapeDtypeStruct((rows, GM_WIDTH), BF16),
        compiler_params=_params(("parallel",)),
    )(z, z, gln, bln, wc, bst)


ANY_SPEC = pl.BlockSpec(memory_space=pl.ANY)


def _gm_bwd(z, dy, gln, bln, wc, wct, bst, dz, name):
    rows = z.shape[0]
    tr = _pick(rows, 512, GM_CHUNK)
    n_chunk = tr // GM_CHUNK

    def body(zu_ref, zv_ref, dy_ref, gln_ref, bln_ref, wc_ref, wct_ref, bst_ref, _, dz_ref, dws_ref, dbs_ref, dgl_ref,
             dbl_ref):
        first = pl.program_id(0) == 0
        zu, zv, gln = zu_ref[...].astype(F32), zv_ref[...].astype(F32), gln_ref[...]
        u, tu, ta, n, rs, v, mixed = _gm_forward_rows(zu, zv, gln, bln_ref[...], wc_ref, bst_ref, n_chunk)
        dyv = dy_ref[...].astype(F32)
        dzu = dyv * mixed * _gelu_grad(zu, tu)
        dmix = dyv * u
        dmb = dmix.astype(BF16)
        vb = v.astype(BF16)
        dv_rows, dws, dbs = [], [None] * GM_GROUPS, None
        for c in range(n_chunk):
            rsl = slice(c * GM_CHUNK, (c + 1) * GM_CHUNK)
            cols = []
            for g in range(GM_GROUPS):
                csl = slice(g * LANES, (g + 1) * LANES)
                dmc = dmb[rsl, csl]
                cols.append(jnp.dot(wct_ref[g], dmc, preferred_element_type=F32))
                w_part = lax.dot_general(dmc, vb[rsl, csl], (((1,), (1,)), ((), ())), preferred_element_type=F32)
                dws[g] = w_part if dws[g] is None else dws[g] + w_part
            dv_rows.append(jnp.concatenate(cols, axis=1))
            dbs = dmix[rsl, :] if dbs is None else dbs + dmix[rsl, :]
        dv = jnp.concatenate(dv_rows, axis=0) if n_chunk > 1 else dv_rows[0]
        dn = dv * gln
        da = rs * (dn - jnp.mean(dn, axis=-1, keepdims=True) - n * jnp.mean(dn * n, axis=-1, keepdims=True))
        dzv = da * _gelu_grad(zv, ta)
        dz_ref[:, 0:GM_WIDTH] = dzu.astype(BF16)
        dz_ref[:, GM_WIDTH:2 * GM_WIDTH] = dzv.astype(BF16)
        _acc_rows(dgl_ref, dv * n, first)
        _acc_rows(dbl_ref, dv, first)

        @pl.when(first)
        def _():
            for g in range(GM_GROUPS):
                dws_ref[g] = dws[g]
            dbs_ref[...] = dbs

        @pl.when(jnp.logical_not(first))
        def _():
            for g in range(GM_GROUPS):
                dws_ref[g] += dws[g]
            dbs_ref[...] += dbs

    wspec = _fullspec((GM_GROUPS, GM_CHUNK, GM_CHUNK))
    return pl.pallas_call(
        body, name=name, grid=(rows // tr,),
        in_specs=[_rowspec(tr, GM_WIDTH, Z_GM // GM_WIDTH), _rowspec(tr, GM_WIDTH, Z_GM // GM_WIDTH + 1),
                  _rowspec(tr, GM_WIDTH), _fullspec((1, GM_WIDTH)), _fullspec((1, GM_WIDTH)), wspec, wspec, wspec, ANY_SPEC],
        out_specs=[_rowspec(tr, 2 * GM_WIDTH, Z_GM // (2 * GM_WIDTH)), wspec, _fullspec((GM_CHUNK, GM_WIDTH)),
                   _fullspec((1, GM_WIDTH)), _fullspec((1, GM_WIDTH))],
        out_shape=[jax.ShapeDtypeStruct(dz.shape, dz.dtype), jax.ShapeDtypeStruct((GM_GROUPS, GM_CHUNK, GM_CHUNK), F32),
                   jax.ShapeDtypeStruct((GM_CHUNK, GM_WIDTH), F32), jax.ShapeDtypeStruct((1, GM_WIDTH), F32),
                   jax.ShapeDtypeStruct((1, GM_WIDTH), F32)],
        input_output_aliases={8: 0}, compiler_params=_params(("arbitrary",)),
    )(z, z, dy, gln, bln, wc, wct, bst, dz)


def _lat_fwd(z, g_cq, g_ckv, name):
    rows = z.shape[0]
    tr = _row_tile(rows, 4 * MLA_W)

    def body(z_ref, gq_ref, gkv_ref, nq_ref, nkv_ref):
        zb = z_ref[...]
        cq, ckv = zb[:, 0:Q_LORA], zb[:, Q_LORA:Q_LORA + KV_LORA]
        nq_ref[...] = (cq * _rms(cq, Q_LORA) * gq_ref[...]).astype(BF16)
        nkv_ref[...] = (ckv * _rms(ckv, KV_LORA) * gkv_ref[...]).astype(BF16)

    return pl.pallas_call(
        body, name=name, grid=(rows // tr,),
        in_specs=[_rowspec(tr, MLA_W, Z_MLA // MLA_W), _fullspec((1, Q_LORA)), _fullspec((1, KV_LORA))],
        out_specs=[_rowspec(tr, Q_LORA), _rowspec(tr, KV_LORA)],
        out_shape=[jax.ShapeDtypeStruct((rows, Q_LORA), BF16), jax.ShapeDtypeStruct((rows, KV_LORA), BF16)],
        compiler_params=_params(("parallel",)),
    )(z, g_cq, g_ckv)


def _lat_bwd(z, dnq, dnkv, dkpe, g_cq, g_ckv, dz, name):
    rows = z.shape[0]
    tr = _row_tile(rows, 8 * MLA_W)

    def body(z_ref, dnq_ref, dnkv_ref, dkpe_ref, gq_ref, gkv_ref, _, dz_ref, dgq_ref, dgkv_ref):
        first = pl.program_id(0) == 0
        zb = z_ref[...]
        dcq, dgq = _rms_bwd_rows(zb[:, 0:Q_LORA], gq_ref[...], dnq_ref[...], Q_LORA)
        dckv, dgkv = _rms_bwd_rows(zb[:, Q_LORA:Q_LORA + KV_LORA], gkv_ref[...], dnkv_ref[...], KV_LORA)
        dz_ref[:, 0:Q_LORA] = dcq.astype(BF16)
        dz_ref[:, Q_LORA:Q_LORA + KV_LORA] = dckv.astype(BF16)
        dz_ref[:, Q_LORA + KV_LORA:MLA_W] = dkpe_ref[...].astype(BF16)
        _acc_rows(dgq_ref, dgq, first)
        _acc_rows(dgkv_ref, dgkv, first)

    return pl.pallas_call(
        body, name=name, grid=(rows // tr,),
        in_specs=[_rowspec(tr, MLA_W, Z_MLA // MLA_W), _rowspec(tr, Q_LORA), _rowspec(tr, KV_LORA), _rowspec(tr, LANES),
                  _fullspec((1, Q_LORA)), _fullspec((1, KV_LORA)), ANY_SPEC],
        out_specs=[_rowspec(tr, MLA_W, Z_MLA // MLA_W), _fullspec((1, Q_LORA)), _fullspec((1, KV_LORA))],
        out_shape=[jax.ShapeDtypeStruct(dz.shape, dz.dtype), jax.ShapeDtypeStruct((1, Q_LORA), F32),
                   jax.ShapeDtypeStruct((1, KV_LORA), F32)],
        input_output_aliases={6: 0}, compiler_params=_params(("arbitrary",)),
    )(z, dnq, dnkv, dkpe, g_cq, g_ckv, dz)


def _rope(y, cc, ss):
    return y * cc + pltpu.roll(y, 64, 1) * ss


def _rope_bwd(d, cc, ss):
    return d * cc + pltpu.roll(d * ss, 64, 1)


def _qk_fwd(q, kv, z, cc, ss, gqn, gqp, gkn, gkp, name):
    rows = q.shape[0]
    W = MLA_HEADS * HEAD
    tr = _row_tile(rows, 20 * W)
    QS = MLA_SCALE * LOG2E

    def body(q_ref, kv_ref, kpe_ref, cc_ref, ss_ref, gqn_ref, gqp_ref, gkn_ref, gkp_ref, qc_ref, kc_ref, v_ref):
        cc, ss = cc_ref[...], ss_ref[...]
        kpe = kpe_ref[...]
        kp = _rope(kpe * _rms(kpe, MLA_ROPE) * gkp_ref[...], cc, ss).astype(BF16)
        for h in range(MLA_HEADS):
            qn = q_ref[:, h * HEAD:(h + 1) * HEAD]
            qp = q_ref[:, W + h * HEAD:W + (h + 1) * HEAD]
            kn = kv_ref[:, h * HEAD:(h + 1) * HEAD]
            qc_ref[:, h * QCAT:h * QCAT + HEAD] = (qn * _rms(qn, HEAD) * gqn_ref[...] * QS).astype(BF16)
            qc_ref[:, h * QCAT + HEAD:(h + 1) * QCAT] = (_rope(qp * _rms(qp, MLA_ROPE) * gqp_ref[...], cc, ss) * QS).astype(BF16)
            kc_ref[:, h * QCAT:h * QCAT + HEAD] = (kn * _rms(kn, HEAD) * gkn_ref[...]).astype(BF16)
            kc_ref[:, h * QCAT + HEAD:(h + 1) * QCAT] = kp
        v_ref[...] = kv_ref[:, W:2 * W].astype(BF16)

    g = _fullspec((1, HEAD))
    return _call(
        body, None, [q, kv, z, cc, ss, gqn, gqp, gkn, gkp], name=name, grid=(rows // tr,),
        in_specs=[_rowspec(tr, 2 * W), _rowspec(tr, 2 * W), _rowspec(tr, LANES, Z_KPE // LANES), _rowspec(tr, LANES),
                  _rowspec(tr, LANES), g, g, g, g],
        out_specs=[_rowspec(tr, MLA_HEADS * QCAT), _rowspec(tr, MLA_HEADS * QCAT), _rowspec(tr, W)],
        out_shape=[jax.ShapeDtypeStruct((rows, MLA_HEADS * QCAT), BF16), jax.ShapeDtypeStruct((rows, MLA_HEADS * QCAT), BF16),
                   jax.ShapeDtypeStruct((rows, W), BF16)],
        scratch_shapes=[], sem=("parallel",))


def _qk_bwd(q, kv, z, cc, ss, gqn, gqp, gkn, gkp, dqc, dkc, dv, name):
    rows = q.shape[0]
    W = MLA_HEADS * HEAD
    tr = _row_tile(rows, 40 * W)

    def body(q_ref, kv_ref, kpe_ref, cc_ref, ss_ref, gqn_ref, gqp_ref, gkn_ref, gkp_ref, dqc_ref, dkc_ref, dv_ref,
             dq_ref, dkv_ref, dkpe_ref, dgqn_ref, dgqp_ref, dgkn_ref, dgkp_ref):
        first = pl.program_id(0) == 0
        cc, ss = cc_ref[...], ss_ref[...]
        sqn = sqp = skn = dkp = None
        for h in range(MLA_HEADS):
            dx, dg = _rms_bwd_rows(q_ref[:, h * HEAD:(h + 1) * HEAD], gqn_ref[...], dqc_ref[:, h * QCAT:h * QCAT + HEAD], HEAD)
            dq_ref[:, h * HEAD:(h + 1) * HEAD] = dx.astype(BF16)
            sqn = dg if sqn is None else sqn + dg
            dy = _rope_bwd(dqc_ref[:, h * QCAT + HEAD:(h + 1) * QCAT], cc, ss)
            dx, dg = _rms_bwd_rows(q_ref[:, W + h * HEAD:W + (h + 1) * HEAD], gqp_ref[...], dy, MLA_ROPE)
            dq_ref[:, W + h * HEAD:W + (h + 1) * HEAD] = dx.astype(BF16)
            sqp = dg if sqp is None else sqp + dg
            dx, dg = _rms_bwd_rows(kv_ref[:, h * HEAD:(h + 1) * HEAD], gkn_ref[...], dkc_ref[:, h * QCAT:h * QCAT + HEAD], HEAD)
            dkv_ref[:, h * HEAD:(h + 1) * HEAD] = dx.astype(BF16)
            skn = dg if skn is None else skn + dg
            part = dkc_ref[:, h * QCAT + HEAD:(h + 1) * QCAT].astype(F32)
            dkp = part if dkp is None else dkp + part
        dkv_ref[:, W:2 * W] = dv_ref[...].astype(BF16)
        dx, dg = _rms_bwd_rows(kpe_ref[...], gkp_ref[...], _rope_bwd(dkp, cc, ss), MLA_ROPE)
        dkpe_ref[...] = dx
        _acc_rows(dgqn_ref, sqn, first)
        _acc_rows(dgqp_ref, sqp, first)
        _acc_rows(dgkn_ref, skn, first)
        _acc_rows(dgkp_ref, dg, first)

    g = _fullspec((1, HEAD))
    gs = jax.ShapeDtypeStruct((1, HEAD), F32)
    return pl.pallas_call(
        body, name=name, grid=(rows // tr,),
        in_specs=[_rowspec(tr, 2 * W), _rowspec(tr, 2 * W), _rowspec(tr, LANES, Z_KPE // LANES), _rowspec(tr, LANES),
                  _rowspec(tr, LANES), g, g, g, g, _rowspec(tr, MLA_HEADS * QCAT), _rowspec(tr, MLA_HEADS * QCAT),
                  _rowspec(tr, W)],
        out_specs=[_rowspec(tr, 2 * W), _rowspec(tr, 2 * W), _rowspec(tr, LANES), g, g, g, g],
        out_shape=[jax.ShapeDtypeStruct((rows, 2 * W), BF16), jax.ShapeDtypeStruct((rows, 2 * W), BF16),
                   jax.ShapeDtypeStruct((rows, LANES), F32), gs, gs, gs, gs],
        compiler_params=_params(("arbitrary",)),
    )(q, kv, z, cc, ss, gqn, gqp, gkn, gkp, dqc, dkc, dv)


def _headnorm_fwd(x, col, nheads, g, out_scale, name):
    rows = x.shape[0]
    W = nheads * HEAD
    tr = _row_tile(rows, 6 * W)

    def body(x_ref, g_ref, o_ref):
        for h in range(nheads):
            xv = x_ref[:, h * HEAD:(h + 1) * HEAD]
            o_ref[:, h * HEAD:(h + 1) * HEAD] = (xv * _rms(xv, HEAD) * g_ref[...] * out_scale).astype(BF16)

    return pl.pallas_call(
        body, name=name, grid=(rows // tr,),
        in_specs=[_rowspec(tr, W, col), _fullspec((1, HEAD))], out_specs=_rowspec(tr, W),
        out_shape=jax.ShapeDtypeStruct((rows, W), BF16), compiler_params=_params(("parallel",)),
    )(x, g)


def _headnorm_bwd(x, col, nheads, g, dy, tail, name, into=None):
    rows = x.shape[0]
    W = nheads * HEAD
    tr = _row_tile(rows, 12 * W)
    has_tail = tail is not None
    WO = 2 * W if has_tail else W

    def body(*refs):
        if into is not None:
            x_ref, g_ref, dy_ref, _, dx_ref, dg_ref = refs
        elif has_tail:
            x_ref, g_ref, dy_ref, t_ref, dx_ref, dg_ref = refs
        else:
            x_ref, g_ref, dy_ref, dx_ref, dg_ref = refs
        acc = None
        for h in range(nheads):
            sl = slice(h * HEAD, (h + 1) * HEAD)
            dx, dg = _rms_bwd_rows(x_ref[:, sl], g_ref[...], dy_ref[:, sl], HEAD)
            dx_ref[:, sl] = dx.astype(BF16)
            acc = dg if acc is None else acc + dg
        if has_tail:
            dx_ref[:, W:2 * W] = t_ref[...].astype(BF16)
        _acc_rows(dg_ref, acc, pl.program_id(0) == 0)

    ins = [x, g, dy] + ([tail] if has_tail else [])
    specs = [_rowspec(tr, W, col), _fullspec((1, HEAD)), _rowspec(tr, W)] + ([_rowspec(tr, W)] if has_tail else [])
    dx_spec, dx_shape, aliases = _rowspec(tr, WO), jax.ShapeDtypeStruct((rows, WO), BF16), {}
    if into is not None:
        assert not has_tail
        ins, specs = ins + [into[0]], specs + [ANY_SPEC]
        dx_spec, dx_shape, aliases = _rowspec(tr, W, into[1]), jax.ShapeDtypeStruct(into[0].shape, into[0].dtype), {3: 0}
    return pl.pallas_call(
        body, name=name, grid=(rows // tr,), in_specs=specs,
        out_specs=[dx_spec, _fullspec((1, HEAD))], out_shape=[dx_shape, jax.ShapeDtypeStruct((1, HEAD), F32)],
        input_output_aliases=aliases, compiler_params=_params(("arbitrary",)),
    )(*ins)


def _sigmoid(x):
    return 1.0 / (1.0 + jnp.exp(-x.astype(F32)))


def _merge_fwd(z, y_gm, y_mla, y_mem, name):
    rows = z.shape[0]
    tr = _row_tile(rows, 14 * D_MODEL)

    def body(g0_ref, g1_ref, g2_ref, a_ref, b_ref, c_ref, o_ref):
        m = _sigmoid(g0_ref[...]) * a_ref[...] + _sigmoid(g1_ref[...]) * b_ref[...] + _sigmoid(g2_ref[...]) * c_ref[...]
        o_ref[...] = m.astype(BF16)

    r = _rowspec(tr, D_MODEL)
    return pl.pallas_call(
        body, name=name, grid=(rows // tr,),
        in_specs=[_rowspec(tr, D_MODEL, 0), _rowspec(tr, D_MODEL, 1), _rowspec(tr, D_MODEL, 2),r, r, r],
        out_specs=r, out_shape=jax.ShapeDtypeStruct((rows, D_MODEL), BF16), compiler_params=_params(("parallel",)),
    )(z, z, z, y_gm, y_mla, y_mem)


def _merge_bwd(z, y_gm, y_mla, y_mem, dm, name):
    rows = z.shape[0]
    tr = _row_tile(rows, 24 * D_MODEL)

    def body(g0_ref, g1_ref, g2_ref, a_ref, b_ref, c_ref, dm_ref, da_ref, db_ref, dc_ref, dzg_ref):
        dmv = dm_ref[...].astype(F32)
        for k, (g_ref, y_ref, dy_ref) in enumerate(((g0_ref, a_ref, da_ref), (g1_ref, b_ref, db_ref), (g2_ref, c_ref, dc_ref))):
            s = _sigmoid(g_ref[...])
            dy_ref[...] = (dmv * s).astype(BF16)
            dzg_ref[:, k * D_MODEL:(k + 1) * D_MODEL] = (dmv * y_ref[...] * s * (1.0 - s)).astype(BF16)

    r = _rowspec(tr, D_MODEL)
    o = jax.ShapeDtypeStruct((rows, D_MODEL), BF16)
    return pl.pallas_call(
        body, name=name, grid=(rows // tr,),
        in_specs=[_rowspec(tr, D_MODEL, 0), _rowspec(tr, D_MODEL, 1), _rowspec(tr, D_MODEL, 2),r, r, r, r],
        out_specs=[r, r, r, _rowspec(tr, 3 * D_MODEL, 0)],
        out_shape=[o, o, o, jax.ShapeDtypeStruct((rows, Z_COLS), BF16)],
        compiler_params=_params(("parallel",)),
    )(z, z, z, y_gm, y_mla, y_mem, dm)


_NT = (((1,), (1,)), ((), ()))
_TN = (((0,), (0,)), ((), ()))


def _diag_mask(s):
    row = lax.broadcasted_iota(jnp.int32, s.shape, 0)
    col = lax.broadcasted_iota(jnp.int32, s.shape, 1)
    return jnp.where(row >= col, s, NEG)


def _attn_fwd(q, k, v, nb, nheads, dk, v_col0, causal, name, rider=None):
    S, Skv = q.shape[0] // nb, k.shape[0] // nb
    tq = _pick(Skv, ATT_TILE) if causal else _pick(S, 4 * ATT_TILE)
    nq = S // tq

    def body(q_ref, k_ref, v_ref, o_ref, lse_ref):
        for i in range(nq):
            r0 = i * tq
            qb = q_ref[r0:r0 + tq, :]
            if causal:
                spans = ([(0, r0, False)] if i > 0 else []) + [(r0, r0 + tq, True)]
            else:
                spans = [(0, Skv, False)]
            scores = []
            for a, b, masked in spans:
                s = lax.dot_general(qb, k_ref[a:b, :], _NT, preferred_element_type=F32)
                scores.append(_diag_mask(s) if masked else s)
            m = functools.reduce(jnp.maximum, [jnp.max(s, axis=-1, keepdims=True) for s in scores])
            l = acc = None
            for s, (a, b, _) in zip(scores, spans):
                p = jnp.exp2(s - m)
                lp = jnp.sum(p, axis=-1, keepdims=True)
                ap = jnp.dot(p.astype(BF16), v_ref[a:b, :].astype(BF16), preferred_element_type=F32)
                l, acc = (lp, ap) if l is None else (l + lp, acc + ap)
            o_ref[r0:r0 + tq, :] = (acc / l).astype(BF16)
            lse_ref[r0:r0 + tq, :] = m + jnp.log2(l)

    ins = [q, k, v]
    in_specs = [pl.BlockSpec((S, dk), lambda b, h: (b, h)), pl.BlockSpec((Skv, dk), lambda b, h: (b, h)),
                pl.BlockSpec((Skv, HEAD), lambda b, h: (b, v_col0 + h))]
    out_specs = [pl.BlockSpec((S, HEAD), lambda b, h: (b, h)), pl.BlockSpec((None, S, 1), lambda b, h: (h, b, 0))]
    out_shape = [jax.ShapeDtypeStruct((nb * S, nheads * HEAD), BF16), jax.ShapeDtypeStruct((nheads, nb * S, 1), F32)]
    return _call(body, rider, ins, name=name, grid=(nb, nheads), in_specs=in_specs, out_specs=out_specs,
                 out_shape=out_shape, scratch_shapes=[], sem=("parallel", "parallel"))


def _attn_bwd(q, k, v, o, do, lse, nb, nheads, dk, v_col0, scale, causal, name, rider=None):
    S, Skv = q.shape[0] // nb, k.shape[0] // nb
    tk = _pick(Skv, ATT_TILE)
    nkv = Skv // tk

    def body(q_ref, k_ref, v_ref, o_ref, do_ref, lse_ref, dq_ref, dk_ref, dv_ref, delta_ref, dob_ref, dqa_ref):
        dov = do_ref[...]
        delta_ref[...] = jnp.sum(o_ref[...].astype(F32) * dov.astype(F32), axis=-1, keepdims=True)
        dob_ref[...] = dov.astype(BF16)

        for j in range(nkv):
            c0 = j * tk
            kb = k_ref[c0:c0 + tk, :]
            vb = v_ref[c0:c0 + tk, :].astype(BF16)
            if causal:
                spans = [(c0, c0 + tk, True)] + ([(c0 + tk, S, False)] if c0 + tk < S else [])
            else:
                spans = [(0, S, False)]
            dk_acc = dv_acc = None
            for a, b, masked in spans:
                qb = q_ref[a:b, :]
                dob = dob_ref[a:b, :]
                s = lax.dot_general(qb, kb, _NT, preferred_element_type=F32)
                if masked:
                    s = _diag_mask(s)
                p = jnp.exp2(s - lse_ref[a:b, :])
                dp = lax.dot_general(dob, vb, _NT, preferred_element_type=F32)
                ds = (p * (dp - delta_ref[a:b, :])).astype(BF16)
                dv_p = lax.dot_general(p.astype(BF16), dob, _TN, preferred_element_type=F32)
                dk_p = lax.dot_general(ds, qb, _TN, preferred_element_type=F32)
                dk_acc, dv_acc = (dk_p, dv_p) if dk_acc is None else (dk_acc + dk_p, dv_acc + dv_p)
                dq_p = jnp.dot(ds, kb, preferred_element_type=F32) * scale
                if j == 0:
                    dqa_ref[a:b, :] = dq_p
                else:
                    dqa_ref[a:b, :] += dq_p
            dk_ref[c0:c0 + tk, :] = (dk_acc * LN2).astype(BF16)
            dv_ref[c0:c0 + tk, :] = dv_acc.astype(BF16)
        dq_ref[...] = dqa_ref[...].astype(BF16)

    ins = [q, k, v, o, do, lse]
    in_specs = [pl.BlockSpec((S, dk), lambda b, h: (b, h)), pl.BlockSpec((Skv, dk), lambda b, h: (b, h)),
                pl.BlockSpec((Skv, HEAD), lambda b, h: (b, v_col0 + h)), pl.BlockSpec((S, HEAD), lambda b, h: (b, h)),
                pl.BlockSpec((S, HEAD), lambda b, h: (b, h)), pl.BlockSpec((None, S, 1), lambda b, h: (h, b, 0))]
    out_specs = [pl.BlockSpec((S, dk), lambda b, h: (b, h)), pl.BlockSpec((Skv, dk), lambda b, h: (b, h)),
                 pl.BlockSpec((Skv, HEAD), lambda b, h: (b, h))]
    out_shape = [jax.ShapeDtypeStruct((nb * S, nheads * dk), BF16), jax.ShapeDtypeStruct((nb * Skv, nheads * dk), BF16),
                 jax.ShapeDtypeStruct((nb * Skv, nheads * HEAD), BF16)]
    return _call(body, rider, ins, name=name, grid=(nb, nheads), in_specs=in_specs, out_specs=out_specs,
                 out_shape=out_shape,
                 scratch_shapes=[pltpu.VMEM((S, 1), F32), pltpu.VMEM((S, HEAD), BF16), pltpu.VMEM((S, dk), F32)],
                 sem=("parallel", "parallel"))


def _spread_rope(a):
    zero = jnp.zeros(a.shape[:-1] + (32,), a.dtype)
    return jnp.concatenate([a[..., :32], zero, a[..., 32:], zero], axis=-1)


def _gather_rope(a):
    return jnp.concatenate([a[..., 0:32], a[..., 64:96]], axis=-1)


def _win_layout(w):
    return jnp.concatenate([w[:, C_ZG:C_END], w[:, C_ZU:C_CQ], w[:, C_QM:C_ZG], w[:, C_CQ:C_CKV], w[:, C_CKV:C_KPE],
                            _spread_rope(w[:, C_KPE:C_QM])], axis=1)


def _win_unlayout(d):
    return jnp.concatenate([d[:, Z_GM:Z_QM], d[:, Z_MLA:Z_MLA + Q_LORA], d[:, Z_MLA + Q_LORA:Z_KPE],
                            _gather_rope(d[:, Z_KPE:Z_COLS]), d[:, Z_QM:Z_MLA], d[:, 0:Z_GM]], axis=1)


def _wuq_layout(w):
    r = w.reshape(Q_LORA, MLA_HEADS, HEAD + MLA_ROPE)
    return jnp.concatenate([r[:, :, :HEAD].reshape(Q_LORA, -1), _spread_rope(r[:, :, HEAD:]).reshape(Q_LORA, -1)], axis=1)


def _wuq_unlayout(d):
    n = d[:, :MLA_HEADS * HEAD].reshape(Q_LORA, MLA_HEADS, HEAD)
    p = _gather_rope(d[:, MLA_HEADS * HEAD:].reshape(Q_LORA, MLA_HEADS, HEAD))
    return jnp.concatenate([n, p], axis=-1).reshape(Q_LORA, -1)


def _wukv_layout(w):
    r = w.reshape(KV_LORA, MLA_HEADS, 2 * HEAD)
    return jnp.concatenate([r[:, :, :HEAD].reshape(KV_LORA, -1), r[:, :, HEAD:].reshape(KV_LORA, -1)], axis=1)


def _wukv_unlayout(d):
    k = d[:, :MLA_HEADS * HEAD].reshape(KV_LORA, MLA_HEADS, HEAD)
    v = d[:, MLA_HEADS * HEAD:].reshape(KV_LORA, MLA_HEADS, HEAD)
    return jnp.concatenate([k, v], axis=-1).reshape(KV_LORA, -1)


AG_MID = ['w_uq', 'w_ukv', 'w_mem_kv', 'w_o_gm', 'w_o_mla', 'w_o_mem', 'w_out']
AG_FFN = ['w_ff1', 'w_ff2']
RS_GROUPS = {'ffn_proj': ['w_ff2', 'w_ff1', 'w_out', 'w_o_gm', 'w_o_mla', 'w_o_mem'],
             'lat': ['w_uq', 'w_ukv', 'w_mem_kv'], 'in_top': ['w_in'], 'in_bot': ['w_in']}


def _unride(res, rider):
    return (res, None) if rider is None else res


def _local_step(x, mem, positions, target, P, ws):
    B, S, _ = x.shape
    M = mem.shape[1]
    T = B * S
    x2d = x.reshape(T, D_MODEL)
    mem2d = mem.reshape(B * M, D_MODEL)
    tgt2d = target.reshape(T, D_MODEL)

    def row(v):
        return v.reshape(1, -1).astype(F32)

    inv_freq = ROPE_BASE ** (-jnp.arange(0, MLA_ROPE, 2, dtype=F32) / MLA_ROPE)
    zero = jnp.zeros_like(inv_freq)
    ang = positions.reshape(T).astype(F32)[:, None] * jnp.concatenate([inv_freq, zero, inv_freq, zero])
    cc = jnp.cos(ang) * jnp.concatenate([zero + 1.0, zero, zero + 1.0, zero])
    ss = jnp.sin(ang) * jnp.concatenate([zero - 1.0, zero, zero + 1.0, zero])

    g_mix, g_cq, g_ckv, g_ffn, g_mem = row(P['g_mix']), row(P['g_cq']), row(P['g_ckv']), row(P['g_ffn']), row(P['g_mem'])
    gqn, gkn, gmq, gmk = row(P['g_q_nope']), row(P['g_k_nope']), row(P['g_mq']), row(P['g_mk'])
    gqp, gkp = _spread_rope(row(P['g_q_pe'])), _spread_rope(row(P['g_k_pe']))
    gln, bln = row(P['g_gm_ln']), row(P['b_gm_ln'])
    wc = jnp.tril(P['w_spatial'].astype(F32))
    wct = jnp.swapaxes(wc, 1, 2).astype(BF16)
    wc = wc.astype(BF16)
    bst = jnp.broadcast_to(P['b_spatial'].astype(F32)[:, :, None], (GM_GROUPS, GM_CHUNK, LANES))

    ride = ws.gather(['w_in'])
    h, got = _unride(_rms_fwd(x2d, g_mix, "rms_mix", rider=ride), ride)
    w_in = _win_layout(ws.gathered(['w_in'], got)['w_in']).astype(BF16)
    ride = ws.gather(AG_MID)
    z, got = _unride(_matmul(h, w_in, 'nn', ACT, "mm_in", tn_t=1792, rider=ride), ride)
    mid = ws.gathered(AG_MID, got)
    w_uq, w_ukv = _wuq_layout(mid['w_uq']).astype(BF16), _wukv_layout(mid['w_ukv']).astype(BF16)
    w_mem_kv, w_o_gm, w_o_mla, w_o_mem, w_out = (mid[n] for n in ('w_mem_kv', 'w_o_gm', 'w_o_mla', 'w_o_mem', 'w_out'))
    ygm_pre = _gm_fwd(z, gln, bln, wc, bst, "gm_fwd")
    y_gm = _matmul(ygm_pre, w_o_gm, 'nn', ACT, "mm_o_gm")
    nq, nkv = _lat_fwd(z, g_cq, g_ckv, "lat_fwd")
    q = _matmul(nq, w_uq, 'nn', ACT, "mm_uq")
    kv = _matmul(nkv, w_ukv, 'nn', ACT, "mm_ukv")
    qcat, kcat, vv = _qk_fwd(q, kv, z, cc, ss, gqn, gqp, gkn, gkp, "qk_fwd")
    ride = ws.gather(AG_FFN)
    (o, lse), got = _unride(_attn_fwd(qcat, kcat, vv, B, MLA_HEADS, QCAT, 0, True, "mla_attn_fwd", rider=ride), ride)
    ffn = ws.gathered(AG_FFN, got)
    w_ff1, w_ff2 = ffn['w_ff1'], ffn['w_ff2']
    y_mla = _matmul(o, w_o_mla, 'nn', ACT, "mm_o_mla")
    nm = _rms_fwd(mem2d, g_mem, "rms_mem")
    kvm = _matmul(nm, w_mem_kv, 'nn', ACT, "mm_mem_kv")
    qm = _headnorm_fwd(z, Z_QM // (MEM_HEADS * HEAD), MEM_HEADS, gmq, MEM_SCALE * LOG2E, "memq_fwd")
    km = _headnorm_fwd(kvm, 0, MEM_HEADS, gmk, 1.0, "memk_fwd")
    om, lse_m = _attn_fwd(qm, km, kvm, B, MEM_HEADS, HEAD, MEM_HEADS, False, "mem_attn_fwd")
    y_mem = _matmul(om, w_o_mem, 'nn', ACT, "mm_o_mem")
    merged = _merge_fwd(z, y_gm, y_mla, y_mem, "merge_fwd")
    x1 = _matmul(merged, w_out, 'nn', F32, "mm_out", add=x2d)
    h2 = _rms_fwd(x1, g_ffn, "rms_ffn")
    a1 = _matmul(h2, w_ff1, 'nn', BF16, "mm_ff1")
    dx2, dx2b, loss_part = _matmul(a1, w_ff2, 'nn', F32, "mm_ff2", add=x1, relu2_a=True, sq_err_target=tgt2d)

    G = {}
    d_ff2 = _matmul(a1, dx2b, 'tn', BF16, "mm_d_ff2", relu2_a=True)
    da1 = _matmul(dx2b, w_ff2, 'nt', BF16, "mm_da1", relu2_grad=a1)
    d_ff1 = _matmul(h2, da1, 'tn', BF16, "mm_d_ff1", col_shards=N_DEV)
    dh2 = _matmul(da1, w_ff1, 'nt', ACT, "mm_dh2")
    dx1, dx1b, G['g_ffn'] = _rms_bwd(x1, g_ffn, dh2, dx2, "rms_ffn_bwd", dx_dtypes=(F32, BF16))
    d_out = _matmul(merged, dx1b, 'tn', BF16, "mm_d_out")
    dmerged = _matmul(dx1b, w_out, 'nt', ACT, "mm_dmerged")
    dy_gm, dy_mla, dy_mem, dz = _merge_bwd(z, y_gm, y_mla, y_mem, dmerged, "merge_bwd")
    d_o_gm = _matmul(ygm_pre, dy_gm, 'tn', BF16, "mm_d_o_gm")
    d_o_mla = _matmul(o, dy_mla, 'tn', BF16, "mm_d_o_mla")
    d_o_mem = _matmul(om, dy_mem, 'tn', BF16, "mm_d_o_mem")
    dygm_pre = _matmul(dy_gm, w_o_gm, 'nt', ACT, "mm_dygm")
    dz, dws, dbs, G['g_gm_ln'], G['b_gm_ln'] = _gm_bwd(z, dygm_pre, gln, bln, wc, wct, bst, dz, "gm_bwd")
    G['w_spatial'] = jnp.tril(dws)
    G['b_spatial'] = jnp.sum(dbs.reshape(GM_CHUNK, GM_GROUPS, LANES), axis=-1).T
    do = _matmul(dy_mla, w_o_mla, 'nt', ACT, "mm_do")
    ride = ws.scatter('ffn_proj', {'w_ff2': d_ff2, 'w_ff1': d_ff1, 'w_out': d_out, 'w_o_gm': d_o_gm, 'w_o_mla': d_o_mla,
                                   'w_o_mem': d_o_mem})
    (dqc, dkc, dvv), got = _unride(_attn_bwd(qcat, kcat, vv, o, do, lse, B, MLA_HEADS, QCAT, 0, MLA_SCALE, True,
                                             "mla_attn_bwd", rider=ride), ride)
    ws.scattered('ffn_proj', got)
    dq, dkv, dkpe, G['g_q_nope'], dgqp, G['g_k_nope'], dgkp = _qk_bwd(q, kv, z, cc, ss, gqn, gqp, gkn, gkp, dqc, dkc, dvv,
                                                                     "qk_bwd")
    G['g_q_pe'], G['g_k_pe'] = _gather_rope(dgqp), _gather_rope(dgkp)
    d_uq = _wuq_unlayout(_matmul(nq, dq, 'tn', BF16, "mm_d_uq"))
    dnq = _matmul(dq, w_uq, 'nt', ACT, "mm_dnq")
    d_ukv = _wukv_unlayout(_matmul(nkv, dkv, 'tn', BF16, "mm_d_ukv"))
    dnkv = _matmul(dkv, w_ukv, 'nt', ACT, "mm_dnkv")
    dz, G['g_cq'], G['g_ckv'] = _lat_bwd(z, dnq, dnkv, dkpe, g_cq, g_ckv, dz, "lat_bwd")
    dom = _matmul(dy_mem, w_o_mem, 'nt', ACT, "mm_dom")
    dqm, dkm, dvm = _attn_bwd(qm, km, kvm, om, dom, lse_m, B, MEM_HEADS, HEAD, MEM_HEADS, MEM_SCALE, False, "mem_attn_bwd")
    dz, G['g_mq'] = _headnorm_bwd(z, Z_QM // (MEM_HEADS * HEAD), MEM_HEADS, gmq, dqm, None, "memq_bwd",
                                  into=(dz, Z_QM // (MEM_HEADS * HEAD)))
    dkvm, G['g_mk'] = _headnorm_bwd(kvm, 0, MEM_HEADS, gmk, dkm, dvm, "memk_bwd")
    d_mem_kv = _matmul(nm, dkvm, 'tn', BF16, "mm_d_mem_kv")
    dnm = _matmul(dkvm, w_mem_kv, 'nt', ACT, "mm_dnm")
    G['g_mem'], = _rms_bwd(mem2d, g_mem, dnm, None, "rms_mem_bwd", dx_dtypes=())
    half = D_MODEL // 2
    ride = ws.scatter('lat', {'w_uq': d_uq, 'w_ukv': d_ukv, 'w_mem_kv': d_mem_kv})
    d_top, got = _unride(_matmul(h, dz, 'tn', BF16, "mm_d_in_top", tn_t=1792, m_rows=(0, half), rider=ride), ride)
    ws.scattered('lat', got)
    ride = ws.scatter('in_top', {'w_in': _win_unlayout(d_top)})
    d_bot, got = _unride(_matmul(h, dz, 'tn', BF16, "mm_d_in_bot", tn_t=1792, m_rows=(half, half), rider=ride), ride)
    ws.scattered('in_top', got)
    ride = ws.scatter('in_bot', {'w_in': _win_unlayout(d_bot)})
    dh, got = _unride(_matmul(dz, w_in, 'nt', ACT, "mm_dh", rider=ride), ride)
    ws.scattered('in_bot', got)
    gx, G['g_mix'] = _rms_bwd(x2d, g_mix, dh, dx1, "rms_mix_bwd")
    return loss_part, gx.reshape(B, S, D_MODEL), G


def _all_gather8(xs, name):
    def body(x_ref, out_ref, send_sems, recv_sems, local_sem):
        x, y, c = lax.axis_index("x"), lax.axis_index("y"), lax.axis_index("c")
        me, sibling = (x, y, c), (x, y, 1 - c)
        chips = [(1 - x, y), (x, 1 - y), (1 - x, 1 - y)]

        def rows(px, py, pc):
            return out_ref.at[4 * px + 2 * py + pc]

        def copy(k, block, to, src=None):
            return pltpu.make_async_remote_copy(
                src_ref=rows(*block) if src is None else src, dst_ref=rows(*block),
                send_sem=send_sems.at[k], recv_sem=recv_sems.at[k], device_id=to, device_id_type=MESH)

        mine = pltpu.make_async_copy(x_ref, rows(*me), local_sem)
        mine.start()
        first = [copy(0, me, sibling, src=x_ref)]
        first += [copy(1 + j, me, (*chip, c), src=x_ref) for j, chip in enumerate(chips)]
        for cp in first:
            cp.start()
        passed = [copy(4 + j, (*chip, c), sibling) for j, chip in enumerate(chips)]
        for j, chip in enumerate(chips):
            copy(1 + j, (*chip, c), me).wait_recv()
            passed[j].start()
        copy(0, sibling, me).wait_recv()
        for j, chip in enumerate(chips):
            copy(4 + j, (*chip, 1 - c), me).wait_recv()
        for cp in first + passed:
            cp.wait_send()
        mine.wait()

    return pl.pallas_call(
        body, name=name, in_specs=[HBM_SPEC], out_specs=HBM_SPEC,
        out_shape=jax.ShapeDtypeStruct((N_DEV,) + xs.shape, xs.dtype),
        scratch_shapes=[pltpu.SemaphoreType.DMA((7,)), pltpu.SemaphoreType.DMA((7,)), pltpu.SemaphoreType.DMA],
    )(xs)


def _adamw_rows(w, g, m, v):
    m2 = ADAM_B1 * m + (1.0 - ADAM_B1) * g
    v2 = ADAM_B2 * v + (1.0 - ADAM_B2) * (g * g)
    m_hat = m2 / (1.0 - ADAM_B1 ** ADAM_STEP)
    v_hat = v2 / (1.0 - ADAM_B2 ** ADAM_STEP)
    delta = -ADAM_LR * (m_hat / (jnp.sqrt(v_hat) + ADAM_EPS) + ADAM_WD * w)
    return delta, m2, v2


def _sum_adamw(parts, w, m, v, name):
    rows, cols = w.shape
    assert sum(p.shape[1] for p in parts) == rows
    tr = _pick(min(p.shape[1] for p in parts), max(16, 65536 // cols), 16)
    n = parts[0].shape[0]
    counts = [p.shape[1] // tr for p in parts]
    starts = [sum(counts[:k]) for k in range(len(parts))]

    def body(*refs):
        p_refs = refs[:len(parts)]
        w_ref, m_ref, v_ref, g_ref, d_ref, m2_ref, v2_ref = refs[len(parts):]
        g = None
        for p_ref, start in zip(p_refs, starts):
            gk = p_ref[0].astype(F32)
            for k in range(1, n):
                gk = gk + p_ref[k].astype(F32)
            g = gk if g is None else jnp.where(pl.program_id(0) >= start, gk, g)
        delta, m2, v2 = _adamw_rows(w_ref[...], g, m_ref[...], v_ref[...])
        g_ref[...] = g
        d_ref[...] = delta
        m2_ref[...] = m2
        v2_ref[...] = v2

    flat = pl.BlockSpec((tr, cols), lambda i: (i, 0))
    out = jax.ShapeDtypeStruct((rows, cols), F32)
    p_specs = [pl.BlockSpec((n, tr, cols), lambda i, s=s, c=c: (0, jnp.clip(i - s, 0, c - 1), 0))
               for s, c in zip(starts, counts)]
    return pl.pallas_call(
        body, name=name, grid=(rows // tr,), in_specs=p_specs + [flat, flat, flat], out_specs=[flat] * 4,
        out_shape=[out] * 4, compiler_params=_params(("parallel",)),
    )(*parts, w, m, v)


SMALL_WIDTH = {'g_mix': 1024, 'g_cq': 384, 'g_ckv': 256, 'g_q_nope': 128, 'g_q_pe': 128, 'g_k_nope': 128, 'g_k_pe': 128,
               'g_gm_ln': 512, 'b_gm_ln': 512, 'g_mem': 1024, 'g_mq': 128, 'g_mk': 128, 'g_ffn': 1024}
NARROW = ('g_q_pe', 'g_k_pe')


def _small_layout():
    layout, r = {}, 0
    for name in SMALL + ['loss']:
        rows = {'w_spatial': GM_GROUPS * GM_CHUNK, 'b_spatial': GM_GROUPS, 'loss': 1}.get(name) or SMALL_WIDTH[name] // LANES
        layout[name] = (r, rows)
        r += -(-rows // 8) * 8
    return layout, r


def _small_pack(grads, loss_part, name):
    layout, total = _small_layout()
    names = SMALL + ['loss']

    def body(*refs):
        out_ref = refs[-1]
        out_ref[...] = jnp.zeros((total, LANES), F32)
        for ref, n in zip(refs[:-1], names):
            r0, rows = layout[n]
            if n == 'w_spatial':
                for g in range(GM_GROUPS):
                    out_ref[r0 + g * GM_CHUNK:r0 + (g + 1) * GM_CHUNK, :] = ref[g]
            elif n == 'b_spatial':
                out_ref[r0:r0 + rows, :] = ref[...]
            else:
                for k in range(rows):
                    out_ref[r0 + k:r0 + k + 1, :] = ref[:, k * LANES:(k + 1) * LANES]

    return pl.pallas_call(body, name=name, out_shape=jax.ShapeDtypeStruct((total, LANES), F32))(
        *[grads[n] for n in SMALL], loss_part)


def _small_adamw(parts, w, m, v, name):
    layout, _ = _small_layout()
    n_dev = parts.shape[0]

    def body(*refs):
        p_ref = refs[0]
        ins = refs[1:1 + 3 * len(SMALL)]
        outs = refs[1 + 3 * len(SMALL):-1]

        def gsum(r0, rows):
            g = p_ref[0, r0:r0 + rows, :]
            for d in range(1, n_dev):
                g = g + p_ref[d, r0:r0 + rows, :]
            return g

        def step(idx, g, at):
            w_ref, m_ref, v_ref = ins[3 * idx:3 * idx + 3]
            delta, m2, v2 = _adamw_rows(w_ref[at], g, m_ref[at], v_ref[at])
            for ref, val in zip(outs[4 * idx:4 * idx + 4], (g, delta, m2, v2)):
                ref[at] = val

        for idx, n in enumerate(SMALL):
            r0, rows = layout[n]
            if n == 'w_spatial':
                for g in range(GM_GROUPS):
                    step(idx, gsum(r0 + g * GM_CHUNK, GM_CHUNK), (0, g))
            elif n == 'b_spatial':
                step(idx, gsum(r0, rows), (0,))
            else:
                for k in range(rows):
                    step(idx, gsum(r0 + k, 1), (slice(None), slice(k * LANES, (k + 1) * LANES)))
        refs[-1][...] = gsum(layout['loss'][0], 8)

    flat_in = [d[n] for n in SMALL for d in (w, m, v)]
    out_shape = [jax.ShapeDtypeStruct(w[n].shape, F32) for n in SMALL for _ in range(4)]
    res = pl.pallas_call(body, name=name, out_shape=out_shape + [jax.ShapeDtypeStruct((8, LANES), F32)])(parts, *flat_in)
    groups = [{n: res[4 * i + j] for i, n in enumerate(SMALL)} for j in range(4)]
    return groups, res[-1]


def _full_from_gathered(gathered, name):
    r, c = BIG_SHAPE[name]
    if BIG_AXIS[name] == 0:
        return gathered.reshape(r, c)
    return gathered.transpose(1, 0, 2).reshape(r, c)


def _shards_of_full(g, name):
    if g.ndim == 3:
        return g
    r, c = BIG_SHAPE[name]
    if BIG_AXIS[name] == 0:
        return g.reshape(N_DEV, r // N_DEV, c)
    return g.reshape(g.shape[0], N_DEV, c // N_DEV).transpose(1, 0, 2)


class _DistWeights:
    def __init__(self, shards):
        self.shards = shards
        self.received = {}

    def gather(self, names):
        return _Gather2([self.shards[n].astype(BF16) for n in names])

    def gathered(self, names, got):
        return {n: _full_from_gathered(g, n) for n, g in zip(names, got)}

    def scatter(self, key, grads):
        return _Exchange([_shards_of_full(grads[n], n) for n in RS_GROUPS[key]], scatter=True)

    def scattered(self, key, got):
        for n, g in zip(RS_GROUPS[key], got):
            self.received.setdefault(n, []).append(g)


def kernel(x, mem, positions, g_mix, w_in, g_cq, w_uq, g_ckv, w_ukv, g_q_nope, g_q_pe, g_k_nope, g_k_pe, g_gm_ln, b_gm_ln, w_spatial, b_spatial, g_mem, w_mem_kv, g_mq, g_mk, w_o_gm, w_o_mla, w_o_mem, w_out, g_ffn, w_ff1, w_ff2, loss_target, m_g_mix, m_w_in, m_g_cq, m_w_uq, m_g_ckv, m_w_ukv, m_g_q_nope, m_g_q_pe, m_g_k_nope, m_g_k_pe, m_g_gm_ln, m_b_gm_ln, m_w_spatial, m_b_spatial, m_g_mem, m_w_mem_kv, m_g_mq, m_g_mk, m_w_o_gm, m_w_o_mla, m_w_o_mem, m_w_out, m_g_ffn, m_w_ff1, m_w_ff2, v_g_mix, v_w_in, v_g_cq, v_w_uq, v_g_ckv, v_w_ukv, v_g_q_nope, v_g_q_pe, v_g_k_nope, v_g_k_pe, v_g_gm_ln, v_b_gm_ln, v_w_spatial, v_b_spatial, v_g_mem, v_w_mem_kv, v_g_mq, v_g_mk, v_w_o_gm, v_w_o_mla, v_w_o_mem, v_w_out, v_g_ffn, v_w_ff1, v_w_ff2):
    given = dict(locals())
    w = {n: given[n][0] for n in WEIGHTS}
    mom = {n: given['m_' + n][0] for n in WEIGHTS}
    var = {n: given['v_' + n][0] for n in WEIGHTS}

    ws = _DistWeights({n: w[n] for n in BIG})
    loss_part, grad_x, G = _local_step(x, mem, positions, loss_target, {n: w[n] for n in SMALL}, ws)

    outs = {}
    for n in BIG:
        for prefix, res in zip(("grad_", "delta_", "new_m_", "new_v_"),
                               _sum_adamw(ws.received[n], w[n], mom[n], var[n], "adamw_" + n)):
            outs[prefix + n] = res[None]

    def widen(d):
        return {n: (jnp.pad(d[n], ((0, 0), (0, LANES - MLA_ROPE))) if n in NARROW else d[n]) for n in SMALL}

    parts = _all_gather8(_small_pack(widen(G), loss_part, "small_pack"), "ag_small")
    small, loss_rows = _small_adamw(parts, *[widen({n: given[prefix + n] for n in SMALL}) for prefix in ("", "m_", "v_")],
                                    "adamw_small")
    loss = 0.5 * jnp.sum(loss_rows) / D_MODEL
    for prefix, group in zip(("grad_", "delta_", "new_m_", "new_v_"), small):
        for n in SMALL:
            outs[prefix + n] = group[n][:, :MLA_ROPE] if n in NARROW else group[n]
    return (loss, grad_x, *[outs[p + n] for p in ("grad_", "delta_", "new_m_", "new_v_") for n in WEIGHTS])
```

```python
import functools
import math

import jax
import jax.numpy as jnp
from jax import lax
from jax.experimental import pallas as pl
from jax.experimental.pallas import tpu as pltpu

F32 = jnp.float32
BF16 = jnp.bfloat16
ACT = BF16

D_MODEL = 1024
MEM_HEADS = 4
HEAD = 128
GM_WIDTH = 512
GM_CHUNK = 128
GM_GROUPS = 4
MLA_HEADS = 8
MLA_ROPE = 64
Q_LORA = 384
KV_LORA = 256
D_FF = 4096
EPS = 1e-6
ROPE_BASE = 10000.0
MLA_SCALE = 1.0 / math.sqrt(HEAD + MLA_ROPE)
MEM_SCALE = 1.0 / math.sqrt(HEAD)
LOG2E = 1.4426950408889634
LN2 = 0.6931471805599453
ATT_TILE = 256
C_ZU, C_ZV, C_CQ, C_CKV, C_KPE, C_QM, C_ZG, C_END = 0, 512, 1024, 1408, 1664, 1728, 2240, 5312
Z_GM, Z_QM, Z_MLA, Z_KPE, Z_COLS = 3072, 4096, 4608, 5248, 5376
MLA_W = 768
QCAT = 2 * HEAD
ADAM_LR, ADAM_B1, ADAM_B2, ADAM_EPS, ADAM_WD, ADAM_STEP = 0.001, 0.9, 0.999, 1e-08, 0.01, 10
N_DEV = 8
LANES = 128
VMEM_LIMIT = 48 * 1024 * 1024
MAX_K_TILE = 8192
NEG = -1e30

BIG = ['w_in', 'w_uq', 'w_ukv', 'w_mem_kv', 'w_o_gm', 'w_o_mla', 'w_o_mem', 'w_out', 'w_ff1', 'w_ff2']
BIG_AXIS = {'w_in': 1, 'w_uq': 1, 'w_ukv': 1, 'w_mem_kv': 0, 'w_o_gm': 1, 'w_o_mla': 0, 'w_o_mem': 1,
            'w_out': 0, 'w_ff1': 1, 'w_ff2': 0}
BIG_SHAPE = {'w_in': (1024, 5312), 'w_uq': (384, 1536), 'w_ukv': (256, 2048), 'w_mem_kv': (1024, 1024),
             'w_o_gm': (512, 1024), 'w_o_mla': (1024, 1024), 'w_o_mem': (512, 1024), 'w_out': (1024, 1024),
             'w_ff1': (1024, 4096), 'w_ff2': (4096, 1024)}
SMALL = ['g_mix', 'g_cq', 'g_ckv', 'g_q_nope', 'g_q_pe', 'g_k_nope', 'g_k_pe', 'g_gm_ln', 'b_gm_ln',
         'w_spatial', 'b_spatial', 'g_mem', 'g_mq', 'g_mk', 'g_ffn']
WEIGHTS = ['g_mix', 'w_in', 'g_cq', 'w_uq', 'g_ckv', 'w_ukv', 'g_q_nope', 'g_q_pe', 'g_k_nope', 'g_k_pe',
           'g_gm_ln', 'b_gm_ln', 'w_spatial', 'b_spatial', 'g_mem', 'w_mem_kv', 'g_mq', 'g_mk', 'w_o_gm',
           'w_o_mla', 'w_o_mem', 'w_out', 'g_ffn', 'w_ff1', 'w_ff2']


def _pick(n, target, mult=LANES):
    best = None
    t = mult
    while t <= min(n, target):
        if n % t == 0:
            best = t
        t += mult
    return best if best is not None else n


def _params(sem):
    return pltpu.CompilerParams(dimension_semantics=sem, vmem_limit_bytes=VMEM_LIMIT)


MESH = pl.DeviceIdType.MESH
HBM_SPEC = pl.BlockSpec(memory_space=pltpu.HBM)


class _Exchange:
    def __init__(self, srcs, scatter):
        self.srcs, self.scatter = list(srcs), scatter
        self.out_shapes = [jax.ShapeDtypeStruct(s.shape if scatter else (N_DEV,) + s.shape, s.dtype) for s in self.srcs]
        n = len(self.srcs)
        self.scratch = [pltpu.SemaphoreType.DMA((n, N_DEV - 1)), pltpu.SemaphoreType.DMA((n, N_DEV - 1)),
                        pltpu.SemaphoreType.DMA((n,))]

    def _copies(self, src_refs, dst_refs, send_sems, recv_sems, local_sems):
        x, y, c = lax.axis_index("x"), lax.axis_index("y"), lax.axis_index("c")
        me = 4 * x + 2 * y + c
        local, remote = [], []
        for a, (src_ref, dst_ref) in enumerate(zip(src_refs, dst_refs)):
            def mine_for(dev, src_ref=src_ref):
                return src_ref.at[dev] if self.scatter else src_ref

            local.append(pltpu.make_async_copy(mine_for(me), dst_ref.at[me], local_sems.at[a]))
            for k in range(1, N_DEV):
                px = 1 - x if k & 4 else x
                py = 1 - y if k & 2 else y
                pc = 1 - c if k & 1 else c
                remote.append(pltpu.make_async_remote_copy(
                    src_ref=mine_for(4 * px + 2 * py + pc), dst_ref=dst_ref.at[me], send_sem=send_sems.at[a, k - 1],
                    recv_sem=recv_sems.at[a, k - 1], device_id=(px, py, pc), device_id_type=MESH))
        return local, remote

    def start(self, *refs):
        local, remote = self._copies(*refs)
        for cp in local + remote:
            cp.start()

    def forward(self, *refs):
        pass

    def finish(self, *refs):
        local, remote = self._copies(*refs)
        for cp in remote + local:
            cp.wait()


class _Gather2:
    def __init__(self, srcs):
        self.srcs = list(srcs)
        self.out_shapes = [jax.ShapeDtypeStruct((N_DEV,) + s.shape, s.dtype) for s in self.srcs]
        n = len(self.srcs)
        self.scratch = [pltpu.SemaphoreType.DMA((n, N_DEV - 1)), pltpu.SemaphoreType.DMA((n, N_DEV - 1)),
                        pltpu.SemaphoreType.DMA((n,))]

    def _plan(self, src_refs, dst_refs, send_sems, recv_sems, local_sems):
        x, y, c = lax.axis_index("x"), lax.axis_index("y"), lax.axis_index("c")
        chips = [(1 - x, y), (x, 1 - y), (1 - x, 1 - y)]
        plans = []
        for a, (src_ref, dst_ref) in enumerate(zip(src_refs, dst_refs)):
            def copy(k, block, to, src=None, a=a, dst_ref=dst_ref):
                at = dst_ref.at[4 * block[0] + 2 * block[1] + block[2]]
                return pltpu.make_async_remote_copy(src_ref=at if src is None else src, dst_ref=at,
                                                    send_sem=send_sems.at[a, k], recv_sem=recv_sems.at[a, k],
                                                    device_id=to, device_id_type=MESH)

            local = pltpu.make_async_copy(src_ref, dst_ref.at[4 * x + 2 * y + c], local_sems.at[a])
            first = [copy(0, (x, y, c), (x, y, 1 - c), src=src_ref)]
            first += [copy(1 + j, (x, y, c), (*chip, c), src=src_ref) for j, chip in enumerate(chips)]
            passed = [copy(4 + j, (*chip, c), (x, y, 1 - c)) for j, chip in enumerate(chips)]
            arrivals = [copy(1 + j, (*chip, c), (x, y, c)) for j, chip in enumerate(chips)]
            late = [copy(0, (x, y, 1 - c), (x, y, c))] + [copy(4 + j, (*chip, 1 - c), (x, y, c)) for j, chip in enumerate(chips)]
            plans.append((local, first, passed, arrivals, late))
        return plans

    def start(self, *refs):
        for local, first, _, _, _ in self._plan(*refs):
            local.start()
            for cp in first:
                cp.start()

    def forward(self, *refs):
        for _, _, passed, arrivals, _ in self._plan(*refs):
            for arrived, onward in zip(arrivals, passed):
                arrived.wait_recv()
                onward.start()

    def finish(self, *refs):
        for local, first, passed, _, late in self._plan(*refs):
            for cp in late:
                cp.wait_recv()
            for cp in first + passed:
                cp.wait_send()
            local.wait()


def _call(body, rider, ins, *, name, grid, in_specs, out_specs, out_shape, scratch_shapes, sem):
    if rider is None:
        return pl.pallas_call(body, name=name, grid=grid, in_specs=in_specs, out_specs=out_specs, out_shape=out_shape,
                              scratch_shapes=scratch_shapes, compiler_params=_params(sem))(*ins)
    single = not isinstance(out_shape, (list, tuple))
    own_specs, own_shapes = ([out_specs], [out_shape]) if single else (list(out_specs), list(out_shape))
    n_in, n_out, n_sc, n_r = len(ins), len(own_shapes), len(scratch_shapes), len(rider.srcs)
    n_all_in = n_in + n_r

    def carrying(*refs):
        own_in, srcs = refs[:n_in], refs[n_in:n_in + n_r]
        own_out, dsts = refs[n_all_in:n_all_in + n_out], refs[n_all_in + n_out:n_all_in + n_out + n_r]
        own_sc = refs[n_all_in + n_out + n_r:n_all_in + n_out + n_r + n_sc]
        sems = refs[n_all_in + n_out + n_r + n_sc:]
        first = last = late = None
        for d, steps in enumerate(grid):
            f, l = pl.program_id(d) == 0, pl.program_id(d) == steps - 1
            t = pl.program_id(d) == (steps // 2 if d == 0 else 0)
            first, last, late = (f, l, t) if first is None else (first & f, last & l, late & t)

        @pl.when(first)
        def _():
            rider.start(srcs, dsts, *sems)

        @pl.when(late)
        def _():
            rider.forward(srcs, dsts, *sems)

        body(*own_in, *own_out, *own_sc)

        @pl.when(last)
        def _():
            rider.finish(srcs, dsts, *sems)

    res = pl.pallas_call(
        carrying, name=name, grid=grid, in_specs=list(in_specs) + [HBM_SPEC] * n_r,
        out_specs=own_specs + [HBM_SPEC] * n_r, out_shape=own_shapes + rider.out_shapes,
        scratch_shapes=list(scratch_shapes) + rider.scratch, compiler_params=_params(("arbitrary",) * len(grid)),
    )(*ins, *rider.srcs)
    own = res[:n_out]
    return (own[0] if single else list(own)), list(res[n_out:])


def _matmul(a, b, mode, out_dtype, name, add=None, relu2_a=False, relu2_grad=None,
            tm_t=None, tn_t=None, tk_t=None, rider=None, m_rows=None, col_shards=None, sq_err_target=None):
    if mode == 'nn':
        (M, K), (K2, N) = a.shape, b.shape
    elif mode == 'nt':
        (M, K), (N, K2) = a.shape, b.shape
    else:
        (K, M), (K2, N) = a.shape, b.shape
    assert K == K2, (name, a.shape, b.shape)
    m_first = 0
    if m_rows is not None:
        assert mode == 'tn'
        m_first, M = m_rows
    if col_shards is not None:
        assert add is None and relu2_grad is None and sq_err_target is None and tn_t is None
    if mode == 'tn':
        d_tm, d_tn, d_tk = 1024, (2048 if M <= 512 else 1024), 2048
    else:
        wide = add is None and sq_err_target is None and jnp.dtype(out_dtype).itemsize == 2 and K <= D_FF
        d_tm, d_tn, d_tk = (2048 if K <= 1024 else 1024), (1024 if wide else 512), MAX_K_TILE
    tm, tn, tk = _pick(M, tm_t or d_tm), _pick(N, tn_t or d_tn), _pick(K, tk_t or d_tk)
    gm, gn, nk = M // tm, N // tn, K // tk
    if mode == 'nn':
        a_spec = pl.BlockSpec((tm, tk), lambda i, j, k: (i, k))
        b_spec = pl.BlockSpec((tk, tn), lambda i, j, k: (k, j))
        dims = (((1,), (0,)), ((), ()))
    elif mode == 'nt':
        a_spec = pl.BlockSpec((tm, tk), lambda i, j, k: (i, k))
        b_spec = pl.BlockSpec((tn, tk), lambda i, j, k: (j, k))
        dims = (((1,), (1,)), ((), ()))
    else:
        assert m_first % tm == 0
        a_spec = pl.BlockSpec((tk, tm), lambda i, j, k: (k, m_first // tm + i))
        b_spec = pl.BlockSpec((tk, tn), lambda i, j, k: (k, j))
        dims = (((0,), (0,)), ((), ()))
    o_spec = pl.BlockSpec((tm, tn), lambda i, j, k: (i, j))
    shard_w = N // col_shards if col_shards is not None else tn
    assert tn % shard_w == 0
    has_add, has_e, has_t = add is not None, relu2_grad is not None, sq_err_target is not None
    assert not has_t or (nk == 1 and tn % LANES == 0)

    def body(*refs):
        a_ref, b_ref = refs[0], refs[1]
        pos = 2
        add_ref = e_ref = t_ref = None
        if has_add:
            add_ref = refs[pos]
            pos += 1
        if has_e:
            e_ref = refs[pos]
            pos += 1
        if has_t:
            t_ref = refs[pos]
            pos += 1
        o_ref = refs[pos]
        acc_ref = refs[pos + 1] if nk > 1 else None

        av = a_ref[...]
        if relu2_a:
            av = jnp.maximum(av, 0)
            av = av * av
        prod = lax.dot_general(av.astype(BF16), b_ref[...].astype(BF16), dims, preferred_element_type=F32)

        def finish(r):
            if has_add:
                r = r + add_ref[...]
            if has_e:
                r = r * (2.0 * jnp.maximum(e_ref[...].astype(F32), 0.0))
            if has_t:
                err = r - t_ref[...]
                r = err * (1.0 / N)
                refs[pos + 1][...] = r.astype(BF16)
                sq = err * err
                part = sq[:, 0:LANES]
                for c in range(1, tn // LANES):
                    part = part + sq[:, c * LANES:(c + 1) * LANES]
                _acc_rows(refs[pos + 2], part, (pl.program_id(0) == 0) & (pl.program_id(1) == 0))
            if col_shards is not None:
                for s in range(tn // shard_w):
                    o_ref[s] = r[:, s * shard_w:(s + 1) * shard_w].astype(out_dtype)
            else:
                o_ref[...] = r.astype(out_dtype)

        if nk == 1:
            finish(prod)
        else:
            k = pl.program_id(2)

            @pl.when(k == 0)
            def _():
                acc_ref[...] = prod

            @pl.when(k > 0)
            def _():
                acc_ref[...] += prod

            @pl.when(k == nk - 1)
            def _():
                finish(acc_ref[...])

    ins, specs = [a, b], [a_spec, b_spec]
    if has_add:
        ins.append(add)
        specs.append(o_spec)
    if has_e:
        ins.append(relu2_grad)
        specs.append(o_spec)
    out_specs, out_shape, sem = o_spec, jax.ShapeDtypeStruct((M, N), out_dtype), ("parallel", "parallel", "arbitrary")
    if has_t:
        ins.append(sq_err_target)
        specs.append(o_spec)
        out_specs = [o_spec, o_spec, pl.BlockSpec((1, LANES), lambda i, j, k: (0, 0))]
        out_shape = [out_shape, jax.ShapeDtypeStruct((M, N), BF16), jax.ShapeDtypeStruct((1, LANES), F32)]
        sem = ("arbitrary", "arbitrary", "arbitrary")
    if col_shards is not None:
        out_specs = pl.BlockSpec((tn // shard_w, tm, shard_w), lambda i, j, k: (j, i, 0))
        out_shape = jax.ShapeDtypeStruct((col_shards, M, shard_w), out_dtype)
    return _call(body, rider, ins, name=name, grid=(gm, gn, nk), in_specs=specs, out_specs=out_specs, out_shape=out_shape,
                 scratch_shapes=[pltpu.VMEM((tm, tn), F32)] if nk > 1 else [], sem=sem)


ROW_BLOCK_BYTES = 12 * 1024 * 1024


def _row_tile(rows, row_bytes):
    return _pick(rows, max(16, min(1024, ROW_BLOCK_BYTES // row_bytes)), 16)


def _rowspec(tr, width, col=0):
    return pl.BlockSpec((tr, width), lambda i, col=col: (i, col))


def _fullspec(shape):
    nd = len(shape)
    return pl.BlockSpec(shape, lambda i, nd=nd: (0,) * nd)


def _rms(x, width):
    x = x.astype(F32)
    return lax.rsqrt(jnp.sum(x * x, axis=-1, keepdims=True) * (1.0 / width) + EPS)


def _rms_bwd_rows(x, g, dy, width):
    x, dy = x.astype(F32), dy.astype(F32)
    r = _rms(x, width)
    xh = x * r
    dn = dy * g
    dx = r * (dn - xh * (jnp.sum(dn * xh, axis=-1, keepdims=True) * (1.0 / width)))
    return dx, dy * xh


def _acc_rows(ref, val, first):
    s = jnp.sum(val, axis=0, keepdims=True)

    @pl.when(first)
    def _():
        ref[...] = s

    @pl.when(jnp.logical_not(first))
    def _():
        ref[...] += s


def _rms_fwd(x, g, name, rider=None):
    rows, width = x.shape
    tr = _row_tile(rows, 6 * width)

    def body(x_ref, g_ref, o_ref):
        xv = x_ref[...]
        o_ref[...] = (xv * _rms(xv, width) * g_ref[...]).astype(BF16)

    return _call(body, rider, [x, g], name=name, grid=(rows // tr,),
                 in_specs=[_rowspec(tr, width), _fullspec((1, width))], out_specs=_rowspec(tr, width),
                 out_shape=jax.ShapeDtypeStruct((rows, width), BF16), scratch_shapes=[], sem=("parallel",))


def _rms_bwd(x, g, dy, res, name, dx_dtypes=(F32,)):
    rows, width = x.shape
    tr = _row_tile(rows, 18 * width)
    has_res = res is not None
    n_in = 4 if has_res else 3

    def body(*refs):
        x_ref, g_ref, dy_ref = refs[:3]
        dx, dgv = _rms_bwd_rows(x_ref[...], g_ref[...], dy_ref[...], width)
        if has_res:
            dx = dx + refs[3][...]
        for ref, dt in zip(refs[n_in:], dx_dtypes):
            ref[...] = dx.astype(dt)
        _acc_rows(refs[-1], dgv, pl.program_id(0) == 0)

    ins = [x, g, dy] + ([res] if has_res else [])
    specs = [_rowspec(tr, width), _fullspec((1, width)), _rowspec(tr, width)] + ([_rowspec(tr, width)] if has_res else [])
    return pl.pallas_call(
        body, name=name, grid=(rows // tr,), in_specs=specs,
        out_specs=[_rowspec(tr, width)] * len(dx_dtypes) + [_fullspec((1, width))],
        out_shape=[jax.ShapeDtypeStruct((rows, width), dt) for dt in dx_dtypes] + [jax.ShapeDtypeStruct((1, width), F32)],
        compiler_params=_params(("arbitrary",)),
    )(*ins)


_GELU_C = math.sqrt(2.0 / math.pi)


def _gelu(x):
    t = jnp.tanh(_GELU_C * (x + 0.044715 * (x * x * x)))
    return 0.5 * x * (1.0 + t), t


def _gelu_grad(x, t):
    return 0.5 * (1.0 + t) + 0.5 * x * (1.0 - t * t) * (_GELU_C * (1.0 + 3.0 * 0.044715 * (x * x)))


def _gm_forward_rows(zu, zv, gln, bln, wc_ref, bst, n_chunk):
    u, tu = _gelu(zu)
    a, ta = _gelu(zv)
    mu = jnp.mean(a, axis=-1, keepdims=True)
    ac = a - mu
    rs = lax.rsqrt(jnp.mean(ac * ac, axis=-1, keepdims=True) + EPS)
    n = ac * rs
    v = n * gln + bln
    vb = v.astype(BF16)
    rows = []
    for c in range(n_chunk):
        cols = []
        for g in range(GM_GROUPS):
            vc = vb[c * GM_CHUNK:(c + 1) * GM_CHUNK, g * LANES:(g + 1) * LANES]
            mixed = jnp.dot(wc_ref[g], vc, preferred_element_type=F32) + bst[g]
            cols.append(mixed)
        rows.append(jnp.concatenate(cols, axis=1))
    mixed = jnp.concatenate(rows, axis=0) if n_chunk > 1 else rows[0]
    return u, tu, ta, n, rs, v, mixed


def _gm_fwd(z, gln, bln, wc, bst, name):
    rows = z.shape[0]
    tr = _pick(rows, 512, GM_CHUNK)
    n_chunk = tr // GM_CHUNK

    def body(zu_ref, zv_ref, gln_ref, bln_ref, wc_ref, bst_ref, o_ref):
        u, _, _, _, _, _, mixed = _gm_forward_rows(zu_ref[...].astype(F32), zv_ref[...].astype(F32), gln_ref[...], bln_ref[...], wc_ref,
                                                   bst_ref, n_chunk)
        o_ref[...] = (u * mixed).astype(BF16)

    return pl.pallas_call(
        body, name=name, grid=(rows // tr,),
        in_specs=[_rowspec(tr, GM_WIDTH, Z_GM // GM_WIDTH), _rowspec(tr, GM_WIDTH, Z_GM // GM_WIDTH + 1),_fullspec((1, GM_WIDTH)), _fullspec((1, GM_WIDTH)),
                  _fullspec((GM_GROUPS, GM_CHUNK, GM_CHUNK)), _fullspec((GM_GROUPS, GM_CHUNK, LANES))],
        out_specs=_rowspec(tr, GM_WIDTH), out_shape=jax.ShapeDtypeStruct((rows, GM_WIDTH), BF16),
        compiler_params=_params(("parallel",)),
    )(z, z, gln, bln, wc, bst)


ANY_SPEC = pl.BlockSpec(memory_space=pl.ANY)


def _gm_bwd(z, dy, gln, bln, wc, wct, bst, dz, name):
    rows = z.shape[0]
    tr = _pick(rows, 512, GM_CHUNK)
    n_chunk = tr // GM_CHUNK

    def body(zu_ref, zv_ref, dy_ref, gln_ref, bln_ref, wc_ref, wct_ref, bst_ref, _, dz_ref, dws_ref, dbs_ref, dgl_ref,
             dbl_ref):
        first = pl.program_id(0) == 0
        zu, zv, gln = zu_ref[...].astype(F32), zv_ref[...].astype(F32), gln_ref[...]
        u, tu, ta, n, rs, v, mixed = _gm_forward_rows(zu, zv, gln, bln_ref[...], wc_ref, bst_ref, n_chunk)
        dyv = dy_ref[...].astype(F32)
        dzu = dyv * mixed * _gelu_grad(zu, tu)
        dmix = dyv * u
        dmb = dmix.astype(BF16)
        vb = v.astype(BF16)
        dv_rows, dws, dbs = [], [None] * GM_GROUPS, None
        for c in range(n_chunk):
            rsl = slice(c * GM_CHUNK, (c + 1) * GM_CHUNK)
            cols = []
            for g in range(GM_GROUPS):
                csl = slice(g * LANES, (g + 1) * LANES)
                dmc = dmb[rsl, csl]
                cols.append(jnp.dot(wct_ref[g], dmc, preferred_element_type=F32))
                w_part = lax.dot_general(dmc, vb[rsl, csl], (((1,), (1,)), ((), ())), preferred_element_type=F32)
                dws[g] = w_part if dws[g] is None else dws[g] + w_part
            dv_rows.append(jnp.concatenate(cols, axis=1))
            dbs = dmix[rsl, :] if dbs is None else dbs + dmix[rsl, :]
        dv = jnp.concatenate(dv_rows, axis=0) if n_chunk > 1 else dv_rows[0]
        dn = dv * gln
        da = rs * (dn - jnp.mean(dn, axis=-1, keepdims=True) - n * jnp.mean(dn * n, axis=-1, keepdims=True))
        dzv = da * _gelu_grad(zv, ta)
        dz_ref[:, 0:GM_WIDTH] = dzu.astype(BF16)
        dz_ref[:, GM_WIDTH:2 * GM_WIDTH] = dzv.astype(BF16)
        _acc_rows(dgl_ref, dv * n, first)
        _acc_rows(dbl_ref, dv, first)

        @pl.when(first)
        def _():
            for g in range(GM_GROUPS):
                dws_ref[g] = dws[g]
            dbs_ref[...] = dbs

        @pl.when(jnp.logical_not(first))
        def _():
            for g in range(GM_GROUPS):
                dws_ref[g] += dws[g]
            dbs_ref[...] += dbs

    wspec = _fullspec((GM_GROUPS, GM_CHUNK, GM_CHUNK))
    return pl.pallas_call(
        body, name=name, grid=(rows // tr,),
        in_specs=[_rowspec(tr, GM_WIDTH, Z_GM // GM_WIDTH), _rowspec(tr, GM_WIDTH, Z_GM // GM_WIDTH + 1),
                  _rowspec(tr, GM_WIDTH), _fullspec((1, GM_WIDTH)), _fullspec((1, GM_WIDTH)), wspec, wspec, wspec, ANY_SPEC],
        out_specs=[_rowspec(tr, 2 * GM_WIDTH, Z_GM // (2 * GM_WIDTH)), wspec, _fullspec((GM_CHUNK, GM_WIDTH)),
                   _fullspec((1, GM_WIDTH)), _fullspec((1, GM_WIDTH))],
        out_shape=[jax.ShapeDtypeStruct(dz.shape, dz.dtype), jax.ShapeDtypeStruct((GM_GROUPS, GM_CHUNK, GM_CHUNK), F32),
                   jax.ShapeDtypeStruct((GM_CHUNK, GM_WIDTH), F32), jax.ShapeDtypeStruct((1, GM_WIDTH), F32),
                   jax.ShapeDtypeStruct((1, GM_WIDTH), F32)],
        input_output_aliases={8: 0}, compiler_params=_params(("arbitrary",)),
    )(z, z, dy, gln, bln, wc, wct, bst, dz)


def _lat_fwd(z, g_cq, g_ckv, name):
    rows = z.shape[0]
    tr = _row_tile(rows, 4 * MLA_W)

    def body(z_ref, gq_ref, gkv_ref, nq_ref, nkv_ref):
        zb = z_ref[...]
        cq, ckv = zb[:, 0:Q_LORA], zb[:, Q_LORA:Q_LORA + KV_LORA]
        nq_ref[...] = (cq * _rms(cq, Q_LORA) * gq_ref[...]).astype(BF16)
        nkv_ref[...] = (ckv * _rms(ckv, KV_LORA) * gkv_ref[...]).astype(BF16)

    return pl.pallas_call(
        body, name=name, grid=(rows // tr,),
        in_specs=[_rowspec(tr, MLA_W, Z_MLA // MLA_W), _fullspec((1, Q_LORA)), _fullspec((1, KV_LORA))],
        out_specs=[_rowspec(tr, Q_LORA), _rowspec(tr, KV_LORA)],
        out_shape=[jax.ShapeDtypeStruct((rows, Q_LORA), BF16), jax.ShapeDtypeStruct((rows, KV_LORA), BF16)],
        compiler_params=_params(("parallel",)),
    )(z, g_cq, g_ckv)


def _lat_bwd(z, dnq, dnkv, dkpe, g_cq, g_ckv, dz, name):
    rows = z.shape[0]
    tr = _row_tile(rows, 8 * MLA_W)

    def body(z_ref, dnq_ref, dnkv_ref, dkpe_ref, gq_ref, gkv_ref, _, dz_ref, dgq_ref, dgkv_ref):
        first = pl.program_id(0) == 0
        zb = z_ref[...]
        dcq, dgq = _rms_bwd_rows(zb[:, 0:Q_LORA], gq_ref[...], dnq_ref[...], Q_LORA)
        dckv, dgkv = _rms_bwd_rows(zb[:, Q_LORA:Q_LORA + KV_LORA], gkv_ref[...], dnkv_ref[...], KV_LORA)
        dz_ref[:, 0:Q_LORA] = dcq.astype(BF16)
        dz_ref[:, Q_LORA:Q_LORA + KV_LORA] = dckv.astype(BF16)
        dz_ref[:, Q_LORA + KV_LORA:MLA_W] = dkpe_ref[...].astype(BF16)
        _acc_rows(dgq_ref, dgq, first)
        _acc_rows(dgkv_ref, dgkv, first)

    return pl.pallas_call(
        body, name=name, grid=(rows // tr,),
        in_specs=[_rowspec(tr, MLA_W, Z_MLA // MLA_W), _rowspec(tr, Q_LORA), _rowspec(tr, KV_LORA), _rowspec(tr, LANES),
                  _fullspec((1, Q_LORA)), _fullspec((1, KV_LORA)), ANY_SPEC],
        out_specs=[_rowspec(tr, MLA_W, Z_MLA // MLA_W), _fullspec((1, Q_LORA)), _fullspec((1, KV_LORA))],
        out_shape=[jax.ShapeDtypeStruct(dz.shape, dz.dtype), jax.ShapeDtypeStruct((1, Q_LORA), F32),
                   jax.ShapeDtypeStruct((1, KV_LORA), F32)],
        input_output_aliases={6: 0}, compiler_params=_params(("arbitrary",)),
    )(z, dnq, dnkv, dkpe, g_cq, g_ckv, dz)


def _rope(y, cc, ss):
    return y * cc + pltpu.roll(y, 64, 1) * ss


def _rope_bwd(d, cc, ss):
    return d * cc + pltpu.roll(d * ss, 64, 1)


def _qk_fwd(q, kv, z, cc, ss, gqn, gqp, gkn, gkp, name):
    rows = q.shape[0]
    W = MLA_HEADS * HEAD
    tr = _row_tile(rows, 20 * W)
    QS = MLA_SCALE * LOG2E

    def body(q_ref, kv_ref, kpe_ref, cc_ref, ss_ref, gqn_ref, gqp_ref, gkn_ref, gkp_ref, qc_ref, kc_ref, v_ref):
        cc, ss = cc_ref[...], ss_ref[...]
        kpe = kpe_ref[...]
        kp = _rope(kpe * _rms(kpe, MLA_ROPE) * gkp_ref[...], cc, ss).astype(BF16)
        for h in range(MLA_HEADS):
            qn = q_ref[:, h * HEAD:(h + 1) * HEAD]
            qp = q_ref[:, W + h * HEAD:W + (h + 1) * HEAD]
            kn = kv_ref[:, h * HEAD:(h + 1) * HEAD]
            qc_ref[:, h * QCAT:h * QCAT + HEAD] = (qn * _rms(qn, HEAD) * gqn_ref[...] * QS).astype(BF16)
            qc_ref[:, h * QCAT + HEAD:(h + 1) * QCAT] = (_rope(qp * _rms(qp, MLA_ROPE) * gqp_ref[...], cc, ss) * QS).astype(BF16)
            kc_ref[:, h * QCAT:h * QCAT + HEAD] = (kn * _rms(kn, HEAD) * gkn_ref[...]).astype(BF16)
            kc_ref[:, h * QCAT + HEAD:(h + 1) * QCAT] = kp
        v_ref[...] = kv_ref[:, W:2 * W].astype(BF16)

    g = _fullspec((1, HEAD))
    return _call(
        body, None, [q, kv, z, cc, ss, gqn, gqp, gkn, gkp], name=name, grid=(rows // tr,),
        in_specs=[_rowspec(tr, 2 * W), _rowspec(tr, 2 * W), _rowspec(tr, LANES, Z_KPE // LANES), _rowspec(tr, LANES),
                  _rowspec(tr, LANES), g, g, g, g],
        out_specs=[_rowspec(tr, MLA_HEADS * QCAT), _rowspec(tr, MLA_HEADS * QCAT), _rowspec(tr, W)],
        out_shape=[jax.ShapeDtypeStruct((rows, MLA_HEADS * QCAT), BF16), jax.ShapeDtypeStruct((rows, MLA_HEADS * QCAT), BF16),
                   jax.ShapeDtypeStruct((rows, W), BF16)],
        scratch_shapes=[], sem=("parallel",))


def _qk_bwd(q, kv, z, cc, ss, gqn, gqp, gkn, gkp, dqc, dkc, dv, name):
    rows = q.shape[0]
    W = MLA_HEADS * HEAD
    tr = _row_tile(rows, 40 * W)

    def body(q_ref, kv_ref, kpe_ref, cc_ref, ss_ref, gqn_ref, gqp_ref, gkn_ref, gkp_ref, dqc_ref, dkc_ref, dv_ref,
             dq_ref, dkv_ref, dkpe_ref, dgqn_ref, dgqp_ref, dgkn_ref, dgkp_ref):
        first = pl.program_id(0) == 0
        cc, ss = cc_ref[...], ss_ref[...]
        sqn = sqp = skn = dkp = None
        for h in range(MLA_HEADS):
            dx, dg = _rms_bwd_rows(q_ref[:, h * HEAD:(h + 1) * HEAD], gqn_ref[...], dqc_ref[:, h * QCAT:h * QCAT + HEAD], HEAD)
            dq_ref[:, h * HEAD:(h + 1) * HEAD] = dx.astype(BF16)
            sqn = dg if sqn is None else sqn + dg
            dy = _rope_bwd(dqc_ref[:, h * QCAT + HEAD:(h + 1) * QCAT], cc, ss)
            dx, dg = _rms_bwd_rows(q_ref[:, W + h * HEAD:W + (h + 1) * HEAD], gqp_ref[...], dy, MLA_ROPE)
            dq_ref[:, W + h * HEAD:W + (h + 1) * HEAD] = dx.astype(BF16)
            sqp = dg if sqp is None else sqp + dg
            dx, dg = _rms_bwd_rows(kv_ref[:, h * HEAD:(h + 1) * HEAD], gkn_ref[...], dkc_ref[:, h * QCAT:h * QCAT + HEAD], HEAD)
            dkv_ref[:, h * HEAD:(h + 1) * HEAD] = dx.astype(BF16)
            skn = dg if skn is None else skn + dg
            part = dkc_ref[:, h * QCAT + HEAD:(h + 1) * QCAT].astype(F32)
            dkp = part if dkp is None else dkp + part
        dkv_ref[:, W:2 * W] = dv_ref[...].astype(BF16)
        dx, dg = _rms_bwd_rows(kpe_ref[...], gkp_ref[...], _rope_bwd(dkp, cc, ss), MLA_ROPE)
        dkpe_ref[...] = dx
        _acc_rows(dgqn_ref, sqn, first)
        _acc_rows(dgqp_ref, sqp, first)
        _acc_rows(dgkn_ref, skn, first)
        _acc_rows(dgkp_ref, dg, first)

    g = _fullspec((1, HEAD))
    gs = jax.ShapeDtypeStruct((1, HEAD), F32)
    return pl.pallas_call(
        body, name=name, grid=(rows // tr,),
        in_specs=[_rowspec(tr, 2 * W), _rowspec(tr, 2 * W), _rowspec(tr, LANES, Z_KPE // LANES), _rowspec(tr, LANES),
                  _rowspec(tr, LANES), g, g, g, g, _rowspec(tr, MLA_HEADS * QCAT), _rowspec(tr, MLA_HEADS * QCAT),
                  _rowspec(tr, W)],
        out_specs=[_rowspec(tr, 2 * W), _rowspec(tr, 2 * W), _rowspec(tr, LANES), g, g, g, g],
        out_shape=[jax.ShapeDtypeStruct((rows, 2 * W), BF16), jax.ShapeDtypeStruct((rows, 2 * W), BF16),
                   jax.ShapeDtypeStruct((rows, LANES), F32), gs, gs, gs, gs],
        compiler_params=_params(("arbitrary",)),
    )(q, kv, z, cc, ss, gqn, gqp, gkn, gkp, dqc, dkc, dv)


def _headnorm_fwd(x, col, nheads, g, out_scale, name):
    rows = x.shape[0]
    W = nheads * HEAD
    tr = _row_tile(rows, 6 * W)

    def body(x_ref, g_ref, o_ref):
        for h in range(nheads):
            xv = x_ref[:, h * HEAD:(h + 1) * HEAD]
            o_ref[:, h * HEAD:(h + 1) * HEAD] = (xv * _rms(xv, HEAD) * g_ref[...] * out_scale).astype(BF16)

    return pl.pallas_call(
        body, name=name, grid=(rows // tr,),
        in_specs=[_rowspec(tr, W, col), _fullspec((1, HEAD))], out_specs=_rowspec(tr, W),
        out_shape=jax.ShapeDtypeStruct((rows, W), BF16), compiler_params=_params(("parallel",)),
    )(x, g)


def _headnorm_bwd(x, col, nheads, g, dy, tail, name, into=None):
    rows = x.shape[0]
    W = nheads * HEAD
    tr = _row_tile(rows, 12 * W)
    has_tail = tail is not None
    WO = 2 * W if has_tail else W

    def body(*refs):
        if into is not None:
            x_ref, g_ref, dy_ref, _, dx_ref, dg_ref = refs
        elif has_tail:
            x_ref, g_ref, dy_ref, t_ref, dx_ref, dg_ref = refs
        else:
            x_ref, g_ref, dy_ref, dx_ref, dg_ref = refs
        acc = None
        for h in range(nheads):
            sl = slice(h * HEAD, (h + 1) * HEAD)
            dx, dg = _rms_bwd_rows(x_ref[:, sl], g_ref[...], dy_ref[:, sl], HEAD)
            dx_ref[:, sl] = dx.astype(BF16)
            acc = dg if acc is None else acc + dg
        if has_tail:
            dx_ref[:, W:2 * W] = t_ref[...].astype(BF16)
        _acc_rows(dg_ref, acc, pl.program_id(0) == 0)

    ins = [x, g, dy] + ([tail] if has_tail else [])
    specs = [_rowspec(tr, W, col), _fullspec((1, HEAD)), _rowspec(tr, W)] + ([_rowspec(tr, W)] if has_tail else [])
    dx_spec, dx_shape, aliases = _rowspec(tr, WO), jax.ShapeDtypeStruct((rows, WO), BF16), {}
    if into is not None:
        assert not has_tail
        ins, specs = ins + [into[0]], specs + [ANY_SPEC]
        dx_spec, dx_shape, aliases = _rowspec(tr, W, into[1]), jax.ShapeDtypeStruct(into[0].shape, into[0].dtype), {3: 0}
    return pl.pallas_call(
        body, name=name, grid=(rows // tr,), in_specs=specs,
        out_specs=[dx_spec, _fullspec((1, HEAD))], out_shape=[dx_shape, jax.ShapeDtypeStruct((1, HEAD), F32)],
        input_output_aliases=aliases, compiler_params=_params(("arbitrary",)),
    )(*ins)


def _sigmoid(x):
    return 1.0 / (1.0 + jnp.exp(-x.astype(F32)))


def _merge_fwd(z, y_gm, y_mla, y_mem, name):
    rows = z.shape[0]
    tr = _row_tile(rows, 14 * D_MODEL)

    def body(g0_ref, g1_ref, g2_ref, a_ref, b_ref, c_ref, o_ref):
        m = _sigmoid(g0_ref[...]) * a_ref[...] + _sigmoid(g1_ref[...]) * b_ref[...] + _sigmoid(g2_ref[...]) * c_ref[...]
        o_ref[...] = m.astype(BF16)

    r = _rowspec(tr, D_MODEL)
    return pl.pallas_call(
        body, name=name, grid=(rows // tr,),
        in_specs=[_rowspec(tr, D_MODEL, 0), _rowspec(tr, D_MODEL, 1), _rowspec(tr, D_MODEL, 2),r, r, r],
        out_specs=r, out_shape=jax.ShapeDtypeStruct((rows, D_MODEL), BF16), compiler_params=_params(("parallel",)),
    )(z, z, z, y_gm, y_mla, y_mem)


def _merge_bwd(z, y_gm, y_mla, y_mem, dm, name):
    rows = z.shape[0]
    tr = _row_tile(rows, 24 * D_MODEL)

    def body(g0_ref, g1_ref, g2_ref, a_ref, b_ref, c_ref, dm_ref, da_ref, db_ref, dc_ref, dzg_ref):
        dmv = dm_ref[...].astype(F32)
        for k, (g_ref, y_ref, dy_ref) in enumerate(((g0_ref, a_ref, da_ref), (g1_ref, b_ref, db_ref), (g2_ref, c_ref, dc_ref))):
            s = _sigmoid(g_ref[...])
            dy_ref[...] = (dmv * s).astype(BF16)
            dzg_ref[:, k * D_MODEL:(k + 1) * D_MODEL] = (dmv * y_ref[...] * s * (1.0 - s)).astype(BF16)

    r = _rowspec(tr, D_MODEL)
    o = jax.ShapeDtypeStruct((rows, D_MODEL), BF16)
    return pl.pallas_call(
        body, name=name, grid=(rows // tr,),
        in_specs=[_rowspec(tr, D_MODEL, 0), _rowspec(tr, D_MODEL, 1), _rowspec(tr, D_MODEL, 2),r, r, r, r],
        out_specs=[r, r, r, _rowspec(tr, 3 * D_MODEL, 0)],
        out_shape=[o, o, o, jax.ShapeDtypeStruct((rows, Z_COLS), BF16)],
        compiler_params=_params(("parallel",)),
    )(z, z, z, y_gm, y_mla, y_mem, dm)


_NT = (((1,), (1,)), ((), ()))
_TN = (((0,), (0,)), ((), ()))


def _diag_mask(s):
    row = lax.broadcasted_iota(jnp.int32, s.shape, 0)
    col = lax.broadcasted_iota(jnp.int32, s.shape, 1)
    return jnp.where(row >= col, s, NEG)


def _attn_fwd(q, k, v, nb, nheads, dk, v_col0, causal, name, rider=None):
    S, Skv = q.shape[0] // nb, k.shape[0] // nb
    tq = _pick(Skv, ATT_TILE) if causal else _pick(S, 4 * ATT_TILE)
    nq = S // tq

    def body(q_ref, k_ref, v_ref, o_ref, lse_ref):
        for i in range(nq):
            r0 = i * tq
            qb = q_ref[r0:r0 + tq, :]
            if causal:
                spans = ([(0, r0, False)] if i > 0 else []) + [(r0, r0 + tq, True)]
            else:
                spans = [(0, Skv, False)]
            scores = []
            for a, b, masked in spans:
                s = lax.dot_general(qb, k_ref[a:b, :], _NT, preferred_element_type=F32)
                scores.append(_diag_mask(s) if masked else s)
            m = functools.reduce(jnp.maximum, [jnp.max(s, axis=-1, keepdims=True) for s in scores])
            l = acc = None
            for s, (a, b, _) in zip(scores, spans):
                p = jnp.exp2(s - m)
                lp = jnp.sum(p, axis=-1, keepdims=True)
                ap = jnp.dot(p.astype(BF16), v_ref[a:b, :].astype(BF16), preferred_element_type=F32)
                l, acc = (lp, ap) if l is None else (l + lp, acc + ap)
            o_ref[r0:r0 + tq, :] = (acc / l).astype(BF16)
            lse_ref[r0:r0 + tq, :] = m + jnp.log2(l)

    ins = [q, k, v]
    in_specs = [pl.BlockSpec((S, dk), lambda b, h: (b, h)), pl.BlockSpec((Skv, dk), lambda b, h: (b, h)),
                pl.BlockSpec((Skv, HEAD), lambda b, h: (b, v_col0 + h))]
    out_specs = [pl.BlockSpec((S, HEAD), lambda b, h: (b, h)), pl.BlockSpec((None, S, 1), lambda b, h: (h, b, 0))]
    out_shape = [jax.ShapeDtypeStruct((nb * S, nheads * HEAD), BF16), jax.ShapeDtypeStruct((nheads, nb * S, 1), F32)]
    return _call(body, rider, ins, name=name, grid=(nb, nheads), in_specs=in_specs, out_specs=out_specs,
                 out_shape=out_shape, scratch_shapes=[], sem=("parallel", "parallel"))


def _attn_bwd(q, k, v, o, do, lse, nb, nheads, dk, v_col0, scale, causal, name, rider=None):
    S, Skv = q.shape[0] // nb, k.shape[0] // nb
    tk = _pick(Skv, ATT_TILE)
    nkv = Skv // tk

    def body(q_ref, k_ref, v_ref, o_ref, do_ref, lse_ref, dq_ref, dk_ref, dv_ref, delta_ref, dob_ref, dqa_ref):
        dov = do_ref[...]
        delta_ref[...] = jnp.sum(o_ref[...].astype(F32) * dov.astype(F32), axis=-1, keepdims=True)
        dob_ref[...] = dov.astype(BF16)

        for j in range(nkv):
            c0 = j * tk
            kb = k_ref[c0:c0 + tk, :]
            vb = v_ref[c0:c0 + tk, :].astype(BF16)
            if causal:
                spans = [(c0, c0 + tk, True)] + ([(c0 + tk, S, False)] if c0 + tk < S else [])
            else:
                spans = [(0, S, False)]
            dk_acc = dv_acc = None
            for a, b, masked in spans:
                qb = q_ref[a:b, :]
                dob = dob_ref[a:b, :]
                s = lax.dot_general(qb, kb, _NT, preferred_element_type=F32)
                if masked:
                    s = _diag_mask(s)
                p = jnp.exp2(s - lse_ref[a:b, :])
                dp = lax.dot_general(dob, vb, _NT, preferred_element_type=F32)
                ds = (p * (dp - delta_ref[a:b, :])).astype(BF16)
                dv_p = lax.dot_general(p.astype(BF16), dob, _TN, preferred_element_type=F32)
                dk_p = lax.dot_general(ds, qb, _TN, preferred_element_type=F32)
                dk_acc, dv_acc = (dk_p, dv_p) if dk_acc is None else (dk_acc + dk_p, dv_acc + dv_p)
                dq_p = jnp.dot(ds, kb, preferred_element_type=F32) * scale
                if j == 0:
                    dqa_ref[a:b, :] = dq_p
                else:
                    dqa_ref[a:b, :] += dq_p
            dk_ref[c0:c0 + tk, :] = (dk_acc * LN2).astype(BF16)
            dv_ref[c0:c0 + tk, :] = dv_acc.astype(BF16)
        dq_ref[...] = dqa_ref[...].astype(BF16)

    ins = [q, k, v, o, do, lse]
    in_specs = [pl.BlockSpec((S, dk), lambda b, h: (b, h)), pl.BlockSpec((Skv, dk), lambda b, h: (b, h)),
                pl.BlockSpec((Skv, HEAD), lambda b, h: (b, v_col0 + h)), pl.BlockSpec((S, HEAD), lambda b, h: (b, h)),
                pl.BlockSpec((S, HEAD), lambda b, h: (b, h)), pl.BlockSpec((None, S, 1), lambda b, h: (h, b, 0))]
    out_specs = [pl.BlockSpec((S, dk), lambda b, h: (b, h)), pl.BlockSpec((Skv, dk), lambda b, h: (b, h)),
                 pl.BlockSpec((Skv, HEAD), lambda b, h: (b, h))]
    out_shape = [jax.ShapeDtypeStruct((nb * S, nheads * dk), BF16), jax.ShapeDtypeStruct((nb * Skv, nheads * dk), BF16),
                 jax.ShapeDtypeStruct((nb * Skv, nheads * HEAD), BF16)]
    return _call(body, rider, ins, name=name, grid=(nb, nheads), in_specs=in_specs, out_specs=out_specs,
                 out_shape=out_shape,
                 scratch_shapes=[pltpu.VMEM((S, 1), F32), pltpu.VMEM((S, HEAD), BF16), pltpu.VMEM((S, dk), F32)],
                 sem=("parallel", "parallel"))


def _spread_rope(a):
    zero = jnp.zeros(a.shape[:-1] + (32,), a.dtype)
    return jnp.concatenate([a[..., :32], zero, a[..., 32:], zero], axis=-1)


def _gather_rope(a):
    return jnp.concatenate([a[..., 0:32], a[..., 64:96]], axis=-1)


def _win_layout(w):
    return jnp.concatenate([w[:, C_ZG:C_END], w[:, C_ZU:C_CQ], w[:, C_QM:C_ZG], w[:, C_CQ:C_CKV], w[:, C_CKV:C_KPE],
                            _spread_rope(w[:, C_KPE:C_QM])], axis=1)


def _win_unlayout(d):
    return jnp.concatenate([d[:, Z_GM:Z_QM], d[:, Z_MLA:Z_MLA + Q_LORA], d[:, Z_MLA + Q_LORA:Z_KPE],
                            _gather_rope(d[:, Z_KPE:Z_COLS]), d[:, Z_QM:Z_MLA], d[:, 0:Z_GM]], axis=1)


def _wuq_layout(w):
    r = w.reshape(Q_LORA, MLA_HEADS, HEAD + MLA_ROPE)
    return jnp.concatenate([r[:, :, :HEAD].reshape(Q_LORA, -1), _spread_rope(r[:, :, HEAD:]).reshape(Q_LORA, -1)], axis=1)


def _wuq_unlayout(d):
    n = d[:, :MLA_HEADS * HEAD].reshape(Q_LORA, MLA_HEADS, HEAD)
    p = _gather_rope(d[:, MLA_HEADS * HEAD:].reshape(Q_LORA, MLA_HEADS, HEAD))
    return jnp.concatenate([n, p], axis=-1).reshape(Q_LORA, -1)


def _wukv_layout(w):
    r = w.reshape(KV_LORA, MLA_HEADS, 2 * HEAD)
    return jnp.concatenate([r[:, :, :HEAD].reshape(KV_LORA, -1), r[:, :, HEAD:].reshape(KV_LORA, -1)], axis=1)


def _wukv_unlayout(d):
    k = d[:, :MLA_HEADS * HEAD].reshape(KV_LORA, MLA_HEADS, HEAD)
    v = d[:, MLA_HEADS * HEAD:].reshape(KV_LORA, MLA_HEADS, HEAD)
    return jnp.concatenate([k, v], axis=-1).reshape(KV_LORA, -1)


AG_MID = ['w_uq', 'w_ukv', 'w_mem_kv', 'w_o_gm', 'w_o_mla', 'w_o_mem', 'w_out']
AG_FFN = ['w_ff1', 'w_ff2']
RS_GROUPS = {'ffn_proj': ['w_ff2', 'w_ff1', 'w_out', 'w_o_gm', 'w_o_mla', 'w_o_mem'],
             'lat': ['w_uq', 'w_ukv', 'w_mem_kv'], 'in_top': ['w_in'], 'in_bot': ['w_in']}


def _unride(res, rider):
    return (res, None) if rider is None else res


def _local_step(x, mem, positions, target, P, ws):
    B, S, _ = x.shape
    M = mem.shape[1]
    T = B * S
    x2d = x.reshape(T, D_MODEL)
    mem2d = mem.reshape(B * M, D_MODEL)
    tgt2d = target.reshape(T, D_MODEL)

    def row(v):
        return v.reshape(1, -1).astype(F32)

    inv_freq = ROPE_BASE ** (-jnp.arange(0, MLA_ROPE, 2, dtype=F32) / MLA_ROPE)
    zero = jnp.zeros_like(inv_freq)
    ang = positions.reshape(T).astype(F32)[:, None] * jnp.concatenate([inv_freq, zero, inv_freq, zero])
    cc = jnp.cos(ang) * jnp.concatenate([zero + 1.0, zero, zero + 1.0, zero])
    ss = jnp.sin(ang) * jnp.concatenate([zero - 1.0, zero, zero + 1.0, zero])

    g_mix, g_cq, g_ckv, g_ffn, g_mem = row(P['g_mix']), row(P['g_cq']), row(P['g_ckv']), row(P['g_ffn']), row(P['g_mem'])
    gqn, gkn, gmq, gmk = row(P['g_q_nope']), row(P['g_k_nope']), row(P['g_mq']), row(P['g_mk'])
    gqp, gkp = _spread_rope(row(P['g_q_pe'])), _spread_rope(row(P['g_k_pe']))
    gln, bln = row(P['g_gm_ln']), row(P['b_gm_ln'])
    wc = jnp.tril(P['w_spatial'].astype(F32))
    wct = jnp.swapaxes(wc, 1, 2).astype(BF16)
    wc = wc.astype(BF16)
    bst = jnp.broadcast_to(P['b_spatial'].astype(F32)[:, :, None], (GM_GROUPS, GM_CHUNK, LANES))

    ride = ws.gather(['w_in'])
    h, got = _unride(_rms_fwd(x2d, g_mix, "rms_mix", rider=ride), ride)
    w_in = _win_layout(ws.gathered(['w_in'], got)['w_in']).astype(BF16)
    ride = ws.gather(AG_MID)
    z, got = _unride(_matmul(h, w_in, 'nn', ACT, "mm_in", tn_t=1792, rider=ride), ride)
    mid = ws.gathered(AG_MID, got)
    w_uq, w_ukv = _wuq_layout(mid['w_uq']).astype(BF16), _wukv_layout(mid['w_ukv']).astype(BF16)
    w_mem_kv, w_o_gm, w_o_mla, w_o_mem, w_out = (mid[n] for n in ('w_mem_kv', 'w_o_gm', 'w_o_mla', 'w_o_mem', 'w_out'))
    ygm_pre = _gm_fwd(z, gln, bln, wc, bst, "gm_fwd")
    y_gm = _matmul(ygm_pre, w_o_gm, 'nn', ACT, "mm_o_gm")
    nq, nkv = _lat_fwd(z, g_cq, g_ckv, "lat_fwd")
    q = _matmul(nq, w_uq, 'nn', ACT, "mm_uq")
    kv = _matmul(nkv, w_ukv, 'nn', ACT, "mm_ukv")
    qcat, kcat, vv = _qk_fwd(q, kv, z, cc, ss, gqn, gqp, gkn, gkp, "qk_fwd")
    ride = ws.gather(AG_FFN)
    (o, lse), got = _unride(_attn_fwd(qcat, kcat, vv, B, MLA_HEADS, QCAT, 0, True, "mla_attn_fwd", rider=ride), ride)
    ffn = ws.gathered(AG_FFN, got)
    w_ff1, w_ff2 = ffn['w_ff1'], ffn['w_ff2']
    y_mla = _matmul(o, w_o_mla, 'nn', ACT, "mm_o_mla")
    nm = _rms_fwd(mem2d, g_mem, "rms_mem")
    kvm = _matmul(nm, w_mem_kv, 'nn', ACT, "mm_mem_kv")
    qm = _headnorm_fwd(z, Z_QM // (MEM_HEADS * HEAD), MEM_HEADS, gmq, MEM_SCALE * LOG2E, "memq_fwd")
    km = _headnorm_fwd(kvm, 0, MEM_HEADS, gmk, 1.0, "memk_fwd")
    om, lse_m = _attn_fwd(qm, km, kvm, B, MEM_HEADS, HEAD, MEM_HEADS, False, "mem_attn_fwd")
    y_mem = _matmul(om, w_o_mem, 'nn', ACT, "mm_o_mem")
    merged = _merge_fwd(z, y_gm, y_mla, y_mem, "merge_fwd")
    x1 = _matmul(merged, w_out, 'nn', F32, "mm_out", add=x2d)
    h2 = _rms_fwd(x1, g_ffn, "rms_ffn")
    a1 = _matmul(h2, w_ff1, 'nn', BF16, "mm_ff1")
    dx2, dx2b, loss_part = _matmul(a1, w_ff2, 'nn', F32, "mm_ff2", add=x1, relu2_a=True, sq_err_target=tgt2d)

    G = {}
    d_ff2 = _matmul(a1, dx2b, 'tn', BF16, "mm_d_ff2", relu2_a=True)
    da1 = _matmul(dx2b, w_ff2, 'nt', BF16, "mm_da1", relu2_grad=a1)
    d_ff1 = _matmul(h2, da1, 'tn', BF16, "mm_d_ff1", col_shards=N_DEV)
    dh2 = _matmul(da1, w_ff1, 'nt', ACT, "mm_dh2")
    dx1, dx1b, G['g_ffn'] = _rms_bwd(x1, g_ffn, dh2, dx2, "rms_ffn_bwd", dx_dtypes=(F32, BF16))
    d_out = _matmul(merged, dx1b, 'tn', BF16, "mm_d_out")
    dmerged = _matmul(dx1b, w_out, 'nt', ACT, "mm_dmerged")
    dy_gm, dy_mla, dy_mem, dz = _merge_bwd(z, y_gm, y_mla, y_mem, dmerged, "merge_bwd")
    d_o_gm = _matmul(ygm_pre, dy_gm, 'tn', BF16, "mm_d_o_gm")
    d_o_mla = _matmul(o, dy_mla, 'tn', BF16, "mm_d_o_mla")
    d_o_mem = _matmul(om, dy_mem, 'tn', BF16, "mm_d_o_mem")
    dygm_pre = _matmul(dy_gm, w_o_gm, 'nt', ACT, "mm_dygm")
    dz, dws, dbs, G['g_gm_ln'], G['b_gm_ln'] = _gm_bwd(z, dygm_pre, gln, bln, wc, wct, bst, dz, "gm_bwd")
    G['w_spatial'] = jnp.tril(dws)
    G['b_spatial'] = jnp.sum(dbs.reshape(GM_CHUNK, GM_GROUPS, LANES), axis=-1).T
    do = _matmul(dy_mla, w_o_mla, 'nt', ACT, "mm_do")
    ride = ws.scatter('ffn_proj', {'w_ff2': d_ff2, 'w_ff1': d_ff1, 'w_out': d_out, 'w_o_gm': d_o_gm, 'w_o_mla': d_o_mla,
                                   'w_o_mem': d_o_mem})
    (dqc, dkc, dvv), got = _unride(_attn_bwd(qcat, kcat, vv, o, do, lse, B, MLA_HEADS, QCAT, 0, MLA_SCALE, True,
                                             "mla_attn_bwd", rider=ride), ride)
    ws.scattered('ffn_proj', got)
    dq, dkv, dkpe, G['g_q_nope'], dgqp, G['g_k_nope'], dgkp = _qk_bwd(q, kv, z, cc, ss, gqn, gqp, gkn, gkp, dqc, dkc, dvv,
                                                                     "qk_bwd")
    G['g_q_pe'], G['g_k_pe'] = _gather_rope(dgqp), _gather_rope(dgkp)
    d_uq = _wuq_unlayout(_matmul(nq, dq, 'tn', BF16, "mm_d_uq"))
    dnq = _matmul(dq, w_uq, 'nt', ACT, "mm_dnq")
    d_ukv = _wukv_unlayout(_matmul(nkv, dkv, 'tn', BF16, "mm_d_ukv"))
    dnkv = _matmul(dkv, w_ukv, 'nt', ACT, "mm_dnkv")
    dz, G['g_cq'], G['g_ckv'] = _lat_bwd(z, dnq, dnkv, dkpe, g_cq, g_ckv, dz, "lat_bwd")
    dom = _matmul(dy_mem, w_o_mem, 'nt', ACT, "mm_dom")
    dqm, dkm, dvm = _attn_bwd(qm, km, kvm, om, dom, lse_m, B, MEM_HEADS, HEAD, MEM_HEADS, MEM_SCALE, False, "mem_attn_bwd")
    dz, G['g_mq'] = _headnorm_bwd(z, Z_QM // (MEM_HEADS * HEAD), MEM_HEADS, gmq, dqm, None, "memq_bwd",
                                  into=(dz, Z_QM // (MEM_HEADS * HEAD)))
    dkvm, G['g_mk'] = _headnorm_bwd(kvm, 0, MEM_HEADS, gmk, dkm, dvm, "memk_bwd")
    d_mem_kv = _matmul(nm, dkvm, 'tn', BF16, "mm_d_mem_kv")
    dnm = _matmul(dkvm, w_mem_kv, 'nt', ACT, "mm_dnm")
    G['g_mem'], = _rms_bwd(mem2d, g_mem, dnm, None, "rms_mem_bwd", dx_dtypes=())
    half = D_MODEL // 2
    ride = ws.scatter('lat', {'w_uq': d_uq, 'w_ukv': d_ukv, 'w_mem_kv': d_mem_kv})
    d_top, got = _unride(_matmul(h, dz, 'tn', BF16, "mm_d_in_top", tn_t=1792, m_rows=(0, half), rider=ride), ride)
    ws.scattered('lat', got)
    ride = ws.scatter('in_top', {'w_in': _win_unlayout(d_top)})
    d_bot, got = _unride(_matmul(h, dz, 'tn', BF16, "mm_d_in_bot", tn_t=1792, m_rows=(half, half), rider=ride), ride)
    ws.scattered('in_top', got)
    ride = ws.scatter('in_bot', {'w_in': _win_unlayout(d_bot)})
    dh, got = _unride(_matmul(dz, w_in, 'nt', ACT, "mm_dh", rider=ride), ride)
    ws.scattered('in_bot', got)
    gx, G['g_mix'] = _rms_bwd(x2d, g_mix, dh, dx1, "rms_mix_bwd")
    return loss_part, gx.reshape(B, S, D_MODEL), G


def _all_gather8(xs, name):
    def body(x_ref, out_ref, send_sems, recv_sems, local_sem):
        x, y, c = lax.axis_index("x"), lax.axis_index("y"), lax.axis_index("c")
        me, sibling = (x, y, c), (x, y, 1 - c)
        chips = [(1 - x, y), (x, 1 - y), (1 - x, 1 - y)]

        def rows(px, py, pc):
            return out_ref.at[4 * px + 2 * py + pc]

        def copy(k, block, to, src=None):
            return pltpu.make_async_remote_copy(
                src_ref=rows(*block) if src is None else src, dst_ref=rows(*block),
                send_sem=send_sems.at[k], recv_sem=recv_sems.at[k], device_id=to, device_id_type=MESH)

        mine = pltpu.make_async_copy(x_ref, rows(*me), local_sem)
        mine.start()
        first = [copy(0, me, sibling, src=x_ref)]
        first += [copy(1 + j, me, (*chip, c), src=x_ref) for j, chip in enumerate(chips)]
        for cp in first:
            cp.start()
        passed = [copy(4 + j, (*chip, c), sibling) for j, chip in enumerate(chips)]
        for j, chip in enumerate(chips):
            copy(1 + j, (*chip, c), me).wait_recv()
            passed[j].start()
        copy(0, sibling, me).wait_recv()
        for j, chip in enumerate(chips):
            copy(4 + j, (*chip, 1 - c), me).wait_recv()
        for cp in first + passed:
            cp.wait_send()
        mine.wait()

    return pl.pallas_call(
        body, name=name, in_specs=[HBM_SPEC], out_specs=HBM_SPEC,
        out_shape=jax.ShapeDtypeStruct((N_DEV,) + xs.shape, xs.dtype),
        scratch_shapes=[pltpu.SemaphoreType.DMA((7,)), pltpu.SemaphoreType.DMA((7,)), pltpu.SemaphoreType.DMA],
    )(xs)


def _adamw_rows(w, g, m, v):
    m2 = ADAM_B1 * m + (1.0 - ADAM_B1) * g
    v2 = ADAM_B2 * v + (1.0 - ADAM_B2) * (g * g)
    m_hat = m2 / (1.0 - ADAM_B1 ** ADAM_STEP)
    v_hat = v2 / (1.0 - ADAM_B2 ** ADAM_STEP)
    delta = -ADAM_LR * (m_hat / (jnp.sqrt(v_hat) + ADAM_EPS) + ADAM_WD * w)
    return delta, m2, v2


def _sum_adamw(parts, w, m, v, name):
    rows, cols = w.shape
    assert sum(p.shape[1] for p in parts) == rows
    tr = _pick(min(p.shape[1] for p in parts), max(16, 65536 // cols), 16)
    n = parts[0].shape[0]
    counts = [p.shape[1] // tr for p in parts]
    starts = [sum(counts[:k]) for k in range(len(parts))]

    def body(*refs):
        p_refs = refs[:len(parts)]
        w_ref, m_ref, v_ref, g_ref, d_ref, m2_ref, v2_ref = refs[len(parts):]
        g = None
        for p_ref, start in zip(p_refs, starts):
            gk = p_ref[0].astype(F32)
            for k in range(1, n):
                gk = gk + p_ref[k].astype(F32)
            g = gk if g is None else jnp.where(pl.program_id(0) >= start, gk, g)
        delta, m2, v2 = _adamw_rows(w_ref[...], g, m_ref[...], v_ref[...])
        g_ref[...] = g
        d_ref[...] = delta
        m2_ref[...] = m2
        v2_ref[...] = v2

    flat = pl.BlockSpec((tr, cols), lambda i: (i, 0))
    out = jax.ShapeDtypeStruct((rows, cols), F32)
    p_specs = [pl.BlockSpec((n, tr, cols), lambda i, s=s, c=c: (0, jnp.clip(i - s, 0, c - 1), 0))
               for s, c in zip(starts, counts)]
    return pl.pallas_call(
        body, name=name, grid=(rows // tr,), in_specs=p_specs + [flat, flat, flat], out_specs=[flat] * 4,
        out_shape=[out] * 4, compiler_params=_params(("parallel",)),
    )(*parts, w, m, v)


SMALL_WIDTH = {'g_mix': 1024, 'g_cq': 384, 'g_ckv': 256, 'g_q_nope': 128, 'g_q_pe': 128, 'g_k_nope': 128, 'g_k_pe': 128,
               'g_gm_ln': 512, 'b_gm_ln': 512, 'g_mem': 1024, 'g_mq': 128, 'g_mk': 128, 'g_ffn': 1024}
NARROW = ('g_q_pe', 'g_k_pe')


def _small_layout():
    layout, r = {}, 0
    for name in SMALL + ['loss']:
        rows = {'w_spatial': GM_GROUPS * GM_CHUNK, 'b_spatial': GM_GROUPS, 'loss': 1}.get(name) or SMALL_WIDTH[name] // LANES
        layout[name] = (r, rows)
        r += -(-rows // 8) * 8
    return layout, r


def _small_pack(grads, loss_part, name):
    layout, total = _small_layout()
    names = SMALL + ['loss']

    def body(*refs):
        out_ref = refs[-1]
        out_ref[...] = jnp.zeros((total, LANES), F32)
        for ref, n in zip(refs[:-1], names):
            r0, rows = layout[n]
            if n == 'w_spatial':
                for g in range(GM_GROUPS):
                    out_ref[r0 + g * GM_CHUNK:r0 + (g + 1) * GM_CHUNK, :] = ref[g]
            elif n == 'b_spatial':
                out_ref[r0:r0 + rows, :] = ref[...]
            else:
                for k in range(rows):
                    out_ref[r0 + k:r0 + k + 1, :] = ref[:, k * LANES:(k + 1) * LANES]

    return pl.pallas_call(body, name=name, out_shape=jax.ShapeDtypeStruct((total, LANES), F32))(
        *[grads[n] for n in SMALL], loss_part)


def _small_adamw(parts, w, m, v, name):
    layout, _ = _small_layout()
    n_dev = parts.shape[0]

    def body(*refs):
        p_ref = refs[0]
        ins = refs[1:1 + 3 * len(SMALL)]
        outs = refs[1 + 3 * len(SMALL):-1]

        def gsum(r0, rows):
            g = p_ref[0, r0:r0 + rows, :]
            for d in range(1, n_dev):
                g = g + p_ref[d, r0:r0 + rows, :]
            return g

        def step(idx, g, at):
            w_ref, m_ref, v_ref = ins[3 * idx:3 * idx + 3]
            delta, m2, v2 = _adamw_rows(w_ref[at], g, m_ref[at], v_ref[at])
            for ref, val in zip(outs[4 * idx:4 * idx + 4], (g, delta, m2, v2)):
                ref[at] = val

        for idx, n in enumerate(SMALL):
            r0, rows = layout[n]
            if n == 'w_spatial':
                for g in range(GM_GROUPS):
                    step(idx, gsum(r0 + g * GM_CHUNK, GM_CHUNK), (0, g))
            elif n == 'b_spatial':
                step(idx, gsum(r0, rows), (0,))
            else:
                for k in range(rows):
                    step(idx, gsum(r0 + k, 1), (slice(None), slice(k * LANES, (k + 1) * LANES)))
        refs[-1][...] = gsum(layout['loss'][0], 8)

    flat_in = [d[n] for n in SMALL for d in (w, m, v)]
    out_shape = [jax.ShapeDtypeStruct(w[n].shape, F32) for n in SMALL for _ in range(4)]
    res = pl.pallas_call(body, name=name, out_shape=out_shape + [jax.ShapeDtypeStruct((8, LANES), F32)])(parts, *flat_in)
    groups = [{n: res[4 * i + j] for i, n in enumerate(SMALL)} for j in range(4)]
    return groups, res[-1]


def _full_from_gathered(gathered, name):
    r, c = BIG_SHAPE[name]
    if BIG_AXIS[name] == 0:
        return gathered.reshape(r, c)
    return gathered.transpose(1, 0, 2).reshape(r, c)


def _shards_of_full(g, name):
    if g.ndim == 3:
        return g
    r, c = BIG_SHAPE[name]
    if BIG_AXIS[name] == 0:
        return g.reshape(N_DEV, r // N_DEV, c)
    return g.reshape(g.shape[0], N_DEV, c // N_DEV).transpose(1, 0, 2)


class _DistWeights:
    def __init__(self, shards):
        self.shards = shards
        self.received = {}

    def gather(self, names):
        return _Gather2([self.shards[n].astype(BF16) for n in names])

    def gathered(self, names, got):
        return {n: _full_from_gathered(g, n) for n, g in zip(names, got)}

    def scatter(self, key, grads):
        return _Exchange([_shards_of_full(grads[n], n) for n in RS_GROUPS[key]], scatter=True)

    def scattered(self, key, got):
        for n, g in zip(RS_GROUPS[key], got):
            self.received.setdefault(n, []).append(g)


def kernel(x, mem, positions, g_mix, w_in, g_cq, w_uq, g_ckv, w_ukv, g_q_nope, g_q_pe, g_k_nope, g_k_pe, g_gm_ln, b_gm_ln, w_spatial, b_spatial, g_mem, w_mem_kv, g_mq, g_mk, w_o_gm, w_o_mla, w_o_mem, w_out, g_ffn, w_ff1, w_ff2, loss_target, m_g_mix, m_w_in, m_g_cq, m_w_uq, m_g_ckv, m_w_ukv, m_g_q_nope, m_g_q_pe, m_g_k_nope, m_g_k_pe, m_g_gm_ln, m_b_gm_ln, m_w_spatial, m_b_spatial, m_g_mem, m_w_mem_kv, m_g_mq, m_g_mk, m_w_o_gm, m_w_o_mla, m_w_o_mem, m_w_out, m_g_ffn, m_w_ff1, m_w_ff2, v_g_mix, v_w_in, v_g_cq, v_w_uq, v_g_ckv, v_w_ukv, v_g_q_nope, v_g_q_pe, v_g_k_nope, v_g_k_pe, v_g_gm_ln, v_b_gm_ln, v_w_spatial, v_b_spatial, v_g_mem, v_w_mem_kv, v_g_mq, v_g_mk, v_w_o_gm, v_w_o_mla, v_w_o_mem, v_w_out, v_g_ffn, v_w_ff1, v_w_ff2):
    given = dict(locals())
    w = {n: given[n][0] for n in WEIGHTS}
    mom = {n: given['m_' + n][0] for n in WEIGHTS}
    var = {n: given['v_' + n][0] for n in WEIGHTS}

    ws = _DistWeights({n: w[n] for n in BIG})
    loss_part, grad_x, G = _local_step(x, mem, positions, loss_target, {n: w[n] for n in SMALL}, ws)

    outs = {}
    for n in BIG:
        for prefix, res in zip(("grad_", "delta_", "new_m_", "new_v_"),
                               _sum_adamw(ws.received[n], w[n], mom[n], var[n], "adamw_" + n)):
            outs[prefix + n] = res[None]

    def widen(d):
        return {n: (jnp.pad(d[n], ((0, 0), (0, LANES - MLA_ROPE))) if n in NARROW else d[n]) for n in SMALL}

    parts = _all_gather8(_small_pack(widen(G), loss_part, "small_pack"), "ag_small")
    small, loss_rows = _small_adamw(parts, *[widen({n: given[prefix + n] for n in SMALL}) for prefix in ("", "m_", "v_")],
                                    "adamw_small")
    loss = 0.5 * jnp.sum(loss_rows) / D_MODEL
    for prefix, group in zip(("grad_", "delta_", "new_m_", "new_v_"), small):
        for n in SMALL:
            outs[prefix + n] = group[n][:, :MLA_ROPE] if n in NARROW else group[n]
    return (loss, grad_x, *[outs[p + n] for p in ("grad_", "delta_", "new_m_", "new_v_") for n in WEIGHTS])
```

```python
import functools
import math

import jax
import jax.numpy as jnp
from jax import lax
from jax.experimental import pallas as pl
from jax.experimental.pallas import tpu as pltpu

F32 = jnp.float32
BF16 = jnp.bfloat16
ACT = BF16

D_MODEL = 1024
MEM_HEADS = 4
HEAD = 128
GM_WIDTH = 512
GM_CHUNK = 128
GM_GROUPS = 4
MLA_HEADS = 8
MLA_ROPE = 64
Q_LORA = 384
KV_LORA = 256
D_FF = 4096
EPS = 1e-6
ROPE_BASE = 10000.0
MLA_SCALE = 1.0 / math.sqrt(HEAD + MLA_ROPE)
MEM_SCALE = 1.0 / math.sqrt(HEAD)
LOG2E = 1.4426950408889634
LN2 = 0.6931471805599453
ATT_TILE = 256
C_ZU, C_ZV, C_CQ, C_CKV, C_KPE, C_QM, C_ZG, C_END = 0, 512, 1024, 1408, 1664, 1728, 2240, 5312
Z_GM, Z_QM, Z_MLA, Z_KPE, Z_COLS = 3072, 4096, 4608, 5248, 5376
MLA_W = 768
QCAT = 2 * HEAD
ADAM_LR, ADAM_B1, ADAM_B2, ADAM_EPS, ADAM_WD, ADAM_STEP = 0.001, 0.9, 0.999, 1e-08, 0.01, 10
N_DEV = 8
LANES = 128
VMEM_LIMIT = 48 * 1024 * 1024
MAX_K_TILE = 8192
NEG = -1e30

BIG = ['w_in', 'w_uq', 'w_ukv', 'w_mem_kv', 'w_o_gm', 'w_o_mla', 'w_o_mem', 'w_out', 'w_ff1', 'w_ff2']
BIG_AXIS = {'w_in': 1, 'w_uq': 1, 'w_ukv': 1, 'w_mem_kv': 0, 'w_o_gm': 1, 'w_o_mla': 0, 'w_o_mem': 1,
            'w_out': 0, 'w_ff1': 1, 'w_ff2': 0}
BIG_SHAPE = {'w_in': (1024, 5312), 'w_uq': (384, 1536), 'w_ukv': (256, 2048), 'w_mem_kv': (1024, 1024),
             'w_o_gm': (512, 1024), 'w_o_mla': (1024, 1024), 'w_o_mem': (512, 1024), 'w_out': (1024, 1024),
             'w_ff1': (1024, 4096), 'w_ff2': (4096, 1024)}
SMALL = ['g_mix', 'g_cq', 'g_ckv', 'g_q_nope', 'g_q_pe', 'g_k_nope', 'g_k_pe', 'g_gm_ln', 'b_gm_ln',
         'w_spatial', 'b_spatial', 'g_mem', 'g_mq', 'g_mk', 'g_ffn']
WEIGHTS = ['g_mix', 'w_in', 'g_cq', 'w_uq', 'g_ckv', 'w_ukv', 'g_q_nope', 'g_q_pe', 'g_k_nope', 'g_k_pe',
           'g_gm_ln', 'b_gm_ln', 'w_spatial', 'b_spatial', 'g_mem', 'w_mem_kv', 'g_mq', 'g_mk', 'w_o_gm',
           'w_o_mla', 'w_o_mem', 'w_out', 'g_ffn', 'w_ff1', 'w_ff2']


def _pick(n, target, mult=LANES):
    best = None
    t = mult
    while t <= min(n, target):
        if n % t == 0:
            best = t
        t += mult
    return best if best is not None else n


def _params(sem):
    return pltpu.CompilerParams(dimension_semantics=sem, vmem_limit_bytes=VMEM_LIMIT)


MESH = pl.DeviceIdType.MESH
HBM_SPEC = pl.BlockSpec(memory_space=pltpu.HBM)


class _Exchange:
    def __init__(self, srcs, scatter):
        self.srcs, self.scatter = list(srcs), scatter
        self.out_shapes = [jax.ShapeDtypeStruct(s.shape if scatter else (N_DEV,) + s.shape, s.dtype) for s in self.srcs]
        n = len(self.srcs)
        self.scratch = [pltpu.SemaphoreType.DMA((n, N_DEV - 1)), pltpu.SemaphoreType.DMA((n, N_DEV - 1)),
                        pltpu.SemaphoreType.DMA((n,))]

    def _copies(self, src_refs, dst_refs, send_sems, recv_sems, local_sems):
        x, y, c = lax.axis_index("x"), lax.axis_index("y"), lax.axis_index("c")
        me = 4 * x + 2 * y + c
        local, remote = [], []
        for a, (src_ref, dst_ref) in enumerate(zip(src_refs, dst_refs)):
            def mine_for(dev, src_ref=src_ref):
                return src_ref.at[dev] if self.scatter else src_ref

            local.append(pltpu.make_async_copy(mine_for(me), dst_ref.at[me], local_sems.at[a]))
            for k in range(1, N_DEV):
                px = 1 - x if k & 4 else x
                py = 1 - y if k & 2 else y
                pc = 1 - c if k & 1 else c
                remote.append(pltpu.make_async_remote_copy(
                    src_ref=mine_for(4 * px + 2 * py + pc), dst_ref=dst_ref.at[me], send_sem=send_sems.at[a, k - 1],
                    recv_sem=recv_sems.at[a, k - 1], device_id=(px, py, pc), device_id_type=MESH))
        return local, remote

    def start(self, *refs):
        local, remote = self._copies(*refs)
        for cp in local + remote:
            cp.start()

    def forward(self, *refs):
        pass

    def finish(self, *refs):
        local, remote = self._copies(*refs)
        for cp in remote + local:
            cp.wait()


class _Gather2:
    def __init__(self, srcs):
        self.srcs = list(srcs)
        self.out_shapes = [jax.ShapeDtypeStruct((N_DEV,) + s.shape, s.dtype) for s in self.srcs]
        n = len(self.srcs)
        self.scratch = [pltpu.SemaphoreType.DMA((n, N_DEV - 1)), pltpu.SemaphoreType.DMA((n, N_DEV - 1)),
                        pltpu.SemaphoreType.DMA((n,))]

    def _plan(self, src_refs, dst_refs, send_sems, recv_sems, local_sems):
        x, y, c = lax.axis_index("x"), lax.axis_index("y"), lax.axis_index("c")
        chips = [(1 - x, y), (x, 1 - y), (1 - x, 1 - y)]
        plans = []
        for a, (src_ref, dst_ref) in enumerate(zip(src_refs, dst_refs)):
            def copy(k, block, to, src=None, a=a, dst_ref=dst_ref):
                at = dst_ref.at[4 * block[0] + 2 * block[1] + block[2]]
                return pltpu.make_async_remote_copy(src_ref=at if src is None else src, dst_ref=at,
                                                    send_sem=send_sems.at[a, k], recv_sem=recv_sems.at[a, k],
                                                    device_id=to, device_id_type=MESH)

            local = pltpu.make_async_copy(src_ref, dst_ref.at[4 * x + 2 * y + c], local_sems.at[a])
            first = [copy(0, (x, y, c), (x, y, 1 - c), src=src_ref)]
            first += [copy(1 + j, (x, y, c), (*chip, c), src=src_ref) for j, chip in enumerate(chips)]
            passed = [copy(4 + j, (*chip, c), (x, y, 1 - c)) for j, chip in enumerate(chips)]
            arrivals = [copy(1 + j, (*chip, c), (x, y, c)) for j, chip in enumerate(chips)]
            late = [copy(0, (x, y, 1 - c), (x, y, c))] + [copy(4 + j, (*chip, 1 - c), (x, y, c)) for j, chip in enumerate(chips)]
            plans.append((local, first, passed, arrivals, late))
        return plans

    def start(self, *refs):
        for local, first, _, _, _ in self._plan(*refs):
            local.start()
            for cp in first:
                cp.start()

    def forward(self, *refs):
        for _, _, passed, arrivals, _ in self._plan(*refs):
            for arrived, onward in zip(arrivals, passed):
                arrived.wait_recv()
                onward.start()

    def finish(self, *refs):
        for local, first, passed, _, late in self._plan(*refs):
            for cp in late:
                cp.wait_recv()
            for cp in first + passed:
                cp.wait_send()
            local.wait()


def _call(body, rider, ins, *, name, grid, in_specs, out_specs, out_shape, scratch_shapes, sem):
    if rider is None:
        return pl.pallas_call(body, name=name, grid=grid, in_specs=in_specs, out_specs=out_specs, out_shape=out_shape,
                              scratch_shapes=scratch_shapes, compiler_params=_params(sem))(*ins)
    single = not isinstance(out_shape, (list, tuple))
    own_specs, own_shapes = ([out_specs], [out_shape]) if single else (list(out_specs), list(out_shape))
    n_in, n_out, n_sc, n_r = len(ins), len(own_shapes), len(scratch_shapes), len(rider.srcs)
    n_all_in = n_in + n_r

    def carrying(*refs):
        own_in, srcs = refs[:n_in], refs[n_in:n_in + n_r]
        own_out, dsts = refs[n_all_in:n_all_in + n_out], refs[n_all_in + n_out:n_all_in + n_out + n_r]
        own_sc = refs[n_all_in + n_out + n_r:n_all_in + n_out + n_r + n_sc]
        sems = refs[n_all_in + n_out + n_r + n_sc:]
        first = last = late = None
        for d, steps in enumerate(grid):
            f, l = pl.program_id(d) == 0, pl.program_id(d) == steps - 1
            t = pl.program_id(d) == ((3 * steps) // 4 if d == 0 else 0)
            first, last, late = (f, l, t) if first is None else (first & f, last & l, late & t)

        @pl.when(first)
        def _():
            rider.start(srcs, dsts, *sems)

        @pl.when(late)
        def _():
            rider.forward(srcs, dsts, *sems)

        body(*own_in, *own_out, *own_sc)

        @pl.when(last)
        def _():
            rider.finish(srcs, dsts, *sems)

    res = pl.pallas_call(
        carrying, name=name, grid=grid, in_specs=list(in_specs) + [HBM_SPEC] * n_r,
        out_specs=own_specs + [HBM_SPEC] * n_r, out_shape=own_shapes + rider.out_shapes,
        scratch_shapes=list(scratch_shapes) + rider.scratch, compiler_params=_params(("arbitrary",) * len(grid)),
    )(*ins, *rider.srcs)
    own = res[:n_out]
    return (own[0] if single else list(own)), list(res[n_out:])


def _matmul(a, b, mode, out_dtype, name, add=None, relu2_a=False, relu2_grad=None,
            tm_t=None, tn_t=None, tk_t=None, rider=None, m_rows=None, col_shards=None, sq_err_target=None):
    if mode == 'nn':
        (M, K), (K2, N) = a.shape, b.shape
    elif mode == 'nt':
        (M, K), (N, K2) = a.shape, b.shape
    else:
        (K, M), (K2, N) = a.shape, b.shape
    assert K == K2, (name, a.shape, b.shape)
    m_first = 0
    if m_rows is not None:
        assert mode == 'tn'
        m_first, M = m_rows
    if col_shards is not None:
        assert add is None and relu2_grad is None and sq_err_target is None and tn_t is None
    if mode == 'tn':
        d_tm, d_tn, d_tk = 1024, (2048 if M <= 512 else 1024), 2048
    else:
        wide = add is None and sq_err_target is None and jnp.dtype(out_dtype).itemsize == 2 and K <= D_FF
        d_tm, d_tn, d_tk = (2048 if K <= 1024 else 1024), (1024 if wide else 512), MAX_K_TILE
    tm, tn, tk = _pick(M, tm_t or d_tm), _pick(N, tn_t or d_tn), _pick(K, tk_t or d_tk)
    gm, gn, nk = M // tm, N // tn, K // tk
    if mode == 'nn':
        a_spec = pl.BlockSpec((tm, tk), lambda i, j, k: (i, k))
        b_spec = pl.BlockSpec((tk, tn), lambda i, j, k: (k, j))
        dims = (((1,), (0,)), ((), ()))
    elif mode == 'nt':
        a_spec = pl.BlockSpec((tm, tk), lambda i, j, k: (i, k))
        b_spec = pl.BlockSpec((tn, tk), lambda i, j, k: (j, k))
        dims = (((1,), (1,)), ((), ()))
    else:
        assert m_first % tm == 0
        a_spec = pl.BlockSpec((tk, tm), lambda i, j, k: (k, m_first // tm + i))
        b_spec = pl.BlockSpec((tk, tn), lambda i, j, k: (k, j))
        dims = (((0,), (0,)), ((), ()))
    o_spec = pl.BlockSpec((tm, tn), lambda i, j, k: (i, j))
    shard_w = N // col_shards if col_shards is not None else tn
    assert tn % shard_w == 0
    has_add, has_e, has_t = add is not None, relu2_grad is not None, sq_err_target is not None
    assert not has_t or (nk == 1 and tn % LANES == 0)

    def body(*refs):
        a_ref, b_ref = refs[0], refs[1]
        pos = 2
        add_ref = e_ref = t_ref = None
        if has_add:
            add_ref = refs[pos]
            pos += 1
        if has_e:
            e_ref = refs[pos]
            pos += 1
        if has_t:
            t_ref = refs[pos]
            pos += 1
        o_ref = refs[pos]
        acc_ref = refs[pos + 1] if nk > 1 else None

        av = a_ref[...]
        if relu2_a:
            av = jnp.maximum(av, 0)
            av = av * av
        prod = lax.dot_general(av.astype(BF16), b_ref[...].astype(BF16), dims, preferred_element_type=F32)

        def finish(r):
            if has_add:
                r = r + add_ref[...]
            if has_e:
                r = r * (2.0 * jnp.maximum(e_ref[...].astype(F32), 0.0))
            if has_t:
                err = r - t_ref[...]
                r = err * (1.0 / N)
                refs[pos + 1][...] = r.astype(BF16)
                sq = err * err
                part = sq[:, 0:LANES]
                for c in range(1, tn // LANES):
                    part = part + sq[:, c * LANES:(c + 1) * LANES]
                _acc_rows(refs[pos + 2], part, (pl.program_id(0) == 0) & (pl.program_id(1) == 0))
            if col_shards is not None:
                for s in range(tn // shard_w):
                    o_ref[s] = r[:, s * shard_w:(s + 1) * shard_w].astype(out_dtype)
            else:
                o_ref[...] = r.astype(out_dtype)

        if nk == 1:
            finish(prod)
        else:
            k = pl.program_id(2)

            @pl.when(k == 0)
            def _():
                acc_ref[...] = prod

            @pl.when(k > 0)
            def _():
                acc_ref[...] += prod

            @pl.when(k == nk - 1)
            def _():
                finish(acc_ref[...])

    ins, specs = [a, b], [a_spec, b_spec]
    if has_add:
        ins.append(add)
        specs.append(o_spec)
    if has_e:
        ins.append(relu2_grad)
        specs.append(o_spec)
    out_specs, out_shape, sem = o_spec, jax.ShapeDtypeStruct((M, N), out_dtype), ("parallel", "parallel", "arbitrary")
    if has_t:
        ins.append(sq_err_target)
        specs.append(o_spec)
        out_specs = [o_spec, o_spec, pl.BlockSpec((1, LANES), lambda i, j, k: (0, 0))]
        out_shape = [out_shape, jax.ShapeDtypeStruct((M, N), BF16), jax.ShapeDtypeStruct((1, LANES), F32)]
        sem = ("arbitrary", "arbitrary", "arbitrary")
    if col_shards is not None:
        out_specs = pl.BlockSpec((tn // shard_w, tm, shard_w), lambda i, j, k: (j, i, 0))
        out_shape = jax.ShapeDtypeStruct((col_shards, M, shard_w), out_dtype)
    return _call(body, rider, ins, name=name, grid=(gm, gn, nk), in_specs=specs, out_specs=out_specs, out_shape=out_shape,
                 scratch_shapes=[pltpu.VMEM((tm, tn), F32)] if nk > 1 else [], sem=sem)


ROW_BLOCK_BYTES = 12 * 1024 * 1024


def _row_tile(rows, row_bytes):
    return _pick(rows, max(16, min(1024, ROW_BLOCK_BYTES // row_bytes)), 16)


def _rowspec(tr, width, col=0):
    return pl.BlockSpec((tr, width), lambda i, col=col: (i, col))


def _fullspec(shape):
    nd = len(shape)
    return pl.BlockSpec(shape, lambda i, nd=nd: (0,) * nd)


def _rms(x, width):
    x = x.astype(F32)
    return lax.rsqrt(jnp.sum(x * x, axis=-1, keepdims=True) * (1.0 / width) + EPS)


def _rms_bwd_rows(x, g, dy, width):
    x, dy = x.astype(F32), dy.astype(F32)
    r = _rms(x, width)
    xh = x * r
    dn = dy * g
    dx = r * (dn - xh * (jnp.sum(dn * xh, axis=-1, keepdims=True) * (1.0 / width)))
    return dx, dy * xh


def _acc_rows(ref, val, first):
    s = jnp.sum(val, axis=0, keepdims=True)

    @pl.when(first)
    def _():
        ref[...] = s

    @pl.when(jnp.logical_not(first))
    def _():
        ref[...] += s


def _rms_fwd(x, g, name, rider=None):
    rows, width = x.shape
    tr = _row_tile(rows, 6 * width)

    def body(x_ref, g_ref, o_ref):
        xv = x_ref[...]
        o_ref[...] = (xv * _rms(xv, width) * g_ref[...]).astype(BF16)

    return _call(body, rider, [x, g], name=name, grid=(rows // tr,),
                 in_specs=[_rowspec(tr, width), _fullspec((1, width))], out_specs=_rowspec(tr, width),
                 out_shape=jax.ShapeDtypeStruct((rows, width), BF16), scratch_shapes=[], sem=("parallel",))


def _rms_bwd(x, g, dy, res, name, dx_dtypes=(F32,)):
    rows, width = x.shape
    tr = _row_tile(rows, 18 * width)
    has_res = res is not None
    n_in = 4 if has_res else 3

    def body(*refs):
        x_ref, g_ref, dy_ref = refs[:3]
        dx, dgv = _rms_bwd_rows(x_ref[...], g_ref[...], dy_ref[...], width)
        if has_res:
            dx = dx + refs[3][...]
        for ref, dt in zip(refs[n_in:], dx_dtypes):
            ref[...] = dx.astype(dt)
        _acc_rows(refs[-1], dgv, pl.program_id(0) == 0)

    ins = [x, g, dy] + ([res] if has_res else [])
    specs = [_rowspec(tr, width), _fullspec((1, width)), _rowspec(tr, width)] + ([_rowspec(tr, width)] if has_res else [])
    return pl.pallas_call(
        body, name=name, grid=(rows // tr,), in_specs=specs,
        out_specs=[_rowspec(tr, width)] * len(dx_dtypes) + [_fullspec((1, width))],
        out_shape=[jax.ShapeDtypeStruct((rows, width), dt) for dt in dx_dtypes] + [jax.ShapeDtypeStruct((1, width), F32)],
        compiler_params=_params(("arbitrary",)),
    )(*ins)


_GELU_C = math.sqrt(2.0 / math.pi)


def _gelu(x):
    t = jnp.tanh(_GELU_C * (x + 0.044715 * (x * x * x)))
    return 0.5 * x * (1.0 + t), t


def _gelu_grad(x, t):
    return 0.5 * (1.0 + t) + 0.5 * x * (1.0 - t * t) * (_GELU_C * (1.0 + 3.0 * 0.044715 * (x * x)))


def _gm_forward_rows(zu, zv, gln, bln, wc_ref, bst, n_chunk):
    u, tu = _gelu(zu)
    a, ta = _gelu(zv)
    mu = jnp.mean(a, axis=-1, keepdims=True)
    ac = a - mu
    rs = lax.rsqrt(jnp.mean(ac * ac, axis=-1, keepdims=True) + EPS)
    n = ac * rs
    v = n * gln + bln
    vb = v.astype(BF16)
    rows = []
    for c in range(n_chunk):
        cols = []
        for g in range(GM_GROUPS):
            vc = vb[c * GM_CHUNK:(c + 1) * GM_CHUNK, g * LANES:(g + 1) * LANES]
            mixed = jnp.dot(wc_ref[g], vc, preferred_element_type=F32) + bst[g]
            cols.append(mixed)
        rows.append(jnp.concatenate(cols, axis=1))
    mixed = jnp.concatenate(rows, axis=0) if n_chunk > 1 else rows[0]
    return u, tu, ta, n, rs, v, mixed


def _gm_fwd(z, gln, bln, wc, bst, name):
    rows = z.shape[0]
    tr = _pick(rows, 512, GM_CHUNK)
    n_chunk = tr // GM_CHUNK

    def body(zu_ref, zv_ref, gln_ref, bln_ref, wc_ref, bst_ref, o_ref):
        u, _, _, _, _, _, mixed = _gm_forward_rows(zu_ref[...].astype(F32), zv_ref[...].astype(F32), gln_ref[...], bln_ref[...], wc_ref,
                                                   bst_ref, n_chunk)
        o_ref[...] = (u * mixed).astype(BF16)

    return pl.pallas_call(
        body, name=name, grid=(rows // tr,),
        in_specs=[_rowspec(tr, GM_WIDTH, Z_GM // GM_WIDTH), _rowspec(tr, GM_WIDTH, Z_GM // GM_WIDTH + 1),_fullspec((1, GM_WIDTH)), _fullspec((1, GM_WIDTH)),
                  _fullspec((GM_GROUPS, GM_CHUNK, GM_CHUNK)), _fullspec((GM_GROUPS, GM_CHUNK, LANES))],
        out_specs=_rowspec(tr, GM_WIDTH), out_shape=jax.ShapeDtypeStruct((rows, GM_WIDTH), BF16),
        compiler_params=_params(("parallel",)),
    )(z, z, gln, bln, wc, bst)


ANY_SPEC = pl.BlockSpec(memory_space=pl.ANY)


def _gm_bwd(z, dy, gln, bln, wc, wct, bst, dz, name):
    rows = z.shape[0]
    tr = _pick(rows, 512, GM_CHUNK)
    n_chunk = tr // GM_CHUNK

    def body(zu_ref, zv_ref, dy_ref, gln_ref, bln_ref, wc_ref, wct_ref, bst_ref, _, dz_ref, dws_ref, dbs_ref, dgl_ref,
             dbl_ref):
        first = pl.program_id(0) == 0
        zu, zv, gln = zu_ref[...].astype(F32), zv_ref[...].astype(F32), gln_ref[...]
        u, tu, ta, n, rs, v, mixed = _gm_forward_rows(zu, zv, gln, bln_ref[...], wc_ref, bst_ref, n_chunk)
        dyv = dy_ref[...].astype(F32)
        dzu = dyv * mixed * _gelu_grad(zu, tu)
        dmix = dyv * u
        dmb = dmix.astype(BF16)
        vb = v.astype(BF16)
        dv_rows, dws, dbs = [], [None] * GM_GROUPS, None
        for c in range(n_chunk):
            rsl = slice(c * GM_CHUNK, (c + 1) * GM_CHUNK)
            cols = []
            for g in range(GM_GROUPS):
                csl = slice(g * LANES, (g + 1) * LANES)
                dmc = dmb[rsl, csl]
                cols.append(jnp.dot(wct_ref[g], dmc, preferred_element_type=F32))
                w_part = lax.dot_general(dmc, vb[rsl, csl], (((1,), (1,)), ((), ())), preferred_element_type=F32)
                dws[g] = w_part if dws[g] is None else dws[g] + w_part
            dv_rows.append(jnp.concatenate(cols, axis=1))
            dbs = dmix[rsl, :] if dbs is None else dbs + dmix[rsl, :]
        dv = jnp.concatenate(dv_rows, axis=0) if n_chunk > 1 else dv_rows[0]
        dn = dv * gln
        da = rs * (dn - jnp.mean(dn, axis=-1, keepdims=True) - n * jnp.mean(dn * n, axis=-1, keepdims=True))
        dzv = da * _gelu_grad(zv, ta)
        dz_ref[:, 0:GM_WIDTH] = dzu.astype(BF16)
        dz_ref[:, GM_WIDTH:2 * GM_WIDTH] = dzv.astype(BF16)
        _acc_rows(dgl_ref, dv * n, first)
        _acc_rows(dbl_ref, dv, first)

        @pl.when(first)
        def _():
            for g in range(GM_GROUPS):
                dws_ref[g] = dws[g]
            dbs_ref[...] = dbs

        @pl.when(jnp.logical_not(first))
        def _():
            for g in range(GM_GROUPS):
                dws_ref[g] += dws[g]
            dbs_ref[...] += dbs

    wspec = _fullspec((GM_GROUPS, GM_CHUNK, GM_CHUNK))
    return pl.pallas_call(
        body, name=name, grid=(rows // tr,),
        in_specs=[_rowspec(tr, GM_WIDTH, Z_GM // GM_WIDTH), _rowspec(tr, GM_WIDTH, Z_GM // GM_WIDTH + 1),
                  _rowspec(tr, GM_WIDTH), _fullspec((1, GM_WIDTH)), _fullspec((1, GM_WIDTH)), wspec, wspec, wspec, ANY_SPEC],
        out_specs=[_rowspec(tr, 2 * GM_WIDTH, Z_GM // (2 * GM_WIDTH)), wspec, _fullspec((GM_CHUNK, GM_WIDTH)),
                   _fullspec((1, GM_WIDTH)), _fullspec((1, GM_WIDTH))],
        out_shape=[jax.ShapeDtypeStruct(dz.shape, dz.dtype), jax.ShapeDtypeStruct((GM_GROUPS, GM_CHUNK, GM_CHUNK), F32),
                   jax.ShapeDtypeStruct((GM_CHUNK, GM_WIDTH), F32), jax.ShapeDtypeStruct((1, GM_WIDTH), F32),
                   jax.ShapeDtypeStruct((1, GM_WIDTH), F32)],
        input_output_aliases={8: 0}, compiler_params=_params(("arbitrary",)),
    )(z, z, dy, gln, bln, wc, wct, bst, dz)


def _lat_fwd(z, g_cq, g_ckv, name):
    rows = z.shape[0]
    tr = _row_tile(rows, 4 * MLA_W)

    def body(z_ref, gq_ref, gkv_ref, nq_ref, nkv_ref):
        zb = z_ref[...]
        cq, ckv = zb[:, 0:Q_LORA], zb[:, Q_LORA:Q_LORA + KV_LORA]
        nq_ref[...] = (cq * _rms(cq, Q_LORA) * gq_ref[...]).astype(BF16)
        nkv_ref[...] = (ckv * _rms(ckv, KV_LORA) * gkv_ref[...]).astype(BF16)

    return pl.pallas_call(
        body, name=name, grid=(rows // tr,),
        in_specs=[_rowspec(tr, MLA_W, Z_MLA // MLA_W), _fullspec((1, Q_LORA)), _fullspec((1, KV_LORA))],
        out_specs=[_rowspec(tr, Q_LORA), _rowspec(tr, KV_LORA)],
        out_shape=[jax.ShapeDtypeStruct((rows, Q_LORA), BF16), jax.ShapeDtypeStruct((rows, KV_LORA), BF16)],
        compiler_params=_params(("parallel",)),
    )(z, g_cq, g_ckv)


def _lat_bwd(z, dnq, dnkv, dkpe, g_cq, g_ckv, dz, name):
    rows = z.shape[0]
    tr = _row_tile(rows, 8 * MLA_W)

    def body(z_ref, dnq_ref, dnkv_ref, dkpe_ref, gq_ref, gkv_ref, _, dz_ref, dgq_ref, dgkv_ref):
        first = pl.program_id(0) == 0
        zb = z_ref[...]
        dcq, dgq = _rms_bwd_rows(zb[:, 0:Q_LORA], gq_ref[...], dnq_ref[...], Q_LORA)
        dckv, dgkv = _rms_bwd_rows(zb[:, Q_LORA:Q_LORA + KV_LORA], gkv_ref[...], dnkv_ref[...], KV_LORA)
        dz_ref[:, 0:Q_LORA] = dcq.astype(BF16)
        dz_ref[:, Q_LORA:Q_LORA + KV_LORA] = dckv.astype(BF16)
        dz_ref[:, Q_LORA + KV_LORA:MLA_W] = dkpe_ref[...].astype(BF16)
        _acc_rows(dgq_ref, dgq, first)
        _acc_rows(dgkv_ref, dgkv, first)

    return pl.pallas_call(
        body, name=name, grid=(rows // tr,),
        in_specs=[_rowspec(tr, MLA_W, Z_MLA // MLA_W), _rowspec(tr, Q_LORA), _rowspec(tr, KV_LORA), _rowspec(tr, LANES),
                  _fullspec((1, Q_LORA)), _fullspec((1, KV_LORA)), ANY_SPEC],
        out_specs=[_rowspec(tr, MLA_W, Z_MLA // MLA_W), _fullspec((1, Q_LORA)), _fullspec((1, KV_LORA))],
        out_shape=[jax.ShapeDtypeStruct(dz.shape, dz.dtype), jax.ShapeDtypeStruct((1, Q_LORA), F32),
                   jax.ShapeDtypeStruct((1, KV_LORA), F32)],
        input_output_aliases={6: 0}, compiler_params=_params(("arbitrary",)),
    )(z, dnq, dnkv, dkpe, g_cq, g_ckv, dz)


def _rope(y, cc, ss):
    return y * cc + pltpu.roll(y, 64, 1) * ss


def _rope_bwd(d, cc, ss):
    return d * cc + pltpu.roll(d * ss, 64, 1)


def _qk_fwd(q, kv, z, cc, ss, gqn, gqp, gkn, gkp, name):
    rows = q.shape[0]
    W = MLA_HEADS * HEAD
    tr = _row_tile(rows, 20 * W)
    QS = MLA_SCALE * LOG2E

    def body(q_ref, kv_ref, kpe_ref, cc_ref, ss_ref, gqn_ref, gqp_ref, gkn_ref, gkp_ref, qc_ref, kc_ref, v_ref):
        cc, ss = cc_ref[...], ss_ref[...]
        kpe = kpe_ref[...]
        kp = _rope(kpe * _rms(kpe, MLA_ROPE) * gkp_ref[...], cc, ss).astype(BF16)
        for h in range(MLA_HEADS):
            qn = q_ref[:, h * HEAD:(h + 1) * HEAD]
            qp = q_ref[:, W + h * HEAD:W + (h + 1) * HEAD]
            kn = kv_ref[:, h * HEAD:(h + 1) * HEAD]
            qc_ref[:, h * QCAT:h * QCAT + HEAD] = (qn * _rms(qn, HEAD) * gqn_ref[...] * QS).astype(BF16)
            qc_ref[:, h * QCAT + HEAD:(h + 1) * QCAT] = (_rope(qp * _rms(qp, MLA_ROPE) * gqp_ref[...], cc, ss) * QS).astype(BF16)
            kc_ref[:, h * QCAT:h * QCAT + HEAD] = (kn * _rms(kn, HEAD) * gkn_ref[...]).astype(BF16)
            kc_ref[:, h * QCAT + HEAD:(h + 1) * QCAT] = kp
        v_ref[...] = kv_ref[:, W:2 * W].astype(BF16)

    g = _fullspec((1, HEAD))
    return _call(
        body, None, [q, kv, z, cc, ss, gqn, gqp, gkn, gkp], name=name, grid=(rows // tr,),
        in_specs=[_rowspec(tr, 2 * W), _rowspec(tr, 2 * W), _rowspec(tr, LANES, Z_KPE // LANES), _rowspec(tr, LANES),
                  _rowspec(tr, LANES), g, g, g, g],
        out_specs=[_rowspec(tr, MLA_HEADS * QCAT), _rowspec(tr, MLA_HEADS * QCAT), _rowspec(tr, W)],
        out_shape=[jax.ShapeDtypeStruct((rows, MLA_HEADS * QCAT), BF16), jax.ShapeDtypeStruct((rows, MLA_HEADS * QCAT), BF16),
                   jax.ShapeDtypeStruct((rows, W), BF16)],
        scratch_shapes=[], sem=("parallel",))


def _qk_bwd(q, kv, z, cc, ss, gqn, gqp, gkn, gkp, dqc, dkc, dv, name):
    rows = q.shape[0]
    W = MLA_HEADS * HEAD
    tr = _row_tile(rows, 24 * W)

    def body(q_ref, kv_ref, kpe_ref, cc_ref, ss_ref, gqn_ref, gqp_ref, gkn_ref, gkp_ref, dqc_ref, dkc_ref, dv_ref,
             dq_ref, dkv_ref, dkpe_ref, dgqn_ref, dgqp_ref, dgkn_ref, dgkp_ref):
        first = pl.program_id(0) == 0
        cc, ss = cc_ref[...], ss_ref[...]
        sqn = sqp = skn = dkp = None
        for h in range(MLA_HEADS):
            dx, dg = _rms_bwd_rows(q_ref[:, h * HEAD:(h + 1) * HEAD], gqn_ref[...], dqc_ref[:, h * QCAT:h * QCAT + HEAD], HEAD)
            dq_ref[:, h * HEAD:(h + 1) * HEAD] = dx.astype(BF16)
            sqn = dg if sqn is None else sqn + dg
            dy = _rope_bwd(dqc_ref[:, h * QCAT + HEAD:(h + 1) * QCAT], cc, ss)
            dx, dg = _rms_bwd_rows(q_ref[:, W + h * HEAD:W + (h + 1) * HEAD], gqp_ref[...], dy, MLA_ROPE)
            dq_ref[:, W + h * HEAD:W + (h + 1) * HEAD] = dx.astype(BF16)
            sqp = dg if sqp is None else sqp + dg
            dx, dg = _rms_bwd_rows(kv_ref[:, h * HEAD:(h + 1) * HEAD], gkn_ref[...], dkc_ref[:, h * QCAT:h * QCAT + HEAD], HEAD)
            dkv_ref[:, h * HEAD:(h + 1) * HEAD] = dx.astype(BF16)
            skn = dg if skn is None else skn + dg
            part = dkc_ref[:, h * QCAT + HEAD:(h + 1) * QCAT].astype(F32)
            dkp = part if dkp is None else dkp + part
        dkv_ref[:, W:2 * W] = dv_ref[...].astype(BF16)
        dx, dg = _rms_bwd_rows(kpe_ref[...], gkp_ref[...], _rope_bwd(dkp, cc, ss), MLA_ROPE)
        dkpe_ref[...] = dx
        _acc_rows(dgqn_ref, sqn, first)
        _acc_rows(dgqp_ref, sqp, first)
        _acc_rows(dgkn_ref, skn, first)
        _acc_rows(dgkp_ref, dg, first)

    g = _fullspec((1, HEAD))
    gs = jax.ShapeDtypeStruct((1, HEAD), F32)
    return pl.pallas_call(
        body, name=name, grid=(rows // tr,),
        in_specs=[_rowspec(tr, 2 * W), _rowspec(tr, 2 * W), _rowspec(tr, LANES, Z_KPE // LANES), _rowspec(tr, LANES),
                  _rowspec(tr, LANES), g, g, g, g, _rowspec(tr, MLA_HEADS * QCAT), _rowspec(tr, MLA_HEADS * QCAT),
                  _rowspec(tr, W)],
        out_specs=[_rowspec(tr, 2 * W), _rowspec(tr, 2 * W), _rowspec(tr, LANES), g, g, g, g],
        out_shape=[jax.ShapeDtypeStruct((rows, 2 * W), BF16), jax.ShapeDtypeStruct((rows, 2 * W), BF16),
                   jax.ShapeDtypeStruct((rows, LANES), F32), gs, gs, gs, gs],
        compiler_params=_params(("arbitrary",)),
    )(q, kv, z, cc, ss, gqn, gqp, gkn, gkp, dqc, dkc, dv)


def _headnorm_fwd(x, col, nheads, g, out_scale, name):
    rows = x.shape[0]
    W = nheads * HEAD
    tr = _row_tile(rows, 6 * W)

    def body(x_ref, g_ref, o_ref):
        for h in range(nheads):
            xv = x_ref[:, h * HEAD:(h + 1) * HEAD]
            o_ref[:, h * HEAD:(h + 1) * HEAD] = (xv * _rms(xv, HEAD) * g_ref[...] * out_scale).astype(BF16)

    return pl.pallas_call(
        body, name=name, grid=(rows // tr,),
        in_specs=[_rowspec(tr, W, col), _fullspec((1, HEAD))], out_specs=_rowspec(tr, W),
        out_shape=jax.ShapeDtypeStruct((rows, W), BF16), compiler_params=_params(("parallel",)),
    )(x, g)


def _headnorm_bwd(x, col, nheads, g, dy, tail, name, into=None):
    rows = x.shape[0]
    W = nheads * HEAD
    tr = _row_tile(rows, 12 * W)
    has_tail = tail is not None
    WO = 2 * W if has_tail else W

    def body(*refs):
        if into is not None:
            x_ref, g_ref, dy_ref, _, dx_ref, dg_ref = refs
        elif has_tail:
            x_ref, g_ref, dy_ref, t_ref, dx_ref, dg_ref = refs
        else:
            x_ref, g_ref, dy_ref, dx_ref, dg_ref = refs
        acc = None
        for h in range(nheads):
            sl = slice(h * HEAD, (h + 1) * HEAD)
            dx, dg = _rms_bwd_rows(x_ref[:, sl], g_ref[...], dy_ref[:, sl], HEAD)
            dx_ref[:, sl] = dx.astype(BF16)
            acc = dg if acc is None else acc + dg
        if has_tail:
            dx_ref[:, W:2 * W] = t_ref[...].astype(BF16)
        _acc_rows(dg_ref, acc, pl.program_id(0) == 0)

    ins = [x, g, dy] + ([tail] if has_tail else [])
    specs = [_rowspec(tr, W, col), _fullspec((1, HEAD)), _rowspec(tr, W)] + ([_rowspec(tr, W)] if has_tail else [])
    dx_spec, dx_shape, aliases = _rowspec(tr, WO), jax.ShapeDtypeStruct((rows, WO), BF16), {}
    if into is not None:
        assert not has_tail
        ins, specs = ins + [into[0]], specs + [ANY_SPEC]
        dx_spec, dx_shape, aliases = _rowspec(tr, W, into[1]), jax.ShapeDtypeStruct(into[0].shape, into[0].dtype), {3: 0}
    return pl.pallas_call(
        body, name=name, grid=(rows // tr,), in_specs=specs,
        out_specs=[dx_spec, _fullspec((1, HEAD))], out_shape=[dx_shape, jax.ShapeDtypeStruct((1, HEAD), F32)],
        input_output_aliases=aliases, compiler_params=_params(("arbitrary",)),
    )(*ins)


def _sigmoid(x):
    return 1.0 / (1.0 + jnp.exp(-x.astype(F32)))


def _merge_fwd(z, y_gm, y_mla, y_mem, name):
    rows = z.shape[0]
    tr = _row_tile(rows, 14 * D_MODEL)

    def body(g0_ref, g1_ref, g2_ref, a_ref, b_ref, c_ref, o_ref):
        m = _sigmoid(g0_ref[...]) * a_ref[...] + _sigmoid(g1_ref[...]) * b_ref[...] + _sigmoid(g2_ref[...]) * c_ref[...]
        o_ref[...] = m.astype(BF16)

    r = _rowspec(tr, D_MODEL)
    return pl.pallas_call(
        body, name=name, grid=(rows // tr,),
        in_specs=[_rowspec(tr, D_MODEL, 0), _rowspec(tr, D_MODEL, 1), _rowspec(tr, D_MODEL, 2),r, r, r],
        out_specs=r, out_shape=jax.ShapeDtypeStruct((rows, D_MODEL), BF16), compiler_params=_params(("parallel",)),
    )(z, z, z, y_gm, y_mla, y_mem)


def _merge_bwd(z, y_gm, y_mla, y_mem, dm, name):
    rows = z.shape[0]
    tr = _row_tile(rows, 24 * D_MODEL)

    def body(g0_ref, g1_ref, g2_ref, a_ref, b_ref, c_ref, dm_ref, da_ref, db_ref, dc_ref, dzg_ref):
        dmv = dm_ref[...].astype(F32)
        for k, (g_ref, y_ref, dy_ref) in enumerate(((g0_ref, a_ref, da_ref), (g1_ref, b_ref, db_ref), (g2_ref, c_ref, dc_ref))):
            s = _sigmoid(g_ref[...])
            dy_ref[...] = (dmv * s).astype(BF16)
            dzg_ref[:, k * D_MODEL:(k + 1) * D_MODEL] = (dmv * y_ref[...] * s * (1.0 - s)).astype(BF16)

    r = _rowspec(tr, D_MODEL)
    o = jax.ShapeDtypeStruct((rows, D_MODEL), BF16)
    return pl.pallas_call(
        body, name=name, grid=(rows // tr,),
        in_specs=[_rowspec(tr, D_MODEL, 0), _rowspec(tr, D_MODEL, 1), _rowspec(tr, D_MODEL, 2),r, r, r, r],
        out_specs=[r, r, r, _rowspec(tr, 3 * D_MODEL, 0)],
        out_shape=[o, o, o, jax.ShapeDtypeStruct((rows, Z_COLS), BF16)],
        compiler_params=_params(("parallel",)),
    )(z, z, z, y_gm, y_mla, y_mem, dm)


_NT = (((1,), (1,)), ((), ()))
_TN = (((0,), (0,)), ((), ()))


def _diag_mask(s):
    row = lax.broadcasted_iota(jnp.int32, s.shape, 0)
    col = lax.broadcasted_iota(jnp.int32, s.shape, 1)
    return jnp.where(row >= col, s, NEG)


def _attn_fwd(q, k, v, nb, nheads, dk, v_col0, causal, name, rider=None):
    S, Skv = q.shape[0] // nb, k.shape[0] // nb
    tq = _pick(Skv, ATT_TILE) if causal else _pick(S, 4 * ATT_TILE)
    nq = S // tq

    def body(q_ref, k_ref, v_ref, o_ref, lse_ref):
        for i in range(nq):
            r0 = i * tq
            qb = q_ref[r0:r0 + tq, :]
            if causal:
                spans = ([(0, r0, False)] if i > 0 else []) + [(r0, r0 + tq, True)]
            else:
                spans = [(0, Skv, False)]
            scores = []
            for a, b, masked in spans:
                s = lax.dot_general(qb, k_ref[a:b, :], _NT, preferred_element_type=F32)
                scores.append(_diag_mask(s) if masked else s)
            m = functools.reduce(jnp.maximum, [jnp.max(s, axis=-1, keepdims=True) for s in scores])
            l = acc = None
            for s, (a, b, _) in zip(scores, spans):
                p = jnp.exp2(s - m)
                lp = jnp.sum(p, axis=-1, keepdims=True)
                ap = jnp.dot(p.astype(BF16), v_ref[a:b, :].astype(BF16), preferred_element_type=F32)
                l, acc = (lp, ap) if l is None else (l + lp, acc + ap)
            o_ref[r0:r0 + tq, :] = (acc / l).astype(BF16)
            lse_ref[r0:r0 + tq, :] = m + jnp.log2(l)

    ins = [q, k, v]
    in_specs = [pl.BlockSpec((S, dk), lambda b, h: (b, h)), pl.BlockSpec((Skv, dk), lambda b, h: (b, h)),
                pl.BlockSpec((Skv, HEAD), lambda b, h: (b, v_col0 + h))]
    out_specs = [pl.BlockSpec((S, HEAD), lambda b, h: (b, h)), pl.BlockSpec((None, S, 1), lambda b, h: (h, b, 0))]
    out_shape = [jax.ShapeDtypeStruct((nb * S, nheads * HEAD), BF16), jax.ShapeDtypeStruct((nheads, nb * S, 1), F32)]
    return _call(body, rider, ins, name=name, grid=(nb, nheads), in_specs=in_specs, out_specs=out_specs,
                 out_shape=out_shape, scratch_shapes=[], sem=("parallel", "parallel"))


def _attn_bwd(q, k, v, o, do, lse, nb, nheads, dk, v_col0, scale, causal, name, rider=None):
    S, Skv = q.shape[0] // nb, k.shape[0] // nb
    tk = _pick(Skv, ATT_TILE)
    nkv = Skv // tk

    def body(q_ref, k_ref, v_ref, o_ref, do_ref, lse_ref, dq_ref, dk_ref, dv_ref, delta_ref, dob_ref, dqa_ref):
        dov = do_ref[...]
        delta_ref[...] = jnp.sum(o_ref[...].astype(F32) * dov.astype(F32), axis=-1, keepdims=True)
        dob_ref[...] = dov.astype(BF16)

        for j in range(nkv):
            c0 = j * tk
            kb = k_ref[c0:c0 + tk, :]
            vb = v_ref[c0:c0 + tk, :].astype(BF16)
            if causal:
                spans = [(c0, c0 + tk, True)] + ([(c0 + tk, S, False)] if c0 + tk < S else [])
            else:
                spans = [(0, S, False)]
            dk_acc = dv_acc = None
            for a, b, masked in spans:
                qb = q_ref[a:b, :]
                dob = dob_ref[a:b, :]
                s = lax.dot_general(qb, kb, _NT, preferred_element_type=F32)
                if masked:
                    s = _diag_mask(s)
                p = jnp.exp2(s - lse_ref[a:b, :])
                dp = lax.dot_general(dob, vb, _NT, preferred_element_type=F32)
                ds = (p * (dp - delta_ref[a:b, :])).astype(BF16)
                dv_p = lax.dot_general(p.astype(BF16), dob, _TN, preferred_element_type=F32)
                dk_p = lax.dot_general(ds, qb, _TN, preferred_element_type=F32)
                dk_acc, dv_acc = (dk_p, dv_p) if dk_acc is None else (dk_acc + dk_p, dv_acc + dv_p)
                dq_p = jnp.dot(ds, kb, preferred_element_type=F32) * scale
                if j == 0:
                    dqa_ref[a:b, :] = dq_p
                else:
                    dqa_ref[a:b, :] += dq_p
            dk_ref[c0:c0 + tk, :] = (dk_acc * LN2).astype(BF16)
            dv_ref[c0:c0 + tk, :] = dv_acc.astype(BF16)
        dq_ref[...] = dqa_ref[...].astype(BF16)

    ins = [q, k, v, o, do, lse]
    in_specs = [pl.BlockSpec((S, dk), lambda b, h: (b, h)), pl.BlockSpec((Skv, dk), lambda b, h: (b, h)),
                pl.BlockSpec((Skv, HEAD), lambda b, h: (b, v_col0 + h)), pl.BlockSpec((S, HEAD), lambda b, h: (b, h)),
                pl.BlockSpec((S, HEAD), lambda b, h: (b, h)), pl.BlockSpec((None, S, 1), lambda b, h: (h, b, 0))]
    out_specs = [pl.BlockSpec((S, dk), lambda b, h: (b, h)), pl.BlockSpec((Skv, dk), lambda b, h: (b, h)),
                 pl.BlockSpec((Skv, HEAD), lambda b, h: (b, h))]
    out_shape = [jax.ShapeDtypeStruct((nb * S, nheads * dk), BF16), jax.ShapeDtypeStruct((nb * Skv, nheads * dk), BF16),
                 jax.ShapeDtypeStruct((nb * Skv, nheads * HEAD), BF16)]
    return _call(body, rider, ins, name=name, grid=(nb, nheads), in_specs=in_specs, out_specs=out_specs,
                 out_shape=out_shape,
                 scratch_shapes=[pltpu.VMEM((S, 1), F32), pltpu.VMEM((S, HEAD), BF16), pltpu.VMEM((S, dk), F32)],
                 sem=("parallel", "parallel"))


def _spread_rope(a):
    zero = jnp.zeros(a.shape[:-1] + (32,), a.dtype)
    return jnp.concatenate([a[..., :32], zero, a[..., 32:], zero], axis=-1)


def _gather_rope(a):
    return jnp.concatenate([a[..., 0:32], a[..., 64:96]], axis=-1)


def _win_layout(w):
    return jnp.concatenate([w[:, C_ZG:C_END], w[:, C_ZU:C_CQ], w[:, C_QM:C_ZG], w[:, C_CQ:C_CKV], w[:, C_CKV:C_KPE],
                            _spread_rope(w[:, C_KPE:C_QM])], axis=1)


def _win_unlayout(d):
    return jnp.concatenate([d[:, Z_GM:Z_QM], d[:, Z_MLA:Z_MLA + Q_LORA], d[:, Z_MLA + Q_LORA:Z_KPE],
                            _gather_rope(d[:, Z_KPE:Z_COLS]), d[:, Z_QM:Z_MLA], d[:, 0:Z_GM]], axis=1)


def _wuq_layout(w):
    r = w.reshape(Q_LORA, MLA_HEADS, HEAD + MLA_ROPE)
    return jnp.concatenate([r[:, :, :HEAD].reshape(Q_LORA, -1), _spread_rope(r[:, :, HEAD:]).reshape(Q_LORA, -1)], axis=1)


def _wuq_unlayout(d):
    n = d[:, :MLA_HEADS * HEAD].reshape(Q_LORA, MLA_HEADS, HEAD)
    p = _gather_rope(d[:, MLA_HEADS * HEAD:].reshape(Q_LORA, MLA_HEADS, HEAD))
    return jnp.concatenate([n, p], axis=-1).reshape(Q_LORA, -1)


def _wukv_layout(w):
    r = w.reshape(KV_LORA, MLA_HEADS, 2 * HEAD)
    return jnp.concatenate([r[:, :, :HEAD].reshape(KV_LORA, -1), r[:, :, HEAD:].reshape(KV_LORA, -1)], axis=1)


def _wukv_unlayout(d):
    k = d[:, :MLA_HEADS * HEAD].reshape(KV_LORA, MLA_HEADS, HEAD)
    v = d[:, MLA_HEADS * HEAD:].reshape(KV_LORA, MLA_HEADS, HEAD)
    return jnp.concatenate([k, v], axis=-1).reshape(KV_LORA, -1)


AG_MID = ['w_uq', 'w_ukv', 'w_mem_kv', 'w_o_gm', 'w_o_mla', 'w_o_mem', 'w_out']
AG_FFN = ['w_ff1', 'w_ff2']
RS_GROUPS = {'ffn_proj': ['w_ff2', 'w_ff1', 'w_out', 'w_o_gm', 'w_o_mla', 'w_o_mem'],
             'lat': ['w_uq', 'w_ukv', 'w_mem_kv'], 'in_top': ['w_in'], 'in_bot': ['w_in']}


def _unride(res, rider):
    return (res, None) if rider is None else res


def _local_step(x, mem, positions, target, P, ws):
    B, S, _ = x.shape
    M = mem.shape[1]
    T = B * S
    x2d = x.reshape(T, D_MODEL)
    mem2d = mem.reshape(B * M, D_MODEL)
    tgt2d = target.reshape(T, D_MODEL)

    def row(v):
        return v.reshape(1, -1).astype(F32)

    inv_freq = ROPE_BASE ** (-jnp.arange(0, MLA_ROPE, 2, dtype=F32) / MLA_ROPE)
    zero = jnp.zeros_like(inv_freq)
    ang = positions.reshape(T).astype(F32)[:, None] * jnp.concatenate([inv_freq, zero, inv_freq, zero])
    cc = jnp.cos(ang) * jnp.concatenate([zero + 1.0, zero, zero + 1.0, zero])
    ss = jnp.sin(ang) * jnp.concatenate([zero - 1.0, zero, zero + 1.0, zero])

    g_mix, g_cq, g_ckv, g_ffn, g_mem = row(P['g_mix']), row(P['g_cq']), row(P['g_ckv']), row(P['g_ffn']), row(P['g_mem'])
    gqn, gkn, gmq, gmk = row(P['g_q_nope']), row(P['g_k_nope']), row(P['g_mq']), row(P['g_mk'])
    gqp, gkp = _spread_rope(row(P['g_q_pe'])), _spread_rope(row(P['g_k_pe']))
    gln, bln = row(P['g_gm_ln']), row(P['b_gm_ln'])
    wc = jnp.tril(P['w_spatial'].astype(F32))
    wct = jnp.swapaxes(wc, 1, 2).astype(BF16)
    wc = wc.astype(BF16)
    bst = jnp.broadcast_to(P['b_spatial'].astype(F32)[:, :, None], (GM_GROUPS, GM_CHUNK, LANES))

    ride = ws.gather(['w_in'])
    h, got = _unride(_rms_fwd(x2d, g_mix, "rms_mix", rider=ride), ride)
    w_in = _win_layout(ws.gathered(['w_in'], got)['w_in']).astype(BF16)
    ride = ws.gather(AG_MID)
    z, got = _unride(_matmul(h, w_in, 'nn', ACT, "mm_in", tn_t=1792, rider=ride), ride)
    mid = ws.gathered(AG_MID, got)
    w_uq, w_ukv = _wuq_layout(mid['w_uq']).astype(BF16), _wukv_layout(mid['w_ukv']).astype(BF16)
    w_mem_kv, w_o_gm, w_o_mla, w_o_mem, w_out = (mid[n] for n in ('w_mem_kv', 'w_o_gm', 'w_o_mla', 'w_o_mem', 'w_out'))
    ygm_pre = _gm_fwd(z, gln, bln, wc, bst, "gm_fwd")
    y_gm = _matmul(ygm_pre, w_o_gm, 'nn', ACT, "mm_o_gm")
    nq, nkv = _lat_fwd(z, g_cq, g_ckv, "lat_fwd")
    q = _matmul(nq, w_uq, 'nn', ACT, "mm_uq")
    kv = _matmul(nkv, w_ukv, 'nn', ACT, "mm_ukv")
    qcat, kcat, vv = _qk_fwd(q, kv, z, cc, ss, gqn, gqp, gkn, gkp, "qk_fwd")
    ride = ws.gather(AG_FFN)
    (o, lse), got = _unride(_attn_fwd(qcat, kcat, vv, B, MLA_HEADS, QCAT, 0, True, "mla_attn_fwd", rider=ride), ride)
    ffn = ws.gathered(AG_FFN, got)
    w_ff1, w_ff2 = ffn['w_ff1'], ffn['w_ff2']
    y_mla = _matmul(o, w_o_mla, 'nn', ACT, "mm_o_mla")
    nm = _rms_fwd(mem2d, g_mem, "rms_mem")
    kvm = _matmul(nm, w_mem_kv, 'nn', ACT, "mm_mem_kv")
    qm = _headnorm_fwd(z, Z_QM // (MEM_HEADS * HEAD), MEM_HEADS, gmq, MEM_SCALE * LOG2E, "memq_fwd")
    km = _headnorm_fwd(kvm, 0, MEM_HEADS, gmk, 1.0, "memk_fwd")
    om, lse_m = _attn_fwd(qm, km, kvm, B, MEM_HEADS, HEAD, MEM_HEADS, False, "mem_attn_fwd")
    y_mem = _matmul(om, w_o_mem, 'nn', ACT, "mm_o_mem")
    merged = _merge_fwd(z, y_gm, y_mla, y_mem, "merge_fwd")
    x1 = _matmul(merged, w_out, 'nn', F32, "mm_out", add=x2d)
    h2 = _rms_fwd(x1, g_ffn, "rms_ffn")
    a1 = _matmul(h2, w_ff1, 'nn', BF16, "mm_ff1")
    dx2, dx2b, loss_part = _matmul(a1, w_ff2, 'nn', F32, "mm_ff2", add=x1, relu2_a=True, sq_err_target=tgt2d)

    G = {}
    d_ff2 = _matmul(a1, dx2b, 'tn', BF16, "mm_d_ff2", relu2_a=True)
    da1 = _matmul(dx2b, w_ff2, 'nt', BF16, "mm_da1", relu2_grad=a1)
    d_ff1 = _matmul(h2, da1, 'tn', BF16, "mm_d_ff1", col_shards=N_DEV)
    dh2 = _matmul(da1, w_ff1, 'nt', ACT, "mm_dh2")
    dx1, dx1b, G['g_ffn'] = _rms_bwd(x1, g_ffn, dh2, dx2, "rms_ffn_bwd", dx_dtypes=(F32, BF16))
    d_out = _matmul(merged, dx1b, 'tn', BF16, "mm_d_out")
    dmerged = _matmul(dx1b, w_out, 'nt', ACT, "mm_dmerged")
    dy_gm, dy_mla, dy_mem, dz = _merge_bwd(z, y_gm, y_mla, y_mem, dmerged, "merge_bwd")
    d_o_gm = _matmul(ygm_pre, dy_gm, 'tn', BF16, "mm_d_o_gm")
    d_o_mla = _matmul(o, dy_mla, 'tn', BF16, "mm_d_o_mla")
    d_o_mem = _matmul(om, dy_mem, 'tn', BF16, "mm_d_o_mem")
    dygm_pre = _matmul(dy_gm, w_o_gm, 'nt', ACT, "mm_dygm")
    dz, dws, dbs, G['g_gm_ln'], G['b_gm_ln'] = _gm_bwd(z, dygm_pre, gln, bln, wc, wct, bst, dz, "gm_bwd")
    G['w_spatial'] = jnp.tril(dws)
    G['b_spatial'] = jnp.sum(dbs.reshape(GM_CHUNK, GM_GROUPS, LANES), axis=-1).T
    do = _matmul(dy_mla, w_o_mla, 'nt', ACT, "mm_do")
    ride = ws.scatter('ffn_proj', {'w_ff2': d_ff2, 'w_ff1': d_ff1, 'w_out': d_out, 'w_o_gm': d_o_gm, 'w_o_mla': d_o_mla,
                                   'w_o_mem': d_o_mem})
    (dqc, dkc, dvv), got = _unride(_attn_bwd(qcat, kcat, vv, o, do, lse, B, MLA_HEADS, QCAT, 0, MLA_SCALE, True,
                                             "mla_attn_bwd", rider=ride), ride)
    ws.scattered('ffn_proj', got)
    dq, dkv, dkpe, G['g_q_nope'], dgqp, G['g_k_nope'], dgkp = _qk_bwd(q, kv, z, cc, ss, gqn, gqp, gkn, gkp, dqc, dkc, dvv,
                                                                     "qk_bwd")
    G['g_q_pe'], G['g_k_pe'] = _gather_rope(dgqp), _gather_rope(dgkp)
    d_uq = _wuq_unlayout(_matmul(nq, dq, 'tn', BF16, "mm_d_uq"))
    dnq = _matmul(dq, w_uq, 'nt', ACT, "mm_dnq")
    d_ukv = _wukv_unlayout(_matmul(nkv, dkv, 'tn', BF16, "mm_d_ukv"))
    dnkv = _matmul(dkv, w_ukv, 'nt', ACT, "mm_dnkv")
    dz, G['g_cq'], G['g_ckv'] = _lat_bwd(z, dnq, dnkv, dkpe, g_cq, g_ckv, dz, "lat_bwd")
    dom = _matmul(dy_mem, w_o_mem, 'nt', ACT, "mm_dom")
    dqm, dkm, dvm = _attn_bwd(qm, km, kvm, om, dom, lse_m, B, MEM_HEADS, HEAD, MEM_HEADS, MEM_SCALE, False, "mem_attn_bwd")
    dz, G['g_mq'] = _headnorm_bwd(z, Z_QM // (MEM_HEADS * HEAD), MEM_HEADS, gmq, dqm, None, "memq_bwd",
                                  into=(dz, Z_QM // (MEM_HEADS * HEAD)))
    dkvm, G['g_mk'] = _headnorm_bwd(kvm, 0, MEM_HEADS, gmk, dkm, dvm, "memk_bwd")
    d_mem_kv = _matmul(nm, dkvm, 'tn', BF16, "mm_d_mem_kv")
    dnm = _matmul(dkvm, w_mem_kv, 'nt', ACT, "mm_dnm")
    G['g_mem'], = _rms_bwd(mem2d, g_mem, dnm, None, "rms_mem_bwd", dx_dtypes=())
    half = D_MODEL // 2
    ride = ws.scatter('lat', {'w_uq': d_uq, 'w_ukv': d_ukv, 'w_mem_kv': d_mem_kv})
    d_top, got = _unride(_matmul(h, dz, 'tn', BF16, "mm_d_in_top", tn_t=1792, m_rows=(0, half), rider=ride), ride)
    ws.scattered('lat', got)
    ride = ws.scatter('in_top', {'w_in': _win_unlayout(d_top)})
    d_bot, got = _unride(_matmul(h, dz, 'tn', BF16, "mm_d_in_bot", tn_t=1792, m_rows=(half, half), rider=ride), ride)
    ws.scattered('in_top', got)
    ride = ws.scatter('in_bot', {'w_in': _win_unlayout(d_bot)})
    dh, got = _unride(_matmul(dz, w_in, 'nt', ACT, "mm_dh", rider=ride), ride)
    ws.scattered('in_bot', got)
    gx, G['g_mix'] = _rms_bwd(x2d, g_mix, dh, dx1, "rms_mix_bwd")
    return loss_part, gx.reshape(B, S, D_MODEL), G


def _all_gather8(xs, name):
    def body(x_ref, out_ref, send_sems, recv_sems, local_sem):
        x, y, c = lax.axis_index("x"), lax.axis_index("y"), lax.axis_index("c")
        me, sibling = (x, y, c), (x, y, 1 - c)
        chips = [(1 - x, y), (x, 1 - y), (1 - x, 1 - y)]

        def rows(px, py, pc):
            return out_ref.at[4 * px + 2 * py + pc]

        def copy(k, block, to, src=None):
            return pltpu.make_async_remote_copy(
                src_ref=rows(*block) if src is None else src, dst_ref=rows(*block),
                send_sem=send_sems.at[k], recv_sem=recv_sems.at[k], device_id=to, device_id_type=MESH)

        mine = pltpu.make_async_copy(x_ref, rows(*me), local_sem)
        mine.start()
        first = [copy(0, me, sibling, src=x_ref)]
        first += [copy(1 + j, me, (*chip, c), src=x_ref) for j, chip in enumerate(chips)]
        for cp in first:
            cp.start()
        passed = [copy(4 + j, (*chip, c), sibling) for j, chip in enumerate(chips)]
        for j, chip in enumerate(chips):
            copy(1 + j, (*chip, c), me).wait_recv()
            passed[j].start()
        copy(0, sibling, me).wait_recv()
        for j, chip in enumerate(chips):
            copy(4 + j, (*chip, 1 - c), me).wait_recv()
        for cp in first + passed:
            cp.wait_send()
        mine.wait()

    return pl.pallas_call(
        body, name=name, in_specs=[HBM_SPEC], out_specs=HBM_SPEC,
        out_shape=jax.ShapeDtypeStruct((N_DEV,) + xs.shape, xs.dtype),
        scratch_shapes=[pltpu.SemaphoreType.DMA((7,)), pltpu.SemaphoreType.DMA((7,)), pltpu.SemaphoreType.DMA],
    )(xs)


def _adamw_rows(w, g, m, v):
    m2 = ADAM_B1 * m + (1.0 - ADAM_B1) * g
    v2 = ADAM_B2 * v + (1.0 - ADAM_B2) * (g * g)
    m_hat = m2 / (1.0 - ADAM_B1 ** ADAM_STEP)
    v_hat = v2 / (1.0 - ADAM_B2 ** ADAM_STEP)
    delta = -ADAM_LR * (m_hat / (jnp.sqrt(v_hat) + ADAM_EPS) + ADAM_WD * w)
    return delta, m2, v2


def _sum_adamw(parts, w, m, v, name):
    rows, cols = w.shape
    assert sum(p.shape[1] for p in parts) == rows
    tr = _pick(min(p.shape[1] for p in parts), max(16, 65536 // cols), 16)
    n = parts[0].shape[0]
    counts = [p.shape[1] // tr for p in parts]
    starts = [sum(counts[:k]) for k in range(len(parts))]

    def body(*refs):
        p_refs = refs[:len(parts)]
        w_ref, m_ref, v_ref, g_ref, d_ref, m2_ref, v2_ref = refs[len(parts):]
        g = None
        for p_ref, start in zip(p_refs, starts):
            gk = p_ref[0].astype(F32)
            for k in range(1, n):
                gk = gk + p_ref[k].astype(F32)
            g = gk if g is None else jnp.where(pl.program_id(0) >= start, gk, g)
        delta, m2, v2 = _adamw_rows(w_ref[...], g, m_ref[...], v_ref[...])
        g_ref[...] = g
        d_ref[...] = delta
        m2_ref[...] = m2
        v2_ref[...] = v2

    flat = pl.BlockSpec((tr, cols), lambda i: (i, 0))
    out = jax.ShapeDtypeStruct((rows, cols), F32)
    p_specs = [pl.BlockSpec((n, tr, cols), lambda i, s=s, c=c: (0, jnp.clip(i - s, 0, c - 1), 0))
               for s, c in zip(starts, counts)]
    return pl.pallas_call(
        body, name=name, grid=(rows // tr,), in_specs=p_specs + [flat, flat, flat], out_specs=[flat] * 4,
        out_shape=[out] * 4, compiler_params=_params(("parallel",)),
    )(*parts, w, m, v)


SMALL_WIDTH = {'g_mix': 1024, 'g_cq': 384, 'g_ckv': 256, 'g_q_nope': 128, 'g_q_pe': 128, 'g_k_nope': 128, 'g_k_pe': 128,
               'g_gm_ln': 512, 'b_gm_ln': 512, 'g_mem': 1024, 'g_mq': 128, 'g_mk': 128, 'g_ffn': 1024}
NARROW = ('g_q_pe', 'g_k_pe')


def _small_layout():
    layout, r = {}, 0
    for name in SMALL + ['loss']:
        rows = {'w_spatial': GM_GROUPS * GM_CHUNK, 'b_spatial': GM_GROUPS, 'loss': 1}.get(name) or SMALL_WIDTH[name] // LANES
        layout[name] = (r, rows)
        r += -(-rows // 8) * 8
    return layout, r


def _small_pack(grads, loss_part, name):
    layout, total = _small_layout()
    names = SMALL + ['loss']

    def body(*refs):
        out_ref = refs[-1]
        out_ref[...] = jnp.zeros((total, LANES), F32)
        for ref, n in zip(refs[:-1], names):
            r0, rows = layout[n]
            if n == 'w_spatial':
                for g in range(GM_GROUPS):
                    out_ref[r0 + g * GM_CHUNK:r0 + (g + 1) * GM_CHUNK, :] = ref[g]
            elif n == 'b_spatial':
                out_ref[r0:r0 + rows, :] = ref[...]
            else:
                for k in range(rows):
                    out_ref[r0 + k:r0 + k + 1, :] = ref[:, k * LANES:(k + 1) * LANES]

    return pl.pallas_call(body, name=name, out_shape=jax.ShapeDtypeStruct((total, LANES), F32))(
        *[grads[n] for n in SMALL], loss_part)


def _small_adamw(parts, w, m, v, name):
    layout, _ = _small_layout()
    n_dev = parts.shape[0]

    def body(*refs):
        p_ref = refs[0]
        ins = refs[1:1 + 3 * len(SMALL)]
        outs = refs[1 + 3 * len(SMALL):-1]

        def gsum(r0, rows):
            g = p_ref[0, r0:r0 + rows, :]
            for d in range(1, n_dev):
                g = g + p_ref[d, r0:r0 + rows, :]
            return g

        def step(idx, g, at):
            w_ref, m_ref, v_ref = ins[3 * idx:3 * idx + 3]
            delta, m2, v2 = _adamw_rows(w_ref[at], g, m_ref[at], v_ref[at])
            for ref, val in zip(outs[4 * idx:4 * idx + 4], (g, delta, m2, v2)):
                ref[at] = val

        for idx, n in enumerate(SMALL):
            r0, rows = layout[n]
            if n == 'w_spatial':
                for g in range(GM_GROUPS):
                    step(idx, gsum(r0 + g * GM_CHUNK, GM_CHUNK), (0, g))
            elif n == 'b_spatial':
                step(idx, gsum(r0, rows), (0,))
            else:
                for k in range(rows):
                    step(idx, gsum(r0 + k, 1), (slice(None), slice(k * LANES, (k + 1) * LANES)))
        refs[-1][...] = gsum(layout['loss'][0], 8)

    flat_in = [d[n] for n in SMALL for d in (w, m, v)]
    out_shape = [jax.ShapeDtypeStruct(w[n].shape, F32) for n in SMALL for _ in range(4)]
    res = pl.pallas_call(body, name=name, out_shape=out_shape + [jax.ShapeDtypeStruct((8, LANES), F32)])(parts, *flat_in)
    groups = [{n: res[4 * i + j] for i, n in enumerate(SMALL)} for j in range(4)]
    return groups, res[-1]


def _full_from_gathered(gathered, name):
    r, c = BIG_SHAPE[name]
    if BIG_AXIS[name] == 0:
        return gathered.reshape(r, c)
    return gathered.transpose(1, 0, 2).reshape(r, c)


def _shards_of_full(g, name):
    if g.ndim == 3:
        return g
    r, c = BIG_SHAPE[name]
    if BIG_AXIS[name] == 0:
        return g.reshape(N_DEV, r // N_DEV, c)
    return g.reshape(g.shape[0], N_DEV, c // N_DEV).transpose(1, 0, 2)


class _DistWeights:
    def __init__(self, shards):
        self.shards = shards
        self.received = {}

    def gather(self, names):
        return _Gather2([self.shards[n].astype(BF16) for n in names])

    def gathered(self, names, got):
        return {n: _full_from_gathered(g, n) for n, g in zip(names, got)}

    def scatter(self, key, grads):
        return _Exchange([_shards_of_full(grads[n], n) for n in RS_GROUPS[key]], scatter=True)

    def scattered(self, key, got):
        for n, g in zip(RS_GROUPS[key], got):
            self.received.setdefault(n, []).append(g)


def kernel(x, mem, positions, g_mix, w_in, g_cq, w_uq, g_ckv, w_ukv, g_q_nope, g_q_pe, g_k_nope, g_k_pe, g_gm_ln, b_gm_ln, w_spatial, b_spatial, g_mem, w_mem_kv, g_mq, g_mk, w_o_gm, w_o_mla, w_o_mem, w_out, g_ffn, w_ff1, w_ff2, loss_target, m_g_mix, m_w_in, m_g_cq, m_w_uq, m_g_ckv, m_w_ukv, m_g_q_nope, m_g_q_pe, m_g_k_nope, m_g_k_pe, m_g_gm_ln, m_b_gm_ln, m_w_spatial, m_b_spatial, m_g_mem, m_w_mem_kv, m_g_mq, m_g_mk, m_w_o_gm, m_w_o_mla, m_w_o_mem, m_w_out, m_g_ffn, m_w_ff1, m_w_ff2, v_g_mix, v_w_in, v_g_cq, v_w_uq, v_g_ckv, v_w_ukv, v_g_q_nope, v_g_q_pe, v_g_k_nope, v_g_k_pe, v_g_gm_ln, v_b_gm_ln, v_w_spatial, v_b_spatial, v_g_mem, v_w_mem_kv, v_g_mq, v_g_mk, v_w_o_gm, v_w_o_mla, v_w_o_mem, v_w_out, v_g_ffn, v_w_ff1, v_w_ff2):
    given = dict(locals())
    w = {n: given[n][0] for n in WEIGHTS}
    mom = {n: given['m_' + n][0] for n in WEIGHTS}
    var = {n: given['v_' + n][0] for n in WEIGHTS}

    ws = _DistWeights({n: w[n] for n in BIG})
    loss_part, grad_x, G = _local_step(x, mem, positions, loss_target, {n: w[n] for n in SMALL}, ws)

    outs = {}
    for n in BIG:
        for prefix, res in zip(("grad_", "delta_", "new_m_", "new_v_"),
                               _sum_adamw(ws.received[n], w[n], mom[n], var[n], "adamw_" + n)):
            outs[prefix + n] = res[None]

    def widen(d):
        return {n: (jnp.pad(d[n], ((0, 0), (0, LANES - MLA_ROPE))) if n in NARROW else d[n]) for n in SMALL}

    parts = _all_gather8(_small_pack(widen(G), loss_part, "small_pack"), "ag_small")
    small, loss_rows = _small_adamw(parts, *[widen({n: given[prefix + n] for n in SMALL}) for prefix in ("", "m_", "v_")],
                                    "adamw_small")
    loss = 0.5 * jnp.sum(loss_rows) / D_MODEL
    for prefix, group in zip(("grad_", "delta_", "new_m_", "new_v_"), small):
        for n in SMALL:
            outs[prefix + n] = group[n][:, :MLA_ROPE] if n in NARROW else group[n]
    return (loss, grad_x, *[outs[p + n] for p in ("grad_", "delta_", "new_m_", "new_v_") for n in WEIGHTS])
```

```python
import functools
import math

import jax
import jax.numpy as jnp
from jax import lax
from jax.experimental import pallas as pl
from jax.experimental.pallas import tpu as pltpu

F32 = jnp.float32
BF16 = jnp.bfloat16
ACT = BF16

D_MODEL = 1024
MEM_HEADS = 4
HEAD = 128
GM_WIDTH = 512
GM_CHUNK = 128
GM_GROUPS = 4
MLA_HEADS = 8
MLA_ROPE = 64
Q_LORA = 384
KV_LORA = 256
D_FF = 4096
EPS = 1e-6
ROPE_BASE = 10000.0
MLA_SCALE = 1.0 / math.sqrt(HEAD + MLA_ROPE)
MEM_SCALE = 1.0 / math.sqrt(HEAD)
LOG2E = 1.4426950408889634
LN2 = 0.6931471805599453
ATT_TILE = 256
C_ZU, C_ZV, C_CQ, C_CKV, C_KPE, C_QM, C_ZG, C_END = 0, 512, 1024, 1408, 1664, 1728, 2240, 5312
Z_GM, Z_QM, Z_MLA, Z_KPE, Z_COLS = 3072, 4096, 4608, 5248, 5376
MLA_W = 768
QCAT = 2 * HEAD
ADAM_LR, ADAM_B1, ADAM_B2, ADAM_EPS, ADAM_WD, ADAM_STEP = 0.001, 0.9, 0.999, 1e-08, 0.01, 10
N_DEV = 8
LANES = 128
VMEM_LIMIT = 48 * 1024 * 1024
MAX_K_TILE = 8192
NEG = -1e30

BIG = ['w_in', 'w_uq', 'w_ukv', 'w_mem_kv', 'w_o_gm', 'w_o_mla', 'w_o_mem', 'w_out', 'w_ff1', 'w_ff2']
BIG_AXIS = {'w_in': 1, 'w_uq': 1, 'w_ukv': 1, 'w_mem_kv': 0, 'w_o_gm': 1, 'w_o_mla': 0, 'w_o_mem': 1,
            'w_out': 0, 'w_ff1': 1, 'w_ff2': 0}
BIG_SHAPE = {'w_in': (1024, 5312), 'w_uq': (384, 1536), 'w_ukv': (256, 2048), 'w_mem_kv': (1024, 1024),
             'w_o_gm': (512, 1024), 'w_o_mla': (1024, 1024), 'w_o_mem': (512, 1024), 'w_out': (1024, 1024),
             'w_ff1': (1024, 4096), 'w_ff2': (4096, 1024)}
SMALL = ['g_mix', 'g_cq', 'g_ckv', 'g_q_nope', 'g_q_pe', 'g_k_nope', 'g_k_pe', 'g_gm_ln', 'b_gm_ln',
         'w_spatial', 'b_spatial', 'g_mem', 'g_mq', 'g_mk', 'g_ffn']
WEIGHTS = ['g_mix', 'w_in', 'g_cq', 'w_uq', 'g_ckv', 'w_ukv', 'g_q_nope', 'g_q_pe', 'g_k_nope', 'g_k_pe',
           'g_gm_ln', 'b_gm_ln', 'w_spatial', 'b_spatial', 'g_mem', 'w_mem_kv', 'g_mq', 'g_mk', 'w_o_gm',
           'w_o_mla', 'w_o_mem', 'w_out', 'g_ffn', 'w_ff1', 'w_ff2']


def _pick(n, target, mult=LANES):
    best = None
    t = mult
    while t <= min(n, target):
        if n % t == 0:
            best = t
        t += mult
    return best if best is not None else n


def _params(sem):
    return pltpu.CompilerParams(dimension_semantics=sem, vmem_limit_bytes=VMEM_LIMIT)


MESH = pl.DeviceIdType.MESH
HBM_SPEC = pl.BlockSpec(memory_space=pltpu.HBM)


class _Exchange:
    def __init__(self, srcs, scatter):
        self.srcs, self.scatter = list(srcs), scatter
        self.out_shapes = [jax.ShapeDtypeStruct(s.shape if scatter else (N_DEV,) + s.shape, s.dtype) for s in self.srcs]
        n = len(self.srcs)
        self.scratch = [pltpu.SemaphoreType.DMA((n, N_DEV - 1)), pltpu.SemaphoreType.DMA((n, N_DEV - 1)),
                        pltpu.SemaphoreType.DMA((n,))]

    def _copies(self, src_refs, dst_refs, send_sems, recv_sems, local_sems):
        x, y, c = lax.axis_index("x"), lax.axis_index("y"), lax.axis_index("c")
        me = 4 * x + 2 * y + c
        local, remote = [], []
        for a, (src_ref, dst_ref) in enumerate(zip(src_refs, dst_refs)):
            def mine_for(dev, src_ref=src_ref):
                return src_ref.at[dev] if self.scatter else src_ref

            local.append(pltpu.make_async_copy(mine_for(me), dst_ref.at[me], local_sems.at[a]))
            for k in range(1, N_DEV):
                px = 1 - x if k & 4 else x
                py = 1 - y if k & 2 else y
                pc = 1 - c if k & 1 else c
                remote.append(pltpu.make_async_remote_copy(
                    src_ref=mine_for(4 * px + 2 * py + pc), dst_ref=dst_ref.at[me], send_sem=send_sems.at[a, k - 1],
                    recv_sem=recv_sems.at[a, k - 1], device_id=(px, py, pc), device_id_type=MESH))
        return local, remote

    def start(self, *refs):
        local, remote = self._copies(*refs)
        for cp in local + remote:
            cp.start()

    def forward(self, *refs):
        pass

    def finish(self, *refs):
        local, remote = self._copies(*refs)
        for cp in remote + local:
            cp.wait()


class _Gather2:
    def __init__(self, srcs):
        self.srcs = list(srcs)
        self.out_shapes = [jax.ShapeDtypeStruct((N_DEV,) + s.shape, s.dtype) for s in self.srcs]
        n = len(self.srcs)
        self.scratch = [pltpu.SemaphoreType.DMA((n, N_DEV - 1)), pltpu.SemaphoreType.DMA((n, N_DEV - 1)),
                        pltpu.SemaphoreType.DMA((n,))]

    def _plan(self, src_refs, dst_refs, send_sems, recv_sems, local_sems):
        x, y, c = lax.axis_index("x"), lax.axis_index("y"), lax.axis_index("c")
        chips = [(1 - x, y), (x, 1 - y), (1 - x, 1 - y)]
        plans = []
        for a, (src_ref, dst_ref) in enumerate(zip(src_refs, dst_refs)):
            def copy(k, block, to, src=None, a=a, dst_ref=dst_ref):
                at = dst_ref.at[4 * block[0] + 2 * block[1] + block[2]]
                return pltpu.make_async_remote_copy(src_ref=at if src is None else src, dst_ref=at,
                                                    send_sem=send_sems.at[a, k], recv_sem=recv_sems.at[a, k],
                                                    device_id=to, device_id_type=MESH)

            local = pltpu.make_async_copy(src_ref, dst_ref.at[4 * x + 2 * y + c], local_sems.at[a])
            first = [copy(0, (x, y, c), (x, y, 1 - c), src=src_ref)]
            first += [copy(1 + j, (x, y, c), (*chip, c), src=src_ref) for j, chip in enumerate(chips)]
            passed = [copy(4 + j, (*chip, c), (x, y, 1 - c)) for j, chip in enumerate(chips)]
            arrivals = [copy(1 + j, (*chip, c), (x, y, c)) for j, chip in enumerate(chips)]
            late = [copy(0, (x, y, 1 - c), (x, y, c))] + [copy(4 + j, (*chip, 1 - c), (x, y, c)) for j, chip in enumerate(chips)]
            plans.append((local, first, passed, arrivals, late))
        return plans

    def start(self, *refs):
        for local, first, _, _, _ in self._plan(*refs):
            local.start()
            for cp in first:
                cp.start()

    def forward(self, *refs):
        for _, _, passed, arrivals, _ in self._plan(*refs):
            for arrived, onward in zip(arrivals, passed):
                arrived.wait_recv()
                onward.start()

    def finish(self, *refs):
        for local, first, passed, _, late in self._plan(*refs):
            for cp in late:
                cp.wait_recv()
            for cp in first + passed:
                cp.wait_send()
            local.wait()


def _call(body, rider, ins, *, name, grid, in_specs, out_specs, out_shape, scratch_shapes, sem):
    if rider is None:
        return pl.pallas_call(body, name=name, grid=grid, in_specs=in_specs, out_specs=out_specs, out_shape=out_shape,
                              scratch_shapes=scratch_shapes, compiler_params=_params(sem))(*ins)
    single = not isinstance(out_shape, (list, tuple))
    own_specs, own_shapes = ([out_specs], [out_shape]) if single else (list(out_specs), list(out_shape))
    n_in, n_out, n_sc, n_r = len(ins), len(own_shapes), len(scratch_shapes), len(rider.srcs)
    n_all_in = n_in + n_r

    def carrying(*refs):
        own_in, srcs = refs[:n_in], refs[n_in:n_in + n_r]
        own_out, dsts = refs[n_all_in:n_all_in + n_out], refs[n_all_in + n_out:n_all_in + n_out + n_r]
        own_sc = refs[n_all_in + n_out + n_r:n_all_in + n_out + n_r + n_sc]
        sems = refs[n_all_in + n_out + n_r + n_sc:]
        first = last = late = None
        for d, steps in enumerate(grid):
            f, l = pl.program_id(d) == 0, pl.program_id(d) == steps - 1
            t = pl.program_id(d) == ((3 * steps) // 4 if d == 0 else 0)
            first, last, late = (f, l, t) if first is None else (first & f, last & l, late & t)

        @pl.when(first)
        def _():
            rider.start(srcs, dsts, *sems)

        @pl.when(late)
        def _():
            rider.forward(srcs, dsts, *sems)

        body(*own_in, *own_out, *own_sc)

        @pl.when(last)
        def _():
            rider.finish(srcs, dsts, *sems)

    res = pl.pallas_call(
        carrying, name=name, grid=grid, in_specs=list(in_specs) + [HBM_SPEC] * n_r,
        out_specs=own_specs + [HBM_SPEC] * n_r, out_shape=own_shapes + rider.out_shapes,
        scratch_shapes=list(scratch_shapes) + rider.scratch, compiler_params=_params(("arbitrary",) * len(grid)),
    )(*ins, *rider.srcs)
    own = res[:n_out]
    return (own[0] if single else list(own)), list(res[n_out:])


def _matmul(a, b, mode, out_dtype, name, add=None, relu2_a=False, relu2_grad=None,
            tm_t=None, tn_t=None, tk_t=None, rider=None, m_rows=None, col_shards=None, sq_err_target=None):
    if mode == 'nn':
        (M, K), (K2, N) = a.shape, b.shape
    elif mode == 'nt':
        (M, K), (N, K2) = a.shape, b.shape
    else:
        (K, M), (K2, N) = a.shape, b.shape
    assert K == K2, (name, a.shape, b.shape)
    m_first = 0
    if m_rows is not None:
        assert mode == 'tn'
        m_first, M = m_rows
    if col_shards is not None:
        assert add is None and relu2_grad is None and sq_err_target is None and tn_t is None
    if mode == 'tn':
        d_tm, d_tn, d_tk = 1024, (2048 if M <= 512 else 1024), 2048
    else:
        wide = add is None and sq_err_target is None and jnp.dtype(out_dtype).itemsize == 2 and K <= D_FF
        d_tm, d_tn, d_tk = (2048 if K <= 1024 else 1024), (1024 if wide else 512), MAX_K_TILE
    tm, tn, tk = _pick(M, tm_t or d_tm), _pick(N, tn_t or d_tn), _pick(K, tk_t or d_tk)
    gm, gn, nk = M // tm, N // tn, K // tk
    if mode == 'nn':
        a_spec = pl.BlockSpec((tm, tk), lambda i, j, k: (i, k))
        b_spec = pl.BlockSpec((tk, tn), lambda i, j, k: (k, j))
        dims = (((1,), (0,)), ((), ()))
    elif mode == 'nt':
        a_spec = pl.BlockSpec((tm, tk), lambda i, j, k: (i, k))
        b_spec = pl.BlockSpec((tn, tk), lambda i, j, k: (j, k))
        dims = (((1,), (1,)), ((), ()))
    else:
        assert m_first % tm == 0
        a_spec = pl.BlockSpec((tk, tm), lambda i, j, k: (k, m_first // tm + i))
        b_spec = pl.BlockSpec((tk, tn), lambda i, j, k: (k, j))
        dims = (((0,), (0,)), ((), ()))
    o_spec = pl.BlockSpec((tm, tn), lambda i, j, k: (i, j))
    shard_w = N // col_shards if col_shards is not None else tn
    assert tn % shard_w == 0
    has_add, has_e, has_t = add is not None, relu2_grad is not None, sq_err_target is not None
    assert not has_t or (nk == 1 and tn % LANES == 0)

    def body(*refs):
        a_ref, b_ref = refs[0], refs[1]
        pos = 2
        add_ref = e_ref = t_ref = None
        if has_add:
            add_ref = refs[pos]
            pos += 1
        if has_e:
            e_ref = refs[pos]
            pos += 1
        if has_t:
            t_ref = refs[pos]
            pos += 1
        o_ref = refs[pos]
        acc_ref = refs[pos + 1] if nk > 1 else None

        av = a_ref[...]
        if relu2_a:
            av = jnp.maximum(av, 0)
            av = av * av
        prod = lax.dot_general(av.astype(BF16), b_ref[...].astype(BF16), dims, preferred_element_type=F32)

        def finish(r):
            if has_add:
                r = r + add_ref[...]
            if has_e:
                r = r * (2.0 * jnp.maximum(e_ref[...].astype(F32), 0.0))
            if has_t:
                err = r - t_ref[...]
                r = err * (1.0 / N)
                refs[pos + 1][...] = r.astype(BF16)
                sq = err * err
                part = sq[:, 0:LANES]
                for c in range(1, tn // LANES):
                    part = part + sq[:, c * LANES:(c + 1) * LANES]
                _acc_rows(refs[pos + 2], part, (pl.program_id(0) == 0) & (pl.program_id(1) == 0))
            if col_shards is not None:
                for s in range(tn // shard_w):
                    o_ref[s] = r[:, s * shard_w:(s + 1) * shard_w].astype(out_dtype)
            else:
                o_ref[...] = r.astype(out_dtype)

        if nk == 1:
            finish(prod)
        else:
            k = pl.program_id(2)

            @pl.when(k == 0)
            def _():
                acc_ref[...] = prod

            @pl.when(k > 0)
            def _():
                acc_ref[...] += prod

            @pl.when(k == nk - 1)
            def _():
                finish(acc_ref[...])

    ins, specs = [a, b], [a_spec, b_spec]
    if has_add:
        ins.append(add)
        specs.append(o_spec)
    if has_e:
        ins.append(relu2_grad)
        specs.append(o_spec)
    out_specs, out_shape, sem = o_spec, jax.ShapeDtypeStruct((M, N), out_dtype), ("parallel", "parallel", "arbitrary")
    if has_t:
        ins.append(sq_err_target)
        specs.append(o_spec)
        out_specs = [o_spec, o_spec, pl.BlockSpec((1, LANES), lambda i, j, k: (0, 0))]
        out_shape = [out_shape, jax.ShapeDtypeStruct((M, N), BF16), jax.ShapeDtypeStruct((1, LANES), F32)]
        sem = ("arbitrary", "arbitrary", "arbitrary")
    if col_shards is not None:
        out_specs = pl.BlockSpec((tn // shard_w, tm, shard_w), lambda i, j, k: (j, i, 0))
        out_shape = jax.ShapeDtypeStruct((col_shards, M, shard_w), out_dtype)
    return _call(body, rider, ins, name=name, grid=(gm, gn, nk), in_specs=specs, out_specs=out_specs, out_shape=out_shape,
                 scratch_shapes=[pltpu.VMEM((tm, tn), F32)] if nk > 1 else [], sem=sem)


ROW_BLOCK_BYTES = 12 * 1024 * 1024


def _row_tile(rows, row_bytes):
    return _pick(rows, max(16, min(1024, ROW_BLOCK_BYTES // row_bytes)), 16)


def _rowspec(tr, width, col=0):
    return pl.BlockSpec((tr, width), lambda i, col=col: (i, col))


def _fullspec(shape):
    nd = len(shape)
    return pl.BlockSpec(shape, lambda i, nd=nd: (0,) * nd)


def _rms(x, width):
    x = x.astype(F32)
    return lax.rsqrt(jnp.sum(x * x, axis=-1, keepdims=True) * (1.0 / width) + EPS)


def _rms_bwd_rows(x, g, dy, width):
    x, dy = x.astype(F32), dy.astype(F32)
    r = _rms(x, width)
    xh = x * r
    dn = dy * g
    dx = r * (dn - xh * (jnp.sum(dn * xh, axis=-1, keepdims=True) * (1.0 / width)))
    return dx, dy * xh


def _acc_rows(ref, val, first):
    s = jnp.sum(val, axis=0, keepdims=True)

    @pl.when(first)
    def _():
        ref[...] = s

    @pl.when(jnp.logical_not(first))
    def _():
        ref[...] += s


def _rms_fwd(x, g, name, rider=None):
    rows, width = x.shape
    tr = _row_tile(rows, 6 * width)

    def body(x_ref, g_ref, o_ref):
        xv = x_ref[...]
        o_ref[...] = (xv * _rms(xv, width) * g_ref[...]).astype(BF16)

    return _call(body, rider, [x, g], name=name, grid=(rows // tr,),
                 in_specs=[_rowspec(tr, width), _fullspec((1, width))], out_specs=_rowspec(tr, width),
                 out_shape=jax.ShapeDtypeStruct((rows, width), BF16), scratch_shapes=[], sem=("parallel",))


def _rms_bwd(x, g, dy, res, name, dx_dtypes=(F32,)):
    rows, width = x.shape
    tr = _row_tile(rows, 18 * width)
    has_res = res is not None
    n_in = 4 if has_res else 3

    def body(*refs):
        x_ref, g_ref, dy_ref = refs[:3]
        dx, dgv = _rms_bwd_rows(x_ref[...], g_ref[...], dy_ref[...], width)
        if has_res:
            dx = dx + refs[3][...]
        for ref, dt in zip(refs[n_in:], dx_dtypes):
            ref[...] = dx.astype(dt)
        _acc_rows(refs[-1], dgv, pl.program_id(0) == 0)

    ins = [x, g, dy] + ([res] if has_res else [])
    specs = [_rowspec(tr, width), _fullspec((1, width)), _rowspec(tr, width)] + ([_rowspec(tr, width)] if has_res else [])
    return pl.pallas_call(
        body, name=name, grid=(rows // tr,), in_specs=specs,
        out_specs=[_rowspec(tr, width)] * len(dx_dtypes) + [_fullspec((1, width))],
        out_shape=[jax.ShapeDtypeStruct((rows, width), dt) for dt in dx_dtypes] + [jax.ShapeDtypeStruct((1, width), F32)],
        compiler_params=_params(("arbitrary",)),
    )(*ins)


_GELU_C = math.sqrt(2.0 / math.pi)


def _gelu(x):
    t = jnp.tanh(_GELU_C * (x + 0.044715 * (x * x * x)))
    return 0.5 * x * (1.0 + t), t


def _gelu_grad(x, t):
    return 0.5 * (1.0 + t) + 0.5 * x * (1.0 - t * t) * (_GELU_C * (1.0 + 3.0 * 0.044715 * (x * x)))


def _gm_forward_rows(zu, zv, gln, bln, wc_ref, bst, n_chunk):
    u, tu = _gelu(zu)
    a, ta = _gelu(zv)
    mu = jnp.mean(a, axis=-1, keepdims=True)
    ac = a - mu
    rs = lax.rsqrt(jnp.mean(ac * ac, axis=-1, keepdims=True) + EPS)
    n = ac * rs
    v = n * gln + bln
    vb = v.astype(BF16)
    rows = []
    for c in range(n_chunk):
        cols = []
        for g in range(GM_GROUPS):
            vc = vb[c * GM_CHUNK:(c + 1) * GM_CHUNK, g * LANES:(g + 1) * LANES]
            mixed = jnp.dot(wc_ref[g], vc, preferred_element_type=F32) + bst[g]
            cols.append(mixed)
        rows.append(jnp.concatenate(cols, axis=1))
    mixed = jnp.concatenate(rows, axis=0) if n_chunk > 1 else rows[0]
    return u, tu, ta, n, rs, v, mixed


def _gm_fwd(z, gln, bln, wc, bst, name):
    rows = z.shape[0]
    tr = _pick(rows, 1024, GM_CHUNK)
    n_chunk = tr // GM_CHUNK

    def body(zu_ref, zv_ref, gln_ref, bln_ref, wc_ref, bst_ref, o_ref):
        u, _, _, _, _, _, mixed = _gm_forward_rows(zu_ref[...].astype(F32), zv_ref[...].astype(F32), gln_ref[...], bln_ref[...], wc_ref,
                                                   bst_ref, n_chunk)
        o_ref[...] = (u * mixed).astype(BF16)

    return pl.pallas_call(
        body, name=name, grid=(rows // tr,),
        in_specs=[_rowspec(tr, GM_WIDTH, Z_GM // GM_WIDTH), _rowspec(tr, GM_WIDTH, Z_GM // GM_WIDTH + 1),_fullspec((1, GM_WIDTH)), _fullspec((1, GM_WIDTH)),
                  _fullspec((GM_GROUPS, GM_CHUNK, GM_CHUNK)), _fullspec((GM_GROUPS, GM_CHUNK, LANES))],
        out_specs=_rowspec(tr, GM_WIDTH), out_shape=jax.ShapeDtypeStruct((rows, GM_WIDTH), BF16),
        compiler_params=_params(("parallel",)),
    )(z, z, gln, bln, wc, bst)


ANY_SPEC = pl.BlockSpec(memory_space=pl.ANY)


def _gm_bwd(z, dy, gln, bln, wc, wct, bst, dz, name):
    rows = z.shape[0]
    tr = _pick(rows, 1024, GM_CHUNK)
    n_chunk = tr // GM_CHUNK

    def body(zu_ref, zv_ref, dy_ref, gln_ref, bln_ref, wc_ref, wct_ref, bst_ref, _, dz_ref, dws_ref, dbs_ref, dgl_ref,
             dbl_ref):
        first = pl.program_id(0) == 0
        zu, zv, gln = zu_ref[...].astype(F32), zv_ref[...].astype(F32), gln_ref[...]
        u, tu, ta, n, rs, v, mixed = _gm_forward_rows(zu, zv, gln, bln_ref[...], wc_ref, bst_ref, n_chunk)
        dyv = dy_ref[...].astype(F32)
        dzu = dyv * mixed * _gelu_grad(zu, tu)
        dmix = dyv * u
        dmb = dmix.astype(BF16)
        vb = v.astype(BF16)
        dv_rows, dws, dbs = [], [None] * GM_GROUPS, None
        for c in range(n_chunk):
            rsl = slice(c * GM_CHUNK, (c + 1) * GM_CHUNK)
            cols = []
            for g in range(GM_GROUPS):
                csl = slice(g * LANES, (g + 1) * LANES)
                dmc = dmb[rsl, csl]
                cols.append(jnp.dot(wct_ref[g], dmc, preferred_element_type=F32))
                w_part = lax.dot_general(dmc, vb[rsl, csl], (((1,), (1,)), ((), ())), preferred_element_type=F32)
                dws[g] = w_part if dws[g] is None else dws[g] + w_part
            dv_rows.append(jnp.concatenate(cols, axis=1))
            dbs = dmix[rsl, :] if dbs is None else dbs + dmix[rsl, :]
        dv = jnp.concatenate(dv_rows, axis=0) if n_chunk > 1 else dv_rows[0]
        dn = dv * gln
        da = rs * (dn - jnp.mean(dn, axis=-1, keepdims=True) - n * jnp.mean(dn * n, axis=-1, keepdims=True))
        dzv = da * _gelu_grad(zv, ta)
        dz_ref[:, 0:GM_WIDTH] = dzu.astype(BF16)
        dz_ref[:, GM_WIDTH:2 * GM_WIDTH] = dzv.astype(BF16)
        _acc_rows(dgl_ref, dv * n, first)
        _acc_rows(dbl_ref, dv, first)

        @pl.when(first)
        def _():
            for g in range(GM_GROUPS):
                dws_ref[g] = dws[g]
            dbs_ref[...] = dbs

        @pl.when(jnp.logical_not(first))
        def _():
            for g in range(GM_GROUPS):
                dws_ref[g] += dws[g]
            dbs_ref[...] += dbs

    wspec = _fullspec((GM_GROUPS, GM_CHUNK, GM_CHUNK))
    return pl.pallas_call(
        body, name=name, grid=(rows // tr,),
        in_specs=[_rowspec(tr, GM_WIDTH, Z_GM // GM_WIDTH), _rowspec(tr, GM_WIDTH, Z_GM // GM_WIDTH + 1),
                  _rowspec(tr, GM_WIDTH), _fullspec((1, GM_WIDTH)), _fullspec((1, GM_WIDTH)), wspec, wspec, wspec, ANY_SPEC],
        out_specs=[_rowspec(tr, 2 * GM_WIDTH, Z_GM // (2 * GM_WIDTH)), wspec, _fullspec((GM_CHUNK, GM_WIDTH)),
                   _fullspec((1, GM_WIDTH)), _fullspec((1, GM_WIDTH))],
        out_shape=[jax.ShapeDtypeStruct(dz.shape, dz.dtype), jax.ShapeDtypeStruct((GM_GROUPS, GM_CHUNK, GM_CHUNK), F32),
                   jax.ShapeDtypeStruct((GM_CHUNK, GM_WIDTH), F32), jax.ShapeDtypeStruct((1, GM_WIDTH), F32),
                   jax.ShapeDtypeStruct((1, GM_WIDTH), F32)],
        input_output_aliases={8: 0}, compiler_params=_params(("arbitrary",)),
    )(z, z, dy, gln, bln, wc, wct, bst, dz)


def _lat_fwd(z, g_cq, g_ckv, name):
    rows = z.shape[0]
    tr = _row_tile(rows, 4 * MLA_W)

    def body(z_ref, gq_ref, gkv_ref, nq_ref, nkv_ref):
        zb = z_ref[...]
        cq, ckv = zb[:, 0:Q_LORA], zb[:, Q_LORA:Q_LORA + KV_LORA]
        nq_ref[...] = (cq * _rms(cq, Q_LORA) * gq_ref[...]).astype(BF16)
        nkv_ref[...] = (ckv * _rms(ckv, KV_LORA) * gkv_ref[...]).astype(BF16)

    return pl.pallas_call(
        body, name=name, grid=(rows // tr,),
        in_specs=[_rowspec(tr, MLA_W, Z_MLA // MLA_W), _fullspec((1, Q_LORA)), _fullspec((1, KV_LORA))],
        out_specs=[_rowspec(tr, Q_LORA), _rowspec(tr, KV_LORA)],
        out_shape=[jax.ShapeDtypeStruct((rows, Q_LORA), BF16), jax.ShapeDtypeStruct((rows, KV_LORA), BF16)],
        compiler_params=_params(("parallel",)),
    )(z, g_cq, g_ckv)


def _lat_bwd(z, dnq, dnkv, dkpe, g_cq, g_ckv, dz, name):
    rows = z.shape[0]
    tr = _row_tile(rows, 8 * MLA_W)

    def body(z_ref, dnq_ref, dnkv_ref, dkpe_ref, gq_ref, gkv_ref, _, dz_ref, dgq_ref, dgkv_ref):
        first = pl.program_id(0) == 0
        zb = z_ref[...]
        dcq, dgq = _rms_bwd_rows(zb[:, 0:Q_LORA], gq_ref[...], dnq_ref[...], Q_LORA)
        dckv, dgkv = _rms_bwd_rows(zb[:, Q_LORA:Q_LORA + KV_LORA], gkv_ref[...], dnkv_ref[...], KV_LORA)
        dz_ref[:, 0:Q_LORA] = dcq.astype(BF16)
        dz_ref[:, Q_LORA:Q_LORA + KV_LORA] = dckv.astype(BF16)
        dz_ref[:, Q_LORA + KV_LORA:MLA_W] = dkpe_ref[...].astype(BF16)
        _acc_rows(dgq_ref, dgq, first)
        _acc_rows(dgkv_ref, dgkv, first)

    return pl.pallas_call(
        body, name=name, grid=(rows // tr,),
        in_specs=[_rowspec(tr, MLA_W, Z_MLA // MLA_W), _rowspec(tr, Q_LORA), _rowspec(tr, KV_LORA), _rowspec(tr, LANES),
                  _fullspec((1, Q_LORA)), _fullspec((1, KV_LORA)), ANY_SPEC],
        out_specs=[_rowspec(tr, MLA_W, Z_MLA // MLA_W), _fullspec((1, Q_LORA)), _fullspec((1, KV_LORA))],
        out_shape=[jax.ShapeDtypeStruct(dz.shape, dz.dtype), jax.ShapeDtypeStruct((1, Q_LORA), F32),
                   jax.ShapeDtypeStruct((1, KV_LORA), F32)],
        input_output_aliases={6: 0}, compiler_params=_params(("arbitrary",)),
    )(z, dnq, dnkv, dkpe, g_cq, g_ckv, dz)


def _rope(y, cc, ss):
    return y * cc + pltpu.roll(y, 64, 1) * ss


def _rope_bwd(d, cc, ss):
    return d * cc + pltpu.roll(d * ss, 64, 1)


def _qk_fwd(q, kv, z, cc, ss, gqn, gqp, gkn, gkp, name):
    rows = q.shape[0]
    W = MLA_HEADS * HEAD
    tr = _row_tile(rows, 20 * W)
    QS = MLA_SCALE * LOG2E

    def body(q_ref, kv_ref, kpe_ref, cc_ref, ss_ref, gqn_ref, gqp_ref, gkn_ref, gkp_ref, qc_ref, kc_ref, v_ref):
        cc, ss = cc_ref[...], ss_ref[...]
        kpe = kpe_ref[...]
        kp = _rope(kpe * _rms(kpe, MLA_ROPE) * gkp_ref[...], cc, ss).astype(BF16)
        for h in range(MLA_HEADS):
            qn = q_ref[:, h * HEAD:(h + 1) * HEAD]
            qp = q_ref[:, W + h * HEAD:W + (h + 1) * HEAD]
            kn = kv_ref[:, h * HEAD:(h + 1) * HEAD]
            qc_ref[:, h * QCAT:h * QCAT + HEAD] = (qn * _rms(qn, HEAD) * gqn_ref[...] * QS).astype(BF16)
            qc_ref[:, h * QCAT + HEAD:(h + 1) * QCAT] = (_rope(qp * _rms(qp, MLA_ROPE) * gqp_ref[...], cc, ss) * QS).astype(BF16)
            kc_ref[:, h * QCAT:h * QCAT + HEAD] = (kn * _rms(kn, HEAD) * gkn_ref[...]).astype(BF16)
            kc_ref[:, h * QCAT + HEAD:(h + 1) * QCAT] = kp
        v_ref[...] = kv_ref[:, W:2 * W].astype(BF16)

    g = _fullspec((1, HEAD))
    return _call(
        body, None, [q, kv, z, cc, ss, gqn, gqp, gkn, gkp], name=name, grid=(rows // tr,),
        in_specs=[_rowspec(tr, 2 * W), _rowspec(tr, 2 * W), _rowspec(tr, LANES, Z_KPE // LANES), _rowspec(tr, LANES),
                  _rowspec(tr, LANES), g, g, g, g],
        out_specs=[_rowspec(tr, MLA_HEADS * QCAT), _rowspec(tr, MLA_HEADS * QCAT), _rowspec(tr, W)],
        out_shape=[jax.ShapeDtypeStruct((rows, MLA_HEADS * QCAT), BF16), jax.ShapeDtypeStruct((rows, MLA_HEADS * QCAT), BF16),
                   jax.ShapeDtypeStruct((rows, W), BF16)],
        scratch_shapes=[], sem=("parallel",))


def _qk_bwd(q, kv, z, cc, ss, gqn, gqp, gkn, gkp, dqc, dkc, dv, name):
    rows = q.shape[0]
    W = MLA_HEADS * HEAD
    tr = _row_tile(rows, 24 * W)

    def body(q_ref, kv_ref, kpe_ref, cc_ref, ss_ref, gqn_ref, gqp_ref, gkn_ref, gkp_ref, dqc_ref, dkc_ref, dv_ref,
             dq_ref, dkv_ref, dkpe_ref, dgqn_ref, dgqp_ref, dgkn_ref, dgkp_ref):
        first = pl.program_id(0) == 0
        cc, ss = cc_ref[...], ss_ref[...]
        sqn = sqp = skn = dkp = None
        for h in range(MLA_HEADS):
            dx, dg = _rms_bwd_rows(q_ref[:, h * HEAD:(h + 1) * HEAD], gqn_ref[...], dqc_ref[:, h * QCAT:h * QCAT + HEAD], HEAD)
            dq_ref[:, h * HEAD:(h + 1) * HEAD] = dx.astype(BF16)
            sqn = dg if sqn is None else sqn + dg
            dy = _rope_bwd(dqc_ref[:, h * QCAT + HEAD:(h + 1) * QCAT], cc, ss)
            dx, dg = _rms_bwd_rows(q_ref[:, W + h * HEAD:W + (h + 1) * HEAD], gqp_ref[...], dy, MLA_ROPE)
            dq_ref[:, W + h * HEAD:W + (h + 1) * HEAD] = dx.astype(BF16)
            sqp = dg if sqp is None else sqp + dg
            dx, dg = _rms_bwd_rows(kv_ref[:, h * HEAD:(h + 1) * HEAD], gkn_ref[...], dkc_ref[:, h * QCAT:h * QCAT + HEAD], HEAD)
            dkv_ref[:, h * HEAD:(h + 1) * HEAD] = dx.astype(BF16)
            skn = dg if skn is None else skn + dg
            part = dkc_ref[:, h * QCAT + HEAD:(h + 1) * QCAT].astype(F32)
            dkp = part if dkp is None else dkp + part
        dkv_ref[:, W:2 * W] = dv_ref[...].astype(BF16)
        dx, dg = _rms_bwd_rows(kpe_ref[...], gkp_ref[...], _rope_bwd(dkp, cc, ss), MLA_ROPE)
        dkpe_ref[...] = dx
        _acc_rows(dgqn_ref, sqn, first)
        _acc_rows(dgqp_ref, sqp, first)
        _acc_rows(dgkn_ref, skn, first)
        _acc_rows(dgkp_ref, dg, first)

    g = _fullspec((1, HEAD))
    gs = jax.ShapeDtypeStruct((1, HEAD), F32)
    return pl.pallas_call(
        body, name=name, grid=(rows // tr,),
        in_specs=[_rowspec(tr, 2 * W), _rowspec(tr, 2 * W), _rowspec(tr, LANES, Z_KPE // LANES), _rowspec(tr, LANES),
                  _rowspec(tr, LANES), g, g, g, g, _rowspec(tr, MLA_HEADS * QCAT), _rowspec(tr, MLA_HEADS * QCAT),
                  _rowspec(tr, W)],
        out_specs=[_rowspec(tr, 2 * W), _rowspec(tr, 2 * W), _rowspec(tr, LANES), g, g, g, g],
        out_shape=[jax.ShapeDtypeStruct((rows, 2 * W), BF16), jax.ShapeDtypeStruct((rows, 2 * W), BF16),
                   jax.ShapeDtypeStruct((rows, LANES), F32), gs, gs, gs, gs],
        compiler_params=_params(("arbitrary",)),
    )(q, kv, z, cc, ss, gqn, gqp, gkn, gkp, dqc, dkc, dv)


def _headnorm_fwd(x, col, nheads, g, out_scale, name):
    rows = x.shape[0]
    W = nheads * HEAD
    tr = _row_tile(rows, 6 * W)

    def body(x_ref, g_ref, o_ref):
        for h in range(nheads):
            xv = x_ref[:, h * HEAD:(h + 1) * HEAD]
            o_ref[:, h * HEAD:(h + 1) * HEAD] = (xv * _rms(xv, HEAD) * g_ref[...] * out_scale).astype(BF16)

    return pl.pallas_call(
        body, name=name, grid=(rows // tr,),
        in_specs=[_rowspec(tr, W, col), _fullspec((1, HEAD))], out_specs=_rowspec(tr, W),
        out_shape=jax.ShapeDtypeStruct((rows, W), BF16), compiler_params=_params(("parallel",)),
    )(x, g)


def _headnorm_bwd(x, col, nheads, g, dy, tail, name, into=None):
    rows = x.shape[0]
    W = nheads * HEAD
    tr = _row_tile(rows, 12 * W)
    has_tail = tail is not None
    WO = 2 * W if has_tail else W

    def body(*refs):
        if into is not None:
            x_ref, g_ref, dy_ref, _, dx_ref, dg_ref = refs
        elif has_tail:
            x_ref, g_ref, dy_ref, t_ref, dx_ref, dg_ref = refs
        else:
            x_ref, g_ref, dy_ref, dx_ref, dg_ref = refs
        acc = None
        for h in range(nheads):
            sl = slice(h * HEAD, (h + 1) * HEAD)
            dx, dg = _rms_bwd_rows(x_ref[:, sl], g_ref[...], dy_ref[:, sl], HEAD)
            dx_ref[:, sl] = dx.astype(BF16)
            acc = dg if acc is None else acc + dg
        if has_tail:
            dx_ref[:, W:2 * W] = t_ref[...].astype(BF16)
        _acc_rows(dg_ref, acc, pl.program_id(0) == 0)

    ins = [x, g, dy] + ([tail] if has_tail else [])
    specs = [_rowspec(tr, W, col), _fullspec((1, HEAD)), _rowspec(tr, W)] + ([_rowspec(tr, W)] if has_tail else [])
    dx_spec, dx_shape, aliases = _rowspec(tr, WO), jax.ShapeDtypeStruct((rows, WO), BF16), {}
    if into is not None:
        assert not has_tail
        ins, specs = ins + [into[0]], specs + [ANY_SPEC]
        dx_spec, dx_shape, aliases = _rowspec(tr, W, into[1]), jax.ShapeDtypeStruct(into[0].shape, into[0].dtype), {3: 0}
    return pl.pallas_call(
        body, name=name, grid=(rows // tr,), in_specs=specs,
        out_specs=[dx_spec, _fullspec((1, HEAD))], out_shape=[dx_shape, jax.ShapeDtypeStruct((1, HEAD), F32)],
        input_output_aliases=aliases, compiler_params=_params(("arbitrary",)),
    )(*ins)


def _sigmoid(x):
    return 1.0 / (1.0 + jnp.exp(-x.astype(F32)))


def _merge_fwd(z, y_gm, y_mla, y_mem, name):
    rows = z.shape[0]
    tr = _row_tile(rows, 14 * D_MODEL)

    def body(g0_ref, g1_ref, g2_ref, a_ref, b_ref, c_ref, o_ref):
        m = _sigmoid(g0_ref[...]) * a_ref[...] + _sigmoid(g1_ref[...]) * b_ref[...] + _sigmoid(g2_ref[...]) * c_ref[...]
        o_ref[...] = m.astype(BF16)

    r = _rowspec(tr, D_MODEL)
    return pl.pallas_call(
        body, name=name, grid=(rows // tr,),
        in_specs=[_rowspec(tr, D_MODEL, 0), _rowspec(tr, D_MODEL, 1), _rowspec(tr, D_MODEL, 2),r, r, r],
        out_specs=r, out_shape=jax.ShapeDtypeStruct((rows, D_MODEL), BF16), compiler_params=_params(("parallel",)),
    )(z, z, z, y_gm, y_mla, y_mem)


def _merge_bwd(z, y_gm, y_mla, y_mem, dm, name):
    rows = z.shape[0]
    tr = _row_tile(rows, 24 * D_MODEL)

    def body(g0_ref, g1_ref, g2_ref, a_ref, b_ref, c_ref, dm_ref, da_ref, db_ref, dc_ref, dzg_ref):
        dmv = dm_ref[...].astype(F32)
        for k, (g_ref, y_ref, dy_ref) in enumerate(((g0_ref, a_ref, da_ref), (g1_ref, b_ref, db_ref), (g2_ref, c_ref, dc_ref))):
            s = _sigmoid(g_ref[...])
            dy_ref[...] = (dmv * s).astype(BF16)
            dzg_ref[:, k * D_MODEL:(k + 1) * D_MODEL] = (dmv * y_ref[...] * s * (1.0 - s)).astype(BF16)

    r = _rowspec(tr, D_MODEL)
    o = jax.ShapeDtypeStruct((rows, D_MODEL), BF16)
    return pl.pallas_call(
        body, name=name, grid=(rows // tr,),
        in_specs=[_rowspec(tr, D_MODEL, 0), _rowspec(tr, D_MODEL, 1), _rowspec(tr, D_MODEL, 2),r, r, r, r],
        out_specs=[r, r, r, _rowspec(tr, 3 * D_MODEL, 0)],
        out_shape=[o, o, o, jax.ShapeDtypeStruct((rows, Z_COLS), BF16)],
        compiler_params=_params(("parallel",)),
    )(z, z, z, y_gm, y_mla, y_mem, dm)


_NT = (((1,), (1,)), ((), ()))
_TN = (((0,), (0,)), ((), ()))


def _diag_mask(s):
    row = lax.broadcasted_iota(jnp.int32, s.shape, 0)
    col = lax.broadcasted_iota(jnp.int32, s.shape, 1)
    return jnp.where(row >= col, s, NEG)


def _attn_fwd(q, k, v, nb, nheads, dk, v_col0, causal, name, rider=None):
    S, Skv = q.shape[0] // nb, k.shape[0] // nb
    tq = _pick(Skv, ATT_TILE) if causal else _pick(S, 4 * ATT_TILE)
    nq = S // tq

    def body(q_ref, k_ref, v_ref, o_ref, lse_ref):
        for i in range(nq):
            r0 = i * tq
            qb = q_ref[r0:r0 + tq, :]
            if causal:
                spans = ([(0, r0, False)] if i > 0 else []) + [(r0, r0 + tq, True)]
            else:
                spans = [(0, Skv, False)]
            scores = []
            for a, b, masked in spans:
                s = lax.dot_general(qb, k_ref[a:b, :], _NT, preferred_element_type=F32)
                scores.append(_diag_mask(s) if masked else s)
            m = functools.reduce(jnp.maximum, [jnp.max(s, axis=-1, keepdims=True) for s in scores])
            l = acc = None
            for s, (a, b, _) in zip(scores, spans):
                p = jnp.exp2(s - m)
                lp = jnp.sum(p, axis=-1, keepdims=True)
                ap = jnp.dot(p.astype(BF16), v_ref[a:b, :].astype(BF16), preferred_element_type=F32)
                l, acc = (lp, ap) if l is None else (l + lp, acc + ap)
            o_ref[r0:r0 + tq, :] = (acc / l).astype(BF16)
            lse_ref[r0:r0 + tq, :] = m + jnp.log2(l)

    ins = [q, k, v]
    in_specs = [pl.BlockSpec((S, dk), lambda b, h: (b, h)), pl.BlockSpec((Skv, dk), lambda b, h: (b, h)),
                pl.BlockSpec((Skv, HEAD), lambda b, h: (b, v_col0 + h))]
    out_specs = [pl.BlockSpec((S, HEAD), lambda b, h: (b, h)), pl.BlockSpec((None, S, 1), lambda b, h: (h, b, 0))]
    out_shape = [jax.ShapeDtypeStruct((nb * S, nheads * HEAD), BF16), jax.ShapeDtypeStruct((nheads, nb * S, 1), F32)]
    return _call(body, rider, ins, name=name, grid=(nb, nheads), in_specs=in_specs, out_specs=out_specs,
                 out_shape=out_shape, scratch_shapes=[], sem=("parallel", "parallel"))


def _attn_bwd(q, k, v, o, do, lse, nb, nheads, dk, v_col0, scale, causal, name, rider=None):
    S, Skv = q.shape[0] // nb, k.shape[0] // nb
    tk = _pick(Skv, ATT_TILE)
    nkv = Skv // tk

    def body(q_ref, k_ref, v_ref, o_ref, do_ref, lse_ref, dq_ref, dk_ref, dv_ref, delta_ref, dob_ref, dqa_ref):
        dov = do_ref[...]
        delta_ref[...] = jnp.sum(o_ref[...].astype(F32) * dov.astype(F32), axis=-1, keepdims=True)
        dob_ref[...] = dov.astype(BF16)

        for j in range(nkv):
            c0 = j * tk
            kb = k_ref[c0:c0 + tk, :]
            vb = v_ref[c0:c0 + tk, :].astype(BF16)
            if causal:
                spans = [(c0, c0 + tk, True)] + ([(c0 + tk, S, False)] if c0 + tk < S else [])
            else:
                spans = [(0, S, False)]
            dk_acc = dv_acc = None
            for a, b, masked in spans:
                qb = q_ref[a:b, :]
                dob = dob_ref[a:b, :]
                s = lax.dot_general(qb, kb, _NT, preferred_element_type=F32)
                if masked:
                    s = _diag_mask(s)
                p = jnp.exp2(s - lse_ref[a:b, :])
                dp = lax.dot_general(dob, vb, _NT, preferred_element_type=F32)
                ds = (p * (dp - delta_ref[a:b, :])).astype(BF16)
                dv_p = lax.dot_general(p.astype(BF16), dob, _TN, preferred_element_type=F32)
                dk_p = lax.dot_general(ds, qb, _TN, preferred_element_type=F32)
                dk_acc, dv_acc = (dk_p, dv_p) if dk_acc is None else (dk_acc + dk_p, dv_acc + dv_p)
                dq_p = jnp.dot(ds, kb, preferred_element_type=F32) * scale
                if j == 0:
                    dqa_ref[a:b, :] = dq_p
                else:
                    dqa_ref[a:b, :] += dq_p
            dk_ref[c0:c0 + tk, :] = (dk_acc * LN2).astype(BF16)
            dv_ref[c0:c0 + tk, :] = dv_acc.astype(BF16)
        dq_ref[...] = dqa_ref[...].astype(BF16)

    ins = [q, k, v, o, do, lse]
    in_specs = [pl.BlockSpec((S, dk), lambda b, h: (b, h)), pl.BlockSpec((Skv, dk), lambda b, h: (b, h)),
                pl.BlockSpec((Skv, HEAD), lambda b, h: (b, v_col0 + h)), pl.BlockSpec((S, HEAD), lambda b, h: (b, h)),
                pl.BlockSpec((S, HEAD), lambda b, h: (b, h)), pl.BlockSpec((None, S, 1), lambda b, h: (h, b, 0))]
    out_specs = [pl.BlockSpec((S, dk), lambda b, h: (b, h)), pl.BlockSpec((Skv, dk), lambda b, h: (b, h)),
                 pl.BlockSpec((Skv, HEAD), lambda b, h: (b, h))]
    out_shape = [jax.ShapeDtypeStruct((nb * S, nheads * dk), BF16), jax.ShapeDtypeStruct((nb * Skv, nheads * dk), BF16),
                 jax.ShapeDtypeStruct((nb * Skv, nheads * HEAD), BF16)]
    return _call(body, rider, ins, name=name, grid=(nb, nheads), in_specs=in_specs, out_specs=out_specs,
                 out_shape=out_shape,
                 scratch_shapes=[pltpu.VMEM((S, 1), F32), pltpu.VMEM((S, HEAD), BF16), pltpu.VMEM((S, dk), F32)],
                 sem=("parallel", "parallel"))


def _spread_rope(a):
    zero = jnp.zeros(a.shape[:-1] + (32,), a.dtype)
    return jnp.concatenate([a[..., :32], zero, a[..., 32:], zero], axis=-1)


def _gather_rope(a):
    return jnp.concatenate([a[..., 0:32], a[..., 64:96]], axis=-1)


def _win_layout(w):
    return jnp.concatenate([w[:, C_ZG:C_END], w[:, C_ZU:C_CQ], w[:, C_QM:C_ZG], w[:, C_CQ:C_CKV], w[:, C_CKV:C_KPE],
                            _spread_rope(w[:, C_KPE:C_QM])], axis=1)


def _win_unlayout(d):
    return jnp.concatenate([d[:, Z_GM:Z_QM], d[:, Z_MLA:Z_MLA + Q_LORA], d[:, Z_MLA + Q_LORA:Z_KPE],
                            _gather_rope(d[:, Z_KPE:Z_COLS]), d[:, Z_QM:Z_MLA], d[:, 0:Z_GM]], axis=1)


def _wuq_layout(w):
    r = w.reshape(Q_LORA, MLA_HEADS, HEAD + MLA_ROPE)
    return jnp.concatenate([r[:, :, :HEAD].reshape(Q_LORA, -1), _spread_rope(r[:, :, HEAD:]).reshape(Q_LORA, -1)], axis=1)


def _wuq_unlayout(d):
    n = d[:, :MLA_HEADS * HEAD].reshape(Q_LORA, MLA_HEADS, HEAD)
    p = _gather_rope(d[:, MLA_HEADS * HEAD:].reshape(Q_LORA, MLA_HEADS, HEAD))
    return jnp.concatenate([n, p], axis=-1).reshape(Q_LORA, -1)


def _wukv_layout(w):
    r = w.reshape(KV_LORA, MLA_HEADS, 2 * HEAD)
    return jnp.concatenate([r[:, :, :HEAD].reshape(KV_LORA, -1), r[:, :, HEAD:].reshape(KV_LORA, -1)], axis=1)


def _wukv_unlayout(d):
    k = d[:, :MLA_HEADS * HEAD].reshape(KV_LORA, MLA_HEADS, HEAD)
    v = d[:, MLA_HEADS * HEAD:].reshape(KV_LORA, MLA_HEADS, HEAD)
    return jnp.concatenate([k, v], axis=-1).reshape(KV_LORA, -1)


AG_MID = ['w_uq', 'w_ukv', 'w_mem_kv', 'w_o_gm', 'w_o_mla', 'w_o_mem', 'w_out']
AG_FFN = ['w_ff1', 'w_ff2']
RS_GROUPS = {'ffn_proj': ['w_ff2', 'w_ff1', 'w_out', 'w_o_gm', 'w_o_mla', 'w_o_mem'],
             'lat': ['w_uq', 'w_ukv', 'w_mem_kv'], 'in_top': ['w_in'], 'in_bot': ['w_in']}


def _unride(res, rider):
    return (res, None) if rider is None else res


def _local_step(x, mem, positions, target, P, ws):
    B, S, _ = x.shape
    M = mem.shape[1]
    T = B * S
    x2d = x.reshape(T, D_MODEL)
    mem2d = mem.reshape(B * M, D_MODEL)
    tgt2d = target.reshape(T, D_MODEL)

    def row(v):
        return v.reshape(1, -1).astype(F32)

    inv_freq = ROPE_BASE ** (-jnp.arange(0, MLA_ROPE, 2, dtype=F32) / MLA_ROPE)
    zero = jnp.zeros_like(inv_freq)
    ang = positions.reshape(T).astype(F32)[:, None] * jnp.concatenate([inv_freq, zero, inv_freq, zero])
    cc = jnp.cos(ang) * jnp.concatenate([zero + 1.0, zero, zero + 1.0, zero])
    ss = jnp.sin(ang) * jnp.concatenate([zero - 1.0, zero, zero + 1.0, zero])

    g_mix, g_cq, g_ckv, g_ffn, g_mem = row(P['g_mix']), row(P['g_cq']), row(P['g_ckv']), row(P['g_ffn']), row(P['g_mem'])
    gqn, gkn, gmq, gmk = row(P['g_q_nope']), row(P['g_k_nope']), row(P['g_mq']), row(P['g_mk'])
    gqp, gkp = _spread_rope(row(P['g_q_pe'])), _spread_rope(row(P['g_k_pe']))
    gln, bln = row(P['g_gm_ln']), row(P['b_gm_ln'])
    wc = jnp.tril(P['w_spatial'].astype(F32))
    wct = jnp.swapaxes(wc, 1, 2).astype(BF16)
    wc = wc.astype(BF16)
    bst = jnp.broadcast_to(P['b_spatial'].astype(F32)[:, :, None], (GM_GROUPS, GM_CHUNK, LANES))

    ride = ws.gather(['w_in'])
    h, got = _unride(_rms_fwd(x2d, g_mix, "rms_mix", rider=ride), ride)
    w_in = _win_layout(ws.gathered(['w_in'], got)['w_in']).astype(BF16)
    ride = ws.gather(AG_MID)
    z, got = _unride(_matmul(h, w_in, 'nn', ACT, "mm_in", tn_t=1792, rider=ride), ride)
    mid = ws.gathered(AG_MID, got)
    w_uq, w_ukv = _wuq_layout(mid['w_uq']).astype(BF16), _wukv_layout(mid['w_ukv']).astype(BF16)
    w_mem_kv, w_o_gm, w_o_mla, w_o_mem, w_out = (mid[n] for n in ('w_mem_kv', 'w_o_gm', 'w_o_mla', 'w_o_mem', 'w_out'))
    ygm_pre = _gm_fwd(z, gln, bln, wc, bst, "gm_fwd")
    y_gm = _matmul(ygm_pre, w_o_gm, 'nn', ACT, "mm_o_gm")
    nq, nkv = _lat_fwd(z, g_cq, g_ckv, "lat_fwd")
    q = _matmul(nq, w_uq, 'nn', ACT, "mm_uq")
    kv = _matmul(nkv, w_ukv, 'nn', ACT, "mm_ukv")
    qcat, kcat, vv = _qk_fwd(q, kv, z, cc, ss, gqn, gqp, gkn, gkp, "qk_fwd")
    ride = ws.gather(AG_FFN)
    (o, lse), got = _unride(_attn_fwd(qcat, kcat, vv, B, MLA_HEADS, QCAT, 0, True, "mla_attn_fwd", rider=ride), ride)
    ffn = ws.gathered(AG_FFN, got)
    w_ff1, w_ff2 = ffn['w_ff1'], ffn['w_ff2']
    y_mla = _matmul(o, w_o_mla, 'nn', ACT, "mm_o_mla")
    nm = _rms_fwd(mem2d, g_mem, "rms_mem")
    kvm = _matmul(nm, w_mem_kv, 'nn', ACT, "mm_mem_kv")
    qm = _headnorm_fwd(z, Z_QM // (MEM_HEADS * HEAD), MEM_HEADS, gmq, MEM_SCALE * LOG2E, "memq_fwd")
    km = _headnorm_fwd(kvm, 0, MEM_HEADS, gmk, 1.0, "memk_fwd")
    om, lse_m = _attn_fwd(qm, km, kvm, B, MEM_HEADS, HEAD, MEM_HEADS, False, "mem_attn_fwd")
    y_mem = _matmul(om, w_o_mem, 'nn', ACT, "mm_o_mem")
    merged = _merge_fwd(z, y_gm, y_mla, y_mem, "merge_fwd")
    x1 = _matmul(merged, w_out, 'nn', F32, "mm_out", add=x2d)
    h2 = _rms_fwd(x1, g_ffn, "rms_ffn")
    a1 = _matmul(h2, w_ff1, 'nn', BF16, "mm_ff1")
    dx2, dx2b, loss_part = _matmul(a1, w_ff2, 'nn', F32, "mm_ff2", add=x1, relu2_a=True, sq_err_target=tgt2d)

    G = {}
    d_ff2 = _matmul(a1, dx2b, 'tn', BF16, "mm_d_ff2", relu2_a=True)
    da1 = _matmul(dx2b, w_ff2, 'nt', BF16, "mm_da1", relu2_grad=a1)
    d_ff1 = _matmul(h2, da1, 'tn', BF16, "mm_d_ff1", col_shards=N_DEV)
    dh2 = _matmul(da1, w_ff1, 'nt', ACT, "mm_dh2")
    dx1, dx1b, G['g_ffn'] = _rms_bwd(x1, g_ffn, dh2, dx2, "rms_ffn_bwd", dx_dtypes=(F32, BF16))
    d_out = _matmul(merged, dx1b, 'tn', BF16, "mm_d_out")
    dmerged = _matmul(dx1b, w_out, 'nt', ACT, "mm_dmerged")
    dy_gm, dy_mla, dy_mem, dz = _merge_bwd(z, y_gm, y_mla, y_mem, dmerged, "merge_bwd")
    d_o_gm = _matmul(ygm_pre, dy_gm, 'tn', BF16, "mm_d_o_gm")
    d_o_mla = _matmul(o, dy_mla, 'tn', BF16, "mm_d_o_mla")
    d_o_mem = _matmul(om, dy_mem, 'tn', BF16, "mm_d_o_mem")
    dygm_pre = _matmul(dy_gm, w_o_gm, 'nt', ACT, "mm_dygm")
    dz, dws, dbs, G['g_gm_ln'], G['b_gm_ln'] = _gm_bwd(z, dygm_pre, gln, bln, wc, wct, bst, dz, "gm_bwd")
    G['w_spatial'] = jnp.tril(dws)
    G['b_spatial'] = jnp.sum(dbs.reshape(GM_CHUNK, GM_GROUPS, LANES), axis=-1).T
    do = _matmul(dy_mla, w_o_mla, 'nt', ACT, "mm_do")
    ride = ws.scatter('ffn_proj', {'w_ff2': d_ff2, 'w_ff1': d_ff1, 'w_out': d_out, 'w_o_gm': d_o_gm, 'w_o_mla': d_o_mla,
                                   'w_o_mem': d_o_mem})
    (dqc, dkc, dvv), got = _unride(_attn_bwd(qcat, kcat, vv, o, do, lse, B, MLA_HEADS, QCAT, 0, MLA_SCALE, True,
                                             "mla_attn_bwd", rider=ride), ride)
    ws.scattered('ffn_proj', got)
    dq, dkv, dkpe, G['g_q_nope'], dgqp, G['g_k_nope'], dgkp = _qk_bwd(q, kv, z, cc, ss, gqn, gqp, gkn, gkp, dqc, dkc, dvv,
                                                                     "qk_bwd")
    G['g_q_pe'], G['g_k_pe'] = _gather_rope(dgqp), _gather_rope(dgkp)
    d_uq = _wuq_unlayout(_matmul(nq, dq, 'tn', BF16, "mm_d_uq"))
    dnq = _matmul(dq, w_uq, 'nt', ACT, "mm_dnq")
    d_ukv = _wukv_unlayout(_matmul(nkv, dkv, 'tn', BF16, "mm_d_ukv"))
    dnkv = _matmul(dkv, w_ukv, 'nt', ACT, "mm_dnkv")
    dz, G['g_cq'], G['g_ckv'] = _lat_bwd(z, dnq, dnkv, dkpe, g_cq, g_ckv, dz, "lat_bwd")
    dom = _matmul(dy_mem, w_o_mem, 'nt', ACT, "mm_dom")
    dqm, dkm, dvm = _attn_bwd(qm, km, kvm, om, dom, lse_m, B, MEM_HEADS, HEAD, MEM_HEADS, MEM_SCALE, False, "mem_attn_bwd")
    dz, G['g_mq'] = _headnorm_bwd(z, Z_QM // (MEM_HEADS * HEAD), MEM_HEADS, gmq, dqm, None, "memq_bwd",
                                  into=(dz, Z_QM // (MEM_HEADS * HEAD)))
    dkvm, G['g_mk'] = _headnorm_bwd(kvm, 0, MEM_HEADS, gmk, dkm, dvm, "memk_bwd")
    d_mem_kv = _matmul(nm, dkvm, 'tn', BF16, "mm_d_mem_kv")
    dnm = _matmul(dkvm, w_mem_kv, 'nt', ACT, "mm_dnm")
    G['g_mem'], = _rms_bwd(mem2d, g_mem, dnm, None, "rms_mem_bwd", dx_dtypes=())
    half = D_MODEL // 2
    ride = ws.scatter('lat', {'w_uq': d_uq, 'w_ukv': d_ukv, 'w_mem_kv': d_mem_kv})
    d_top, got = _unride(_matmul(h, dz, 'tn', BF16, "mm_d_in_top", tn_t=1792, m_rows=(0, half), rider=ride), ride)
    ws.scattered('lat', got)
    ride = ws.scatter('in_top', {'w_in': _win_unlayout(d_top)})
    d_bot, got = _unride(_matmul(h, dz, 'tn', BF16, "mm_d_in_bot", tn_t=1792, m_rows=(half, half), rider=ride), ride)
    ws.scattered('in_top', got)
    ride = ws.scatter('in_bot', {'w_in': _win_unlayout(d_bot)})
    dh, got = _unride(_matmul(dz, w_in, 'nt', ACT, "mm_dh", rider=ride), ride)
    ws.scattered('in_bot', got)
    gx, G['g_mix'] = _rms_bwd(x2d, g_mix, dh, dx1, "rms_mix_bwd")
    return loss_part, gx.reshape(B, S, D_MODEL), G


def _all_gather8(xs, name):
    def body(x_ref, out_ref, send_sems, recv_sems, local_sem):
        x, y, c = lax.axis_index("x"), lax.axis_index("y"), lax.axis_index("c")
        me, sibling = (x, y, c), (x, y, 1 - c)
        chips = [(1 - x, y), (x, 1 - y), (1 - x, 1 - y)]

        def rows(px, py, pc):
            return out_ref.at[4 * px + 2 * py + pc]

        def copy(k, block, to, src=None):
            return pltpu.make_async_remote_copy(
                src_ref=rows(*block) if src is None else src, dst_ref=rows(*block),
                send_sem=send_sems.at[k], recv_sem=recv_sems.at[k], device_id=to, device_id_type=MESH)

        mine = pltpu.make_async_copy(x_ref, rows(*me), local_sem)
        mine.start()
        first = [copy(0, me, sibling, src=x_ref)]
        first += [copy(1 + j, me, (*chip, c), src=x_ref) for j, chip in enumerate(chips)]
        for cp in first:
            cp.start()
        passed = [copy(4 + j, (*chip, c), sibling) for j, chip in enumerate(chips)]
        for j, chip in enumerate(chips):
            copy(1 + j, (*chip, c), me).wait_recv()
            passed[j].start()
        copy(0, sibling, me).wait_recv()
        for j, chip in enumerate(chips):
            copy(4 + j, (*chip, 1 - c), me).wait_recv()
        for cp in first + passed:
            cp.wait_send()
        mine.wait()

    return pl.pallas_call(
        body, name=name, in_specs=[HBM_SPEC], out_specs=HBM_SPEC,
        out_shape=jax.ShapeDtypeStruct((N_DEV,) + xs.shape, xs.dtype),
        scratch_shapes=[pltpu.SemaphoreType.DMA((7,)), pltpu.SemaphoreType.DMA((7,)), pltpu.SemaphoreType.DMA],
    )(xs)


ADAMW_TILE_ELEMS = 256 * 1024


def _adamw_rows(w, g, m, v):
    m2 = ADAM_B1 * m + (1.0 - ADAM_B1) * g
    v2 = ADAM_B2 * v + (1.0 - ADAM_B2) * (g * g)
    m_hat = m2 / (1.0 - ADAM_B1 ** ADAM_STEP)
    v_hat = v2 / (1.0 - ADAM_B2 ** ADAM_STEP)
    delta = -ADAM_LR * (m_hat / (jnp.sqrt(v_hat) + ADAM_EPS) + ADAM_WD * w)
    return delta, m2, v2


def _sum_adamw(parts, w, m, v, name):
    rows, cols = w.shape
    assert sum(p.shape[1] for p in parts) == rows
    tr = _pick(min(p.shape[1] for p in parts), max(16, ADAMW_TILE_ELEMS // cols), 16)
    n = parts[0].shape[0]
    counts = [p.shape[1] // tr for p in parts]
    starts = [sum(counts[:k]) for k in range(len(parts))]

    def body(*refs):
        p_refs = refs[:len(parts)]
        w_ref, m_ref, v_ref, g_ref, d_ref, m2_ref, v2_ref = refs[len(parts):]
        g = None
        for p_ref, start in zip(p_refs, starts):
            gk = p_ref[0].astype(F32)
            for k in range(1, n):
                gk = gk + p_ref[k].astype(F32)
            g = gk if g is None else jnp.where(pl.program_id(0) >= start, gk, g)
        delta, m2, v2 = _adamw_rows(w_ref[...], g, m_ref[...], v_ref[...])
        g_ref[...] = g
        d_ref[...] = delta
        m2_ref[...] = m2
        v2_ref[...] = v2

    flat = pl.BlockSpec((tr, cols), lambda i: (i, 0))
    out = jax.ShapeDtypeStruct((rows, cols), F32)
    p_specs = [pl.BlockSpec((n, tr, cols), lambda i, s=s, c=c: (0, jnp.clip(i - s, 0, c - 1), 0))
               for s, c in zip(starts, counts)]
    return pl.pallas_call(
        body, name=name, grid=(rows // tr,), in_specs=p_specs + [flat, flat, flat], out_specs=[flat] * 4,
        out_shape=[out] * 4, compiler_params=_params(("parallel",)),
    )(*parts, w, m, v)


SMALL_WIDTH = {'g_mix': 1024, 'g_cq': 384, 'g_ckv': 256, 'g_q_nope': 128, 'g_q_pe': 128, 'g_k_nope': 128, 'g_k_pe': 128,
               'g_gm_ln': 512, 'b_gm_ln': 512, 'g_mem': 1024, 'g_mq': 128, 'g_mk': 128, 'g_ffn': 1024}
NARROW = ('g_q_pe', 'g_k_pe')


def _small_layout():
    layout, r = {}, 0
    for name in SMALL + ['loss']:
        rows = {'w_spatial': GM_GROUPS * GM_CHUNK, 'b_spatial': GM_GROUPS, 'loss': 1}.get(name) or SMALL_WIDTH[name] // LANES
        layout[name] = (r, rows)
        r += -(-rows // 8) * 8
    return layout, r


def _small_pack(grads, loss_part, name):
    layout, total = _small_layout()
    names = SMALL + ['loss']

    def body(*refs):
        out_ref = refs[-1]
        out_ref[...] = jnp.zeros((total, LANES), F32)
        for ref, n in zip(refs[:-1], names):
            r0, rows = layout[n]
            if n == 'w_spatial':
                for g in range(GM_GROUPS):
                    out_ref[r0 + g * GM_CHUNK:r0 + (g + 1) * GM_CHUNK, :] = ref[g]
            elif n == 'b_spatial':
                out_ref[r0:r0 + rows, :] = ref[...]
            else:
                for k in range(rows):
                    out_ref[r0 + k:r0 + k + 1, :] = ref[:, k * LANES:(k + 1) * LANES]

    return pl.pallas_call(body, name=name, out_shape=jax.ShapeDtypeStruct((total, LANES), F32))(
        *[grads[n] for n in SMALL], loss_part)


def _small_adamw(parts, w, m, v, name):
    layout, _ = _small_layout()
    n_dev = parts.shape[0]

    def body(*refs):
        p_ref = refs[0]
        ins = refs[1:1 + 3 * len(SMALL)]
        outs = refs[1 + 3 * len(SMALL):-1]

        def gsum(r0, rows):
            g = p_ref[0, r0:r0 + rows, :]
            for d in range(1, n_dev):
                g = g + p_ref[d, r0:r0 + rows, :]
            return g

        def step(idx, g, at):
            w_ref, m_ref, v_ref = ins[3 * idx:3 * idx + 3]
            delta, m2, v2 = _adamw_rows(w_ref[at], g, m_ref[at], v_ref[at])
            for ref, val in zip(outs[4 * idx:4 * idx + 4], (g, delta, m2, v2)):
                ref[at] = val

        for idx, n in enumerate(SMALL):
            r0, rows = layout[n]
            if n == 'w_spatial':
                for g in range(GM_GROUPS):
                    step(idx, gsum(r0 + g * GM_CHUNK, GM_CHUNK), (0, g))
            elif n == 'b_spatial':
                step(idx, gsum(r0, rows), (0,))
            else:
                for k in range(rows):
                    step(idx, gsum(r0 + k, 1), (slice(None), slice(k * LANES, (k + 1) * LANES)))
        refs[-1][...] = gsum(layout['loss'][0], 8)

    flat_in = [d[n] for n in SMALL for d in (w, m, v)]
    out_shape = [jax.ShapeDtypeStruct(w[n].shape, F32) for n in SMALL for _ in range(4)]
    res = pl.pallas_call(body, name=name, out_shape=out_shape + [jax.ShapeDtypeStruct((8, LANES), F32)])(parts, *flat_in)
    groups = [{n: res[4 * i + j] for i, n in enumerate(SMALL)} for j in range(4)]
    return groups, res[-1]


def _full_from_gathered(gathered, name):
    r, c = BIG_SHAPE[name]
    if BIG_AXIS[name] == 0:
        return gathered.reshape(r, c)
    return gathered.transpose(1, 0, 2).reshape(r, c)


def _shards_of_full(g, name):
    if g.ndim == 3:
        return g
    r, c = BIG_SHAPE[name]
    if BIG_AXIS[name] == 0:
        return g.reshape(N_DEV, r // N_DEV, c)
    return g.reshape(g.shape[0], N_DEV, c // N_DEV).transpose(1, 0, 2)


class _DistWeights:
    def __init__(self, shards):
        self.shards = shards
        self.received = {}

    def gather(self, names):
        return _Gather2([self.shards[n].astype(BF16) for n in names])

    def gathered(self, names, got):
        return {n: _full_from_gathered(g, n) for n, g in zip(names, got)}

    def scatter(self, key, grads):
        return _Exchange([_shards_of_full(grads[n], n) for n in RS_GROUPS[key]], scatter=True)

    def scattered(self, key, got):
        for n, g in zip(RS_GROUPS[key], got):
            self.received.setdefault(n, []).append(g)


def kernel(x, mem, positions, g_mix, w_in, g_cq, w_uq, g_ckv, w_ukv, g_q_nope, g_q_pe, g_k_nope, g_k_pe, g_gm_ln, b_gm_ln, w_spatial, b_spatial, g_mem, w_mem_kv, g_mq, g_mk, w_o_gm, w_o_mla, w_o_mem, w_out, g_ffn, w_ff1, w_ff2, loss_target, m_g_mix, m_w_in, m_g_cq, m_w_uq, m_g_ckv, m_w_ukv, m_g_q_nope, m_g_q_pe, m_g_k_nope, m_g_k_pe, m_g_gm_ln, m_b_gm_ln, m_w_spatial, m_b_spatial, m_g_mem, m_w_mem_kv, m_g_mq, m_g_mk, m_w_o_gm, m_w_o_mla, m_w_o_mem, m_w_out, m_g_ffn, m_w_ff1, m_w_ff2, v_g_mix, v_w_in, v_g_cq, v_w_uq, v_g_ckv, v_w_ukv, v_g_q_nope, v_g_q_pe, v_g_k_nope, v_g_k_pe, v_g_gm_ln, v_b_gm_ln, v_w_spatial, v_b_spatial, v_g_mem, v_w_mem_kv, v_g_mq, v_g_mk, v_w_o_gm, v_w_o_mla, v_w_o_mem, v_w_out, v_g_ffn, v_w_ff1, v_w_ff2):
    given = dict(locals())
    w = {n: given[n][0] for n in WEIGHTS}
    mom = {n: given['m_' + n][0] for n in WEIGHTS}
    var = {n: given['v_' + n][0] for n in WEIGHTS}

    ws = _DistWeights({n: w[n] for n in BIG})
    loss_part, grad_x, G = _local_step(x, mem, positions, loss_target, {n: w[n] for n in SMALL}, ws)

    outs = {}
    for n in BIG:
        for prefix, res in zip(("grad_", "delta_", "new_m_", "new_v_"),
                               _sum_adamw(ws.received[n], w[n], mom[n], var[n], "adamw_" + n)):
            outs[prefix + n] = res[None]

    def widen(d):
        return {n: (jnp.pad(d[n], ((0, 0), (0, LANES - MLA_ROPE))) if n in NARROW else d[n]) for n in SMALL}

    parts = _all_gather8(_small_pack(widen(G), loss_part, "small_pack"), "ag_small")
    small, loss_rows = _small_adamw(parts, *[widen({n: given[prefix + n] for n in SMALL}) for prefix in ("", "m_", "v_")],
                                    "adamw_small")
    loss = 0.5 * jnp.sum(loss_rows) / D_MODEL
    for prefix, group in zip(("grad_", "delta_", "new_m_", "new_v_"), small):
        for n in SMALL:
            outs[prefix + n] = group[n][:, :MLA_ROPE] if n in NARROW else group[n]
    return (loss, grad_x, *[outs[p + n] for p in ("grad_", "delta_", "new_m_", "new_v_") for n in WEIGHTS])
```

```python
import functools
import math

import jax
import jax.numpy as jnp
from jax import lax
from jax.experimental import pallas as pl
from jax.experimental.pallas import tpu as pltpu

F32 = jnp.float32
BF16 = jnp.bfloat16
ACT = BF16

D_MODEL = 1024
MEM_HEADS = 4
HEAD = 128
GM_WIDTH = 512
GM_CHUNK = 128
GM_GROUPS = 4
MLA_HEADS = 8
MLA_ROPE = 64
Q_LORA = 384
KV_LORA = 256
D_FF = 4096
EPS = 1e-6
ROPE_BASE = 10000.0
MLA_SCALE = 1.0 / math.sqrt(HEAD + MLA_ROPE)
MEM_SCALE = 1.0 / math.sqrt(HEAD)
LOG2E = 1.4426950408889634
LN2 = 0.6931471805599453
ATT_TILE = 256
C_ZU, C_ZV, C_CQ, C_CKV, C_KPE, C_QM, C_ZG, C_END = 0, 512, 1024, 1408, 1664, 1728, 2240, 5312
Z_GM, Z_QM, Z_MLA, Z_KPE, Z_COLS = 3072, 4096, 4608, 5248, 5376
MLA_W = 768
QCAT = 2 * HEAD
ADAM_LR, ADAM_B1, ADAM_B2, ADAM_EPS, ADAM_WD, ADAM_STEP = 0.001, 0.9, 0.999, 1e-08, 0.01, 10
N_DEV = 8
LANES = 128
VMEM_LIMIT = 48 * 1024 * 1024
MAX_K_TILE = 8192
NEG = -1e30

BIG = ['w_in', 'w_uq', 'w_ukv', 'w_mem_kv', 'w_o_gm', 'w_o_mla', 'w_o_mem', 'w_out', 'w_ff1', 'w_ff2']
BIG_AXIS = {'w_in': 1, 'w_uq': 1, 'w_ukv': 1, 'w_mem_kv': 0, 'w_o_gm': 1, 'w_o_mla': 0, 'w_o_mem': 1,
            'w_out': 0, 'w_ff1': 1, 'w_ff2': 0}
BIG_SHAPE = {'w_in': (1024, 5312), 'w_uq': (384, 1536), 'w_ukv': (256, 2048), 'w_mem_kv': (1024, 1024),
             'w_o_gm': (512, 1024), 'w_o_mla': (1024, 1024), 'w_o_mem': (512, 1024), 'w_out': (1024, 1024),
             'w_ff1': (1024, 4096), 'w_ff2': (4096, 1024)}
SMALL = ['g_mix', 'g_cq', 'g_ckv', 'g_q_nope', 'g_q_pe', 'g_k_nope', 'g_k_pe', 'g_gm_ln', 'b_gm_ln',
         'w_spatial', 'b_spatial', 'g_mem', 'g_mq', 'g_mk', 'g_ffn']
WEIGHTS = ['g_mix', 'w_in', 'g_cq', 'w_uq', 'g_ckv', 'w_ukv', 'g_q_nope', 'g_q_pe', 'g_k_nope', 'g_k_pe',
           'g_gm_ln', 'b_gm_ln', 'w_spatial', 'b_spatial', 'g_mem', 'w_mem_kv', 'g_mq', 'g_mk', 'w_o_gm',
           'w_o_mla', 'w_o_mem', 'w_out', 'g_ffn', 'w_ff1', 'w_ff2']


def _pick(n, target, mult=LANES):
    best = None
    t = mult
    while t <= min(n, target):
        if n % t == 0:
            best = t
        t += mult
    return best if best is not None else n


def _params(sem):
    return pltpu.CompilerParams(dimension_semantics=sem, vmem_limit_bytes=VMEM_LIMIT)


MESH = pl.DeviceIdType.MESH
HBM_SPEC = pl.BlockSpec(memory_space=pltpu.HBM)


class _Exchange:
    def __init__(self, srcs, scatter):
        self.srcs, self.scatter = list(srcs), scatter
        self.out_shapes = [jax.ShapeDtypeStruct(s.shape if scatter else (N_DEV,) + s.shape, s.dtype) for s in self.srcs]
        n = len(self.srcs)
        self.scratch = [pltpu.SemaphoreType.DMA((n, N_DEV - 1)), pltpu.SemaphoreType.DMA((n, N_DEV - 1)),
                        pltpu.SemaphoreType.DMA((n,))]

    def _copies(self, src_refs, dst_refs, send_sems, recv_sems, local_sems):
        x, y, c = lax.axis_index("x"), lax.axis_index("y"), lax.axis_index("c")
        me = 4 * x + 2 * y + c
        local, remote = [], []
        for a, (src_ref, dst_ref) in enumerate(zip(src_refs, dst_refs)):
            def mine_for(dev, src_ref=src_ref):
                return src_ref.at[dev] if self.scatter else src_ref

            local.append(pltpu.make_async_copy(mine_for(me), dst_ref.at[me], local_sems.at[a]))
            for k in range(1, N_DEV):
                px = 1 - x if k & 4 else x
                py = 1 - y if k & 2 else y
                pc = 1 - c if k & 1 else c
                remote.append(pltpu.make_async_remote_copy(
                    src_ref=mine_for(4 * px + 2 * py + pc), dst_ref=dst_ref.at[me], send_sem=send_sems.at[a, k - 1],
                    recv_sem=recv_sems.at[a, k - 1], device_id=(px, py, pc), device_id_type=MESH))
        return local, remote

    def start(self, *refs):
        local, remote = self._copies(*refs)
        for cp in local + remote:
            cp.start()

    def forward(self, *refs):
        pass

    def finish(self, *refs):
        local, remote = self._copies(*refs)
        for cp in remote + local:
            cp.wait()


class _Gather2:
    def __init__(self, srcs):
        self.srcs = list(srcs)
        self.out_shapes = [jax.ShapeDtypeStruct((N_DEV,) + s.shape, s.dtype) for s in self.srcs]
        n = len(self.srcs)
        self.scratch = [pltpu.SemaphoreType.DMA((n, N_DEV - 1)), pltpu.SemaphoreType.DMA((n, N_DEV - 1)),
                        pltpu.SemaphoreType.DMA((n,))]

    def _plan(self, src_refs, dst_refs, send_sems, recv_sems, local_sems):
        x, y, c = lax.axis_index("x"), lax.axis_index("y"), lax.axis_index("c")
        chips = [(1 - x, y), (x, 1 - y), (1 - x, 1 - y)]
        plans = []
        for a, (src_ref, dst_ref) in enumerate(zip(src_refs, dst_refs)):
            def copy(k, block, to, src=None, a=a, dst_ref=dst_ref):
                at = dst_ref.at[4 * block[0] + 2 * block[1] + block[2]]
                return pltpu.make_async_remote_copy(src_ref=at if src is None else src, dst_ref=at,
                                                    send_sem=send_sems.at[a, k], recv_sem=recv_sems.at[a, k],
                                                    device_id=to, device_id_type=MESH)

            local = pltpu.make_async_copy(src_ref, dst_ref.at[4 * x + 2 * y + c], local_sems.at[a])
            first = [copy(0, (x, y, c), (x, y, 1 - c), src=src_ref)]
            first += [copy(1 + j, (x, y, c), (*chip, c), src=src_ref) for j, chip in enumerate(chips)]
            passed = [copy(4 + j, (*chip, c), (x, y, 1 - c)) for j, chip in enumerate(chips)]
            arrivals = [copy(1 + j, (*chip, c), (x, y, c)) for j, chip in enumerate(chips)]
            late = [copy(0, (x, y, 1 - c), (x, y, c))] + [copy(4 + j, (*chip, 1 - c), (x, y, c)) for j, chip in enumerate(chips)]
            plans.append((local, first, passed, arrivals, late))
        return plans

    def start(self, *refs):
        for local, first, _, _, _ in self._plan(*refs):
            local.start()
            for cp in first:
                cp.start()

    def forward(self, *refs):
        for _, _, passed, arrivals, _ in self._plan(*refs):
            for arrived, onward in zip(arrivals, passed):
                arrived.wait_recv()
                onward.start()

    def finish(self, *refs):
        for local, first, passed, _, late in self._plan(*refs):
            for cp in late:
                cp.wait_recv()
            for cp in first + passed:
                cp.wait_send()
            local.wait()


def _call(body, rider, ins, *, name, grid, in_specs, out_specs, out_shape, scratch_shapes, sem):
    if rider is None:
        return pl.pallas_call(body, name=name, grid=grid, in_specs=in_specs, out_specs=out_specs, out_shape=out_shape,
                              scratch_shapes=scratch_shapes, compiler_params=_params(sem))(*ins)
    single = not isinstance(out_shape, (list, tuple))
    own_specs, own_shapes = ([out_specs], [out_shape]) if single else (list(out_specs), list(out_shape))
    n_in, n_out, n_sc, n_r = len(ins), len(own_shapes), len(scratch_shapes), len(rider.srcs)
    n_all_in = n_in + n_r

    def carrying(*refs):
        own_in, srcs = refs[:n_in], refs[n_in:n_in + n_r]
        own_out, dsts = refs[n_all_in:n_all_in + n_out], refs[n_all_in + n_out:n_all_in + n_out + n_r]
        own_sc = refs[n_all_in + n_out + n_r:n_all_in + n_out + n_r + n_sc]
        sems = refs[n_all_in + n_out + n_r + n_sc:]
        first = last = late = None
        for d, steps in enumerate(grid):
            f, l = pl.program_id(d) == 0, pl.program_id(d) == steps - 1
            t = pl.program_id(d) == ((3 * steps) // 4 if d == 0 else 0)
            first, last, late = (f, l, t) if first is None else (first & f, last & l, late & t)

        @pl.when(first)
        def _():
            rider.start(srcs, dsts, *sems)

        @pl.when(late)
        def _():
            rider.forward(srcs, dsts, *sems)

        body(*own_in, *own_out, *own_sc)

        @pl.when(last)
        def _():
            rider.finish(srcs, dsts, *sems)

    res = pl.pallas_call(
        carrying, name=name, grid=grid, in_specs=list(in_specs) + [HBM_SPEC] * n_r,
        out_specs=own_specs + [HBM_SPEC] * n_r, out_shape=own_shapes + rider.out_shapes,
        scratch_shapes=list(scratch_shapes) + rider.scratch, compiler_params=_params(("arbitrary",) * len(grid)),
    )(*ins, *rider.srcs)
    own = res[:n_out]
    return (own[0] if single else list(own)), list(res[n_out:])


def _matmul(a, b, mode, out_dtype, name, add=None, relu2_a=False, relu2_grad=None,
            tm_t=None, tn_t=None, tk_t=None, rider=None, m_rows=None, col_shards=None, sq_err_target=None):
    if mode == 'nn':
        (M, K), (K2, N) = a.shape, b.shape
    elif mode == 'nt':
        (M, K), (N, K2) = a.shape, b.shape
    else:
        (K, M), (K2, N) = a.shape, b.shape
    assert K == K2, (name, a.shape, b.shape)
    m_first = 0
    if m_rows is not None:
        assert mode == 'tn'
        m_first, M = m_rows
    if col_shards is not None:
        assert add is None and relu2_grad is None and sq_err_target is None and tn_t is None
    if mode == 'tn':
        d_tm, d_tn, d_tk = 1024, (2048 if M <= 512 else 1024), 2048
    else:
        wide = add is None and sq_err_target is None and jnp.dtype(out_dtype).itemsize == 2 and K <= D_FF
        d_tm, d_tn, d_tk = (2048 if K <= 1024 else 1024), (1024 if wide else 512), MAX_K_TILE
    tm, tn, tk = _pick(M, tm_t or d_tm), _pick(N, tn_t or d_tn), _pick(K, tk_t or d_tk)
    gm, gn, nk = M // tm, N // tn, K // tk
    if mode == 'nn':
        a_spec = pl.BlockSpec((tm, tk), lambda i, j, k: (i, k))
        b_spec = pl.BlockSpec((tk, tn), lambda i, j, k: (k, j))
        dims = (((1,), (0,)), ((), ()))
    elif mode == 'nt':
        a_spec = pl.BlockSpec((tm, tk), lambda i, j, k: (i, k))
        b_spec = pl.BlockSpec((tn, tk), lambda i, j, k: (j, k))
        dims = (((1,), (1,)), ((), ()))
    else:
        assert m_first % tm == 0
        a_spec = pl.BlockSpec((tk, tm), lambda i, j, k: (k, m_first // tm + i))
        b_spec = pl.BlockSpec((tk, tn), lambda i, j, k: (k, j))
        dims = (((0,), (0,)), ((), ()))
    o_spec = pl.BlockSpec((tm, tn), lambda i, j, k: (i, j))
    shard_w = N // col_shards if col_shards is not None else tn
    assert tn % shard_w == 0
    has_add, has_e, has_t = add is not None, relu2_grad is not None, sq_err_target is not None
    assert not has_t or (nk == 1 and tn % LANES == 0)

    def body(*refs):
        a_ref, b_ref = refs[0], refs[1]
        pos = 2
        add_ref = e_ref = t_ref = None
        if has_add:
            add_ref = refs[pos]
            pos += 1
        if has_e:
            e_ref = refs[pos]
            pos += 1
        if has_t:
            t_ref = refs[pos]
            pos += 1
        o_ref = refs[pos]
        acc_ref = refs[pos + 1] if nk > 1 else None

        av = a_ref[...]
        if relu2_a:
            av = jnp.maximum(av, 0)
            av = av * av
        prod = lax.dot_general(av.astype(BF16), b_ref[...].astype(BF16), dims, preferred_element_type=F32)

        def finish(r):
            if has_add:
                r = r + add_ref[...]
            if has_e:
                r = r * (2.0 * jnp.maximum(e_ref[...].astype(F32), 0.0))
            if has_t:
                err = r - t_ref[...]
                r = err * (1.0 / N)
                refs[pos + 1][...] = r.astype(BF16)
                sq = err * err
                part = sq[:, 0:LANES]
                for c in range(1, tn // LANES):
                    part = part + sq[:, c * LANES:(c + 1) * LANES]
                _acc_rows(refs[pos + 2], part, (pl.program_id(0) == 0) & (pl.program_id(1) == 0))
            if col_shards is not None:
                for s in range(tn // shard_w):
                    o_ref[s] = r[:, s * shard_w:(s + 1) * shard_w].astype(out_dtype)
            else:
                o_ref[...] = r.astype(out_dtype)

        if nk == 1:
            finish(prod)
        else:
            k = pl.program_id(2)

            @pl.when(k == 0)
            def _():
                acc_ref[...] = prod

            @pl.when(k > 0)
            def _():
                acc_ref[...] += prod

            @pl.when(k == nk - 1)
            def _():
                finish(acc_ref[...])

    ins, specs = [a, b], [a_spec, b_spec]
    if has_add:
        ins.append(add)
        specs.append(o_spec)
    if has_e:
        ins.append(relu2_grad)
        specs.append(o_spec)
    out_specs, out_shape, sem = o_spec, jax.ShapeDtypeStruct((M, N), out_dtype), ("parallel", "parallel", "arbitrary")
    if has_t:
        ins.append(sq_err_target)
        specs.append(o_spec)
        out_specs = [o_spec, o_spec, pl.BlockSpec((1, LANES), lambda i, j, k: (0, 0))]
        out_shape = [out_shape, jax.ShapeDtypeStruct((M, N), BF16), jax.ShapeDtypeStruct((1, LANES), F32)]
        sem = ("arbitrary", "arbitrary", "arbitrary")
    if col_shards is not None:
        out_specs = pl.BlockSpec((tn // shard_w, tm, shard_w), lambda i, j, k: (j, i, 0))
        out_shape = jax.ShapeDtypeStruct((col_shards, M, shard_w), out_dtype)
    return _call(body, rider, ins, name=name, grid=(gm, gn, nk), in_specs=specs, out_specs=out_specs, out_shape=out_shape,
                 scratch_shapes=[pltpu.VMEM((tm, tn), F32)] if nk > 1 else [], sem=sem)


ROW_BLOCK_BYTES = 12 * 1024 * 1024


def _row_tile(rows, row_bytes):
    return _pick(rows, max(16, min(1024, ROW_BLOCK_BYTES // row_bytes)), 16)


def _rowspec(tr, width, col=0):
    return pl.BlockSpec((tr, width), lambda i, col=col: (i, col))


def _fullspec(shape):
    nd = len(shape)
    return pl.BlockSpec(shape, lambda i, nd=nd: (0,) * nd)


def _rms(x, width):
    x = x.astype(F32)
    return lax.rsqrt(jnp.sum(x * x, axis=-1, keepdims=True) * (1.0 / width) + EPS)


def _rms_bwd_rows(x, g, dy, width):
    x, dy = x.astype(F32), dy.astype(F32)
    r = _rms(x, width)
    xh = x * r
    dn = dy * g
    dx = r * (dn - xh * (jnp.sum(dn * xh, axis=-1, keepdims=True) * (1.0 / width)))
    return dx, dy * xh


def _acc_rows(ref, val, first):
    s = jnp.sum(val, axis=0, keepdims=True)

    @pl.when(first)
    def _():
        ref[...] = s

    @pl.when(jnp.logical_not(first))
    def _():
        ref[...] += s


def _rms_fwd(x, g, name, rider=None):
    rows, width = x.shape
    tr = _row_tile(rows, 6 * width)

    def body(x_ref, g_ref, o_ref):
        xv = x_ref[...]
        o_ref[...] = (xv * _rms(xv, width) * g_ref[...]).astype(BF16)

    return _call(body, rider, [x, g], name=name, grid=(rows // tr,),
                 in_specs=[_rowspec(tr, width), _fullspec((1, width))], out_specs=_rowspec(tr, width),
                 out_shape=jax.ShapeDtypeStruct((rows, width), BF16), scratch_shapes=[], sem=("parallel",))


def _rms_bwd(x, g, dy, res, name, dx_dtypes=(F32,)):
    rows, width = x.shape
    tr = _row_tile(rows, 18 * width)
    has_res = res is not None
    n_in = 4 if has_res else 3

    def body(*refs):
        x_ref, g_ref, dy_ref = refs[:3]
        dx, dgv = _rms_bwd_rows(x_ref[...], g_ref[...], dy_ref[...], width)
        if has_res:
            dx = dx + refs[3][...]
        for ref, dt in zip(refs[n_in:], dx_dtypes):
            ref[...] = dx.astype(dt)
        _acc_rows(refs[-1], dgv, pl.program_id(0) == 0)

    ins = [x, g, dy] + ([res] if has_res else [])
    specs = [_rowspec(tr, width), _fullspec((1, width)), _rowspec(tr, width)] + ([_rowspec(tr, width)] if has_res else [])
    return pl.pallas_call(
        body, name=name, grid=(rows // tr,), in_specs=specs,
        out_specs=[_rowspec(tr, width)] * len(dx_dtypes) + [_fullspec((1, width))],
        out_shape=[jax.ShapeDtypeStruct((rows, width), dt) for dt in dx_dtypes] + [jax.ShapeDtypeStruct((1, width), F32)],
        compiler_params=_params(("arbitrary",)),
    )(*ins)


_GELU_C = math.sqrt(2.0 / math.pi)


def _gelu(x):
    t = jnp.tanh(_GELU_C * (x + 0.044715 * (x * x * x)))
    return 0.5 * x * (1.0 + t), t


def _gelu_grad(x, t):
    return 0.5 * (1.0 + t) + 0.5 * x * (1.0 - t * t) * (_GELU_C * (1.0 + 3.0 * 0.044715 * (x * x)))


def _gm_forward_rows(zu, zv, gln, bln, wc_ref, bst, n_chunk):
    u, tu = _gelu(zu)
    a, ta = _gelu(zv)
    mu = jnp.mean(a, axis=-1, keepdims=True)
    ac = a - mu
    rs = lax.rsqrt(jnp.mean(ac * ac, axis=-1, keepdims=True) + EPS)
    n = ac * rs
    v = n * gln + bln
    vb = v.astype(BF16)
    rows = []
    for c in range(n_chunk):
        cols = []
        for g in range(GM_GROUPS):
            vc = vb[c * GM_CHUNK:(c + 1) * GM_CHUNK, g * LANES:(g + 1) * LANES]
            mixed = jnp.dot(wc_ref[g], vc, preferred_element_type=F32) + bst[g]
            cols.append(mixed)
        rows.append(jnp.concatenate(cols, axis=1))
    mixed = jnp.concatenate(rows, axis=0) if n_chunk > 1 else rows[0]
    return u, tu, ta, n, rs, v, mixed


def _gm_fwd(z, gln, bln, wc, bst, name):
    rows = z.shape[0]
    tr = _pick(rows, 1024, GM_CHUNK)
    n_chunk = tr // GM_CHUNK

    def body(zu_ref, zv_ref, gln_ref, bln_ref, wc_ref, bst_ref, o_ref):
        u, _, _, _, _, _, mixed = _gm_forward_rows(zu_ref[...].astype(F32), zv_ref[...].astype(F32), gln_ref[...], bln_ref[...], wc_ref,
                                                   bst_ref, n_chunk)
        o_ref[...] = (u * mixed).astype(BF16)

    return pl.pallas_call(
        body, name=name, grid=(rows // tr,),
        in_specs=[_rowspec(tr, GM_WIDTH, Z_GM // GM_WIDTH), _rowspec(tr, GM_WIDTH, Z_GM // GM_WIDTH + 1),_fullspec((1, GM_WIDTH)), _fullspec((1, GM_WIDTH)),
                  _fullspec((GM_GROUPS, GM_CHUNK, GM_CHUNK)), _fullspec((GM_GROUPS, GM_CHUNK, LANES))],
        out_specs=_rowspec(tr, GM_WIDTH), out_shape=jax.ShapeDtypeStruct((rows, GM_WIDTH), BF16),
        compiler_params=_params(("parallel",)),
    )(z, z, gln, bln, wc, bst)


ANY_SPEC = pl.BlockSpec(memory_space=pl.ANY)


def _gm_bwd(z, dy, gln, bln, wc, wct, bst, dz, name):
    rows = z.shape[0]
    tr = _pick(rows, 1024, GM_CHUNK)
    n_chunk = tr // GM_CHUNK

    def body(zu_ref, zv_ref, dy_ref, gln_ref, bln_ref, wc_ref, wct_ref, bst_ref, _, dz_ref, dws_ref, dbs_ref, dgl_ref,
             dbl_ref):
        first = pl.program_id(0) == 0
        zu, zv, gln = zu_ref[...].astype(F32), zv_ref[...].astype(F32), gln_ref[...]
        u, tu, ta, n, rs, v, mixed = _gm_forward_rows(zu, zv, gln, bln_ref[...], wc_ref, bst_ref, n_chunk)
        dyv = dy_ref[...].astype(F32)
        dzu = dyv * mixed * _gelu_grad(zu, tu)
        dmix = dyv * u
        dmb = dmix.astype(BF16)
        vb = v.astype(BF16)
        dv_rows, dws, dbs = [], [None] * GM_GROUPS, None
        for c in range(n_chunk):
            rsl = slice(c * GM_CHUNK, (c + 1) * GM_CHUNK)
            cols = []
            for g in range(GM_GROUPS):
                csl = slice(g * LANES, (g + 1) * LANES)
                dmc = dmb[rsl, csl]
                cols.append(jnp.dot(wct_ref[g], dmc, preferred_element_type=F32))
                w_part = lax.dot_general(dmc, vb[rsl, csl], (((1,), (1,)), ((), ())), preferred_element_type=F32)
                dws[g] = w_part if dws[g] is None else dws[g] + w_part
            dv_rows.append(jnp.concatenate(cols, axis=1))
            dbs = dmix[rsl, :] if dbs is None else dbs + dmix[rsl, :]
        dv = jnp.concatenate(dv_rows, axis=0) if n_chunk > 1 else dv_rows[0]
        dn = dv * gln
        da = rs * (dn - jnp.mean(dn, axis=-1, keepdims=True) - n * jnp.mean(dn * n, axis=-1, keepdims=True))
        dzv = da * _gelu_grad(zv, ta)
        dz_ref[:, 0:GM_WIDTH] = dzu.astype(BF16)
        dz_ref[:, GM_WIDTH:2 * GM_WIDTH] = dzv.astype(BF16)
        _acc_rows(dgl_ref, dv * n, first)
        _acc_rows(dbl_ref, dv, first)

        @pl.when(first)
        def _():
            for g in range(GM_GROUPS):
                dws_ref[g] = dws[g]
            dbs_ref[...] = dbs

        @pl.when(jnp.logical_not(first))
        def _():
            for g in range(GM_GROUPS):
                dws_ref[g] += dws[g]
            dbs_ref[...] += dbs

    wspec = _fullspec((GM_GROUPS, GM_CHUNK, GM_CHUNK))
    return pl.pallas_call(
        body, name=name, grid=(rows // tr,),
        in_specs=[_rowspec(tr, GM_WIDTH, Z_GM // GM_WIDTH), _rowspec(tr, GM_WIDTH, Z_GM // GM_WIDTH + 1),
                  _rowspec(tr, GM_WIDTH), _fullspec((1, GM_WIDTH)), _fullspec((1, GM_WIDTH)), wspec, wspec, wspec, ANY_SPEC],
        out_specs=[_rowspec(tr, 2 * GM_WIDTH, Z_GM // (2 * GM_WIDTH)), wspec, _fullspec((GM_CHUNK, GM_WIDTH)),
                   _fullspec((1, GM_WIDTH)), _fullspec((1, GM_WIDTH))],
        out_shape=[jax.ShapeDtypeStruct(dz.shape, dz.dtype), jax.ShapeDtypeStruct((GM_GROUPS, GM_CHUNK, GM_CHUNK), F32),
                   jax.ShapeDtypeStruct((GM_CHUNK, GM_WIDTH), F32), jax.ShapeDtypeStruct((1, GM_WIDTH), F32),
                   jax.ShapeDtypeStruct((1, GM_WIDTH), F32)],
        input_output_aliases={8: 0}, compiler_params=_params(("arbitrary",)),
    )(z, z, dy, gln, bln, wc, wct, bst, dz)


def _lat_fwd(z, g_cq, g_ckv, name):
    rows = z.shape[0]
    tr = _row_tile(rows, 4 * MLA_W)

    def body(z_ref, gq_ref, gkv_ref, nq_ref, nkv_ref):
        zb = z_ref[...]
        cq, ckv = zb[:, 0:Q_LORA], zb[:, Q_LORA:Q_LORA + KV_LORA]
        nq_ref[...] = (cq * _rms(cq, Q_LORA) * gq_ref[...]).astype(BF16)
        nkv_ref[...] = (ckv * _rms(ckv, KV_LORA) * gkv_ref[...]).astype(BF16)

    return pl.pallas_call(
        body, name=name, grid=(rows // tr,),
        in_specs=[_rowspec(tr, MLA_W, Z_MLA // MLA_W), _fullspec((1, Q_LORA)), _fullspec((1, KV_LORA))],
        out_specs=[_rowspec(tr, Q_LORA), _rowspec(tr, KV_LORA)],
        out_shape=[jax.ShapeDtypeStruct((rows, Q_LORA), BF16), jax.ShapeDtypeStruct((rows, KV_LORA), BF16)],
        compiler_params=_params(("parallel",)),
    )(z, g_cq, g_ckv)


def _lat_bwd(z, dnq, dnkv, dkpe, g_cq, g_ckv, dz, name):
    rows = z.shape[0]
    tr = _row_tile(rows, 8 * MLA_W)

    def body(z_ref, dnq_ref, dnkv_ref, dkpe_ref, gq_ref, gkv_ref, _, dz_ref, dgq_ref, dgkv_ref):
        first = pl.program_id(0) == 0
        zb = z_ref[...]
        dcq, dgq = _rms_bwd_rows(zb[:, 0:Q_LORA], gq_ref[...], dnq_ref[...], Q_LORA)
        dckv, dgkv = _rms_bwd_rows(zb[:, Q_LORA:Q_LORA + KV_LORA], gkv_ref[...], dnkv_ref[...], KV_LORA)
        dz_ref[:, 0:Q_LORA] = dcq.astype(BF16)
        dz_ref[:, Q_LORA:Q_LORA + KV_LORA] = dckv.astype(BF16)
        dz_ref[:, Q_LORA + KV_LORA:MLA_W] = dkpe_ref[...].astype(BF16)
        _acc_rows(dgq_ref, dgq, first)
        _acc_rows(dgkv_ref, dgkv, first)

    return pl.pallas_call(
        body, name=name, grid=(rows // tr,),
        in_specs=[_rowspec(tr, MLA_W, Z_MLA // MLA_W), _rowspec(tr, Q_LORA), _rowspec(tr, KV_LORA), _rowspec(tr, LANES),
                  _fullspec((1, Q_LORA)), _fullspec((1, KV_LORA)), ANY_SPEC],
        out_specs=[_rowspec(tr, MLA_W, Z_MLA // MLA_W), _fullspec((1, Q_LORA)), _fullspec((1, KV_LORA))],
        out_shape=[jax.ShapeDtypeStruct(dz.shape, dz.dtype), jax.ShapeDtypeStruct((1, Q_LORA), F32),
                   jax.ShapeDtypeStruct((1, KV_LORA), F32)],
        input_output_aliases={6: 0}, compiler_params=_params(("arbitrary",)),
    )(z, dnq, dnkv, dkpe, g_cq, g_ckv, dz)


def _rope(y, cc, ss):
    return y * cc + pltpu.roll(y, 64, 1) * ss


def _rope_bwd(d, cc, ss):
    return d * cc + pltpu.roll(d * ss, 64, 1)


def _qk_fwd(q, kv, z, cc, ss, gqn, gqp, gkn, gkp, name):
    rows = q.shape[0]
    W = MLA_HEADS * HEAD
    tr = _row_tile(rows, 20 * W)
    QS = MLA_SCALE * LOG2E

    def body(q_ref, kv_ref, kpe_ref, cc_ref, ss_ref, gqn_ref, gqp_ref, gkn_ref, gkp_ref, qc_ref, kc_ref, v_ref):
        cc, ss = cc_ref[...], ss_ref[...]
        kpe = kpe_ref[...]
        kp = _rope(kpe * _rms(kpe, MLA_ROPE) * gkp_ref[...], cc, ss).astype(BF16)
        for h in range(MLA_HEADS):
            qn = q_ref[:, h * HEAD:(h + 1) * HEAD]
            qp = q_ref[:, W + h * HEAD:W + (h + 1) * HEAD]
            kn = kv_ref[:, h * HEAD:(h + 1) * HEAD]
            qc_ref[:, h * QCAT:h * QCAT + HEAD] = (qn * _rms(qn, HEAD) * gqn_ref[...] * QS).astype(BF16)
            qc_ref[:, h * QCAT + HEAD:(h + 1) * QCAT] = (_rope(qp * _rms(qp, MLA_ROPE) * gqp_ref[...], cc, ss) * QS).astype(BF16)
            kc_ref[:, h * QCAT:h * QCAT + HEAD] = (kn * _rms(kn, HEAD) * gkn_ref[...]).astype(BF16)
            kc_ref[:, h * QCAT + HEAD:(h + 1) * QCAT] = kp
        v_ref[...] = kv_ref[:, W:2 * W].astype(BF16)

    g = _fullspec((1, HEAD))
    return _call(
        body, None, [q, kv, z, cc, ss, gqn, gqp, gkn, gkp], name=name, grid=(rows // tr,),
        in_specs=[_rowspec(tr, 2 * W), _rowspec(tr, 2 * W), _rowspec(tr, LANES, Z_KPE // LANES), _rowspec(tr, LANES),
                  _rowspec(tr, LANES), g, g, g, g],
        out_specs=[_rowspec(tr, MLA_HEADS * QCAT), _rowspec(tr, MLA_HEADS * QCAT), _rowspec(tr, W)],
        out_shape=[jax.ShapeDtypeStruct((rows, MLA_HEADS * QCAT), BF16), jax.ShapeDtypeStruct((rows, MLA_HEADS * QCAT), BF16),
                   jax.ShapeDtypeStruct((rows, W), BF16)],
        scratch_shapes=[], sem=("parallel",))


def _qk_bwd(q, kv, z, cc, ss, gqn, gqp, gkn, gkp, dqc, dkc, dv, name):
    rows = q.shape[0]
    W = MLA_HEADS * HEAD
    tr = _row_tile(rows, 24 * W)

    def body(q_ref, kv_ref, kpe_ref, cc_ref, ss_ref, gqn_ref, gqp_ref, gkn_ref, gkp_ref, dqc_ref, dkc_ref, dv_ref,
             dq_ref, dkv_ref, dkpe_ref, dgqn_ref, dgqp_ref, dgkn_ref, dgkp_ref):
        first = pl.program_id(0) == 0
        cc, ss = cc_ref[...], ss_ref[...]
        sqn = sqp = skn = dkp = None
        for h in range(MLA_HEADS):
            dx, dg = _rms_bwd_rows(q_ref[:, h * HEAD:(h + 1) * HEAD], gqn_ref[...], dqc_ref[:, h * QCAT:h * QCAT + HEAD], HEAD)
            dq_ref[:, h * HEAD:(h + 1) * HEAD] = dx.astype(BF16)
            sqn = dg if sqn is None else sqn + dg
            dy = _rope_bwd(dqc_ref[:, h * QCAT + HEAD:(h + 1) * QCAT], cc, ss)
            dx, dg = _rms_bwd_rows(q_ref[:, W + h * HEAD:W + (h + 1) * HEAD], gqp_ref[...], dy, MLA_ROPE)
            dq_ref[:, W + h * HEAD:W + (h + 1) * HEAD] = dx.astype(BF16)
            sqp = dg if sqp is None else sqp + dg
            dx, dg = _rms_bwd_rows(kv_ref[:, h * HEAD:(h + 1) * HEAD], gkn_ref[...], dkc_ref[:, h * QCAT:h * QCAT + HEAD], HEAD)
            dkv_ref[:, h * HEAD:(h + 1) * HEAD] = dx.astype(BF16)
            skn = dg if skn is None else skn + dg
            part = dkc_ref[:, h * QCAT + HEAD:(h + 1) * QCAT].astype(F32)
            dkp = part if dkp is None else dkp + part
        dkv_ref[:, W:2 * W] = dv_ref[...].astype(BF16)
        dx, dg = _rms_bwd_rows(kpe_ref[...], gkp_ref[...], _rope_bwd(dkp, cc, ss), MLA_ROPE)
        dkpe_ref[...] = dx
        _acc_rows(dgqn_ref, sqn, first)
        _acc_rows(dgqp_ref, sqp, first)
        _acc_rows(dgkn_ref, skn, first)
        _acc_rows(dgkp_ref, dg, first)

    g = _fullspec((1, HEAD))
    gs = jax.ShapeDtypeStruct((1, HEAD), F32)
    return pl.pallas_call(
        body, name=name, grid=(rows // tr,),
        in_specs=[_rowspec(tr, 2 * W), _rowspec(tr, 2 * W), _rowspec(tr, LANES, Z_KPE // LANES), _rowspec(tr, LANES),
                  _rowspec(tr, LANES), g, g, g, g, _rowspec(tr, MLA_HEADS * QCAT), _rowspec(tr, MLA_HEADS * QCAT),
                  _rowspec(tr, W)],
        out_specs=[_rowspec(tr, 2 * W), _rowspec(tr, 2 * W), _rowspec(tr, LANES), g, g, g, g],
        out_shape=[jax.ShapeDtypeStruct((rows, 2 * W), BF16), jax.ShapeDtypeStruct((rows, 2 * W), BF16),
                   jax.ShapeDtypeStruct((rows, LANES), F32), gs, gs, gs, gs],
        compiler_params=_params(("arbitrary",)),
    )(q, kv, z, cc, ss, gqn, gqp, gkn, gkp, dqc, dkc, dv)


def _headnorm_fwd(x, col, nheads, g, out_scale, name):
    rows = x.shape[0]
    W = nheads * HEAD
    tr = _row_tile(rows, 6 * W)

    def body(x_ref, g_ref, o_ref):
        for h in range(nheads):
            xv = x_ref[:, h * HEAD:(h + 1) * HEAD]
            o_ref[:, h * HEAD:(h + 1) * HEAD] = (xv * _rms(xv, HEAD) * g_ref[...] * out_scale).astype(BF16)

    return pl.pallas_call(
        body, name=name, grid=(rows // tr,),
        in_specs=[_rowspec(tr, W, col), _fullspec((1, HEAD))], out_specs=_rowspec(tr, W),
        out_shape=jax.ShapeDtypeStruct((rows, W), BF16), compiler_params=_params(("parallel",)),
    )(x, g)


def _headnorm_bwd(x, col, nheads, g, dy, tail, name, into=None):
    rows = x.shape[0]
    W = nheads * HEAD
    tr = _row_tile(rows, 12 * W)
    has_tail = tail is not None
    WO = 2 * W if has_tail else W

    def body(*refs):
        if into is not None:
            x_ref, g_ref, dy_ref, _, dx_ref, dg_ref = refs
        elif has_tail:
            x_ref, g_ref, dy_ref, t_ref, dx_ref, dg_ref = refs
        else:
            x_ref, g_ref, dy_ref, dx_ref, dg_ref = refs
        acc = None
        for h in range(nheads):
            sl = slice(h * HEAD, (h + 1) * HEAD)
            dx, dg = _rms_bwd_rows(x_ref[:, sl], g_ref[...], dy_ref[:, sl], HEAD)
            dx_ref[:, sl] = dx.astype(BF16)
            acc = dg if acc is None else acc + dg
        if has_tail:
            dx_ref[:, W:2 * W] = t_ref[...].astype(BF16)
        _acc_rows(dg_ref, acc, pl.program_id(0) == 0)

    ins = [x, g, dy] + ([tail] if has_tail else [])
    specs = [_rowspec(tr, W, col), _fullspec((1, HEAD)), _rowspec(tr, W)] + ([_rowspec(tr, W)] if has_tail else [])
    dx_spec, dx_shape, aliases = _rowspec(tr, WO), jax.ShapeDtypeStruct((rows, WO), BF16), {}
    if into is not None:
        assert not has_tail
        ins, specs = ins + [into[0]], specs + [ANY_SPEC]
        dx_spec, dx_shape, aliases = _rowspec(tr, W, into[1]), jax.ShapeDtypeStruct(into[0].shape, into[0].dtype), {3: 0}
    return pl.pallas_call(
        body, name=name, grid=(rows // tr,), in_specs=specs,
        out_specs=[dx_spec, _fullspec((1, HEAD))], out_shape=[dx_shape, jax.ShapeDtypeStruct((1, HEAD), F32)],
        input_output_aliases=aliases, compiler_params=_params(("arbitrary",)),
    )(*ins)


def _sigmoid(x):
    return 1.0 / (1.0 + jnp.exp(-x.astype(F32)))


def _merge_fwd(z, y_gm, y_mla, y_mem, name):
    rows = z.shape[0]
    tr = _row_tile(rows, 14 * D_MODEL)

    def body(g0_ref, g1_ref, g2_ref, a_ref, b_ref, c_ref, o_ref):
        m = _sigmoid(g0_ref[...]) * a_ref[...] + _sigmoid(g1_ref[...]) * b_ref[...] + _sigmoid(g2_ref[...]) * c_ref[...]
        o_ref[...] = m.astype(BF16)

    r = _rowspec(tr, D_MODEL)
    return pl.pallas_call(
        body, name=name, grid=(rows // tr,),
        in_specs=[_rowspec(tr, D_MODEL, 0), _rowspec(tr, D_MODEL, 1), _rowspec(tr, D_MODEL, 2),r, r, r],
        out_specs=r, out_shape=jax.ShapeDtypeStruct((rows, D_MODEL), BF16), compiler_params=_params(("parallel",)),
    )(z, z, z, y_gm, y_mla, y_mem)


def _merge_bwd(z, y_gm, y_mla, y_mem, dm, name):
    rows = z.shape[0]
    tr = _row_tile(rows, 24 * D_MODEL)

    def body(g0_ref, g1_ref, g2_ref, a_ref, b_ref, c_ref, dm_ref, da_ref, db_ref, dc_ref, dzg_ref):
        dmv = dm_ref[...].astype(F32)
        for k, (g_ref, y_ref, dy_ref) in enumerate(((g0_ref, a_ref, da_ref), (g1_ref, b_ref, db_ref), (g2_ref, c_ref, dc_ref))):
            s = _sigmoid(g_ref[...])
            dy_ref[...] = (dmv * s).astype(BF16)
            dzg_ref[:, k * D_MODEL:(k + 1) * D_MODEL] = (dmv * y_ref[...] * s * (1.0 - s)).astype(BF16)

    r = _rowspec(tr, D_MODEL)
    o = jax.ShapeDtypeStruct((rows, D_MODEL), BF16)
    return pl.pallas_call(
        body, name=name, grid=(rows // tr,),
        in_specs=[_rowspec(tr, D_MODEL, 0), _rowspec(tr, D_MODEL, 1), _rowspec(tr, D_MODEL, 2),r, r, r, r],
        out_specs=[r, r, r, _rowspec(tr, 3 * D_MODEL, 0)],
        out_shape=[o, o, o, jax.ShapeDtypeStruct((rows, Z_COLS), BF16)],
        compiler_params=_params(("parallel",)),
    )(z, z, z, y_gm, y_mla, y_mem, dm)


_NT = (((1,), (1,)), ((), ()))
_TN = (((0,), (0,)), ((), ()))


def _diag_mask(s):
    row = lax.broadcasted_iota(jnp.int32, s.shape, 0)
    col = lax.broadcasted_iota(jnp.int32, s.shape, 1)
    return jnp.where(row >= col, s, NEG)


def _attn_fwd(q, k, v, nb, nheads, dk, v_col0, causal, name, rider=None):
    S, Skv = q.shape[0] // nb, k.shape[0] // nb
    tq = _pick(Skv, ATT_TILE) if causal else _pick(S, 4 * ATT_TILE)
    nq = S // tq

    def body(q_ref, k_ref, v_ref, o_ref, lse_ref):
        for i in range(nq):
            r0 = i * tq
            qb = q_ref[r0:r0 + tq, :]
            if causal:
                spans = ([(0, r0, False)] if i > 0 else []) + [(r0, r0 + tq, True)]
            else:
                spans = [(0, Skv, False)]
            scores = []
            for a, b, masked in spans:
                s = lax.dot_general(qb, k_ref[a:b, :], _NT, preferred_element_type=F32)
                scores.append(_diag_mask(s) if masked else s)
            m = functools.reduce(jnp.maximum, [jnp.max(s, axis=-1, keepdims=True) for s in scores])
            l = acc = None
            for s, (a, b, _) in zip(scores, spans):
                p = jnp.exp2(s - m)
                lp = jnp.sum(p, axis=-1, keepdims=True)
                ap = jnp.dot(p.astype(BF16), v_ref[a:b, :].astype(BF16), preferred_element_type=F32)
                l, acc = (lp, ap) if l is None else (l + lp, acc + ap)
            o_ref[r0:r0 + tq, :] = (acc / l).astype(BF16)
            lse_ref[r0:r0 + tq, :] = m + jnp.log2(l)

    ins = [q, k, v]
    in_specs = [pl.BlockSpec((S, dk), lambda b, h: (b, h)), pl.BlockSpec((Skv, dk), lambda b, h: (b, h)),
                pl.BlockSpec((Skv, HEAD), lambda b, h: (b, v_col0 + h))]
    out_specs = [pl.BlockSpec((S, HEAD), lambda b, h: (b, h)), pl.BlockSpec((None, S, 1), lambda b, h: (h, b, 0))]
    out_shape = [jax.ShapeDtypeStruct((nb * S, nheads * HEAD), BF16), jax.ShapeDtypeStruct((nheads, nb * S, 1), F32)]
    return _call(body, rider, ins, name=name, grid=(nb, nheads), in_specs=in_specs, out_specs=out_specs,
                 out_shape=out_shape, scratch_shapes=[], sem=("parallel", "parallel"))


def _attn_bwd(q, k, v, o, do, lse, nb, nheads, dk, v_col0, scale, causal, name, rider=None):
    S, Skv = q.shape[0] // nb, k.shape[0] // nb
    tk = _pick(Skv, ATT_TILE)
    nkv = Skv // tk

    def body(q_ref, k_ref, v_ref, o_ref, do_ref, lse_ref, dq_ref, dk_ref, dv_ref, delta_ref, dob_ref, dqa_ref):
        dov = do_ref[...]
        delta_ref[...] = jnp.sum(o_ref[...].astype(F32) * dov.astype(F32), axis=-1, keepdims=True)
        dob_ref[...] = dov.astype(BF16)

        for j in range(nkv):
            c0 = j * tk
            kb = k_ref[c0:c0 + tk, :]
            vb = v_ref[c0:c0 + tk, :].astype(BF16)
            if causal:
                spans = [(c0, c0 + tk, True)] + ([(c0 + tk, S, False)] if c0 + tk < S else [])
            else:
                spans = [(0, S, False)]
            dk_acc = dv_acc = None
            for a, b, masked in spans:
                qb = q_ref[a:b, :]
                dob = dob_ref[a:b, :]
                s = lax.dot_general(qb, kb, _NT, preferred_element_type=F32)
                if masked:
                    s = _diag_mask(s)
                p = jnp.exp2(s - lse_ref[a:b, :])
                dp = lax.dot_general(dob, vb, _NT, preferred_element_type=F32)
                ds = (p * (dp - delta_ref[a:b, :])).astype(BF16)
                dv_p = lax.dot_general(p.astype(BF16), dob, _TN, preferred_element_type=F32)
                dk_p = lax.dot_general(ds, qb, _TN, preferred_element_type=F32)
                dk_acc, dv_acc = (dk_p, dv_p) if dk_acc is None else (dk_acc + dk_p, dv_acc + dv_p)
                dq_p = jnp.dot(ds, kb, preferred_element_type=F32) * scale
                if j == 0:
                    dqa_ref[a:b, :] = dq_p
                else:
                    dqa_ref[a:b, :] += dq_p
            dk_ref[c0:c0 + tk, :] = (dk_acc * LN2).astype(BF16)
            dv_ref[c0:c0 + tk, :] = dv_acc.astype(BF16)
        dq_ref[...] = dqa_ref[...].astype(BF16)

    ins = [q, k, v, o, do, lse]
    in_specs = [pl.BlockSpec((S, dk), lambda b, h: (b, h)), pl.BlockSpec((Skv, dk), lambda b, h: (b, h)),
                pl.BlockSpec((Skv, HEAD), lambda b, h: (b, v_col0 + h)), pl.BlockSpec((S, HEAD), lambda b, h: (b, h)),
                pl.BlockSpec((S, HEAD), lambda b, h: (b, h)), pl.BlockSpec((None, S, 1), lambda b, h: (h, b, 0))]
    out_specs = [pl.BlockSpec((S, dk), lambda b, h: (b, h)), pl.BlockSpec((Skv, dk), lambda b, h: (b, h)),
                 pl.BlockSpec((Skv, HEAD), lambda b, h: (b, h))]
    out_shape = [jax.ShapeDtypeStruct((nb * S, nheads * dk), BF16), jax.ShapeDtypeStruct((nb * Skv, nheads * dk), BF16),
                 jax.ShapeDtypeStruct((nb * Skv, nheads * HEAD), BF16)]
    return _call(body, rider, ins, name=name, grid=(nb, nheads), in_specs=in_specs, out_specs=out_specs,
                 out_shape=out_shape,
                 scratch_shapes=[pltpu.VMEM((S, 1), F32), pltpu.VMEM((S, HEAD), BF16), pltpu.VMEM((S, dk), F32)],
                 sem=("parallel", "parallel"))


def _spread_rope(a):
    zero = jnp.zeros(a.shape[:-1] + (32,), a.dtype)
    return jnp.concatenate([a[..., :32], zero, a[..., 32:], zero], axis=-1)


def _gather_rope(a):
    return jnp.concatenate([a[..., 0:32], a[..., 64:96]], axis=-1)


def _win_layout(w):
    return jnp.concatenate([w[:, C_ZG:C_END], w[:, C_ZU:C_CQ], w[:, C_QM:C_ZG], w[:, C_CQ:C_CKV], w[:, C_CKV:C_KPE],
                            _spread_rope(w[:, C_KPE:C_QM])], axis=1)


def _win_unlayout(d):
    return jnp.concatenate([d[:, Z_GM:Z_QM], d[:, Z_MLA:Z_MLA + Q_LORA], d[:, Z_MLA + Q_LORA:Z_KPE],
                            _gather_rope(d[:, Z_KPE:Z_COLS]), d[:, Z_QM:Z_MLA], d[:, 0:Z_GM]], axis=1)


def _wuq_layout(w):
    r = w.reshape(Q_LORA, MLA_HEADS, HEAD + MLA_ROPE)
    return jnp.concatenate([r[:, :, :HEAD].reshape(Q_LORA, -1), _spread_rope(r[:, :, HEAD:]).reshape(Q_LORA, -1)], axis=1)


def _wuq_unlayout(d):
    n = d[:, :MLA_HEADS * HEAD].reshape(Q_LORA, MLA_HEADS, HEAD)
    p = _gather_rope(d[:, MLA_HEADS * HEAD:].reshape(Q_LORA, MLA_HEADS, HEAD))
    return jnp.concatenate([n, p], axis=-1).reshape(Q_LORA, -1)


def _wukv_layout(w):
    r = w.reshape(KV_LORA, MLA_HEADS, 2 * HEAD)
    return jnp.concatenate([r[:, :, :HEAD].reshape(KV_LORA, -1), r[:, :, HEAD:].reshape(KV_LORA, -1)], axis=1)


def _wukv_unlayout(d):
    k = d[:, :MLA_HEADS * HEAD].reshape(KV_LORA, MLA_HEADS, HEAD)
    v = d[:, MLA_HEADS * HEAD:].reshape(KV_LORA, MLA_HEADS, HEAD)
    return jnp.concatenate([k, v], axis=-1).reshape(KV_LORA, -1)


AG_MID = ['w_uq', 'w_ukv', 'w_mem_kv', 'w_o_gm', 'w_o_mla', 'w_o_mem', 'w_out']
AG_FFN = ['w_ff1', 'w_ff2']
RS_GROUPS = {'ffn_proj': ['w_ff2', 'w_ff1', 'w_out', 'w_o_gm', 'w_o_mla', 'w_o_mem'],
             'lat': ['w_uq', 'w_ukv', 'w_mem_kv'], 'in_top': ['w_in'], 'in_bot': ['w_in']}


def _unride(res, rider):
    return (res, None) if rider is None else res


def _local_step(x, mem, positions, target, P, ws):
    B, S, _ = x.shape
    M = mem.shape[1]
    T = B * S
    x2d = x.reshape(T, D_MODEL)
    mem2d = mem.reshape(B * M, D_MODEL)
    tgt2d = target.reshape(T, D_MODEL)

    def row(v):
        return v.reshape(1, -1).astype(F32)

    inv_freq = ROPE_BASE ** (-jnp.arange(0, MLA_ROPE, 2, dtype=F32) / MLA_ROPE)
    zero = jnp.zeros_like(inv_freq)
    ang = positions.reshape(T).astype(F32)[:, None] * jnp.concatenate([inv_freq, zero, inv_freq, zero])
    cc = jnp.cos(ang) * jnp.concatenate([zero + 1.0, zero, zero + 1.0, zero])
    ss = jnp.sin(ang) * jnp.concatenate([zero - 1.0, zero, zero + 1.0, zero])

    g_mix, g_cq, g_ckv, g_ffn, g_mem = row(P['g_mix']), row(P['g_cq']), row(P['g_ckv']), row(P['g_ffn']), row(P['g_mem'])
    gqn, gkn, gmq, gmk = row(P['g_q_nope']), row(P['g_k_nope']), row(P['g_mq']), row(P['g_mk'])
    gqp, gkp = _spread_rope(row(P['g_q_pe'])), _spread_rope(row(P['g_k_pe']))
    gln, bln = row(P['g_gm_ln']), row(P['b_gm_ln'])
    wc = jnp.tril(P['w_spatial'].astype(F32))
    wct = jnp.swapaxes(wc, 1, 2).astype(BF16)
    wc = wc.astype(BF16)
    bst = jnp.broadcast_to(P['b_spatial'].astype(F32)[:, :, None], (GM_GROUPS, GM_CHUNK, LANES))

    ride = ws.gather(['w_in'])
    h, got = _unride(_rms_fwd(x2d, g_mix, "rms_mix", rider=ride), ride)
    w_in = _win_layout(ws.gathered(['w_in'], got)['w_in']).astype(BF16)
    ride = ws.gather(AG_MID)
    z, got = _unride(_matmul(h, w_in, 'nn', ACT, "mm_in", tn_t=1792, rider=ride), ride)
    mid = ws.gathered(AG_MID, got)
    w_uq, w_ukv = _wuq_layout(mid['w_uq']).astype(BF16), _wukv_layout(mid['w_ukv']).astype(BF16)
    w_mem_kv, w_o_gm, w_o_mla, w_o_mem, w_out = (mid[n] for n in ('w_mem_kv', 'w_o_gm', 'w_o_mla', 'w_o_mem', 'w_out'))
    ygm_pre = _gm_fwd(z, gln, bln, wc, bst, "gm_fwd")
    y_gm = _matmul(ygm_pre, w_o_gm, 'nn', ACT, "mm_o_gm")
    nq, nkv = _lat_fwd(z, g_cq, g_ckv, "lat_fwd")
    q = _matmul(nq, w_uq, 'nn', ACT, "mm_uq")
    kv = _matmul(nkv, w_ukv, 'nn', ACT, "mm_ukv")
    qcat, kcat, vv = _qk_fwd(q, kv, z, cc, ss, gqn, gqp, gkn, gkp, "qk_fwd")
    ride = ws.gather(AG_FFN)
    (o, lse), got = _unride(_attn_fwd(qcat, kcat, vv, B, MLA_HEADS, QCAT, 0, True, "mla_attn_fwd", rider=ride), ride)
    ffn = ws.gathered(AG_FFN, got)
    w_ff1, w_ff2 = ffn['w_ff1'], ffn['w_ff2']
    y_mla = _matmul(o, w_o_mla, 'nn', ACT, "mm_o_mla")
    nm = _rms_fwd(mem2d, g_mem, "rms_mem")
    kvm = _matmul(nm, w_mem_kv, 'nn', ACT, "mm_mem_kv")
    qm = _headnorm_fwd(z, Z_QM // (MEM_HEADS * HEAD), MEM_HEADS, gmq, MEM_SCALE * LOG2E, "memq_fwd")
    km = _headnorm_fwd(kvm, 0, MEM_HEADS, gmk, 1.0, "memk_fwd")
    om, lse_m = _attn_fwd(qm, km, kvm, B, MEM_HEADS, HEAD, MEM_HEADS, False, "mem_attn_fwd")
    y_mem = _matmul(om, w_o_mem, 'nn', ACT, "mm_o_mem")
    merged = _merge_fwd(z, y_gm, y_mla, y_mem, "merge_fwd")
    x1 = _matmul(merged, w_out, 'nn', F32, "mm_out", add=x2d)
    h2 = _rms_fwd(x1, g_ffn, "rms_ffn")
    a1 = _matmul(h2, w_ff1, 'nn', BF16, "mm_ff1")
    dx2, dx2b, loss_part = _matmul(a1, w_ff2, 'nn', F32, "mm_ff2", add=x1, relu2_a=True, sq_err_target=tgt2d,
                                   tm_t=512, tn_t=D_MODEL)

    G = {}
    d_ff2 = _matmul(a1, dx2b, 'tn', BF16, "mm_d_ff2", relu2_a=True)
    da1 = _matmul(dx2b, w_ff2, 'nt', BF16, "mm_da1", relu2_grad=a1)
    d_ff1 = _matmul(h2, da1, 'tn', BF16, "mm_d_ff1", col_shards=N_DEV)
    dh2 = _matmul(da1, w_ff1, 'nt', ACT, "mm_dh2")
    dx1, dx1b, G['g_ffn'] = _rms_bwd(x1, g_ffn, dh2, dx2, "rms_ffn_bwd", dx_dtypes=(F32, BF16))
    d_out = _matmul(merged, dx1b, 'tn', BF16, "mm_d_out")
    dmerged = _matmul(dx1b, w_out, 'nt', ACT, "mm_dmerged")
    dy_gm, dy_mla, dy_mem, dz = _merge_bwd(z, y_gm, y_mla, y_mem, dmerged, "merge_bwd")
    d_o_gm = _matmul(ygm_pre, dy_gm, 'tn', BF16, "mm_d_o_gm")
    d_o_mla = _matmul(o, dy_mla, 'tn', BF16, "mm_d_o_mla")
    d_o_mem = _matmul(om, dy_mem, 'tn', BF16, "mm_d_o_mem")
    dygm_pre = _matmul(dy_gm, w_o_gm, 'nt', ACT, "mm_dygm")
    dz, dws, dbs, G['g_gm_ln'], G['b_gm_ln'] = _gm_bwd(z, dygm_pre, gln, bln, wc, wct, bst, dz, "gm_bwd")
    G['w_spatial'] = jnp.tril(dws)
    G['b_spatial'] = jnp.sum(dbs.reshape(GM_CHUNK, GM_GROUPS, LANES), axis=-1).T
    do = _matmul(dy_mla, w_o_mla, 'nt', ACT, "mm_do")
    ride = ws.scatter('ffn_proj', {'w_ff2': d_ff2, 'w_ff1': d_ff1, 'w_out': d_out, 'w_o_gm': d_o_gm, 'w_o_mla': d_o_mla,
                                   'w_o_mem': d_o_mem})
    (dqc, dkc, dvv), got = _unride(_attn_bwd(qcat, kcat, vv, o, do, lse, B, MLA_HEADS, QCAT, 0, MLA_SCALE, True,
                                             "mla_attn_bwd", rider=ride), ride)
    ws.scattered('ffn_proj', got)
    dq, dkv, dkpe, G['g_q_nope'], dgqp, G['g_k_nope'], dgkp = _qk_bwd(q, kv, z, cc, ss, gqn, gqp, gkn, gkp, dqc, dkc, dvv,
                                                                     "qk_bwd")
    G['g_q_pe'], G['g_k_pe'] = _gather_rope(dgqp), _gather_rope(dgkp)
    d_uq = _wuq_unlayout(_matmul(nq, dq, 'tn', BF16, "mm_d_uq"))
    dnq = _matmul(dq, w_uq, 'nt', ACT, "mm_dnq")
    d_ukv = _wukv_unlayout(_matmul(nkv, dkv, 'tn', BF16, "mm_d_ukv"))
    dnkv = _matmul(dkv, w_ukv, 'nt', ACT, "mm_dnkv")
    dz, G['g_cq'], G['g_ckv'] = _lat_bwd(z, dnq, dnkv, dkpe, g_cq, g_ckv, dz, "lat_bwd")
    dom = _matmul(dy_mem, w_o_mem, 'nt', ACT, "mm_dom")
    dqm, dkm, dvm = _attn_bwd(qm, km, kvm, om, dom, lse_m, B, MEM_HEADS, HEAD, MEM_HEADS, MEM_SCALE, False, "mem_attn_bwd")
    dz, G['g_mq'] = _headnorm_bwd(z, Z_QM // (MEM_HEADS * HEAD), MEM_HEADS, gmq, dqm, None, "memq_bwd",
                                  into=(dz, Z_QM // (MEM_HEADS * HEAD)))
    dkvm, G['g_mk'] = _headnorm_bwd(kvm, 0, MEM_HEADS, gmk, dkm, dvm, "memk_bwd")
    d_mem_kv = _matmul(nm, dkvm, 'tn', BF16, "mm_d_mem_kv")
    dnm = _matmul(dkvm, w_mem_kv, 'nt', ACT, "mm_dnm")
    G['g_mem'], = _rms_bwd(mem2d, g_mem, dnm, None, "rms_mem_bwd", dx_dtypes=())
    half = D_MODEL // 2
    ride = ws.scatter('lat', {'w_uq': d_uq, 'w_ukv': d_ukv, 'w_mem_kv': d_mem_kv})
    d_top, got = _unride(_matmul(h, dz, 'tn', BF16, "mm_d_in_top", tn_t=1792, m_rows=(0, half), rider=ride), ride)
    ws.scattered('lat', got)
    ride = ws.scatter('in_top', {'w_in': _win_unlayout(d_top)})
    d_bot, got = _unride(_matmul(h, dz, 'tn', BF16, "mm_d_in_bot", tn_t=1792, m_rows=(half, half), rider=ride), ride)
    ws.scattered('in_top', got)
    ride = ws.scatter('in_bot', {'w_in': _win_unlayout(d_bot)})
    dh, got = _unride(_matmul(dz, w_in, 'nt', ACT, "mm_dh", rider=ride), ride)
    ws.scattered('in_bot', got)
    gx, G['g_mix'] = _rms_bwd(x2d, g_mix, dh, dx1, "rms_mix_bwd")
    return loss_part, gx.reshape(B, S, D_MODEL), G


def _all_gather8(xs, name):
    def body(x_ref, out_ref, send_sems, recv_sems, local_sem):
        x, y, c = lax.axis_index("x"), lax.axis_index("y"), lax.axis_index("c")
        me, sibling = (x, y, c), (x, y, 1 - c)
        chips = [(1 - x, y), (x, 1 - y), (1 - x, 1 - y)]

        def rows(px, py, pc):
            return out_ref.at[4 * px + 2 * py + pc]

        def copy(k, block, to, src=None):
            return pltpu.make_async_remote_copy(
                src_ref=rows(*block) if src is None else src, dst_ref=rows(*block),
                send_sem=send_sems.at[k], recv_sem=recv_sems.at[k], device_id=to, device_id_type=MESH)

        mine = pltpu.make_async_copy(x_ref, rows(*me), local_sem)
        mine.start()
        first = [copy(0, me, sibling, src=x_ref)]
        first += [copy(1 + j, me, (*chip, c), src=x_ref) for j, chip in enumerate(chips)]
        for cp in first:
            cp.start()
        passed = [copy(4 + j, (*chip, c), sibling) for j, chip in enumerate(chips)]
        for j, chip in enumerate(chips):
            copy(1 + j, (*chip, c), me).wait_recv()
            passed[j].start()
        copy(0, sibling, me).wait_recv()
        for j, chip in enumerate(chips):
            copy(4 + j, (*chip, 1 - c), me).wait_recv()
        for cp in first + passed:
            cp.wait_send()
        mine.wait()

    return pl.pallas_call(
        body, name=name, in_specs=[HBM_SPEC], out_specs=HBM_SPEC,
        out_shape=jax.ShapeDtypeStruct((N_DEV,) + xs.shape, xs.dtype),
        scratch_shapes=[pltpu.SemaphoreType.DMA((7,)), pltpu.SemaphoreType.DMA((7,)), pltpu.SemaphoreType.DMA],
    )(xs)


ADAMW_TILE_ELEMS = 256 * 1024


def _adamw_rows(w, g, m, v):
    m2 = ADAM_B1 * m + (1.0 - ADAM_B1) * g
    v2 = ADAM_B2 * v + (1.0 - ADAM_B2) * (g * g)
    m_hat = m2 / (1.0 - ADAM_B1 ** ADAM_STEP)
    v_hat = v2 / (1.0 - ADAM_B2 ** ADAM_STEP)
    delta = -ADAM_LR * (m_hat / (jnp.sqrt(v_hat) + ADAM_EPS) + ADAM_WD * w)
    return delta, m2, v2


def _sum_adamw(parts, w, m, v, name):
    rows, cols = w.shape
    assert sum(p.shape[1] for p in parts) == rows
    tr = _pick(min(p.shape[1] for p in parts), max(16, ADAMW_TILE_ELEMS // cols), 16)
    n = parts[0].shape[0]
    counts = [p.shape[1] // tr for p in parts]
    starts = [sum(counts[:k]) for k in range(len(parts))]

    def body(*refs):
        p_refs = refs[:len(parts)]
        w_ref, m_ref, v_ref, g_ref, d_ref, m2_ref, v2_ref = refs[len(parts):]
        g = None
        for p_ref, start in zip(p_refs, starts):
            gk = p_ref[0].astype(F32)
            for k in range(1, n):
                gk = gk + p_ref[k].astype(F32)
            g = gk if g is None else jnp.where(pl.program_id(0) >= start, gk, g)
        delta, m2, v2 = _adamw_rows(w_ref[...], g, m_ref[...], v_ref[...])
        g_ref[...] = g
        d_ref[...] = delta
        m2_ref[...] = m2
        v2_ref[...] = v2

    flat = pl.BlockSpec((tr, cols), lambda i: (i, 0))
    out = jax.ShapeDtypeStruct((rows, cols), F32)
    p_specs = [pl.BlockSpec((n, tr, cols), lambda i, s=s, c=c: (0, jnp.clip(i - s, 0, c - 1), 0))
               for s, c in zip(starts, counts)]
    return pl.pallas_call(
        body, name=name, grid=(rows // tr,), in_specs=p_specs + [flat, flat, flat], out_specs=[flat] * 4,
        out_shape=[out] * 4, compiler_params=_params(("parallel",)),
    )(*parts, w, m, v)


SMALL_WIDTH = {'g_mix': 1024, 'g_cq': 384, 'g_ckv': 256, 'g_q_nope': 128, 'g_q_pe': 128, 'g_k_nope': 128, 'g_k_pe': 128,
               'g_gm_ln': 512, 'b_gm_ln': 512, 'g_mem': 1024, 'g_mq': 128, 'g_mk': 128, 'g_ffn': 1024}
NARROW = ('g_q_pe', 'g_k_pe')


def _small_layout():
    layout, r = {}, 0
    for name in SMALL + ['loss']:
        rows = {'w_spatial': GM_GROUPS * GM_CHUNK, 'b_spatial': GM_GROUPS, 'loss': 1}.get(name) or SMALL_WIDTH[name] // LANES
        layout[name] = (r, rows)
        r += -(-rows // 8) * 8
    return layout, r


def _small_pack(grads, loss_part, name):
    layout, total = _small_layout()
    names = SMALL + ['loss']

    def body(*refs):
        out_ref = refs[-1]
        out_ref[...] = jnp.zeros((total, LANES), F32)
        for ref, n in zip(refs[:-1], names):
            r0, rows = layout[n]
            if n == 'w_spatial':
                for g in range(GM_GROUPS):
                    out_ref[r0 + g * GM_CHUNK:r0 + (g + 1) * GM_CHUNK, :] = ref[g]
            elif n == 'b_spatial':
                out_ref[r0:r0 + rows, :] = ref[...]
            else:
                for k in range(rows):
                    out_ref[r0 + k:r0 + k + 1, :] = ref[:, k * LANES:(k + 1) * LANES]

    return pl.pallas_call(body, name=name, out_shape=jax.ShapeDtypeStruct((total, LANES), F32))(
        *[grads[n] for n in SMALL], loss_part)


def _small_adamw(parts, w, m, v, name):
    layout, _ = _small_layout()
    n_dev = parts.shape[0]

    def body(*refs):
        p_ref = refs[0]
        ins = refs[1:1 + 3 * len(SMALL)]
        outs = refs[1 + 3 * len(SMALL):-1]

        def gsum(r0, rows):
            g = p_ref[0, r0:r0 + rows, :]
            for d in range(1, n_dev):
                g = g + p_ref[d, r0:r0 + rows, :]
            return g

        def step(idx, g, at):
            w_ref, m_ref, v_ref = ins[3 * idx:3 * idx + 3]
            delta, m2, v2 = _adamw_rows(w_ref[at], g, m_ref[at], v_ref[at])
            for ref, val in zip(outs[4 * idx:4 * idx + 4], (g, delta, m2, v2)):
                ref[at] = val

        for idx, n in enumerate(SMALL):
            r0, rows = layout[n]
            if n == 'w_spatial':
                for g in range(GM_GROUPS):
                    step(idx, gsum(r0 + g * GM_CHUNK, GM_CHUNK), (0, g))
            elif n == 'b_spatial':
                step(idx, gsum(r0, rows), (0,))
            else:
                for k in range(rows):
                    step(idx, gsum(r0 + k, 1), (slice(None), slice(k * LANES, (k + 1) * LANES)))
        refs[-1][...] = gsum(layout['loss'][0], 8)

    flat_in = [d[n] for n in SMALL for d in (w, m, v)]
    out_shape = [jax.ShapeDtypeStruct(w[n].shape, F32) for n in SMALL for _ in range(4)]
    res = pl.pallas_call(body, name=name, out_shape=out_shape + [jax.ShapeDtypeStruct((8, LANES), F32)])(parts, *flat_in)
    groups = [{n: res[4 * i + j] for i, n in enumerate(SMALL)} for j in range(4)]
    return groups, res[-1]


def _full_from_gathered(gathered, name):
    r, c = BIG_SHAPE[name]
    if BIG_AXIS[name] == 0:
        return gathered.reshape(r, c)
    return gathered.transpose(1, 0, 2).reshape(r, c)


def _shards_of_full(g, name):
    if g.ndim == 3:
        return g
    r, c = BIG_SHAPE[name]
    if BIG_AXIS[name] == 0:
        return g.reshape(N_DEV, r // N_DEV, c)
    return g.reshape(g.shape[0], N_DEV, c // N_DEV).transpose(1, 0, 2)


class _DistWeights:
    def __init__(self, shards):
        self.shards = shards
        self.received = {}

    def gather(self, names):
        return _Gather2([self.shards[n].astype(BF16) for n in names])

    def gathered(self, names, got):
        return {n: _full_from_gathered(g, n) for n, g in zip(names, got)}

    def scatter(self, key, grads):
        return _Exchange([_shards_of_full(grads[n], n) for n in RS_GROUPS[key]], scatter=True)

    def scattered(self, key, got):
        for n, g in zip(RS_GROUPS[key], got):
            self.received.setdefault(n, []).append(g)


def kernel(x, mem, positions, g_mix, w_in, g_cq, w_uq, g_ckv, w_ukv, g_q_nope, g_q_pe, g_k_nope, g_k_pe, g_gm_ln, b_gm_ln, w_spatial, b_spatial, g_mem, w_mem_kv, g_mq, g_mk, w_o_gm, w_o_mla, w_o_mem, w_out, g_ffn, w_ff1, w_ff2, loss_target, m_g_mix, m_w_in, m_g_cq, m_w_uq, m_g_ckv, m_w_ukv, m_g_q_nope, m_g_q_pe, m_g_k_nope, m_g_k_pe, m_g_gm_ln, m_b_gm_ln, m_w_spatial, m_b_spatial, m_g_mem, m_w_mem_kv, m_g_mq, m_g_mk, m_w_o_gm, m_w_o_mla, m_w_o_mem, m_w_out, m_g_ffn, m_w_ff1, m_w_ff2, v_g_mix, v_w_in, v_g_cq, v_w_uq, v_g_ckv, v_w_ukv, v_g_q_nope, v_g_q_pe, v_g_k_nope, v_g_k_pe, v_g_gm_ln, v_b_gm_ln, v_w_spatial, v_b_spatial, v_g_mem, v_w_mem_kv, v_g_mq, v_g_mk, v_w_o_gm, v_w_o_mla, v_w_o_mem, v_w_out, v_g_ffn, v_w_ff1, v_w_ff2):
    given = dict(locals())
    w = {n: given[n][0] for n in WEIGHTS}
    mom = {n: given['m_' + n][0] for n in WEIGHTS}
    var = {n: given['v_' + n][0] for n in WEIGHTS}

    ws = _DistWeights({n: w[n] for n in BIG})
    loss_part, grad_x, G = _local_step(x, mem, positions, loss_target, {n: w[n] for n in SMALL}, ws)

    outs = {}
    for n in BIG:
        for prefix, res in zip(("grad_", "delta_", "new_m_", "new_v_"),
                               _sum_adamw(ws.received[n], w[n], mom[n], var[n], "adamw_" + n)):
            outs[prefix + n] = res[None]

    def widen(d):
        return {n: (jnp.pad(d[n], ((0, 0), (0, LANES - MLA_ROPE))) if n in NARROW else d[n]) for n in SMALL}

    parts = _all_gather8(_small_pack(widen(G), loss_part, "small_pack"), "ag_small")
    small, loss_rows = _small_adamw(parts, *[widen({n: given[prefix + n] for n in SMALL}) for prefix in ("", "m_", "v_")],
                                    "adamw_small")
    loss = 0.5 * jnp.sum(loss_rows) / D_MODEL
    for prefix, group in zip(("grad_", "delta_", "new_m_", "new_v_"), small):
        for n in SMALL:
            outs[prefix + n] = group[n][:, :MLA_ROPE] if n in NARROW else group[n]
    return (loss, grad_x, *[outs[p + n] for p in ("grad_", "delta_", "new_m_", "new_v_") for n in WEIGHTS])
```

```python
import functools
import math

import jax
import jax.numpy as jnp
from jax import lax
from jax.experimental import pallas as pl
from jax.experimental.pallas import tpu as pltpu

F32 = jnp.float32
BF16 = jnp.bfloat16
ACT = BF16

D_MODEL = 1024
MEM_HEADS = 4
HEAD = 128
GM_WIDTH = 512
GM_CHUNK = 128
GM_GROUPS = 4
MLA_HEADS = 8
MLA_ROPE = 64
Q_LORA = 384
KV_LORA = 256
D_FF = 4096
EPS = 1e-6
ROPE_BASE = 10000.0
MLA_SCALE = 1.0 / math.sqrt(HEAD + MLA_ROPE)
MEM_SCALE = 1.0 / math.sqrt(HEAD)
LOG2E = 1.4426950408889634
LN2 = 0.6931471805599453
ATT_TILE = 256
C_ZU, C_ZV, C_CQ, C_CKV, C_KPE, C_QM, C_ZG, C_END = 0, 512, 1024, 1408, 1664, 1728, 2240, 5312
Z_GM, Z_QM, Z_MLA, Z_KPE, Z_COLS = 3072, 4096, 4608, 5248, 5376
MLA_W = 768
QCAT = 2 * HEAD
ADAM_LR, ADAM_B1, ADAM_B2, ADAM_EPS, ADAM_WD, ADAM_STEP = 0.001, 0.9, 0.999, 1e-08, 0.01, 10
N_DEV = 8
LANES = 128
VMEM_LIMIT = 48 * 1024 * 1024
MAX_K_TILE = 8192
NEG = -1e30

BIG = ['w_in', 'w_uq', 'w_ukv', 'w_mem_kv', 'w_o_gm', 'w_o_mla', 'w_o_mem', 'w_out', 'w_ff1', 'w_ff2']
BIG_AXIS = {'w_in': 1, 'w_uq': 1, 'w_ukv': 1, 'w_mem_kv': 0, 'w_o_gm': 1, 'w_o_mla': 0, 'w_o_mem': 1,
            'w_out': 0, 'w_ff1': 1, 'w_ff2': 0}
BIG_SHAPE = {'w_in': (1024, 5312), 'w_uq': (384, 1536), 'w_ukv': (256, 2048), 'w_mem_kv': (1024, 1024),
             'w_o_gm': (512, 1024), 'w_o_mla': (1024, 1024), 'w_o_mem': (512, 1024), 'w_out': (1024, 1024),
             'w_ff1': (1024, 4096), 'w_ff2': (4096, 1024)}
SMALL = ['g_mix', 'g_cq', 'g_ckv', 'g_q_nope', 'g_q_pe', 'g_k_nope', 'g_k_pe', 'g_gm_ln', 'b_gm_ln',
         'w_spatial', 'b_spatial', 'g_mem', 'g_mq', 'g_mk', 'g_ffn']
WEIGHTS = ['g_mix', 'w_in', 'g_cq', 'w_uq', 'g_ckv', 'w_ukv', 'g_q_nope', 'g_q_pe', 'g_k_nope', 'g_k_pe',
           'g_gm_ln', 'b_gm_ln', 'w_spatial', 'b_spatial', 'g_mem', 'w_mem_kv', 'g_mq', 'g_mk', 'w_o_gm',
           'w_o_mla', 'w_o_mem', 'w_out', 'g_ffn', 'w_ff1', 'w_ff2']


def _pick(n, target, mult=LANES):
    best = None
    t = mult
    while t <= min(n, target):
        if n % t == 0:
            best = t
        t += mult
    return best if best is not None else n


def _params(sem):
    return pltpu.CompilerParams(dimension_semantics=sem, vmem_limit_bytes=VMEM_LIMIT)


MESH = pl.DeviceIdType.MESH
HBM_SPEC = pl.BlockSpec(memory_space=pltpu.HBM)


class _Exchange:
    def __init__(self, srcs, scatter):
        self.srcs, self.scatter = list(srcs), scatter
        self.out_shapes = [jax.ShapeDtypeStruct(s.shape if scatter else (N_DEV,) + s.shape, s.dtype) for s in self.srcs]
        n = len(self.srcs)
        self.scratch = [pltpu.SemaphoreType.DMA((n, N_DEV - 1)), pltpu.SemaphoreType.DMA((n, N_DEV - 1)),
                        pltpu.SemaphoreType.DMA((n,))]

    def _copies(self, src_refs, dst_refs, send_sems, recv_sems, local_sems):
        x, y, c = lax.axis_index("x"), lax.axis_index("y"), lax.axis_index("c")
        me = 4 * x + 2 * y + c
        local, remote = [], []
        for a, (src_ref, dst_ref) in enumerate(zip(src_refs, dst_refs)):
            def mine_for(dev, src_ref=src_ref):
                return src_ref.at[dev] if self.scatter else src_ref

            local.append(pltpu.make_async_copy(mine_for(me), dst_ref.at[me], local_sems.at[a]))
            for k in range(1, N_DEV):
                px = 1 - x if k & 4 else x
                py = 1 - y if k & 2 else y
                pc = 1 - c if k & 1 else c
                remote.append(pltpu.make_async_remote_copy(
                    src_ref=mine_for(4 * px + 2 * py + pc), dst_ref=dst_ref.at[me], send_sem=send_sems.at[a, k - 1],
                    recv_sem=recv_sems.at[a, k - 1], device_id=(px, py, pc), device_id_type=MESH))
        return local, remote

    def start(self, *refs):
        local, remote = self._copies(*refs)
        for cp in local + remote:
            cp.start()

    def forward(self, *refs):
        pass

    def finish(self, *refs):
        local, remote = self._copies(*refs)
        for cp in remote + local:
            cp.wait()


class _Gather2:
    def __init__(self, srcs):
        self.srcs = list(srcs)
        self.out_shapes = [jax.ShapeDtypeStruct((N_DEV,) + s.shape, s.dtype) for s in self.srcs]
        n = len(self.srcs)
        self.scratch = [pltpu.SemaphoreType.DMA((n, N_DEV - 1)), pltpu.SemaphoreType.DMA((n, N_DEV - 1)),
                        pltpu.SemaphoreType.DMA((n,))]

    def _plan(self, src_refs, dst_refs, send_sems, recv_sems, local_sems):
        x, y, c = lax.axis_index("x"), lax.axis_index("y"), lax.axis_index("c")
        chips = [(1 - x, y), (x, 1 - y), (1 - x, 1 - y)]
        plans = []
        for a, (src_ref, dst_ref) in enumerate(zip(src_refs, dst_refs)):
            def copy(k, block, to, src=None, a=a, dst_ref=dst_ref):
                at = dst_ref.at[4 * block[0] + 2 * block[1] + block[2]]
                return pltpu.make_async_remote_copy(src_ref=at if src is None else src, dst_ref=at,
                                                    send_sem=send_sems.at[a, k], recv_sem=recv_sems.at[a, k],
                                                    device_id=to, device_id_type=MESH)

            local = pltpu.make_async_copy(src_ref, dst_ref.at[4 * x + 2 * y + c], local_sems.at[a])
            first = [copy(0, (x, y, c), (x, y, 1 - c), src=src_ref)]
            first += [copy(1 + j, (x, y, c), (*chip, c), src=src_ref) for j, chip in enumerate(chips)]
            passed = [copy(4 + j, (*chip, c), (x, y, 1 - c)) for j, chip in enumerate(chips)]
            arrivals = [copy(1 + j, (*chip, c), (x, y, c)) for j, chip in enumerate(chips)]
            late = [copy(0, (x, y, 1 - c), (x, y, c))] + [copy(4 + j, (*chip, 1 - c), (x, y, c)) for j, chip in enumerate(chips)]
            plans.append((local, first, passed, arrivals, late))
        return plans

    def start(self, *refs):
        for local, first, _, _, _ in self._plan(*refs):
            local.start()
            for cp in first:
                cp.start()

    def forward(self, *refs):
        for _, _, passed, arrivals, _ in self._plan(*refs):
            for arrived, onward in zip(arrivals, passed):
                arrived.wait_recv()
                onward.start()

    def finish(self, *refs):
        for local, first, passed, _, late in self._plan(*refs):
            for cp in late:
                cp.wait_recv()
            for cp in first + passed:
                cp.wait_send()
            local.wait()


def _call(body, rider, ins, *, name, grid, in_specs, out_specs, out_shape, scratch_shapes, sem):
    if rider is None:
        return pl.pallas_call(body, name=name, grid=grid, in_specs=in_specs, out_specs=out_specs, out_shape=out_shape,
                              scratch_shapes=scratch_shapes, compiler_params=_params(sem))(*ins)
    single = not isinstance(out_shape, (list, tuple))
    own_specs, own_shapes = ([out_specs], [out_shape]) if single else (list(out_specs), list(out_shape))
    n_in, n_out, n_sc, n_r = len(ins), len(own_shapes), len(scratch_shapes), len(rider.srcs)
    n_all_in = n_in + n_r

    def carrying(*refs):
        own_in, srcs = refs[:n_in], refs[n_in:n_in + n_r]
        own_out, dsts = refs[n_all_in:n_all_in + n_out], refs[n_all_in + n_out:n_all_in + n_out + n_r]
        own_sc = refs[n_all_in + n_out + n_r:n_all_in + n_out + n_r + n_sc]
        sems = refs[n_all_in + n_out + n_r + n_sc:]
        first = last = late = None
        for d, steps in enumerate(grid):
            f, l = pl.program_id(d) == 0, pl.program_id(d) == steps - 1
            t = pl.program_id(d) == ((3 * steps) // 4 if d == 0 else 0)
            first, last, late = (f, l, t) if first is None else (first & f, last & l, late & t)

        @pl.when(first)
        def _():
            rider.start(srcs, dsts, *sems)

        @pl.when(late)
        def _():
            rider.forward(srcs, dsts, *sems)

        body(*own_in, *own_out, *own_sc)

        @pl.when(last)
        def _():
            rider.finish(srcs, dsts, *sems)

    res = pl.pallas_call(
        carrying, name=name, grid=grid, in_specs=list(in_specs) + [HBM_SPEC] * n_r,
        out_specs=own_specs + [HBM_SPEC] * n_r, out_shape=own_shapes + rider.out_shapes,
        scratch_shapes=list(scratch_shapes) + rider.scratch, compiler_params=_params(("arbitrary",) * len(grid)),
    )(*ins, *rider.srcs)
    own = res[:n_out]
    return (own[0] if single else list(own)), list(res[n_out:])


def _matmul(a, b, mode, out_dtype, name, add=None, relu2_a=False, relu2_grad=None,
            tm_t=None, tn_t=None, tk_t=None, rider=None, m_rows=None, col_shards=None, sq_err_target=None,
            rms_gain=None):
    if mode == 'nn':
        (M, K), (K2, N) = a.shape, b.shape
    elif mode == 'nt':
        (M, K), (N, K2) = a.shape, b.shape
    else:
        (K, M), (K2, N) = a.shape, b.shape
    assert K == K2, (name, a.shape, b.shape)
    m_first = 0
    if m_rows is not None:
        assert mode == 'tn'
        m_first, M = m_rows
    if col_shards is not None:
        assert add is None and relu2_grad is None and sq_err_target is None and tn_t is None
    if mode == 'tn':
        d_tm, d_tn, d_tk = 1024, (2048 if M <= 512 else 1024), 2048
    else:
        wide = add is None and sq_err_target is None and jnp.dtype(out_dtype).itemsize == 2 and K <= D_FF
        d_tm, d_tn, d_tk = (2048 if K <= 1024 else 1024), (1024 if wide else 512), MAX_K_TILE
    tm, tn, tk = _pick(M, tm_t or d_tm), _pick(N, tn_t or d_tn), _pick(K, tk_t or d_tk)
    gm, gn, nk = M // tm, N // tn, K // tk
    if mode == 'nn':
        a_spec = pl.BlockSpec((tm, tk), lambda i, j, k: (i, k))
        b_spec = pl.BlockSpec((tk, tn), lambda i, j, k: (k, j))
        dims = (((1,), (0,)), ((), ()))
    elif mode == 'nt':
        a_spec = pl.BlockSpec((tm, tk), lambda i, j, k: (i, k))
        b_spec = pl.BlockSpec((tn, tk), lambda i, j, k: (j, k))
        dims = (((1,), (1,)), ((), ()))
    else:
        assert m_first % tm == 0
        a_spec = pl.BlockSpec((tk, tm), lambda i, j, k: (k, m_first // tm + i))
        b_spec = pl.BlockSpec((tk, tn), lambda i, j, k: (k, j))
        dims = (((0,), (0,)), ((), ()))
    o_spec = pl.BlockSpec((tm, tn), lambda i, j, k: (i, j))
    shard_w = N // col_shards if col_shards is not None else tn
    assert tn % shard_w == 0
    has_add, has_e, has_t = add is not None, relu2_grad is not None, sq_err_target is not None
    has_n = rms_gain is not None
    assert not has_t or (nk == 1 and tn % LANES == 0)
    assert not has_n or (nk == 1 and tn == N and not has_t and col_shards is None)

    def body(*refs):
        a_ref, b_ref = refs[0], refs[1]
        pos = 2
        add_ref = e_ref = t_ref = None
        if has_add:
            add_ref = refs[pos]
            pos += 1
        if has_e:
            e_ref = refs[pos]
            pos += 1
        if has_t:
            t_ref = refs[pos]
            pos += 1
        if has_n:
            gain_ref = refs[pos]
            pos += 1
        o_ref = refs[pos]
        acc_ref = refs[pos + 1] if nk > 1 else None

        av = a_ref[...]
        if relu2_a:
            av = jnp.maximum(av, 0)
            av = av * av
        prod = lax.dot_general(av.astype(BF16), b_ref[...].astype(BF16), dims, preferred_element_type=F32)

        def finish(r):
            if has_add:
                r = r + add_ref[...]
            if has_e:
                r = r * (2.0 * jnp.maximum(e_ref[...].astype(F32), 0.0))
            if has_t:
                err = r - t_ref[...]
                r = err * (1.0 / N)
                refs[pos + 1][...] = r.astype(BF16)
                sq = err * err
                part = sq[:, 0:LANES]
                for c in range(1, tn // LANES):
                    part = part + sq[:, c * LANES:(c + 1) * LANES]
                _acc_rows(refs[pos + 2], part, (pl.program_id(0) == 0) & (pl.program_id(1) == 0))
            if has_n:
                refs[pos + 1][...] = (r * _rms(r, N) * gain_ref[...]).astype(BF16)
            if col_shards is not None:
                for s in range(tn // shard_w):
                    o_ref[s] = r[:, s * shard_w:(s + 1) * shard_w].astype(out_dtype)
            else:
                o_ref[...] = r.astype(out_dtype)

        if nk == 1:
            finish(prod)
        else:
            k = pl.program_id(2)

            @pl.when(k == 0)
            def _():
                acc_ref[...] = prod

            @pl.when(k > 0)
            def _():
                acc_ref[...] += prod

            @pl.when(k == nk - 1)
            def _():
                finish(acc_ref[...])

    ins, specs = [a, b], [a_spec, b_spec]
    if has_add:
        ins.append(add)
        specs.append(o_spec)
    if has_e:
        ins.append(relu2_grad)
        specs.append(o_spec)
    out_specs, out_shape, sem = o_spec, jax.ShapeDtypeStruct((M, N), out_dtype), ("parallel", "parallel", "arbitrary")
    if has_t:
        ins.append(sq_err_target)
        specs.append(o_spec)
        out_specs = [o_spec, o_spec, pl.BlockSpec((1, LANES), lambda i, j, k: (0, 0))]
        out_shape = [out_shape, jax.ShapeDtypeStruct((M, N), BF16), jax.ShapeDtypeStruct((1, LANES), F32)]
        sem = ("arbitrary", "arbitrary", "arbitrary")
    if has_n:
        ins.append(rms_gain)
        specs.append(pl.BlockSpec((1, N), lambda i, j, k: (0, 0)))
        out_specs = [o_spec, o_spec]
        out_shape = [out_shape, jax.ShapeDtypeStruct((M, N), BF16)]
    if col_shards is not None:
        out_specs = pl.BlockSpec((tn // shard_w, tm, shard_w), lambda i, j, k: (j, i, 0))
        out_shape = jax.ShapeDtypeStruct((col_shards, M, shard_w), out_dtype)
    return _call(body, rider, ins, name=name, grid=(gm, gn, nk), in_specs=specs, out_specs=out_specs, out_shape=out_shape,
                 scratch_shapes=[pltpu.VMEM((tm, tn), F32)] if nk > 1 else [], sem=sem)


ROW_BLOCK_BYTES = 12 * 1024 * 1024


def _row_tile(rows, row_bytes):
    return _pick(rows, max(16, min(1024, ROW_BLOCK_BYTES // row_bytes)), 16)


def _rowspec(tr, width, col=0):
    return pl.BlockSpec((tr, width), lambda i, col=col: (i, col))


def _fullspec(shape):
    nd = len(shape)
    return pl.BlockSpec(shape, lambda i, nd=nd: (0,) * nd)


def _rms(x, width):
    x = x.astype(F32)
    return lax.rsqrt(jnp.sum(x * x, axis=-1, keepdims=True) * (1.0 / width) + EPS)


def _rms_bwd_rows(x, g, dy, width):
    x, dy = x.astype(F32), dy.astype(F32)
    r = _rms(x, width)
    xh = x * r
    dn = dy * g
    dx = r * (dn - xh * (jnp.sum(dn * xh, axis=-1, keepdims=True) * (1.0 / width)))
    return dx, dy * xh


def _acc_rows(ref, val, first):
    s = jnp.sum(val, axis=0, keepdims=True)

    @pl.when(first)
    def _():
        ref[...] = s

    @pl.when(jnp.logical_not(first))
    def _():
        ref[...] += s


def _rms_fwd(x, g, name, rider=None):
    rows, width = x.shape
    tr = _row_tile(rows, 6 * width)

    def body(x_ref, g_ref, o_ref):
        xv = x_ref[...]
        o_ref[...] = (xv * _rms(xv, width) * g_ref[...]).astype(BF16)

    return _call(body, rider, [x, g], name=name, grid=(rows // tr,),
                 in_specs=[_rowspec(tr, width), _fullspec((1, width))], out_specs=_rowspec(tr, width),
                 out_shape=jax.ShapeDtypeStruct((rows, width), BF16), scratch_shapes=[], sem=("parallel",))


def _rms_bwd(x, g, dy, res, name, dx_dtypes=(F32,)):
    rows, width = x.shape
    tr = _row_tile(rows, 18 * width)
    has_res = res is not None
    n_in = 4 if has_res else 3

    def body(*refs):
        x_ref, g_ref, dy_ref = refs[:3]
        dx, dgv = _rms_bwd_rows(x_ref[...], g_ref[...], dy_ref[...], width)
        if has_res:
            dx = dx + refs[3][...]
        for ref, dt in zip(refs[n_in:], dx_dtypes):
            ref[...] = dx.astype(dt)
        _acc_rows(refs[-1], dgv, pl.program_id(0) == 0)

    ins = [x, g, dy] + ([res] if has_res else [])
    specs = [_rowspec(tr, width), _fullspec((1, width)), _rowspec(tr, width)] + ([_rowspec(tr, width)] if has_res else [])
    return pl.pallas_call(
        body, name=name, grid=(rows // tr,), in_specs=specs,
        out_specs=[_rowspec(tr, width)] * len(dx_dtypes) + [_fullspec((1, width))],
        out_shape=[jax.ShapeDtypeStruct((rows, width), dt) for dt in dx_dtypes] + [jax.ShapeDtypeStruct((1, width), F32)],
        compiler_params=_params(("arbitrary",)),
    )(*ins)


_GELU_C = math.sqrt(2.0 / math.pi)


def _gelu(x):
    t = jnp.tanh(_GELU_C * (x + 0.044715 * (x * x * x)))
    return 0.5 * x * (1.0 + t), t


def _gelu_grad(x, t):
    return 0.5 * (1.0 + t) + 0.5 * x * (1.0 - t * t) * (_GELU_C * (1.0 + 3.0 * 0.044715 * (x * x)))


def _gm_forward_rows(zu, zv, gln, bln, wc_ref, bst, n_chunk):
    u, tu = _gelu(zu)
    a, ta = _gelu(zv)
    mu = jnp.mean(a, axis=-1, keepdims=True)
    ac = a - mu
    rs = lax.rsqrt(jnp.mean(ac * ac, axis=-1, keepdims=True) + EPS)
    n = ac * rs
    v = n * gln + bln
    vb = v.astype(BF16)
    rows = []
    for c in range(n_chunk):
        cols = []
        for g in range(GM_GROUPS):
            vc = vb[c * GM_CHUNK:(c + 1) * GM_CHUNK, g * LANES:(g + 1) * LANES]
            mixed = jnp.dot(wc_ref[g], vc, preferred_element_type=F32) + bst[g]
            cols.append(mixed)
        rows.append(jnp.concatenate(cols, axis=1))
    mixed = jnp.concatenate(rows, axis=0) if n_chunk > 1 else rows[0]
    return u, tu, ta, n, rs, v, mixed


def _gm_fwd(z, gln, bln, wc, bst, name):
    rows = z.shape[0]
    tr = _pick(rows, 1024, GM_CHUNK)
    n_chunk = tr // GM_CHUNK

    def body(zu_ref, zv_ref, gln_ref, bln_ref, wc_ref, bst_ref, o_ref):
        u, _, _, _, _, _, mixed = _gm_forward_rows(zu_ref[...].astype(F32), zv_ref[...].astype(F32), gln_ref[...], bln_ref[...], wc_ref,
                                                   bst_ref, n_chunk)
        o_ref[...] = (u * mixed).astype(BF16)

    return pl.pallas_call(
        body, name=name, grid=(rows // tr,),
        in_specs=[_rowspec(tr, GM_WIDTH, Z_GM // GM_WIDTH), _rowspec(tr, GM_WIDTH, Z_GM // GM_WIDTH + 1),_fullspec((1, GM_WIDTH)), _fullspec((1, GM_WIDTH)),
                  _fullspec((GM_GROUPS, GM_CHUNK, GM_CHUNK)), _fullspec((GM_GROUPS, GM_CHUNK, LANES))],
        out_specs=_rowspec(tr, GM_WIDTH), out_shape=jax.ShapeDtypeStruct((rows, GM_WIDTH), BF16),
        compiler_params=_params(("parallel",)),
    )(z, z, gln, bln, wc, bst)


ANY_SPEC = pl.BlockSpec(memory_space=pl.ANY)


def _gm_bwd(z, dy, gln, bln, wc, wct, bst, dz, name):
    rows = z.shape[0]
    tr = _pick(rows, 1024, GM_CHUNK)
    n_chunk = tr // GM_CHUNK

    def body(zu_ref, zv_ref, dy_ref, gln_ref, bln_ref, wc_ref, wct_ref, bst_ref, _, dz_ref, dws_ref, dbs_ref, dgl_ref,
             dbl_ref):
        first = pl.program_id(0) == 0
        zu, zv, gln = zu_ref[...].astype(F32), zv_ref[...].astype(F32), gln_ref[...]
        u, tu, ta, n, rs, v, mixed = _gm_forward_rows(zu, zv, gln, bln_ref[...], wc_ref, bst_ref, n_chunk)
        dyv = dy_ref[...].astype(F32)
        dzu = dyv * mixed * _gelu_grad(zu, tu)
        dmix = dyv * u
        dmb = dmix.astype(BF16)
        vb = v.astype(BF16)
        dv_rows, dws, dbs = [], [None] * GM_GROUPS, None
        for c in range(n_chunk):
            rsl = slice(c * GM_CHUNK, (c + 1) * GM_CHUNK)
            cols = []
            for g in range(GM_GROUPS):
                csl = slice(g * LANES, (g + 1) * LANES)
                dmc = dmb[rsl, csl]
                cols.append(jnp.dot(wct_ref[g], dmc, preferred_element_type=F32))
                w_part = lax.dot_general(dmc, vb[rsl, csl], (((1,), (1,)), ((), ())), preferred_element_type=F32)
                dws[g] = w_part if dws[g] is None else dws[g] + w_part
            dv_rows.append(jnp.concatenate(cols, axis=1))
            dbs = dmix[rsl, :] if dbs is None else dbs + dmix[rsl, :]
        dv = jnp.concatenate(dv_rows, axis=0) if n_chunk > 1 else dv_rows[0]
        dn = dv * gln
        da = rs * (dn - jnp.mean(dn, axis=-1, keepdims=True) - n * jnp.mean(dn * n, axis=-1, keepdims=True))
        dzv = da * _gelu_grad(zv, ta)
        dz_ref[:, 0:GM_WIDTH] = dzu.astype(BF16)
        dz_ref[:, GM_WIDTH:2 * GM_WIDTH] = dzv.astype(BF16)
        _acc_rows(dgl_ref, dv * n, first)
        _acc_rows(dbl_ref, dv, first)

        @pl.when(first)
        def _():
            for g in range(GM_GROUPS):
                dws_ref[g] = dws[g]
            dbs_ref[...] = dbs

        @pl.when(jnp.logical_not(first))
        def _():
            for g in range(GM_GROUPS):
                dws_ref[g] += dws[g]
            dbs_ref[...] += dbs

    wspec = _fullspec((GM_GROUPS, GM_CHUNK, GM_CHUNK))
    return pl.pallas_call(
        body, name=name, grid=(rows // tr,),
        in_specs=[_rowspec(tr, GM_WIDTH, Z_GM // GM_WIDTH), _rowspec(tr, GM_WIDTH, Z_GM // GM_WIDTH + 1),
                  _rowspec(tr, GM_WIDTH), _fullspec((1, GM_WIDTH)), _fullspec((1, GM_WIDTH)), wspec, wspec, wspec, ANY_SPEC],
        out_specs=[_rowspec(tr, 2 * GM_WIDTH, Z_GM // (2 * GM_WIDTH)), wspec, _fullspec((GM_CHUNK, GM_WIDTH)),
                   _fullspec((1, GM_WIDTH)), _fullspec((1, GM_WIDTH))],
        out_shape=[jax.ShapeDtypeStruct(dz.shape, dz.dtype), jax.ShapeDtypeStruct((GM_GROUPS, GM_CHUNK, GM_CHUNK), F32),
                   jax.ShapeDtypeStruct((GM_CHUNK, GM_WIDTH), F32), jax.ShapeDtypeStruct((1, GM_WIDTH), F32),
                   jax.ShapeDtypeStruct((1, GM_WIDTH), F32)],
        input_output_aliases={8: 0}, compiler_params=_params(("arbitrary",)),
    )(z, z, dy, gln, bln, wc, wct, bst, dz)


def _lat_fwd(z, g_cq, g_ckv, name):
    rows = z.shape[0]
    tr = _row_tile(rows, 4 * MLA_W)

    def body(z_ref, gq_ref, gkv_ref, nq_ref, nkv_ref):
        zb = z_ref[...]
        cq, ckv = zb[:, 0:Q_LORA], zb[:, Q_LORA:Q_LORA + KV_LORA]
        nq_ref[...] = (cq * _rms(cq, Q_LORA) * gq_ref[...]).astype(BF16)
        nkv_ref[...] = (ckv * _rms(ckv, KV_LORA) * gkv_ref[...]).astype(BF16)

    return pl.pallas_call(
        body, name=name, grid=(rows // tr,),
        in_specs=[_rowspec(tr, MLA_W, Z_MLA // MLA_W), _fullspec((1, Q_LORA)), _fullspec((1, KV_LORA))],
        out_specs=[_rowspec(tr, Q_LORA), _rowspec(tr, KV_LORA)],
        out_shape=[jax.ShapeDtypeStruct((rows, Q_LORA), BF16), jax.ShapeDtypeStruct((rows, KV_LORA), BF16)],
        compiler_params=_params(("parallel",)),
    )(z, g_cq, g_ckv)


def _lat_bwd(z, dnq, dnkv, dkpe, g_cq, g_ckv, dz, name):
    rows = z.shape[0]
    tr = _row_tile(rows, 8 * MLA_W)

    def body(z_ref, dnq_ref, dnkv_ref, dkpe_ref, gq_ref, gkv_ref, _, dz_ref, dgq_ref, dgkv_ref):
        first = pl.program_id(0) == 0
        zb = z_ref[...]
        dcq, dgq = _rms_bwd_rows(zb[:, 0:Q_LORA], gq_ref[...], dnq_ref[...], Q_LORA)
        dckv, dgkv = _rms_bwd_rows(zb[:, Q_LORA:Q_LORA + KV_LORA], gkv_ref[...], dnkv_ref[...], KV_LORA)
        dz_ref[:, 0:Q_LORA] = dcq.astype(BF16)
        dz_ref[:, Q_LORA:Q_LORA + KV_LORA] = dckv.astype(BF16)
        dz_ref[:, Q_LORA + KV_LORA:MLA_W] = dkpe_ref[...].astype(BF16)
        _acc_rows(dgq_ref, dgq, first)
        _acc_rows(dgkv_ref, dgkv, first)

    return pl.pallas_call(
        body, name=name, grid=(rows // tr,),
        in_specs=[_rowspec(tr, MLA_W, Z_MLA // MLA_W), _rowspec(tr, Q_LORA), _rowspec(tr, KV_LORA), _rowspec(tr, LANES),
                  _fullspec((1, Q_LORA)), _fullspec((1, KV_LORA)), ANY_SPEC],
        out_specs=[_rowspec(tr, MLA_W, Z_MLA // MLA_W), _fullspec((1, Q_LORA)), _fullspec((1, KV_LORA))],
        out_shape=[jax.ShapeDtypeStruct(dz.shape, dz.dtype), jax.ShapeDtypeStruct((1, Q_LORA), F32),
                   jax.ShapeDtypeStruct((1, KV_LORA), F32)],
        input_output_aliases={6: 0}, compiler_params=_params(("arbitrary",)),
    )(z, dnq, dnkv, dkpe, g_cq, g_ckv, dz)


def _rope(y, cc, ss):
    return y * cc + pltpu.roll(y, 64, 1) * ss


def _rope_bwd(d, cc, ss):
    return d * cc + pltpu.roll(d * ss, 64, 1)


def _qk_fwd(q, kv, z, cc, ss, gqn, gqp, gkn, gkp, name):
    rows = q.shape[0]
    W = MLA_HEADS * HEAD
    tr = _row_tile(rows, 20 * W)
    QS = MLA_SCALE * LOG2E

    def body(q_ref, kv_ref, kpe_ref, cc_ref, ss_ref, gqn_ref, gqp_ref, gkn_ref, gkp_ref, qc_ref, kc_ref, v_ref):
        cc, ss = cc_ref[...], ss_ref[...]
        kpe = kpe_ref[...]
        kp = _rope(kpe * _rms(kpe, MLA_ROPE) * gkp_ref[...], cc, ss).astype(BF16)
        for h in range(MLA_HEADS):
            qn = q_ref[:, h * HEAD:(h + 1) * HEAD]
            qp = q_ref[:, W + h * HEAD:W + (h + 1) * HEAD]
            kn = kv_ref[:, h * HEAD:(h + 1) * HEAD]
            qc_ref[:, h * QCAT:h * QCAT + HEAD] = (qn * _rms(qn, HEAD) * gqn_ref[...] * QS).astype(BF16)
            qc_ref[:, h * QCAT + HEAD:(h + 1) * QCAT] = (_rope(qp * _rms(qp, MLA_ROPE) * gqp_ref[...], cc, ss) * QS).astype(BF16)
            kc_ref[:, h * QCAT:h * QCAT + HEAD] = (kn * _rms(kn, HEAD) * gkn_ref[...]).astype(BF16)
            kc_ref[:, h * QCAT + HEAD:(h + 1) * QCAT] = kp
        v_ref[...] = kv_ref[:, W:2 * W].astype(BF16)

    g = _fullspec((1, HEAD))
    return _call(
        body, None, [q, kv, z, cc, ss, gqn, gqp, gkn, gkp], name=name, grid=(rows // tr,),
        in_specs=[_rowspec(tr, 2 * W), _rowspec(tr, 2 * W), _rowspec(tr, LANES, Z_KPE // LANES), _rowspec(tr, LANES),
                  _rowspec(tr, LANES), g, g, g, g],
        out_specs=[_rowspec(tr, MLA_HEADS * QCAT), _rowspec(tr, MLA_HEADS * QCAT), _rowspec(tr, W)],
        out_shape=[jax.ShapeDtypeStruct((rows, MLA_HEADS * QCAT), BF16), jax.ShapeDtypeStruct((rows, MLA_HEADS * QCAT), BF16),
                   jax.ShapeDtypeStruct((rows, W), BF16)],
        scratch_shapes=[], sem=("parallel",))


def _qk_bwd(q, kv, z, cc, ss, gqn, gqp, gkn, gkp, dqc, dkc, dv, name):
    rows = q.shape[0]
    W = MLA_HEADS * HEAD
    tr = _row_tile(rows, 24 * W)

    def body(q_ref, kv_ref, kpe_ref, cc_ref, ss_ref, gqn_ref, gqp_ref, gkn_ref, gkp_ref, dqc_ref, dkc_ref, dv_ref,
             dq_ref, dkv_ref, dkpe_ref, dgqn_ref, dgqp_ref, dgkn_ref, dgkp_ref):
        first = pl.program_id(0) == 0
        cc, ss = cc_ref[...], ss_ref[...]
        sqn = sqp = skn = dkp = None
        for h in range(MLA_HEADS):
            dx, dg = _rms_bwd_rows(q_ref[:, h * HEAD:(h + 1) * HEAD], gqn_ref[...], dqc_ref[:, h * QCAT:h * QCAT + HEAD], HEAD)
            dq_ref[:, h * HEAD:(h + 1) * HEAD] = dx.astype(BF16)
            sqn = dg if sqn is None else sqn + dg
            dy = _rope_bwd(dqc_ref[:, h * QCAT + HEAD:(h + 1) * QCAT], cc, ss)
            dx, dg = _rms_bwd_rows(q_ref[:, W + h * HEAD:W + (h + 1) * HEAD], gqp_ref[...], dy, MLA_ROPE)
            dq_ref[:, W + h * HEAD:W + (h + 1) * HEAD] = dx.astype(BF16)
            sqp = dg if sqp is None else sqp + dg
            dx, dg = _rms_bwd_rows(kv_ref[:, h * HEAD:(h + 1) * HEAD], gkn_ref[...], dkc_ref[:, h * QCAT:h * QCAT + HEAD], HEAD)
            dkv_ref[:, h * HEAD:(h + 1) * HEAD] = dx.astype(BF16)
            skn = dg if skn is None else skn + dg
            part = dkc_ref[:, h * QCAT + HEAD:(h + 1) * QCAT].astype(F32)
            dkp = part if dkp is None else dkp + part
        dkv_ref[:, W:2 * W] = dv_ref[...].astype(BF16)
        dx, dg = _rms_bwd_rows(kpe_ref[...], gkp_ref[...], _rope_bwd(dkp, cc, ss), MLA_ROPE)
        dkpe_ref[...] = dx
        _acc_rows(dgqn_ref, sqn, first)
        _acc_rows(dgqp_ref, sqp, first)
        _acc_rows(dgkn_ref, skn, first)
        _acc_rows(dgkp_ref, dg, first)

    g = _fullspec((1, HEAD))
    gs = jax.ShapeDtypeStruct((1, HEAD), F32)
    return pl.pallas_call(
        body, name=name, grid=(rows // tr,),
        in_specs=[_rowspec(tr, 2 * W), _rowspec(tr, 2 * W), _rowspec(tr, LANES, Z_KPE // LANES), _rowspec(tr, LANES),
                  _rowspec(tr, LANES), g, g, g, g, _rowspec(tr, MLA_HEADS * QCAT), _rowspec(tr, MLA_HEADS * QCAT),
                  _rowspec(tr, W)],
        out_specs=[_rowspec(tr, 2 * W), _rowspec(tr, 2 * W), _rowspec(tr, LANES), g, g, g, g],
        out_shape=[jax.ShapeDtypeStruct((rows, 2 * W), BF16), jax.ShapeDtypeStruct((rows, 2 * W), BF16),
                   jax.ShapeDtypeStruct((rows, LANES), F32), gs, gs, gs, gs],
        compiler_params=_params(("arbitrary",)),
    )(q, kv, z, cc, ss, gqn, gqp, gkn, gkp, dqc, dkc, dv)


def _headnorm_fwd(x, col, nheads, g, out_scale, name):
    rows = x.shape[0]
    W = nheads * HEAD
    tr = _row_tile(rows, 6 * W)

    def body(x_ref, g_ref, o_ref):
        for h in range(nheads):
            xv = x_ref[:, h * HEAD:(h + 1) * HEAD]
            o_ref[:, h * HEAD:(h + 1) * HEAD] = (xv * _rms(xv, HEAD) * g_ref[...] * out_scale).astype(BF16)

    return pl.pallas_call(
        body, name=name, grid=(rows // tr,),
        in_specs=[_rowspec(tr, W, col), _fullspec((1, HEAD))], out_specs=_rowspec(tr, W),
        out_shape=jax.ShapeDtypeStruct((rows, W), BF16), compiler_params=_params(("parallel",)),
    )(x, g)


def _headnorm_bwd(x, col, nheads, g, dy, tail, name, into=None):
    rows = x.shape[0]
    W = nheads * HEAD
    tr = _row_tile(rows, 12 * W)
    has_tail = tail is not None
    WO = 2 * W if has_tail else W

    def body(*refs):
        if into is not None:
            x_ref, g_ref, dy_ref, _, dx_ref, dg_ref = refs
        elif has_tail:
            x_ref, g_ref, dy_ref, t_ref, dx_ref, dg_ref = refs
        else:
            x_ref, g_ref, dy_ref, dx_ref, dg_ref = refs
        acc = None
        for h in range(nheads):
            sl = slice(h * HEAD, (h + 1) * HEAD)
            dx, dg = _rms_bwd_rows(x_ref[:, sl], g_ref[...], dy_ref[:, sl], HEAD)
            dx_ref[:, sl] = dx.astype(BF16)
            acc = dg if acc is None else acc + dg
        if has_tail:
            dx_ref[:, W:2 * W] = t_ref[...].astype(BF16)
        _acc_rows(dg_ref, acc, pl.program_id(0) == 0)

    ins = [x, g, dy] + ([tail] if has_tail else [])
    specs = [_rowspec(tr, W, col), _fullspec((1, HEAD)), _rowspec(tr, W)] + ([_rowspec(tr, W)] if has_tail else [])
    dx_spec, dx_shape, aliases = _rowspec(tr, WO), jax.ShapeDtypeStruct((rows, WO), BF16), {}
    if into is not None:
        assert not has_tail
        ins, specs = ins + [into[0]], specs + [ANY_SPEC]
        dx_spec, dx_shape, aliases = _rowspec(tr, W, into[1]), jax.ShapeDtypeStruct(into[0].shape, into[0].dtype), {3: 0}
    return pl.pallas_call(
        body, name=name, grid=(rows // tr,), in_specs=specs,
        out_specs=[dx_spec, _fullspec((1, HEAD))], out_shape=[dx_shape, jax.ShapeDtypeStruct((1, HEAD), F32)],
        input_output_aliases=aliases, compiler_params=_params(("arbitrary",)),
    )(*ins)


def _sigmoid(x):
    return 1.0 / (1.0 + jnp.exp(-x.astype(F32)))


def _merge_fwd(z, y_gm, y_mla, y_mem, name):
    rows = z.shape[0]
    tr = _row_tile(rows, 14 * D_MODEL)

    def body(g0_ref, g1_ref, g2_ref, a_ref, b_ref, c_ref, o_ref):
        m = _sigmoid(g0_ref[...]) * a_ref[...] + _sigmoid(g1_ref[...]) * b_ref[...] + _sigmoid(g2_ref[...]) * c_ref[...]
        o_ref[...] = m.astype(BF16)

    r = _rowspec(tr, D_MODEL)
    return pl.pallas_call(
        body, name=name, grid=(rows // tr,),
        in_specs=[_rowspec(tr, D_MODEL, 0), _rowspec(tr, D_MODEL, 1), _rowspec(tr, D_MODEL, 2),r, r, r],
        out_specs=r, out_shape=jax.ShapeDtypeStruct((rows, D_MODEL), BF16), compiler_params=_params(("parallel",)),
    )(z, z, z, y_gm, y_mla, y_mem)


def _merge_bwd(z, y_gm, y_mla, y_mem, dm, name):
    rows = z.shape[0]
    tr = _row_tile(rows, 24 * D_MODEL)

    def body(g0_ref, g1_ref, g2_ref, a_ref, b_ref, c_ref, dm_ref, da_ref, db_ref, dc_ref, dzg_ref):
        dmv = dm_ref[...].astype(F32)
        for k, (g_ref, y_ref, dy_ref) in enumerate(((g0_ref, a_ref, da_ref), (g1_ref, b_ref, db_ref), (g2_ref, c_ref, dc_ref))):
            s = _sigmoid(g_ref[...])
            dy_ref[...] = (dmv * s).astype(BF16)
            dzg_ref[:, k * D_MODEL:(k + 1) * D_MODEL] = (dmv * y_ref[...] * s * (1.0 - s)).astype(BF16)

    r = _rowspec(tr, D_MODEL)
    o = jax.ShapeDtypeStruct((rows, D_MODEL), BF16)
    return pl.pallas_call(
        body, name=name, grid=(rows // tr,),
        in_specs=[_rowspec(tr, D_MODEL, 0), _rowspec(tr, D_MODEL, 1), _rowspec(tr, D_MODEL, 2),r, r, r, r],
        out_specs=[r, r, r, _rowspec(tr, 3 * D_MODEL, 0)],
        out_shape=[o, o, o, jax.ShapeDtypeStruct((rows, Z_COLS), BF16)],
        compiler_params=_params(("parallel",)),
    )(z, z, z, y_gm, y_mla, y_mem, dm)


_NT = (((1,), (1,)), ((), ()))
_TN = (((0,), (0,)), ((), ()))


def _diag_mask(s):
    row = lax.broadcasted_iota(jnp.int32, s.shape, 0)
    col = lax.broadcasted_iota(jnp.int32, s.shape, 1)
    return jnp.where(row >= col, s, NEG)


def _attn_fwd(q, k, v, nb, nheads, dk, v_col0, causal, name, rider=None):
    S, Skv = q.shape[0] // nb, k.shape[0] // nb
    tq = _pick(Skv, ATT_TILE) if causal else _pick(S, 4 * ATT_TILE)
    nq = S // tq

    def body(q_ref, k_ref, v_ref, o_ref, lse_ref):
        for i in range(nq):
            r0 = i * tq
            qb = q_ref[r0:r0 + tq, :]
            if causal:
                spans = ([(0, r0, False)] if i > 0 else []) + [(r0, r0 + tq, True)]
            else:
                spans = [(0, Skv, False)]
            scores = []
            for a, b, masked in spans:
                s = lax.dot_general(qb, k_ref[a:b, :], _NT, preferred_element_type=F32)
                scores.append(_diag_mask(s) if masked else s)
            m = functools.reduce(jnp.maximum, [jnp.max(s, axis=-1, keepdims=True) for s in scores])
            l = acc = None
            for s, (a, b, _) in zip(scores, spans):
                p = jnp.exp2(s - m)
                lp = jnp.sum(p, axis=-1, keepdims=True)
                ap = jnp.dot(p.astype(BF16), v_ref[a:b, :].astype(BF16), preferred_element_type=F32)
                l, acc = (lp, ap) if l is None else (l + lp, acc + ap)
            o_ref[r0:r0 + tq, :] = (acc / l).astype(BF16)
            lse_ref[r0:r0 + tq, :] = m + jnp.log2(l)

    ins = [q, k, v]
    in_specs = [pl.BlockSpec((S, dk), lambda b, h: (b, h)), pl.BlockSpec((Skv, dk), lambda b, h: (b, h)),
                pl.BlockSpec((Skv, HEAD), lambda b, h: (b, v_col0 + h))]
    out_specs = [pl.BlockSpec((S, HEAD), lambda b, h: (b, h)), pl.BlockSpec((None, S, 1), lambda b, h: (h, b, 0))]
    out_shape = [jax.ShapeDtypeStruct((nb * S, nheads * HEAD), BF16), jax.ShapeDtypeStruct((nheads, nb * S, 1), F32)]
    return _call(body, rider, ins, name=name, grid=(nb, nheads), in_specs=in_specs, out_specs=out_specs,
                 out_shape=out_shape, scratch_shapes=[], sem=("parallel", "parallel"))


def _attn_bwd(q, k, v, o, do, lse, nb, nheads, dk, v_col0, scale, causal, name, rider=None):
    S, Skv = q.shape[0] // nb, k.shape[0] // nb
    tk = _pick(Skv, ATT_TILE)
    nkv = Skv // tk

    def body(q_ref, k_ref, v_ref, o_ref, do_ref, lse_ref, dq_ref, dk_ref, dv_ref, delta_ref, dob_ref, dqa_ref):
        dov = do_ref[...]
        delta_ref[...] = jnp.sum(o_ref[...].astype(F32) * dov.astype(F32), axis=-1, keepdims=True)
        dob_ref[...] = dov.astype(BF16)

        for j in range(nkv):
            c0 = j * tk
            kb = k_ref[c0:c0 + tk, :]
            vb = v_ref[c0:c0 + tk, :].astype(BF16)
            if causal:
                spans = [(c0, c0 + tk, True)] + ([(c0 + tk, S, False)] if c0 + tk < S else [])
            else:
                spans = [(0, S, False)]
            dk_acc = dv_acc = None
            for a, b, masked in spans:
                qb = q_ref[a:b, :]
                dob = dob_ref[a:b, :]
                s = lax.dot_general(qb, kb, _NT, preferred_element_type=F32)
                if masked:
                    s = _diag_mask(s)
                p = jnp.exp2(s - lse_ref[a:b, :])
                dp = lax.dot_general(dob, vb, _NT, preferred_element_type=F32)
                ds = (p * (dp - delta_ref[a:b, :])).astype(BF16)
                dv_p = lax.dot_general(p.astype(BF16), dob, _TN, preferred_element_type=F32)
                dk_p = lax.dot_general(ds, qb, _TN, preferred_element_type=F32)
                dk_acc, dv_acc = (dk_p, dv_p) if dk_acc is None else (dk_acc + dk_p, dv_acc + dv_p)
                dq_p = jnp.dot(ds, kb, preferred_element_type=F32) * scale
                if j == 0:
                    dqa_ref[a:b, :] = dq_p
                else:
                    dqa_ref[a:b, :] += dq_p
            dk_ref[c0:c0 + tk, :] = (dk_acc * LN2).astype(BF16)
            dv_ref[c0:c0 + tk, :] = dv_acc.astype(BF16)
        dq_ref[...] = dqa_ref[...].astype(BF16)

    ins = [q, k, v, o, do, lse]
    in_specs = [pl.BlockSpec((S, dk), lambda b, h: (b, h)), pl.BlockSpec((Skv, dk), lambda b, h: (b, h)),
                pl.BlockSpec((Skv, HEAD), lambda b, h: (b, v_col0 + h)), pl.BlockSpec((S, HEAD), lambda b, h: (b, h)),
                pl.BlockSpec((S, HEAD), lambda b, h: (b, h)), pl.BlockSpec((None, S, 1), lambda b, h: (h, b, 0))]
    out_specs = [pl.BlockSpec((S, dk), lambda b, h: (b, h)), pl.BlockSpec((Skv, dk), lambda b, h: (b, h)),
                 pl.BlockSpec((Skv, HEAD), lambda b, h: (b, h))]
    out_shape = [jax.ShapeDtypeStruct((nb * S, nheads * dk), BF16), jax.ShapeDtypeStruct((nb * Skv, nheads * dk), BF16),
                 jax.ShapeDtypeStruct((nb * Skv, nheads * HEAD), BF16)]
    return _call(body, rider, ins, name=name, grid=(nb, nheads), in_specs=in_specs, out_specs=out_specs,
                 out_shape=out_shape,
                 scratch_shapes=[pltpu.VMEM((S, 1), F32), pltpu.VMEM((S, HEAD), BF16), pltpu.VMEM((S, dk), F32)],
                 sem=("parallel", "parallel"))


def _spread_rope(a):
    zero = jnp.zeros(a.shape[:-1] + (32,), a.dtype)
    return jnp.concatenate([a[..., :32], zero, a[..., 32:], zero], axis=-1)


def _gather_rope(a):
    return jnp.concatenate([a[..., 0:32], a[..., 64:96]], axis=-1)


def _win_layout(w):
    return jnp.concatenate([w[:, C_ZG:C_END], w[:, C_ZU:C_CQ], w[:, C_QM:C_ZG], w[:, C_CQ:C_CKV], w[:, C_CKV:C_KPE],
                            _spread_rope(w[:, C_KPE:C_QM])], axis=1)


def _win_unlayout(d):
    return jnp.concatenate([d[:, Z_GM:Z_QM], d[:, Z_MLA:Z_MLA + Q_LORA], d[:, Z_MLA + Q_LORA:Z_KPE],
                            _gather_rope(d[:, Z_KPE:Z_COLS]), d[:, Z_QM:Z_MLA], d[:, 0:Z_GM]], axis=1)


def _wuq_layout(w):
    r = w.reshape(Q_LORA, MLA_HEADS, HEAD + MLA_ROPE)
    return jnp.concatenate([r[:, :, :HEAD].reshape(Q_LORA, -1), _spread_rope(r[:, :, HEAD:]).reshape(Q_LORA, -1)], axis=1)


def _wuq_unlayout(d):
    n = d[:, :MLA_HEADS * HEAD].reshape(Q_LORA, MLA_HEADS, HEAD)
    p = _gather_rope(d[:, MLA_HEADS * HEAD:].reshape(Q_LORA, MLA_HEADS, HEAD))
    return jnp.concatenate([n, p], axis=-1).reshape(Q_LORA, -1)


def _wukv_layout(w):
    r = w.reshape(KV_LORA, MLA_HEADS, 2 * HEAD)
    return jnp.concatenate([r[:, :, :HEAD].reshape(KV_LORA, -1), r[:, :, HEAD:].reshape(KV_LORA, -1)], axis=1)


def _wukv_unlayout(d):
    k = d[:, :MLA_HEADS * HEAD].reshape(KV_LORA, MLA_HEADS, HEAD)
    v = d[:, MLA_HEADS * HEAD:].reshape(KV_LORA, MLA_HEADS, HEAD)
    return jnp.concatenate([k, v], axis=-1).reshape(KV_LORA, -1)


AG_MID = ['w_uq', 'w_ukv', 'w_mem_kv', 'w_o_gm', 'w_o_mla', 'w_o_mem', 'w_out']
AG_FFN = ['w_ff1', 'w_ff2']
RS_GROUPS = {'ffn_proj': ['w_ff2', 'w_ff1', 'w_out', 'w_o_gm', 'w_o_mla', 'w_o_mem'],
             'lat': ['w_uq', 'w_ukv', 'w_mem_kv'], 'in_top': ['w_in'], 'in_bot': ['w_in']}


def _unride(res, rider):
    return (res, None) if rider is None else res


def _local_step(x, mem, positions, target, P, ws):
    B, S, _ = x.shape
    M = mem.shape[1]
    T = B * S
    x2d = x.reshape(T, D_MODEL)
    mem2d = mem.reshape(B * M, D_MODEL)
    tgt2d = target.reshape(T, D_MODEL)

    def row(v):
        return v.reshape(1, -1).astype(F32)

    inv_freq = ROPE_BASE ** (-jnp.arange(0, MLA_ROPE, 2, dtype=F32) / MLA_ROPE)
    zero = jnp.zeros_like(inv_freq)
    ang = positions.reshape(T).astype(F32)[:, None] * jnp.concatenate([inv_freq, zero, inv_freq, zero])
    cc = jnp.cos(ang) * jnp.concatenate([zero + 1.0, zero, zero + 1.0, zero])
    ss = jnp.sin(ang) * jnp.concatenate([zero - 1.0, zero, zero + 1.0, zero])

    g_mix, g_cq, g_ckv, g_ffn, g_mem = row(P['g_mix']), row(P['g_cq']), row(P['g_ckv']), row(P['g_ffn']), row(P['g_mem'])
    gqn, gkn, gmq, gmk = row(P['g_q_nope']), row(P['g_k_nope']), row(P['g_mq']), row(P['g_mk'])
    gqp, gkp = _spread_rope(row(P['g_q_pe'])), _spread_rope(row(P['g_k_pe']))
    gln, bln = row(P['g_gm_ln']), row(P['b_gm_ln'])
    wc = jnp.tril(P['w_spatial'].astype(F32))
    wct = jnp.swapaxes(wc, 1, 2).astype(BF16)
    wc = wc.astype(BF16)
    bst = jnp.broadcast_to(P['b_spatial'].astype(F32)[:, :, None], (GM_GROUPS, GM_CHUNK, LANES))

    ride = ws.gather(['w_in'])
    h, got = _unride(_rms_fwd(x2d, g_mix, "rms_mix", rider=ride), ride)
    w_in = _win_layout(ws.gathered(['w_in'], got)['w_in']).astype(BF16)
    ride = ws.gather(AG_MID)
    z, got = _unride(_matmul(h, w_in, 'nn', ACT, "mm_in", tn_t=1792, rider=ride), ride)
    mid = ws.gathered(AG_MID, got)
    w_uq, w_ukv = _wuq_layout(mid['w_uq']).astype(BF16), _wukv_layout(mid['w_ukv']).astype(BF16)
    w_mem_kv, w_o_gm, w_o_mla, w_o_mem, w_out = (mid[n] for n in ('w_mem_kv', 'w_o_gm', 'w_o_mla', 'w_o_mem', 'w_out'))
    ygm_pre = _gm_fwd(z, gln, bln, wc, bst, "gm_fwd")
    y_gm = _matmul(ygm_pre, w_o_gm, 'nn', ACT, "mm_o_gm")
    nq, nkv = _lat_fwd(z, g_cq, g_ckv, "lat_fwd")
    q = _matmul(nq, w_uq, 'nn', ACT, "mm_uq")
    kv = _matmul(nkv, w_ukv, 'nn', ACT, "mm_ukv")
    qcat, kcat, vv = _qk_fwd(q, kv, z, cc, ss, gqn, gqp, gkn, gkp, "qk_fwd")
    ride = ws.gather(AG_FFN)
    (o, lse), got = _unride(_attn_fwd(qcat, kcat, vv, B, MLA_HEADS, QCAT, 0, True, "mla_attn_fwd", rider=ride), ride)
    ffn = ws.gathered(AG_FFN, got)
    w_ff1, w_ff2 = ffn['w_ff1'], ffn['w_ff2']
    y_mla = _matmul(o, w_o_mla, 'nn', ACT, "mm_o_mla")
    nm = _rms_fwd(mem2d, g_mem, "rms_mem")
    kvm = _matmul(nm, w_mem_kv, 'nn', ACT, "mm_mem_kv")
    qm = _headnorm_fwd(z, Z_QM // (MEM_HEADS * HEAD), MEM_HEADS, gmq, MEM_SCALE * LOG2E, "memq_fwd")
    km = _headnorm_fwd(kvm, 0, MEM_HEADS, gmk, 1.0, "memk_fwd")
    om, lse_m = _attn_fwd(qm, km, kvm, B, MEM_HEADS, HEAD, MEM_HEADS, False, "mem_attn_fwd")
    y_mem = _matmul(om, w_o_mem, 'nn', ACT, "mm_o_mem")
    merged = _merge_fwd(z, y_gm, y_mla, y_mem, "merge_fwd")
    x1, h2 = _matmul(merged, w_out, 'nn', F32, "mm_out", add=x2d, rms_gain=g_ffn, tm_t=1024, tn_t=D_MODEL)
    a1 = _matmul(h2, w_ff1, 'nn', BF16, "mm_ff1")
    dx2, dx2b, loss_part = _matmul(a1, w_ff2, 'nn', F32, "mm_ff2", add=x1, relu2_a=True, sq_err_target=tgt2d,
                                   tm_t=512, tn_t=D_MODEL)

    G = {}
    d_ff2 = _matmul(a1, dx2b, 'tn', BF16, "mm_d_ff2", relu2_a=True)
    da1 = _matmul(dx2b, w_ff2, 'nt', BF16, "mm_da1", relu2_grad=a1)
    d_ff1 = _matmul(h2, da1, 'tn', BF16, "mm_d_ff1", col_shards=N_DEV)
    dh2 = _matmul(da1, w_ff1, 'nt', ACT, "mm_dh2")
    dx1, dx1b, G['g_ffn'] = _rms_bwd(x1, g_ffn, dh2, dx2, "rms_ffn_bwd", dx_dtypes=(F32, BF16))
    d_out = _matmul(merged, dx1b, 'tn', BF16, "mm_d_out")
    dmerged = _matmul(dx1b, w_out, 'nt', ACT, "mm_dmerged")
    dy_gm, dy_mla, dy_mem, dz = _merge_bwd(z, y_gm, y_mla, y_mem, dmerged, "merge_bwd")
    d_o_gm = _matmul(ygm_pre, dy_gm, 'tn', BF16, "mm_d_o_gm")
    d_o_mla = _matmul(o, dy_mla, 'tn', BF16, "mm_d_o_mla")
    d_o_mem = _matmul(om, dy_mem, 'tn', BF16, "mm_d_o_mem")
    dygm_pre = _matmul(dy_gm, w_o_gm, 'nt', ACT, "mm_dygm")
    dz, dws, dbs, G['g_gm_ln'], G['b_gm_ln'] = _gm_bwd(z, dygm_pre, gln, bln, wc, wct, bst, dz, "gm_bwd")
    G['w_spatial'] = jnp.tril(dws)
    G['b_spatial'] = jnp.sum(dbs.reshape(GM_CHUNK, GM_GROUPS, LANES), axis=-1).T
    do = _matmul(dy_mla, w_o_mla, 'nt', ACT, "mm_do")
    ride = ws.scatter('ffn_proj', {'w_ff2': d_ff2, 'w_ff1': d_ff1, 'w_out': d_out, 'w_o_gm': d_o_gm, 'w_o_mla': d_o_mla,
                                   'w_o_mem': d_o_mem})
    (dqc, dkc, dvv), got = _unride(_attn_bwd(qcat, kcat, vv, o, do, lse, B, MLA_HEADS, QCAT, 0, MLA_SCALE, True,
                                             "mla_attn_bwd", rider=ride), ride)
    ws.scattered('ffn_proj', got)
    dq, dkv, dkpe, G['g_q_nope'], dgqp, G['g_k_nope'], dgkp = _qk_bwd(q, kv, z, cc, ss, gqn, gqp, gkn, gkp, dqc, dkc, dvv,
                                                                     "qk_bwd")
    G['g_q_pe'], G['g_k_pe'] = _gather_rope(dgqp), _gather_rope(dgkp)
    d_uq = _wuq_unlayout(_matmul(nq, dq, 'tn', BF16, "mm_d_uq"))
    dnq = _matmul(dq, w_uq, 'nt', ACT, "mm_dnq")
    d_ukv = _wukv_unlayout(_matmul(nkv, dkv, 'tn', BF16, "mm_d_ukv"))
    dnkv = _matmul(dkv, w_ukv, 'nt', ACT, "mm_dnkv")
    dz, G['g_cq'], G['g_ckv'] = _lat_bwd(z, dnq, dnkv, dkpe, g_cq, g_ckv, dz, "lat_bwd")
    dom = _matmul(dy_mem, w_o_mem, 'nt', ACT, "mm_dom")
    dqm, dkm, dvm = _attn_bwd(qm, km, kvm, om, dom, lse_m, B, MEM_HEADS, HEAD, MEM_HEADS, MEM_SCALE, False, "mem_attn_bwd")
    dz, G['g_mq'] = _headnorm_bwd(z, Z_QM // (MEM_HEADS * HEAD), MEM_HEADS, gmq, dqm, None, "memq_bwd",
                                  into=(dz, Z_QM // (MEM_HEADS * HEAD)))
    dkvm, G['g_mk'] = _headnorm_bwd(kvm, 0, MEM_HEADS, gmk, dkm, dvm, "memk_bwd")
    d_mem_kv = _matmul(nm, dkvm, 'tn', BF16, "mm_d_mem_kv")
    dnm = _matmul(dkvm, w_mem_kv, 'nt', ACT, "mm_dnm")
    G['g_mem'], = _rms_bwd(mem2d, g_mem, dnm, None, "rms_mem_bwd", dx_dtypes=())
    half = D_MODEL // 2
    ride = ws.scatter('lat', {'w_uq': d_uq, 'w_ukv': d_ukv, 'w_mem_kv': d_mem_kv})
    d_top, got = _unride(_matmul(h, dz, 'tn', BF16, "mm_d_in_top", tn_t=1792, m_rows=(0, half), rider=ride), ride)
    ws.scattered('lat', got)
    ride = ws.scatter('in_top', {'w_in': _win_unlayout(d_top)})
    d_bot, got = _unride(_matmul(h, dz, 'tn', BF16, "mm_d_in_bot", tn_t=1792, m_rows=(half, half), rider=ride), ride)
    ws.scattered('in_top', got)
    ride = ws.scatter('in_bot', {'w_in': _win_unlayout(d_bot)})
    dh, got = _unride(_matmul(dz, w_in, 'nt', ACT, "mm_dh", rider=ride), ride)
    ws.scattered('in_bot', got)
    gx, G['g_mix'] = _rms_bwd(x2d, g_mix, dh, dx1, "rms_mix_bwd")
    return loss_part, gx.reshape(B, S, D_MODEL), G


def _all_gather8(xs, name):
    def body(x_ref, out_ref, send_sems, recv_sems, local_sem):
        x, y, c = lax.axis_index("x"), lax.axis_index("y"), lax.axis_index("c")
        me, sibling = (x, y, c), (x, y, 1 - c)
        chips = [(1 - x, y), (x, 1 - y), (1 - x, 1 - y)]

        def rows(px, py, pc):
            return out_ref.at[4 * px + 2 * py + pc]

        def copy(k, block, to, src=None):
            return pltpu.make_async_remote_copy(
                src_ref=rows(*block) if src is None else src, dst_ref=rows(*block),
                send_sem=send_sems.at[k], recv_sem=recv_sems.at[k], device_id=to, device_id_type=MESH)

        mine = pltpu.make_async_copy(x_ref, rows(*me), local_sem)
        mine.start()
        first = [copy(0, me, sibling, src=x_ref)]
        first += [copy(1 + j, me, (*chip, c), src=x_ref) for j, chip in enumerate(chips)]
        for cp in first:
            cp.start()
        passed = [copy(4 + j, (*chip, c), sibling) for j, chip in enumerate(chips)]
        for j, chip in enumerate(chips):
            copy(1 + j, (*chip, c), me).wait_recv()
            passed[j].start()
        copy(0, sibling, me).wait_recv()
        for j, chip in enumerate(chips):
            copy(4 + j, (*chip, 1 - c), me).wait_recv()
        for cp in first + passed:
            cp.wait_send()
        mine.wait()

    return pl.pallas_call(
        body, name=name, in_specs=[HBM_SPEC], out_specs=HBM_SPEC,
        out_shape=jax.ShapeDtypeStruct((N_DEV,) + xs.shape, xs.dtype),
        scratch_shapes=[pltpu.SemaphoreType.DMA((7,)), pltpu.SemaphoreType.DMA((7,)), pltpu.SemaphoreType.DMA],
    )(xs)


ADAMW_TILE_ELEMS = 256 * 1024


def _adamw_rows(w, g, m, v):
    m2 = ADAM_B1 * m + (1.0 - ADAM_B1) * g
    v2 = ADAM_B2 * v + (1.0 - ADAM_B2) * (g * g)
    m_hat = m2 / (1.0 - ADAM_B1 ** ADAM_STEP)
    v_hat = v2 / (1.0 - ADAM_B2 ** ADAM_STEP)
    delta = -ADAM_LR * (m_hat / (jnp.sqrt(v_hat) + ADAM_EPS) + ADAM_WD * w)
    return delta, m2, v2


def _sum_adamw(parts, w, m, v, name):
    rows, cols = w.shape
    assert sum(p.shape[1] for p in parts) == rows
    tr = _pick(min(p.shape[1] for p in parts), max(16, ADAMW_TILE_ELEMS // cols), 16)
    n = parts[0].shape[0]
    counts = [p.shape[1] // tr for p in parts]
    starts = [sum(counts[:k]) for k in range(len(parts))]

    def body(*refs):
        p_refs = refs[:len(parts)]
        w_ref, m_ref, v_ref, g_ref, d_ref, m2_ref, v2_ref = refs[len(parts):]
        g = None
        for p_ref, start in zip(p_refs, starts):
            gk = p_ref[0].astype(F32)
            for k in range(1, n):
                gk = gk + p_ref[k].astype(F32)
            g = gk if g is None else jnp.where(pl.program_id(0) >= start, gk, g)
        delta, m2, v2 = _adamw_rows(w_ref[...], g, m_ref[...], v_ref[...])
        g_ref[...] = g
        d_ref[...] = delta
        m2_ref[...] = m2
        v2_ref[...] = v2

    flat = pl.BlockSpec((tr, cols), lambda i: (i, 0))
    out = jax.ShapeDtypeStruct((rows, cols), F32)
    p_specs = [pl.BlockSpec((n, tr, cols), lambda i, s=s, c=c: (0, jnp.clip(i - s, 0, c - 1), 0))
               for s, c in zip(starts, counts)]
    return pl.pallas_call(
        body, name=name, grid=(rows // tr,), in_specs=p_specs + [flat, flat, flat], out_specs=[flat] * 4,
        out_shape=[out] * 4, compiler_params=_params(("parallel",)),
    )(*parts, w, m, v)


SMALL_WIDTH = {'g_mix': 1024, 'g_cq': 384, 'g_ckv': 256, 'g_q_nope': 128, 'g_q_pe': 128, 'g_k_nope': 128, 'g_k_pe': 128,
               'g_gm_ln': 512, 'b_gm_ln': 512, 'g_mem': 1024, 'g_mq': 128, 'g_mk': 128, 'g_ffn': 1024}
NARROW = ('g_q_pe', 'g_k_pe')


def _small_layout():
    layout, r = {}, 0
    for name in SMALL + ['loss']:
        rows = {'w_spatial': GM_GROUPS * GM_CHUNK, 'b_spatial': GM_GROUPS, 'loss': 1}.get(name) or SMALL_WIDTH[name] // LANES
        layout[name] = (r, rows)
        r += -(-rows // 8) * 8
    return layout, r


def _small_pack(grads, loss_part, name):
    layout, total = _small_layout()
    names = SMALL + ['loss']

    def body(*refs):
        out_ref = refs[-1]
        out_ref[...] = jnp.zeros((total, LANES), F32)
        for ref, n in zip(refs[:-1], names):
            r0, rows = layout[n]
            if n == 'w_spatial':
                for g in range(GM_GROUPS):
                    out_ref[r0 + g * GM_CHUNK:r0 + (g + 1) * GM_CHUNK, :] = ref[g]
            elif n == 'b_spatial':
                out_ref[r0:r0 + rows, :] = ref[...]
            else:
                for k in range(rows):
                    out_ref[r0 + k:r0 + k + 1, :] = ref[:, k * LANES:(k + 1) * LANES]

    return pl.pallas_call(body, name=name, out_shape=jax.ShapeDtypeStruct((total, LANES), F32))(
        *[grads[n] for n in SMALL], loss_part)


def _small_adamw(parts, w, m, v, name):
    layout, _ = _small_layout()
    n_dev = parts.shape[0]

    def body(*refs):
        p_ref = refs[0]
        ins = refs[1:1 + 3 * len(SMALL)]
        outs = refs[1 + 3 * len(SMALL):-1]

        def gsum(r0, rows):
            g = p_ref[0, r0:r0 + rows, :]
            for d in range(1, n_dev):
                g = g + p_ref[d, r0:r0 + rows, :]
            return g

        def step(idx, g, at):
            w_ref, m_ref, v_ref = ins[3 * idx:3 * idx + 3]
            delta, m2, v2 = _adamw_rows(w_ref[at], g, m_ref[at], v_ref[at])
            for ref, val in zip(outs[4 * idx:4 * idx + 4], (g, delta, m2, v2)):
                ref[at] = val

        for idx, n in enumerate(SMALL):
            r0, rows = layout[n]
            if n == 'w_spatial':
                for g in range(GM_GROUPS):
                    step(idx, gsum(r0 + g * GM_CHUNK, GM_CHUNK), (0, g))
            elif n == 'b_spatial':
                step(idx, gsum(r0, rows), (0,))
            else:
                for k in range(rows):
                    step(idx, gsum(r0 + k, 1), (slice(None), slice(k * LANES, (k + 1) * LANES)))
        refs[-1][...] = gsum(layout['loss'][0], 8)

    flat_in = [d[n] for n in SMALL for d in (w, m, v)]
    out_shape = [jax.ShapeDtypeStruct(w[n].shape, F32) for n in SMALL for _ in range(4)]
    res = pl.pallas_call(body, name=name, out_shape=out_shape + [jax.ShapeDtypeStruct((8, LANES), F32)])(parts, *flat_in)
    groups = [{n: res[4 * i + j] for i, n in enumerate(SMALL)} for j in range(4)]
    return groups, res[-1]


def _full_from_gathered(gathered, name):
    r, c = BIG_SHAPE[name]
    if BIG_AXIS[name] == 0:
        return gathered.reshape(r, c)
    return gathered.transpose(1, 0, 2).reshape(r, c)


def _shards_of_full(g, name):
    if g.ndim == 3:
        return g
    r, c = BIG_SHAPE[name]
    if BIG_AXIS[name] == 0:
        return g.reshape(N_DEV, r // N_DEV, c)
    return g.reshape(g.shape[0], N_DEV, c // N_DEV).transpose(1, 0, 2)


class _DistWeights:
    def __init__(self, shards):
        self.shards = shards
        self.received = {}

    def gather(self, names):
        return _Gather2([self.shards[n].astype(BF16) for n in names])

    def gathered(self, names, got):
        return {n: _full_from_gathered(g, n) for n, g in zip(names, got)}

    def scatter(self, key, grads):
        return _Exchange([_shards_of_full(grads[n], n) for n in RS_GROUPS[key]], scatter=True)

    def scattered(self, key, got):
        for n, g in zip(RS_GROUPS[key], got):
            self.received.setdefault(n, []).append(g)


def kernel(x, mem, positions, g_mix, w_in, g_cq, w_uq, g_ckv, w_ukv, g_q_nope, g_q_pe, g_k_nope, g_k_pe, g_gm_ln, b_gm_ln, w_spatial, b_spatial, g_mem, w_mem_kv, g_mq, g_mk, w_o_gm, w_o_mla, w_o_mem, w_out, g_ffn, w_ff1, w_ff2, loss_target, m_g_mix, m_w_in, m_g_cq, m_w_uq, m_g_ckv, m_w_ukv, m_g_q_nope, m_g_q_pe, m_g_k_nope, m_g_k_pe, m_g_gm_ln, m_b_gm_ln, m_w_spatial, m_b_spatial, m_g_mem, m_w_mem_kv, m_g_mq, m_g_mk, m_w_o_gm, m_w_o_mla, m_w_o_mem, m_w_out, m_g_ffn, m_w_ff1, m_w_ff2, v_g_mix, v_w_in, v_g_cq, v_w_uq, v_g_ckv, v_w_ukv, v_g_q_nope, v_g_q_pe, v_g_k_nope, v_g_k_pe, v_g_gm_ln, v_b_gm_ln, v_w_spatial, v_b_spatial, v_g_mem, v_w_mem_kv, v_g_mq, v_g_mk, v_w_o_gm, v_w_o_mla, v_w_o_mem, v_w_out, v_g_ffn, v_w_ff1, v_w_ff2):
    given = dict(locals())
    w = {n: given[n][0] for n in WEIGHTS}
    mom = {n: given['m_' + n][0] for n in WEIGHTS}
    var = {n: given['v_' + n][0] for n in WEIGHTS}

    ws = _DistWeights({n: w[n] for n in BIG})
    loss_part, grad_x, G = _local_step(x, mem, positions, loss_target, {n: w[n] for n in SMALL}, ws)

    outs = {}
    for n in BIG:
        for prefix, res in zip(("grad_", "delta_", "new_m_", "new_v_"),
                               _sum_adamw(ws.received[n], w[n], mom[n], var[n], "adamw_" + n)):
            outs[prefix + n] = res[None]

    def widen(d):
        return {n: (jnp.pad(d[n], ((0, 0), (0, LANES - MLA_ROPE))) if n in NARROW else d[n]) for n in SMALL}

    parts = _all_gather8(_small_pack(widen(G), loss_part, "small_pack"), "ag_small")
    small, loss_rows = _small_adamw(parts, *[widen({n: given[prefix + n] for n in SMALL}) for prefix in ("", "m_", "v_")],
                                    "adamw_small")
    loss = 0.5 * jnp.sum(loss_rows) / D_MODEL
    for prefix, group in zip(("grad_", "delta_", "new_m_", "new_v_"), small):
        for n in SMALL:
            outs[prefix + n] = group[n][:, :MLA_ROPE] if n in NARROW else group[n]
    return (loss, grad_x, *[outs[p + n] for p in ("grad_", "delta_", "new_m_", "new_v_") for n in WEIGHTS])
```

```python
import functools
import math

import jax
import jax.numpy as jnp
from jax import lax
from jax.experimental import pallas as pl
from jax.experimental.pallas import tpu as pltpu

F32 = jnp.float32
BF16 = jnp.bfloat16
ACT = BF16

D_MODEL = 1024
MEM_HEADS = 4
HEAD = 128
GM_WIDTH = 512
GM_CHUNK = 128
GM_GROUPS = 4
MLA_HEADS = 8
MLA_ROPE = 64
Q_LORA = 384
KV_LORA = 256
D_FF = 4096
EPS = 1e-6
ROPE_BASE = 10000.0
MLA_SCALE = 1.0 / math.sqrt(HEAD + MLA_ROPE)
MEM_SCALE = 1.0 / math.sqrt(HEAD)
LOG2E = 1.4426950408889634
LN2 = 0.6931471805599453
ATT_TILE = 256
C_ZU, C_ZV, C_CQ, C_CKV, C_KPE, C_QM, C_ZG, C_END = 0, 512, 1024, 1408, 1664, 1728, 2240, 5312
Z_GM, Z_QM, Z_MLA, Z_KPE, Z_COLS = 3072, 4096, 4608, 5248, 5376
MLA_W = 768
QCAT = 2 * HEAD
ADAM_LR, ADAM_B1, ADAM_B2, ADAM_EPS, ADAM_WD, ADAM_STEP = 0.001, 0.9, 0.999, 1e-08, 0.01, 10
N_DEV = 8
LANES = 128
VMEM_LIMIT = 48 * 1024 * 1024
MAX_K_TILE = 8192
NEG = -1e30

BIG = ['w_in', 'w_uq', 'w_ukv', 'w_mem_kv', 'w_o_gm', 'w_o_mla', 'w_o_mem', 'w_out', 'w_ff1', 'w_ff2']
BIG_AXIS = {'w_in': 1, 'w_uq': 1, 'w_ukv': 1, 'w_mem_kv': 0, 'w_o_gm': 1, 'w_o_mla': 0, 'w_o_mem': 1,
            'w_out': 0, 'w_ff1': 1, 'w_ff2': 0}
BIG_SHAPE = {'w_in': (1024, 5312), 'w_uq': (384, 1536), 'w_ukv': (256, 2048), 'w_mem_kv': (1024, 1024),
             'w_o_gm': (512, 1024), 'w_o_mla': (1024, 1024), 'w_o_mem': (512, 1024), 'w_out': (1024, 1024),
             'w_ff1': (1024, 4096), 'w_ff2': (4096, 1024)}
SMALL = ['g_mix', 'g_cq', 'g_ckv', 'g_q_nope', 'g_q_pe', 'g_k_nope', 'g_k_pe', 'g_gm_ln', 'b_gm_ln',
         'w_spatial', 'b_spatial', 'g_mem', 'g_mq', 'g_mk', 'g_ffn']
WEIGHTS = ['g_mix', 'w_in', 'g_cq', 'w_uq', 'g_ckv', 'w_ukv', 'g_q_nope', 'g_q_pe', 'g_k_nope', 'g_k_pe',
           'g_gm_ln', 'b_gm_ln', 'w_spatial', 'b_spatial', 'g_mem', 'w_mem_kv', 'g_mq', 'g_mk', 'w_o_gm',
           'w_o_mla', 'w_o_mem', 'w_out', 'g_ffn', 'w_ff1', 'w_ff2']


def _pick(n, target, mult=LANES):
    best = None
    t = mult
    while t <= min(n, target):
        if n % t == 0:
            best = t
        t += mult
    return best if best is not None else n


def _params(sem):
    return pltpu.CompilerParams(dimension_semantics=sem, vmem_limit_bytes=VMEM_LIMIT)


MESH = pl.DeviceIdType.MESH
HBM_SPEC = pl.BlockSpec(memory_space=pltpu.HBM)


class _Exchange:
    def __init__(self, srcs, scatter):
        self.srcs, self.scatter = list(srcs), scatter
        self.out_shapes = [jax.ShapeDtypeStruct(s.shape if scatter else (N_DEV,) + s.shape, s.dtype) for s in self.srcs]
        n = len(self.srcs)
        self.scratch = [pltpu.SemaphoreType.DMA((n, N_DEV - 1)), pltpu.SemaphoreType.DMA((n, N_DEV - 1)),
                        pltpu.SemaphoreType.DMA((n,))]

    def _copies(self, src_refs, dst_refs, send_sems, recv_sems, local_sems):
        x, y, c = lax.axis_index("x"), lax.axis_index("y"), lax.axis_index("c")
        me = 4 * x + 2 * y + c
        local, remote = [], []
        for a, (src_ref, dst_ref) in enumerate(zip(src_refs, dst_refs)):
            def mine_for(dev, src_ref=src_ref):
                return src_ref.at[dev] if self.scatter else src_ref

            local.append(pltpu.make_async_copy(mine_for(me), dst_ref.at[me], local_sems.at[a]))
            for k in range(1, N_DEV):
                px = 1 - x if k & 4 else x
                py = 1 - y if k & 2 else y
                pc = 1 - c if k & 1 else c
                remote.append(pltpu.make_async_remote_copy(
                    src_ref=mine_for(4 * px + 2 * py + pc), dst_ref=dst_ref.at[me], send_sem=send_sems.at[a, k - 1],
                    recv_sem=recv_sems.at[a, k - 1], device_id=(px, py, pc), device_id_type=MESH))
        return local, remote

    def start(self, *refs):
        local, remote = self._copies(*refs)
        for cp in local + remote:
            cp.start()

    def forward(self, *refs):
        pass

    def finish(self, *refs):
        local, remote = self._copies(*refs)
        for cp in remote + local:
            cp.wait()


class _Gather2:
    def __init__(self, srcs):
        self.srcs = list(srcs)
        self.out_shapes = [jax.ShapeDtypeStruct((N_DEV,) + s.shape, s.dtype) for s in self.srcs]
        n = len(self.srcs)
        self.scratch = [pltpu.SemaphoreType.DMA((n, N_DEV - 1)), pltpu.SemaphoreType.DMA((n, N_DEV - 1)),
                        pltpu.SemaphoreType.DMA((n,))]

    def _plan(self, src_refs, dst_refs, send_sems, recv_sems, local_sems):
        x, y, c = lax.axis_index("x"), lax.axis_index("y"), lax.axis_index("c")
        chips = [(1 - x, y), (x, 1 - y), (1 - x, 1 - y)]
        plans = []
        for a, (src_ref, dst_ref) in enumerate(zip(src_refs, dst_refs)):
            def copy(k, block, to, src=None, a=a, dst_ref=dst_ref):
                at = dst_ref.at[4 * block[0] + 2 * block[1] + block[2]]
                return pltpu.make_async_remote_copy(src_ref=at if src is None else src, dst_ref=at,
                                                    send_sem=send_sems.at[a, k], recv_sem=recv_sems.at[a, k],
                                                    device_id=to, device_id_type=MESH)

            local = pltpu.make_async_copy(src_ref, dst_ref.at[4 * x + 2 * y + c], local_sems.at[a])
            first = [copy(0, (x, y, c), (x, y, 1 - c), src=src_ref)]
            first += [copy(1 + j, (x, y, c), (*chip, c), src=src_ref) for j, chip in enumerate(chips)]
            passed = [copy(4 + j, (*chip, c), (x, y, 1 - c)) for j, chip in enumerate(chips)]
            arrivals = [copy(1 + j, (*chip, c), (x, y, c)) for j, chip in enumerate(chips)]
            late = [copy(0, (x, y, 1 - c), (x, y, c))] + [copy(4 + j, (*chip, 1 - c), (x, y, c)) for j, chip in enumerate(chips)]
            plans.append((local, first, passed, arrivals, late))
        return plans

    def start(self, *refs):
        for local, first, _, _, _ in self._plan(*refs):
            local.start()
            for cp in first:
                cp.start()

    def forward(self, *refs):
        for _, _, passed, arrivals, _ in self._plan(*refs):
            for arrived, onward in zip(arrivals, passed):
                arrived.wait_recv()
                onward.start()

    def finish(self, *refs):
        for local, first, passed, _, late in self._plan(*refs):
            for cp in late:
                cp.wait_recv()
            for cp in first + passed:
                cp.wait_send()
            local.wait()


def _call(body, rider, ins, *, name, grid, in_specs, out_specs, out_shape, scratch_shapes, sem):
    if rider is None:
        return pl.pallas_call(body, name=name, grid=grid, in_specs=in_specs, out_specs=out_specs, out_shape=out_shape,
                              scratch_shapes=scratch_shapes, compiler_params=_params(sem))(*ins)
    single = not isinstance(out_shape, (list, tuple))
    own_specs, own_shapes = ([out_specs], [out_shape]) if single else (list(out_specs), list(out_shape))
    n_in, n_out, n_sc, n_r = len(ins), len(own_shapes), len(scratch_shapes), len(rider.srcs)
    n_all_in = n_in + n_r

    def carrying(*refs):
        own_in, srcs = refs[:n_in], refs[n_in:n_in + n_r]
        own_out, dsts = refs[n_all_in:n_all_in + n_out], refs[n_all_in + n_out:n_all_in + n_out + n_r]
        own_sc = refs[n_all_in + n_out + n_r:n_all_in + n_out + n_r + n_sc]
        sems = refs[n_all_in + n_out + n_r + n_sc:]
        first = last = late = None
        for d, steps in enumerate(grid):
            f, l = pl.program_id(d) == 0, pl.program_id(d) == steps - 1
            t = pl.program_id(d) == ((3 * steps) // 4 if d == 0 else 0)
            first, last, late = (f, l, t) if first is None else (first & f, last & l, late & t)

        @pl.when(first)
        def _():
            rider.start(srcs, dsts, *sems)

        @pl.when(late)
        def _():
            rider.forward(srcs, dsts, *sems)

        body(*own_in, *own_out, *own_sc)

        @pl.when(last)
        def _():
            rider.finish(srcs, dsts, *sems)

    res = pl.pallas_call(
        carrying, name=name, grid=grid, in_specs=list(in_specs) + [HBM_SPEC] * n_r,
        out_specs=own_specs + [HBM_SPEC] * n_r, out_shape=own_shapes + rider.out_shapes,
        scratch_shapes=list(scratch_shapes) + rider.scratch, compiler_params=_params(("arbitrary",) * len(grid)),
    )(*ins, *rider.srcs)
    own = res[:n_out]
    return (own[0] if single else list(own)), list(res[n_out:])


def _matmul(a, b, mode, out_dtype, name, add=None, relu2_a=False, relu2_grad=None,
            tm_t=None, tn_t=None, tk_t=None, rider=None, m_rows=None, col_shards=None, sq_err_target=None,
            rms_gain=None, rms_bwd=None):
    if mode == 'nn':
        (M, K), (K2, N) = a.shape, b.shape
    elif mode == 'nt':
        (M, K), (N, K2) = a.shape, b.shape
    else:
        (K, M), (K2, N) = a.shape, b.shape
    assert K == K2, (name, a.shape, b.shape)
    m_first = 0
    if m_rows is not None:
        assert mode == 'tn'
        m_first, M = m_rows
    if col_shards is not None:
        assert add is None and relu2_grad is None and sq_err_target is None and tn_t is None
    if mode == 'tn':
        d_tm, d_tn, d_tk = 1024, (2048 if M <= 512 else 1024), 2048
    else:
        wide = add is None and sq_err_target is None and jnp.dtype(out_dtype).itemsize == 2 and K <= D_FF
        d_tm, d_tn, d_tk = (2048 if K <= 1024 else 1024), (1024 if wide else 512), MAX_K_TILE
    tm, tn, tk = _pick(M, tm_t or d_tm), _pick(N, tn_t or d_tn), _pick(K, tk_t or d_tk)
    gm, gn, nk = M // tm, N // tn, K // tk
    if mode == 'nn':
        a_spec = pl.BlockSpec((tm, tk), lambda i, j, k: (i, k))
        b_spec = pl.BlockSpec((tk, tn), lambda i, j, k: (k, j))
        dims = (((1,), (0,)), ((), ()))
    elif mode == 'nt':
        a_spec = pl.BlockSpec((tm, tk), lambda i, j, k: (i, k))
        b_spec = pl.BlockSpec((tn, tk), lambda i, j, k: (j, k))
        dims = (((1,), (1,)), ((), ()))
    else:
        assert m_first % tm == 0
        a_spec = pl.BlockSpec((tk, tm), lambda i, j, k: (k, m_first // tm + i))
        b_spec = pl.BlockSpec((tk, tn), lambda i, j, k: (k, j))
        dims = (((0,), (0,)), ((), ()))
    o_spec = pl.BlockSpec((tm, tn), lambda i, j, k: (i, j))
    shard_w = N // col_shards if col_shards is not None else tn
    assert tn % shard_w == 0
    has_add, has_e, has_t = add is not None, relu2_grad is not None, sq_err_target is not None
    has_n = rms_gain is not None
    assert not has_t or (nk == 1 and tn % LANES == 0)
    assert not has_n or (nk == 1 and tn == N and not has_t and col_shards is None)
    has_b = rms_bwd is not None
    assert not has_b or (nk == 1 and tn == N and not (has_add or has_e or has_t or has_n) and col_shards is None)

    def body(*refs):
        a_ref, b_ref = refs[0], refs[1]
        pos = 2
        add_ref = e_ref = t_ref = None
        if has_add:
            add_ref = refs[pos]
            pos += 1
        if has_e:
            e_ref = refs[pos]
            pos += 1
        if has_t:
            t_ref = refs[pos]
            pos += 1
        if has_n:
            gain_ref = refs[pos]
            pos += 1
        if has_b:
            x_ref, gain_ref, res_ref = refs[pos:pos + 3]
            pos += 3
        o_ref = refs[pos]
        acc_ref = refs[pos + 1] if nk > 1 else None

        av = a_ref[...]
        if relu2_a:
            av = jnp.maximum(av, 0)
            av = av * av
        prod = lax.dot_general(av.astype(BF16), b_ref[...].astype(BF16), dims, preferred_element_type=F32)

        def finish(r):
            if has_add:
                r = r + add_ref[...]
            if has_e:
                r = r * (2.0 * jnp.maximum(e_ref[...].astype(F32), 0.0))
            if has_t:
                err = r - t_ref[...]
                r = err * (1.0 / N)
                refs[pos + 1][...] = r.astype(BF16)
                sq = err * err
                part = sq[:, 0:LANES]
                for c in range(1, tn // LANES):
                    part = part + sq[:, c * LANES:(c + 1) * LANES]
                _acc_rows(refs[pos + 2], part, (pl.program_id(0) == 0) & (pl.program_id(1) == 0))
            if has_n:
                refs[pos + 1][...] = (r * _rms(r, N) * gain_ref[...]).astype(BF16)
            if has_b:
                dx, dgv = _rms_bwd_rows(x_ref[...], gain_ref[...], r.astype(out_dtype), N)
                dx = dx + res_ref[...]
                o_ref[...] = dx
                refs[pos + 1][...] = dx.astype(BF16)
                _acc_rows(refs[pos + 2], dgv, (pl.program_id(0) == 0) & (pl.program_id(1) == 0))
            elif col_shards is not None:
                for s in range(tn // shard_w):
                    o_ref[s] = r[:, s * shard_w:(s + 1) * shard_w].astype(out_dtype)
            else:
                o_ref[...] = r.astype(out_dtype)

        if nk == 1:
            finish(prod)
        else:
            k = pl.program_id(2)

            @pl.when(k == 0)
            def _():
                acc_ref[...] = prod

            @pl.when(k > 0)
            def _():
                acc_ref[...] += prod

            @pl.when(k == nk - 1)
            def _():
                finish(acc_ref[...])

    ins, specs = [a, b], [a_spec, b_spec]
    if has_add:
        ins.append(add)
        specs.append(o_spec)
    if has_e:
        ins.append(relu2_grad)
        specs.append(o_spec)
    out_specs, out_shape, sem = o_spec, jax.ShapeDtypeStruct((M, N), out_dtype), ("parallel", "parallel", "arbitrary")
    if has_t:
        ins.append(sq_err_target)
        specs.append(o_spec)
        out_specs = [o_spec, o_spec, pl.BlockSpec((1, LANES), lambda i, j, k: (0, 0))]
        out_shape = [out_shape, jax.ShapeDtypeStruct((M, N), BF16), jax.ShapeDtypeStruct((1, LANES), F32)]
        sem = ("arbitrary", "arbitrary", "arbitrary")
    if has_n:
        ins.append(rms_gain)
        specs.append(pl.BlockSpec((1, N), lambda i, j, k: (0, 0)))
        out_specs = [o_spec, o_spec]
        out_shape = [out_shape, jax.ShapeDtypeStruct((M, N), BF16)]
    if has_b:
        x, g, res = rms_bwd
        ins += [x, g, res]
        specs += [o_spec, pl.BlockSpec((1, N), lambda i, j, k: (0, 0)), o_spec]
        out_specs = [o_spec, o_spec, pl.BlockSpec((1, N), lambda i, j, k: (0, 0))]
        out_shape = [jax.ShapeDtypeStruct((M, N), F32), jax.ShapeDtypeStruct((M, N), BF16), jax.ShapeDtypeStruct((1, N), F32)]
        sem = ("arbitrary", "arbitrary", "arbitrary")
    if col_shards is not None:
        out_specs = pl.BlockSpec((tn // shard_w, tm, shard_w), lambda i, j, k: (j, i, 0))
        out_shape = jax.ShapeDtypeStruct((col_shards, M, shard_w), out_dtype)
    return _call(body, rider, ins, name=name, grid=(gm, gn, nk), in_specs=specs, out_specs=out_specs, out_shape=out_shape,
                 scratch_shapes=[pltpu.VMEM((tm, tn), F32)] if nk > 1 else [], sem=sem)


ROW_BLOCK_BYTES = 12 * 1024 * 1024


def _row_tile(rows, row_bytes):
    return _pick(rows, max(16, min(1024, ROW_BLOCK_BYTES // row_bytes)), 16)


def _rowspec(tr, width, col=0):
    return pl.BlockSpec((tr, width), lambda i, col=col: (i, col))


def _fullspec(shape):
    nd = len(shape)
    return pl.BlockSpec(shape, lambda i, nd=nd: (0,) * nd)


def _rms(x, width):
    x = x.astype(F32)
    return lax.rsqrt(jnp.sum(x * x, axis=-1, keepdims=True) * (1.0 / width) + EPS)


def _rms_bwd_rows(x, g, dy, width):
    x, dy = x.astype(F32), dy.astype(F32)
    r = _rms(x, width)
    xh = x * r
    dn = dy * g
    dx = r * (dn - xh * (jnp.sum(dn * xh, axis=-1, keepdims=True) * (1.0 / width)))
    return dx, dy * xh


def _acc_rows(ref, val, first):
    s = jnp.sum(val, axis=0, keepdims=True)

    @pl.when(first)
    def _():
        ref[...] = s

    @pl.when(jnp.logical_not(first))
    def _():
        ref[...] += s


def _rms_fwd(x, g, name, rider=None):
    rows, width = x.shape
    tr = _row_tile(rows, 6 * width)

    def body(x_ref, g_ref, o_ref):
        xv = x_ref[...]
        o_ref[...] = (xv * _rms(xv, width) * g_ref[...]).astype(BF16)

    return _call(body, rider, [x, g], name=name, grid=(rows // tr,),
                 in_specs=[_rowspec(tr, width), _fullspec((1, width))], out_specs=_rowspec(tr, width),
                 out_shape=jax.ShapeDtypeStruct((rows, width), BF16), scratch_shapes=[], sem=("parallel",))


def _rms_bwd(x, g, dy, res, name, dx_dtypes=(F32,)):
    rows, width = x.shape
    tr = _row_tile(rows, 18 * width)
    has_res = res is not None
    n_in = 4 if has_res else 3

    def body(*refs):
        x_ref, g_ref, dy_ref = refs[:3]
        dx, dgv = _rms_bwd_rows(x_ref[...], g_ref[...], dy_ref[...], width)
        if has_res:
            dx = dx + refs[3][...]
        for ref, dt in zip(refs[n_in:], dx_dtypes):
            ref[...] = dx.astype(dt)
        _acc_rows(refs[-1], dgv, pl.program_id(0) == 0)

    ins = [x, g, dy] + ([res] if has_res else [])
    specs = [_rowspec(tr, width), _fullspec((1, width)), _rowspec(tr, width)] + ([_rowspec(tr, width)] if has_res else [])
    return pl.pallas_call(
        body, name=name, grid=(rows // tr,), in_specs=specs,
        out_specs=[_rowspec(tr, width)] * len(dx_dtypes) + [_fullspec((1, width))],
        out_shape=[jax.ShapeDtypeStruct((rows, width), dt) for dt in dx_dtypes] + [jax.ShapeDtypeStruct((1, width), F32)],
        compiler_params=_params(("arbitrary",)),
    )(*ins)


_GELU_C = math.sqrt(2.0 / math.pi)


def _gelu(x):
    t = jnp.tanh(_GELU_C * (x + 0.044715 * (x * x * x)))
    return 0.5 * x * (1.0 + t), t


def _gelu_grad(x, t):
    return 0.5 * (1.0 + t) + 0.5 * x * (1.0 - t * t) * (_GELU_C * (1.0 + 3.0 * 0.044715 * (x * x)))


def _gm_forward_rows(zu, zv, gln, bln, wc_ref, bst, n_chunk):
    u, tu = _gelu(zu)
    a, ta = _gelu(zv)
    mu = jnp.mean(a, axis=-1, keepdims=True)
    ac = a - mu
    rs = lax.rsqrt(jnp.mean(ac * ac, axis=-1, keepdims=True) + EPS)
    n = ac * rs
    v = n * gln + bln
    vb = v.astype(BF16)
    rows = []
    for c in range(n_chunk):
        cols = []
        for g in range(GM_GROUPS):
            vc = vb[c * GM_CHUNK:(c + 1) * GM_CHUNK, g * LANES:(g + 1) * LANES]
            mixed = jnp.dot(wc_ref[g], vc, preferred_element_type=F32) + bst[g]
            cols.append(mixed)
        rows.append(jnp.concatenate(cols, axis=1))
    mixed = jnp.concatenate(rows, axis=0) if n_chunk > 1 else rows[0]
    return u, tu, ta, n, rs, v, mixed


def _gm_fwd(z, gln, bln, wc, bst, name):
    rows = z.shape[0]
    tr = _pick(rows, 1024, GM_CHUNK)
    n_chunk = tr // GM_CHUNK

    def body(zu_ref, zv_ref, gln_ref, bln_ref, wc_ref, bst_ref, o_ref):
        u, _, _, _, _, _, mixed = _gm_forward_rows(zu_ref[...].astype(F32), zv_ref[...].astype(F32), gln_ref[...], bln_ref[...], wc_ref,
                                                   bst_ref, n_chunk)
        o_ref[...] = (u * mixed).astype(BF16)

    return pl.pallas_call(
        body, name=name, grid=(rows // tr,),
        in_specs=[_rowspec(tr, GM_WIDTH, Z_GM // GM_WIDTH), _rowspec(tr, GM_WIDTH, Z_GM // GM_WIDTH + 1),_fullspec((1, GM_WIDTH)), _fullspec((1, GM_WIDTH)),
                  _fullspec((GM_GROUPS, GM_CHUNK, GM_CHUNK)), _fullspec((GM_GROUPS, GM_CHUNK, LANES))],
        out_specs=_rowspec(tr, GM_WIDTH), out_shape=jax.ShapeDtypeStruct((rows, GM_WIDTH), BF16),
        compiler_params=_params(("parallel",)),
    )(z, z, gln, bln, wc, bst)


ANY_SPEC = pl.BlockSpec(memory_space=pl.ANY)


def _gm_bwd(z, dy, gln, bln, wc, wct, bst, dz, name):
    rows = z.shape[0]
    tr = _pick(rows, 1024, GM_CHUNK)
    n_chunk = tr // GM_CHUNK

    def body(zu_ref, zv_ref, dy_ref, gln_ref, bln_ref, wc_ref, wct_ref, bst_ref, _, dz_ref, dws_ref, dbs_ref, dgl_ref,
             dbl_ref):
        first = pl.program_id(0) == 0
        zu, zv, gln = zu_ref[...].astype(F32), zv_ref[...].astype(F32), gln_ref[...]
        u, tu, ta, n, rs, v, mixed = _gm_forward_rows(zu, zv, gln, bln_ref[...], wc_ref, bst_ref, n_chunk)
        dyv = dy_ref[...].astype(F32)
        dzu = dyv * mixed * _gelu_grad(zu, tu)
        dmix = dyv * u
        dmb = dmix.astype(BF16)
        vb = v.astype(BF16)
        dv_rows, dws, dbs = [], [None] * GM_GROUPS, None
        for c in range(n_chunk):
            rsl = slice(c * GM_CHUNK, (c + 1) * GM_CHUNK)
            cols = []
            for g in range(GM_GROUPS):
                csl = slice(g * LANES, (g + 1) * LANES)
                dmc = dmb[rsl, csl]
                cols.append(jnp.dot(wct_ref[g], dmc, preferred_element_type=F32))
                w_part = lax.dot_general(dmc, vb[rsl, csl], (((1,), (1,)), ((), ())), preferred_element_type=F32)
                dws[g] = w_part if dws[g] is None else dws[g] + w_part
            dv_rows.append(jnp.concatenate(cols, axis=1))
            dbs = dmix[rsl, :] if dbs is None else dbs + dmix[rsl, :]
        dv = jnp.concatenate(dv_rows, axis=0) if n_chunk > 1 else dv_rows[0]
        dn = dv * gln
        da = rs * (dn - jnp.mean(dn, axis=-1, keepdims=True) - n * jnp.mean(dn * n, axis=-1, keepdims=True))
        dzv = da * _gelu_grad(zv, ta)
        dz_ref[:, 0:GM_WIDTH] = dzu.astype(BF16)
        dz_ref[:, GM_WIDTH:2 * GM_WIDTH] = dzv.astype(BF16)
        _acc_rows(dgl_ref, dv * n, first)
        _acc_rows(dbl_ref, dv, first)

        @pl.when(first)
        def _():
            for g in range(GM_GROUPS):
                dws_ref[g] = dws[g]
            dbs_ref[...] = dbs

        @pl.when(jnp.logical_not(first))
        def _():
            for g in range(GM_GROUPS):
                dws_ref[g] += dws[g]
            dbs_ref[...] += dbs

    wspec = _fullspec((GM_GROUPS, GM_CHUNK, GM_CHUNK))
    return pl.pallas_call(
        body, name=name, grid=(rows // tr,),
        in_specs=[_rowspec(tr, GM_WIDTH, Z_GM // GM_WIDTH), _rowspec(tr, GM_WIDTH, Z_GM // GM_WIDTH + 1),
                  _rowspec(tr, GM_WIDTH), _fullspec((1, GM_WIDTH)), _fullspec((1, GM_WIDTH)), wspec, wspec, wspec, ANY_SPEC],
        out_specs=[_rowspec(tr, 2 * GM_WIDTH, Z_GM // (2 * GM_WIDTH)), wspec, _fullspec((GM_CHUNK, GM_WIDTH)),
                   _fullspec((1, GM_WIDTH)), _fullspec((1, GM_WIDTH))],
        out_shape=[jax.ShapeDtypeStruct(dz.shape, dz.dtype), jax.ShapeDtypeStruct((GM_GROUPS, GM_CHUNK, GM_CHUNK), F32),
                   jax.ShapeDtypeStruct((GM_CHUNK, GM_WIDTH), F32), jax.ShapeDtypeStruct((1, GM_WIDTH), F32),
                   jax.ShapeDtypeStruct((1, GM_WIDTH), F32)],
        input_output_aliases={8: 0}, compiler_params=_params(("arbitrary",)),
    )(z, z, dy, gln, bln, wc, wct, bst, dz)


def _lat_fwd(z, g_cq, g_ckv, name):
    rows = z.shape[0]
    tr = _row_tile(rows, 4 * MLA_W)

    def body(z_ref, gq_ref, gkv_ref, nq_ref, nkv_ref):
        zb = z_ref[...]
        cq, ckv = zb[:, 0:Q_LORA], zb[:, Q_LORA:Q_LORA + KV_LORA]
        nq_ref[...] = (cq * _rms(cq, Q_LORA) * gq_ref[...]).astype(BF16)
        nkv_ref[...] = (ckv * _rms(ckv, KV_LORA) * gkv_ref[...]).astype(BF16)

    return pl.pallas_call(
        body, name=name, grid=(rows // tr,),
        in_specs=[_rowspec(tr, MLA_W, Z_MLA // MLA_W), _fullspec((1, Q_LORA)), _fullspec((1, KV_LORA))],
        out_specs=[_rowspec(tr, Q_LORA), _rowspec(tr, KV_LORA)],
        out_shape=[jax.ShapeDtypeStruct((rows, Q_LORA), BF16), jax.ShapeDtypeStruct((rows, KV_LORA), BF16)],
        compiler_params=_params(("parallel",)),
    )(z, g_cq, g_ckv)


def _lat_bwd(z, dnq, dnkv, dkpe, g_cq, g_ckv, dz, name):
    rows = z.shape[0]
    tr = _row_tile(rows, 8 * MLA_W)

    def body(z_ref, dnq_ref, dnkv_ref, dkpe_ref, gq_ref, gkv_ref, _, dz_ref, dgq_ref, dgkv_ref):
        first = pl.program_id(0) == 0
        zb = z_ref[...]
        dcq, dgq = _rms_bwd_rows(zb[:, 0:Q_LORA], gq_ref[...], dnq_ref[...], Q_LORA)
        dckv, dgkv = _rms_bwd_rows(zb[:, Q_LORA:Q_LORA + KV_LORA], gkv_ref[...], dnkv_ref[...], KV_LORA)
        dz_ref[:, 0:Q_LORA] = dcq.astype(BF16)
        dz_ref[:, Q_LORA:Q_LORA + KV_LORA] = dckv.astype(BF16)
        dz_ref[:, Q_LORA + KV_LORA:MLA_W] = dkpe_ref[...].astype(BF16)
        _acc_rows(dgq_ref, dgq, first)
        _acc_rows(dgkv_ref, dgkv, first)

    return pl.pallas_call(
        body, name=name, grid=(rows // tr,),
        in_specs=[_rowspec(tr, MLA_W, Z_MLA // MLA_W), _rowspec(tr, Q_LORA), _rowspec(tr, KV_LORA), _rowspec(tr, LANES),
                  _fullspec((1, Q_LORA)), _fullspec((1, KV_LORA)), ANY_SPEC],
        out_specs=[_rowspec(tr, MLA_W, Z_MLA // MLA_W), _fullspec((1, Q_LORA)), _fullspec((1, KV_LORA))],
        out_shape=[jax.ShapeDtypeStruct(dz.shape, dz.dtype), jax.ShapeDtypeStruct((1, Q_LORA), F32),
                   jax.ShapeDtypeStruct((1, KV_LORA), F32)],
        input_output_aliases={6: 0}, compiler_params=_params(("arbitrary",)),
    )(z, dnq, dnkv, dkpe, g_cq, g_ckv, dz)


def _rope(y, cc, ss):
    return y * cc + pltpu.roll(y, 64, 1) * ss


def _rope_bwd(d, cc, ss):
    return d * cc + pltpu.roll(d * ss, 64, 1)


def _qk_fwd(q, kv, z, cc, ss, gqn, gqp, gkn, gkp, name):
    rows = q.shape[0]
    W = MLA_HEADS * HEAD
    tr = _row_tile(rows, 20 * W)
    QS = MLA_SCALE * LOG2E

    def body(q_ref, kv_ref, kpe_ref, cc_ref, ss_ref, gqn_ref, gqp_ref, gkn_ref, gkp_ref, qc_ref, kc_ref, v_ref):
        cc, ss = cc_ref[...], ss_ref[...]
        kpe = kpe_ref[...]
        kp = _rope(kpe * _rms(kpe, MLA_ROPE) * gkp_ref[...], cc, ss).astype(BF16)
        for h in range(MLA_HEADS):
            qn = q_ref[:, h * HEAD:(h + 1) * HEAD]
            qp = q_ref[:, W + h * HEAD:W + (h + 1) * HEAD]
            kn = kv_ref[:, h * HEAD:(h + 1) * HEAD]
            qc_ref[:, h * QCAT:h * QCAT + HEAD] = (qn * _rms(qn, HEAD) * gqn_ref[...] * QS).astype(BF16)
            qc_ref[:, h * QCAT + HEAD:(h + 1) * QCAT] = (_rope(qp * _rms(qp, MLA_ROPE) * gqp_ref[...], cc, ss) * QS).astype(BF16)
            kc_ref[:, h * QCAT:h * QCAT + HEAD] = (kn * _rms(kn, HEAD) * gkn_ref[...]).astype(BF16)
            kc_ref[:, h * QCAT + HEAD:(h + 1) * QCAT] = kp
        v_ref[...] = kv_ref[:, W:2 * W].astype(BF16)

    g = _fullspec((1, HEAD))
    return _call(
        body, None, [q, kv, z, cc, ss, gqn, gqp, gkn, gkp], name=name, grid=(rows // tr,),
        in_specs=[_rowspec(tr, 2 * W), _rowspec(tr, 2 * W), _rowspec(tr, LANES, Z_KPE // LANES), _rowspec(tr, LANES),
                  _rowspec(tr, LANES), g, g, g, g],
        out_specs=[_rowspec(tr, MLA_HEADS * QCAT), _rowspec(tr, MLA_HEADS * QCAT), _rowspec(tr, W)],
        out_shape=[jax.ShapeDtypeStruct((rows, MLA_HEADS * QCAT), BF16), jax.ShapeDtypeStruct((rows, MLA_HEADS * QCAT), BF16),
                   jax.ShapeDtypeStruct((rows, W), BF16)],
        scratch_shapes=[], sem=("parallel",))


def _qk_bwd(q, kv, z, cc, ss, gqn, gqp, gkn, gkp, dqc, dkc, dv, name):
    rows = q.shape[0]
    W = MLA_HEADS * HEAD
    tr = _row_tile(rows, 24 * W)

    def body(q_ref, kv_ref, kpe_ref, cc_ref, ss_ref, gqn_ref, gqp_ref, gkn_ref, gkp_ref, dqc_ref, dkc_ref, dv_ref,
             dq_ref, dkv_ref, dkpe_ref, dgqn_ref, dgqp_ref, dgkn_ref, dgkp_ref):
        first = pl.program_id(0) == 0
        cc, ss = cc_ref[...], ss_ref[...]
        sqn = sqp = skn = dkp = None
        for h in range(MLA_HEADS):
            dx, dg = _rms_bwd_rows(q_ref[:, h * HEAD:(h + 1) * HEAD], gqn_ref[...], dqc_ref[:, h * QCAT:h * QCAT + HEAD], HEAD)
            dq_ref[:, h * HEAD:(h + 1) * HEAD] = dx.astype(BF16)
            sqn = dg if sqn is None else sqn + dg
            dy = _rope_bwd(dqc_ref[:, h * QCAT + HEAD:(h + 1) * QCAT], cc, ss)
            dx, dg = _rms_bwd_rows(q_ref[:, W + h * HEAD:W + (h + 1) * HEAD], gqp_ref[...], dy, MLA_ROPE)
            dq_ref[:, W + h * HEAD:W + (h + 1) * HEAD] = dx.astype(BF16)
            sqp = dg if sqp is None else sqp + dg
            dx, dg = _rms_bwd_rows(kv_ref[:, h * HEAD:(h + 1) * HEAD], gkn_ref[...], dkc_ref[:, h * QCAT:h * QCAT + HEAD], HEAD)
            dkv_ref[:, h * HEAD:(h + 1) * HEAD] = dx.astype(BF16)
            skn = dg if skn is None else skn + dg
            part = dkc_ref[:, h * QCAT + HEAD:(h + 1) * QCAT].astype(F32)
            dkp = part if dkp is None else dkp + part
        dkv_ref[:, W:2 * W] = dv_ref[...].astype(BF16)
        dx, dg = _rms_bwd_rows(kpe_ref[...], gkp_ref[...], _rope_bwd(dkp, cc, ss), MLA_ROPE)
        dkpe_ref[...] = dx
        _acc_rows(dgqn_ref, sqn, first)
        _acc_rows(dgqp_ref, sqp, first)
        _acc_rows(dgkn_ref, skn, first)
        _acc_rows(dgkp_ref, dg, first)

    g = _fullspec((1, HEAD))
    gs = jax.ShapeDtypeStruct((1, HEAD), F32)
    return pl.pallas_call(
        body, name=name, grid=(rows // tr,),
        in_specs=[_rowspec(tr, 2 * W), _rowspec(tr, 2 * W), _rowspec(tr, LANES, Z_KPE // LANES), _rowspec(tr, LANES),
                  _rowspec(tr, LANES), g, g, g, g, _rowspec(tr, MLA_HEADS * QCAT), _rowspec(tr, MLA_HEADS * QCAT),
                  _rowspec(tr, W)],
        out_specs=[_rowspec(tr, 2 * W), _rowspec(tr, 2 * W), _rowspec(tr, LANES), g, g, g, g],
        out_shape=[jax.ShapeDtypeStruct((rows, 2 * W), BF16), jax.ShapeDtypeStruct((rows, 2 * W), BF16),
                   jax.ShapeDtypeStruct((rows, LANES), F32), gs, gs, gs, gs],
        compiler_params=_params(("arbitrary",)),
    )(q, kv, z, cc, ss, gqn, gqp, gkn, gkp, dqc, dkc, dv)


def _headnorm_fwd(x, col, nheads, g, out_scale, name):
    rows = x.shape[0]
    W = nheads * HEAD
    tr = _row_tile(rows, 6 * W)

    def body(x_ref, g_ref, o_ref):
        for h in range(nheads):
            xv = x_ref[:, h * HEAD:(h + 1) * HEAD]
            o_ref[:, h * HEAD:(h + 1) * HEAD] = (xv * _rms(xv, HEAD) * g_ref[...] * out_scale).astype(BF16)

    return pl.pallas_call(
        body, name=name, grid=(rows // tr,),
        in_specs=[_rowspec(tr, W, col), _fullspec((1, HEAD))], out_specs=_rowspec(tr, W),
        out_shape=jax.ShapeDtypeStruct((rows, W), BF16), compiler_params=_params(("parallel",)),
    )(x, g)


def _headnorm_bwd(x, col, nheads, g, dy, tail, name, into=None):
    rows = x.shape[0]
    W = nheads * HEAD
    tr = _row_tile(rows, 12 * W)
    has_tail = tail is not None
    WO = 2 * W if has_tail else W

    def body(*refs):
        if into is not None:
            x_ref, g_ref, dy_ref, _, dx_ref, dg_ref = refs
        elif has_tail:
            x_ref, g_ref, dy_ref, t_ref, dx_ref, dg_ref = refs
        else:
            x_ref, g_ref, dy_ref, dx_ref, dg_ref = refs
        acc = None
        for h in range(nheads):
            sl = slice(h * HEAD, (h + 1) * HEAD)
            dx, dg = _rms_bwd_rows(x_ref[:, sl], g_ref[...], dy_ref[:, sl], HEAD)
            dx_ref[:, sl] = dx.astype(BF16)
            acc = dg if acc is None else acc + dg
        if has_tail:
            dx_ref[:, W:2 * W] = t_ref[...].astype(BF16)
        _acc_rows(dg_ref, acc, pl.program_id(0) == 0)

    ins = [x, g, dy] + ([tail] if has_tail else [])
    specs = [_rowspec(tr, W, col), _fullspec((1, HEAD)), _rowspec(tr, W)] + ([_rowspec(tr, W)] if has_tail else [])
    dx_spec, dx_shape, aliases = _rowspec(tr, WO), jax.ShapeDtypeStruct((rows, WO), BF16), {}
    if into is not None:
        assert not has_tail
        ins, specs = ins + [into[0]], specs + [ANY_SPEC]
        dx_spec, dx_shape, aliases = _rowspec(tr, W, into[1]), jax.ShapeDtypeStruct(into[0].shape, into[0].dtype), {3: 0}
    return pl.pallas_call(
        body, name=name, grid=(rows // tr,), in_specs=specs,
        out_specs=[dx_spec, _fullspec((1, HEAD))], out_shape=[dx_shape, jax.ShapeDtypeStruct((1, HEAD), F32)],
        input_output_aliases=aliases, compiler_params=_params(("arbitrary",)),
    )(*ins)


def _sigmoid(x):
    return 1.0 / (1.0 + jnp.exp(-x.astype(F32)))


def _merge_fwd(z, y_gm, y_mla, y_mem, name):
    rows = z.shape[0]
    tr = _row_tile(rows, 14 * D_MODEL)

    def body(g0_ref, g1_ref, g2_ref, a_ref, b_ref, c_ref, o_ref):
        m = _sigmoid(g0_ref[...]) * a_ref[...] + _sigmoid(g1_ref[...]) * b_ref[...] + _sigmoid(g2_ref[...]) * c_ref[...]
        o_ref[...] = m.astype(BF16)

    r = _rowspec(tr, D_MODEL)
    return pl.pallas_call(
        body, name=name, grid=(rows // tr,),
        in_specs=[_rowspec(tr, D_MODEL, 0), _rowspec(tr, D_MODEL, 1), _rowspec(tr, D_MODEL, 2),r, r, r],
        out_specs=r, out_shape=jax.ShapeDtypeStruct((rows, D_MODEL), BF16), compiler_params=_params(("parallel",)),
    )(z, z, z, y_gm, y_mla, y_mem)


def _merge_bwd(z, y_gm, y_mla, y_mem, dm, name):
    rows = z.shape[0]
    tr = _row_tile(rows, 24 * D_MODEL)

    def body(g0_ref, g1_ref, g2_ref, a_ref, b_ref, c_ref, dm_ref, da_ref, db_ref, dc_ref, dzg_ref):
        dmv = dm_ref[...].astype(F32)
        for k, (g_ref, y_ref, dy_ref) in enumerate(((g0_ref, a_ref, da_ref), (g1_ref, b_ref, db_ref), (g2_ref, c_ref, dc_ref))):
            s = _sigmoid(g_ref[...])
            dy_ref[...] = (dmv * s).astype(BF16)
            dzg_ref[:, k * D_MODEL:(k + 1) * D_MODEL] = (dmv * y_ref[...] * s * (1.0 - s)).astype(BF16)

    r = _rowspec(tr, D_MODEL)
    o = jax.ShapeDtypeStruct((rows, D_MODEL), BF16)
    return pl.pallas_call(
        body, name=name, grid=(rows // tr,),
        in_specs=[_rowspec(tr, D_MODEL, 0), _rowspec(tr, D_MODEL, 1), _rowspec(tr, D_MODEL, 2),r, r, r, r],
        out_specs=[r, r, r, _rowspec(tr, 3 * D_MODEL, 0)],
        out_shape=[o, o, o, jax.ShapeDtypeStruct((rows, Z_COLS), BF16)],
        compiler_params=_params(("parallel",)),
    )(z, z, z, y_gm, y_mla, y_mem, dm)


_NT = (((1,), (1,)), ((), ()))
_TN = (((0,), (0,)), ((), ()))


def _diag_mask(s):
    row = lax.broadcasted_iota(jnp.int32, s.shape, 0)
    col = lax.broadcasted_iota(jnp.int32, s.shape, 1)
    return jnp.where(row >= col, s, NEG)


def _attn_fwd(q, k, v, nb, nheads, dk, v_col0, causal, name, rider=None):
    S, Skv = q.shape[0] // nb, k.shape[0] // nb
    tq = _pick(Skv, ATT_TILE) if causal else _pick(S, 4 * ATT_TILE)
    nq = S // tq

    def body(q_ref, k_ref, v_ref, o_ref, lse_ref):
        for i in range(nq):
            r0 = i * tq
            qb = q_ref[r0:r0 + tq, :]
            if causal:
                spans = ([(0, r0, False)] if i > 0 else []) + [(r0, r0 + tq, True)]
            else:
                spans = [(0, Skv, False)]
            scores = []
            for a, b, masked in spans:
                s = lax.dot_general(qb, k_ref[a:b, :], _NT, preferred_element_type=F32)
                scores.append(_diag_mask(s) if masked else s)
            m = functools.reduce(jnp.maximum, [jnp.max(s, axis=-1, keepdims=True) for s in scores])
            l = acc = None
            for s, (a, b, _) in zip(scores, spans):
                p = jnp.exp2(s - m)
                lp = jnp.sum(p, axis=-1, keepdims=True)
                ap = jnp.dot(p.astype(BF16), v_ref[a:b, :].astype(BF16), preferred_element_type=F32)
                l, acc = (lp, ap) if l is None else (l + lp, acc + ap)
            o_ref[r0:r0 + tq, :] = (acc / l).astype(BF16)
            lse_ref[r0:r0 + tq, :] = m + jnp.log2(l)

    ins = [q, k, v]
    in_specs = [pl.BlockSpec((S, dk), lambda b, h: (b, h)), pl.BlockSpec((Skv, dk), lambda b, h: (b, h)),
                pl.BlockSpec((Skv, HEAD), lambda b, h: (b, v_col0 + h))]
    out_specs = [pl.BlockSpec((S, HEAD), lambda b, h: (b, h)), pl.BlockSpec((None, S, 1), lambda b, h: (h, b, 0))]
    out_shape = [jax.ShapeDtypeStruct((nb * S, nheads * HEAD), BF16), jax.ShapeDtypeStruct((nheads, nb * S, 1), F32)]
    return _call(body, rider, ins, name=name, grid=(nb, nheads), in_specs=in_specs, out_specs=out_specs,
                 out_shape=out_shape, scratch_shapes=[], sem=("parallel", "parallel"))


def _attn_bwd(q, k, v, o, do, lse, nb, nheads, dk, v_col0, scale, causal, name, rider=None):
    S, Skv = q.shape[0] // nb, k.shape[0] // nb
    tk = _pick(Skv, ATT_TILE)
    nkv = Skv // tk

    def body(q_ref, k_ref, v_ref, o_ref, do_ref, lse_ref, dq_ref, dk_ref, dv_ref, delta_ref, dob_ref, dqa_ref):
        dov = do_ref[...]
        delta_ref[...] = jnp.sum(o_ref[...].astype(F32) * dov.astype(F32), axis=-1, keepdims=True)
        dob_ref[...] = dov.astype(BF16)

        for j in range(nkv):
            c0 = j * tk
            kb = k_ref[c0:c0 + tk, :]
            vb = v_ref[c0:c0 + tk, :].astype(BF16)
            if causal:
                spans = [(c0, c0 + tk, True)] + ([(c0 + tk, S, False)] if c0 + tk < S else [])
            else:
                spans = [(0, S, False)]
            dk_acc = dv_acc = None
            for a, b, masked in spans:
                qb = q_ref[a:b, :]
                dob = dob_ref[a:b, :]
                s = lax.dot_general(qb, kb, _NT, preferred_element_type=F32)
                if masked:
                    s = _diag_mask(s)
                p = jnp.exp2(s - lse_ref[a:b, :])
                dp = lax.dot_general(dob, vb, _NT, preferred_element_type=F32)
                ds = (p * (dp - delta_ref[a:b, :])).astype(BF16)
                dv_p = lax.dot_general(p.astype(BF16), dob, _TN, preferred_element_type=F32)
                dk_p = lax.dot_general(ds, qb, _TN, preferred_element_type=F32)
                dk_acc, dv_acc = (dk_p, dv_p) if dk_acc is None else (dk_acc + dk_p, dv_acc + dv_p)
                dq_p = jnp.dot(ds, kb, preferred_element_type=F32) * scale
                if j == 0:
                    dqa_ref[a:b, :] = dq_p
                else:
                    dqa_ref[a:b, :] += dq_p
            dk_ref[c0:c0 + tk, :] = (dk_acc * LN2).astype(BF16)
            dv_ref[c0:c0 + tk, :] = dv_acc.astype(BF16)
        dq_ref[...] = dqa_ref[...].astype(BF16)

    ins = [q, k, v, o, do, lse]
    in_specs = [pl.BlockSpec((S, dk), lambda b, h: (b, h)), pl.BlockSpec((Skv, dk), lambda b, h: (b, h)),
                pl.BlockSpec((Skv, HEAD), lambda b, h: (b, v_col0 + h)), pl.BlockSpec((S, HEAD), lambda b, h: (b, h)),
                pl.BlockSpec((S, HEAD), lambda b, h: (b, h)), pl.BlockSpec((None, S, 1), lambda b, h: (h, b, 0))]
    out_specs = [pl.BlockSpec((S, dk), lambda b, h: (b, h)), pl.BlockSpec((Skv, dk), lambda b, h: (b, h)),
                 pl.BlockSpec((Skv, HEAD), lambda b, h: (b, h))]
    out_shape = [jax.ShapeDtypeStruct((nb * S, nheads * dk), BF16), jax.ShapeDtypeStruct((nb * Skv, nheads * dk), BF16),
                 jax.ShapeDtypeStruct((nb * Skv, nheads * HEAD), BF16)]
    return _call(body, rider, ins, name=name, grid=(nb, nheads), in_specs=in_specs, out_specs=out_specs,
                 out_shape=out_shape,
                 scratch_shapes=[pltpu.VMEM((S, 1), F32), pltpu.VMEM((S, HEAD), BF16), pltpu.VMEM((S, dk), F32)],
                 sem=("parallel", "parallel"))


def _spread_rope(a):
    zero = jnp.zeros(a.shape[:-1] + (32,), a.dtype)
    return jnp.concatenate([a[..., :32], zero, a[..., 32:], zero], axis=-1)


def _gather_rope(a):
    return jnp.concatenate([a[..., 0:32], a[..., 64:96]], axis=-1)


def _win_layout(w):
    return jnp.concatenate([w[:, C_ZG:C_END], w[:, C_ZU:C_CQ], w[:, C_QM:C_ZG], w[:, C_CQ:C_CKV], w[:, C_CKV:C_KPE],
                            _spread_rope(w[:, C_KPE:C_QM])], axis=1)


def _win_unlayout(d):
    return jnp.concatenate([d[:, Z_GM:Z_QM], d[:, Z_MLA:Z_MLA + Q_LORA], d[:, Z_MLA + Q_LORA:Z_KPE],
                            _gather_rope(d[:, Z_KPE:Z_COLS]), d[:, Z_QM:Z_MLA], d[:, 0:Z_GM]], axis=1)


def _wuq_layout(w):
    r = w.reshape(Q_LORA, MLA_HEADS, HEAD + MLA_ROPE)
    return jnp.concatenate([r[:, :, :HEAD].reshape(Q_LORA, -1), _spread_rope(r[:, :, HEAD:]).reshape(Q_LORA, -1)], axis=1)


def _wuq_unlayout(d):
    n = d[:, :MLA_HEADS * HEAD].reshape(Q_LORA, MLA_HEADS, HEAD)
    p = _gather_rope(d[:, MLA_HEADS * HEAD:].reshape(Q_LORA, MLA_HEADS, HEAD))
    return jnp.concatenate([n, p], axis=-1).reshape(Q_LORA, -1)


def _wukv_layout(w):
    r = w.reshape(KV_LORA, MLA_HEADS, 2 * HEAD)
    return jnp.concatenate([r[:, :, :HEAD].reshape(KV_LORA, -1), r[:, :, HEAD:].reshape(KV_LORA, -1)], axis=1)


def _wukv_unlayout(d):
    k = d[:, :MLA_HEADS * HEAD].reshape(KV_LORA, MLA_HEADS, HEAD)
    v = d[:, MLA_HEADS * HEAD:].reshape(KV_LORA, MLA_HEADS, HEAD)
    return jnp.concatenate([k, v], axis=-1).reshape(KV_LORA, -1)


AG_MID = ['w_uq', 'w_ukv', 'w_mem_kv', 'w_o_gm', 'w_o_mla', 'w_o_mem', 'w_out']
AG_FFN = ['w_ff1', 'w_ff2']
RS_GROUPS = {'ffn_proj': ['w_ff2', 'w_ff1', 'w_out', 'w_o_gm', 'w_o_mla', 'w_o_mem'],
             'lat': ['w_uq', 'w_ukv', 'w_mem_kv'], 'in_top': ['w_in'], 'in_bot': ['w_in']}


def _unride(res, rider):
    return (res, None) if rider is None else res


def _local_step(x, mem, positions, target, P, ws):
    B, S, _ = x.shape
    M = mem.shape[1]
    T = B * S
    x2d = x.reshape(T, D_MODEL)
    mem2d = mem.reshape(B * M, D_MODEL)
    tgt2d = target.reshape(T, D_MODEL)

    def row(v):
        return v.reshape(1, -1).astype(F32)

    inv_freq = ROPE_BASE ** (-jnp.arange(0, MLA_ROPE, 2, dtype=F32) / MLA_ROPE)
    zero = jnp.zeros_like(inv_freq)
    ang = positions.reshape(T).astype(F32)[:, None] * jnp.concatenate([inv_freq, zero, inv_freq, zero])
    cc = jnp.cos(ang) * jnp.concatenate([zero + 1.0, zero, zero + 1.0, zero])
    ss = jnp.sin(ang) * jnp.concatenate([zero - 1.0, zero, zero + 1.0, zero])

    g_mix, g_cq, g_ckv, g_ffn, g_mem = row(P['g_mix']), row(P['g_cq']), row(P['g_ckv']), row(P['g_ffn']), row(P['g_mem'])
    gqn, gkn, gmq, gmk = row(P['g_q_nope']), row(P['g_k_nope']), row(P['g_mq']), row(P['g_mk'])
    gqp, gkp = _spread_rope(row(P['g_q_pe'])), _spread_rope(row(P['g_k_pe']))
    gln, bln = row(P['g_gm_ln']), row(P['b_gm_ln'])
    wc = jnp.tril(P['w_spatial'].astype(F32))
    wct = jnp.swapaxes(wc, 1, 2).astype(BF16)
    wc = wc.astype(BF16)
    bst = jnp.broadcast_to(P['b_spatial'].astype(F32)[:, :, None], (GM_GROUPS, GM_CHUNK, LANES))

    ride = ws.gather(['w_in'])
    h, got = _unride(_rms_fwd(x2d, g_mix, "rms_mix", rider=ride), ride)
    w_in = _win_layout(ws.gathered(['w_in'], got)['w_in']).astype(BF16)
    ride = ws.gather(AG_MID)
    z, got = _unride(_matmul(h, w_in, 'nn', ACT, "mm_in", tn_t=1792, rider=ride), ride)
    mid = ws.gathered(AG_MID, got)
    w_uq, w_ukv = _wuq_layout(mid['w_uq']).astype(BF16), _wukv_layout(mid['w_ukv']).astype(BF16)
    w_mem_kv, w_o_gm, w_o_mla, w_o_mem, w_out = (mid[n] for n in ('w_mem_kv', 'w_o_gm', 'w_o_mla', 'w_o_mem', 'w_out'))
    ygm_pre = _gm_fwd(z, gln, bln, wc, bst, "gm_fwd")
    y_gm = _matmul(ygm_pre, w_o_gm, 'nn', ACT, "mm_o_gm")
    nq, nkv = _lat_fwd(z, g_cq, g_ckv, "lat_fwd")
    q = _matmul(nq, w_uq, 'nn', ACT, "mm_uq")
    kv = _matmul(nkv, w_ukv, 'nn', ACT, "mm_ukv")
    qcat, kcat, vv = _qk_fwd(q, kv, z, cc, ss, gqn, gqp, gkn, gkp, "qk_fwd")
    ride = ws.gather(AG_FFN)
    (o, lse), got = _unride(_attn_fwd(qcat, kcat, vv, B, MLA_HEADS, QCAT, 0, True, "mla_attn_fwd", rider=ride), ride)
    ffn = ws.gathered(AG_FFN, got)
    w_ff1, w_ff2 = ffn['w_ff1'], ffn['w_ff2']
    y_mla = _matmul(o, w_o_mla, 'nn', ACT, "mm_o_mla")
    nm = _rms_fwd(mem2d, g_mem, "rms_mem")
    kvm = _matmul(nm, w_mem_kv, 'nn', ACT, "mm_mem_kv")
    qm = _headnorm_fwd(z, Z_QM // (MEM_HEADS * HEAD), MEM_HEADS, gmq, MEM_SCALE * LOG2E, "memq_fwd")
    km = _headnorm_fwd(kvm, 0, MEM_HEADS, gmk, 1.0, "memk_fwd")
    om, lse_m = _attn_fwd(qm, km, kvm, B, MEM_HEADS, HEAD, MEM_HEADS, False, "mem_attn_fwd")
    y_mem = _matmul(om, w_o_mem, 'nn', ACT, "mm_o_mem")
    merged = _merge_fwd(z, y_gm, y_mla, y_mem, "merge_fwd")
    x1, h2 = _matmul(merged, w_out, 'nn', F32, "mm_out", add=x2d, rms_gain=g_ffn, tm_t=1024, tn_t=D_MODEL)
    a1 = _matmul(h2, w_ff1, 'nn', BF16, "mm_ff1")
    dx2, dx2b, loss_part = _matmul(a1, w_ff2, 'nn', F32, "mm_ff2", add=x1, relu2_a=True, sq_err_target=tgt2d,
                                   tm_t=512, tn_t=D_MODEL)

    G = {}
    d_ff2 = _matmul(a1, dx2b, 'tn', BF16, "mm_d_ff2", relu2_a=True)
    da1 = _matmul(dx2b, w_ff2, 'nt', BF16, "mm_da1", relu2_grad=a1)
    d_ff1 = _matmul(h2, da1, 'tn', BF16, "mm_d_ff1", col_shards=N_DEV)
    dx1, dx1b, G['g_ffn'] = _matmul(da1, w_ff1, 'nt', ACT, "mm_dh2", rms_bwd=(x1, g_ffn, dx2), tm_t=512, tn_t=D_MODEL)
    d_out = _matmul(merged, dx1b, 'tn', BF16, "mm_d_out")
    dmerged = _matmul(dx1b, w_out, 'nt', ACT, "mm_dmerged")
    dy_gm, dy_mla, dy_mem, dz = _merge_bwd(z, y_gm, y_mla, y_mem, dmerged, "merge_bwd")
    d_o_gm = _matmul(ygm_pre, dy_gm, 'tn', BF16, "mm_d_o_gm")
    d_o_mla = _matmul(o, dy_mla, 'tn', BF16, "mm_d_o_mla")
    d_o_mem = _matmul(om, dy_mem, 'tn', BF16, "mm_d_o_mem")
    dygm_pre = _matmul(dy_gm, w_o_gm, 'nt', ACT, "mm_dygm")
    dz, dws, dbs, G['g_gm_ln'], G['b_gm_ln'] = _gm_bwd(z, dygm_pre, gln, bln, wc, wct, bst, dz, "gm_bwd")
    G['w_spatial'] = jnp.tril(dws)
    G['b_spatial'] = jnp.sum(dbs.reshape(GM_CHUNK, GM_GROUPS, LANES), axis=-1).T
    do = _matmul(dy_mla, w_o_mla, 'nt', ACT, "mm_do")
    ride = ws.scatter('ffn_proj', {'w_ff2': d_ff2, 'w_ff1': d_ff1, 'w_out': d_out, 'w_o_gm': d_o_gm, 'w_o_mla': d_o_mla,
                                   'w_o_mem': d_o_mem})
    (dqc, dkc, dvv), got = _unride(_attn_bwd(qcat, kcat, vv, o, do, lse, B, MLA_HEADS, QCAT, 0, MLA_SCALE, True,
                                             "mla_attn_bwd", rider=ride), ride)
    ws.scattered('ffn_proj', got)
    dq, dkv, dkpe, G['g_q_nope'], dgqp, G['g_k_nope'], dgkp = _qk_bwd(q, kv, z, cc, ss, gqn, gqp, gkn, gkp, dqc, dkc, dvv,
                                                                     "qk_bwd")
    G['g_q_pe'], G['g_k_pe'] = _gather_rope(dgqp), _gather_rope(dgkp)
    d_uq = _wuq_unlayout(_matmul(nq, dq, 'tn', BF16, "mm_d_uq"))
    dnq = _matmul(dq, w_uq, 'nt', ACT, "mm_dnq")
    d_ukv = _wukv_unlayout(_matmul(nkv, dkv, 'tn', BF16, "mm_d_ukv"))
    dnkv = _matmul(dkv, w_ukv, 'nt', ACT, "mm_dnkv")
    dz, G['g_cq'], G['g_ckv'] = _lat_bwd(z, dnq, dnkv, dkpe, g_cq, g_ckv, dz, "lat_bwd")
    dom = _matmul(dy_mem, w_o_mem, 'nt', ACT, "mm_dom")
    dqm, dkm, dvm = _attn_bwd(qm, km, kvm, om, dom, lse_m, B, MEM_HEADS, HEAD, MEM_HEADS, MEM_SCALE, False, "mem_attn_bwd")
    dz, G['g_mq'] = _headnorm_bwd(z, Z_QM // (MEM_HEADS * HEAD), MEM_HEADS, gmq, dqm, None, "memq_bwd",
                                  into=(dz, Z_QM // (MEM_HEADS * HEAD)))
    dkvm, G['g_mk'] = _headnorm_bwd(kvm, 0, MEM_HEADS, gmk, dkm, dvm, "memk_bwd")
    d_mem_kv = _matmul(nm, dkvm, 'tn', BF16, "mm_d_mem_kv")
    dnm = _matmul(dkvm, w_mem_kv, 'nt', ACT, "mm_dnm")
    G['g_mem'], = _rms_bwd(mem2d, g_mem, dnm, None, "rms_mem_bwd", dx_dtypes=())
    half = D_MODEL // 2
    ride = ws.scatter('lat', {'w_uq': d_uq, 'w_ukv': d_ukv, 'w_mem_kv': d_mem_kv})
    d_top, got = _unride(_matmul(h, dz, 'tn', BF16, "mm_d_in_top", tn_t=1792, m_rows=(0, half), rider=ride), ride)
    ws.scattered('lat', got)
    ride = ws.scatter('in_top', {'w_in': _win_unlayout(d_top)})
    d_bot, got = _unride(_matmul(h, dz, 'tn', BF16, "mm_d_in_bot", tn_t=1792, m_rows=(half, half), rider=ride), ride)
    ws.scattered('in_top', got)
    ride = ws.scatter('in_bot', {'w_in': _win_unlayout(d_bot)})
    dh, got = _unride(_matmul(dz, w_in, 'nt', ACT, "mm_dh", rider=ride), ride)
    ws.scattered('in_bot', got)
    gx, G['g_mix'] = _rms_bwd(x2d, g_mix, dh, dx1, "rms_mix_bwd")
    return loss_part, gx.reshape(B, S, D_MODEL), G


def _all_gather8(xs, name):
    def body(x_ref, out_ref, send_sems, recv_sems, local_sem):
        x, y, c = lax.axis_index("x"), lax.axis_index("y"), lax.axis_index("c")
        me, sibling = (x, y, c), (x, y, 1 - c)
        chips = [(1 - x, y), (x, 1 - y), (1 - x, 1 - y)]

        def rows(px, py, pc):
            return out_ref.at[4 * px + 2 * py + pc]

        def copy(k, block, to, src=None):
            return pltpu.make_async_remote_copy(
                src_ref=rows(*block) if src is None else src, dst_ref=rows(*block),
                send_sem=send_sems.at[k], recv_sem=recv_sems.at[k], device_id=to, device_id_type=MESH)

        mine = pltpu.make_async_copy(x_ref, rows(*me), local_sem)
        mine.start()
        first = [copy(0, me, sibling, src=x_ref)]
        first += [copy(1 + j, me, (*chip, c), src=x_ref) for j, chip in enumerate(chips)]
        for cp in first:
            cp.start()
        passed = [copy(4 + j, (*chip, c), sibling) for j, chip in enumerate(chips)]
        for j, chip in enumerate(chips):
            copy(1 + j, (*chip, c), me).wait_recv()
            passed[j].start()
        copy(0, sibling, me).wait_recv()
        for j, chip in enumerate(chips):
            copy(4 + j, (*chip, 1 - c), me).wait_recv()
        for cp in first + passed:
            cp.wait_send()
        mine.wait()

    return pl.pallas_call(
        body, name=name, in_specs=[HBM_SPEC], out_specs=HBM_SPEC,
        out_shape=jax.ShapeDtypeStruct((N_DEV,) + xs.shape, xs.dtype),
        scratch_shapes=[pltpu.SemaphoreType.DMA((7,)), pltpu.SemaphoreType.DMA((7,)), pltpu.SemaphoreType.DMA],
    )(xs)


ADAMW_TILE_ELEMS = 256 * 1024


def _adamw_rows(w, g, m, v):
    m2 = ADAM_B1 * m + (1.0 - ADAM_B1) * g
    v2 = ADAM_B2 * v + (1.0 - ADAM_B2) * (g * g)
    m_hat = m2 / (1.0 - ADAM_B1 ** ADAM_STEP)
    v_hat = v2 / (1.0 - ADAM_B2 ** ADAM_STEP)
    delta = -ADAM_LR * (m_hat / (jnp.sqrt(v_hat) + ADAM_EPS) + ADAM_WD * w)
    return delta, m2, v2


def _sum_adamw(parts, w, m, v, name):
    rows, cols = w.shape
    assert sum(p.shape[1] for p in parts) == rows
    tr = _pick(min(p.shape[1] for p in parts), max(16, ADAMW_TILE_ELEMS // cols), 16)
    n = parts[0].shape[0]
    counts = [p.shape[1] // tr for p in parts]
    starts = [sum(counts[:k]) for k in range(len(parts))]

    def body(*refs):
        p_refs = refs[:len(parts)]
        w_ref, m_ref, v_ref, g_ref, d_ref, m2_ref, v2_ref = refs[len(parts):]
        g = None
        for p_ref, start in zip(p_refs, starts):
            gk = p_ref[0].astype(F32)
            for k in range(1, n):
                gk = gk + p_ref[k].astype(F32)
            g = gk if g is None else jnp.where(pl.program_id(0) >= start, gk, g)
        delta, m2, v2 = _adamw_rows(w_ref[...], g, m_ref[...], v_ref[...])
        g_ref[...] = g
        d_ref[...] = delta
        m2_ref[...] = m2
        v2_ref[...] = v2

    flat = pl.BlockSpec((tr, cols), lambda i: (i, 0))
    out = jax.ShapeDtypeStruct((rows, cols), F32)
    p_specs = [pl.BlockSpec((n, tr, cols), lambda i, s=s, c=c: (0, jnp.clip(i - s, 0, c - 1), 0))
               for s, c in zip(starts, counts)]
    return pl.pallas_call(
        body, name=name, grid=(rows // tr,), in_specs=p_specs + [flat, flat, flat], out_specs=[flat] * 4,
        out_shape=[out] * 4, compiler_params=_params(("parallel",)),
    )(*parts, w, m, v)


SMALL_WIDTH = {'g_mix': 1024, 'g_cq': 384, 'g_ckv': 256, 'g_q_nope': 128, 'g_q_pe': 128, 'g_k_nope': 128, 'g_k_pe': 128,
               'g_gm_ln': 512, 'b_gm_ln': 512, 'g_mem': 1024, 'g_mq': 128, 'g_mk': 128, 'g_ffn': 1024}
NARROW = ('g_q_pe', 'g_k_pe')


def _small_layout():
    layout, r = {}, 0
    for name in SMALL + ['loss']:
        rows = {'w_spatial': GM_GROUPS * GM_CHUNK, 'b_spatial': GM_GROUPS, 'loss': 1}.get(name) or SMALL_WIDTH[name] // LANES
        layout[name] = (r, rows)
        r += -(-rows // 8) * 8
    return layout, r


def _small_pack(grads, loss_part, name):
    layout, total = _small_layout()
    names = SMALL + ['loss']

    def body(*refs):
        out_ref = refs[-1]
        out_ref[...] = jnp.zeros((total, LANES), F32)
        for ref, n in zip(refs[:-1], names):
            r0, rows = layout[n]
            if n == 'w_spatial':
                for g in range(GM_GROUPS):
                    out_ref[r0 + g * GM_CHUNK:r0 + (g + 1) * GM_CHUNK, :] = ref[g]
            elif n == 'b_spatial':
                out_ref[r0:r0 + rows, :] = ref[...]
            else:
                for k in range(rows):
                    out_ref[r0 + k:r0 + k + 1, :] = ref[:, k * LANES:(k + 1) * LANES]

    return pl.pallas_call(body, name=name, out_shape=jax.ShapeDtypeStruct((total, LANES), F32))(
        *[grads[n] for n in SMALL], loss_part)


def _small_adamw(parts, w, m, v, name):
    layout, _ = _small_layout()
    n_dev = parts.shape[0]

    def body(*refs):
        p_ref = refs[0]
        ins = refs[1:1 + 3 * len(SMALL)]
        outs = refs[1 + 3 * len(SMALL):-1]

        def gsum(r0, rows):
            g = p_ref[0, r0:r0 + rows, :]
            for d in range(1, n_dev):
                g = g + p_ref[d, r0:r0 + rows, :]
            return g

        def step(idx, g, at):
            w_ref, m_ref, v_ref = ins[3 * idx:3 * idx + 3]
            delta, m2, v2 = _adamw_rows(w_ref[at], g, m_ref[at], v_ref[at])
            for ref, val in zip(outs[4 * idx:4 * idx + 4], (g, delta, m2, v2)):
                ref[at] = val

        for idx, n in enumerate(SMALL):
            r0, rows = layout[n]
            if n == 'w_spatial':
                for g in range(GM_GROUPS):
                    step(idx, gsum(r0 + g * GM_CHUNK, GM_CHUNK), (0, g))
            elif n == 'b_spatial':
                step(idx, gsum(r0, rows), (0,))
            else:
                for k in range(rows):
                    step(idx, gsum(r0 + k, 1), (slice(None), slice(k * LANES, (k + 1) * LANES)))
        refs[-1][...] = gsum(layout['loss'][0], 8)

    flat_in = [d[n] for n in SMALL for d in (w, m, v)]
    out_shape = [jax.ShapeDtypeStruct(w[n].shape, F32) for n in SMALL for _ in range(4)]
    res = pl.pallas_call(body, name=name, out_shape=out_shape + [jax.ShapeDtypeStruct((8, LANES), F32)])(parts, *flat_in)
    groups = [{n: res[4 * i + j] for i, n in enumerate(SMALL)} for j in range(4)]
    return groups, res[-1]


def _full_from_gathered(gathered, name):
    r, c = BIG_SHAPE[name]
    if BIG_AXIS[name] == 0:
        return gathered.reshape(r, c)
    return gathered.transpose(1, 0, 2).reshape(r, c)


def _shards_of_full(g, name):
    if g.ndim == 3:
        return g
    r, c = BIG_SHAPE[name]
    if BIG_AXIS[name] == 0:
        return g.reshape(N_DEV, r // N_DEV, c)
    return g.reshape(g.shape[0], N_DEV, c // N_DEV).transpose(1, 0, 2)


class _DistWeights:
    def __init__(self, shards):
        self.shards = shards
        self.received = {}

    def gather(self, names):
        return _Gather2([self.shards[n].astype(BF16) for n in names])

    def gathered(self, names, got):
        return {n: _full_from_gathered(g, n) for n, g in zip(names, got)}

    def scatter(self, key, grads):
        return _Exchange([_shards_of_full(grads[n], n) for n in RS_GROUPS[key]], scatter=True)

    def scattered(self, key, got):
        for n, g in zip(RS_GROUPS[key], got):
            self.received.setdefault(n, []).append(g)


def kernel(x, mem, positions, g_mix, w_in, g_cq, w_uq, g_ckv, w_ukv, g_q_nope, g_q_pe, g_k_nope, g_k_pe, g_gm_ln, b_gm_ln, w_spatial, b_spatial, g_mem, w_mem_kv, g_mq, g_mk, w_o_gm, w_o_mla, w_o_mem, w_out, g_ffn, w_ff1, w_ff2, loss_target, m_g_mix, m_w_in, m_g_cq, m_w_uq, m_g_ckv, m_w_ukv, m_g_q_nope, m_g_q_pe, m_g_k_nope, m_g_k_pe, m_g_gm_ln, m_b_gm_ln, m_w_spatial, m_b_spatial, m_g_mem, m_w_mem_kv, m_g_mq, m_g_mk, m_w_o_gm, m_w_o_mla, m_w_o_mem, m_w_out, m_g_ffn, m_w_ff1, m_w_ff2, v_g_mix, v_w_in, v_g_cq, v_w_uq, v_g_ckv, v_w_ukv, v_g_q_nope, v_g_q_pe, v_g_k_nope, v_g_k_pe, v_g_gm_ln, v_b_gm_ln, v_w_spatial, v_b_spatial, v_g_mem, v_w_mem_kv, v_g_mq, v_g_mk, v_w_o_gm, v_w_o_mla, v_w_o_mem, v_w_out, v_g_ffn, v_w_ff1, v_w_ff2):
    given = dict(locals())
    w = {n: given[n][0] for n in WEIGHTS}
    mom = {n: given['m_' + n][0] for n in WEIGHTS}
    var = {n: given['v_' + n][0] for n in WEIGHTS}

    ws = _DistWeights({n: w[n] for n in BIG})
    loss_part, grad_x, G = _local_step(x, mem, positions, loss_target, {n: w[n] for n in SMALL}, ws)

    outs = {}
    for n in BIG:
        for prefix, res in zip(("grad_", "delta_", "new_m_", "new_v_"),
                               _sum_adamw(ws.received[n], w[n], mom[n], var[n], "adamw_" + n)):
            outs[prefix + n] = res[None]

    def widen(d):
        return {n: (jnp.pad(d[n], ((0, 0), (0, LANES - MLA_ROPE))) if n in NARROW else d[n]) for n in SMALL}

    parts = _all_gather8(_small_pack(widen(G), loss_part, "small_pack"), "ag_small")
    small, loss_rows = _small_adamw(parts, *[widen({n: given[prefix + n] for n in SMALL}) for prefix in ("", "m_", "v_")],
                                    "adamw_small")
    loss = 0.5 * jnp.sum(loss_rows) / D_MODEL
    for prefix, group in zip(("grad_", "delta_", "new_m_", "new_v_"), small):
        for n in SMALL:
            outs[prefix + n] = group[n][:, :MLA_ROPE] if n in NARROW else group[n]
    return (loss, grad_x, *[outs[p + n] for p in ("grad_", "delta_", "new_m_", "new_v_") for n in WEIGHTS])
```

```python
import functools
import math

import jax
import jax.numpy as jnp
from jax import lax
from jax.experimental import pallas as pl
from jax.experimental.pallas import tpu as pltpu

F32 = jnp.float32
BF16 = jnp.bfloat16
ACT = BF16

D_MODEL = 1024
MEM_HEADS = 4
HEAD = 128
GM_WIDTH = 512
GM_CHUNK = 128
GM_GROUPS = 4
MLA_HEADS = 8
MLA_ROPE = 64
Q_LORA = 384
KV_LORA = 256
D_FF = 4096
EPS = 1e-6
ROPE_BASE = 10000.0
MLA_SCALE = 1.0 / math.sqrt(HEAD + MLA_ROPE)
MEM_SCALE = 1.0 / math.sqrt(HEAD)
LOG2E = 1.4426950408889634
LN2 = 0.6931471805599453
ATT_TILE = 256
C_ZU, C_ZV, C_CQ, C_CKV, C_KPE, C_QM, C_ZG, C_END = 0, 512, 1024, 1408, 1664, 1728, 2240, 5312
Z_GM, Z_QM, Z_MLA, Z_KPE, Z_COLS = 3072, 4096, 4608, 5248, 5376
MLA_W = 768
QCAT = 2 * HEAD
ADAM_LR, ADAM_B1, ADAM_B2, ADAM_EPS, ADAM_WD, ADAM_STEP = 0.001, 0.9, 0.999, 1e-08, 0.01, 10
N_DEV = 8
LANES = 128
VMEM_LIMIT = 52 * 1024 * 1024
MAX_K_TILE = 8192
NEG = -1e30

BIG = ['w_in', 'w_uq', 'w_ukv', 'w_mem_kv', 'w_o_gm', 'w_o_mla', 'w_o_mem', 'w_out', 'w_ff1', 'w_ff2']
BIG_AXIS = {'w_in': 1, 'w_uq': 1, 'w_ukv': 1, 'w_mem_kv': 0, 'w_o_gm': 1, 'w_o_mla': 0, 'w_o_mem': 1,
            'w_out': 0, 'w_ff1': 1, 'w_ff2': 0}
BIG_SHAPE = {'w_in': (1024, 5312), 'w_uq': (384, 1536), 'w_ukv': (256, 2048), 'w_mem_kv': (1024, 1024),
             'w_o_gm': (512, 1024), 'w_o_mla': (1024, 1024), 'w_o_mem': (512, 1024), 'w_out': (1024, 1024),
             'w_ff1': (1024, 4096), 'w_ff2': (4096, 1024)}
SMALL = ['g_mix', 'g_cq', 'g_ckv', 'g_q_nope', 'g_q_pe', 'g_k_nope', 'g_k_pe', 'g_gm_ln', 'b_gm_ln',
         'w_spatial', 'b_spatial', 'g_mem', 'g_mq', 'g_mk', 'g_ffn']
WEIGHTS = ['g_mix', 'w_in', 'g_cq', 'w_uq', 'g_ckv', 'w_ukv', 'g_q_nope', 'g_q_pe', 'g_k_nope', 'g_k_pe',
           'g_gm_ln', 'b_gm_ln', 'w_spatial', 'b_spatial', 'g_mem', 'w_mem_kv', 'g_mq', 'g_mk', 'w_o_gm',
           'w_o_mla', 'w_o_mem', 'w_out', 'g_ffn', 'w_ff1', 'w_ff2']


def _pick(n, target, mult=LANES):
    best = None
    t = mult
    while t <= min(n, target):
        if n % t == 0:
            best = t
        t += mult
    return best if best is not None else n


def _params(sem):
    return pltpu.CompilerParams(dimension_semantics=sem, vmem_limit_bytes=VMEM_LIMIT)


MESH = pl.DeviceIdType.MESH
HBM_SPEC = pl.BlockSpec(memory_space=pltpu.HBM)


class _Exchange:
    def __init__(self, srcs, scatter):
        self.srcs, self.scatter = list(srcs), scatter
        self.out_shapes = [jax.ShapeDtypeStruct(s.shape if scatter else (N_DEV,) + s.shape, s.dtype) for s in self.srcs]
        n = len(self.srcs)
        self.scratch = [pltpu.SemaphoreType.DMA((n, N_DEV - 1)), pltpu.SemaphoreType.DMA((n, N_DEV - 1)),
                        pltpu.SemaphoreType.DMA((n,))]

    def _copies(self, src_refs, dst_refs, send_sems, recv_sems, local_sems):
        x, y, c = lax.axis_index("x"), lax.axis_index("y"), lax.axis_index("c")
        me = 4 * x + 2 * y + c
        local, remote = [], []
        for a, (src_ref, dst_ref) in enumerate(zip(src_refs, dst_refs)):
            def mine_for(dev, src_ref=src_ref):
                return src_ref.at[dev] if self.scatter else src_ref

            local.append(pltpu.make_async_copy(mine_for(me), dst_ref.at[me], local_sems.at[a]))
            for k in range(1, N_DEV):
                px = 1 - x if k & 4 else x
                py = 1 - y if k & 2 else y
                pc = 1 - c if k & 1 else c
                remote.append(pltpu.make_async_remote_copy(
                    src_ref=mine_for(4 * px + 2 * py + pc), dst_ref=dst_ref.at[me], send_sem=send_sems.at[a, k - 1],
                    recv_sem=recv_sems.at[a, k - 1], device_id=(px, py, pc), device_id_type=MESH))
        return local, remote

    def start(self, *refs):
        local, remote = self._copies(*refs)
        for cp in local + remote:
            cp.start()

    def forward(self, *refs):
        pass

    def finish(self, *refs):
        local, remote = self._copies(*refs)
        for cp in remote + local:
            cp.wait()


class _Gather2:
    def __init__(self, srcs):
        self.srcs = list(srcs)
        self.out_shapes = [jax.ShapeDtypeStruct((N_DEV,) + s.shape, s.dtype) for s in self.srcs]
        n = len(self.srcs)
        self.scratch = [pltpu.SemaphoreType.DMA((n, N_DEV - 1)), pltpu.SemaphoreType.DMA((n, N_DEV - 1)),
                        pltpu.SemaphoreType.DMA((n,))]

    def _plan(self, src_refs, dst_refs, send_sems, recv_sems, local_sems):
        x, y, c = lax.axis_index("x"), lax.axis_index("y"), lax.axis_index("c")
        chips = [(1 - x, y), (x, 1 - y), (1 - x, 1 - y)]
        plans = []
        for a, (src_ref, dst_ref) in enumerate(zip(src_refs, dst_refs)):
            def copy(k, block, to, src=None, a=a, dst_ref=dst_ref):
                at = dst_ref.at[4 * block[0] + 2 * block[1] + block[2]]
                return pltpu.make_async_remote_copy(src_ref=at if src is None else src, dst_ref=at,
                                                    send_sem=send_sems.at[a, k], recv_sem=recv_sems.at[a, k],
                                                    device_id=to, device_id_type=MESH)

            local = pltpu.make_async_copy(src_ref, dst_ref.at[4 * x + 2 * y + c], local_sems.at[a])
            first = [copy(0, (x, y, c), (x, y, 1 - c), src=src_ref)]
            first += [copy(1 + j, (x, y, c), (*chip, c), src=src_ref) for j, chip in enumerate(chips)]
            passed = [copy(4 + j, (*chip, c), (x, y, 1 - c)) for j, chip in enumerate(chips)]
            arrivals = [copy(1 + j, (*chip, c), (x, y, c)) for j, chip in enumerate(chips)]
            late = [copy(0, (x, y, 1 - c), (x, y, c))] + [copy(4 + j, (*chip, 1 - c), (x, y, c)) for j, chip in enumerate(chips)]
            plans.append((local, first, passed, arrivals, late))
        return plans

    def start(self, *refs):
        for local, first, _, _, _ in self._plan(*refs):
            local.start()
            for cp in first:
                cp.start()

    def forward(self, *refs):
        for _, _, passed, arrivals, _ in self._plan(*refs):
            for arrived, onward in zip(arrivals, passed):
                arrived.wait_recv()
                onward.start()

    def finish(self, *refs):
        for local, first, passed, _, late in self._plan(*refs):
            for cp in late:
                cp.wait_recv()
            for cp in first + passed:
                cp.wait_send()
            local.wait()


def _call(body, rider, ins, *, name, grid, in_specs, out_specs, out_shape, scratch_shapes, sem):
    if rider is None:
        return pl.pallas_call(body, name=name, grid=grid, in_specs=in_specs, out_specs=out_specs, out_shape=out_shape,
                              scratch_shapes=scratch_shapes, compiler_params=_params(sem))(*ins)
    single = not isinstance(out_shape, (list, tuple))
    own_specs, own_shapes = ([out_specs], [out_shape]) if single else (list(out_specs), list(out_shape))
    n_in, n_out, n_sc, n_r = len(ins), len(own_shapes), len(scratch_shapes), len(rider.srcs)
    n_all_in = n_in + n_r

    def carrying(*refs):
        own_in, srcs = refs[:n_in], refs[n_in:n_in + n_r]
        own_out, dsts = refs[n_all_in:n_all_in + n_out], refs[n_all_in + n_out:n_all_in + n_out + n_r]
        own_sc = refs[n_all_in + n_out + n_r:n_all_in + n_out + n_r + n_sc]
        sems = refs[n_all_in + n_out + n_r + n_sc:]
        first = last = late = None
        for d, steps in enumerate(grid):
            f, l = pl.program_id(d) == 0, pl.program_id(d) == steps - 1
            t = pl.program_id(d) == ((3 * steps) // 4 if d == 0 else 0)
            first, last, late = (f, l, t) if first is None else (first & f, last & l, late & t)

        @pl.when(first)
        def _():
            rider.start(srcs, dsts, *sems)

        @pl.when(late)
        def _():
            rider.forward(srcs, dsts, *sems)

        body(*own_in, *own_out, *own_sc)

        @pl.when(last)
        def _():
            rider.finish(srcs, dsts, *sems)

    res = pl.pallas_call(
        carrying, name=name, grid=grid, in_specs=list(in_specs) + [HBM_SPEC] * n_r,
        out_specs=own_specs + [HBM_SPEC] * n_r, out_shape=own_shapes + rider.out_shapes,
        scratch_shapes=list(scratch_shapes) + rider.scratch, compiler_params=_params(("arbitrary",) * len(grid)),
    )(*ins, *rider.srcs)
    own = res[:n_out]
    return (own[0] if single else list(own)), list(res[n_out:])


def _matmul(a, b, mode, out_dtype, name, add=None, relu2_a=False, relu2_grad=None,
            tm_t=None, tn_t=None, tk_t=None, rider=None, m_rows=None, col_shards=None, sq_err_target=None,
            rms_gain=None, rms_bwd=None, latents=None):
    if mode == 'nn':
        (M, K), (K2, N) = a.shape, b.shape
    elif mode == 'nt':
        (M, K), (N, K2) = a.shape, b.shape
    else:
        (K, M), (K2, N) = a.shape, b.shape
    assert K == K2, (name, a.shape, b.shape)
    m_first = 0
    if m_rows is not None:
        assert mode == 'tn'
        m_first, M = m_rows
    if col_shards is not None:
        assert add is None and relu2_grad is None and sq_err_target is None and tn_t is None
    if mode == 'tn':
        d_tm, d_tn, d_tk = 1024, (2048 if M <= 512 else 1024), 2048
    else:
        wide = add is None and sq_err_target is None and jnp.dtype(out_dtype).itemsize == 2 and K <= D_FF
        d_tm, d_tn, d_tk = (2048 if K <= 1024 else 1024), (1024 if wide else 512), MAX_K_TILE
    tm, tn, tk = _pick(M, tm_t or d_tm), _pick(N, tn_t or d_tn), _pick(K, tk_t or d_tk)
    gm, gn, nk = M // tm, N // tn, K // tk
    if mode == 'nn':
        a_spec = pl.BlockSpec((tm, tk), lambda i, j, k: (i, k))
        b_spec = pl.BlockSpec((tk, tn), lambda i, j, k: (k, j))
        dims = (((1,), (0,)), ((), ()))
    elif mode == 'nt':
        a_spec = pl.BlockSpec((tm, tk), lambda i, j, k: (i, k))
        b_spec = pl.BlockSpec((tn, tk), lambda i, j, k: (j, k))
        dims = (((1,), (1,)), ((), ()))
    else:
        assert m_first % tm == 0
        a_spec = pl.BlockSpec((tk, tm), lambda i, j, k: (k, m_first // tm + i))
        b_spec = pl.BlockSpec((tk, tn), lambda i, j, k: (k, j))
        dims = (((0,), (0,)), ((), ()))
    o_spec = pl.BlockSpec((tm, tn), lambda i, j, k: (i, j))
    shard_w = N // col_shards if col_shards is not None else tn
    assert tn % shard_w == 0
    has_add, has_e, has_t = add is not None, relu2_grad is not None, sq_err_target is not None
    has_n = rms_gain is not None
    assert not has_t or (nk == 1 and tn % LANES == 0)
    assert not has_n or (nk == 1 and tn == N and not has_t and col_shards is None)
    has_l = latents is not None
    lat_j, lat_c = Z_MLA // tn, Z_MLA % tn
    assert not has_l or (nk == 1 and lat_c % LANES == 0 and lat_c + Q_LORA + KV_LORA <= tn
                         and not (has_add or has_e or has_t or has_n) and col_shards is None)
    has_b = rms_bwd is not None
    assert not has_b or (nk == 1 and tn == N and not (has_add or has_e or has_t or has_n) and col_shards is None)

    def body(*refs):
        a_ref, b_ref = refs[0], refs[1]
        pos = 2
        add_ref = e_ref = t_ref = None
        if has_add:
            add_ref = refs[pos]
            pos += 1
        if has_e:
            e_ref = refs[pos]
            pos += 1
        if has_t:
            t_ref = refs[pos]
            pos += 1
        if has_n:
            gain_ref = refs[pos]
            pos += 1
        if has_b:
            x_ref, gain_ref, res_ref = refs[pos:pos + 3]
            pos += 3
        if has_l:
            gq_ref, gkv_ref = refs[pos:pos + 2]
            pos += 2
        o_ref = refs[pos]
        acc_ref = refs[pos + 1] if nk > 1 else None

        av = a_ref[...]
        if relu2_a:
            av = jnp.maximum(av, 0)
            av = av * av
        prod = lax.dot_general(av.astype(BF16), b_ref[...].astype(BF16), dims, preferred_element_type=F32)

        def finish(r):
            if has_add:
                r = r + add_ref[...]
            if has_e:
                r = r * (2.0 * jnp.maximum(e_ref[...].astype(F32), 0.0))
            if has_t:
                err = r - t_ref[...]
                r = err * (1.0 / N)
                refs[pos + 1][...] = r.astype(BF16)
                sq = err * err
                part = sq[:, 0:LANES]
                for c in range(1, tn // LANES):
                    part = part + sq[:, c * LANES:(c + 1) * LANES]
                _acc_rows(refs[pos + 2], part, (pl.program_id(0) == 0) & (pl.program_id(1) == 0))
            if has_n:
                refs[pos + 1][...] = (r * _rms(r, N) * gain_ref[...]).astype(BF16)
            if has_l:
                @pl.when(pl.program_id(1) == lat_j)
                def _():
                    cq = r[:, lat_c:lat_c + Q_LORA].astype(out_dtype)
                    ckv = r[:, lat_c + Q_LORA:lat_c + Q_LORA + KV_LORA].astype(out_dtype)
                    refs[pos + 1][...] = (cq * _rms(cq, Q_LORA) * gq_ref[...]).astype(BF16)
                    refs[pos + 2][...] = (ckv * _rms(ckv, KV_LORA) * gkv_ref[...]).astype(BF16)
            if has_b:
                dx, dgv = _rms_bwd_rows(x_ref[...], gain_ref[...], r.astype(out_dtype), N)
                dx = dx + res_ref[...]
                o_ref[...] = dx
                refs[pos + 1][...] = dx.astype(BF16)
                _acc_rows(refs[pos + 2], dgv, (pl.program_id(0) == 0) & (pl.program_id(1) == 0))
            elif col_shards is not None:
                for s in range(tn // shard_w):
                    o_ref[s] = r[:, s * shard_w:(s + 1) * shard_w].astype(out_dtype)
            else:
                o_ref[...] = r.astype(out_dtype)

        if nk == 1:
            finish(prod)
        else:
            k = pl.program_id(2)

            @pl.when(k == 0)
            def _():
                acc_ref[...] = prod

            @pl.when(k > 0)
            def _():
                acc_ref[...] += prod

            @pl.when(k == nk - 1)
            def _():
                finish(acc_ref[...])

    ins, specs = [a, b], [a_spec, b_spec]
    if has_add:
        ins.append(add)
        specs.append(o_spec)
    if has_e:
        ins.append(relu2_grad)
        specs.append(o_spec)
    out_specs, out_shape, sem = o_spec, jax.ShapeDtypeStruct((M, N), out_dtype), ("parallel", "parallel", "arbitrary")
    if has_t:
        ins.append(sq_err_target)
        specs.append(o_spec)
        out_specs = [o_spec, o_spec, pl.BlockSpec((1, LANES), lambda i, j, k: (0, 0))]
        out_shape = [out_shape, jax.ShapeDtypeStruct((M, N), BF16), jax.ShapeDtypeStruct((1, LANES), F32)]
        sem = ("arbitrary", "arbitrary", "arbitrary")
    if has_n:
        ins.append(rms_gain)
        specs.append(pl.BlockSpec((1, N), lambda i, j, k: (0, 0)))
        out_specs = [o_spec, o_spec]
        out_shape = [out_shape, jax.ShapeDtypeStruct((M, N), BF16)]
    if has_l:
        ins += list(latents)
        specs += [pl.BlockSpec((1, Q_LORA), lambda i, j, k: (0, 0)), pl.BlockSpec((1, KV_LORA), lambda i, j, k: (0, 0))]
        out_specs = [o_spec, pl.BlockSpec((tm, Q_LORA), lambda i, j, k: (i, 0)), pl.BlockSpec((tm, KV_LORA), lambda i, j, k: (i, 0))]
        out_shape = [out_shape, jax.ShapeDtypeStruct((M, Q_LORA), BF16), jax.ShapeDtypeStruct((M, KV_LORA), BF16)]
        sem = ("parallel", "arbitrary", "arbitrary")
    if has_b:
        x, g, res = rms_bwd
        ins += [x, g, res]
        specs += [o_spec, pl.BlockSpec((1, N), lambda i, j, k: (0, 0)), o_spec]
        out_specs = [o_spec, o_spec, pl.BlockSpec((1, N), lambda i, j, k: (0, 0))]
        out_shape = [jax.ShapeDtypeStruct((M, N), F32), jax.ShapeDtypeStruct((M, N), BF16), jax.ShapeDtypeStruct((1, N), F32)]
        sem = ("arbitrary", "arbitrary", "arbitrary")
    if col_shards is not None:
        out_specs = pl.BlockSpec((tn // shard_w, tm, shard_w), lambda i, j, k: (j, i, 0))
        out_shape = jax.ShapeDtypeStruct((col_shards, M, shard_w), out_dtype)
    return _call(body, rider, ins, name=name, grid=(gm, gn, nk), in_specs=specs, out_specs=out_specs, out_shape=out_shape,
                 scratch_shapes=[pltpu.VMEM((tm, tn), F32)] if nk > 1 else [], sem=sem)


ROW_BLOCK_BYTES = 12 * 1024 * 1024


def _row_tile(rows, row_bytes):
    return _pick(rows, max(16, min(1024, ROW_BLOCK_BYTES // row_bytes)), 16)


def _rowspec(tr, width, col=0):
    return pl.BlockSpec((tr, width), lambda i, col=col: (i, col))


def _fullspec(shape):
    nd = len(shape)
    return pl.BlockSpec(shape, lambda i, nd=nd: (0,) * nd)


def _rms(x, width):
    x = x.astype(F32)
    return lax.rsqrt(jnp.sum(x * x, axis=-1, keepdims=True) * (1.0 / width) + EPS)


def _rms_bwd_rows(x, g, dy, width):
    x, dy = x.astype(F32), dy.astype(F32)
    r = _rms(x, width)
    xh = x * r
    dn = dy * g
    dx = r * (dn - xh * (jnp.sum(dn * xh, axis=-1, keepdims=True) * (1.0 / width)))
    return dx, dy * xh


def _acc_rows(ref, val, first):
    s = jnp.sum(val, axis=0, keepdims=True)

    @pl.when(first)
    def _():
        ref[...] = s

    @pl.when(jnp.logical_not(first))
    def _():
        ref[...] += s


def _rms_fwd(x, g, name, rider=None):
    rows, width = x.shape
    tr = _row_tile(rows, 6 * width)

    def body(x_ref, g_ref, o_ref):
        xv = x_ref[...]
        o_ref[...] = (xv * _rms(xv, width) * g_ref[...]).astype(BF16)

    return _call(body, rider, [x, g], name=name, grid=(rows // tr,),
                 in_specs=[_rowspec(tr, width), _fullspec((1, width))], out_specs=_rowspec(tr, width),
                 out_shape=jax.ShapeDtypeStruct((rows, width), BF16), scratch_shapes=[], sem=("parallel",))


def _rms_bwd(x, g, dy, res, name, dx_dtypes=(F32,)):
    rows, width = x.shape
    tr = _row_tile(rows, 18 * width)
    has_res = res is not None
    n_in = 4 if has_res else 3

    def body(*refs):
        x_ref, g_ref, dy_ref = refs[:3]
        dx, dgv = _rms_bwd_rows(x_ref[...], g_ref[...], dy_ref[...], width)
        if has_res:
            dx = dx + refs[3][...]
        for ref, dt in zip(refs[n_in:], dx_dtypes):
            ref[...] = dx.astype(dt)
        _acc_rows(refs[-1], dgv, pl.program_id(0) == 0)

    ins = [x, g, dy] + ([res] if has_res else [])
    specs = [_rowspec(tr, width), _fullspec((1, width)), _rowspec(tr, width)] + ([_rowspec(tr, width)] if has_res else [])
    return pl.pallas_call(
        body, name=name, grid=(rows // tr,), in_specs=specs,
        out_specs=[_rowspec(tr, width)] * len(dx_dtypes) + [_fullspec((1, width))],
        out_shape=[jax.ShapeDtypeStruct((rows, width), dt) for dt in dx_dtypes] + [jax.ShapeDtypeStruct((1, width), F32)],
        compiler_params=_params(("arbitrary",)),
    )(*ins)


_GELU_C = math.sqrt(2.0 / math.pi)


def _gelu(x):
    t = jnp.tanh(_GELU_C * (x + 0.044715 * (x * x * x)))
    return 0.5 * x * (1.0 + t), t


def _gelu_grad(x, t):
    return 0.5 * (1.0 + t) + 0.5 * x * (1.0 - t * t) * (_GELU_C * (1.0 + 3.0 * 0.044715 * (x * x)))


def _gm_forward_rows(zu, zv, gln, bln, wc_ref, bst, n_chunk):
    u, tu = _gelu(zu)
    a, ta = _gelu(zv)
    mu = jnp.mean(a, axis=-1, keepdims=True)
    ac = a - mu
    rs = lax.rsqrt(jnp.mean(ac * ac, axis=-1, keepdims=True) + EPS)
    n = ac * rs
    v = n * gln + bln
    vb = v.astype(BF16)
    rows = []
    for c in range(n_chunk):
        cols = []
        for g in range(GM_GROUPS):
            vc = vb[c * GM_CHUNK:(c + 1) * GM_CHUNK, g * LANES:(g + 1) * LANES]
            mixed = jnp.dot(wc_ref[g], vc, preferred_element_type=F32) + bst[g]
            cols.append(mixed)
        rows.append(jnp.concatenate(cols, axis=1))
    mixed = jnp.concatenate(rows, axis=0) if n_chunk > 1 else rows[0]
    return u, tu, ta, n, rs, v, mixed


def _gm_fwd(z, gln, bln, wc, bst, name):
    rows = z.shape[0]
    tr = _pick(rows, 1024, GM_CHUNK)
    n_chunk = tr // GM_CHUNK

    def body(zu_ref, zv_ref, gln_ref, bln_ref, wc_ref, bst_ref, o_ref):
        u, _, _, _, _, _, mixed = _gm_forward_rows(zu_ref[...].astype(F32), zv_ref[...].astype(F32), gln_ref[...], bln_ref[...], wc_ref,
                                                   bst_ref, n_chunk)
        o_ref[...] = (u * mixed).astype(BF16)

    return pl.pallas_call(
        body, name=name, grid=(rows // tr,),
        in_specs=[_rowspec(tr, GM_WIDTH, Z_GM // GM_WIDTH), _rowspec(tr, GM_WIDTH, Z_GM // GM_WIDTH + 1),_fullspec((1, GM_WIDTH)), _fullspec((1, GM_WIDTH)),
                  _fullspec((GM_GROUPS, GM_CHUNK, GM_CHUNK)), _fullspec((GM_GROUPS, GM_CHUNK, LANES))],
        out_specs=_rowspec(tr, GM_WIDTH), out_shape=jax.ShapeDtypeStruct((rows, GM_WIDTH), BF16),
        compiler_params=_params(("parallel",)),
    )(z, z, gln, bln, wc, bst)


ANY_SPEC = pl.BlockSpec(memory_space=pl.ANY)


def _gm_bwd(z, dy, gln, bln, wc, wct, bst, dz, name):
    rows = z.shape[0]
    tr = _pick(rows, 1024, GM_CHUNK)
    n_chunk = tr // GM_CHUNK

    def body(zu_ref, zv_ref, dy_ref, gln_ref, bln_ref, wc_ref, wct_ref, bst_ref, _, dz_ref, dws_ref, dbs_ref, dgl_ref,
             dbl_ref):
        first = pl.program_id(0) == 0
        zu, zv, gln = zu_ref[...].astype(F32), zv_ref[...].astype(F32), gln_ref[...]
        u, tu, ta, n, rs, v, mixed = _gm_forward_rows(zu, zv, gln, bln_ref[...], wc_ref, bst_ref, n_chunk)
        dyv = dy_ref[...].astype(F32)
        dzu = dyv * mixed * _gelu_grad(zu, tu)
        dmix = dyv * u
        dmb = dmix.astype(BF16)
        vb = v.astype(BF16)
        dv_rows, dws, dbs = [], [None] * GM_GROUPS, None
        for c in range(n_chunk):
            rsl = slice(c * GM_CHUNK, (c + 1) * GM_CHUNK)
            cols = []
            for g in range(GM_GROUPS):
                csl = slice(g * LANES, (g + 1) * LANES)
                dmc = dmb[rsl, csl]
                cols.append(jnp.dot(wct_ref[g], dmc, preferred_element_type=F32))
                w_part = lax.dot_general(dmc, vb[rsl, csl], (((1,), (1,)), ((), ())), preferred_element_type=F32)
                dws[g] = w_part if dws[g] is None else dws[g] + w_part
            dv_rows.append(jnp.concatenate(cols, axis=1))
            dbs = dmix[rsl, :] if dbs is None else dbs + dmix[rsl, :]
        dv = jnp.concatenate(dv_rows, axis=0) if n_chunk > 1 else dv_rows[0]
        dn = dv * gln
        da = rs * (dn - jnp.mean(dn, axis=-1, keepdims=True) - n * jnp.mean(dn * n, axis=-1, keepdims=True))
        dzv = da * _gelu_grad(zv, ta)
        dz_ref[:, 0:GM_WIDTH] = dzu.astype(BF16)
        dz_ref[:, GM_WIDTH:2 * GM_WIDTH] = dzv.astype(BF16)
        _acc_rows(dgl_ref, dv * n, first)
        _acc_rows(dbl_ref, dv, first)

        @pl.when(first)
        def _():
            for g in range(GM_GROUPS):
                dws_ref[g] = dws[g]
            dbs_ref[...] = dbs

        @pl.when(jnp.logical_not(first))
        def _():
            for g in range(GM_GROUPS):
                dws_ref[g] += dws[g]
            dbs_ref[...] += dbs

    wspec = _fullspec((GM_GROUPS, GM_CHUNK, GM_CHUNK))
    return pl.pallas_call(
        body, name=name, grid=(rows // tr,),
        in_specs=[_rowspec(tr, GM_WIDTH, Z_GM // GM_WIDTH), _rowspec(tr, GM_WIDTH, Z_GM // GM_WIDTH + 1),
                  _rowspec(tr, GM_WIDTH), _fullspec((1, GM_WIDTH)), _fullspec((1, GM_WIDTH)), wspec, wspec, wspec, ANY_SPEC],
        out_specs=[_rowspec(tr, 2 * GM_WIDTH, Z_GM // (2 * GM_WIDTH)), wspec, _fullspec((GM_CHUNK, GM_WIDTH)),
                   _fullspec((1, GM_WIDTH)), _fullspec((1, GM_WIDTH))],
        out_shape=[jax.ShapeDtypeStruct(dz.shape, dz.dtype), jax.ShapeDtypeStruct((GM_GROUPS, GM_CHUNK, GM_CHUNK), F32),
                   jax.ShapeDtypeStruct((GM_CHUNK, GM_WIDTH), F32), jax.ShapeDtypeStruct((1, GM_WIDTH), F32),
                   jax.ShapeDtypeStruct((1, GM_WIDTH), F32)],
        input_output_aliases={8: 0}, compiler_params=_params(("arbitrary",)),
    )(z, z, dy, gln, bln, wc, wct, bst, dz)


def _lat_bwd(z, dnq, dnkv, dkpe, g_cq, g_ckv, dz, name):
    rows = z.shape[0]
    tr = _row_tile(rows, 8 * MLA_W)

    def body(z_ref, dnq_ref, dnkv_ref, dkpe_ref, gq_ref, gkv_ref, _, dz_ref, dgq_ref, dgkv_ref):
        first = pl.program_id(0) == 0
        zb = z_ref[...]
        dcq, dgq = _rms_bwd_rows(zb[:, 0:Q_LORA], gq_ref[...], dnq_ref[...], Q_LORA)
        dckv, dgkv = _rms_bwd_rows(zb[:, Q_LORA:Q_LORA + KV_LORA], gkv_ref[...], dnkv_ref[...], KV_LORA)
        dz_ref[:, 0:Q_LORA] = dcq.astype(BF16)
        dz_ref[:, Q_LORA:Q_LORA + KV_LORA] = dckv.astype(BF16)
        dz_ref[:, Q_LORA + KV_LORA:MLA_W] = dkpe_ref[...].astype(BF16)
        _acc_rows(dgq_ref, dgq, first)
        _acc_rows(dgkv_ref, dgkv, first)

    return pl.pallas_call(
        body, name=name, grid=(rows // tr,),
        in_specs=[_rowspec(tr, MLA_W, Z_MLA // MLA_W), _rowspec(tr, Q_LORA), _rowspec(tr, KV_LORA), _rowspec(tr, LANES),
                  _fullspec((1, Q_LORA)), _fullspec((1, KV_LORA)), ANY_SPEC],
        out_specs=[_rowspec(tr, MLA_W, Z_MLA // MLA_W), _fullspec((1, Q_LORA)), _fullspec((1, KV_LORA))],
        out_shape=[jax.ShapeDtypeStruct(dz.shape, dz.dtype), jax.ShapeDtypeStruct((1, Q_LORA), F32),
                   jax.ShapeDtypeStruct((1, KV_LORA), F32)],
        input_output_aliases={6: 0}, compiler_params=_params(("arbitrary",)),
    )(z, dnq, dnkv, dkpe, g_cq, g_ckv, dz)


def _rope(y, cc, ss):
    return y * cc + pltpu.roll(y, 64, 1) * ss


def _rope_bwd(d, cc, ss):
    return d * cc + pltpu.roll(d * ss, 64, 1)


def _qk_fwd(q, kv, z, cc, ss, gqn, gqp, gkn, gkp, name):
    rows = q.shape[0]
    W = MLA_HEADS * HEAD
    tr = _row_tile(rows, 20 * W)
    QS = MLA_SCALE * LOG2E

    def body(q_ref, kv_ref, kpe_ref, cc_ref, ss_ref, gqn_ref, gqp_ref, gkn_ref, gkp_ref, qc_ref, kc_ref, v_ref):
        cc, ss = cc_ref[...], ss_ref[...]
        kpe = kpe_ref[...]
        kp = _rope(kpe * _rms(kpe, MLA_ROPE) * gkp_ref[...], cc, ss).astype(BF16)
        for h in range(MLA_HEADS):
            qn = q_ref[:, h * HEAD:(h + 1) * HEAD]
            qp = q_ref[:, W + h * HEAD:W + (h + 1) * HEAD]
            kn = kv_ref[:, h * HEAD:(h + 1) * HEAD]
            qc_ref[:, h * QCAT:h * QCAT + HEAD] = (qn * _rms(qn, HEAD) * gqn_ref[...] * QS).astype(BF16)
            qc_ref[:, h * QCAT + HEAD:(h + 1) * QCAT] = (_rope(qp * _rms(qp, MLA_ROPE) * gqp_ref[...], cc, ss) * QS).astype(BF16)
            kc_ref[:, h * QCAT:h * QCAT + HEAD] = (kn * _rms(kn, HEAD) * gkn_ref[...]).astype(BF16)
            kc_ref[:, h * QCAT + HEAD:(h + 1) * QCAT] = kp
        v_ref[...] = kv_ref[:, W:2 * W].astype(BF16)

    g = _fullspec((1, HEAD))
    return _call(
        body, None, [q, kv, z, cc, ss, gqn, gqp, gkn, gkp], name=name, grid=(rows // tr,),
        in_specs=[_rowspec(tr, 2 * W), _rowspec(tr, 2 * W), _rowspec(tr, LANES, Z_KPE // LANES), _rowspec(tr, LANES),
                  _rowspec(tr, LANES), g, g, g, g],
        out_specs=[_rowspec(tr, MLA_HEADS * QCAT), _rowspec(tr, MLA_HEADS * QCAT), _rowspec(tr, W)],
        out_shape=[jax.ShapeDtypeStruct((rows, MLA_HEADS * QCAT), BF16), jax.ShapeDtypeStruct((rows, MLA_HEADS * QCAT), BF16),
                   jax.ShapeDtypeStruct((rows, W), BF16)],
        scratch_shapes=[], sem=("parallel",))


def _qk_bwd(q, kv, z, cc, ss, gqn, gqp, gkn, gkp, dqc, dkc, dv, name):
    rows = q.shape[0]
    W = MLA_HEADS * HEAD
    tr = _row_tile(rows, 24 * W)

    def body(q_ref, kv_ref, kpe_ref, cc_ref, ss_ref, gqn_ref, gqp_ref, gkn_ref, gkp_ref, dqc_ref, dkc_ref, dv_ref,
             dq_ref, dkv_ref, dkpe_ref, dgqn_ref, dgqp_ref, dgkn_ref, dgkp_ref):
        first = pl.program_id(0) == 0
        cc, ss = cc_ref[...], ss_ref[...]
        sqn = sqp = skn = dkp = None
        for h in range(MLA_HEADS):
            dx, dg = _rms_bwd_rows(q_ref[:, h * HEAD:(h + 1) * HEAD], gqn_ref[...], dqc_ref[:, h * QCAT:h * QCAT + HEAD], HEAD)
            dq_ref[:, h * HEAD:(h + 1) * HEAD] = dx.astype(BF16)
            sqn = dg if sqn is None else sqn + dg
            dy = _rope_bwd(dqc_ref[:, h * QCAT + HEAD:(h + 1) * QCAT], cc, ss)
            dx, dg = _rms_bwd_rows(q_ref[:, W + h * HEAD:W + (h + 1) * HEAD], gqp_ref[...], dy, MLA_ROPE)
            dq_ref[:, W + h * HEAD:W + (h + 1) * HEAD] = dx.astype(BF16)
            sqp = dg if sqp is None else sqp + dg
            dx, dg = _rms_bwd_rows(kv_ref[:, h * HEAD:(h + 1) * HEAD], gkn_ref[...], dkc_ref[:, h * QCAT:h * QCAT + HEAD], HEAD)
            dkv_ref[:, h * HEAD:(h + 1) * HEAD] = dx.astype(BF16)
            skn = dg if skn is None else skn + dg
            part = dkc_ref[:, h * QCAT + HEAD:(h + 1) * QCAT].astype(F32)
            dkp = part if dkp is None else dkp + part
        dkv_ref[:, W:2 * W] = dv_ref[...].astype(BF16)
        dx, dg = _rms_bwd_rows(kpe_ref[...], gkp_ref[...], _rope_bwd(dkp, cc, ss), MLA_ROPE)
        dkpe_ref[...] = dx
        _acc_rows(dgqn_ref, sqn, first)
        _acc_rows(dgqp_ref, sqp, first)
        _acc_rows(dgkn_ref, skn, first)
        _acc_rows(dgkp_ref, dg, first)

    g = _fullspec((1, HEAD))
    gs = jax.ShapeDtypeStruct((1, HEAD), F32)
    return pl.pallas_call(
        body, name=name, grid=(rows // tr,),
        in_specs=[_rowspec(tr, 2 * W), _rowspec(tr, 2 * W), _rowspec(tr, LANES, Z_KPE // LANES), _rowspec(tr, LANES),
                  _rowspec(tr, LANES), g, g, g, g, _rowspec(tr, MLA_HEADS * QCAT), _rowspec(tr, MLA_HEADS * QCAT),
                  _rowspec(tr, W)],
        out_specs=[_rowspec(tr, 2 * W), _rowspec(tr, 2 * W), _rowspec(tr, LANES), g, g, g, g],
        out_shape=[jax.ShapeDtypeStruct((rows, 2 * W), BF16), jax.ShapeDtypeStruct((rows, 2 * W), BF16),
                   jax.ShapeDtypeStruct((rows, LANES), F32), gs, gs, gs, gs],
        compiler_params=_params(("arbitrary",)),
    )(q, kv, z, cc, ss, gqn, gqp, gkn, gkp, dqc, dkc, dv)


def _headnorm_fwd(x, col, nheads, g, out_scale, name):
    rows = x.shape[0]
    W = nheads * HEAD
    tr = _row_tile(rows, 6 * W)

    def body(x_ref, g_ref, o_ref):
        for h in range(nheads):
            xv = x_ref[:, h * HEAD:(h + 1) * HEAD]
            o_ref[:, h * HEAD:(h + 1) * HEAD] = (xv * _rms(xv, HEAD) * g_ref[...] * out_scale).astype(BF16)

    return pl.pallas_call(
        body, name=name, grid=(rows // tr,),
        in_specs=[_rowspec(tr, W, col), _fullspec((1, HEAD))], out_specs=_rowspec(tr, W),
        out_shape=jax.ShapeDtypeStruct((rows, W), BF16), compiler_params=_params(("parallel",)),
    )(x, g)


def _headnorm_bwd(x, col, nheads, g, dy, tail, name, into=None):
    rows = x.shape[0]
    W = nheads * HEAD
    tr = _row_tile(rows, 12 * W)
    has_tail = tail is not None
    WO = 2 * W if has_tail else W

    def body(*refs):
        if into is not None:
            x_ref, g_ref, dy_ref, _, dx_ref, dg_ref = refs
        elif has_tail:
            x_ref, g_ref, dy_ref, t_ref, dx_ref, dg_ref = refs
        else:
            x_ref, g_ref, dy_ref, dx_ref, dg_ref = refs
        acc = None
        for h in range(nheads):
            sl = slice(h * HEAD, (h + 1) * HEAD)
            dx, dg = _rms_bwd_rows(x_ref[:, sl], g_ref[...], dy_ref[:, sl], HEAD)
            dx_ref[:, sl] = dx.astype(BF16)
            acc = dg if acc is None else acc + dg
        if has_tail:
            dx_ref[:, W:2 * W] = t_ref[...].astype(BF16)
        _acc_rows(dg_ref, acc, pl.program_id(0) == 0)

    ins = [x, g, dy] + ([tail] if has_tail else [])
    specs = [_rowspec(tr, W, col), _fullspec((1, HEAD)), _rowspec(tr, W)] + ([_rowspec(tr, W)] if has_tail else [])
    dx_spec, dx_shape, aliases = _rowspec(tr, WO), jax.ShapeDtypeStruct((rows, WO), BF16), {}
    if into is not None:
        assert not has_tail
        ins, specs = ins + [into[0]], specs + [ANY_SPEC]
        dx_spec, dx_shape, aliases = _rowspec(tr, W, into[1]), jax.ShapeDtypeStruct(into[0].shape, into[0].dtype), {3: 0}
    return pl.pallas_call(
        body, name=name, grid=(rows // tr,), in_specs=specs,
        out_specs=[dx_spec, _fullspec((1, HEAD))], out_shape=[dx_shape, jax.ShapeDtypeStruct((1, HEAD), F32)],
        input_output_aliases=aliases, compiler_params=_params(("arbitrary",)),
    )(*ins)


def _sigmoid(x):
    return 1.0 / (1.0 + jnp.exp(-x.astype(F32)))


def _merge_fwd(z, y_gm, y_mla, y_mem, name):
    rows = z.shape[0]
    tr = _row_tile(rows, 14 * D_MODEL)

    def body(g0_ref, g1_ref, g2_ref, a_ref, b_ref, c_ref, o_ref):
        m = _sigmoid(g0_ref[...]) * a_ref[...] + _sigmoid(g1_ref[...]) * b_ref[...] + _sigmoid(g2_ref[...]) * c_ref[...]
        o_ref[...] = m.astype(BF16)

    r = _rowspec(tr, D_MODEL)
    return pl.pallas_call(
        body, name=name, grid=(rows // tr,),
        in_specs=[_rowspec(tr, D_MODEL, 0), _rowspec(tr, D_MODEL, 1), _rowspec(tr, D_MODEL, 2),r, r, r],
        out_specs=r, out_shape=jax.ShapeDtypeStruct((rows, D_MODEL), BF16), compiler_params=_params(("parallel",)),
    )(z, z, z, y_gm, y_mla, y_mem)


def _merge_bwd(z, y_gm, y_mla, y_mem, dm, name):
    rows = z.shape[0]
    tr = _row_tile(rows, 24 * D_MODEL)

    def body(g0_ref, g1_ref, g2_ref, a_ref, b_ref, c_ref, dm_ref, da_ref, db_ref, dc_ref, dzg_ref):
        dmv = dm_ref[...].astype(F32)
        for k, (g_ref, y_ref, dy_ref) in enumerate(((g0_ref, a_ref, da_ref), (g1_ref, b_ref, db_ref), (g2_ref, c_ref, dc_ref))):
            s = _sigmoid(g_ref[...])
            dy_ref[...] = (dmv * s).astype(BF16)
            dzg_ref[:, k * D_MODEL:(k + 1) * D_MODEL] = (dmv * y_ref[...] * s * (1.0 - s)).astype(BF16)

    r = _rowspec(tr, D_MODEL)
    o = jax.ShapeDtypeStruct((rows, D_MODEL), BF16)
    return pl.pallas_call(
        body, name=name, grid=(rows // tr,),
        in_specs=[_rowspec(tr, D_MODEL, 0), _rowspec(tr, D_MODEL, 1), _rowspec(tr, D_MODEL, 2),r, r, r, r],
        out_specs=[r, r, r, _rowspec(tr, 3 * D_MODEL, 0)],
        out_shape=[o, o, o, jax.ShapeDtypeStruct((rows, Z_COLS), BF16)],
        compiler_params=_params(("parallel",)),
    )(z, z, z, y_gm, y_mla, y_mem, dm)


_NT = (((1,), (1,)), ((), ()))
_TN = (((0,), (0,)), ((), ()))


def _diag_mask(s):
    row = lax.broadcasted_iota(jnp.int32, s.shape, 0)
    col = lax.broadcasted_iota(jnp.int32, s.shape, 1)
    return jnp.where(row >= col, s, NEG)


def _attn_fwd(q, k, v, nb, nheads, dk, v_col0, causal, name, rider=None):
    S, Skv = q.shape[0] // nb, k.shape[0] // nb
    tq = _pick(Skv, ATT_TILE) if causal else _pick(S, 4 * ATT_TILE)
    nq = S // tq

    def body(q_ref, k_ref, v_ref, o_ref, lse_ref):
        for i in range(nq):
            r0 = i * tq
            qb = q_ref[r0:r0 + tq, :]
            if causal:
                spans = ([(0, r0, False)] if i > 0 else []) + [(r0, r0 + tq, True)]
            else:
                spans = [(0, Skv, False)]
            scores = []
            for a, b, masked in spans:
                s = lax.dot_general(qb, k_ref[a:b, :], _NT, preferred_element_type=F32)
                scores.append(_diag_mask(s) if masked else s)
            m = functools.reduce(jnp.maximum, [jnp.max(s, axis=-1, keepdims=True) for s in scores])
            l = acc = None
            for s, (a, b, _) in zip(scores, spans):
                p = jnp.exp2(s - m)
                lp = jnp.sum(p, axis=-1, keepdims=True)
                ap = jnp.dot(p.astype(BF16), v_ref[a:b, :].astype(BF16), preferred_element_type=F32)
                l, acc = (lp, ap) if l is None else (l + lp, acc + ap)
            o_ref[r0:r0 + tq, :] = (acc / l).astype(BF16)
            lse_ref[r0:r0 + tq, :] = m + jnp.log2(l)

    ins = [q, k, v]
    in_specs = [pl.BlockSpec((S, dk), lambda b, h: (b, h)), pl.BlockSpec((Skv, dk), lambda b, h: (b, h)),
                pl.BlockSpec((Skv, HEAD), lambda b, h: (b, v_col0 + h))]
    out_specs = [pl.BlockSpec((S, HEAD), lambda b, h: (b, h)), pl.BlockSpec((None, S, 1), lambda b, h: (h, b, 0))]
    out_shape = [jax.ShapeDtypeStruct((nb * S, nheads * HEAD), BF16), jax.ShapeDtypeStruct((nheads, nb * S, 1), F32)]
    return _call(body, rider, ins, name=name, grid=(nb, nheads), in_specs=in_specs, out_specs=out_specs,
                 out_shape=out_shape, scratch_shapes=[], sem=("parallel", "parallel"))


def _attn_bwd(q, k, v, o, do, lse, nb, nheads, dk, v_col0, scale, causal, name, rider=None):
    S, Skv = q.shape[0] // nb, k.shape[0] // nb
    tk = _pick(Skv, ATT_TILE)
    nkv = Skv // tk

    def body(q_ref, k_ref, v_ref, o_ref, do_ref, lse_ref, dq_ref, dk_ref, dv_ref, delta_ref, dob_ref, dqa_ref):
        dov = do_ref[...]
        delta_ref[...] = jnp.sum(o_ref[...].astype(F32) * dov.astype(F32), axis=-1, keepdims=True)
        dob_ref[...] = dov.astype(BF16)

        for j in range(nkv):
            c0 = j * tk
            kb = k_ref[c0:c0 + tk, :]
            vb = v_ref[c0:c0 + tk, :].astype(BF16)
            if causal:
                spans = [(c0, c0 + tk, True)] + ([(c0 + tk, S, False)] if c0 + tk < S else [])
            else:
                spans = [(0, S, False)]
            dk_acc = dv_acc = None
            for a, b, masked in spans:
                qb = q_ref[a:b, :]
                dob = dob_ref[a:b, :]
                s = lax.dot_general(qb, kb, _NT, preferred_element_type=F32)
                if masked:
                    s = _diag_mask(s)
                p = jnp.exp2(s - lse_ref[a:b, :])
                dp = lax.dot_general(dob, vb, _NT, preferred_element_type=F32)
                ds = (p * (dp - delta_ref[a:b, :])).astype(BF16)
                dv_p = lax.dot_general(p.astype(BF16), dob, _TN, preferred_element_type=F32)
                dk_p = lax.dot_general(ds, qb, _TN, preferred_element_type=F32)
                dk_acc, dv_acc = (dk_p, dv_p) if dk_acc is None else (dk_acc + dk_p, dv_acc + dv_p)
                dq_p = jnp.dot(ds, kb, preferred_element_type=F32) * scale
                if j == 0:
                    dqa_ref[a:b, :] = dq_p
                else:
                    dqa_ref[a:b, :] += dq_p
            dk_ref[c0:c0 + tk, :] = (dk_acc * LN2).astype(BF16)
            dv_ref[c0:c0 + tk, :] = dv_acc.astype(BF16)
        dq_ref[...] = dqa_ref[...].astype(BF16)

    ins = [q, k, v, o, do, lse]
    in_specs = [pl.BlockSpec((S, dk), lambda b, h: (b, h)), pl.BlockSpec((Skv, dk), lambda b, h: (b, h)),
                pl.BlockSpec((Skv, HEAD), lambda b, h: (b, v_col0 + h)), pl.BlockSpec((S, HEAD), lambda b, h: (b, h)),
                pl.BlockSpec((S, HEAD), lambda b, h: (b, h)), pl.BlockSpec((None, S, 1), lambda b, h: (h, b, 0))]
    out_specs = [pl.BlockSpec((S, dk), lambda b, h: (b, h)), pl.BlockSpec((Skv, dk), lambda b, h: (b, h)),
                 pl.BlockSpec((Skv, HEAD), lambda b, h: (b, h))]
    out_shape = [jax.ShapeDtypeStruct((nb * S, nheads * dk), BF16), jax.ShapeDtypeStruct((nb * Skv, nheads * dk), BF16),
                 jax.ShapeDtypeStruct((nb * Skv, nheads * HEAD), BF16)]
    return _call(body, rider, ins, name=name, grid=(nb, nheads), in_specs=in_specs, out_specs=out_specs,
                 out_shape=out_shape,
                 scratch_shapes=[pltpu.VMEM((S, 1), F32), pltpu.VMEM((S, HEAD), BF16), pltpu.VMEM((S, dk), F32)],
                 sem=("parallel", "parallel"))


def _spread_rope(a):
    zero = jnp.zeros(a.shape[:-1] + (32,), a.dtype)
    return jnp.concatenate([a[..., :32], zero, a[..., 32:], zero], axis=-1)


def _gather_rope(a):
    return jnp.concatenate([a[..., 0:32], a[..., 64:96]], axis=-1)


def _win_layout(w):
    return jnp.concatenate([w[:, C_ZG:C_END], w[:, C_ZU:C_CQ], w[:, C_QM:C_ZG], w[:, C_CQ:C_CKV], w[:, C_CKV:C_KPE],
                            _spread_rope(w[:, C_KPE:C_QM])], axis=1)


def _win_unlayout(d):
    return jnp.concatenate([d[:, Z_GM:Z_QM], d[:, Z_MLA:Z_MLA + Q_LORA], d[:, Z_MLA + Q_LORA:Z_KPE],
                            _gather_rope(d[:, Z_KPE:Z_COLS]), d[:, Z_QM:Z_MLA], d[:, 0:Z_GM]], axis=1)


def _wuq_layout(w):
    r = w.reshape(Q_LORA, MLA_HEADS, HEAD + MLA_ROPE)
    return jnp.concatenate([r[:, :, :HEAD].reshape(Q_LORA, -1), _spread_rope(r[:, :, HEAD:]).reshape(Q_LORA, -1)], axis=1)


def _wuq_unlayout(d):
    n = d[:, :MLA_HEADS * HEAD].reshape(Q_LORA, MLA_HEADS, HEAD)
    p = _gather_rope(d[:, MLA_HEADS * HEAD:].reshape(Q_LORA, MLA_HEADS, HEAD))
    return jnp.concatenate([n, p], axis=-1).reshape(Q_LORA, -1)


def _wukv_layout(w):
    r = w.reshape(KV_LORA, MLA_HEADS, 2 * HEAD)
    return jnp.concatenate([r[:, :, :HEAD].reshape(KV_LORA, -1), r[:, :, HEAD:].reshape(KV_LORA, -1)], axis=1)


def _wukv_unlayout(d):
    k = d[:, :MLA_HEADS * HEAD].reshape(KV_LORA, MLA_HEADS, HEAD)
    v = d[:, MLA_HEADS * HEAD:].reshape(KV_LORA, MLA_HEADS, HEAD)
    return jnp.concatenate([k, v], axis=-1).reshape(KV_LORA, -1)


AG_MID = ['w_uq', 'w_ukv', 'w_mem_kv', 'w_o_gm', 'w_o_mla', 'w_o_mem', 'w_out']
AG_FFN = ['w_ff1', 'w_ff2']
RS_GROUPS = {'ffn_proj': ['w_ff2', 'w_ff1', 'w_out', 'w_o_gm', 'w_o_mla', 'w_o_mem'],
             'lat': ['w_uq', 'w_ukv', 'w_mem_kv'], 'in_top': ['w_in'], 'in_bot': ['w_in']}


def _unride(res, rider):
    return (res, None) if rider is None else res


def _local_step(x, mem, positions, target, P, ws):
    B, S, _ = x.shape
    M = mem.shape[1]
    T = B * S
    x2d = x.reshape(T, D_MODEL)
    mem2d = mem.reshape(B * M, D_MODEL)
    tgt2d = target.reshape(T, D_MODEL)

    def row(v):
        return v.reshape(1, -1).astype(F32)

    inv_freq = ROPE_BASE ** (-jnp.arange(0, MLA_ROPE, 2, dtype=F32) / MLA_ROPE)
    zero = jnp.zeros_like(inv_freq)
    ang = positions.reshape(T).astype(F32)[:, None] * jnp.concatenate([inv_freq, zero, inv_freq, zero])
    cc = jnp.cos(ang) * jnp.concatenate([zero + 1.0, zero, zero + 1.0, zero])
    ss = jnp.sin(ang) * jnp.concatenate([zero - 1.0, zero, zero + 1.0, zero])

    g_mix, g_cq, g_ckv, g_ffn, g_mem = row(P['g_mix']), row(P['g_cq']), row(P['g_ckv']), row(P['g_ffn']), row(P['g_mem'])
    gqn, gkn, gmq, gmk = row(P['g_q_nope']), row(P['g_k_nope']), row(P['g_mq']), row(P['g_mk'])
    gqp, gkp = _spread_rope(row(P['g_q_pe'])), _spread_rope(row(P['g_k_pe']))
    gln, bln = row(P['g_gm_ln']), row(P['b_gm_ln'])
    wc = jnp.tril(P['w_spatial'].astype(F32))
    wct = jnp.swapaxes(wc, 1, 2).astype(BF16)
    wc = wc.astype(BF16)
    bst = jnp.broadcast_to(P['b_spatial'].astype(F32)[:, :, None], (GM_GROUPS, GM_CHUNK, LANES))

    ride = ws.gather(['w_in'])
    h, got = _unride(_rms_fwd(x2d, g_mix, "rms_mix", rider=ride), ride)
    w_in = _win_layout(ws.gathered(['w_in'], got)['w_in']).astype(BF16)
    ride = ws.gather(AG_MID)
    (z, nq, nkv), got = _unride(_matmul(h, w_in, 'nn', ACT, "mm_in", tn_t=1792, rider=ride, latents=(g_cq, g_ckv)),
                                ride)
    mid = ws.gathered(AG_MID, got)
    w_uq, w_ukv = _wuq_layout(mid['w_uq']).astype(BF16), _wukv_layout(mid['w_ukv']).astype(BF16)
    w_mem_kv, w_o_gm, w_o_mla, w_o_mem, w_out = (mid[n] for n in ('w_mem_kv', 'w_o_gm', 'w_o_mla', 'w_o_mem', 'w_out'))
    ygm_pre = _gm_fwd(z, gln, bln, wc, bst, "gm_fwd")
    y_gm = _matmul(ygm_pre, w_o_gm, 'nn', ACT, "mm_o_gm")
    q = _matmul(nq, w_uq, 'nn', ACT, "mm_uq")
    kv = _matmul(nkv, w_ukv, 'nn', ACT, "mm_ukv")
    qcat, kcat, vv = _qk_fwd(q, kv, z, cc, ss, gqn, gqp, gkn, gkp, "qk_fwd")
    ride = ws.gather(AG_FFN)
    (o, lse), got = _unride(_attn_fwd(qcat, kcat, vv, B, MLA_HEADS, QCAT, 0, True, "mla_attn_fwd", rider=ride), ride)
    ffn = ws.gathered(AG_FFN, got)
    w_ff1, w_ff2 = ffn['w_ff1'], ffn['w_ff2']
    y_mla = _matmul(o, w_o_mla, 'nn', ACT, "mm_o_mla")
    nm = _rms_fwd(mem2d, g_mem, "rms_mem")
    kvm = _matmul(nm, w_mem_kv, 'nn', ACT, "mm_mem_kv")
    qm = _headnorm_fwd(z, Z_QM // (MEM_HEADS * HEAD), MEM_HEADS, gmq, MEM_SCALE * LOG2E, "memq_fwd")
    km = _headnorm_fwd(kvm, 0, MEM_HEADS, gmk, 1.0, "memk_fwd")
    om, lse_m = _attn_fwd(qm, km, kvm, B, MEM_HEADS, HEAD, MEM_HEADS, False, "mem_attn_fwd")
    y_mem = _matmul(om, w_o_mem, 'nn', ACT, "mm_o_mem")
    merged = _merge_fwd(z, y_gm, y_mla, y_mem, "merge_fwd")
    x1, h2 = _matmul(merged, w_out, 'nn', F32, "mm_out", add=x2d, rms_gain=g_ffn, tm_t=1024, tn_t=D_MODEL)
    a1 = _matmul(h2, w_ff1, 'nn', BF16, "mm_ff1")
    dx2, dx2b, loss_part = _matmul(a1, w_ff2, 'nn', F32, "mm_ff2", add=x1, relu2_a=True, sq_err_target=tgt2d,
                                   tm_t=512, tn_t=D_MODEL)

    G = {}
    d_ff2 = _matmul(a1, dx2b, 'tn', BF16, "mm_d_ff2", relu2_a=True)
    da1 = _matmul(dx2b, w_ff2, 'nt', BF16, "mm_da1", relu2_grad=a1)
    d_ff1 = _matmul(h2, da1, 'tn', BF16, "mm_d_ff1", col_shards=N_DEV)
    dx1, dx1b, G['g_ffn'] = _matmul(da1, w_ff1, 'nt', ACT, "mm_dh2", rms_bwd=(x1, g_ffn, dx2), tm_t=512, tn_t=D_MODEL)
    d_out = _matmul(merged, dx1b, 'tn', BF16, "mm_d_out")
    dmerged = _matmul(dx1b, w_out, 'nt', ACT, "mm_dmerged")
    dy_gm, dy_mla, dy_mem, dz = _merge_bwd(z, y_gm, y_mla, y_mem, dmerged, "merge_bwd")
    d_o_gm = _matmul(ygm_pre, dy_gm, 'tn', BF16, "mm_d_o_gm")
    d_o_mla = _matmul(o, dy_mla, 'tn', BF16, "mm_d_o_mla")
    d_o_mem = _matmul(om, dy_mem, 'tn', BF16, "mm_d_o_mem")
    dygm_pre = _matmul(dy_gm, w_o_gm, 'nt', ACT, "mm_dygm")
    dz, dws, dbs, G['g_gm_ln'], G['b_gm_ln'] = _gm_bwd(z, dygm_pre, gln, bln, wc, wct, bst, dz, "gm_bwd")
    G['w_spatial'] = jnp.tril(dws)
    G['b_spatial'] = jnp.sum(dbs.reshape(GM_CHUNK, GM_GROUPS, LANES), axis=-1).T
    do = _matmul(dy_mla, w_o_mla, 'nt', ACT, "mm_do")
    ride = ws.scatter('ffn_proj', {'w_ff2': d_ff2, 'w_ff1': d_ff1, 'w_out': d_out, 'w_o_gm': d_o_gm, 'w_o_mla': d_o_mla,
                                   'w_o_mem': d_o_mem})
    (dqc, dkc, dvv), got = _unride(_attn_bwd(qcat, kcat, vv, o, do, lse, B, MLA_HEADS, QCAT, 0, MLA_SCALE, True,
                                             "mla_attn_bwd", rider=ride), ride)
    ws.scattered('ffn_proj', got)
    dq, dkv, dkpe, G['g_q_nope'], dgqp, G['g_k_nope'], dgkp = _qk_bwd(q, kv, z, cc, ss, gqn, gqp, gkn, gkp, dqc, dkc, dvv,
                                                                     "qk_bwd")
    G['g_q_pe'], G['g_k_pe'] = _gather_rope(dgqp), _gather_rope(dgkp)
    d_uq = _wuq_unlayout(_matmul(nq, dq, 'tn', BF16, "mm_d_uq"))
    dnq = _matmul(dq, w_uq, 'nt', ACT, "mm_dnq")
    d_ukv = _wukv_unlayout(_matmul(nkv, dkv, 'tn', BF16, "mm_d_ukv"))
    dnkv = _matmul(dkv, w_ukv, 'nt', ACT, "mm_dnkv")
    dz, G['g_cq'], G['g_ckv'] = _lat_bwd(z, dnq, dnkv, dkpe, g_cq, g_ckv, dz, "lat_bwd")
    dom = _matmul(dy_mem, w_o_mem, 'nt', ACT, "mm_dom")
    dqm, dkm, dvm = _attn_bwd(qm, km, kvm, om, dom, lse_m, B, MEM_HEADS, HEAD, MEM_HEADS, MEM_SCALE, False, "mem_attn_bwd")
    dz, G['g_mq'] = _headnorm_bwd(z, Z_QM // (MEM_HEADS * HEAD), MEM_HEADS, gmq, dqm, None, "memq_bwd",
                                  into=(dz, Z_QM // (MEM_HEADS * HEAD)))
    dkvm, G['g_mk'] = _headnorm_bwd(kvm, 0, MEM_HEADS, gmk, dkm, dvm, "memk_bwd")
    d_mem_kv = _matmul(nm, dkvm, 'tn', BF16, "mm_d_mem_kv")
    dnm = _matmul(dkvm, w_mem_kv, 'nt', ACT, "mm_dnm")
    G['g_mem'], = _rms_bwd(mem2d, g_mem, dnm, None, "rms_mem_bwd", dx_dtypes=())
    half = D_MODEL // 2
    ride = ws.scatter('lat', {'w_uq': d_uq, 'w_ukv': d_ukv, 'w_mem_kv': d_mem_kv})
    d_top, got = _unride(_matmul(h, dz, 'tn', BF16, "mm_d_in_top", tn_t=1792, m_rows=(0, half), rider=ride), ride)
    ws.scattered('lat', got)
    ride = ws.scatter('in_top', {'w_in': _win_unlayout(d_top)})
    d_bot, got = _unride(_matmul(h, dz, 'tn', BF16, "mm_d_in_bot", tn_t=1792, m_rows=(half, half), rider=ride), ride)
    ws.scattered('in_top', got)
    ride = ws.scatter('in_bot', {'w_in': _win_unlayout(d_bot)})
    dh, got = _unride(_matmul(dz, w_in, 'nt', ACT, "mm_dh", rider=ride), ride)
    ws.scattered('in_bot', got)
    gx, G['g_mix'] = _rms_bwd(x2d, g_mix, dh, dx1, "rms_mix_bwd")
    return loss_part, gx.reshape(B, S, D_MODEL), G


def _all_gather8(xs, name):
    def body(x_ref, out_ref, send_sems, recv_sems, local_sem):
        x, y, c = lax.axis_index("x"), lax.axis_index("y"), lax.axis_index("c")
        me, sibling = (x, y, c), (x, y, 1 - c)
        chips = [(1 - x, y), (x, 1 - y), (1 - x, 1 - y)]

        def rows(px, py, pc):
            return out_ref.at[4 * px + 2 * py + pc]

        def copy(k, block, to, src=None):
            return pltpu.make_async_remote_copy(
                src_ref=rows(*block) if src is None else src, dst_ref=rows(*block),
                send_sem=send_sems.at[k], recv_sem=recv_sems.at[k], device_id=to, device_id_type=MESH)

        mine = pltpu.make_async_copy(x_ref, rows(*me), local_sem)
        mine.start()
        first = [copy(0, me, sibling, src=x_ref)]
        first += [copy(1 + j, me, (*chip, c), src=x_ref) for j, chip in enumerate(chips)]
        for cp in first:
            cp.start()
        passed = [copy(4 + j, (*chip, c), sibling) for j, chip in enumerate(chips)]
        for j, chip in enumerate(chips):
            copy(1 + j, (*chip, c), me).wait_recv()
            passed[j].start()
        copy(0, sibling, me).wait_recv()
        for j, chip in enumerate(chips):
            copy(4 + j, (*chip, 1 - c), me).wait_recv()
        for cp in first + passed:
            cp.wait_send()
        mine.wait()

    return pl.pallas_call(
        body, name=name, in_specs=[HBM_SPEC], out_specs=HBM_SPEC,
        out_shape=jax.ShapeDtypeStruct((N_DEV,) + xs.shape, xs.dtype),
        scratch_shapes=[pltpu.SemaphoreType.DMA((7,)), pltpu.SemaphoreType.DMA((7,)), pltpu.SemaphoreType.DMA],
    )(xs)


ADAMW_TILE_ELEMS = 256 * 1024


def _adamw_rows(w, g, m, v):
    m2 = ADAM_B1 * m + (1.0 - ADAM_B1) * g
    v2 = ADAM_B2 * v + (1.0 - ADAM_B2) * (g * g)
    m_hat = m2 / (1.0 - ADAM_B1 ** ADAM_STEP)
    v_hat = v2 / (1.0 - ADAM_B2 ** ADAM_STEP)
    delta = -ADAM_LR * (m_hat / (jnp.sqrt(v_hat) + ADAM_EPS) + ADAM_WD * w)
    return delta, m2, v2


def _sum_adamw(parts, w, m, v, name):
    rows, cols = w.shape
    assert sum(p.shape[1] for p in parts) == rows
    tr = _pick(min(p.shape[1] for p in parts), max(16, ADAMW_TILE_ELEMS // cols), 16)
    n = parts[0].shape[0]
    counts = [p.shape[1] // tr for p in parts]
    starts = [sum(counts[:k]) for k in range(len(parts))]

    def body(*refs):
        p_refs = refs[:len(parts)]
        w_ref, m_ref, v_ref, g_ref, d_ref, m2_ref, v2_ref = refs[len(parts):]
        g = None
        for p_ref, start in zip(p_refs, starts):
            gk = p_ref[0].astype(F32)
            for k in range(1, n):
                gk = gk + p_ref[k].astype(F32)
            g = gk if g is None else jnp.where(pl.program_id(0) >= start, gk, g)
        delta, m2, v2 = _adamw_rows(w_ref[...], g, m_ref[...], v_ref[...])
        g_ref[...] = g
        d_ref[...] = delta
        m2_ref[...] = m2
        v2_ref[...] = v2

    flat = pl.BlockSpec((tr, cols), lambda i: (i, 0))
    out = jax.ShapeDtypeStruct((rows, cols), F32)
    p_specs = [pl.BlockSpec((n, tr, cols), lambda i, s=s, c=c: (0, jnp.clip(i - s, 0, c - 1), 0))
               for s, c in zip(starts, counts)]
    return pl.pallas_call(
        body, name=name, grid=(rows // tr,), in_specs=p_specs + [flat, flat, flat], out_specs=[flat] * 4,
        out_shape=[out] * 4, compiler_params=_params(("parallel",)),
    )(*parts, w, m, v)


SMALL_WIDTH = {'g_mix': 1024, 'g_cq': 384, 'g_ckv': 256, 'g_q_nope': 128, 'g_q_pe': 128, 'g_k_nope': 128, 'g_k_pe': 128,
               'g_gm_ln': 512, 'b_gm_ln': 512, 'g_mem': 1024, 'g_mq': 128, 'g_mk': 128, 'g_ffn': 1024}
NARROW = ('g_q_pe', 'g_k_pe')


def _small_layout():
    layout, r = {}, 0
    for name in SMALL + ['loss']:
        rows = {'w_spatial': GM_GROUPS * GM_CHUNK, 'b_spatial': GM_GROUPS, 'loss': 1}.get(name) or SMALL_WIDTH[name] // LANES
        layout[name] = (r, rows)
        r += -(-rows // 8) * 8
    return layout, r


def _small_pack(grads, loss_part, name):
    layout, total = _small_layout()
    names = SMALL + ['loss']

    def body(*refs):
        out_ref = refs[-1]
        out_ref[...] = jnp.zeros((total, LANES), F32)
        for ref, n in zip(refs[:-1], names):
            r0, rows = layout[n]
            if n == 'w_spatial':
                for g in range(GM_GROUPS):
                    out_ref[r0 + g * GM_CHUNK:r0 + (g + 1) * GM_CHUNK, :] = ref[g]
            elif n == 'b_spatial':
                out_ref[r0:r0 + rows, :] = ref[...]
            else:
                for k in range(rows):
                    out_ref[r0 + k:r0 + k + 1, :] = ref[:, k * LANES:(k + 1) * LANES]

    return pl.pallas_call(body, name=name, out_shape=jax.ShapeDtypeStruct((total, LANES), F32))(
        *[grads[n] for n in SMALL], loss_part)


def _small_adamw(parts, w, m, v, name):
    layout, _ = _small_layout()
    n_dev = parts.shape[0]

    def body(*refs):
        p_ref = refs[0]
        ins = refs[1:1 + 3 * len(SMALL)]
        outs = refs[1 + 3 * len(SMALL):-1]

        def gsum(r0, rows):
            g = p_ref[0, r0:r0 + rows, :]
            for d in range(1, n_dev):
                g = g + p_ref[d, r0:r0 + rows, :]
            return g

        def step(idx, g, at):
            w_ref, m_ref, v_ref = ins[3 * idx:3 * idx + 3]
            delta, m2, v2 = _adamw_rows(w_ref[at], g, m_ref[at], v_ref[at])
            for ref, val in zip(outs[4 * idx:4 * idx + 4], (g, delta, m2, v2)):
                ref[at] = val

        for idx, n in enumerate(SMALL):
            r0, rows = layout[n]
            if n == 'w_spatial':
                for g in range(GM_GROUPS):
                    step(idx, gsum(r0 + g * GM_CHUNK, GM_CHUNK), (0, g))
            elif n == 'b_spatial':
                step(idx, gsum(r0, rows), (0,))
            else:
                for k in range(rows):
                    step(idx, gsum(r0 + k, 1), (slice(None), slice(k * LANES, (k + 1) * LANES)))
        refs[-1][...] = gsum(layout['loss'][0], 8)

    flat_in = [d[n] for n in SMALL for d in (w, m, v)]
    out_shape = [jax.ShapeDtypeStruct(w[n].shape, F32) for n in SMALL for _ in range(4)]
    res = pl.pallas_call(body, name=name, out_shape=out_shape + [jax.ShapeDtypeStruct((8, LANES), F32)])(parts, *flat_in)
    groups = [{n: res[4 * i + j] for i, n in enumerate(SMALL)} for j in range(4)]
    return groups, res[-1]


def _full_from_gathered(gathered, name):
    r, c = BIG_SHAPE[name]
    if BIG_AXIS[name] == 0:
        return gathered.reshape(r, c)
    return gathered.transpose(1, 0, 2).reshape(r, c)


def _shards_of_full(g, name):
    if g.ndim == 3:
        return g
    r, c = BIG_SHAPE[name]
    if BIG_AXIS[name] == 0:
        return g.reshape(N_DEV, r // N_DEV, c)
    return g.reshape(g.shape[0], N_DEV, c // N_DEV).transpose(1, 0, 2)


class _DistWeights:
    def __init__(self, shards):
        self.shards = shards
        self.received = {}

    def gather(self, names):
        return _Gather2([self.shards[n].astype(BF16) for n in names])

    def gathered(self, names, got):
        return {n: _full_from_gathered(g, n) for n, g in zip(names, got)}

    def scatter(self, key, grads):
        return _Exchange([_shards_of_full(grads[n], n) for n in RS_GROUPS[key]], scatter=True)

    def scattered(self, key, got):
        for n, g in zip(RS_GROUPS[key], got):
            self.received.setdefault(n, []).append(g)


def kernel(x, mem, positions, g_mix, w_in, g_cq, w_uq, g_ckv, w_ukv, g_q_nope, g_q_pe, g_k_nope, g_k_pe, g_gm_ln, b_gm_ln, w_spatial, b_spatial, g_mem, w_mem_kv, g_mq, g_mk, w_o_gm, w_o_mla, w_o_mem, w_out, g_ffn, w_ff1, w_ff2, loss_target, m_g_mix, m_w_in, m_g_cq, m_w_uq, m_g_ckv, m_w_ukv, m_g_q_nope, m_g_q_pe, m_g_k_nope, m_g_k_pe, m_g_gm_ln, m_b_gm_ln, m_w_spatial, m_b_spatial, m_g_mem, m_w_mem_kv, m_g_mq, m_g_mk, m_w_o_gm, m_w_o_mla, m_w_o_mem, m_w_out, m_g_ffn, m_w_ff1, m_w_ff2, v_g_mix, v_w_in, v_g_cq, v_w_uq, v_g_ckv, v_w_ukv, v_g_q_nope, v_g_q_pe, v_g_k_nope, v_g_k_pe, v_g_gm_ln, v_b_gm_ln, v_w_spatial, v_b_spatial, v_g_mem, v_w_mem_kv, v_g_mq, v_g_mk, v_w_o_gm, v_w_o_mla, v_w_o_mem, v_w_out, v_g_ffn, v_w_ff1, v_w_ff2):
    given = dict(locals())
    w = {n: given[n][0] for n in WEIGHTS}
    mom = {n: given['m_' + n][0] for n in WEIGHTS}
    var = {n: given['v_' + n][0] for n in WEIGHTS}

    ws = _DistWeights({n: w[n] for n in BIG})
    loss_part, grad_x, G = _local_step(x, mem, positions, loss_target, {n: w[n] for n in SMALL}, ws)

    outs = {}
    for n in BIG:
        for prefix, res in zip(("grad_", "delta_", "new_m_", "new_v_"),
                               _sum_adamw(ws.received[n], w[n], mom[n], var[n], "adamw_" + n)):
            outs[prefix + n] = res[None]

    def widen(d):
        return {n: (jnp.pad(d[n], ((0, 0), (0, LANES - MLA_ROPE))) if n in NARROW else d[n]) for n in SMALL}

    parts = _all_gather8(_small_pack(widen(G), loss_part, "small_pack"), "ag_small")
    small, loss_rows = _small_adamw(parts, *[widen({n: given[prefix + n] for n in SMALL}) for prefix in ("", "m_", "v_")],
                                    "adamw_small")
    loss = 0.5 * jnp.sum(loss_rows) / D_MODEL
    for prefix, group in zip(("grad_", "delta_", "new_m_", "new_v_"), small):
        for n in SMALL:
            outs[prefix + n] = group[n][:, :MLA_ROPE] if n in NARROW else group[n]
    return (loss, grad_x, *[outs[p + n] for p in ("grad_", "delta_", "new_m_", "new_v_") for n in WEIGHTS])
```
